```python
import math
import jax, jax.numpy as jnp
from jax import lax
import numpy as np

D_MODEL = 2048
BATCH = 8
SEQ = 2048
DEPTH = 1

SGU_GROUP_DIM = 128
SGU_WIDTH = D_MODEL // 2
SGU_GROUPS = SGU_WIDTH // SGU_GROUP_DIM
CHUNK = 128
HEAD_DIM = 128
N_HEADS = (D_MODEL // 2) // HEAD_DIM
N_KV_HEADS = 2
GQA_GROUP = N_HEADS // N_KV_HEADS
ATT_WIDTH = N_HEADS * HEAD_DIM
KV_WIDTH = N_KV_HEADS * HEAD_DIM
WINDOW = 128
BLOCK = 128
REL_BUCKETS = 32
REL_MAX_DIST = 128
D_FF = ((8 * D_MODEL // 3 + 255) // 256) * 256
EPS = 1e-6
NEG = -1e30

IN_SPLITS = [SGU_WIDTH, SGU_WIDTH, ATT_WIDTH, KV_WIDTH, KV_WIDTH, D_MODEL, D_MODEL]
IN_COLS = int(sum(IN_SPLITS))
IN_OFFSETS = [int(o) for o in np.cumsum(IN_SPLITS)[:-1]]

kernel_name = "hybrid_sgu_swa_gated_encoder"


def rms_norm(x, g):
    xf = x.astype(jnp.float32)
    y = xf * lax.rsqrt(jnp.mean(xf * xf, axis=-1, keepdims=True) + EPS)
    return (y * g.astype(jnp.float32)).astype(x.dtype)


def t5_bucket(rel):
    nb = REL_BUCKETS // 2
    ret = jnp.where(rel > 0, nb, 0)
    n = jnp.abs(rel)
    max_exact = nb // 2
    nf = jnp.maximum(n, 1).astype(jnp.float32)
    large = max_exact + (jnp.log(nf / max_exact) / math.log(REL_MAX_DIST / max_exact)
                         * (nb - max_exact)).astype(jnp.int32)
    large = jnp.minimum(large, nb - 1)
    return ret + jnp.where(n < max_exact, n, large)


def band_structure(seq):
    nblk = seq // BLOCK
    qi = jnp.arange(BLOCK)[:, None]
    kj = jnp.arange(3 * BLOCK)[None, :]
    rel = kj - BLOCK - qi
    key_pos = jnp.arange(nblk)[:, None, None] * BLOCK + kj[None] - BLOCK
    valid = (jnp.abs(rel)[None] <= WINDOW) & (key_pos >= 0) & (key_pos < seq)
    return rel, valid


def band(t, nblk):
    tp = jnp.pad(t, ((0, 0), (BLOCK, BLOCK), (0, 0), (0, 0)))
    tp = tp.reshape(t.shape[0], nblk + 2, BLOCK, t.shape[2], t.shape[3])
    return jnp.concatenate([tp[:, :-2], tp[:, 1:-1], tp[:, 2:]], axis=2)


def windowed_gqa(q, k, v, rel_bias, sink):
    B, S = q.shape[0], q.shape[1]
    nb = S // BLOCK
    q = q.reshape(B, nb, BLOCK, N_KV_HEADS, GQA_GROUP, HEAD_DIM)
    kb = band(k.reshape(B, S, N_KV_HEADS, HEAD_DIM), nb)
    vb = band(v.reshape(B, S, N_KV_HEADS, HEAD_DIM), nb)
    s = jnp.einsum('bnqkgd,bnjkd->bnkgqj', q, kb).astype(jnp.float32) * (HEAD_DIM ** -0.5)
    rel, valid = band_structure(S)
    bias = rel_bias.astype(jnp.float32)[t5_bucket(rel)]
    bias = bias.transpose(2, 0, 1).reshape(N_KV_HEADS, GQA_GROUP, BLOCK, 3 * BLOCK)
    s = jnp.where(valid[None, :, None, None], s + bias, NEG)
    sink_logit = jnp.broadcast_to(
        sink.astype(jnp.float32).reshape(N_KV_HEADS, GQA_GROUP)[None, None, :, :, None, None],
        s.shape[:-1] + (1,))
    p = jax.nn.softmax(jnp.concatenate([s, sink_logit], axis=-1), axis=-1)[..., :-1]
    o = jnp.einsum('bnkgqj,bnjkd->bnqkgd', p.astype(vb.dtype), vb)
    return o.reshape(B, S, ATT_WIDTH)


def chunked_sgu(u, v, v_gain, w_s, b_s):
    B, S = u.shape[0], u.shape[1]
    nc = S // CHUNK
    v = rms_norm(v, v_gain).reshape(B, nc, CHUNK, SGU_GROUPS, SGU_GROUP_DIM)
    mixed = jnp.einsum('gpq,bcqge->bcpge', w_s, v) + b_s.T[:, :, None]
    return u * mixed.reshape(B, S, SGU_WIDTH)


def _fwd_setup_inputs(seed: int = 0) -> dict:
    key = jax.random.key(seed)
    ks = jax.random.split(key, 20)
    f32 = jnp.float32

    def nrm(k, shape, scale):
        return jax.random.normal(k, shape, f32) * scale

    return {
        "x": nrm(ks[0], (BATCH, SEQ, D_MODEL), 1.0),
        "w_in": nrm(ks[1], (DEPTH, D_MODEL, IN_COLS), D_MODEL ** -0.5),
        "norm_mix": 1.0 + nrm(ks[2], (DEPTH, D_MODEL), 0.05),
        "sgu_v_gain": 1.0 + nrm(ks[3], (DEPTH, SGU_WIDTH), 0.05),
        "sgu_w_s": nrm(ks[4], (DEPTH, SGU_GROUPS, CHUNK, CHUNK), 0.5 * CHUNK ** -0.5),
        "sgu_b_s": 1.0 + nrm(ks[5], (DEPTH, SGU_GROUPS, CHUNK), 0.1),
        "w_a_out": nrm(ks[6], (DEPTH, SGU_WIDTH, D_MODEL), SGU_WIDTH ** -0.5),
        "attn_sink": nrm(ks[7], (DEPTH, N_HEADS), 0.5),
        "rel_bias": nrm(ks[8], (REL_BUCKETS, N_HEADS), 0.5),
        "w_b_out": nrm(ks[9], (DEPTH, ATT_WIDTH, D_MODEL), ATT_WIDTH ** -0.5),
        "w_o": nrm(ks[10], (DEPTH, D_MODEL, D_MODEL), D_MODEL ** -0.5),
        "norm_ffn": 1.0 + nrm(ks[11], (DEPTH, D_MODEL), 0.05),
        "w_gate": nrm(ks[12], (DEPTH, D_MODEL, D_FF), D_MODEL ** -0.5),
        "w_up": nrm(ks[13], (DEPTH, D_MODEL, D_FF), D_MODEL ** -0.5),
        "w_down": nrm(ks[14], (DEPTH, D_FF, D_MODEL), D_FF ** -0.5),
        "norm_final": 1.0 + nrm(ks[15], (D_MODEL,), 0.05),
    }


def _fwd_reference(x, w_in, norm_mix, sgu_v_gain, sgu_w_s, sgu_b_s, w_a_out, attn_sink, rel_bias,
              w_b_out, w_o, norm_ffn, w_gate, w_up, w_down, norm_final):
    for l in range(DEPTH):
        h = rms_norm(x, norm_mix[l])
        z = h @ w_in[l]
        zu, zv, q, k, v, ga, gb = jnp.split(z, IN_OFFSETS, axis=-1)
        y_a = chunked_sgu(jax.nn.gelu(zu), jax.nn.gelu(zv), sgu_v_gain[l],
                          sgu_w_s[l], sgu_b_s[l]) @ w_a_out[l]
        y_b = windowed_gqa(q, k, v, rel_bias, attn_sink[l]) @ w_b_out[l]
        m = jax.nn.sigmoid(ga) * y_a + jax.nn.sigmoid(gb) * y_b
        x = x + m @ w_o[l]
        h = rms_norm(x, norm_ffn[l])
        x = x + (jax.nn.silu(h @ w_gate[l]) * (h @ w_up[l])) @ w_down[l]
    return rms_norm(x, norm_final)


import jax as _jax
import jax.numpy as _jnp

TWIN_FORMAT = 'train_step'
FWD_PARAMS = ['x', 'w_in', 'norm_mix', 'sgu_v_gain', 'sgu_w_s', 'sgu_b_s', 'w_a_out', 'attn_sink', 'rel_bias', 'w_b_out', 'w_o', 'norm_ffn', 'w_gate', 'w_up', 'w_down', 'norm_final']
TWIN_WEIGHTS = ['w_in', 'norm_mix', 'sgu_v_gain', 'sgu_w_s', 'sgu_b_s', 'w_a_out', 'attn_sink', 'rel_bias', 'w_b_out', 'w_o', 'norm_ffn', 'w_gate', 'w_up', 'w_down', 'norm_final']
TWIN_DIFF_INPUT = 'x'
TWIN_INPUTS = ['x', 'w_in', 'norm_mix', 'sgu_v_gain', 'sgu_w_s', 'sgu_b_s', 'w_a_out', 'attn_sink', 'rel_bias', 'w_b_out', 'w_o', 'norm_ffn', 'w_gate', 'w_up', 'w_down', 'norm_final', 'loss_target', 'm_w_in', 'm_norm_mix', 'm_sgu_v_gain', 'm_sgu_w_s', 'm_sgu_b_s', 'm_w_a_out', 'm_attn_sink', 'm_rel_bias', 'm_w_b_out', 'm_w_o', 'm_norm_ffn', 'm_w_gate', 'm_w_up', 'm_w_down', 'm_norm_final', 'v_w_in', 'v_norm_mix', 'v_sgu_v_gain', 'v_sgu_w_s', 'v_sgu_b_s', 'v_w_a_out', 'v_attn_sink', 'v_rel_bias', 'v_w_b_out', 'v_w_o', 'v_norm_ffn', 'v_w_gate', 'v_w_up', 'v_w_down', 'v_norm_final']
TWIN_OUTPUTS = ['loss', 'grad_x', 'grad_w_in', 'grad_norm_mix', 'grad_sgu_v_gain', 'grad_sgu_w_s', 'grad_sgu_b_s', 'grad_w_a_out', 'grad_attn_sink', 'grad_rel_bias', 'grad_w_b_out', 'grad_w_o', 'grad_norm_ffn', 'grad_w_gate', 'grad_w_up', 'grad_w_down', 'grad_norm_final', 'delta_w_in', 'delta_norm_mix', 'delta_sgu_v_gain', 'delta_sgu_w_s', 'delta_sgu_b_s', 'delta_w_a_out', 'delta_attn_sink', 'delta_rel_bias', 'delta_w_b_out', 'delta_w_o', 'delta_norm_ffn', 'delta_w_gate', 'delta_w_up', 'delta_w_down', 'delta_norm_final', 'new_m_w_in', 'new_m_norm_mix', 'new_m_sgu_v_gain', 'new_m_sgu_w_s', 'new_m_sgu_b_s', 'new_m_w_a_out', 'new_m_attn_sink', 'new_m_rel_bias', 'new_m_w_b_out', 'new_m_w_o', 'new_m_norm_ffn', 'new_m_w_gate', 'new_m_w_up', 'new_m_w_down', 'new_m_norm_final', 'new_v_w_in', 'new_v_norm_mix', 'new_v_sgu_v_gain', 'new_v_sgu_w_s', 'new_v_sgu_b_s', 'new_v_w_a_out', 'new_v_attn_sink', 'new_v_rel_bias', 'new_v_w_b_out', 'new_v_w_o', 'new_v_norm_ffn', 'new_v_w_gate', 'new_v_w_up', 'new_v_w_down', 'new_v_norm_final']
TWIN_LEAF_KINDS = {'loss': 'loss', 'grad_x': 'grad_x', 'grad_w_in': 'grad_w', 'grad_norm_mix': 'grad_w', 'grad_sgu_v_gain': 'grad_w', 'grad_sgu_w_s': 'grad_w', 'grad_sgu_b_s': 'grad_w', 'grad_w_a_out': 'grad_w', 'grad_attn_sink': 'grad_w', 'grad_rel_bias': 'grad_w', 'grad_w_b_out': 'grad_w', 'grad_w_o': 'grad_w', 'grad_norm_ffn': 'grad_w', 'grad_w_gate': 'grad_w', 'grad_w_up': 'grad_w', 'grad_w_down': 'grad_w', 'grad_norm_final': 'grad_w', 'delta_w_in': 'delta_w', 'delta_norm_mix': 'delta_w', 'delta_sgu_v_gain': 'delta_w', 'delta_sgu_w_s': 'delta_w', 'delta_sgu_b_s': 'delta_w', 'delta_w_a_out': 'delta_w', 'delta_attn_sink': 'delta_w', 'delta_rel_bias': 'delta_w', 'delta_w_b_out': 'delta_w', 'delta_w_o': 'delta_w', 'delta_norm_ffn': 'delta_w', 'delta_w_gate': 'delta_w', 'delta_w_up': 'delta_w', 'delta_w_down': 'delta_w', 'delta_norm_final': 'delta_w', 'new_m_w_in': 'new_m', 'new_m_norm_mix': 'new_m', 'new_m_sgu_v_gain': 'new_m', 'new_m_sgu_w_s': 'new_m', 'new_m_sgu_b_s': 'new_m', 'new_m_w_a_out': 'new_m', 'new_m_attn_sink': 'new_m', 'new_m_rel_bias': 'new_m', 'new_m_w_b_out': 'new_m', 'new_m_w_o': 'new_m', 'new_m_norm_ffn': 'new_m', 'new_m_w_gate': 'new_m', 'new_m_w_up': 'new_m', 'new_m_w_down': 'new_m', 'new_m_norm_final': 'new_m', 'new_v_w_in': 'new_v', 'new_v_norm_mix': 'new_v', 'new_v_sgu_v_gain': 'new_v', 'new_v_sgu_w_s': 'new_v', 'new_v_sgu_b_s': 'new_v', 'new_v_w_a_out': 'new_v', 'new_v_attn_sink': 'new_v', 'new_v_rel_bias': 'new_v', 'new_v_w_b_out': 'new_v', 'new_v_w_o': 'new_v', 'new_v_norm_ffn': 'new_v', 'new_v_w_gate': 'new_v', 'new_v_w_up': 'new_v', 'new_v_w_down': 'new_v', 'new_v_norm_final': 'new_v'}


def _forward(args):
    return _fwd_reference(*[args[k] for k in FWD_PARAMS])


def _output_shape():
    out = _jax.eval_shape(lambda: _forward(_fwd_setup_inputs(0)))
    return out.shape, out.dtype

N_MICROBATCH = 1
ADAM_LR = 0.001
ADAM_B1 = 0.9
ADAM_B2 = 0.999
ADAM_EPS = 1e-08
ADAM_WD = 0.01
ADAM_STEP = 10
PER_EXAMPLE_BATCH_AXIS = {'x': 0, 'loss_target': 0}
SHARED_INPUTS = []
_WEIGHT_DTYPES = {'w_in': _jnp.float32, 'norm_mix': _jnp.float32, 'sgu_v_gain': _jnp.float32, 'sgu_w_s': _jnp.float32, 'sgu_b_s': _jnp.float32, 'w_a_out': _jnp.float32, 'attn_sink': _jnp.float32, 'rel_bias': _jnp.float32, 'w_b_out': _jnp.float32, 'w_o': _jnp.float32, 'norm_ffn': _jnp.float32, 'w_gate': _jnp.float32, 'w_up': _jnp.float32, 'w_down': _jnp.float32, 'norm_final': _jnp.float32}
MOMENT_SCALE = {'w_in': 1.674886e-02, 'norm_mix': 3.304397e-02, 'sgu_v_gain': 1.638961e-02, 'sgu_w_s': 3.272604e-02, 'sgu_b_s': 3.304814e-02, 'w_a_out': 2.764525e-02, 'attn_sink': 2.313076e-04, 'rel_bias': 1.281201e-02, 'w_b_out': 4.619124e-03, 'w_o': 2.791234e-02, 'norm_ffn': 4.520643e-02, 'w_gate': 1.897629e-02, 'w_up': 1.844212e-02, 'w_down': 3.063385e-02, 'norm_final': 7.997139e+00}


def _to_microbatches(a, axis):
    t = _jnp.moveaxis(a, axis, 0)
    t = t.reshape((N_MICROBATCH, t.shape[0] // N_MICROBATCH) + t.shape[1:])
    return _jnp.moveaxis(t, 1, axis + 1)


def setup_inputs(seed: int = 0) -> dict:
    inp = _fwd_setup_inputs(seed)
    key = _jax.random.fold_in(_jax.random.key(seed), 7919)
    shape, _ = _output_shape()
    out = dict(inp)
    out["loss_target"] = _jax.random.normal(_jax.random.fold_in(key, 0), shape, _jnp.float32)
    for i, name in enumerate(TWIN_WEIGHTS):
        w = inp[name].astype(_jnp.float32)
        if MOMENT_SCALE is None:
            s = _jnp.sqrt(_jnp.mean(_jnp.square(w)) + 1e-30)
        else:
            s = MOMENT_SCALE[name]
        km, kv = _jax.random.split(_jax.random.fold_in(key, i + 1))
        out[name] = w
        out["m_" + name] = s * _jax.random.normal(km, w.shape, _jnp.float32)
        out["v_" + name] = (s * s) * _jax.random.uniform(kv, w.shape, _jnp.float32, 0.5, 1.5)
    if N_MICROBATCH > 1:
        for name, axis in PER_EXAMPLE_BATCH_AXIS.items():
            out[name] = _to_microbatches(out[name], axis)
    return {'x': out['x'], 'w_in': out['w_in'], 'norm_mix': out['norm_mix'], 'sgu_v_gain': out['sgu_v_gain'], 'sgu_w_s': out['sgu_w_s'], 'sgu_b_s': out['sgu_b_s'], 'w_a_out': out['w_a_out'], 'attn_sink': out['attn_sink'], 'rel_bias': out['rel_bias'], 'w_b_out': out['w_b_out'], 'w_o': out['w_o'], 'norm_ffn': out['norm_ffn'], 'w_gate': out['w_gate'], 'w_up': out['w_up'], 'w_down': out['w_down'], 'norm_final': out['norm_final'], 'loss_target': out['loss_target'], 'm_w_in': out['m_w_in'], 'm_norm_mix': out['m_norm_mix'], 'm_sgu_v_gain': out['m_sgu_v_gain'], 'm_sgu_w_s': out['m_sgu_w_s'], 'm_sgu_b_s': out['m_sgu_b_s'], 'm_w_a_out': out['m_w_a_out'], 'm_attn_sink': out['m_attn_sink'], 'm_rel_bias': out['m_rel_bias'], 'm_w_b_out': out['m_w_b_out'], 'm_w_o': out['m_w_o'], 'm_norm_ffn': out['m_norm_ffn'], 'm_w_gate': out['m_w_gate'], 'm_w_up': out['m_w_up'], 'm_w_down': out['m_w_down'], 'm_norm_final': out['m_norm_final'], 'v_w_in': out['v_w_in'], 'v_norm_mix': out['v_norm_mix'], 'v_sgu_v_gain': out['v_sgu_v_gain'], 'v_sgu_w_s': out['v_sgu_w_s'], 'v_sgu_b_s': out['v_sgu_b_s'], 'v_w_a_out': out['v_w_a_out'], 'v_attn_sink': out['v_attn_sink'], 'v_rel_bias': out['v_rel_bias'], 'v_w_b_out': out['v_w_b_out'], 'v_w_o': out['v_w_o'], 'v_norm_ffn': out['v_norm_ffn'], 'v_w_gate': out['v_w_gate'], 'v_w_up': out['v_w_up'], 'v_w_down': out['v_w_down'], 'v_norm_final': out['v_norm_final']}


def _loss(weights, diff, rest, loss_target):
    with _jax.named_scope("forward"):
        args = {**rest, TWIN_DIFF_INPUT: diff, **{k: w.astype(_WEIGHT_DTYPES[k]) for k, w in weights.items()}}
        y = _forward(args)
    with _jax.named_scope("loss_head"):
        err = _jnp.square(y.astype(_jnp.float32) - loss_target)
        return 0.5 * _jnp.sum(_jnp.mean(err, axis=-1)) if err.ndim else 0.5 * err


def _adamw(w, g, m, v):
    m = ADAM_B1 * m + (1.0 - ADAM_B1) * g
    v = ADAM_B2 * v + (1.0 - ADAM_B2) * _jnp.square(g)
    m_hat = m / (1.0 - ADAM_B1 ** ADAM_STEP)
    v_hat = v / (1.0 - ADAM_B2 ** ADAM_STEP)
    delta = -ADAM_LR * (m_hat / (_jnp.sqrt(v_hat) + ADAM_EPS) + ADAM_WD * w)
    return delta, m, v


def reference(x, w_in, norm_mix, sgu_v_gain, sgu_w_s, sgu_b_s, w_a_out, attn_sink, rel_bias, w_b_out, w_o, norm_ffn, w_gate, w_up, w_down, norm_final, loss_target, m_w_in, m_norm_mix, m_sgu_v_gain, m_sgu_w_s, m_sgu_b_s, m_w_a_out, m_attn_sink, m_rel_bias, m_w_b_out, m_w_o, m_norm_ffn, m_w_gate, m_w_up, m_w_down, m_norm_final, v_w_in, v_norm_mix, v_sgu_v_gain, v_sgu_w_s, v_sgu_b_s, v_w_a_out, v_attn_sink, v_rel_bias, v_w_b_out, v_w_o, v_norm_ffn, v_w_gate, v_w_up, v_w_down, v_norm_final):
    given = dict(x=x, w_in=w_in, norm_mix=norm_mix, sgu_v_gain=sgu_v_gain, sgu_w_s=sgu_w_s, sgu_b_s=sgu_b_s, w_a_out=w_a_out, attn_sink=attn_sink, rel_bias=rel_bias, w_b_out=w_b_out, w_o=w_o, norm_ffn=norm_ffn, w_gate=w_gate, w_up=w_up, w_down=w_down, norm_final=norm_final, loss_target=loss_target, m_w_in=m_w_in, m_norm_mix=m_norm_mix, m_sgu_v_gain=m_sgu_v_gain, m_sgu_w_s=m_sgu_w_s, m_sgu_b_s=m_sgu_b_s, m_w_a_out=m_w_a_out, m_attn_sink=m_attn_sink, m_rel_bias=m_rel_bias, m_w_b_out=m_w_b_out, m_w_o=m_w_o, m_norm_ffn=m_norm_ffn, m_w_gate=m_w_gate, m_w_up=m_w_up, m_w_down=m_w_down, m_norm_final=m_norm_final, v_w_in=v_w_in, v_norm_mix=v_norm_mix, v_sgu_v_gain=v_sgu_v_gain, v_sgu_w_s=v_sgu_w_s, v_sgu_b_s=v_sgu_b_s, v_w_a_out=v_w_a_out, v_attn_sink=v_attn_sink, v_rel_bias=v_rel_bias, v_w_b_out=v_w_b_out, v_w_o=v_w_o, v_norm_ffn=v_norm_ffn, v_w_gate=v_w_gate, v_w_up=v_w_up, v_w_down=v_w_down, v_norm_final=v_norm_final)
    weights = {n: given[n] for n in TWIN_WEIGHTS}
    shared = {n: given[n] for n in SHARED_INPUTS}
    per_example = {n: given[n] for n in ['x']}
    grad_fn = _jax.value_and_grad(_loss, argnums=(0, 1))

    def one_microbatch(ex, loss_target):
        ex = dict(ex)
        diff = ex.pop(TWIN_DIFF_INPUT)
        return grad_fn(weights, diff, {**shared, **ex}, loss_target)

    if N_MICROBATCH == 1:
        loss, (grad_w, grad_x) = one_microbatch(per_example, given["loss_target"])
    else:
        def body(carry, xs):
            loss_sum, grad_sum = carry
            l_k, (gw_k, gx_k) = one_microbatch(xs[0], xs[1])
            with _jax.named_scope("update"):
                return (loss_sum + l_k, _jax.tree.map(_jnp.add, grad_sum, gw_k)), gx_k

        init = (_jnp.zeros((), _jnp.float32), _jax.tree.map(_jnp.zeros_like, weights))
        (loss, grad_w), grad_x = _jax.lax.scan(body, init, (per_example, given["loss_target"]))
    with _jax.named_scope("update"):
        delta_w, new_m, new_v = {}, {}, {}
        for n in TWIN_WEIGHTS:
            delta_w[n], new_m[n], new_v[n] = _adamw(weights[n], grad_w[n], given["m_" + n], given["v_" + n])
    return (loss, grad_x, *[grad_w[n] for n in TWIN_WEIGHTS], *[delta_w[n] for n in TWIN_WEIGHTS],
            *[new_m[n] for n in TWIN_WEIGHTS], *[new_v[n] for n in TWIN_WEIGHTS])
```

```python
import functools
import math

import jax
import jax.numpy as jnp
import numpy as np
from jax import lax
from jax.experimental import pallas as pl
from jax.experimental.pallas import tpu as pltpu

F32 = jnp.float32
BF16 = jnp.bfloat16
I32 = jnp.int32
MESH = pl.DeviceIdType.MESH

EPS = 1e-6
NEG = -1e30
BLK = 128
HEAD_DIM = 128
N_KV_HEADS = 2
REL_BUCKETS = 32
REL_MAX_DIST = 128
N_CHIPS = 4
ADAM_LR, ADAM_B1, ADAM_B2, ADAM_EPS, ADAM_WD, ADAM_STEP = 0.001, 0.9, 0.999, 1e-08, 0.01, 10

LANES = 128
VMEM_CAP = 60 * 1024 * 1024

NN = (((1,), (0,)), ((), ()))
NT = (((1,), (1,)), ((), ()))
TN = (((0,), (0,)), ((), ()))
ANY = pl.BlockSpec(memory_space=pl.ANY)


def _tile(n, cands):
    for t in cands:
        if n % t == 0:
            return t
    return n


def _params(vmem_bytes=None, **kw):
    if vmem_bytes is not None:
        kw["vmem_limit_bytes"] = int(min(max(vmem_bytes, 32 * 1024 * 1024), VMEM_CAP))
    return pltpu.CompilerParams(**kw)


def _nbytes(shape, dtype):
    return int(np.prod(shape)) * jnp.dtype(dtype).itemsize


def _sigmoid(x):
    return 1.0 / (1.0 + jnp.exp(-x))


_GC = 0.7978845608028654
_GA = 0.044715


def _gelu(x):
    return 0.5 * x * (1.0 + jnp.tanh(_GC * (x + _GA * (x * x * x))))


def _gelu_grad(x):
    t = jnp.tanh(_GC * (x + _GA * (x * x * x)))
    return 0.5 * (1.0 + t) + 0.5 * x * (1.0 - t * t) * (_GC * (1.0 + 3.0 * _GA * (x * x)))


def _bf(v):
    return v if v.dtype == BF16 else v.astype(BF16)


def _mm(name, grid, ins, in_specs, out_shape, out_specs, pairs, n_acc, acc_tile, nk, epilogue, vmem_bytes):
    n_in, n_out = len(ins), len(out_shape)

    def body(*refs):
        in_refs, out_refs, accs = refs[:n_in], refs[n_in : n_in + n_out], refs[n_in + n_out :]

        def products():
            vals = [None] * n_acc
            for a_i, b_i, dn, acc_i in pairs:
                d = lax.dot_general(_bf(in_refs[a_i][...]), _bf(in_refs[b_i][...]), dn, preferred_element_type=F32)
                vals[acc_i] = d if vals[acc_i] is None else vals[acc_i] + d
            return vals

        if nk == 1:
            epilogue(in_refs, products(), out_refs)
            return
        k = pl.program_id(2)
        vals = products()

        @pl.when(k == 0)
        def _():
            for a, v in zip(accs, vals):
                a[...] = v

        @pl.when(k > 0)
        def _():
            for a, v in zip(accs, vals):
                a[...] += v

        @pl.when(k == nk - 1)
        def _():
            epilogue(in_refs, [a[...] for a in accs], out_refs)

    scratch = [pltpu.VMEM(acc_tile, F32) for _ in range(n_acc)] if nk > 1 else []
    return pl.pallas_call(
        body,
        name=name,
        grid=grid,
        in_specs=in_specs,
        out_specs=out_specs,
        out_shape=out_shape,
        scratch_shapes=scratch,
        compiler_params=_params(vmem_bytes),
    )(*ins)


def _mm_vmem(tiles):
    return sum(_nbytes(s, d) * c for s, d, c in tiles) + 4 * 1024 * 1024


def _store_cast(ref, v):
    ref[...] = v.astype(ref.dtype)


def _rows8(v):
    r, d = v.shape
    return v.reshape(r // 8, 8, d).sum(axis=0)


def _rms_fwd(name, x, g):
    s, d = x.shape
    tm = _tile(s, (256, 128))

    def body(x_ref, g_ref, h_ref):
        xv = x_ref[...]
        r = lax.rsqrt(jnp.mean(xv * xv, axis=-1, keepdims=True) + EPS)
        h_ref[...] = ((xv * r) * g_ref[...]).astype(BF16)

    return pl.pallas_call(
        body,
        name=name,
        grid=(s // tm,),
        in_specs=[pl.BlockSpec((tm, d), lambda i: (i, 0)), pl.BlockSpec((1, d), lambda i: (0, 0))],
        out_specs=pl.BlockSpec((tm, d), lambda i: (i, 0)),
        out_shape=jax.ShapeDtypeStruct((s, d), BF16),
    )(x, g)


def _rms_bwd(name, x, g, dh, dres):
    s, d = x.shape
    tm = _tile(s, (256, 128))
    n = s // tm

    def body(x_ref, g_ref, dh_ref, dres_ref, dx_ref, dxb_ref, dg_ref, acc_ref):
        i = pl.program_id(0)
        xv = x_ref[...]
        r = lax.rsqrt(jnp.mean(xv * xv, axis=-1, keepdims=True) + EPS)
        xh = xv * r
        dhv = dh_ref[...]
        dxh = dhv * g_ref[...]
        dx = r * (dxh - xh * jnp.mean(dxh * xh, axis=-1, keepdims=True)) + dres_ref[...]
        dx_ref[...] = dx
        dxb_ref[...] = dx.astype(BF16)
        part = _rows8(dhv * xh)

        @pl.when(i == 0)
        def _():
            acc_ref[...] = part

        @pl.when(i > 0)
        def _():
            acc_ref[...] += part

        @pl.when(i == n - 1)
        def _():
            dg_ref[...] = jnp.sum(acc_ref[...], axis=0, keepdims=True)

    row = pl.BlockSpec((tm, d), lambda i: (i, 0))
    vec = pl.BlockSpec((1, d), lambda i: (0, 0))
    return pl.pallas_call(
        body,
        name=name,
        grid=(n,),
        in_specs=[row, vec, row, row],
        out_specs=[row, row, vec],
        out_shape=[jax.ShapeDtypeStruct((s, d), F32), jax.ShapeDtypeStruct((s, d), BF16), jax.ShapeDtypeStruct((1, d), F32)],
        scratch_shapes=[pltpu.VMEM((8, d), F32)],
    )(x, g, dh, dres)


def _head(x3, g, target):
    s, d = x3.shape
    tm = _tile(s, (256, 128))
    n = s // tm

    def body(x_ref, g_ref, t_ref, dx_ref, dxb_ref, dg_ref, loss_ref, acc_g, acc_l):
        i = pl.program_id(0)
        xv = x_ref[...]
        gv = g_ref[...]
        r = lax.rsqrt(jnp.mean(xv * xv, axis=-1, keepdims=True) + EPS)
        xh = xv * r
        e = xh * gv - t_ref[...]
        dy = e * (1.0 / d)
        dxh = dy * gv
        dx = r * (dxh - xh * jnp.mean(dxh * xh, axis=-1, keepdims=True))
        dx_ref[...] = dx
        dxb_ref[...] = dx.astype(BF16)
        pg = _rows8(dy * xh)
        plo = _rows8(e * e)

        @pl.when(i == 0)
        def _():
            acc_g[...] = pg
            acc_l[...] = plo

        @pl.when(i > 0)
        def _():
            acc_g[...] += pg
            acc_l[...] += plo

        @pl.when(i == n - 1)
        def _():
            dg_ref[...] = jnp.sum(acc_g[...], axis=0, keepdims=True)
            loss_ref[...] = jnp.full((1, LANES), (0.5 / d) * jnp.sum(acc_l[...]), F32)

    row = pl.BlockSpec((tm, d), lambda i: (i, 0))
    vec = pl.BlockSpec((1, d), lambda i: (0, 0))
    return pl.pallas_call(
        body,
        name="head",
        grid=(n,),
        in_specs=[row, vec, row],
        out_specs=[row, row, vec, pl.BlockSpec((1, LANES), lambda i: (0, 0))],
        out_shape=[
            jax.ShapeDtypeStruct((s, d), F32),
            jax.ShapeDtypeStruct((s, d), BF16),
            jax.ShapeDtypeStruct((1, d), F32),
            jax.ShapeDtypeStruct((1, LANES), F32),
        ],
        scratch_shapes=[pltpu.VMEM((8, d), F32), pltpu.VMEM((8, d), F32)],
    )(x3, g, target)


def _sgu_fwd(z, gain, ws_b, b_col, w_sgu):
    s = z.shape[0]
    groups = ws_b.shape[0]

    def body(zu_ref, zv_ref, gain_ref, ws_ref, b_ref, a_ref):
        vv = _gelu(zv_ref[...])
        r = lax.rsqrt(jnp.mean(vv * vv, axis=-1, keepdims=True) + EPS)
        vn = ((vv * r) * gain_ref[...]).astype(BF16)
        u = _gelu(zu_ref[...])
        for g in range(groups):
            sl = slice(g * BLK, (g + 1) * BLK)
            mixed = jnp.dot(ws_ref[g], vn[:, sl], preferred_element_type=F32) + b_ref[g]
            a_ref[:, sl] = (u[:, sl] * mixed).astype(BF16)

    return pl.pallas_call(
        body,
        name="sgu_fwd",
        grid=(s // BLK,),
        in_specs=[
            pl.BlockSpec((BLK, w_sgu), lambda c: (c, 0)),
            pl.BlockSpec((BLK, w_sgu), lambda c: (c, 1)),
            pl.BlockSpec((1, w_sgu), lambda c: (0, 0)),
            pl.BlockSpec((groups, BLK, BLK), lambda c: (0, 0, 0)),
            pl.BlockSpec((groups, BLK, 1), lambda c: (0, 0, 0)),
        ],
        out_specs=pl.BlockSpec((BLK, w_sgu), lambda c: (c, 0)),
        out_shape=jax.ShapeDtypeStruct((s, w_sgu), BF16),
    )(z, z, gain, ws_b, b_col)


def _sgu_bwd(z, da, gain, ws_b, wst_b, b_col, w_sgu, n_cols):
    s = z.shape[0]
    groups = ws_b.shape[0]
    n = s // BLK

    def body(zu_ref, zv_ref, da_ref, gain_ref, ws_ref, wst_ref, b_ref, dz_ref, dws_ref, dbs_ref, dgain_ref, acc_gain):
        c = pl.program_id(0)
        zu = zu_ref[...]
        zv = zv_ref[...]
        gain_v = gain_ref[...]
        vv = _gelu(zv)
        r = lax.rsqrt(jnp.mean(vv * vv, axis=-1, keepdims=True) + EPS)
        xh = vv * r
        vn = (xh * gain_v).astype(BF16)
        u = _gelu(zu)
        dav = da_ref[...]
        dmix = dav * u
        dmix_b = dmix.astype(BF16)
        dvn_parts = []
        for g in range(groups):
            sl = slice(g * BLK, (g + 1) * BLK)
            mixed = jnp.dot(ws_ref[g], vn[:, sl], preferred_element_type=F32) + b_ref[g]
            dz_ref[:, sl] = (dav[:, sl] * mixed * _gelu_grad(zu[:, sl])).astype(BF16)
            dvn_parts.append(jnp.dot(wst_ref[g], dmix_b[:, sl], preferred_element_type=F32))
            dws_g = lax.dot_general(dmix_b[:, sl], vn[:, sl], NT, preferred_element_type=F32)
            dbs_g = jnp.sum(dmix[:, sl], axis=1, keepdims=True)

            @pl.when(c == 0)
            def _():
                dws_ref[g] = dws_g
                dbs_ref[g] = dbs_g

            @pl.when(c > 0)
            def _():
                dws_ref[g] += dws_g
                dbs_ref[g] += dbs_g

        dvn = jnp.concatenate(dvn_parts, axis=1)
        dxh = dvn * gain_v
        dvv = r * (dxh - xh * jnp.mean(dxh * xh, axis=-1, keepdims=True))
        dz_ref[:, w_sgu:] = (dvv * _gelu_grad(zv)).astype(BF16)
        pg = _rows8(dvn * xh)

        @pl.when(c == 0)
        def _():
            acc_gain[...] = pg

        @pl.when(c > 0)
        def _():
            acc_gain[...] += pg

        @pl.when(c == n - 1)
        def _():
            dgain_ref[...] = jnp.sum(acc_gain[...], axis=0, keepdims=True)

    full3 = pl.BlockSpec((groups, BLK, BLK), lambda c: (0, 0, 0))
    col3 = pl.BlockSpec((groups, BLK, 1), lambda c: (0, 0, 0))
    vec = pl.BlockSpec((1, w_sgu), lambda c: (0, 0))
    return pl.pallas_call(
        body,
        name="sgu_bwd",
        grid=(n,),
        in_specs=[
            pl.BlockSpec((BLK, w_sgu), lambda c: (c, 0)),
            pl.BlockSpec((BLK, w_sgu), lambda c: (c, 1)),
            pl.BlockSpec((BLK, w_sgu), lambda c: (c, 0)),
            vec,
            full3,
            full3,
            col3,
        ],
        out_specs=[pl.BlockSpec((BLK, 2 * w_sgu), lambda c: (c, 0)), full3, col3, vec],
        out_shape=[
            jax.ShapeDtypeStruct((s, n_cols), BF16),
            jax.ShapeDtypeStruct((groups, BLK, BLK), F32),
            jax.ShapeDtypeStruct((groups, BLK, 1), F32),
            jax.ShapeDtypeStruct((1, w_sgu), F32),
        ],
        scratch_shapes=[pltpu.VMEM((8, w_sgu), F32)],
    )(z, z, da, gain, ws_b, wst_b, b_col)


def _attn_softmax(sink_ref, q_ref, k_ref, v_ref, bias_ref, s_len, grp):
    kv = pl.program_id(0)
    n = pl.program_id(1)
    start = pl.multiple_of(n * BLK, BLK)
    kb = k_ref[pl.ds(start, 3 * BLK), :]
    vb = v_ref[pl.ds(start, 3 * BLK), :]
    qv = q_ref[...]
    qs = jnp.concatenate([qv[:, g * HEAD_DIM : (g + 1) * HEAD_DIM] for g in range(grp)], axis=0).astype(BF16)
    sc = lax.dot_general(qs, kb, NT, preferred_element_type=F32) * (HEAD_DIM**-0.5)
    sc = sc + bias_ref[...].reshape(grp * BLK, 3 * BLK)
    kpos = start + lax.broadcasted_iota(I32, (1, 3 * BLK), 1) - BLK
    sc = jnp.where((kpos >= 0) & (kpos < s_len), sc, NEG)
    sink = jnp.concatenate([jnp.full((BLK, 1), sink_ref[kv * grp + g], F32) for g in range(grp)], axis=0)
    m = jnp.maximum(jnp.max(sc, axis=-1, keepdims=True), sink)
    p = jnp.exp(sc - m)
    esink = jnp.exp(sink - m)
    den = jnp.sum(p, axis=-1, keepdims=True) + esink
    return start, qs, kb, vb, p / den, esink / den


def _attn_specs(s, grp, q_blk0):
    qw = grp * HEAD_DIM
    return [
        pl.BlockSpec(memory_space=pltpu.SMEM),
        pl.BlockSpec((BLK, qw), lambda kv, n: (n, q_blk0 + kv)),
        pl.BlockSpec((s + 2 * BLK, HEAD_DIM), lambda kv, n: (0, kv)),
        pl.BlockSpec((s + 2 * BLK, HEAD_DIM), lambda kv, n: (0, kv)),
        pl.BlockSpec((grp, BLK, 3 * BLK), lambda kv, n: (kv, 0, 0)),
    ]


def _attn_fwd(sink, z, k_pad, v_pad, bias_tab, grp, q_blk0):
    s = z.shape[0]
    qw = grp * HEAD_DIM

    def body(sink_ref, q_ref, k_ref, v_ref, bias_ref, o_ref):
        _, _, _, vb, pn, _ = _attn_softmax(sink_ref, q_ref, k_ref, v_ref, bias_ref, s, grp)
        o = jnp.dot(pn.astype(BF16), vb, preferred_element_type=F32)
        for g in range(grp):
            o_ref[:, g * HEAD_DIM : (g + 1) * HEAD_DIM] = o[g * BLK : (g + 1) * BLK].astype(BF16)

    return pl.pallas_call(
        body,
        name="attn_fwd",
        grid=(N_KV_HEADS, s // BLK),
        in_specs=_attn_specs(s, grp, q_blk0),
        out_specs=pl.BlockSpec((BLK, qw), lambda kv, n: (n, kv)),
        out_shape=jax.ShapeDtypeStruct((s, N_KV_HEADS * qw), BF16),
    )(sink, z, k_pad, v_pad, bias_tab)


def _attn_bwd(sink, z, k_pad, v_pad, bias_tab, dout, dz, grp, q_blk0):
    s = z.shape[0]
    qw = grp * HEAD_DIM
    nb = s // BLK
    heads = N_KV_HEADS * grp

    def body(sink_ref, q_ref, k_ref, v_ref, bias_ref, do_ref, dz_in, dq_ref, dk_ref, dv_ref, dbias_ref, dsink_ref, dk_acc, dv_acc):
        del dz_in
        kv = pl.program_id(0)
        n = pl.program_id(1)
        start, qs, kb, vb, pn, psink = _attn_softmax(sink_ref, q_ref, k_ref, v_ref, bias_ref, s, grp)
        dov = do_ref[...]
        dos = jnp.concatenate([dov[:, g * HEAD_DIM : (g + 1) * HEAD_DIM] for g in range(grp)], axis=0)
        dp = lax.dot_general(dos, vb, NT, preferred_element_type=F32)
        dvb = lax.dot_general(pn.astype(BF16), dos, TN, preferred_element_type=F32)
        delta = jnp.sum(pn * dp, axis=-1, keepdims=True)
        ds = pn * (dp - delta)
        dsb = (ds * (HEAD_DIM**-0.5)).astype(BF16)
        dq = jnp.dot(dsb, kb, preferred_element_type=F32)
        dkb = lax.dot_general(dsb, qs, TN, preferred_element_type=F32)
        for g in range(grp):
            dq_ref[:, g * HEAD_DIM : (g + 1) * HEAD_DIM] = dq[g * BLK : (g + 1) * BLK].astype(BF16)

        @pl.when(n == 0)
        def _():
            dk_acc[...] = jnp.zeros_like(dk_acc)
            dv_acc[...] = jnp.zeros_like(dv_acc)
            dbias_ref[...] = jnp.zeros_like(dbias_ref)

        @pl.when((n == 0) & (kv == 0))
        def _():
            dsink_ref[...] = jnp.zeros_like(dsink_ref)

        dk_acc[pl.ds(start, 3 * BLK), :] += dkb
        dv_acc[pl.ds(start, 3 * BLK), :] += dvb
        dbias_ref[...] += ds.reshape(grp, BLK, 3 * BLK)
        row = lax.broadcasted_iota(I32, (heads, LANES), 0)
        sd = psink * delta
        upd = jnp.zeros((heads, LANES), F32)
        for g in range(grp):
            upd = jnp.where(row == kv * grp + g, -jnp.sum(sd[g * BLK : (g + 1) * BLK]), upd)
        dsink_ref[...] += upd

        @pl.when(n == nb - 1)
        def _():
            dk_ref[...] = dk_acc[...]
            dv_ref[...] = dv_acc[...]

    pad_spec = pl.BlockSpec((s + 2 * BLK, HEAD_DIM), lambda kv, n: (0, kv))
    kvw = N_KV_HEADS * HEAD_DIM
    n_in = 7
    return pl.pallas_call(
        body,
        name="attn_bwd",
        grid=(N_KV_HEADS, nb),
        in_specs=_attn_specs(s, grp, q_blk0) + [pl.BlockSpec((BLK, qw), lambda kv, n: (n, kv)), ANY],
        out_specs=[
            pl.BlockSpec((BLK, qw), lambda kv, n: (n, q_blk0 + kv)),
            pad_spec,
            pad_spec,
            pl.BlockSpec((grp, BLK, 3 * BLK), lambda kv, n: (kv, 0, 0)),
            pl.BlockSpec((heads, LANES), lambda kv, n: (0, 0)),
        ],
        out_shape=[
            jax.ShapeDtypeStruct(dz.shape, BF16),
            jax.ShapeDtypeStruct((s + 2 * BLK, kvw), F32),
            jax.ShapeDtypeStruct((s + 2 * BLK, kvw), F32),
            jax.ShapeDtypeStruct((heads, BLK, 3 * BLK), F32),
            jax.ShapeDtypeStruct((heads, LANES), F32),
        ],
        scratch_shapes=[pltpu.VMEM((s + 2 * BLK, HEAD_DIM), F32), pltpu.VMEM((s + 2 * BLK, HEAD_DIM), F32)],
        input_output_aliases={n_in - 1: 0},
    )(sink, z, k_pad, v_pad, bias_tab, dout, dz)


def _dkv_to_dz(dk_pad, dv_pad, dz, blk_idx):
    s = dz.shape[0]
    kvw = dk_pad.shape[1]

    def body(dk_ref, dv_ref, dz_in, out_ref):
        del dz_in
        out_ref[:, :kvw] = dk_ref[...].astype(BF16)
        out_ref[:, kvw:] = dv_ref[...].astype(BF16)

    src = pl.BlockSpec((BLK, kvw), lambda i: (i + 1, 0))
    return pl.pallas_call(
        body,
        name="dkv_to_dz",
        grid=(s // BLK,),
        in_specs=[src, src, ANY],
        out_specs=pl.BlockSpec((BLK, 2 * kvw), lambda i: (i, blk_idx)),
        out_shape=jax.ShapeDtypeStruct(dz.shape, BF16),
        input_output_aliases={2: 0},
    )(dk_pad, dv_pad, dz)


def _relbias_bwd(dbias_tab, bucket):
    heads = dbias_tab.shape[0]

    def body(dt_ref, bk_ref, out_ref):
        lane = lax.broadcasted_iota(I32, (1, LANES), 1)
        bk = bk_ref[...]
        rows = []
        for h in range(heads):
            dt = dt_ref[h]
            acc = jnp.zeros((1, LANES), F32)
            for b in range(REL_BUCKETS):
                acc = jnp.where(lane == b, jnp.sum(jnp.where(bk == b, dt, 0.0)), acc)
            rows.append(acc)
        out_ref[...] = jnp.concatenate(rows, axis=0)

    return pl.pallas_call(body, name="relbias_bwd", out_shape=jax.ShapeDtypeStruct((heads, LANES), F32))(dbias_tab, bucket)


def _t5_bucket(rel):
    nb = REL_BUCKETS // 2
    ret = jnp.where(rel > 0, nb, 0)
    n = jnp.abs(rel)
    max_exact = nb // 2
    nf = jnp.maximum(n, 1).astype(F32)
    large = max_exact + (jnp.log(nf / max_exact) / math.log(REL_MAX_DIST / max_exact) * (nb - max_exact)).astype(I32)
    large = jnp.minimum(large, nb - 1)
    return ret + jnp.where(n < max_exact, n, large)


def _band_tables(rel_bias):
    qi = jnp.arange(BLK)[:, None]
    kj = jnp.arange(3 * BLK)[None, :]
    rel = kj - BLK - qi
    bucket = _t5_bucket(rel).astype(I32)
    bias = jnp.transpose(rel_bias.astype(F32)[bucket], (2, 0, 1))
    return jnp.where((jnp.abs(rel) <= BLK)[None], bias, NEG), bucket


def _ew_tiles(shape):
    r, c = shape
    tn = c if c <= 2048 else _tile(c, (2048, 1920, 1536, 1408, 1024, 512))
    tm = _tile(r, (128, 64, 32, 16, 8))
    return tm, tn


def _cast_bf16(name, w):
    tm, tn = _ew_tiles(w.shape)
    spec = pl.BlockSpec((tm, tn), lambda i, j: (i, j))

    def body(w_ref, o_ref):
        o_ref[...] = w_ref[...].astype(BF16)

    return pl.pallas_call(
        body, name=name, grid=(w.shape[0] // tm, w.shape[1] // tn), in_specs=[spec], out_specs=spec,
        out_shape=jax.ShapeDtypeStruct(w.shape, BF16),
    )(w)


def _adamw(name, w, g, m, v):
    tm, tn = _ew_tiles(w.shape)
    spec = pl.BlockSpec((tm, tn), lambda i, j: (i, j))

    def body(w_ref, g_ref, m_ref, v_ref, d_ref, nm_ref, nv_ref):
        gv = g_ref[...]
        nm = ADAM_B1 * m_ref[...] + (1.0 - ADAM_B1) * gv
        nv = ADAM_B2 * v_ref[...] + (1.0 - ADAM_B2) * (gv * gv)
        m_hat = nm / (1.0 - ADAM_B1**ADAM_STEP)
        v_hat = nv / (1.0 - ADAM_B2**ADAM_STEP)
        d_ref[...] = -ADAM_LR * (m_hat / (jnp.sqrt(v_hat) + ADAM_EPS) + ADAM_WD * w_ref[...])
        nm_ref[...] = nm
        nv_ref[...] = nv

    out = jax.ShapeDtypeStruct(w.shape, F32)
    return pl.pallas_call(
        body, name=name, grid=(w.shape[0] // tm, w.shape[1] // tn), in_specs=[spec] * 4, out_specs=[spec] * 3,
        out_shape=[out, out, out],
    )(w, g, m, v)


def _pair_add(name, cidx, g_full, r_sib, kind):
    hr, hc = r_sib.shape
    tm, tn = _ew_tiles((hr, hc))
    nbi, nbj = hr // tm, hc // tn
    if kind == "col":
        g_spec = pl.BlockSpec((tm, tn), lambda i, j, c: (c[0] * nbi + i, j))
    else:
        g_spec = pl.BlockSpec((tm, tn), lambda i, j, c: (i, c[0] * nbj + j))
    spec = pl.BlockSpec((tm, tn), lambda i, j, c: (i, j))

    def body(c_ref, g_ref, r_ref, o_ref):
        del c_ref
        o_ref[...] = (g_ref[...].astype(F32) + r_ref[...].astype(F32)).astype(BF16)

    return pl.pallas_call(
        body,
        name=name,
        grid_spec=pltpu.PrefetchScalarGridSpec(num_scalar_prefetch=1, grid=(nbi, nbj), in_specs=[g_spec, spec], out_specs=spec),
        out_shape=jax.ShapeDtypeStruct((hr, hc), BF16),
    )(cidx, g_full, r_sib)


def _chip_sum(name, qidx, c_half, r_ici, kind):
    _, pr, pc = r_ici.shape
    tm, tn = _ew_tiles((pr, pc))
    nbi, nbj = pr // tm, pc // tn
    if kind == "col":
        own_spec = pl.BlockSpec((tm, tn), lambda i, j, q: (i, q[0] * nbj + j))
    else:
        own_spec = pl.BlockSpec((tm, tn), lambda i, j, q: (q[0] * nbi + i, j))

    def body(q_ref, own_ref, r_ref, o_ref):
        q = q_ref[0]
        own = own_ref[...].astype(F32)
        recv = [r_ref[r].astype(F32) for r in range(3)]
        total = None
        for chip in range(N_CHIPS):
            d = chip ^ q
            term = jnp.where(d == 0, own, jnp.where(d == 2, recv[0], jnp.where(d == 1, recv[1], recv[2])))
            total = term if total is None else total + term
        o_ref[...] = total

    return pl.pallas_call(
        body,
        name=name,
        grid_spec=pltpu.PrefetchScalarGridSpec(
            num_scalar_prefetch=1,
            grid=(nbi, nbj),
            in_specs=[own_spec, pl.BlockSpec((3, tm, tn), lambda i, j, q: (0, i, j))],
            out_specs=pl.BlockSpec((tm, tn), lambda i, j, q: (i, j)),
        ),
        out_shape=jax.ShapeDtypeStruct((pr, pc), F32),
    )(qidx, c_half, r_ici)


_REL_MASK = (2, 1, 3)


def _place():
    x, y, c = lax.axis_index("x"), lax.axis_index("y"), lax.axis_index("c")
    chips = [(1 - x, y), (x, 1 - y), (1 - x, 1 - y)]
    return x, y, c, 2 * x + y, chips


def _shard_view(ref, kind, chip):
    if kind == "col":
        w = ref.shape[1] // N_CHIPS
        return ref.at[:, pl.ds(pl.multiple_of(chip * w, LANES), w)]
    h = ref.shape[0] // N_CHIPS
    return ref.at[pl.ds(pl.multiple_of(chip * h, 16), h), :]


def _row_half(ref, half):
    h = ref.shape[0] // 2
    return ref.at[pl.ds(pl.multiple_of(half * h, 16), h), :]


def _pair_half(ref, kind, half):
    if kind == "col":
        return _row_half(ref, half)
    w = ref.shape[1] // 2
    return ref.at[:, pl.ds(pl.multiple_of(half * w, LANES), w)]


def _remote(src, dst, send_sem, recv_sem, dev):
    return pltpu.make_async_remote_copy(src_ref=src, dst_ref=dst, send_sem=send_sem, recv_sem=recv_sem, device_id=dev, device_id_type=MESH)


def _all_gather(shards, kinds):
    n_w = len(shards)
    fulls = []
    for s, kind in zip(shards, kinds):
        r, c = s.shape
        fulls.append(jax.ShapeDtypeStruct((r, c * N_CHIPS) if kind == "col" else (r * N_CHIPS, c), BF16))

    def body(*refs):
        s_refs, g_refs = refs[:n_w], refs[n_w : 2 * n_w]
        loc_sem, ici_send, ici_recv, d2d_send, d2d_recv = refs[2 * n_w :]
        x, y, c, q, chips = _place()
        sib = (x, y, 1 - c)
        local = [pltpu.make_async_copy(s_refs[w], _shard_view(g_refs[w], kinds[w], q), loc_sem.at[w]) for w in range(n_w)]
        for cp in local:
            cp.start()
        sends = []
        for w in range(n_w):
            mine = _row_half(_shard_view(g_refs[w], kinds[w], q), c)
            for r, chip in enumerate(chips):
                sends.append(_remote(_row_half(s_refs[w], c), mine, ici_send.at[w, r], ici_recv.at[w, r], (*chip, c)))
        for cp in sends:
            cp.start()
        passed = []
        for w in range(n_w):
            for r in range(3):
                landed = _row_half(_shard_view(g_refs[w], kinds[w], q ^ _REL_MASK[r]), c)
                _remote(landed, landed, ici_send.at[w, r], ici_recv.at[w, r], sib).wait_recv()
                fwd = _remote(landed, landed, d2d_send.at[w, r], d2d_recv.at[w, r], sib)
                fwd.start()
                passed.append(fwd)
        for w in range(n_w):
            for r in range(3):
                other = _row_half(_shard_view(g_refs[w], kinds[w], q ^ _REL_MASK[r]), 1 - c)
                _remote(other, other, d2d_send.at[w, r], d2d_recv.at[w, r], sib).wait_recv()
        for cp in sends + passed:
            cp.wait_send()
        for cp in local:
            cp.wait()

    return pl.pallas_call(
        body,
        name="all_gather_weights",
        in_specs=[ANY] * n_w,
        out_specs=[ANY] * n_w,
        out_shape=fulls,
        scratch_shapes=[
            pltpu.SemaphoreType.DMA((n_w,)),
            pltpu.SemaphoreType.DMA((n_w, 3)),
            pltpu.SemaphoreType.DMA((n_w, 3)),
            pltpu.SemaphoreType.DMA((n_w, 3)),
            pltpu.SemaphoreType.DMA((n_w, 3)),
        ],
        compiler_params=_params(has_side_effects=True),
    )(*shards)


def _pair_exchange(grads, kinds):
    n_w = len(grads)
    outs = []
    for g, kind in zip(grads, kinds):
        r, c = g.shape
        outs.append(jax.ShapeDtypeStruct((r // 2, c) if kind == "col" else (r, c // 2), BF16))

    def body(*refs):
        g_refs, o_refs = refs[:n_w], refs[n_w : 2 * n_w]
        send, recv = refs[2 * n_w :]
        x, y, c, _, _ = _place()
        sib = (x, y, 1 - c)
        cps = [_remote(_pair_half(g_refs[w], kinds[w], 1 - c), o_refs[w], send.at[w], recv.at[w], sib) for w in range(n_w)]
        for cp in cps:
            cp.start()
        for cp in cps:
            cp.wait()

    return pl.pallas_call(
        body,
        name="grad_pair_exchange",
        in_specs=[ANY] * n_w,
        out_specs=[ANY] * n_w,
        out_shape=outs,
        scratch_shapes=[pltpu.SemaphoreType.DMA((n_w,)), pltpu.SemaphoreType.DMA((n_w,))],
        compiler_params=_params(has_side_effects=True),
    )(*grads)


def _chip_exchange(halves, kinds):
    n_w = len(halves)
    outs = []
    for h, kind in zip(halves, kinds):
        r, c = h.shape
        outs.append(jax.ShapeDtypeStruct((3, r, c // N_CHIPS) if kind == "col" else (3, r // N_CHIPS, c), BF16))

    def body(*refs):
        h_refs, o_refs = refs[:n_w], refs[n_w : 2 * n_w]
        send, recv = refs[2 * n_w :]
        _, _, c, q, chips = _place()
        cps = []
        for w in range(n_w):
            for r, chip in enumerate(chips):
                piece = _shard_view(h_refs[w], kinds[w], q ^ _REL_MASK[r])
                cps.append(_remote(piece, o_refs[w].at[r], send.at[w, r], recv.at[w, r], (*chip, c)))
        for cp in cps:
            cp.start()
        for cp in cps:
            cp.wait()

    return pl.pallas_call(
        body,
        name="grad_chip_exchange",
        in_specs=[ANY] * n_w,
        out_specs=[ANY] * n_w,
        out_shape=outs,
        scratch_shapes=[pltpu.SemaphoreType.DMA((n_w, 3)), pltpu.SemaphoreType.DMA((n_w, 3))],
        compiler_params=_params(has_side_effects=True),
    )(*halves)


def _pair_share(pieces, kinds):
    n_w = len(pieces)
    outs = []
    for p, kind in zip(pieces, kinds):
        r, c = p.shape
        outs.append(jax.ShapeDtypeStruct((r * 2, c) if kind == "col" else (r, c * 2), F32))

    def body(*refs):
        p_refs, o_refs = refs[:n_w], refs[n_w : 2 * n_w]
        loc_sem, send, recv = refs[2 * n_w :]
        x, y, c, _, _ = _place()
        sib = (x, y, 1 - c)
        local, cps = [], []
        for w in range(n_w):
            mine = _pair_half(o_refs[w], kinds[w], c)
            local.append(pltpu.make_async_copy(p_refs[w], mine, loc_sem.at[w]))
            cps.append(_remote(p_refs[w], mine, send.at[w], recv.at[w], sib))
        for cp in local + cps:
            cp.start()
        for w in range(n_w):
            other = _pair_half(o_refs[w], kinds[w], 1 - c)
            _remote(other, other, send.at[w], recv.at[w], sib).wait_recv()
        for cp in cps:
            cp.wait_send()
        for cp in local:
            cp.wait()

    return pl.pallas_call(
        body,
        name="grad_pair_share",
        in_specs=[ANY] * n_w,
        out_specs=[ANY] * n_w,
        out_shape=outs,
        scratch_shapes=[pltpu.SemaphoreType.DMA((n_w,)), pltpu.SemaphoreType.DMA((n_w,)), pltpu.SemaphoreType.DMA((n_w,))],
        compiler_params=_params(has_side_effects=True),
    )(*pieces)


def _small_all_reduce(p):
    rows = p.shape[0]
    n_dev = 2 * N_CHIPS

    def body(p_ref, o_ref, buf, loc_sem, send, recv):
        x, y, c, q, _ = _place()
        me = 2 * q + c
        own = pltpu.make_async_copy(p_ref, buf.at[me], loc_sem)
        own.start()
        cps = []
        for d in range(1, n_dev):
            dev = (x ^ ((d >> 2) & 1), y ^ ((d >> 1) & 1), c ^ (d & 1))
            cps.append(_remote(p_ref, buf.at[me], send.at[d - 1], recv.at[d - 1], dev))
        for cp in cps:
            cp.start()
        for d in range(1, n_dev):
            slot = buf.at[me ^ d]
            _remote(slot, slot, send.at[d - 1], recv.at[d - 1], (x, y, c)).wait_recv()
        own.wait()
        total = buf[0]
        for d in range(1, n_dev):
            total = total + buf[d]
        o_ref[...] = total
        for cp in cps:
            cp.wait_send()

    return pl.pallas_call(
        body,
        name="small_all_reduce",
        in_specs=[ANY],
        out_specs=pl.BlockSpec(memory_space=pltpu.VMEM),
        out_shape=jax.ShapeDtypeStruct(p.shape, F32),
        scratch_shapes=[
            pltpu.VMEM((n_dev, rows, LANES), F32),
            pltpu.SemaphoreType.DMA,
            pltpu.SemaphoreType.DMA((n_dev - 1,)),
            pltpu.SemaphoreType.DMA((n_dev - 1,)),
        ],
        compiler_params=_params(has_side_effects=True),
    )(p)


def _pack(parts):
    rows = []
    for a in parts:
        flat = a.reshape(-1).astype(F32)
        n = flat.shape[0]
        padded = -(-n // (8 * LANES)) * (8 * LANES)
        rows.append(jnp.pad(flat, (0, padded - n)).reshape(-1, LANES))
    return jnp.concatenate(rows, axis=0)


def _unpack(packed, shapes):
    out, row = [], 0
    for shp in shapes:
        n = int(np.prod(shp))
        nrows = -(-n // (8 * LANES)) * 8
        out.append(packed[row : row + nrows].reshape(-1)[:n].reshape(shp))
        row += nrows
    return out


def kernel(x, w_in, norm_mix, sgu_v_gain, sgu_w_s, sgu_b_s, w_a_out, attn_sink, rel_bias, w_b_out, w_o, norm_ffn, w_gate, w_up, w_down, norm_final, loss_target, m_w_in, m_norm_mix, m_sgu_v_gain, m_sgu_w_s, m_sgu_b_s, m_w_a_out, m_attn_sink, m_rel_bias, m_w_b_out, m_w_o, m_norm_ffn, m_w_gate, m_w_up, m_w_down, m_norm_final, v_w_in, v_norm_mix, v_sgu_v_gain, v_sgu_w_s, v_sgu_b_s, v_w_a_out, v_attn_sink, v_rel_bias, v_w_b_out, v_w_o, v_norm_ffn, v_w_gate, v_w_up, v_w_down, v_norm_final):
    s, d = x.shape[1], x.shape[2]
    w_sgu = sgu_v_gain.shape[1]
    groups = sgu_w_s.shape[1]
    heads = attn_sink.shape[1]
    grp = heads // N_KV_HEADS
    w_att = heads * HEAD_DIM
    w_kv = N_KV_HEADS * HEAD_DIM
    d_ff = w_gate.shape[2] * N_CHIPS
    n_in = w_in.shape[2] * N_CHIPS
    off_q = 2 * w_sgu
    off_k = off_q + w_att
    off_g = off_k + 2 * w_kv
    assert n_in == off_g + 2 * d and groups * BLK == w_sgu and s % BLK == 0

    x2d = x.reshape(s, d)
    tgt = loss_target.reshape(s, d)
    c_idx = lax.axis_index("c").astype(I32).reshape(1)
    q_idx = (2 * lax.axis_index("x") + lax.axis_index("y")).astype(I32).reshape(1)

    big = [("w_in", w_in[0], "col"), ("w_a", w_a_out[0], "col"), ("w_b", w_b_out[0], "col"), ("w_o", w_o[0], "row"),
           ("w_gate", w_gate[0], "col"), ("w_up", w_up[0], "col"), ("w_down", w_down[0], "row")]
    kinds = [k for _, _, k in big]
    shards_b = [_cast_bf16("cast_" + n, w) for n, w, _ in big]
    g_in, g_a, g_b, g_o, g_gate, g_up, g_down = _all_gather(shards_b, kinds)

    ws_b = sgu_w_s[0].astype(BF16)
    wst_b = jnp.swapaxes(sgu_w_s[0], 1, 2).astype(BF16)
    b_col = sgu_b_s[0].reshape(groups, BLK, 1)
    bias_tab, bucket = _band_tables(rel_bias)
    sink = attn_sink[0]

    tm = _tile(s, (1024, 512, 256, 128))

    h1 = _rms_fwd("rms_mix", x2d, norm_mix)

    tn = _tile(n_in, (768, 640, 512))
    z = _mm(
        "mm_z", (s // tm, n_in // tn, 1), [h1, g_in],
        [pl.BlockSpec((tm, d), lambda i, j, k: (i, 0)), pl.BlockSpec((d, tn), lambda i, j, k: (0, j))],
        [jax.ShapeDtypeStruct((s, n_in), F32)], [pl.BlockSpec((tm, tn), lambda i, j, k: (i, j))],
        [(0, 1, NN, 0)], 1, (tm, tn), 1, lambda ins, vals, outs: _store_cast(outs[0], vals[0]),
        _mm_vmem([((tm, d), BF16, 2), ((d, tn), BF16, 2), ((tm, tn), F32, 3)]),
    )[0]

    a_act = _sgu_fwd(z, sgu_v_gain, ws_b, b_col, w_sgu)

    kv_b = z[:, off_k:off_g].astype(BF16)
    k_pad = jnp.pad(kv_b[:, :w_kv], ((BLK, BLK), (0, 0)))
    v_pad = jnp.pad(kv_b[:, w_kv:], ((BLK, BLK), (0, 0)))
    q_blk0 = off_q // (grp * HEAD_DIM)
    att = _attn_fwd(sink, z, k_pad, v_pad, bias_tab, grp, q_blk0)

    tg = _tile(d, (512,))
    ga0, gb0 = off_g // tg, (off_g + d) // tg

    def ep_gate(ins, vals, outs):
        sa, sb = _sigmoid(ins[4][...]), _sigmoid(ins[5][...])
        outs[0][...] = (sa * vals[0] + sb * vals[1]).astype(BF16)
        outs[1][...] = vals[0]
        outs[2][...] = vals[1]

    t_out = pl.BlockSpec((tm, tg), lambda i, j, k: (i, j))
    m_act, y_a, y_b = _mm(
        "mm_branches", (s // tm, d // tg, 1), [a_act, g_a, att, g_b, z, z],
        [pl.BlockSpec((tm, w_sgu), lambda i, j, k: (i, 0)), pl.BlockSpec((w_sgu, tg), lambda i, j, k: (0, j)),
         pl.BlockSpec((tm, w_att), lambda i, j, k: (i, 0)), pl.BlockSpec((w_att, tg), lambda i, j, k: (0, j)),
         pl.BlockSpec((tm, tg), lambda i, j, k: (i, ga0 + j)), pl.BlockSpec((tm, tg), lambda i, j, k: (i, gb0 + j))],
        [jax.ShapeDtypeStruct((s, d), BF16), jax.ShapeDtypeStruct((s, d), F32), jax.ShapeDtypeStruct((s, d), F32)],
        [t_out, t_out, t_out], [(0, 1, NN, 0), (2, 3, NN, 1)], 2, (tm, tg), 1, ep_gate,
        _mm_vmem([((tm, w_sgu), BF16, 4), ((w_sgu, tg), BF16, 4), ((tm, tg), F32, 12)]),
    )

    tn = _tile(d, (1024, 512))

    def ep_residual(ins, vals, outs):
        outs[0][...] = ins[2][...] + vals[0]

    x2 = _mm(
        "mm_wo", (s // tm, d // tn, 1), [m_act, g_o, x2d],
        [pl.BlockSpec((tm, d), lambda i, j, k: (i, 0)), pl.BlockSpec((d, tn), lambda i, j, k: (0, j)),
         pl.BlockSpec((tm, tn), lambda i, j, k: (i, j))],
        [jax.ShapeDtypeStruct((s, d), F32)], [pl.BlockSpec((tm, tn), lambda i, j, k: (i, j))],
        [(0, 1, NN, 0)], 1, (tm, tn), 1, ep_residual,
        _mm_vmem([((tm, d), BF16, 2), ((d, tn), BF16, 2), ((tm, tn), F32, 5)]),
    )[0]

    h2 = _rms_fwd("rms_ffn", x2, norm_ffn)

    tf = _tile(d_ff, (512,))

    def ep_swiglu(ins, vals, outs):
        gt, up = vals
        outs[0][...] = gt
        outs[1][...] = up
        outs[2][...] = ((gt * _sigmoid(gt)) * up).astype(BF16)

    f_out = pl.BlockSpec((tm, tf), lambda i, j, k: (i, j))
    gt, up, f_act = _mm(
        "mm_gate_up", (s // tm, d_ff // tf, 1), [h2, g_gate, g_up],
        [pl.BlockSpec((tm, d), lambda i, j, k: (i, 0)), pl.BlockSpec((d, tf), lambda i, j, k: (0, j)),
         pl.BlockSpec((d, tf), lambda i, j, k: (0, j))],
        [jax.ShapeDtypeStruct((s, d_ff), F32), jax.ShapeDtypeStruct((s, d_ff), F32), jax.ShapeDtypeStruct((s, d_ff), BF16)],
        [f_out, f_out, f_out], [(0, 1, NN, 0), (0, 2, NN, 1)], 2, (tm, tf), 1, ep_swiglu,
        _mm_vmem([((tm, d), BF16, 2), ((d, tf), BF16, 4), ((tm, tf), F32, 8)]),
    )

    tkf = _tile(d_ff, (1408, 1024, 512))
    nkf = d_ff // tkf
    x3 = _mm(
        "mm_down", (s // tm, d // tn, nkf), [f_act, g_down, x2],
        [pl.BlockSpec((tm, tkf), lambda i, j, k: (i, k)), pl.BlockSpec((tkf, tn), lambda i, j, k: (k, j)),
         pl.BlockSpec((tm, tn), lambda i, j, k: (i, j))],
        [jax.ShapeDtypeStruct((s, d), F32)], [pl.BlockSpec((tm, tn), lambda i, j, k: (i, j))],
        [(0, 1, NN, 0)], 1, (tm, tn), nkf, ep_residual,
        _mm_vmem([((tm, tkf), BF16, 2), ((tkf, tn), BF16, 2), ((tm, tn), F32, 6)]),
    )[0]

    dx3, dx3b, dg_final, loss_part = _head(x3, norm_final.reshape(1, d), tgt)

    def ep_swiglu_bwd(ins, vals, outs):
        df = vals[0]
        gtv, upv = ins[2][...], ins[3][...]
        sg = _sigmoid(gtv)
        outs[0][...] = (df * upv * (sg + gtv * sg * (1.0 - sg))).astype(BF16)
        outs[1][...] = (df * (gtv * sg)).astype(BF16)

    dgt, dup = _mm(
        "mm_dswiglu", (s // tm, d_ff // tf, 1), [dx3b, g_down, gt, up],
        [pl.BlockSpec((tm, d), lambda i, j, k: (i, 0)), pl.BlockSpec((tf, d), lambda i, j, k: (j, 0)), f_out, f_out],
        [jax.ShapeDtypeStruct((s, d_ff), BF16), jax.ShapeDtypeStruct((s, d_ff), BF16)], [f_out, f_out],
        [(0, 1, NT, 0)], 1, (tm, tf), 1, ep_swiglu_bwd,
        _mm_vmem([((tm, d), BF16, 2), ((tf, d), BF16, 2), ((tm, tf), F32, 8)]),
    )

    def ep_bf16(ins, vals, outs):
        for o, v in zip(outs, vals):
            o[...] = v.astype(BF16)

    def ep_f32(ins, vals, outs):
        for o, v in zip(outs, vals):
            o[...] = v

    twn = _tile(d, (1024, 512))
    gw_down = _mm(
        "mm_gw_down", (d_ff // tkf, d // twn, 1), [f_act, dx3b],
        [pl.BlockSpec((s, tkf), lambda i, j, k: (0, i)), pl.BlockSpec((s, twn), lambda i, j, k: (0, j))],
        [jax.ShapeDtypeStruct((d_ff, d), BF16)], [pl.BlockSpec((tkf, twn), lambda i, j, k: (i, j))],
        [(0, 1, TN, 0)], 1, (tkf, twn), 1, ep_bf16,
        _mm_vmem([((s, tkf), BF16, 3), ((s, twn), BF16, 2), ((tkf, twn), F32, 3)]),
    )[0]

    dh2 = _mm(
        "mm_dh2", (s // tm, d // tn, nkf), [dgt, g_gate, dup, g_up],
        [pl.BlockSpec((tm, tkf), lambda i, j, k: (i, k)), pl.BlockSpec((tn, tkf), lambda i, j, k: (j, k)),
         pl.BlockSpec((tm, tkf), lambda i, j, k: (i, k)), pl.BlockSpec((tn, tkf), lambda i, j, k: (j, k))],
        [jax.ShapeDtypeStruct((s, d), F32)], [pl.BlockSpec((tm, tn), lambda i, j, k: (i, j))],
        [(0, 1, NT, 0), (2, 3, NT, 0)], 1, (tm, tn), nkf, ep_f32,
        _mm_vmem([((tm, tkf), BF16, 4), ((tn, tkf), BF16, 4), ((tm, tn), F32, 5)]),
    )[0]

    twr = _tile(d, (1024, 512))
    w_tile = pl.BlockSpec((twr, tf), lambda i, j, k: (i, j))
    gw_gate, gw_up = _mm(
        "mm_gw_gate_up", (d // twr, d_ff // tf, 1), [h2, dgt, dup],
        [pl.BlockSpec((s, twr), lambda i, j, k: (0, i)), pl.BlockSpec((s, tf), lambda i, j, k: (0, j)),
         pl.BlockSpec((s, tf), lambda i, j, k: (0, j))],
        [jax.ShapeDtypeStruct((d, d_ff), BF16), jax.ShapeDtypeStruct((d, d_ff), BF16)], [w_tile, w_tile],
        [(0, 1, TN, 0), (0, 2, TN, 1)], 2, (twr, tf), 1, ep_bf16,
        _mm_vmem([((s, twr), BF16, 3), ((s, tf), BF16, 4), ((twr, tf), F32, 6)]),
    )

    dx2, dx2b, dg_ffn = _rms_bwd("rms_ffn_bwd", x2, norm_ffn, dh2, dx3)

    nj = d // tg

    def gate_bwd_body(dx_ref, wo_ref, ga_ref, gb_ref, ya_ref, yb_ref, dz_in, dya_ref, dyb_ref, dz_ref, keep):
        del dz_in
        j = pl.program_id(1)

        @pl.when(j < nj)
        def _():
            dm = lax.dot_general(dx_ref[...], wo_ref[...], NT, preferred_element_type=F32)
            sa, sb = _sigmoid(ga_ref[...]), _sigmoid(gb_ref[...])
            dya_ref[...] = (dm * sa).astype(BF16)
            dyb_ref[...] = (dm * sb).astype(BF16)
            dz_ref[...] = (dm * ya_ref[...] * (sa * (1.0 - sa))).astype(BF16)
            keep[jnp.minimum(j, nj - 1)] = (dm * yb_ref[...] * (sb * (1.0 - sb))).astype(BF16)

        @pl.when(j >= nj)
        def _():
            dz_ref[...] = keep[jnp.maximum(j - nj, 0)]

    def lo(j):
        return jnp.minimum(j, nj - 1)

    def gate_bwd(dz):
        t_lo = pl.BlockSpec((tm, tg), lambda i, j: (i, lo(j)))
        return pl.pallas_call(
            gate_bwd_body,
            name="mm_dgate",
            grid=(s // tm, 2 * nj),
            in_specs=[
                pl.BlockSpec((tm, d), lambda i, j: (i, 0)),
                pl.BlockSpec((tg, d), lambda i, j: (lo(j), 0)),
                pl.BlockSpec((tm, tg), lambda i, j: (i, ga0 + lo(j))),
                pl.BlockSpec((tm, tg), lambda i, j: (i, gb0 + lo(j))),
                t_lo,
                t_lo,
                ANY,
            ],
            out_specs=[t_lo, t_lo, pl.BlockSpec((tm, tg), lambda i, j: (i, ga0 + j))],
            out_shape=[jax.ShapeDtypeStruct((s, d), BF16), jax.ShapeDtypeStruct((s, d), BF16), jax.ShapeDtypeStruct(dz.shape, BF16)],
            scratch_shapes=[pltpu.VMEM((nj, tm, tg), BF16)],
            input_output_aliases={6: 2},
            compiler_params=_params(_mm_vmem([((tm, d), BF16, 2), ((tg, d), BF16, 2), ((tm, tg), F32, 14), ((nj, tm, tg), BF16, 1)])),
        )(dx2b, g_o, z, z, y_a, y_b, dz)

    gw_o = _mm(
        "mm_gw_o", (d // twr, d // twn, 1), [m_act, dx2b],
        [pl.BlockSpec((s, twr), lambda i, j, k: (0, i)), pl.BlockSpec((s, twn), lambda i, j, k: (0, j))],
        [jax.ShapeDtypeStruct((d, d), BF16)], [pl.BlockSpec((twr, twn), lambda i, j, k: (i, j))],
        [(0, 1, TN, 0)], 1, (twr, twn), 1, ep_bf16,
        _mm_vmem([((s, twr), BF16, 3), ((s, twn), BF16, 2), ((twr, twn), F32, 3)]),
    )[0]

    dz0 = jnp.zeros((8, LANES), BF16)
    del dz0

    def gate_bwd_create():
        t_lo = pl.BlockSpec((tm, tg), lambda i, j: (i, lo(j)))

        def body(dx_ref, wo_ref, ga_ref, gb_ref, ya_ref, yb_ref, dya_ref, dyb_ref, dz_ref, keep):
            gate_bwd_body(dx_ref, wo_ref, ga_ref, gb_ref, ya_ref, yb_ref, None, dya_ref, dyb_ref, dz_ref, keep)

        return pl.pallas_call(
            body,
            name="mm_dgate",
            grid=(s // tm, 2 * nj),
            in_specs=[
                pl.BlockSpec((tm, d), lambda i, j: (i, 0)),
                pl.BlockSpec((tg, d), lambda i, j: (lo(j), 0)),
                pl.BlockSpec((tm, tg), lambda i, j: (i, ga0 + lo(j))),
                pl.BlockSpec((tm, tg), lambda i, j: (i, gb0 + lo(j))),
                t_lo,
                t_lo,
            ],
            out_specs=[t_lo, t_lo, pl.BlockSpec((tm, tg), lambda i, j: (i, ga0 + j))],
            out_shape=[jax.ShapeDtypeStruct((s, d), BF16), jax.ShapeDtypeStruct((s, d), BF16), jax.ShapeDtypeStruct((s, n_in), BF16)],
            scratch_shapes=[pltpu.VMEM((nj, tm, tg), BF16)],
            compiler_params=_params(_mm_vmem([((tm, d), BF16, 2), ((tg, d), BF16, 2), ((tm, tg), F32, 14), ((nj, tm, tg), BF16, 1)])),
        )(dx2b, g_o, z, z, y_a, y_b)

    del gate_bwd
    dya, dyb, dz = gate_bwd_create()

    tb = _tile(w_sgu, (1024, 512))
    b_out = pl.BlockSpec((tm, tb), lambda i, j, k: (i, j))

    def ep_branch_bwd(ins, vals, outs):
        outs[0][...] = vals[0]
        outs[1][...] = vals[1].astype(BF16)

    da, datt = _mm(
        "mm_dbranches", (s // tm, w_sgu // tb, 1), [dya, g_a, dyb, g_b],
        [pl.BlockSpec((tm, d), lambda i, j, k: (i, 0)), pl.BlockSpec((tb, d), lambda i, j, k: (j, 0)),
         pl.BlockSpec((tm, d), lambda i, j, k: (i, 0)), pl.BlockSpec((tb, d), lambda i, j, k: (j, 0))],
        [jax.ShapeDtypeStruct((s, w_sgu), F32), jax.ShapeDtypeStruct((s, w_att), BF16)], [b_out, b_out],
        [(0, 1, NT, 0), (2, 3, NT, 1)], 2, (tm, tb), 1, ep_branch_bwd,
        _mm_vmem([((tm, d), BF16, 4), ((tb, d), BF16, 4), ((tm, tb), F32, 6)]),
    )

    wb_tile = pl.BlockSpec((tb, twn), lambda i, j, k: (i, j))
    gw_a, gw_b = _mm(
        "mm_gw_branches", (w_sgu // tb, d // twn, 1), [a_act, dya, att, dyb],
        [pl.BlockSpec((s, tb), lambda i, j, k: (0, i)), pl.BlockSpec((s, twn), lambda i, j, k: (0, j)),
         pl.BlockSpec((s, tb), lambda i, j, k: (0, i)), pl.BlockSpec((s, twn), lambda i, j, k: (0, j))],
        [jax.ShapeDtypeStruct((w_sgu, d), BF16), jax.ShapeDtypeStruct((w_att, d), BF16)], [wb_tile, wb_tile],
        [(0, 1, TN, 0), (2, 3, TN, 1)], 2, (tb, twn), 1, ep_bf16,
        _mm_vmem([((s, tb), BF16, 5), ((s, twn), BF16, 4), ((tb, twn), F32, 6)]),
    )

    dz, dws, dbs, dgain = _sgu_bwd_into(z, da, sgu_v_gain, ws_b, wst_b, b_col, w_sgu, dz)
    dz, dk_pad, dv_pad, dbias_tab, dsink = _attn_bwd(sink, z, k_pad, v_pad, bias_tab, datt, dz, grp, q_blk0)
    dz = _dkv_to_dz(dk_pad, dv_pad, dz, off_k // (2 * w_kv))
    drel = _relbias_bwd(dbias_tab, bucket)

    tkz = _tile(n_in, (1920, 1536, 1280, 1024))
    nkz = n_in // tkz
    dh1 = _mm(
        "mm_dh1", (s // tm, d // tn, nkz), [dz, g_in],
        [pl.BlockSpec((tm, tkz), lambda i, j, k: (i, k)), pl.BlockSpec((tn, tkz), lambda i, j, k: (j, k))],
        [jax.ShapeDtypeStruct((s, d), F32)], [pl.BlockSpec((tm, tn), lambda i, j, k: (i, j))],
        [(0, 1, NT, 0)], 1, (tm, tn), nkz, ep_f32,
        _mm_vmem([((tm, tkz), BF16, 2), ((tn, tkz), BF16, 2), ((tm, tn), F32, 5)]),
    )[0]

    tzn = _tile(n_in, (768, 640, 512))
    gw_in = _mm(
        "mm_gw_in", (d // twr, n_in // tzn, 1), [h1, dz],
        [pl.BlockSpec((s, twr), lambda i, j, k: (0, i)), pl.BlockSpec((s, tzn), lambda i, j, k: (0, j))],
        [jax.ShapeDtypeStruct((d, n_in), BF16)], [pl.BlockSpec((twr, tzn), lambda i, j, k: (i, j))],
        [(0, 1, TN, 0)], 1, (twr, tzn), 1, ep_bf16,
        _mm_vmem([((s, twr), BF16, 3), ((s, tzn), BF16, 2), ((twr, tzn), F32, 3)]),
    )[0]

    grad_x, _, dg_mix = _rms_bwd("rms_mix_bwd", x2d, norm_mix, dh1, dx2)

    full_grads = [gw_in, gw_a, gw_b, gw_o, gw_gate, gw_up, gw_down]
    from_sib = _pair_exchange(full_grads, kinds)
    names = [n for n, _, _ in big]
    halves = [_pair_add("pair_add_" + n, c_idx, g, r, k) for n, g, r, k in zip(names, full_grads, from_sib, kinds)]
    from_chips = _chip_exchange(halves, kinds)
    pieces = [_chip_sum("chip_sum_" + n, q_idx, h, r, k) for n, h, r, k in zip(names, halves, from_chips, kinds)]
    grads_big = _pair_share(pieces, kinds)

    small_w = [norm_mix, sgu_v_gain, sgu_w_s, sgu_b_s, attn_sink, rel_bias, norm_ffn, norm_final]
    small_m = [m_norm_mix, m_sgu_v_gain, m_sgu_w_s, m_sgu_b_s, m_attn_sink, m_rel_bias, m_norm_ffn, m_norm_final]
    small_v = [v_norm_mix, v_sgu_v_gain, v_sgu_w_s, v_sgu_b_s, v_attn_sink, v_rel_bias, v_norm_ffn, v_norm_final]
    small_shapes = [w.shape for w in small_w]
    local_small = [dg_mix, dgain, dws, dbs, dsink[:, 0], drel[:, :REL_BUCKETS].T, dg_ffn, dg_final]
    local_small = [g.reshape(shp) for g, shp in zip(local_small, small_shapes)]
    packed_g = _small_all_reduce(_pack(local_small + [loss_part[0, :1]]))
    g_small = _unpack(packed_g, small_shapes + [(1,)])
    loss = g_small[-1].reshape(())
    g_small = g_small[:-1]
    zero1 = jnp.zeros((1,), F32)
    pw, pg, pm, pv = _pack(small_w + [zero1]), _pack(g_small + [zero1]), _pack(small_m + [zero1]), _pack(small_v + [zero1])
    d_small, nm_small, nv_small = [_unpack(a, small_shapes) for a in _adamw("adamw_small", pw, pg, pm, pv)]

    big_m = [m_w_in, m_w_a_out, m_w_b_out, m_w_o, m_w_gate, m_w_up, m_w_down]
    big_v = [v_w_in, v_w_a_out, v_w_b_out, v_w_o, v_w_gate, v_w_up, v_w_down]
    upd = [_adamw("adamw_" + n, w, g, mm[0], vv[0]) for (n, w, _), g, mm, vv in zip(big, grads_big, big_m, big_v)]

    order = ["w_in", "norm_mix", "sgu_v_gain", "sgu_w_s", "sgu_b_s", "w_a", "attn_sink", "rel_bias", "w_b", "w_o", "norm_ffn",
             "w_gate", "w_up", "w_down", "norm_final"]
    small_names = ["norm_mix", "sgu_v_gain", "sgu_w_s", "sgu_b_s", "attn_sink", "rel_bias", "norm_ffn", "norm_final"]
    table = {}
    for i, n in enumerate(names):
        table[n] = (grads_big[i][None], upd[i][0][None], upd[i][1][None], upd[i][2][None])
    for i, n in enumerate(small_names):
        table[n] = (g_small[i], d_small[i], nm_small[i], nv_small[i])
    outs = [loss, grad_x.reshape(1, s, d)]
    for part in range(4):
        outs += [table[n][part] for n in order]
    return tuple(outs)


def _sgu_bwd_into(z, da, gain, ws_b, wst_b, b_col, w_sgu, dz):
    s = z.shape[0]
    groups = ws_b.shape[0]
    n = s // BLK

    def body(zu_ref, zv_ref, da_ref, gain_ref, ws_ref, wst_ref, b_ref, dz_in, dz_ref, dws_ref, dbs_ref, dgain_ref, acc_gain):
        del dz_in
        c = pl.program_id(0)
        zu = zu_ref[...]
        zv = zv_ref[...]
        gain_v = gain_ref[...]
        vv = _gelu(zv)
        r = lax.rsqrt(jnp.mean(vv * vv, axis=-1, keepdims=True) + EPS)
        xh = vv * r
        vn = (xh * gain_v).astype(BF16)
        u = _gelu(zu)
        dav = da_ref[...]
        dmix = dav * u
        dmix_b = dmix.astype(BF16)
        dvn_parts = []
        for g in range(groups):
            sl = slice(g * BLK, (g + 1) * BLK)
            mixed = jnp.dot(ws_ref[g], vn[:, sl], preferred_element_type=F32) + b_ref[g]
            dz_ref[:, sl] = (dav[:, sl] * mixed * _gelu_grad(zu[:, sl])).astype(BF16)
            dvn_parts.append(jnp.dot(wst_ref[g], dmix_b[:, sl], preferred_element_type=F32))
            dws_g = lax.dot_general(dmix_b[:, sl], vn[:, sl], NT, preferred_element_type=F32)
            dbs_g = jnp.sum(dmix[:, sl], axis=1, keepdims=True)

            @pl.when(c == 0)
            def _():
                dws_ref[g] = dws_g
                dbs_ref[g] = dbs_g

            @pl.when(c > 0)
            def _():
                dws_ref[g] += dws_g
                dbs_ref[g] += dbs_g

        dvn = jnp.concatenate(dvn_parts, axis=1)
        dxh = dvn * gain_v
        dvv = r * (dxh - xh * jnp.mean(dxh * xh, axis=-1, keepdims=True))
        dz_ref[:, w_sgu:] = (dvv * _gelu_grad(zv)).astype(BF16)
        pg = _rows8(dvn * xh)

        @pl.when(c == 0)
        def _():
            acc_gain[...] = pg

        @pl.when(c > 0)
        def _():
            acc_gain[...] += pg

        @pl.when(c == n - 1)
        def _():
            dgain_ref[...] = jnp.sum(acc_gain[...], axis=0, keepdims=True)

    full3 = pl.BlockSpec((groups, BLK, BLK), lambda c: (0, 0, 0))
    col3 = pl.BlockSpec((groups, BLK, 1), lambda c: (0, 0, 0))
    vec = pl.BlockSpec((1, w_sgu), lambda c: (0, 0))
    return pl.pallas_call(
        body,
        name="sgu_bwd",
        grid=(n,),
        in_specs=[
            pl.BlockSpec((BLK, w_sgu), lambda c: (c, 0)),
            pl.BlockSpec((BLK, w_sgu), lambda c: (c, 1)),
            pl.BlockSpec((BLK, w_sgu), lambda c: (c, 0)),
            vec,
            full3,
            full3,
            col3,
            ANY,
        ],
        out_specs=[pl.BlockSpec((BLK, 2 * w_sgu), lambda c: (c, 0)), full3, col3, vec],
        out_shape=[
            jax.ShapeDtypeStruct(dz.shape, BF16),
            jax.ShapeDtypeStruct((groups, BLK, BLK), F32),
            jax.ShapeDtypeStruct((groups, BLK, 1), F32),
            jax.ShapeDtypeStruct((1, w_sgu), F32),
        ],
        scratch_shapes=[pltpu.VMEM((8, w_sgu), F32)],
        input_output_aliases={7: 0},
    )(z, z, da, gain, ws_b, wst_b, b_col, dz)
```

```python
import functools
import math

import jax
import jax.numpy as jnp
import numpy as np
from jax import lax
from jax.experimental import pallas as pl
from jax.experimental.pallas import tpu as pltpu

F32 = jnp.float32
BF16 = jnp.bfloat16
I32 = jnp.int32
MESH = pl.DeviceIdType.MESH

EPS = 1e-6
NEG = -1e30
BLK = 128
HEAD_DIM = 128
N_KV_HEADS = 2
REL_BUCKETS = 32
REL_MAX_DIST = 128
N_CHIPS = 4
ADAM_LR, ADAM_B1, ADAM_B2, ADAM_EPS, ADAM_WD, ADAM_STEP = 0.001, 0.9, 0.999, 1e-08, 0.01, 10

LANES = 128
VMEM_CAP = 60 * 1024 * 1024

NN = (((1,), (0,)), ((), ()))
NT = (((1,), (1,)), ((), ()))
TN = (((0,), (0,)), ((), ()))
ANY = pl.BlockSpec(memory_space=pl.ANY)


def _tile(n, cands):
    for t in cands:
        if n % t == 0:
            return t
    return n


def _params(vmem_bytes=None, **kw):
    if vmem_bytes is not None:
        kw["vmem_limit_bytes"] = int(min(max(vmem_bytes, 32 * 1024 * 1024), VMEM_CAP))
    return pltpu.CompilerParams(**kw)


def _nbytes(shape, dtype):
    return int(np.prod(shape)) * jnp.dtype(dtype).itemsize


def _sigmoid(x):
    return 1.0 / (1.0 + jnp.exp(-x))


_GC = 0.7978845608028654
_GA = 0.044715


def _gelu(x):
    return 0.5 * x * (1.0 + jnp.tanh(_GC * (x + _GA * (x * x * x))))


def _gelu_grad(x):
    t = jnp.tanh(_GC * (x + _GA * (x * x * x)))
    return 0.5 * (1.0 + t) + 0.5 * x * (1.0 - t * t) * (_GC * (1.0 + 3.0 * _GA * (x * x)))


def _bf(v):
    return v if v.dtype == BF16 else v.astype(BF16)


def _mm(name, grid, ins, in_specs, out_shape, out_specs, pairs, n_acc, acc_tile, nk, epilogue, vmem_bytes):
    n_in, n_out = len(ins), len(out_shape)

    def body(*refs):
        in_refs, out_refs, accs = refs[:n_in], refs[n_in : n_in + n_out], refs[n_in + n_out :]

        def products():
            vals = [None] * n_acc
            for a_i, b_i, dn, acc_i in pairs:
                d = lax.dot_general(_bf(in_refs[a_i][...]), _bf(in_refs[b_i][...]), dn, preferred_element_type=F32)
                vals[acc_i] = d if vals[acc_i] is None else vals[acc_i] + d
            return vals

        if nk == 1:
            epilogue(in_refs, products(), out_refs)
            return
        k = pl.program_id(2)
        vals = products()

        @pl.when(k == 0)
        def _():
            for a, v in zip(accs, vals):
                a[...] = v

        @pl.when(k > 0)
        def _():
            for a, v in zip(accs, vals):
                a[...] += v

        @pl.when(k == nk - 1)
        def _():
            epilogue(in_refs, [a[...] for a in accs], out_refs)

    scratch = [pltpu.VMEM(acc_tile, F32) for _ in range(n_acc)] if nk > 1 else []
    return pl.pallas_call(
        body,
        name=name,
        grid=grid,
        in_specs=in_specs,
        out_specs=out_specs,
        out_shape=out_shape,
        scratch_shapes=scratch,
        compiler_params=_params(vmem_bytes),
    )(*ins)


def _mm_vmem(tiles):
    return sum(_nbytes(s, d) * c for s, d, c in tiles) + 4 * 1024 * 1024


def _store_cast(ref, v):
    ref[...] = v.astype(ref.dtype)


def _rows8(v):
    r, d = v.shape
    return v.reshape(r // 8, 8, d).sum(axis=0)


def _rms_fwd(name, x, g):
    s, d = x.shape
    tm = _tile(s, (256, 128))

    def body(x_ref, g_ref, h_ref):
        xv = x_ref[...]
        r = lax.rsqrt(jnp.mean(xv * xv, axis=-1, keepdims=True) + EPS)
        h_ref[...] = ((xv * r) * g_ref[...]).astype(BF16)

    return pl.pallas_call(
        body,
        name=name,
        grid=(s // tm,),
        in_specs=[pl.BlockSpec((tm, d), lambda i: (i, 0)), pl.BlockSpec((1, d), lambda i: (0, 0))],
        out_specs=pl.BlockSpec((tm, d), lambda i: (i, 0)),
        out_shape=jax.ShapeDtypeStruct((s, d), BF16),
    )(x, g)


def _rms_bwd(name, x, g, dh, dres):
    s, d = x.shape
    tm = _tile(s, (256, 128))
    n = s // tm

    def body(x_ref, g_ref, dh_ref, dres_ref, dx_ref, dxb_ref, dg_ref, acc_ref):
        i = pl.program_id(0)
        xv = x_ref[...]
        r = lax.rsqrt(jnp.mean(xv * xv, axis=-1, keepdims=True) + EPS)
        xh = xv * r
        dhv = dh_ref[...]
        dxh = dhv * g_ref[...]
        dx = r * (dxh - xh * jnp.mean(dxh * xh, axis=-1, keepdims=True)) + dres_ref[...]
        dx_ref[...] = dx
        dxb_ref[...] = dx.astype(BF16)
        part = _rows8(dhv * xh)

        @pl.when(i == 0)
        def _():
            acc_ref[...] = part

        @pl.when(i > 0)
        def _():
            acc_ref[...] += part

        @pl.when(i == n - 1)
        def _():
            dg_ref[...] = jnp.sum(acc_ref[...], axis=0, keepdims=True)

    row = pl.BlockSpec((tm, d), lambda i: (i, 0))
    vec = pl.BlockSpec((1, d), lambda i: (0, 0))
    return pl.pallas_call(
        body,
        name=name,
        grid=(n,),
        in_specs=[row, vec, row, row],
        out_specs=[row, row, vec],
        out_shape=[jax.ShapeDtypeStruct((s, d), F32), jax.ShapeDtypeStruct((s, d), BF16), jax.ShapeDtypeStruct((1, d), F32)],
        scratch_shapes=[pltpu.VMEM((8, d), F32)],
    )(x, g, dh, dres)


def _head(x3, g, target):
    s, d = x3.shape
    tm = _tile(s, (256, 128))
    n = s // tm

    def body(x_ref, g_ref, t_ref, dx_ref, dxb_ref, dg_ref, loss_ref, acc_g, acc_l):
        i = pl.program_id(0)
        xv = x_ref[...]
        gv = g_ref[...]
        r = lax.rsqrt(jnp.mean(xv * xv, axis=-1, keepdims=True) + EPS)
        xh = xv * r
        e = xh * gv - t_ref[...]
        dy = e * (1.0 / d)
        dxh = dy * gv
        dx = r * (dxh - xh * jnp.mean(dxh * xh, axis=-1, keepdims=True))
        dx_ref[...] = dx
        dxb_ref[...] = dx.astype(BF16)
        pg = _rows8(dy * xh)
        plo = _rows8(e * e)

        @pl.when(i == 0)
        def _():
            acc_g[...] = pg
            acc_l[...] = plo

        @pl.when(i > 0)
        def _():
            acc_g[...] += pg
            acc_l[...] += plo

        @pl.when(i == n - 1)
        def _():
            dg_ref[...] = jnp.sum(acc_g[...], axis=0, keepdims=True)
            loss_ref[...] = jnp.full((1, LANES), (0.5 / d) * jnp.sum(acc_l[...]), F32)

    row = pl.BlockSpec((tm, d), lambda i: (i, 0))
    vec = pl.BlockSpec((1, d), lambda i: (0, 0))
    return pl.pallas_call(
        body,
        name="head",
        grid=(n,),
        in_specs=[row, vec, row],
        out_specs=[row, row, vec, pl.BlockSpec((1, LANES), lambda i: (0, 0))],
        out_shape=[
            jax.ShapeDtypeStruct((s, d), F32),
            jax.ShapeDtypeStruct((s, d), BF16),
            jax.ShapeDtypeStruct((1, d), F32),
            jax.ShapeDtypeStruct((1, LANES), F32),
        ],
        scratch_shapes=[pltpu.VMEM((8, d), F32), pltpu.VMEM((8, d), F32)],
    )(x3, g, target)


def _sgu_fwd(z, gain, ws_b, b_col, w_sgu):
    s = z.shape[0]
    groups = ws_b.shape[0]

    def body(zu_ref, zv_ref, gain_ref, ws_ref, b_ref, a_ref):
        vv = _gelu(zv_ref[...])
        r = lax.rsqrt(jnp.mean(vv * vv, axis=-1, keepdims=True) + EPS)
        vn = ((vv * r) * gain_ref[...]).astype(BF16)
        u = _gelu(zu_ref[...])
        for g in range(groups):
            sl = slice(g * BLK, (g + 1) * BLK)
            mixed = jnp.dot(ws_ref[g], vn[:, sl], preferred_element_type=F32) + b_ref[g]
            a_ref[:, sl] = (u[:, sl] * mixed).astype(BF16)

    return pl.pallas_call(
        body,
        name="sgu_fwd",
        grid=(s // BLK,),
        in_specs=[
            pl.BlockSpec((BLK, w_sgu), lambda c: (c, 0)),
            pl.BlockSpec((BLK, w_sgu), lambda c: (c, 1)),
            pl.BlockSpec((1, w_sgu), lambda c: (0, 0)),
            pl.BlockSpec((groups, BLK, BLK), lambda c: (0, 0, 0)),
            pl.BlockSpec((groups, BLK, 1), lambda c: (0, 0, 0)),
        ],
        out_specs=pl.BlockSpec((BLK, w_sgu), lambda c: (c, 0)),
        out_shape=jax.ShapeDtypeStruct((s, w_sgu), BF16),
    )(z, z, gain, ws_b, b_col)


def _sgu_bwd(z, da, gain, ws_b, wst_b, b_col, w_sgu, n_cols):
    s = z.shape[0]
    groups = ws_b.shape[0]
    n = s // BLK

    def body(zu_ref, zv_ref, da_ref, gain_ref, ws_ref, wst_ref, b_ref, dz_ref, dws_ref, dbs_ref, dgain_ref, acc_gain):
        c = pl.program_id(0)
        zu = zu_ref[...]
        zv = zv_ref[...]
        gain_v = gain_ref[...]
        vv = _gelu(zv)
        r = lax.rsqrt(jnp.mean(vv * vv, axis=-1, keepdims=True) + EPS)
        xh = vv * r
        vn = (xh * gain_v).astype(BF16)
        u = _gelu(zu)
        dav = da_ref[...]
        dmix = dav * u
        dmix_b = dmix.astype(BF16)
        dvn_parts = []
        for g in range(groups):
            sl = slice(g * BLK, (g + 1) * BLK)
            mixed = jnp.dot(ws_ref[g], vn[:, sl], preferred_element_type=F32) + b_ref[g]
            dz_ref[:, sl] = (dav[:, sl] * mixed * _gelu_grad(zu[:, sl])).astype(BF16)
            dvn_parts.append(jnp.dot(wst_ref[g], dmix_b[:, sl], preferred_element_type=F32))
            dws_g = lax.dot_general(dmix_b[:, sl], vn[:, sl], NT, preferred_element_type=F32)
            dbs_g = jnp.sum(dmix[:, sl], axis=1, keepdims=True)

            @pl.when(c == 0)
            def _():
                dws_ref[g] = dws_g
                dbs_ref[g] = dbs_g

            @pl.when(c > 0)
            def _():
                dws_ref[g] += dws_g
                dbs_ref[g] += dbs_g

        dvn = jnp.concatenate(dvn_parts, axis=1)
        dxh = dvn * gain_v
        dvv = r * (dxh - xh * jnp.mean(dxh * xh, axis=-1, keepdims=True))
        dz_ref[:, w_sgu:] = (dvv * _gelu_grad(zv)).astype(BF16)
        pg = _rows8(dvn * xh)

        @pl.when(c == 0)
        def _():
            acc_gain[...] = pg

        @pl.when(c > 0)
        def _():
            acc_gain[...] += pg

        @pl.when(c == n - 1)
        def _():
            dgain_ref[...] = jnp.sum(acc_gain[...], axis=0, keepdims=True)

    full3 = pl.BlockSpec((groups, BLK, BLK), lambda c: (0, 0, 0))
    col3 = pl.BlockSpec((groups, BLK, 1), lambda c: (0, 0, 0))
    vec = pl.BlockSpec((1, w_sgu), lambda c: (0, 0))
    return pl.pallas_call(
        body,
        name="sgu_bwd",
        grid=(n,),
        in_specs=[
            pl.BlockSpec((BLK, w_sgu), lambda c: (c, 0)),
            pl.BlockSpec((BLK, w_sgu), lambda c: (c, 1)),
            pl.BlockSpec((BLK, w_sgu), lambda c: (c, 0)),
            vec,
            full3,
            full3,
            col3,
        ],
        out_specs=[pl.BlockSpec((BLK, 2 * w_sgu), lambda c: (c, 0)), full3, col3, vec],
        out_shape=[
            jax.ShapeDtypeStruct((s, n_cols), BF16),
            jax.ShapeDtypeStruct((groups, BLK, BLK), F32),
            jax.ShapeDtypeStruct((groups, BLK, 1), F32),
            jax.ShapeDtypeStruct((1, w_sgu), F32),
        ],
        scratch_shapes=[pltpu.VMEM((8, w_sgu), F32)],
    )(z, z, da, gain, ws_b, wst_b, b_col)


def _attn_softmax(sink_ref, q_ref, k_ref, v_ref, bias_ref, s_len, grp):
    kv = pl.program_id(0)
    n = pl.program_id(1)
    start = pl.multiple_of(n * BLK, BLK)
    kb = k_ref[pl.ds(start, 3 * BLK), :]
    vb = v_ref[pl.ds(start, 3 * BLK), :]
    qv = q_ref[...]
    qs = jnp.concatenate([qv[:, g * HEAD_DIM : (g + 1) * HEAD_DIM] for g in range(grp)], axis=0).astype(BF16)
    sc = lax.dot_general(qs, kb, NT, preferred_element_type=F32) * (HEAD_DIM**-0.5)
    sc = sc + bias_ref[...].reshape(grp * BLK, 3 * BLK)
    kpos = start + lax.broadcasted_iota(I32, (1, 3 * BLK), 1) - BLK
    sc = jnp.where((kpos >= 0) & (kpos < s_len), sc, NEG)
    sink = jnp.concatenate([jnp.full((BLK, 1), sink_ref[kv * grp + g], F32) for g in range(grp)], axis=0)
    m = jnp.maximum(jnp.max(sc, axis=-1, keepdims=True), sink)
    p = jnp.exp(sc - m)
    esink = jnp.exp(sink - m)
    den = jnp.sum(p, axis=-1, keepdims=True) + esink
    return start, qs, kb, vb, p / den, esink / den


def _attn_specs(s, grp, q_blk0):
    qw = grp * HEAD_DIM
    return [
        pl.BlockSpec(memory_space=pltpu.SMEM),
        pl.BlockSpec((BLK, qw), lambda kv, n: (n, q_blk0 + kv)),
        pl.BlockSpec((s + 2 * BLK, HEAD_DIM), lambda kv, n: (0, kv)),
        pl.BlockSpec((s + 2 * BLK, HEAD_DIM), lambda kv, n: (0, kv)),
        pl.BlockSpec((grp, BLK, 3 * BLK), lambda kv, n: (kv, 0, 0)),
    ]


def _attn_fwd(sink, z, k_pad, v_pad, bias_tab, grp, q_blk0):
    s = z.shape[0]
    qw = grp * HEAD_DIM

    def body(sink_ref, q_ref, k_ref, v_ref, bias_ref, o_ref):
        _, _, _, vb, pn, _ = _attn_softmax(sink_ref, q_ref, k_ref, v_ref, bias_ref, s, grp)
        o = jnp.dot(pn.astype(BF16), vb, preferred_element_type=F32)
        for g in range(grp):
            o_ref[:, g * HEAD_DIM : (g + 1) * HEAD_DIM] = o[g * BLK : (g + 1) * BLK].astype(BF16)

    return pl.pallas_call(
        body,
        name="attn_fwd",
        grid=(N_KV_HEADS, s // BLK),
        in_specs=_attn_specs(s, grp, q_blk0),
        out_specs=pl.BlockSpec((BLK, qw), lambda kv, n: (n, kv)),
        out_shape=jax.ShapeDtypeStruct((s, N_KV_HEADS * qw), BF16),
    )(sink, z, k_pad, v_pad, bias_tab)


def _attn_bwd(sink, z, k_pad, v_pad, bias_tab, dout, dz, grp, q_blk0):
    s = z.shape[0]
    qw = grp * HEAD_DIM
    nb = s // BLK
    heads = N_KV_HEADS * grp

    def body(sink_ref, q_ref, k_ref, v_ref, bias_ref, do_ref, dz_in, dq_ref, dk_ref, dv_ref, dbias_ref, dsink_ref, dk_acc, dv_acc):
        del dz_in
        kv = pl.program_id(0)
        n = pl.program_id(1)
        start, qs, kb, vb, pn, psink = _attn_softmax(sink_ref, q_ref, k_ref, v_ref, bias_ref, s, grp)
        dov = do_ref[...]
        dos = jnp.concatenate([dov[:, g * HEAD_DIM : (g + 1) * HEAD_DIM] for g in range(grp)], axis=0)
        dp = lax.dot_general(dos, vb, NT, preferred_element_type=F32)
        dvb = lax.dot_general(pn.astype(BF16), dos, TN, preferred_element_type=F32)
        delta = jnp.sum(pn * dp, axis=-1, keepdims=True)
        ds = pn * (dp - delta)
        dsb = (ds * (HEAD_DIM**-0.5)).astype(BF16)
        dq = jnp.dot(dsb, kb, preferred_element_type=F32)
        dkb = lax.dot_general(dsb, qs, TN, preferred_element_type=F32)
        for g in range(grp):
            dq_ref[:, g * HEAD_DIM : (g + 1) * HEAD_DIM] = dq[g * BLK : (g + 1) * BLK].astype(BF16)

        @pl.when(n == 0)
        def _():
            dk_acc[...] = jnp.zeros_like(dk_acc)
            dv_acc[...] = jnp.zeros_like(dv_acc)
            dbias_ref[...] = jnp.zeros_like(dbias_ref)

        @pl.when((n == 0) & (kv == 0))
        def _():
            dsink_ref[...] = jnp.zeros_like(dsink_ref)

        dk_acc[pl.ds(start, 3 * BLK), :] += dkb
        dv_acc[pl.ds(start, 3 * BLK), :] += dvb
        dbias_ref[...] += ds.reshape(grp, BLK, 3 * BLK)
        row = lax.broadcasted_iota(I32, (heads, LANES), 0)
        sd = psink * delta
        upd = jnp.zeros((heads, LANES), F32)
        for g in range(grp):
            upd = jnp.where(row == kv * grp + g, -jnp.sum(sd[g * BLK : (g + 1) * BLK]), upd)
        dsink_ref[...] += upd

        @pl.when(n == nb - 1)
        def _():
            dk_ref[...] = dk_acc[...]
            dv_ref[...] = dv_acc[...]

    pad_spec = pl.BlockSpec((s + 2 * BLK, HEAD_DIM), lambda kv, n: (0, kv))
    kvw = N_KV_HEADS * HEAD_DIM
    n_in = 7
    return pl.pallas_call(
        body,
        name="attn_bwd",
        grid=(N_KV_HEADS, nb),
        in_specs=_attn_specs(s, grp, q_blk0) + [pl.BlockSpec((BLK, qw), lambda kv, n: (n, kv)), ANY],
        out_specs=[
            pl.BlockSpec((BLK, qw), lambda kv, n: (n, q_blk0 + kv)),
            pad_spec,
            pad_spec,
            pl.BlockSpec((grp, BLK, 3 * BLK), lambda kv, n: (kv, 0, 0)),
            pl.BlockSpec((heads, LANES), lambda kv, n: (0, 0)),
        ],
        out_shape=[
            jax.ShapeDtypeStruct(dz.shape, BF16),
            jax.ShapeDtypeStruct((s + 2 * BLK, kvw), F32),
            jax.ShapeDtypeStruct((s + 2 * BLK, kvw), F32),
            jax.ShapeDtypeStruct((heads, BLK, 3 * BLK), F32),
            jax.ShapeDtypeStruct((heads, LANES), F32),
        ],
        scratch_shapes=[pltpu.VMEM((s + 2 * BLK, HEAD_DIM), F32), pltpu.VMEM((s + 2 * BLK, HEAD_DIM), F32)],
        input_output_aliases={n_in - 1: 0},
    )(sink, z, k_pad, v_pad, bias_tab, dout, dz)


def _dkv_to_dz(dk_pad, dv_pad, dz, blk_idx):
    s = dz.shape[0]
    kvw = dk_pad.shape[1]

    def body(dk_ref, dv_ref, dz_in, out_ref):
        del dz_in
        out_ref[:, :kvw] = dk_ref[...].astype(BF16)
        out_ref[:, kvw:] = dv_ref[...].astype(BF16)

    src = pl.BlockSpec((BLK, kvw), lambda i: (i + 1, 0))
    return pl.pallas_call(
        body,
        name="dkv_to_dz",
        grid=(s // BLK,),
        in_specs=[src, src, ANY],
        out_specs=pl.BlockSpec((BLK, 2 * kvw), lambda i: (i, blk_idx)),
        out_shape=jax.ShapeDtypeStruct(dz.shape, BF16),
        input_output_aliases={2: 0},
    )(dk_pad, dv_pad, dz)


def _relbias_bwd(dbias_tab, bucket):
    heads = dbias_tab.shape[0]

    def body(dt_ref, bk_ref, out_ref):
        lane = lax.broadcasted_iota(I32, (1, LANES), 1)
        bk = bk_ref[...]
        rows = []
        for h in range(heads):
            dt = dt_ref[h]
            acc = jnp.zeros((1, LANES), F32)
            for b in range(REL_BUCKETS):
                acc = jnp.where(lane == b, jnp.sum(jnp.where(bk == b, dt, 0.0)), acc)
            rows.append(acc)
        out_ref[...] = jnp.concatenate(rows, axis=0)

    return pl.pallas_call(body, name="relbias_bwd", out_shape=jax.ShapeDtypeStruct((heads, LANES), F32))(dbias_tab, bucket)


def _t5_bucket(rel):
    nb = REL_BUCKETS // 2
    ret = jnp.where(rel > 0, nb, 0)
    n = jnp.abs(rel)
    max_exact = nb // 2
    nf = jnp.maximum(n, 1).astype(F32)
    large = max_exact + (jnp.log(nf / max_exact) / math.log(REL_MAX_DIST / max_exact) * (nb - max_exact)).astype(I32)
    large = jnp.minimum(large, nb - 1)
    return ret + jnp.where(n < max_exact, n, large)


def _band_tables(rel_bias):
    qi = jnp.arange(BLK)[:, None]
    kj = jnp.arange(3 * BLK)[None, :]
    rel = kj - BLK - qi
    bucket = _t5_bucket(rel).astype(I32)
    heads = rel_bias.shape[1]
    masked = jnp.where(jnp.abs(rel) <= BLK, bucket, -1)

    def body(rb_ref, bk_ref, out_ref):
        bk = bk_ref[...]
        for h in range(heads):
            tab = jnp.full(bk.shape, NEG, F32)
            for b in range(REL_BUCKETS):
                tab = jnp.where(bk == b, rb_ref[b, h], tab)
            out_ref[h] = tab

    bias_tab = pl.pallas_call(
        body,
        name="bias_table",
        in_specs=[pl.BlockSpec(memory_space=pltpu.SMEM), pl.BlockSpec(memory_space=pltpu.VMEM)],
        out_specs=pl.BlockSpec(memory_space=pltpu.VMEM),
        out_shape=jax.ShapeDtypeStruct((heads, BLK, 3 * BLK), F32),
    )(rel_bias.astype(F32), masked)
    return bias_tab, bucket


def _ew_tiles(shape):
    r, c = shape
    tn = c if c <= 2048 else _tile(c, (2048, 1920, 1536, 1408, 1024, 512))
    tm = _tile(r, (128, 64, 32, 16, 8))
    return tm, tn


def _cast_into_full(name, qidx, w, kind):
    r, c = w.shape
    tm, tn = _ew_tiles(w.shape)
    nbi, nbj = r // tm, c // tn
    if kind == "col":
        full, out_spec = (r, c * N_CHIPS), pl.BlockSpec((tm, tn), lambda i, j, q: (i, q[0] * nbj + j))
    else:
        full, out_spec = (r * N_CHIPS, c), pl.BlockSpec((tm, tn), lambda i, j, q: (q[0] * nbi + i, j))

    def body(q_ref, w_ref, o_ref):
        del q_ref
        o_ref[...] = w_ref[...].astype(BF16)

    return pl.pallas_call(
        body,
        name=name,
        grid_spec=pltpu.PrefetchScalarGridSpec(
            num_scalar_prefetch=1, grid=(nbi, nbj), in_specs=[pl.BlockSpec((tm, tn), lambda i, j, q: (i, j))], out_specs=out_spec
        ),
        out_shape=jax.ShapeDtypeStruct(full, BF16),
    )(qidx, w)


def _adamw(name, w, g, m, v):
    tm, tn = _ew_tiles(w.shape)
    if _nbytes(w.shape, F32) <= 1024 * 1024:
        tm, tn = w.shape
    spec = pl.BlockSpec((tm, tn), lambda i, j: (i, j))

    def body(w_ref, g_ref, m_ref, v_ref, d_ref, nm_ref, nv_ref):
        gv = g_ref[...]
        nm = ADAM_B1 * m_ref[...] + (1.0 - ADAM_B1) * gv
        nv = ADAM_B2 * v_ref[...] + (1.0 - ADAM_B2) * (gv * gv)
        m_hat = nm / (1.0 - ADAM_B1**ADAM_STEP)
        v_hat = nv / (1.0 - ADAM_B2**ADAM_STEP)
        d_ref[...] = -ADAM_LR * (m_hat / (jnp.sqrt(v_hat) + ADAM_EPS) + ADAM_WD * w_ref[...])
        nm_ref[...] = nm
        nv_ref[...] = nv

    out = jax.ShapeDtypeStruct(w.shape, F32)
    return pl.pallas_call(
        body, name=name, grid=(w.shape[0] // tm, w.shape[1] // tn), in_specs=[spec] * 4, out_specs=[spec] * 3,
        out_shape=[out, out, out],
    )(w, g, m, v)


def _pair_add(name, cidx, g_full, r_sib, kind):
    hr, hc = r_sib.shape
    tm, tn = _ew_tiles((hr, hc))
    nbi, nbj = hr // tm, hc // tn
    if kind == "col":
        g_spec = pl.BlockSpec((tm, tn), lambda i, j, c: (c[0] * nbi + i, j))
    else:
        g_spec = pl.BlockSpec((tm, tn), lambda i, j, c: (i, c[0] * nbj + j))
    spec = pl.BlockSpec((tm, tn), lambda i, j, c: (i, j))

    def body(c_ref, g_ref, r_ref, o_ref):
        del c_ref
        o_ref[...] = (g_ref[...].astype(F32) + r_ref[...].astype(F32)).astype(BF16)

    return pl.pallas_call(
        body,
        name=name,
        grid_spec=pltpu.PrefetchScalarGridSpec(num_scalar_prefetch=1, grid=(nbi, nbj), in_specs=[g_spec, spec], out_specs=spec),
        out_shape=jax.ShapeDtypeStruct((hr, hc), BF16),
    )(cidx, g_full, r_sib)


def _chip_sum(name, qidx, c_half, r_ici, kind):
    _, pr, pc = r_ici.shape
    tm, tn = _ew_tiles((pr, pc))
    nbi, nbj = pr // tm, pc // tn
    if kind == "col":
        own_spec = pl.BlockSpec((tm, tn), lambda i, j, q: (i, q[0] * nbj + j))
        full, out_spec = (2 * pr, pc), pl.BlockSpec((tm, tn), lambda i, j, q: (q[1] * nbi + i, j))
    else:
        own_spec = pl.BlockSpec((tm, tn), lambda i, j, q: (q[0] * nbi + i, j))
        full, out_spec = (pr, 2 * pc), pl.BlockSpec((tm, tn), lambda i, j, q: (i, q[1] * nbj + j))

    def body(q_ref, own_ref, r_ref, o_ref):
        q = q_ref[0]
        own = own_ref[...].astype(F32)
        recv = [r_ref[r].astype(F32) for r in range(3)]
        total = None
        for chip in range(N_CHIPS):
            d = chip ^ q
            term = jnp.where(d == 0, own, jnp.where(d == 2, recv[0], jnp.where(d == 1, recv[1], recv[2])))
            total = term if total is None else total + term
        o_ref[...] = total

    return pl.pallas_call(
        body,
        name=name,
        grid_spec=pltpu.PrefetchScalarGridSpec(
            num_scalar_prefetch=1,
            grid=(nbi, nbj),
            in_specs=[own_spec, pl.BlockSpec((3, tm, tn), lambda i, j, q: (0, i, j))],
            out_specs=out_spec,
        ),
        out_shape=jax.ShapeDtypeStruct(full, F32),
    )(qidx, c_half, r_ici)


_REL_MASK = (2, 1, 3)


def _place():
    x, y, c = lax.axis_index("x"), lax.axis_index("y"), lax.axis_index("c")
    chips = [(1 - x, y), (x, 1 - y), (1 - x, 1 - y)]
    return x, y, c, 2 * x + y, chips


def _shard_view(ref, kind, chip):
    if kind == "col":
        w = ref.shape[1] // N_CHIPS
        return ref.at[:, pl.ds(pl.multiple_of(chip * w, LANES), w)]
    h = ref.shape[0] // N_CHIPS
    return ref.at[pl.ds(pl.multiple_of(chip * h, 16), h), :]


def _row_half(ref, half):
    h = ref.shape[0] // 2
    return ref.at[pl.ds(pl.multiple_of(half * h, 16), h), :]


def _pair_half(ref, kind, half):
    if kind == "col":
        return _row_half(ref, half)
    w = ref.shape[1] // 2
    return ref.at[:, pl.ds(pl.multiple_of(half * w, LANES), w)]


def _remote(src, dst, send_sem, recv_sem, dev):
    return pltpu.make_async_remote_copy(src_ref=src, dst_ref=dst, send_sem=send_sem, recv_sem=recv_sem, device_id=dev, device_id_type=MESH)


def _all_gather(shards, kinds):
    n_w = len(shards)
    fulls = [jax.ShapeDtypeStruct(s.shape, BF16) for s in shards]

    def body(*refs):
        g_refs = refs[n_w : 2 * n_w]
        ici_send, ici_recv, d2d_send, d2d_recv = refs[2 * n_w :]
        x, y, c, q, chips = _place()
        sib = (x, y, 1 - c)
        sends = []
        for w in range(n_w):
            mine = _row_half(_shard_view(g_refs[w], kinds[w], q), c)
            for r, chip in enumerate(chips):
                sends.append(_remote(mine, mine, ici_send.at[w, r], ici_recv.at[w, r], (*chip, c)))
        for cp in sends:
            cp.start()
        passed = []
        for w in range(n_w):
            for r in range(3):
                landed = _row_half(_shard_view(g_refs[w], kinds[w], q ^ _REL_MASK[r]), c)
                _remote(landed, landed, ici_send.at[w, r], ici_recv.at[w, r], sib).wait_recv()
                fwd = _remote(landed, landed, d2d_send.at[w, r], d2d_recv.at[w, r], sib)
                fwd.start()
                passed.append(fwd)
        for w in range(n_w):
            for r in range(3):
                other = _row_half(_shard_view(g_refs[w], kinds[w], q ^ _REL_MASK[r]), 1 - c)
                _remote(other, other, d2d_send.at[w, r], d2d_recv.at[w, r], sib).wait_recv()
        for cp in sends + passed:
            cp.wait_send()

    return pl.pallas_call(
        body,
        name="all_gather_weights",
        in_specs=[ANY] * n_w,
        out_specs=[ANY] * n_w,
        out_shape=fulls,
        scratch_shapes=[pltpu.SemaphoreType.DMA((n_w, 3)) for _ in range(4)],
        input_output_aliases={w: w for w in range(n_w)},
    )(*shards)


def _pair_exchange(grads, kinds):
    n_w = len(grads)
    outs = []
    for g, kind in zip(grads, kinds):
        r, c = g.shape
        outs.append(jax.ShapeDtypeStruct((r // 2, c) if kind == "col" else (r, c // 2), BF16))

    def body(*refs):
        g_refs, o_refs = refs[:n_w], refs[n_w : 2 * n_w]
        send, recv = refs[2 * n_w :]
        x, y, c, _, _ = _place()
        sib = (x, y, 1 - c)
        cps = [_remote(_pair_half(g_refs[w], kinds[w], 1 - c), o_refs[w], send.at[w], recv.at[w], sib) for w in range(n_w)]
        for cp in cps:
            cp.start()
        for cp in cps:
            cp.wait()

    return pl.pallas_call(
        body,
        name="grad_pair_exchange",
        in_specs=[ANY] * n_w,
        out_specs=[ANY] * n_w,
        out_shape=outs,
        scratch_shapes=[pltpu.SemaphoreType.DMA((n_w,)), pltpu.SemaphoreType.DMA((n_w,))],
        compiler_params=_params(has_side_effects=True),
    )(*grads)


def _chip_exchange(halves, kinds):
    n_w = len(halves)
    outs = []
    for h, kind in zip(halves, kinds):
        r, c = h.shape
        outs.append(jax.ShapeDtypeStruct((3, r, c // N_CHIPS) if kind == "col" else (3, r // N_CHIPS, c), BF16))

    def body(*refs):
        h_refs, o_refs = refs[:n_w], refs[n_w : 2 * n_w]
        send, recv = refs[2 * n_w :]
        _, _, c, q, chips = _place()
        cps = []
        for w in range(n_w):
            for r, chip in enumerate(chips):
                piece = _shard_view(h_refs[w], kinds[w], q ^ _REL_MASK[r])
                cps.append(_remote(piece, o_refs[w].at[r], send.at[w, r], recv.at[w, r], (*chip, c)))
        for cp in cps:
            cp.start()
        for cp in cps:
            cp.wait()

    return pl.pallas_call(
        body,
        name="grad_chip_exchange",
        in_specs=[ANY] * n_w,
        out_specs=[ANY] * n_w,
        out_shape=outs,
        scratch_shapes=[pltpu.SemaphoreType.DMA((n_w, 3)), pltpu.SemaphoreType.DMA((n_w, 3))],
        compiler_params=_params(has_side_effects=True),
    )(*halves)


def _pair_share(pieces, kinds):
    n_w = len(pieces)
    outs = [jax.ShapeDtypeStruct(p.shape, F32) for p in pieces]

    def body(*refs):
        o_refs = refs[n_w : 2 * n_w]
        send, recv = refs[2 * n_w :]
        x, y, c, _, _ = _place()
        sib = (x, y, 1 - c)
        cps = []
        for w in range(n_w):
            mine = _pair_half(o_refs[w], kinds[w], c)
            cps.append(_remote(mine, mine, send.at[w], recv.at[w], sib))
        for cp in cps:
            cp.start()
        for w in range(n_w):
            other = _pair_half(o_refs[w], kinds[w], 1 - c)
            _remote(other, other, send.at[w], recv.at[w], sib).wait_recv()
        for cp in cps:
            cp.wait_send()

    return pl.pallas_call(
        body,
        name="grad_pair_share",
        in_specs=[ANY] * n_w,
        out_specs=[ANY] * n_w,
        out_shape=outs,
        scratch_shapes=[pltpu.SemaphoreType.DMA((n_w,)), pltpu.SemaphoreType.DMA((n_w,))],
        input_output_aliases={w: w for w in range(n_w)},
    )(*pieces)


def _small_all_reduce(p):
    rows = p.shape[0]
    n_dev = 2 * N_CHIPS

    def body(p_ref, o_ref, buf, loc_sem, send, recv):
        x, y, c, q, _ = _place()
        me = 2 * q + c
        own = pltpu.make_async_copy(p_ref, buf.at[me], loc_sem)
        own.start()
        cps = []
        for d in range(1, n_dev):
            dev = (x ^ ((d >> 2) & 1), y ^ ((d >> 1) & 1), c ^ (d & 1))
            cps.append(_remote(p_ref, buf.at[me], send.at[d - 1], recv.at[d - 1], dev))
        for cp in cps:
            cp.start()
        for d in range(1, n_dev):
            slot = buf.at[me ^ d]
            _remote(slot, slot, send.at[d - 1], recv.at[d - 1], (x, y, c)).wait_recv()
        own.wait()
        total = buf[0]
        for d in range(1, n_dev):
            total = total + buf[d]
        o_ref[...] = total
        for cp in cps:
            cp.wait_send()

    return pl.pallas_call(
        body,
        name="small_all_reduce",
        in_specs=[ANY],
        out_specs=pl.BlockSpec(memory_space=pltpu.VMEM),
        out_shape=jax.ShapeDtypeStruct(p.shape, F32),
        scratch_shapes=[
            pltpu.VMEM((n_dev, rows, LANES), F32),
            pltpu.SemaphoreType.DMA,
            pltpu.SemaphoreType.DMA((n_dev - 1,)),
            pltpu.SemaphoreType.DMA((n_dev - 1,)),
        ],
        compiler_params=_params(has_side_effects=True),
    )(p)


def _pack(parts):
    rows = []
    for a in parts:
        flat = a.reshape(-1).astype(F32)
        n = flat.shape[0]
        padded = -(-n // (8 * LANES)) * (8 * LANES)
        rows.append(jnp.pad(flat, (0, padded - n)).reshape(-1, LANES))
    return jnp.concatenate(rows, axis=0)


def _unpack(packed, shapes):
    out, row = [], 0
    for shp in shapes:
        n = int(np.prod(shp))
        nrows = -(-n // (8 * LANES)) * 8
        out.append(packed[row : row + nrows].reshape(-1)[:n].reshape(shp))
        row += nrows
    return out


def kernel(x, w_in, norm_mix, sgu_v_gain, sgu_w_s, sgu_b_s, w_a_out, attn_sink, rel_bias, w_b_out, w_o, norm_ffn, w_gate, w_up, w_down, norm_final, loss_target, m_w_in, m_norm_mix, m_sgu_v_gain, m_sgu_w_s, m_sgu_b_s, m_w_a_out, m_attn_sink, m_rel_bias, m_w_b_out, m_w_o, m_norm_ffn, m_w_gate, m_w_up, m_w_down, m_norm_final, v_w_in, v_norm_mix, v_sgu_v_gain, v_sgu_w_s, v_sgu_b_s, v_w_a_out, v_attn_sink, v_rel_bias, v_w_b_out, v_w_o, v_norm_ffn, v_w_gate, v_w_up, v_w_down, v_norm_final):
    s, d = x.shape[1], x.shape[2]
    w_sgu = sgu_v_gain.shape[1]
    groups = sgu_w_s.shape[1]
    heads = attn_sink.shape[1]
    grp = heads // N_KV_HEADS
    w_att = heads * HEAD_DIM
    w_kv = N_KV_HEADS * HEAD_DIM
    d_ff = w_gate.shape[2] * N_CHIPS
    n_in = w_in.shape[2] * N_CHIPS
    off_q = 2 * w_sgu
    off_k = off_q + w_att
    off_g = off_k + 2 * w_kv
    assert n_in == off_g + 2 * d and groups * BLK == w_sgu and s % BLK == 0

    x2d = x.reshape(s, d)
    tgt = loss_target.reshape(s, d)
    c_idx = lax.axis_index("c").astype(I32).reshape(1)
    q_idx = (2 * lax.axis_index("x") + lax.axis_index("y")).astype(I32).reshape(1)

    big = [("w_in", w_in[0], "col"), ("w_a", w_a_out[0], "col"), ("w_b", w_b_out[0], "col"), ("w_o", w_o[0], "row"),
           ("w_gate", w_gate[0], "col"), ("w_up", w_up[0], "col"), ("w_down", w_down[0], "row")]
    kinds = [k for _, _, k in big]
    shards_b = [_cast_into_full("cast_" + n, q_idx, w, k) for n, w, k in big]
    g_in, g_a, g_b, g_o, g_gate, g_up, g_down = _all_gather(shards_b, kinds)

    ws_b = sgu_w_s[0].astype(BF16)
    wst_b = jnp.swapaxes(sgu_w_s[0], 1, 2).astype(BF16)
    b_col = sgu_b_s[0].reshape(groups, BLK, 1)
    bias_tab, bucket = _band_tables(rel_bias)
    sink = attn_sink[0]

    tm = _tile(s, (1024, 512, 256, 128))

    h1 = _rms_fwd("rms_mix", x2d, norm_mix)

    tn = _tile(n_in, (768, 640, 512))
    z = _mm(
        "mm_z", (s // tm, n_in // tn, 1), [h1, g_in],
        [pl.BlockSpec((tm, d), lambda i, j, k: (i, 0)), pl.BlockSpec((d, tn), lambda i, j, k: (0, j))],
        [jax.ShapeDtypeStruct((s, n_in), F32)], [pl.BlockSpec((tm, tn), lambda i, j, k: (i, j))],
        [(0, 1, NN, 0)], 1, (tm, tn), 1, lambda ins, vals, outs: _store_cast(outs[0], vals[0]),
        _mm_vmem([((tm, d), BF16, 2), ((d, tn), BF16, 2), ((tm, tn), F32, 3)]),
    )[0]

    a_act = _sgu_fwd(z, sgu_v_gain, ws_b, b_col, w_sgu)

    kv_b = z[:, off_k:off_g].astype(BF16)
    k_pad = jnp.pad(kv_b[:, :w_kv], ((BLK, BLK), (0, 0)))
    v_pad = jnp.pad(kv_b[:, w_kv:], ((BLK, BLK), (0, 0)))
    q_blk0 = off_q // (grp * HEAD_DIM)
    att = _attn_fwd(sink, z, k_pad, v_pad, bias_tab, grp, q_blk0)

    tg = _tile(d, (512,))
    ga0, gb0 = off_g // tg, (off_g + d) // tg

    def ep_gate(ins, vals, outs):
        sa, sb = _sigmoid(ins[4][...]), _sigmoid(ins[5][...])
        outs[0][...] = (sa * vals[0] + sb * vals[1]).astype(BF16)
        outs[1][...] = vals[0]
        outs[2][...] = vals[1]

    t_out = pl.BlockSpec((tm, tg), lambda i, j, k: (i, j))
    m_act, y_a, y_b = _mm(
        "mm_branches", (s // tm, d // tg, 1), [a_act, g_a, att, g_b, z, z],
        [pl.BlockSpec((tm, w_sgu), lambda i, j, k: (i, 0)), pl.BlockSpec((w_sgu, tg), lambda i, j, k: (0, j)),
         pl.BlockSpec((tm, w_att), lambda i, j, k: (i, 0)), pl.BlockSpec((w_att, tg), lambda i, j, k: (0, j)),
         pl.BlockSpec((tm, tg), lambda i, j, k: (i, ga0 + j)), pl.BlockSpec((tm, tg), lambda i, j, k: (i, gb0 + j))],
        [jax.ShapeDtypeStruct((s, d), BF16), jax.ShapeDtypeStruct((s, d), F32), jax.ShapeDtypeStruct((s, d), F32)],
        [t_out, t_out, t_out], [(0, 1, NN, 0), (2, 3, NN, 1)], 2, (tm, tg), 1, ep_gate,
        _mm_vmem([((tm, w_sgu), BF16, 4), ((w_sgu, tg), BF16, 4), ((tm, tg), F32, 12)]),
    )

    tn = _tile(d, (1024, 512))

    def ep_residual(ins, vals, outs):
        outs[0][...] = ins[2][...] + vals[0]

    x2 = _mm(
        "mm_wo", (s // tm, d // tn, 1), [m_act, g_o, x2d],
        [pl.BlockSpec((tm, d), lambda i, j, k: (i, 0)), pl.BlockSpec((d, tn), lambda i, j, k: (0, j)),
         pl.BlockSpec((tm, tn), lambda i, j, k: (i, j))],
        [jax.ShapeDtypeStruct((s, d), F32)], [pl.BlockSpec((tm, tn), lambda i, j, k: (i, j))],
        [(0, 1, NN, 0)], 1, (tm, tn), 1, ep_residual,
        _mm_vmem([((tm, d), BF16, 2), ((d, tn), BF16, 2), ((tm, tn), F32, 5)]),
    )[0]

    h2 = _rms_fwd("rms_ffn", x2, norm_ffn)

    tf = _tile(d_ff, (512,))

    def ep_swiglu(ins, vals, outs):
        gt, up = vals
        outs[0][...] = gt
        outs[1][...] = up
        outs[2][...] = ((gt * _sigmoid(gt)) * up).astype(BF16)

    f_out = pl.BlockSpec((tm, tf), lambda i, j, k: (i, j))
    gt, up, f_act = _mm(
        "mm_gate_up", (s // tm, d_ff // tf, 1), [h2, g_gate, g_up],
        [pl.BlockSpec((tm, d), lambda i, j, k: (i, 0)), pl.BlockSpec((d, tf), lambda i, j, k: (0, j)),
         pl.BlockSpec((d, tf), lambda i, j, k: (0, j))],
        [jax.ShapeDtypeStruct((s, d_ff), F32), jax.ShapeDtypeStruct((s, d_ff), F32), jax.ShapeDtypeStruct((s, d_ff), BF16)],
        [f_out, f_out, f_out], [(0, 1, NN, 0), (0, 2, NN, 1)], 2, (tm, tf), 1, ep_swiglu,
        _mm_vmem([((tm, d), BF16, 2), ((d, tf), BF16, 4), ((tm, tf), F32, 8)]),
    )

    tkf = _tile(d_ff, (1408, 1024, 512))
    nkf = d_ff // tkf
    x3 = _mm(
        "mm_down", (s // tm, d // tn, nkf), [f_act, g_down, x2],
        [pl.BlockSpec((tm, tkf), lambda i, j, k: (i, k)), pl.BlockSpec((tkf, tn), lambda i, j, k: (k, j)),
         pl.BlockSpec((tm, tn), lambda i, j, k: (i, j))],
        [jax.ShapeDtypeStruct((s, d), F32)], [pl.BlockSpec((tm, tn), lambda i, j, k: (i, j))],
        [(0, 1, NN, 0)], 1, (tm, tn), nkf, ep_residual,
        _mm_vmem([((tm, tkf), BF16, 2), ((tkf, tn), BF16, 2), ((tm, tn), F32, 6)]),
    )[0]

    dx3, dx3b, dg_final, loss_part = _head(x3, norm_final.reshape(1, d), tgt)

    def ep_swiglu_bwd(ins, vals, outs):
        df = vals[0]
        gtv, upv = ins[2][...], ins[3][...]
        sg = _sigmoid(gtv)
        outs[0][...] = (df * upv * (sg + gtv * sg * (1.0 - sg))).astype(BF16)
        outs[1][...] = (df * (gtv * sg)).astype(BF16)

    dgt, dup = _mm(
        "mm_dswiglu", (s // tm, d_ff // tf, 1), [dx3b, g_down, gt, up],
        [pl.BlockSpec((tm, d), lambda i, j, k: (i, 0)), pl.BlockSpec((tf, d), lambda i, j, k: (j, 0)), f_out, f_out],
        [jax.ShapeDtypeStruct((s, d_ff), BF16), jax.ShapeDtypeStruct((s, d_ff), BF16)], [f_out, f_out],
        [(0, 1, NT, 0)], 1, (tm, tf), 1, ep_swiglu_bwd,
        _mm_vmem([((tm, d), BF16, 2), ((tf, d), BF16, 2), ((tm, tf), F32, 8)]),
    )

    def ep_bf16(ins, vals, outs):
        for o, v in zip(outs, vals):
            o[...] = v.astype(BF16)

    def ep_f32(ins, vals, outs):
        for o, v in zip(outs, vals):
            o[...] = v

    twn = _tile(d, (1024, 512))
    gw_down = _mm(
        "mm_gw_down", (d_ff // tkf, d // twn, 1), [f_act, dx3b],
        [pl.BlockSpec((s, tkf), lambda i, j, k: (0, i)), pl.BlockSpec((s, twn), lambda i, j, k: (0, j))],
        [jax.ShapeDtypeStruct((d_ff, d), BF16)], [pl.BlockSpec((tkf, twn), lambda i, j, k: (i, j))],
        [(0, 1, TN, 0)], 1, (tkf, twn), 1, ep_bf16,
        _mm_vmem([((s, tkf), BF16, 3), ((s, twn), BF16, 2), ((tkf, twn), F32, 3)]),
    )[0]

    dh2 = _mm(
        "mm_dh2", (s // tm, d // tn, nkf), [dgt, g_gate, dup, g_up],
        [pl.BlockSpec((tm, tkf), lambda i, j, k: (i, k)), pl.BlockSpec((tn, tkf), lambda i, j, k: (j, k)),
         pl.BlockSpec((tm, tkf), lambda i, j, k: (i, k)), pl.BlockSpec((tn, tkf), lambda i, j, k: (j, k))],
        [jax.ShapeDtypeStruct((s, d), F32)], [pl.BlockSpec((tm, tn), lambda i, j, k: (i, j))],
        [(0, 1, NT, 0), (2, 3, NT, 0)], 1, (tm, tn), nkf, ep_f32,
        _mm_vmem([((tm, tkf), BF16, 4), ((tn, tkf), BF16, 4), ((tm, tn), F32, 5)]),
    )[0]

    twr = _tile(d, (1024, 512))
    w_tile = pl.BlockSpec((twr, tf), lambda i, j, k: (i, j))
    gw_gate, gw_up = _mm(
        "mm_gw_gate_up", (d // twr, d_ff // tf, 1), [h2, dgt, dup],
        [pl.BlockSpec((s, twr), lambda i, j, k: (0, i)), pl.BlockSpec((s, tf), lambda i, j, k: (0, j)),
         pl.BlockSpec((s, tf), lambda i, j, k: (0, j))],
        [jax.ShapeDtypeStruct((d, d_ff), BF16), jax.ShapeDtypeStruct((d, d_ff), BF16)], [w_tile, w_tile],
        [(0, 1, TN, 0), (0, 2, TN, 1)], 2, (twr, tf), 1, ep_bf16,
        _mm_vmem([((s, twr), BF16, 3), ((s, tf), BF16, 4), ((twr, tf), F32, 6)]),
    )

    dx2, dx2b, dg_ffn = _rms_bwd("rms_ffn_bwd", x2, norm_ffn, dh2, dx3)

    nj = d // tg

    def gate_bwd_body(dx_ref, wo_ref, ga_ref, gb_ref, ya_ref, yb_ref, dz_in, dya_ref, dyb_ref, dz_ref, keep):
        del dz_in
        j = pl.program_id(1)

        @pl.when(j < nj)
        def _():
            dm = lax.dot_general(dx_ref[...], wo_ref[...], NT, preferred_element_type=F32)
            sa, sb = _sigmoid(ga_ref[...]), _sigmoid(gb_ref[...])
            dya_ref[...] = (dm * sa).astype(BF16)
            dyb_ref[...] = (dm * sb).astype(BF16)
            dz_ref[...] = (dm * ya_ref[...] * (sa * (1.0 - sa))).astype(BF16)
            keep[jnp.minimum(j, nj - 1)] = (dm * yb_ref[...] * (sb * (1.0 - sb))).astype(BF16)

        @pl.when(j >= nj)
        def _():
            dz_ref[...] = keep[jnp.maximum(j - nj, 0)]

    def lo(j):
        return jnp.minimum(j, nj - 1)

    def gate_bwd(dz):
        t_lo = pl.BlockSpec((tm, tg), lambda i, j: (i, lo(j)))
        return pl.pallas_call(
            gate_bwd_body,
            name="mm_dgate",
            grid=(s // tm, 2 * nj),
            in_specs=[
                pl.BlockSpec((tm, d), lambda i, j: (i, 0)),
                pl.BlockSpec((tg, d), lambda i, j: (lo(j), 0)),
                pl.BlockSpec((tm, tg), lambda i, j: (i, ga0 + lo(j))),
                pl.BlockSpec((tm, tg), lambda i, j: (i, gb0 + lo(j))),
                t_lo,
                t_lo,
                ANY,
            ],
            out_specs=[t_lo, t_lo, pl.BlockSpec((tm, tg), lambda i, j: (i, ga0 + j))],
            out_shape=[jax.ShapeDtypeStruct((s, d), BF16), jax.ShapeDtypeStruct((s, d), BF16), jax.ShapeDtypeStruct(dz.shape, BF16)],
            scratch_shapes=[pltpu.VMEM((nj, tm, tg), BF16)],
            input_output_aliases={6: 2},
            compiler_params=_params(_mm_vmem([((tm, d), BF16, 2), ((tg, d), BF16, 2), ((tm, tg), F32, 14), ((nj, tm, tg), BF16, 1)])),
        )(dx2b, g_o, z, z, y_a, y_b, dz)

    gw_o = _mm(
        "mm_gw_o", (d // twr, d // twn, 1), [m_act, dx2b],
        [pl.BlockSpec((s, twr), lambda i, j, k: (0, i)), pl.BlockSpec((s, twn), lambda i, j, k: (0, j))],
        [jax.ShapeDtypeStruct((d, d), BF16)], [pl.BlockSpec((twr, twn), lambda i, j, k: (i, j))],
        [(0, 1, TN, 0)], 1, (twr, twn), 1, ep_bf16,
        _mm_vmem([((s, twr), BF16, 3), ((s, twn), BF16, 2), ((twr, twn), F32, 3)]),
    )[0]

    dz0 = jnp.zeros((8, LANES), BF16)
    del dz0

    def gate_bwd_create():
        t_lo = pl.BlockSpec((tm, tg), lambda i, j: (i, lo(j)))

        def body(dx_ref, wo_ref, ga_ref, gb_ref, ya_ref, yb_ref, dya_ref, dyb_ref, dz_ref, keep):
            gate_bwd_body(dx_ref, wo_ref, ga_ref, gb_ref, ya_ref, yb_ref, None, dya_ref, dyb_ref, dz_ref, keep)

        return pl.pallas_call(
            body,
            name="mm_dgate",
            grid=(s // tm, 2 * nj),
            in_specs=[
                pl.BlockSpec((tm, d), lambda i, j: (i, 0)),
                pl.BlockSpec((tg, d), lambda i, j: (lo(j), 0)),
                pl.BlockSpec((tm, tg), lambda i, j: (i, ga0 + lo(j))),
                pl.BlockSpec((tm, tg), lambda i, j: (i, gb0 + lo(j))),
                t_lo,
                t_lo,
            ],
            out_specs=[t_lo, t_lo, pl.BlockSpec((tm, tg), lambda i, j: (i, ga0 + j))],
            out_shape=[jax.ShapeDtypeStruct((s, d), BF16), jax.ShapeDtypeStruct((s, d), BF16), jax.ShapeDtypeStruct((s, n_in), BF16)],
            scratch_shapes=[pltpu.VMEM((nj, tm, tg), BF16)],
            compiler_params=_params(_mm_vmem([((tm, d), BF16, 2), ((tg, d), BF16, 2), ((tm, tg), F32, 14), ((nj, tm, tg), BF16, 1)])),
        )(dx2b, g_o, z, z, y_a, y_b)

    del gate_bwd
    dya, dyb, dz = gate_bwd_create()

    tb = _tile(w_sgu, (1024, 512))
    b_out = pl.BlockSpec((tm, tb), lambda i, j, k: (i, j))

    def ep_branch_bwd(ins, vals, outs):
        outs[0][...] = vals[0]
        outs[1][...] = vals[1].astype(BF16)

    da, datt = _mm(
        "mm_dbranches", (s // tm, w_sgu // tb, 1), [dya, g_a, dyb, g_b],
        [pl.BlockSpec((tm, d), lambda i, j, k: (i, 0)), pl.BlockSpec((tb, d), lambda i, j, k: (j, 0)),
         pl.BlockSpec((tm, d), lambda i, j, k: (i, 0)), pl.BlockSpec((tb, d), lambda i, j, k: (j, 0))],
        [jax.ShapeDtypeStruct((s, w_sgu), F32), jax.ShapeDtypeStruct((s, w_att), BF16)], [b_out, b_out],
        [(0, 1, NT, 0), (2, 3, NT, 1)], 2, (tm, tb), 1, ep_branch_bwd,
        _mm_vmem([((tm, d), BF16, 4), ((tb, d), BF16, 4), ((tm, tb), F32, 6)]),
    )

    wb_tile = pl.BlockSpec((tb, twn), lambda i, j, k: (i, j))
    gw_a, gw_b = _mm(
        "mm_gw_branches", (w_sgu // tb, d // twn, 1), [a_act, dya, att, dyb],
        [pl.BlockSpec((s, tb), lambda i, j, k: (0, i)), pl.BlockSpec((s, twn), lambda i, j, k: (0, j)),
         pl.BlockSpec((s, tb), lambda i, j, k: (0, i)), pl.BlockSpec((s, twn), lambda i, j, k: (0, j))],
        [jax.ShapeDtypeStruct((w_sgu, d), BF16), jax.ShapeDtypeStruct((w_att, d), BF16)], [wb_tile, wb_tile],
        [(0, 1, TN, 0), (2, 3, TN, 1)], 2, (tb, twn), 1, ep_bf16,
        _mm_vmem([((s, tb), BF16, 5), ((s, twn), BF16, 4), ((tb, twn), F32, 6)]),
    )

    dz, dws, dbs, dgain = _sgu_bwd_into(z, da, sgu_v_gain, ws_b, wst_b, b_col, w_sgu, dz)
    dz, dk_pad, dv_pad, dbias_tab, dsink = _attn_bwd(sink, z, k_pad, v_pad, bias_tab, datt, dz, grp, q_blk0)
    dz = _dkv_to_dz(dk_pad, dv_pad, dz, off_k // (2 * w_kv))
    drel = _relbias_bwd(dbias_tab, bucket)

    tkz = _tile(n_in, (1920, 1536, 1280, 1024))
    nkz = n_in // tkz
    dh1 = _mm(
        "mm_dh1", (s // tm, d // tn, nkz), [dz, g_in],
        [pl.BlockSpec((tm, tkz), lambda i, j, k: (i, k)), pl.BlockSpec((tn, tkz), lambda i, j, k: (j, k))],
        [jax.ShapeDtypeStruct((s, d), F32)], [pl.BlockSpec((tm, tn), lambda i, j, k: (i, j))],
        [(0, 1, NT, 0)], 1, (tm, tn), nkz, ep_f32,
        _mm_vmem([((tm, tkz), BF16, 2), ((tn, tkz), BF16, 2), ((tm, tn), F32, 5)]),
    )[0]

    tzn = _tile(n_in, (768, 640, 512))
    gw_in = _mm(
        "mm_gw_in", (d // twr, n_in // tzn, 1), [h1, dz],
        [pl.BlockSpec((s, twr), lambda i, j, k: (0, i)), pl.BlockSpec((s, tzn), lambda i, j, k: (0, j))],
        [jax.ShapeDtypeStruct((d, n_in), BF16)], [pl.BlockSpec((twr, tzn), lambda i, j, k: (i, j))],
        [(0, 1, TN, 0)], 1, (twr, tzn), 1, ep_bf16,
        _mm_vmem([((s, twr), BF16, 3), ((s, tzn), BF16, 2), ((twr, tzn), F32, 3)]),
    )[0]

    grad_x, _, dg_mix = _rms_bwd("rms_mix_bwd", x2d, norm_mix, dh1, dx2)

    full_grads = [gw_in, gw_a, gw_b, gw_o, gw_gate, gw_up, gw_down]
    from_sib = _pair_exchange(full_grads, kinds)
    names = [n for n, _, _ in big]
    halves = [_pair_add("pair_add_" + n, c_idx, g, r, k) for n, g, r, k in zip(names, full_grads, from_sib, kinds)]
    from_chips = _chip_exchange(halves, kinds)
    qc_idx = jnp.concatenate([q_idx, c_idx])
    pieces = [_chip_sum("chip_sum_" + n, qc_idx, h, r, k) for n, h, r, k in zip(names, halves, from_chips, kinds)]
    grads_big = _pair_share(pieces, kinds)

    small_w = [norm_mix, sgu_v_gain, sgu_w_s, sgu_b_s, attn_sink, rel_bias, norm_ffn, norm_final]
    small_m = [m_norm_mix, m_sgu_v_gain, m_sgu_w_s, m_sgu_b_s, m_attn_sink, m_rel_bias, m_norm_ffn, m_norm_final]
    small_v = [v_norm_mix, v_sgu_v_gain, v_sgu_w_s, v_sgu_b_s, v_attn_sink, v_rel_bias, v_norm_ffn, v_norm_final]
    small_shapes = [w.shape for w in small_w]
    local_small = [dg_mix, dgain, dws, dbs, dsink[:, 0], drel[:, :REL_BUCKETS].T, dg_ffn, dg_final]
    local_small = [g.reshape(shp) for g, shp in zip(local_small, small_shapes)]
    packed_g = _small_all_reduce(_pack(local_small + [loss_part[0, :1]]))
    g_small = _unpack(packed_g, small_shapes + [(1,)])
    loss = g_small[-1].reshape(())
    g_small = g_small[:-1]
    zero1 = jnp.zeros((1,), F32)
    pw, pg, pm, pv = _pack(small_w + [zero1]), _pack(g_small + [zero1]), _pack(small_m + [zero1]), _pack(small_v + [zero1])
    d_small, nm_small, nv_small = [_unpack(a, small_shapes) for a in _adamw("adamw_small", pw, pg, pm, pv)]

    big_m = [m_w_in, m_w_a_out, m_w_b_out, m_w_o, m_w_gate, m_w_up, m_w_down]
    big_v = [v_w_in, v_w_a_out, v_w_b_out, v_w_o, v_w_gate, v_w_up, v_w_down]
    upd = [_adamw("adamw_" + n, w, g, mm[0], vv[0]) for (n, w, _), g, mm, vv in zip(big, grads_big, big_m, big_v)]

    order = ["w_in", "norm_mix", "sgu_v_gain", "sgu_w_s", "sgu_b_s", "w_a", "attn_sink", "rel_bias", "w_b", "w_o", "norm_ffn",
             "w_gate", "w_up", "w_down", "norm_final"]
    small_names = ["norm_mix", "sgu_v_gain", "sgu_w_s", "sgu_b_s", "attn_sink", "rel_bias", "norm_ffn", "norm_final"]
    table = {}
    for i, n in enumerate(names):
        table[n] = (grads_big[i][None], upd[i][0][None], upd[i][1][None], upd[i][2][None])
    for i, n in enumerate(small_names):
        table[n] = (g_small[i], d_small[i], nm_small[i], nv_small[i])
    outs = [loss, grad_x.reshape(1, s, d)]
    for part in range(4):
        outs += [table[n][part] for n in order]
    return tuple(outs)


def _sgu_bwd_into(z, da, gain, ws_b, wst_b, b_col, w_sgu, dz):
    s = z.shape[0]
    groups = ws_b.shape[0]
    n = s // BLK

    def body(zu_ref, zv_ref, da_ref, gain_ref, ws_ref, wst_ref, b_ref, dz_in, dz_ref, dws_ref, dbs_ref, dgain_ref, acc_gain):
        del dz_in
        c = pl.program_id(0)
        zu = zu_ref[...]
        zv = zv_ref[...]
        gain_v = gain_ref[...]
        vv = _gelu(zv)
        r = lax.rsqrt(jnp.mean(vv * vv, axis=-1, keepdims=True) + EPS)
        xh = vv * r
        vn = (xh * gain_v).astype(BF16)
        u = _gelu(zu)
        dav = da_ref[...]
        dmix = dav * u
        dmix_b = dmix.astype(BF16)
        dvn_parts = []
        for g in range(groups):
            sl = slice(g * BLK, (g + 1) * BLK)
            mixed = jnp.dot(ws_ref[g], vn[:, sl], preferred_element_type=F32) + b_ref[g]
            dz_ref[:, sl] = (dav[:, sl] * mixed * _gelu_grad(zu[:, sl])).astype(BF16)
            dvn_parts.append(jnp.dot(wst_ref[g], dmix_b[:, sl], preferred_element_type=F32))
            dws_g = lax.dot_general(dmix_b[:, sl], vn[:, sl], NT, preferred_element_type=F32)
            dbs_g = jnp.sum(dmix[:, sl], axis=1, keepdims=True)

            @pl.when(c == 0)
            def _():
                dws_ref[g] = dws_g
                dbs_ref[g] = dbs_g

            @pl.when(c > 0)
            def _():
                dws_ref[g] += dws_g
                dbs_ref[g] += dbs_g

        dvn = jnp.concatenate(dvn_parts, axis=1)
        dxh = dvn * gain_v
        dvv = r * (dxh - xh * jnp.mean(dxh * xh, axis=-1, keepdims=True))
        dz_ref[:, w_sgu:] = (dvv * _gelu_grad(zv)).astype(BF16)
        pg = _rows8(dvn * xh)

        @pl.when(c == 0)
        def _():
            acc_gain[...] = pg

        @pl.when(c > 0)
        def _():
            acc_gain[...] += pg

        @pl.when(c == n - 1)
        def _():
            dgain_ref[...] = jnp.sum(acc_gain[...], axis=0, keepdims=True)

    full3 = pl.BlockSpec((groups, BLK, BLK), lambda c: (0, 0, 0))
    col3 = pl.BlockSpec((groups, BLK, 1), lambda c: (0, 0, 0))
    vec = pl.BlockSpec((1, w_sgu), lambda c: (0, 0))
    return pl.pallas_call(
        body,
        name="sgu_bwd",
        grid=(n,),
        in_specs=[
            pl.BlockSpec((BLK, w_sgu), lambda c: (c, 0)),
            pl.BlockSpec((BLK, w_sgu), lambda c: (c, 1)),
            pl.BlockSpec((BLK, w_sgu), lambda c: (c, 0)),
            vec,
            full3,
            full3,
            col3,
            ANY,
        ],
        out_specs=[pl.BlockSpec((BLK, 2 * w_sgu), lambda c: (c, 0)), full3, col3, vec],
        out_shape=[
            jax.ShapeDtypeStruct(dz.shape, BF16),
            jax.ShapeDtypeStruct((groups, BLK, BLK), F32),
            jax.ShapeDtypeStruct((groups, BLK, 1), F32),
            jax.ShapeDtypeStruct((1, w_sgu), F32),
        ],
        scratch_shapes=[pltpu.VMEM((8, w_sgu), F32)],
        input_output_aliases={7: 0},
    )(z, z, da, gain, ws_b, wst_b, b_col, dz)
```

```python
import math

import jax
import jax.numpy as jnp
import numpy as np
from jax import lax
from jax.experimental import pallas as pl
from jax.experimental.pallas import tpu as pltpu

F32 = jnp.float32
BF16 = jnp.bfloat16
I32 = jnp.int32
MESH = pl.DeviceIdType.MESH

EPS = 1e-6
NEG = -1e30
BLK = 128
HEAD_DIM = 128
N_KV_HEADS = 2
REL_BUCKETS = 32
REL_MAX_DIST = 128
N_CHIPS = 4
ADAM_LR, ADAM_B1, ADAM_B2, ADAM_EPS, ADAM_WD, ADAM_STEP = 0.001, 0.9, 0.999, 1e-08, 0.01, 10

LANES = 128
VMEM_CAP = 60 * 1024 * 1024

NN = (((1,), (0,)), ((), ()))
NT = (((1,), (1,)), ((), ()))
TN = (((0,), (0,)), ((), ()))
ANY = pl.BlockSpec(memory_space=pl.ANY)
HBM_SPEC = pl.BlockSpec(memory_space=pltpu.HBM)
SEM_SPEC = pl.BlockSpec(memory_space=pltpu.SEMAPHORE)
EFFECT = pltpu.SideEffectType.DATAFLOW_SIDE_EFFECTING


def _tile(n, cands):
    for t in cands:
        if n % t == 0:
            return t
    return n


def _params(vmem_bytes=None, **kw):
    if vmem_bytes is not None:
        kw["vmem_limit_bytes"] = int(min(max(vmem_bytes, 32 * 1024 * 1024), VMEM_CAP))
    return pltpu.CompilerParams(**kw)


def _nbytes(shape, dtype):
    return int(np.prod(shape)) * jnp.dtype(dtype).itemsize


def _sigmoid(x):
    return 1.0 / (1.0 + jnp.exp(-x))


_GC = 0.7978845608028654
_GA = 0.044715


def _gelu(x):
    return 0.5 * x * (1.0 + jnp.tanh(_GC * (x + _GA * (x * x * x))))


def _gelu_grad(x):
    t = jnp.tanh(_GC * (x + _GA * (x * x * x)))
    return 0.5 * (1.0 + t) + 0.5 * x * (1.0 - t * t) * (_GC * (1.0 + 3.0 * _GA * (x * x)))


def _bf(v):
    return v if v.dtype == BF16 else v.astype(BF16)


def _mm(name, grid, ins, in_specs, out_shape, out_specs, pairs, n_acc, acc_tile, nk, epilogue, vmem_bytes, after=()):
    n_in, n_out = len(ins) + len(after), len(out_shape)

    def body(*refs):
        in_refs, out_refs, accs = refs[:n_in], refs[n_in : n_in + n_out], refs[n_in + n_out :]

        def products():
            vals = [None] * n_acc
            for a_i, b_i, dn, acc_i in pairs:
                d = lax.dot_general(_bf(in_refs[a_i][...]), _bf(in_refs[b_i][...]), dn, preferred_element_type=F32)
                vals[acc_i] = d if vals[acc_i] is None else vals[acc_i] + d
            return vals

        if nk == 1:
            epilogue(in_refs, products(), out_refs)
            return
        k = pl.program_id(2)
        vals = products()

        @pl.when(k == 0)
        def _():
            for a, v in zip(accs, vals):
                a[...] = v

        @pl.when(k > 0)
        def _():
            for a, v in zip(accs, vals):
                a[...] += v

        @pl.when(k == nk - 1)
        def _():
            epilogue(in_refs, [a[...] for a in accs], out_refs)

    scratch = [pltpu.VMEM(acc_tile, F32) for _ in range(n_acc)] if nk > 1 else []
    return pl.pallas_call(
        body,
        name=name,
        grid=grid,
        in_specs=list(in_specs) + [ANY] * len(after),
        out_specs=out_specs,
        out_shape=out_shape,
        scratch_shapes=scratch,
        compiler_params=_params(vmem_bytes),
    )(*ins, *after)


def _mm_vmem(tiles):
    return sum(_nbytes(s, d) * c for s, d, c in tiles) + 4 * 1024 * 1024


def _store_cast(ref, v):
    ref[...] = v.astype(ref.dtype)


def _rows8(v):
    r, d = v.shape
    return v.reshape(r // 8, 8, d).sum(axis=0)


def _rms_fwd(name, x, g, after=()):
    s, d = x.shape
    tm = _tile(s, (256, 128))

    def body(x_ref, g_ref, *rest):
        h_ref = rest[-1]
        xv = x_ref[...]
        r = lax.rsqrt(jnp.mean(xv * xv, axis=-1, keepdims=True) + EPS)
        h_ref[...] = ((xv * r) * g_ref[...]).astype(BF16)

    return pl.pallas_call(
        body,
        name=name,
        grid=(s // tm,),
        in_specs=[pl.BlockSpec((tm, d), lambda i: (i, 0)), pl.BlockSpec((1, d), lambda i: (0, 0))] + [ANY] * len(after),
        out_specs=pl.BlockSpec((tm, d), lambda i: (i, 0)),
        out_shape=jax.ShapeDtypeStruct((s, d), BF16),
    )(x, g, *after)


def _rms_bwd(name, x, g, dh, dres, after=()):
    s, d = x.shape
    tm = _tile(s, (256, 128))
    n = s // tm
    n_after = len(after)

    def body(x_ref, g_ref, dh_ref, dres_ref, *rest):
        dx_ref, dxb_ref, dg_ref, acc_ref = rest[n_after:]
        i = pl.program_id(0)
        xv = x_ref[...]
        r = lax.rsqrt(jnp.mean(xv * xv, axis=-1, keepdims=True) + EPS)
        xh = xv * r
        dhv = dh_ref[...]
        dxh = dhv * g_ref[...]
        dx = r * (dxh - xh * jnp.mean(dxh * xh, axis=-1, keepdims=True)) + dres_ref[...]
        dx_ref[...] = dx
        dxb_ref[...] = dx.astype(BF16)
        part = _rows8(dhv * xh)

        @pl.when(i == 0)
        def _():
            acc_ref[...] = part

        @pl.when(i > 0)
        def _():
            acc_ref[...] += part

        @pl.when(i == n - 1)
        def _():
            dg_ref[...] = jnp.sum(acc_ref[...], axis=0, keepdims=True)

    row = pl.BlockSpec((tm, d), lambda i: (i, 0))
    vec = pl.BlockSpec((1, d), lambda i: (0, 0))
    return pl.pallas_call(
        body,
        name=name,
        grid=(n,),
        in_specs=[row, vec, row, row] + [ANY] * n_after,
        out_specs=[row, row, vec],
        out_shape=[jax.ShapeDtypeStruct((s, d), F32), jax.ShapeDtypeStruct((s, d), BF16), jax.ShapeDtypeStruct((1, d), F32)],
        scratch_shapes=[pltpu.VMEM((8, d), F32)],
    )(x, g, dh, dres, *after)


def _head(x3, g, target):
    s, d = x3.shape
    tm = _tile(s, (256, 128))
    n = s // tm

    def body(x_ref, g_ref, t_ref, dx_ref, dxb_ref, dg_ref, loss_ref, acc_g, acc_l):
        i = pl.program_id(0)
        xv = x_ref[...]
        gv = g_ref[...]
        r = lax.rsqrt(jnp.mean(xv * xv, axis=-1, keepdims=True) + EPS)
        xh = xv * r
        e = xh * gv - t_ref[...]
        dy = e * (1.0 / d)
        dxh = dy * gv
        dx = r * (dxh - xh * jnp.mean(dxh * xh, axis=-1, keepdims=True))
        dx_ref[...] = dx
        dxb_ref[...] = dx.astype(BF16)
        pg = _rows8(dy * xh)
        plo = _rows8(e * e)

        @pl.when(i == 0)
        def _():
            acc_g[...] = pg
            acc_l[...] = plo

        @pl.when(i > 0)
        def _():
            acc_g[...] += pg
            acc_l[...] += plo

        @pl.when(i == n - 1)
        def _():
            dg_ref[...] = jnp.sum(acc_g[...], axis=0, keepdims=True)
            loss_ref[...] = jnp.full((1, LANES), (0.5 / d) * jnp.sum(acc_l[...]), F32)

    row = pl.BlockSpec((tm, d), lambda i: (i, 0))
    vec = pl.BlockSpec((1, d), lambda i: (0, 0))
    return pl.pallas_call(
        body,
        name="head",
        grid=(n,),
        in_specs=[row, vec, row],
        out_specs=[row, row, vec, pl.BlockSpec((1, LANES), lambda i: (0, 0))],
        out_shape=[
            jax.ShapeDtypeStruct((s, d), F32),
            jax.ShapeDtypeStruct((s, d), BF16),
            jax.ShapeDtypeStruct((1, d), F32),
            jax.ShapeDtypeStruct((1, LANES), F32),
        ],
        scratch_shapes=[pltpu.VMEM((8, d), F32), pltpu.VMEM((8, d), F32)],
    )(x3, g, target)


def _sgu_fwd(z, gain, ws_b, b_col, w_sgu):
    s = z.shape[0]
    groups = ws_b.shape[0]

    def body(zu_ref, zv_ref, gain_ref, ws_ref, b_ref, a_ref):
        vv = _gelu(zv_ref[...])
        r = lax.rsqrt(jnp.mean(vv * vv, axis=-1, keepdims=True) + EPS)
        vn = ((vv * r) * gain_ref[...]).astype(BF16)
        u = _gelu(zu_ref[...])
        for g in range(groups):
            sl = slice(g * BLK, (g + 1) * BLK)
            mixed = jnp.dot(ws_ref[g], vn[:, sl], preferred_element_type=F32) + b_ref[g]
            a_ref[:, sl] = (u[:, sl] * mixed).astype(BF16)

    return pl.pallas_call(
        body,
        name="sgu_fwd",
        grid=(s // BLK,),
        in_specs=[
            pl.BlockSpec((BLK, w_sgu), lambda c: (c, 0)),
            pl.BlockSpec((BLK, w_sgu), lambda c: (c, 1)),
            pl.BlockSpec((1, w_sgu), lambda c: (0, 0)),
            pl.BlockSpec((groups, BLK, BLK), lambda c: (0, 0, 0)),
            pl.BlockSpec((groups, BLK, 1), lambda c: (0, 0, 0)),
        ],
        out_specs=pl.BlockSpec((BLK, w_sgu), lambda c: (c, 0)),
        out_shape=jax.ShapeDtypeStruct((s, w_sgu), BF16),
    )(z, z, gain, ws_b, b_col)


def _sgu_bwd(z, da, gain, ws_b, wst_b, b_col, w_sgu, dz, after=()):
    s = z.shape[0]
    groups = ws_b.shape[0]
    n = s // BLK
    n_skip = 1 + len(after)

    def body(zu_ref, zv_ref, da_ref, gain_ref, ws_ref, wst_ref, b_ref, *rest):
        dz_ref, dws_ref, dbs_ref, dgain_ref, acc_gain = rest[n_skip:]
        c = pl.program_id(0)
        zu = zu_ref[...]
        zv = zv_ref[...]
        gain_v = gain_ref[...]
        vv = _gelu(zv)
        r = lax.rsqrt(jnp.mean(vv * vv, axis=-1, keepdims=True) + EPS)
        xh = vv * r
        vn = (xh * gain_v).astype(BF16)
        u = _gelu(zu)
        dav = da_ref[...]
        dmix = dav * u
        dmix_b = dmix.astype(BF16)
        dvn_parts = []
        for g in range(groups):
            sl = slice(g * BLK, (g + 1) * BLK)
            mixed = jnp.dot(ws_ref[g], vn[:, sl], preferred_element_type=F32) + b_ref[g]
            dz_ref[:, sl] = (dav[:, sl] * mixed * _gelu_grad(zu[:, sl])).astype(BF16)
            dvn_parts.append(jnp.dot(wst_ref[g], dmix_b[:, sl], preferred_element_type=F32))
            dws_g = lax.dot_general(dmix_b[:, sl], vn[:, sl], NT, preferred_element_type=F32)
            dbs_g = jnp.sum(dmix[:, sl], axis=1, keepdims=True)

            @pl.when(c == 0)
            def _():
                dws_ref[g] = dws_g
                dbs_ref[g] = dbs_g

            @pl.when(c > 0)
            def _():
                dws_ref[g] += dws_g
                dbs_ref[g] += dbs_g

        dvn = jnp.concatenate(dvn_parts, axis=1)
        dxh = dvn * gain_v
        dvv = r * (dxh - xh * jnp.mean(dxh * xh, axis=-1, keepdims=True))
        dz_ref[:, w_sgu:] = (dvv * _gelu_grad(zv)).astype(BF16)
        pg = _rows8(dvn * xh)

        @pl.when(c == 0)
        def _():
            acc_gain[...] = pg

        @pl.when(c > 0)
        def _():
            acc_gain[...] += pg

        @pl.when(c == n - 1)
        def _():
            dgain_ref[...] = jnp.sum(acc_gain[...], axis=0, keepdims=True)

    full3 = pl.BlockSpec((groups, BLK, BLK), lambda c: (0, 0, 0))
    col3 = pl.BlockSpec((groups, BLK, 1), lambda c: (0, 0, 0))
    vec = pl.BlockSpec((1, w_sgu), lambda c: (0, 0))
    return pl.pallas_call(
        body,
        name="sgu_bwd",
        grid=(n,),
        in_specs=[
            pl.BlockSpec((BLK, w_sgu), lambda c: (c, 0)),
            pl.BlockSpec((BLK, w_sgu), lambda c: (c, 1)),
            pl.BlockSpec((BLK, w_sgu), lambda c: (c, 0)),
            vec,
            full3,
            full3,
            col3,
            ANY,
        ]
        + [ANY] * len(after),
        out_specs=[pl.BlockSpec((BLK, 2 * w_sgu), lambda c: (c, 0)), full3, col3, vec],
        out_shape=[
            jax.ShapeDtypeStruct(dz.shape, BF16),
            jax.ShapeDtypeStruct((groups, BLK, BLK), F32),
            jax.ShapeDtypeStruct((groups, BLK, 1), F32),
            jax.ShapeDtypeStruct((1, w_sgu), F32),
        ],
        scratch_shapes=[pltpu.VMEM((8, w_sgu), F32)],
        input_output_aliases={7: 0},
    )(z, z, da, gain, ws_b, wst_b, b_col, dz, *after)


def _attn_softmax(sink_ref, q_ref, k_ref, v_ref, bias_ref, s_len, grp):
    kv = pl.program_id(0)
    n = pl.program_id(1)
    start = pl.multiple_of(n * BLK, BLK)
    kb = k_ref[pl.ds(start, 3 * BLK), :]
    vb = v_ref[pl.ds(start, 3 * BLK), :]
    qv = q_ref[...]
    qs = jnp.concatenate([qv[:, g * HEAD_DIM : (g + 1) * HEAD_DIM] for g in range(grp)], axis=0).astype(BF16)
    sc = lax.dot_general(qs, kb, NT, preferred_element_type=F32) * (HEAD_DIM**-0.5)
    sc = sc + bias_ref[...].reshape(grp * BLK, 3 * BLK)
    kpos = start + lax.broadcasted_iota(I32, (1, 3 * BLK), 1) - BLK
    sc = jnp.where((kpos >= 0) & (kpos < s_len), sc, NEG)
    sink = jnp.concatenate([jnp.full((BLK, 1), sink_ref[kv * grp + g], F32) for g in range(grp)], axis=0)
    m = jnp.maximum(jnp.max(sc, axis=-1, keepdims=True), sink)
    p = jnp.exp(sc - m)
    esink = jnp.exp(sink - m)
    den = jnp.sum(p, axis=-1, keepdims=True) + esink
    return start, qs, kb, vb, p / den, esink / den


def _attn_specs(s, grp, q_blk0):
    qw = grp * HEAD_DIM
    return [
        pl.BlockSpec(memory_space=pltpu.SMEM),
        pl.BlockSpec((BLK, qw), lambda kv, n: (n, q_blk0 + kv)),
        pl.BlockSpec((s + 2 * BLK, HEAD_DIM), lambda kv, n: (0, kv)),
        pl.BlockSpec((s + 2 * BLK, HEAD_DIM), lambda kv, n: (0, kv)),
        pl.BlockSpec((grp, BLK, 3 * BLK), lambda kv, n: (kv, 0, 0)),
    ]


def _attn_fwd(sink, z, k_pad, v_pad, bias_tab, grp, q_blk0):
    s = z.shape[0]
    qw = grp * HEAD_DIM

    def body(sink_ref, q_ref, k_ref, v_ref, bias_ref, o_ref):
        _, _, _, vb, pn, _ = _attn_softmax(sink_ref, q_ref, k_ref, v_ref, bias_ref, s, grp)
        o = jnp.dot(pn.astype(BF16), vb, preferred_element_type=F32)
        for g in range(grp):
            o_ref[:, g * HEAD_DIM : (g + 1) * HEAD_DIM] = o[g * BLK : (g + 1) * BLK].astype(BF16)

    return pl.pallas_call(
        body,
        name="attn_fwd",
        grid=(N_KV_HEADS, s // BLK),
        in_specs=_attn_specs(s, grp, q_blk0),
        out_specs=pl.BlockSpec((BLK, qw), lambda kv, n: (n, kv)),
        out_shape=jax.ShapeDtypeStruct((s, N_KV_HEADS * qw), BF16),
    )(sink, z, k_pad, v_pad, bias_tab)


def _attn_bwd(sink, z, k_pad, v_pad, bias_tab, dout, dz, grp, q_blk0):
    s = z.shape[0]
    qw = grp * HEAD_DIM
    nb = s // BLK
    heads = N_KV_HEADS * grp

    def body(sink_ref, q_ref, k_ref, v_ref, bias_ref, do_ref, dz_in, dq_ref, dk_ref, dv_ref, dbias_ref, dsink_ref, dk_acc, dv_acc):
        del dz_in
        kv = pl.program_id(0)
        n = pl.program_id(1)
        start, qs, kb, vb, pn, psink = _attn_softmax(sink_ref, q_ref, k_ref, v_ref, bias_ref, s, grp)
        dov = do_ref[...]
        dos = jnp.concatenate([dov[:, g * HEAD_DIM : (g + 1) * HEAD_DIM] for g in range(grp)], axis=0)
        dp = lax.dot_general(dos, vb, NT, preferred_element_type=F32)
        dvb = lax.dot_general(pn.astype(BF16), dos, TN, preferred_element_type=F32)
        delta = jnp.sum(pn * dp, axis=-1, keepdims=True)
        ds = pn * (dp - delta)
        dsb = (ds * (HEAD_DIM**-0.5)).astype(BF16)
        dq = jnp.dot(dsb, kb, preferred_element_type=F32)
        dkb = lax.dot_general(dsb, qs, TN, preferred_element_type=F32)
        for g in range(grp):
            dq_ref[:, g * HEAD_DIM : (g + 1) * HEAD_DIM] = dq[g * BLK : (g + 1) * BLK].astype(BF16)

        @pl.when(n == 0)
        def _():
            dk_acc[...] = jnp.zeros_like(dk_acc)
            dv_acc[...] = jnp.zeros_like(dv_acc)
            dbias_ref[...] = jnp.zeros_like(dbias_ref)

        @pl.when((n == 0) & (kv == 0))
        def _():
            dsink_ref[...] = jnp.zeros_like(dsink_ref)

        dk_acc[pl.ds(start, 3 * BLK), :] += dkb
        dv_acc[pl.ds(start, 3 * BLK), :] += dvb
        dbias_ref[...] += ds.reshape(grp, BLK, 3 * BLK)
        row = lax.broadcasted_iota(I32, (heads, LANES), 0)
        sd = psink * delta
        upd = jnp.zeros((heads, LANES), F32)
        for g in range(grp):
            upd = jnp.where(row == kv * grp + g, -jnp.sum(sd[g * BLK : (g + 1) * BLK]), upd)
        dsink_ref[...] += upd

        @pl.when(n == nb - 1)
        def _():
            dk_ref[...] = dk_acc[...]
            dv_ref[...] = dv_acc[...]

    pad_spec = pl.BlockSpec((s + 2 * BLK, HEAD_DIM), lambda kv, n: (0, kv))
    kvw = N_KV_HEADS * HEAD_DIM
    return pl.pallas_call(
        body,
        name="attn_bwd",
        grid=(N_KV_HEADS, nb),
        in_specs=_attn_specs(s, grp, q_blk0) + [pl.BlockSpec((BLK, qw), lambda kv, n: (n, kv)), ANY],
        out_specs=[
            pl.BlockSpec((BLK, qw), lambda kv, n: (n, q_blk0 + kv)),
            pad_spec,
            pad_spec,
            pl.BlockSpec((grp, BLK, 3 * BLK), lambda kv, n: (kv, 0, 0)),
            pl.BlockSpec((heads, LANES), lambda kv, n: (0, 0)),
        ],
        out_shape=[
            jax.ShapeDtypeStruct(dz.shape, BF16),
            jax.ShapeDtypeStruct((s + 2 * BLK, kvw), F32),
            jax.ShapeDtypeStruct((s + 2 * BLK, kvw), F32),
            jax.ShapeDtypeStruct((heads, BLK, 3 * BLK), F32),
            jax.ShapeDtypeStruct((heads, LANES), F32),
        ],
        scratch_shapes=[pltpu.VMEM((s + 2 * BLK, HEAD_DIM), F32), pltpu.VMEM((s + 2 * BLK, HEAD_DIM), F32)],
        input_output_aliases={6: 0},
    )(sink, z, k_pad, v_pad, bias_tab, dout, dz)


def _dkv_to_dz(dk_pad, dv_pad, dz, blk_idx):
    s = dz.shape[0]
    kvw = dk_pad.shape[1]

    def body(dk_ref, dv_ref, dz_in, out_ref):
        del dz_in
        out_ref[:, :kvw] = dk_ref[...].astype(BF16)
        out_ref[:, kvw:] = dv_ref[...].astype(BF16)

    src = pl.BlockSpec((BLK, kvw), lambda i: (i + 1, 0))
    return pl.pallas_call(
        body,
        name="dkv_to_dz",
        grid=(s // BLK,),
        in_specs=[src, src, ANY],
        out_specs=pl.BlockSpec((BLK, 2 * kvw), lambda i: (i, blk_idx)),
        out_shape=jax.ShapeDtypeStruct(dz.shape, BF16),
        input_output_aliases={2: 0},
    )(dk_pad, dv_pad, dz)


def _relbias_bwd(dbias_tab, bucket):
    heads = dbias_tab.shape[0]

    def body(dt_ref, bk_ref, out_ref):
        lane = lax.broadcasted_iota(I32, (1, LANES), 1)
        bk = bk_ref[...]
        rows = []
        for h in range(heads):
            dt = dt_ref[h]
            acc = jnp.zeros((1, LANES), F32)
            for b in range(REL_BUCKETS):
                acc = jnp.where(lane == b, jnp.sum(jnp.where(bk == b, dt, 0.0)), acc)
            rows.append(acc)
        out_ref[...] = jnp.concatenate(rows, axis=0)

    return pl.pallas_call(body, name="relbias_bwd", out_shape=jax.ShapeDtypeStruct((heads, LANES), F32))(dbias_tab, bucket)


def _t5_bucket(rel):
    nb = REL_BUCKETS // 2
    ret = jnp.where(rel > 0, nb, 0)
    n = jnp.abs(rel)
    max_exact = nb // 2
    nf = jnp.maximum(n, 1).astype(F32)
    large = max_exact + (jnp.log(nf / max_exact) / math.log(REL_MAX_DIST / max_exact) * (nb - max_exact)).astype(I32)
    large = jnp.minimum(large, nb - 1)
    return ret + jnp.where(n < max_exact, n, large)


def _band_tables(rel_bias):
    qi = jnp.arange(BLK)[:, None]
    kj = jnp.arange(3 * BLK)[None, :]
    rel = kj - BLK - qi
    bucket = _t5_bucket(rel).astype(I32)
    heads = rel_bias.shape[1]
    masked = jnp.where(jnp.abs(rel) <= BLK, bucket, -1)

    def body(rb_ref, bk_ref, out_ref):
        bk = bk_ref[...]
        for h in range(heads):
            tab = jnp.full(bk.shape, NEG, F32)
            for b in range(REL_BUCKETS):
                tab = jnp.where(bk == b, rb_ref[b, h], tab)
            out_ref[h] = tab

    bias_tab = pl.pallas_call(
        body,
        name="bias_table",
        in_specs=[pl.BlockSpec(memory_space=pltpu.SMEM), pl.BlockSpec(memory_space=pltpu.VMEM)],
        out_specs=pl.BlockSpec(memory_space=pltpu.VMEM),
        out_shape=jax.ShapeDtypeStruct((heads, BLK, 3 * BLK), F32),
    )(rel_bias.astype(F32), masked)
    return bias_tab, bucket


def _ew_tiles(shape):
    r, c = shape
    tn = c if c <= 2048 else _tile(c, (2048, 1920, 1536, 1408, 1024, 512))
    tm = _tile(r, (128, 64, 32, 16, 8))
    return tm, tn


def _cast_into_full(name, qidx, w, kind):
    r, c = w.shape
    tm, tn = _ew_tiles(w.shape)
    nbi, nbj = r // tm, c // tn
    if kind == "col":
        full, out_spec = (r, c * N_CHIPS), pl.BlockSpec((tm, tn), lambda i, j, q: (i, q[0] * nbj + j))
    else:
        full, out_spec = (r * N_CHIPS, c), pl.BlockSpec((tm, tn), lambda i, j, q: (q[0] * nbi + i, j))

    def body(q_ref, w_ref, o_ref):
        del q_ref
        o_ref[...] = w_ref[...].astype(BF16)

    return pl.pallas_call(
        body,
        name=name,
        grid_spec=pltpu.PrefetchScalarGridSpec(
            num_scalar_prefetch=1, grid=(nbi, nbj), in_specs=[pl.BlockSpec((tm, tn), lambda i, j, q: (i, j))], out_specs=out_spec
        ),
        out_shape=jax.ShapeDtypeStruct(full, BF16),
    )(qidx, w)


def _adamw(name, w, g, m, v, after=()):
    tm, tn = _ew_tiles(w.shape)
    if _nbytes(w.shape, F32) <= 1024 * 1024:
        tm, tn = w.shape
    spec = pl.BlockSpec((tm, tn), lambda i, j: (i, j))
    n_after = len(after)

    def body(w_ref, g_ref, m_ref, v_ref, *rest):
        d_ref, nm_ref, nv_ref = rest[n_after:]
        gv = g_ref[...]
        nm = ADAM_B1 * m_ref[...] + (1.0 - ADAM_B1) * gv
        nv = ADAM_B2 * v_ref[...] + (1.0 - ADAM_B2) * (gv * gv)
        m_hat = nm / (1.0 - ADAM_B1**ADAM_STEP)
        v_hat = nv / (1.0 - ADAM_B2**ADAM_STEP)
        d_ref[...] = -ADAM_LR * (m_hat / (jnp.sqrt(v_hat) + ADAM_EPS) + ADAM_WD * w_ref[...])
        nm_ref[...] = nm
        nv_ref[...] = nv

    out = jax.ShapeDtypeStruct(w.shape, F32)
    return pl.pallas_call(
        body, name=name, grid=(w.shape[0] // tm, w.shape[1] // tn), in_specs=[spec] * 4 + [ANY] * n_after,
        out_specs=[spec] * 3, out_shape=[out, out, out],
    )(w, g, m, v, *after)


def _pair_add(name, cidx, g_full, r_sib, kind):
    hr, hc = r_sib.shape
    tm, tn = _ew_tiles((hr, hc))
    nbi, nbj = hr // tm, hc // tn
    if kind == "col":
        g_spec = pl.BlockSpec((tm, tn), lambda i, j, c: (c[0] * nbi + i, j))
    else:
        g_spec = pl.BlockSpec((tm, tn), lambda i, j, c: (i, c[0] * nbj + j))
    spec = pl.BlockSpec((tm, tn), lambda i, j, c: (i, j))

    def body(c_ref, g_ref, r_ref, o_ref):
        del c_ref
        o_ref[...] = (g_ref[...].astype(F32) + r_ref[...].astype(F32)).astype(BF16)

    return pl.pallas_call(
        body,
        name=name,
        grid_spec=pltpu.PrefetchScalarGridSpec(num_scalar_prefetch=1, grid=(nbi, nbj), in_specs=[g_spec, spec], out_specs=spec),
        out_shape=jax.ShapeDtypeStruct((hr, hc), BF16),
    )(cidx, g_full, r_sib)


def _chip_sum(name, qidx, c_half, r_ici, kind):
    _, pr, pc = r_ici.shape
    tm, tn = _ew_tiles((pr, pc))
    nbi, nbj = pr // tm, pc // tn
    if kind == "col":
        own_spec = pl.BlockSpec((tm, tn), lambda i, j, q: (i, q[0] * nbj + j))
        full, out_spec = (2 * pr, pc), pl.BlockSpec((tm, tn), lambda i, j, q: (q[1] * nbi + i, j))
    else:
        own_spec = pl.BlockSpec((tm, tn), lambda i, j, q: (q[0] * nbi + i, j))
        full, out_spec = (pr, 2 * pc), pl.BlockSpec((tm, tn), lambda i, j, q: (i, q[1] * nbj + j))

    def body(q_ref, own_ref, r_ref, o_ref):
        q = q_ref[0]
        own = own_ref[...].astype(F32)
        recv = [r_ref[r].astype(F32) for r in range(3)]
        total = None
        for chip in range(N_CHIPS):
            d = chip ^ q
            term = jnp.where(d == 0, own, jnp.where(d == 2, recv[0], jnp.where(d == 1, recv[1], recv[2])))
            total = term if total is None else total + term
        o_ref[...] = total

    return pl.pallas_call(
        body,
        name=name,
        grid_spec=pltpu.PrefetchScalarGridSpec(
            num_scalar_prefetch=1,
            grid=(nbi, nbj),
            in_specs=[own_spec, pl.BlockSpec((3, tm, tn), lambda i, j, q: (0, i, j))],
            out_specs=out_spec,
        ),
        out_shape=jax.ShapeDtypeStruct(full, F32),
    )(qidx, c_half, r_ici)


_REL_MASK = (2, 1, 3)


def _place():
    x, y, c = lax.axis_index("x"), lax.axis_index("y"), lax.axis_index("c")
    chips = [(1 - x, y), (x, 1 - y), (1 - x, 1 - y)]
    return x, y, c, 2 * x + y, chips


def _shard_view(ref, kind, chip):
    if kind == "col":
        w = ref.shape[1] // N_CHIPS
        return ref.at[:, pl.ds(pl.multiple_of(chip * w, LANES), w)]
    h = ref.shape[0] // N_CHIPS
    return ref.at[pl.ds(pl.multiple_of(chip * h, 16), h), :]


def _row_half(ref, half):
    h = ref.shape[0] // 2
    return ref.at[pl.ds(pl.multiple_of(half * h, 16), h), :]


def _pair_half(ref, kind, half):
    if kind == "col":
        return _row_half(ref, half)
    w = ref.shape[1] // 2
    return ref.at[:, pl.ds(pl.multiple_of(half * w, LANES), w)]


def _remote(src, dst, send_sem, recv_sem, dev):
    return pltpu.make_async_remote_copy(src_ref=src, dst_ref=dst, send_sem=send_sem, recv_sem=recv_sem, device_id=dev, device_id_type=MESH)


def _hbm(a):
    return pltpu.with_memory_space_constraint(a, pltpu.HBM)


def _gather_start(fulls, kinds):
    n_w = len(fulls)

    def body(*refs):
        g = refs[:n_w]
        send_sem, recv_sem = refs[n_w], refs[n_w + 1]
        token = refs[-1]
        _, _, c, q, chips = _place()
        for w in range(n_w):
            mine = _row_half(_shard_view(g[w], kinds[w], q), c)
            for r, chip in enumerate(chips):
                _remote(mine, mine, send_sem.at[3 * w + r], recv_sem.at[3 * w + r], (*chip, c)).start()
        token[...] = jnp.zeros_like(token)

    res = pl.pallas_call(
        body,
        name="gather_start",
        out_shape=(
            pltpu.SemaphoreType.DMA((3 * n_w,)),
            pltpu.SemaphoreType.DMA((3 * n_w,)),
            *[pltpu.HBM(f.shape, f.dtype) for f in fulls],
            jax.ShapeDtypeStruct((8, LANES), F32),
        ),
        in_specs=[HBM_SPEC] * n_w,
        out_specs=(SEM_SPEC, SEM_SPEC, *[HBM_SPEC] * n_w, pl.BlockSpec(memory_space=pltpu.VMEM)),
        input_output_aliases={w: w + 2 for w in range(n_w)},
        compiler_params=pltpu.CompilerParams(has_side_effects=EFFECT),
    )(*[_hbm(f) for f in fulls])
    return res[0], res[1], list(res[2 : 2 + n_w]), res[-1]


def _gather_wait(name, fulls, kinds, w_ids, send_sem, recv_sem, after):
    n = len(fulls)

    def body(*refs):
        g = refs[:n]
        s_sem, r_sem = refs[n], refs[n + 1]
        x, y, c, q, _ = _place()
        for i, w in enumerate(w_ids):
            mine = _row_half(_shard_view(g[i], kinds[i], q), c)
            for r in range(3):
                landed = _row_half(_shard_view(g[i], kinds[i], q ^ _REL_MASK[r]), c)
                cp = _remote(mine, landed, s_sem.at[3 * w + r], r_sem.at[3 * w + r], (x, y, 1 - c))
                cp.wait_send()
                cp.wait_recv()

    res = pl.pallas_call(
        body,
        name=name,
        out_shape=[pltpu.HBM(f.shape, f.dtype) for f in fulls],
        in_specs=[HBM_SPEC] * n + [SEM_SPEC, SEM_SPEC, ANY],
        out_specs=[HBM_SPEC] * n,
        input_output_aliases={i: i for i in range(n)},
        compiler_params=pltpu.CompilerParams(has_side_effects=EFFECT),
    )(*fulls, send_sem, recv_sem, after)
    return list(res)


def _gather_forward(name, fulls, kinds):
    n = len(fulls)

    def body(*refs):
        g = refs[n : 2 * n]
        send, recv = refs[2 * n :]
        x, y, c, q, _ = _place()
        sib = (x, y, 1 - c)
        cps = []
        for i in range(n):
            for r in range(3):
                landed = _row_half(_shard_view(g[i], kinds[i], q ^ _REL_MASK[r]), c)
                cps.append(_remote(landed, landed, send.at[i, r], recv.at[i, r], sib))
        for cp in cps:
            cp.start()
        for i in range(n):
            for r in range(3):
                other = _row_half(_shard_view(g[i], kinds[i], q ^ _REL_MASK[r]), 1 - c)
                _remote(other, other, send.at[i, r], recv.at[i, r], sib).wait_recv()
        for cp in cps:
            cp.wait_send()

    res = pl.pallas_call(
        body,
        name=name,
        in_specs=[ANY] * n,
        out_specs=[ANY] * n,
        out_shape=[jax.ShapeDtypeStruct(f.shape, f.dtype) for f in fulls],
        scratch_shapes=[pltpu.SemaphoreType.DMA((n, 3)), pltpu.SemaphoreType.DMA((n, 3))],
        input_output_aliases={i: i for i in range(n)},
    )(*fulls)
    return list(res)


def _pair_exchange(name, grads, kinds):
    n_w = len(grads)
    outs = []
    for g, kind in zip(grads, kinds):
        r, c = g.shape
        outs.append(jax.ShapeDtypeStruct((r // 2, c) if kind == "col" else (r, c // 2), BF16))

    def body(*refs):
        g_refs, o_refs = refs[:n_w], refs[n_w : 2 * n_w]
        send, recv = refs[2 * n_w :]
        x, y, c, _, _ = _place()
        sib = (x, y, 1 - c)
        cps = [_remote(_pair_half(g_refs[w], kinds[w], 1 - c), o_refs[w], send.at[w], recv.at[w], sib) for w in range(n_w)]
        for cp in cps:
            cp.start()
        for cp in cps:
            cp.wait()

    return list(pl.pallas_call(
        body,
        name=name,
        in_specs=[ANY] * n_w,
        out_specs=[ANY] * n_w,
        out_shape=outs,
        scratch_shapes=[pltpu.SemaphoreType.DMA((n_w,)), pltpu.SemaphoreType.DMA((n_w,))],
    )(*grads))


def _piece_shape(half_shape, kind):
    r, c = half_shape
    return (3, r, c // N_CHIPS) if kind == "col" else (3, r // N_CHIPS, c)


def _chip_send_start(name, halves, kinds):
    n = len(halves)
    lands = [lax.empty(_piece_shape(h.shape, k), BF16) for h, k in zip(halves, kinds)]

    def body(*refs):
        h, land = refs[:n], refs[n : 2 * n]
        send_sem, recv_sem = refs[2 * n], refs[2 * n + 1]
        _, _, c, q, chips = _place()
        for i in range(n):
            for r, chip in enumerate(chips):
                piece = _shard_view(h[i], kinds[i], q ^ _REL_MASK[r])
                _remote(piece, land[i].at[r], send_sem.at[3 * i + r], recv_sem.at[3 * i + r], (*chip, c)).start()

    res = pl.pallas_call(
        body,
        name=name,
        out_shape=(
            pltpu.SemaphoreType.DMA((3 * n,)),
            pltpu.SemaphoreType.DMA((3 * n,)),
            *[pltpu.HBM(a.shape, a.dtype) for a in halves],
            *[pltpu.HBM(a.shape, a.dtype) for a in lands],
        ),
        in_specs=[HBM_SPEC] * (2 * n),
        out_specs=(SEM_SPEC, SEM_SPEC, *[HBM_SPEC] * (2 * n)),
        input_output_aliases={i: i + 2 for i in range(2 * n)},
        compiler_params=pltpu.CompilerParams(has_side_effects=EFFECT),
    )(*[_hbm(a) for a in halves], *[_hbm(a) for a in lands])
    return res[0], res[1], list(res[2 : 2 + n]), list(res[2 + n :])


def _chip_send_wait(name, halves, lands, kinds, send_sem, recv_sem, after):
    n = len(halves)

    def body(*refs):
        h, land = refs[:n], refs[n : 2 * n]
        s_sem, r_sem = refs[2 * n], refs[2 * n + 1]
        x, y, c, q, _ = _place()
        for i in range(n):
            for r in range(3):
                piece = _shard_view(h[i], kinds[i], q ^ _REL_MASK[r])
                cp = _remote(piece, land[i].at[r], s_sem.at[3 * i + r], r_sem.at[3 * i + r], (x, y, 1 - c))
                cp.wait_send()
                cp.wait_recv()

    res = pl.pallas_call(
        body,
        name=name,
        out_shape=[pltpu.HBM(a.shape, a.dtype) for a in halves] + [pltpu.HBM(a.shape, a.dtype) for a in lands],
        in_specs=[HBM_SPEC] * (2 * n) + [SEM_SPEC, SEM_SPEC, ANY],
        out_specs=[HBM_SPEC] * (2 * n),
        input_output_aliases={i: i for i in range(2 * n)},
        compiler_params=pltpu.CompilerParams(has_side_effects=EFFECT),
    )(*halves, *lands, send_sem, recv_sem, after)
    return list(res[:n]), list(res[n:])


def _pair_share(name, pieces, kinds):
    n_w = len(pieces)

    def body(*refs):
        o_refs = refs[n_w : 2 * n_w]
        send, recv = refs[2 * n_w :]
        x, y, c, _, _ = _place()
        sib = (x, y, 1 - c)
        cps = []
        for w in range(n_w):
            mine = _pair_half(o_refs[w], kinds[w], c)
            cps.append(_remote(mine, mine, send.at[w], recv.at[w], sib))
        for cp in cps:
            cp.start()
        for w in range(n_w):
            other = _pair_half(o_refs[w], kinds[w], 1 - c)
            _remote(other, other, send.at[w], recv.at[w], sib).wait_recv()
        for cp in cps:
            cp.wait_send()

    return list(pl.pallas_call(
        body,
        name=name,
        in_specs=[ANY] * n_w,
        out_specs=[ANY] * n_w,
        out_shape=[jax.ShapeDtypeStruct(p.shape, F32) for p in pieces],
        scratch_shapes=[pltpu.SemaphoreType.DMA((n_w,)), pltpu.SemaphoreType.DMA((n_w,))],
        input_output_aliases={w: w for w in range(n_w)},
    )(*pieces))


def _small_all_reduce(p):
    rows = p.shape[0]
    n_dev = 2 * N_CHIPS

    def body(p_ref, o_ref, buf, loc_sem, send, recv):
        x, y, c, q, _ = _place()
        me = 2 * q + c
        own = pltpu.make_async_copy(p_ref, buf.at[me], loc_sem)
        own.start()
        cps = []
        for d in range(1, n_dev):
            dev = (x ^ ((d >> 2) & 1), y ^ ((d >> 1) & 1), c ^ (d & 1))
            cps.append(_remote(p_ref, buf.at[me], send.at[d - 1], recv.at[d - 1], dev))
        for cp in cps:
            cp.start()
        for d in range(1, n_dev):
            slot = buf.at[me ^ d]
            _remote(slot, slot, send.at[d - 1], recv.at[d - 1], (x, y, c)).wait_recv()
        own.wait()
        total = buf[0]
        for d in range(1, n_dev):
            total = total + buf[d]
        o_ref[...] = total
        for cp in cps:
            cp.wait_send()

    return pl.pallas_call(
        body,
        name="small_all_reduce",
        in_specs=[ANY],
        out_specs=pl.BlockSpec(memory_space=pltpu.VMEM),
        out_shape=jax.ShapeDtypeStruct(p.shape, F32),
        scratch_shapes=[
            pltpu.VMEM((n_dev, rows, LANES), F32),
            pltpu.SemaphoreType.DMA,
            pltpu.SemaphoreType.DMA((n_dev - 1,)),
            pltpu.SemaphoreType.DMA((n_dev - 1,)),
        ],
    )(p)


def _pack(parts):
    rows = []
    for a in parts:
        flat = a.reshape(-1).astype(F32)
        n = flat.shape[0]
        padded = -(-n // (8 * LANES)) * (8 * LANES)
        rows.append(jnp.pad(flat, (0, padded - n)).reshape(-1, LANES))
    return jnp.concatenate(rows, axis=0)


def _unpack(packed, shapes):
    out, row = [], 0
    for shp in shapes:
        n = int(np.prod(shp))
        nrows = -(-n // (8 * LANES)) * 8
        out.append(packed[row : row + nrows].reshape(-1)[:n].reshape(shp))
        row += nrows
    return out


def kernel(x, w_in, norm_mix, sgu_v_gain, sgu_w_s, sgu_b_s, w_a_out, attn_sink, rel_bias, w_b_out, w_o, norm_ffn, w_gate, w_up, w_down, norm_final, loss_target, m_w_in, m_norm_mix, m_sgu_v_gain, m_sgu_w_s, m_sgu_b_s, m_w_a_out, m_attn_sink, m_rel_bias, m_w_b_out, m_w_o, m_norm_ffn, m_w_gate, m_w_up, m_w_down, m_norm_final, v_w_in, v_norm_mix, v_sgu_v_gain, v_sgu_w_s, v_sgu_b_s, v_w_a_out, v_attn_sink, v_rel_bias, v_w_b_out, v_w_o, v_norm_ffn, v_w_gate, v_w_up, v_w_down, v_norm_final):
    s, d = x.shape[1], x.shape[2]
    w_sgu = sgu_v_gain.shape[1]
    groups = sgu_w_s.shape[1]
    heads = attn_sink.shape[1]
    grp = heads // N_KV_HEADS
    w_att = heads * HEAD_DIM
    w_kv = N_KV_HEADS * HEAD_DIM
    d_ff = w_gate.shape[2] * N_CHIPS
    n_in = w_in.shape[2] * N_CHIPS
    off_q = 2 * w_sgu
    off_k = off_q + w_att
    off_g = off_k + 2 * w_kv
    assert n_in == off_g + 2 * d and groups * BLK == w_sgu and s % BLK == 0

    x2d = x.reshape(s, d)
    tgt = loss_target.reshape(s, d)
    c_idx = lax.axis_index("c").astype(I32).reshape(1)
    q_idx = (2 * lax.axis_index("x") + lax.axis_index("y")).astype(I32).reshape(1)
    qc_idx = jnp.concatenate([q_idx, c_idx])

    W_IN, W_A, W_B, W_O, W_GATE, W_UP, W_DOWN = range(7)
    names = ["w_in", "w_a", "w_b", "w_o", "w_gate", "w_up", "w_down"]
    kinds = ["col", "col", "col", "row", "col", "col", "row"]
    big_w = [w_in[0], w_a_out[0], w_b_out[0], w_o[0], w_gate[0], w_up[0], w_down[0]]
    big_m = [m_w_in[0], m_w_a_out[0], m_w_b_out[0], m_w_o[0], m_w_gate[0], m_w_up[0], m_w_down[0]]
    big_v = [v_w_in[0], v_w_a_out[0], v_w_b_out[0], v_w_o[0], v_w_gate[0], v_w_up[0], v_w_down[0]]
    fulls = [_cast_into_full("cast_" + n, q_idx, w, k) for n, w, k in zip(names, big_w, kinds)]
    ag_send, ag_recv, fulls, token = _gather_start(fulls, kinds)

    def gathered(tag, ids, after):
        got = _gather_wait("gather_wait_" + tag, [fulls[i] for i in ids], [kinds[i] for i in ids], ids, ag_send, ag_recv, after)
        return _gather_forward("gather_fwd_" + tag, got, [kinds[i] for i in ids])

    ws_b = sgu_w_s[0].astype(BF16)
    wst_b = jnp.swapaxes(sgu_w_s[0], 1, 2).astype(BF16)
    b_col = sgu_b_s[0].reshape(groups, BLK, 1)
    bias_tab, bucket = _band_tables(rel_bias)
    sink = attn_sink[0]

    tm = _tile(s, (1024, 512, 256, 128))

    h1 = _rms_fwd("rms_mix", x2d, norm_mix, after=(token,))
    (g_in,) = gathered("in", [W_IN], h1)

    tn = _tile(n_in, (768, 640, 512))
    z = _mm(
        "mm_z", (s // tm, n_in // tn, 1), [h1, g_in],
        [pl.BlockSpec((tm, d), lambda i, j, k: (i, 0)), pl.BlockSpec((d, tn), lambda i, j, k: (0, j))],
        [jax.ShapeDtypeStruct((s, n_in), F32)], [pl.BlockSpec((tm, tn), lambda i, j, k: (i, j))],
        [(0, 1, NN, 0)], 1, (tm, tn), 1, lambda ins, vals, outs: _store_cast(outs[0], vals[0]),
        _mm_vmem([((tm, d), BF16, 2), ((d, tn), BF16, 2), ((tm, tn), F32, 3)]),
    )[0]
    g_a, g_b, g_o = gathered("mix", [W_A, W_B, W_O], z)

    a_act = _sgu_fwd(z, sgu_v_gain, ws_b, b_col, w_sgu)

    kv_b = z[:, off_k:off_g].astype(BF16)
    k_pad = jnp.pad(kv_b[:, :w_kv], ((BLK, BLK), (0, 0)))
    v_pad = jnp.pad(kv_b[:, w_kv:], ((BLK, BLK), (0, 0)))
    q_blk0 = off_q // (grp * HEAD_DIM)
    att = _attn_fwd(sink, z, k_pad, v_pad, bias_tab, grp, q_blk0)

    tg = _tile(d, (512,))
    ga0, gb0 = off_g // tg, (off_g + d) // tg

    def ep_gate(ins, vals, outs):
        sa, sb = _sigmoid(ins[4][...]), _sigmoid(ins[5][...])
        outs[0][...] = (sa * vals[0] + sb * vals[1]).astype(BF16)
        outs[1][...] = vals[0]
        outs[2][...] = vals[1]

    t_out = pl.BlockSpec((tm, tg), lambda i, j, k: (i, j))
    m_act, y_a, y_b = _mm(
        "mm_branches", (s // tm, d // tg, 1), [a_act, g_a, att, g_b, z, z],
        [pl.BlockSpec((tm, w_sgu), lambda i, j, k: (i, 0)), pl.BlockSpec((w_sgu, tg), lambda i, j, k: (0, j)),
         pl.BlockSpec((tm, w_att), lambda i, j, k: (i, 0)), pl.BlockSpec((w_att, tg), lambda i, j, k: (0, j)),
         pl.BlockSpec((tm, tg), lambda i, j, k: (i, ga0 + j)), pl.BlockSpec((tm, tg), lambda i, j, k: (i, gb0 + j))],
        [jax.ShapeDtypeStruct((s, d), BF16), jax.ShapeDtypeStruct((s, d), F32), jax.ShapeDtypeStruct((s, d), F32)],
        [t_out, t_out, t_out], [(0, 1, NN, 0), (2, 3, NN, 1)], 2, (tm, tg), 1, ep_gate,
        _mm_vmem([((tm, w_sgu), BF16, 4), ((w_sgu, tg), BF16, 4), ((tm, tg), F32, 12)]),
    )

    tn = _tile(d, (1024, 512))

    def ep_residual(ins, vals, outs):
        outs[0][...] = ins[2][...] + vals[0]

    x2 = _mm(
        "mm_wo", (s // tm, d // tn, 1), [m_act, g_o, x2d],
        [pl.BlockSpec((tm, d), lambda i, j, k: (i, 0)), pl.BlockSpec((d, tn), lambda i, j, k: (0, j)),
         pl.BlockSpec((tm, tn), lambda i, j, k: (i, j))],
        [jax.ShapeDtypeStruct((s, d), F32)], [pl.BlockSpec((tm, tn), lambda i, j, k: (i, j))],
        [(0, 1, NN, 0)], 1, (tm, tn), 1, ep_residual,
        _mm_vmem([((tm, d), BF16, 2), ((d, tn), BF16, 2), ((tm, tn), F32, 5)]),
    )[0]

    h2 = _rms_fwd("rms_ffn", x2, norm_ffn)
    g_gate, g_up = gathered("ffn_in", [W_GATE, W_UP], h2)

    tf = _tile(d_ff, (512,))

    def ep_swiglu(ins, vals, outs):
        gt, up = vals
        outs[0][...] = gt
        outs[1][...] = up
        outs[2][...] = ((gt * _sigmoid(gt)) * up).astype(BF16)

    f_out = pl.BlockSpec((tm, tf), lambda i, j, k: (i, j))
    gt, up, f_act = _mm(
        "mm_gate_up", (s // tm, d_ff // tf, 1), [h2, g_gate, g_up],
        [pl.BlockSpec((tm, d), lambda i, j, k: (i, 0)), pl.BlockSpec((d, tf), lambda i, j, k: (0, j)),
         pl.BlockSpec((d, tf), lambda i, j, k: (0, j))],
        [jax.ShapeDtypeStruct((s, d_ff), F32), jax.ShapeDtypeStruct((s, d_ff), F32), jax.ShapeDtypeStruct((s, d_ff), BF16)],
        [f_out, f_out, f_out], [(0, 1, NN, 0), (0, 2, NN, 1)], 2, (tm, tf), 1, ep_swiglu,
        _mm_vmem([((tm, d), BF16, 2), ((d, tf), BF16, 4), ((tm, tf), F32, 8)]),
    )
    (g_down,) = gathered("ffn_out", [W_DOWN], f_act)

    tkf = _tile(d_ff, (1408, 1024, 512))
    nkf = d_ff // tkf
    x3 = _mm(
        "mm_down", (s // tm, d // tn, nkf), [f_act, g_down, x2],
        [pl.BlockSpec((tm, tkf), lambda i, j, k: (i, k)), pl.BlockSpec((tkf, tn), lambda i, j, k: (k, j)),
         pl.BlockSpec((tm, tn), lambda i, j, k: (i, j))],
        [jax.ShapeDtypeStruct((s, d), F32)], [pl.BlockSpec((tm, tn), lambda i, j, k: (i, j))],
        [(0, 1, NN, 0)], 1, (tm, tn), nkf, ep_residual,
        _mm_vmem([((tm, tkf), BF16, 2), ((tkf, tn), BF16, 2), ((tm, tn), F32, 6)]),
    )[0]

    dx3, dx3b, dg_final, loss_part = _head(x3, norm_final.reshape(1, d), tgt)

    def reduce_start(tag, ids, grads):
        ks = [kinds[i] for i in ids]
        from_sib = _pair_exchange("pair_exchange_" + tag, grads, ks)
        halves = [_pair_add("pair_add_" + names[i], c_idx, g, r, k) for i, g, r, k in zip(ids, grads, from_sib, ks)]
        return (ids, ks) + _chip_send_start("chip_send_" + tag, halves, ks)

    def reduce_finish(tag, started, after):
        ids, ks, send_sem, recv_sem, halves, lands = started
        halves, lands = _chip_send_wait("chip_wait_" + tag, halves, lands, ks, send_sem, recv_sem, after)
        pieces = [_chip_sum("chip_sum_" + names[i], qc_idx, h, r, k) for i, h, r, k in zip(ids, halves, lands, ks)]
        return _pair_share("pair_share_" + tag, pieces, ks)

    def ep_swiglu_bwd(ins, vals, outs):
        df = vals[0]
        gtv, upv = ins[2][...], ins[3][...]
        sg = _sigmoid(gtv)
        outs[0][...] = (df * upv * (sg + gtv * sg * (1.0 - sg))).astype(BF16)
        outs[1][...] = (df * (gtv * sg)).astype(BF16)

    dgt, dup = _mm(
        "mm_dswiglu", (s // tm, d_ff // tf, 1), [dx3b, g_down, gt, up],
        [pl.BlockSpec((tm, d), lambda i, j, k: (i, 0)), pl.BlockSpec((tf, d), lambda i, j, k: (j, 0)), f_out, f_out],
        [jax.ShapeDtypeStruct((s, d_ff), BF16), jax.ShapeDtypeStruct((s, d_ff), BF16)], [f_out, f_out],
        [(0, 1, NT, 0)], 1, (tm, tf), 1, ep_swiglu_bwd,
        _mm_vmem([((tm, d), BF16, 2), ((tf, d), BF16, 2), ((tm, tf), F32, 8)]),
    )

    def ep_bf16(ins, vals, outs):
        for o, v in zip(outs, vals):
            o[...] = v.astype(BF16)

    def ep_f32(ins, vals, outs):
        for o, v in zip(outs, vals):
            o[...] = v

    twn = _tile(d, (1024, 512))
    gw_down = _mm(
        "mm_gw_down", (d_ff // tkf, d // twn, 1), [f_act, dx3b],
        [pl.BlockSpec((s, tkf), lambda i, j, k: (0, i)), pl.BlockSpec((s, twn), lambda i, j, k: (0, j))],
        [jax.ShapeDtypeStruct((d_ff, d), BF16)], [pl.BlockSpec((tkf, twn), lambda i, j, k: (i, j))],
        [(0, 1, TN, 0)], 1, (tkf, twn), 1, ep_bf16,
        _mm_vmem([((s, tkf), BF16, 3), ((s, twn), BF16, 2), ((tkf, twn), F32, 3)]),
    )[0]
    red_down = reduce_start("down", [W_DOWN], [gw_down])

    dh2 = _mm(
        "mm_dh2", (s // tm, d // tn, nkf), [dgt, g_gate, dup, g_up],
        [pl.BlockSpec((tm, tkf), lambda i, j, k: (i, k)), pl.BlockSpec((tn, tkf), lambda i, j, k: (j, k)),
         pl.BlockSpec((tm, tkf), lambda i, j, k: (i, k)), pl.BlockSpec((tn, tkf), lambda i, j, k: (j, k))],
        [jax.ShapeDtypeStruct((s, d), F32)], [pl.BlockSpec((tm, tn), lambda i, j, k: (i, j))],
        [(0, 1, NT, 0), (2, 3, NT, 0)], 1, (tm, tn), nkf, ep_f32,
        _mm_vmem([((tm, tkf), BF16, 4), ((tn, tkf), BF16, 4), ((tm, tn), F32, 5)]),
        after=tuple(red_down[4]),
    )[0]

    twr = _tile(d, (1024, 512))
    w_tile = pl.BlockSpec((twr, tf), lambda i, j, k: (i, j))
    gw_gate, gw_up = _mm(
        "mm_gw_gate_up", (d // twr, d_ff // tf, 1), [h2, dgt, dup],
        [pl.BlockSpec((s, twr), lambda i, j, k: (0, i)), pl.BlockSpec((s, tf), lambda i, j, k: (0, j)),
         pl.BlockSpec((s, tf), lambda i, j, k: (0, j))],
        [jax.ShapeDtypeStruct((d, d_ff), BF16), jax.ShapeDtypeStruct((d, d_ff), BF16)], [w_tile, w_tile],
        [(0, 1, TN, 0), (0, 2, TN, 1)], 2, (twr, tf), 1, ep_bf16,
        _mm_vmem([((s, twr), BF16, 3), ((s, tf), BF16, 4), ((twr, tf), F32, 6)]),
    )
    red_ffn = reduce_start("ffn_in", [W_GATE, W_UP], [gw_gate, gw_up])

    dx2, dx2b, dg_ffn = _rms_bwd("rms_ffn_bwd", x2, norm_ffn, dh2, dx3, after=tuple(red_ffn[4]))

    nj = d // tg

    def lo(j):
        return jnp.minimum(j, nj - 1)

    def gate_bwd_body(dx_ref, wo_ref, ga_ref, gb_ref, ya_ref, yb_ref, dya_ref, dyb_ref, dz_ref, keep):
        j = pl.program_id(1)

        @pl.when(j < nj)
        def _():
            dm = lax.dot_general(dx_ref[...], wo_ref[...], NT, preferred_element_type=F32)
            sa, sb = _sigmoid(ga_ref[...]), _sigmoid(gb_ref[...])
            dya_ref[...] = (dm * sa).astype(BF16)
            dyb_ref[...] = (dm * sb).astype(BF16)
            dz_ref[...] = (dm * ya_ref[...] * (sa * (1.0 - sa))).astype(BF16)
            keep[lo(j)] = (dm * yb_ref[...] * (sb * (1.0 - sb))).astype(BF16)

        @pl.when(j >= nj)
        def _():
            dz_ref[...] = keep[jnp.maximum(j - nj, 0)]

    t_lo = pl.BlockSpec((tm, tg), lambda i, j: (i, lo(j)))
    dya, dyb, dz = pl.pallas_call(
        gate_bwd_body,
        name="mm_dgate",
        grid=(s // tm, 2 * nj),
        in_specs=[
            pl.BlockSpec((tm, d), lambda i, j: (i, 0)),
            pl.BlockSpec((tg, d), lambda i, j: (lo(j), 0)),
            pl.BlockSpec((tm, tg), lambda i, j: (i, ga0 + lo(j))),
            pl.BlockSpec((tm, tg), lambda i, j: (i, gb0 + lo(j))),
            t_lo,
            t_lo,
        ],
        out_specs=[t_lo, t_lo, pl.BlockSpec((tm, tg), lambda i, j: (i, ga0 + j))],
        out_shape=[jax.ShapeDtypeStruct((s, d), BF16), jax.ShapeDtypeStruct((s, d), BF16), jax.ShapeDtypeStruct((s, n_in), BF16)],
        scratch_shapes=[pltpu.VMEM((nj, tm, tg), BF16)],
        compiler_params=_params(_mm_vmem([((tm, d), BF16, 2), ((tg, d), BF16, 2), ((tm, tg), F32, 14), ((nj, tm, tg), BF16, 1)])),
    )(dx2b, g_o, z, z, y_a, y_b)

    gw_o = _mm(
        "mm_gw_o", (d // twr, d // twn, 1), [m_act, dx2b],
        [pl.BlockSpec((s, twr), lambda i, j, k: (0, i)), pl.BlockSpec((s, twn), lambda i, j, k: (0, j))],
        [jax.ShapeDtypeStruct((d, d), BF16)], [pl.BlockSpec((twr, twn), lambda i, j, k: (i, j))],
        [(0, 1, TN, 0)], 1, (twr, twn), 1, ep_bf16,
        _mm_vmem([((s, twr), BF16, 3), ((s, twn), BF16, 2), ((twr, twn), F32, 3)]),
    )[0]
    red_o = reduce_start("w_o", [W_O], [gw_o])

    tb = _tile(w_sgu, (1024, 512))
    b_out = pl.BlockSpec((tm, tb), lambda i, j, k: (i, j))

    def ep_branch_bwd(ins, vals, outs):
        outs[0][...] = vals[0]
        outs[1][...] = vals[1].astype(BF16)

    da, datt = _mm(
        "mm_dbranches", (s // tm, w_sgu // tb, 1), [dya, g_a, dyb, g_b],
        [pl.BlockSpec((tm, d), lambda i, j, k: (i, 0)), pl.BlockSpec((tb, d), lambda i, j, k: (j, 0)),
         pl.BlockSpec((tm, d), lambda i, j, k: (i, 0)), pl.BlockSpec((tb, d), lambda i, j, k: (j, 0))],
        [jax.ShapeDtypeStruct((s, w_sgu), F32), jax.ShapeDtypeStruct((s, w_att), BF16)], [b_out, b_out],
        [(0, 1, NT, 0), (2, 3, NT, 1)], 2, (tm, tb), 1, ep_branch_bwd,
        _mm_vmem([((tm, d), BF16, 4), ((tb, d), BF16, 4), ((tm, tb), F32, 6)]),
        after=tuple(red_o[4]),
    )

    wb_tile = pl.BlockSpec((tb, twn), lambda i, j, k: (i, j))
    gw_a, gw_b = _mm(
        "mm_gw_branches", (w_sgu // tb, d // twn, 1), [a_act, dya, att, dyb],
        [pl.BlockSpec((s, tb), lambda i, j, k: (0, i)), pl.BlockSpec((s, twn), lambda i, j, k: (0, j)),
         pl.BlockSpec((s, tb), lambda i, j, k: (0, i)), pl.BlockSpec((s, twn), lambda i, j, k: (0, j))],
        [jax.ShapeDtypeStruct((w_sgu, d), BF16), jax.ShapeDtypeStruct((w_att, d), BF16)], [wb_tile, wb_tile],
        [(0, 1, TN, 0), (2, 3, TN, 1)], 2, (tb, twn), 1, ep_bf16,
        _mm_vmem([((s, tb), BF16, 5), ((s, twn), BF16, 4), ((tb, twn), F32, 6)]),
    )
    red_mix = reduce_start("mix", [W_A, W_B], [gw_a, gw_b])

    dz, dws, dbs, dgain = _sgu_bwd(z, da, sgu_v_gain, ws_b, wst_b, b_col, w_sgu, dz, after=tuple(red_mix[4]))
    dz, dk_pad, dv_pad, dbias_tab, dsink = _attn_bwd(sink, z, k_pad, v_pad, bias_tab, datt, dz, grp, q_blk0)
    dz = _dkv_to_dz(dk_pad, dv_pad, dz, off_k // (2 * w_kv))
    drel = _relbias_bwd(dbias_tab, bucket)

    tzn = _tile(n_in, (768, 640, 512))
    gw_in = _mm(
        "mm_gw_in", (d // twr, n_in // tzn, 1), [h1, dz],
        [pl.BlockSpec((s, twr), lambda i, j, k: (0, i)), pl.BlockSpec((s, tzn), lambda i, j, k: (0, j))],
        [jax.ShapeDtypeStruct((d, n_in), BF16)], [pl.BlockSpec((twr, tzn), lambda i, j, k: (i, j))],
        [(0, 1, TN, 0)], 1, (twr, tzn), 1, ep_bf16,
        _mm_vmem([((s, twr), BF16, 3), ((s, tzn), BF16, 2), ((twr, tzn), F32, 3)]),
    )[0]
    red_in = reduce_start("w_in", [W_IN], [gw_in])

    tkz = _tile(n_in, (1920, 1536, 1280, 1024))
    nkz = n_in // tkz
    dh1 = _mm(
        "mm_dh1", (s // tm, d // tn, nkz), [dz, g_in],
        [pl.BlockSpec((tm, tkz), lambda i, j, k: (i, k)), pl.BlockSpec((tn, tkz), lambda i, j, k: (j, k))],
        [jax.ShapeDtypeStruct((s, d), F32)], [pl.BlockSpec((tm, tn), lambda i, j, k: (i, j))],
        [(0, 1, NT, 0)], 1, (tm, tn), nkz, ep_f32,
        _mm_vmem([((tm, tkz), BF16, 2), ((tn, tkz), BF16, 2), ((tm, tn), F32, 5)]),
        after=tuple(red_in[4]),
    )[0]

    grad_x, _, dg_mix = _rms_bwd("rms_mix_bwd", x2d, norm_mix, dh1, dx2)

    small_w = [norm_mix, sgu_v_gain, sgu_w_s, sgu_b_s, attn_sink, rel_bias, norm_ffn, norm_final]
    small_m = [m_norm_mix, m_sgu_v_gain, m_sgu_w_s, m_sgu_b_s, m_attn_sink, m_rel_bias, m_norm_ffn, m_norm_final]
    small_v = [v_norm_mix, v_sgu_v_gain, v_sgu_w_s, v_sgu_b_s, v_attn_sink, v_rel_bias, v_norm_ffn, v_norm_final]
    small_shapes = [w.shape for w in small_w]
    local_small = [dg_mix, dgain, dws, dbs, dsink[:, 0], drel[:, :REL_BUCKETS].T, dg_ffn, dg_final]
    local_small = [g.reshape(shp) for g, shp in zip(local_small, small_shapes)]
    packed_g = _small_all_reduce(_pack(local_small + [loss_part[0, :1]]))
    g_small = _unpack(packed_g, small_shapes + [(1,)])
    loss = g_small[-1].reshape(())
    g_small = g_small[:-1]
    zero1 = jnp.zeros((1,), F32)
    pw, pg, pm, pv = _pack(small_w + [zero1]), _pack(g_small + [zero1]), _pack(small_m + [zero1]), _pack(small_v + [zero1])
    small_upd = _adamw("adamw_small", pw, pg, pm, pv)
    d_small, nm_small, nv_small = [_unpack(a, small_shapes) for a in small_upd]

    grads_big, upd = [None] * 7, [None] * 7
    after = small_upd[0]
    for tag, started in (("down", red_down), ("ffn_in", red_ffn), ("w_o", red_o), ("mix", red_mix), ("w_in", red_in)):
        shared = reduce_finish(tag, started, after)
        for i, g in zip(started[0], shared):
            grads_big[i] = g
            upd[i] = _adamw("adamw_" + names[i], big_w[i], g, big_m[i], big_v[i])
            after = upd[i][0]

    small_names = ["norm_mix", "sgu_v_gain", "sgu_w_s", "sgu_b_s", "attn_sink", "rel_bias", "norm_ffn", "norm_final"]
    table = {}
    for i, n in enumerate(names):
        table[n] = (grads_big[i][None], upd[i][0][None], upd[i][1][None], upd[i][2][None])
    for i, n in enumerate(small_names):
        table[n] = (g_small[i], d_small[i], nm_small[i], nv_small[i])
    order = ["w_in", "norm_mix", "sgu_v_gain", "sgu_w_s", "sgu_b_s", "w_a", "attn_sink", "rel_bias", "w_b", "w_o", "norm_ffn",
             "w_gate", "w_up", "w_down", "norm_final"]
    outs = [loss, grad_x.reshape(1, s, d)]
    for part in range(4):
        outs += [table[n][part] for n in order]
    return tuple(outs)
```

```python
import math

import jax
import jax.numpy as jnp
import numpy as np
from jax import lax
from jax.experimental import pallas as pl
from jax.experimental.pallas import tpu as pltpu

F32 = jnp.float32
BF16 = jnp.bfloat16
I32 = jnp.int32
MESH = pl.DeviceIdType.MESH

EPS = 1e-6
NEG = -1e30
BLK = 128
HEAD_DIM = 128
N_KV_HEADS = 2
REL_BUCKETS = 32
REL_MAX_DIST = 128
N_CHIPS = 4
ADAM_LR, ADAM_B1, ADAM_B2, ADAM_EPS, ADAM_WD, ADAM_STEP = 0.001, 0.9, 0.999, 1e-08, 0.01, 10

LANES = 128
VMEM_CAP = 60 * 1024 * 1024

NN = (((1,), (0,)), ((), ()))
NT = (((1,), (1,)), ((), ()))
TN = (((0,), (0,)), ((), ()))
ANY = pl.BlockSpec(memory_space=pl.ANY)
HBM_SPEC = pl.BlockSpec(memory_space=pltpu.HBM)
SEM_SPEC = pl.BlockSpec(memory_space=pltpu.SEMAPHORE)
EFFECT = pltpu.SideEffectType.DATAFLOW_SIDE_EFFECTING


def _tile(n, cands):
    for t in cands:
        if n % t == 0:
            return t
    return n


def _params(vmem_bytes=None, **kw):
    if vmem_bytes is not None:
        kw["vmem_limit_bytes"] = int(min(max(vmem_bytes, 32 * 1024 * 1024), VMEM_CAP))
    return pltpu.CompilerParams(**kw)


def _nbytes(shape, dtype):
    return int(np.prod(shape)) * jnp.dtype(dtype).itemsize


def _sigmoid(x):
    return 1.0 / (1.0 + jnp.exp(-x))


_GC = 0.7978845608028654
_GA = 0.044715


def _gelu(x):
    return 0.5 * x * (1.0 + jnp.tanh(_GC * (x + _GA * (x * x * x))))


def _gelu_grad(x):
    t = jnp.tanh(_GC * (x + _GA * (x * x * x)))
    return 0.5 * (1.0 + t) + 0.5 * x * (1.0 - t * t) * (_GC * (1.0 + 3.0 * _GA * (x * x)))


def _bf(v):
    return v if v.dtype == BF16 else v.astype(BF16)


def _mm(name, grid, ins, in_specs, out_shape, out_specs, pairs, n_acc, acc_tile, nk, epilogue, vmem_bytes, after=()):
    n_in, n_out = len(ins) + len(after), len(out_shape)

    def body(*refs):
        in_refs, out_refs, accs = refs[:n_in], refs[n_in : n_in + n_out], refs[n_in + n_out :]

        def products():
            vals = [None] * n_acc
            for a_i, b_i, dn, acc_i in pairs:
                d = lax.dot_general(_bf(in_refs[a_i][...]), _bf(in_refs[b_i][...]), dn, preferred_element_type=F32)
                vals[acc_i] = d if vals[acc_i] is None else vals[acc_i] + d
            return vals

        if nk == 1:
            epilogue(in_refs, products(), out_refs)
            return
        k = pl.program_id(2)
        vals = products()

        @pl.when(k == 0)
        def _():
            for a, v in zip(accs, vals):
                a[...] = v

        @pl.when(k > 0)
        def _():
            for a, v in zip(accs, vals):
                a[...] += v

        @pl.when(k == nk - 1)
        def _():
            epilogue(in_refs, [a[...] for a in accs], out_refs)

    scratch = [pltpu.VMEM(acc_tile, F32) for _ in range(n_acc)] if nk > 1 else []
    return pl.pallas_call(
        body,
        name=name,
        grid=grid,
        in_specs=list(in_specs) + [ANY] * len(after),
        out_specs=out_specs,
        out_shape=out_shape,
        scratch_shapes=scratch,
        compiler_params=_params(vmem_bytes),
    )(*ins, *after)


def _mm_vmem(tiles):
    return sum(_nbytes(s, d) * c for s, d, c in tiles) + 4 * 1024 * 1024


def _store_cast(ref, v):
    ref[...] = v.astype(ref.dtype)


def _rows8(v):
    r, d = v.shape
    return v.reshape(r // 8, 8, d).sum(axis=0)


def _rms_fwd(name, x, g, after=()):
    s, d = x.shape
    tm = _tile(s, (256, 128))

    def body(x_ref, g_ref, *rest):
        h_ref = rest[-1]
        xv = x_ref[...]
        r = lax.rsqrt(jnp.mean(xv * xv, axis=-1, keepdims=True) + EPS)
        h_ref[...] = ((xv * r) * g_ref[...]).astype(BF16)

    return pl.pallas_call(
        body,
        name=name,
        grid=(s // tm,),
        in_specs=[pl.BlockSpec((tm, d), lambda i: (i, 0)), pl.BlockSpec((1, d), lambda i: (0, 0))] + [ANY] * len(after),
        out_specs=pl.BlockSpec((tm, d), lambda i: (i, 0)),
        out_shape=jax.ShapeDtypeStruct((s, d), BF16),
    )(x, g, *after)


def _rms_bwd(name, x, g, dh, dres, after=()):
    s, d = x.shape
    tm = _tile(s, (256, 128))
    n = s // tm
    n_after = len(after)

    def body(x_ref, g_ref, dh_ref, dres_ref, *rest):
        dx_ref, dxb_ref, dg_ref, acc_ref = rest[n_after:]
        i = pl.program_id(0)
        xv = x_ref[...]
        r = lax.rsqrt(jnp.mean(xv * xv, axis=-1, keepdims=True) + EPS)
        xh = xv * r
        dhv = dh_ref[...]
        dxh = dhv * g_ref[...]
        dx = r * (dxh - xh * jnp.mean(dxh * xh, axis=-1, keepdims=True)) + dres_ref[...]
        dx_ref[...] = dx
        dxb_ref[...] = dx.astype(BF16)
        part = _rows8(dhv * xh)

        @pl.when(i == 0)
        def _():
            acc_ref[...] = part

        @pl.when(i > 0)
        def _():
            acc_ref[...] += part

        @pl.when(i == n - 1)
        def _():
            dg_ref[...] = jnp.sum(acc_ref[...], axis=0, keepdims=True)

    row = pl.BlockSpec((tm, d), lambda i: (i, 0))
    vec = pl.BlockSpec((1, d), lambda i: (0, 0))
    return pl.pallas_call(
        body,
        name=name,
        grid=(n,),
        in_specs=[row, vec, row, row] + [ANY] * n_after,
        out_specs=[row, row, vec],
        out_shape=[jax.ShapeDtypeStruct((s, d), F32), jax.ShapeDtypeStruct((s, d), BF16), jax.ShapeDtypeStruct((1, d), F32)],
        scratch_shapes=[pltpu.VMEM((8, d), F32)],
    )(x, g, dh, dres, *after)


def _head(x3, g, target):
    s, d = x3.shape
    tm = _tile(s, (256, 128))
    n = s // tm

    def body(x_ref, g_ref, t_ref, dx_ref, dxb_ref, dg_ref, loss_ref, acc_g, acc_l):
        i = pl.program_id(0)
        xv = x_ref[...]
        gv = g_ref[...]
        r = lax.rsqrt(jnp.mean(xv * xv, axis=-1, keepdims=True) + EPS)
        xh = xv * r
        e = xh * gv - t_ref[...]
        dy = e * (1.0 / d)
        dxh = dy * gv
        dx = r * (dxh - xh * jnp.mean(dxh * xh, axis=-1, keepdims=True))
        dx_ref[...] = dx
        dxb_ref[...] = dx.astype(BF16)
        pg = _rows8(dy * xh)
        plo = _rows8(e * e)

        @pl.when(i == 0)
        def _():
            acc_g[...] = pg
            acc_l[...] = plo

        @pl.when(i > 0)
        def _():
            acc_g[...] += pg
            acc_l[...] += plo

        @pl.when(i == n - 1)
        def _():
            dg_ref[...] = jnp.sum(acc_g[...], axis=0, keepdims=True)
            loss_ref[...] = jnp.full((1, LANES), (0.5 / d) * jnp.sum(acc_l[...]), F32)

    row = pl.BlockSpec((tm, d), lambda i: (i, 0))
    vec = pl.BlockSpec((1, d), lambda i: (0, 0))
    return pl.pallas_call(
        body,
        name="head",
        grid=(n,),
        in_specs=[row, vec, row],
        out_specs=[row, row, vec, pl.BlockSpec((1, LANES), lambda i: (0, 0))],
        out_shape=[
            jax.ShapeDtypeStruct((s, d), F32),
            jax.ShapeDtypeStruct((s, d), BF16),
            jax.ShapeDtypeStruct((1, d), F32),
            jax.ShapeDtypeStruct((1, LANES), F32),
        ],
        scratch_shapes=[pltpu.VMEM((8, d), F32), pltpu.VMEM((8, d), F32)],
    )(x3, g, target)


def _sgu_fwd(z, gain, ws_b, b_col, w_sgu):
    s = z.shape[0]
    groups = ws_b.shape[0]

    def body(zu_ref, zv_ref, gain_ref, ws_ref, b_ref, a_ref):
        vv = _gelu(zv_ref[...].astype(F32))
        r = lax.rsqrt(jnp.mean(vv * vv, axis=-1, keepdims=True) + EPS)
        vn = ((vv * r) * gain_ref[...]).astype(BF16)
        u = _gelu(zu_ref[...].astype(F32))
        for g in range(groups):
            sl = slice(g * BLK, (g + 1) * BLK)
            mixed = jnp.dot(ws_ref[g], vn[:, sl], preferred_element_type=F32) + b_ref[g]
            a_ref[:, sl] = (u[:, sl] * mixed).astype(BF16)

    return pl.pallas_call(
        body,
        name="sgu_fwd",
        grid=(s // BLK,),
        in_specs=[
            pl.BlockSpec((BLK, w_sgu), lambda c: (c, 0)),
            pl.BlockSpec((BLK, w_sgu), lambda c: (c, 1)),
            pl.BlockSpec((1, w_sgu), lambda c: (0, 0)),
            pl.BlockSpec((groups, BLK, BLK), lambda c: (0, 0, 0)),
            pl.BlockSpec((groups, BLK, 1), lambda c: (0, 0, 0)),
        ],
        out_specs=pl.BlockSpec((BLK, w_sgu), lambda c: (c, 0)),
        out_shape=jax.ShapeDtypeStruct((s, w_sgu), BF16),
    )(z, z, gain, ws_b, b_col)


def _sgu_bwd(z, da, gain, ws_b, wst_b, b_col, w_sgu, dz, after=()):
    s = z.shape[0]
    groups = ws_b.shape[0]
    n = s // BLK
    n_skip = 1 + len(after)

    def body(zu_ref, zv_ref, da_ref, gain_ref, ws_ref, wst_ref, b_ref, *rest):
        dz_ref, dws_ref, dbs_ref, dgain_ref, acc_gain = rest[n_skip:]
        c = pl.program_id(0)
        zu = zu_ref[...].astype(F32)
        zv = zv_ref[...].astype(F32)
        gain_v = gain_ref[...]
        vv = _gelu(zv)
        r = lax.rsqrt(jnp.mean(vv * vv, axis=-1, keepdims=True) + EPS)
        xh = vv * r
        vn = (xh * gain_v).astype(BF16)
        u = _gelu(zu)
        dav = da_ref[...].astype(F32)
        dmix = dav * u
        dmix_b = dmix.astype(BF16)
        dvn_parts = []
        for g in range(groups):
            sl = slice(g * BLK, (g + 1) * BLK)
            mixed = jnp.dot(ws_ref[g], vn[:, sl], preferred_element_type=F32) + b_ref[g]
            dz_ref[:, sl] = (dav[:, sl] * mixed * _gelu_grad(zu[:, sl])).astype(BF16)
            dvn_parts.append(jnp.dot(wst_ref[g], dmix_b[:, sl], preferred_element_type=F32))
            dws_g = lax.dot_general(dmix_b[:, sl], vn[:, sl], NT, preferred_element_type=F32)
            dbs_g = jnp.sum(dmix[:, sl], axis=1, keepdims=True)

            @pl.when(c == 0)
            def _():
                dws_ref[g] = dws_g
                dbs_ref[g] = dbs_g

            @pl.when(c > 0)
            def _():
                dws_ref[g] += dws_g
                dbs_ref[g] += dbs_g

        dvn = jnp.concatenate(dvn_parts, axis=1)
        dxh = dvn * gain_v
        dvv = r * (dxh - xh * jnp.mean(dxh * xh, axis=-1, keepdims=True))
        dz_ref[:, w_sgu:] = (dvv * _gelu_grad(zv)).astype(BF16)
        pg = _rows8(dvn * xh)

        @pl.when(c == 0)
        def _():
            acc_gain[...] = pg

        @pl.when(c > 0)
        def _():
            acc_gain[...] += pg

        @pl.when(c == n - 1)
        def _():
            dgain_ref[...] = jnp.sum(acc_gain[...], axis=0, keepdims=True)

    full3 = pl.BlockSpec((groups, BLK, BLK), lambda c: (0, 0, 0))
    col3 = pl.BlockSpec((groups, BLK, 1), lambda c: (0, 0, 0))
    vec = pl.BlockSpec((1, w_sgu), lambda c: (0, 0))
    return pl.pallas_call(
        body,
        name="sgu_bwd",
        grid=(n,),
        in_specs=[
            pl.BlockSpec((BLK, w_sgu), lambda c: (c, 0)),
            pl.BlockSpec((BLK, w_sgu), lambda c: (c, 1)),
            pl.BlockSpec((BLK, w_sgu), lambda c: (c, 0)),
            vec,
            full3,
            full3,
            col3,
            ANY,
        ]
        + [ANY] * len(after),
        out_specs=[pl.BlockSpec((BLK, 2 * w_sgu), lambda c: (c, 0)), full3, col3, vec],
        out_shape=[
            jax.ShapeDtypeStruct(dz.shape, BF16),
            jax.ShapeDtypeStruct((groups, BLK, BLK), F32),
            jax.ShapeDtypeStruct((groups, BLK, 1), F32),
            jax.ShapeDtypeStruct((1, w_sgu), F32),
        ],
        scratch_shapes=[pltpu.VMEM((8, w_sgu), F32)],
        input_output_aliases={7: 0},
    )(z, z, da, gain, ws_b, wst_b, b_col, dz, *after)


def _attn_softmax(sink_ref, q_ref, k_ref, v_ref, bias_ref, s_len, grp):
    kv = pl.program_id(0)
    n = pl.program_id(1)
    start = pl.multiple_of(n * BLK, BLK)
    kb = k_ref[pl.ds(start, 3 * BLK), :]
    vb = v_ref[pl.ds(start, 3 * BLK), :]
    qv = q_ref[...]
    qs = jnp.concatenate([qv[:, g * HEAD_DIM : (g + 1) * HEAD_DIM] for g in range(grp)], axis=0).astype(BF16)
    sc = lax.dot_general(qs, kb, NT, preferred_element_type=F32) * (HEAD_DIM**-0.5)
    sc = sc + bias_ref[...].reshape(grp * BLK, 3 * BLK)
    kpos = start + lax.broadcasted_iota(I32, (1, 3 * BLK), 1) - BLK
    sc = jnp.where((kpos >= 0) & (kpos < s_len), sc, NEG)
    sink = jnp.concatenate([jnp.full((BLK, 1), sink_ref[kv * grp + g], F32) for g in range(grp)], axis=0)
    m = jnp.maximum(jnp.max(sc, axis=-1, keepdims=True), sink)
    p = jnp.exp(sc - m)
    esink = jnp.exp(sink - m)
    den = jnp.sum(p, axis=-1, keepdims=True) + esink
    return start, qs, kb, vb, p / den, esink / den


def _attn_specs(s, grp, q_blk0):
    qw = grp * HEAD_DIM
    return [
        pl.BlockSpec(memory_space=pltpu.SMEM),
        pl.BlockSpec((BLK, qw), lambda kv, n: (n, q_blk0 + kv)),
        pl.BlockSpec((s + 2 * BLK, HEAD_DIM), lambda kv, n: (0, kv)),
        pl.BlockSpec((s + 2 * BLK, HEAD_DIM), lambda kv, n: (0, kv)),
        pl.BlockSpec((grp, BLK, 3 * BLK), lambda kv, n: (kv, 0, 0)),
    ]


def _attn_fwd(sink, z, k_pad, v_pad, bias_tab, grp, q_blk0):
    s = z.shape[0]
    qw = grp * HEAD_DIM

    def body(sink_ref, q_ref, k_ref, v_ref, bias_ref, o_ref):
        _, _, _, vb, pn, _ = _attn_softmax(sink_ref, q_ref, k_ref, v_ref, bias_ref, s, grp)
        o = jnp.dot(pn.astype(BF16), vb, preferred_element_type=F32)
        for g in range(grp):
            o_ref[:, g * HEAD_DIM : (g + 1) * HEAD_DIM] = o[g * BLK : (g + 1) * BLK].astype(BF16)

    return pl.pallas_call(
        body,
        name="attn_fwd",
        grid=(N_KV_HEADS, s // BLK),
        in_specs=_attn_specs(s, grp, q_blk0),
        out_specs=pl.BlockSpec((BLK, qw), lambda kv, n: (n, kv)),
        out_shape=jax.ShapeDtypeStruct((s, N_KV_HEADS * qw), BF16),
    )(sink, z, k_pad, v_pad, bias_tab)


def _attn_bwd(sink, z, k_pad, v_pad, bias_tab, dout, dz, grp, q_blk0):
    s = z.shape[0]
    qw = grp * HEAD_DIM
    nb = s // BLK
    heads = N_KV_HEADS * grp

    def body(sink_ref, q_ref, k_ref, v_ref, bias_ref, do_ref, dz_in, dq_ref, dk_ref, dv_ref, dbias_ref, dsink_ref, dk_acc, dv_acc):
        del dz_in
        kv = pl.program_id(0)
        n = pl.program_id(1)
        start, qs, kb, vb, pn, psink = _attn_softmax(sink_ref, q_ref, k_ref, v_ref, bias_ref, s, grp)
        dov = do_ref[...]
        dos = jnp.concatenate([dov[:, g * HEAD_DIM : (g + 1) * HEAD_DIM] for g in range(grp)], axis=0)
        dp = lax.dot_general(dos, vb, NT, preferred_element_type=F32)
        dvb = lax.dot_general(pn.astype(BF16), dos, TN, preferred_element_type=F32)
        delta = jnp.sum(pn * dp, axis=-1, keepdims=True)
        ds = pn * (dp - delta)
        dsb = (ds * (HEAD_DIM**-0.5)).astype(BF16)
        dq = jnp.dot(dsb, kb, preferred_element_type=F32)
        dkb = lax.dot_general(dsb, qs, TN, preferred_element_type=F32)
        for g in range(grp):
            dq_ref[:, g * HEAD_DIM : (g + 1) * HEAD_DIM] = dq[g * BLK : (g + 1) * BLK].astype(BF16)

        @pl.when(n == 0)
        def _():
            dk_acc[...] = jnp.zeros_like(dk_acc)
            dv_acc[...] = jnp.zeros_like(dv_acc)
            dbias_ref[...] = jnp.zeros_like(dbias_ref)

        @pl.when((n == 0) & (kv == 0))
        def _():
            dsink_ref[...] = jnp.zeros_like(dsink_ref)

        dk_acc[pl.ds(start, 3 * BLK), :] += dkb
        dv_acc[pl.ds(start, 3 * BLK), :] += dvb
        dbias_ref[...] += ds.reshape(grp, BLK, 3 * BLK)
        row = lax.broadcasted_iota(I32, (heads, LANES), 0)
        sd = psink * delta
        upd = jnp.zeros((heads, LANES), F32)
        for g in range(grp):
            upd = jnp.where(row == kv * grp + g, -jnp.sum(sd[g * BLK : (g + 1) * BLK]), upd)
        dsink_ref[...] += upd

        @pl.when(n == nb - 1)
        def _():
            dk_ref[...] = dk_acc[...]
            dv_ref[...] = dv_acc[...]

    pad_spec = pl.BlockSpec((s + 2 * BLK, HEAD_DIM), lambda kv, n: (0, kv))
    kvw = N_KV_HEADS * HEAD_DIM
    return pl.pallas_call(
        body,
        name="attn_bwd",
        grid=(N_KV_HEADS, nb),
        in_specs=_attn_specs(s, grp, q_blk0) + [pl.BlockSpec((BLK, qw), lambda kv, n: (n, kv)), ANY],
        out_specs=[
            pl.BlockSpec((BLK, qw), lambda kv, n: (n, q_blk0 + kv)),
            pad_spec,
            pad_spec,
            pl.BlockSpec((grp, BLK, 3 * BLK), lambda kv, n: (kv, 0, 0)),
            pl.BlockSpec((heads, LANES), lambda kv, n: (0, 0)),
        ],
        out_shape=[
            jax.ShapeDtypeStruct(dz.shape, BF16),
            jax.ShapeDtypeStruct((s + 2 * BLK, kvw), F32),
            jax.ShapeDtypeStruct((s + 2 * BLK, kvw), F32),
            jax.ShapeDtypeStruct((heads, BLK, 3 * BLK), F32),
            jax.ShapeDtypeStruct((heads, LANES), F32),
        ],
        scratch_shapes=[pltpu.VMEM((s + 2 * BLK, HEAD_DIM), F32), pltpu.VMEM((s + 2 * BLK, HEAD_DIM), F32)],
        input_output_aliases={6: 0},
    )(sink, z, k_pad, v_pad, bias_tab, dout, dz)


def _dkv_to_dz(dk_pad, dv_pad, dz, blk_idx):
    s = dz.shape[0]
    kvw = dk_pad.shape[1]

    def body(dk_ref, dv_ref, dz_in, out_ref):
        del dz_in
        out_ref[:, :kvw] = dk_ref[...].astype(BF16)
        out_ref[:, kvw:] = dv_ref[...].astype(BF16)

    src = pl.BlockSpec((BLK, kvw), lambda i: (i + 1, 0))
    return pl.pallas_call(
        body,
        name="dkv_to_dz",
        grid=(s // BLK,),
        in_specs=[src, src, ANY],
        out_specs=pl.BlockSpec((BLK, 2 * kvw), lambda i: (i, blk_idx)),
        out_shape=jax.ShapeDtypeStruct(dz.shape, BF16),
        input_output_aliases={2: 0},
    )(dk_pad, dv_pad, dz)


def _relbias_bwd(dbias_tab, bucket):
    heads = dbias_tab.shape[0]

    def body(dt_ref, bk_ref, out_ref):
        lane = lax.broadcasted_iota(I32, (1, LANES), 1)
        bk = bk_ref[...]
        rows = []
        for h in range(heads):
            dt = dt_ref[h]
            acc = jnp.zeros((1, LANES), F32)
            for b in range(REL_BUCKETS):
                acc = jnp.where(lane == b, jnp.sum(jnp.where(bk == b, dt, 0.0)), acc)
            rows.append(acc)
        out_ref[...] = jnp.concatenate(rows, axis=0)

    return pl.pallas_call(body, name="relbias_bwd", out_shape=jax.ShapeDtypeStruct((heads, LANES), F32))(dbias_tab, bucket)


def _t5_bucket(rel):
    nb = REL_BUCKETS // 2
    ret = jnp.where(rel > 0, nb, 0)
    n = jnp.abs(rel)
    max_exact = nb // 2
    nf = jnp.maximum(n, 1).astype(F32)
    large = max_exact + (jnp.log(nf / max_exact) / math.log(REL_MAX_DIST / max_exact) * (nb - max_exact)).astype(I32)
    large = jnp.minimum(large, nb - 1)
    return ret + jnp.where(n < max_exact, n, large)


def _band_tables(rel_bias):
    qi = jnp.arange(BLK)[:, None]
    kj = jnp.arange(3 * BLK)[None, :]
    rel = kj - BLK - qi
    bucket = _t5_bucket(rel).astype(I32)
    heads = rel_bias.shape[1]
    masked = jnp.where(jnp.abs(rel) <= BLK, bucket, -1)

    def body(rb_ref, bk_ref, out_ref):
        bk = bk_ref[...]
        for h in range(heads):
            tab = jnp.full(bk.shape, NEG, F32)
            for b in range(REL_BUCKETS):
                tab = jnp.where(bk == b, rb_ref[b, h], tab)
            out_ref[h] = tab

    bias_tab = pl.pallas_call(
        body,
        name="bias_table",
        in_specs=[pl.BlockSpec(memory_space=pltpu.SMEM), pl.BlockSpec(memory_space=pltpu.VMEM)],
        out_specs=pl.BlockSpec(memory_space=pltpu.VMEM),
        out_shape=jax.ShapeDtypeStruct((heads, BLK, 3 * BLK), F32),
    )(rel_bias.astype(F32), masked)
    return bias_tab, bucket


EW_BLOCK_ELEMS = 512 * 1024


def _ew_tiles(shape, elems=EW_BLOCK_ELEMS // 2):
    r, c = shape
    tn = c if c <= 2048 else _tile(c, (2048, 1920, 1536, 1408, 1024, 512))
    tm = _tile(r, [t for t in (1024, 512, 256, 128, 64, 32, 16, 8) if t * tn <= elems] or [8])
    return tm, tn


def _cast_into_full(name, qidx, w, kind, after=()):
    r, c = w.shape
    tm, tn = _ew_tiles(w.shape, EW_BLOCK_ELEMS)
    nbi, nbj = r // tm, c // tn
    if kind == "col":
        full, out_spec = (r, c * N_CHIPS), pl.BlockSpec((tm, tn), lambda i, j, q: (i, q[0] * nbj + j))
    else:
        full, out_spec = (r * N_CHIPS, c), pl.BlockSpec((tm, tn), lambda i, j, q: (q[0] * nbi + i, j))

    def body(q_ref, w_ref, *rest):
        del q_ref
        rest[-1][...] = w_ref[...].astype(BF16)

    return pl.pallas_call(
        body,
        name=name,
        grid_spec=pltpu.PrefetchScalarGridSpec(
            num_scalar_prefetch=1,
            grid=(nbi, nbj),
            in_specs=[pl.BlockSpec((tm, tn), lambda i, j, q: (i, j))] + [ANY] * len(after),
            out_specs=out_spec,
        ),
        out_shape=jax.ShapeDtypeStruct(full, BF16),
    )(qidx, w, *after)


def _adamw(name, w, g, m, v, after=()):
    tm, tn = _ew_tiles(w.shape)
    if _nbytes(w.shape, F32) <= 1024 * 1024:
        tm, tn = w.shape
    spec = pl.BlockSpec((tm, tn), lambda i, j: (i, j))
    n_after = len(after)

    def body(w_ref, g_ref, m_ref, v_ref, *rest):
        d_ref, nm_ref, nv_ref = rest[n_after:]
        gv = g_ref[...]
        nm = ADAM_B1 * m_ref[...] + (1.0 - ADAM_B1) * gv
        nv = ADAM_B2 * v_ref[...] + (1.0 - ADAM_B2) * (gv * gv)
        m_hat = nm / (1.0 - ADAM_B1**ADAM_STEP)
        v_hat = nv / (1.0 - ADAM_B2**ADAM_STEP)
        d_ref[...] = -ADAM_LR * (m_hat / (jnp.sqrt(v_hat) + ADAM_EPS) + ADAM_WD * w_ref[...])
        nm_ref[...] = nm
        nv_ref[...] = nv

    out = jax.ShapeDtypeStruct(w.shape, F32)
    return pl.pallas_call(
        body, name=name, grid=(w.shape[0] // tm, w.shape[1] // tn), in_specs=[spec] * 4 + [ANY] * n_after,
        out_specs=[spec] * 3, out_shape=[out, out, out],
    )(w, g, m, v, *after)


def _pair_add(name, cidx, g_full, r_sib, kind):
    hr, hc = r_sib.shape
    tm, tn = _ew_tiles((hr, hc), EW_BLOCK_ELEMS)
    nbi, nbj = hr // tm, hc // tn
    if kind == "col":
        g_spec = pl.BlockSpec((tm, tn), lambda i, j, c: (c[0] * nbi + i, j))
    else:
        g_spec = pl.BlockSpec((tm, tn), lambda i, j, c: (i, c[0] * nbj + j))
    spec = pl.BlockSpec((tm, tn), lambda i, j, c: (i, j))

    def body(c_ref, g_ref, r_ref, o_ref):
        del c_ref
        o_ref[...] = (g_ref[...].astype(F32) + r_ref[...].astype(F32)).astype(BF16)

    return pl.pallas_call(
        body,
        name=name,
        grid_spec=pltpu.PrefetchScalarGridSpec(num_scalar_prefetch=1, grid=(nbi, nbj), in_specs=[g_spec, spec], out_specs=spec),
        out_shape=jax.ShapeDtypeStruct((hr, hc), BF16),
    )(cidx, g_full, r_sib)


def _chip_sum(name, qidx, c_half, r_ici, kind):
    _, pr, pc = r_ici.shape
    tm, tn = _ew_tiles((pr, pc), EW_BLOCK_ELEMS)
    nbi, nbj = pr // tm, pc // tn
    if kind == "col":
        own_spec = pl.BlockSpec((tm, tn), lambda i, j, q: (i, q[0] * nbj + j))
        full, out_spec = (2 * pr, pc), pl.BlockSpec((tm, tn), lambda i, j, q: (q[1] * nbi + i, j))
    else:
        own_spec = pl.BlockSpec((tm, tn), lambda i, j, q: (q[0] * nbi + i, j))
        full, out_spec = (pr, 2 * pc), pl.BlockSpec((tm, tn), lambda i, j, q: (i, q[1] * nbj + j))

    def body(q_ref, own_ref, r_ref, o_ref):
        q = q_ref[0]
        own = own_ref[...].astype(F32)
        recv = [r_ref[r].astype(F32) for r in range(3)]
        total = None
        for chip in range(N_CHIPS):
            d = chip ^ q
            term = jnp.where(d == 0, own, jnp.where(d == 2, recv[0], jnp.where(d == 1, recv[1], recv[2])))
            total = term if total is None else total + term
        o_ref[...] = total

    return pl.pallas_call(
        body,
        name=name,
        grid_spec=pltpu.PrefetchScalarGridSpec(
            num_scalar_prefetch=1,
            grid=(nbi, nbj),
            in_specs=[own_spec, pl.BlockSpec((3, tm, tn), lambda i, j, q: (0, i, j))],
            out_specs=out_spec,
        ),
        out_shape=jax.ShapeDtypeStruct(full, F32),
    )(qidx, c_half, r_ici)


_REL_MASK = (2, 1, 3)


def _place():
    x, y, c = lax.axis_index("x"), lax.axis_index("y"), lax.axis_index("c")
    chips = [(1 - x, y), (x, 1 - y), (1 - x, 1 - y)]
    return x, y, c, 2 * x + y, chips


def _shard_view(ref, kind, chip):
    if kind == "col":
        w = ref.shape[1] // N_CHIPS
        return ref.at[:, pl.ds(pl.multiple_of(chip * w, LANES), w)]
    h = ref.shape[0] // N_CHIPS
    return ref.at[pl.ds(pl.multiple_of(chip * h, 16), h), :]


def _row_half(ref, half):
    h = ref.shape[0] // 2
    return ref.at[pl.ds(pl.multiple_of(half * h, 16), h), :]


def _pair_half(ref, kind, half):
    if kind == "col":
        return _row_half(ref, half)
    w = ref.shape[1] // 2
    return ref.at[:, pl.ds(pl.multiple_of(half * w, LANES), w)]


def _remote(src, dst, send_sem, recv_sem, dev):
    return pltpu.make_async_remote_copy(src_ref=src, dst_ref=dst, send_sem=send_sem, recv_sem=recv_sem, device_id=dev, device_id_type=MESH)


def _hbm(a):
    return pltpu.with_memory_space_constraint(a, pltpu.HBM)


def _gather_start(name, fulls, kinds):
    n_w = len(fulls)

    def body(*refs):
        g = refs[:n_w]
        send_sem, recv_sem = refs[n_w], refs[n_w + 1]
        token = refs[-1]
        _, _, c, q, chips = _place()
        for w in range(n_w):
            mine = _row_half(_shard_view(g[w], kinds[w], q), c)
            for r, chip in enumerate(chips):
                _remote(mine, mine, send_sem.at[3 * w + r], recv_sem.at[3 * w + r], (*chip, c)).start()
        token[...] = jnp.zeros_like(token)

    res = pl.pallas_call(
        body,
        name=name,
        out_shape=(
            pltpu.SemaphoreType.DMA((3 * n_w,)),
            pltpu.SemaphoreType.DMA((3 * n_w,)),
            *[pltpu.HBM(f.shape, f.dtype) for f in fulls],
            jax.ShapeDtypeStruct((8, LANES), F32),
        ),
        in_specs=[HBM_SPEC] * n_w,
        out_specs=(SEM_SPEC, SEM_SPEC, *[HBM_SPEC] * n_w, pl.BlockSpec(memory_space=pltpu.VMEM)),
        input_output_aliases={w: w + 2 for w in range(n_w)},
        compiler_params=pltpu.CompilerParams(has_side_effects=EFFECT),
    )(*[_hbm(f) for f in fulls])
    return res[0], res[1], list(res[2 : 2 + n_w]), res[-1]


def _gather_wait(name, fulls, kinds, w_ids, send_sem, recv_sem, after):
    n = len(fulls)

    def body(*refs):
        g = refs[:n]
        s_sem, r_sem = refs[n], refs[n + 1]
        x, y, c, q, _ = _place()
        for i, w in enumerate(w_ids):
            mine = _row_half(_shard_view(g[i], kinds[i], q), c)
            for r in range(3):
                landed = _row_half(_shard_view(g[i], kinds[i], q ^ _REL_MASK[r]), c)
                cp = _remote(mine, landed, s_sem.at[3 * w + r], r_sem.at[3 * w + r], (x, y, 1 - c))
                cp.wait_send()
                cp.wait_recv()

    res = pl.pallas_call(
        body,
        name=name,
        out_shape=[pltpu.HBM(f.shape, f.dtype) for f in fulls],
        in_specs=[HBM_SPEC] * n + [SEM_SPEC, SEM_SPEC, ANY],
        out_specs=[HBM_SPEC] * n,
        input_output_aliases={i: i for i in range(n)},
        compiler_params=pltpu.CompilerParams(has_side_effects=EFFECT),
    )(*fulls, send_sem, recv_sem, after)
    return list(res)


def _gather_forward(name, fulls, kinds):
    n = len(fulls)

    def body(*refs):
        g = refs[n : 2 * n]
        send, recv = refs[2 * n :]
        x, y, c, q, _ = _place()
        sib = (x, y, 1 - c)
        cps = []
        for i in range(n):
            for r in range(3):
                landed = _row_half(_shard_view(g[i], kinds[i], q ^ _REL_MASK[r]), c)
                cps.append(_remote(landed, landed, send.at[i, r], recv.at[i, r], sib))
        for cp in cps:
            cp.start()
        for i in range(n):
            for r in range(3):
                other = _row_half(_shard_view(g[i], kinds[i], q ^ _REL_MASK[r]), 1 - c)
                _remote(other, other, send.at[i, r], recv.at[i, r], sib).wait_recv()
        for cp in cps:
            cp.wait_send()

    res = pl.pallas_call(
        body,
        name=name,
        in_specs=[ANY] * n,
        out_specs=[ANY] * n,
        out_shape=[jax.ShapeDtypeStruct(f.shape, f.dtype) for f in fulls],
        scratch_shapes=[pltpu.SemaphoreType.DMA((n, 3)), pltpu.SemaphoreType.DMA((n, 3))],
        input_output_aliases={i: i for i in range(n)},
    )(*fulls)
    return list(res)


def _split_start(name, bufs, n_sems, copies):
    n = len(bufs)

    def body(*refs):
        for cp in copies(refs[:n], refs[n], refs[n + 1]):
            cp.start()

    res = pl.pallas_call(
        body,
        name=name,
        out_shape=(
            pltpu.SemaphoreType.DMA((n_sems,)),
            pltpu.SemaphoreType.DMA((n_sems,)),
            *[pltpu.HBM(b.shape, b.dtype) for b in bufs],
        ),
        in_specs=[HBM_SPEC] * n,
        out_specs=(SEM_SPEC, SEM_SPEC, *[HBM_SPEC] * n),
        input_output_aliases={i: i + 2 for i in range(n)},
        compiler_params=pltpu.CompilerParams(has_side_effects=EFFECT),
    )(*[_hbm(b) for b in bufs])
    return res[0], res[1], list(res[2:])


def _split_wait(name, bufs, send_sem, recv_sem, copies, after):
    n = len(bufs)

    def body(*refs):
        for cp in copies(refs[:n], refs[n], refs[n + 1]):
            cp.wait_send()
            cp.wait_recv()

    res = pl.pallas_call(
        body,
        name=name,
        out_shape=[pltpu.HBM(b.shape, b.dtype) for b in bufs],
        in_specs=[HBM_SPEC] * n + [SEM_SPEC, SEM_SPEC, ANY],
        out_specs=[HBM_SPEC] * n,
        input_output_aliases={i: i for i in range(n)},
        compiler_params=pltpu.CompilerParams(has_side_effects=EFFECT),
    )(*bufs, send_sem, recv_sem, after)
    return list(res)


def _pair_exchange_copies(kinds):
    n = len(kinds)

    def copies(refs, send_sem, recv_sem):
        x, y, c, _, _ = _place()
        return [
            _remote(_pair_half(refs[w], kinds[w], 1 - c), refs[n + w], send_sem.at[w], recv_sem.at[w], (x, y, 1 - c))
            for w in range(n)
        ]

    return copies


def _pair_share_copies(kinds, waiting):
    def copies(refs, send_sem, recv_sem):
        x, y, c, _, _ = _place()
        out = []
        for w, kind in enumerate(kinds):
            mine = _pair_half(refs[w], kind, c)
            dst = _pair_half(refs[w], kind, 1 - c) if waiting else mine
            out.append(_remote(mine, dst, send_sem.at[w], recv_sem.at[w], (x, y, 1 - c)))
        return out

    return copies


def _piece_shape(half_shape, kind):
    r, c = half_shape
    return (3, r, c // N_CHIPS) if kind == "col" else (3, r // N_CHIPS, c)


def _chip_send_start(name, halves, kinds):
    n = len(halves)
    lands = [lax.empty(_piece_shape(h.shape, k), BF16) for h, k in zip(halves, kinds)]

    def body(*refs):
        h, land = refs[:n], refs[n : 2 * n]
        send_sem, recv_sem = refs[2 * n], refs[2 * n + 1]
        _, _, c, q, chips = _place()
        for i in range(n):
            for r, chip in enumerate(chips):
                piece = _shard_view(h[i], kinds[i], q ^ _REL_MASK[r])
                _remote(piece, land[i].at[r], send_sem.at[3 * i + r], recv_sem.at[3 * i + r], (*chip, c)).start()

    res = pl.pallas_call(
        body,
        name=name,
        out_shape=(
            pltpu.SemaphoreType.DMA((3 * n,)),
            pltpu.SemaphoreType.DMA((3 * n,)),
            *[pltpu.HBM(a.shape, a.dtype) for a in halves],
            *[pltpu.HBM(a.shape, a.dtype) for a in lands],
        ),
        in_specs=[HBM_SPEC] * (2 * n),
        out_specs=(SEM_SPEC, SEM_SPEC, *[HBM_SPEC] * (2 * n)),
        input_output_aliases={i: i + 2 for i in range(2 * n)},
        compiler_params=pltpu.CompilerParams(has_side_effects=EFFECT),
    )(*[_hbm(a) for a in halves], *[_hbm(a) for a in lands])
    return res[0], res[1], list(res[2 : 2 + n]), list(res[2 + n :])


def _chip_send_wait(name, halves, lands, kinds, send_sem, recv_sem, after):
    n = len(halves)

    def body(*refs):
        h, land = refs[:n], refs[n : 2 * n]
        s_sem, r_sem = refs[2 * n], refs[2 * n + 1]
        x, y, c, q, _ = _place()
        for i in range(n):
            for r in range(3):
                piece = _shard_view(h[i], kinds[i], q ^ _REL_MASK[r])
                cp = _remote(piece, land[i].at[r], s_sem.at[3 * i + r], r_sem.at[3 * i + r], (x, y, 1 - c))
                cp.wait_send()
                cp.wait_recv()

    res = pl.pallas_call(
        body,
        name=name,
        out_shape=[pltpu.HBM(a.shape, a.dtype) for a in halves] + [pltpu.HBM(a.shape, a.dtype) for a in lands],
        in_specs=[HBM_SPEC] * (2 * n) + [SEM_SPEC, SEM_SPEC, ANY],
        out_specs=[HBM_SPEC] * (2 * n),
        input_output_aliases={i: i for i in range(2 * n)},
        compiler_params=pltpu.CompilerParams(has_side_effects=EFFECT),
    )(*halves, *lands, send_sem, recv_sem, after)
    return list(res[:n]), list(res[n:])


def _small_all_reduce(p):
    rows = p.shape[0]
    n_dev = 2 * N_CHIPS

    def body(p_ref, o_ref, buf, loc_sem, send, recv):
        x, y, c, q, _ = _place()
        me = 2 * q + c
        own = pltpu.make_async_copy(p_ref, buf.at[me], loc_sem)
        own.start()
        cps = []
        for d in range(1, n_dev):
            dev = (x ^ ((d >> 2) & 1), y ^ ((d >> 1) & 1), c ^ (d & 1))
            cps.append(_remote(p_ref, buf.at[me], send.at[d - 1], recv.at[d - 1], dev))
        for cp in cps:
            cp.start()
        for d in range(1, n_dev):
            slot = buf.at[me ^ d]
            _remote(slot, slot, send.at[d - 1], recv.at[d - 1], (x, y, c)).wait_recv()
        own.wait()
        total = buf[0]
        for d in range(1, n_dev):
            total = total + buf[d]
        o_ref[...] = total
        for cp in cps:
            cp.wait_send()

    return pl.pallas_call(
        body,
        name="small_all_reduce",
        in_specs=[ANY],
        out_specs=pl.BlockSpec(memory_space=pltpu.VMEM),
        out_shape=jax.ShapeDtypeStruct(p.shape, F32),
        scratch_shapes=[
            pltpu.VMEM((n_dev, rows, LANES), F32),
            pltpu.SemaphoreType.DMA,
            pltpu.SemaphoreType.DMA((n_dev - 1,)),
            pltpu.SemaphoreType.DMA((n_dev - 1,)),
        ],
    )(p)


def _pack(parts):
    rows = []
    for a in parts:
        flat = a.reshape(-1).astype(F32)
        n = flat.shape[0]
        padded = -(-n // (8 * LANES)) * (8 * LANES)
        rows.append(jnp.pad(flat, (0, padded - n)).reshape(-1, LANES))
    return jnp.concatenate(rows, axis=0)


def _unpack(packed, shapes):
    out, row = [], 0
    for shp in shapes:
        n = int(np.prod(shp))
        nrows = -(-n // (8 * LANES)) * 8
        out.append(packed[row : row + nrows].reshape(-1)[:n].reshape(shp))
        row += nrows
    return out


def kernel(x, w_in, norm_mix, sgu_v_gain, sgu_w_s, sgu_b_s, w_a_out, attn_sink, rel_bias, w_b_out, w_o, norm_ffn, w_gate, w_up, w_down, norm_final, loss_target, m_w_in, m_norm_mix, m_sgu_v_gain, m_sgu_w_s, m_sgu_b_s, m_w_a_out, m_attn_sink, m_rel_bias, m_w_b_out, m_w_o, m_norm_ffn, m_w_gate, m_w_up, m_w_down, m_norm_final, v_w_in, v_norm_mix, v_sgu_v_gain, v_sgu_w_s, v_sgu_b_s, v_w_a_out, v_attn_sink, v_rel_bias, v_w_b_out, v_w_o, v_norm_ffn, v_w_gate, v_w_up, v_w_down, v_norm_final):
    s, d = x.shape[1], x.shape[2]
    w_sgu = sgu_v_gain.shape[1]
    groups = sgu_w_s.shape[1]
    heads = attn_sink.shape[1]
    grp = heads // N_KV_HEADS
    w_att = heads * HEAD_DIM
    w_kv = N_KV_HEADS * HEAD_DIM
    d_ff = w_gate.shape[2] * N_CHIPS
    n_in = w_in.shape[2] * N_CHIPS
    off_q = 2 * w_sgu
    off_k = off_q + w_att
    off_g = off_k + 2 * w_kv
    assert n_in == off_g + 2 * d and groups * BLK == w_sgu and s % BLK == 0

    x2d = x.reshape(s, d)
    tgt = loss_target.reshape(s, d)
    c_idx = lax.axis_index("c").astype(I32).reshape(1)
    q_idx = (2 * lax.axis_index("x") + lax.axis_index("y")).astype(I32).reshape(1)
    qc_idx = jnp.concatenate([q_idx, c_idx])

    W_IN, W_A, W_B, W_O, W_GATE, W_UP, W_DOWN = range(7)
    names = ["w_in", "w_a", "w_b", "w_o", "w_gate", "w_up", "w_down"]
    kinds = ["col", "col", "col", "row", "col", "col", "row"]
    big_w = [w_in[0], w_a_out[0], w_b_out[0], w_o[0], w_gate[0], w_up[0], w_down[0]]
    big_m = [m_w_in[0], m_w_a_out[0], m_w_b_out[0], m_w_o[0], m_w_gate[0], m_w_up[0], m_w_down[0]]
    big_v = [v_w_in[0], v_w_a_out[0], v_w_b_out[0], v_w_o[0], v_w_gate[0], v_w_up[0], v_w_down[0]]
    full_in = _cast_into_full("cast_w_in", q_idx, big_w[W_IN], kinds[W_IN])
    in_send, in_recv, (full_in,), token = _gather_start("gather_start_in", [full_in], [kinds[W_IN]])
    rest = [_cast_into_full("cast_" + names[i], q_idx, big_w[i], kinds[i], after=(token,)) for i in range(1, 7)]
    ag_send, ag_recv, rest, token = _gather_start("gather_start_rest", rest, kinds[1:])
    fulls = [full_in] + rest

    def gathered(tag, ids, after):
        if ids == [W_IN]:
            sems, pos = (in_send, in_recv), [0]
        else:
            sems, pos = (ag_send, ag_recv), [i - 1 for i in ids]
        got = _gather_wait("gather_wait_" + tag, [fulls[i] for i in ids], [kinds[i] for i in ids], pos, *sems, after)
        return _gather_forward("gather_fwd_" + tag, got, [kinds[i] for i in ids])

    ws_b = sgu_w_s[0].astype(BF16)
    wst_b = jnp.swapaxes(sgu_w_s[0], 1, 2).astype(BF16)
    b_col = sgu_b_s[0].reshape(groups, BLK, 1)
    bias_tab, bucket = _band_tables(rel_bias)
    sink = attn_sink[0]

    tm = _tile(s, (1024, 512, 256, 128))

    h1 = _rms_fwd("rms_mix", x2d, norm_mix, after=(token,))
    (g_in,) = gathered("in", [W_IN], h1)

    tn = _tile(n_in, (768, 640, 512))
    z = _mm(
        "mm_z", (s // tm, n_in // tn, 1), [h1, g_in],
        [pl.BlockSpec((tm, d), lambda i, j, k: (i, 0)), pl.BlockSpec((d, tn), lambda i, j, k: (0, j))],
        [jax.ShapeDtypeStruct((s, n_in), BF16)], [pl.BlockSpec((tm, tn), lambda i, j, k: (i, j))],
        [(0, 1, NN, 0)], 1, (tm, tn), 1, lambda ins, vals, outs: _store_cast(outs[0], vals[0]),
        _mm_vmem([((tm, d), BF16, 2), ((d, tn), BF16, 2), ((tm, tn), F32, 3)]),
    )[0]
    g_a, g_b, g_o = gathered("mix", [W_A, W_B, W_O], z)

    a_act = _sgu_fwd(z, sgu_v_gain, ws_b, b_col, w_sgu)

    kv_b = z[:, off_k:off_g]
    k_pad = jnp.pad(kv_b[:, :w_kv], ((BLK, BLK), (0, 0)))
    v_pad = jnp.pad(kv_b[:, w_kv:], ((BLK, BLK), (0, 0)))
    q_blk0 = off_q // (grp * HEAD_DIM)
    att = _attn_fwd(sink, z, k_pad, v_pad, bias_tab, grp, q_blk0)

    tg = _tile(d, (512,))
    ga0, gb0 = off_g // tg, (off_g + d) // tg

    def ep_gate(ins, vals, outs):
        sa, sb = _sigmoid(ins[4][...].astype(F32)), _sigmoid(ins[5][...].astype(F32))
        outs[0][...] = (sa * vals[0] + sb * vals[1]).astype(BF16)
        outs[1][...] = vals[0].astype(BF16)
        outs[2][...] = vals[1].astype(BF16)

    t_out = pl.BlockSpec((tm, tg), lambda i, j, k: (i, j))
    m_act, y_a, y_b = _mm(
        "mm_branches", (s // tm, d // tg, 1), [a_act, g_a, att, g_b, z, z],
        [pl.BlockSpec((tm, w_sgu), lambda i, j, k: (i, 0)), pl.BlockSpec((w_sgu, tg), lambda i, j, k: (0, j)),
         pl.BlockSpec((tm, w_att), lambda i, j, k: (i, 0)), pl.BlockSpec((w_att, tg), lambda i, j, k: (0, j)),
         pl.BlockSpec((tm, tg), lambda i, j, k: (i, ga0 + j)), pl.BlockSpec((tm, tg), lambda i, j, k: (i, gb0 + j))],
        [jax.ShapeDtypeStruct((s, d), BF16)] * 3,
        [t_out, t_out, t_out], [(0, 1, NN, 0), (2, 3, NN, 1)], 2, (tm, tg), 1, ep_gate,
        _mm_vmem([((tm, w_sgu), BF16, 4), ((w_sgu, tg), BF16, 4), ((tm, tg), F32, 12)]),
    )

    tn = _tile(d, (1024, 512))

    def ep_residual(ins, vals, outs):
        outs[0][...] = ins[2][...] + vals[0]

    x2 = _mm(
        "mm_wo", (s // tm, d // tn, 1), [m_act, g_o, x2d],
        [pl.BlockSpec((tm, d), lambda i, j, k: (i, 0)), pl.BlockSpec((d, tn), lambda i, j, k: (0, j)),
         pl.BlockSpec((tm, tn), lambda i, j, k: (i, j))],
        [jax.ShapeDtypeStruct((s, d), F32)], [pl.BlockSpec((tm, tn), lambda i, j, k: (i, j))],
        [(0, 1, NN, 0)], 1, (tm, tn), 1, ep_residual,
        _mm_vmem([((tm, d), BF16, 2), ((d, tn), BF16, 2), ((tm, tn), F32, 5)]),
    )[0]

    h2 = _rms_fwd("rms_ffn", x2, norm_ffn)
    g_gate, g_up = gathered("ffn_in", [W_GATE, W_UP], h2)

    tf = _tile(d_ff, (512,))

    def ep_swiglu(ins, vals, outs):
        gt, up = vals
        outs[0][...] = gt.astype(BF16)
        outs[1][...] = up.astype(BF16)
        outs[2][...] = ((gt * _sigmoid(gt)) * up).astype(BF16)

    f_out = pl.BlockSpec((tm, tf), lambda i, j, k: (i, j))
    gt, up, f_act = _mm(
        "mm_gate_up", (s // tm, d_ff // tf, 1), [h2, g_gate, g_up],
        [pl.BlockSpec((tm, d), lambda i, j, k: (i, 0)), pl.BlockSpec((d, tf), lambda i, j, k: (0, j)),
         pl.BlockSpec((d, tf), lambda i, j, k: (0, j))],
        [jax.ShapeDtypeStruct((s, d_ff), BF16)] * 3,
        [f_out, f_out, f_out], [(0, 1, NN, 0), (0, 2, NN, 1)], 2, (tm, tf), 1, ep_swiglu,
        _mm_vmem([((tm, d), BF16, 2), ((d, tf), BF16, 4), ((tm, tf), F32, 8)]),
    )
    (g_down,) = gathered("ffn_out", [W_DOWN], f_act)

    tkf = _tile(d_ff, (1408, 1024, 512))
    nkf = d_ff // tkf
    x3 = _mm(
        "mm_down", (s // tm, d // tn, nkf), [f_act, g_down, x2],
        [pl.BlockSpec((tm, tkf), lambda i, j, k: (i, k)), pl.BlockSpec((tkf, tn), lambda i, j, k: (k, j)),
         pl.BlockSpec((tm, tn), lambda i, j, k: (i, j))],
        [jax.ShapeDtypeStruct((s, d), F32)], [pl.BlockSpec((tm, tn), lambda i, j, k: (i, j))],
        [(0, 1, NN, 0)], 1, (tm, tn), nkf, ep_residual,
        _mm_vmem([((tm, tkf), BF16, 2), ((tkf, tn), BF16, 2), ((tm, tn), F32, 6)]),
    )[0]

    dx3, dx3b, dg_final, loss_part = _head(x3, norm_final.reshape(1, d), tgt)

    def reduce_a(tag, ids, grads):
        ks = [kinds[i] for i in ids]
        lands = [lax.empty((g.shape[0] // 2, g.shape[1]) if k == "col" else (g.shape[0], g.shape[1] // 2), BF16)
                 for g, k in zip(grads, ks)]
        send, recv, bufs = _split_start("pair_send_" + tag, list(grads) + lands, len(ids), _pair_exchange_copies(ks))
        return {"tag": tag, "ids": ids, "ks": ks, "pair": (send, recv, bufs), "token": bufs[0]}

    def reduce_b(st, after):
        tag, ids, ks = st["tag"], st["ids"], st["ks"]
        send, recv, bufs = st["pair"]
        bufs = _split_wait("pair_wait_" + tag, bufs, send, recv, _pair_exchange_copies(ks), after)
        grads, from_sib = bufs[: len(ids)], bufs[len(ids) :]
        halves = [_pair_add("pair_add_" + names[i], c_idx, g, r, k) for i, g, r, k in zip(ids, grads, from_sib, ks)]
        st["chip"] = _chip_send_start("chip_send_" + tag, halves, ks)
        st["token"] = st["chip"][2][0]

    def reduce_c(st, after):
        tag, ids, ks = st["tag"], st["ids"], st["ks"]
        send, recv, halves, lands = st["chip"]
        halves, lands = _chip_send_wait("chip_wait_" + tag, halves, lands, ks, send, recv, after)
        pieces = [_chip_sum("chip_sum_" + names[i], qc_idx, h, r, k) for i, h, r, k in zip(ids, halves, lands, ks)]
        st["share"] = _split_start("share_send_" + tag, pieces, len(ids), _pair_share_copies(ks, False))
        st["token"] = st["share"][2][0]

    def reduce_d(st, after):
        send, recv, bufs = st["share"]
        return _split_wait("share_wait_" + st["tag"], bufs, send, recv, _pair_share_copies(st["ks"], True), after)

    def ep_swiglu_bwd(ins, vals, outs):
        df = vals[0]
        gtv, upv = ins[2][...].astype(F32), ins[3][...].astype(F32)
        sg = _sigmoid(gtv)
        outs[0][...] = (df * upv * (sg + gtv * sg * (1.0 - sg))).astype(BF16)
        outs[1][...] = (df * (gtv * sg)).astype(BF16)

    dgt, dup = _mm(
        "mm_dswiglu", (s // tm, d_ff // tf, 1), [dx3b, g_down, gt, up],
        [pl.BlockSpec((tm, d), lambda i, j, k: (i, 0)), pl.BlockSpec((tf, d), lambda i, j, k: (j, 0)), f_out, f_out],
        [jax.ShapeDtypeStruct((s, d_ff), BF16), jax.ShapeDtypeStruct((s, d_ff), BF16)], [f_out, f_out],
        [(0, 1, NT, 0)], 1, (tm, tf), 1, ep_swiglu_bwd,
        _mm_vmem([((tm, d), BF16, 2), ((tf, d), BF16, 2), ((tm, tf), F32, 8)]),
    )

    def ep_bf16(ins, vals, outs):
        for o, v in zip(outs, vals):
            o[...] = v.astype(BF16)

    def ep_f32(ins, vals, outs):
        for o, v in zip(outs, vals):
            o[...] = v

    twn = _tile(d, (1024, 512))
    gw_down = _mm(
        "mm_gw_down", (d_ff // tkf, d // twn, 1), [f_act, dx3b],
        [pl.BlockSpec((s, tkf), lambda i, j, k: (0, i)), pl.BlockSpec((s, twn), lambda i, j, k: (0, j))],
        [jax.ShapeDtypeStruct((d_ff, d), BF16)], [pl.BlockSpec((tkf, twn), lambda i, j, k: (i, j))],
        [(0, 1, TN, 0)], 1, (tkf, twn), 1, ep_bf16,
        _mm_vmem([((s, tkf), BF16, 3), ((s, twn), BF16, 2), ((tkf, twn), F32, 3)]),
    )[0]
    red_down = reduce_a("down", [W_DOWN], [gw_down])

    dh2 = _mm(
        "mm_dh2", (s // tm, d // tn, nkf), [dgt, g_gate, dup, g_up],
        [pl.BlockSpec((tm, tkf), lambda i, j, k: (i, k)), pl.BlockSpec((tn, tkf), lambda i, j, k: (j, k)),
         pl.BlockSpec((tm, tkf), lambda i, j, k: (i, k)), pl.BlockSpec((tn, tkf), lambda i, j, k: (j, k))],
        [jax.ShapeDtypeStruct((s, d), F32)], [pl.BlockSpec((tm, tn), lambda i, j, k: (i, j))],
        [(0, 1, NT, 0), (2, 3, NT, 0)], 1, (tm, tn), nkf, ep_f32,
        _mm_vmem([((tm, tkf), BF16, 4), ((tn, tkf), BF16, 4), ((tm, tn), F32, 5)]),
        after=(red_down["token"],),
    )[0]
    reduce_b(red_down, dh2)

    twr = _tile(d, (1024, 512))
    w_tile = pl.BlockSpec((twr, tf), lambda i, j, k: (i, j))
    gw_gate, gw_up = _mm(
        "mm_gw_gate_up", (d // twr, d_ff // tf, 1), [h2, dgt, dup],
        [pl.BlockSpec((s, twr), lambda i, j, k: (0, i)), pl.BlockSpec((s, tf), lambda i, j, k: (0, j)),
         pl.BlockSpec((s, tf), lambda i, j, k: (0, j))],
        [jax.ShapeDtypeStruct((d, d_ff), BF16), jax.ShapeDtypeStruct((d, d_ff), BF16)], [w_tile, w_tile],
        [(0, 1, TN, 0), (0, 2, TN, 1)], 2, (twr, tf), 1, ep_bf16,
        _mm_vmem([((s, twr), BF16, 3), ((s, tf), BF16, 4), ((twr, tf), F32, 6)]),
        after=(red_down["token"],),
    )
    red_ffn = reduce_a("ffn_in", [W_GATE, W_UP], [gw_gate, gw_up])

    dx2, dx2b, dg_ffn = _rms_bwd("rms_ffn_bwd", x2, norm_ffn, dh2, dx3, after=(red_ffn["token"],))

    nj = d // tg

    def lo(j):
        return jnp.minimum(j, nj - 1)

    def gate_bwd_body(dx_ref, wo_ref, ga_ref, gb_ref, ya_ref, yb_ref, dya_ref, dyb_ref, dz_ref, keep):
        j = pl.program_id(1)

        @pl.when(j < nj)
        def _():
            dm = lax.dot_general(dx_ref[...], wo_ref[...], NT, preferred_element_type=F32)
            sa, sb = _sigmoid(ga_ref[...].astype(F32)), _sigmoid(gb_ref[...].astype(F32))
            dya_ref[...] = (dm * sa).astype(BF16)
            dyb_ref[...] = (dm * sb).astype(BF16)
            dz_ref[...] = (dm * ya_ref[...].astype(F32) * (sa * (1.0 - sa))).astype(BF16)
            keep[lo(j)] = (dm * yb_ref[...].astype(F32) * (sb * (1.0 - sb))).astype(BF16)

        @pl.when(j >= nj)
        def _():
            dz_ref[...] = keep[jnp.maximum(j - nj, 0)]

    t_lo = pl.BlockSpec((tm, tg), lambda i, j: (i, lo(j)))
    dya, dyb, dz = pl.pallas_call(
        gate_bwd_body,
        name="mm_dgate",
        grid=(s // tm, 2 * nj),
        in_specs=[
            pl.BlockSpec((tm, d), lambda i, j: (i, 0)),
            pl.BlockSpec((tg, d), lambda i, j: (lo(j), 0)),
            pl.BlockSpec((tm, tg), lambda i, j: (i, ga0 + lo(j))),
            pl.BlockSpec((tm, tg), lambda i, j: (i, gb0 + lo(j))),
            t_lo,
            t_lo,
        ],
        out_specs=[t_lo, t_lo, pl.BlockSpec((tm, tg), lambda i, j: (i, ga0 + j))],
        out_shape=[jax.ShapeDtypeStruct((s, d), BF16), jax.ShapeDtypeStruct((s, d), BF16), jax.ShapeDtypeStruct((s, n_in), BF16)],
        scratch_shapes=[pltpu.VMEM((nj, tm, tg), BF16)],
        compiler_params=_params(_mm_vmem([((tm, d), BF16, 2), ((tg, d), BF16, 2), ((tm, tg), F32, 14), ((nj, tm, tg), BF16, 1)])),
    )(dx2b, g_o, z, z, y_a, y_b)
    reduce_b(red_ffn, dya)

    gw_o = _mm(
        "mm_gw_o", (d // twr, d // twn, 1), [m_act, dx2b],
        [pl.BlockSpec((s, twr), lambda i, j, k: (0, i)), pl.BlockSpec((s, twn), lambda i, j, k: (0, j))],
        [jax.ShapeDtypeStruct((d, d), BF16)], [pl.BlockSpec((twr, twn), lambda i, j, k: (i, j))],
        [(0, 1, TN, 0)], 1, (twr, twn), 1, ep_bf16,
        _mm_vmem([((s, twr), BF16, 3), ((s, twn), BF16, 2), ((twr, twn), F32, 3)]),
        after=(red_ffn["token"],),
    )[0]
    red_o = reduce_a("w_o", [W_O], [gw_o])

    tb = _tile(w_sgu, (1024, 512))
    b_out = pl.BlockSpec((tm, tb), lambda i, j, k: (i, j))

    def ep_branch_bwd(ins, vals, outs):
        outs[0][...] = vals[0].astype(BF16)
        outs[1][...] = vals[1].astype(BF16)

    da, datt = _mm(
        "mm_dbranches", (s // tm, w_sgu // tb, 1), [dya, g_a, dyb, g_b],
        [pl.BlockSpec((tm, d), lambda i, j, k: (i, 0)), pl.BlockSpec((tb, d), lambda i, j, k: (j, 0)),
         pl.BlockSpec((tm, d), lambda i, j, k: (i, 0)), pl.BlockSpec((tb, d), lambda i, j, k: (j, 0))],
        [jax.ShapeDtypeStruct((s, w_sgu), BF16), jax.ShapeDtypeStruct((s, w_att), BF16)], [b_out, b_out],
        [(0, 1, NT, 0), (2, 3, NT, 1)], 2, (tm, tb), 1, ep_branch_bwd,
        _mm_vmem([((tm, d), BF16, 4), ((tb, d), BF16, 4), ((tm, tb), F32, 6)]),
        after=(red_o["token"],),
    )
    reduce_b(red_o, da)

    wb_tile = pl.BlockSpec((tb, twn), lambda i, j, k: (i, j))
    gw_a, gw_b = _mm(
        "mm_gw_branches", (w_sgu // tb, d // twn, 1), [a_act, dya, att, dyb],
        [pl.BlockSpec((s, tb), lambda i, j, k: (0, i)), pl.BlockSpec((s, twn), lambda i, j, k: (0, j)),
         pl.BlockSpec((s, tb), lambda i, j, k: (0, i)), pl.BlockSpec((s, twn), lambda i, j, k: (0, j))],
        [jax.ShapeDtypeStruct((w_sgu, d), BF16), jax.ShapeDtypeStruct((w_att, d), BF16)], [wb_tile, wb_tile],
        [(0, 1, TN, 0), (2, 3, TN, 1)], 2, (tb, twn), 1, ep_bf16,
        _mm_vmem([((s, tb), BF16, 5), ((s, twn), BF16, 4), ((tb, twn), F32, 6)]),
        after=(red_o["token"],),
    )
    red_mix = reduce_a("mix", [W_A, W_B], [gw_a, gw_b])

    dz, dws, dbs, dgain = _sgu_bwd(z, da, sgu_v_gain, ws_b, wst_b, b_col, w_sgu, dz, after=(red_mix["token"],))
    dz, dk_pad, dv_pad, dbias_tab, dsink = _attn_bwd(sink, z, k_pad, v_pad, bias_tab, datt, dz, grp, q_blk0)
    dz = _dkv_to_dz(dk_pad, dv_pad, dz, off_k // (2 * w_kv))
    drel = _relbias_bwd(dbias_tab, bucket)
    reduce_b(red_mix, dz)

    tzn = _tile(n_in, (768, 640, 512))
    gw_in = _mm(
        "mm_gw_in", (d // twr, n_in // tzn, 1), [h1, dz],
        [pl.BlockSpec((s, twr), lambda i, j, k: (0, i)), pl.BlockSpec((s, tzn), lambda i, j, k: (0, j))],
        [jax.ShapeDtypeStruct((d, n_in), BF16)], [pl.BlockSpec((twr, tzn), lambda i, j, k: (i, j))],
        [(0, 1, TN, 0)], 1, (twr, tzn), 1, ep_bf16,
        _mm_vmem([((s, twr), BF16, 3), ((s, tzn), BF16, 2), ((twr, tzn), F32, 3)]),
        after=(red_mix["token"],),
    )[0]
    red_in = reduce_a("w_in", [W_IN], [gw_in])

    tkz = _tile(n_in, (1920, 1536, 1280, 1024))
    nkz = n_in // tkz
    dh1 = _mm(
        "mm_dh1", (s // tm, d // tn, nkz), [dz, g_in],
        [pl.BlockSpec((tm, tkz), lambda i, j, k: (i, k)), pl.BlockSpec((tn, tkz), lambda i, j, k: (j, k))],
        [jax.ShapeDtypeStruct((s, d), F32)], [pl.BlockSpec((tm, tn), lambda i, j, k: (i, j))],
        [(0, 1, NT, 0)], 1, (tm, tn), nkz, ep_f32,
        _mm_vmem([((tm, tkz), BF16, 2), ((tn, tkz), BF16, 2), ((tm, tn), F32, 5)]),
        after=(red_in["token"],),
    )[0]

    grad_x, _, dg_mix = _rms_bwd("rms_mix_bwd", x2d, norm_mix, dh1, dx2)

    small_w = [norm_mix, sgu_v_gain, sgu_w_s, sgu_b_s, attn_sink, rel_bias, norm_ffn, norm_final]
    small_m = [m_norm_mix, m_sgu_v_gain, m_sgu_w_s, m_sgu_b_s, m_attn_sink, m_rel_bias, m_norm_ffn, m_norm_final]
    small_v = [v_norm_mix, v_sgu_v_gain, v_sgu_w_s, v_sgu_b_s, v_attn_sink, v_rel_bias, v_norm_ffn, v_norm_final]
    small_shapes = [w.shape for w in small_w]
    local_small = [dg_mix, dgain, dws, dbs, dsink[:, 0], drel[:, :REL_BUCKETS].T, dg_ffn, dg_final]
    local_small = [g.reshape(shp) for g, shp in zip(local_small, small_shapes)]
    packed_g = _small_all_reduce(_pack(local_small + [loss_part[0, :1]]))
    g_small = _unpack(packed_g, small_shapes + [(1,)])
    loss = g_small[-1].reshape(())
    g_small = g_small[:-1]
    zero1 = jnp.zeros((1,), F32)
    pw, pg, pm, pv = _pack(small_w + [zero1]), _pack(g_small + [zero1]), _pack(small_m + [zero1]), _pack(small_v + [zero1])
    small_upd = _adamw("adamw_small", pw, pg, pm, pv)
    d_small, nm_small, nv_small = [_unpack(a, small_shapes) for a in small_upd]

    reduce_b(red_in, small_upd[0])
    grads_big, upd = [None] * 7, [None] * 7

    def finish(st, after):
        for i, g in zip(st["ids"], reduce_d(st, after)):
            grads_big[i] = g
            upd[i] = _adamw("adamw_" + names[i], big_w[i], g, big_m[i], big_v[i])
            after = upd[i][0]
        return after

    after, prev = red_in["token"], None
    for st in (red_down, red_ffn, red_o, red_mix, red_in):
        reduce_c(st, after)
        after = st["token"] if prev is None else finish(prev, st["token"])
        prev = st
    finish(prev, after)

    small_names = ["norm_mix", "sgu_v_gain", "sgu_w_s", "sgu_b_s", "attn_sink", "rel_bias", "norm_ffn", "norm_final"]
    table = {}
    for i, n in enumerate(names):
        table[n] = (grads_big[i][None], upd[i][0][None], upd[i][1][None], upd[i][2][None])
    for i, n in enumerate(small_names):
        table[n] = (g_small[i], d_small[i], nm_small[i], nv_small[i])
    order = ["w_in", "norm_mix", "sgu_v_gain", "sgu_w_s", "sgu_b_s", "w_a", "attn_sink", "rel_bias", "w_b", "w_o", "norm_ffn",
             "w_gate", "w_up", "w_down", "norm_final"]
    outs = [loss, grad_x.reshape(1, s, d)]
    for part in range(4):
        outs += [table[n][part] for n in order]
    return tuple(outs)
```

```python
import math

import jax
import jax.numpy as jnp
import numpy as np
from jax import lax
from jax.experimental import pallas as pl
from jax.experimental.pallas import tpu as pltpu

F32 = jnp.float32
BF16 = jnp.bfloat16
I32 = jnp.int32
MESH = pl.DeviceIdType.MESH

EPS = 1e-6
NEG = -1e30
BLK = 128
HEAD_DIM = 128
N_KV_HEADS = 2
REL_BUCKETS = 32
REL_MAX_DIST = 128
N_CHIPS = 4
ADAM_LR, ADAM_B1, ADAM_B2, ADAM_EPS, ADAM_WD, ADAM_STEP = 0.001, 0.9, 0.999, 1e-08, 0.01, 10

LANES = 128
VMEM_CAP = 60 * 1024 * 1024

NN = (((1,), (0,)), ((), ()))
NT = (((1,), (1,)), ((), ()))
TN = (((0,), (0,)), ((), ()))
ANY = pl.BlockSpec(memory_space=pl.ANY)
HBM_SPEC = pl.BlockSpec(memory_space=pltpu.HBM)
SEM_SPEC = pl.BlockSpec(memory_space=pltpu.SEMAPHORE)
EFFECT = pltpu.SideEffectType.DATAFLOW_SIDE_EFFECTING


def _tile(n, cands):
    for t in cands:
        if n % t == 0:
            return t
    return n


def _params(vmem_bytes=None, **kw):
    if vmem_bytes is not None:
        kw["vmem_limit_bytes"] = int(min(max(vmem_bytes, 32 * 1024 * 1024), VMEM_CAP))
    return pltpu.CompilerParams(**kw)


def _nbytes(shape, dtype):
    return int(np.prod(shape)) * jnp.dtype(dtype).itemsize


def _sigmoid(x):
    return 1.0 / (1.0 + jnp.exp(-x))


_GC = 0.7978845608028654
_GA = 0.044715


def _gelu(x):
    return 0.5 * x * (1.0 + jnp.tanh(_GC * (x + _GA * (x * x * x))))


def _gelu_grad(x):
    t = jnp.tanh(_GC * (x + _GA * (x * x * x)))
    return 0.5 * (1.0 + t) + 0.5 * x * (1.0 - t * t) * (_GC * (1.0 + 3.0 * _GA * (x * x)))


def _bf(v):
    return v if v.dtype == BF16 else v.astype(BF16)


def _mm(name, grid, ins, in_specs, out_shape, out_specs, pairs, n_acc, acc_tile, nk, epilogue, vmem_bytes, after=()):
    n_in, n_out = len(ins) + len(after), len(out_shape)

    def body(*refs):
        in_refs, out_refs, accs = refs[:n_in], refs[n_in : n_in + n_out], refs[n_in + n_out :]

        def products():
            vals = [None] * n_acc
            for a_i, b_i, dn, acc_i in pairs:
                d = lax.dot_general(_bf(in_refs[a_i][...]), _bf(in_refs[b_i][...]), dn, preferred_element_type=F32)
                vals[acc_i] = d if vals[acc_i] is None else vals[acc_i] + d
            return vals

        if nk == 1:
            epilogue(in_refs, products(), out_refs)
            return
        k = pl.program_id(2)
        vals = products()

        @pl.when(k == 0)
        def _():
            for a, v in zip(accs, vals):
                a[...] = v

        @pl.when(k > 0)
        def _():
            for a, v in zip(accs, vals):
                a[...] += v

        @pl.when(k == nk - 1)
        def _():
            epilogue(in_refs, [a[...] for a in accs], out_refs)

    scratch = [pltpu.VMEM(acc_tile, F32) for _ in range(n_acc)] if nk > 1 else []
    return pl.pallas_call(
        body,
        name=name,
        grid=grid,
        in_specs=list(in_specs) + [ANY] * len(after),
        out_specs=out_specs,
        out_shape=out_shape,
        scratch_shapes=scratch,
        compiler_params=_params(vmem_bytes),
    )(*ins, *after)


def _mm_vmem(tiles):
    return sum(_nbytes(s, d) * c for s, d, c in tiles) + 4 * 1024 * 1024


def _store_cast(ref, v):
    ref[...] = v.astype(ref.dtype)


def _rows8(v):
    r, d = v.shape
    return v.reshape(r // 8, 8, d).sum(axis=0)


def _rms_fwd(name, x, g, after=()):
    s, d = x.shape
    tm = _tile(s, (256, 128))

    def body(x_ref, g_ref, *rest):
        h_ref = rest[-1]
        xv = x_ref[...]
        r = lax.rsqrt(jnp.mean(xv * xv, axis=-1, keepdims=True) + EPS)
        h_ref[...] = ((xv * r) * g_ref[...]).astype(BF16)

    return pl.pallas_call(
        body,
        name=name,
        grid=(s // tm,),
        in_specs=[pl.BlockSpec((tm, d), lambda i: (i, 0)), pl.BlockSpec((1, d), lambda i: (0, 0))] + [ANY] * len(after),
        out_specs=pl.BlockSpec((tm, d), lambda i: (i, 0)),
        out_shape=jax.ShapeDtypeStruct((s, d), BF16),
    )(x, g, *after)


def _rms_bwd(name, x, g, dh, dres, after=()):
    s, d = x.shape
    tm = _tile(s, (256, 128))
    n = s // tm
    n_after = len(after)

    def body(x_ref, g_ref, dh_ref, dres_ref, *rest):
        dx_ref, dxb_ref, dg_ref, acc_ref = rest[n_after:]
        i = pl.program_id(0)
        xv = x_ref[...]
        r = lax.rsqrt(jnp.mean(xv * xv, axis=-1, keepdims=True) + EPS)
        xh = xv * r
        dhv = dh_ref[...]
        dxh = dhv * g_ref[...]
        dx = r * (dxh - xh * jnp.mean(dxh * xh, axis=-1, keepdims=True)) + dres_ref[...]
        dx_ref[...] = dx
        dxb_ref[...] = dx.astype(BF16)
        part = _rows8(dhv * xh)

        @pl.when(i == 0)
        def _():
            acc_ref[...] = part

        @pl.when(i > 0)
        def _():
            acc_ref[...] += part

        @pl.when(i == n - 1)
        def _():
            dg_ref[...] = jnp.sum(acc_ref[...], axis=0, keepdims=True)

    row = pl.BlockSpec((tm, d), lambda i: (i, 0))
    vec = pl.BlockSpec((1, d), lambda i: (0, 0))
    return pl.pallas_call(
        body,
        name=name,
        grid=(n,),
        in_specs=[row, vec, row, row] + [ANY] * n_after,
        out_specs=[row, row, vec],
        out_shape=[jax.ShapeDtypeStruct((s, d), F32), jax.ShapeDtypeStruct((s, d), BF16), jax.ShapeDtypeStruct((1, d), F32)],
        scratch_shapes=[pltpu.VMEM((8, d), F32)],
    )(x, g, dh, dres, *after)


def _head(x3, g, target):
    s, d = x3.shape
    tm = _tile(s, (256, 128))
    n = s // tm

    def body(x_ref, g_ref, t_ref, dx_ref, dxb_ref, dg_ref, loss_ref, acc_g, acc_l):
        i = pl.program_id(0)
        xv = x_ref[...]
        gv = g_ref[...]
        r = lax.rsqrt(jnp.mean(xv * xv, axis=-1, keepdims=True) + EPS)
        xh = xv * r
        e = xh * gv - t_ref[...]
        dy = e * (1.0 / d)
        dxh = dy * gv
        dx = r * (dxh - xh * jnp.mean(dxh * xh, axis=-1, keepdims=True))
        dx_ref[...] = dx
        dxb_ref[...] = dx.astype(BF16)
        pg = _rows8(dy * xh)
        plo = _rows8(e * e)

        @pl.when(i == 0)
        def _():
            acc_g[...] = pg
            acc_l[...] = plo

        @pl.when(i > 0)
        def _():
            acc_g[...] += pg
            acc_l[...] += plo

        @pl.when(i == n - 1)
        def _():
            dg_ref[...] = jnp.sum(acc_g[...], axis=0, keepdims=True)
            loss_ref[...] = jnp.full((1, LANES), (0.5 / d) * jnp.sum(acc_l[...]), F32)

    row = pl.BlockSpec((tm, d), lambda i: (i, 0))
    vec = pl.BlockSpec((1, d), lambda i: (0, 0))
    return pl.pallas_call(
        body,
        name="head",
        grid=(n,),
        in_specs=[row, vec, row],
        out_specs=[row, row, vec, pl.BlockSpec((1, LANES), lambda i: (0, 0))],
        out_shape=[
            jax.ShapeDtypeStruct((s, d), F32),
            jax.ShapeDtypeStruct((s, d), BF16),
            jax.ShapeDtypeStruct((1, d), F32),
            jax.ShapeDtypeStruct((1, LANES), F32),
        ],
        scratch_shapes=[pltpu.VMEM((8, d), F32), pltpu.VMEM((8, d), F32)],
    )(x3, g, target)


def _sgu_fwd(z, gain, ws_b, b_col, w_sgu):
    s = z.shape[0]
    groups = ws_b.shape[0]

    def body(zu_ref, zv_ref, gain_ref, ws_ref, b_ref, a_ref):
        vv = _gelu(zv_ref[...].astype(F32))
        r = lax.rsqrt(jnp.mean(vv * vv, axis=-1, keepdims=True) + EPS)
        vn = ((vv * r) * gain_ref[...]).astype(BF16)
        u = _gelu(zu_ref[...].astype(F32))
        for g in range(groups):
            sl = slice(g * BLK, (g + 1) * BLK)
            mixed = jnp.dot(ws_ref[g], vn[:, sl], preferred_element_type=F32) + b_ref[g]
            a_ref[:, sl] = (u[:, sl] * mixed).astype(BF16)

    return pl.pallas_call(
        body,
        name="sgu_fwd",
        grid=(s // BLK,),
        in_specs=[
            pl.BlockSpec((BLK, w_sgu), lambda c: (c, 0)),
            pl.BlockSpec((BLK, w_sgu), lambda c: (c, 1)),
            pl.BlockSpec((1, w_sgu), lambda c: (0, 0)),
            pl.BlockSpec((groups, BLK, BLK), lambda c: (0, 0, 0)),
            pl.BlockSpec((groups, BLK, 1), lambda c: (0, 0, 0)),
        ],
        out_specs=pl.BlockSpec((BLK, w_sgu), lambda c: (c, 0)),
        out_shape=jax.ShapeDtypeStruct((s, w_sgu), BF16),
    )(z, z, gain, ws_b, b_col)


def _sgu_bwd(z, da, gain, ws_b, wst_b, b_col, w_sgu, dz, after=()):
    s = z.shape[0]
    groups = ws_b.shape[0]
    n = s // BLK
    n_skip = 1 + len(after)

    def body(zu_ref, zv_ref, da_ref, gain_ref, ws_ref, wst_ref, b_ref, *rest):
        dz_ref, dws_ref, dbs_ref, dgain_ref, acc_gain = rest[n_skip:]
        c = pl.program_id(0)
        zu = zu_ref[...].astype(F32)
        zv = zv_ref[...].astype(F32)
        gain_v = gain_ref[...]
        vv = _gelu(zv)
        r = lax.rsqrt(jnp.mean(vv * vv, axis=-1, keepdims=True) + EPS)
        xh = vv * r
        vn = (xh * gain_v).astype(BF16)
        u = _gelu(zu)
        dav = da_ref[...].astype(F32)
        dmix = dav * u
        dmix_b = dmix.astype(BF16)
        dvn_parts = []
        for g in range(groups):
            sl = slice(g * BLK, (g + 1) * BLK)
            mixed = jnp.dot(ws_ref[g], vn[:, sl], preferred_element_type=F32) + b_ref[g]
            dz_ref[:, sl] = (dav[:, sl] * mixed * _gelu_grad(zu[:, sl])).astype(BF16)
            dvn_parts.append(jnp.dot(wst_ref[g], dmix_b[:, sl], preferred_element_type=F32))
            dws_g = lax.dot_general(dmix_b[:, sl], vn[:, sl], NT, preferred_element_type=F32)
            dbs_g = jnp.sum(dmix[:, sl], axis=1, keepdims=True)

            @pl.when(c == 0)
            def _():
                dws_ref[g] = dws_g
                dbs_ref[g] = dbs_g

            @pl.when(c > 0)
            def _():
                dws_ref[g] += dws_g
                dbs_ref[g] += dbs_g

        dvn = jnp.concatenate(dvn_parts, axis=1)
        dxh = dvn * gain_v
        dvv = r * (dxh - xh * jnp.mean(dxh * xh, axis=-1, keepdims=True))
        dz_ref[:, w_sgu:] = (dvv * _gelu_grad(zv)).astype(BF16)
        pg = _rows8(dvn * xh)

        @pl.when(c == 0)
        def _():
            acc_gain[...] = pg

        @pl.when(c > 0)
        def _():
            acc_gain[...] += pg

        @pl.when(c == n - 1)
        def _():
            dgain_ref[...] = jnp.sum(acc_gain[...], axis=0, keepdims=True)

    full3 = pl.BlockSpec((groups, BLK, BLK), lambda c: (0, 0, 0))
    col3 = pl.BlockSpec((groups, BLK, 1), lambda c: (0, 0, 0))
    vec = pl.BlockSpec((1, w_sgu), lambda c: (0, 0))
    return pl.pallas_call(
        body,
        name="sgu_bwd",
        grid=(n,),
        in_specs=[
            pl.BlockSpec((BLK, w_sgu), lambda c: (c, 0)),
            pl.BlockSpec((BLK, w_sgu), lambda c: (c, 1)),
            pl.BlockSpec((BLK, w_sgu), lambda c: (c, 0)),
            vec,
            full3,
            full3,
            col3,
            ANY,
        ]
        + [ANY] * len(after),
        out_specs=[pl.BlockSpec((BLK, 2 * w_sgu), lambda c: (c, 0)), full3, col3, vec],
        out_shape=[
            jax.ShapeDtypeStruct(dz.shape, BF16),
            jax.ShapeDtypeStruct((groups, BLK, BLK), F32),
            jax.ShapeDtypeStruct((groups, BLK, 1), F32),
            jax.ShapeDtypeStruct((1, w_sgu), F32),
        ],
        scratch_shapes=[pltpu.VMEM((8, w_sgu), F32)],
        input_output_aliases={7: 0},
    )(z, z, da, gain, ws_b, wst_b, b_col, dz, *after)


def _attn_softmax(sink_ref, q_ref, k_ref, v_ref, bias_ref, s_len, grp):
    kv = pl.program_id(0)
    n = pl.program_id(1)
    start = pl.multiple_of(n * BLK, BLK)
    kb = k_ref[pl.ds(start, 3 * BLK), :]
    vb = v_ref[pl.ds(start, 3 * BLK), :]
    qv = q_ref[...]
    qs = jnp.concatenate([qv[:, g * HEAD_DIM : (g + 1) * HEAD_DIM] for g in range(grp)], axis=0).astype(BF16)
    sc = lax.dot_general(qs, kb, NT, preferred_element_type=F32) * (HEAD_DIM**-0.5)
    sc = sc + bias_ref[...].reshape(grp * BLK, 3 * BLK)
    kpos = start + lax.broadcasted_iota(I32, (1, 3 * BLK), 1) - BLK
    sc = jnp.where((kpos >= 0) & (kpos < s_len), sc, NEG)
    sink = jnp.concatenate([jnp.full((BLK, 1), sink_ref[kv * grp + g], F32) for g in range(grp)], axis=0)
    m = jnp.maximum(jnp.max(sc, axis=-1, keepdims=True), sink)
    p = jnp.exp(sc - m)
    esink = jnp.exp(sink - m)
    den = jnp.sum(p, axis=-1, keepdims=True) + esink
    return start, qs, kb, vb, p / den, esink / den


def _attn_specs(s, grp, q_blk0):
    qw = grp * HEAD_DIM
    return [
        pl.BlockSpec(memory_space=pltpu.SMEM),
        pl.BlockSpec((BLK, qw), lambda kv, n: (n, q_blk0 + kv)),
        pl.BlockSpec((s + 2 * BLK, HEAD_DIM), lambda kv, n: (0, kv)),
        pl.BlockSpec((s + 2 * BLK, HEAD_DIM), lambda kv, n: (0, kv)),
        pl.BlockSpec((grp, BLK, 3 * BLK), lambda kv, n: (kv, 0, 0)),
    ]


def _attn_fwd(sink, z, k_pad, v_pad, bias_tab, grp, q_blk0):
    s = z.shape[0]
    qw = grp * HEAD_DIM

    def body(sink_ref, q_ref, k_ref, v_ref, bias_ref, o_ref):
        _, _, _, vb, pn, _ = _attn_softmax(sink_ref, q_ref, k_ref, v_ref, bias_ref, s, grp)
        o = jnp.dot(pn.astype(BF16), vb, preferred_element_type=F32)
        for g in range(grp):
            o_ref[:, g * HEAD_DIM : (g + 1) * HEAD_DIM] = o[g * BLK : (g + 1) * BLK].astype(BF16)

    return pl.pallas_call(
        body,
        name="attn_fwd",
        grid=(N_KV_HEADS, s // BLK),
        in_specs=_attn_specs(s, grp, q_blk0),
        out_specs=pl.BlockSpec((BLK, qw), lambda kv, n: (n, kv)),
        out_shape=jax.ShapeDtypeStruct((s, N_KV_HEADS * qw), BF16),
    )(sink, z, k_pad, v_pad, bias_tab)


def _attn_bwd(sink, z, k_pad, v_pad, bias_tab, dout, dz, grp, q_blk0):
    s = z.shape[0]
    qw = grp * HEAD_DIM
    nb = s // BLK
    heads = N_KV_HEADS * grp

    def body(sink_ref, q_ref, k_ref, v_ref, bias_ref, do_ref, dz_in, dq_ref, dk_ref, dv_ref, dbias_ref, dsink_ref, dk_acc, dv_acc):
        del dz_in
        kv = pl.program_id(0)
        n = pl.program_id(1)
        start, qs, kb, vb, pn, psink = _attn_softmax(sink_ref, q_ref, k_ref, v_ref, bias_ref, s, grp)
        dov = do_ref[...]
        dos = jnp.concatenate([dov[:, g * HEAD_DIM : (g + 1) * HEAD_DIM] for g in range(grp)], axis=0)
        dp = lax.dot_general(dos, vb, NT, preferred_element_type=F32)
        dvb = lax.dot_general(pn.astype(BF16), dos, TN, preferred_element_type=F32)
        delta = jnp.sum(pn * dp, axis=-1, keepdims=True)
        ds = pn * (dp - delta)
        dsb = (ds * (HEAD_DIM**-0.5)).astype(BF16)
        dq = jnp.dot(dsb, kb, preferred_element_type=F32)
        dkb = lax.dot_general(dsb, qs, TN, preferred_element_type=F32)
        for g in range(grp):
            dq_ref[:, g * HEAD_DIM : (g + 1) * HEAD_DIM] = dq[g * BLK : (g + 1) * BLK].astype(BF16)

        @pl.when(n == 0)
        def _():
            dk_acc[...] = jnp.zeros_like(dk_acc)
            dv_acc[...] = jnp.zeros_like(dv_acc)
            dbias_ref[...] = jnp.zeros_like(dbias_ref)

        @pl.when((n == 0) & (kv == 0))
        def _():
            dsink_ref[...] = jnp.zeros_like(dsink_ref)

        dk_acc[pl.ds(start, 3 * BLK), :] += dkb
        dv_acc[pl.ds(start, 3 * BLK), :] += dvb
        dbias_ref[...] += ds.reshape(grp, BLK, 3 * BLK)
        row = lax.broadcasted_iota(I32, (heads, LANES), 0)
        sd = psink * delta
        upd = jnp.zeros((heads, LANES), F32)
        for g in range(grp):
            upd = jnp.where(row == kv * grp + g, -jnp.sum(sd[g * BLK : (g + 1) * BLK]), upd)
        dsink_ref[...] += upd

        @pl.when(n == nb - 1)
        def _():
            dk_ref[...] = dk_acc[...]
            dv_ref[...] = dv_acc[...]

    pad_spec = pl.BlockSpec((s + 2 * BLK, HEAD_DIM), lambda kv, n: (0, kv))
    kvw = N_KV_HEADS * HEAD_DIM
    return pl.pallas_call(
        body,
        name="attn_bwd",
        grid=(N_KV_HEADS, nb),
        in_specs=_attn_specs(s, grp, q_blk0) + [pl.BlockSpec((BLK, qw), lambda kv, n: (n, kv)), ANY],
        out_specs=[
            pl.BlockSpec((BLK, qw), lambda kv, n: (n, q_blk0 + kv)),
            pad_spec,
            pad_spec,
            pl.BlockSpec((grp, BLK, 3 * BLK), lambda kv, n: (kv, 0, 0)),
            pl.BlockSpec((heads, LANES), lambda kv, n: (0, 0)),
        ],
        out_shape=[
            jax.ShapeDtypeStruct(dz.shape, BF16),
            jax.ShapeDtypeStruct((s + 2 * BLK, kvw), F32),
            jax.ShapeDtypeStruct((s + 2 * BLK, kvw), F32),
            jax.ShapeDtypeStruct((heads, BLK, 3 * BLK), F32),
            jax.ShapeDtypeStruct((heads, LANES), F32),
        ],
        scratch_shapes=[pltpu.VMEM((s + 2 * BLK, HEAD_DIM), F32), pltpu.VMEM((s + 2 * BLK, HEAD_DIM), F32)],
        input_output_aliases={6: 0},
    )(sink, z, k_pad, v_pad, bias_tab, dout, dz)


def _dkv_to_dz(dk_pad, dv_pad, dz, blk_idx):
    s = dz.shape[0]
    kvw = dk_pad.shape[1]

    def body(dk_ref, dv_ref, dz_in, out_ref):
        del dz_in
        out_ref[:, :kvw] = dk_ref[...].astype(BF16)
        out_ref[:, kvw:] = dv_ref[...].astype(BF16)

    src = pl.BlockSpec((BLK, kvw), lambda i: (i + 1, 0))
    return pl.pallas_call(
        body,
        name="dkv_to_dz",
        grid=(s // BLK,),
        in_specs=[src, src, ANY],
        out_specs=pl.BlockSpec((BLK, 2 * kvw), lambda i: (i, blk_idx)),
        out_shape=jax.ShapeDtypeStruct(dz.shape, BF16),
        input_output_aliases={2: 0},
    )(dk_pad, dv_pad, dz)


def _relbias_bwd(dbias_tab, bucket):
    heads = dbias_tab.shape[0]

    def body(dt_ref, bk_ref, out_ref):
        lane = lax.broadcasted_iota(I32, (1, LANES), 1)
        bk = bk_ref[...]
        rows = []
        for h in range(heads):
            dt = dt_ref[h]
            acc = jnp.zeros((1, LANES), F32)
            for b in range(REL_BUCKETS):
                acc = jnp.where(lane == b, jnp.sum(jnp.where(bk == b, dt, 0.0)), acc)
            rows.append(acc)
        out_ref[...] = jnp.concatenate(rows, axis=0)

    return pl.pallas_call(body, name="relbias_bwd", out_shape=jax.ShapeDtypeStruct((heads, LANES), F32))(dbias_tab, bucket)


def _t5_bucket(rel):
    nb = REL_BUCKETS // 2
    ret = jnp.where(rel > 0, nb, 0)
    n = jnp.abs(rel)
    max_exact = nb // 2
    nf = jnp.maximum(n, 1).astype(F32)
    large = max_exact + (jnp.log(nf / max_exact) / math.log(REL_MAX_DIST / max_exact) * (nb - max_exact)).astype(I32)
    large = jnp.minimum(large, nb - 1)
    return ret + jnp.where(n < max_exact, n, large)


def _band_tables(rel_bias):
    qi = jnp.arange(BLK)[:, None]
    kj = jnp.arange(3 * BLK)[None, :]
    rel = kj - BLK - qi
    bucket = _t5_bucket(rel).astype(I32)
    heads = rel_bias.shape[1]
    masked = jnp.where(jnp.abs(rel) <= BLK, bucket, -1)

    def body(rb_ref, bk_ref, out_ref):
        bk = bk_ref[...]
        for h in range(heads):
            tab = jnp.full(bk.shape, NEG, F32)
            for b in range(REL_BUCKETS):
                tab = jnp.where(bk == b, rb_ref[b, h], tab)
            out_ref[h] = tab

    bias_tab = pl.pallas_call(
        body,
        name="bias_table",
        in_specs=[pl.BlockSpec(memory_space=pltpu.SMEM), pl.BlockSpec(memory_space=pltpu.VMEM)],
        out_specs=pl.BlockSpec(memory_space=pltpu.VMEM),
        out_shape=jax.ShapeDtypeStruct((heads, BLK, 3 * BLK), F32),
    )(rel_bias.astype(F32), masked)
    return bias_tab, bucket


EW_BLOCK_ELEMS = 512 * 1024


def _ew_tiles(shape, elems=EW_BLOCK_ELEMS // 2):
    r, c = shape
    tn = c if c <= 2048 else _tile(c, (2048, 1920, 1536, 1408, 1024, 512))
    tm = _tile(r, [t for t in (1024, 512, 256, 128, 64, 32, 16, 8) if t * tn <= elems] or [8])
    return tm, tn


def _cast_into_full(name, qidx, w, kind, after=()):
    r, c = w.shape
    tm, tn = _ew_tiles(w.shape, EW_BLOCK_ELEMS)
    nbi, nbj = r // tm, c // tn
    if kind == "col":
        full, out_spec = (r, c * N_CHIPS), pl.BlockSpec((tm, tn), lambda i, j, q: (i, q[0] * nbj + j))
    else:
        full, out_spec = (r * N_CHIPS, c), pl.BlockSpec((tm, tn), lambda i, j, q: (q[0] * nbi + i, j))

    def body(q_ref, w_ref, *rest):
        del q_ref
        rest[-1][...] = w_ref[...].astype(BF16)

    return pl.pallas_call(
        body,
        name=name,
        grid_spec=pltpu.PrefetchScalarGridSpec(
            num_scalar_prefetch=1,
            grid=(nbi, nbj),
            in_specs=[pl.BlockSpec((tm, tn), lambda i, j, q: (i, j))] + [ANY] * len(after),
            out_specs=out_spec,
        ),
        out_shape=jax.ShapeDtypeStruct(full, BF16),
    )(qidx, w, *after)


def _adamw(name, w, g, m, v, after=()):
    tm, tn = _ew_tiles(w.shape)
    if _nbytes(w.shape, F32) <= 1024 * 1024:
        tm, tn = w.shape
    spec = pl.BlockSpec((tm, tn), lambda i, j: (i, j))
    n_after = len(after)

    def body(w_ref, g_ref, m_ref, v_ref, *rest):
        d_ref, nm_ref, nv_ref, g_out_ref = rest[n_after:]
        gv = g_ref[...]
        g_out_ref[...] = gv
        nm = ADAM_B1 * m_ref[...] + (1.0 - ADAM_B1) * gv
        nv = ADAM_B2 * v_ref[...] + (1.0 - ADAM_B2) * (gv * gv)
        m_hat = nm / (1.0 - ADAM_B1**ADAM_STEP)
        v_hat = nv / (1.0 - ADAM_B2**ADAM_STEP)
        d_ref[...] = -ADAM_LR * (m_hat / (jnp.sqrt(v_hat) + ADAM_EPS) + ADAM_WD * w_ref[...])
        nm_ref[...] = nm
        nv_ref[...] = nv

    out = jax.ShapeDtypeStruct(w.shape, F32)
    return pl.pallas_call(
        body, name=name, grid=(w.shape[0] // tm, w.shape[1] // tn), in_specs=[spec] * 4 + [ANY] * n_after,
        out_specs=[spec] * 4, out_shape=[out, out, out, out],
    )(w, g, m, v, *after)


def _pair_add(name, cidx, g_full, r_sib, kind):
    hr, hc = r_sib.shape
    tm, tn = _ew_tiles((hr, hc), EW_BLOCK_ELEMS)
    nbi, nbj = hr // tm, hc // tn
    if kind == "col":
        g_spec = pl.BlockSpec((tm, tn), lambda i, j, c: (c[0] * nbi + i, j))
    else:
        g_spec = pl.BlockSpec((tm, tn), lambda i, j, c: (i, c[0] * nbj + j))
    spec = pl.BlockSpec((tm, tn), lambda i, j, c: (i, j))

    def body(c_ref, g_ref, r_ref, o_ref):
        del c_ref
        o_ref[...] = (g_ref[...].astype(F32) + r_ref[...].astype(F32)).astype(BF16)

    return pl.pallas_call(
        body,
        name=name,
        grid_spec=pltpu.PrefetchScalarGridSpec(num_scalar_prefetch=1, grid=(nbi, nbj), in_specs=[g_spec, spec], out_specs=spec),
        out_shape=jax.ShapeDtypeStruct((hr, hc), BF16),
    )(cidx, g_full, r_sib)


def _chip_sum(name, qidx, c_half, r_ici, kind):
    _, pr, pc = r_ici.shape
    tm, tn = _ew_tiles((pr, pc), EW_BLOCK_ELEMS)
    nbi, nbj = pr // tm, pc // tn
    if kind == "col":
        own_spec = pl.BlockSpec((tm, tn), lambda i, j, q: (i, q[0] * nbj + j))
        full, out_spec = (2 * pr, pc), pl.BlockSpec((tm, tn), lambda i, j, q: (q[1] * nbi + i, j))
    else:
        own_spec = pl.BlockSpec((tm, tn), lambda i, j, q: (q[0] * nbi + i, j))
        full, out_spec = (pr, 2 * pc), pl.BlockSpec((tm, tn), lambda i, j, q: (i, q[1] * nbj + j))

    def body(q_ref, own_ref, r_ref, o_ref):
        q = q_ref[0]
        own = own_ref[...].astype(F32)
        recv = [r_ref[r].astype(F32) for r in range(3)]
        total = None
        for chip in range(N_CHIPS):
            d = chip ^ q
            term = jnp.where(d == 0, own, jnp.where(d == 2, recv[0], jnp.where(d == 1, recv[1], recv[2])))
            total = term if total is None else total + term
        o_ref[...] = total

    return pl.pallas_call(
        body,
        name=name,
        grid_spec=pltpu.PrefetchScalarGridSpec(
            num_scalar_prefetch=1,
            grid=(nbi, nbj),
            in_specs=[own_spec, pl.BlockSpec((3, tm, tn), lambda i, j, q: (0, i, j))],
            out_specs=out_spec,
        ),
        out_shape=jax.ShapeDtypeStruct(full, F32),
    )(qidx, c_half, r_ici)


_REL_MASK = (2, 1, 3)


def _place():
    x, y, c = lax.axis_index("x"), lax.axis_index("y"), lax.axis_index("c")
    chips = [(1 - x, y), (x, 1 - y), (1 - x, 1 - y)]
    return x, y, c, 2 * x + y, chips


def _shard_view(ref, kind, chip):
    if kind == "col":
        w = ref.shape[1] // N_CHIPS
        return ref.at[:, pl.ds(pl.multiple_of(chip * w, LANES), w)]
    h = ref.shape[0] // N_CHIPS
    return ref.at[pl.ds(pl.multiple_of(chip * h, 16), h), :]


def _row_half(ref, half):
    h = ref.shape[0] // 2
    return ref.at[pl.ds(pl.multiple_of(half * h, 16), h), :]


def _pair_half(ref, kind, half):
    if kind == "col":
        return _row_half(ref, half)
    w = ref.shape[1] // 2
    return ref.at[:, pl.ds(pl.multiple_of(half * w, LANES), w)]


def _remote(src, dst, send_sem, recv_sem, dev):
    return pltpu.make_async_remote_copy(src_ref=src, dst_ref=dst, send_sem=send_sem, recv_sem=recv_sem, device_id=dev, device_id_type=MESH)


def _hbm(a):
    return pltpu.with_memory_space_constraint(a, pltpu.HBM)


def _gather_start(name, fulls, kinds):
    n_w = len(fulls)

    def body(*refs):
        g = refs[:n_w]
        send_sem, recv_sem = refs[n_w], refs[n_w + 1]
        token = refs[-1]
        _, _, c, q, chips = _place()
        for w in range(n_w):
            mine = _row_half(_shard_view(g[w], kinds[w], q), c)
            for r, chip in enumerate(chips):
                _remote(mine, mine, send_sem.at[3 * w + r], recv_sem.at[3 * w + r], (*chip, c)).start()
        token[...] = jnp.zeros_like(token)

    res = pl.pallas_call(
        body,
        name=name,
        out_shape=(
            pltpu.SemaphoreType.DMA((3 * n_w,)),
            pltpu.SemaphoreType.DMA((3 * n_w,)),
            *[pltpu.HBM(f.shape, f.dtype) for f in fulls],
            jax.ShapeDtypeStruct((8, LANES), F32),
        ),
        in_specs=[HBM_SPEC] * n_w,
        out_specs=(SEM_SPEC, SEM_SPEC, *[HBM_SPEC] * n_w, pl.BlockSpec(memory_space=pltpu.VMEM)),
        input_output_aliases={w: w + 2 for w in range(n_w)},
        compiler_params=pltpu.CompilerParams(has_side_effects=EFFECT),
    )(*[_hbm(f) for f in fulls])
    return res[0], res[1], list(res[2 : 2 + n_w]), res[-1]


def _gather_wait(name, fulls, kinds, w_ids, send_sem, recv_sem, after):
    n = len(fulls)

    def body(*refs):
        g = refs[:n]
        s_sem, r_sem = refs[n], refs[n + 1]
        x, y, c, q, _ = _place()
        for i, w in enumerate(w_ids):
            mine = _row_half(_shard_view(g[i], kinds[i], q), c)
            for r in range(3):
                landed = _row_half(_shard_view(g[i], kinds[i], q ^ _REL_MASK[r]), c)
                cp = _remote(mine, landed, s_sem.at[3 * w + r], r_sem.at[3 * w + r], (x, y, 1 - c))
                cp.wait_send()
                cp.wait_recv()

    res = pl.pallas_call(
        body,
        name=name,
        out_shape=[pltpu.HBM(f.shape, f.dtype) for f in fulls],
        in_specs=[HBM_SPEC] * n + [SEM_SPEC, SEM_SPEC, ANY],
        out_specs=[HBM_SPEC] * n,
        input_output_aliases={i: i for i in range(n)},
        compiler_params=pltpu.CompilerParams(has_side_effects=EFFECT),
    )(*fulls, send_sem, recv_sem, after)
    return list(res)


def _gather_forward(name, fulls, kinds):
    n = len(fulls)

    def body(*refs):
        g = refs[n : 2 * n]
        send, recv = refs[2 * n :]
        x, y, c, q, _ = _place()
        sib = (x, y, 1 - c)
        cps = []
        for i in range(n):
            for r in range(3):
                landed = _row_half(_shard_view(g[i], kinds[i], q ^ _REL_MASK[r]), c)
                cps.append(_remote(landed, landed, send.at[i, r], recv.at[i, r], sib))
        for cp in cps:
            cp.start()
        for i in range(n):
            for r in range(3):
                other = _row_half(_shard_view(g[i], kinds[i], q ^ _REL_MASK[r]), 1 - c)
                _remote(other, other, send.at[i, r], recv.at[i, r], sib).wait_recv()
        for cp in cps:
            cp.wait_send()

    res = pl.pallas_call(
        body,
        name=name,
        in_specs=[ANY] * n,
        out_specs=[ANY] * n,
        out_shape=[jax.ShapeDtypeStruct(f.shape, f.dtype) for f in fulls],
        scratch_shapes=[pltpu.SemaphoreType.DMA((n, 3)), pltpu.SemaphoreType.DMA((n, 3))],
        input_output_aliases={i: i for i in range(n)},
    )(*fulls)
    return list(res)


def _split_start(name, bufs, n_sems, copies):
    n = len(bufs)

    def body(*refs):
        for cp in copies(refs[:n], refs[n], refs[n + 1]):
            cp.start()

    res = pl.pallas_call(
        body,
        name=name,
        out_shape=(
            pltpu.SemaphoreType.DMA((n_sems,)),
            pltpu.SemaphoreType.DMA((n_sems,)),
            *[pltpu.HBM(b.shape, b.dtype) for b in bufs],
        ),
        in_specs=[HBM_SPEC] * n,
        out_specs=(SEM_SPEC, SEM_SPEC, *[HBM_SPEC] * n),
        input_output_aliases={i: i + 2 for i in range(n)},
        compiler_params=pltpu.CompilerParams(has_side_effects=EFFECT),
    )(*[_hbm(b) for b in bufs])
    return res[0], res[1], list(res[2:])


def _split_wait(name, bufs, send_sem, recv_sem, copies, after):
    n = len(bufs)

    def body(*refs):
        for cp in copies(refs[:n], refs[n], refs[n + 1]):
            cp.wait_send()
            cp.wait_recv()

    res = pl.pallas_call(
        body,
        name=name,
        out_shape=[pltpu.HBM(b.shape, b.dtype) for b in bufs],
        in_specs=[HBM_SPEC] * n + [SEM_SPEC, SEM_SPEC, ANY],
        out_specs=[HBM_SPEC] * n,
        input_output_aliases={i: i for i in range(n)},
        compiler_params=pltpu.CompilerParams(has_side_effects=EFFECT),
    )(*bufs, send_sem, recv_sem, after)
    return list(res)


def _pair_exchange_copies(kinds):
    n = len(kinds)

    def copies(refs, send_sem, recv_sem):
        x, y, c, _, _ = _place()
        return [
            _remote(_pair_half(refs[w], kinds[w], 1 - c), refs[n + w], send_sem.at[w], recv_sem.at[w], (x, y, 1 - c))
            for w in range(n)
        ]

    return copies


def _pair_share_copies(kinds, waiting):
    def copies(refs, send_sem, recv_sem):
        x, y, c, _, _ = _place()
        out = []
        for w, kind in enumerate(kinds):
            mine = _pair_half(refs[w], kind, c)
            dst = _pair_half(refs[w], kind, 1 - c) if waiting else mine
            out.append(_remote(mine, dst, send_sem.at[w], recv_sem.at[w], (x, y, 1 - c)))
        return out

    return copies


def _piece_shape(half_shape, kind):
    r, c = half_shape
    return (3, r, c // N_CHIPS) if kind == "col" else (3, r // N_CHIPS, c)


def _chip_send_start(name, halves, kinds):
    n = len(halves)
    lands = [lax.empty(_piece_shape(h.shape, k), BF16) for h, k in zip(halves, kinds)]

    def body(*refs):
        h, land = refs[:n], refs[n : 2 * n]
        send_sem, recv_sem = refs[2 * n], refs[2 * n + 1]
        _, _, c, q, chips = _place()
        for i in range(n):
            for r, chip in enumerate(chips):
                piece = _shard_view(h[i], kinds[i], q ^ _REL_MASK[r])
                _remote(piece, land[i].at[r], send_sem.at[3 * i + r], recv_sem.at[3 * i + r], (*chip, c)).start()

    res = pl.pallas_call(
        body,
        name=name,
        out_shape=(
            pltpu.SemaphoreType.DMA((3 * n,)),
            pltpu.SemaphoreType.DMA((3 * n,)),
            *[pltpu.HBM(a.shape, a.dtype) for a in halves],
            *[pltpu.HBM(a.shape, a.dtype) for a in lands],
        ),
        in_specs=[HBM_SPEC] * (2 * n),
        out_specs=(SEM_SPEC, SEM_SPEC, *[HBM_SPEC] * (2 * n)),
        input_output_aliases={i: i + 2 for i in range(2 * n)},
        compiler_params=pltpu.CompilerParams(has_side_effects=EFFECT),
    )(*[_hbm(a) for a in halves], *[_hbm(a) for a in lands])
    return res[0], res[1], list(res[2 : 2 + n]), list(res[2 + n :])


def _chip_send_wait(name, halves, lands, kinds, send_sem, recv_sem, after):
    n = len(halves)

    def body(*refs):
        h, land = refs[:n], refs[n : 2 * n]
        s_sem, r_sem = refs[2 * n], refs[2 * n + 1]
        x, y, c, q, _ = _place()
        for i in range(n):
            for r in range(3):
                piece = _shard_view(h[i], kinds[i], q ^ _REL_MASK[r])
                cp = _remote(piece, land[i].at[r], s_sem.at[3 * i + r], r_sem.at[3 * i + r], (x, y, 1 - c))
                cp.wait_send()
                cp.wait_recv()

    res = pl.pallas_call(
        body,
        name=name,
        out_shape=[pltpu.HBM(a.shape, a.dtype) for a in halves] + [pltpu.HBM(a.shape, a.dtype) for a in lands],
        in_specs=[HBM_SPEC] * (2 * n) + [SEM_SPEC, SEM_SPEC, ANY],
        out_specs=[HBM_SPEC] * (2 * n),
        input_output_aliases={i: i for i in range(2 * n)},
        compiler_params=pltpu.CompilerParams(has_side_effects=EFFECT),
    )(*halves, *lands, send_sem, recv_sem, after)
    return list(res[:n]), list(res[n:])


def _small_all_reduce(p):
    rows = p.shape[0]
    n_dev = 2 * N_CHIPS

    def body(p_ref, o_ref, buf, loc_sem, send, recv):
        x, y, c, q, _ = _place()
        me = 2 * q + c
        own = pltpu.make_async_copy(p_ref, buf.at[me], loc_sem)
        own.start()
        cps = []
        for d in range(1, n_dev):
            dev = (x ^ ((d >> 2) & 1), y ^ ((d >> 1) & 1), c ^ (d & 1))
            cps.append(_remote(p_ref, buf.at[me], send.at[d - 1], recv.at[d - 1], dev))
        for cp in cps:
            cp.start()
        for d in range(1, n_dev):
            slot = buf.at[me ^ d]
            _remote(slot, slot, send.at[d - 1], recv.at[d - 1], (x, y, c)).wait_recv()
        own.wait()
        total = buf[0]
        for d in range(1, n_dev):
            total = total + buf[d]
        o_ref[...] = total
        for cp in cps:
            cp.wait_send()

    return pl.pallas_call(
        body,
        name="small_all_reduce",
        in_specs=[ANY],
        out_specs=pl.BlockSpec(memory_space=pltpu.VMEM),
        out_shape=jax.ShapeDtypeStruct(p.shape, F32),
        scratch_shapes=[
            pltpu.VMEM((n_dev, rows, LANES), F32),
            pltpu.SemaphoreType.DMA,
            pltpu.SemaphoreType.DMA((n_dev - 1,)),
            pltpu.SemaphoreType.DMA((n_dev - 1,)),
        ],
    )(p)


def _small_exchange_copies(waiting):
    def copies(refs, send_sem, recv_sem):
        p, land = refs
        x, y, c, q, _ = _place()
        me = 2 * q + c
        out = []
        for dd in range(1, 2 * N_CHIPS):
            dev = (x ^ ((dd >> 2) & 1), y ^ ((dd >> 1) & 1), c ^ (dd & 1))
            dst = land.at[me ^ dd] if waiting else land.at[me]
            out.append(_remote(p, dst, send_sem.at[dd - 1], recv_sem.at[dd - 1], dev))
        return out

    return copies


def _small_sum(me_idx, p, land):
    rows = p.shape[0]
    n_dev = 2 * N_CHIPS

    def body(me_ref, p_ref, land_ref, o_ref):
        me = me_ref[0]
        total = None
        for dev in range(n_dev):
            term = jnp.where(me == dev, p_ref[...], land_ref[dev])
            total = term if total is None else total + term
        o_ref[...] = total

    return pl.pallas_call(
        body,
        name="small_sum",
        grid_spec=pltpu.PrefetchScalarGridSpec(
            num_scalar_prefetch=1,
            grid=(1,),
            in_specs=[pl.BlockSpec((rows, LANES), lambda i, m: (0, 0)), pl.BlockSpec((n_dev, rows, LANES), lambda i, m: (0, 0, 0))],
            out_specs=pl.BlockSpec((rows, LANES), lambda i, m: (0, 0)),
        ),
        out_shape=jax.ShapeDtypeStruct(p.shape, F32),
    )(me_idx, p, land)


def _pack(parts):
    rows = []
    for a in parts:
        flat = a.reshape(-1).astype(F32)
        n = flat.shape[0]
        padded = -(-n // (8 * LANES)) * (8 * LANES)
        rows.append(jnp.pad(flat, (0, padded - n)).reshape(-1, LANES))
    return jnp.concatenate(rows, axis=0)


def _unpack(packed, shapes):
    out, row = [], 0
    for shp in shapes:
        n = int(np.prod(shp))
        nrows = -(-n // (8 * LANES)) * 8
        out.append(packed[row : row + nrows].reshape(-1)[:n].reshape(shp))
        row += nrows
    return out


def kernel(x, w_in, norm_mix, sgu_v_gain, sgu_w_s, sgu_b_s, w_a_out, attn_sink, rel_bias, w_b_out, w_o, norm_ffn, w_gate, w_up, w_down, norm_final, loss_target, m_w_in, m_norm_mix, m_sgu_v_gain, m_sgu_w_s, m_sgu_b_s, m_w_a_out, m_attn_sink, m_rel_bias, m_w_b_out, m_w_o, m_norm_ffn, m_w_gate, m_w_up, m_w_down, m_norm_final, v_w_in, v_norm_mix, v_sgu_v_gain, v_sgu_w_s, v_sgu_b_s, v_w_a_out, v_attn_sink, v_rel_bias, v_w_b_out, v_w_o, v_norm_ffn, v_w_gate, v_w_up, v_w_down, v_norm_final):
    s, d = x.shape[1], x.shape[2]
    w_sgu = sgu_v_gain.shape[1]
    groups = sgu_w_s.shape[1]
    heads = attn_sink.shape[1]
    grp = heads // N_KV_HEADS
    w_att = heads * HEAD_DIM
    w_kv = N_KV_HEADS * HEAD_DIM
    d_ff = w_gate.shape[2] * N_CHIPS
    n_in = w_in.shape[2] * N_CHIPS
    off_q = 2 * w_sgu
    off_k = off_q + w_att
    off_g = off_k + 2 * w_kv
    assert n_in == off_g + 2 * d and groups * BLK == w_sgu and s % BLK == 0

    x2d = x.reshape(s, d)
    tgt = loss_target.reshape(s, d)
    c_idx = lax.axis_index("c").astype(I32).reshape(1)
    q_idx = (2 * lax.axis_index("x") + lax.axis_index("y")).astype(I32).reshape(1)
    qc_idx = jnp.concatenate([q_idx, c_idx])

    W_IN, W_A, W_B, W_O, W_GATE, W_UP, W_DOWN = range(7)
    names = ["w_in", "w_a", "w_b", "w_o", "w_gate", "w_up", "w_down"]
    kinds = ["col", "col", "col", "row", "col", "col", "row"]
    big_w = [w_in[0], w_a_out[0], w_b_out[0], w_o[0], w_gate[0], w_up[0], w_down[0]]
    big_m = [m_w_in[0], m_w_a_out[0], m_w_b_out[0], m_w_o[0], m_w_gate[0], m_w_up[0], m_w_down[0]]
    big_v = [v_w_in[0], v_w_a_out[0], v_w_b_out[0], v_w_o[0], v_w_gate[0], v_w_up[0], v_w_down[0]]
    full_in = _cast_into_full("cast_w_in", q_idx, big_w[W_IN], kinds[W_IN])
    in_send, in_recv, (full_in,), token = _gather_start("gather_start_in", [full_in], [kinds[W_IN]])
    rest = [_cast_into_full("cast_" + names[i], q_idx, big_w[i], kinds[i], after=(token,)) for i in range(1, 7)]
    ag_send, ag_recv, rest, token = _gather_start("gather_start_rest", rest, kinds[1:])
    fulls = [full_in] + rest

    def gathered(tag, ids, after):
        if ids == [W_IN]:
            sems, pos = (in_send, in_recv), [0]
        else:
            sems, pos = (ag_send, ag_recv), [i - 1 for i in ids]
        got = _gather_wait("gather_wait_" + tag, [fulls[i] for i in ids], [kinds[i] for i in ids], pos, *sems, after)
        return _gather_forward("gather_fwd_" + tag, got, [kinds[i] for i in ids])

    ws_b = sgu_w_s[0].astype(BF16)
    wst_b = jnp.swapaxes(sgu_w_s[0], 1, 2).astype(BF16)
    b_col = sgu_b_s[0].reshape(groups, BLK, 1)
    bias_tab, bucket = _band_tables(rel_bias)
    sink = attn_sink[0]

    tm = _tile(s, (1024, 512, 256, 128))

    h1 = _rms_fwd("rms_mix", x2d, norm_mix, after=(token,))
    (g_in,) = gathered("in", [W_IN], h1)

    tn = _tile(n_in, (768, 640, 512))
    z = _mm(
        "mm_z", (s // tm, n_in // tn, 1), [h1, g_in],
        [pl.BlockSpec((tm, d), lambda i, j, k: (i, 0)), pl.BlockSpec((d, tn), lambda i, j, k: (0, j))],
        [jax.ShapeDtypeStruct((s, n_in), BF16)], [pl.BlockSpec((tm, tn), lambda i, j, k: (i, j))],
        [(0, 1, NN, 0)], 1, (tm, tn), 1, lambda ins, vals, outs: _store_cast(outs[0], vals[0]),
        _mm_vmem([((tm, d), BF16, 2), ((d, tn), BF16, 2), ((tm, tn), F32, 3)]),
    )[0]
    g_a, g_b, g_o = gathered("mix", [W_A, W_B, W_O], z)

    a_act = _sgu_fwd(z, sgu_v_gain, ws_b, b_col, w_sgu)

    kv_b = z[:, off_k:off_g]
    k_pad = jnp.pad(kv_b[:, :w_kv], ((BLK, BLK), (0, 0)))
    v_pad = jnp.pad(kv_b[:, w_kv:], ((BLK, BLK), (0, 0)))
    q_blk0 = off_q // (grp * HEAD_DIM)
    att = _attn_fwd(sink, z, k_pad, v_pad, bias_tab, grp, q_blk0)

    tg = _tile(d, (512,))
    ga0, gb0 = off_g // tg, (off_g + d) // tg

    def ep_gate(ins, vals, outs):
        sa, sb = _sigmoid(ins[4][...].astype(F32)), _sigmoid(ins[5][...].astype(F32))
        outs[0][...] = (sa * vals[0] + sb * vals[1]).astype(BF16)
        outs[1][...] = vals[0].astype(BF16)
        outs[2][...] = vals[1].astype(BF16)

    t_out = pl.BlockSpec((tm, tg), lambda i, j, k: (i, j))
    m_act, y_a, y_b = _mm(
        "mm_branches", (s // tm, d // tg, 1), [a_act, g_a, att, g_b, z, z],
        [pl.BlockSpec((tm, w_sgu), lambda i, j, k: (i, 0)), pl.BlockSpec((w_sgu, tg), lambda i, j, k: (0, j)),
         pl.BlockSpec((tm, w_att), lambda i, j, k: (i, 0)), pl.BlockSpec((w_att, tg), lambda i, j, k: (0, j)),
         pl.BlockSpec((tm, tg), lambda i, j, k: (i, ga0 + j)), pl.BlockSpec((tm, tg), lambda i, j, k: (i, gb0 + j))],
        [jax.ShapeDtypeStruct((s, d), BF16)] * 3,
        [t_out, t_out, t_out], [(0, 1, NN, 0), (2, 3, NN, 1)], 2, (tm, tg), 1, ep_gate,
        _mm_vmem([((tm, w_sgu), BF16, 4), ((w_sgu, tg), BF16, 4), ((tm, tg), F32, 12)]),
    )

    tn = _tile(d, (1024, 512))

    def ep_residual(ins, vals, outs):
        outs[0][...] = ins[2][...] + vals[0]

    x2 = _mm(
        "mm_wo", (s // tm, d // tn, 1), [m_act, g_o, x2d],
        [pl.BlockSpec((tm, d), lambda i, j, k: (i, 0)), pl.BlockSpec((d, tn), lambda i, j, k: (0, j)),
         pl.BlockSpec((tm, tn), lambda i, j, k: (i, j))],
        [jax.ShapeDtypeStruct((s, d), F32)], [pl.BlockSpec((tm, tn), lambda i, j, k: (i, j))],
        [(0, 1, NN, 0)], 1, (tm, tn), 1, ep_residual,
        _mm_vmem([((tm, d), BF16, 2), ((d, tn), BF16, 2), ((tm, tn), F32, 5)]),
    )[0]

    h2 = _rms_fwd("rms_ffn", x2, norm_ffn)
    g_gate, g_up = gathered("ffn_in", [W_GATE, W_UP], h2)

    tf = _tile(d_ff, (512,))

    def ep_swiglu(ins, vals, outs):
        gt, up = vals
        outs[0][...] = gt.astype(BF16)
        outs[1][...] = up.astype(BF16)
        outs[2][...] = ((gt * _sigmoid(gt)) * up).astype(BF16)

    f_out = pl.BlockSpec((tm, tf), lambda i, j, k: (i, j))
    gt, up, f_act = _mm(
        "mm_gate_up", (s // tm, d_ff // tf, 1), [h2, g_gate, g_up],
        [pl.BlockSpec((tm, d), lambda i, j, k: (i, 0)), pl.BlockSpec((d, tf), lambda i, j, k: (0, j)),
         pl.BlockSpec((d, tf), lambda i, j, k: (0, j))],
        [jax.ShapeDtypeStruct((s, d_ff), BF16)] * 3,
        [f_out, f_out, f_out], [(0, 1, NN, 0), (0, 2, NN, 1)], 2, (tm, tf), 1, ep_swiglu,
        _mm_vmem([((tm, d), BF16, 2), ((d, tf), BF16, 4), ((tm, tf), F32, 8)]),
    )
    (g_down,) = gathered("ffn_out", [W_DOWN], f_act)

    tkf = _tile(d_ff, (1408, 1024, 512))
    nkf = d_ff // tkf
    x3 = _mm(
        "mm_down", (s // tm, d // tn, nkf), [f_act, g_down, x2],
        [pl.BlockSpec((tm, tkf), lambda i, j, k: (i, k)), pl.BlockSpec((tkf, tn), lambda i, j, k: (k, j)),
         pl.BlockSpec((tm, tn), lambda i, j, k: (i, j))],
        [jax.ShapeDtypeStruct((s, d), F32)], [pl.BlockSpec((tm, tn), lambda i, j, k: (i, j))],
        [(0, 1, NN, 0)], 1, (tm, tn), nkf, ep_residual,
        _mm_vmem([((tm, tkf), BF16, 2), ((tkf, tn), BF16, 2), ((tm, tn), F32, 6)]),
    )[0]

    dx3, dx3b, dg_final, loss_part = _head(x3, norm_final.reshape(1, d), tgt)

    def reduce_a(tag, ids, grads):
        ks = [kinds[i] for i in ids]
        lands = [lax.empty((g.shape[0] // 2, g.shape[1]) if k == "col" else (g.shape[0], g.shape[1] // 2), BF16)
                 for g, k in zip(grads, ks)]
        send, recv, bufs = _split_start("pair_send_" + tag, list(grads) + lands, len(ids), _pair_exchange_copies(ks))
        return {"tag": tag, "ids": ids, "ks": ks, "pair": (send, recv, bufs), "token": bufs[0]}

    def reduce_b(st, after):
        tag, ids, ks = st["tag"], st["ids"], st["ks"]
        send, recv, bufs = st["pair"]
        bufs = _split_wait("pair_wait_" + tag, bufs, send, recv, _pair_exchange_copies(ks), after)
        grads, from_sib = bufs[: len(ids)], bufs[len(ids) :]
        halves = [_pair_add("pair_add_" + names[i], c_idx, g, r, k) for i, g, r, k in zip(ids, grads, from_sib, ks)]
        st["chip"] = _chip_send_start("chip_send_" + tag, halves, ks)
        st["token"] = st["chip"][2][0]

    def reduce_c(st, after):
        tag, ids, ks = st["tag"], st["ids"], st["ks"]
        send, recv, halves, lands = st["chip"]
        halves, lands = _chip_send_wait("chip_wait_" + tag, halves, lands, ks, send, recv, after)
        pieces = [_chip_sum("chip_sum_" + names[i], qc_idx, h, r, k) for i, h, r, k in zip(ids, halves, lands, ks)]
        st["share"] = _split_start("share_send_" + tag, pieces, len(ids), _pair_share_copies(ks, False))
        st["token"] = st["share"][2][0]

    def reduce_d(st, after):
        send, recv, bufs = st["share"]
        return _split_wait("share_wait_" + st["tag"], bufs, send, recv, _pair_share_copies(st["ks"], True), after)

    def ep_swiglu_bwd(ins, vals, outs):
        df = vals[0]
        gtv, upv = ins[2][...].astype(F32), ins[3][...].astype(F32)
        sg = _sigmoid(gtv)
        outs[0][...] = (df * upv * (sg + gtv * sg * (1.0 - sg))).astype(BF16)
        outs[1][...] = (df * (gtv * sg)).astype(BF16)

    dgt, dup = _mm(
        "mm_dswiglu", (s // tm, d_ff // tf, 1), [dx3b, g_down, gt, up],
        [pl.BlockSpec((tm, d), lambda i, j, k: (i, 0)), pl.BlockSpec((tf, d), lambda i, j, k: (j, 0)), f_out, f_out],
        [jax.ShapeDtypeStruct((s, d_ff), BF16), jax.ShapeDtypeStruct((s, d_ff), BF16)], [f_out, f_out],
        [(0, 1, NT, 0)], 1, (tm, tf), 1, ep_swiglu_bwd,
        _mm_vmem([((tm, d), BF16, 2), ((tf, d), BF16, 2), ((tm, tf), F32, 8)]),
    )

    def ep_bf16(ins, vals, outs):
        for o, v in zip(outs, vals):
            o[...] = v.astype(BF16)

    def ep_f32(ins, vals, outs):
        for o, v in zip(outs, vals):
            o[...] = v

    twn = _tile(d, (1024, 512))
    gw_down = _mm(
        "mm_gw_down", (d_ff // tkf, d // twn, 1), [f_act, dx3b],
        [pl.BlockSpec((s, tkf), lambda i, j, k: (0, i)), pl.BlockSpec((s, twn), lambda i, j, k: (0, j))],
        [jax.ShapeDtypeStruct((d_ff, d), BF16)], [pl.BlockSpec((tkf, twn), lambda i, j, k: (i, j))],
        [(0, 1, TN, 0)], 1, (tkf, twn), 1, ep_bf16,
        _mm_vmem([((s, tkf), BF16, 3), ((s, twn), BF16, 2), ((tkf, twn), F32, 3)]),
    )[0]
    red_down = reduce_a("down", [W_DOWN], [gw_down])

    dh2 = _mm(
        "mm_dh2", (s // tm, d // tn, nkf), [dgt, g_gate, dup, g_up],
        [pl.BlockSpec((tm, tkf), lambda i, j, k: (i, k)), pl.BlockSpec((tn, tkf), lambda i, j, k: (j, k)),
         pl.BlockSpec((tm, tkf), lambda i, j, k: (i, k)), pl.BlockSpec((tn, tkf), lambda i, j, k: (j, k))],
        [jax.ShapeDtypeStruct((s, d), F32)], [pl.BlockSpec((tm, tn), lambda i, j, k: (i, j))],
        [(0, 1, NT, 0), (2, 3, NT, 0)], 1, (tm, tn), nkf, ep_f32,
        _mm_vmem([((tm, tkf), BF16, 4), ((tn, tkf), BF16, 4), ((tm, tn), F32, 5)]),
        after=(red_down["token"],),
    )[0]
    reduce_b(red_down, dh2)

    twr = _tile(d, (1024, 512))
    w_tile = pl.BlockSpec((twr, tf), lambda i, j, k: (i, j))
    gw_gate, gw_up = _mm(
        "mm_gw_gate_up", (d // twr, d_ff // tf, 1), [h2, dgt, dup],
        [pl.BlockSpec((s, twr), lambda i, j, k: (0, i)), pl.BlockSpec((s, tf), lambda i, j, k: (0, j)),
         pl.BlockSpec((s, tf), lambda i, j, k: (0, j))],
        [jax.ShapeDtypeStruct((d, d_ff), BF16), jax.ShapeDtypeStruct((d, d_ff), BF16)], [w_tile, w_tile],
        [(0, 1, TN, 0), (0, 2, TN, 1)], 2, (twr, tf), 1, ep_bf16,
        _mm_vmem([((s, twr), BF16, 3), ((s, tf), BF16, 4), ((twr, tf), F32, 6)]),
        after=(red_down["token"],),
    )
    red_ffn = reduce_a("ffn_in", [W_GATE, W_UP], [gw_gate, gw_up])

    dx2, dx2b, dg_ffn = _rms_bwd("rms_ffn_bwd", x2, norm_ffn, dh2, dx3, after=(red_ffn["token"],))

    nj = d // tg

    def lo(j):
        return jnp.minimum(j, nj - 1)

    def gate_bwd_body(dx_ref, wo_ref, ga_ref, gb_ref, ya_ref, yb_ref, dya_ref, dyb_ref, dz_ref, keep):
        j = pl.program_id(1)

        @pl.when(j < nj)
        def _():
            dm = lax.dot_general(dx_ref[...], wo_ref[...], NT, preferred_element_type=F32)
            sa, sb = _sigmoid(ga_ref[...].astype(F32)), _sigmoid(gb_ref[...].astype(F32))
            dya_ref[...] = (dm * sa).astype(BF16)
            dyb_ref[...] = (dm * sb).astype(BF16)
            dz_ref[...] = (dm * ya_ref[...].astype(F32) * (sa * (1.0 - sa))).astype(BF16)
            keep[lo(j)] = (dm * yb_ref[...].astype(F32) * (sb * (1.0 - sb))).astype(BF16)

        @pl.when(j >= nj)
        def _():
            dz_ref[...] = keep[jnp.maximum(j - nj, 0)]

    t_lo = pl.BlockSpec((tm, tg), lambda i, j: (i, lo(j)))
    dya, dyb, dz = pl.pallas_call(
        gate_bwd_body,
        name="mm_dgate",
        grid=(s // tm, 2 * nj),
        in_specs=[
            pl.BlockSpec((tm, d), lambda i, j: (i, 0)),
            pl.BlockSpec((tg, d), lambda i, j: (lo(j), 0)),
            pl.BlockSpec((tm, tg), lambda i, j: (i, ga0 + lo(j))),
            pl.BlockSpec((tm, tg), lambda i, j: (i, gb0 + lo(j))),
            t_lo,
            t_lo,
        ],
        out_specs=[t_lo, t_lo, pl.BlockSpec((tm, tg), lambda i, j: (i, ga0 + j))],
        out_shape=[jax.ShapeDtypeStruct((s, d), BF16), jax.ShapeDtypeStruct((s, d), BF16), jax.ShapeDtypeStruct((s, n_in), BF16)],
        scratch_shapes=[pltpu.VMEM((nj, tm, tg), BF16)],
        compiler_params=_params(_mm_vmem([((tm, d), BF16, 2), ((tg, d), BF16, 2), ((tm, tg), F32, 14), ((nj, tm, tg), BF16, 1)])),
    )(dx2b, g_o, z, z, y_a, y_b)
    reduce_b(red_ffn, dya)

    gw_o = _mm(
        "mm_gw_o", (d // twr, d // twn, 1), [m_act, dx2b],
        [pl.BlockSpec((s, twr), lambda i, j, k: (0, i)), pl.BlockSpec((s, twn), lambda i, j, k: (0, j))],
        [jax.ShapeDtypeStruct((d, d), BF16)], [pl.BlockSpec((twr, twn), lambda i, j, k: (i, j))],
        [(0, 1, TN, 0)], 1, (twr, twn), 1, ep_bf16,
        _mm_vmem([((s, twr), BF16, 3), ((s, twn), BF16, 2), ((twr, twn), F32, 3)]),
        after=(red_ffn["token"],),
    )[0]
    red_o = reduce_a("w_o", [W_O], [gw_o])

    tb = _tile(w_sgu, (1024, 512))
    b_out = pl.BlockSpec((tm, tb), lambda i, j, k: (i, j))

    def ep_branch_bwd(ins, vals, outs):
        outs[0][...] = vals[0].astype(BF16)
        outs[1][...] = vals[1].astype(BF16)

    da, datt = _mm(
        "mm_dbranches", (s // tm, w_sgu // tb, 1), [dya, g_a, dyb, g_b],
        [pl.BlockSpec((tm, d), lambda i, j, k: (i, 0)), pl.BlockSpec((tb, d), lambda i, j, k: (j, 0)),
         pl.BlockSpec((tm, d), lambda i, j, k: (i, 0)), pl.BlockSpec((tb, d), lambda i, j, k: (j, 0))],
        [jax.ShapeDtypeStruct((s, w_sgu), BF16), jax.ShapeDtypeStruct((s, w_att), BF16)], [b_out, b_out],
        [(0, 1, NT, 0), (2, 3, NT, 1)], 2, (tm, tb), 1, ep_branch_bwd,
        _mm_vmem([((tm, d), BF16, 4), ((tb, d), BF16, 4), ((tm, tb), F32, 6)]),
        after=(red_o["token"],),
    )
    reduce_b(red_o, da)

    wb_tile = pl.BlockSpec((tb, twn), lambda i, j, k: (i, j))
    gw_a, gw_b = _mm(
        "mm_gw_branches", (w_sgu // tb, d // twn, 1), [a_act, dya, att, dyb],
        [pl.BlockSpec((s, tb), lambda i, j, k: (0, i)), pl.BlockSpec((s, twn), lambda i, j, k: (0, j)),
         pl.BlockSpec((s, tb), lambda i, j, k: (0, i)), pl.BlockSpec((s, twn), lambda i, j, k: (0, j))],
        [jax.ShapeDtypeStruct((w_sgu, d), BF16), jax.ShapeDtypeStruct((w_att, d), BF16)], [wb_tile, wb_tile],
        [(0, 1, TN, 0), (2, 3, TN, 1)], 2, (tb, twn), 1, ep_bf16,
        _mm_vmem([((s, tb), BF16, 5), ((s, twn), BF16, 4), ((tb, twn), F32, 6)]),
        after=(red_o["token"],),
    )
    red_mix = reduce_a("mix", [W_A, W_B], [gw_a, gw_b])

    dz, dws, dbs, dgain = _sgu_bwd(z, da, sgu_v_gain, ws_b, wst_b, b_col, w_sgu, dz, after=(red_mix["token"],))
    dz, dk_pad, dv_pad, dbias_tab, dsink = _attn_bwd(sink, z, k_pad, v_pad, bias_tab, datt, dz, grp, q_blk0)
    dz = _dkv_to_dz(dk_pad, dv_pad, dz, off_k // (2 * w_kv))
    drel = _relbias_bwd(dbias_tab, bucket)
    reduce_b(red_mix, dz)

    small_w = [norm_mix, sgu_v_gain, sgu_w_s, sgu_b_s, attn_sink, rel_bias, norm_ffn, norm_final]
    small_m = [m_norm_mix, m_sgu_v_gain, m_sgu_w_s, m_sgu_b_s, m_attn_sink, m_rel_bias, m_norm_ffn, m_norm_final]
    small_v = [v_norm_mix, v_sgu_v_gain, v_sgu_w_s, v_sgu_b_s, v_attn_sink, v_rel_bias, v_norm_ffn, v_norm_final]
    small_shapes = [w.shape for w in small_w]
    early = [dgain, dws, dbs, dsink[:, 0], drel[:, :REL_BUCKETS].T, dg_ffn, dg_final]
    p_early = _pack([g.reshape(shp) for g, shp in zip(early, small_shapes[1:])] + [loss_part[0, :1]])
    land = jnp.zeros((2 * N_CHIPS,) + p_early.shape, F32)
    sm_send, sm_recv, (p_early, land) = _split_start("small_send", [p_early, land], 2 * N_CHIPS - 1, _small_exchange_copies(False))

    tzn = _tile(n_in, (768, 640, 512))
    gw_in = _mm(
        "mm_gw_in", (d // twr, n_in // tzn, 1), [h1, dz],
        [pl.BlockSpec((s, twr), lambda i, j, k: (0, i)), pl.BlockSpec((s, tzn), lambda i, j, k: (0, j))],
        [jax.ShapeDtypeStruct((d, n_in), BF16)], [pl.BlockSpec((twr, tzn), lambda i, j, k: (i, j))],
        [(0, 1, TN, 0)], 1, (twr, tzn), 1, ep_bf16,
        _mm_vmem([((s, twr), BF16, 3), ((s, tzn), BF16, 2), ((twr, tzn), F32, 3)]),
        after=(red_mix["token"], p_early),
    )[0]
    red_in = reduce_a("w_in", [W_IN], [gw_in])

    tkz = _tile(n_in, (1920, 1536, 1280, 1024))
    nkz = n_in // tkz
    dh1 = _mm(
        "mm_dh1", (s // tm, d // tn, nkz), [dz, g_in],
        [pl.BlockSpec((tm, tkz), lambda i, j, k: (i, k)), pl.BlockSpec((tn, tkz), lambda i, j, k: (j, k))],
        [jax.ShapeDtypeStruct((s, d), F32)], [pl.BlockSpec((tm, tn), lambda i, j, k: (i, j))],
        [(0, 1, NT, 0)], 1, (tm, tn), nkz, ep_f32,
        _mm_vmem([((tm, tkz), BF16, 2), ((tn, tkz), BF16, 2), ((tm, tn), F32, 5)]),
        after=(red_in["token"],),
    )[0]

    reduce_b(red_in, dh1)
    grad_x, _, dg_mix = _rms_bwd("rms_mix_bwd", x2d, norm_mix, dh1, dx2, after=(red_in["token"],))

    mix_sum = _small_all_reduce(_pack([dg_mix.reshape(small_shapes[0])]))
    p_early, land = _split_wait("small_wait", [p_early, land], sm_send, sm_recv, _small_exchange_copies(True), mix_sum)
    me_idx = 2 * q_idx + c_idx
    packed_g = jnp.concatenate([mix_sum, _small_sum(me_idx, p_early, land)], axis=0)
    g_small = _unpack(packed_g, small_shapes + [(1,)])
    loss = g_small[-1].reshape(())
    g_small = g_small[:-1]
    zero1 = jnp.zeros((1,), F32)
    pw, pg, pm, pv = _pack(small_w + [zero1]), _pack(g_small + [zero1]), _pack(small_m + [zero1]), _pack(small_v + [zero1])
    small_upd = _adamw("adamw_small", pw, pg, pm, pv)
    d_small, nm_small, nv_small = [_unpack(a, small_shapes) for a in small_upd[:3]]

    grads_big, upd = [None] * 7, [None] * 7

    def finish(st, after):
        for i, g in zip(st["ids"], reduce_d(st, after)):
            upd[i] = _adamw("adamw_" + names[i], big_w[i], g, big_m[i], big_v[i])
            grads_big[i] = upd[i][3]
            after = upd[i][0]
        return after

    after, prev = small_upd[0], None
    for st in (red_down, red_ffn, red_o, red_mix, red_in):
        reduce_c(st, after)
        after = st["token"] if prev is None else finish(prev, st["token"])
        prev = st
    finish(prev, after)

    small_names = ["norm_mix", "sgu_v_gain", "sgu_w_s", "sgu_b_s", "attn_sink", "rel_bias", "norm_ffn", "norm_final"]
    table = {}
    for i, n in enumerate(names):
        table[n] = (grads_big[i][None], upd[i][0][None], upd[i][1][None], upd[i][2][None])
    for i, n in enumerate(small_names):
        table[n] = (g_small[i], d_small[i], nm_small[i], nv_small[i])
    order = ["w_in", "norm_mix", "sgu_v_gain", "sgu_w_s", "sgu_b_s", "w_a", "attn_sink", "rel_bias", "w_b", "w_o", "norm_ffn",
             "w_gate", "w_up", "w_down", "norm_final"]
    outs = [loss, grad_x.reshape(1, s, d)]
    for part in range(4):
        outs += [table[n][part] for n in order]
    return tuple(outs)
```

```python
import math

import jax
import jax.numpy as jnp
import numpy as np
from jax import lax
from jax.experimental import pallas as pl
from jax.experimental.pallas import tpu as pltpu

F32 = jnp.float32
BF16 = jnp.bfloat16
I32 = jnp.int32
MESH = pl.DeviceIdType.MESH

EPS = 1e-6
NEG = -1e30
BLK = 128
HEAD_DIM = 128
N_KV_HEADS = 2
REL_BUCKETS = 32
REL_MAX_DIST = 128
N_CHIPS = 4
ADAM_LR, ADAM_B1, ADAM_B2, ADAM_EPS, ADAM_WD, ADAM_STEP = 0.001, 0.9, 0.999, 1e-08, 0.01, 10

LANES = 128
VMEM_CAP = 60 * 1024 * 1024

NN = (((1,), (0,)), ((), ()))
NT = (((1,), (1,)), ((), ()))
TN = (((0,), (0,)), ((), ()))
ANY = pl.BlockSpec(memory_space=pl.ANY)
HBM_SPEC = pl.BlockSpec(memory_space=pltpu.HBM)
SEM_SPEC = pl.BlockSpec(memory_space=pltpu.SEMAPHORE)
EFFECT = pltpu.SideEffectType.DATAFLOW_SIDE_EFFECTING


def _tile(n, cands):
    for t in cands:
        if n % t == 0:
            return t
    return n


def _params(vmem_bytes=None, **kw):
    if vmem_bytes is not None:
        kw["vmem_limit_bytes"] = int(min(max(vmem_bytes, 32 * 1024 * 1024), VMEM_CAP))
    return pltpu.CompilerParams(**kw)


def _nbytes(shape, dtype):
    return int(np.prod(shape)) * jnp.dtype(dtype).itemsize


def _sigmoid(x):
    return 1.0 / (1.0 + jnp.exp(-x))


_GC = 0.7978845608028654
_GA = 0.044715


def _gelu(x):
    return 0.5 * x * (1.0 + jnp.tanh(_GC * (x + _GA * (x * x * x))))


def _gelu_grad(x):
    t = jnp.tanh(_GC * (x + _GA * (x * x * x)))
    return 0.5 * (1.0 + t) + 0.5 * x * (1.0 - t * t) * (_GC * (1.0 + 3.0 * _GA * (x * x)))


def _bf(v):
    return v if v.dtype == BF16 else v.astype(BF16)


def _mm(name, grid, ins, in_specs, out_shape, out_specs, pairs, n_acc, acc_tile, nk, epilogue, vmem_bytes, after=()):
    n_in, n_out = len(ins) + len(after), len(out_shape)

    def body(*refs):
        in_refs, out_refs, accs = refs[:n_in], refs[n_in : n_in + n_out], refs[n_in + n_out :]

        def products():
            vals = [None] * n_acc
            for a_i, b_i, dn, acc_i in pairs:
                d = lax.dot_general(_bf(in_refs[a_i][...]), _bf(in_refs[b_i][...]), dn, preferred_element_type=F32)
                vals[acc_i] = d if vals[acc_i] is None else vals[acc_i] + d
            return vals

        if nk == 1:
            epilogue(in_refs, products(), out_refs)
            return
        k = pl.program_id(2)
        vals = products()

        @pl.when(k == 0)
        def _():
            for a, v in zip(accs, vals):
                a[...] = v

        @pl.when(k > 0)
        def _():
            for a, v in zip(accs, vals):
                a[...] += v

        @pl.when(k == nk - 1)
        def _():
            epilogue(in_refs, [a[...] for a in accs], out_refs)

    scratch = [pltpu.VMEM(acc_tile, F32) for _ in range(n_acc)] if nk > 1 else []
    return pl.pallas_call(
        body,
        name=name,
        grid=grid,
        in_specs=list(in_specs) + [ANY] * len(after),
        out_specs=out_specs,
        out_shape=out_shape,
        scratch_shapes=scratch,
        compiler_params=_params(vmem_bytes),
    )(*ins, *after)


def _mm_vmem(tiles):
    return sum(_nbytes(s, d) * c for s, d, c in tiles) + 4 * 1024 * 1024


def _store_cast(ref, v):
    ref[...] = v.astype(ref.dtype)


def _rows8(v):
    r, d = v.shape
    return v.reshape(r // 8, 8, d).sum(axis=0)


def _rms_fwd(name, x, g, after=()):
    s, d = x.shape
    tm = _tile(s, (256, 128))

    def body(x_ref, g_ref, *rest):
        h_ref = rest[-1]
        xv = x_ref[...]
        r = lax.rsqrt(jnp.mean(xv * xv, axis=-1, keepdims=True) + EPS)
        h_ref[...] = ((xv * r) * g_ref[...]).astype(BF16)

    return pl.pallas_call(
        body,
        name=name,
        grid=(s // tm,),
        in_specs=[pl.BlockSpec((tm, d), lambda i: (i, 0)), pl.BlockSpec((1, d), lambda i: (0, 0))] + [ANY] * len(after),
        out_specs=pl.BlockSpec((tm, d), lambda i: (i, 0)),
        out_shape=jax.ShapeDtypeStruct((s, d), BF16),
    )(x, g, *after)


def _rms_bwd(name, x, g, dh, dres, after=()):
    s, d = x.shape
    tm = _tile(s, (256, 128))
    n = s // tm
    n_after = len(after)

    def body(x_ref, g_ref, dh_ref, dres_ref, *rest):
        dx_ref, dxb_ref, dg_ref, acc_ref = rest[n_after:]
        i = pl.program_id(0)
        xv = x_ref[...]
        r = lax.rsqrt(jnp.mean(xv * xv, axis=-1, keepdims=True) + EPS)
        xh = xv * r
        dhv = dh_ref[...]
        dxh = dhv * g_ref[...]
        dx = r * (dxh - xh * jnp.mean(dxh * xh, axis=-1, keepdims=True)) + dres_ref[...]
        dx_ref[...] = dx
        dxb_ref[...] = dx.astype(BF16)
        part = _rows8(dhv * xh)

        @pl.when(i == 0)
        def _():
            acc_ref[...] = part

        @pl.when(i > 0)
        def _():
            acc_ref[...] += part

        @pl.when(i == n - 1)
        def _():
            dg_ref[...] = jnp.sum(acc_ref[...], axis=0, keepdims=True)

    row = pl.BlockSpec((tm, d), lambda i: (i, 0))
    vec = pl.BlockSpec((1, d), lambda i: (0, 0))
    return pl.pallas_call(
        body,
        name=name,
        grid=(n,),
        in_specs=[row, vec, row, row] + [ANY] * n_after,
        out_specs=[row, row, vec],
        out_shape=[jax.ShapeDtypeStruct((s, d), F32), jax.ShapeDtypeStruct((s, d), BF16), jax.ShapeDtypeStruct((1, d), F32)],
        scratch_shapes=[pltpu.VMEM((8, d), F32)],
    )(x, g, dh, dres, *after)


def _head(x3, g, target):
    s, d = x3.shape
    tm = _tile(s, (256, 128))
    n = s // tm

    def body(x_ref, g_ref, t_ref, dx_ref, dxb_ref, dg_ref, loss_ref, acc_g, acc_l):
        i = pl.program_id(0)
        xv = x_ref[...]
        gv = g_ref[...]
        r = lax.rsqrt(jnp.mean(xv * xv, axis=-1, keepdims=True) + EPS)
        xh = xv * r
        e = xh * gv - t_ref[...]
        dy = e * (1.0 / d)
        dxh = dy * gv
        dx = r * (dxh - xh * jnp.mean(dxh * xh, axis=-1, keepdims=True))
        dx_ref[...] = dx
        dxb_ref[...] = dx.astype(BF16)
        pg = _rows8(dy * xh)
        plo = _rows8(e * e)

        @pl.when(i == 0)
        def _():
            acc_g[...] = pg
            acc_l[...] = plo

        @pl.when(i > 0)
        def _():
            acc_g[...] += pg
            acc_l[...] += plo

        @pl.when(i == n - 1)
        def _():
            dg_ref[...] = jnp.sum(acc_g[...], axis=0, keepdims=True)
            loss_ref[...] = jnp.full((1, LANES), (0.5 / d) * jnp.sum(acc_l[...]), F32)

    row = pl.BlockSpec((tm, d), lambda i: (i, 0))
    vec = pl.BlockSpec((1, d), lambda i: (0, 0))
    return pl.pallas_call(
        body,
        name="head",
        grid=(n,),
        in_specs=[row, vec, row],
        out_specs=[row, row, vec, pl.BlockSpec((1, LANES), lambda i: (0, 0))],
        out_shape=[
            jax.ShapeDtypeStruct((s, d), F32),
            jax.ShapeDtypeStruct((s, d), BF16),
            jax.ShapeDtypeStruct((1, d), F32),
            jax.ShapeDtypeStruct((1, LANES), F32),
        ],
        scratch_shapes=[pltpu.VMEM((8, d), F32), pltpu.VMEM((8, d), F32)],
    )(x3, g, target)


def _sgu_fwd(z, gain, ws_b, b_col, w_sgu):
    s = z.shape[0]
    groups = ws_b.shape[0]

    def body(zu_ref, zv_ref, gain_ref, ws_ref, b_ref, a_ref):
        vv = _gelu(zv_ref[...].astype(F32))
        r = lax.rsqrt(jnp.mean(vv * vv, axis=-1, keepdims=True) + EPS)
        vn = ((vv * r) * gain_ref[...]).astype(BF16)
        u = _gelu(zu_ref[...].astype(F32))
        for g in range(groups):
            sl = slice(g * BLK, (g + 1) * BLK)
            mixed = jnp.dot(ws_ref[g], vn[:, sl], preferred_element_type=F32) + b_ref[g]
            a_ref[:, sl] = (u[:, sl] * mixed).astype(BF16)

    return pl.pallas_call(
        body,
        name="sgu_fwd",
        grid=(s // BLK,),
        in_specs=[
            pl.BlockSpec((BLK, w_sgu), lambda c: (c, 0)),
            pl.BlockSpec((BLK, w_sgu), lambda c: (c, 1)),
            pl.BlockSpec((1, w_sgu), lambda c: (0, 0)),
            pl.BlockSpec((groups, BLK, BLK), lambda c: (0, 0, 0)),
            pl.BlockSpec((groups, BLK, 1), lambda c: (0, 0, 0)),
        ],
        out_specs=pl.BlockSpec((BLK, w_sgu), lambda c: (c, 0)),
        out_shape=jax.ShapeDtypeStruct((s, w_sgu), BF16),
    )(z, z, gain, ws_b, b_col)


def _sgu_bwd(z, da, gain, ws_b, wst_b, b_col, w_sgu, dz, after=()):
    s = z.shape[0]
    groups = ws_b.shape[0]
    n = s // BLK
    n_skip = 1 + len(after)

    def body(zu_ref, zv_ref, da_ref, gain_ref, ws_ref, wst_ref, b_ref, *rest):
        dz_ref, dws_ref, dbs_ref, dgain_ref, acc_gain = rest[n_skip:]
        c = pl.program_id(0)
        zu = zu_ref[...].astype(F32)
        zv = zv_ref[...].astype(F32)
        gain_v = gain_ref[...]
        vv = _gelu(zv)
        r = lax.rsqrt(jnp.mean(vv * vv, axis=-1, keepdims=True) + EPS)
        xh = vv * r
        vn = (xh * gain_v).astype(BF16)
        u = _gelu(zu)
        dav = da_ref[...].astype(F32)
        dmix = dav * u
        dmix_b = dmix.astype(BF16)
        dvn_parts = []
        for g in range(groups):
            sl = slice(g * BLK, (g + 1) * BLK)
            mixed = jnp.dot(ws_ref[g], vn[:, sl], preferred_element_type=F32) + b_ref[g]
            dz_ref[:, sl] = (dav[:, sl] * mixed * _gelu_grad(zu[:, sl])).astype(BF16)
            dvn_parts.append(jnp.dot(wst_ref[g], dmix_b[:, sl], preferred_element_type=F32))
            dws_g = lax.dot_general(dmix_b[:, sl], vn[:, sl], NT, preferred_element_type=F32)
            dbs_g = jnp.sum(dmix[:, sl], axis=1, keepdims=True)

            @pl.when(c == 0)
            def _():
                dws_ref[g] = dws_g
                dbs_ref[g] = dbs_g

            @pl.when(c > 0)
            def _():
                dws_ref[g] += dws_g
                dbs_ref[g] += dbs_g

        dvn = jnp.concatenate(dvn_parts, axis=1)
        dxh = dvn * gain_v
        dvv = r * (dxh - xh * jnp.mean(dxh * xh, axis=-1, keepdims=True))
        dz_ref[:, w_sgu:] = (dvv * _gelu_grad(zv)).astype(BF16)
        pg = _rows8(dvn * xh)

        @pl.when(c == 0)
        def _():
            acc_gain[...] = pg

        @pl.when(c > 0)
        def _():
            acc_gain[...] += pg

        @pl.when(c == n - 1)
        def _():
            dgain_ref[...] = jnp.sum(acc_gain[...], axis=0, keepdims=True)

    full3 = pl.BlockSpec((groups, BLK, BLK), lambda c: (0, 0, 0))
    col3 = pl.BlockSpec((groups, BLK, 1), lambda c: (0, 0, 0))
    vec = pl.BlockSpec((1, w_sgu), lambda c: (0, 0))
    return pl.pallas_call(
        body,
        name="sgu_bwd",
        grid=(n,),
        in_specs=[
            pl.BlockSpec((BLK, w_sgu), lambda c: (c, 0)),
            pl.BlockSpec((BLK, w_sgu), lambda c: (c, 1)),
            pl.BlockSpec((BLK, w_sgu), lambda c: (c, 0)),
            vec,
            full3,
            full3,
            col3,
            ANY,
        ]
        + [ANY] * len(after),
        out_specs=[pl.BlockSpec((BLK, 2 * w_sgu), lambda c: (c, 0)), full3, col3, vec],
        out_shape=[
            jax.ShapeDtypeStruct(dz.shape, BF16),
            jax.ShapeDtypeStruct((groups, BLK, BLK), F32),
            jax.ShapeDtypeStruct((groups, BLK, 1), F32),
            jax.ShapeDtypeStruct((1, w_sgu), F32),
        ],
        scratch_shapes=[pltpu.VMEM((8, w_sgu), F32)],
        input_output_aliases={7: 0},
    )(z, z, da, gain, ws_b, wst_b, b_col, dz, *after)


def _attn_softmax(sink_ref, q_ref, k_ref, v_ref, bias_ref, s_len, grp):
    kv = pl.program_id(0)
    n = pl.program_id(1)
    start = pl.multiple_of(n * BLK, BLK)
    kb = k_ref[pl.ds(start, 3 * BLK), :]
    vb = v_ref[pl.ds(start, 3 * BLK), :]
    qv = q_ref[...]
    qs = jnp.concatenate([qv[:, g * HEAD_DIM : (g + 1) * HEAD_DIM] for g in range(grp)], axis=0).astype(BF16)
    sc = lax.dot_general(qs, kb, NT, preferred_element_type=F32) * (HEAD_DIM**-0.5)
    sc = sc + bias_ref[...].reshape(grp * BLK, 3 * BLK)
    kpos = start + lax.broadcasted_iota(I32, (1, 3 * BLK), 1) - BLK
    sc = jnp.where((kpos >= 0) & (kpos < s_len), sc, NEG)
    sink = jnp.concatenate([jnp.full((BLK, 1), sink_ref[kv * grp + g], F32) for g in range(grp)], axis=0)
    m = jnp.maximum(jnp.max(sc, axis=-1, keepdims=True), sink)
    p = jnp.exp(sc - m)
    esink = jnp.exp(sink - m)
    den = jnp.sum(p, axis=-1, keepdims=True) + esink
    return start, qs, kb, vb, p / den, esink / den


def _attn_specs(s, grp, q_blk0):
    qw = grp * HEAD_DIM
    return [
        pl.BlockSpec(memory_space=pltpu.SMEM),
        pl.BlockSpec((BLK, qw), lambda kv, n: (n, q_blk0 + kv)),
        pl.BlockSpec((s + 2 * BLK, HEAD_DIM), lambda kv, n: (0, kv)),
        pl.BlockSpec((s + 2 * BLK, HEAD_DIM), lambda kv, n: (0, kv)),
        pl.BlockSpec((grp, BLK, 3 * BLK), lambda kv, n: (kv, 0, 0)),
    ]


def _attn_fwd(sink, z, k_pad, v_pad, bias_tab, grp, q_blk0):
    s = z.shape[0]
    qw = grp * HEAD_DIM

    def body(sink_ref, q_ref, k_ref, v_ref, bias_ref, o_ref):
        _, _, _, vb, pn, _ = _attn_softmax(sink_ref, q_ref, k_ref, v_ref, bias_ref, s, grp)
        o = jnp.dot(pn.astype(BF16), vb, preferred_element_type=F32)
        for g in range(grp):
            o_ref[:, g * HEAD_DIM : (g + 1) * HEAD_DIM] = o[g * BLK : (g + 1) * BLK].astype(BF16)

    return pl.pallas_call(
        body,
        name="attn_fwd",
        grid=(N_KV_HEADS, s // BLK),
        in_specs=_attn_specs(s, grp, q_blk0),
        out_specs=pl.BlockSpec((BLK, qw), lambda kv, n: (n, kv)),
        out_shape=jax.ShapeDtypeStruct((s, N_KV_HEADS * qw), BF16),
    )(sink, z, k_pad, v_pad, bias_tab)


def _attn_bwd(sink, z, k_pad, v_pad, bias_tab, dout, dz, grp, q_blk0):
    s = z.shape[0]
    qw = grp * HEAD_DIM
    nb = s // BLK
    heads = N_KV_HEADS * grp

    def body(sink_ref, q_ref, k_ref, v_ref, bias_ref, do_ref, dz_in, dq_ref, dk_ref, dv_ref, dbias_ref, dsink_ref, dk_acc, dv_acc):
        del dz_in
        kv = pl.program_id(0)
        n = pl.program_id(1)
        start, qs, kb, vb, pn, psink = _attn_softmax(sink_ref, q_ref, k_ref, v_ref, bias_ref, s, grp)
        dov = do_ref[...]
        dos = jnp.concatenate([dov[:, g * HEAD_DIM : (g + 1) * HEAD_DIM] for g in range(grp)], axis=0)
        dp = lax.dot_general(dos, vb, NT, preferred_element_type=F32)
        dvb = lax.dot_general(pn.astype(BF16), dos, TN, preferred_element_type=F32)
        delta = jnp.sum(pn * dp, axis=-1, keepdims=True)
        ds = pn * (dp - delta)
        dsb = (ds * (HEAD_DIM**-0.5)).astype(BF16)
        dq = jnp.dot(dsb, kb, preferred_element_type=F32)
        dkb = lax.dot_general(dsb, qs, TN, preferred_element_type=F32)
        for g in range(grp):
            dq_ref[:, g * HEAD_DIM : (g + 1) * HEAD_DIM] = dq[g * BLK : (g + 1) * BLK].astype(BF16)

        @pl.when(n == 0)
        def _():
            dk_acc[...] = jnp.zeros_like(dk_acc)
            dv_acc[...] = jnp.zeros_like(dv_acc)
            dbias_ref[...] = jnp.zeros_like(dbias_ref)

        @pl.when((n == 0) & (kv == 0))
        def _():
            dsink_ref[...] = jnp.zeros_like(dsink_ref)

        dk_acc[pl.ds(start, 3 * BLK), :] += dkb
        dv_acc[pl.ds(start, 3 * BLK), :] += dvb
        dbias_ref[...] += ds.reshape(grp, BLK, 3 * BLK)
        row = lax.broadcasted_iota(I32, (heads, LANES), 0)
        sd = psink * delta
        upd = jnp.zeros((heads, LANES), F32)
        for g in range(grp):
            upd = jnp.where(row == kv * grp + g, -jnp.sum(sd[g * BLK : (g + 1) * BLK]), upd)
        dsink_ref[...] += upd

        @pl.when(n == nb - 1)
        def _():
            dk_ref[...] = dk_acc[...]
            dv_ref[...] = dv_acc[...]

    pad_spec = pl.BlockSpec((s + 2 * BLK, HEAD_DIM), lambda kv, n: (0, kv))
    kvw = N_KV_HEADS * HEAD_DIM
    return pl.pallas_call(
        body,
        name="attn_bwd",
        grid=(N_KV_HEADS, nb),
        in_specs=_attn_specs(s, grp, q_blk0) + [pl.BlockSpec((BLK, qw), lambda kv, n: (n, kv)), ANY],
        out_specs=[
            pl.BlockSpec((BLK, qw), lambda kv, n: (n, q_blk0 + kv)),
            pad_spec,
            pad_spec,
            pl.BlockSpec((grp, BLK, 3 * BLK), lambda kv, n: (kv, 0, 0)),
            pl.BlockSpec((heads, LANES), lambda kv, n: (0, 0)),
        ],
        out_shape=[
            jax.ShapeDtypeStruct(dz.shape, BF16),
            jax.ShapeDtypeStruct((s + 2 * BLK, kvw), F32),
            jax.ShapeDtypeStruct((s + 2 * BLK, kvw), F32),
            jax.ShapeDtypeStruct((heads, BLK, 3 * BLK), F32),
            jax.ShapeDtypeStruct((heads, LANES), F32),
        ],
        scratch_shapes=[pltpu.VMEM((s + 2 * BLK, HEAD_DIM), F32), pltpu.VMEM((s + 2 * BLK, HEAD_DIM), F32)],
        input_output_aliases={6: 0},
    )(sink, z, k_pad, v_pad, bias_tab, dout, dz)


def _dkv_to_dz(dk_pad, dv_pad, dz, blk_idx):
    s = dz.shape[0]
    kvw = dk_pad.shape[1]

    def body(dk_ref, dv_ref, dz_in, out_ref):
        del dz_in
        out_ref[:, :kvw] = dk_ref[...].astype(BF16)
        out_ref[:, kvw:] = dv_ref[...].astype(BF16)

    src = pl.BlockSpec((BLK, kvw), lambda i: (i + 1, 0))
    return pl.pallas_call(
        body,
        name="dkv_to_dz",
        grid=(s // BLK,),
        in_specs=[src, src, ANY],
        out_specs=pl.BlockSpec((BLK, 2 * kvw), lambda i: (i, blk_idx)),
        out_shape=jax.ShapeDtypeStruct(dz.shape, BF16),
        input_output_aliases={2: 0},
    )(dk_pad, dv_pad, dz)


def _relbias_bwd(dbias_tab, bucket):
    heads = dbias_tab.shape[0]

    def body(dt_ref, bk_ref, out_ref):
        lane = lax.broadcasted_iota(I32, (1, LANES), 1)
        bk = bk_ref[...]
        rows = []
        for h in range(heads):
            dt = dt_ref[h]
            acc = jnp.zeros((1, LANES), F32)
            for b in range(REL_BUCKETS):
                acc = jnp.where(lane == b, jnp.sum(jnp.where(bk == b, dt, 0.0)), acc)
            rows.append(acc)
        out_ref[...] = jnp.concatenate(rows, axis=0)

    return pl.pallas_call(body, name="relbias_bwd", out_shape=jax.ShapeDtypeStruct((heads, LANES), F32))(dbias_tab, bucket)


def _t5_bucket(rel):
    nb = REL_BUCKETS // 2
    ret = jnp.where(rel > 0, nb, 0)
    n = jnp.abs(rel)
    max_exact = nb // 2
    nf = jnp.maximum(n, 1).astype(F32)
    large = max_exact + (jnp.log(nf / max_exact) / math.log(REL_MAX_DIST / max_exact) * (nb - max_exact)).astype(I32)
    large = jnp.minimum(large, nb - 1)
    return ret + jnp.where(n < max_exact, n, large)


def _band_tables(rel_bias):
    qi = jnp.arange(BLK)[:, None]
    kj = jnp.arange(3 * BLK)[None, :]
    rel = kj - BLK - qi
    bucket = _t5_bucket(rel).astype(I32)
    heads = rel_bias.shape[1]
    masked = jnp.where(jnp.abs(rel) <= BLK, bucket, -1)

    def body(rb_ref, bk_ref, out_ref):
        bk = bk_ref[...]
        for h in range(heads):
            tab = jnp.full(bk.shape, NEG, F32)
            for b in range(REL_BUCKETS):
                tab = jnp.where(bk == b, rb_ref[b, h], tab)
            out_ref[h] = tab

    bias_tab = pl.pallas_call(
        body,
        name="bias_table",
        in_specs=[pl.BlockSpec(memory_space=pltpu.SMEM), pl.BlockSpec(memory_space=pltpu.VMEM)],
        out_specs=pl.BlockSpec(memory_space=pltpu.VMEM),
        out_shape=jax.ShapeDtypeStruct((heads, BLK, 3 * BLK), F32),
    )(rel_bias.astype(F32), masked)
    return bias_tab, bucket


EW_BLOCK_ELEMS = 512 * 1024


def _ew_tiles(shape, elems=EW_BLOCK_ELEMS // 2):
    r, c = shape
    tn = c if c <= 2048 else _tile(c, (2048, 1920, 1536, 1408, 1024, 512))
    tm = _tile(r, [t for t in (1024, 512, 256, 128, 64, 32, 16, 8) if t * tn <= elems] or [8])
    return tm, tn


def _cast_into_full(name, qidx, w, kind, after=()):
    r, c = w.shape
    tm, tn = _ew_tiles(w.shape, EW_BLOCK_ELEMS)
    nbi, nbj = r // tm, c // tn
    if kind == "col":
        full, out_spec = (r, c * N_CHIPS), pl.BlockSpec((tm, tn), lambda i, j, q: (i, q[0] * nbj + j))
    else:
        full, out_spec = (r * N_CHIPS, c), pl.BlockSpec((tm, tn), lambda i, j, q: (q[0] * nbi + i, j))

    def body(q_ref, w_ref, *rest):
        del q_ref
        rest[-1][...] = w_ref[...].astype(BF16)

    return pl.pallas_call(
        body,
        name=name,
        grid_spec=pltpu.PrefetchScalarGridSpec(
            num_scalar_prefetch=1,
            grid=(nbi, nbj),
            in_specs=[pl.BlockSpec((tm, tn), lambda i, j, q: (i, j))] + [ANY] * len(after),
            out_specs=out_spec,
        ),
        out_shape=jax.ShapeDtypeStruct(full, BF16),
    )(qidx, w, *after)


def _adamw(name, w, g, m, v, after=()):
    tm, tn = _ew_tiles(w.shape)
    if _nbytes(w.shape, F32) <= 1024 * 1024:
        tm, tn = w.shape
    spec = pl.BlockSpec((tm, tn), lambda i, j: (i, j))
    n_after = len(after)

    def body(w_ref, g_ref, m_ref, v_ref, *rest):
        d_ref, nm_ref, nv_ref, g_out_ref = rest[n_after:]
        gv = g_ref[...]
        g_out_ref[...] = gv
        nm = ADAM_B1 * m_ref[...] + (1.0 - ADAM_B1) * gv
        nv = ADAM_B2 * v_ref[...] + (1.0 - ADAM_B2) * (gv * gv)
        m_hat = nm / (1.0 - ADAM_B1**ADAM_STEP)
        v_hat = nv / (1.0 - ADAM_B2**ADAM_STEP)
        d_ref[...] = -ADAM_LR * (m_hat / (jnp.sqrt(v_hat) + ADAM_EPS) + ADAM_WD * w_ref[...])
        nm_ref[...] = nm
        nv_ref[...] = nv

    out = jax.ShapeDtypeStruct(w.shape, F32)
    return pl.pallas_call(
        body, name=name, grid=(w.shape[0] // tm, w.shape[1] // tn), in_specs=[spec] * 4 + [ANY] * n_after,
        out_specs=[spec] * 4, out_shape=[out, out, out, out],
    )(w, g, m, v, *after)


def _pair_add(name, cidx, g_full, r_sib, kind):
    hr, hc = r_sib.shape
    tm, tn = _ew_tiles((hr, hc), EW_BLOCK_ELEMS)
    nbi, nbj = hr // tm, hc // tn
    if kind == "col":
        g_spec = pl.BlockSpec((tm, tn), lambda i, j, c: (c[0] * nbi + i, j))
    else:
        g_spec = pl.BlockSpec((tm, tn), lambda i, j, c: (i, c[0] * nbj + j))
    spec = pl.BlockSpec((tm, tn), lambda i, j, c: (i, j))

    def body(c_ref, g_ref, r_ref, o_ref):
        del c_ref
        o_ref[...] = (g_ref[...].astype(F32) + r_ref[...].astype(F32)).astype(BF16)

    return pl.pallas_call(
        body,
        name=name,
        grid_spec=pltpu.PrefetchScalarGridSpec(num_scalar_prefetch=1, grid=(nbi, nbj), in_specs=[g_spec, spec], out_specs=spec),
        out_shape=jax.ShapeDtypeStruct((hr, hc), BF16),
    )(cidx, g_full, r_sib)


def _chip_sum(name, qidx, c_half, r_ici, kind):
    _, pr, pc = r_ici.shape
    tm, tn = _ew_tiles((pr, pc), EW_BLOCK_ELEMS)
    nbi, nbj = pr // tm, pc // tn
    if kind == "col":
        own_spec = pl.BlockSpec((tm, tn), lambda i, j, q: (i, q[0] * nbj + j))
        full, out_spec = (2 * pr, pc), pl.BlockSpec((tm, tn), lambda i, j, q: (q[1] * nbi + i, j))
    else:
        own_spec = pl.BlockSpec((tm, tn), lambda i, j, q: (q[0] * nbi + i, j))
        full, out_spec = (pr, 2 * pc), pl.BlockSpec((tm, tn), lambda i, j, q: (i, q[1] * nbj + j))

    def body(q_ref, own_ref, r_ref, o_ref):
        q = q_ref[0]
        own = own_ref[...].astype(F32)
        recv = [r_ref[r].astype(F32) for r in range(3)]
        total = None
        for chip in range(N_CHIPS):
            d = chip ^ q
            term = jnp.where(d == 0, own, jnp.where(d == 2, recv[0], jnp.where(d == 1, recv[1], recv[2])))
            total = term if total is None else total + term
        o_ref[...] = total

    return pl.pallas_call(
        body,
        name=name,
        grid_spec=pltpu.PrefetchScalarGridSpec(
            num_scalar_prefetch=1,
            grid=(nbi, nbj),
            in_specs=[own_spec, pl.BlockSpec((3, tm, tn), lambda i, j, q: (0, i, j))],
            out_specs=out_spec,
        ),
        out_shape=jax.ShapeDtypeStruct(full, F32),
    )(qidx, c_half, r_ici)


_REL_MASK = (2, 1, 3)


def _place():
    x, y, c = lax.axis_index("x"), lax.axis_index("y"), lax.axis_index("c")
    chips = [(1 - x, y), (x, 1 - y), (1 - x, 1 - y)]
    return x, y, c, 2 * x + y, chips


def _shard_view(ref, kind, chip):
    if kind == "col":
        w = ref.shape[1] // N_CHIPS
        return ref.at[:, pl.ds(pl.multiple_of(chip * w, LANES), w)]
    h = ref.shape[0] // N_CHIPS
    return ref.at[pl.ds(pl.multiple_of(chip * h, 16), h), :]


def _row_half(ref, half):
    h = ref.shape[0] // 2
    return ref.at[pl.ds(pl.multiple_of(half * h, 16), h), :]


def _pair_half(ref, kind, half):
    if kind == "col":
        return _row_half(ref, half)
    w = ref.shape[1] // 2
    return ref.at[:, pl.ds(pl.multiple_of(half * w, LANES), w)]


def _remote(src, dst, send_sem, recv_sem, dev):
    return pltpu.make_async_remote_copy(src_ref=src, dst_ref=dst, send_sem=send_sem, recv_sem=recv_sem, device_id=dev, device_id_type=MESH)


def _hbm(a):
    return pltpu.with_memory_space_constraint(a, pltpu.HBM)


def _gather_start(name, fulls, kinds):
    n_w = len(fulls)

    def body(*refs):
        g = refs[:n_w]
        send_sem, recv_sem = refs[n_w], refs[n_w + 1]
        token = refs[-1]
        _, _, c, q, chips = _place()
        for w in range(n_w):
            mine = _row_half(_shard_view(g[w], kinds[w], q), c)
            for r, chip in enumerate(chips):
                _remote(mine, mine, send_sem.at[3 * w + r], recv_sem.at[3 * w + r], (*chip, c)).start()
        token[...] = jnp.zeros_like(token)

    res = pl.pallas_call(
        body,
        name=name,
        out_shape=(
            pltpu.SemaphoreType.DMA((3 * n_w,)),
            pltpu.SemaphoreType.DMA((3 * n_w,)),
            *[pltpu.HBM(f.shape, f.dtype) for f in fulls],
            jax.ShapeDtypeStruct((8, LANES), F32),
        ),
        in_specs=[HBM_SPEC] * n_w,
        out_specs=(SEM_SPEC, SEM_SPEC, *[HBM_SPEC] * n_w, pl.BlockSpec(memory_space=pltpu.VMEM)),
        input_output_aliases={w: w + 2 for w in range(n_w)},
        compiler_params=pltpu.CompilerParams(has_side_effects=EFFECT),
    )(*[_hbm(f) for f in fulls])
    return res[0], res[1], list(res[2 : 2 + n_w]), res[-1]


def _gather_wait(name, fulls, kinds, w_ids, send_sem, recv_sem, after):
    n = len(fulls)

    def body(*refs):
        g = refs[:n]
        s_sem, r_sem = refs[n], refs[n + 1]
        x, y, c, q, _ = _place()
        for i, w in enumerate(w_ids):
            mine = _row_half(_shard_view(g[i], kinds[i], q), c)
            for r in range(3):
                landed = _row_half(_shard_view(g[i], kinds[i], q ^ _REL_MASK[r]), c)
                cp = _remote(mine, landed, s_sem.at[3 * w + r], r_sem.at[3 * w + r], (x, y, 1 - c))
                cp.wait_send()
                cp.wait_recv()

    res = pl.pallas_call(
        body,
        name=name,
        out_shape=[pltpu.HBM(f.shape, f.dtype) for f in fulls],
        in_specs=[HBM_SPEC] * n + [SEM_SPEC, SEM_SPEC, ANY],
        out_specs=[HBM_SPEC] * n,
        input_output_aliases={i: i for i in range(n)},
        compiler_params=pltpu.CompilerParams(has_side_effects=EFFECT),
    )(*fulls, send_sem, recv_sem, after)
    return list(res)


def _gather_forward(name, fulls, kinds):
    n = len(fulls)

    def body(*refs):
        g = refs[n : 2 * n]
        send, recv = refs[2 * n :]
        x, y, c, q, _ = _place()
        sib = (x, y, 1 - c)
        cps = []
        for i in range(n):
            for r in range(3):
                landed = _row_half(_shard_view(g[i], kinds[i], q ^ _REL_MASK[r]), c)
                cps.append(_remote(landed, landed, send.at[i, r], recv.at[i, r], sib))
        for cp in cps:
            cp.start()
        for i in range(n):
            for r in range(3):
                other = _row_half(_shard_view(g[i], kinds[i], q ^ _REL_MASK[r]), 1 - c)
                _remote(other, other, send.at[i, r], recv.at[i, r], sib).wait_recv()
        for cp in cps:
            cp.wait_send()

    res = pl.pallas_call(
        body,
        name=name,
        in_specs=[ANY] * n,
        out_specs=[ANY] * n,
        out_shape=[jax.ShapeDtypeStruct(f.shape, f.dtype) for f in fulls],
        scratch_shapes=[pltpu.SemaphoreType.DMA((n, 3)), pltpu.SemaphoreType.DMA((n, 3))],
        input_output_aliases={i: i for i in range(n)},
    )(*fulls)
    return list(res)


def _split_start(name, bufs, n_sems, copies):
    n = len(bufs)

    def body(*refs):
        for cp in copies(refs[:n], refs[n], refs[n + 1]):
            cp.start()

    res = pl.pallas_call(
        body,
        name=name,
        out_shape=(
            pltpu.SemaphoreType.DMA((n_sems,)),
            pltpu.SemaphoreType.DMA((n_sems,)),
            *[pltpu.HBM(b.shape, b.dtype) for b in bufs],
        ),
        in_specs=[HBM_SPEC] * n,
        out_specs=(SEM_SPEC, SEM_SPEC, *[HBM_SPEC] * n),
        input_output_aliases={i: i + 2 for i in range(n)},
        compiler_params=pltpu.CompilerParams(has_side_effects=EFFECT),
    )(*[_hbm(b) for b in bufs])
    return res[0], res[1], list(res[2:])


def _split_wait(name, bufs, send_sem, recv_sem, copies, after):
    n = len(bufs)

    def body(*refs):
        for cp in copies(refs[:n], refs[n], refs[n + 1]):
            cp.wait_send()
            cp.wait_recv()

    res = pl.pallas_call(
        body,
        name=name,
        out_shape=[pltpu.HBM(b.shape, b.dtype) for b in bufs],
        in_specs=[HBM_SPEC] * n + [SEM_SPEC, SEM_SPEC, ANY],
        out_specs=[HBM_SPEC] * n,
        input_output_aliases={i: i for i in range(n)},
        compiler_params=pltpu.CompilerParams(has_side_effects=EFFECT),
    )(*bufs, send_sem, recv_sem, after)
    return list(res)


def _pair_exchange_copies(kinds):
    n = len(kinds)

    def copies(refs, send_sem, recv_sem):
        x, y, c, _, _ = _place()
        return [
            _remote(_pair_half(refs[w], kinds[w], 1 - c), refs[n + w], send_sem.at[w], recv_sem.at[w], (x, y, 1 - c))
            for w in range(n)
        ]

    return copies


def _pair_share_copies(kinds, waiting):
    def copies(refs, send_sem, recv_sem):
        x, y, c, _, _ = _place()
        out = []
        for w, kind in enumerate(kinds):
            mine = _pair_half(refs[w], kind, c)
            dst = _pair_half(refs[w], kind, 1 - c) if waiting else mine
            out.append(_remote(mine, dst, send_sem.at[w], recv_sem.at[w], (x, y, 1 - c)))
        return out

    return copies


def _piece_shape(half_shape, kind):
    r, c = half_shape
    return (3, r, c // N_CHIPS) if kind == "col" else (3, r // N_CHIPS, c)


def _chip_send_start(name, halves, kinds):
    n = len(halves)
    lands = [lax.empty(_piece_shape(h.shape, k), BF16) for h, k in zip(halves, kinds)]

    def body(*refs):
        h, land = refs[:n], refs[n : 2 * n]
        send_sem, recv_sem = refs[2 * n], refs[2 * n + 1]
        _, _, c, q, chips = _place()
        for i in range(n):
            for r, chip in enumerate(chips):
                piece = _shard_view(h[i], kinds[i], q ^ _REL_MASK[r])
                _remote(piece, land[i].at[r], send_sem.at[3 * i + r], recv_sem.at[3 * i + r], (*chip, c)).start()

    res = pl.pallas_call(
        body,
        name=name,
        out_shape=(
            pltpu.SemaphoreType.DMA((3 * n,)),
            pltpu.SemaphoreType.DMA((3 * n,)),
            *[pltpu.HBM(a.shape, a.dtype) for a in halves],
            *[pltpu.HBM(a.shape, a.dtype) for a in lands],
        ),
        in_specs=[HBM_SPEC] * (2 * n),
        out_specs=(SEM_SPEC, SEM_SPEC, *[HBM_SPEC] * (2 * n)),
        input_output_aliases={i: i + 2 for i in range(2 * n)},
        compiler_params=pltpu.CompilerParams(has_side_effects=EFFECT),
    )(*[_hbm(a) for a in halves], *[_hbm(a) for a in lands])
    return res[0], res[1], list(res[2 : 2 + n]), list(res[2 + n :])


def _chip_send_wait(name, halves, lands, kinds, send_sem, recv_sem, after):
    n = len(halves)

    def body(*refs):
        h, land = refs[:n], refs[n : 2 * n]
        s_sem, r_sem = refs[2 * n], refs[2 * n + 1]
        x, y, c, q, _ = _place()
        for i in range(n):
            for r in range(3):
                piece = _shard_view(h[i], kinds[i], q ^ _REL_MASK[r])
                cp = _remote(piece, land[i].at[r], s_sem.at[3 * i + r], r_sem.at[3 * i + r], (x, y, 1 - c))
                cp.wait_send()
                cp.wait_recv()

    res = pl.pallas_call(
        body,
        name=name,
        out_shape=[pltpu.HBM(a.shape, a.dtype) for a in halves] + [pltpu.HBM(a.shape, a.dtype) for a in lands],
        in_specs=[HBM_SPEC] * (2 * n) + [SEM_SPEC, SEM_SPEC, ANY],
        out_specs=[HBM_SPEC] * (2 * n),
        input_output_aliases={i: i for i in range(2 * n)},
        compiler_params=pltpu.CompilerParams(has_side_effects=EFFECT),
    )(*halves, *lands, send_sem, recv_sem, after)
    return list(res[:n]), list(res[n:])


def _small_all_reduce(p):
    rows = p.shape[0]
    n_dev = 2 * N_CHIPS

    def body(p_ref, o_ref, buf, loc_sem, send, recv):
        x, y, c, q, _ = _place()
        me = 2 * q + c
        own = pltpu.make_async_copy(p_ref, buf.at[me], loc_sem)
        own.start()
        cps = []
        for d in range(1, n_dev):
            dev = (x ^ ((d >> 2) & 1), y ^ ((d >> 1) & 1), c ^ (d & 1))
            cps.append(_remote(p_ref, buf.at[me], send.at[d - 1], recv.at[d - 1], dev))
        for cp in cps:
            cp.start()
        for d in range(1, n_dev):
            slot = buf.at[me ^ d]
            _remote(slot, slot, send.at[d - 1], recv.at[d - 1], (x, y, c)).wait_recv()
        own.wait()
        total = buf[0]
        for d in range(1, n_dev):
            total = total + buf[d]
        o_ref[...] = total
        for cp in cps:
            cp.wait_send()

    return pl.pallas_call(
        body,
        name="small_all_reduce",
        in_specs=[ANY],
        out_specs=pl.BlockSpec(memory_space=pltpu.VMEM),
        out_shape=jax.ShapeDtypeStruct(p.shape, F32),
        scratch_shapes=[
            pltpu.VMEM((n_dev, rows, LANES), F32),
            pltpu.SemaphoreType.DMA,
            pltpu.SemaphoreType.DMA((n_dev - 1,)),
            pltpu.SemaphoreType.DMA((n_dev - 1,)),
        ],
    )(p)


def _small_exchange_copies(waiting):
    def copies(refs, send_sem, recv_sem):
        p, land = refs
        x, y, c, q, _ = _place()
        me = 2 * q + c
        out = []
        for dd in range(1, 2 * N_CHIPS):
            dev = (x ^ ((dd >> 2) & 1), y ^ ((dd >> 1) & 1), c ^ (dd & 1))
            dst = land.at[me ^ dd] if waiting else land.at[me]
            out.append(_remote(p, dst, send_sem.at[dd - 1], recv_sem.at[dd - 1], dev))
        return out

    return copies


def _small_sum(name, me_idx, p, land):
    rows = p.shape[0]
    n_dev = 2 * N_CHIPS

    def body(me_ref, p_ref, land_ref, o_ref):
        me = me_ref[0]
        total = None
        for dev in range(n_dev):
            term = jnp.where(me == dev, p_ref[...], land_ref[dev])
            total = term if total is None else total + term
        o_ref[...] = total

    return pl.pallas_call(
        body,
        name=name,
        grid_spec=pltpu.PrefetchScalarGridSpec(
            num_scalar_prefetch=1,
            grid=(1,),
            in_specs=[pl.BlockSpec((rows, LANES), lambda i, m: (0, 0)), pl.BlockSpec((n_dev, rows, LANES), lambda i, m: (0, 0, 0))],
            out_specs=pl.BlockSpec((rows, LANES), lambda i, m: (0, 0)),
        ),
        out_shape=jax.ShapeDtypeStruct(p.shape, F32),
    )(me_idx, p, land)


def _pack(parts):
    rows = []
    for a in parts:
        flat = a.reshape(-1).astype(F32)
        n = flat.shape[0]
        padded = -(-n // (8 * LANES)) * (8 * LANES)
        rows.append(jnp.pad(flat, (0, padded - n)).reshape(-1, LANES))
    return jnp.concatenate(rows, axis=0)


def _unpack(packed, shapes):
    out, row = [], 0
    for shp in shapes:
        n = int(np.prod(shp))
        nrows = -(-n // (8 * LANES)) * 8
        out.append(packed[row : row + nrows].reshape(-1)[:n].reshape(shp))
        row += nrows
    return out


def kernel(x, w_in, norm_mix, sgu_v_gain, sgu_w_s, sgu_b_s, w_a_out, attn_sink, rel_bias, w_b_out, w_o, norm_ffn, w_gate, w_up, w_down, norm_final, loss_target, m_w_in, m_norm_mix, m_sgu_v_gain, m_sgu_w_s, m_sgu_b_s, m_w_a_out, m_attn_sink, m_rel_bias, m_w_b_out, m_w_o, m_norm_ffn, m_w_gate, m_w_up, m_w_down, m_norm_final, v_w_in, v_norm_mix, v_sgu_v_gain, v_sgu_w_s, v_sgu_b_s, v_w_a_out, v_attn_sink, v_rel_bias, v_w_b_out, v_w_o, v_norm_ffn, v_w_gate, v_w_up, v_w_down, v_norm_final):
    s, d = x.shape[1], x.shape[2]
    w_sgu = sgu_v_gain.shape[1]
    groups = sgu_w_s.shape[1]
    heads = attn_sink.shape[1]
    grp = heads // N_KV_HEADS
    w_att = heads * HEAD_DIM
    w_kv = N_KV_HEADS * HEAD_DIM
    d_ff = w_gate.shape[2] * N_CHIPS
    n_in = w_in.shape[2] * N_CHIPS
    off_q = 2 * w_sgu
    off_k = off_q + w_att
    off_g = off_k + 2 * w_kv
    assert n_in == off_g + 2 * d and groups * BLK == w_sgu and s % BLK == 0

    x2d = x.reshape(s, d)
    tgt = loss_target.reshape(s, d)
    c_idx = lax.axis_index("c").astype(I32).reshape(1)
    q_idx = (2 * lax.axis_index("x") + lax.axis_index("y")).astype(I32).reshape(1)
    qc_idx = jnp.concatenate([q_idx, c_idx])

    W_IN, W_A, W_B, W_O, W_GATE, W_UP, W_DOWN = range(7)
    names = ["w_in", "w_a", "w_b", "w_o", "w_gate", "w_up", "w_down"]
    kinds = ["col", "col", "col", "row", "col", "col", "row"]
    big_w = [w_in[0], w_a_out[0], w_b_out[0], w_o[0], w_gate[0], w_up[0], w_down[0]]
    big_m = [m_w_in[0], m_w_a_out[0], m_w_b_out[0], m_w_o[0], m_w_gate[0], m_w_up[0], m_w_down[0]]
    big_v = [v_w_in[0], v_w_a_out[0], v_w_b_out[0], v_w_o[0], v_w_gate[0], v_w_up[0], v_w_down[0]]
    full_in = _cast_into_full("cast_w_in", q_idx, big_w[W_IN], kinds[W_IN])
    in_send, in_recv, (full_in,), token = _gather_start("gather_start_in", [full_in], [kinds[W_IN]])
    rest = [_cast_into_full("cast_" + names[i], q_idx, big_w[i], kinds[i], after=(token,)) for i in range(1, 7)]
    ag_send, ag_recv, rest, token = _gather_start("gather_start_rest", rest, kinds[1:])
    fulls = [full_in] + rest

    def gathered(tag, ids, after):
        if ids == [W_IN]:
            sems, pos = (in_send, in_recv), [0]
        else:
            sems, pos = (ag_send, ag_recv), [i - 1 for i in ids]
        got = _gather_wait("gather_wait_" + tag, [fulls[i] for i in ids], [kinds[i] for i in ids], pos, *sems, after)
        return _gather_forward("gather_fwd_" + tag, got, [kinds[i] for i in ids])

    ws_b = sgu_w_s[0].astype(BF16)
    wst_b = jnp.swapaxes(sgu_w_s[0], 1, 2).astype(BF16)
    b_col = sgu_b_s[0].reshape(groups, BLK, 1)
    bias_tab, bucket = _band_tables(rel_bias)
    sink = attn_sink[0]

    tm = _tile(s, (1024, 512, 256, 128))

    h1 = _rms_fwd("rms_mix", x2d, norm_mix, after=(token,))
    (g_in,) = gathered("in", [W_IN], h1)

    tn = _tile(n_in, (768, 640, 512))
    z = _mm(
        "mm_z", (s // tm, n_in // tn, 1), [h1, g_in],
        [pl.BlockSpec((tm, d), lambda i, j, k: (i, 0)), pl.BlockSpec((d, tn), lambda i, j, k: (0, j))],
        [jax.ShapeDtypeStruct((s, n_in), BF16)], [pl.BlockSpec((tm, tn), lambda i, j, k: (i, j))],
        [(0, 1, NN, 0)], 1, (tm, tn), 1, lambda ins, vals, outs: _store_cast(outs[0], vals[0]),
        _mm_vmem([((tm, d), BF16, 2), ((d, tn), BF16, 2), ((tm, tn), F32, 3)]),
    )[0]
    g_a, g_b, g_o = gathered("mix", [W_A, W_B, W_O], z)

    a_act = _sgu_fwd(z, sgu_v_gain, ws_b, b_col, w_sgu)

    kv_b = z[:, off_k:off_g]
    k_pad = jnp.pad(kv_b[:, :w_kv], ((BLK, BLK), (0, 0)))
    v_pad = jnp.pad(kv_b[:, w_kv:], ((BLK, BLK), (0, 0)))
    q_blk0 = off_q // (grp * HEAD_DIM)
    att = _attn_fwd(sink, z, k_pad, v_pad, bias_tab, grp, q_blk0)

    tg = _tile(d, (512,))
    ga0, gb0 = off_g // tg, (off_g + d) // tg

    def ep_gate(ins, vals, outs):
        sa, sb = _sigmoid(ins[4][...].astype(F32)), _sigmoid(ins[5][...].astype(F32))
        outs[0][...] = (sa * vals[0] + sb * vals[1]).astype(BF16)
        outs[1][...] = vals[0].astype(BF16)
        outs[2][...] = vals[1].astype(BF16)

    t_out = pl.BlockSpec((tm, tg), lambda i, j, k: (i, j))
    m_act, y_a, y_b = _mm(
        "mm_branches", (s // tm, d // tg, 1), [a_act, g_a, att, g_b, z, z],
        [pl.BlockSpec((tm, w_sgu), lambda i, j, k: (i, 0)), pl.BlockSpec((w_sgu, tg), lambda i, j, k: (0, j)),
         pl.BlockSpec((tm, w_att), lambda i, j, k: (i, 0)), pl.BlockSpec((w_att, tg), lambda i, j, k: (0, j)),
         pl.BlockSpec((tm, tg), lambda i, j, k: (i, ga0 + j)), pl.BlockSpec((tm, tg), lambda i, j, k: (i, gb0 + j))],
        [jax.ShapeDtypeStruct((s, d), BF16)] * 3,
        [t_out, t_out, t_out], [(0, 1, NN, 0), (2, 3, NN, 1)], 2, (tm, tg), 1, ep_gate,
        _mm_vmem([((tm, w_sgu), BF16, 4), ((w_sgu, tg), BF16, 4), ((tm, tg), F32, 12)]),
    )

    tn = _tile(d, (1024, 512))

    def ep_residual(ins, vals, outs):
        outs[0][...] = ins[2][...] + vals[0]

    x2 = _mm(
        "mm_wo", (s // tm, d // tn, 1), [m_act, g_o, x2d],
        [pl.BlockSpec((tm, d), lambda i, j, k: (i, 0)), pl.BlockSpec((d, tn), lambda i, j, k: (0, j)),
         pl.BlockSpec((tm, tn), lambda i, j, k: (i, j))],
        [jax.ShapeDtypeStruct((s, d), F32)], [pl.BlockSpec((tm, tn), lambda i, j, k: (i, j))],
        [(0, 1, NN, 0)], 1, (tm, tn), 1, ep_residual,
        _mm_vmem([((tm, d), BF16, 2), ((d, tn), BF16, 2), ((tm, tn), F32, 5)]),
    )[0]

    h2 = _rms_fwd("rms_ffn", x2, norm_ffn)
    g_gate, g_up = gathered("ffn_in", [W_GATE, W_UP], h2)

    tf = _tile(d_ff, (512,))

    def ep_swiglu(ins, vals, outs):
        gt, up = vals
        outs[0][...] = gt.astype(BF16)
        outs[1][...] = up.astype(BF16)
        outs[2][...] = ((gt * _sigmoid(gt)) * up).astype(BF16)

    f_out = pl.BlockSpec((tm, tf), lambda i, j, k: (i, j))
    gt, up, f_act = _mm(
        "mm_gate_up", (s // tm, d_ff // tf, 1), [h2, g_gate, g_up],
        [pl.BlockSpec((tm, d), lambda i, j, k: (i, 0)), pl.BlockSpec((d, tf), lambda i, j, k: (0, j)),
         pl.BlockSpec((d, tf), lambda i, j, k: (0, j))],
        [jax.ShapeDtypeStruct((s, d_ff), BF16)] * 3,
        [f_out, f_out, f_out], [(0, 1, NN, 0), (0, 2, NN, 1)], 2, (tm, tf), 1, ep_swiglu,
        _mm_vmem([((tm, d), BF16, 2), ((d, tf), BF16, 4), ((tm, tf), F32, 8)]),
    )
    (g_down,) = gathered("ffn_out", [W_DOWN], f_act)

    tkf = _tile(d_ff, (1408, 1024, 512))
    nkf = d_ff // tkf
    x3 = _mm(
        "mm_down", (s // tm, d // tn, nkf), [f_act, g_down, x2],
        [pl.BlockSpec((tm, tkf), lambda i, j, k: (i, k)), pl.BlockSpec((tkf, tn), lambda i, j, k: (k, j)),
         pl.BlockSpec((tm, tn), lambda i, j, k: (i, j))],
        [jax.ShapeDtypeStruct((s, d), F32)], [pl.BlockSpec((tm, tn), lambda i, j, k: (i, j))],
        [(0, 1, NN, 0)], 1, (tm, tn), nkf, ep_residual,
        _mm_vmem([((tm, tkf), BF16, 2), ((tkf, tn), BF16, 2), ((tm, tn), F32, 6)]),
    )[0]

    dx3, dx3b, dg_final, loss_part = _head(x3, norm_final.reshape(1, d), tgt)

    def reduce_a(tag, ids, grads):
        ks = [kinds[i] for i in ids]
        lands = [lax.empty((g.shape[0] // 2, g.shape[1]) if k == "col" else (g.shape[0], g.shape[1] // 2), BF16)
                 for g, k in zip(grads, ks)]
        send, recv, bufs = _split_start("pair_send_" + tag, list(grads) + lands, len(ids), _pair_exchange_copies(ks))
        return {"tag": tag, "ids": ids, "ks": ks, "pair": (send, recv, bufs), "token": bufs[0]}

    def reduce_b(st, after):
        tag, ids, ks = st["tag"], st["ids"], st["ks"]
        send, recv, bufs = st["pair"]
        bufs = _split_wait("pair_wait_" + tag, bufs, send, recv, _pair_exchange_copies(ks), after)
        grads, from_sib = bufs[: len(ids)], bufs[len(ids) :]
        halves = [_pair_add("pair_add_" + names[i], c_idx, g, r, k) for i, g, r, k in zip(ids, grads, from_sib, ks)]
        st["chip"] = _chip_send_start("chip_send_" + tag, halves, ks)
        st["token"] = st["chip"][2][0]

    def reduce_c(st, after):
        tag, ids, ks = st["tag"], st["ids"], st["ks"]
        send, recv, halves, lands = st["chip"]
        halves, lands = _chip_send_wait("chip_wait_" + tag, halves, lands, ks, send, recv, after)
        pieces = [_chip_sum("chip_sum_" + names[i], qc_idx, h, r, k) for i, h, r, k in zip(ids, halves, lands, ks)]
        st["share"] = _split_start("share_send_" + tag, pieces, len(ids), _pair_share_copies(ks, False))
        st["token"] = st["share"][2][0]

    def reduce_d(st, after):
        send, recv, bufs = st["share"]
        return _split_wait("share_wait_" + st["tag"], bufs, send, recv, _pair_share_copies(st["ks"], True), after)

    def ep_swiglu_bwd(ins, vals, outs):
        df = vals[0]
        gtv, upv = ins[2][...].astype(F32), ins[3][...].astype(F32)
        sg = _sigmoid(gtv)
        outs[0][...] = (df * upv * (sg + gtv * sg * (1.0 - sg))).astype(BF16)
        outs[1][...] = (df * (gtv * sg)).astype(BF16)

    dgt, dup = _mm(
        "mm_dswiglu", (s // tm, d_ff // tf, 1), [dx3b, g_down, gt, up],
        [pl.BlockSpec((tm, d), lambda i, j, k: (i, 0)), pl.BlockSpec((tf, d), lambda i, j, k: (j, 0)), f_out, f_out],
        [jax.ShapeDtypeStruct((s, d_ff), BF16), jax.ShapeDtypeStruct((s, d_ff), BF16)], [f_out, f_out],
        [(0, 1, NT, 0)], 1, (tm, tf), 1, ep_swiglu_bwd,
        _mm_vmem([((tm, d), BF16, 2), ((tf, d), BF16, 2), ((tm, tf), F32, 8)]),
    )

    def ep_bf16(ins, vals, outs):
        for o, v in zip(outs, vals):
            o[...] = v.astype(BF16)

    def ep_f32(ins, vals, outs):
        for o, v in zip(outs, vals):
            o[...] = v

    twn = _tile(d, (1024, 512))
    gw_down = _mm(
        "mm_gw_down", (d_ff // tkf, d // twn, 1), [f_act, dx3b],
        [pl.BlockSpec((s, tkf), lambda i, j, k: (0, i)), pl.BlockSpec((s, twn), lambda i, j, k: (0, j))],
        [jax.ShapeDtypeStruct((d_ff, d), BF16)], [pl.BlockSpec((tkf, twn), lambda i, j, k: (i, j))],
        [(0, 1, TN, 0)], 1, (tkf, twn), 1, ep_bf16,
        _mm_vmem([((s, tkf), BF16, 3), ((s, twn), BF16, 2), ((tkf, twn), F32, 3)]),
    )[0]
    red_down = reduce_a("down", [W_DOWN], [gw_down])

    dh2 = _mm(
        "mm_dh2", (s // tm, d // tn, nkf), [dgt, g_gate, dup, g_up],
        [pl.BlockSpec((tm, tkf), lambda i, j, k: (i, k)), pl.BlockSpec((tn, tkf), lambda i, j, k: (j, k)),
         pl.BlockSpec((tm, tkf), lambda i, j, k: (i, k)), pl.BlockSpec((tn, tkf), lambda i, j, k: (j, k))],
        [jax.ShapeDtypeStruct((s, d), F32)], [pl.BlockSpec((tm, tn), lambda i, j, k: (i, j))],
        [(0, 1, NT, 0), (2, 3, NT, 0)], 1, (tm, tn), nkf, ep_f32,
        _mm_vmem([((tm, tkf), BF16, 4), ((tn, tkf), BF16, 4), ((tm, tn), F32, 5)]),
        after=(red_down["token"],),
    )[0]
    reduce_b(red_down, dh2)

    twr = _tile(d, (1024, 512))
    w_tile = pl.BlockSpec((twr, tf), lambda i, j, k: (i, j))
    gw_gate, gw_up = _mm(
        "mm_gw_gate_up", (d // twr, d_ff // tf, 1), [h2, dgt, dup],
        [pl.BlockSpec((s, twr), lambda i, j, k: (0, i)), pl.BlockSpec((s, tf), lambda i, j, k: (0, j)),
         pl.BlockSpec((s, tf), lambda i, j, k: (0, j))],
        [jax.ShapeDtypeStruct((d, d_ff), BF16), jax.ShapeDtypeStruct((d, d_ff), BF16)], [w_tile, w_tile],
        [(0, 1, TN, 0), (0, 2, TN, 1)], 2, (twr, tf), 1, ep_bf16,
        _mm_vmem([((s, twr), BF16, 3), ((s, tf), BF16, 4), ((twr, tf), F32, 6)]),
        after=(red_down["token"],),
    )
    red_ffn = reduce_a("ffn_in", [W_GATE, W_UP], [gw_gate, gw_up])

    dx2, dx2b, dg_ffn = _rms_bwd("rms_ffn_bwd", x2, norm_ffn, dh2, dx3, after=(red_ffn["token"],))

    nj = d // tg

    def lo(j):
        return jnp.minimum(j, nj - 1)

    def gate_bwd_body(dx_ref, wo_ref, ga_ref, gb_ref, ya_ref, yb_ref, dya_ref, dyb_ref, dz_ref, keep):
        j = pl.program_id(1)

        @pl.when(j < nj)
        def _():
            dm = lax.dot_general(dx_ref[...], wo_ref[...], NT, preferred_element_type=F32)
            sa, sb = _sigmoid(ga_ref[...].astype(F32)), _sigmoid(gb_ref[...].astype(F32))
            dya_ref[...] = (dm * sa).astype(BF16)
            dyb_ref[...] = (dm * sb).astype(BF16)
            dz_ref[...] = (dm * ya_ref[...].astype(F32) * (sa * (1.0 - sa))).astype(BF16)
            keep[lo(j)] = (dm * yb_ref[...].astype(F32) * (sb * (1.0 - sb))).astype(BF16)

        @pl.when(j >= nj)
        def _():
            dz_ref[...] = keep[jnp.maximum(j - nj, 0)]

    t_lo = pl.BlockSpec((tm, tg), lambda i, j: (i, lo(j)))
    dya, dyb, dz = pl.pallas_call(
        gate_bwd_body,
        name="mm_dgate",
        grid=(s // tm, 2 * nj),
        in_specs=[
            pl.BlockSpec((tm, d), lambda i, j: (i, 0)),
            pl.BlockSpec((tg, d), lambda i, j: (lo(j), 0)),
            pl.BlockSpec((tm, tg), lambda i, j: (i, ga0 + lo(j))),
            pl.BlockSpec((tm, tg), lambda i, j: (i, gb0 + lo(j))),
            t_lo,
            t_lo,
        ],
        out_specs=[t_lo, t_lo, pl.BlockSpec((tm, tg), lambda i, j: (i, ga0 + j))],
        out_shape=[jax.ShapeDtypeStruct((s, d), BF16), jax.ShapeDtypeStruct((s, d), BF16), jax.ShapeDtypeStruct((s, n_in), BF16)],
        scratch_shapes=[pltpu.VMEM((nj, tm, tg), BF16)],
        compiler_params=_params(_mm_vmem([((tm, d), BF16, 2), ((tg, d), BF16, 2), ((tm, tg), F32, 14), ((nj, tm, tg), BF16, 1)])),
    )(dx2b, g_o, z, z, y_a, y_b)
    reduce_b(red_ffn, dya)

    gw_o = _mm(
        "mm_gw_o", (d // twr, d // twn, 1), [m_act, dx2b],
        [pl.BlockSpec((s, twr), lambda i, j, k: (0, i)), pl.BlockSpec((s, twn), lambda i, j, k: (0, j))],
        [jax.ShapeDtypeStruct((d, d), BF16)], [pl.BlockSpec((twr, twn), lambda i, j, k: (i, j))],
        [(0, 1, TN, 0)], 1, (twr, twn), 1, ep_bf16,
        _mm_vmem([((s, twr), BF16, 3), ((s, twn), BF16, 2), ((twr, twn), F32, 3)]),
        after=(red_ffn["token"],),
    )[0]
    red_o = reduce_a("w_o", [W_O], [gw_o])

    tb = _tile(w_sgu, (1024, 512))
    b_out = pl.BlockSpec((tm, tb), lambda i, j, k: (i, j))

    def ep_branch_bwd(ins, vals, outs):
        outs[0][...] = vals[0].astype(BF16)
        outs[1][...] = vals[1].astype(BF16)

    da, datt = _mm(
        "mm_dbranches", (s // tm, w_sgu // tb, 1), [dya, g_a, dyb, g_b],
        [pl.BlockSpec((tm, d), lambda i, j, k: (i, 0)), pl.BlockSpec((tb, d), lambda i, j, k: (j, 0)),
         pl.BlockSpec((tm, d), lambda i, j, k: (i, 0)), pl.BlockSpec((tb, d), lambda i, j, k: (j, 0))],
        [jax.ShapeDtypeStruct((s, w_sgu), BF16), jax.ShapeDtypeStruct((s, w_att), BF16)], [b_out, b_out],
        [(0, 1, NT, 0), (2, 3, NT, 1)], 2, (tm, tb), 1, ep_branch_bwd,
        _mm_vmem([((tm, d), BF16, 4), ((tb, d), BF16, 4), ((tm, tb), F32, 6)]),
        after=(red_o["token"],),
    )
    reduce_b(red_o, da)

    wb_tile = pl.BlockSpec((tb, twn), lambda i, j, k: (i, j))
    gw_a, gw_b = _mm(
        "mm_gw_branches", (w_sgu // tb, d // twn, 1), [a_act, dya, att, dyb],
        [pl.BlockSpec((s, tb), lambda i, j, k: (0, i)), pl.BlockSpec((s, twn), lambda i, j, k: (0, j)),
         pl.BlockSpec((s, tb), lambda i, j, k: (0, i)), pl.BlockSpec((s, twn), lambda i, j, k: (0, j))],
        [jax.ShapeDtypeStruct((w_sgu, d), BF16), jax.ShapeDtypeStruct((w_att, d), BF16)], [wb_tile, wb_tile],
        [(0, 1, TN, 0), (2, 3, TN, 1)], 2, (tb, twn), 1, ep_bf16,
        _mm_vmem([((s, tb), BF16, 5), ((s, twn), BF16, 4), ((tb, twn), F32, 6)]),
        after=(red_o["token"],),
    )
    red_mix = reduce_a("mix", [W_A, W_B], [gw_a, gw_b])

    dz, dws, dbs, dgain = _sgu_bwd(z, da, sgu_v_gain, ws_b, wst_b, b_col, w_sgu, dz, after=(red_mix["token"],))
    dz, dk_pad, dv_pad, dbias_tab, dsink = _attn_bwd(sink, z, k_pad, v_pad, bias_tab, datt, dz, grp, q_blk0)
    dz = _dkv_to_dz(dk_pad, dv_pad, dz, off_k // (2 * w_kv))
    drel = _relbias_bwd(dbias_tab, bucket)
    reduce_b(red_mix, dz)

    small_w = [norm_mix, sgu_v_gain, sgu_w_s, sgu_b_s, attn_sink, rel_bias, norm_ffn, norm_final]
    small_m = [m_norm_mix, m_sgu_v_gain, m_sgu_w_s, m_sgu_b_s, m_attn_sink, m_rel_bias, m_norm_ffn, m_norm_final]
    small_v = [v_norm_mix, v_sgu_v_gain, v_sgu_w_s, v_sgu_b_s, v_attn_sink, v_rel_bias, v_norm_ffn, v_norm_final]
    small_shapes = [w.shape for w in small_w]
    early = [dgain, dws, dbs, dsink[:, 0], drel[:, :REL_BUCKETS].T, dg_ffn, dg_final]
    p_early = _pack([g.reshape(shp) for g, shp in zip(early, small_shapes[1:])] + [loss_part[0, :1]])
    land = jnp.zeros((2 * N_CHIPS,) + p_early.shape, F32)
    sm_send, sm_recv, (p_early, land) = _split_start("small_send", [p_early, land], 2 * N_CHIPS - 1, _small_exchange_copies(False))

    tzn = _tile(n_in, (768, 640, 512))
    gw_in = _mm(
        "mm_gw_in", (d // twr, n_in // tzn, 1), [h1, dz],
        [pl.BlockSpec((s, twr), lambda i, j, k: (0, i)), pl.BlockSpec((s, tzn), lambda i, j, k: (0, j))],
        [jax.ShapeDtypeStruct((d, n_in), BF16)], [pl.BlockSpec((twr, tzn), lambda i, j, k: (i, j))],
        [(0, 1, TN, 0)], 1, (twr, tzn), 1, ep_bf16,
        _mm_vmem([((s, twr), BF16, 3), ((s, tzn), BF16, 2), ((twr, tzn), F32, 3)]),
        after=(red_mix["token"], p_early),
    )[0]
    red_in = reduce_a("w_in", [W_IN], [gw_in])

    tkz = _tile(n_in, (1920, 1536, 1280, 1024))
    nkz = n_in // tkz
    dh1 = _mm(
        "mm_dh1", (s // tm, d // tn, nkz), [dz, g_in],
        [pl.BlockSpec((tm, tkz), lambda i, j, k: (i, k)), pl.BlockSpec((tn, tkz), lambda i, j, k: (j, k))],
        [jax.ShapeDtypeStruct((s, d), F32)], [pl.BlockSpec((tm, tn), lambda i, j, k: (i, j))],
        [(0, 1, NT, 0)], 1, (tm, tn), nkz, ep_f32,
        _mm_vmem([((tm, tkz), BF16, 2), ((tn, tkz), BF16, 2), ((tm, tn), F32, 5)]),
        after=(red_in["token"],),
    )[0]

    reduce_b(red_in, dh1)
    grad_x, _, dg_mix = _rms_bwd("rms_mix_bwd", x2d, norm_mix, dh1, dx2, after=(red_in["token"],))

    p_mix = _pack([dg_mix.reshape(small_shapes[0])])
    land_mix = jnp.zeros((2 * N_CHIPS,) + p_mix.shape, F32)
    mx_send, mx_recv, (p_mix, land_mix) = _split_start("mix_send", [p_mix, land_mix], 2 * N_CHIPS - 1, _small_exchange_copies(False))

    grads_big, upd = [None] * 7, [None] * 7

    def finish(st, after):
        for i, g in zip(st["ids"], reduce_d(st, after)):
            upd[i] = _adamw("adamw_" + names[i], big_w[i], g, big_m[i], big_v[i])
            grads_big[i] = upd[i][3]
            after = upd[i][0]
        return after

    after, prev = p_mix, None
    for st in (red_down, red_ffn, red_o, red_mix, red_in):
        reduce_c(st, after)
        after = st["token"] if prev is None else finish(prev, st["token"])
        prev = st

    p_early, land = _split_wait("small_wait", [p_early, land], sm_send, sm_recv, _small_exchange_copies(True), after)
    p_mix, land_mix = _split_wait("mix_wait", [p_mix, land_mix], mx_send, mx_recv, _small_exchange_copies(True), p_early)
    me_idx = 2 * q_idx + c_idx
    packed_g = jnp.concatenate([_small_sum("mix_sum", me_idx, p_mix, land_mix), _small_sum("small_sum", me_idx, p_early, land)], axis=0)
    g_small = _unpack(packed_g, small_shapes + [(1,)])
    loss = g_small[-1].reshape(())
    g_small = g_small[:-1]
    zero1 = jnp.zeros((1,), F32)
    pw, pg, pm, pv = _pack(small_w + [zero1]), _pack(g_small + [zero1]), _pack(small_m + [zero1]), _pack(small_v + [zero1])
    small_upd = _adamw("adamw_small", pw, pg, pm, pv)
    d_small, nm_small, nv_small = [_unpack(a, small_shapes) for a in small_upd[:3]]
    finish(prev, small_upd[0])

    small_names = ["norm_mix", "sgu_v_gain", "sgu_w_s", "sgu_b_s", "attn_sink", "rel_bias", "norm_ffn", "norm_final"]
    table = {}
    for i, n in enumerate(names):
        table[n] = (grads_big[i][None], upd[i][0][None], upd[i][1][None], upd[i][2][None])
    for i, n in enumerate(small_names):
        table[n] = (g_small[i], d_small[i], nm_small[i], nv_small[i])
    order = ["w_in", "norm_mix", "sgu_v_gain", "sgu_w_s", "sgu_b_s", "w_a", "attn_sink", "rel_bias", "w_b", "w_o", "norm_ffn",
             "w_gate", "w_up", "w_down", "norm_final"]
    outs = [loss, grad_x.reshape(1, s, d)]
    for part in range(4):
        outs += [table[n][part] for n in order]
    return tuple(outs)
```

```python
import math

import jax
import jax.numpy as jnp
import numpy as np
from jax import lax
from jax.experimental import pallas as pl
from jax.experimental.pallas import tpu as pltpu

F32 = jnp.float32
BF16 = jnp.bfloat16
I32 = jnp.int32
MESH = pl.DeviceIdType.MESH

EPS = 1e-6
NEG = -1e30
BLK = 128
HEAD_DIM = 128
N_KV_HEADS = 2
REL_BUCKETS = 32
REL_MAX_DIST = 128
N_CHIPS = 4
ADAM_LR, ADAM_B1, ADAM_B2, ADAM_EPS, ADAM_WD, ADAM_STEP = 0.001, 0.9, 0.999, 1e-08, 0.01, 10

LANES = 128
MXU_COLS = 256
VMEM_CAP = 60 * 1024 * 1024

NN = (((1,), (0,)), ((), ()))
NT = (((1,), (1,)), ((), ()))
TN = (((0,), (0,)), ((), ()))
ANY = pl.BlockSpec(memory_space=pl.ANY)
HBM_SPEC = pl.BlockSpec(memory_space=pltpu.HBM)
SEM_SPEC = pl.BlockSpec(memory_space=pltpu.SEMAPHORE)
EFFECT = pltpu.SideEffectType.DATAFLOW_SIDE_EFFECTING


def _tile(n, cands):
    for t in cands:
        if n % t == 0:
            return t
    return n


def _params(vmem_bytes=None, **kw):
    if vmem_bytes is not None:
        kw["vmem_limit_bytes"] = int(min(max(vmem_bytes, 32 * 1024 * 1024), VMEM_CAP))
    return pltpu.CompilerParams(**kw)


def _nbytes(shape, dtype):
    return int(np.prod(shape)) * jnp.dtype(dtype).itemsize


def _sigmoid(x):
    return 1.0 / (1.0 + jnp.exp(-x))


_GC = 0.7978845608028654
_GA = 0.044715


def _gelu(x):
    return 0.5 * x * (1.0 + jnp.tanh(_GC * (x + _GA * (x * x * x))))


def _gelu_grad(x):
    t = jnp.tanh(_GC * (x + _GA * (x * x * x)))
    return 0.5 * (1.0 + t) + 0.5 * x * (1.0 - t * t) * (_GC * (1.0 + 3.0 * _GA * (x * x)))


def _bf(v):
    return v if v.dtype == BF16 else v.astype(BF16)


def _mm(name, grid, ins, in_specs, out_shape, out_specs, pairs, n_acc, tile, nk, epilogue, vmem_bytes, after=(), col_chunks=1):
    assert nk == 1 and tile[1] % col_chunks == 0
    n_in, n_out = len(ins) + len(after), len(out_shape)
    width = tile[1] // col_chunks

    def body(*refs):
        in_refs, out_refs = refs[:n_in], refs[n_in : n_in + n_out]
        for ch in range(col_chunks):
            cs = slice(ch * width, (ch + 1) * width) if col_chunks > 1 else slice(None)
            vals = [None] * n_acc
            for a_i, b_i, dn, acc_i in pairs:
                rhs = in_refs[b_i][cs, :] if dn == NT else in_refs[b_i][:, cs]
                d = lax.dot_general(_bf(in_refs[a_i][...]), _bf(rhs), dn, preferred_element_type=F32)
                vals[acc_i] = d if vals[acc_i] is None else vals[acc_i] + d
            epilogue(in_refs, vals, out_refs, cs)

    return pl.pallas_call(
        body,
        name=name,
        grid=grid,
        in_specs=list(in_specs) + [ANY] * len(after),
        out_specs=out_specs,
        out_shape=out_shape,
        compiler_params=_params(vmem_bytes),
    )(*ins, *after)


def _put(ref, cs, v):
    ref[:, cs] = v.astype(ref.dtype)


def _mm_vmem(tiles):
    return sum(_nbytes(s, d) * c for s, d, c in tiles) + 4 * 1024 * 1024


def _rows8(v):
    r, d = v.shape
    return v.reshape(r // 8, 8, d).sum(axis=0)


def _rms_fwd(name, x, g, after=()):
    s, d = x.shape
    tm = _tile(s, (256, 128))

    def body(x_ref, g_ref, *rest):
        h_ref = rest[-1]
        xv = x_ref[...]
        r = lax.rsqrt(jnp.mean(xv * xv, axis=-1, keepdims=True) + EPS)
        h_ref[...] = ((xv * r) * g_ref[...]).astype(BF16)

    return pl.pallas_call(
        body,
        name=name,
        grid=(s // tm,),
        in_specs=[pl.BlockSpec((tm, d), lambda i: (i, 0)), pl.BlockSpec((1, d), lambda i: (0, 0))] + [ANY] * len(after),
        out_specs=pl.BlockSpec((tm, d), lambda i: (i, 0)),
        out_shape=jax.ShapeDtypeStruct((s, d), BF16),
    )(x, g, *after)


def _rms_bwd(name, x, g, dh, dres, after=()):
    s, d = x.shape
    tm = _tile(s, (256, 128))
    n = s // tm
    n_after = len(after)

    def body(x_ref, g_ref, dh_ref, dres_ref, *rest):
        dx_ref, dxb_ref, dg_ref, acc_ref = rest[n_after:]
        i = pl.program_id(0)
        xv = x_ref[...]
        r = lax.rsqrt(jnp.mean(xv * xv, axis=-1, keepdims=True) + EPS)
        xh = xv * r
        dhv = dh_ref[...]
        dxh = dhv * g_ref[...]
        dx = r * (dxh - xh * jnp.mean(dxh * xh, axis=-1, keepdims=True)) + dres_ref[...]
        dx_ref[...] = dx
        dxb_ref[...] = dx.astype(BF16)
        part = _rows8(dhv * xh)

        @pl.when(i == 0)
        def _():
            acc_ref[...] = part

        @pl.when(i > 0)
        def _():
            acc_ref[...] += part

        @pl.when(i == n - 1)
        def _():
            dg_ref[...] = jnp.sum(acc_ref[...], axis=0, keepdims=True)

    row = pl.BlockSpec((tm, d), lambda i: (i, 0))
    vec = pl.BlockSpec((1, d), lambda i: (0, 0))
    return pl.pallas_call(
        body,
        name=name,
        grid=(n,),
        in_specs=[row, vec, row, row] + [ANY] * n_after,
        out_specs=[row, row, vec],
        out_shape=[jax.ShapeDtypeStruct((s, d), F32), jax.ShapeDtypeStruct((s, d), BF16), jax.ShapeDtypeStruct((1, d), F32)],
        scratch_shapes=[pltpu.VMEM((8, d), F32)],
    )(x, g, dh, dres, *after)


def _head(x3, g, target):
    s, d = x3.shape
    tm = _tile(s, (256, 128))
    n = s // tm

    def body(x_ref, g_ref, t_ref, dx_ref, dxb_ref, dg_ref, loss_ref, acc_g, acc_l):
        i = pl.program_id(0)
        xv = x_ref[...]
        gv = g_ref[...]
        r = lax.rsqrt(jnp.mean(xv * xv, axis=-1, keepdims=True) + EPS)
        xh = xv * r
        e = xh * gv - t_ref[...]
        dy = e * (1.0 / d)
        dxh = dy * gv
        dx = r * (dxh - xh * jnp.mean(dxh * xh, axis=-1, keepdims=True))
        dx_ref[...] = dx
        dxb_ref[...] = dx.astype(BF16)
        pg = _rows8(dy * xh)
        plo = _rows8(e * e)

        @pl.when(i == 0)
        def _():
            acc_g[...] = pg
            acc_l[...] = plo

        @pl.when(i > 0)
        def _():
            acc_g[...] += pg
            acc_l[...] += plo

        @pl.when(i == n - 1)
        def _():
            dg_ref[...] = jnp.sum(acc_g[...], axis=0, keepdims=True)
            loss_ref[...] = jnp.full((1, LANES), (0.5 / d) * jnp.sum(acc_l[...]), F32)

    row = pl.BlockSpec((tm, d), lambda i: (i, 0))
    vec = pl.BlockSpec((1, d), lambda i: (0, 0))
    return pl.pallas_call(
        body,
        name="head",
        grid=(n,),
        in_specs=[row, vec, row],
        out_specs=[row, row, vec, pl.BlockSpec((1, LANES), lambda i: (0, 0))],
        out_shape=[
            jax.ShapeDtypeStruct((s, d), F32),
            jax.ShapeDtypeStruct((s, d), BF16),
            jax.ShapeDtypeStruct((1, d), F32),
            jax.ShapeDtypeStruct((1, LANES), F32),
        ],
        scratch_shapes=[pltpu.VMEM((8, d), F32), pltpu.VMEM((8, d), F32)],
    )(x3, g, target)


def _sgu_fwd(z, gain, ws_b, b_col, w_sgu):
    s = z.shape[0]
    groups = ws_b.shape[0]

    def body(zu_ref, zv_ref, gain_ref, ws_ref, b_ref, a_ref):
        vv = _gelu(zv_ref[...].astype(F32))
        r = lax.rsqrt(jnp.mean(vv * vv, axis=-1, keepdims=True) + EPS)
        vn = ((vv * r) * gain_ref[...]).astype(BF16)
        u = _gelu(zu_ref[...].astype(F32))
        for g in range(groups):
            sl = slice(g * BLK, (g + 1) * BLK)
            mixed = jnp.dot(ws_ref[g], vn[:, sl], preferred_element_type=F32) + b_ref[g]
            a_ref[:, sl] = (u[:, sl] * mixed).astype(BF16)

    return pl.pallas_call(
        body,
        name="sgu_fwd",
        grid=(s // BLK,),
        in_specs=[
            pl.BlockSpec((BLK, w_sgu), lambda c: (c, 0)),
            pl.BlockSpec((BLK, w_sgu), lambda c: (c, 1)),
            pl.BlockSpec((1, w_sgu), lambda c: (0, 0)),
            pl.BlockSpec((groups, BLK, BLK), lambda c: (0, 0, 0)),
            pl.BlockSpec((groups, BLK, 1), lambda c: (0, 0, 0)),
        ],
        out_specs=pl.BlockSpec((BLK, w_sgu), lambda c: (c, 0)),
        out_shape=jax.ShapeDtypeStruct((s, w_sgu), BF16),
    )(z, z, gain, ws_b, b_col)


def _sgu_bwd(z, da, gain, ws_b, wst_b, b_col, w_sgu, dz, after=()):
    s = z.shape[0]
    groups = ws_b.shape[0]
    n = s // BLK
    n_skip = 1 + len(after)

    def body(zu_ref, zv_ref, da_ref, gain_ref, ws_ref, wst_ref, b_ref, *rest):
        dz_ref, dws_ref, dbs_ref, dgain_ref, acc_gain = rest[n_skip:]
        c = pl.program_id(0)
        zu = zu_ref[...].astype(F32)
        zv = zv_ref[...].astype(F32)
        gain_v = gain_ref[...]
        vv = _gelu(zv)
        r = lax.rsqrt(jnp.mean(vv * vv, axis=-1, keepdims=True) + EPS)
        xh = vv * r
        vn = (xh * gain_v).astype(BF16)
        u = _gelu(zu)
        dav = da_ref[...].astype(F32)
        dmix = dav * u
        dmix_b = dmix.astype(BF16)
        dvn_parts = []
        for g in range(groups):
            sl = slice(g * BLK, (g + 1) * BLK)
            mixed = jnp.dot(ws_ref[g], vn[:, sl], preferred_element_type=F32) + b_ref[g]
            dz_ref[:, sl] = (dav[:, sl] * mixed * _gelu_grad(zu[:, sl])).astype(BF16)
            dvn_parts.append(jnp.dot(wst_ref[g], dmix_b[:, sl], preferred_element_type=F32))
            dws_g = lax.dot_general(dmix_b[:, sl], vn[:, sl], NT, preferred_element_type=F32)
            dbs_g = jnp.sum(dmix[:, sl], axis=1, keepdims=True)

            @pl.when(c == 0)
            def _():
                dws_ref[g] = dws_g
                dbs_ref[g] = dbs_g

            @pl.when(c > 0)
            def _():
                dws_ref[g] += dws_g
                dbs_ref[g] += dbs_g

        dvn = jnp.concatenate(dvn_parts, axis=1)
        dxh = dvn * gain_v
        dvv = r * (dxh - xh * jnp.mean(dxh * xh, axis=-1, keepdims=True))
        dz_ref[:, w_sgu:] = (dvv * _gelu_grad(zv)).astype(BF16)
        pg = _rows8(dvn * xh)

        @pl.when(c == 0)
        def _():
            acc_gain[...] = pg

        @pl.when(c > 0)
        def _():
            acc_gain[...] += pg

        @pl.when(c == n - 1)
        def _():
            dgain_ref[...] = jnp.sum(acc_gain[...], axis=0, keepdims=True)

    full3 = pl.BlockSpec((groups, BLK, BLK), lambda c: (0, 0, 0))
    col3 = pl.BlockSpec((groups, BLK, 1), lambda c: (0, 0, 0))
    vec = pl.BlockSpec((1, w_sgu), lambda c: (0, 0))
    return pl.pallas_call(
        body,
        name="sgu_bwd",
        grid=(n,),
        in_specs=[
            pl.BlockSpec((BLK, w_sgu), lambda c: (c, 0)),
            pl.BlockSpec((BLK, w_sgu), lambda c: (c, 1)),
            pl.BlockSpec((BLK, w_sgu), lambda c: (c, 0)),
            vec,
            full3,
            full3,
            col3,
            ANY,
        ]
        + [ANY] * len(after),
        out_specs=[pl.BlockSpec((BLK, 2 * w_sgu), lambda c: (c, 0)), full3, col3, vec],
        out_shape=[
            jax.ShapeDtypeStruct(dz.shape, BF16),
            jax.ShapeDtypeStruct((groups, BLK, BLK), F32),
            jax.ShapeDtypeStruct((groups, BLK, 1), F32),
            jax.ShapeDtypeStruct((1, w_sgu), F32),
        ],
        scratch_shapes=[pltpu.VMEM((8, w_sgu), F32)],
        input_output_aliases={7: 0},
    )(z, z, da, gain, ws_b, wst_b, b_col, dz, *after)


def _attn_softmax(sink_ref, q_ref, k_ref, v_ref, bias_ref, s_len, grp):
    kv = pl.program_id(0)
    n = pl.program_id(1)
    start = pl.multiple_of(n * BLK, BLK)
    kb = k_ref[pl.ds(start, 3 * BLK), :]
    vb = v_ref[pl.ds(start, 3 * BLK), :]
    qv = q_ref[...]
    qs = jnp.concatenate([qv[:, g * HEAD_DIM : (g + 1) * HEAD_DIM] for g in range(grp)], axis=0).astype(BF16)
    sc = lax.dot_general(qs, kb, NT, preferred_element_type=F32) * (HEAD_DIM**-0.5)
    sc = sc + bias_ref[...].reshape(grp * BLK, 3 * BLK)
    kpos = start + lax.broadcasted_iota(I32, (1, 3 * BLK), 1) - BLK
    sc = jnp.where((kpos >= 0) & (kpos < s_len), sc, NEG)
    sink = jnp.concatenate([jnp.full((BLK, 1), sink_ref[kv * grp + g], F32) for g in range(grp)], axis=0)
    m = jnp.maximum(jnp.max(sc, axis=-1, keepdims=True), sink)
    p = jnp.exp(sc - m)
    esink = jnp.exp(sink - m)
    den = jnp.sum(p, axis=-1, keepdims=True) + esink
    return start, qs, kb, vb, p / den, esink / den


def _attn_specs(s, grp, q_blk0):
    qw = grp * HEAD_DIM
    return [
        pl.BlockSpec(memory_space=pltpu.SMEM),
        pl.BlockSpec((BLK, qw), lambda kv, n: (n, q_blk0 + kv)),
        pl.BlockSpec((s + 2 * BLK, HEAD_DIM), lambda kv, n: (0, kv)),
        pl.BlockSpec((s + 2 * BLK, HEAD_DIM), lambda kv, n: (0, kv)),
        pl.BlockSpec((grp, BLK, 3 * BLK), lambda kv, n: (kv, 0, 0)),
    ]


def _attn_fwd(sink, z, k_pad, v_pad, bias_tab, grp, q_blk0):
    s = z.shape[0]
    qw = grp * HEAD_DIM

    def body(sink_ref, q_ref, k_ref, v_ref, bias_ref, o_ref):
        _, _, _, vb, pn, _ = _attn_softmax(sink_ref, q_ref, k_ref, v_ref, bias_ref, s, grp)
        o = jnp.dot(pn.astype(BF16), vb, preferred_element_type=F32)
        for g in range(grp):
            o_ref[:, g * HEAD_DIM : (g + 1) * HEAD_DIM] = o[g * BLK : (g + 1) * BLK].astype(BF16)

    return pl.pallas_call(
        body,
        name="attn_fwd",
        grid=(N_KV_HEADS, s // BLK),
        in_specs=_attn_specs(s, grp, q_blk0),
        out_specs=pl.BlockSpec((BLK, qw), lambda kv, n: (n, kv)),
        out_shape=jax.ShapeDtypeStruct((s, N_KV_HEADS * qw), BF16),
    )(sink, z, k_pad, v_pad, bias_tab)


def _attn_bwd(sink, z, k_pad, v_pad, bias_tab, dout, dz, grp, q_blk0):
    s = z.shape[0]
    qw = grp * HEAD_DIM
    nb = s // BLK
    heads = N_KV_HEADS * grp

    def body(sink_ref, q_ref, k_ref, v_ref, bias_ref, do_ref, dz_in, dq_ref, dk_ref, dv_ref, dbias_ref, dsink_ref, dk_acc, dv_acc):
        del dz_in
        kv = pl.program_id(0)
        n = pl.program_id(1)
        start, qs, kb, vb, pn, psink = _attn_softmax(sink_ref, q_ref, k_ref, v_ref, bias_ref, s, grp)
        dov = do_ref[...]
        dos = jnp.concatenate([dov[:, g * HEAD_DIM : (g + 1) * HEAD_DIM] for g in range(grp)], axis=0)
        dp = lax.dot_general(dos, vb, NT, preferred_element_type=F32)
        dvb = lax.dot_general(pn.astype(BF16), dos, TN, preferred_element_type=F32)
        delta = jnp.sum(pn * dp, axis=-1, keepdims=True)
        ds = pn * (dp - delta)
        dsb = (ds * (HEAD_DIM**-0.5)).astype(BF16)
        dq = jnp.dot(dsb, kb, preferred_element_type=F32)
        dkb = lax.dot_general(dsb, qs, TN, preferred_element_type=F32)
        for g in range(grp):
            dq_ref[:, g * HEAD_DIM : (g + 1) * HEAD_DIM] = dq[g * BLK : (g + 1) * BLK].astype(BF16)

        @pl.when(n == 0)
        def _():
            dk_acc[...] = jnp.zeros_like(dk_acc)
            dv_acc[...] = jnp.zeros_like(dv_acc)
            dbias_ref[...] = jnp.zeros_like(dbias_ref)

        @pl.when((n == 0) & (kv == 0))
        def _():
            dsink_ref[...] = jnp.zeros_like(dsink_ref)

        dk_acc[pl.ds(start, 3 * BLK), :] += dkb
        dv_acc[pl.ds(start, 3 * BLK), :] += dvb
        dbias_ref[...] += ds.reshape(grp, BLK, 3 * BLK)
        row = lax.broadcasted_iota(I32, (heads, LANES), 0)
        sd = psink * delta
        upd = jnp.zeros((heads, LANES), F32)
        for g in range(grp):
            upd = jnp.where(row == kv * grp + g, -jnp.sum(sd[g * BLK : (g + 1) * BLK]), upd)
        dsink_ref[...] += upd

        @pl.when(n == nb - 1)
        def _():
            dk_ref[...] = dk_acc[...]
            dv_ref[...] = dv_acc[...]

    pad_spec = pl.BlockSpec((s + 2 * BLK, HEAD_DIM), lambda kv, n: (0, kv))
    kvw = N_KV_HEADS * HEAD_DIM
    return pl.pallas_call(
        body,
        name="attn_bwd",
        grid=(N_KV_HEADS, nb),
        in_specs=_attn_specs(s, grp, q_blk0) + [pl.BlockSpec((BLK, qw), lambda kv, n: (n, kv)), ANY],
        out_specs=[
            pl.BlockSpec((BLK, qw), lambda kv, n: (n, q_blk0 + kv)),
            pad_spec,
            pad_spec,
            pl.BlockSpec((grp, BLK, 3 * BLK), lambda kv, n: (kv, 0, 0)),
            pl.BlockSpec((heads, LANES), lambda kv, n: (0, 0)),
        ],
        out_shape=[
            jax.ShapeDtypeStruct(dz.shape, BF16),
            jax.ShapeDtypeStruct((s + 2 * BLK, kvw), F32),
            jax.ShapeDtypeStruct((s + 2 * BLK, kvw), F32),
            jax.ShapeDtypeStruct((heads, BLK, 3 * BLK), F32),
            jax.ShapeDtypeStruct((heads, LANES), F32),
        ],
        scratch_shapes=[pltpu.VMEM((s + 2 * BLK, HEAD_DIM), F32), pltpu.VMEM((s + 2 * BLK, HEAD_DIM), F32)],
        input_output_aliases={6: 0},
    )(sink, z, k_pad, v_pad, bias_tab, dout, dz)


def _dkv_to_dz(dk_pad, dv_pad, dz, blk_idx):
    s = dz.shape[0]
    kvw = dk_pad.shape[1]

    def body(dk_ref, dv_ref, dz_in, out_ref):
        del dz_in
        out_ref[:, :kvw] = dk_ref[...].astype(BF16)
        out_ref[:, kvw:] = dv_ref[...].astype(BF16)

    src = pl.BlockSpec((BLK, kvw), lambda i: (i + 1, 0))
    return pl.pallas_call(
        body,
        name="dkv_to_dz",
        grid=(s // BLK,),
        in_specs=[src, src, ANY],
        out_specs=pl.BlockSpec((BLK, 2 * kvw), lambda i: (i, blk_idx)),
        out_shape=jax.ShapeDtypeStruct(dz.shape, BF16),
        input_output_aliases={2: 0},
    )(dk_pad, dv_pad, dz)


def _relbias_bwd(dbias_tab, bucket):
    heads = dbias_tab.shape[0]

    def body(dt_ref, bk_ref, out_ref):
        lane = lax.broadcasted_iota(I32, (1, LANES), 1)
        bk = bk_ref[...]
        rows = []
        for h in range(heads):
            dt = dt_ref[h]
            acc = jnp.zeros((1, LANES), F32)
            for b in range(REL_BUCKETS):
                acc = jnp.where(lane == b, jnp.sum(jnp.where(bk == b, dt, 0.0)), acc)
            rows.append(acc)
        out_ref[...] = jnp.concatenate(rows, axis=0)

    return pl.pallas_call(body, name="relbias_bwd", out_shape=jax.ShapeDtypeStruct((heads, LANES), F32))(dbias_tab, bucket)


def _t5_bucket(rel):
    nb = REL_BUCKETS // 2
    ret = jnp.where(rel > 0, nb, 0)
    n = jnp.abs(rel)
    max_exact = nb // 2
    nf = jnp.maximum(n, 1).astype(F32)
    large = max_exact + (jnp.log(nf / max_exact) / math.log(REL_MAX_DIST / max_exact) * (nb - max_exact)).astype(I32)
    large = jnp.minimum(large, nb - 1)
    return ret + jnp.where(n < max_exact, n, large)


def _band_tables(rel_bias):
    qi = jnp.arange(BLK)[:, None]
    kj = jnp.arange(3 * BLK)[None, :]
    rel = kj - BLK - qi
    bucket = _t5_bucket(rel).astype(I32)
    heads = rel_bias.shape[1]
    masked = jnp.where(jnp.abs(rel) <= BLK, bucket, -1)

    def body(rb_ref, bk_ref, out_ref):
        bk = bk_ref[...]
        for h in range(heads):
            tab = jnp.full(bk.shape, NEG, F32)
            for b in range(REL_BUCKETS):
                tab = jnp.where(bk == b, rb_ref[b, h], tab)
            out_ref[h] = tab

    bias_tab = pl.pallas_call(
        body,
        name="bias_table",
        in_specs=[pl.BlockSpec(memory_space=pltpu.SMEM), pl.BlockSpec(memory_space=pltpu.VMEM)],
        out_specs=pl.BlockSpec(memory_space=pltpu.VMEM),
        out_shape=jax.ShapeDtypeStruct((heads, BLK, 3 * BLK), F32),
    )(rel_bias.astype(F32), masked)
    return bias_tab, bucket


EW_BLOCK_ELEMS = 512 * 1024


def _ew_tiles(shape, elems=EW_BLOCK_ELEMS // 2):
    r, c = shape
    tn = c if c <= 2048 else _tile(c, (2048, 1920, 1536, 1408, 1024, 512))
    tm = _tile(r, [t for t in (1024, 512, 256, 128, 64, 32, 16, 8) if t * tn <= elems] or [8])
    return tm, tn


def _cast_into_full(name, qidx, w, kind, after=()):
    r, c = w.shape
    tm, tn = _ew_tiles(w.shape, EW_BLOCK_ELEMS)
    nbi, nbj = r // tm, c // tn
    if kind == "col":
        full, out_spec = (r, c * N_CHIPS), pl.BlockSpec((tm, tn), lambda i, j, q: (i, q[0] * nbj + j))
    else:
        full, out_spec = (r * N_CHIPS, c), pl.BlockSpec((tm, tn), lambda i, j, q: (q[0] * nbi + i, j))

    def body(q_ref, w_ref, *rest):
        del q_ref
        rest[-1][...] = w_ref[...].astype(BF16)

    return pl.pallas_call(
        body,
        name=name,
        grid_spec=pltpu.PrefetchScalarGridSpec(
            num_scalar_prefetch=1,
            grid=(nbi, nbj),
            in_specs=[pl.BlockSpec((tm, tn), lambda i, j, q: (i, j))] + [ANY] * len(after),
            out_specs=out_spec,
        ),
        out_shape=jax.ShapeDtypeStruct(full, BF16),
    )(qidx, w, *after)


def _adamw(name, w, g, m, v, after=()):
    tm, tn = _ew_tiles(w.shape)
    if _nbytes(w.shape, F32) <= 1024 * 1024:
        tm, tn = w.shape
    spec = pl.BlockSpec((tm, tn), lambda i, j: (i, j))
    n_after = len(after)

    def body(w_ref, g_ref, m_ref, v_ref, *rest):
        d_ref, nm_ref, nv_ref, g_out_ref = rest[n_after:]
        gv = g_ref[...]
        g_out_ref[...] = gv
        nm = ADAM_B1 * m_ref[...] + (1.0 - ADAM_B1) * gv
        nv = ADAM_B2 * v_ref[...] + (1.0 - ADAM_B2) * (gv * gv)
        m_hat = nm / (1.0 - ADAM_B1**ADAM_STEP)
        v_hat = nv / (1.0 - ADAM_B2**ADAM_STEP)
        d_ref[...] = -ADAM_LR * (m_hat / (jnp.sqrt(v_hat) + ADAM_EPS) + ADAM_WD * w_ref[...])
        nm_ref[...] = nm
        nv_ref[...] = nv

    out = jax.ShapeDtypeStruct(w.shape, F32)
    return pl.pallas_call(
        body, name=name, grid=(w.shape[0] // tm, w.shape[1] // tn), in_specs=[spec] * 4 + [ANY] * n_after,
        out_specs=[spec] * 4, out_shape=[out, out, out, out],
    )(w, g, m, v, *after)


def _pair_add(name, cidx, g_full, r_sib, kind):
    hr, hc = r_sib.shape
    tm, tn = _ew_tiles((hr, hc), EW_BLOCK_ELEMS)
    nbi, nbj = hr // tm, hc // tn
    if kind == "col":
        g_spec = pl.BlockSpec((tm, tn), lambda i, j, c: (c[0] * nbi + i, j))
    else:
        g_spec = pl.BlockSpec((tm, tn), lambda i, j, c: (i, c[0] * nbj + j))
    spec = pl.BlockSpec((tm, tn), lambda i, j, c: (i, j))

    def body(c_ref, g_ref, r_ref, o_ref):
        del c_ref
        o_ref[...] = (g_ref[...].astype(F32) + r_ref[...].astype(F32)).astype(BF16)

    return pl.pallas_call(
        body,
        name=name,
        grid_spec=pltpu.PrefetchScalarGridSpec(num_scalar_prefetch=1, grid=(nbi, nbj), in_specs=[g_spec, spec], out_specs=spec),
        out_shape=jax.ShapeDtypeStruct((hr, hc), BF16),
    )(cidx, g_full, r_sib)


def _chip_sum(name, qidx, c_half, r_ici, kind):
    _, pr, pc = r_ici.shape
    tm, tn = _ew_tiles((pr, pc), EW_BLOCK_ELEMS)
    nbi, nbj = pr // tm, pc // tn
    if kind == "col":
        own_spec = pl.BlockSpec((tm, tn), lambda i, j, q: (i, q[0] * nbj + j))
        full, out_spec = (2 * pr, pc), pl.BlockSpec((tm, tn), lambda i, j, q: (q[1] * nbi + i, j))
    else:
        own_spec = pl.BlockSpec((tm, tn), lambda i, j, q: (q[0] * nbi + i, j))
        full, out_spec = (pr, 2 * pc), pl.BlockSpec((tm, tn), lambda i, j, q: (i, q[1] * nbj + j))

    def body(q_ref, own_ref, r_ref, o_ref):
        q = q_ref[0]
        own = own_ref[...].astype(F32)
        recv = [r_ref[r].astype(F32) for r in range(3)]
        total = None
        for chip in range(N_CHIPS):
            d = chip ^ q
            term = jnp.where(d == 0, own, jnp.where(d == 2, recv[0], jnp.where(d == 1, recv[1], recv[2])))
            total = term if total is None else total + term
        o_ref[...] = total

    return pl.pallas_call(
        body,
        name=name,
        grid_spec=pltpu.PrefetchScalarGridSpec(
            num_scalar_prefetch=1,
            grid=(nbi, nbj),
            in_specs=[own_spec, pl.BlockSpec((3, tm, tn), lambda i, j, q: (0, i, j))],
            out_specs=out_spec,
        ),
        out_shape=jax.ShapeDtypeStruct(full, F32),
    )(qidx, c_half, r_ici)


_REL_MASK = (2, 1, 3)


def _place():
    x, y, c = lax.axis_index("x"), lax.axis_index("y"), lax.axis_index("c")
    chips = [(1 - x, y), (x, 1 - y), (1 - x, 1 - y)]
    return x, y, c, 2 * x + y, chips


def _shard_view(ref, kind, chip):
    if kind == "col":
        w = ref.shape[1] // N_CHIPS
        return ref.at[:, pl.ds(pl.multiple_of(chip * w, LANES), w)]
    h = ref.shape[0] // N_CHIPS
    return ref.at[pl.ds(pl.multiple_of(chip * h, 16), h), :]


def _row_half(ref, half):
    h = ref.shape[0] // 2
    return ref.at[pl.ds(pl.multiple_of(half * h, 16), h), :]


def _pair_half(ref, kind, half):
    if kind == "col":
        return _row_half(ref, half)
    w = ref.shape[1] // 2
    return ref.at[:, pl.ds(pl.multiple_of(half * w, LANES), w)]


def _remote(src, dst, send_sem, recv_sem, dev):
    return pltpu.make_async_remote_copy(src_ref=src, dst_ref=dst, send_sem=send_sem, recv_sem=recv_sem, device_id=dev, device_id_type=MESH)


def _hbm(a):
    return pltpu.with_memory_space_constraint(a, pltpu.HBM)


def _gather_start(name, fulls, kinds):
    n_w = len(fulls)

    def body(*refs):
        g = refs[:n_w]
        send_sem, recv_sem = refs[n_w], refs[n_w + 1]
        token = refs[-1]
        _, _, c, q, chips = _place()
        for w in range(n_w):
            mine = _row_half(_shard_view(g[w], kinds[w], q), c)
            for r, chip in enumerate(chips):
                _remote(mine, mine, send_sem.at[3 * w + r], recv_sem.at[3 * w + r], (*chip, c)).start()
        token[...] = jnp.zeros_like(token)

    res = pl.pallas_call(
        body,
        name=name,
        out_shape=(
            pltpu.SemaphoreType.DMA((3 * n_w,)),
            pltpu.SemaphoreType.DMA((3 * n_w,)),
            *[pltpu.HBM(f.shape, f.dtype) for f in fulls],
            jax.ShapeDtypeStruct((8, LANES), F32),
        ),
        in_specs=[HBM_SPEC] * n_w,
        out_specs=(SEM_SPEC, SEM_SPEC, *[HBM_SPEC] * n_w, pl.BlockSpec(memory_space=pltpu.VMEM)),
        input_output_aliases={w: w + 2 for w in range(n_w)},
        compiler_params=pltpu.CompilerParams(has_side_effects=EFFECT),
    )(*[_hbm(f) for f in fulls])
    return res[0], res[1], list(res[2 : 2 + n_w]), res[-1]


def _gather_wait(name, fulls, kinds, w_ids, send_sem, recv_sem, after):
    n = len(fulls)

    def body(*refs):
        g = refs[:n]
        s_sem, r_sem = refs[n], refs[n + 1]
        x, y, c, q, _ = _place()
        for i, w in enumerate(w_ids):
            mine = _row_half(_shard_view(g[i], kinds[i], q), c)
            for r in range(3):
                landed = _row_half(_shard_view(g[i], kinds[i], q ^ _REL_MASK[r]), c)
                cp = _remote(mine, landed, s_sem.at[3 * w + r], r_sem.at[3 * w + r], (x, y, 1 - c))
                cp.wait_send()
                cp.wait_recv()

    res = pl.pallas_call(
        body,
        name=name,
        out_shape=[pltpu.HBM(f.shape, f.dtype) for f in fulls],
        in_specs=[HBM_SPEC] * n + [SEM_SPEC, SEM_SPEC, ANY],
        out_specs=[HBM_SPEC] * n,
        input_output_aliases={i: i for i in range(n)},
        compiler_params=pltpu.CompilerParams(has_side_effects=EFFECT),
    )(*fulls, send_sem, recv_sem, after)
    return list(res)


def _gather_forward(name, fulls, kinds):
    n = len(fulls)

    def body(*refs):
        g = refs[n : 2 * n]
        send, recv = refs[2 * n :]
        x, y, c, q, _ = _place()
        sib = (x, y, 1 - c)
        cps = []
        for i in range(n):
            for r in range(3):
                landed = _row_half(_shard_view(g[i], kinds[i], q ^ _REL_MASK[r]), c)
                cps.append(_remote(landed, landed, send.at[i, r], recv.at[i, r], sib))
        for cp in cps:
            cp.start()
        for i in range(n):
            for r in range(3):
                other = _row_half(_shard_view(g[i], kinds[i], q ^ _REL_MASK[r]), 1 - c)
                _remote(other, other, send.at[i, r], recv.at[i, r], sib).wait_recv()
        for cp in cps:
            cp.wait_send()

    res = pl.pallas_call(
        body,
        name=name,
        in_specs=[ANY] * n,
        out_specs=[ANY] * n,
        out_shape=[jax.ShapeDtypeStruct(f.shape, f.dtype) for f in fulls],
        scratch_shapes=[pltpu.SemaphoreType.DMA((n, 3)), pltpu.SemaphoreType.DMA((n, 3))],
        input_output_aliases={i: i for i in range(n)},
    )(*fulls)
    return list(res)


def _split_start(name, bufs, n_sems, copies):
    n = len(bufs)

    def body(*refs):
        for cp in copies(refs[:n], refs[n], refs[n + 1]):
            cp.start()

    res = pl.pallas_call(
        body,
        name=name,
        out_shape=(
            pltpu.SemaphoreType.DMA((n_sems,)),
            pltpu.SemaphoreType.DMA((n_sems,)),
            *[pltpu.HBM(b.shape, b.dtype) for b in bufs],
        ),
        in_specs=[HBM_SPEC] * n,
        out_specs=(SEM_SPEC, SEM_SPEC, *[HBM_SPEC] * n),
        input_output_aliases={i: i + 2 for i in range(n)},
        compiler_params=pltpu.CompilerParams(has_side_effects=EFFECT),
    )(*[_hbm(b) for b in bufs])
    return res[0], res[1], list(res[2:])


def _split_wait(name, bufs, send_sem, recv_sem, copies, after):
    n = len(bufs)

    def body(*refs):
        for cp in copies(refs[:n], refs[n], refs[n + 1]):
            cp.wait_send()
            cp.wait_recv()

    res = pl.pallas_call(
        body,
        name=name,
        out_shape=[pltpu.HBM(b.shape, b.dtype) for b in bufs],
        in_specs=[HBM_SPEC] * n + [SEM_SPEC, SEM_SPEC, ANY],
        out_specs=[HBM_SPEC] * n,
        input_output_aliases={i: i for i in range(n)},
        compiler_params=pltpu.CompilerParams(has_side_effects=EFFECT),
    )(*bufs, send_sem, recv_sem, after)
    return list(res)


def _pair_exchange_copies(kinds):
    n = len(kinds)

    def copies(refs, send_sem, recv_sem):
        x, y, c, _, _ = _place()
        return [
            _remote(_pair_half(refs[w], kinds[w], 1 - c), refs[n + w], send_sem.at[w], recv_sem.at[w], (x, y, 1 - c))
            for w in range(n)
        ]

    return copies


def _pair_share_copies(kinds, waiting):
    def copies(refs, send_sem, recv_sem):
        x, y, c, _, _ = _place()
        out = []
        for w, kind in enumerate(kinds):
            mine = _pair_half(refs[w], kind, c)
            dst = _pair_half(refs[w], kind, 1 - c) if waiting else mine
            out.append(_remote(mine, dst, send_sem.at[w], recv_sem.at[w], (x, y, 1 - c)))
        return out

    return copies


def _piece_shape(half_shape, kind):
    r, c = half_shape
    return (3, r, c // N_CHIPS) if kind == "col" else (3, r // N_CHIPS, c)


def _chip_send_start(name, halves, kinds):
    n = len(halves)
    lands = [lax.empty(_piece_shape(h.shape, k), BF16) for h, k in zip(halves, kinds)]

    def body(*refs):
        h, land = refs[:n], refs[n : 2 * n]
        send_sem, recv_sem = refs[2 * n], refs[2 * n + 1]
        _, _, c, q, chips = _place()
        for i in range(n):
            for r, chip in enumerate(chips):
                piece = _shard_view(h[i], kinds[i], q ^ _REL_MASK[r])
                _remote(piece, land[i].at[r], send_sem.at[3 * i + r], recv_sem.at[3 * i + r], (*chip, c)).start()

    res = pl.pallas_call(
        body,
        name=name,
        out_shape=(
            pltpu.SemaphoreType.DMA((3 * n,)),
            pltpu.SemaphoreType.DMA((3 * n,)),
            *[pltpu.HBM(a.shape, a.dtype) for a in halves],
            *[pltpu.HBM(a.shape, a.dtype) for a in lands],
        ),
        in_specs=[HBM_SPEC] * (2 * n),
        out_specs=(SEM_SPEC, SEM_SPEC, *[HBM_SPEC] * (2 * n)),
        input_output_aliases={i: i + 2 for i in range(2 * n)},
        compiler_params=pltpu.CompilerParams(has_side_effects=EFFECT),
    )(*[_hbm(a) for a in halves], *[_hbm(a) for a in lands])
    return res[0], res[1], list(res[2 : 2 + n]), list(res[2 + n :])


def _chip_send_wait(name, halves, lands, kinds, send_sem, recv_sem, after):
    n = len(halves)

    def body(*refs):
        h, land = refs[:n], refs[n : 2 * n]
        s_sem, r_sem = refs[2 * n], refs[2 * n + 1]
        x, y, c, q, _ = _place()
        for i in range(n):
            for r in range(3):
                piece = _shard_view(h[i], kinds[i], q ^ _REL_MASK[r])
                cp = _remote(piece, land[i].at[r], s_sem.at[3 * i + r], r_sem.at[3 * i + r], (x, y, 1 - c))
                cp.wait_send()
                cp.wait_recv()

    res = pl.pallas_call(
        body,
        name=name,
        out_shape=[pltpu.HBM(a.shape, a.dtype) for a in halves] + [pltpu.HBM(a.shape, a.dtype) for a in lands],
        in_specs=[HBM_SPEC] * (2 * n) + [SEM_SPEC, SEM_SPEC, ANY],
        out_specs=[HBM_SPEC] * (2 * n),
        input_output_aliases={i: i for i in range(2 * n)},
        compiler_params=pltpu.CompilerParams(has_side_effects=EFFECT),
    )(*halves, *lands, send_sem, recv_sem, after)
    return list(res[:n]), list(res[n:])


def _small_all_reduce(p):
    rows = p.shape[0]
    n_dev = 2 * N_CHIPS

    def body(p_ref, o_ref, buf, loc_sem, send, recv):
        x, y, c, q, _ = _place()
        me = 2 * q + c
        own = pltpu.make_async_copy(p_ref, buf.at[me], loc_sem)
        own.start()
        cps = []
        for d in range(1, n_dev):
            dev = (x ^ ((d >> 2) & 1), y ^ ((d >> 1) & 1), c ^ (d & 1))
            cps.append(_remote(p_ref, buf.at[me], send.at[d - 1], recv.at[d - 1], dev))
        for cp in cps:
            cp.start()
        for d in range(1, n_dev):
            slot = buf.at[me ^ d]
            _remote(slot, slot, send.at[d - 1], recv.at[d - 1], (x, y, c)).wait_recv()
        own.wait()
        total = buf[0]
        for d in range(1, n_dev):
            total = total + buf[d]
        o_ref[...] = total
        for cp in cps:
            cp.wait_send()

    return pl.pallas_call(
        body,
        name="small_all_reduce",
        in_specs=[ANY],
        out_specs=pl.BlockSpec(memory_space=pltpu.VMEM),
        out_shape=jax.ShapeDtypeStruct(p.shape, F32),
        scratch_shapes=[
            pltpu.VMEM((n_dev, rows, LANES), F32),
            pltpu.SemaphoreType.DMA,
            pltpu.SemaphoreType.DMA((n_dev - 1,)),
            pltpu.SemaphoreType.DMA((n_dev - 1,)),
        ],
    )(p)


def _small_exchange_copies(waiting):
    def copies(refs, send_sem, recv_sem):
        p, land = refs
        x, y, c, q, _ = _place()
        me = 2 * q + c
        out = []
        for dd in range(1, 2 * N_CHIPS):
            dev = (x ^ ((dd >> 2) & 1), y ^ ((dd >> 1) & 1), c ^ (dd & 1))
            dst = land.at[me ^ dd] if waiting else land.at[me]
            out.append(_remote(p, dst, send_sem.at[dd - 1], recv_sem.at[dd - 1], dev))
        return out

    return copies


def _small_sum(name, me_idx, p, land):
    rows = p.shape[0]
    n_dev = 2 * N_CHIPS

    def body(me_ref, p_ref, land_ref, o_ref):
        me = me_ref[0]
        total = None
        for dev in range(n_dev):
            term = jnp.where(me == dev, p_ref[...], land_ref[dev])
            total = term if total is None else total + term
        o_ref[...] = total

    return pl.pallas_call(
        body,
        name=name,
        grid_spec=pltpu.PrefetchScalarGridSpec(
            num_scalar_prefetch=1,
            grid=(1,),
            in_specs=[pl.BlockSpec((rows, LANES), lambda i, m: (0, 0)), pl.BlockSpec((n_dev, rows, LANES), lambda i, m: (0, 0, 0))],
            out_specs=pl.BlockSpec((rows, LANES), lambda i, m: (0, 0)),
        ),
        out_shape=jax.ShapeDtypeStruct(p.shape, F32),
    )(me_idx, p, land)


def _pack(parts):
    rows = []
    for a in parts:
        flat = a.reshape(-1).astype(F32)
        n = flat.shape[0]
        padded = -(-n // (8 * LANES)) * (8 * LANES)
        rows.append(jnp.pad(flat, (0, padded - n)).reshape(-1, LANES))
    return jnp.concatenate(rows, axis=0)


def _unpack(packed, shapes):
    out, row = [], 0
    for shp in shapes:
        n = int(np.prod(shp))
        nrows = -(-n // (8 * LANES)) * 8
        out.append(packed[row : row + nrows].reshape(-1)[:n].reshape(shp))
        row += nrows
    return out


def kernel(x, w_in, norm_mix, sgu_v_gain, sgu_w_s, sgu_b_s, w_a_out, attn_sink, rel_bias, w_b_out, w_o, norm_ffn, w_gate, w_up, w_down, norm_final, loss_target, m_w_in, m_norm_mix, m_sgu_v_gain, m_sgu_w_s, m_sgu_b_s, m_w_a_out, m_attn_sink, m_rel_bias, m_w_b_out, m_w_o, m_norm_ffn, m_w_gate, m_w_up, m_w_down, m_norm_final, v_w_in, v_norm_mix, v_sgu_v_gain, v_sgu_w_s, v_sgu_b_s, v_w_a_out, v_attn_sink, v_rel_bias, v_w_b_out, v_w_o, v_norm_ffn, v_w_gate, v_w_up, v_w_down, v_norm_final):
    s, d = x.shape[1], x.shape[2]
    w_sgu = sgu_v_gain.shape[1]
    groups = sgu_w_s.shape[1]
    heads = attn_sink.shape[1]
    grp = heads // N_KV_HEADS
    w_att = heads * HEAD_DIM
    w_kv = N_KV_HEADS * HEAD_DIM
    d_ff = w_gate.shape[2] * N_CHIPS
    n_in = w_in.shape[2] * N_CHIPS
    off_q = 2 * w_sgu
    off_k = off_q + w_att
    off_g = off_k + 2 * w_kv
    assert n_in == off_g + 2 * d and groups * BLK == w_sgu and s % BLK == 0

    x2d = x.reshape(s, d)
    tgt = loss_target.reshape(s, d)
    c_idx = lax.axis_index("c").astype(I32).reshape(1)
    q_idx = (2 * lax.axis_index("x") + lax.axis_index("y")).astype(I32).reshape(1)
    qc_idx = jnp.concatenate([q_idx, c_idx])

    W_IN, W_A, W_B, W_O, W_GATE, W_UP, W_DOWN = range(7)
    names = ["w_in", "w_a", "w_b", "w_o", "w_gate", "w_up", "w_down"]
    kinds = ["col", "col", "col", "row", "col", "col", "row"]
    big_w = [w_in[0], w_a_out[0], w_b_out[0], w_o[0], w_gate[0], w_up[0], w_down[0]]
    big_m = [m_w_in[0], m_w_a_out[0], m_w_b_out[0], m_w_o[0], m_w_gate[0], m_w_up[0], m_w_down[0]]
    big_v = [v_w_in[0], v_w_a_out[0], v_w_b_out[0], v_w_o[0], v_w_gate[0], v_w_up[0], v_w_down[0]]
    full_in = _cast_into_full("cast_w_in", q_idx, big_w[W_IN], kinds[W_IN])
    in_send, in_recv, (full_in,), token = _gather_start("gather_start_in", [full_in], [kinds[W_IN]])
    rest = [_cast_into_full("cast_" + names[i], q_idx, big_w[i], kinds[i], after=(token,)) for i in range(1, 7)]
    ag_send, ag_recv, rest, token = _gather_start("gather_start_rest", rest, kinds[1:])
    fulls = [full_in] + rest

    def gathered(tag, ids, after):
        if ids == [W_IN]:
            sems, pos = (in_send, in_recv), [0]
        else:
            sems, pos = (ag_send, ag_recv), [i - 1 for i in ids]
        got = _gather_wait("gather_wait_" + tag, [fulls[i] for i in ids], [kinds[i] for i in ids], pos, *sems, after)
        return _gather_forward("gather_fwd_" + tag, got, [kinds[i] for i in ids])

    ws_b = sgu_w_s[0].astype(BF16)
    wst_b = jnp.swapaxes(sgu_w_s[0], 1, 2).astype(BF16)
    b_col = sgu_b_s[0].reshape(groups, BLK, 1)
    bias_tab, bucket = _band_tables(rel_bias)
    sink = attn_sink[0]

    tm = _tile(s, (1024, 512, 256, 128))

    h1 = _rms_fwd("rms_mix", x2d, norm_mix, after=(token,))
    (g_in,) = gathered("in", [W_IN], h1)

    tn = _tile(n_in, (768, 640, 512))
    z = _mm(
        "mm_z", (s // tm, n_in // tn, 1), [h1, g_in],
        [pl.BlockSpec((tm, d), lambda i, j, k: (i, 0)), pl.BlockSpec((d, tn), lambda i, j, k: (0, j))],
        [jax.ShapeDtypeStruct((s, n_in), BF16)], [pl.BlockSpec((tm, tn), lambda i, j, k: (i, j))],
        [(0, 1, NN, 0)], 1, (tm, tn), 1, lambda ins, vals, outs, cs: _put(outs[0], cs, vals[0]),
        _mm_vmem([((tm, d), BF16, 2), ((d, tn), BF16, 2), ((tm, tn), F32, 3)]),
    )[0]
    g_a, g_b, g_o = gathered("mix", [W_A, W_B, W_O], z)

    a_act = _sgu_fwd(z, sgu_v_gain, ws_b, b_col, w_sgu)

    kv_b = z[:, off_k:off_g]
    k_pad = jnp.pad(kv_b[:, :w_kv], ((BLK, BLK), (0, 0)))
    v_pad = jnp.pad(kv_b[:, w_kv:], ((BLK, BLK), (0, 0)))
    q_blk0 = off_q // (grp * HEAD_DIM)
    att = _attn_fwd(sink, z, k_pad, v_pad, bias_tab, grp, q_blk0)

    tg = _tile(d, (512,))
    ga0, gb0 = off_g // tg, (off_g + d) // tg

    def ep_gate(ins, vals, outs, cs):
        sa, sb = _sigmoid(ins[4][:, cs].astype(F32)), _sigmoid(ins[5][:, cs].astype(F32))
        _put(outs[0], cs, sa * vals[0] + sb * vals[1])
        _put(outs[1], cs, vals[0])
        _put(outs[2], cs, vals[1])

    t_out = pl.BlockSpec((tm, tg), lambda i, j, k: (i, j))
    m_act, y_a, y_b = _mm(
        "mm_branches", (s // tm, d // tg, 1), [a_act, g_a, att, g_b, z, z],
        [pl.BlockSpec((tm, w_sgu), lambda i, j, k: (i, 0)), pl.BlockSpec((w_sgu, tg), lambda i, j, k: (0, j)),
         pl.BlockSpec((tm, w_att), lambda i, j, k: (i, 0)), pl.BlockSpec((w_att, tg), lambda i, j, k: (0, j)),
         pl.BlockSpec((tm, tg), lambda i, j, k: (i, ga0 + j)), pl.BlockSpec((tm, tg), lambda i, j, k: (i, gb0 + j))],
        [jax.ShapeDtypeStruct((s, d), BF16)] * 3,
        [t_out, t_out, t_out], [(0, 1, NN, 0), (2, 3, NN, 1)], 2, (tm, tg), 1, ep_gate,
        _mm_vmem([((tm, w_sgu), BF16, 4), ((w_sgu, tg), BF16, 4), ((tm, tg), F32, 12)]),    )

    tn = _tile(d, (1024, 512))

    def ep_residual(ins, vals, outs, cs):
        _put(outs[0], cs, ins[2][:, cs] + vals[0])

    x2 = _mm(
        "mm_wo", (s // tm, d // tn, 1), [m_act, g_o, x2d],
        [pl.BlockSpec((tm, d), lambda i, j, k: (i, 0)), pl.BlockSpec((d, tn), lambda i, j, k: (0, j)),
         pl.BlockSpec((tm, tn), lambda i, j, k: (i, j))],
        [jax.ShapeDtypeStruct((s, d), F32)], [pl.BlockSpec((tm, tn), lambda i, j, k: (i, j))],
        [(0, 1, NN, 0)], 1, (tm, tn), 1, ep_residual,
        _mm_vmem([((tm, d), BF16, 2), ((d, tn), BF16, 2), ((tm, tn), F32, 5)]),    )[0]

    h2 = _rms_fwd("rms_ffn", x2, norm_ffn)
    g_gate, g_up = gathered("ffn_in", [W_GATE, W_UP], h2)

    tf = _tile(d_ff, (512,))

    def ep_swiglu(ins, vals, outs, cs):
        gt, up = vals
        _put(outs[0], cs, gt)
        _put(outs[1], cs, up)
        _put(outs[2], cs, (gt * _sigmoid(gt)) * up)

    f_out = pl.BlockSpec((tm, tf), lambda i, j, k: (i, j))
    gt, up, f_act = _mm(
        "mm_gate_up", (s // tm, d_ff // tf, 1), [h2, g_gate, g_up],
        [pl.BlockSpec((tm, d), lambda i, j, k: (i, 0)), pl.BlockSpec((d, tf), lambda i, j, k: (0, j)),
         pl.BlockSpec((d, tf), lambda i, j, k: (0, j))],
        [jax.ShapeDtypeStruct((s, d_ff), BF16)] * 3,
        [f_out, f_out, f_out], [(0, 1, NN, 0), (0, 2, NN, 1)], 2, (tm, tf), 1, ep_swiglu,
        _mm_vmem([((tm, d), BF16, 2), ((d, tf), BF16, 4), ((tm, tf), F32, 8)]),    )
    (g_down,) = gathered("ffn_out", [W_DOWN], f_act)

    tkf = _tile(d_ff, (1408, 1024, 512))
    nkf = d_ff // tkf
    tml, tnl = _tile(s, (512, 256, 128)), _tile(d, (512,))
    x3 = _mm(
        "mm_down", (s // tml, d // tnl, 1), [f_act, g_down, x2],
        [pl.BlockSpec((tml, d_ff), lambda i, j, k: (i, 0)), pl.BlockSpec((d_ff, tnl), lambda i, j, k: (0, j)),
         pl.BlockSpec((tml, tnl), lambda i, j, k: (i, j))],
        [jax.ShapeDtypeStruct((s, d), F32)], [pl.BlockSpec((tml, tnl), lambda i, j, k: (i, j))],
        [(0, 1, NN, 0)], 1, (tml, tnl), 1, ep_residual,
        _mm_vmem([((tml, d_ff), BF16, 2), ((d_ff, tnl), BF16, 2), ((tml, tnl), F32, 6)]),    )[0]

    dx3, dx3b, dg_final, loss_part = _head(x3, norm_final.reshape(1, d), tgt)

    def reduce_a(tag, ids, grads):
        ks = [kinds[i] for i in ids]
        lands = [lax.empty((g.shape[0] // 2, g.shape[1]) if k == "col" else (g.shape[0], g.shape[1] // 2), BF16)
                 for g, k in zip(grads, ks)]
        send, recv, bufs = _split_start("pair_send_" + tag, list(grads) + lands, len(ids), _pair_exchange_copies(ks))
        return {"tag": tag, "ids": ids, "ks": ks, "pair": (send, recv, bufs), "token": bufs[0]}

    def reduce_b(st, after):
        tag, ids, ks = st["tag"], st["ids"], st["ks"]
        send, recv, bufs = st["pair"]
        bufs = _split_wait("pair_wait_" + tag, bufs, send, recv, _pair_exchange_copies(ks), after)
        grads, from_sib = bufs[: len(ids)], bufs[len(ids) :]
        halves = [_pair_add("pair_add_" + names[i], c_idx, g, r, k) for i, g, r, k in zip(ids, grads, from_sib, ks)]
        st["chip"] = _chip_send_start("chip_send_" + tag, halves, ks)
        st["token"] = st["chip"][2][0]

    def reduce_c(st, after):
        tag, ids, ks = st["tag"], st["ids"], st["ks"]
        send, recv, halves, lands = st["chip"]
        halves, lands = _chip_send_wait("chip_wait_" + tag, halves, lands, ks, send, recv, after)
        pieces = [_chip_sum("chip_sum_" + names[i], qc_idx, h, r, k) for i, h, r, k in zip(ids, halves, lands, ks)]
        st["share"] = _split_start("share_send_" + tag, pieces, len(ids), _pair_share_copies(ks, False))
        st["token"] = st["share"][2][0]

    def reduce_d(st, after):
        send, recv, bufs = st["share"]
        return _split_wait("share_wait_" + st["tag"], bufs, send, recv, _pair_share_copies(st["ks"], True), after)

    def ep_swiglu_bwd(ins, vals, outs, cs):
        df = vals[0]
        gtv, upv = ins[2][:, cs].astype(F32), ins[3][:, cs].astype(F32)
        sg = _sigmoid(gtv)
        _put(outs[0], cs, df * upv * (sg + gtv * sg * (1.0 - sg)))
        _put(outs[1], cs, df * (gtv * sg))

    dgt, dup = _mm(
        "mm_dswiglu", (s // tm, d_ff // tf, 1), [dx3b, g_down, gt, up],
        [pl.BlockSpec((tm, d), lambda i, j, k: (i, 0)), pl.BlockSpec((tf, d), lambda i, j, k: (j, 0)), f_out, f_out],
        [jax.ShapeDtypeStruct((s, d_ff), BF16), jax.ShapeDtypeStruct((s, d_ff), BF16)], [f_out, f_out],
        [(0, 1, NT, 0)], 1, (tm, tf), 1, ep_swiglu_bwd,
        _mm_vmem([((tm, d), BF16, 2), ((tf, d), BF16, 2), ((tm, tf), F32, 8)]),    )

    def ep_store(ins, vals, outs, cs):
        for o, v in zip(outs, vals):
            _put(o, cs, v)

    twn = _tile(d, (1024, 512))
    gw_down = _mm(
        "mm_gw_down", (d_ff // tkf, d // twn, 1), [f_act, dx3b],
        [pl.BlockSpec((s, tkf), lambda i, j, k: (0, i)), pl.BlockSpec((s, twn), lambda i, j, k: (0, j))],
        [jax.ShapeDtypeStruct((d_ff, d), BF16)], [pl.BlockSpec((tkf, twn), lambda i, j, k: (i, j))],
        [(0, 1, TN, 0)], 1, (tkf, twn), 1, ep_store,
        _mm_vmem([((s, tkf), BF16, 3), ((s, twn), BF16, 2), ((tkf, twn), F32, 3)]),
    )[0]
    red_down = reduce_a("down", [W_DOWN], [gw_down])

    tn2 = _tile(d, (256,))
    dh2 = _mm(
        "mm_dh2", (s // tml, d // tn2, 1), [dgt, g_gate, dup, g_up],
        [pl.BlockSpec((tml, d_ff), lambda i, j, k: (i, 0)), pl.BlockSpec((tn2, d_ff), lambda i, j, k: (j, 0)),
         pl.BlockSpec((tml, d_ff), lambda i, j, k: (i, 0)), pl.BlockSpec((tn2, d_ff), lambda i, j, k: (j, 0))],
        [jax.ShapeDtypeStruct((s, d), F32)], [pl.BlockSpec((tml, tn2), lambda i, j, k: (i, j))],
        [(0, 1, NT, 0), (2, 3, NT, 0)], 1, (tml, tn2), 1, ep_store,
        _mm_vmem([((tml, d_ff), BF16, 4), ((tn2, d_ff), BF16, 4), ((tml, tn2), F32, 5)]),
        after=(red_down["token"],),
    )[0]
    reduce_b(red_down, dh2)

    twr = _tile(d, (1024, 512))
    w_tile = pl.BlockSpec((twr, tf), lambda i, j, k: (i, j))
    gw_gate, gw_up = _mm(
        "mm_gw_gate_up", (d // twr, d_ff // tf, 1), [h2, dgt, dup],
        [pl.BlockSpec((s, twr), lambda i, j, k: (0, i)), pl.BlockSpec((s, tf), lambda i, j, k: (0, j)),
         pl.BlockSpec((s, tf), lambda i, j, k: (0, j))],
        [jax.ShapeDtypeStruct((d, d_ff), BF16), jax.ShapeDtypeStruct((d, d_ff), BF16)], [w_tile, w_tile],
        [(0, 1, TN, 0), (0, 2, TN, 1)], 2, (twr, tf), 1, ep_store,
        _mm_vmem([((s, twr), BF16, 3), ((s, tf), BF16, 4), ((twr, tf), F32, 6)]),
        after=(red_down["token"],),
    )
    red_ffn = reduce_a("ffn_in", [W_GATE, W_UP], [gw_gate, gw_up])

    dx2, dx2b, dg_ffn = _rms_bwd("rms_ffn_bwd", x2, norm_ffn, dh2, dx3, after=(red_ffn["token"],))

    nj = d // tg

    def lo(j):
        return jnp.minimum(j, nj - 1)

    def gate_bwd_body(dx_ref, wo_ref, ga_ref, gb_ref, ya_ref, yb_ref, dya_ref, dyb_ref, dz_ref, keep):
        j = pl.program_id(1)

        @pl.when(j < nj)
        def _():
            dm = lax.dot_general(dx_ref[...], wo_ref[...], NT, preferred_element_type=F32)
            sa, sb = _sigmoid(ga_ref[...].astype(F32)), _sigmoid(gb_ref[...].astype(F32))
            dya_ref[...] = (dm * sa).astype(BF16)
            dyb_ref[...] = (dm * sb).astype(BF16)
            dz_ref[...] = (dm * ya_ref[...].astype(F32) * (sa * (1.0 - sa))).astype(BF16)
            keep[lo(j)] = (dm * yb_ref[...].astype(F32) * (sb * (1.0 - sb))).astype(BF16)

        @pl.when(j >= nj)
        def _():
            dz_ref[...] = keep[jnp.maximum(j - nj, 0)]

    t_lo = pl.BlockSpec((tm, tg), lambda i, j: (i, lo(j)))
    dya, dyb, dz = pl.pallas_call(
        gate_bwd_body,
        name="mm_dgate",
        grid=(s // tm, 2 * nj),
        in_specs=[
            pl.BlockSpec((tm, d), lambda i, j: (i, 0)),
            pl.BlockSpec((tg, d), lambda i, j: (lo(j), 0)),
            pl.BlockSpec((tm, tg), lambda i, j: (i, ga0 + lo(j))),
            pl.BlockSpec((tm, tg), lambda i, j: (i, gb0 + lo(j))),
            t_lo,
            t_lo,
        ],
        out_specs=[t_lo, t_lo, pl.BlockSpec((tm, tg), lambda i, j: (i, ga0 + j))],
        out_shape=[jax.ShapeDtypeStruct((s, d), BF16), jax.ShapeDtypeStruct((s, d), BF16), jax.ShapeDtypeStruct((s, n_in), BF16)],
        scratch_shapes=[pltpu.VMEM((nj, tm, tg), BF16)],
        compiler_params=_params(_mm_vmem([((tm, d), BF16, 2), ((tg, d), BF16, 2), ((tm, tg), F32, 14), ((nj, tm, tg), BF16, 1)])),
    )(dx2b, g_o, z, z, y_a, y_b)
    reduce_b(red_ffn, dya)

    gw_o = _mm(
        "mm_gw_o", (d // twr, d // twn, 1), [m_act, dx2b],
        [pl.BlockSpec((s, twr), lambda i, j, k: (0, i)), pl.BlockSpec((s, twn), lambda i, j, k: (0, j))],
        [jax.ShapeDtypeStruct((d, d), BF16)], [pl.BlockSpec((twr, twn), lambda i, j, k: (i, j))],
        [(0, 1, TN, 0)], 1, (twr, twn), 1, ep_store,
        _mm_vmem([((s, twr), BF16, 3), ((s, twn), BF16, 2), ((twr, twn), F32, 3)]),
        after=(red_ffn["token"],),
    )[0]
    red_o = reduce_a("w_o", [W_O], [gw_o])

    tb = _tile(w_sgu, (1024, 512))
    b_out = pl.BlockSpec((tm, tb), lambda i, j, k: (i, j))

    da, datt = _mm(
        "mm_dbranches", (s // tm, w_sgu // tb, 1), [dya, g_a, dyb, g_b],
        [pl.BlockSpec((tm, d), lambda i, j, k: (i, 0)), pl.BlockSpec((tb, d), lambda i, j, k: (j, 0)),
         pl.BlockSpec((tm, d), lambda i, j, k: (i, 0)), pl.BlockSpec((tb, d), lambda i, j, k: (j, 0))],
        [jax.ShapeDtypeStruct((s, w_sgu), BF16), jax.ShapeDtypeStruct((s, w_att), BF16)], [b_out, b_out],
        [(0, 1, NT, 0), (2, 3, NT, 1)], 2, (tm, tb), 1, ep_store,
        _mm_vmem([((tm, d), BF16, 4), ((tb, d), BF16, 4), ((tm, tb), F32, 6)]),        after=(red_o["token"],),
    )
    reduce_b(red_o, da)

    wb_tile = pl.BlockSpec((tb, twn), lambda i, j, k: (i, j))
    gw_a, gw_b = _mm(
        "mm_gw_branches", (w_sgu // tb, d // twn, 1), [a_act, dya, att, dyb],
        [pl.BlockSpec((s, tb), lambda i, j, k: (0, i)), pl.BlockSpec((s, twn), lambda i, j, k: (0, j)),
         pl.BlockSpec((s, tb), lambda i, j, k: (0, i)), pl.BlockSpec((s, twn), lambda i, j, k: (0, j))],
        [jax.ShapeDtypeStruct((w_sgu, d), BF16), jax.ShapeDtypeStruct((w_att, d), BF16)], [wb_tile, wb_tile],
        [(0, 1, TN, 0), (2, 3, TN, 1)], 2, (tb, twn), 1, ep_store,
        _mm_vmem([((s, tb), BF16, 5), ((s, twn), BF16, 4), ((tb, twn), F32, 6)]),
        after=(red_o["token"],),
    )
    red_mix = reduce_a("mix", [W_A, W_B], [gw_a, gw_b])

    dz, dws, dbs, dgain = _sgu_bwd(z, da, sgu_v_gain, ws_b, wst_b, b_col, w_sgu, dz, after=(red_mix["token"],))
    dz, dk_pad, dv_pad, dbias_tab, dsink = _attn_bwd(sink, z, k_pad, v_pad, bias_tab, datt, dz, grp, q_blk0)
    dz = _dkv_to_dz(dk_pad, dv_pad, dz, off_k // (2 * w_kv))
    drel = _relbias_bwd(dbias_tab, bucket)
    reduce_b(red_mix, dz)

    small_w = [norm_mix, sgu_v_gain, sgu_w_s, sgu_b_s, attn_sink, rel_bias, norm_ffn, norm_final]
    small_m = [m_norm_mix, m_sgu_v_gain, m_sgu_w_s, m_sgu_b_s, m_attn_sink, m_rel_bias, m_norm_ffn, m_norm_final]
    small_v = [v_norm_mix, v_sgu_v_gain, v_sgu_w_s, v_sgu_b_s, v_attn_sink, v_rel_bias, v_norm_ffn, v_norm_final]
    small_shapes = [w.shape for w in small_w]
    early = [dgain, dws, dbs, dsink[:, 0], drel[:, :REL_BUCKETS].T, dg_ffn, dg_final]
    p_early = _pack([g.reshape(shp) for g, shp in zip(early, small_shapes[1:])] + [loss_part[0, :1]])
    land = jnp.zeros((2 * N_CHIPS,) + p_early.shape, F32)
    sm_send, sm_recv, (p_early, land) = _split_start("small_send", [p_early, land], 2 * N_CHIPS - 1, _small_exchange_copies(False))

    tzn = _tile(n_in, (768, 640, 512))
    gw_in = _mm(
        "mm_gw_in", (d // twr, n_in // tzn, 1), [h1, dz],
        [pl.BlockSpec((s, twr), lambda i, j, k: (0, i)), pl.BlockSpec((s, tzn), lambda i, j, k: (0, j))],
        [jax.ShapeDtypeStruct((d, n_in), BF16)], [pl.BlockSpec((twr, tzn), lambda i, j, k: (i, j))],
        [(0, 1, TN, 0)], 1, (twr, tzn), 1, ep_store,
        _mm_vmem([((s, twr), BF16, 3), ((s, tzn), BF16, 2), ((twr, tzn), F32, 3)]),
        after=(red_mix["token"], p_early),
    )[0]
    red_in = reduce_a("w_in", [W_IN], [gw_in])

    dh1 = _mm(
        "mm_dh1", (s // tml, d // tnl, 1), [dz, g_in],
        [pl.BlockSpec((tml, n_in), lambda i, j, k: (i, 0)), pl.BlockSpec((tnl, n_in), lambda i, j, k: (j, 0))],
        [jax.ShapeDtypeStruct((s, d), F32)], [pl.BlockSpec((tml, tnl), lambda i, j, k: (i, j))],
        [(0, 1, NT, 0)], 1, (tml, tnl), 1, ep_store,
        _mm_vmem([((tml, n_in), BF16, 2), ((tnl, n_in), BF16, 2), ((tml, tnl), F32, 5)]),
        after=(red_in["token"],),
    )[0]

    reduce_b(red_in, dh1)
    grad_x, _, dg_mix = _rms_bwd("rms_mix_bwd", x2d, norm_mix, dh1, dx2, after=(red_in["token"],))

    p_mix = _pack([dg_mix.reshape(small_shapes[0])])
    land_mix = jnp.zeros((2 * N_CHIPS,) + p_mix.shape, F32)
    mx_send, mx_recv, (p_mix, land_mix) = _split_start("mix_send", [p_mix, land_mix], 2 * N_CHIPS - 1, _small_exchange_copies(False))

    grads_big, upd = [None] * 7, [None] * 7

    def finish(st, after):
        for i, g in zip(st["ids"], reduce_d(st, after)):
            upd[i] = _adamw("adamw_" + names[i], big_w[i], g, big_m[i], big_v[i])
            grads_big[i] = upd[i][3]
            after = upd[i][0]
        return after

    after, prev = p_mix, None
    for st in (red_down, red_ffn, red_o, red_mix, red_in):
        reduce_c(st, after)
        after = st["token"] if prev is None else finish(prev, st["token"])
        prev = st

    p_early, land = _split_wait("small_wait", [p_early, land], sm_send, sm_recv, _small_exchange_copies(True), after)
    p_mix, land_mix = _split_wait("mix_wait", [p_mix, land_mix], mx_send, mx_recv, _small_exchange_copies(True), p_early)
    me_idx = 2 * q_idx + c_idx
    packed_g = jnp.concatenate([_small_sum("mix_sum", me_idx, p_mix, land_mix), _small_sum("small_sum", me_idx, p_early, land)], axis=0)
    g_small = _unpack(packed_g, small_shapes + [(1,)])
    loss = g_small[-1].reshape(())
    g_small = g_small[:-1]
    zero1 = jnp.zeros((1,), F32)
    pw, pg, pm, pv = _pack(small_w + [zero1]), _pack(g_small + [zero1]), _pack(small_m + [zero1]), _pack(small_v + [zero1])
    small_upd = _adamw("adamw_small", pw, pg, pm, pv)
    d_small, nm_small, nv_small = [_unpack(a, small_shapes) for a in small_upd[:3]]
    finish(prev, small_upd[0])

    small_names = ["norm_mix", "sgu_v_gain", "sgu_w_s", "sgu_b_s", "attn_sink", "rel_bias", "norm_ffn", "norm_final"]
    table = {}
    for i, n in enumerate(names):
        table[n] = (grads_big[i][None], upd[i][0][None], upd[i][1][None], upd[i][2][None])
    for i, n in enumerate(small_names):
        table[n] = (g_small[i], d_small[i], nm_small[i], nv_small[i])
    order = ["w_in", "norm_mix", "sgu_v_gain", "sgu_w_s", "sgu_b_s", "w_a", "attn_sink", "rel_bias", "w_b", "w_o", "norm_ffn",
             "w_gate", "w_up", "w_down", "norm_final"]
    outs = [loss, grad_x.reshape(1, s, d)]
    for part in range(4):
        outs += [table[n][part] for n in order]
    return tuple(outs)
```

```python
import math

import jax
import jax.numpy as jnp
import numpy as np
from jax import lax
from jax.experimental import pallas as pl
from jax.experimental.pallas import tpu as pltpu

F32 = jnp.float32
BF16 = jnp.bfloat16
I32 = jnp.int32
MESH = pl.DeviceIdType.MESH

EPS = 1e-6
NEG = -1e30
BLK = 128
HEAD_DIM = 128
N_KV_HEADS = 2
REL_BUCKETS = 32
REL_MAX_DIST = 128
N_CHIPS = 4
ADAM_LR, ADAM_B1, ADAM_B2, ADAM_EPS, ADAM_WD, ADAM_STEP = 0.001, 0.9, 0.999, 1e-08, 0.01, 10

LANES = 128
MXU_COLS = 256
VMEM_CAP = 60 * 1024 * 1024

NN = (((1,), (0,)), ((), ()))
NT = (((1,), (1,)), ((), ()))
TN = (((0,), (0,)), ((), ()))
ANY = pl.BlockSpec(memory_space=pl.ANY)
HBM_SPEC = pl.BlockSpec(memory_space=pltpu.HBM)
SEM_SPEC = pl.BlockSpec(memory_space=pltpu.SEMAPHORE)
EFFECT = pltpu.SideEffectType.DATAFLOW_SIDE_EFFECTING


def _tile(n, cands):
    for t in cands:
        if n % t == 0:
            return t
    return n


def _params(vmem_bytes=None, **kw):
    if vmem_bytes is not None:
        kw["vmem_limit_bytes"] = int(min(max(vmem_bytes, 32 * 1024 * 1024), VMEM_CAP))
    return pltpu.CompilerParams(**kw)


def _nbytes(shape, dtype):
    return int(np.prod(shape)) * jnp.dtype(dtype).itemsize


def _sigmoid(x):
    return 1.0 / (1.0 + jnp.exp(-x))


_GC = 0.7978845608028654
_GA = 0.044715


def _gelu(x):
    return 0.5 * x * (1.0 + jnp.tanh(_GC * (x + _GA * (x * x * x))))


def _gelu_grad(x):
    t = jnp.tanh(_GC * (x + _GA * (x * x * x)))
    return 0.5 * (1.0 + t) + 0.5 * x * (1.0 - t * t) * (_GC * (1.0 + 3.0 * _GA * (x * x)))


def _bf(v):
    return v if v.dtype == BF16 else v.astype(BF16)


def _mm(name, grid, ins, in_specs, out_shape, out_specs, pairs, n_acc, tile, nk, epilogue, vmem_bytes, after=(), col_chunks=1):
    assert nk == 1 and tile[1] % col_chunks == 0
    n_in, n_out = len(ins) + len(after), len(out_shape)
    width = tile[1] // col_chunks

    def body(*refs):
        in_refs, out_refs = refs[:n_in], refs[n_in : n_in + n_out]
        for ch in range(col_chunks):
            cs = slice(ch * width, (ch + 1) * width) if col_chunks > 1 else slice(None)
            vals = [None] * n_acc
            for a_i, b_i, dn, acc_i in pairs:
                rhs = in_refs[b_i][cs, :] if dn == NT else in_refs[b_i][:, cs]
                d = lax.dot_general(_bf(in_refs[a_i][...]), _bf(rhs), dn, preferred_element_type=F32)
                vals[acc_i] = d if vals[acc_i] is None else vals[acc_i] + d
            epilogue(in_refs, vals, out_refs, cs)

    return pl.pallas_call(
        body,
        name=name,
        grid=grid,
        in_specs=list(in_specs) + [ANY] * len(after),
        out_specs=out_specs,
        out_shape=out_shape,
        compiler_params=_params(vmem_bytes),
    )(*ins, *after)


def _put(ref, cs, v):
    ref[:, cs] = v.astype(ref.dtype)


def _mm_vmem(tiles):
    return sum(_nbytes(s, d) * c for s, d, c in tiles) + 4 * 1024 * 1024


def _rows8(v):
    r, d = v.shape
    return v.reshape(r // 8, 8, d).sum(axis=0)


def _rms_fwd(name, x, g, after=()):
    s, d = x.shape
    tm = _tile(s, (256, 128))

    def body(x_ref, g_ref, *rest):
        h_ref = rest[-1]
        xv = x_ref[...]
        r = lax.rsqrt(jnp.mean(xv * xv, axis=-1, keepdims=True) + EPS)
        h_ref[...] = ((xv * r) * g_ref[...]).astype(BF16)

    return pl.pallas_call(
        body,
        name=name,
        grid=(s // tm,),
        in_specs=[pl.BlockSpec((tm, d), lambda i: (i, 0)), pl.BlockSpec((1, d), lambda i: (0, 0))] + [ANY] * len(after),
        out_specs=pl.BlockSpec((tm, d), lambda i: (i, 0)),
        out_shape=jax.ShapeDtypeStruct((s, d), BF16),
    )(x, g, *after)


def _rms_bwd(name, x, g, dh, dres, after=()):
    s, d = x.shape
    tm = _tile(s, (256, 128))
    n = s // tm
    n_after = len(after)

    def body(x_ref, g_ref, dh_ref, dres_ref, *rest):
        dx_ref, dxb_ref, dg_ref, acc_ref = rest[n_after:]
        i = pl.program_id(0)
        xv = x_ref[...]
        r = lax.rsqrt(jnp.mean(xv * xv, axis=-1, keepdims=True) + EPS)
        xh = xv * r
        dhv = dh_ref[...]
        dxh = dhv * g_ref[...]
        dx = r * (dxh - xh * jnp.mean(dxh * xh, axis=-1, keepdims=True)) + dres_ref[...]
        dx_ref[...] = dx
        dxb_ref[...] = dx.astype(BF16)
        part = _rows8(dhv * xh)

        @pl.when(i == 0)
        def _():
            acc_ref[...] = part

        @pl.when(i > 0)
        def _():
            acc_ref[...] += part

        @pl.when(i == n - 1)
        def _():
            dg_ref[...] = jnp.sum(acc_ref[...], axis=0, keepdims=True)

    row = pl.BlockSpec((tm, d), lambda i: (i, 0))
    vec = pl.BlockSpec((1, d), lambda i: (0, 0))
    return pl.pallas_call(
        body,
        name=name,
        grid=(n,),
        in_specs=[row, vec, row, row] + [ANY] * n_after,
        out_specs=[row, row, vec],
        out_shape=[jax.ShapeDtypeStruct((s, d), F32), jax.ShapeDtypeStruct((s, d), BF16), jax.ShapeDtypeStruct((1, d), F32)],
        scratch_shapes=[pltpu.VMEM((8, d), F32)],
    )(x, g, dh, dres, *after)


def _head(x3, g, target):
    s, d = x3.shape
    tm = _tile(s, (256, 128))
    n = s // tm

    def body(x_ref, g_ref, t_ref, dx_ref, dxb_ref, dg_ref, loss_ref, acc_g, acc_l):
        i = pl.program_id(0)
        xv = x_ref[...]
        gv = g_ref[...]
        r = lax.rsqrt(jnp.mean(xv * xv, axis=-1, keepdims=True) + EPS)
        xh = xv * r
        e = xh * gv - t_ref[...]
        dy = e * (1.0 / d)
        dxh = dy * gv
        dx = r * (dxh - xh * jnp.mean(dxh * xh, axis=-1, keepdims=True))
        dx_ref[...] = dx
        dxb_ref[...] = dx.astype(BF16)
        pg = _rows8(dy * xh)
        plo = _rows8(e * e)

        @pl.when(i == 0)
        def _():
            acc_g[...] = pg
            acc_l[...] = plo

        @pl.when(i > 0)
        def _():
            acc_g[...] += pg
            acc_l[...] += plo

        @pl.when(i == n - 1)
        def _():
            dg_ref[...] = jnp.sum(acc_g[...], axis=0, keepdims=True)
            loss_ref[...] = jnp.full((1, LANES), (0.5 / d) * jnp.sum(acc_l[...]), F32)

    row = pl.BlockSpec((tm, d), lambda i: (i, 0))
    vec = pl.BlockSpec((1, d), lambda i: (0, 0))
    return pl.pallas_call(
        body,
        name="head",
        grid=(n,),
        in_specs=[row, vec, row],
        out_specs=[row, row, vec, pl.BlockSpec((1, LANES), lambda i: (0, 0))],
        out_shape=[
            jax.ShapeDtypeStruct((s, d), F32),
            jax.ShapeDtypeStruct((s, d), BF16),
            jax.ShapeDtypeStruct((1, d), F32),
            jax.ShapeDtypeStruct((1, LANES), F32),
        ],
        scratch_shapes=[pltpu.VMEM((8, d), F32), pltpu.VMEM((8, d), F32)],
    )(x3, g, target)


def _sgu_fwd(z, gain, ws_b, b_col, w_sgu):
    s = z.shape[0]
    groups = ws_b.shape[0]

    def body(zu_ref, zv_ref, gain_ref, ws_ref, b_ref, a_ref):
        vv = _gelu(zv_ref[...].astype(F32))
        r = lax.rsqrt(jnp.mean(vv * vv, axis=-1, keepdims=True) + EPS)
        vn = ((vv * r) * gain_ref[...]).astype(BF16)
        u = _gelu(zu_ref[...].astype(F32))
        for g in range(groups):
            sl = slice(g * BLK, (g + 1) * BLK)
            mixed = jnp.dot(ws_ref[g], vn[:, sl], preferred_element_type=F32) + b_ref[g]
            a_ref[:, sl] = (u[:, sl] * mixed).astype(BF16)

    return pl.pallas_call(
        body,
        name="sgu_fwd",
        grid=(s // BLK,),
        in_specs=[
            pl.BlockSpec((BLK, w_sgu), lambda c: (c, 0)),
            pl.BlockSpec((BLK, w_sgu), lambda c: (c, 1)),
            pl.BlockSpec((1, w_sgu), lambda c: (0, 0)),
            pl.BlockSpec((groups, BLK, BLK), lambda c: (0, 0, 0)),
            pl.BlockSpec((groups, BLK, 1), lambda c: (0, 0, 0)),
        ],
        out_specs=pl.BlockSpec((BLK, w_sgu), lambda c: (c, 0)),
        out_shape=jax.ShapeDtypeStruct((s, w_sgu), BF16),
    )(z, z, gain, ws_b, b_col)


def _sgu_bwd(z, da, gain, ws_b, wst_b, b_col, w_sgu, dz, after=()):
    s = z.shape[0]
    groups = ws_b.shape[0]
    n = s // BLK
    n_skip = 1 + len(after)

    def body(zu_ref, zv_ref, da_ref, gain_ref, ws_ref, wst_ref, b_ref, *rest):
        dz_ref, dws_ref, dbs_ref, dgain_ref, acc_gain = rest[n_skip:]
        c = pl.program_id(0)
        zu = zu_ref[...].astype(F32)
        zv = zv_ref[...].astype(F32)
        gain_v = gain_ref[...]
        vv = _gelu(zv)
        r = lax.rsqrt(jnp.mean(vv * vv, axis=-1, keepdims=True) + EPS)
        xh = vv * r
        vn = (xh * gain_v).astype(BF16)
        u = _gelu(zu)
        dav = da_ref[...].astype(F32)
        dmix = dav * u
        dmix_b = dmix.astype(BF16)
        dvn_parts = []
        for g in range(groups):
            sl = slice(g * BLK, (g + 1) * BLK)
            mixed = jnp.dot(ws_ref[g], vn[:, sl], preferred_element_type=F32) + b_ref[g]
            dz_ref[:, sl] = (dav[:, sl] * mixed * _gelu_grad(zu[:, sl])).astype(BF16)
            dvn_parts.append(jnp.dot(wst_ref[g], dmix_b[:, sl], preferred_element_type=F32))
            dws_g = lax.dot_general(dmix_b[:, sl], vn[:, sl], NT, preferred_element_type=F32)
            dbs_g = jnp.sum(dmix[:, sl], axis=1, keepdims=True)

            @pl.when(c == 0)
            def _():
                dws_ref[g] = dws_g
                dbs_ref[g] = dbs_g

            @pl.when(c > 0)
            def _():
                dws_ref[g] += dws_g
                dbs_ref[g] += dbs_g

        dvn = jnp.concatenate(dvn_parts, axis=1)
        dxh = dvn * gain_v
        dvv = r * (dxh - xh * jnp.mean(dxh * xh, axis=-1, keepdims=True))
        dz_ref[:, w_sgu:] = (dvv * _gelu_grad(zv)).astype(BF16)
        pg = _rows8(dvn * xh)

        @pl.when(c == 0)
        def _():
            acc_gain[...] = pg

        @pl.when(c > 0)
        def _():
            acc_gain[...] += pg

        @pl.when(c == n - 1)
        def _():
            dgain_ref[...] = jnp.sum(acc_gain[...], axis=0, keepdims=True)

    full3 = pl.BlockSpec((groups, BLK, BLK), lambda c: (0, 0, 0))
    col3 = pl.BlockSpec((groups, BLK, 1), lambda c: (0, 0, 0))
    vec = pl.BlockSpec((1, w_sgu), lambda c: (0, 0))
    return pl.pallas_call(
        body,
        name="sgu_bwd",
        grid=(n,),
        in_specs=[
            pl.BlockSpec((BLK, w_sgu), lambda c: (c, 0)),
            pl.BlockSpec((BLK, w_sgu), lambda c: (c, 1)),
            pl.BlockSpec((BLK, w_sgu), lambda c: (c, 0)),
            vec,
            full3,
            full3,
            col3,
            ANY,
        ]
        + [ANY] * len(after),
        out_specs=[pl.BlockSpec((BLK, 2 * w_sgu), lambda c: (c, 0)), full3, col3, vec],
        out_shape=[
            jax.ShapeDtypeStruct(dz.shape, BF16),
            jax.ShapeDtypeStruct((groups, BLK, BLK), F32),
            jax.ShapeDtypeStruct((groups, BLK, 1), F32),
            jax.ShapeDtypeStruct((1, w_sgu), F32),
        ],
        scratch_shapes=[pltpu.VMEM((8, w_sgu), F32)],
        input_output_aliases={7: 0},
    )(z, z, da, gain, ws_b, wst_b, b_col, dz, *after)


def _attn_softmax(sink_ref, q_ref, k_ref, v_ref, bias_ref, s_len, grp):
    kv = pl.program_id(0)
    n = pl.program_id(1)
    start = pl.multiple_of(n * BLK, BLK)
    kb = k_ref[pl.ds(start, 3 * BLK), :]
    vb = v_ref[pl.ds(start, 3 * BLK), :]
    qv = q_ref[...]
    qs = jnp.concatenate([qv[:, g * HEAD_DIM : (g + 1) * HEAD_DIM] for g in range(grp)], axis=0).astype(BF16)
    sc = lax.dot_general(qs, kb, NT, preferred_element_type=F32) * (HEAD_DIM**-0.5)
    sc = sc + bias_ref[...].reshape(grp * BLK, 3 * BLK)
    kpos = start + lax.broadcasted_iota(I32, (1, 3 * BLK), 1) - BLK
    sc = jnp.where((kpos >= 0) & (kpos < s_len), sc, NEG)
    sink = jnp.concatenate([jnp.full((BLK, 1), sink_ref[kv * grp + g], F32) for g in range(grp)], axis=0)
    m = jnp.maximum(jnp.max(sc, axis=-1, keepdims=True), sink)
    p = jnp.exp(sc - m)
    esink = jnp.exp(sink - m)
    den = jnp.sum(p, axis=-1, keepdims=True) + esink
    return start, qs, kb, vb, p / den, esink / den


def _attn_specs(s, grp, q_blk0):
    qw = grp * HEAD_DIM
    return [
        pl.BlockSpec(memory_space=pltpu.SMEM),
        pl.BlockSpec((BLK, qw), lambda kv, n: (n, q_blk0 + kv)),
        pl.BlockSpec((s + 2 * BLK, HEAD_DIM), lambda kv, n: (0, kv)),
        pl.BlockSpec((s + 2 * BLK, HEAD_DIM), lambda kv, n: (0, kv)),
        pl.BlockSpec((grp, BLK, 3 * BLK), lambda kv, n: (kv, 0, 0)),
    ]


def _attn_fwd(sink, z, k_pad, v_pad, bias_tab, grp, q_blk0):
    s = z.shape[0]
    qw = grp * HEAD_DIM

    def body(sink_ref, q_ref, k_ref, v_ref, bias_ref, o_ref):
        _, _, _, vb, pn, _ = _attn_softmax(sink_ref, q_ref, k_ref, v_ref, bias_ref, s, grp)
        o = jnp.dot(pn.astype(BF16), vb, preferred_element_type=F32)
        for g in range(grp):
            o_ref[:, g * HEAD_DIM : (g + 1) * HEAD_DIM] = o[g * BLK : (g + 1) * BLK].astype(BF16)

    return pl.pallas_call(
        body,
        name="attn_fwd",
        grid=(N_KV_HEADS, s // BLK),
        in_specs=_attn_specs(s, grp, q_blk0),
        out_specs=pl.BlockSpec((BLK, qw), lambda kv, n: (n, kv)),
        out_shape=jax.ShapeDtypeStruct((s, N_KV_HEADS * qw), BF16),
    )(sink, z, k_pad, v_pad, bias_tab)


def _attn_bwd(sink, z, k_pad, v_pad, bias_tab, dout, dz, grp, q_blk0):
    s = z.shape[0]
    qw = grp * HEAD_DIM
    nb = s // BLK
    heads = N_KV_HEADS * grp

    def body(sink_ref, q_ref, k_ref, v_ref, bias_ref, do_ref, dz_in, dq_ref, dk_ref, dv_ref, dbias_ref, dsink_ref, dk_acc, dv_acc):
        del dz_in
        kv = pl.program_id(0)
        n = pl.program_id(1)
        start, qs, kb, vb, pn, psink = _attn_softmax(sink_ref, q_ref, k_ref, v_ref, bias_ref, s, grp)
        dov = do_ref[...]
        dos = jnp.concatenate([dov[:, g * HEAD_DIM : (g + 1) * HEAD_DIM] for g in range(grp)], axis=0)
        dp = lax.dot_general(dos, vb, NT, preferred_element_type=F32)
        dvb = lax.dot_general(pn.astype(BF16), dos, TN, preferred_element_type=F32)
        delta = jnp.sum(pn * dp, axis=-1, keepdims=True)
        ds = pn * (dp - delta)
        dsb = (ds * (HEAD_DIM**-0.5)).astype(BF16)
        dq = jnp.dot(dsb, kb, preferred_element_type=F32)
        dkb = lax.dot_general(dsb, qs, TN, preferred_element_type=F32)
        for g in range(grp):
            dq_ref[:, g * HEAD_DIM : (g + 1) * HEAD_DIM] = dq[g * BLK : (g + 1) * BLK].astype(BF16)

        @pl.when(n == 0)
        def _():
            dk_acc[...] = jnp.zeros_like(dk_acc)
            dv_acc[...] = jnp.zeros_like(dv_acc)
            dbias_ref[...] = jnp.zeros_like(dbias_ref)

        @pl.when((n == 0) & (kv == 0))
        def _():
            dsink_ref[...] = jnp.zeros_like(dsink_ref)

        dk_acc[pl.ds(start, 3 * BLK), :] += dkb
        dv_acc[pl.ds(start, 3 * BLK), :] += dvb
        dbias_ref[...] += ds.reshape(grp, BLK, 3 * BLK)
        row = lax.broadcasted_iota(I32, (heads, LANES), 0)
        sd = psink * delta
        upd = jnp.zeros((heads, LANES), F32)
        for g in range(grp):
            upd = jnp.where(row == kv * grp + g, -jnp.sum(sd[g * BLK : (g + 1) * BLK]), upd)
        dsink_ref[...] += upd

        @pl.when(n == nb - 1)
        def _():
            dk_ref[...] = dk_acc[...]
            dv_ref[...] = dv_acc[...]

    pad_spec = pl.BlockSpec((s + 2 * BLK, HEAD_DIM), lambda kv, n: (0, kv))
    kvw = N_KV_HEADS * HEAD_DIM
    return pl.pallas_call(
        body,
        name="attn_bwd",
        grid=(N_KV_HEADS, nb),
        in_specs=_attn_specs(s, grp, q_blk0) + [pl.BlockSpec((BLK, qw), lambda kv, n: (n, kv)), ANY],
        out_specs=[
            pl.BlockSpec((BLK, qw), lambda kv, n: (n, q_blk0 + kv)),
            pad_spec,
            pad_spec,
            pl.BlockSpec((grp, BLK, 3 * BLK), lambda kv, n: (kv, 0, 0)),
            pl.BlockSpec((heads, LANES), lambda kv, n: (0, 0)),
        ],
        out_shape=[
            jax.ShapeDtypeStruct(dz.shape, BF16),
            jax.ShapeDtypeStruct((s + 2 * BLK, kvw), F32),
            jax.ShapeDtypeStruct((s + 2 * BLK, kvw), F32),
            jax.ShapeDtypeStruct((heads, BLK, 3 * BLK), F32),
            jax.ShapeDtypeStruct((heads, LANES), F32),
        ],
        scratch_shapes=[pltpu.VMEM((s + 2 * BLK, HEAD_DIM), F32), pltpu.VMEM((s + 2 * BLK, HEAD_DIM), F32)],
        input_output_aliases={6: 0},
    )(sink, z, k_pad, v_pad, bias_tab, dout, dz)


def _dkv_to_dz(dk_pad, dv_pad, dz, blk_idx):
    s = dz.shape[0]
    kvw = dk_pad.shape[1]

    def body(dk_ref, dv_ref, dz_in, out_ref):
        del dz_in
        out_ref[:, :kvw] = dk_ref[...].astype(BF16)
        out_ref[:, kvw:] = dv_ref[...].astype(BF16)

    src = pl.BlockSpec((BLK, kvw), lambda i: (i + 1, 0))
    return pl.pallas_call(
        body,
        name="dkv_to_dz",
        grid=(s // BLK,),
        in_specs=[src, src, ANY],
        out_specs=pl.BlockSpec((BLK, 2 * kvw), lambda i: (i, blk_idx)),
        out_shape=jax.ShapeDtypeStruct(dz.shape, BF16),
        input_output_aliases={2: 0},
    )(dk_pad, dv_pad, dz)


def _relbias_bwd(dbias_tab, bucket):
    heads = dbias_tab.shape[0]

    def body(dt_ref, bk_ref, out_ref):
        lane = lax.broadcasted_iota(I32, (1, LANES), 1)
        bk = bk_ref[...]
        rows = []
        for h in range(heads):
            dt = dt_ref[h]
            acc = jnp.zeros((1, LANES), F32)
            for b in range(REL_BUCKETS):
                acc = jnp.where(lane == b, jnp.sum(jnp.where(bk == b, dt, 0.0)), acc)
            rows.append(acc)
        out_ref[...] = jnp.concatenate(rows, axis=0)

    return pl.pallas_call(body, name="relbias_bwd", out_shape=jax.ShapeDtypeStruct((heads, LANES), F32))(dbias_tab, bucket)


def _t5_bucket(rel):
    nb = REL_BUCKETS // 2
    ret = jnp.where(rel > 0, nb, 0)
    n = jnp.abs(rel)
    max_exact = nb // 2
    nf = jnp.maximum(n, 1).astype(F32)
    large = max_exact + (jnp.log(nf / max_exact) / math.log(REL_MAX_DIST / max_exact) * (nb - max_exact)).astype(I32)
    large = jnp.minimum(large, nb - 1)
    return ret + jnp.where(n < max_exact, n, large)


def _band_tables(rel_bias):
    qi = jnp.arange(BLK)[:, None]
    kj = jnp.arange(3 * BLK)[None, :]
    rel = kj - BLK - qi
    bucket = _t5_bucket(rel).astype(I32)
    heads = rel_bias.shape[1]
    masked = jnp.where(jnp.abs(rel) <= BLK, bucket, -1)

    def body(rb_ref, bk_ref, out_ref):
        bk = bk_ref[...]
        for h in range(heads):
            tab = jnp.full(bk.shape, NEG, F32)
            for b in range(REL_BUCKETS):
                tab = jnp.where(bk == b, rb_ref[b, h], tab)
            out_ref[h] = tab

    bias_tab = pl.pallas_call(
        body,
        name="bias_table",
        in_specs=[pl.BlockSpec(memory_space=pltpu.SMEM), pl.BlockSpec(memory_space=pltpu.VMEM)],
        out_specs=pl.BlockSpec(memory_space=pltpu.VMEM),
        out_shape=jax.ShapeDtypeStruct((heads, BLK, 3 * BLK), F32),
    )(rel_bias.astype(F32), masked)
    return bias_tab, bucket


EW_BLOCK_ELEMS = 512 * 1024


def _ew_tiles(shape, elems=EW_BLOCK_ELEMS // 2):
    r, c = shape
    tn = c if c <= 2048 else _tile(c, (2048, 1920, 1536, 1408, 1024, 512))
    tm = _tile(r, [t for t in (1024, 512, 256, 128, 64, 32, 16, 8) if t * tn <= elems] or [8])
    return tm, tn


def _cast_into_full(name, qidx, w, kind, after=()):
    r, c = w.shape
    tm, tn = _ew_tiles(w.shape, EW_BLOCK_ELEMS)
    nbi, nbj = r // tm, c // tn
    if kind == "col":
        full, out_spec = (r, c * N_CHIPS), pl.BlockSpec((tm, tn), lambda i, j, q: (i, q[0] * nbj + j))
    else:
        full, out_spec = (r * N_CHIPS, c), pl.BlockSpec((tm, tn), lambda i, j, q: (q[0] * nbi + i, j))

    def body(q_ref, w_ref, *rest):
        del q_ref
        rest[-1][...] = w_ref[...].astype(BF16)

    return pl.pallas_call(
        body,
        name=name,
        grid_spec=pltpu.PrefetchScalarGridSpec(
            num_scalar_prefetch=1,
            grid=(nbi, nbj),
            in_specs=[pl.BlockSpec((tm, tn), lambda i, j, q: (i, j))] + [ANY] * len(after),
            out_specs=out_spec,
        ),
        out_shape=jax.ShapeDtypeStruct(full, BF16),
    )(qidx, w, *after)


def _adamw(name, w, g, m, v, after=()):
    tm, tn = _ew_tiles(w.shape, EW_BLOCK_ELEMS)
    if _nbytes(w.shape, F32) <= 1024 * 1024:
        tm, tn = w.shape
    spec = pl.BlockSpec((tm, tn), lambda i, j: (i, j))
    n_after = len(after)

    def body(w_ref, g_ref, m_ref, v_ref, *rest):
        d_ref, nm_ref, nv_ref, g_out_ref = rest[n_after:]
        gv = g_ref[...]
        g_out_ref[...] = gv
        nm = ADAM_B1 * m_ref[...] + (1.0 - ADAM_B1) * gv
        nv = ADAM_B2 * v_ref[...] + (1.0 - ADAM_B2) * (gv * gv)
        m_hat = nm / (1.0 - ADAM_B1**ADAM_STEP)
        v_hat = nv / (1.0 - ADAM_B2**ADAM_STEP)
        d_ref[...] = -ADAM_LR * (m_hat / (jnp.sqrt(v_hat) + ADAM_EPS) + ADAM_WD * w_ref[...])
        nm_ref[...] = nm
        nv_ref[...] = nv

    out = jax.ShapeDtypeStruct(w.shape, F32)
    return pl.pallas_call(
        body, name=name, grid=(w.shape[0] // tm, w.shape[1] // tn), in_specs=[spec] * 4 + [ANY] * n_after,
        out_specs=[spec] * 4, out_shape=[out, out, out, out],
        compiler_params=_params(_mm_vmem([((tm, tn), F32, 16)])),
    )(w, g, m, v, *after)


def _pair_add(name, cidx, g_full, r_sib, kind):
    hr, hc = r_sib.shape
    tm, tn = _ew_tiles((hr, hc), 2 * EW_BLOCK_ELEMS)
    nbi, nbj = hr // tm, hc // tn
    if kind == "col":
        g_spec = pl.BlockSpec((tm, tn), lambda i, j, c: (c[0] * nbi + i, j))
    else:
        g_spec = pl.BlockSpec((tm, tn), lambda i, j, c: (i, c[0] * nbj + j))
    spec = pl.BlockSpec((tm, tn), lambda i, j, c: (i, j))

    def body(c_ref, g_ref, r_ref, o_ref):
        del c_ref
        o_ref[...] = (g_ref[...].astype(F32) + r_ref[...].astype(F32)).astype(BF16)

    return pl.pallas_call(
        body,
        name=name,
        grid_spec=pltpu.PrefetchScalarGridSpec(num_scalar_prefetch=1, grid=(nbi, nbj), in_specs=[g_spec, spec], out_specs=spec),
        out_shape=jax.ShapeDtypeStruct((hr, hc), BF16),
        compiler_params=_params(_mm_vmem([((tm, tn), BF16, 6), ((tm, tn), F32, 3)])),
    )(cidx, g_full, r_sib)


def _chip_sum(name, qidx, c_half, r_ici, kind):
    _, pr, pc = r_ici.shape
    tm, tn = _ew_tiles((pr, pc), 2 * EW_BLOCK_ELEMS)
    nbi, nbj = pr // tm, pc // tn
    if kind == "col":
        own_spec = pl.BlockSpec((tm, tn), lambda i, j, q: (i, q[0] * nbj + j))
        full, out_spec = (2 * pr, pc), pl.BlockSpec((tm, tn), lambda i, j, q: (q[1] * nbi + i, j))
    else:
        own_spec = pl.BlockSpec((tm, tn), lambda i, j, q: (q[0] * nbi + i, j))
        full, out_spec = (pr, 2 * pc), pl.BlockSpec((tm, tn), lambda i, j, q: (i, q[1] * nbj + j))

    def body(q_ref, own_ref, r_ref, o_ref):
        q = q_ref[0]
        own = own_ref[...].astype(F32)
        recv = [r_ref[r].astype(F32) for r in range(3)]
        total = None
        for chip in range(N_CHIPS):
            d = chip ^ q
            term = jnp.where(d == 0, own, jnp.where(d == 2, recv[0], jnp.where(d == 1, recv[1], recv[2])))
            total = term if total is None else total + term
        o_ref[...] = total

    return pl.pallas_call(
        body,
        name=name,
        grid_spec=pltpu.PrefetchScalarGridSpec(
            num_scalar_prefetch=1,
            grid=(nbi, nbj),
            in_specs=[own_spec, pl.BlockSpec((3, tm, tn), lambda i, j, q: (0, i, j))],
            out_specs=out_spec,
        ),
        out_shape=jax.ShapeDtypeStruct(full, F32),
        compiler_params=_params(_mm_vmem([((tm, tn), BF16, 8), ((tm, tn), F32, 6)])),
    )(qidx, c_half, r_ici)


_REL_MASK = (2, 1, 3)


def _place():
    x, y, c = lax.axis_index("x"), lax.axis_index("y"), lax.axis_index("c")
    chips = [(1 - x, y), (x, 1 - y), (1 - x, 1 - y)]
    return x, y, c, 2 * x + y, chips


def _shard_view(ref, kind, chip):
    if kind == "col":
        w = ref.shape[1] // N_CHIPS
        return ref.at[:, pl.ds(pl.multiple_of(chip * w, LANES), w)]
    h = ref.shape[0] // N_CHIPS
    return ref.at[pl.ds(pl.multiple_of(chip * h, 16), h), :]


def _row_half(ref, half):
    h = ref.shape[0] // 2
    return ref.at[pl.ds(pl.multiple_of(half * h, 16), h), :]


def _pair_half(ref, kind, half):
    if kind == "col":
        return _row_half(ref, half)
    w = ref.shape[1] // 2
    return ref.at[:, pl.ds(pl.multiple_of(half * w, LANES), w)]


def _remote(src, dst, send_sem, recv_sem, dev):
    return pltpu.make_async_remote_copy(src_ref=src, dst_ref=dst, send_sem=send_sem, recv_sem=recv_sem, device_id=dev, device_id_type=MESH)


def _hbm(a):
    return pltpu.with_memory_space_constraint(a, pltpu.HBM)


def _gather_start(name, fulls, kinds):
    n_w = len(fulls)

    def body(*refs):
        g = refs[:n_w]
        send_sem, recv_sem = refs[n_w], refs[n_w + 1]
        token = refs[-1]
        _, _, c, q, chips = _place()
        for w in range(n_w):
            mine = _row_half(_shard_view(g[w], kinds[w], q), c)
            for r, chip in enumerate(chips):
                _remote(mine, mine, send_sem.at[3 * w + r], recv_sem.at[3 * w + r], (*chip, c)).start()
        token[...] = jnp.zeros_like(token)

    res = pl.pallas_call(
        body,
        name=name,
        out_shape=(
            pltpu.SemaphoreType.DMA((3 * n_w,)),
            pltpu.SemaphoreType.DMA((3 * n_w,)),
            *[pltpu.HBM(f.shape, f.dtype) for f in fulls],
            jax.ShapeDtypeStruct((8, LANES), F32),
        ),
        in_specs=[HBM_SPEC] * n_w,
        out_specs=(SEM_SPEC, SEM_SPEC, *[HBM_SPEC] * n_w, pl.BlockSpec(memory_space=pltpu.VMEM)),
        input_output_aliases={w: w + 2 for w in range(n_w)},
        compiler_params=pltpu.CompilerParams(has_side_effects=EFFECT),
    )(*[_hbm(f) for f in fulls])
    return res[0], res[1], list(res[2 : 2 + n_w]), res[-1]


def _gather_wait(name, fulls, kinds, w_ids, send_sem, recv_sem, after):
    n = len(fulls)

    def body(*refs):
        g = refs[:n]
        s_sem, r_sem = refs[n], refs[n + 1]
        x, y, c, q, _ = _place()
        for i, w in enumerate(w_ids):
            mine = _row_half(_shard_view(g[i], kinds[i], q), c)
            for r in range(3):
                landed = _row_half(_shard_view(g[i], kinds[i], q ^ _REL_MASK[r]), c)
                cp = _remote(mine, landed, s_sem.at[3 * w + r], r_sem.at[3 * w + r], (x, y, 1 - c))
                cp.wait_send()
                cp.wait_recv()

    res = pl.pallas_call(
        body,
        name=name,
        out_shape=[pltpu.HBM(f.shape, f.dtype) for f in fulls],
        in_specs=[HBM_SPEC] * n + [SEM_SPEC, SEM_SPEC, ANY],
        out_specs=[HBM_SPEC] * n,
        input_output_aliases={i: i for i in range(n)},
        compiler_params=pltpu.CompilerParams(has_side_effects=EFFECT),
    )(*fulls, send_sem, recv_sem, after)
    return list(res)


def _gather_forward(name, fulls, kinds):
    n = len(fulls)

    def body(*refs):
        g = refs[n : 2 * n]
        send, recv = refs[2 * n :]
        x, y, c, q, _ = _place()
        sib = (x, y, 1 - c)
        cps = []
        for i in range(n):
            for r in range(3):
                landed = _row_half(_shard_view(g[i], kinds[i], q ^ _REL_MASK[r]), c)
                cps.append(_remote(landed, landed, send.at[i, r], recv.at[i, r], sib))
        for cp in cps:
            cp.start()
        for i in range(n):
            for r in range(3):
                other = _row_half(_shard_view(g[i], kinds[i], q ^ _REL_MASK[r]), 1 - c)
                _remote(other, other, send.at[i, r], recv.at[i, r], sib).wait_recv()
        for cp in cps:
            cp.wait_send()

    res = pl.pallas_call(
        body,
        name=name,
        in_specs=[ANY] * n,
        out_specs=[ANY] * n,
        out_shape=[jax.ShapeDtypeStruct(f.shape, f.dtype) for f in fulls],
        scratch_shapes=[pltpu.SemaphoreType.DMA((n, 3)), pltpu.SemaphoreType.DMA((n, 3))],
        input_output_aliases={i: i for i in range(n)},
    )(*fulls)
    return list(res)


def _split_start(name, bufs, n_sems, copies):
    n = len(bufs)

    def body(*refs):
        for cp in copies(refs[:n], refs[n], refs[n + 1]):
            cp.start()

    res = pl.pallas_call(
        body,
        name=name,
        out_shape=(
            pltpu.SemaphoreType.DMA((n_sems,)),
            pltpu.SemaphoreType.DMA((n_sems,)),
            *[pltpu.HBM(b.shape, b.dtype) for b in bufs],
        ),
        in_specs=[HBM_SPEC] * n,
        out_specs=(SEM_SPEC, SEM_SPEC, *[HBM_SPEC] * n),
        input_output_aliases={i: i + 2 for i in range(n)},
        compiler_params=pltpu.CompilerParams(has_side_effects=EFFECT),
    )(*[_hbm(b) for b in bufs])
    return res[0], res[1], list(res[2:])


def _split_wait(name, bufs, send_sem, recv_sem, copies, after):
    n = len(bufs)

    def body(*refs):
        for cp in copies(refs[:n], refs[n], refs[n + 1]):
            cp.wait_send()
            cp.wait_recv()

    res = pl.pallas_call(
        body,
        name=name,
        out_shape=[pltpu.HBM(b.shape, b.dtype) for b in bufs],
        in_specs=[HBM_SPEC] * n + [SEM_SPEC, SEM_SPEC, ANY],
        out_specs=[HBM_SPEC] * n,
        input_output_aliases={i: i for i in range(n)},
        compiler_params=pltpu.CompilerParams(has_side_effects=EFFECT),
    )(*bufs, send_sem, recv_sem, after)
    return list(res)


def _pair_exchange_copies(kinds):
    n = len(kinds)

    def copies(refs, send_sem, recv_sem):
        x, y, c, _, _ = _place()
        return [
            _remote(_pair_half(refs[w], kinds[w], 1 - c), refs[n + w], send_sem.at[w], recv_sem.at[w], (x, y, 1 - c))
            for w in range(n)
        ]

    return copies


def _pair_share_copies(kinds, waiting):
    def copies(refs, send_sem, recv_sem):
        x, y, c, _, _ = _place()
        out = []
        for w, kind in enumerate(kinds):
            mine = _pair_half(refs[w], kind, c)
            dst = _pair_half(refs[w], kind, 1 - c) if waiting else mine
            out.append(_remote(mine, dst, send_sem.at[w], recv_sem.at[w], (x, y, 1 - c)))
        return out

    return copies


def _piece_shape(half_shape, kind):
    r, c = half_shape
    return (3, r, c // N_CHIPS) if kind == "col" else (3, r // N_CHIPS, c)


def _chip_send_start(name, halves, kinds):
    n = len(halves)
    lands = [lax.empty(_piece_shape(h.shape, k), BF16) for h, k in zip(halves, kinds)]

    def body(*refs):
        h, land = refs[:n], refs[n : 2 * n]
        send_sem, recv_sem = refs[2 * n], refs[2 * n + 1]
        _, _, c, q, chips = _place()
        for i in range(n):
            for r, chip in enumerate(chips):
                piece = _shard_view(h[i], kinds[i], q ^ _REL_MASK[r])
                _remote(piece, land[i].at[r], send_sem.at[3 * i + r], recv_sem.at[3 * i + r], (*chip, c)).start()

    res = pl.pallas_call(
        body,
        name=name,
        out_shape=(
            pltpu.SemaphoreType.DMA((3 * n,)),
            pltpu.SemaphoreType.DMA((3 * n,)),
            *[pltpu.HBM(a.shape, a.dtype) for a in halves],
            *[pltpu.HBM(a.shape, a.dtype) for a in lands],
        ),
        in_specs=[HBM_SPEC] * (2 * n),
        out_specs=(SEM_SPEC, SEM_SPEC, *[HBM_SPEC] * (2 * n)),
        input_output_aliases={i: i + 2 for i in range(2 * n)},
        compiler_params=pltpu.CompilerParams(has_side_effects=EFFECT),
    )(*[_hbm(a) for a in halves], *[_hbm(a) for a in lands])
    return res[0], res[1], list(res[2 : 2 + n]), list(res[2 + n :])


def _chip_send_wait(name, halves, lands, kinds, send_sem, recv_sem, after):
    n = len(halves)

    def body(*refs):
        h, land = refs[:n], refs[n : 2 * n]
        s_sem, r_sem = refs[2 * n], refs[2 * n + 1]
        x, y, c, q, _ = _place()
        for i in range(n):
            for r in range(3):
                piece = _shard_view(h[i], kinds[i], q ^ _REL_MASK[r])
                cp = _remote(piece, land[i].at[r], s_sem.at[3 * i + r], r_sem.at[3 * i + r], (x, y, 1 - c))
                cp.wait_send()
                cp.wait_recv()

    res = pl.pallas_call(
        body,
        name=name,
        out_shape=[pltpu.HBM(a.shape, a.dtype) for a in halves] + [pltpu.HBM(a.shape, a.dtype) for a in lands],
        in_specs=[HBM_SPEC] * (2 * n) + [SEM_SPEC, SEM_SPEC, ANY],
        out_specs=[HBM_SPEC] * (2 * n),
        input_output_aliases={i: i for i in range(2 * n)},
        compiler_params=pltpu.CompilerParams(has_side_effects=EFFECT),
    )(*halves, *lands, send_sem, recv_sem, after)
    return list(res[:n]), list(res[n:])


def _small_all_reduce(p):
    rows = p.shape[0]
    n_dev = 2 * N_CHIPS

    def body(p_ref, o_ref, buf, loc_sem, send, recv):
        x, y, c, q, _ = _place()
        me = 2 * q + c
        own = pltpu.make_async_copy(p_ref, buf.at[me], loc_sem)
        own.start()
        cps = []
        for d in range(1, n_dev):
            dev = (x ^ ((d >> 2) & 1), y ^ ((d >> 1) & 1), c ^ (d & 1))
            cps.append(_remote(p_ref, buf.at[me], send.at[d - 1], recv.at[d - 1], dev))
        for cp in cps:
            cp.start()
        for d in range(1, n_dev):
            slot = buf.at[me ^ d]
            _remote(slot, slot, send.at[d - 1], recv.at[d - 1], (x, y, c)).wait_recv()
        own.wait()
        total = buf[0]
        for d in range(1, n_dev):
            total = total + buf[d]
        o_ref[...] = total
        for cp in cps:
            cp.wait_send()

    return pl.pallas_call(
        body,
        name="small_all_reduce",
        in_specs=[ANY],
        out_specs=pl.BlockSpec(memory_space=pltpu.VMEM),
        out_shape=jax.ShapeDtypeStruct(p.shape, F32),
        scratch_shapes=[
            pltpu.VMEM((n_dev, rows, LANES), F32),
            pltpu.SemaphoreType.DMA,
            pltpu.SemaphoreType.DMA((n_dev - 1,)),
            pltpu.SemaphoreType.DMA((n_dev - 1,)),
        ],
    )(p)


def _small_exchange_copies(waiting):
    def copies(refs, send_sem, recv_sem):
        p, land = refs
        x, y, c, q, _ = _place()
        me = 2 * q + c
        out = []
        for dd in range(1, 2 * N_CHIPS):
            dev = (x ^ ((dd >> 2) & 1), y ^ ((dd >> 1) & 1), c ^ (dd & 1))
            dst = land.at[me ^ dd] if waiting else land.at[me]
            out.append(_remote(p, dst, send_sem.at[dd - 1], recv_sem.at[dd - 1], dev))
        return out

    return copies


def _small_sum(name, me_idx, p, land):
    rows = p.shape[0]
    n_dev = 2 * N_CHIPS

    def body(me_ref, p_ref, land_ref, o_ref):
        me = me_ref[0]
        total = None
        for dev in range(n_dev):
            term = jnp.where(me == dev, p_ref[...], land_ref[dev])
            total = term if total is None else total + term
        o_ref[...] = total

    return pl.pallas_call(
        body,
        name=name,
        grid_spec=pltpu.PrefetchScalarGridSpec(
            num_scalar_prefetch=1,
            grid=(1,),
            in_specs=[pl.BlockSpec((rows, LANES), lambda i, m: (0, 0)), pl.BlockSpec((n_dev, rows, LANES), lambda i, m: (0, 0, 0))],
            out_specs=pl.BlockSpec((rows, LANES), lambda i, m: (0, 0)),
        ),
        out_shape=jax.ShapeDtypeStruct(p.shape, F32),
    )(me_idx, p, land)


def _pack(parts):
    rows = []
    for a in parts:
        flat = a.reshape(-1).astype(F32)
        n = flat.shape[0]
        padded = -(-n // (8 * LANES)) * (8 * LANES)
        rows.append(jnp.pad(flat, (0, padded - n)).reshape(-1, LANES))
    return jnp.concatenate(rows, axis=0)


def _unpack(packed, shapes):
    out, row = [], 0
    for shp in shapes:
        n = int(np.prod(shp))
        nrows = -(-n // (8 * LANES)) * 8
        out.append(packed[row : row + nrows].reshape(-1)[:n].reshape(shp))
        row += nrows
    return out


def kernel(x, w_in, norm_mix, sgu_v_gain, sgu_w_s, sgu_b_s, w_a_out, attn_sink, rel_bias, w_b_out, w_o, norm_ffn, w_gate, w_up, w_down, norm_final, loss_target, m_w_in, m_norm_mix, m_sgu_v_gain, m_sgu_w_s, m_sgu_b_s, m_w_a_out, m_attn_sink, m_rel_bias, m_w_b_out, m_w_o, m_norm_ffn, m_w_gate, m_w_up, m_w_down, m_norm_final, v_w_in, v_norm_mix, v_sgu_v_gain, v_sgu_w_s, v_sgu_b_s, v_w_a_out, v_attn_sink, v_rel_bias, v_w_b_out, v_w_o, v_norm_ffn, v_w_gate, v_w_up, v_w_down, v_norm_final):
    s, d = x.shape[1], x.shape[2]
    w_sgu = sgu_v_gain.shape[1]
    groups = sgu_w_s.shape[1]
    heads = attn_sink.shape[1]
    grp = heads // N_KV_HEADS
    w_att = heads * HEAD_DIM
    w_kv = N_KV_HEADS * HEAD_DIM
    d_ff = w_gate.shape[2] * N_CHIPS
    n_in = w_in.shape[2] * N_CHIPS
    off_q = 2 * w_sgu
    off_k = off_q + w_att
    off_g = off_k + 2 * w_kv
    assert n_in == off_g + 2 * d and groups * BLK == w_sgu and s % BLK == 0

    x2d = x.reshape(s, d)
    tgt = loss_target.reshape(s, d)
    c_idx = lax.axis_index("c").astype(I32).reshape(1)
    q_idx = (2 * lax.axis_index("x") + lax.axis_index("y")).astype(I32).reshape(1)
    qc_idx = jnp.concatenate([q_idx, c_idx])

    W_IN, W_A, W_B, W_O, W_GATE, W_UP, W_DOWN = range(7)
    names = ["w_in", "w_a", "w_b", "w_o", "w_gate", "w_up", "w_down"]
    kinds = ["col", "col", "col", "row", "col", "col", "row"]
    big_w = [w_in[0], w_a_out[0], w_b_out[0], w_o[0], w_gate[0], w_up[0], w_down[0]]
    big_m = [m_w_in[0], m_w_a_out[0], m_w_b_out[0], m_w_o[0], m_w_gate[0], m_w_up[0], m_w_down[0]]
    big_v = [v_w_in[0], v_w_a_out[0], v_w_b_out[0], v_w_o[0], v_w_gate[0], v_w_up[0], v_w_down[0]]
    full_in = _cast_into_full("cast_w_in", q_idx, big_w[W_IN], kinds[W_IN])
    in_send, in_recv, (full_in,), token = _gather_start("gather_start_in", [full_in], [kinds[W_IN]])
    rest = [_cast_into_full("cast_" + names[i], q_idx, big_w[i], kinds[i], after=(token,)) for i in range(1, 7)]
    ag_send, ag_recv, rest, token = _gather_start("gather_start_rest", rest, kinds[1:])
    fulls = [full_in] + rest

    def gathered(tag, ids, after):
        if ids == [W_IN]:
            sems, pos = (in_send, in_recv), [0]
        else:
            sems, pos = (ag_send, ag_recv), [i - 1 for i in ids]
        got = _gather_wait("gather_wait_" + tag, [fulls[i] for i in ids], [kinds[i] for i in ids], pos, *sems, after)
        return _gather_forward("gather_fwd_" + tag, got, [kinds[i] for i in ids])

    ws_b = sgu_w_s[0].astype(BF16)
    wst_b = jnp.swapaxes(sgu_w_s[0], 1, 2).astype(BF16)
    b_col = sgu_b_s[0].reshape(groups, BLK, 1)
    bias_tab, bucket = _band_tables(rel_bias)
    sink = attn_sink[0]

    tm = _tile(s, (1024, 512, 256, 128))

    h1 = _rms_fwd("rms_mix", x2d, norm_mix, after=(token,))
    (g_in,) = gathered("in", [W_IN], h1)

    tn = _tile(n_in, (768, 640, 512))
    z = _mm(
        "mm_z", (s // tm, n_in // tn, 1), [h1, g_in],
        [pl.BlockSpec((tm, d), lambda i, j, k: (i, 0)), pl.BlockSpec((d, tn), lambda i, j, k: (0, j))],
        [jax.ShapeDtypeStruct((s, n_in), BF16)], [pl.BlockSpec((tm, tn), lambda i, j, k: (i, j))],
        [(0, 1, NN, 0)], 1, (tm, tn), 1, lambda ins, vals, outs, cs: _put(outs[0], cs, vals[0]),
        _mm_vmem([((tm, d), BF16, 2), ((d, tn), BF16, 2), ((tm, tn), F32, 3)]),
    )[0]
    g_a, g_b, g_o = gathered("mix", [W_A, W_B, W_O], z)

    a_act = _sgu_fwd(z, sgu_v_gain, ws_b, b_col, w_sgu)

    kv_b = z[:, off_k:off_g]
    k_pad = jnp.pad(kv_b[:, :w_kv], ((BLK, BLK), (0, 0)))
    v_pad = jnp.pad(kv_b[:, w_kv:], ((BLK, BLK), (0, 0)))
    q_blk0 = off_q // (grp * HEAD_DIM)
    att = _attn_fwd(sink, z, k_pad, v_pad, bias_tab, grp, q_blk0)

    tg = _tile(d, (512,))
    ga0, gb0 = off_g // tg, (off_g + d) // tg

    def ep_gate(ins, vals, outs, cs):
        sa, sb = _sigmoid(ins[4][:, cs].astype(F32)), _sigmoid(ins[5][:, cs].astype(F32))
        _put(outs[0], cs, sa * vals[0] + sb * vals[1])
        _put(outs[1], cs, vals[0])
        _put(outs[2], cs, vals[1])

    t_out = pl.BlockSpec((tm, tg), lambda i, j, k: (i, j))
    m_act, y_a, y_b = _mm(
        "mm_branches", (s // tm, d // tg, 1), [a_act, g_a, att, g_b, z, z],
        [pl.BlockSpec((tm, w_sgu), lambda i, j, k: (i, 0)), pl.BlockSpec((w_sgu, tg), lambda i, j, k: (0, j)),
         pl.BlockSpec((tm, w_att), lambda i, j, k: (i, 0)), pl.BlockSpec((w_att, tg), lambda i, j, k: (0, j)),
         pl.BlockSpec((tm, tg), lambda i, j, k: (i, ga0 + j)), pl.BlockSpec((tm, tg), lambda i, j, k: (i, gb0 + j))],
        [jax.ShapeDtypeStruct((s, d), BF16)] * 3,
        [t_out, t_out, t_out], [(0, 1, NN, 0), (2, 3, NN, 1)], 2, (tm, tg), 1, ep_gate,
        _mm_vmem([((tm, w_sgu), BF16, 4), ((w_sgu, tg), BF16, 4), ((tm, tg), F32, 12)]),    )

    tn = _tile(d, (1024, 512))

    def ep_residual(ins, vals, outs, cs):
        _put(outs[0], cs, ins[2][:, cs] + vals[0])

    x2 = _mm(
        "mm_wo", (s // tm, d // tn, 1), [m_act, g_o, x2d],
        [pl.BlockSpec((tm, d), lambda i, j, k: (i, 0)), pl.BlockSpec((d, tn), lambda i, j, k: (0, j)),
         pl.BlockSpec((tm, tn), lambda i, j, k: (i, j))],
        [jax.ShapeDtypeStruct((s, d), F32)], [pl.BlockSpec((tm, tn), lambda i, j, k: (i, j))],
        [(0, 1, NN, 0)], 1, (tm, tn), 1, ep_residual,
        _mm_vmem([((tm, d), BF16, 2), ((d, tn), BF16, 2), ((tm, tn), F32, 5)]),    )[0]

    h2 = _rms_fwd("rms_ffn", x2, norm_ffn)
    g_gate, g_up = gathered("ffn_in", [W_GATE, W_UP], h2)

    tf = _tile(d_ff, (512,))

    def ep_swiglu(ins, vals, outs, cs):
        gt, up = vals
        _put(outs[0], cs, gt)
        _put(outs[1], cs, up)
        _put(outs[2], cs, (gt * _sigmoid(gt)) * up)

    f_out = pl.BlockSpec((tm, tf), lambda i, j, k: (i, j))
    gt, up, f_act = _mm(
        "mm_gate_up", (s // tm, d_ff // tf, 1), [h2, g_gate, g_up],
        [pl.BlockSpec((tm, d), lambda i, j, k: (i, 0)), pl.BlockSpec((d, tf), lambda i, j, k: (0, j)),
         pl.BlockSpec((d, tf), lambda i, j, k: (0, j))],
        [jax.ShapeDtypeStruct((s, d_ff), BF16)] * 3,
        [f_out, f_out, f_out], [(0, 1, NN, 0), (0, 2, NN, 1)], 2, (tm, tf), 1, ep_swiglu,
        _mm_vmem([((tm, d), BF16, 2), ((d, tf), BF16, 4), ((tm, tf), F32, 8)]),    )
    (g_down,) = gathered("ffn_out", [W_DOWN], f_act)

    tkf = _tile(d_ff, (1408, 1024, 512))
    nkf = d_ff // tkf
    tml, tnl = _tile(s, (512, 256, 128)), _tile(d, (512,))
    x3 = _mm(
        "mm_down", (s // tml, d // tnl, 1), [f_act, g_down, x2],
        [pl.BlockSpec((tml, d_ff), lambda i, j, k: (i, 0)), pl.BlockSpec((d_ff, tnl), lambda i, j, k: (0, j)),
         pl.BlockSpec((tml, tnl), lambda i, j, k: (i, j))],
        [jax.ShapeDtypeStruct((s, d), F32)], [pl.BlockSpec((tml, tnl), lambda i, j, k: (i, j))],
        [(0, 1, NN, 0)], 1, (tml, tnl), 1, ep_residual,
        _mm_vmem([((tml, d_ff), BF16, 2), ((d_ff, tnl), BF16, 2), ((tml, tnl), F32, 6)]),    )[0]

    dx3, dx3b, dg_final, loss_part = _head(x3, norm_final.reshape(1, d), tgt)

    def reduce_a(tag, ids, grads):
        ks = [kinds[i] for i in ids]
        lands = [lax.empty((g.shape[0] // 2, g.shape[1]) if k == "col" else (g.shape[0], g.shape[1] // 2), BF16)
                 for g, k in zip(grads, ks)]
        send, recv, bufs = _split_start("pair_send_" + tag, list(grads) + lands, len(ids), _pair_exchange_copies(ks))
        return {"tag": tag, "ids": ids, "ks": ks, "pair": (send, recv, bufs), "token": bufs[0]}

    def reduce_b(st, after):
        tag, ids, ks = st["tag"], st["ids"], st["ks"]
        send, recv, bufs = st["pair"]
        bufs = _split_wait("pair_wait_" + tag, bufs, send, recv, _pair_exchange_copies(ks), after)
        grads, from_sib = bufs[: len(ids)], bufs[len(ids) :]
        halves = [_pair_add("pair_add_" + names[i], c_idx, g, r, k) for i, g, r, k in zip(ids, grads, from_sib, ks)]
        st["chip"] = _chip_send_start("chip_send_" + tag, halves, ks)
        st["token"] = st["chip"][2][0]

    def reduce_c(st, after):
        tag, ids, ks = st["tag"], st["ids"], st["ks"]
        send, recv, halves, lands = st["chip"]
        halves, lands = _chip_send_wait("chip_wait_" + tag, halves, lands, ks, send, recv, after)
        pieces = [_chip_sum("chip_sum_" + names[i], qc_idx, h, r, k) for i, h, r, k in zip(ids, halves, lands, ks)]
        st["share"] = _split_start("share_send_" + tag, pieces, len(ids), _pair_share_copies(ks, False))
        st["token"] = st["share"][2][0]

    def reduce_d(st, after):
        send, recv, bufs = st["share"]
        return _split_wait("share_wait_" + st["tag"], bufs, send, recv, _pair_share_copies(st["ks"], True), after)

    def ep_swiglu_bwd(ins, vals, outs, cs):
        df = vals[0]
        gtv, upv = ins[2][:, cs].astype(F32), ins[3][:, cs].astype(F32)
        sg = _sigmoid(gtv)
        _put(outs[0], cs, df * upv * (sg + gtv * sg * (1.0 - sg)))
        _put(outs[1], cs, df * (gtv * sg))

    dgt, dup = _mm(
        "mm_dswiglu", (s // tm, d_ff // tf, 1), [dx3b, g_down, gt, up],
        [pl.BlockSpec((tm, d), lambda i, j, k: (i, 0)), pl.BlockSpec((tf, d), lambda i, j, k: (j, 0)), f_out, f_out],
        [jax.ShapeDtypeStruct((s, d_ff), BF16), jax.ShapeDtypeStruct((s, d_ff), BF16)], [f_out, f_out],
        [(0, 1, NT, 0)], 1, (tm, tf), 1, ep_swiglu_bwd,
        _mm_vmem([((tm, d), BF16, 2), ((tf, d), BF16, 2), ((tm, tf), F32, 8)]),    )

    def ep_store(ins, vals, outs, cs):
        for o, v in zip(outs, vals):
            _put(o, cs, v)

    twn = _tile(d, (1024, 512))
    gw_down = _mm(
        "mm_gw_down", (d_ff // tkf, d // twn, 1), [f_act, dx3b],
        [pl.BlockSpec((s, tkf), lambda i, j, k: (0, i)), pl.BlockSpec((s, twn), lambda i, j, k: (0, j))],
        [jax.ShapeDtypeStruct((d_ff, d), BF16)], [pl.BlockSpec((tkf, twn), lambda i, j, k: (i, j))],
        [(0, 1, TN, 0)], 1, (tkf, twn), 1, ep_store,
        _mm_vmem([((s, tkf), BF16, 3), ((s, twn), BF16, 2), ((tkf, twn), F32, 3)]),
    )[0]
    red_down = reduce_a("down", [W_DOWN], [gw_down])

    tn2 = _tile(d, (256,))
    dh2_specs = [pl.BlockSpec((tm, d_ff), lambda i, j, k: (i, 0)), pl.BlockSpec((tn2, d_ff), lambda i, j, k: (j, 0))]
    dh2_tile = pl.BlockSpec((tm, tn2), lambda i, j, k: (i, j))
    dh2_vmem = _mm_vmem([((tm, d_ff), BF16, 2), ((tn2, d_ff), BF16, 2), ((tm, tn2), F32, 7)])
    dh2 = _mm(
        "mm_dh2_gate", (s // tm, d // tn2, 1), [dgt, g_gate], dh2_specs,
        [jax.ShapeDtypeStruct((s, d), F32)], [dh2_tile], [(0, 1, NT, 0)], 1, (tm, tn2), 1, ep_store, dh2_vmem,
        after=(red_down["token"],),
    )[0]
    dh2 = _mm(
        "mm_dh2_up", (s // tm, d // tn2, 1), [dup, g_up, dh2], dh2_specs + [dh2_tile],
        [jax.ShapeDtypeStruct((s, d), F32)], [dh2_tile], [(0, 1, NT, 0)], 1, (tm, tn2), 1, ep_residual, dh2_vmem,
    )[0]
    reduce_b(red_down, dh2)

    twr = _tile(d, (1024, 512))
    w_tile = pl.BlockSpec((twr, tf), lambda i, j, k: (i, j))
    gw_gate, gw_up = _mm(
        "mm_gw_gate_up", (d // twr, d_ff // tf, 1), [h2, dgt, dup],
        [pl.BlockSpec((s, twr), lambda i, j, k: (0, i)), pl.BlockSpec((s, tf), lambda i, j, k: (0, j)),
         pl.BlockSpec((s, tf), lambda i, j, k: (0, j))],
        [jax.ShapeDtypeStruct((d, d_ff), BF16), jax.ShapeDtypeStruct((d, d_ff), BF16)], [w_tile, w_tile],
        [(0, 1, TN, 0), (0, 2, TN, 1)], 2, (twr, tf), 1, ep_store,
        _mm_vmem([((s, twr), BF16, 3), ((s, tf), BF16, 4), ((twr, tf), F32, 6)]),
        after=(red_down["token"],),
    )
    red_ffn = reduce_a("ffn_in", [W_GATE, W_UP], [gw_gate, gw_up])

    dx2, dx2b, dg_ffn = _rms_bwd("rms_ffn_bwd", x2, norm_ffn, dh2, dx3, after=(red_ffn["token"],))

    nj = d // tg

    def lo(j):
        return jnp.minimum(j, nj - 1)

    def gate_bwd_body(dx_ref, wo_ref, ga_ref, gb_ref, ya_ref, yb_ref, dya_ref, dyb_ref, dz_ref, keep):
        j = pl.program_id(1)

        @pl.when(j < nj)
        def _():
            dm = lax.dot_general(dx_ref[...], wo_ref[...], NT, preferred_element_type=F32)
            sa, sb = _sigmoid(ga_ref[...].astype(F32)), _sigmoid(gb_ref[...].astype(F32))
            dya_ref[...] = (dm * sa).astype(BF16)
            dyb_ref[...] = (dm * sb).astype(BF16)
            dz_ref[...] = (dm * ya_ref[...].astype(F32) * (sa * (1.0 - sa))).astype(BF16)
            keep[lo(j)] = (dm * yb_ref[...].astype(F32) * (sb * (1.0 - sb))).astype(BF16)

        @pl.when(j >= nj)
        def _():
            dz_ref[...] = keep[jnp.maximum(j - nj, 0)]

    t_lo = pl.BlockSpec((tm, tg), lambda i, j: (i, lo(j)))
    dya, dyb, dz = pl.pallas_call(
        gate_bwd_body,
        name="mm_dgate",
        grid=(s // tm, 2 * nj),
        in_specs=[
            pl.BlockSpec((tm, d), lambda i, j: (i, 0)),
            pl.BlockSpec((tg, d), lambda i, j: (lo(j), 0)),
            pl.BlockSpec((tm, tg), lambda i, j: (i, ga0 + lo(j))),
            pl.BlockSpec((tm, tg), lambda i, j: (i, gb0 + lo(j))),
            t_lo,
            t_lo,
        ],
        out_specs=[t_lo, t_lo, pl.BlockSpec((tm, tg), lambda i, j: (i, ga0 + j))],
        out_shape=[jax.ShapeDtypeStruct((s, d), BF16), jax.ShapeDtypeStruct((s, d), BF16), jax.ShapeDtypeStruct((s, n_in), BF16)],
        scratch_shapes=[pltpu.VMEM((nj, tm, tg), BF16)],
        compiler_params=_params(_mm_vmem([((tm, d), BF16, 2), ((tg, d), BF16, 2), ((tm, tg), F32, 14), ((nj, tm, tg), BF16, 1)])),
    )(dx2b, g_o, z, z, y_a, y_b)
    reduce_b(red_ffn, dya)

    gw_o = _mm(
        "mm_gw_o", (d // twr, d // twn, 1), [m_act, dx2b],
        [pl.BlockSpec((s, twr), lambda i, j, k: (0, i)), pl.BlockSpec((s, twn), lambda i, j, k: (0, j))],
        [jax.ShapeDtypeStruct((d, d), BF16)], [pl.BlockSpec((twr, twn), lambda i, j, k: (i, j))],
        [(0, 1, TN, 0)], 1, (twr, twn), 1, ep_store,
        _mm_vmem([((s, twr), BF16, 3), ((s, twn), BF16, 2), ((twr, twn), F32, 3)]),
        after=(red_ffn["token"],),
    )[0]
    red_o = reduce_a("w_o", [W_O], [gw_o])

    tb = _tile(w_sgu, (1024, 512))
    b_out = pl.BlockSpec((tm, tb), lambda i, j, k: (i, j))

    da, datt = _mm(
        "mm_dbranches", (s // tm, w_sgu // tb, 1), [dya, g_a, dyb, g_b],
        [pl.BlockSpec((tm, d), lambda i, j, k: (i, 0)), pl.BlockSpec((tb, d), lambda i, j, k: (j, 0)),
         pl.BlockSpec((tm, d), lambda i, j, k: (i, 0)), pl.BlockSpec((tb, d), lambda i, j, k: (j, 0))],
        [jax.ShapeDtypeStruct((s, w_sgu), BF16), jax.ShapeDtypeStruct((s, w_att), BF16)], [b_out, b_out],
        [(0, 1, NT, 0), (2, 3, NT, 1)], 2, (tm, tb), 1, ep_store,
        _mm_vmem([((tm, d), BF16, 4), ((tb, d), BF16, 4), ((tm, tb), F32, 6)]),        after=(red_o["token"],),
    )
    reduce_b(red_o, da)

    wb_tile = pl.BlockSpec((tb, twn), lambda i, j, k: (i, j))
    gw_a, gw_b = _mm(
        "mm_gw_branches", (w_sgu // tb, d // twn, 1), [a_act, dya, att, dyb],
        [pl.BlockSpec((s, tb), lambda i, j, k: (0, i)), pl.BlockSpec((s, twn), lambda i, j, k: (0, j)),
         pl.BlockSpec((s, tb), lambda i, j, k: (0, i)), pl.BlockSpec((s, twn), lambda i, j, k: (0, j))],
        [jax.ShapeDtypeStruct((w_sgu, d), BF16), jax.ShapeDtypeStruct((w_att, d), BF16)], [wb_tile, wb_tile],
        [(0, 1, TN, 0), (2, 3, TN, 1)], 2, (tb, twn), 1, ep_store,
        _mm_vmem([((s, tb), BF16, 5), ((s, twn), BF16, 4), ((tb, twn), F32, 6)]),
        after=(red_o["token"],),
    )
    red_mix = reduce_a("mix", [W_A, W_B], [gw_a, gw_b])

    dz, dws, dbs, dgain = _sgu_bwd(z, da, sgu_v_gain, ws_b, wst_b, b_col, w_sgu, dz, after=(red_mix["token"],))
    dz, dk_pad, dv_pad, dbias_tab, dsink = _attn_bwd(sink, z, k_pad, v_pad, bias_tab, datt, dz, grp, q_blk0)
    dz = _dkv_to_dz(dk_pad, dv_pad, dz, off_k // (2 * w_kv))
    drel = _relbias_bwd(dbias_tab, bucket)
    reduce_b(red_mix, dz)

    small_w = [norm_mix, sgu_v_gain, sgu_w_s, sgu_b_s, attn_sink, rel_bias, norm_ffn, norm_final]
    small_m = [m_norm_mix, m_sgu_v_gain, m_sgu_w_s, m_sgu_b_s, m_attn_sink, m_rel_bias, m_norm_ffn, m_norm_final]
    small_v = [v_norm_mix, v_sgu_v_gain, v_sgu_w_s, v_sgu_b_s, v_attn_sink, v_rel_bias, v_norm_ffn, v_norm_final]
    small_shapes = [w.shape for w in small_w]
    early = [dgain, dws, dbs, dsink[:, 0], drel[:, :REL_BUCKETS].T, dg_ffn, dg_final]
    p_early = _pack([g.reshape(shp) for g, shp in zip(early, small_shapes[1:])] + [loss_part[0, :1]])
    land = jnp.zeros((2 * N_CHIPS,) + p_early.shape, F32)
    sm_send, sm_recv, (p_early, land) = _split_start("small_send", [p_early, land], 2 * N_CHIPS - 1, _small_exchange_copies(False))

    tzn = _tile(n_in, (768, 640, 512))
    gw_in = _mm(
        "mm_gw_in", (d // twr, n_in // tzn, 1), [h1, dz],
        [pl.BlockSpec((s, twr), lambda i, j, k: (0, i)), pl.BlockSpec((s, tzn), lambda i, j, k: (0, j))],
        [jax.ShapeDtypeStruct((d, n_in), BF16)], [pl.BlockSpec((twr, tzn), lambda i, j, k: (i, j))],
        [(0, 1, TN, 0)], 1, (twr, tzn), 1, ep_store,
        _mm_vmem([((s, twr), BF16, 3), ((s, tzn), BF16, 2), ((twr, tzn), F32, 3)]),
        after=(red_mix["token"], p_early),
    )[0]
    red_in = reduce_a("w_in", [W_IN], [gw_in])

    dh1 = _mm(
        "mm_dh1", (s // tm, d // tn2, 1), [dz, g_in],
        [pl.BlockSpec((tm, n_in), lambda i, j, k: (i, 0)), pl.BlockSpec((tn2, n_in), lambda i, j, k: (j, 0))],
        [jax.ShapeDtypeStruct((s, d), F32)], [pl.BlockSpec((tm, tn2), lambda i, j, k: (i, j))],
        [(0, 1, NT, 0)], 1, (tm, tn2), 1, ep_store,
        _mm_vmem([((tm, n_in), BF16, 2), ((tn2, n_in), BF16, 2), ((tm, tn2), F32, 5)]),
        after=(red_in["token"],),
    )[0]

    reduce_b(red_in, dh1)
    grad_x, _, dg_mix = _rms_bwd("rms_mix_bwd", x2d, norm_mix, dh1, dx2, after=(red_in["token"],))

    p_mix = _pack([dg_mix.reshape(small_shapes[0])])
    land_mix = jnp.zeros((2 * N_CHIPS,) + p_mix.shape, F32)
    mx_send, mx_recv, (p_mix, land_mix) = _split_start("mix_send", [p_mix, land_mix], 2 * N_CHIPS - 1, _small_exchange_copies(False))

    grads_big, upd = [None] * 7, [None] * 7

    def finish(st, after):
        for i, g in zip(st["ids"], reduce_d(st, after)):
            upd[i] = _adamw("adamw_" + names[i], big_w[i], g, big_m[i], big_v[i])
            grads_big[i] = upd[i][3]
            after = upd[i][0]
        return after

    after, prev = p_mix, None
    for st in (red_down, red_ffn, red_o, red_mix, red_in):
        reduce_c(st, after)
        after = st["token"] if prev is None else finish(prev, st["token"])
        prev = st

    p_early, land = _split_wait("small_wait", [p_early, land], sm_send, sm_recv, _small_exchange_copies(True), after)
    p_mix, land_mix = _split_wait("mix_wait", [p_mix, land_mix], mx_send, mx_recv, _small_exchange_copies(True), p_early)
    me_idx = 2 * q_idx + c_idx
    packed_g = jnp.concatenate([_small_sum("mix_sum", me_idx, p_mix, land_mix), _small_sum("small_sum", me_idx, p_early, land)], axis=0)
    g_small = _unpack(packed_g, small_shapes + [(1,)])
    loss = g_small[-1].reshape(())
    g_small = g_small[:-1]
    zero1 = jnp.zeros((1,), F32)
    pw, pg, pm, pv = _pack(small_w + [zero1]), _pack(g_small + [zero1]), _pack(small_m + [zero1]), _pack(small_v + [zero1])
    small_upd = _adamw("adamw_small", pw, pg, pm, pv)
    d_small, nm_small, nv_small = [_unpack(a, small_shapes) for a in small_upd[:3]]
    finish(prev, small_upd[0])

    small_names = ["norm_mix", "sgu_v_gain", "sgu_w_s", "sgu_b_s", "attn_sink", "rel_bias", "norm_ffn", "norm_final"]
    table = {}
    for i, n in enumerate(names):
        table[n] = (grads_big[i][None], upd[i][0][None], upd[i][1][None], upd[i][2][None])
    for i, n in enumerate(small_names):
        table[n] = (g_small[i], d_small[i], nm_small[i], nv_small[i])
    order = ["w_in", "norm_mix", "sgu_v_gain", "sgu_w_s", "sgu_b_s", "w_a", "attn_sink", "rel_bias", "w_b", "w_o", "norm_ffn",
             "w_gate", "w_up", "w_down", "norm_final"]
    outs = [loss, grad_x.reshape(1, s, d)]
    for part in range(4):
        outs += [table[n][part] for n in order]
    return tuple(outs)
```

```python
import math

import jax
import jax.numpy as jnp
import numpy as np
from jax import lax
from jax.experimental import pallas as pl
from jax.experimental.pallas import tpu as pltpu

F32 = jnp.float32
BF16 = jnp.bfloat16
I32 = jnp.int32
MESH = pl.DeviceIdType.MESH

EPS = 1e-6
NEG = -1e30
BLK = 128
HEAD_DIM = 128
N_KV_HEADS = 2
REL_BUCKETS = 32
REL_MAX_DIST = 128
N_CHIPS = 4
ADAM_LR, ADAM_B1, ADAM_B2, ADAM_EPS, ADAM_WD, ADAM_STEP = 0.001, 0.9, 0.999, 1e-08, 0.01, 10

LANES = 128
MXU_COLS = 256
VMEM_CAP = 60 * 1024 * 1024

NN = (((1,), (0,)), ((), ()))
NT = (((1,), (1,)), ((), ()))
TN = (((0,), (0,)), ((), ()))
ANY = pl.BlockSpec(memory_space=pl.ANY)
HBM_SPEC = pl.BlockSpec(memory_space=pltpu.HBM)
SEM_SPEC = pl.BlockSpec(memory_space=pltpu.SEMAPHORE)
EFFECT = pltpu.SideEffectType.DATAFLOW_SIDE_EFFECTING


def _tile(n, cands):
    for t in cands:
        if n % t == 0:
            return t
    return n


PIN_BYTES = 64 * 1024


def _pin_hbm(a):
    big = hasattr(a, "dtype") and jnp.issubdtype(a.dtype, jnp.floating) and _nbytes(a.shape, a.dtype) >= PIN_BYTES
    return pltpu.with_memory_space_constraint(a, pltpu.HBM) if big else a


def _pallas(body, *, out_shape, **kw):
    def pin(o):
        big = isinstance(o, jax.ShapeDtypeStruct) and jnp.issubdtype(o.dtype, jnp.floating) and _nbytes(o.shape, o.dtype) >= PIN_BYTES
        return pltpu.HBM(o.shape, o.dtype) if big else o

    shapes = type(out_shape)(pin(o) for o in out_shape) if isinstance(out_shape, (list, tuple)) else pin(out_shape)
    call = pl.pallas_call(body, out_shape=shapes, **kw)
    return lambda *args: call(*[_pin_hbm(a) for a in args])


def _params(vmem_bytes=None, **kw):
    if vmem_bytes is not None:
        kw["vmem_limit_bytes"] = int(min(max(vmem_bytes, 32 * 1024 * 1024), VMEM_CAP))
    return pltpu.CompilerParams(**kw)


def _nbytes(shape, dtype):
    return int(np.prod(shape)) * jnp.dtype(dtype).itemsize


def _sigmoid(x):
    return 1.0 / (1.0 + jnp.exp(-x))


_GC = 0.7978845608028654
_GA = 0.044715


def _gelu(x):
    return 0.5 * x * (1.0 + jnp.tanh(_GC * (x + _GA * (x * x * x))))


def _gelu_grad(x):
    t = jnp.tanh(_GC * (x + _GA * (x * x * x)))
    return 0.5 * (1.0 + t) + 0.5 * x * (1.0 - t * t) * (_GC * (1.0 + 3.0 * _GA * (x * x)))


def _bf(v):
    return v if v.dtype == BF16 else v.astype(BF16)


def _mm(name, grid, ins, in_specs, out_shape, out_specs, pairs, n_acc, tile, nk, epilogue, vmem_bytes, after=(), col_chunks=1):
    assert nk == 1 and tile[1] % col_chunks == 0
    n_in, n_out = len(ins) + len(after), len(out_shape)
    width = tile[1] // col_chunks

    def body(*refs):
        in_refs, out_refs = refs[:n_in], refs[n_in : n_in + n_out]
        for ch in range(col_chunks):
            cs = slice(ch * width, (ch + 1) * width) if col_chunks > 1 else slice(None)
            vals = [None] * n_acc
            for a_i, b_i, dn, acc_i in pairs:
                rhs = in_refs[b_i][cs, :] if dn == NT else in_refs[b_i][:, cs]
                d = lax.dot_general(_bf(in_refs[a_i][...]), _bf(rhs), dn, preferred_element_type=F32)
                vals[acc_i] = d if vals[acc_i] is None else vals[acc_i] + d
            epilogue(in_refs, vals, out_refs, cs)

    return _pallas(
        body,
        name=name,
        grid=grid,
        in_specs=list(in_specs) + [ANY] * len(after),
        out_specs=out_specs,
        out_shape=out_shape,
        compiler_params=_params(vmem_bytes),
    )(*ins, *after)


def _put(ref, cs, v):
    ref[:, cs] = v.astype(ref.dtype)


def _mm_vmem(tiles):
    return sum(_nbytes(s, d) * c for s, d, c in tiles) + 4 * 1024 * 1024


def _rows8(v):
    r, d = v.shape
    return v.reshape(r // 8, 8, d).sum(axis=0)


def _rms_fwd(name, x, g, after=()):
    s, d = x.shape
    tm = _tile(s, (256, 128))

    def body(x_ref, g_ref, *rest):
        h_ref = rest[-1]
        xv = x_ref[...]
        r = lax.rsqrt(jnp.mean(xv * xv, axis=-1, keepdims=True) + EPS)
        h_ref[...] = ((xv * r) * g_ref[...]).astype(BF16)

    return _pallas(
        body,
        name=name,
        grid=(s // tm,),
        in_specs=[pl.BlockSpec((tm, d), lambda i: (i, 0)), pl.BlockSpec((1, d), lambda i: (0, 0))] + [ANY] * len(after),
        out_specs=pl.BlockSpec((tm, d), lambda i: (i, 0)),
        out_shape=jax.ShapeDtypeStruct((s, d), BF16),
    )(x, g, *after)


def _rms_bwd(name, x, g, dh, dres, after=()):
    s, d = x.shape
    tm = _tile(s, (256, 128))
    n = s // tm
    n_after = len(after)

    def body(x_ref, g_ref, dh_ref, dres_ref, *rest):
        dx_ref, dxb_ref, dg_ref, acc_ref = rest[n_after:]
        i = pl.program_id(0)
        xv = x_ref[...]
        r = lax.rsqrt(jnp.mean(xv * xv, axis=-1, keepdims=True) + EPS)
        xh = xv * r
        dhv = dh_ref[...]
        dxh = dhv * g_ref[...]
        dx = r * (dxh - xh * jnp.mean(dxh * xh, axis=-1, keepdims=True)) + dres_ref[...]
        dx_ref[...] = dx
        dxb_ref[...] = dx.astype(BF16)
        part = _rows8(dhv * xh)

        @pl.when(i == 0)
        def _():
            acc_ref[...] = part

        @pl.when(i > 0)
        def _():
            acc_ref[...] += part

        @pl.when(i == n - 1)
        def _():
            dg_ref[...] = jnp.sum(acc_ref[...], axis=0, keepdims=True)

    row = pl.BlockSpec((tm, d), lambda i: (i, 0))
    vec = pl.BlockSpec((1, d), lambda i: (0, 0))
    return _pallas(
        body,
        name=name,
        grid=(n,),
        in_specs=[row, vec, row, row] + [ANY] * n_after,
        out_specs=[row, row, vec],
        out_shape=[jax.ShapeDtypeStruct((s, d), F32), jax.ShapeDtypeStruct((s, d), BF16), jax.ShapeDtypeStruct((1, d), F32)],
        scratch_shapes=[pltpu.VMEM((8, d), F32)],
    )(x, g, dh, dres, *after)


def _head(x3, g, target):
    s, d = x3.shape
    tm = _tile(s, (256, 128))
    n = s // tm

    def body(x_ref, g_ref, t_ref, dx_ref, dxb_ref, dg_ref, loss_ref, acc_g, acc_l):
        i = pl.program_id(0)
        xv = x_ref[...]
        gv = g_ref[...]
        r = lax.rsqrt(jnp.mean(xv * xv, axis=-1, keepdims=True) + EPS)
        xh = xv * r
        e = xh * gv - t_ref[...]
        dy = e * (1.0 / d)
        dxh = dy * gv
        dx = r * (dxh - xh * jnp.mean(dxh * xh, axis=-1, keepdims=True))
        dx_ref[...] = dx
        dxb_ref[...] = dx.astype(BF16)
        pg = _rows8(dy * xh)
        plo = _rows8(e * e)

        @pl.when(i == 0)
        def _():
            acc_g[...] = pg
            acc_l[...] = plo

        @pl.when(i > 0)
        def _():
            acc_g[...] += pg
            acc_l[...] += plo

        @pl.when(i == n - 1)
        def _():
            dg_ref[...] = jnp.sum(acc_g[...], axis=0, keepdims=True)
            loss_ref[...] = jnp.full((1, LANES), (0.5 / d) * jnp.sum(acc_l[...]), F32)

    row = pl.BlockSpec((tm, d), lambda i: (i, 0))
    vec = pl.BlockSpec((1, d), lambda i: (0, 0))
    return _pallas(
        body,
        name="head",
        grid=(n,),
        in_specs=[row, vec, row],
        out_specs=[row, row, vec, pl.BlockSpec((1, LANES), lambda i: (0, 0))],
        out_shape=[
            jax.ShapeDtypeStruct((s, d), F32),
            jax.ShapeDtypeStruct((s, d), BF16),
            jax.ShapeDtypeStruct((1, d), F32),
            jax.ShapeDtypeStruct((1, LANES), F32),
        ],
        scratch_shapes=[pltpu.VMEM((8, d), F32), pltpu.VMEM((8, d), F32)],
    )(x3, g, target)


def _sgu_fwd(z, gain, ws_b, b_col, w_sgu):
    s = z.shape[0]
    groups = ws_b.shape[0]

    def body(zu_ref, zv_ref, gain_ref, ws_ref, b_ref, a_ref):
        vv = _gelu(zv_ref[...].astype(F32))
        r = lax.rsqrt(jnp.mean(vv * vv, axis=-1, keepdims=True) + EPS)
        vn = ((vv * r) * gain_ref[...]).astype(BF16)
        u = _gelu(zu_ref[...].astype(F32))
        for g in range(groups):
            sl = slice(g * BLK, (g + 1) * BLK)
            mixed = jnp.dot(ws_ref[g], vn[:, sl], preferred_element_type=F32) + b_ref[g]
            a_ref[:, sl] = (u[:, sl] * mixed).astype(BF16)

    return _pallas(
        body,
        name="sgu_fwd",
        grid=(s // BLK,),
        in_specs=[
            pl.BlockSpec((BLK, w_sgu), lambda c: (c, 0)),
            pl.BlockSpec((BLK, w_sgu), lambda c: (c, 1)),
            pl.BlockSpec((1, w_sgu), lambda c: (0, 0)),
            pl.BlockSpec((groups, BLK, BLK), lambda c: (0, 0, 0)),
            pl.BlockSpec((groups, BLK, 1), lambda c: (0, 0, 0)),
        ],
        out_specs=pl.BlockSpec((BLK, w_sgu), lambda c: (c, 0)),
        out_shape=jax.ShapeDtypeStruct((s, w_sgu), BF16),
    )(z, z, gain, ws_b, b_col)


def _sgu_bwd(z, da, gain, ws_b, wst_b, b_col, w_sgu, dz, after=()):
    s = z.shape[0]
    groups = ws_b.shape[0]
    n = s // BLK
    n_skip = 1 + len(after)

    def body(zu_ref, zv_ref, da_ref, gain_ref, ws_ref, wst_ref, b_ref, *rest):
        dz_ref, dws_ref, dbs_ref, dgain_ref, acc_gain = rest[n_skip:]
        c = pl.program_id(0)
        zu = zu_ref[...].astype(F32)
        zv = zv_ref[...].astype(F32)
        gain_v = gain_ref[...]
        vv = _gelu(zv)
        r = lax.rsqrt(jnp.mean(vv * vv, axis=-1, keepdims=True) + EPS)
        xh = vv * r
        vn = (xh * gain_v).astype(BF16)
        u = _gelu(zu)
        dav = da_ref[...].astype(F32)
        dmix = dav * u
        dmix_b = dmix.astype(BF16)
        dvn_parts = []
        for g in range(groups):
            sl = slice(g * BLK, (g + 1) * BLK)
            mixed = jnp.dot(ws_ref[g], vn[:, sl], preferred_element_type=F32) + b_ref[g]
            dz_ref[:, sl] = (dav[:, sl] * mixed * _gelu_grad(zu[:, sl])).astype(BF16)
            dvn_parts.append(jnp.dot(wst_ref[g], dmix_b[:, sl], preferred_element_type=F32))
            dws_g = lax.dot_general(dmix_b[:, sl], vn[:, sl], NT, preferred_element_type=F32)
            dbs_g = jnp.sum(dmix[:, sl], axis=1, keepdims=True)

            @pl.when(c == 0)
            def _():
                dws_ref[g] = dws_g
                dbs_ref[g] = dbs_g

            @pl.when(c > 0)
            def _():
                dws_ref[g] += dws_g
                dbs_ref[g] += dbs_g

        dvn = jnp.concatenate(dvn_parts, axis=1)
        dxh = dvn * gain_v
        dvv = r * (dxh - xh * jnp.mean(dxh * xh, axis=-1, keepdims=True))
        dz_ref[:, w_sgu:] = (dvv * _gelu_grad(zv)).astype(BF16)
        pg = _rows8(dvn * xh)

        @pl.when(c == 0)
        def _():
            acc_gain[...] = pg

        @pl.when(c > 0)
        def _():
            acc_gain[...] += pg

        @pl.when(c == n - 1)
        def _():
            dgain_ref[...] = jnp.sum(acc_gain[...], axis=0, keepdims=True)

    full3 = pl.BlockSpec((groups, BLK, BLK), lambda c: (0, 0, 0))
    col3 = pl.BlockSpec((groups, BLK, 1), lambda c: (0, 0, 0))
    vec = pl.BlockSpec((1, w_sgu), lambda c: (0, 0))
    return _pallas(
        body,
        name="sgu_bwd",
        grid=(n,),
        in_specs=[
            pl.BlockSpec((BLK, w_sgu), lambda c: (c, 0)),
            pl.BlockSpec((BLK, w_sgu), lambda c: (c, 1)),
            pl.BlockSpec((BLK, w_sgu), lambda c: (c, 0)),
            vec,
            full3,
            full3,
            col3,
            ANY,
        ]
        + [ANY] * len(after),
        out_specs=[pl.BlockSpec((BLK, 2 * w_sgu), lambda c: (c, 0)), full3, col3, vec],
        out_shape=[
            jax.ShapeDtypeStruct(dz.shape, BF16),
            jax.ShapeDtypeStruct((groups, BLK, BLK), F32),
            jax.ShapeDtypeStruct((groups, BLK, 1), F32),
            jax.ShapeDtypeStruct((1, w_sgu), F32),
        ],
        scratch_shapes=[pltpu.VMEM((8, w_sgu), F32)],
        input_output_aliases={7: 0},
    )(z, z, da, gain, ws_b, wst_b, b_col, dz, *after)


def _attn_softmax(sink_ref, q_ref, k_ref, v_ref, bias_ref, s_len, grp):
    kv = pl.program_id(0)
    n = pl.program_id(1)
    start = pl.multiple_of(n * BLK, BLK)
    kb = k_ref[pl.ds(start, 3 * BLK), :]
    vb = v_ref[pl.ds(start, 3 * BLK), :]
    qv = q_ref[...]
    qs = jnp.concatenate([qv[:, g * HEAD_DIM : (g + 1) * HEAD_DIM] for g in range(grp)], axis=0).astype(BF16)
    sc = lax.dot_general(qs, kb, NT, preferred_element_type=F32) * (HEAD_DIM**-0.5)
    sc = sc + bias_ref[...].reshape(grp * BLK, 3 * BLK)
    kpos = start + lax.broadcasted_iota(I32, (1, 3 * BLK), 1) - BLK
    sc = jnp.where((kpos >= 0) & (kpos < s_len), sc, NEG)
    sink = jnp.concatenate([jnp.full((BLK, 1), sink_ref[kv * grp + g], F32) for g in range(grp)], axis=0)
    m = jnp.maximum(jnp.max(sc, axis=-1, keepdims=True), sink)
    p = jnp.exp(sc - m)
    esink = jnp.exp(sink - m)
    den = jnp.sum(p, axis=-1, keepdims=True) + esink
    return start, qs, kb, vb, p / den, esink / den


def _attn_specs(s, grp, q_blk0):
    qw = grp * HEAD_DIM
    return [
        pl.BlockSpec(memory_space=pltpu.SMEM),
        pl.BlockSpec((BLK, qw), lambda kv, n: (n, q_blk0 + kv)),
        pl.BlockSpec((s + 2 * BLK, HEAD_DIM), lambda kv, n: (0, kv)),
        pl.BlockSpec((s + 2 * BLK, HEAD_DIM), lambda kv, n: (0, kv)),
        pl.BlockSpec((grp, BLK, 3 * BLK), lambda kv, n: (kv, 0, 0)),
    ]


def _attn_fwd(sink, z, k_pad, v_pad, bias_tab, grp, q_blk0):
    s = z.shape[0]
    qw = grp * HEAD_DIM

    def body(sink_ref, q_ref, k_ref, v_ref, bias_ref, o_ref):
        _, _, _, vb, pn, _ = _attn_softmax(sink_ref, q_ref, k_ref, v_ref, bias_ref, s, grp)
        o = jnp.dot(pn.astype(BF16), vb, preferred_element_type=F32)
        for g in range(grp):
            o_ref[:, g * HEAD_DIM : (g + 1) * HEAD_DIM] = o[g * BLK : (g + 1) * BLK].astype(BF16)

    return _pallas(
        body,
        name="attn_fwd",
        grid=(N_KV_HEADS, s // BLK),
        in_specs=_attn_specs(s, grp, q_blk0),
        out_specs=pl.BlockSpec((BLK, qw), lambda kv, n: (n, kv)),
        out_shape=jax.ShapeDtypeStruct((s, N_KV_HEADS * qw), BF16),
    )(sink, z, k_pad, v_pad, bias_tab)


def _attn_bwd(sink, z, k_pad, v_pad, bias_tab, dout, dz, grp, q_blk0):
    s = z.shape[0]
    qw = grp * HEAD_DIM
    nb = s // BLK
    heads = N_KV_HEADS * grp

    def body(sink_ref, q_ref, k_ref, v_ref, bias_ref, do_ref, dz_in, dq_ref, dk_ref, dv_ref, dbias_ref, dsink_ref, dk_acc, dv_acc):
        del dz_in
        kv = pl.program_id(0)
        n = pl.program_id(1)
        start, qs, kb, vb, pn, psink = _attn_softmax(sink_ref, q_ref, k_ref, v_ref, bias_ref, s, grp)
        dov = do_ref[...]
        dos = jnp.concatenate([dov[:, g * HEAD_DIM : (g + 1) * HEAD_DIM] for g in range(grp)], axis=0)
        dp = lax.dot_general(dos, vb, NT, preferred_element_type=F32)
        dvb = lax.dot_general(pn.astype(BF16), dos, TN, preferred_element_type=F32)
        delta = jnp.sum(pn * dp, axis=-1, keepdims=True)
        ds = pn * (dp - delta)
        dsb = (ds * (HEAD_DIM**-0.5)).astype(BF16)
        dq = jnp.dot(dsb, kb, preferred_element_type=F32)
        dkb = lax.dot_general(dsb, qs, TN, preferred_element_type=F32)
        for g in range(grp):
            dq_ref[:, g * HEAD_DIM : (g + 1) * HEAD_DIM] = dq[g * BLK : (g + 1) * BLK].astype(BF16)

        @pl.when(n == 0)
        def _():
            dk_acc[...] = jnp.zeros_like(dk_acc)
            dv_acc[...] = jnp.zeros_like(dv_acc)
            dbias_ref[...] = jnp.zeros_like(dbias_ref)

        @pl.when((n == 0) & (kv == 0))
        def _():
            dsink_ref[...] = jnp.zeros_like(dsink_ref)

        dk_acc[pl.ds(start, 3 * BLK), :] += dkb
        dv_acc[pl.ds(start, 3 * BLK), :] += dvb
        dbias_ref[...] += ds.reshape(grp, BLK, 3 * BLK)
        row = lax.broadcasted_iota(I32, (heads, LANES), 0)
        sd = psink * delta
        upd = jnp.zeros((heads, LANES), F32)
        for g in range(grp):
            upd = jnp.where(row == kv * grp + g, -jnp.sum(sd[g * BLK : (g + 1) * BLK]), upd)
        dsink_ref[...] += upd

        @pl.when(n == nb - 1)
        def _():
            dk_ref[...] = dk_acc[...]
            dv_ref[...] = dv_acc[...]

    pad_spec = pl.BlockSpec((s + 2 * BLK, HEAD_DIM), lambda kv, n: (0, kv))
    kvw = N_KV_HEADS * HEAD_DIM
    return _pallas(
        body,
        name="attn_bwd",
        grid=(N_KV_HEADS, nb),
        in_specs=_attn_specs(s, grp, q_blk0) + [pl.BlockSpec((BLK, qw), lambda kv, n: (n, kv)), ANY],
        out_specs=[
            pl.BlockSpec((BLK, qw), lambda kv, n: (n, q_blk0 + kv)),
            pad_spec,
            pad_spec,
            pl.BlockSpec((grp, BLK, 3 * BLK), lambda kv, n: (kv, 0, 0)),
            pl.BlockSpec((heads, LANES), lambda kv, n: (0, 0)),
        ],
        out_shape=[
            jax.ShapeDtypeStruct(dz.shape, BF16),
            jax.ShapeDtypeStruct((s + 2 * BLK, kvw), F32),
            jax.ShapeDtypeStruct((s + 2 * BLK, kvw), F32),
            jax.ShapeDtypeStruct((heads, BLK, 3 * BLK), F32),
            jax.ShapeDtypeStruct((heads, LANES), F32),
        ],
        scratch_shapes=[pltpu.VMEM((s + 2 * BLK, HEAD_DIM), F32), pltpu.VMEM((s + 2 * BLK, HEAD_DIM), F32)],
        input_output_aliases={6: 0},
    )(sink, z, k_pad, v_pad, bias_tab, dout, dz)


def _dkv_to_dz(dk_pad, dv_pad, dz, blk_idx):
    s = dz.shape[0]
    kvw = dk_pad.shape[1]

    def body(dk_ref, dv_ref, dz_in, out_ref):
        del dz_in
        out_ref[:, :kvw] = dk_ref[...].astype(BF16)
        out_ref[:, kvw:] = dv_ref[...].astype(BF16)

    src = pl.BlockSpec((BLK, kvw), lambda i: (i + 1, 0))
    return _pallas(
        body,
        name="dkv_to_dz",
        grid=(s // BLK,),
        in_specs=[src, src, ANY],
        out_specs=pl.BlockSpec((BLK, 2 * kvw), lambda i: (i, blk_idx)),
        out_shape=jax.ShapeDtypeStruct(dz.shape, BF16),
        input_output_aliases={2: 0},
    )(dk_pad, dv_pad, dz)


def _relbias_bwd(dbias_tab, bucket):
    heads = dbias_tab.shape[0]

    def body(dt_ref, bk_ref, out_ref):
        lane = lax.broadcasted_iota(I32, (1, LANES), 1)
        bk = bk_ref[...]
        rows = []
        for h in range(heads):
            dt = dt_ref[h]
            acc = jnp.zeros((1, LANES), F32)
            for b in range(REL_BUCKETS):
                acc = jnp.where(lane == b, jnp.sum(jnp.where(bk == b, dt, 0.0)), acc)
            rows.append(acc)
        out_ref[...] = jnp.concatenate(rows, axis=0)

    return _pallas(body, name="relbias_bwd", out_shape=jax.ShapeDtypeStruct((heads, LANES), F32))(dbias_tab, bucket)


def _t5_bucket(rel):
    nb = REL_BUCKETS // 2
    ret = jnp.where(rel > 0, nb, 0)
    n = jnp.abs(rel)
    max_exact = nb // 2
    nf = jnp.maximum(n, 1).astype(F32)
    large = max_exact + (jnp.log(nf / max_exact) / math.log(REL_MAX_DIST / max_exact) * (nb - max_exact)).astype(I32)
    large = jnp.minimum(large, nb - 1)
    return ret + jnp.where(n < max_exact, n, large)


def _band_tables(rel_bias):
    qi = jnp.arange(BLK)[:, None]
    kj = jnp.arange(3 * BLK)[None, :]
    rel = kj - BLK - qi
    bucket = _t5_bucket(rel).astype(I32)
    heads = rel_bias.shape[1]
    masked = jnp.where(jnp.abs(rel) <= BLK, bucket, -1)

    def body(rb_ref, bk_ref, out_ref):
        bk = bk_ref[...]
        for h in range(heads):
            tab = jnp.full(bk.shape, NEG, F32)
            for b in range(REL_BUCKETS):
                tab = jnp.where(bk == b, rb_ref[b, h], tab)
            out_ref[h] = tab

    bias_tab = _pallas(
        body,
        name="bias_table",
        in_specs=[pl.BlockSpec(memory_space=pltpu.SMEM), pl.BlockSpec(memory_space=pltpu.VMEM)],
        out_specs=pl.BlockSpec(memory_space=pltpu.VMEM),
        out_shape=jax.ShapeDtypeStruct((heads, BLK, 3 * BLK), F32),
    )(rel_bias.astype(F32), masked)
    return bias_tab, bucket


EW_BLOCK_ELEMS = 512 * 1024


def _ew_tiles(shape, elems=EW_BLOCK_ELEMS // 2):
    r, c = shape
    tn = c if c <= 2048 else _tile(c, (2048, 1920, 1536, 1408, 1024, 512))
    tm = _tile(r, [t for t in (1024, 512, 256, 128, 64, 32, 16, 8) if t * tn <= elems] or [8])
    return tm, tn


def _cast_into_full(name, qidx, w, kind, after=()):
    r, c = w.shape
    tm, tn = _ew_tiles(w.shape, EW_BLOCK_ELEMS)
    nbi, nbj = r // tm, c // tn
    if kind == "col":
        full, out_spec = (r, c * N_CHIPS), pl.BlockSpec((tm, tn), lambda i, j, q: (i, q[0] * nbj + j))
    else:
        full, out_spec = (r * N_CHIPS, c), pl.BlockSpec((tm, tn), lambda i, j, q: (q[0] * nbi + i, j))

    def body(q_ref, w_ref, *rest):
        del q_ref
        rest[-1][...] = w_ref[...].astype(BF16)

    return _pallas(
        body,
        name=name,
        grid_spec=pltpu.PrefetchScalarGridSpec(
            num_scalar_prefetch=1,
            grid=(nbi, nbj),
            in_specs=[pl.BlockSpec((tm, tn), lambda i, j, q: (i, j))] + [ANY] * len(after),
            out_specs=out_spec,
        ),
        out_shape=jax.ShapeDtypeStruct(full, BF16),
    )(qidx, w, *after)


def _adamw(name, w, g, m, v, after=()):
    tm, tn = _ew_tiles(w.shape, EW_BLOCK_ELEMS)
    if _nbytes(w.shape, F32) <= 1024 * 1024:
        tm, tn = w.shape
    spec = pl.BlockSpec((tm, tn), lambda i, j: (i, j))
    n_after = len(after)

    def body(w_ref, g_ref, m_ref, v_ref, *rest):
        d_ref, nm_ref, nv_ref, g_out_ref = rest[n_after:]
        gv = g_ref[...]
        g_out_ref[...] = gv
        nm = ADAM_B1 * m_ref[...] + (1.0 - ADAM_B1) * gv
        nv = ADAM_B2 * v_ref[...] + (1.0 - ADAM_B2) * (gv * gv)
        m_hat = nm / (1.0 - ADAM_B1**ADAM_STEP)
        v_hat = nv / (1.0 - ADAM_B2**ADAM_STEP)
        d_ref[...] = -ADAM_LR * (m_hat / (jnp.sqrt(v_hat) + ADAM_EPS) + ADAM_WD * w_ref[...])
        nm_ref[...] = nm
        nv_ref[...] = nv

    out = jax.ShapeDtypeStruct(w.shape, F32)
    return _pallas(
        body, name=name, grid=(w.shape[0] // tm, w.shape[1] // tn), in_specs=[spec] * 4 + [ANY] * n_after,
        out_specs=[spec] * 4, out_shape=[out, out, out, out],
        compiler_params=_params(_mm_vmem([((tm, tn), F32, 24)])),
    )(w, g, m, v, *after)


def _pair_add(name, cidx, g_full, r_sib, kind):
    hr, hc = r_sib.shape
    tm, tn = _ew_tiles((hr, hc), 2 * EW_BLOCK_ELEMS)
    nbi, nbj = hr // tm, hc // tn
    if kind == "col":
        g_spec = pl.BlockSpec((tm, tn), lambda i, j, c: (c[0] * nbi + i, j))
    else:
        g_spec = pl.BlockSpec((tm, tn), lambda i, j, c: (i, c[0] * nbj + j))
    spec = pl.BlockSpec((tm, tn), lambda i, j, c: (i, j))

    def body(c_ref, g_ref, r_ref, o_ref):
        del c_ref
        o_ref[...] = (g_ref[...].astype(F32) + r_ref[...].astype(F32)).astype(BF16)

    return _pallas(
        body,
        name=name,
        grid_spec=pltpu.PrefetchScalarGridSpec(num_scalar_prefetch=1, grid=(nbi, nbj), in_specs=[g_spec, spec], out_specs=spec),
        out_shape=jax.ShapeDtypeStruct((hr, hc), BF16),
        compiler_params=_params(_mm_vmem([((tm, tn), BF16, 6), ((tm, tn), F32, 3)])),
    )(cidx, g_full, r_sib)


def _chip_sum(name, qidx, c_half, r_ici, kind):
    _, pr, pc = r_ici.shape
    tm, tn = _ew_tiles((pr, pc), 2 * EW_BLOCK_ELEMS)
    nbi, nbj = pr // tm, pc // tn
    if kind == "col":
        own_spec = pl.BlockSpec((tm, tn), lambda i, j, q: (i, q[0] * nbj + j))
        full, out_spec = (2 * pr, pc), pl.BlockSpec((tm, tn), lambda i, j, q: (q[1] * nbi + i, j))
    else:
        own_spec = pl.BlockSpec((tm, tn), lambda i, j, q: (q[0] * nbi + i, j))
        full, out_spec = (pr, 2 * pc), pl.BlockSpec((tm, tn), lambda i, j, q: (i, q[1] * nbj + j))

    def body(q_ref, own_ref, r_ref, o_ref):
        q = q_ref[0]
        own = own_ref[...].astype(F32)
        recv = [r_ref[r].astype(F32) for r in range(3)]
        total = None
        for chip in range(N_CHIPS):
            d = chip ^ q
            term = jnp.where(d == 0, own, jnp.where(d == 2, recv[0], jnp.where(d == 1, recv[1], recv[2])))
            total = term if total is None else total + term
        o_ref[...] = total

    return _pallas(
        body,
        name=name,
        grid_spec=pltpu.PrefetchScalarGridSpec(
            num_scalar_prefetch=1,
            grid=(nbi, nbj),
            in_specs=[own_spec, pl.BlockSpec((3, tm, tn), lambda i, j, q: (0, i, j))],
            out_specs=out_spec,
        ),
        out_shape=jax.ShapeDtypeStruct(full, F32),
        compiler_params=_params(_mm_vmem([((tm, tn), BF16, 8), ((tm, tn), F32, 6)])),
    )(qidx, c_half, r_ici)


_REL_MASK = (2, 1, 3)


def _place():
    x, y, c = lax.axis_index("x"), lax.axis_index("y"), lax.axis_index("c")
    chips = [(1 - x, y), (x, 1 - y), (1 - x, 1 - y)]
    return x, y, c, 2 * x + y, chips


def _shard_view(ref, kind, chip):
    if kind == "col":
        w = ref.shape[1] // N_CHIPS
        return ref.at[:, pl.ds(pl.multiple_of(chip * w, LANES), w)]
    h = ref.shape[0] // N_CHIPS
    return ref.at[pl.ds(pl.multiple_of(chip * h, 16), h), :]


def _row_half(ref, half):
    h = ref.shape[0] // 2
    return ref.at[pl.ds(pl.multiple_of(half * h, 16), h), :]


def _pair_half(ref, kind, half):
    if kind == "col":
        return _row_half(ref, half)
    w = ref.shape[1] // 2
    return ref.at[:, pl.ds(pl.multiple_of(half * w, LANES), w)]


def _remote(src, dst, send_sem, recv_sem, dev):
    return pltpu.make_async_remote_copy(src_ref=src, dst_ref=dst, send_sem=send_sem, recv_sem=recv_sem, device_id=dev, device_id_type=MESH)


def _hbm(a):
    return pltpu.with_memory_space_constraint(a, pltpu.HBM)


def _gather_start(name, fulls, kinds):
    n_w = len(fulls)

    def body(*refs):
        g = refs[:n_w]
        send_sem, recv_sem = refs[n_w], refs[n_w + 1]
        token = refs[-1]
        _, _, c, q, chips = _place()
        for w in range(n_w):
            mine = _row_half(_shard_view(g[w], kinds[w], q), c)
            for r, chip in enumerate(chips):
                _remote(mine, mine, send_sem.at[3 * w + r], recv_sem.at[3 * w + r], (*chip, c)).start()
        token[...] = jnp.zeros_like(token)

    res = _pallas(
        body,
        name=name,
        out_shape=(
            pltpu.SemaphoreType.DMA((3 * n_w,)),
            pltpu.SemaphoreType.DMA((3 * n_w,)),
            *[pltpu.HBM(f.shape, f.dtype) for f in fulls],
            jax.ShapeDtypeStruct((8, LANES), F32),
        ),
        in_specs=[HBM_SPEC] * n_w,
        out_specs=(SEM_SPEC, SEM_SPEC, *[HBM_SPEC] * n_w, pl.BlockSpec(memory_space=pltpu.VMEM)),
        input_output_aliases={w: w + 2 for w in range(n_w)},
        compiler_params=pltpu.CompilerParams(has_side_effects=EFFECT),
    )(*[_hbm(f) for f in fulls])
    return res[0], res[1], list(res[2 : 2 + n_w]), res[-1]


def _gather_wait(name, fulls, kinds, w_ids, send_sem, recv_sem, after):
    n = len(fulls)

    def body(*refs):
        g = refs[:n]
        s_sem, r_sem = refs[n], refs[n + 1]
        x, y, c, q, _ = _place()
        for i, w in enumerate(w_ids):
            mine = _row_half(_shard_view(g[i], kinds[i], q), c)
            for r in range(3):
                landed = _row_half(_shard_view(g[i], kinds[i], q ^ _REL_MASK[r]), c)
                cp = _remote(mine, landed, s_sem.at[3 * w + r], r_sem.at[3 * w + r], (x, y, 1 - c))
                cp.wait_send()
                cp.wait_recv()

    res = _pallas(
        body,
        name=name,
        out_shape=[pltpu.HBM(f.shape, f.dtype) for f in fulls],
        in_specs=[HBM_SPEC] * n + [SEM_SPEC, SEM_SPEC, ANY],
        out_specs=[HBM_SPEC] * n,
        input_output_aliases={i: i for i in range(n)},
        compiler_params=pltpu.CompilerParams(has_side_effects=EFFECT),
    )(*fulls, send_sem, recv_sem, after)
    return list(res)


def _gather_forward(name, fulls, kinds):
    n = len(fulls)

    def body(*refs):
        g = refs[n : 2 * n]
        send, recv = refs[2 * n :]
        x, y, c, q, _ = _place()
        sib = (x, y, 1 - c)
        cps = []
        for i in range(n):
            for r in range(3):
                landed = _row_half(_shard_view(g[i], kinds[i], q ^ _REL_MASK[r]), c)
                cps.append(_remote(landed, landed, send.at[i, r], recv.at[i, r], sib))
        for cp in cps:
            cp.start()
        for i in range(n):
            for r in range(3):
                other = _row_half(_shard_view(g[i], kinds[i], q ^ _REL_MASK[r]), 1 - c)
                _remote(other, other, send.at[i, r], recv.at[i, r], sib).wait_recv()
        for cp in cps:
            cp.wait_send()

    res = _pallas(
        body,
        name=name,
        in_specs=[ANY] * n,
        out_specs=[ANY] * n,
        out_shape=[jax.ShapeDtypeStruct(f.shape, f.dtype) for f in fulls],
        scratch_shapes=[pltpu.SemaphoreType.DMA((n, 3)), pltpu.SemaphoreType.DMA((n, 3))],
        input_output_aliases={i: i for i in range(n)},
    )(*fulls)
    return list(res)


def _split_start(name, bufs, n_sems, copies):
    n = len(bufs)

    def body(*refs):
        for cp in copies(refs[:n], refs[n], refs[n + 1]):
            cp.start()

    res = _pallas(
        body,
        name=name,
        out_shape=(
            pltpu.SemaphoreType.DMA((n_sems,)),
            pltpu.SemaphoreType.DMA((n_sems,)),
            *[pltpu.HBM(b.shape, b.dtype) for b in bufs],
        ),
        in_specs=[HBM_SPEC] * n,
        out_specs=(SEM_SPEC, SEM_SPEC, *[HBM_SPEC] * n),
        input_output_aliases={i: i + 2 for i in range(n)},
        compiler_params=pltpu.CompilerParams(has_side_effects=EFFECT),
    )(*[_hbm(b) for b in bufs])
    return res[0], res[1], list(res[2:])


def _split_wait(name, bufs, send_sem, recv_sem, copies, after):
    n = len(bufs)

    def body(*refs):
        for cp in copies(refs[:n], refs[n], refs[n + 1]):
            cp.wait_send()
            cp.wait_recv()

    res = _pallas(
        body,
        name=name,
        out_shape=[pltpu.HBM(b.shape, b.dtype) for b in bufs],
        in_specs=[HBM_SPEC] * n + [SEM_SPEC, SEM_SPEC, ANY],
        out_specs=[HBM_SPEC] * n,
        input_output_aliases={i: i for i in range(n)},
        compiler_params=pltpu.CompilerParams(has_side_effects=EFFECT),
    )(*bufs, send_sem, recv_sem, after)
    return list(res)


def _pair_exchange_copies(kinds):
    n = len(kinds)

    def copies(refs, send_sem, recv_sem):
        x, y, c, _, _ = _place()
        return [
            _remote(_pair_half(refs[w], kinds[w], 1 - c), refs[n + w], send_sem.at[w], recv_sem.at[w], (x, y, 1 - c))
            for w in range(n)
        ]

    return copies


def _pair_share_copies(kinds, waiting):
    def copies(refs, send_sem, recv_sem):
        x, y, c, _, _ = _place()
        out = []
        for w, kind in enumerate(kinds):
            mine = _pair_half(refs[w], kind, c)
            dst = _pair_half(refs[w], kind, 1 - c) if waiting else mine
            out.append(_remote(mine, dst, send_sem.at[w], recv_sem.at[w], (x, y, 1 - c)))
        return out

    return copies


def _piece_shape(half_shape, kind):
    r, c = half_shape
    return (3, r, c // N_CHIPS) if kind == "col" else (3, r // N_CHIPS, c)


def _chip_send_start(name, halves, kinds):
    n = len(halves)
    lands = [lax.empty(_piece_shape(h.shape, k), BF16) for h, k in zip(halves, kinds)]

    def body(*refs):
        h, land = refs[:n], refs[n : 2 * n]
        send_sem, recv_sem = refs[2 * n], refs[2 * n + 1]
        _, _, c, q, chips = _place()
        for i in range(n):
            for r, chip in enumerate(chips):
                piece = _shard_view(h[i], kinds[i], q ^ _REL_MASK[r])
                _remote(piece, land[i].at[r], send_sem.at[3 * i + r], recv_sem.at[3 * i + r], (*chip, c)).start()

    res = _pallas(
        body,
        name=name,
        out_shape=(
            pltpu.SemaphoreType.DMA((3 * n,)),
            pltpu.SemaphoreType.DMA((3 * n,)),
            *[pltpu.HBM(a.shape, a.dtype) for a in halves],
            *[pltpu.HBM(a.shape, a.dtype) for a in lands],
        ),
        in_specs=[HBM_SPEC] * (2 * n),
        out_specs=(SEM_SPEC, SEM_SPEC, *[HBM_SPEC] * (2 * n)),
        input_output_aliases={i: i + 2 for i in range(2 * n)},
        compiler_params=pltpu.CompilerParams(has_side_effects=EFFECT),
    )(*[_hbm(a) for a in halves], *[_hbm(a) for a in lands])
    return res[0], res[1], list(res[2 : 2 + n]), list(res[2 + n :])


def _chip_send_wait(name, halves, lands, kinds, send_sem, recv_sem, after):
    n = len(halves)

    def body(*refs):
        h, land = refs[:n], refs[n : 2 * n]
        s_sem, r_sem = refs[2 * n], refs[2 * n + 1]
        x, y, c, q, _ = _place()
        for i in range(n):
            for r in range(3):
                piece = _shard_view(h[i], kinds[i], q ^ _REL_MASK[r])
                cp = _remote(piece, land[i].at[r], s_sem.at[3 * i + r], r_sem.at[3 * i + r], (x, y, 1 - c))
                cp.wait_send()
                cp.wait_recv()

    res = _pallas(
        body,
        name=name,
        out_shape=[pltpu.HBM(a.shape, a.dtype) for a in halves] + [pltpu.HBM(a.shape, a.dtype) for a in lands],
        in_specs=[HBM_SPEC] * (2 * n) + [SEM_SPEC, SEM_SPEC, ANY],
        out_specs=[HBM_SPEC] * (2 * n),
        input_output_aliases={i: i for i in range(2 * n)},
        compiler_params=pltpu.CompilerParams(has_side_effects=EFFECT),
    )(*halves, *lands, send_sem, recv_sem, after)
    return list(res[:n]), list(res[n:])


def _small_all_reduce(p):
    rows = p.shape[0]
    n_dev = 2 * N_CHIPS

    def body(p_ref, o_ref, buf, loc_sem, send, recv):
        x, y, c, q, _ = _place()
        me = 2 * q + c
        own = pltpu.make_async_copy(p_ref, buf.at[me], loc_sem)
        own.start()
        cps = []
        for d in range(1, n_dev):
            dev = (x ^ ((d >> 2) & 1), y ^ ((d >> 1) & 1), c ^ (d & 1))
            cps.append(_remote(p_ref, buf.at[me], send.at[d - 1], recv.at[d - 1], dev))
        for cp in cps:
            cp.start()
        for d in range(1, n_dev):
            slot = buf.at[me ^ d]
            _remote(slot, slot, send.at[d - 1], recv.at[d - 1], (x, y, c)).wait_recv()
        own.wait()
        total = buf[0]
        for d in range(1, n_dev):
            total = total + buf[d]
        o_ref[...] = total
        for cp in cps:
            cp.wait_send()

    return _pallas(
        body,
        name="small_all_reduce",
        in_specs=[ANY],
        out_specs=pl.BlockSpec(memory_space=pltpu.VMEM),
        out_shape=jax.ShapeDtypeStruct(p.shape, F32),
        scratch_shapes=[
            pltpu.VMEM((n_dev, rows, LANES), F32),
            pltpu.SemaphoreType.DMA,
            pltpu.SemaphoreType.DMA((n_dev - 1,)),
            pltpu.SemaphoreType.DMA((n_dev - 1,)),
        ],
    )(p)


def _small_exchange_copies(waiting):
    def copies(refs, send_sem, recv_sem):
        p, land = refs
        x, y, c, q, _ = _place()
        me = 2 * q + c
        out = []
        for dd in range(1, 2 * N_CHIPS):
            dev = (x ^ ((dd >> 2) & 1), y ^ ((dd >> 1) & 1), c ^ (dd & 1))
            dst = land.at[me ^ dd] if waiting else land.at[me]
            out.append(_remote(p, dst, send_sem.at[dd - 1], recv_sem.at[dd - 1], dev))
        return out

    return copies


def _small_sum(name, me_idx, p, land):
    rows = p.shape[0]
    n_dev = 2 * N_CHIPS

    def body(me_ref, p_ref, land_ref, o_ref):
        me = me_ref[0]
        total = None
        for dev in range(n_dev):
            term = jnp.where(me == dev, p_ref[...], land_ref[dev])
            total = term if total is None else total + term
        o_ref[...] = total

    return _pallas(
        body,
        name=name,
        grid_spec=pltpu.PrefetchScalarGridSpec(
            num_scalar_prefetch=1,
            grid=(1,),
            in_specs=[pl.BlockSpec((rows, LANES), lambda i, m: (0, 0)), pl.BlockSpec((n_dev, rows, LANES), lambda i, m: (0, 0, 0))],
            out_specs=pl.BlockSpec((rows, LANES), lambda i, m: (0, 0)),
        ),
        out_shape=jax.ShapeDtypeStruct(p.shape, F32),
    )(me_idx, p, land)


def _pack(parts):
    rows = []
    for a in parts:
        flat = a.reshape(-1).astype(F32)
        n = flat.shape[0]
        padded = -(-n // (8 * LANES)) * (8 * LANES)
        rows.append(jnp.pad(flat, (0, padded - n)).reshape(-1, LANES))
    return jnp.concatenate(rows, axis=0)


def _unpack(packed, shapes):
    out, row = [], 0
    for shp in shapes:
        n = int(np.prod(shp))
        nrows = -(-n // (8 * LANES)) * 8
        out.append(packed[row : row + nrows].reshape(-1)[:n].reshape(shp))
        row += nrows
    return out


def kernel(x, w_in, norm_mix, sgu_v_gain, sgu_w_s, sgu_b_s, w_a_out, attn_sink, rel_bias, w_b_out, w_o, norm_ffn, w_gate, w_up, w_down, norm_final, loss_target, m_w_in, m_norm_mix, m_sgu_v_gain, m_sgu_w_s, m_sgu_b_s, m_w_a_out, m_attn_sink, m_rel_bias, m_w_b_out, m_w_o, m_norm_ffn, m_w_gate, m_w_up, m_w_down, m_norm_final, v_w_in, v_norm_mix, v_sgu_v_gain, v_sgu_w_s, v_sgu_b_s, v_w_a_out, v_attn_sink, v_rel_bias, v_w_b_out, v_w_o, v_norm_ffn, v_w_gate, v_w_up, v_w_down, v_norm_final):
    s, d = x.shape[1], x.shape[2]
    w_sgu = sgu_v_gain.shape[1]
    groups = sgu_w_s.shape[1]
    heads = attn_sink.shape[1]
    grp = heads // N_KV_HEADS
    w_att = heads * HEAD_DIM
    w_kv = N_KV_HEADS * HEAD_DIM
    d_ff = w_gate.shape[2] * N_CHIPS
    n_in = w_in.shape[2] * N_CHIPS
    off_q = 2 * w_sgu
    off_k = off_q + w_att
    off_g = off_k + 2 * w_kv
    assert n_in == off_g + 2 * d and groups * BLK == w_sgu and s % BLK == 0

    x2d = x.reshape(s, d)
    tgt = loss_target.reshape(s, d)
    c_idx = lax.axis_index("c").astype(I32).reshape(1)
    q_idx = (2 * lax.axis_index("x") + lax.axis_index("y")).astype(I32).reshape(1)
    qc_idx = jnp.concatenate([q_idx, c_idx])

    W_IN, W_A, W_B, W_O, W_GATE, W_UP, W_DOWN = range(7)
    names = ["w_in", "w_a", "w_b", "w_o", "w_gate", "w_up", "w_down"]
    kinds = ["col", "col", "col", "row", "col", "col", "row"]
    big_w = [w_in[0], w_a_out[0], w_b_out[0], w_o[0], w_gate[0], w_up[0], w_down[0]]
    big_m = [m_w_in[0], m_w_a_out[0], m_w_b_out[0], m_w_o[0], m_w_gate[0], m_w_up[0], m_w_down[0]]
    big_v = [v_w_in[0], v_w_a_out[0], v_w_b_out[0], v_w_o[0], v_w_gate[0], v_w_up[0], v_w_down[0]]
    full_in = _cast_into_full("cast_w_in", q_idx, big_w[W_IN], kinds[W_IN])
    in_send, in_recv, (full_in,), token = _gather_start("gather_start_in", [full_in], [kinds[W_IN]])
    rest = [_cast_into_full("cast_" + names[i], q_idx, big_w[i], kinds[i], after=(token,)) for i in range(1, 7)]
    ag_send, ag_recv, rest, token = _gather_start("gather_start_rest", rest, kinds[1:])
    fulls = [full_in] + rest

    def gathered(tag, ids, after):
        if ids == [W_IN]:
            sems, pos = (in_send, in_recv), [0]
        else:
            sems, pos = (ag_send, ag_recv), [i - 1 for i in ids]
        got = _gather_wait("gather_wait_" + tag, [fulls[i] for i in ids], [kinds[i] for i in ids], pos, *sems, after)
        return _gather_forward("gather_fwd_" + tag, got, [kinds[i] for i in ids])

    ws_b = sgu_w_s[0].astype(BF16)
    wst_b = jnp.swapaxes(sgu_w_s[0], 1, 2).astype(BF16)
    b_col = sgu_b_s[0].reshape(groups, BLK, 1)
    bias_tab, bucket = _band_tables(rel_bias)
    sink = attn_sink[0]

    tm = _tile(s, (1024, 512, 256, 128))

    h1 = _rms_fwd("rms_mix", x2d, norm_mix, after=(token,))
    (g_in,) = gathered("in", [W_IN], h1)

    tn = _tile(n_in, (768, 640, 512))
    z = _mm(
        "mm_z", (s // tm, n_in // tn, 1), [h1, g_in],
        [pl.BlockSpec((tm, d), lambda i, j, k: (i, 0)), pl.BlockSpec((d, tn), lambda i, j, k: (0, j))],
        [jax.ShapeDtypeStruct((s, n_in), BF16)], [pl.BlockSpec((tm, tn), lambda i, j, k: (i, j))],
        [(0, 1, NN, 0)], 1, (tm, tn), 1, lambda ins, vals, outs, cs: _put(outs[0], cs, vals[0]),
        _mm_vmem([((tm, d), BF16, 2), ((d, tn), BF16, 2), ((tm, tn), F32, 3)]),
    )[0]
    g_a, g_b, g_o = gathered("mix", [W_A, W_B, W_O], z)

    a_act = _sgu_fwd(z, sgu_v_gain, ws_b, b_col, w_sgu)

    kv_b = z[:, off_k:off_g]
    k_pad = jnp.pad(kv_b[:, :w_kv], ((BLK, BLK), (0, 0)))
    v_pad = jnp.pad(kv_b[:, w_kv:], ((BLK, BLK), (0, 0)))
    q_blk0 = off_q // (grp * HEAD_DIM)
    att = _attn_fwd(sink, z, k_pad, v_pad, bias_tab, grp, q_blk0)

    tg = _tile(d, (512,))
    ga0, gb0 = off_g // tg, (off_g + d) // tg

    def ep_gate(ins, vals, outs, cs):
        sa, sb = _sigmoid(ins[4][:, cs].astype(F32)), _sigmoid(ins[5][:, cs].astype(F32))
        _put(outs[0], cs, sa * vals[0] + sb * vals[1])
        _put(outs[1], cs, vals[0])
        _put(outs[2], cs, vals[1])

    t_out = pl.BlockSpec((tm, tg), lambda i, j, k: (i, j))
    m_act, y_a, y_b = _mm(
        "mm_branches", (s // tm, d // tg, 1), [a_act, g_a, att, g_b, z, z],
        [pl.BlockSpec((tm, w_sgu), lambda i, j, k: (i, 0)), pl.BlockSpec((w_sgu, tg), lambda i, j, k: (0, j)),
         pl.BlockSpec((tm, w_att), lambda i, j, k: (i, 0)), pl.BlockSpec((w_att, tg), lambda i, j, k: (0, j)),
         pl.BlockSpec((tm, tg), lambda i, j, k: (i, ga0 + j)), pl.BlockSpec((tm, tg), lambda i, j, k: (i, gb0 + j))],
        [jax.ShapeDtypeStruct((s, d), BF16)] * 3,
        [t_out, t_out, t_out], [(0, 1, NN, 0), (2, 3, NN, 1)], 2, (tm, tg), 1, ep_gate,
        _mm_vmem([((tm, w_sgu), BF16, 4), ((w_sgu, tg), BF16, 4), ((tm, tg), F32, 12)]),    )

    tn = _tile(d, (1024, 512))

    def ep_residual(ins, vals, outs, cs):
        _put(outs[0], cs, ins[2][:, cs] + vals[0])

    x2 = _mm(
        "mm_wo", (s // tm, d // tn, 1), [m_act, g_o, x2d],
        [pl.BlockSpec((tm, d), lambda i, j, k: (i, 0)), pl.BlockSpec((d, tn), lambda i, j, k: (0, j)),
         pl.BlockSpec((tm, tn), lambda i, j, k: (i, j))],
        [jax.ShapeDtypeStruct((s, d), F32)], [pl.BlockSpec((tm, tn), lambda i, j, k: (i, j))],
        [(0, 1, NN, 0)], 1, (tm, tn), 1, ep_residual,
        _mm_vmem([((tm, d), BF16, 2), ((d, tn), BF16, 2), ((tm, tn), F32, 5)]),    )[0]

    h2 = _rms_fwd("rms_ffn", x2, norm_ffn)
    g_gate, g_up = gathered("ffn_in", [W_GATE, W_UP], h2)

    tf = _tile(d_ff, (512,))

    def ep_swiglu(ins, vals, outs, cs):
        gt, up = vals
        _put(outs[0], cs, gt)
        _put(outs[1], cs, up)
        _put(outs[2], cs, (gt * _sigmoid(gt)) * up)

    f_out = pl.BlockSpec((tm, tf), lambda i, j, k: (i, j))
    gt, up, f_act = _mm(
        "mm_gate_up", (s // tm, d_ff // tf, 1), [h2, g_gate, g_up],
        [pl.BlockSpec((tm, d), lambda i, j, k: (i, 0)), pl.BlockSpec((d, tf), lambda i, j, k: (0, j)),
         pl.BlockSpec((d, tf), lambda i, j, k: (0, j))],
        [jax.ShapeDtypeStruct((s, d_ff), BF16)] * 3,
        [f_out, f_out, f_out], [(0, 1, NN, 0), (0, 2, NN, 1)], 2, (tm, tf), 1, ep_swiglu,
        _mm_vmem([((tm, d), BF16, 2), ((d, tf), BF16, 4), ((tm, tf), F32, 8)]),    )
    (g_down,) = gathered("ffn_out", [W_DOWN], f_act)

    tkf = _tile(d_ff, (1408, 1024, 512))
    nkf = d_ff // tkf
    tml, tnl = _tile(s, (512, 256, 128)), _tile(d, (512,))
    x3 = _mm(
        "mm_down", (s // tml, d // tnl, 1), [f_act, g_down, x2],
        [pl.BlockSpec((tml, d_ff), lambda i, j, k: (i, 0)), pl.BlockSpec((d_ff, tnl), lambda i, j, k: (0, j)),
         pl.BlockSpec((tml, tnl), lambda i, j, k: (i, j))],
        [jax.ShapeDtypeStruct((s, d), F32)], [pl.BlockSpec((tml, tnl), lambda i, j, k: (i, j))],
        [(0, 1, NN, 0)], 1, (tml, tnl), 1, ep_residual,
        _mm_vmem([((tml, d_ff), BF16, 2), ((d_ff, tnl), BF16, 2), ((tml, tnl), F32, 6)]),    )[0]

    dx3, dx3b, dg_final, loss_part = _head(x3, norm_final.reshape(1, d), tgt)

    def reduce_a(tag, ids, grads):
        ks = [kinds[i] for i in ids]
        lands = [lax.empty((g.shape[0] // 2, g.shape[1]) if k == "col" else (g.shape[0], g.shape[1] // 2), BF16)
                 for g, k in zip(grads, ks)]
        send, recv, bufs = _split_start("pair_send_" + tag, list(grads) + lands, len(ids), _pair_exchange_copies(ks))
        return {"tag": tag, "ids": ids, "ks": ks, "pair": (send, recv, bufs), "token": bufs[0]}

    def reduce_b(st, after):
        tag, ids, ks = st["tag"], st["ids"], st["ks"]
        send, recv, bufs = st["pair"]
        bufs = _split_wait("pair_wait_" + tag, bufs, send, recv, _pair_exchange_copies(ks), after)
        grads, from_sib = bufs[: len(ids)], bufs[len(ids) :]
        halves = [_pair_add("pair_add_" + names[i], c_idx, g, r, k) for i, g, r, k in zip(ids, grads, from_sib, ks)]
        st["chip"] = _chip_send_start("chip_send_" + tag, halves, ks)
        st["token"] = st["chip"][2][0]

    def reduce_c(st, after):
        tag, ids, ks = st["tag"], st["ids"], st["ks"]
        send, recv, halves, lands = st["chip"]
        halves, lands = _chip_send_wait("chip_wait_" + tag, halves, lands, ks, send, recv, after)
        pieces = [_chip_sum("chip_sum_" + names[i], qc_idx, h, r, k) for i, h, r, k in zip(ids, halves, lands, ks)]
        st["share"] = _split_start("share_send_" + tag, pieces, len(ids), _pair_share_copies(ks, False))
        st["token"] = st["share"][2][0]

    def reduce_d(st, after):
        send, recv, bufs = st["share"]
        return _split_wait("share_wait_" + st["tag"], bufs, send, recv, _pair_share_copies(st["ks"], True), after)

    def ep_swiglu_bwd(ins, vals, outs, cs):
        df = vals[0]
        gtv, upv = ins[2][:, cs].astype(F32), ins[3][:, cs].astype(F32)
        sg = _sigmoid(gtv)
        _put(outs[0], cs, df * upv * (sg + gtv * sg * (1.0 - sg)))
        _put(outs[1], cs, df * (gtv * sg))

    dgt, dup = _mm(
        "mm_dswiglu", (s // tm, d_ff // tf, 1), [dx3b, g_down, gt, up],
        [pl.BlockSpec((tm, d), lambda i, j, k: (i, 0)), pl.BlockSpec((tf, d), lambda i, j, k: (j, 0)), f_out, f_out],
        [jax.ShapeDtypeStruct((s, d_ff), BF16), jax.ShapeDtypeStruct((s, d_ff), BF16)], [f_out, f_out],
        [(0, 1, NT, 0)], 1, (tm, tf), 1, ep_swiglu_bwd,
        _mm_vmem([((tm, d), BF16, 2), ((tf, d), BF16, 2), ((tm, tf), F32, 8)]),    )

    def ep_store(ins, vals, outs, cs):
        for o, v in zip(outs, vals):
            _put(o, cs, v)

    twn = _tile(d, (1024, 512))
    gw_down = _mm(
        "mm_gw_down", (d_ff // tkf, d // twn, 1), [f_act, dx3b],
        [pl.BlockSpec((s, tkf), lambda i, j, k: (0, i)), pl.BlockSpec((s, twn), lambda i, j, k: (0, j))],
        [jax.ShapeDtypeStruct((d_ff, d), BF16)], [pl.BlockSpec((tkf, twn), lambda i, j, k: (i, j))],
        [(0, 1, TN, 0)], 1, (tkf, twn), 1, ep_store,
        _mm_vmem([((s, tkf), BF16, 3), ((s, twn), BF16, 2), ((tkf, twn), F32, 3)]),
    )[0]
    red_down = reduce_a("down", [W_DOWN], [gw_down])

    tn2 = _tile(d, (256,))
    dh2_specs = [pl.BlockSpec((tm, d_ff), lambda i, j, k: (i, 0)), pl.BlockSpec((tn2, d_ff), lambda i, j, k: (j, 0))]
    dh2_tile = pl.BlockSpec((tm, tn2), lambda i, j, k: (i, j))
    dh2_vmem = _mm_vmem([((tm, d_ff), BF16, 2), ((tn2, d_ff), BF16, 2), ((tm, tn2), F32, 7)])
    dh2 = _mm(
        "mm_dh2_gate", (s // tm, d // tn2, 1), [dgt, g_gate], dh2_specs,
        [jax.ShapeDtypeStruct((s, d), F32)], [dh2_tile], [(0, 1, NT, 0)], 1, (tm, tn2), 1, ep_store, dh2_vmem,
        after=(red_down["token"],),
    )[0]
    dh2 = _mm(
        "mm_dh2_up", (s // tm, d // tn2, 1), [dup, g_up, dh2], dh2_specs + [dh2_tile],
        [jax.ShapeDtypeStruct((s, d), F32)], [dh2_tile], [(0, 1, NT, 0)], 1, (tm, tn2), 1, ep_residual, dh2_vmem,
    )[0]
    reduce_b(red_down, dh2)

    twr = _tile(d, (1024, 512))
    w_tile = pl.BlockSpec((twr, tf), lambda i, j, k: (i, j))
    gw_gate, gw_up = _mm(
        "mm_gw_gate_up", (d // twr, d_ff // tf, 1), [h2, dgt, dup],
        [pl.BlockSpec((s, twr), lambda i, j, k: (0, i)), pl.BlockSpec((s, tf), lambda i, j, k: (0, j)),
         pl.BlockSpec((s, tf), lambda i, j, k: (0, j))],
        [jax.ShapeDtypeStruct((d, d_ff), BF16), jax.ShapeDtypeStruct((d, d_ff), BF16)], [w_tile, w_tile],
        [(0, 1, TN, 0), (0, 2, TN, 1)], 2, (twr, tf), 1, ep_store,
        _mm_vmem([((s, twr), BF16, 3), ((s, tf), BF16, 4), ((twr, tf), F32, 6)]),
        after=(red_down["token"],),
    )
    red_ffn = reduce_a("ffn_in", [W_GATE, W_UP], [gw_gate, gw_up])

    dx2, dx2b, dg_ffn = _rms_bwd("rms_ffn_bwd", x2, norm_ffn, dh2, dx3, after=(red_ffn["token"],))

    nj = d // tg

    def lo(j):
        return jnp.minimum(j, nj - 1)

    def gate_bwd_body(dx_ref, wo_ref, ga_ref, gb_ref, ya_ref, yb_ref, dya_ref, dyb_ref, dz_ref, keep):
        j = pl.program_id(1)

        @pl.when(j < nj)
        def _():
            dm = lax.dot_general(dx_ref[...], wo_ref[...], NT, preferred_element_type=F32)
            sa, sb = _sigmoid(ga_ref[...].astype(F32)), _sigmoid(gb_ref[...].astype(F32))
            dya_ref[...] = (dm * sa).astype(BF16)
            dyb_ref[...] = (dm * sb).astype(BF16)
            dz_ref[...] = (dm * ya_ref[...].astype(F32) * (sa * (1.0 - sa))).astype(BF16)
            keep[lo(j)] = (dm * yb_ref[...].astype(F32) * (sb * (1.0 - sb))).astype(BF16)

        @pl.when(j >= nj)
        def _():
            dz_ref[...] = keep[jnp.maximum(j - nj, 0)]

    t_lo = pl.BlockSpec((tm, tg), lambda i, j: (i, lo(j)))
    dya, dyb, dz = _pallas(
        gate_bwd_body,
        name="mm_dgate",
        grid=(s // tm, 2 * nj),
        in_specs=[
            pl.BlockSpec((tm, d), lambda i, j: (i, 0)),
            pl.BlockSpec((tg, d), lambda i, j: (lo(j), 0)),
            pl.BlockSpec((tm, tg), lambda i, j: (i, ga0 + lo(j))),
            pl.BlockSpec((tm, tg), lambda i, j: (i, gb0 + lo(j))),
            t_lo,
            t_lo,
        ],
        out_specs=[t_lo, t_lo, pl.BlockSpec((tm, tg), lambda i, j: (i, ga0 + j))],
        out_shape=[jax.ShapeDtypeStruct((s, d), BF16), jax.ShapeDtypeStruct((s, d), BF16), jax.ShapeDtypeStruct((s, n_in), BF16)],
        scratch_shapes=[pltpu.VMEM((nj, tm, tg), BF16)],
        compiler_params=_params(_mm_vmem([((tm, d), BF16, 2), ((tg, d), BF16, 2), ((tm, tg), F32, 14), ((nj, tm, tg), BF16, 1)])),
    )(dx2b, g_o, z, z, y_a, y_b)
    reduce_b(red_ffn, dya)

    gw_o = _mm(
        "mm_gw_o", (d // twr, d // twn, 1), [m_act, dx2b],
        [pl.BlockSpec((s, twr), lambda i, j, k: (0, i)), pl.BlockSpec((s, twn), lambda i, j, k: (0, j))],
        [jax.ShapeDtypeStruct((d, d), BF16)], [pl.BlockSpec((twr, twn), lambda i, j, k: (i, j))],
        [(0, 1, TN, 0)], 1, (twr, twn), 1, ep_store,
        _mm_vmem([((s, twr), BF16, 3), ((s, twn), BF16, 2), ((twr, twn), F32, 3)]),
        after=(red_ffn["token"],),
    )[0]
    red_o = reduce_a("w_o", [W_O], [gw_o])

    tb = _tile(w_sgu, (1024, 512))
    b_out = pl.BlockSpec((tm, tb), lambda i, j, k: (i, j))

    da, datt = _mm(
        "mm_dbranches", (s // tm, w_sgu // tb, 1), [dya, g_a, dyb, g_b],
        [pl.BlockSpec((tm, d), lambda i, j, k: (i, 0)), pl.BlockSpec((tb, d), lambda i, j, k: (j, 0)),
         pl.BlockSpec((tm, d), lambda i, j, k: (i, 0)), pl.BlockSpec((tb, d), lambda i, j, k: (j, 0))],
        [jax.ShapeDtypeStruct((s, w_sgu), BF16), jax.ShapeDtypeStruct((s, w_att), BF16)], [b_out, b_out],
        [(0, 1, NT, 0), (2, 3, NT, 1)], 2, (tm, tb), 1, ep_store,
        _mm_vmem([((tm, d), BF16, 4), ((tb, d), BF16, 4), ((tm, tb), F32, 6)]),        after=(red_o["token"],),
    )
    reduce_b(red_o, da)

    wb_tile = pl.BlockSpec((tb, twn), lambda i, j, k: (i, j))
    gw_a, gw_b = _mm(
        "mm_gw_branches", (w_sgu // tb, d // twn, 1), [a_act, dya, att, dyb],
        [pl.BlockSpec((s, tb), lambda i, j, k: (0, i)), pl.BlockSpec((s, twn), lambda i, j, k: (0, j)),
         pl.BlockSpec((s, tb), lambda i, j, k: (0, i)), pl.BlockSpec((s, twn), lambda i, j, k: (0, j))],
        [jax.ShapeDtypeStruct((w_sgu, d), BF16), jax.ShapeDtypeStruct((w_att, d), BF16)], [wb_tile, wb_tile],
        [(0, 1, TN, 0), (2, 3, TN, 1)], 2, (tb, twn), 1, ep_store,
        _mm_vmem([((s, tb), BF16, 5), ((s, twn), BF16, 4), ((tb, twn), F32, 6)]),
        after=(red_o["token"],),
    )
    red_mix = reduce_a("mix", [W_A, W_B], [gw_a, gw_b])

    dz, dws, dbs, dgain = _sgu_bwd(z, da, sgu_v_gain, ws_b, wst_b, b_col, w_sgu, dz, after=(red_mix["token"],))
    dz, dk_pad, dv_pad, dbias_tab, dsink = _attn_bwd(sink, z, k_pad, v_pad, bias_tab, datt, dz, grp, q_blk0)
    dz = _dkv_to_dz(dk_pad, dv_pad, dz, off_k // (2 * w_kv))
    drel = _relbias_bwd(dbias_tab, bucket)
    reduce_b(red_mix, dz)

    small_w = [norm_mix, sgu_v_gain, sgu_w_s, sgu_b_s, attn_sink, rel_bias, norm_ffn, norm_final]
    small_m = [m_norm_mix, m_sgu_v_gain, m_sgu_w_s, m_sgu_b_s, m_attn_sink, m_rel_bias, m_norm_ffn, m_norm_final]
    small_v = [v_norm_mix, v_sgu_v_gain, v_sgu_w_s, v_sgu_b_s, v_attn_sink, v_rel_bias, v_norm_ffn, v_norm_final]
    small_shapes = [w.shape for w in small_w]
    early = [dgain, dws, dbs, dsink[:, 0], drel[:, :REL_BUCKETS].T, dg_ffn, dg_final]
    p_early = _pack([g.reshape(shp) for g, shp in zip(early, small_shapes[1:])] + [loss_part[0, :1]])
    land = jnp.zeros((2 * N_CHIPS,) + p_early.shape, F32)
    sm_send, sm_recv, (p_early, land) = _split_start("small_send", [p_early, land], 2 * N_CHIPS - 1, _small_exchange_copies(False))

    tzn = _tile(n_in, (768, 640, 512))
    gw_in = _mm(
        "mm_gw_in", (d // twr, n_in // tzn, 1), [h1, dz],
        [pl.BlockSpec((s, twr), lambda i, j, k: (0, i)), pl.BlockSpec((s, tzn), lambda i, j, k: (0, j))],
        [jax.ShapeDtypeStruct((d, n_in), BF16)], [pl.BlockSpec((twr, tzn), lambda i, j, k: (i, j))],
        [(0, 1, TN, 0)], 1, (twr, tzn), 1, ep_store,
        _mm_vmem([((s, twr), BF16, 3), ((s, tzn), BF16, 2), ((twr, tzn), F32, 3)]),
        after=(red_mix["token"], p_early),
    )[0]
    red_in = reduce_a("w_in", [W_IN], [gw_in])

    dh1 = _mm(
        "mm_dh1", (s // tm, d // tn2, 1), [dz, g_in],
        [pl.BlockSpec((tm, n_in), lambda i, j, k: (i, 0)), pl.BlockSpec((tn2, n_in), lambda i, j, k: (j, 0))],
        [jax.ShapeDtypeStruct((s, d), F32)], [pl.BlockSpec((tm, tn2), lambda i, j, k: (i, j))],
        [(0, 1, NT, 0)], 1, (tm, tn2), 1, ep_store,
        _mm_vmem([((tm, n_in), BF16, 2), ((tn2, n_in), BF16, 2), ((tm, tn2), F32, 5)]),
        after=(red_in["token"],),
    )[0]

    reduce_b(red_in, dh1)
    grad_x, _, dg_mix = _rms_bwd("rms_mix_bwd", x2d, norm_mix, dh1, dx2, after=(red_in["token"],))

    p_mix = _pack([dg_mix.reshape(small_shapes[0])])
    land_mix = jnp.zeros((2 * N_CHIPS,) + p_mix.shape, F32)
    mx_send, mx_recv, (p_mix, land_mix) = _split_start("mix_send", [p_mix, land_mix], 2 * N_CHIPS - 1, _small_exchange_copies(False))

    grads_big, upd = [None] * 7, [None] * 7

    def finish(st, after):
        for i, g in zip(st["ids"], reduce_d(st, after)):
            upd[i] = _adamw("adamw_" + names[i], big_w[i], g, big_m[i], big_v[i])
            grads_big[i] = upd[i][3]
            after = upd[i][0]
        return after

    after, prev = p_mix, None
    for st in (red_down, red_ffn, red_o, red_mix, red_in):
        reduce_c(st, after)
        after = st["token"] if prev is None else finish(prev, st["token"])
        prev = st

    p_early, land = _split_wait("small_wait", [p_early, land], sm_send, sm_recv, _small_exchange_copies(True), after)
    p_mix, land_mix = _split_wait("mix_wait", [p_mix, land_mix], mx_send, mx_recv, _small_exchange_copies(True), p_early)
    me_idx = 2 * q_idx + c_idx
    packed_g = jnp.concatenate([_small_sum("mix_sum", me_idx, p_mix, land_mix), _small_sum("small_sum", me_idx, p_early, land)], axis=0)
    g_small = _unpack(packed_g, small_shapes + [(1,)])
    loss = g_small[-1].reshape(())
    g_small = g_small[:-1]
    zero1 = jnp.zeros((1,), F32)
    pw, pg, pm, pv = _pack(small_w + [zero1]), _pack(g_small + [zero1]), _pack(small_m + [zero1]), _pack(small_v + [zero1])
    small_upd = _adamw("adamw_small", pw, pg, pm, pv)
    d_small, nm_small, nv_small = [_unpack(a, small_shapes) for a in small_upd[:3]]
    finish(prev, small_upd[0])

    small_names = ["norm_mix", "sgu_v_gain", "sgu_w_s", "sgu_b_s", "attn_sink", "rel_bias", "norm_ffn", "norm_final"]
    table = {}
    for i, n in enumerate(names):
        table[n] = (grads_big[i][None], upd[i][0][None], upd[i][1][None], upd[i][2][None])
    for i, n in enumerate(small_names):
        table[n] = (g_small[i], d_small[i], nm_small[i], nv_small[i])
    order = ["w_in", "norm_mix", "sgu_v_gain", "sgu_w_s", "sgu_b_s", "w_a", "attn_sink", "rel_bias", "w_b", "w_o", "norm_ffn",
             "w_gate", "w_up", "w_down", "norm_final"]
    outs = [loss, grad_x.reshape(1, s, d)]
    for part in range(4):
        outs += [table[n][part] for n in order]
    return tuple(outs)
```

```python
import math

import jax
import jax.numpy as jnp
import numpy as np
from jax import lax
from jax.experimental import pallas as pl
from jax.experimental.pallas import tpu as pltpu
from jax.experimental.pallas import tpu_sc as plsc

F32 = jnp.float32
BF16 = jnp.bfloat16
I32 = jnp.int32
MESH = pl.DeviceIdType.MESH

EPS = 1e-6
NEG = -1e30
BLK = 128
HEAD_DIM = 128
N_KV_HEADS = 2
REL_BUCKETS = 32
REL_MAX_DIST = 128
N_CHIPS = 4
ADAM_LR, ADAM_B1, ADAM_B2, ADAM_EPS, ADAM_WD, ADAM_STEP = 0.001, 0.9, 0.999, 1e-08, 0.01, 10

LANES = 128
MXU_COLS = 256
VMEM_CAP = 60 * 1024 * 1024

NN = (((1,), (0,)), ((), ()))
NT = (((1,), (1,)), ((), ()))
TN = (((0,), (0,)), ((), ()))
ANY = pl.BlockSpec(memory_space=pl.ANY)
HBM_SPEC = pl.BlockSpec(memory_space=pltpu.HBM)
SEM_SPEC = pl.BlockSpec(memory_space=pltpu.SEMAPHORE)
EFFECT = pltpu.SideEffectType.DATAFLOW_SIDE_EFFECTING


def _tile(n, cands):
    for t in cands:
        if n % t == 0:
            return t
    return n


PIN_BYTES = 64 * 1024


def _pin_hbm(a):
    big = hasattr(a, "dtype") and jnp.issubdtype(a.dtype, jnp.floating) and _nbytes(a.shape, a.dtype) >= PIN_BYTES
    return pltpu.with_memory_space_constraint(a, pltpu.HBM) if big else a


def _pallas(body, *, out_shape, **kw):
    def pin(o):
        big = isinstance(o, jax.ShapeDtypeStruct) and jnp.issubdtype(o.dtype, jnp.floating) and _nbytes(o.shape, o.dtype) >= PIN_BYTES
        return pltpu.HBM(o.shape, o.dtype) if big else o

    shapes = type(out_shape)(pin(o) for o in out_shape) if isinstance(out_shape, (list, tuple)) else pin(out_shape)
    call = pl.pallas_call(body, out_shape=shapes, **kw)
    return lambda *args: call(*[_pin_hbm(a) for a in args])


def _params(vmem_bytes=None, **kw):
    if vmem_bytes is not None:
        kw["vmem_limit_bytes"] = int(min(max(vmem_bytes, 32 * 1024 * 1024), VMEM_CAP))
    return pltpu.CompilerParams(**kw)


def _nbytes(shape, dtype):
    return int(np.prod(shape)) * jnp.dtype(dtype).itemsize


def _sigmoid(x):
    return 1.0 / (1.0 + jnp.exp(-x))


_GC = 0.7978845608028654
_GA = 0.044715


def _gelu(x):
    return 0.5 * x * (1.0 + jnp.tanh(_GC * (x + _GA * (x * x * x))))


def _gelu_grad(x):
    t = jnp.tanh(_GC * (x + _GA * (x * x * x)))
    return 0.5 * (1.0 + t) + 0.5 * x * (1.0 - t * t) * (_GC * (1.0 + 3.0 * _GA * (x * x)))


def _bf(v):
    return v if v.dtype == BF16 else v.astype(BF16)


def _mm(name, grid, ins, in_specs, out_shape, out_specs, pairs, n_acc, tile, nk, epilogue, vmem_bytes, after=(), col_chunks=1):
    assert nk == 1 and tile[1] % col_chunks == 0
    n_in, n_out = len(ins) + len(after), len(out_shape)
    width = tile[1] // col_chunks

    def body(*refs):
        in_refs, out_refs = refs[:n_in], refs[n_in : n_in + n_out]
        for ch in range(col_chunks):
            cs = slice(ch * width, (ch + 1) * width) if col_chunks > 1 else slice(None)
            vals = [None] * n_acc
            for a_i, b_i, dn, acc_i in pairs:
                rhs = in_refs[b_i][cs, :] if dn == NT else in_refs[b_i][:, cs]
                d = lax.dot_general(_bf(in_refs[a_i][...]), _bf(rhs), dn, preferred_element_type=F32)
                vals[acc_i] = d if vals[acc_i] is None else vals[acc_i] + d
            epilogue(in_refs, vals, out_refs, cs)

    return _pallas(
        body,
        name=name,
        grid=grid,
        in_specs=list(in_specs) + [ANY] * len(after),
        out_specs=out_specs,
        out_shape=out_shape,
        compiler_params=_params(vmem_bytes),
    )(*ins, *after)


def _put(ref, cs, v):
    ref[:, cs] = v.astype(ref.dtype)


def _mm_vmem(tiles):
    return sum(_nbytes(s, d) * c for s, d, c in tiles) + 4 * 1024 * 1024


def _rows8(v):
    r, d = v.shape
    return v.reshape(r // 8, 8, d).sum(axis=0)


def _rms_fwd(name, x, g, after=()):
    s, d = x.shape
    tm = _tile(s, (256, 128))

    def body(x_ref, g_ref, *rest):
        h_ref = rest[-1]
        xv = x_ref[...]
        r = lax.rsqrt(jnp.mean(xv * xv, axis=-1, keepdims=True) + EPS)
        h_ref[...] = ((xv * r) * g_ref[...]).astype(BF16)

    return _pallas(
        body,
        name=name,
        grid=(s // tm,),
        in_specs=[pl.BlockSpec((tm, d), lambda i: (i, 0)), pl.BlockSpec((1, d), lambda i: (0, 0))] + [ANY] * len(after),
        out_specs=pl.BlockSpec((tm, d), lambda i: (i, 0)),
        out_shape=jax.ShapeDtypeStruct((s, d), BF16),
    )(x, g, *after)


def _rms_bwd(name, x, g, dh, dres, after=()):
    s, d = x.shape
    tm = _tile(s, (256, 128))
    n = s // tm
    n_after = len(after)

    def body(x_ref, g_ref, dh_ref, dres_ref, *rest):
        dx_ref, dxb_ref, dg_ref, acc_ref = rest[n_after:]
        i = pl.program_id(0)
        xv = x_ref[...]
        r = lax.rsqrt(jnp.mean(xv * xv, axis=-1, keepdims=True) + EPS)
        xh = xv * r
        dhv = dh_ref[...]
        dxh = dhv * g_ref[...]
        dx = r * (dxh - xh * jnp.mean(dxh * xh, axis=-1, keepdims=True)) + dres_ref[...]
        dx_ref[...] = dx
        dxb_ref[...] = dx.astype(BF16)
        part = _rows8(dhv * xh)

        @pl.when(i == 0)
        def _():
            acc_ref[...] = part

        @pl.when(i > 0)
        def _():
            acc_ref[...] += part

        @pl.when(i == n - 1)
        def _():
            dg_ref[...] = jnp.sum(acc_ref[...], axis=0, keepdims=True)

    row = pl.BlockSpec((tm, d), lambda i: (i, 0))
    vec = pl.BlockSpec((1, d), lambda i: (0, 0))
    return _pallas(
        body,
        name=name,
        grid=(n,),
        in_specs=[row, vec, row, row] + [ANY] * n_after,
        out_specs=[row, row, vec],
        out_shape=[jax.ShapeDtypeStruct((s, d), F32), jax.ShapeDtypeStruct((s, d), BF16), jax.ShapeDtypeStruct((1, d), F32)],
        scratch_shapes=[pltpu.VMEM((8, d), F32)],
    )(x, g, dh, dres, *after)


def _head(x3, g, target):
    s, d = x3.shape
    tm = _tile(s, (256, 128))
    n = s // tm

    def body(x_ref, g_ref, t_ref, dx_ref, dxb_ref, dg_ref, loss_ref, acc_g, acc_l):
        i = pl.program_id(0)
        xv = x_ref[...]
        gv = g_ref[...]
        r = lax.rsqrt(jnp.mean(xv * xv, axis=-1, keepdims=True) + EPS)
        xh = xv * r
        e = xh * gv - t_ref[...]
        dy = e * (1.0 / d)
        dxh = dy * gv
        dx = r * (dxh - xh * jnp.mean(dxh * xh, axis=-1, keepdims=True))
        dx_ref[...] = dx
        dxb_ref[...] = dx.astype(BF16)
        pg = _rows8(dy * xh)
        plo = _rows8(e * e)

        @pl.when(i == 0)
        def _():
            acc_g[...] = pg
            acc_l[...] = plo

        @pl.when(i > 0)
        def _():
            acc_g[...] += pg
            acc_l[...] += plo

        @pl.when(i == n - 1)
        def _():
            dg_ref[...] = jnp.sum(acc_g[...], axis=0, keepdims=True)
            loss_ref[...] = jnp.full((1, LANES), (0.5 / d) * jnp.sum(acc_l[...]), F32)

    row = pl.BlockSpec((tm, d), lambda i: (i, 0))
    vec = pl.BlockSpec((1, d), lambda i: (0, 0))
    return _pallas(
        body,
        name="head",
        grid=(n,),
        in_specs=[row, vec, row],
        out_specs=[row, row, vec, pl.BlockSpec((1, LANES), lambda i: (0, 0))],
        out_shape=[
            jax.ShapeDtypeStruct((s, d), F32),
            jax.ShapeDtypeStruct((s, d), BF16),
            jax.ShapeDtypeStruct((1, d), F32),
            jax.ShapeDtypeStruct((1, LANES), F32),
        ],
        scratch_shapes=[pltpu.VMEM((8, d), F32), pltpu.VMEM((8, d), F32)],
    )(x3, g, target)


def _sgu_fwd(z, gain, ws_b, b_col, w_sgu):
    s = z.shape[0]
    groups = ws_b.shape[0]

    def body(zu_ref, zv_ref, gain_ref, ws_ref, b_ref, a_ref):
        vv = _gelu(zv_ref[...].astype(F32))
        r = lax.rsqrt(jnp.mean(vv * vv, axis=-1, keepdims=True) + EPS)
        vn = ((vv * r) * gain_ref[...]).astype(BF16)
        u = _gelu(zu_ref[...].astype(F32))
        for g in range(groups):
            sl = slice(g * BLK, (g + 1) * BLK)
            mixed = jnp.dot(ws_ref[g], vn[:, sl], preferred_element_type=F32) + b_ref[g]
            a_ref[:, sl] = (u[:, sl] * mixed).astype(BF16)

    return _pallas(
        body,
        name="sgu_fwd",
        grid=(s // BLK,),
        in_specs=[
            pl.BlockSpec((BLK, w_sgu), lambda c: (c, 0)),
            pl.BlockSpec((BLK, w_sgu), lambda c: (c, 1)),
            pl.BlockSpec((1, w_sgu), lambda c: (0, 0)),
            pl.BlockSpec((groups, BLK, BLK), lambda c: (0, 0, 0)),
            pl.BlockSpec((groups, BLK, 1), lambda c: (0, 0, 0)),
        ],
        out_specs=pl.BlockSpec((BLK, w_sgu), lambda c: (c, 0)),
        out_shape=jax.ShapeDtypeStruct((s, w_sgu), BF16),
    )(z, z, gain, ws_b, b_col)


def _sgu_bwd(z, da, gain, ws_b, wst_b, b_col, w_sgu, dz, after=()):
    s = z.shape[0]
    groups = ws_b.shape[0]
    n = s // BLK
    n_skip = 1 + len(after)

    def body(zu_ref, zv_ref, da_ref, gain_ref, ws_ref, wst_ref, b_ref, *rest):
        dz_ref, dws_ref, dbs_ref, dgain_ref, acc_gain = rest[n_skip:]
        c = pl.program_id(0)
        zu = zu_ref[...].astype(F32)
        zv = zv_ref[...].astype(F32)
        gain_v = gain_ref[...]
        vv = _gelu(zv)
        r = lax.rsqrt(jnp.mean(vv * vv, axis=-1, keepdims=True) + EPS)
        xh = vv * r
        vn = (xh * gain_v).astype(BF16)
        u = _gelu(zu)
        dav = da_ref[...].astype(F32)
        dmix = dav * u
        dmix_b = dmix.astype(BF16)
        dvn_parts = []
        for g in range(groups):
            sl = slice(g * BLK, (g + 1) * BLK)
            mixed = jnp.dot(ws_ref[g], vn[:, sl], preferred_element_type=F32) + b_ref[g]
            dz_ref[:, sl] = (dav[:, sl] * mixed * _gelu_grad(zu[:, sl])).astype(BF16)
            dvn_parts.append(jnp.dot(wst_ref[g], dmix_b[:, sl], preferred_element_type=F32))
            dws_g = lax.dot_general(dmix_b[:, sl], vn[:, sl], NT, preferred_element_type=F32)
            dbs_g = jnp.sum(dmix[:, sl], axis=1, keepdims=True)

            @pl.when(c == 0)
            def _():
                dws_ref[g] = dws_g
                dbs_ref[g] = dbs_g

            @pl.when(c > 0)
            def _():
                dws_ref[g] += dws_g
                dbs_ref[g] += dbs_g

        dvn = jnp.concatenate(dvn_parts, axis=1)
        dxh = dvn * gain_v
        dvv = r * (dxh - xh * jnp.mean(dxh * xh, axis=-1, keepdims=True))
        dz_ref[:, w_sgu:] = (dvv * _gelu_grad(zv)).astype(BF16)
        pg = _rows8(dvn * xh)

        @pl.when(c == 0)
        def _():
            acc_gain[...] = pg

        @pl.when(c > 0)
        def _():
            acc_gain[...] += pg

        @pl.when(c == n - 1)
        def _():
            dgain_ref[...] = jnp.sum(acc_gain[...], axis=0, keepdims=True)

    full3 = pl.BlockSpec((groups, BLK, BLK), lambda c: (0, 0, 0))
    col3 = pl.BlockSpec((groups, BLK, 1), lambda c: (0, 0, 0))
    vec = pl.BlockSpec((1, w_sgu), lambda c: (0, 0))
    return _pallas(
        body,
        name="sgu_bwd",
        grid=(n,),
        in_specs=[
            pl.BlockSpec((BLK, w_sgu), lambda c: (c, 0)),
            pl.BlockSpec((BLK, w_sgu), lambda c: (c, 1)),
            pl.BlockSpec((BLK, w_sgu), lambda c: (c, 0)),
            vec,
            full3,
            full3,
            col3,
            ANY,
        ]
        + [ANY] * len(after),
        out_specs=[pl.BlockSpec((BLK, 2 * w_sgu), lambda c: (c, 0)), full3, col3, vec],
        out_shape=[
            jax.ShapeDtypeStruct(dz.shape, BF16),
            jax.ShapeDtypeStruct((groups, BLK, BLK), F32),
            jax.ShapeDtypeStruct((groups, BLK, 1), F32),
            jax.ShapeDtypeStruct((1, w_sgu), F32),
        ],
        scratch_shapes=[pltpu.VMEM((8, w_sgu), F32)],
        input_output_aliases={7: 0},
    )(z, z, da, gain, ws_b, wst_b, b_col, dz, *after)


def _attn_softmax(sink_ref, q_ref, k_ref, v_ref, bias_ref, s_len, grp):
    kv = pl.program_id(0)
    n = pl.program_id(1)
    start = pl.multiple_of(n * BLK, BLK)
    kb = k_ref[pl.ds(start, 3 * BLK), :]
    vb = v_ref[pl.ds(start, 3 * BLK), :]
    qv = q_ref[...]
    qs = jnp.concatenate([qv[:, g * HEAD_DIM : (g + 1) * HEAD_DIM] for g in range(grp)], axis=0).astype(BF16)
    sc = lax.dot_general(qs, kb, NT, preferred_element_type=F32) * (HEAD_DIM**-0.5)
    sc = sc + bias_ref[...].reshape(grp * BLK, 3 * BLK)
    kpos = start + lax.broadcasted_iota(I32, (1, 3 * BLK), 1) - BLK
    sc = jnp.where((kpos >= 0) & (kpos < s_len), sc, NEG)
    sink = jnp.concatenate([jnp.full((BLK, 1), sink_ref[kv * grp + g], F32) for g in range(grp)], axis=0)
    m = jnp.maximum(jnp.max(sc, axis=-1, keepdims=True), sink)
    p = jnp.exp(sc - m)
    esink = jnp.exp(sink - m)
    den = jnp.sum(p, axis=-1, keepdims=True) + esink
    return start, qs, kb, vb, p / den, esink / den


def _attn_specs(s, grp, q_blk0):
    qw = grp * HEAD_DIM
    return [
        pl.BlockSpec(memory_space=pltpu.SMEM),
        pl.BlockSpec((BLK, qw), lambda kv, n: (n, q_blk0 + kv)),
        pl.BlockSpec((s + 2 * BLK, HEAD_DIM), lambda kv, n: (0, kv)),
        pl.BlockSpec((s + 2 * BLK, HEAD_DIM), lambda kv, n: (0, kv)),
        pl.BlockSpec((grp, BLK, 3 * BLK), lambda kv, n: (kv, 0, 0)),
    ]


def _attn_fwd(sink, z, k_pad, v_pad, bias_tab, grp, q_blk0):
    s = z.shape[0]
    qw = grp * HEAD_DIM

    def body(sink_ref, q_ref, k_ref, v_ref, bias_ref, o_ref):
        _, _, _, vb, pn, _ = _attn_softmax(sink_ref, q_ref, k_ref, v_ref, bias_ref, s, grp)
        o = jnp.dot(pn.astype(BF16), vb, preferred_element_type=F32)
        for g in range(grp):
            o_ref[:, g * HEAD_DIM : (g + 1) * HEAD_DIM] = o[g * BLK : (g + 1) * BLK].astype(BF16)

    return _pallas(
        body,
        name="attn_fwd",
        grid=(N_KV_HEADS, s // BLK),
        in_specs=_attn_specs(s, grp, q_blk0),
        out_specs=pl.BlockSpec((BLK, qw), lambda kv, n: (n, kv)),
        out_shape=jax.ShapeDtypeStruct((s, N_KV_HEADS * qw), BF16),
    )(sink, z, k_pad, v_pad, bias_tab)


def _attn_bwd(sink, z, k_pad, v_pad, bias_tab, dout, dz, grp, q_blk0):
    s = z.shape[0]
    qw = grp * HEAD_DIM
    nb = s // BLK
    heads = N_KV_HEADS * grp

    def body(sink_ref, q_ref, k_ref, v_ref, bias_ref, do_ref, dz_in, dq_ref, dk_ref, dv_ref, dbias_ref, dsink_ref, dk_acc, dv_acc):
        del dz_in
        kv = pl.program_id(0)
        n = pl.program_id(1)
        start, qs, kb, vb, pn, psink = _attn_softmax(sink_ref, q_ref, k_ref, v_ref, bias_ref, s, grp)
        dov = do_ref[...]
        dos = jnp.concatenate([dov[:, g * HEAD_DIM : (g + 1) * HEAD_DIM] for g in range(grp)], axis=0)
        dp = lax.dot_general(dos, vb, NT, preferred_element_type=F32)
        dvb = lax.dot_general(pn.astype(BF16), dos, TN, preferred_element_type=F32)
        delta = jnp.sum(pn * dp, axis=-1, keepdims=True)
        ds = pn * (dp - delta)
        dsb = (ds * (HEAD_DIM**-0.5)).astype(BF16)
        dq = jnp.dot(dsb, kb, preferred_element_type=F32)
        dkb = lax.dot_general(dsb, qs, TN, preferred_element_type=F32)
        for g in range(grp):
            dq_ref[:, g * HEAD_DIM : (g + 1) * HEAD_DIM] = dq[g * BLK : (g + 1) * BLK].astype(BF16)

        @pl.when(n == 0)
        def _():
            dk_acc[...] = jnp.zeros_like(dk_acc)
            dv_acc[...] = jnp.zeros_like(dv_acc)
            dbias_ref[...] = jnp.zeros_like(dbias_ref)

        @pl.when((n == 0) & (kv == 0))
        def _():
            dsink_ref[...] = jnp.zeros_like(dsink_ref)

        dk_acc[pl.ds(start, 3 * BLK), :] += dkb
        dv_acc[pl.ds(start, 3 * BLK), :] += dvb
        dbias_ref[...] += ds.reshape(grp, BLK, 3 * BLK)
        row = lax.broadcasted_iota(I32, (heads, LANES), 0)
        sd = psink * delta
        upd = jnp.zeros((heads, LANES), F32)
        for g in range(grp):
            upd = jnp.where(row == kv * grp + g, -jnp.sum(sd[g * BLK : (g + 1) * BLK]), upd)
        dsink_ref[...] += upd

        @pl.when(n == nb - 1)
        def _():
            dk_ref[...] = dk_acc[...]
            dv_ref[...] = dv_acc[...]

    pad_spec = pl.BlockSpec((s + 2 * BLK, HEAD_DIM), lambda kv, n: (0, kv))
    kvw = N_KV_HEADS * HEAD_DIM
    return _pallas(
        body,
        name="attn_bwd",
        grid=(N_KV_HEADS, nb),
        in_specs=_attn_specs(s, grp, q_blk0) + [pl.BlockSpec((BLK, qw), lambda kv, n: (n, kv)), ANY],
        out_specs=[
            pl.BlockSpec((BLK, qw), lambda kv, n: (n, q_blk0 + kv)),
            pad_spec,
            pad_spec,
            pl.BlockSpec((grp, BLK, 3 * BLK), lambda kv, n: (kv, 0, 0)),
            pl.BlockSpec((heads, LANES), lambda kv, n: (0, 0)),
        ],
        out_shape=[
            jax.ShapeDtypeStruct(dz.shape, BF16),
            jax.ShapeDtypeStruct((s + 2 * BLK, kvw), F32),
            jax.ShapeDtypeStruct((s + 2 * BLK, kvw), F32),
            jax.ShapeDtypeStruct((heads, BLK, 3 * BLK), F32),
            jax.ShapeDtypeStruct((heads, LANES), F32),
        ],
        scratch_shapes=[pltpu.VMEM((s + 2 * BLK, HEAD_DIM), F32), pltpu.VMEM((s + 2 * BLK, HEAD_DIM), F32)],
        input_output_aliases={6: 0},
    )(sink, z, k_pad, v_pad, bias_tab, dout, dz)


def _dkv_to_dz(dk_pad, dv_pad, dz, blk_idx):
    s = dz.shape[0]
    kvw = dk_pad.shape[1]

    def body(dk_ref, dv_ref, dz_in, out_ref):
        del dz_in
        out_ref[:, :kvw] = dk_ref[...].astype(BF16)
        out_ref[:, kvw:] = dv_ref[...].astype(BF16)

    src = pl.BlockSpec((BLK, kvw), lambda i: (i + 1, 0))
    return _pallas(
        body,
        name="dkv_to_dz",
        grid=(s // BLK,),
        in_specs=[src, src, ANY],
        out_specs=pl.BlockSpec((BLK, 2 * kvw), lambda i: (i, blk_idx)),
        out_shape=jax.ShapeDtypeStruct(dz.shape, BF16),
        input_output_aliases={2: 0},
    )(dk_pad, dv_pad, dz)


def _relbias_bwd(dbias_tab, bucket):
    heads = dbias_tab.shape[0]

    def body(dt_ref, bk_ref, out_ref):
        lane = lax.broadcasted_iota(I32, (1, LANES), 1)
        bk = bk_ref[...]
        rows = []
        for h in range(heads):
            dt = dt_ref[h]
            acc = jnp.zeros((1, LANES), F32)
            for b in range(REL_BUCKETS):
                acc = jnp.where(lane == b, jnp.sum(jnp.where(bk == b, dt, 0.0)), acc)
            rows.append(acc)
        out_ref[...] = jnp.concatenate(rows, axis=0)

    return _pallas(body, name="relbias_bwd", out_shape=jax.ShapeDtypeStruct((heads, LANES), F32))(dbias_tab, bucket)


def _t5_bucket(rel):
    nb = REL_BUCKETS // 2
    ret = jnp.where(rel > 0, nb, 0)
    n = jnp.abs(rel)
    max_exact = nb // 2
    nf = jnp.maximum(n, 1).astype(F32)
    large = max_exact + (jnp.log(nf / max_exact) / math.log(REL_MAX_DIST / max_exact) * (nb - max_exact)).astype(I32)
    large = jnp.minimum(large, nb - 1)
    return ret + jnp.where(n < max_exact, n, large)


def _band_tables(rel_bias):
    qi = jnp.arange(BLK)[:, None]
    kj = jnp.arange(3 * BLK)[None, :]
    rel = kj - BLK - qi
    bucket = _t5_bucket(rel).astype(I32)
    heads = rel_bias.shape[1]
    masked = jnp.where(jnp.abs(rel) <= BLK, bucket, -1)

    def body(rb_ref, bk_ref, out_ref):
        bk = bk_ref[...]
        for h in range(heads):
            tab = jnp.full(bk.shape, NEG, F32)
            for b in range(REL_BUCKETS):
                tab = jnp.where(bk == b, rb_ref[b, h], tab)
            out_ref[h] = tab

    bias_tab = _pallas(
        body,
        name="bias_table",
        in_specs=[pl.BlockSpec(memory_space=pltpu.SMEM), pl.BlockSpec(memory_space=pltpu.VMEM)],
        out_specs=pl.BlockSpec(memory_space=pltpu.VMEM),
        out_shape=jax.ShapeDtypeStruct((heads, BLK, 3 * BLK), F32),
    )(rel_bias.astype(F32), masked)
    return bias_tab, bucket


EW_BLOCK_ELEMS = 512 * 1024


def _ew_tiles(shape, elems=EW_BLOCK_ELEMS // 2):
    r, c = shape
    tn = c if c <= 2048 else _tile(c, (2048, 1920, 1536, 1408, 1024, 512))
    tm = _tile(r, [t for t in (1024, 512, 256, 128, 64, 32, 16, 8) if t * tn <= elems] or [8])
    return tm, tn


def _cast_into_full(name, qidx, w, kind, after=()):
    r, c = w.shape
    tm, tn = _ew_tiles(w.shape, EW_BLOCK_ELEMS)
    nbi, nbj = r // tm, c // tn
    if kind == "col":
        full, out_spec = (r, c * N_CHIPS), pl.BlockSpec((tm, tn), lambda i, j, q: (i, q[0] * nbj + j))
    else:
        full, out_spec = (r * N_CHIPS, c), pl.BlockSpec((tm, tn), lambda i, j, q: (q[0] * nbi + i, j))

    def body(q_ref, w_ref, *rest):
        del q_ref
        rest[-1][...] = w_ref[...].astype(BF16)

    return _pallas(
        body,
        name=name,
        grid_spec=pltpu.PrefetchScalarGridSpec(
            num_scalar_prefetch=1,
            grid=(nbi, nbj),
            in_specs=[pl.BlockSpec((tm, tn), lambda i, j, q: (i, j))] + [ANY] * len(after),
            out_specs=out_spec,
        ),
        out_shape=jax.ShapeDtypeStruct(full, BF16),
    )(qidx, w, *after)


def _adamw(name, w, g, m, v, after=()):
    tm, tn = _ew_tiles(w.shape, EW_BLOCK_ELEMS)
    if _nbytes(w.shape, F32) <= 1024 * 1024:
        tm, tn = w.shape
    spec = pl.BlockSpec((tm, tn), lambda i, j: (i, j))
    n_after = len(after)

    def body(w_ref, g_ref, m_ref, v_ref, *rest):
        d_ref, nm_ref, nv_ref, g_out_ref = rest[n_after:]
        gv = g_ref[...]
        g_out_ref[...] = gv
        nm = ADAM_B1 * m_ref[...] + (1.0 - ADAM_B1) * gv
        nv = ADAM_B2 * v_ref[...] + (1.0 - ADAM_B2) * (gv * gv)
        m_hat = nm / (1.0 - ADAM_B1**ADAM_STEP)
        v_hat = nv / (1.0 - ADAM_B2**ADAM_STEP)
        d_ref[...] = -ADAM_LR * (m_hat / (jnp.sqrt(v_hat) + ADAM_EPS) + ADAM_WD * w_ref[...])
        nm_ref[...] = nm
        nv_ref[...] = nv

    out = jax.ShapeDtypeStruct(w.shape, F32)
    return _pallas(
        body, name=name, grid=(w.shape[0] // tm, w.shape[1] // tn), in_specs=[spec] * 4 + [ANY] * n_after,
        out_specs=[spec] * 4, out_shape=[out, out, out, out],
        compiler_params=_params(_mm_vmem([((tm, tn), F32, 24)])),
    )(w, g, m, v, *after)


SC_TILES = 32
SC_LANES = 16
SC_ROWS = 8


def _sc_adamw(name, w, g, m, v):
    r, c = w.shape
    row_tiles = next(t for t in (32, 16, 8) if r % (t * SC_ROWS) == 0 and c % ((SC_TILES // t) * LANES) == 0)
    col_tiles = SC_TILES // row_tiles
    rows, cw = r // row_tiles, c // col_tiles
    c1, c2 = 1.0 - ADAM_B1**ADAM_STEP, 1.0 - ADAM_B2**ADAM_STEP

    def body(w_hbm, g_hbm, m_hbm, v_hbm, d_hbm, nm_hbm, nv_hbm, go_hbm, wb, gb, mb, vb, sem):
        tile = lax.axis_index("subcore") * 2 + lax.axis_index("score")
        r0 = (tile // col_tiles) * rows
        c0 = (tile % col_tiles) * cw

        @pl.loop(0, rows, step=SC_ROWS)
        def _(rr):
            win = (pl.ds(r0 + rr, SC_ROWS), pl.ds(c0, cw))
            loads = [pltpu.make_async_copy(src.at[win], dst, sem) for src, dst in
                     ((w_hbm, wb), (g_hbm, gb), (m_hbm, mb), (v_hbm, vb))]
            for cp in loads:
                cp.start()
            for cp in loads:
                cp.wait()
            for row in range(SC_ROWS):

                @plsc.parallel_loop(0, cw, step=SC_LANES, unroll=4)
                def _(i):
                    sl = (row, pl.ds(i, SC_LANES))
                    gv = gb[sl]
                    nm = ADAM_B1 * mb[sl] + (1.0 - ADAM_B1) * gv
                    nv = ADAM_B2 * vb[sl] + (1.0 - ADAM_B2) * (gv * gv)
                    wb[sl] = -ADAM_LR * ((nm / c1) / (jnp.sqrt(nv / c2) + ADAM_EPS) + ADAM_WD * wb[sl])
                    mb[sl] = nm
                    vb[sl] = nv

            stores = [pltpu.make_async_copy(src, dst.at[win], sem) for src, dst in
                      ((wb, d_hbm), (mb, nm_hbm), (vb, nv_hbm), (gb, go_hbm))]
            for cp in stores:
                cp.start()
            for cp in stores:
                cp.wait()

    out = jax.ShapeDtypeStruct(w.shape, F32)
    return pl.kernel(
        body,
        name=name,
        out_type=(out, out, out, out),
        mesh=plsc.VectorSubcoreMesh(core_axis_name="score", subcore_axis_name="subcore"),
        scratch_types=[pltpu.VMEM((SC_ROWS, cw), F32)] * 4 + [pltpu.SemaphoreType.DMA],
    )(w, g, m, v)


def _pair_add(name, cidx, g_full, r_sib, kind):
    hr, hc = r_sib.shape
    tm, tn = _ew_tiles((hr, hc), 2 * EW_BLOCK_ELEMS)
    nbi, nbj = hr // tm, hc // tn
    if kind == "col":
        g_spec = pl.BlockSpec((tm, tn), lambda i, j, c: (c[0] * nbi + i, j))
    else:
        g_spec = pl.BlockSpec((tm, tn), lambda i, j, c: (i, c[0] * nbj + j))
    spec = pl.BlockSpec((tm, tn), lambda i, j, c: (i, j))

    def body(c_ref, g_ref, r_ref, o_ref):
        del c_ref
        o_ref[...] = (g_ref[...].astype(F32) + r_ref[...].astype(F32)).astype(BF16)

    return _pallas(
        body,
        name=name,
        grid_spec=pltpu.PrefetchScalarGridSpec(num_scalar_prefetch=1, grid=(nbi, nbj), in_specs=[g_spec, spec], out_specs=spec),
        out_shape=jax.ShapeDtypeStruct((hr, hc), BF16),
        compiler_params=_params(_mm_vmem([((tm, tn), BF16, 6), ((tm, tn), F32, 3)])),
    )(cidx, g_full, r_sib)


def _chip_sum(name, qidx, c_half, r_ici, kind):
    _, pr, pc = r_ici.shape
    tm, tn = _ew_tiles((pr, pc), 2 * EW_BLOCK_ELEMS)
    nbi, nbj = pr // tm, pc // tn
    if kind == "col":
        own_spec = pl.BlockSpec((tm, tn), lambda i, j, q: (i, q[0] * nbj + j))
        full, out_spec = (2 * pr, pc), pl.BlockSpec((tm, tn), lambda i, j, q: (q[1] * nbi + i, j))
    else:
        own_spec = pl.BlockSpec((tm, tn), lambda i, j, q: (q[0] * nbi + i, j))
        full, out_spec = (pr, 2 * pc), pl.BlockSpec((tm, tn), lambda i, j, q: (i, q[1] * nbj + j))

    def body(q_ref, own_ref, r_ref, o_ref):
        q = q_ref[0]
        own = own_ref[...].astype(F32)
        recv = [r_ref[r].astype(F32) for r in range(3)]
        total = None
        for chip in range(N_CHIPS):
            d = chip ^ q
            term = jnp.where(d == 0, own, jnp.where(d == 2, recv[0], jnp.where(d == 1, recv[1], recv[2])))
            total = term if total is None else total + term
        o_ref[...] = total

    return _pallas(
        body,
        name=name,
        grid_spec=pltpu.PrefetchScalarGridSpec(
            num_scalar_prefetch=1,
            grid=(nbi, nbj),
            in_specs=[own_spec, pl.BlockSpec((3, tm, tn), lambda i, j, q: (0, i, j))],
            out_specs=out_spec,
        ),
        out_shape=jax.ShapeDtypeStruct(full, F32),
        compiler_params=_params(_mm_vmem([((tm, tn), BF16, 8), ((tm, tn), F32, 6)])),
    )(qidx, c_half, r_ici)


_REL_MASK = (2, 1, 3)


def _place():
    x, y, c = lax.axis_index("x"), lax.axis_index("y"), lax.axis_index("c")
    chips = [(1 - x, y), (x, 1 - y), (1 - x, 1 - y)]
    return x, y, c, 2 * x + y, chips


def _shard_view(ref, kind, chip):
    if kind == "col":
        w = ref.shape[1] // N_CHIPS
        return ref.at[:, pl.ds(pl.multiple_of(chip * w, LANES), w)]
    h = ref.shape[0] // N_CHIPS
    return ref.at[pl.ds(pl.multiple_of(chip * h, 16), h), :]


def _row_half(ref, half):
    h = ref.shape[0] // 2
    return ref.at[pl.ds(pl.multiple_of(half * h, 16), h), :]


def _pair_half(ref, kind, half):
    if kind == "col":
        return _row_half(ref, half)
    w = ref.shape[1] // 2
    return ref.at[:, pl.ds(pl.multiple_of(half * w, LANES), w)]


def _remote(src, dst, send_sem, recv_sem, dev):
    return pltpu.make_async_remote_copy(src_ref=src, dst_ref=dst, send_sem=send_sem, recv_sem=recv_sem, device_id=dev, device_id_type=MESH)


def _hbm(a):
    return pltpu.with_memory_space_constraint(a, pltpu.HBM)


def _gather_start(name, fulls, kinds):
    n_w = len(fulls)

    def body(*refs):
        g = refs[:n_w]
        send_sem, recv_sem = refs[n_w], refs[n_w + 1]
        token = refs[-1]
        _, _, c, q, chips = _place()
        for w in range(n_w):
            mine = _row_half(_shard_view(g[w], kinds[w], q), c)
            for r, chip in enumerate(chips):
                _remote(mine, mine, send_sem.at[3 * w + r], recv_sem.at[3 * w + r], (*chip, c)).start()
        token[...] = jnp.zeros_like(token)

    res = _pallas(
        body,
        name=name,
        out_shape=(
            pltpu.SemaphoreType.DMA((3 * n_w,)),
            pltpu.SemaphoreType.DMA((3 * n_w,)),
            *[pltpu.HBM(f.shape, f.dtype) for f in fulls],
            jax.ShapeDtypeStruct((8, LANES), F32),
        ),
        in_specs=[HBM_SPEC] * n_w,
        out_specs=(SEM_SPEC, SEM_SPEC, *[HBM_SPEC] * n_w, pl.BlockSpec(memory_space=pltpu.VMEM)),
        input_output_aliases={w: w + 2 for w in range(n_w)},
        compiler_params=pltpu.CompilerParams(has_side_effects=EFFECT),
    )(*[_hbm(f) for f in fulls])
    return res[0], res[1], list(res[2 : 2 + n_w]), res[-1]


def _gather_wait(name, fulls, kinds, w_ids, send_sem, recv_sem, after):
    n = len(fulls)

    def body(*refs):
        g = refs[:n]
        s_sem, r_sem = refs[n], refs[n + 1]
        x, y, c, q, _ = _place()
        for i, w in enumerate(w_ids):
            mine = _row_half(_shard_view(g[i], kinds[i], q), c)
            for r in range(3):
                landed = _row_half(_shard_view(g[i], kinds[i], q ^ _REL_MASK[r]), c)
                cp = _remote(mine, landed, s_sem.at[3 * w + r], r_sem.at[3 * w + r], (x, y, 1 - c))
                cp.wait_send()
                cp.wait_recv()

    res = _pallas(
        body,
        name=name,
        out_shape=[pltpu.HBM(f.shape, f.dtype) for f in fulls],
        in_specs=[HBM_SPEC] * n + [SEM_SPEC, SEM_SPEC, ANY],
        out_specs=[HBM_SPEC] * n,
        input_output_aliases={i: i for i in range(n)},
        compiler_params=pltpu.CompilerParams(has_side_effects=EFFECT),
    )(*fulls, send_sem, recv_sem, after)
    return list(res)


def _gather_forward(name, fulls, kinds):
    n = len(fulls)

    def body(*refs):
        g = refs[n : 2 * n]
        send, recv = refs[2 * n :]
        x, y, c, q, _ = _place()
        sib = (x, y, 1 - c)
        cps = []
        for i in range(n):
            for r in range(3):
                landed = _row_half(_shard_view(g[i], kinds[i], q ^ _REL_MASK[r]), c)
                cps.append(_remote(landed, landed, send.at[i, r], recv.at[i, r], sib))
        for cp in cps:
            cp.start()
        for i in range(n):
            for r in range(3):
                other = _row_half(_shard_view(g[i], kinds[i], q ^ _REL_MASK[r]), 1 - c)
                _remote(other, other, send.at[i, r], recv.at[i, r], sib).wait_recv()
        for cp in cps:
            cp.wait_send()

    res = _pallas(
        body,
        name=name,
        in_specs=[ANY] * n,
        out_specs=[ANY] * n,
        out_shape=[jax.ShapeDtypeStruct(f.shape, f.dtype) for f in fulls],
        scratch_shapes=[pltpu.SemaphoreType.DMA((n, 3)), pltpu.SemaphoreType.DMA((n, 3))],
        input_output_aliases={i: i for i in range(n)},
    )(*fulls)
    return list(res)


def _split_start(name, bufs, n_sems, copies):
    n = len(bufs)

    def body(*refs):
        for cp in copies(refs[:n], refs[n], refs[n + 1]):
            cp.start()

    res = _pallas(
        body,
        name=name,
        out_shape=(
            pltpu.SemaphoreType.DMA((n_sems,)),
            pltpu.SemaphoreType.DMA((n_sems,)),
            *[pltpu.HBM(b.shape, b.dtype) for b in bufs],
        ),
        in_specs=[HBM_SPEC] * n,
        out_specs=(SEM_SPEC, SEM_SPEC, *[HBM_SPEC] * n),
        input_output_aliases={i: i + 2 for i in range(n)},
        compiler_params=pltpu.CompilerParams(has_side_effects=EFFECT),
    )(*[_hbm(b) for b in bufs])
    return res[0], res[1], list(res[2:])


def _split_wait(name, bufs, send_sem, recv_sem, copies, after):
    n = len(bufs)

    def body(*refs):
        for cp in copies(refs[:n], refs[n], refs[n + 1]):
            cp.wait_send()
            cp.wait_recv()

    res = _pallas(
        body,
        name=name,
        out_shape=[pltpu.HBM(b.shape, b.dtype) for b in bufs],
        in_specs=[HBM_SPEC] * n + [SEM_SPEC, SEM_SPEC, ANY],
        out_specs=[HBM_SPEC] * n,
        input_output_aliases={i: i for i in range(n)},
        compiler_params=pltpu.CompilerParams(has_side_effects=EFFECT),
    )(*bufs, send_sem, recv_sem, after)
    return list(res)


def _pair_exchange_copies(kinds):
    n = len(kinds)

    def copies(refs, send_sem, recv_sem):
        x, y, c, _, _ = _place()
        return [
            _remote(_pair_half(refs[w], kinds[w], 1 - c), refs[n + w], send_sem.at[w], recv_sem.at[w], (x, y, 1 - c))
            for w in range(n)
        ]

    return copies


def _pair_share_copies(kinds, waiting):
    def copies(refs, send_sem, recv_sem):
        x, y, c, _, _ = _place()
        out = []
        for w, kind in enumerate(kinds):
            mine = _pair_half(refs[w], kind, c)
            dst = _pair_half(refs[w], kind, 1 - c) if waiting else mine
            out.append(_remote(mine, dst, send_sem.at[w], recv_sem.at[w], (x, y, 1 - c)))
        return out

    return copies


def _piece_shape(half_shape, kind):
    r, c = half_shape
    return (3, r, c // N_CHIPS) if kind == "col" else (3, r // N_CHIPS, c)


def _chip_send_start(name, halves, kinds):
    n = len(halves)
    lands = [lax.empty(_piece_shape(h.shape, k), BF16) for h, k in zip(halves, kinds)]

    def body(*refs):
        h, land = refs[:n], refs[n : 2 * n]
        send_sem, recv_sem = refs[2 * n], refs[2 * n + 1]
        _, _, c, q, chips = _place()
        for i in range(n):
            for r, chip in enumerate(chips):
                piece = _shard_view(h[i], kinds[i], q ^ _REL_MASK[r])
                _remote(piece, land[i].at[r], send_sem.at[3 * i + r], recv_sem.at[3 * i + r], (*chip, c)).start()

    res = _pallas(
        body,
        name=name,
        out_shape=(
            pltpu.SemaphoreType.DMA((3 * n,)),
            pltpu.SemaphoreType.DMA((3 * n,)),
            *[pltpu.HBM(a.shape, a.dtype) for a in halves],
            *[pltpu.HBM(a.shape, a.dtype) for a in lands],
        ),
        in_specs=[HBM_SPEC] * (2 * n),
        out_specs=(SEM_SPEC, SEM_SPEC, *[HBM_SPEC] * (2 * n)),
        input_output_aliases={i: i + 2 for i in range(2 * n)},
        compiler_params=pltpu.CompilerParams(has_side_effects=EFFECT),
    )(*[_hbm(a) for a in halves], *[_hbm(a) for a in lands])
    return res[0], res[1], list(res[2 : 2 + n]), list(res[2 + n :])


def _chip_send_wait(name, halves, lands, kinds, send_sem, recv_sem, after):
    n = len(halves)

    def body(*refs):
        h, land = refs[:n], refs[n : 2 * n]
        s_sem, r_sem = refs[2 * n], refs[2 * n + 1]
        x, y, c, q, _ = _place()
        for i in range(n):
            for r in range(3):
                piece = _shard_view(h[i], kinds[i], q ^ _REL_MASK[r])
                cp = _remote(piece, land[i].at[r], s_sem.at[3 * i + r], r_sem.at[3 * i + r], (x, y, 1 - c))
                cp.wait_send()
                cp.wait_recv()

    res = _pallas(
        body,
        name=name,
        out_shape=[pltpu.HBM(a.shape, a.dtype) for a in halves] + [pltpu.HBM(a.shape, a.dtype) for a in lands],
        in_specs=[HBM_SPEC] * (2 * n) + [SEM_SPEC, SEM_SPEC, ANY],
        out_specs=[HBM_SPEC] * (2 * n),
        input_output_aliases={i: i for i in range(2 * n)},
        compiler_params=pltpu.CompilerParams(has_side_effects=EFFECT),
    )(*halves, *lands, send_sem, recv_sem, after)
    return list(res[:n]), list(res[n:])


def _small_all_reduce(p):
    rows = p.shape[0]
    n_dev = 2 * N_CHIPS

    def body(p_ref, o_ref, buf, loc_sem, send, recv):
        x, y, c, q, _ = _place()
        me = 2 * q + c
        own = pltpu.make_async_copy(p_ref, buf.at[me], loc_sem)
        own.start()
        cps = []
        for d in range(1, n_dev):
            dev = (x ^ ((d >> 2) & 1), y ^ ((d >> 1) & 1), c ^ (d & 1))
            cps.append(_remote(p_ref, buf.at[me], send.at[d - 1], recv.at[d - 1], dev))
        for cp in cps:
            cp.start()
        for d in range(1, n_dev):
            slot = buf.at[me ^ d]
            _remote(slot, slot, send.at[d - 1], recv.at[d - 1], (x, y, c)).wait_recv()
        own.wait()
        total = buf[0]
        for d in range(1, n_dev):
            total = total + buf[d]
        o_ref[...] = total
        for cp in cps:
            cp.wait_send()

    return _pallas(
        body,
        name="small_all_reduce",
        in_specs=[ANY],
        out_specs=pl.BlockSpec(memory_space=pltpu.VMEM),
        out_shape=jax.ShapeDtypeStruct(p.shape, F32),
        scratch_shapes=[
            pltpu.VMEM((n_dev, rows, LANES), F32),
            pltpu.SemaphoreType.DMA,
            pltpu.SemaphoreType.DMA((n_dev - 1,)),
            pltpu.SemaphoreType.DMA((n_dev - 1,)),
        ],
    )(p)


def _small_exchange_copies(waiting):
    def copies(refs, send_sem, recv_sem):
        p, land = refs
        x, y, c, q, _ = _place()
        me = 2 * q + c
        out = []
        for dd in range(1, 2 * N_CHIPS):
            dev = (x ^ ((dd >> 2) & 1), y ^ ((dd >> 1) & 1), c ^ (dd & 1))
            dst = land.at[me ^ dd] if waiting else land.at[me]
            out.append(_remote(p, dst, send_sem.at[dd - 1], recv_sem.at[dd - 1], dev))
        return out

    return copies


def _small_sum(name, me_idx, p, land):
    rows = p.shape[0]
    n_dev = 2 * N_CHIPS

    def body(me_ref, p_ref, land_ref, o_ref):
        me = me_ref[0]
        total = None
        for dev in range(n_dev):
            term = jnp.where(me == dev, p_ref[...], land_ref[dev])
            total = term if total is None else total + term
        o_ref[...] = total

    return _pallas(
        body,
        name=name,
        grid_spec=pltpu.PrefetchScalarGridSpec(
            num_scalar_prefetch=1,
            grid=(1,),
            in_specs=[pl.BlockSpec((rows, LANES), lambda i, m: (0, 0)), pl.BlockSpec((n_dev, rows, LANES), lambda i, m: (0, 0, 0))],
            out_specs=pl.BlockSpec((rows, LANES), lambda i, m: (0, 0)),
        ),
        out_shape=jax.ShapeDtypeStruct(p.shape, F32),
    )(me_idx, p, land)


def _pack(parts):
    rows = []
    for a in parts:
        flat = a.reshape(-1).astype(F32)
        n = flat.shape[0]
        padded = -(-n // (8 * LANES)) * (8 * LANES)
        rows.append(jnp.pad(flat, (0, padded - n)).reshape(-1, LANES))
    return jnp.concatenate(rows, axis=0)


def _unpack(packed, shapes):
    out, row = [], 0
    for shp in shapes:
        n = int(np.prod(shp))
        nrows = -(-n // (8 * LANES)) * 8
        out.append(packed[row : row + nrows].reshape(-1)[:n].reshape(shp))
        row += nrows
    return out


def kernel(x, w_in, norm_mix, sgu_v_gain, sgu_w_s, sgu_b_s, w_a_out, attn_sink, rel_bias, w_b_out, w_o, norm_ffn, w_gate, w_up, w_down, norm_final, loss_target, m_w_in, m_norm_mix, m_sgu_v_gain, m_sgu_w_s, m_sgu_b_s, m_w_a_out, m_attn_sink, m_rel_bias, m_w_b_out, m_w_o, m_norm_ffn, m_w_gate, m_w_up, m_w_down, m_norm_final, v_w_in, v_norm_mix, v_sgu_v_gain, v_sgu_w_s, v_sgu_b_s, v_w_a_out, v_attn_sink, v_rel_bias, v_w_b_out, v_w_o, v_norm_ffn, v_w_gate, v_w_up, v_w_down, v_norm_final):
    s, d = x.shape[1], x.shape[2]
    w_sgu = sgu_v_gain.shape[1]
    groups = sgu_w_s.shape[1]
    heads = attn_sink.shape[1]
    grp = heads // N_KV_HEADS
    w_att = heads * HEAD_DIM
    w_kv = N_KV_HEADS * HEAD_DIM
    d_ff = w_gate.shape[2] * N_CHIPS
    n_in = w_in.shape[2] * N_CHIPS
    off_q = 2 * w_sgu
    off_k = off_q + w_att
    off_g = off_k + 2 * w_kv
    assert n_in == off_g + 2 * d and groups * BLK == w_sgu and s % BLK == 0

    x2d = x.reshape(s, d)
    tgt = loss_target.reshape(s, d)
    c_idx = lax.axis_index("c").astype(I32).reshape(1)
    q_idx = (2 * lax.axis_index("x") + lax.axis_index("y")).astype(I32).reshape(1)
    qc_idx = jnp.concatenate([q_idx, c_idx])

    W_IN, W_A, W_B, W_O, W_GATE, W_UP, W_DOWN = range(7)
    names = ["w_in", "w_a", "w_b", "w_o", "w_gate", "w_up", "w_down"]
    kinds = ["col", "col", "col", "row", "col", "col", "row"]
    big_w = [w_in[0], w_a_out[0], w_b_out[0], w_o[0], w_gate[0], w_up[0], w_down[0]]
    big_m = [m_w_in[0], m_w_a_out[0], m_w_b_out[0], m_w_o[0], m_w_gate[0], m_w_up[0], m_w_down[0]]
    big_v = [v_w_in[0], v_w_a_out[0], v_w_b_out[0], v_w_o[0], v_w_gate[0], v_w_up[0], v_w_down[0]]
    full_in = _cast_into_full("cast_w_in", q_idx, big_w[W_IN], kinds[W_IN])
    in_send, in_recv, (full_in,), token = _gather_start("gather_start_in", [full_in], [kinds[W_IN]])
    rest = [_cast_into_full("cast_" + names[i], q_idx, big_w[i], kinds[i], after=(token,)) for i in range(1, 7)]
    ag_send, ag_recv, rest, token = _gather_start("gather_start_rest", rest, kinds[1:])
    fulls = [full_in] + rest

    def gathered(tag, ids, after):
        if ids == [W_IN]:
            sems, pos = (in_send, in_recv), [0]
        else:
            sems, pos = (ag_send, ag_recv), [i - 1 for i in ids]
        got = _gather_wait("gather_wait_" + tag, [fulls[i] for i in ids], [kinds[i] for i in ids], pos, *sems, after)
        return _gather_forward("gather_fwd_" + tag, got, [kinds[i] for i in ids])

    ws_b = sgu_w_s[0].astype(BF16)
    wst_b = jnp.swapaxes(sgu_w_s[0], 1, 2).astype(BF16)
    b_col = sgu_b_s[0].reshape(groups, BLK, 1)
    bias_tab, bucket = _band_tables(rel_bias)
    sink = attn_sink[0]

    tm = _tile(s, (1024, 512, 256, 128))

    h1 = _rms_fwd("rms_mix", x2d, norm_mix, after=(token,))
    (g_in,) = gathered("in", [W_IN], h1)

    tn = _tile(n_in, (768, 640, 512))
    z = _mm(
        "mm_z", (s // tm, n_in // tn, 1), [h1, g_in],
        [pl.BlockSpec((tm, d), lambda i, j, k: (i, 0)), pl.BlockSpec((d, tn), lambda i, j, k: (0, j))],
        [jax.ShapeDtypeStruct((s, n_in), BF16)], [pl.BlockSpec((tm, tn), lambda i, j, k: (i, j))],
        [(0, 1, NN, 0)], 1, (tm, tn), 1, lambda ins, vals, outs, cs: _put(outs[0], cs, vals[0]),
        _mm_vmem([((tm, d), BF16, 2), ((d, tn), BF16, 2), ((tm, tn), F32, 3)]),
    )[0]
    g_a, g_b, g_o = gathered("mix", [W_A, W_B, W_O], z)

    a_act = _sgu_fwd(z, sgu_v_gain, ws_b, b_col, w_sgu)

    kv_b = z[:, off_k:off_g]
    k_pad = jnp.pad(kv_b[:, :w_kv], ((BLK, BLK), (0, 0)))
    v_pad = jnp.pad(kv_b[:, w_kv:], ((BLK, BLK), (0, 0)))
    q_blk0 = off_q // (grp * HEAD_DIM)
    att = _attn_fwd(sink, z, k_pad, v_pad, bias_tab, grp, q_blk0)

    tg = _tile(d, (512,))
    ga0, gb0 = off_g // tg, (off_g + d) // tg

    def ep_gate(ins, vals, outs, cs):
        sa, sb = _sigmoid(ins[4][:, cs].astype(F32)), _sigmoid(ins[5][:, cs].astype(F32))
        _put(outs[0], cs, sa * vals[0] + sb * vals[1])
        _put(outs[1], cs, vals[0])
        _put(outs[2], cs, vals[1])

    t_out = pl.BlockSpec((tm, tg), lambda i, j, k: (i, j))
    m_act, y_a, y_b = _mm(
        "mm_branches", (s // tm, d // tg, 1), [a_act, g_a, att, g_b, z, z],
        [pl.BlockSpec((tm, w_sgu), lambda i, j, k: (i, 0)), pl.BlockSpec((w_sgu, tg), lambda i, j, k: (0, j)),
         pl.BlockSpec((tm, w_att), lambda i, j, k: (i, 0)), pl.BlockSpec((w_att, tg), lambda i, j, k: (0, j)),
         pl.BlockSpec((tm, tg), lambda i, j, k: (i, ga0 + j)), pl.BlockSpec((tm, tg), lambda i, j, k: (i, gb0 + j))],
        [jax.ShapeDtypeStruct((s, d), BF16)] * 3,
        [t_out, t_out, t_out], [(0, 1, NN, 0), (2, 3, NN, 1)], 2, (tm, tg), 1, ep_gate,
        _mm_vmem([((tm, w_sgu), BF16, 4), ((w_sgu, tg), BF16, 4), ((tm, tg), F32, 12)]),    )

    tn = _tile(d, (1024, 512))

    def ep_residual(ins, vals, outs, cs):
        _put(outs[0], cs, ins[2][:, cs] + vals[0])

    x2 = _mm(
        "mm_wo", (s // tm, d // tn, 1), [m_act, g_o, x2d],
        [pl.BlockSpec((tm, d), lambda i, j, k: (i, 0)), pl.BlockSpec((d, tn), lambda i, j, k: (0, j)),
         pl.BlockSpec((tm, tn), lambda i, j, k: (i, j))],
        [jax.ShapeDtypeStruct((s, d), F32)], [pl.BlockSpec((tm, tn), lambda i, j, k: (i, j))],
        [(0, 1, NN, 0)], 1, (tm, tn), 1, ep_residual,
        _mm_vmem([((tm, d), BF16, 2), ((d, tn), BF16, 2), ((tm, tn), F32, 5)]),    )[0]

    h2 = _rms_fwd("rms_ffn", x2, norm_ffn)
    g_gate, g_up = gathered("ffn_in", [W_GATE, W_UP], h2)

    tf = _tile(d_ff, (512,))

    def ep_swiglu(ins, vals, outs, cs):
        gt, up = vals
        _put(outs[0], cs, gt)
        _put(outs[1], cs, up)
        _put(outs[2], cs, (gt * _sigmoid(gt)) * up)

    f_out = pl.BlockSpec((tm, tf), lambda i, j, k: (i, j))
    gt, up, f_act = _mm(
        "mm_gate_up", (s // tm, d_ff // tf, 1), [h2, g_gate, g_up],
        [pl.BlockSpec((tm, d), lambda i, j, k: (i, 0)), pl.BlockSpec((d, tf), lambda i, j, k: (0, j)),
         pl.BlockSpec((d, tf), lambda i, j, k: (0, j))],
        [jax.ShapeDtypeStruct((s, d_ff), BF16)] * 3,
        [f_out, f_out, f_out], [(0, 1, NN, 0), (0, 2, NN, 1)], 2, (tm, tf), 1, ep_swiglu,
        _mm_vmem([((tm, d), BF16, 2), ((d, tf), BF16, 4), ((tm, tf), F32, 8)]),    )
    (g_down,) = gathered("ffn_out", [W_DOWN], f_act)

    tkf = _tile(d_ff, (1408, 1024, 512))
    nkf = d_ff // tkf
    tml, tnl = _tile(s, (512, 256, 128)), _tile(d, (512,))
    x3 = _mm(
        "mm_down", (s // tml, d // tnl, 1), [f_act, g_down, x2],
        [pl.BlockSpec((tml, d_ff), lambda i, j, k: (i, 0)), pl.BlockSpec((d_ff, tnl), lambda i, j, k: (0, j)),
         pl.BlockSpec((tml, tnl), lambda i, j, k: (i, j))],
        [jax.ShapeDtypeStruct((s, d), F32)], [pl.BlockSpec((tml, tnl), lambda i, j, k: (i, j))],
        [(0, 1, NN, 0)], 1, (tml, tnl), 1, ep_residual,
        _mm_vmem([((tml, d_ff), BF16, 2), ((d_ff, tnl), BF16, 2), ((tml, tnl), F32, 6)]),    )[0]

    dx3, dx3b, dg_final, loss_part = _head(x3, norm_final.reshape(1, d), tgt)

    def reduce_a(tag, ids, grads):
        ks = [kinds[i] for i in ids]
        lands = [lax.empty((g.shape[0] // 2, g.shape[1]) if k == "col" else (g.shape[0], g.shape[1] // 2), BF16)
                 for g, k in zip(grads, ks)]
        send, recv, bufs = _split_start("pair_send_" + tag, list(grads) + lands, len(ids), _pair_exchange_copies(ks))
        return {"tag": tag, "ids": ids, "ks": ks, "pair": (send, recv, bufs), "token": bufs[0]}

    def reduce_b(st, after):
        tag, ids, ks = st["tag"], st["ids"], st["ks"]
        send, recv, bufs = st["pair"]
        bufs = _split_wait("pair_wait_" + tag, bufs, send, recv, _pair_exchange_copies(ks), after)
        grads, from_sib = bufs[: len(ids)], bufs[len(ids) :]
        halves = [_pair_add("pair_add_" + names[i], c_idx, g, r, k) for i, g, r, k in zip(ids, grads, from_sib, ks)]
        st["chip"] = _chip_send_start("chip_send_" + tag, halves, ks)
        st["token"] = st["chip"][2][0]

    def reduce_c(st, after):
        tag, ids, ks = st["tag"], st["ids"], st["ks"]
        send, recv, halves, lands = st["chip"]
        halves, lands = _chip_send_wait("chip_wait_" + tag, halves, lands, ks, send, recv, after)
        pieces = [_chip_sum("chip_sum_" + names[i], qc_idx, h, r, k) for i, h, r, k in zip(ids, halves, lands, ks)]
        st["share"] = _split_start("share_send_" + tag, pieces, len(ids), _pair_share_copies(ks, False))
        st["token"] = st["share"][2][0]

    def reduce_d(st, after):
        send, recv, bufs = st["share"]
        return _split_wait("share_wait_" + st["tag"], bufs, send, recv, _pair_share_copies(st["ks"], True), after)

    grads_big, upd = [None] * 7, [None] * 7
    sc_updates = (W_DOWN,)

    def finish(st, after):
        for i, g in zip(st["ids"], reduce_d(st, after)):
            if i in sc_updates:
                upd[i] = _sc_adamw("sc_adamw_" + names[i], big_w[i], g, big_m[i], big_v[i])
            else:
                upd[i] = _adamw("adamw_" + names[i], big_w[i], g, big_m[i], big_v[i])
                after = upd[i][0]
            grads_big[i] = upd[i][3]
        return after

    def ep_swiglu_bwd(ins, vals, outs, cs):
        df = vals[0]
        gtv, upv = ins[2][:, cs].astype(F32), ins[3][:, cs].astype(F32)
        sg = _sigmoid(gtv)
        _put(outs[0], cs, df * upv * (sg + gtv * sg * (1.0 - sg)))
        _put(outs[1], cs, df * (gtv * sg))

    dgt, dup = _mm(
        "mm_dswiglu", (s // tm, d_ff // tf, 1), [dx3b, g_down, gt, up],
        [pl.BlockSpec((tm, d), lambda i, j, k: (i, 0)), pl.BlockSpec((tf, d), lambda i, j, k: (j, 0)), f_out, f_out],
        [jax.ShapeDtypeStruct((s, d_ff), BF16), jax.ShapeDtypeStruct((s, d_ff), BF16)], [f_out, f_out],
        [(0, 1, NT, 0)], 1, (tm, tf), 1, ep_swiglu_bwd,
        _mm_vmem([((tm, d), BF16, 2), ((tf, d), BF16, 2), ((tm, tf), F32, 8)]),    )

    def ep_store(ins, vals, outs, cs):
        for o, v in zip(outs, vals):
            _put(o, cs, v)

    twn = _tile(d, (1024, 512))
    gw_down = _mm(
        "mm_gw_down", (d_ff // tkf, d // twn, 1), [f_act, dx3b],
        [pl.BlockSpec((s, tkf), lambda i, j, k: (0, i)), pl.BlockSpec((s, twn), lambda i, j, k: (0, j))],
        [jax.ShapeDtypeStruct((d_ff, d), BF16)], [pl.BlockSpec((tkf, twn), lambda i, j, k: (i, j))],
        [(0, 1, TN, 0)], 1, (tkf, twn), 1, ep_store,
        _mm_vmem([((s, tkf), BF16, 3), ((s, twn), BF16, 2), ((tkf, twn), F32, 3)]),
    )[0]
    red_down = reduce_a("down", [W_DOWN], [gw_down])

    tn2 = _tile(d, (256,))
    dh2_specs = [pl.BlockSpec((tm, d_ff), lambda i, j, k: (i, 0)), pl.BlockSpec((tn2, d_ff), lambda i, j, k: (j, 0))]
    dh2_tile = pl.BlockSpec((tm, tn2), lambda i, j, k: (i, j))
    dh2_vmem = _mm_vmem([((tm, d_ff), BF16, 2), ((tn2, d_ff), BF16, 2), ((tm, tn2), F32, 7)])
    dh2 = _mm(
        "mm_dh2_gate", (s // tm, d // tn2, 1), [dgt, g_gate], dh2_specs,
        [jax.ShapeDtypeStruct((s, d), F32)], [dh2_tile], [(0, 1, NT, 0)], 1, (tm, tn2), 1, ep_store, dh2_vmem,
        after=(red_down["token"],),
    )[0]
    dh2 = _mm(
        "mm_dh2_up", (s // tm, d // tn2, 1), [dup, g_up, dh2], dh2_specs + [dh2_tile],
        [jax.ShapeDtypeStruct((s, d), F32)], [dh2_tile], [(0, 1, NT, 0)], 1, (tm, tn2), 1, ep_residual, dh2_vmem,
    )[0]
    reduce_b(red_down, dh2)

    twr = _tile(d, (1024, 512))
    w_tile = pl.BlockSpec((twr, tf), lambda i, j, k: (i, j))
    gw_gate, gw_up = _mm(
        "mm_gw_gate_up", (d // twr, d_ff // tf, 1), [h2, dgt, dup],
        [pl.BlockSpec((s, twr), lambda i, j, k: (0, i)), pl.BlockSpec((s, tf), lambda i, j, k: (0, j)),
         pl.BlockSpec((s, tf), lambda i, j, k: (0, j))],
        [jax.ShapeDtypeStruct((d, d_ff), BF16), jax.ShapeDtypeStruct((d, d_ff), BF16)], [w_tile, w_tile],
        [(0, 1, TN, 0), (0, 2, TN, 1)], 2, (twr, tf), 1, ep_store,
        _mm_vmem([((s, twr), BF16, 3), ((s, tf), BF16, 4), ((twr, tf), F32, 6)]),
        after=(red_down["token"],),
    )
    red_ffn = reduce_a("ffn_in", [W_GATE, W_UP], [gw_gate, gw_up])

    dx2, dx2b, dg_ffn = _rms_bwd("rms_ffn_bwd", x2, norm_ffn, dh2, dx3, after=(red_ffn["token"],))

    nj = d // tg

    def lo(j):
        return jnp.minimum(j, nj - 1)

    def gate_bwd_body(dx_ref, wo_ref, ga_ref, gb_ref, ya_ref, yb_ref, dya_ref, dyb_ref, dz_ref, keep):
        j = pl.program_id(1)

        @pl.when(j < nj)
        def _():
            dm = lax.dot_general(dx_ref[...], wo_ref[...], NT, preferred_element_type=F32)
            sa, sb = _sigmoid(ga_ref[...].astype(F32)), _sigmoid(gb_ref[...].astype(F32))
            dya_ref[...] = (dm * sa).astype(BF16)
            dyb_ref[...] = (dm * sb).astype(BF16)
            dz_ref[...] = (dm * ya_ref[...].astype(F32) * (sa * (1.0 - sa))).astype(BF16)
            keep[lo(j)] = (dm * yb_ref[...].astype(F32) * (sb * (1.0 - sb))).astype(BF16)

        @pl.when(j >= nj)
        def _():
            dz_ref[...] = keep[jnp.maximum(j - nj, 0)]

    t_lo = pl.BlockSpec((tm, tg), lambda i, j: (i, lo(j)))
    dya, dyb, dz = _pallas(
        gate_bwd_body,
        name="mm_dgate",
        grid=(s // tm, 2 * nj),
        in_specs=[
            pl.BlockSpec((tm, d), lambda i, j: (i, 0)),
            pl.BlockSpec((tg, d), lambda i, j: (lo(j), 0)),
            pl.BlockSpec((tm, tg), lambda i, j: (i, ga0 + lo(j))),
            pl.BlockSpec((tm, tg), lambda i, j: (i, gb0 + lo(j))),
            t_lo,
            t_lo,
        ],
        out_specs=[t_lo, t_lo, pl.BlockSpec((tm, tg), lambda i, j: (i, ga0 + j))],
        out_shape=[jax.ShapeDtypeStruct((s, d), BF16), jax.ShapeDtypeStruct((s, d), BF16), jax.ShapeDtypeStruct((s, n_in), BF16)],
        scratch_shapes=[pltpu.VMEM((nj, tm, tg), BF16)],
        compiler_params=_params(_mm_vmem([((tm, d), BF16, 2), ((tg, d), BF16, 2), ((tm, tg), F32, 14), ((nj, tm, tg), BF16, 1)])),
    )(dx2b, g_o, z, z, y_a, y_b)
    reduce_b(red_ffn, dya)
    reduce_c(red_down, red_ffn["token"])

    gw_o = _mm(
        "mm_gw_o", (d // twr, d // twn, 1), [m_act, dx2b],
        [pl.BlockSpec((s, twr), lambda i, j, k: (0, i)), pl.BlockSpec((s, twn), lambda i, j, k: (0, j))],
        [jax.ShapeDtypeStruct((d, d), BF16)], [pl.BlockSpec((twr, twn), lambda i, j, k: (i, j))],
        [(0, 1, TN, 0)], 1, (twr, twn), 1, ep_store,
        _mm_vmem([((s, twr), BF16, 3), ((s, twn), BF16, 2), ((twr, twn), F32, 3)]),
        after=(red_down["token"],),
    )[0]
    red_o = reduce_a("w_o", [W_O], [gw_o])
    finish(red_down, red_o["token"])

    tb = _tile(w_sgu, (1024, 512))
    b_out = pl.BlockSpec((tm, tb), lambda i, j, k: (i, j))

    da, datt = _mm(
        "mm_dbranches", (s // tm, w_sgu // tb, 1), [dya, g_a, dyb, g_b],
        [pl.BlockSpec((tm, d), lambda i, j, k: (i, 0)), pl.BlockSpec((tb, d), lambda i, j, k: (j, 0)),
         pl.BlockSpec((tm, d), lambda i, j, k: (i, 0)), pl.BlockSpec((tb, d), lambda i, j, k: (j, 0))],
        [jax.ShapeDtypeStruct((s, w_sgu), BF16), jax.ShapeDtypeStruct((s, w_att), BF16)], [b_out, b_out],
        [(0, 1, NT, 0), (2, 3, NT, 1)], 2, (tm, tb), 1, ep_store,
        _mm_vmem([((tm, d), BF16, 4), ((tb, d), BF16, 4), ((tm, tb), F32, 6)]),        after=(red_o["token"],),
    )
    reduce_b(red_o, da)

    wb_tile = pl.BlockSpec((tb, twn), lambda i, j, k: (i, j))
    gw_a, gw_b = _mm(
        "mm_gw_branches", (w_sgu // tb, d // twn, 1), [a_act, dya, att, dyb],
        [pl.BlockSpec((s, tb), lambda i, j, k: (0, i)), pl.BlockSpec((s, twn), lambda i, j, k: (0, j)),
         pl.BlockSpec((s, tb), lambda i, j, k: (0, i)), pl.BlockSpec((s, twn), lambda i, j, k: (0, j))],
        [jax.ShapeDtypeStruct((w_sgu, d), BF16), jax.ShapeDtypeStruct((w_att, d), BF16)], [wb_tile, wb_tile],
        [(0, 1, TN, 0), (2, 3, TN, 1)], 2, (tb, twn), 1, ep_store,
        _mm_vmem([((s, tb), BF16, 5), ((s, twn), BF16, 4), ((tb, twn), F32, 6)]),
        after=(red_o["token"],),
    )
    red_mix = reduce_a("mix", [W_A, W_B], [gw_a, gw_b])

    dz, dws, dbs, dgain = _sgu_bwd(z, da, sgu_v_gain, ws_b, wst_b, b_col, w_sgu, dz, after=(red_mix["token"],))
    dz, dk_pad, dv_pad, dbias_tab, dsink = _attn_bwd(sink, z, k_pad, v_pad, bias_tab, datt, dz, grp, q_blk0)
    dz = _dkv_to_dz(dk_pad, dv_pad, dz, off_k // (2 * w_kv))
    drel = _relbias_bwd(dbias_tab, bucket)
    reduce_b(red_mix, dz)

    small_w = [norm_mix, sgu_v_gain, sgu_w_s, sgu_b_s, attn_sink, rel_bias, norm_ffn, norm_final]
    small_m = [m_norm_mix, m_sgu_v_gain, m_sgu_w_s, m_sgu_b_s, m_attn_sink, m_rel_bias, m_norm_ffn, m_norm_final]
    small_v = [v_norm_mix, v_sgu_v_gain, v_sgu_w_s, v_sgu_b_s, v_attn_sink, v_rel_bias, v_norm_ffn, v_norm_final]
    small_shapes = [w.shape for w in small_w]
    early = [dgain, dws, dbs, dsink[:, 0], drel[:, :REL_BUCKETS].T, dg_ffn, dg_final]
    p_early = _pack([g.reshape(shp) for g, shp in zip(early, small_shapes[1:])] + [loss_part[0, :1]])
    land = jnp.zeros((2 * N_CHIPS,) + p_early.shape, F32)
    sm_send, sm_recv, (p_early, land) = _split_start("small_send", [p_early, land], 2 * N_CHIPS - 1, _small_exchange_copies(False))

    tzn = _tile(n_in, (768, 640, 512))
    gw_in = _mm(
        "mm_gw_in", (d // twr, n_in // tzn, 1), [h1, dz],
        [pl.BlockSpec((s, twr), lambda i, j, k: (0, i)), pl.BlockSpec((s, tzn), lambda i, j, k: (0, j))],
        [jax.ShapeDtypeStruct((d, n_in), BF16)], [pl.BlockSpec((twr, tzn), lambda i, j, k: (i, j))],
        [(0, 1, TN, 0)], 1, (twr, tzn), 1, ep_store,
        _mm_vmem([((s, twr), BF16, 3), ((s, tzn), BF16, 2), ((twr, tzn), F32, 3)]),
        after=(red_mix["token"], p_early),
    )[0]
    red_in = reduce_a("w_in", [W_IN], [gw_in])

    dh1 = _mm(
        "mm_dh1", (s // tm, d // tn2, 1), [dz, g_in],
        [pl.BlockSpec((tm, n_in), lambda i, j, k: (i, 0)), pl.BlockSpec((tn2, n_in), lambda i, j, k: (j, 0))],
        [jax.ShapeDtypeStruct((s, d), F32)], [pl.BlockSpec((tm, tn2), lambda i, j, k: (i, j))],
        [(0, 1, NT, 0)], 1, (tm, tn2), 1, ep_store,
        _mm_vmem([((tm, n_in), BF16, 2), ((tn2, n_in), BF16, 2), ((tm, tn2), F32, 5)]),
        after=(red_in["token"],),
    )[0]

    reduce_b(red_in, dh1)
    grad_x, _, dg_mix = _rms_bwd("rms_mix_bwd", x2d, norm_mix, dh1, dx2, after=(red_in["token"],))

    p_mix = _pack([dg_mix.reshape(small_shapes[0])])
    land_mix = jnp.zeros((2 * N_CHIPS,) + p_mix.shape, F32)
    mx_send, mx_recv, (p_mix, land_mix) = _split_start("mix_send", [p_mix, land_mix], 2 * N_CHIPS - 1, _small_exchange_copies(False))

    after, prev = p_mix, None
    for st in (red_ffn, red_o, red_mix, red_in):
        reduce_c(st, after)
        after = st["token"] if prev is None else finish(prev, st["token"])
        prev = st

    p_early, land = _split_wait("small_wait", [p_early, land], sm_send, sm_recv, _small_exchange_copies(True), after)
    p_mix, land_mix = _split_wait("mix_wait", [p_mix, land_mix], mx_send, mx_recv, _small_exchange_copies(True), p_early)
    me_idx = 2 * q_idx + c_idx
    packed_g = jnp.concatenate([_small_sum("mix_sum", me_idx, p_mix, land_mix), _small_sum("small_sum", me_idx, p_early, land)], axis=0)
    g_small = _unpack(packed_g, small_shapes + [(1,)])
    loss = g_small[-1].reshape(())
    g_small = g_small[:-1]
    zero1 = jnp.zeros((1,), F32)
    pw, pg, pm, pv = _pack(small_w + [zero1]), _pack(g_small + [zero1]), _pack(small_m + [zero1]), _pack(small_v + [zero1])
    small_upd = _adamw("adamw_small", pw, pg, pm, pv)
    d_small, nm_small, nv_small = [_unpack(a, small_shapes) for a in small_upd[:3]]
    finish(prev, small_upd[0])

    small_names = ["norm_mix", "sgu_v_gain", "sgu_w_s", "sgu_b_s", "attn_sink", "rel_bias", "norm_ffn", "norm_final"]
    table = {}
    for i, n in enumerate(names):
        table[n] = (grads_big[i][None], upd[i][0][None], upd[i][1][None], upd[i][2][None])
    for i, n in enumerate(small_names):
        table[n] = (g_small[i], d_small[i], nm_small[i], nv_small[i])
    order = ["w_in", "norm_mix", "sgu_v_gain", "sgu_w_s", "sgu_b_s", "w_a", "attn_sink", "rel_bias", "w_b", "w_o", "norm_ffn",
             "w_gate", "w_up", "w_down", "norm_final"]
    outs = [loss, grad_x.reshape(1, s, d)]
    for part in range(4):
        outs += [table[n][part] for n in order]
    return tuple(outs)
```

```python
import math

import jax
import jax.numpy as jnp
import numpy as np
from jax import lax
from jax.experimental import pallas as pl
from jax.experimental.pallas import tpu as pltpu
from jax.experimental.pallas import tpu_sc as plsc

F32 = jnp.float32
BF16 = jnp.bfloat16
I32 = jnp.int32
MESH = pl.DeviceIdType.MESH

EPS = 1e-6
NEG = -1e30
BLK = 128
HEAD_DIM = 128
N_KV_HEADS = 2
REL_BUCKETS = 32
REL_MAX_DIST = 128
N_CHIPS = 4
ADAM_LR, ADAM_B1, ADAM_B2, ADAM_EPS, ADAM_WD, ADAM_STEP = 0.001, 0.9, 0.999, 1e-08, 0.01, 10

LANES = 128
MXU_COLS = 256
VMEM_CAP = 60 * 1024 * 1024

NN = (((1,), (0,)), ((), ()))
NT = (((1,), (1,)), ((), ()))
TN = (((0,), (0,)), ((), ()))
ANY = pl.BlockSpec(memory_space=pl.ANY)
HBM_SPEC = pl.BlockSpec(memory_space=pltpu.HBM)
SEM_SPEC = pl.BlockSpec(memory_space=pltpu.SEMAPHORE)
EFFECT = pltpu.SideEffectType.DATAFLOW_SIDE_EFFECTING


def _tile(n, cands):
    for t in cands:
        if n % t == 0:
            return t
    return n


PIN_BYTES = 64 * 1024


def _pin_hbm(a):
    big = hasattr(a, "dtype") and jnp.issubdtype(a.dtype, jnp.floating) and _nbytes(a.shape, a.dtype) >= PIN_BYTES
    return pltpu.with_memory_space_constraint(a, pltpu.HBM) if big else a


def _pallas(body, *, out_shape, **kw):
    def pin(o):
        big = isinstance(o, jax.ShapeDtypeStruct) and jnp.issubdtype(o.dtype, jnp.floating) and _nbytes(o.shape, o.dtype) >= PIN_BYTES
        return pltpu.HBM(o.shape, o.dtype) if big else o

    shapes = type(out_shape)(pin(o) for o in out_shape) if isinstance(out_shape, (list, tuple)) else pin(out_shape)
    call = pl.pallas_call(body, out_shape=shapes, **kw)
    return lambda *args: call(*[_pin_hbm(a) for a in args])


def _params(vmem_bytes=None, **kw):
    if vmem_bytes is not None:
        kw["vmem_limit_bytes"] = int(min(max(vmem_bytes, 32 * 1024 * 1024), VMEM_CAP))
    return pltpu.CompilerParams(**kw)


def _nbytes(shape, dtype):
    return int(np.prod(shape)) * jnp.dtype(dtype).itemsize


def _sigmoid(x):
    return 1.0 / (1.0 + jnp.exp(-x))


_GC = 0.7978845608028654
_GA = 0.044715


def _gelu(x):
    return 0.5 * x * (1.0 + jnp.tanh(_GC * (x + _GA * (x * x * x))))


def _gelu_grad(x):
    t = jnp.tanh(_GC * (x + _GA * (x * x * x)))
    return 0.5 * (1.0 + t) + 0.5 * x * (1.0 - t * t) * (_GC * (1.0 + 3.0 * _GA * (x * x)))


def _bf(v):
    return v if v.dtype == BF16 else v.astype(BF16)


def _mm(name, grid, ins, in_specs, out_shape, out_specs, pairs, n_acc, tile, nk, epilogue, vmem_bytes, after=(), col_chunks=1):
    assert nk == 1 and tile[1] % col_chunks == 0
    n_in, n_out = len(ins) + len(after), len(out_shape)
    width = tile[1] // col_chunks

    def body(*refs):
        in_refs, out_refs = refs[:n_in], refs[n_in : n_in + n_out]
        for ch in range(col_chunks):
            cs = slice(ch * width, (ch + 1) * width) if col_chunks > 1 else slice(None)
            vals = [None] * n_acc
            for a_i, b_i, dn, acc_i in pairs:
                rhs = in_refs[b_i][cs, :] if dn == NT else in_refs[b_i][:, cs]
                d = lax.dot_general(_bf(in_refs[a_i][...]), _bf(rhs), dn, preferred_element_type=F32)
                vals[acc_i] = d if vals[acc_i] is None else vals[acc_i] + d
            epilogue(in_refs, vals, out_refs, cs)

    return _pallas(
        body,
        name=name,
        grid=grid,
        in_specs=list(in_specs) + [ANY] * len(after),
        out_specs=out_specs,
        out_shape=out_shape,
        compiler_params=_params(vmem_bytes),
    )(*ins, *after)


def _put(ref, cs, v):
    ref[:, cs] = v.astype(ref.dtype)


def _mm_vmem(tiles):
    return sum(_nbytes(s, d) * c for s, d, c in tiles) + 4 * 1024 * 1024


def _rows8(v):
    r, d = v.shape
    return v.reshape(r // 8, 8, d).sum(axis=0)


def _rms_fwd(name, x, g, after=()):
    s, d = x.shape
    tm = _tile(s, (256, 128))

    def body(x_ref, g_ref, *rest):
        h_ref = rest[-1]
        xv = x_ref[...]
        r = lax.rsqrt(jnp.mean(xv * xv, axis=-1, keepdims=True) + EPS)
        h_ref[...] = ((xv * r) * g_ref[...]).astype(BF16)

    return _pallas(
        body,
        name=name,
        grid=(s // tm,),
        in_specs=[pl.BlockSpec((tm, d), lambda i: (i, 0)), pl.BlockSpec((1, d), lambda i: (0, 0))] + [ANY] * len(after),
        out_specs=pl.BlockSpec((tm, d), lambda i: (i, 0)),
        out_shape=jax.ShapeDtypeStruct((s, d), BF16),
    )(x, g, *after)


def _rms_bwd(name, x, g, dh, dres, after=()):
    s, d = x.shape
    tm = _tile(s, (256, 128))
    n = s // tm
    n_after = len(after)

    def body(x_ref, g_ref, dh_ref, dres_ref, *rest):
        dx_ref, dxb_ref, dg_ref, acc_ref = rest[n_after:]
        i = pl.program_id(0)
        xv = x_ref[...]
        r = lax.rsqrt(jnp.mean(xv * xv, axis=-1, keepdims=True) + EPS)
        xh = xv * r
        dhv = dh_ref[...]
        dxh = dhv * g_ref[...]
        dx = r * (dxh - xh * jnp.mean(dxh * xh, axis=-1, keepdims=True)) + dres_ref[...]
        dx_ref[...] = dx
        dxb_ref[...] = dx.astype(BF16)
        part = _rows8(dhv * xh)

        @pl.when(i == 0)
        def _():
            acc_ref[...] = part

        @pl.when(i > 0)
        def _():
            acc_ref[...] += part

        @pl.when(i == n - 1)
        def _():
            dg_ref[...] = jnp.sum(acc_ref[...], axis=0, keepdims=True)

    row = pl.BlockSpec((tm, d), lambda i: (i, 0))
    vec = pl.BlockSpec((1, d), lambda i: (0, 0))
    return _pallas(
        body,
        name=name,
        grid=(n,),
        in_specs=[row, vec, row, row] + [ANY] * n_after,
        out_specs=[row, row, vec],
        out_shape=[jax.ShapeDtypeStruct((s, d), F32), jax.ShapeDtypeStruct((s, d), BF16), jax.ShapeDtypeStruct((1, d), F32)],
        scratch_shapes=[pltpu.VMEM((8, d), F32)],
    )(x, g, dh, dres, *after)


def _head(x3, g, target):
    s, d = x3.shape
    tm = _tile(s, (256, 128))
    n = s // tm

    def body(x_ref, g_ref, t_ref, dx_ref, dxb_ref, dg_ref, loss_ref, acc_g, acc_l):
        i = pl.program_id(0)
        xv = x_ref[...]
        gv = g_ref[...]
        r = lax.rsqrt(jnp.mean(xv * xv, axis=-1, keepdims=True) + EPS)
        xh = xv * r
        e = xh * gv - t_ref[...]
        dy = e * (1.0 / d)
        dxh = dy * gv
        dx = r * (dxh - xh * jnp.mean(dxh * xh, axis=-1, keepdims=True))
        dx_ref[...] = dx
        dxb_ref[...] = dx.astype(BF16)
        pg = _rows8(dy * xh)
        plo = _rows8(e * e)

        @pl.when(i == 0)
        def _():
            acc_g[...] = pg
            acc_l[...] = plo

        @pl.when(i > 0)
        def _():
            acc_g[...] += pg
            acc_l[...] += plo

        @pl.when(i == n - 1)
        def _():
            dg_ref[...] = jnp.sum(acc_g[...], axis=0, keepdims=True)
            loss_ref[...] = jnp.full((1, LANES), (0.5 / d) * jnp.sum(acc_l[...]), F32)

    row = pl.BlockSpec((tm, d), lambda i: (i, 0))
    vec = pl.BlockSpec((1, d), lambda i: (0, 0))
    return _pallas(
        body,
        name="head",
        grid=(n,),
        in_specs=[row, vec, row],
        out_specs=[row, row, vec, pl.BlockSpec((1, LANES), lambda i: (0, 0))],
        out_shape=[
            jax.ShapeDtypeStruct((s, d), F32),
            jax.ShapeDtypeStruct((s, d), BF16),
            jax.ShapeDtypeStruct((1, d), F32),
            jax.ShapeDtypeStruct((1, LANES), F32),
        ],
        scratch_shapes=[pltpu.VMEM((8, d), F32), pltpu.VMEM((8, d), F32)],
    )(x3, g, target)


def _sgu_fwd(z, gain, ws_b, b_col, w_sgu):
    s = z.shape[0]
    groups = ws_b.shape[0]

    def body(zu_ref, zv_ref, gain_ref, ws_ref, b_ref, a_ref):
        vv = _gelu(zv_ref[...].astype(F32))
        r = lax.rsqrt(jnp.mean(vv * vv, axis=-1, keepdims=True) + EPS)
        vn = ((vv * r) * gain_ref[...]).astype(BF16)
        u = _gelu(zu_ref[...].astype(F32))
        for g in range(groups):
            sl = slice(g * BLK, (g + 1) * BLK)
            mixed = jnp.dot(ws_ref[g], vn[:, sl], preferred_element_type=F32) + b_ref[g]
            a_ref[:, sl] = (u[:, sl] * mixed).astype(BF16)

    return _pallas(
        body,
        name="sgu_fwd",
        grid=(s // BLK,),
        in_specs=[
            pl.BlockSpec((BLK, w_sgu), lambda c: (c, 0)),
            pl.BlockSpec((BLK, w_sgu), lambda c: (c, 1)),
            pl.BlockSpec((1, w_sgu), lambda c: (0, 0)),
            pl.BlockSpec((groups, BLK, BLK), lambda c: (0, 0, 0)),
            pl.BlockSpec((groups, BLK, 1), lambda c: (0, 0, 0)),
        ],
        out_specs=pl.BlockSpec((BLK, w_sgu), lambda c: (c, 0)),
        out_shape=jax.ShapeDtypeStruct((s, w_sgu), BF16),
    )(z, z, gain, ws_b, b_col)


def _sgu_bwd(z, da, gain, ws_b, wst_b, b_col, w_sgu, dz, after=()):
    s = z.shape[0]
    groups = ws_b.shape[0]
    n = s // BLK
    n_skip = 1 + len(after)

    def body(zu_ref, zv_ref, da_ref, gain_ref, ws_ref, wst_ref, b_ref, *rest):
        dz_ref, dws_ref, dbs_ref, dgain_ref, acc_gain = rest[n_skip:]
        c = pl.program_id(0)
        zu = zu_ref[...].astype(F32)
        zv = zv_ref[...].astype(F32)
        gain_v = gain_ref[...]
        vv = _gelu(zv)
        r = lax.rsqrt(jnp.mean(vv * vv, axis=-1, keepdims=True) + EPS)
        xh = vv * r
        vn = (xh * gain_v).astype(BF16)
        u = _gelu(zu)
        dav = da_ref[...].astype(F32)
        dmix = dav * u
        dmix_b = dmix.astype(BF16)
        dvn_parts = []
        for g in range(groups):
            sl = slice(g * BLK, (g + 1) * BLK)
            mixed = jnp.dot(ws_ref[g], vn[:, sl], preferred_element_type=F32) + b_ref[g]
            dz_ref[:, sl] = (dav[:, sl] * mixed * _gelu_grad(zu[:, sl])).astype(BF16)
            dvn_parts.append(jnp.dot(wst_ref[g], dmix_b[:, sl], preferred_element_type=F32))
            dws_g = lax.dot_general(dmix_b[:, sl], vn[:, sl], NT, preferred_element_type=F32)
            dbs_g = jnp.sum(dmix[:, sl], axis=1, keepdims=True)

            @pl.when(c == 0)
            def _():
                dws_ref[g] = dws_g
                dbs_ref[g] = dbs_g

            @pl.when(c > 0)
            def _():
                dws_ref[g] += dws_g
                dbs_ref[g] += dbs_g

        dvn = jnp.concatenate(dvn_parts, axis=1)
        dxh = dvn * gain_v
        dvv = r * (dxh - xh * jnp.mean(dxh * xh, axis=-1, keepdims=True))
        dz_ref[:, w_sgu:] = (dvv * _gelu_grad(zv)).astype(BF16)
        pg = _rows8(dvn * xh)

        @pl.when(c == 0)
        def _():
            acc_gain[...] = pg

        @pl.when(c > 0)
        def _():
            acc_gain[...] += pg

        @pl.when(c == n - 1)
        def _():
            dgain_ref[...] = jnp.sum(acc_gain[...], axis=0, keepdims=True)

    full3 = pl.BlockSpec((groups, BLK, BLK), lambda c: (0, 0, 0))
    col3 = pl.BlockSpec((groups, BLK, 1), lambda c: (0, 0, 0))
    vec = pl.BlockSpec((1, w_sgu), lambda c: (0, 0))
    return _pallas(
        body,
        name="sgu_bwd",
        grid=(n,),
        in_specs=[
            pl.BlockSpec((BLK, w_sgu), lambda c: (c, 0)),
            pl.BlockSpec((BLK, w_sgu), lambda c: (c, 1)),
            pl.BlockSpec((BLK, w_sgu), lambda c: (c, 0)),
            vec,
            full3,
            full3,
            col3,
            ANY,
        ]
        + [ANY] * len(after),
        out_specs=[pl.BlockSpec((BLK, 2 * w_sgu), lambda c: (c, 0)), full3, col3, vec],
        out_shape=[
            jax.ShapeDtypeStruct(dz.shape, BF16),
            jax.ShapeDtypeStruct((groups, BLK, BLK), F32),
            jax.ShapeDtypeStruct((groups, BLK, 1), F32),
            jax.ShapeDtypeStruct((1, w_sgu), F32),
        ],
        scratch_shapes=[pltpu.VMEM((8, w_sgu), F32)],
        input_output_aliases={7: 0},
    )(z, z, da, gain, ws_b, wst_b, b_col, dz, *after)


def _attn_softmax(sink_ref, q_ref, k_ref, v_ref, bias_ref, s_len, grp):
    kv = pl.program_id(0)
    n = pl.program_id(1)
    start = pl.multiple_of(n * BLK, BLK)
    kb = k_ref[pl.ds(start, 3 * BLK), :]
    vb = v_ref[pl.ds(start, 3 * BLK), :]
    qv = q_ref[...]
    qs = jnp.concatenate([qv[:, g * HEAD_DIM : (g + 1) * HEAD_DIM] for g in range(grp)], axis=0).astype(BF16)
    sc = lax.dot_general(qs, kb, NT, preferred_element_type=F32) * (HEAD_DIM**-0.5)
    sc = sc + bias_ref[...].reshape(grp * BLK, 3 * BLK)
    kpos = start + lax.broadcasted_iota(I32, (1, 3 * BLK), 1) - BLK
    sc = jnp.where((kpos >= 0) & (kpos < s_len), sc, NEG)
    sink = jnp.concatenate([jnp.full((BLK, 1), sink_ref[kv * grp + g], F32) for g in range(grp)], axis=0)
    m = jnp.maximum(jnp.max(sc, axis=-1, keepdims=True), sink)
    p = jnp.exp(sc - m)
    esink = jnp.exp(sink - m)
    den = jnp.sum(p, axis=-1, keepdims=True) + esink
    return start, qs, kb, vb, p / den, esink / den


def _attn_specs(s, grp, q_blk0):
    qw = grp * HEAD_DIM
    return [
        pl.BlockSpec(memory_space=pltpu.SMEM),
        pl.BlockSpec((BLK, qw), lambda kv, n: (n, q_blk0 + kv)),
        pl.BlockSpec((s + 2 * BLK, HEAD_DIM), lambda kv, n: (0, kv)),
        pl.BlockSpec((s + 2 * BLK, HEAD_DIM), lambda kv, n: (0, kv)),
        pl.BlockSpec((grp, BLK, 3 * BLK), lambda kv, n: (kv, 0, 0)),
    ]


def _attn_fwd(sink, z, k_pad, v_pad, bias_tab, grp, q_blk0):
    s = z.shape[0]
    qw = grp * HEAD_DIM

    def body(sink_ref, q_ref, k_ref, v_ref, bias_ref, o_ref):
        _, _, _, vb, pn, _ = _attn_softmax(sink_ref, q_ref, k_ref, v_ref, bias_ref, s, grp)
        o = jnp.dot(pn.astype(BF16), vb, preferred_element_type=F32)
        for g in range(grp):
            o_ref[:, g * HEAD_DIM : (g + 1) * HEAD_DIM] = o[g * BLK : (g + 1) * BLK].astype(BF16)

    return _pallas(
        body,
        name="attn_fwd",
        grid=(N_KV_HEADS, s // BLK),
        in_specs=_attn_specs(s, grp, q_blk0),
        out_specs=pl.BlockSpec((BLK, qw), lambda kv, n: (n, kv)),
        out_shape=jax.ShapeDtypeStruct((s, N_KV_HEADS * qw), BF16),
    )(sink, z, k_pad, v_pad, bias_tab)


def _attn_bwd(sink, z, k_pad, v_pad, bias_tab, dout, dz, grp, q_blk0):
    s = z.shape[0]
    qw = grp * HEAD_DIM
    nb = s // BLK
    heads = N_KV_HEADS * grp

    def body(sink_ref, q_ref, k_ref, v_ref, bias_ref, do_ref, dz_in, dq_ref, dk_ref, dv_ref, dbias_ref, dsink_ref, dk_acc, dv_acc):
        del dz_in
        kv = pl.program_id(0)
        n = pl.program_id(1)
        start, qs, kb, vb, pn, psink = _attn_softmax(sink_ref, q_ref, k_ref, v_ref, bias_ref, s, grp)
        dov = do_ref[...]
        dos = jnp.concatenate([dov[:, g * HEAD_DIM : (g + 1) * HEAD_DIM] for g in range(grp)], axis=0)
        dp = lax.dot_general(dos, vb, NT, preferred_element_type=F32)
        dvb = lax.dot_general(pn.astype(BF16), dos, TN, preferred_element_type=F32)
        delta = jnp.sum(pn * dp, axis=-1, keepdims=True)
        ds = pn * (dp - delta)
        dsb = (ds * (HEAD_DIM**-0.5)).astype(BF16)
        dq = jnp.dot(dsb, kb, preferred_element_type=F32)
        dkb = lax.dot_general(dsb, qs, TN, preferred_element_type=F32)
        for g in range(grp):
            dq_ref[:, g * HEAD_DIM : (g + 1) * HEAD_DIM] = dq[g * BLK : (g + 1) * BLK].astype(BF16)

        @pl.when(n == 0)
        def _():
            dk_acc[...] = jnp.zeros_like(dk_acc)
            dv_acc[...] = jnp.zeros_like(dv_acc)
            dbias_ref[...] = jnp.zeros_like(dbias_ref)

        @pl.when((n == 0) & (kv == 0))
        def _():
            dsink_ref[...] = jnp.zeros_like(dsink_ref)

        dk_acc[pl.ds(start, 3 * BLK), :] += dkb
        dv_acc[pl.ds(start, 3 * BLK), :] += dvb
        dbias_ref[...] += ds.reshape(grp, BLK, 3 * BLK)
        row = lax.broadcasted_iota(I32, (heads, LANES), 0)
        sd = psink * delta
        upd = jnp.zeros((heads, LANES), F32)
        for g in range(grp):
            upd = jnp.where(row == kv * grp + g, -jnp.sum(sd[g * BLK : (g + 1) * BLK]), upd)
        dsink_ref[...] += upd

        @pl.when(n == nb - 1)
        def _():
            dk_ref[...] = dk_acc[...]
            dv_ref[...] = dv_acc[...]

    pad_spec = pl.BlockSpec((s + 2 * BLK, HEAD_DIM), lambda kv, n: (0, kv))
    kvw = N_KV_HEADS * HEAD_DIM
    return _pallas(
        body,
        name="attn_bwd",
        grid=(N_KV_HEADS, nb),
        in_specs=_attn_specs(s, grp, q_blk0) + [pl.BlockSpec((BLK, qw), lambda kv, n: (n, kv)), ANY],
        out_specs=[
            pl.BlockSpec((BLK, qw), lambda kv, n: (n, q_blk0 + kv)),
            pad_spec,
            pad_spec,
            pl.BlockSpec((grp, BLK, 3 * BLK), lambda kv, n: (kv, 0, 0)),
            pl.BlockSpec((heads, LANES), lambda kv, n: (0, 0)),
        ],
        out_shape=[
            jax.ShapeDtypeStruct(dz.shape, BF16),
            jax.ShapeDtypeStruct((s + 2 * BLK, kvw), F32),
            jax.ShapeDtypeStruct((s + 2 * BLK, kvw), F32),
            jax.ShapeDtypeStruct((heads, BLK, 3 * BLK), F32),
            jax.ShapeDtypeStruct((heads, LANES), F32),
        ],
        scratch_shapes=[pltpu.VMEM((s + 2 * BLK, HEAD_DIM), F32), pltpu.VMEM((s + 2 * BLK, HEAD_DIM), F32)],
        input_output_aliases={6: 0},
    )(sink, z, k_pad, v_pad, bias_tab, dout, dz)


def _dkv_to_dz(dk_pad, dv_pad, dz, blk_idx):
    s = dz.shape[0]
    kvw = dk_pad.shape[1]

    def body(dk_ref, dv_ref, dz_in, out_ref):
        del dz_in
        out_ref[:, :kvw] = dk_ref[...].astype(BF16)
        out_ref[:, kvw:] = dv_ref[...].astype(BF16)

    src = pl.BlockSpec((BLK, kvw), lambda i: (i + 1, 0))
    return _pallas(
        body,
        name="dkv_to_dz",
        grid=(s // BLK,),
        in_specs=[src, src, ANY],
        out_specs=pl.BlockSpec((BLK, 2 * kvw), lambda i: (i, blk_idx)),
        out_shape=jax.ShapeDtypeStruct(dz.shape, BF16),
        input_output_aliases={2: 0},
    )(dk_pad, dv_pad, dz)


def _relbias_bwd(dbias_tab, bucket):
    heads = dbias_tab.shape[0]

    def body(dt_ref, bk_ref, out_ref):
        lane = lax.broadcasted_iota(I32, (1, LANES), 1)
        bk = bk_ref[...]
        rows = []
        for h in range(heads):
            dt = dt_ref[h]
            acc = jnp.zeros((1, LANES), F32)
            for b in range(REL_BUCKETS):
                acc = jnp.where(lane == b, jnp.sum(jnp.where(bk == b, dt, 0.0)), acc)
            rows.append(acc)
        out_ref[...] = jnp.concatenate(rows, axis=0)

    return _pallas(body, name="relbias_bwd", out_shape=jax.ShapeDtypeStruct((heads, LANES), F32))(dbias_tab, bucket)


def _t5_bucket(rel):
    nb = REL_BUCKETS // 2
    ret = jnp.where(rel > 0, nb, 0)
    n = jnp.abs(rel)
    max_exact = nb // 2
    nf = jnp.maximum(n, 1).astype(F32)
    large = max_exact + (jnp.log(nf / max_exact) / math.log(REL_MAX_DIST / max_exact) * (nb - max_exact)).astype(I32)
    large = jnp.minimum(large, nb - 1)
    return ret + jnp.where(n < max_exact, n, large)


def _band_tables(rel_bias):
    qi = jnp.arange(BLK)[:, None]
    kj = jnp.arange(3 * BLK)[None, :]
    rel = kj - BLK - qi
    bucket = _t5_bucket(rel).astype(I32)
    heads = rel_bias.shape[1]
    masked = jnp.where(jnp.abs(rel) <= BLK, bucket, -1)

    def body(rb_ref, bk_ref, out_ref):
        bk = bk_ref[...]
        for h in range(heads):
            tab = jnp.full(bk.shape, NEG, F32)
            for b in range(REL_BUCKETS):
                tab = jnp.where(bk == b, rb_ref[b, h], tab)
            out_ref[h] = tab

    bias_tab = _pallas(
        body,
        name="bias_table",
        in_specs=[pl.BlockSpec(memory_space=pltpu.SMEM), pl.BlockSpec(memory_space=pltpu.VMEM)],
        out_specs=pl.BlockSpec(memory_space=pltpu.VMEM),
        out_shape=jax.ShapeDtypeStruct((heads, BLK, 3 * BLK), F32),
    )(rel_bias.astype(F32), masked)
    return bias_tab, bucket


EW_BLOCK_ELEMS = 512 * 1024


def _ew_tiles(shape, elems=EW_BLOCK_ELEMS // 2):
    r, c = shape
    tn = c if c <= 2048 else _tile(c, (2048, 1920, 1536, 1408, 1024, 512))
    tm = _tile(r, [t for t in (1024, 512, 256, 128, 64, 32, 16, 8) if t * tn <= elems] or [8])
    return tm, tn


def _cast_into_full(name, qidx, w, kind, after=()):
    r, c = w.shape
    tm, tn = _ew_tiles(w.shape, EW_BLOCK_ELEMS)
    nbi, nbj = r // tm, c // tn
    if kind == "col":
        full, out_spec = (r, c * N_CHIPS), pl.BlockSpec((tm, tn), lambda i, j, q: (i, q[0] * nbj + j))
    else:
        full, out_spec = (r * N_CHIPS, c), pl.BlockSpec((tm, tn), lambda i, j, q: (q[0] * nbi + i, j))

    def body(q_ref, w_ref, *rest):
        del q_ref
        rest[-1][...] = w_ref[...].astype(BF16)

    return _pallas(
        body,
        name=name,
        grid_spec=pltpu.PrefetchScalarGridSpec(
            num_scalar_prefetch=1,
            grid=(nbi, nbj),
            in_specs=[pl.BlockSpec((tm, tn), lambda i, j, q: (i, j))] + [ANY] * len(after),
            out_specs=out_spec,
        ),
        out_shape=jax.ShapeDtypeStruct(full, BF16),
    )(qidx, w, *after)


def _adamw(name, w, g, m, v, after=()):
    tm, tn = _ew_tiles(w.shape, EW_BLOCK_ELEMS)
    if _nbytes(w.shape, F32) <= 1024 * 1024:
        tm, tn = w.shape
    spec = pl.BlockSpec((tm, tn), lambda i, j: (i, j))
    n_after = len(after)

    def body(w_ref, g_ref, m_ref, v_ref, *rest):
        d_ref, nm_ref, nv_ref, g_out_ref = rest[n_after:]
        gv = g_ref[...]
        g_out_ref[...] = gv
        nm = ADAM_B1 * m_ref[...] + (1.0 - ADAM_B1) * gv
        nv = ADAM_B2 * v_ref[...] + (1.0 - ADAM_B2) * (gv * gv)
        m_hat = nm / (1.0 - ADAM_B1**ADAM_STEP)
        v_hat = nv / (1.0 - ADAM_B2**ADAM_STEP)
        d_ref[...] = -ADAM_LR * (m_hat / (jnp.sqrt(v_hat) + ADAM_EPS) + ADAM_WD * w_ref[...])
        nm_ref[...] = nm
        nv_ref[...] = nv

    out = jax.ShapeDtypeStruct(w.shape, F32)
    return _pallas(
        body, name=name, grid=(w.shape[0] // tm, w.shape[1] // tn), in_specs=[spec] * 4 + [ANY] * n_after,
        out_specs=[spec] * 4, out_shape=[out, out, out, out],
        compiler_params=_params(_mm_vmem([((tm, tn), F32, 24)])),
    )(w, g, m, v, *after)


SC_TILES = 32
SC_LANES = 16
SC_ROWS = 8


def _sc_adamw(name, w, g, m, v):
    r, c = w.shape
    row_tiles = next(t for t in (32, 16, 8) if r % (t * SC_ROWS) == 0 and c % ((SC_TILES // t) * LANES) == 0)
    col_tiles = SC_TILES // row_tiles
    rows, cw = r // row_tiles, c // col_tiles
    c1, c2 = 1.0 - ADAM_B1**ADAM_STEP, 1.0 - ADAM_B2**ADAM_STEP

    def body(w_hbm, g_hbm, m_hbm, v_hbm, d_hbm, nm_hbm, nv_hbm, go_hbm, wb, gb, mb, vb, sem):
        tile = lax.axis_index("subcore") * 2 + lax.axis_index("score")
        r0 = (tile // col_tiles) * rows
        c0 = (tile % col_tiles) * cw

        @pl.loop(0, rows, step=SC_ROWS)
        def _(rr):
            win = (pl.ds(r0 + rr, SC_ROWS), pl.ds(c0, cw))
            loads = [pltpu.make_async_copy(src.at[win], dst, sem) for src, dst in
                     ((w_hbm, wb), (g_hbm, gb), (m_hbm, mb), (v_hbm, vb))]
            for cp in loads:
                cp.start()
            for cp in loads:
                cp.wait()
            for row in range(SC_ROWS):

                @plsc.parallel_loop(0, cw, step=SC_LANES, unroll=4)
                def _(i):
                    sl = (row, pl.ds(i, SC_LANES))
                    gv = gb[sl]
                    nm = ADAM_B1 * mb[sl] + (1.0 - ADAM_B1) * gv
                    nv = ADAM_B2 * vb[sl] + (1.0 - ADAM_B2) * (gv * gv)
                    wb[sl] = -ADAM_LR * ((nm / c1) / (jnp.sqrt(nv / c2) + ADAM_EPS) + ADAM_WD * wb[sl])
                    mb[sl] = nm
                    vb[sl] = nv

            stores = [pltpu.make_async_copy(src, dst.at[win], sem) for src, dst in
                      ((wb, d_hbm), (mb, nm_hbm), (vb, nv_hbm), (gb, go_hbm))]
            for cp in stores:
                cp.start()
            for cp in stores:
                cp.wait()

    out = jax.ShapeDtypeStruct(w.shape, F32)
    return pl.kernel(
        body,
        name=name,
        out_type=(out, out, out, out),
        mesh=plsc.VectorSubcoreMesh(core_axis_name="score", subcore_axis_name="subcore"),
        scratch_types=[pltpu.VMEM((SC_ROWS, cw), F32)] * 4 + [pltpu.SemaphoreType.DMA],
    )(w, g, m, v)


def _pair_add(name, cidx, g_full, r_sib, kind):
    hr, hc = r_sib.shape
    tm, tn = _ew_tiles((hr, hc), 2 * EW_BLOCK_ELEMS)
    nbi, nbj = hr // tm, hc // tn
    if kind == "col":
        g_spec = pl.BlockSpec((tm, tn), lambda i, j, c: (c[0] * nbi + i, j))
    else:
        g_spec = pl.BlockSpec((tm, tn), lambda i, j, c: (i, c[0] * nbj + j))
    spec = pl.BlockSpec((tm, tn), lambda i, j, c: (i, j))

    def body(c_ref, g_ref, r_ref, o_ref):
        del c_ref
        o_ref[...] = (g_ref[...].astype(F32) + r_ref[...].astype(F32)).astype(BF16)

    return _pallas(
        body,
        name=name,
        grid_spec=pltpu.PrefetchScalarGridSpec(num_scalar_prefetch=1, grid=(nbi, nbj), in_specs=[g_spec, spec], out_specs=spec),
        out_shape=jax.ShapeDtypeStruct((hr, hc), BF16),
        compiler_params=_params(_mm_vmem([((tm, tn), BF16, 6), ((tm, tn), F32, 3)])),
    )(cidx, g_full, r_sib)


def _chip_sum(name, qidx, c_half, r_ici, kind):
    _, pr, pc = r_ici.shape
    tm, tn = _ew_tiles((pr, pc), 2 * EW_BLOCK_ELEMS)
    nbi, nbj = pr // tm, pc // tn
    if kind == "col":
        own_spec = pl.BlockSpec((tm, tn), lambda i, j, q: (i, q[0] * nbj + j))
        full, out_spec = (2 * pr, pc), pl.BlockSpec((tm, tn), lambda i, j, q: (q[1] * nbi + i, j))
    else:
        own_spec = pl.BlockSpec((tm, tn), lambda i, j, q: (q[0] * nbi + i, j))
        full, out_spec = (pr, 2 * pc), pl.BlockSpec((tm, tn), lambda i, j, q: (i, q[1] * nbj + j))

    def body(q_ref, own_ref, r_ref, o_ref):
        q = q_ref[0]
        own = own_ref[...].astype(F32)
        recv = [r_ref[r].astype(F32) for r in range(3)]
        total = None
        for chip in range(N_CHIPS):
            d = chip ^ q
            term = jnp.where(d == 0, own, jnp.where(d == 2, recv[0], jnp.where(d == 1, recv[1], recv[2])))
            total = term if total is None else total + term
        o_ref[...] = total

    return _pallas(
        body,
        name=name,
        grid_spec=pltpu.PrefetchScalarGridSpec(
            num_scalar_prefetch=1,
            grid=(nbi, nbj),
            in_specs=[own_spec, pl.BlockSpec((3, tm, tn), lambda i, j, q: (0, i, j))],
            out_specs=out_spec,
        ),
        out_shape=jax.ShapeDtypeStruct(full, F32),
        compiler_params=_params(_mm_vmem([((tm, tn), BF16, 8), ((tm, tn), F32, 6)])),
    )(qidx, c_half, r_ici)


_REL_MASK = (2, 1, 3)


def _place():
    x, y, c = lax.axis_index("x"), lax.axis_index("y"), lax.axis_index("c")
    chips = [(1 - x, y), (x, 1 - y), (1 - x, 1 - y)]
    return x, y, c, 2 * x + y, chips


def _shard_view(ref, kind, chip):
    if kind == "col":
        w = ref.shape[1] // N_CHIPS
        return ref.at[:, pl.ds(pl.multiple_of(chip * w, LANES), w)]
    h = ref.shape[0] // N_CHIPS
    return ref.at[pl.ds(pl.multiple_of(chip * h, 16), h), :]


def _row_half(ref, half):
    h = ref.shape[0] // 2
    return ref.at[pl.ds(pl.multiple_of(half * h, 16), h), :]


def _pair_half(ref, kind, half):
    if kind == "col":
        return _row_half(ref, half)
    w = ref.shape[1] // 2
    return ref.at[:, pl.ds(pl.multiple_of(half * w, LANES), w)]


def _remote(src, dst, send_sem, recv_sem, dev):
    return pltpu.make_async_remote_copy(src_ref=src, dst_ref=dst, send_sem=send_sem, recv_sem=recv_sem, device_id=dev, device_id_type=MESH)


def _hbm(a):
    return pltpu.with_memory_space_constraint(a, pltpu.HBM)


def _gather_start(name, fulls, kinds):
    n_w = len(fulls)

    def body(*refs):
        g = refs[:n_w]
        send_sem, recv_sem = refs[n_w], refs[n_w + 1]
        token = refs[-1]
        _, _, c, q, chips = _place()
        for w in range(n_w):
            mine = _row_half(_shard_view(g[w], kinds[w], q), c)
            for r, chip in enumerate(chips):
                _remote(mine, mine, send_sem.at[3 * w + r], recv_sem.at[3 * w + r], (*chip, c)).start()
        token[...] = jnp.zeros_like(token)

    res = _pallas(
        body,
        name=name,
        out_shape=(
            pltpu.SemaphoreType.DMA((3 * n_w,)),
            pltpu.SemaphoreType.DMA((3 * n_w,)),
            *[pltpu.HBM(f.shape, f.dtype) for f in fulls],
            jax.ShapeDtypeStruct((8, LANES), F32),
        ),
        in_specs=[HBM_SPEC] * n_w,
        out_specs=(SEM_SPEC, SEM_SPEC, *[HBM_SPEC] * n_w, pl.BlockSpec(memory_space=pltpu.VMEM)),
        input_output_aliases={w: w + 2 for w in range(n_w)},
        compiler_params=pltpu.CompilerParams(has_side_effects=EFFECT),
    )(*[_hbm(f) for f in fulls])
    return res[0], res[1], list(res[2 : 2 + n_w]), res[-1]


def _gather_wait(name, fulls, kinds, w_ids, send_sem, recv_sem, after):
    n = len(fulls)

    def body(*refs):
        g = refs[:n]
        s_sem, r_sem = refs[n], refs[n + 1]
        x, y, c, q, _ = _place()
        for i, w in enumerate(w_ids):
            mine = _row_half(_shard_view(g[i], kinds[i], q), c)
            for r in range(3):
                landed = _row_half(_shard_view(g[i], kinds[i], q ^ _REL_MASK[r]), c)
                cp = _remote(mine, landed, s_sem.at[3 * w + r], r_sem.at[3 * w + r], (x, y, 1 - c))
                cp.wait_send()
                cp.wait_recv()

    res = _pallas(
        body,
        name=name,
        out_shape=[pltpu.HBM(f.shape, f.dtype) for f in fulls],
        in_specs=[HBM_SPEC] * n + [SEM_SPEC, SEM_SPEC, ANY],
        out_specs=[HBM_SPEC] * n,
        input_output_aliases={i: i for i in range(n)},
        compiler_params=pltpu.CompilerParams(has_side_effects=EFFECT),
    )(*fulls, send_sem, recv_sem, after)
    return list(res)


def _gather_forward(name, fulls, kinds):
    n = len(fulls)

    def body(*refs):
        g = refs[n : 2 * n]
        send, recv = refs[2 * n :]
        x, y, c, q, _ = _place()
        sib = (x, y, 1 - c)
        cps = []
        for i in range(n):
            for r in range(3):
                landed = _row_half(_shard_view(g[i], kinds[i], q ^ _REL_MASK[r]), c)
                cps.append(_remote(landed, landed, send.at[i, r], recv.at[i, r], sib))
        for cp in cps:
            cp.start()
        for i in range(n):
            for r in range(3):
                other = _row_half(_shard_view(g[i], kinds[i], q ^ _REL_MASK[r]), 1 - c)
                _remote(other, other, send.at[i, r], recv.at[i, r], sib).wait_recv()
        for cp in cps:
            cp.wait_send()

    res = _pallas(
        body,
        name=name,
        in_specs=[ANY] * n,
        out_specs=[ANY] * n,
        out_shape=[jax.ShapeDtypeStruct(f.shape, f.dtype) for f in fulls],
        scratch_shapes=[pltpu.SemaphoreType.DMA((n, 3)), pltpu.SemaphoreType.DMA((n, 3))],
        input_output_aliases={i: i for i in range(n)},
    )(*fulls)
    return list(res)


def _split_start(name, bufs, n_sems, copies):
    n = len(bufs)

    def body(*refs):
        for cp in copies(refs[:n], refs[n], refs[n + 1]):
            cp.start()

    res = _pallas(
        body,
        name=name,
        out_shape=(
            pltpu.SemaphoreType.DMA((n_sems,)),
            pltpu.SemaphoreType.DMA((n_sems,)),
            *[pltpu.HBM(b.shape, b.dtype) for b in bufs],
        ),
        in_specs=[HBM_SPEC] * n,
        out_specs=(SEM_SPEC, SEM_SPEC, *[HBM_SPEC] * n),
        input_output_aliases={i: i + 2 for i in range(n)},
        compiler_params=pltpu.CompilerParams(has_side_effects=EFFECT),
    )(*[_hbm(b) for b in bufs])
    return res[0], res[1], list(res[2:])


def _split_wait(name, bufs, send_sem, recv_sem, copies, after):
    n = len(bufs)

    def body(*refs):
        for cp in copies(refs[:n], refs[n], refs[n + 1]):
            cp.wait_send()
            cp.wait_recv()

    res = _pallas(
        body,
        name=name,
        out_shape=[pltpu.HBM(b.shape, b.dtype) for b in bufs],
        in_specs=[HBM_SPEC] * n + [SEM_SPEC, SEM_SPEC, ANY],
        out_specs=[HBM_SPEC] * n,
        input_output_aliases={i: i for i in range(n)},
        compiler_params=pltpu.CompilerParams(has_side_effects=EFFECT),
    )(*bufs, send_sem, recv_sem, after)
    return list(res)


def _pair_exchange_copies(kinds):
    n = len(kinds)

    def copies(refs, send_sem, recv_sem):
        x, y, c, _, _ = _place()
        return [
            _remote(_pair_half(refs[w], kinds[w], 1 - c), refs[n + w], send_sem.at[w], recv_sem.at[w], (x, y, 1 - c))
            for w in range(n)
        ]

    return copies


def _pair_share_copies(kinds, waiting):
    def copies(refs, send_sem, recv_sem):
        x, y, c, _, _ = _place()
        out = []
        for w, kind in enumerate(kinds):
            mine = _pair_half(refs[w], kind, c)
            dst = _pair_half(refs[w], kind, 1 - c) if waiting else mine
            out.append(_remote(mine, dst, send_sem.at[w], recv_sem.at[w], (x, y, 1 - c)))
        return out

    return copies


def _piece_shape(half_shape, kind):
    r, c = half_shape
    return (3, r, c // N_CHIPS) if kind == "col" else (3, r // N_CHIPS, c)


def _chip_send_start(name, halves, kinds):
    n = len(halves)
    lands = [lax.empty(_piece_shape(h.shape, k), BF16) for h, k in zip(halves, kinds)]

    def body(*refs):
        h, land = refs[:n], refs[n : 2 * n]
        send_sem, recv_sem = refs[2 * n], refs[2 * n + 1]
        _, _, c, q, chips = _place()
        for i in range(n):
            for r, chip in enumerate(chips):
                piece = _shard_view(h[i], kinds[i], q ^ _REL_MASK[r])
                _remote(piece, land[i].at[r], send_sem.at[3 * i + r], recv_sem.at[3 * i + r], (*chip, c)).start()

    res = _pallas(
        body,
        name=name,
        out_shape=(
            pltpu.SemaphoreType.DMA((3 * n,)),
            pltpu.SemaphoreType.DMA((3 * n,)),
            *[pltpu.HBM(a.shape, a.dtype) for a in halves],
            *[pltpu.HBM(a.shape, a.dtype) for a in lands],
        ),
        in_specs=[HBM_SPEC] * (2 * n),
        out_specs=(SEM_SPEC, SEM_SPEC, *[HBM_SPEC] * (2 * n)),
        input_output_aliases={i: i + 2 for i in range(2 * n)},
        compiler_params=pltpu.CompilerParams(has_side_effects=EFFECT),
    )(*[_hbm(a) for a in halves], *[_hbm(a) for a in lands])
    return res[0], res[1], list(res[2 : 2 + n]), list(res[2 + n :])


def _chip_send_wait(name, halves, lands, kinds, send_sem, recv_sem, after):
    n = len(halves)

    def body(*refs):
        h, land = refs[:n], refs[n : 2 * n]
        s_sem, r_sem = refs[2 * n], refs[2 * n + 1]
        x, y, c, q, _ = _place()
        for i in range(n):
            for r in range(3):
                piece = _shard_view(h[i], kinds[i], q ^ _REL_MASK[r])
                cp = _remote(piece, land[i].at[r], s_sem.at[3 * i + r], r_sem.at[3 * i + r], (x, y, 1 - c))
                cp.wait_send()
                cp.wait_recv()

    res = _pallas(
        body,
        name=name,
        out_shape=[pltpu.HBM(a.shape, a.dtype) for a in halves] + [pltpu.HBM(a.shape, a.dtype) for a in lands],
        in_specs=[HBM_SPEC] * (2 * n) + [SEM_SPEC, SEM_SPEC, ANY],
        out_specs=[HBM_SPEC] * (2 * n),
        input_output_aliases={i: i for i in range(2 * n)},
        compiler_params=pltpu.CompilerParams(has_side_effects=EFFECT),
    )(*halves, *lands, send_sem, recv_sem, after)
    return list(res[:n]), list(res[n:])


def _small_all_reduce(p):
    rows = p.shape[0]
    n_dev = 2 * N_CHIPS

    def body(p_ref, o_ref, buf, loc_sem, send, recv):
        x, y, c, q, _ = _place()
        me = 2 * q + c
        own = pltpu.make_async_copy(p_ref, buf.at[me], loc_sem)
        own.start()
        cps = []
        for d in range(1, n_dev):
            dev = (x ^ ((d >> 2) & 1), y ^ ((d >> 1) & 1), c ^ (d & 1))
            cps.append(_remote(p_ref, buf.at[me], send.at[d - 1], recv.at[d - 1], dev))
        for cp in cps:
            cp.start()
        for d in range(1, n_dev):
            slot = buf.at[me ^ d]
            _remote(slot, slot, send.at[d - 1], recv.at[d - 1], (x, y, c)).wait_recv()
        own.wait()
        total = buf[0]
        for d in range(1, n_dev):
            total = total + buf[d]
        o_ref[...] = total
        for cp in cps:
            cp.wait_send()

    return _pallas(
        body,
        name="small_all_reduce",
        in_specs=[ANY],
        out_specs=pl.BlockSpec(memory_space=pltpu.VMEM),
        out_shape=jax.ShapeDtypeStruct(p.shape, F32),
        scratch_shapes=[
            pltpu.VMEM((n_dev, rows, LANES), F32),
            pltpu.SemaphoreType.DMA,
            pltpu.SemaphoreType.DMA((n_dev - 1,)),
            pltpu.SemaphoreType.DMA((n_dev - 1,)),
        ],
    )(p)


def _small_exchange_copies(waiting):
    def copies(refs, send_sem, recv_sem):
        p, land = refs
        x, y, c, q, _ = _place()
        me = 2 * q + c
        out = []
        for dd in range(1, 2 * N_CHIPS):
            dev = (x ^ ((dd >> 2) & 1), y ^ ((dd >> 1) & 1), c ^ (dd & 1))
            dst = land.at[me ^ dd] if waiting else land.at[me]
            out.append(_remote(p, dst, send_sem.at[dd - 1], recv_sem.at[dd - 1], dev))
        return out

    return copies


def _small_sum(name, me_idx, p, land):
    rows = p.shape[0]
    n_dev = 2 * N_CHIPS

    def body(me_ref, p_ref, land_ref, o_ref):
        me = me_ref[0]
        total = None
        for dev in range(n_dev):
            term = jnp.where(me == dev, p_ref[...], land_ref[dev])
            total = term if total is None else total + term
        o_ref[...] = total

    return _pallas(
        body,
        name=name,
        grid_spec=pltpu.PrefetchScalarGridSpec(
            num_scalar_prefetch=1,
            grid=(1,),
            in_specs=[pl.BlockSpec((rows, LANES), lambda i, m: (0, 0)), pl.BlockSpec((n_dev, rows, LANES), lambda i, m: (0, 0, 0))],
            out_specs=pl.BlockSpec((rows, LANES), lambda i, m: (0, 0)),
        ),
        out_shape=jax.ShapeDtypeStruct(p.shape, F32),
    )(me_idx, p, land)


def _pack(parts):
    rows = []
    for a in parts:
        flat = a.reshape(-1).astype(F32)
        n = flat.shape[0]
        padded = -(-n // (8 * LANES)) * (8 * LANES)
        rows.append(jnp.pad(flat, (0, padded - n)).reshape(-1, LANES))
    return jnp.concatenate(rows, axis=0)


def _unpack(packed, shapes):
    out, row = [], 0
    for shp in shapes:
        n = int(np.prod(shp))
        nrows = -(-n // (8 * LANES)) * 8
        out.append(packed[row : row + nrows].reshape(-1)[:n].reshape(shp))
        row += nrows
    return out


def kernel(x, w_in, norm_mix, sgu_v_gain, sgu_w_s, sgu_b_s, w_a_out, attn_sink, rel_bias, w_b_out, w_o, norm_ffn, w_gate, w_up, w_down, norm_final, loss_target, m_w_in, m_norm_mix, m_sgu_v_gain, m_sgu_w_s, m_sgu_b_s, m_w_a_out, m_attn_sink, m_rel_bias, m_w_b_out, m_w_o, m_norm_ffn, m_w_gate, m_w_up, m_w_down, m_norm_final, v_w_in, v_norm_mix, v_sgu_v_gain, v_sgu_w_s, v_sgu_b_s, v_w_a_out, v_attn_sink, v_rel_bias, v_w_b_out, v_w_o, v_norm_ffn, v_w_gate, v_w_up, v_w_down, v_norm_final):
    s, d = x.shape[1], x.shape[2]
    w_sgu = sgu_v_gain.shape[1]
    groups = sgu_w_s.shape[1]
    heads = attn_sink.shape[1]
    grp = heads // N_KV_HEADS
    w_att = heads * HEAD_DIM
    w_kv = N_KV_HEADS * HEAD_DIM
    d_ff = w_gate.shape[2] * N_CHIPS
    n_in = w_in.shape[2] * N_CHIPS
    off_q = 2 * w_sgu
    off_k = off_q + w_att
    off_g = off_k + 2 * w_kv
    assert n_in == off_g + 2 * d and groups * BLK == w_sgu and s % BLK == 0

    x2d = x.reshape(s, d)
    tgt = loss_target.reshape(s, d)
    c_idx = lax.axis_index("c").astype(I32).reshape(1)
    q_idx = (2 * lax.axis_index("x") + lax.axis_index("y")).astype(I32).reshape(1)
    qc_idx = jnp.concatenate([q_idx, c_idx])

    W_IN, W_A, W_B, W_O, W_GATE, W_UP, W_DOWN = range(7)
    names = ["w_in", "w_a", "w_b", "w_o", "w_gate", "w_up", "w_down"]
    kinds = ["col", "col", "col", "row", "col", "col", "row"]
    big_w = [w_in[0], w_a_out[0], w_b_out[0], w_o[0], w_gate[0], w_up[0], w_down[0]]
    big_m = [m_w_in[0], m_w_a_out[0], m_w_b_out[0], m_w_o[0], m_w_gate[0], m_w_up[0], m_w_down[0]]
    big_v = [v_w_in[0], v_w_a_out[0], v_w_b_out[0], v_w_o[0], v_w_gate[0], v_w_up[0], v_w_down[0]]
    full_in = _cast_into_full("cast_w_in", q_idx, big_w[W_IN], kinds[W_IN])
    in_send, in_recv, (full_in,), token = _gather_start("gather_start_in", [full_in], [kinds[W_IN]])
    rest = [_cast_into_full("cast_" + names[i], q_idx, big_w[i], kinds[i], after=(token,)) for i in range(1, 7)]
    ag_send, ag_recv, rest, token = _gather_start("gather_start_rest", rest, kinds[1:])
    fulls = [full_in] + rest

    def gathered(tag, ids, after):
        if ids == [W_IN]:
            sems, pos = (in_send, in_recv), [0]
        else:
            sems, pos = (ag_send, ag_recv), [i - 1 for i in ids]
        got = _gather_wait("gather_wait_" + tag, [fulls[i] for i in ids], [kinds[i] for i in ids], pos, *sems, after)
        return _gather_forward("gather_fwd_" + tag, got, [kinds[i] for i in ids])

    ws_b = sgu_w_s[0].astype(BF16)
    wst_b = jnp.swapaxes(sgu_w_s[0], 1, 2).astype(BF16)
    b_col = sgu_b_s[0].reshape(groups, BLK, 1)
    bias_tab, bucket = _band_tables(rel_bias)
    sink = attn_sink[0]

    tm = _tile(s, (1024, 512, 256, 128))

    h1 = _rms_fwd("rms_mix", x2d, norm_mix, after=(token,))
    (g_in,) = gathered("in", [W_IN], h1)

    tn = _tile(n_in, (768, 640, 512))
    z = _mm(
        "mm_z", (s // tm, n_in // tn, 1), [h1, g_in],
        [pl.BlockSpec((tm, d), lambda i, j, k: (i, 0)), pl.BlockSpec((d, tn), lambda i, j, k: (0, j))],
        [jax.ShapeDtypeStruct((s, n_in), BF16)], [pl.BlockSpec((tm, tn), lambda i, j, k: (i, j))],
        [(0, 1, NN, 0)], 1, (tm, tn), 1, lambda ins, vals, outs, cs: _put(outs[0], cs, vals[0]),
        _mm_vmem([((tm, d), BF16, 2), ((d, tn), BF16, 2), ((tm, tn), F32, 3)]),
    )[0]
    g_a, g_b, g_o = gathered("mix", [W_A, W_B, W_O], z)

    a_act = _sgu_fwd(z, sgu_v_gain, ws_b, b_col, w_sgu)

    kv_b = z[:, off_k:off_g]
    k_pad = jnp.pad(kv_b[:, :w_kv], ((BLK, BLK), (0, 0)))
    v_pad = jnp.pad(kv_b[:, w_kv:], ((BLK, BLK), (0, 0)))
    q_blk0 = off_q // (grp * HEAD_DIM)
    att = _attn_fwd(sink, z, k_pad, v_pad, bias_tab, grp, q_blk0)

    tg = _tile(d, (512,))
    ga0, gb0 = off_g // tg, (off_g + d) // tg

    def ep_gate(ins, vals, outs, cs):
        sa, sb = _sigmoid(ins[4][:, cs].astype(F32)), _sigmoid(ins[5][:, cs].astype(F32))
        _put(outs[0], cs, sa * vals[0] + sb * vals[1])
        _put(outs[1], cs, vals[0])
        _put(outs[2], cs, vals[1])

    t_out = pl.BlockSpec((tm, tg), lambda i, j, k: (i, j))
    m_act, y_a, y_b = _mm(
        "mm_branches", (s // tm, d // tg, 1), [a_act, g_a, att, g_b, z, z],
        [pl.BlockSpec((tm, w_sgu), lambda i, j, k: (i, 0)), pl.BlockSpec((w_sgu, tg), lambda i, j, k: (0, j)),
         pl.BlockSpec((tm, w_att), lambda i, j, k: (i, 0)), pl.BlockSpec((w_att, tg), lambda i, j, k: (0, j)),
         pl.BlockSpec((tm, tg), lambda i, j, k: (i, ga0 + j)), pl.BlockSpec((tm, tg), lambda i, j, k: (i, gb0 + j))],
        [jax.ShapeDtypeStruct((s, d), BF16)] * 3,
        [t_out, t_out, t_out], [(0, 1, NN, 0), (2, 3, NN, 1)], 2, (tm, tg), 1, ep_gate,
        _mm_vmem([((tm, w_sgu), BF16, 4), ((w_sgu, tg), BF16, 4), ((tm, tg), F32, 12)]),    )

    tn = _tile(d, (1024, 512))

    def ep_residual(ins, vals, outs, cs):
        _put(outs[0], cs, ins[2][:, cs] + vals[0])

    x2 = _mm(
        "mm_wo", (s // tm, d // tn, 1), [m_act, g_o, x2d],
        [pl.BlockSpec((tm, d), lambda i, j, k: (i, 0)), pl.BlockSpec((d, tn), lambda i, j, k: (0, j)),
         pl.BlockSpec((tm, tn), lambda i, j, k: (i, j))],
        [jax.ShapeDtypeStruct((s, d), F32)], [pl.BlockSpec((tm, tn), lambda i, j, k: (i, j))],
        [(0, 1, NN, 0)], 1, (tm, tn), 1, ep_residual,
        _mm_vmem([((tm, d), BF16, 2), ((d, tn), BF16, 2), ((tm, tn), F32, 5)]),    )[0]

    h2 = _rms_fwd("rms_ffn", x2, norm_ffn)
    g_gate, g_up = gathered("ffn_in", [W_GATE, W_UP], h2)

    tf = _tile(d_ff, (512,))

    def ep_swiglu(ins, vals, outs, cs):
        gt, up = vals
        _put(outs[0], cs, gt)
        _put(outs[1], cs, up)
        _put(outs[2], cs, (gt * _sigmoid(gt)) * up)

    f_out = pl.BlockSpec((tm, tf), lambda i, j, k: (i, j))
    gt, up, f_act = _mm(
        "mm_gate_up", (s // tm, d_ff // tf, 1), [h2, g_gate, g_up],
        [pl.BlockSpec((tm, d), lambda i, j, k: (i, 0)), pl.BlockSpec((d, tf), lambda i, j, k: (0, j)),
         pl.BlockSpec((d, tf), lambda i, j, k: (0, j))],
        [jax.ShapeDtypeStruct((s, d_ff), BF16)] * 3,
        [f_out, f_out, f_out], [(0, 1, NN, 0), (0, 2, NN, 1)], 2, (tm, tf), 1, ep_swiglu,
        _mm_vmem([((tm, d), BF16, 2), ((d, tf), BF16, 4), ((tm, tf), F32, 8)]),    )
    (g_down,) = gathered("ffn_out", [W_DOWN], f_act)

    tkf = _tile(d_ff, (1408, 1024, 512))
    nkf = d_ff // tkf
    tml, tnl = _tile(s, (512, 256, 128)), _tile(d, (512,))
    x3 = _mm(
        "mm_down", (s // tml, d // tnl, 1), [f_act, g_down, x2],
        [pl.BlockSpec((tml, d_ff), lambda i, j, k: (i, 0)), pl.BlockSpec((d_ff, tnl), lambda i, j, k: (0, j)),
         pl.BlockSpec((tml, tnl), lambda i, j, k: (i, j))],
        [jax.ShapeDtypeStruct((s, d), F32)], [pl.BlockSpec((tml, tnl), lambda i, j, k: (i, j))],
        [(0, 1, NN, 0)], 1, (tml, tnl), 1, ep_residual,
        _mm_vmem([((tml, d_ff), BF16, 2), ((d_ff, tnl), BF16, 2), ((tml, tnl), F32, 6)]),    )[0]

    dx3, dx3b, dg_final, loss_part = _head(x3, norm_final.reshape(1, d), tgt)

    def reduce_a(tag, ids, grads):
        ks = [kinds[i] for i in ids]
        lands = [lax.empty((g.shape[0] // 2, g.shape[1]) if k == "col" else (g.shape[0], g.shape[1] // 2), BF16)
                 for g, k in zip(grads, ks)]
        send, recv, bufs = _split_start("pair_send_" + tag, list(grads) + lands, len(ids), _pair_exchange_copies(ks))
        return {"tag": tag, "ids": ids, "ks": ks, "pair": (send, recv, bufs), "token": bufs[0]}

    def reduce_b(st, after):
        tag, ids, ks = st["tag"], st["ids"], st["ks"]
        send, recv, bufs = st["pair"]
        bufs = _split_wait("pair_wait_" + tag, bufs, send, recv, _pair_exchange_copies(ks), after)
        grads, from_sib = bufs[: len(ids)], bufs[len(ids) :]
        halves = [_pair_add("pair_add_" + names[i], c_idx, g, r, k) for i, g, r, k in zip(ids, grads, from_sib, ks)]
        st["chip"] = _chip_send_start("chip_send_" + tag, halves, ks)
        st["token"] = st["chip"][2][0]

    def reduce_c(st, after):
        tag, ids, ks = st["tag"], st["ids"], st["ks"]
        send, recv, halves, lands = st["chip"]
        halves, lands = _chip_send_wait("chip_wait_" + tag, halves, lands, ks, send, recv, after)
        pieces = [_chip_sum("chip_sum_" + names[i], qc_idx, h, r, k) for i, h, r, k in zip(ids, halves, lands, ks)]
        st["share"] = _split_start("share_send_" + tag, pieces, len(ids), _pair_share_copies(ks, False))
        st["token"] = st["share"][2][0]

    def reduce_d(st, after):
        send, recv, bufs = st["share"]
        return _split_wait("share_wait_" + st["tag"], bufs, send, recv, _pair_share_copies(st["ks"], True), after)

    grads_big, upd = [None] * 7, [None] * 7
    sc_updates = (W_DOWN, W_GATE, W_UP)

    def finish(st, after):
        shared = reduce_d(st, after)
        after = shared[0]
        for i, g in zip(st["ids"], shared):
            if i in sc_updates:
                upd[i] = _sc_adamw("sc_adamw_" + names[i], big_w[i], g, big_m[i], big_v[i])
            else:
                upd[i] = _adamw("adamw_" + names[i], big_w[i], g, big_m[i], big_v[i], after=(after,))
                after = upd[i][0]
            grads_big[i] = upd[i][3]
        return after

    def ep_swiglu_bwd(ins, vals, outs, cs):
        df = vals[0]
        gtv, upv = ins[2][:, cs].astype(F32), ins[3][:, cs].astype(F32)
        sg = _sigmoid(gtv)
        _put(outs[0], cs, df * upv * (sg + gtv * sg * (1.0 - sg)))
        _put(outs[1], cs, df * (gtv * sg))

    dgt, dup = _mm(
        "mm_dswiglu", (s // tm, d_ff // tf, 1), [dx3b, g_down, gt, up],
        [pl.BlockSpec((tm, d), lambda i, j, k: (i, 0)), pl.BlockSpec((tf, d), lambda i, j, k: (j, 0)), f_out, f_out],
        [jax.ShapeDtypeStruct((s, d_ff), BF16), jax.ShapeDtypeStruct((s, d_ff), BF16)], [f_out, f_out],
        [(0, 1, NT, 0)], 1, (tm, tf), 1, ep_swiglu_bwd,
        _mm_vmem([((tm, d), BF16, 2), ((tf, d), BF16, 2), ((tm, tf), F32, 8)]),    )

    def ep_store(ins, vals, outs, cs):
        for o, v in zip(outs, vals):
            _put(o, cs, v)

    twn = _tile(d, (1024, 512))
    gw_down = _mm(
        "mm_gw_down", (d_ff // tkf, d // twn, 1), [f_act, dx3b],
        [pl.BlockSpec((s, tkf), lambda i, j, k: (0, i)), pl.BlockSpec((s, twn), lambda i, j, k: (0, j))],
        [jax.ShapeDtypeStruct((d_ff, d), BF16)], [pl.BlockSpec((tkf, twn), lambda i, j, k: (i, j))],
        [(0, 1, TN, 0)], 1, (tkf, twn), 1, ep_store,
        _mm_vmem([((s, tkf), BF16, 3), ((s, twn), BF16, 2), ((tkf, twn), F32, 3)]),
    )[0]
    red_down = reduce_a("down", [W_DOWN], [gw_down])

    tn2 = _tile(d, (256,))
    dh2_specs = [pl.BlockSpec((tm, d_ff), lambda i, j, k: (i, 0)), pl.BlockSpec((tn2, d_ff), lambda i, j, k: (j, 0))]
    dh2_tile = pl.BlockSpec((tm, tn2), lambda i, j, k: (i, j))
    dh2_vmem = _mm_vmem([((tm, d_ff), BF16, 2), ((tn2, d_ff), BF16, 2), ((tm, tn2), F32, 7)])
    dh2 = _mm(
        "mm_dh2_gate", (s // tm, d // tn2, 1), [dgt, g_gate], dh2_specs,
        [jax.ShapeDtypeStruct((s, d), F32)], [dh2_tile], [(0, 1, NT, 0)], 1, (tm, tn2), 1, ep_store, dh2_vmem,
        after=(red_down["token"],),
    )[0]
    dh2 = _mm(
        "mm_dh2_up", (s // tm, d // tn2, 1), [dup, g_up, dh2], dh2_specs + [dh2_tile],
        [jax.ShapeDtypeStruct((s, d), F32)], [dh2_tile], [(0, 1, NT, 0)], 1, (tm, tn2), 1, ep_residual, dh2_vmem,
    )[0]
    reduce_b(red_down, dh2)

    twr = _tile(d, (1024, 512))
    w_tile = pl.BlockSpec((twr, tf), lambda i, j, k: (i, j))
    gw_gate, gw_up = _mm(
        "mm_gw_gate_up", (d // twr, d_ff // tf, 1), [h2, dgt, dup],
        [pl.BlockSpec((s, twr), lambda i, j, k: (0, i)), pl.BlockSpec((s, tf), lambda i, j, k: (0, j)),
         pl.BlockSpec((s, tf), lambda i, j, k: (0, j))],
        [jax.ShapeDtypeStruct((d, d_ff), BF16), jax.ShapeDtypeStruct((d, d_ff), BF16)], [w_tile, w_tile],
        [(0, 1, TN, 0), (0, 2, TN, 1)], 2, (twr, tf), 1, ep_store,
        _mm_vmem([((s, twr), BF16, 3), ((s, tf), BF16, 4), ((twr, tf), F32, 6)]),
        after=(red_down["token"],),
    )
    red_ffn = reduce_a("ffn_in", [W_GATE, W_UP], [gw_gate, gw_up])

    dx2, dx2b, dg_ffn = _rms_bwd("rms_ffn_bwd", x2, norm_ffn, dh2, dx3, after=(red_ffn["token"],))

    nj = d // tg

    def lo(j):
        return jnp.minimum(j, nj - 1)

    def gate_bwd_body(dx_ref, wo_ref, ga_ref, gb_ref, ya_ref, yb_ref, dya_ref, dyb_ref, dz_ref, keep):
        j = pl.program_id(1)

        @pl.when(j < nj)
        def _():
            dm = lax.dot_general(dx_ref[...], wo_ref[...], NT, preferred_element_type=F32)
            sa, sb = _sigmoid(ga_ref[...].astype(F32)), _sigmoid(gb_ref[...].astype(F32))
            dya_ref[...] = (dm * sa).astype(BF16)
            dyb_ref[...] = (dm * sb).astype(BF16)
            dz_ref[...] = (dm * ya_ref[...].astype(F32) * (sa * (1.0 - sa))).astype(BF16)
            keep[lo(j)] = (dm * yb_ref[...].astype(F32) * (sb * (1.0 - sb))).astype(BF16)

        @pl.when(j >= nj)
        def _():
            dz_ref[...] = keep[jnp.maximum(j - nj, 0)]

    t_lo = pl.BlockSpec((tm, tg), lambda i, j: (i, lo(j)))
    dya, dyb, dz = _pallas(
        gate_bwd_body,
        name="mm_dgate",
        grid=(s // tm, 2 * nj),
        in_specs=[
            pl.BlockSpec((tm, d), lambda i, j: (i, 0)),
            pl.BlockSpec((tg, d), lambda i, j: (lo(j), 0)),
            pl.BlockSpec((tm, tg), lambda i, j: (i, ga0 + lo(j))),
            pl.BlockSpec((tm, tg), lambda i, j: (i, gb0 + lo(j))),
            t_lo,
            t_lo,
        ],
        out_specs=[t_lo, t_lo, pl.BlockSpec((tm, tg), lambda i, j: (i, ga0 + j))],
        out_shape=[jax.ShapeDtypeStruct((s, d), BF16), jax.ShapeDtypeStruct((s, d), BF16), jax.ShapeDtypeStruct((s, n_in), BF16)],
        scratch_shapes=[pltpu.VMEM((nj, tm, tg), BF16)],
        compiler_params=_params(_mm_vmem([((tm, d), BF16, 2), ((tg, d), BF16, 2), ((tm, tg), F32, 14), ((nj, tm, tg), BF16, 1)])),
    )(dx2b, g_o, z, z, y_a, y_b)
    reduce_b(red_ffn, dya)
    reduce_c(red_down, red_ffn["token"])

    gw_o = _mm(
        "mm_gw_o", (d // twr, d // twn, 1), [m_act, dx2b],
        [pl.BlockSpec((s, twr), lambda i, j, k: (0, i)), pl.BlockSpec((s, twn), lambda i, j, k: (0, j))],
        [jax.ShapeDtypeStruct((d, d), BF16)], [pl.BlockSpec((twr, twn), lambda i, j, k: (i, j))],
        [(0, 1, TN, 0)], 1, (twr, twn), 1, ep_store,
        _mm_vmem([((s, twr), BF16, 3), ((s, twn), BF16, 2), ((twr, twn), F32, 3)]),
        after=(red_down["token"],),
    )[0]
    red_o = reduce_a("w_o", [W_O], [gw_o])
    finish(red_down, red_o["token"])

    tb = _tile(w_sgu, (1024, 512))
    b_out = pl.BlockSpec((tm, tb), lambda i, j, k: (i, j))

    da, datt = _mm(
        "mm_dbranches", (s // tm, w_sgu // tb, 1), [dya, g_a, dyb, g_b],
        [pl.BlockSpec((tm, d), lambda i, j, k: (i, 0)), pl.BlockSpec((tb, d), lambda i, j, k: (j, 0)),
         pl.BlockSpec((tm, d), lambda i, j, k: (i, 0)), pl.BlockSpec((tb, d), lambda i, j, k: (j, 0))],
        [jax.ShapeDtypeStruct((s, w_sgu), BF16), jax.ShapeDtypeStruct((s, w_att), BF16)], [b_out, b_out],
        [(0, 1, NT, 0), (2, 3, NT, 1)], 2, (tm, tb), 1, ep_store,
        _mm_vmem([((tm, d), BF16, 4), ((tb, d), BF16, 4), ((tm, tb), F32, 6)]),        after=(red_o["token"],),
    )
    reduce_b(red_o, da)

    wb_tile = pl.BlockSpec((tb, twn), lambda i, j, k: (i, j))
    gw_a, gw_b = _mm(
        "mm_gw_branches", (w_sgu // tb, d // twn, 1), [a_act, dya, att, dyb],
        [pl.BlockSpec((s, tb), lambda i, j, k: (0, i)), pl.BlockSpec((s, twn), lambda i, j, k: (0, j)),
         pl.BlockSpec((s, tb), lambda i, j, k: (0, i)), pl.BlockSpec((s, twn), lambda i, j, k: (0, j))],
        [jax.ShapeDtypeStruct((w_sgu, d), BF16), jax.ShapeDtypeStruct((w_att, d), BF16)], [wb_tile, wb_tile],
        [(0, 1, TN, 0), (2, 3, TN, 1)], 2, (tb, twn), 1, ep_store,
        _mm_vmem([((s, tb), BF16, 5), ((s, twn), BF16, 4), ((tb, twn), F32, 6)]),
        after=(red_o["token"],),
    )
    red_mix = reduce_a("mix", [W_A, W_B], [gw_a, gw_b])

    dz, dws, dbs, dgain = _sgu_bwd(z, da, sgu_v_gain, ws_b, wst_b, b_col, w_sgu, dz, after=(red_mix["token"],))
    dz, dk_pad, dv_pad, dbias_tab, dsink = _attn_bwd(sink, z, k_pad, v_pad, bias_tab, datt, dz, grp, q_blk0)
    dz = _dkv_to_dz(dk_pad, dv_pad, dz, off_k // (2 * w_kv))
    drel = _relbias_bwd(dbias_tab, bucket)
    reduce_b(red_mix, dz)
    reduce_c(red_ffn, red_mix["token"])

    small_w = [norm_mix, sgu_v_gain, sgu_w_s, sgu_b_s, attn_sink, rel_bias, norm_ffn, norm_final]
    small_m = [m_norm_mix, m_sgu_v_gain, m_sgu_w_s, m_sgu_b_s, m_attn_sink, m_rel_bias, m_norm_ffn, m_norm_final]
    small_v = [v_norm_mix, v_sgu_v_gain, v_sgu_w_s, v_sgu_b_s, v_attn_sink, v_rel_bias, v_norm_ffn, v_norm_final]
    small_shapes = [w.shape for w in small_w]
    early = [dgain, dws, dbs, dsink[:, 0], drel[:, :REL_BUCKETS].T, dg_ffn, dg_final]
    p_early = _pack([g.reshape(shp) for g, shp in zip(early, small_shapes[1:])] + [loss_part[0, :1]])
    land = jnp.zeros((2 * N_CHIPS,) + p_early.shape, F32)
    sm_send, sm_recv, (p_early, land) = _split_start("small_send", [p_early, land], 2 * N_CHIPS - 1, _small_exchange_copies(False))

    tzn = _tile(n_in, (768, 640, 512))
    gw_in = _mm(
        "mm_gw_in", (d // twr, n_in // tzn, 1), [h1, dz],
        [pl.BlockSpec((s, twr), lambda i, j, k: (0, i)), pl.BlockSpec((s, tzn), lambda i, j, k: (0, j))],
        [jax.ShapeDtypeStruct((d, n_in), BF16)], [pl.BlockSpec((twr, tzn), lambda i, j, k: (i, j))],
        [(0, 1, TN, 0)], 1, (twr, tzn), 1, ep_store,
        _mm_vmem([((s, twr), BF16, 3), ((s, tzn), BF16, 2), ((twr, tzn), F32, 3)]),
        after=(red_ffn["token"], p_early),
    )[0]
    red_in = reduce_a("w_in", [W_IN], [gw_in])
    finish(red_ffn, red_in["token"])

    reduce_c(red_o, red_in["token"])
    reduce_c(red_mix, red_o["token"])
    reduce_b(red_in, finish(red_mix, finish(red_o, red_mix["token"])))

    dh1 = _mm(
        "mm_dh1", (s // tm, d // tn2, 1), [dz, g_in],
        [pl.BlockSpec((tm, n_in), lambda i, j, k: (i, 0)), pl.BlockSpec((tn2, n_in), lambda i, j, k: (j, 0))],
        [jax.ShapeDtypeStruct((s, d), F32)], [pl.BlockSpec((tm, tn2), lambda i, j, k: (i, j))],
        [(0, 1, NT, 0)], 1, (tm, tn2), 1, ep_store,
        _mm_vmem([((tm, n_in), BF16, 2), ((tn2, n_in), BF16, 2), ((tm, tn2), F32, 5)]),
        after=(red_in["token"],),
    )[0]

    grad_x, _, dg_mix = _rms_bwd("rms_mix_bwd", x2d, norm_mix, dh1, dx2)

    p_mix = _pack([dg_mix.reshape(small_shapes[0])])
    land_mix = jnp.zeros((2 * N_CHIPS,) + p_mix.shape, F32)
    mx_send, mx_recv, (p_mix, land_mix) = _split_start("mix_send", [p_mix, land_mix], 2 * N_CHIPS - 1, _small_exchange_copies(False))

    reduce_c(red_in, p_mix)
    p_early, land = _split_wait("small_wait", [p_early, land], sm_send, sm_recv, _small_exchange_copies(True), red_in["token"])
    p_mix, land_mix = _split_wait("mix_wait", [p_mix, land_mix], mx_send, mx_recv, _small_exchange_copies(True), p_early)
    me_idx = 2 * q_idx + c_idx
    packed_g = jnp.concatenate([_small_sum("mix_sum", me_idx, p_mix, land_mix), _small_sum("small_sum", me_idx, p_early, land)], axis=0)
    g_small = _unpack(packed_g, small_shapes + [(1,)])
    loss = g_small[-1].reshape(())
    g_small = g_small[:-1]
    zero1 = jnp.zeros((1,), F32)
    pw, pg, pm, pv = _pack(small_w + [zero1]), _pack(g_small + [zero1]), _pack(small_m + [zero1]), _pack(small_v + [zero1])
    small_upd = _adamw("adamw_small", pw, pg, pm, pv)
    d_small, nm_small, nv_small = [_unpack(a, small_shapes) for a in small_upd[:3]]
    finish(red_in, small_upd[0])

    small_names = ["norm_mix", "sgu_v_gain", "sgu_w_s", "sgu_b_s", "attn_sink", "rel_bias", "norm_ffn", "norm_final"]
    table = {}
    for i, n in enumerate(names):
        table[n] = (grads_big[i][None], upd[i][0][None], upd[i][1][None], upd[i][2][None])
    for i, n in enumerate(small_names):
        table[n] = (g_small[i], d_small[i], nm_small[i], nv_small[i])
    order = ["w_in", "norm_mix", "sgu_v_gain", "sgu_w_s", "sgu_b_s", "w_a", "attn_sink", "rel_bias", "w_b", "w_o", "norm_ffn",
             "w_gate", "w_up", "w_down", "norm_final"]
    outs = [loss, grad_x.reshape(1, s, d)]
    for part in range(4):
        outs += [table[n][part] for n in order]
    return tuple(outs)
```

```python
import math

import jax
import jax.numpy as jnp
import numpy as np
from jax import lax
from jax.experimental import pallas as pl
from jax.experimental.pallas import tpu as pltpu

F32 = jnp.float32
BF16 = jnp.bfloat16
I32 = jnp.int32
MESH = pl.DeviceIdType.MESH

EPS = 1e-6
NEG = -1e30
BLK = 128
HEAD_DIM = 128
N_KV_HEADS = 2
REL_BUCKETS = 32
REL_MAX_DIST = 128
N_CHIPS = 4
ADAM_LR, ADAM_B1, ADAM_B2, ADAM_EPS, ADAM_WD, ADAM_STEP = 0.001, 0.9, 0.999, 1e-08, 0.01, 10

LANES = 128
MXU_COLS = 256
VMEM_CAP = 60 * 1024 * 1024

NN = (((1,), (0,)), ((), ()))
NT = (((1,), (1,)), ((), ()))
TN = (((0,), (0,)), ((), ()))
ANY = pl.BlockSpec(memory_space=pl.ANY)
HBM_SPEC = pl.BlockSpec(memory_space=pltpu.HBM)
SEM_SPEC = pl.BlockSpec(memory_space=pltpu.SEMAPHORE)
EFFECT = pltpu.SideEffectType.DATAFLOW_SIDE_EFFECTING


def _tile(n, cands):
    for t in cands:
        if n % t == 0:
            return t
    return n


PIN_BYTES = 64 * 1024


def _pin_hbm(a):
    big = hasattr(a, "dtype") and jnp.issubdtype(a.dtype, jnp.floating) and _nbytes(a.shape, a.dtype) >= PIN_BYTES
    return pltpu.with_memory_space_constraint(a, pltpu.HBM) if big else a


def _pallas(body, *, out_shape, **kw):
    def pin(o):
        big = isinstance(o, jax.ShapeDtypeStruct) and jnp.issubdtype(o.dtype, jnp.floating) and _nbytes(o.shape, o.dtype) >= PIN_BYTES
        return pltpu.HBM(o.shape, o.dtype) if big else o

    shapes = type(out_shape)(pin(o) for o in out_shape) if isinstance(out_shape, (list, tuple)) else pin(out_shape)
    call = pl.pallas_call(body, out_shape=shapes, **kw)
    return lambda *args: call(*[_pin_hbm(a) for a in args])


def _params(vmem_bytes=None, **kw):
    if vmem_bytes is not None:
        kw["vmem_limit_bytes"] = int(min(max(vmem_bytes, 32 * 1024 * 1024), VMEM_CAP))
    return pltpu.CompilerParams(**kw)


def _nbytes(shape, dtype):
    return int(np.prod(shape)) * jnp.dtype(dtype).itemsize


def _sigmoid(x):
    return 1.0 / (1.0 + jnp.exp(-x))


_GC = 0.7978845608028654
_GA = 0.044715


def _gelu(x):
    return 0.5 * x * (1.0 + jnp.tanh(_GC * (x + _GA * (x * x * x))))


def _gelu_grad(x):
    t = jnp.tanh(_GC * (x + _GA * (x * x * x)))
    return 0.5 * (1.0 + t) + 0.5 * x * (1.0 - t * t) * (_GC * (1.0 + 3.0 * _GA * (x * x)))


def _bf(v):
    return v if v.dtype == BF16 else v.astype(BF16)


def _mm(name, grid, ins, in_specs, out_shape, out_specs, pairs, n_acc, tile, nk, epilogue, vmem_bytes, after=(), col_chunks=1):
    assert nk == 1 and tile[1] % col_chunks == 0
    n_in, n_out = len(ins) + len(after), len(out_shape)
    width = tile[1] // col_chunks

    def body(*refs):
        in_refs, out_refs = refs[:n_in], refs[n_in : n_in + n_out]
        for ch in range(col_chunks):
            cs = slice(ch * width, (ch + 1) * width) if col_chunks > 1 else slice(None)
            vals = [None] * n_acc
            for a_i, b_i, dn, acc_i in pairs:
                rhs = in_refs[b_i][cs, :] if dn == NT else in_refs[b_i][:, cs]
                d = lax.dot_general(_bf(in_refs[a_i][...]), _bf(rhs), dn, preferred_element_type=F32)
                vals[acc_i] = d if vals[acc_i] is None else vals[acc_i] + d
            epilogue(in_refs, vals, out_refs, cs)

    return _pallas(
        body,
        name=name,
        grid=grid,
        in_specs=list(in_specs) + [ANY] * len(after),
        out_specs=out_specs,
        out_shape=out_shape,
        compiler_params=_params(vmem_bytes),
    )(*ins, *after)


def _put(ref, cs, v):
    ref[:, cs] = v.astype(ref.dtype)


def _mm_vmem(tiles):
    return sum(_nbytes(s, d) * c for s, d, c in tiles) + 4 * 1024 * 1024


def _rows8(v):
    r, d = v.shape
    return v.reshape(r // 8, 8, d).sum(axis=0)


def _rms_fwd(name, x, g, after=()):
    s, d = x.shape
    tm = _tile(s, (256, 128))

    def body(x_ref, g_ref, *rest):
        h_ref = rest[-1]
        xv = x_ref[...]
        r = lax.rsqrt(jnp.mean(xv * xv, axis=-1, keepdims=True) + EPS)
        h_ref[...] = ((xv * r) * g_ref[...]).astype(BF16)

    return _pallas(
        body,
        name=name,
        grid=(s // tm,),
        in_specs=[pl.BlockSpec((tm, d), lambda i: (i, 0)), pl.BlockSpec((1, d), lambda i: (0, 0))] + [ANY] * len(after),
        out_specs=pl.BlockSpec((tm, d), lambda i: (i, 0)),
        out_shape=jax.ShapeDtypeStruct((s, d), BF16),
    )(x, g, *after)


def _rms_bwd(name, x, g, dh, dres, after=()):
    s, d = x.shape
    tm = _tile(s, (256, 128))
    n = s // tm
    n_after = len(after)

    def body(x_ref, g_ref, dh_ref, dres_ref, *rest):
        dx_ref, dxb_ref, dg_ref, acc_ref = rest[n_after:]
        i = pl.program_id(0)
        xv = x_ref[...]
        r = lax.rsqrt(jnp.mean(xv * xv, axis=-1, keepdims=True) + EPS)
        xh = xv * r
        dhv = dh_ref[...]
        dxh = dhv * g_ref[...]
        dx = r * (dxh - xh * jnp.mean(dxh * xh, axis=-1, keepdims=True)) + dres_ref[...]
        dx_ref[...] = dx
        dxb_ref[...] = dx.astype(BF16)
        part = _rows8(dhv * xh)

        @pl.when(i == 0)
        def _():
            acc_ref[...] = part

        @pl.when(i > 0)
        def _():
            acc_ref[...] += part

        @pl.when(i == n - 1)
        def _():
            dg_ref[...] = jnp.sum(acc_ref[...], axis=0, keepdims=True)

    row = pl.BlockSpec((tm, d), lambda i: (i, 0))
    vec = pl.BlockSpec((1, d), lambda i: (0, 0))
    return _pallas(
        body,
        name=name,
        grid=(n,),
        in_specs=[row, vec, row, row] + [ANY] * n_after,
        out_specs=[row, row, vec],
        out_shape=[jax.ShapeDtypeStruct((s, d), F32), jax.ShapeDtypeStruct((s, d), BF16), jax.ShapeDtypeStruct((1, d), F32)],
        scratch_shapes=[pltpu.VMEM((8, d), F32)],
    )(x, g, dh, dres, *after)


def _head(x3, g, target):
    s, d = x3.shape
    tm = _tile(s, (256, 128))
    n = s // tm

    def body(x_ref, g_ref, t_ref, dx_ref, dxb_ref, dg_ref, loss_ref, acc_g, acc_l):
        i = pl.program_id(0)
        xv = x_ref[...]
        gv = g_ref[...]
        r = lax.rsqrt(jnp.mean(xv * xv, axis=-1, keepdims=True) + EPS)
        xh = xv * r
        e = xh * gv - t_ref[...]
        dy = e * (1.0 / d)
        dxh = dy * gv
        dx = r * (dxh - xh * jnp.mean(dxh * xh, axis=-1, keepdims=True))
        dx_ref[...] = dx
        dxb_ref[...] = dx.astype(BF16)
        pg = _rows8(dy * xh)
        plo = _rows8(e * e)

        @pl.when(i == 0)
        def _():
            acc_g[...] = pg
            acc_l[...] = plo

        @pl.when(i > 0)
        def _():
            acc_g[...] += pg
            acc_l[...] += plo

        @pl.when(i == n - 1)
        def _():
            dg_ref[...] = jnp.sum(acc_g[...], axis=0, keepdims=True)
            loss_ref[...] = jnp.full((1, LANES), (0.5 / d) * jnp.sum(acc_l[...]), F32)

    row = pl.BlockSpec((tm, d), lambda i: (i, 0))
    vec = pl.BlockSpec((1, d), lambda i: (0, 0))
    return _pallas(
        body,
        name="head",
        grid=(n,),
        in_specs=[row, vec, row],
        out_specs=[row, row, vec, pl.BlockSpec((1, LANES), lambda i: (0, 0))],
        out_shape=[
            jax.ShapeDtypeStruct((s, d), F32),
            jax.ShapeDtypeStruct((s, d), BF16),
            jax.ShapeDtypeStruct((1, d), F32),
            jax.ShapeDtypeStruct((1, LANES), F32),
        ],
        scratch_shapes=[pltpu.VMEM((8, d), F32), pltpu.VMEM((8, d), F32)],
    )(x3, g, target)


def _sgu_fwd(z, gain, ws_b, b_col, w_sgu):
    s = z.shape[0]
    groups = ws_b.shape[0]

    def body(zu_ref, zv_ref, gain_ref, ws_ref, b_ref, a_ref):
        vv = _gelu(zv_ref[...].astype(F32))
        r = lax.rsqrt(jnp.mean(vv * vv, axis=-1, keepdims=True) + EPS)
        vn = ((vv * r) * gain_ref[...]).astype(BF16)
        u = _gelu(zu_ref[...].astype(F32))
        for g in range(groups):
            sl = slice(g * BLK, (g + 1) * BLK)
            mixed = jnp.dot(ws_ref[g], vn[:, sl], preferred_element_type=F32) + b_ref[g]
            a_ref[:, sl] = (u[:, sl] * mixed).astype(BF16)

    return _pallas(
        body,
        name="sgu_fwd",
        grid=(s // BLK,),
        in_specs=[
            pl.BlockSpec((BLK, w_sgu), lambda c: (c, 0)),
            pl.BlockSpec((BLK, w_sgu), lambda c: (c, 1)),
            pl.BlockSpec((1, w_sgu), lambda c: (0, 0)),
            pl.BlockSpec((groups, BLK, BLK), lambda c: (0, 0, 0)),
            pl.BlockSpec((groups, BLK, 1), lambda c: (0, 0, 0)),
        ],
        out_specs=pl.BlockSpec((BLK, w_sgu), lambda c: (c, 0)),
        out_shape=jax.ShapeDtypeStruct((s, w_sgu), BF16),
    )(z, z, gain, ws_b, b_col)


def _sgu_bwd(z, da, gain, ws_b, wst_b, b_col, w_sgu, dz, after=()):
    s = z.shape[0]
    groups = ws_b.shape[0]
    n = s // BLK
    n_skip = 1 + len(after)

    def body(zu_ref, zv_ref, da_ref, gain_ref, ws_ref, wst_ref, b_ref, *rest):
        dz_ref, dws_ref, dbs_ref, dgain_ref, acc_gain = rest[n_skip:]
        c = pl.program_id(0)
        zu = zu_ref[...].astype(F32)
        zv = zv_ref[...].astype(F32)
        gain_v = gain_ref[...]
        vv = _gelu(zv)
        r = lax.rsqrt(jnp.mean(vv * vv, axis=-1, keepdims=True) + EPS)
        xh = vv * r
        vn = (xh * gain_v).astype(BF16)
        u = _gelu(zu)
        dav = da_ref[...].astype(F32)
        dmix = dav * u
        dmix_b = dmix.astype(BF16)
        dvn_parts = []
        for g in range(groups):
            sl = slice(g * BLK, (g + 1) * BLK)
            mixed = jnp.dot(ws_ref[g], vn[:, sl], preferred_element_type=F32) + b_ref[g]
            dz_ref[:, sl] = (dav[:, sl] * mixed * _gelu_grad(zu[:, sl])).astype(BF16)
            dvn_parts.append(jnp.dot(wst_ref[g], dmix_b[:, sl], preferred_element_type=F32))
            dws_g = lax.dot_general(dmix_b[:, sl], vn[:, sl], NT, preferred_element_type=F32)
            dbs_g = jnp.sum(dmix[:, sl], axis=1, keepdims=True)

            @pl.when(c == 0)
            def _():
                dws_ref[g] = dws_g
                dbs_ref[g] = dbs_g

            @pl.when(c > 0)
            def _():
                dws_ref[g] += dws_g
                dbs_ref[g] += dbs_g

        dvn = jnp.concatenate(dvn_parts, axis=1)
        dxh = dvn * gain_v
        dvv = r * (dxh - xh * jnp.mean(dxh * xh, axis=-1, keepdims=True))
        dz_ref[:, w_sgu:] = (dvv * _gelu_grad(zv)).astype(BF16)
        pg = _rows8(dvn * xh)

        @pl.when(c == 0)
        def _():
            acc_gain[...] = pg

        @pl.when(c > 0)
        def _():
            acc_gain[...] += pg

        @pl.when(c == n - 1)
        def _():
            dgain_ref[...] = jnp.sum(acc_gain[...], axis=0, keepdims=True)

    full3 = pl.BlockSpec((groups, BLK, BLK), lambda c: (0, 0, 0))
    col3 = pl.BlockSpec((groups, BLK, 1), lambda c: (0, 0, 0))
    vec = pl.BlockSpec((1, w_sgu), lambda c: (0, 0))
    return _pallas(
        body,
        name="sgu_bwd",
        grid=(n,),
        in_specs=[
            pl.BlockSpec((BLK, w_sgu), lambda c: (c, 0)),
            pl.BlockSpec((BLK, w_sgu), lambda c: (c, 1)),
            pl.BlockSpec((BLK, w_sgu), lambda c: (c, 0)),
            vec,
            full3,
            full3,
            col3,
            ANY,
        ]
        + [ANY] * len(after),
        out_specs=[pl.BlockSpec((BLK, 2 * w_sgu), lambda c: (c, 0)), full3, col3, vec],
        out_shape=[
            jax.ShapeDtypeStruct(dz.shape, BF16),
            jax.ShapeDtypeStruct((groups, BLK, BLK), F32),
            jax.ShapeDtypeStruct((groups, BLK, 1), F32),
            jax.ShapeDtypeStruct((1, w_sgu), F32),
        ],
        scratch_shapes=[pltpu.VMEM((8, w_sgu), F32)],
        input_output_aliases={7: 0},
    )(z, z, da, gain, ws_b, wst_b, b_col, dz, *after)


def _attn_softmax(sink_ref, q_ref, k_ref, v_ref, bias_ref, s_len, grp):
    kv = pl.program_id(0)
    n = pl.program_id(1)
    start = pl.multiple_of(n * BLK, BLK)
    kb = k_ref[pl.ds(start, 3 * BLK), :]
    vb = v_ref[pl.ds(start, 3 * BLK), :]
    qv = q_ref[...]
    qs = jnp.concatenate([qv[:, g * HEAD_DIM : (g + 1) * HEAD_DIM] for g in range(grp)], axis=0).astype(BF16)
    sc = lax.dot_general(qs, kb, NT, preferred_element_type=F32) * (HEAD_DIM**-0.5)
    sc = sc + bias_ref[...].reshape(grp * BLK, 3 * BLK)
    kpos = start + lax.broadcasted_iota(I32, (1, 3 * BLK), 1) - BLK
    sc = jnp.where((kpos >= 0) & (kpos < s_len), sc, NEG)
    sink = jnp.concatenate([jnp.full((BLK, 1), sink_ref[kv * grp + g], F32) for g in range(grp)], axis=0)
    m = jnp.maximum(jnp.max(sc, axis=-1, keepdims=True), sink)
    p = jnp.exp(sc - m)
    esink = jnp.exp(sink - m)
    den = jnp.sum(p, axis=-1, keepdims=True) + esink
    return start, qs, kb, vb, p / den, esink / den


def _attn_specs(s, grp, q_blk0):
    qw = grp * HEAD_DIM
    return [
        pl.BlockSpec(memory_space=pltpu.SMEM),
        pl.BlockSpec((BLK, qw), lambda kv, n: (n, q_blk0 + kv)),
        pl.BlockSpec((s + 2 * BLK, HEAD_DIM), lambda kv, n: (0, kv)),
        pl.BlockSpec((s + 2 * BLK, HEAD_DIM), lambda kv, n: (0, kv)),
        pl.BlockSpec((grp, BLK, 3 * BLK), lambda kv, n: (kv, 0, 0)),
    ]


def _attn_fwd(sink, z, k_pad, v_pad, bias_tab, grp, q_blk0):
    s = z.shape[0]
    qw = grp * HEAD_DIM

    def body(sink_ref, q_ref, k_ref, v_ref, bias_ref, o_ref):
        _, _, _, vb, pn, _ = _attn_softmax(sink_ref, q_ref, k_ref, v_ref, bias_ref, s, grp)
        o = jnp.dot(pn.astype(BF16), vb, preferred_element_type=F32)
        for g in range(grp):
            o_ref[:, g * HEAD_DIM : (g + 1) * HEAD_DIM] = o[g * BLK : (g + 1) * BLK].astype(BF16)

    return _pallas(
        body,
        name="attn_fwd",
        grid=(N_KV_HEADS, s // BLK),
        in_specs=_attn_specs(s, grp, q_blk0),
        out_specs=pl.BlockSpec((BLK, qw), lambda kv, n: (n, kv)),
        out_shape=jax.ShapeDtypeStruct((s, N_KV_HEADS * qw), BF16),
    )(sink, z, k_pad, v_pad, bias_tab)


def _attn_bwd(sink, z, k_pad, v_pad, bias_tab, dout, dz, grp, q_blk0):
    s = z.shape[0]
    qw = grp * HEAD_DIM
    nb = s // BLK
    heads = N_KV_HEADS * grp

    def body(sink_ref, q_ref, k_ref, v_ref, bias_ref, do_ref, dz_in, dq_ref, dk_ref, dv_ref, dbias_ref, dsink_ref, dk_acc, dv_acc):
        del dz_in
        kv = pl.program_id(0)
        n = pl.program_id(1)
        start, qs, kb, vb, pn, psink = _attn_softmax(sink_ref, q_ref, k_ref, v_ref, bias_ref, s, grp)
        dov = do_ref[...]
        dos = jnp.concatenate([dov[:, g * HEAD_DIM : (g + 1) * HEAD_DIM] for g in range(grp)], axis=0)
        dp = lax.dot_general(dos, vb, NT, preferred_element_type=F32)
        dvb = lax.dot_general(pn.astype(BF16), dos, TN, preferred_element_type=F32)
        delta = jnp.sum(pn * dp, axis=-1, keepdims=True)
        ds = pn * (dp - delta)
        dsb = (ds * (HEAD_DIM**-0.5)).astype(BF16)
        dq = jnp.dot(dsb, kb, preferred_element_type=F32)
        dkb = lax.dot_general(dsb, qs, TN, preferred_element_type=F32)
        for g in range(grp):
            dq_ref[:, g * HEAD_DIM : (g + 1) * HEAD_DIM] = dq[g * BLK : (g + 1) * BLK].astype(BF16)

        @pl.when(n == 0)
        def _():
            dk_acc[...] = jnp.zeros_like(dk_acc)
            dv_acc[...] = jnp.zeros_like(dv_acc)
            dbias_ref[...] = jnp.zeros_like(dbias_ref)

        @pl.when((n == 0) & (kv == 0))
        def _():
            dsink_ref[...] = jnp.zeros_like(dsink_ref)

        dk_acc[pl.ds(start, 3 * BLK), :] += dkb
        dv_acc[pl.ds(start, 3 * BLK), :] += dvb
        dbias_ref[...] += ds.reshape(grp, BLK, 3 * BLK)
        row = lax.broadcasted_iota(I32, (heads, LANES), 0)
        sd = psink * delta
        upd = jnp.zeros((heads, LANES), F32)
        for g in range(grp):
            upd = jnp.where(row == kv * grp + g, -jnp.sum(sd[g * BLK : (g + 1) * BLK]), upd)
        dsink_ref[...] += upd

        @pl.when(n == nb - 1)
        def _():
            dk_ref[...] = dk_acc[...]
            dv_ref[...] = dv_acc[...]

    pad_spec = pl.BlockSpec((s + 2 * BLK, HEAD_DIM), lambda kv, n: (0, kv))
    kvw = N_KV_HEADS * HEAD_DIM
    return _pallas(
        body,
        name="attn_bwd",
        grid=(N_KV_HEADS, nb),
        in_specs=_attn_specs(s, grp, q_blk0) + [pl.BlockSpec((BLK, qw), lambda kv, n: (n, kv)), ANY],
        out_specs=[
            pl.BlockSpec((BLK, qw), lambda kv, n: (n, q_blk0 + kv)),
            pad_spec,
            pad_spec,
            pl.BlockSpec((grp, BLK, 3 * BLK), lambda kv, n: (kv, 0, 0)),
            pl.BlockSpec((heads, LANES), lambda kv, n: (0, 0)),
        ],
        out_shape=[
            jax.ShapeDtypeStruct(dz.shape, BF16),
            jax.ShapeDtypeStruct((s + 2 * BLK, kvw), F32),
            jax.ShapeDtypeStruct((s + 2 * BLK, kvw), F32),
            jax.ShapeDtypeStruct((heads, BLK, 3 * BLK), F32),
            jax.ShapeDtypeStruct((heads, LANES), F32),
        ],
        scratch_shapes=[pltpu.VMEM((s + 2 * BLK, HEAD_DIM), F32), pltpu.VMEM((s + 2 * BLK, HEAD_DIM), F32)],
        input_output_aliases={6: 0},
    )(sink, z, k_pad, v_pad, bias_tab, dout, dz)


def _dkv_to_dz(dk_pad, dv_pad, dz, blk_idx):
    s = dz.shape[0]
    kvw = dk_pad.shape[1]

    def body(dk_ref, dv_ref, dz_in, out_ref):
        del dz_in
        out_ref[:, :kvw] = dk_ref[...].astype(BF16)
        out_ref[:, kvw:] = dv_ref[...].astype(BF16)

    src = pl.BlockSpec((BLK, kvw), lambda i: (i + 1, 0))
    return _pallas(
        body,
        name="dkv_to_dz",
        grid=(s // BLK,),
        in_specs=[src, src, ANY],
        out_specs=pl.BlockSpec((BLK, 2 * kvw), lambda i: (i, blk_idx)),
        out_shape=jax.ShapeDtypeStruct(dz.shape, BF16),
        input_output_aliases={2: 0},
    )(dk_pad, dv_pad, dz)


def _relbias_bwd(dbias_tab, bucket):
    heads = dbias_tab.shape[0]

    def body(dt_ref, bk_ref, out_ref):
        lane = lax.broadcasted_iota(I32, (1, LANES), 1)
        bk = bk_ref[...]
        rows = []
        for h in range(heads):
            dt = dt_ref[h]
            acc = jnp.zeros((1, LANES), F32)
            for b in range(REL_BUCKETS):
                acc = jnp.where(lane == b, jnp.sum(jnp.where(bk == b, dt, 0.0)), acc)
            rows.append(acc)
        out_ref[...] = jnp.concatenate(rows, axis=0)

    return _pallas(body, name="relbias_bwd", out_shape=jax.ShapeDtypeStruct((heads, LANES), F32))(dbias_tab, bucket)


def _t5_bucket(rel):
    nb = REL_BUCKETS // 2
    ret = jnp.where(rel > 0, nb, 0)
    n = jnp.abs(rel)
    max_exact = nb // 2
    nf = jnp.maximum(n, 1).astype(F32)
    large = max_exact + (jnp.log(nf / max_exact) / math.log(REL_MAX_DIST / max_exact) * (nb - max_exact)).astype(I32)
    large = jnp.minimum(large, nb - 1)
    return ret + jnp.where(n < max_exact, n, large)


def _band_tables(rel_bias):
    qi = jnp.arange(BLK)[:, None]
    kj = jnp.arange(3 * BLK)[None, :]
    rel = kj - BLK - qi
    bucket = _t5_bucket(rel).astype(I32)
    heads = rel_bias.shape[1]
    masked = jnp.where(jnp.abs(rel) <= BLK, bucket, -1)

    def body(rb_ref, bk_ref, out_ref):
        bk = bk_ref[...]
        for h in range(heads):
            tab = jnp.full(bk.shape, NEG, F32)
            for b in range(REL_BUCKETS):
                tab = jnp.where(bk == b, rb_ref[b, h], tab)
            out_ref[h] = tab

    bias_tab = _pallas(
        body,
        name="bias_table",
        in_specs=[pl.BlockSpec(memory_space=pltpu.SMEM), pl.BlockSpec(memory_space=pltpu.VMEM)],
        out_specs=pl.BlockSpec(memory_space=pltpu.VMEM),
        out_shape=jax.ShapeDtypeStruct((heads, BLK, 3 * BLK), F32),
    )(rel_bias.astype(F32), masked)
    return bias_tab, bucket


EW_BLOCK_ELEMS = 512 * 1024


def _ew_tiles(shape, elems=EW_BLOCK_ELEMS // 2):
    r, c = shape
    tn = c if c <= 2048 else _tile(c, (2048, 1920, 1536, 1408, 1024, 512))
    tm = _tile(r, [t for t in (1024, 512, 256, 128, 64, 32, 16, 8) if t * tn <= elems] or [8])
    return tm, tn


def _cast_into_full(name, qidx, w, kind, after=()):
    r, c = w.shape
    tm, tn = _ew_tiles(w.shape, EW_BLOCK_ELEMS)
    nbi, nbj = r // tm, c // tn
    if kind == "col":
        full, out_spec = (r, c * N_CHIPS), pl.BlockSpec((tm, tn), lambda i, j, q: (i, q[0] * nbj + j))
    else:
        full, out_spec = (r * N_CHIPS, c), pl.BlockSpec((tm, tn), lambda i, j, q: (q[0] * nbi + i, j))

    def body(q_ref, w_ref, *rest):
        del q_ref
        rest[-1][...] = w_ref[...].astype(BF16)

    return _pallas(
        body,
        name=name,
        grid_spec=pltpu.PrefetchScalarGridSpec(
            num_scalar_prefetch=1,
            grid=(nbi, nbj),
            in_specs=[pl.BlockSpec((tm, tn), lambda i, j, q: (i, j))] + [ANY] * len(after),
            out_specs=out_spec,
        ),
        out_shape=jax.ShapeDtypeStruct(full, BF16),
    )(qidx, w, *after)


def _adamw(name, w, g, m, v, after=()):
    tm, tn = _ew_tiles(w.shape, EW_BLOCK_ELEMS)
    if _nbytes(w.shape, F32) <= 1024 * 1024:
        tm, tn = w.shape
    spec = pl.BlockSpec((tm, tn), lambda i, j: (i, j))
    n_after = len(after)

    def body(w_ref, g_ref, m_ref, v_ref, *rest):
        d_ref, nm_ref, nv_ref, g_out_ref = rest[n_after:]
        gv = g_ref[...]
        g_out_ref[...] = gv
        nm = ADAM_B1 * m_ref[...] + (1.0 - ADAM_B1) * gv
        nv = ADAM_B2 * v_ref[...] + (1.0 - ADAM_B2) * (gv * gv)
        m_hat = nm / (1.0 - ADAM_B1**ADAM_STEP)
        v_hat = nv / (1.0 - ADAM_B2**ADAM_STEP)
        d_ref[...] = -ADAM_LR * (m_hat / (jnp.sqrt(v_hat) + ADAM_EPS) + ADAM_WD * w_ref[...])
        nm_ref[...] = nm
        nv_ref[...] = nv

    out = jax.ShapeDtypeStruct(w.shape, F32)
    return _pallas(
        body, name=name, grid=(w.shape[0] // tm, w.shape[1] // tn), in_specs=[spec] * 4 + [ANY] * n_after,
        out_specs=[spec] * 4, out_shape=[out, out, out, out],
        compiler_params=_params(_mm_vmem([((tm, tn), F32, 24)])),
    )(w, g, m, v, *after)


def _pair_add(name, cidx, g_full, r_sib, kind):
    hr, hc = r_sib.shape
    tm, tn = _ew_tiles((hr, hc), 2 * EW_BLOCK_ELEMS)
    nbi, nbj = hr // tm, hc // tn
    if kind == "col":
        g_spec = pl.BlockSpec((tm, tn), lambda i, j, c: (c[0] * nbi + i, j))
    else:
        g_spec = pl.BlockSpec((tm, tn), lambda i, j, c: (i, c[0] * nbj + j))
    spec = pl.BlockSpec((tm, tn), lambda i, j, c: (i, j))

    def body(c_ref, g_ref, r_ref, o_ref):
        del c_ref
        o_ref[...] = (g_ref[...].astype(F32) + r_ref[...].astype(F32)).astype(BF16)

    return _pallas(
        body,
        name=name,
        grid_spec=pltpu.PrefetchScalarGridSpec(num_scalar_prefetch=1, grid=(nbi, nbj), in_specs=[g_spec, spec], out_specs=spec),
        out_shape=jax.ShapeDtypeStruct((hr, hc), BF16),
        compiler_params=_params(_mm_vmem([((tm, tn), BF16, 6), ((tm, tn), F32, 3)])),
    )(cidx, g_full, r_sib)


def _chip_sum(name, qidx, c_half, r_ici, kind):
    _, pr, pc = r_ici.shape
    tm, tn = _ew_tiles((pr, pc), 2 * EW_BLOCK_ELEMS)
    nbi, nbj = pr // tm, pc // tn
    if kind == "col":
        own_spec = pl.BlockSpec((tm, tn), lambda i, j, q: (i, q[0] * nbj + j))
        full, out_spec = (2 * pr, pc), pl.BlockSpec((tm, tn), lambda i, j, q: (q[1] * nbi + i, j))
    else:
        own_spec = pl.BlockSpec((tm, tn), lambda i, j, q: (q[0] * nbi + i, j))
        full, out_spec = (pr, 2 * pc), pl.BlockSpec((tm, tn), lambda i, j, q: (i, q[1] * nbj + j))

    def body(q_ref, own_ref, r_ref, o_ref):
        q = q_ref[0]
        own = own_ref[...].astype(F32)
        recv = [r_ref[r].astype(F32) for r in range(3)]
        total = None
        for chip in range(N_CHIPS):
            d = chip ^ q
            term = jnp.where(d == 0, own, jnp.where(d == 2, recv[0], jnp.where(d == 1, recv[1], recv[2])))
            total = term if total is None else total + term
        o_ref[...] = total

    return _pallas(
        body,
        name=name,
        grid_spec=pltpu.PrefetchScalarGridSpec(
            num_scalar_prefetch=1,
            grid=(nbi, nbj),
            in_specs=[own_spec, pl.BlockSpec((3, tm, tn), lambda i, j, q: (0, i, j))],
            out_specs=out_spec,
        ),
        out_shape=jax.ShapeDtypeStruct(full, F32),
        compiler_params=_params(_mm_vmem([((tm, tn), BF16, 8), ((tm, tn), F32, 6)])),
    )(qidx, c_half, r_ici)


_REL_MASK = (2, 1, 3)


def _place():
    x, y, c = lax.axis_index("x"), lax.axis_index("y"), lax.axis_index("c")
    chips = [(1 - x, y), (x, 1 - y), (1 - x, 1 - y)]
    return x, y, c, 2 * x + y, chips


def _shard_view(ref, kind, chip):
    if kind == "col":
        w = ref.shape[1] // N_CHIPS
        return ref.at[:, pl.ds(pl.multiple_of(chip * w, LANES), w)]
    h = ref.shape[0] // N_CHIPS
    return ref.at[pl.ds(pl.multiple_of(chip * h, 16), h), :]


def _row_half(ref, half):
    h = ref.shape[0] // 2
    return ref.at[pl.ds(pl.multiple_of(half * h, 16), h), :]


def _pair_half(ref, kind, half):
    if kind == "col":
        return _row_half(ref, half)
    w = ref.shape[1] // 2
    return ref.at[:, pl.ds(pl.multiple_of(half * w, LANES), w)]


def _remote(src, dst, send_sem, recv_sem, dev):
    return pltpu.make_async_remote_copy(src_ref=src, dst_ref=dst, send_sem=send_sem, recv_sem=recv_sem, device_id=dev, device_id_type=MESH)


def _hbm(a):
    return pltpu.with_memory_space_constraint(a, pltpu.HBM)


def _gather_start(name, fulls, kinds):
    n_w = len(fulls)

    def body(*refs):
        g = refs[:n_w]
        send_sem, recv_sem = refs[n_w], refs[n_w + 1]
        token = refs[-1]
        _, _, c, q, chips = _place()
        for w in range(n_w):
            mine = _row_half(_shard_view(g[w], kinds[w], q), c)
            for r, chip in enumerate(chips):
                _remote(mine, mine, send_sem.at[3 * w + r], recv_sem.at[3 * w + r], (*chip, c)).start()
        token[...] = jnp.zeros_like(token)

    res = _pallas(
        body,
        name=name,
        out_shape=(
            pltpu.SemaphoreType.DMA((3 * n_w,)),
            pltpu.SemaphoreType.DMA((3 * n_w,)),
            *[pltpu.HBM(f.shape, f.dtype) for f in fulls],
            jax.ShapeDtypeStruct((8, LANES), F32),
        ),
        in_specs=[HBM_SPEC] * n_w,
        out_specs=(SEM_SPEC, SEM_SPEC, *[HBM_SPEC] * n_w, pl.BlockSpec(memory_space=pltpu.VMEM)),
        input_output_aliases={w: w + 2 for w in range(n_w)},
        compiler_params=pltpu.CompilerParams(has_side_effects=EFFECT),
    )(*[_hbm(f) for f in fulls])
    return res[0], res[1], list(res[2 : 2 + n_w]), res[-1]


def _gather_wait(name, fulls, kinds, w_ids, send_sem, recv_sem, after):
    n = len(fulls)

    def body(*refs):
        g = refs[:n]
        s_sem, r_sem = refs[n], refs[n + 1]
        x, y, c, q, _ = _place()
        for i, w in enumerate(w_ids):
            mine = _row_half(_shard_view(g[i], kinds[i], q), c)
            for r in range(3):
                landed = _row_half(_shard_view(g[i], kinds[i], q ^ _REL_MASK[r]), c)
                cp = _remote(mine, landed, s_sem.at[3 * w + r], r_sem.at[3 * w + r], (x, y, 1 - c))
                cp.wait_send()
                cp.wait_recv()

    res = _pallas(
        body,
        name=name,
        out_shape=[pltpu.HBM(f.shape, f.dtype) for f in fulls],
        in_specs=[HBM_SPEC] * n + [SEM_SPEC, SEM_SPEC, ANY],
        out_specs=[HBM_SPEC] * n,
        input_output_aliases={i: i for i in range(n)},
        compiler_params=pltpu.CompilerParams(has_side_effects=EFFECT),
    )(*fulls, send_sem, recv_sem, after)
    return list(res)


def _gather_forward(name, fulls, kinds):
    n = len(fulls)

    def body(*refs):
        g = refs[n : 2 * n]
        send, recv = refs[2 * n :]
        x, y, c, q, _ = _place()
        sib = (x, y, 1 - c)
        cps = []
        for i in range(n):
            for r in range(3):
                landed = _row_half(_shard_view(g[i], kinds[i], q ^ _REL_MASK[r]), c)
                cps.append(_remote(landed, landed, send.at[i, r], recv.at[i, r], sib))
        for cp in cps:
            cp.start()
        for i in range(n):
            for r in range(3):
                other = _row_half(_shard_view(g[i], kinds[i], q ^ _REL_MASK[r]), 1 - c)
                _remote(other, other, send.at[i, r], recv.at[i, r], sib).wait_recv()
        for cp in cps:
            cp.wait_send()

    res = _pallas(
        body,
        name=name,
        in_specs=[ANY] * n,
        out_specs=[ANY] * n,
        out_shape=[jax.ShapeDtypeStruct(f.shape, f.dtype) for f in fulls],
        scratch_shapes=[pltpu.SemaphoreType.DMA((n, 3)), pltpu.SemaphoreType.DMA((n, 3))],
        input_output_aliases={i: i for i in range(n)},
    )(*fulls)
    return list(res)


def _split_start(name, bufs, n_sems, copies):
    n = len(bufs)

    def body(*refs):
        for cp in copies(refs[:n], refs[n], refs[n + 1]):
            cp.start()

    res = _pallas(
        body,
        name=name,
        out_shape=(
            pltpu.SemaphoreType.DMA((n_sems,)),
            pltpu.SemaphoreType.DMA((n_sems,)),
            *[pltpu.HBM(b.shape, b.dtype) for b in bufs],
        ),
        in_specs=[HBM_SPEC] * n,
        out_specs=(SEM_SPEC, SEM_SPEC, *[HBM_SPEC] * n),
        input_output_aliases={i: i + 2 for i in range(n)},
        compiler_params=pltpu.CompilerParams(has_side_effects=EFFECT),
    )(*[_hbm(b) for b in bufs])
    return res[0], res[1], list(res[2:])


def _split_wait(name, bufs, send_sem, recv_sem, copies, after):
    n = len(bufs)

    def body(*refs):
        for cp in copies(refs[:n], refs[n], refs[n + 1]):
            cp.wait_send()
            cp.wait_recv()

    res = _pallas(
        body,
        name=name,
        out_shape=[pltpu.HBM(b.shape, b.dtype) for b in bufs],
        in_specs=[HBM_SPEC] * n + [SEM_SPEC, SEM_SPEC, ANY],
        out_specs=[HBM_SPEC] * n,
        input_output_aliases={i: i for i in range(n)},
        compiler_params=pltpu.CompilerParams(has_side_effects=EFFECT),
    )(*bufs, send_sem, recv_sem, after)
    return list(res)


def _pair_exchange_copies(kinds):
    n = len(kinds)

    def copies(refs, send_sem, recv_sem):
        x, y, c, _, _ = _place()
        return [
            _remote(_pair_half(refs[w], kinds[w], 1 - c), refs[n + w], send_sem.at[w], recv_sem.at[w], (x, y, 1 - c))
            for w in range(n)
        ]

    return copies


def _pair_share_copies(kinds, waiting):
    def copies(refs, send_sem, recv_sem):
        x, y, c, _, _ = _place()
        out = []
        for w, kind in enumerate(kinds):
            mine = _pair_half(refs[w], kind, c)
            dst = _pair_half(refs[w], kind, 1 - c) if waiting else mine
            out.append(_remote(mine, dst, send_sem.at[w], recv_sem.at[w], (x, y, 1 - c)))
        return out

    return copies


def _piece_shape(half_shape, kind):
    r, c = half_shape
    return (3, r, c // N_CHIPS) if kind == "col" else (3, r // N_CHIPS, c)


def _chip_send_start(name, halves, kinds):
    n = len(halves)
    lands = [lax.empty(_piece_shape(h.shape, k), BF16) for h, k in zip(halves, kinds)]

    def body(*refs):
        h, land = refs[:n], refs[n : 2 * n]
        send_sem, recv_sem = refs[2 * n], refs[2 * n + 1]
        _, _, c, q, chips = _place()
        for i in range(n):
            for r, chip in enumerate(chips):
                piece = _shard_view(h[i], kinds[i], q ^ _REL_MASK[r])
                _remote(piece, land[i].at[r], send_sem.at[3 * i + r], recv_sem.at[3 * i + r], (*chip, c)).start()

    res = _pallas(
        body,
        name=name,
        out_shape=(
            pltpu.SemaphoreType.DMA((3 * n,)),
            pltpu.SemaphoreType.DMA((3 * n,)),
            *[pltpu.HBM(a.shape, a.dtype) for a in halves],
            *[pltpu.HBM(a.shape, a.dtype) for a in lands],
        ),
        in_specs=[HBM_SPEC] * (2 * n),
        out_specs=(SEM_SPEC, SEM_SPEC, *[HBM_SPEC] * (2 * n)),
        input_output_aliases={i: i + 2 for i in range(2 * n)},
        compiler_params=pltpu.CompilerParams(has_side_effects=EFFECT),
    )(*[_hbm(a) for a in halves], *[_hbm(a) for a in lands])
    return res[0], res[1], list(res[2 : 2 + n]), list(res[2 + n :])


def _chip_send_wait(name, halves, lands, kinds, send_sem, recv_sem, after):
    n = len(halves)

    def body(*refs):
        h, land = refs[:n], refs[n : 2 * n]
        s_sem, r_sem = refs[2 * n], refs[2 * n + 1]
        x, y, c, q, _ = _place()
        for i in range(n):
            for r in range(3):
                piece = _shard_view(h[i], kinds[i], q ^ _REL_MASK[r])
                cp = _remote(piece, land[i].at[r], s_sem.at[3 * i + r], r_sem.at[3 * i + r], (x, y, 1 - c))
                cp.wait_send()
                cp.wait_recv()

    res = _pallas(
        body,
        name=name,
        out_shape=[pltpu.HBM(a.shape, a.dtype) for a in halves] + [pltpu.HBM(a.shape, a.dtype) for a in lands],
        in_specs=[HBM_SPEC] * (2 * n) + [SEM_SPEC, SEM_SPEC, ANY],
        out_specs=[HBM_SPEC] * (2 * n),
        input_output_aliases={i: i for i in range(2 * n)},
        compiler_params=pltpu.CompilerParams(has_side_effects=EFFECT),
    )(*halves, *lands, send_sem, recv_sem, after)
    return list(res[:n]), list(res[n:])


def _small_all_reduce(p):
    rows = p.shape[0]
    n_dev = 2 * N_CHIPS

    def body(p_ref, o_ref, buf, loc_sem, send, recv):
        x, y, c, q, _ = _place()
        me = 2 * q + c
        own = pltpu.make_async_copy(p_ref, buf.at[me], loc_sem)
        own.start()
        cps = []
        for d in range(1, n_dev):
            dev = (x ^ ((d >> 2) & 1), y ^ ((d >> 1) & 1), c ^ (d & 1))
            cps.append(_remote(p_ref, buf.at[me], send.at[d - 1], recv.at[d - 1], dev))
        for cp in cps:
            cp.start()
        for d in range(1, n_dev):
            slot = buf.at[me ^ d]
            _remote(slot, slot, send.at[d - 1], recv.at[d - 1], (x, y, c)).wait_recv()
        own.wait()
        total = buf[0]
        for d in range(1, n_dev):
            total = total + buf[d]
        o_ref[...] = total
        for cp in cps:
            cp.wait_send()

    return _pallas(
        body,
        name="small_all_reduce",
        in_specs=[ANY],
        out_specs=pl.BlockSpec(memory_space=pltpu.VMEM),
        out_shape=jax.ShapeDtypeStruct(p.shape, F32),
        scratch_shapes=[
            pltpu.VMEM((n_dev, rows, LANES), F32),
            pltpu.SemaphoreType.DMA,
            pltpu.SemaphoreType.DMA((n_dev - 1,)),
            pltpu.SemaphoreType.DMA((n_dev - 1,)),
        ],
    )(p)


def _small_exchange_copies(waiting):
    def copies(refs, send_sem, recv_sem):
        p, land = refs
        x, y, c, q, _ = _place()
        me = 2 * q + c
        out = []
        for dd in range(1, 2 * N_CHIPS):
            dev = (x ^ ((dd >> 2) & 1), y ^ ((dd >> 1) & 1), c ^ (dd & 1))
            dst = land.at[me ^ dd] if waiting else land.at[me]
            out.append(_remote(p, dst, send_sem.at[dd - 1], recv_sem.at[dd - 1], dev))
        return out

    return copies


def _small_sum(name, me_idx, p, land):
    rows = p.shape[0]
    n_dev = 2 * N_CHIPS

    def body(me_ref, p_ref, land_ref, o_ref):
        me = me_ref[0]
        total = None
        for dev in range(n_dev):
            term = jnp.where(me == dev, p_ref[...], land_ref[dev])
            total = term if total is None else total + term
        o_ref[...] = total

    return _pallas(
        body,
        name=name,
        grid_spec=pltpu.PrefetchScalarGridSpec(
            num_scalar_prefetch=1,
            grid=(1,),
            in_specs=[pl.BlockSpec((rows, LANES), lambda i, m: (0, 0)), pl.BlockSpec((n_dev, rows, LANES), lambda i, m: (0, 0, 0))],
            out_specs=pl.BlockSpec((rows, LANES), lambda i, m: (0, 0)),
        ),
        out_shape=jax.ShapeDtypeStruct(p.shape, F32),
    )(me_idx, p, land)


def _pack(parts):
    rows = []
    for a in parts:
        flat = a.reshape(-1).astype(F32)
        n = flat.shape[0]
        padded = -(-n // (8 * LANES)) * (8 * LANES)
        rows.append(jnp.pad(flat, (0, padded - n)).reshape(-1, LANES))
    return jnp.concatenate(rows, axis=0)


def _unpack(packed, shapes):
    out, row = [], 0
    for shp in shapes:
        n = int(np.prod(shp))
        nrows = -(-n // (8 * LANES)) * 8
        out.append(packed[row : row + nrows].reshape(-1)[:n].reshape(shp))
        row += nrows
    return out


def kernel(x, w_in, norm_mix, sgu_v_gain, sgu_w_s, sgu_b_s, w_a_out, attn_sink, rel_bias, w_b_out, w_o, norm_ffn, w_gate, w_up, w_down, norm_final, loss_target, m_w_in, m_norm_mix, m_sgu_v_gain, m_sgu_w_s, m_sgu_b_s, m_w_a_out, m_attn_sink, m_rel_bias, m_w_b_out, m_w_o, m_norm_ffn, m_w_gate, m_w_up, m_w_down, m_norm_final, v_w_in, v_norm_mix, v_sgu_v_gain, v_sgu_w_s, v_sgu_b_s, v_w_a_out, v_attn_sink, v_rel_bias, v_w_b_out, v_w_o, v_norm_ffn, v_w_gate, v_w_up, v_w_down, v_norm_final):
    s, d = x.shape[1], x.shape[2]
    w_sgu = sgu_v_gain.shape[1]
    groups = sgu_w_s.shape[1]
    heads = attn_sink.shape[1]
    grp = heads // N_KV_HEADS
    w_att = heads * HEAD_DIM
    w_kv = N_KV_HEADS * HEAD_DIM
    d_ff = w_gate.shape[2] * N_CHIPS
    n_in = w_in.shape[2] * N_CHIPS
    off_q = 2 * w_sgu
    off_k = off_q + w_att
    off_g = off_k + 2 * w_kv
    assert n_in == off_g + 2 * d and groups * BLK == w_sgu and s % BLK == 0

    x2d = x.reshape(s, d)
    tgt = loss_target.reshape(s, d)
    c_idx = lax.axis_index("c").astype(I32).reshape(1)
    q_idx = (2 * lax.axis_index("x") + lax.axis_index("y")).astype(I32).reshape(1)
    qc_idx = jnp.concatenate([q_idx, c_idx])

    W_IN, W_A, W_B, W_O, W_GATE, W_UP, W_DOWN = range(7)
    names = ["w_in", "w_a", "w_b", "w_o", "w_gate", "w_up", "w_down"]
    kinds = ["col", "col", "col", "row", "col", "col", "row"]
    big_w = [w_in[0], w_a_out[0], w_b_out[0], w_o[0], w_gate[0], w_up[0], w_down[0]]
    big_m = [m_w_in[0], m_w_a_out[0], m_w_b_out[0], m_w_o[0], m_w_gate[0], m_w_up[0], m_w_down[0]]
    big_v = [v_w_in[0], v_w_a_out[0], v_w_b_out[0], v_w_o[0], v_w_gate[0], v_w_up[0], v_w_down[0]]
    full_in = _cast_into_full("cast_w_in", q_idx, big_w[W_IN], kinds[W_IN])
    in_send, in_recv, (full_in,), token = _gather_start("gather_start_in", [full_in], [kinds[W_IN]])
    rest = [_cast_into_full("cast_" + names[i], q_idx, big_w[i], kinds[i], after=(token,)) for i in range(1, 7)]
    ag_send, ag_recv, rest, token = _gather_start("gather_start_rest", rest, kinds[1:])
    fulls = [full_in] + rest

    def gathered(tag, ids, after):
        if ids == [W_IN]:
            sems, pos = (in_send, in_recv), [0]
        else:
            sems, pos = (ag_send, ag_recv), [i - 1 for i in ids]
        got = _gather_wait("gather_wait_" + tag, [fulls[i] for i in ids], [kinds[i] for i in ids], pos, *sems, after)
        return _gather_forward("gather_fwd_" + tag, got, [kinds[i] for i in ids])

    ws_b = sgu_w_s[0].astype(BF16)
    wst_b = jnp.swapaxes(sgu_w_s[0], 1, 2).astype(BF16)
    b_col = sgu_b_s[0].reshape(groups, BLK, 1)
    bias_tab, bucket = _band_tables(rel_bias)
    sink = attn_sink[0]

    tm = _tile(s, (1024, 512, 256, 128))

    h1 = _rms_fwd("rms_mix", x2d, norm_mix, after=(token,))
    (g_in,) = gathered("in", [W_IN], h1)

    tn = _tile(n_in, (768, 640, 512))
    z = _mm(
        "mm_z", (s // tm, n_in // tn, 1), [h1, g_in],
        [pl.BlockSpec((tm, d), lambda i, j, k: (i, 0)), pl.BlockSpec((d, tn), lambda i, j, k: (0, j))],
        [jax.ShapeDtypeStruct((s, n_in), BF16)], [pl.BlockSpec((tm, tn), lambda i, j, k: (i, j))],
        [(0, 1, NN, 0)], 1, (tm, tn), 1, lambda ins, vals, outs, cs: _put(outs[0], cs, vals[0]),
        _mm_vmem([((tm, d), BF16, 2), ((d, tn), BF16, 2), ((tm, tn), F32, 3)]),
    )[0]
    g_a, g_b, g_o = gathered("mix", [W_A, W_B, W_O], z)

    a_act = _sgu_fwd(z, sgu_v_gain, ws_b, b_col, w_sgu)

    kv_b = z[:, off_k:off_g]
    k_pad = jnp.pad(kv_b[:, :w_kv], ((BLK, BLK), (0, 0)))
    v_pad = jnp.pad(kv_b[:, w_kv:], ((BLK, BLK), (0, 0)))
    q_blk0 = off_q // (grp * HEAD_DIM)
    att = _attn_fwd(sink, z, k_pad, v_pad, bias_tab, grp, q_blk0)

    tg = _tile(d, (512,))
    ga0, gb0 = off_g // tg, (off_g + d) // tg

    def ep_gate(ins, vals, outs, cs):
        sa, sb = _sigmoid(ins[4][:, cs].astype(F32)), _sigmoid(ins[5][:, cs].astype(F32))
        _put(outs[0], cs, sa * vals[0] + sb * vals[1])
        _put(outs[1], cs, vals[0])
        _put(outs[2], cs, vals[1])

    t_out = pl.BlockSpec((tm, tg), lambda i, j, k: (i, j))
    m_act, y_a, y_b = _mm(
        "mm_branches", (s // tm, d // tg, 1), [a_act, g_a, att, g_b, z, z],
        [pl.BlockSpec((tm, w_sgu), lambda i, j, k: (i, 0)), pl.BlockSpec((w_sgu, tg), lambda i, j, k: (0, j)),
         pl.BlockSpec((tm, w_att), lambda i, j, k: (i, 0)), pl.BlockSpec((w_att, tg), lambda i, j, k: (0, j)),
         pl.BlockSpec((tm, tg), lambda i, j, k: (i, ga0 + j)), pl.BlockSpec((tm, tg), lambda i, j, k: (i, gb0 + j))],
        [jax.ShapeDtypeStruct((s, d), BF16)] * 3,
        [t_out, t_out, t_out], [(0, 1, NN, 0), (2, 3, NN, 1)], 2, (tm, tg), 1, ep_gate,
        _mm_vmem([((tm, w_sgu), BF16, 4), ((w_sgu, tg), BF16, 4), ((tm, tg), F32, 12)]),    )

    tn = _tile(d, (1024, 512))

    def ep_residual(ins, vals, outs, cs):
        _put(outs[0], cs, ins[2][:, cs] + vals[0])

    x2 = _mm(
        "mm_wo", (s // tm, d // tn, 1), [m_act, g_o, x2d],
        [pl.BlockSpec((tm, d), lambda i, j, k: (i, 0)), pl.BlockSpec((d, tn), lambda i, j, k: (0, j)),
         pl.BlockSpec((tm, tn), lambda i, j, k: (i, j))],
        [jax.ShapeDtypeStruct((s, d), F32)], [pl.BlockSpec((tm, tn), lambda i, j, k: (i, j))],
        [(0, 1, NN, 0)], 1, (tm, tn), 1, ep_residual,
        _mm_vmem([((tm, d), BF16, 2), ((d, tn), BF16, 2), ((tm, tn), F32, 5)]),    )[0]

    h2 = _rms_fwd("rms_ffn", x2, norm_ffn)
    (g_gate,) = gathered("gate", [W_GATE], h2)
    (g_up,) = gathered("up", [W_UP], g_gate)

    tf = _tile(d_ff, (512,))

    def ep_swiglu(ins, vals, outs, cs):
        gt, up = vals
        _put(outs[0], cs, gt)
        _put(outs[1], cs, up)
        _put(outs[2], cs, (gt * _sigmoid(gt)) * up)

    f_out = pl.BlockSpec((tm, tf), lambda i, j, k: (i, j))
    gt, up, f_act = _mm(
        "mm_gate_up", (s // tm, d_ff // tf, 1), [h2, g_gate, g_up],
        [pl.BlockSpec((tm, d), lambda i, j, k: (i, 0)), pl.BlockSpec((d, tf), lambda i, j, k: (0, j)),
         pl.BlockSpec((d, tf), lambda i, j, k: (0, j))],
        [jax.ShapeDtypeStruct((s, d_ff), BF16)] * 3,
        [f_out, f_out, f_out], [(0, 1, NN, 0), (0, 2, NN, 1)], 2, (tm, tf), 1, ep_swiglu,
        _mm_vmem([((tm, d), BF16, 2), ((d, tf), BF16, 4), ((tm, tf), F32, 8)]),    )
    (g_down,) = gathered("ffn_out", [W_DOWN], f_act)

    tkf = _tile(d_ff, (1408, 1024, 512))
    nkf = d_ff // tkf
    tml, tnl = _tile(s, (512, 256, 128)), _tile(d, (512,))
    x3 = _mm(
        "mm_down", (s // tml, d // tnl, 1), [f_act, g_down, x2],
        [pl.BlockSpec((tml, d_ff), lambda i, j, k: (i, 0)), pl.BlockSpec((d_ff, tnl), lambda i, j, k: (0, j)),
         pl.BlockSpec((tml, tnl), lambda i, j, k: (i, j))],
        [jax.ShapeDtypeStruct((s, d), F32)], [pl.BlockSpec((tml, tnl), lambda i, j, k: (i, j))],
        [(0, 1, NN, 0)], 1, (tml, tnl), 1, ep_residual,
        _mm_vmem([((tml, d_ff), BF16, 2), ((d_ff, tnl), BF16, 2), ((tml, tnl), F32, 6)]),    )[0]

    dx3, dx3b, dg_final, loss_part = _head(x3, norm_final.reshape(1, d), tgt)

    def reduce_a(tag, ids, grads):
        ks = [kinds[i] for i in ids]
        lands = [lax.empty((g.shape[0] // 2, g.shape[1]) if k == "col" else (g.shape[0], g.shape[1] // 2), BF16)
                 for g, k in zip(grads, ks)]
        send, recv, bufs = _split_start("pair_send_" + tag, list(grads) + lands, len(ids), _pair_exchange_copies(ks))
        return {"tag": tag, "ids": ids, "ks": ks, "pair": (send, recv, bufs), "token": bufs[0]}

    def reduce_b(st, after):
        tag, ids, ks = st["tag"], st["ids"], st["ks"]
        send, recv, bufs = st["pair"]
        bufs = _split_wait("pair_wait_" + tag, bufs, send, recv, _pair_exchange_copies(ks), after)
        grads, from_sib = bufs[: len(ids)], bufs[len(ids) :]
        halves = [_pair_add("pair_add_" + names[i], c_idx, g, r, k) for i, g, r, k in zip(ids, grads, from_sib, ks)]
        st["chip"] = _chip_send_start("chip_send_" + tag, halves, ks)
        st["token"] = st["chip"][2][0]

    def reduce_c(st, after):
        tag, ids, ks = st["tag"], st["ids"], st["ks"]
        send, recv, halves, lands = st["chip"]
        halves, lands = _chip_send_wait("chip_wait_" + tag, halves, lands, ks, send, recv, after)
        pieces = [_chip_sum("chip_sum_" + names[i], qc_idx, h, r, k) for i, h, r, k in zip(ids, halves, lands, ks)]
        st["share"] = _split_start("share_send_" + tag, pieces, len(ids), _pair_share_copies(ks, False))
        st["token"] = st["share"][2][0]

    def reduce_d(st, after):
        send, recv, bufs = st["share"]
        return _split_wait("share_wait_" + st["tag"], bufs, send, recv, _pair_share_copies(st["ks"], True), after)

    grads_big, upd = [None] * 7, [None] * 7

    def finish(st, after):
        shared = reduce_d(st, after)
        after = shared[0]
        for i, g in zip(st["ids"], shared):
            upd[i] = _adamw("adamw_" + names[i], big_w[i], g, big_m[i], big_v[i], after=(after,))
            grads_big[i] = upd[i][3]
            after = upd[i][0]
        return after

    def ep_swiglu_bwd(ins, vals, outs, cs):
        df = vals[0]
        gtv, upv = ins[2][:, cs].astype(F32), ins[3][:, cs].astype(F32)
        sg = _sigmoid(gtv)
        _put(outs[0], cs, df * upv * (sg + gtv * sg * (1.0 - sg)))
        _put(outs[1], cs, df * (gtv * sg))

    dgt, dup = _mm(
        "mm_dswiglu", (s // tm, d_ff // tf, 1), [dx3b, g_down, gt, up],
        [pl.BlockSpec((tm, d), lambda i, j, k: (i, 0)), pl.BlockSpec((tf, d), lambda i, j, k: (j, 0)), f_out, f_out],
        [jax.ShapeDtypeStruct((s, d_ff), BF16), jax.ShapeDtypeStruct((s, d_ff), BF16)], [f_out, f_out],
        [(0, 1, NT, 0)], 1, (tm, tf), 1, ep_swiglu_bwd,
        _mm_vmem([((tm, d), BF16, 2), ((tf, d), BF16, 2), ((tm, tf), F32, 8)]),    )

    def ep_store(ins, vals, outs, cs):
        for o, v in zip(outs, vals):
            _put(o, cs, v)

    twn = _tile(d, (1024, 512))
    gw_down = _mm(
        "mm_gw_down", (d_ff // tkf, d // twn, 1), [f_act, dx3b],
        [pl.BlockSpec((s, tkf), lambda i, j, k: (0, i)), pl.BlockSpec((s, twn), lambda i, j, k: (0, j))],
        [jax.ShapeDtypeStruct((d_ff, d), BF16)], [pl.BlockSpec((tkf, twn), lambda i, j, k: (i, j))],
        [(0, 1, TN, 0)], 1, (tkf, twn), 1, ep_store,
        _mm_vmem([((s, tkf), BF16, 3), ((s, twn), BF16, 2), ((tkf, twn), F32, 3)]),
    )[0]
    red_down = reduce_a("down", [W_DOWN], [gw_down])

    tn2 = _tile(d, (256,))
    dh2_specs = [pl.BlockSpec((tm, d_ff), lambda i, j, k: (i, 0)), pl.BlockSpec((tn2, d_ff), lambda i, j, k: (j, 0))]
    dh2_tile = pl.BlockSpec((tm, tn2), lambda i, j, k: (i, j))
    dh2_vmem = _mm_vmem([((tm, d_ff), BF16, 2), ((tn2, d_ff), BF16, 2), ((tm, tn2), F32, 7)])
    dh2 = _mm(
        "mm_dh2_gate", (s // tm, d // tn2, 1), [dgt, g_gate], dh2_specs,
        [jax.ShapeDtypeStruct((s, d), F32)], [dh2_tile], [(0, 1, NT, 0)], 1, (tm, tn2), 1, ep_store, dh2_vmem,
        after=(red_down["token"],),
    )[0]
    dh2 = _mm(
        "mm_dh2_up", (s // tm, d // tn2, 1), [dup, g_up, dh2], dh2_specs + [dh2_tile],
        [jax.ShapeDtypeStruct((s, d), F32)], [dh2_tile], [(0, 1, NT, 0)], 1, (tm, tn2), 1, ep_residual, dh2_vmem,
    )[0]
    reduce_b(red_down, dh2)

    twr = _tile(d, (1024, 512))
    w_tile = pl.BlockSpec((twr, tf), lambda i, j, k: (i, j))
    gw_gate, gw_up = _mm(
        "mm_gw_gate_up", (d // twr, d_ff // tf, 1), [h2, dgt, dup],
        [pl.BlockSpec((s, twr), lambda i, j, k: (0, i)), pl.BlockSpec((s, tf), lambda i, j, k: (0, j)),
         pl.BlockSpec((s, tf), lambda i, j, k: (0, j))],
        [jax.ShapeDtypeStruct((d, d_ff), BF16), jax.ShapeDtypeStruct((d, d_ff), BF16)], [w_tile, w_tile],
        [(0, 1, TN, 0), (0, 2, TN, 1)], 2, (twr, tf), 1, ep_store,
        _mm_vmem([((s, twr), BF16, 3), ((s, tf), BF16, 4), ((twr, tf), F32, 6)]),
        after=(red_down["token"],),
    )
    red_ffn = reduce_a("ffn_in", [W_GATE, W_UP], [gw_gate, gw_up])

    dx2, dx2b, dg_ffn = _rms_bwd("rms_ffn_bwd", x2, norm_ffn, dh2, dx3, after=(red_ffn["token"],))

    nj = d // tg

    def lo(j):
        return jnp.minimum(j, nj - 1)

    def gate_bwd_body(dx_ref, wo_ref, ga_ref, gb_ref, ya_ref, yb_ref, dya_ref, dyb_ref, dz_ref, keep):
        j = pl.program_id(1)

        @pl.when(j < nj)
        def _():
            dm = lax.dot_general(dx_ref[...], wo_ref[...], NT, preferred_element_type=F32)
            sa, sb = _sigmoid(ga_ref[...].astype(F32)), _sigmoid(gb_ref[...].astype(F32))
            dya_ref[...] = (dm * sa).astype(BF16)
            dyb_ref[...] = (dm * sb).astype(BF16)
            dz_ref[...] = (dm * ya_ref[...].astype(F32) * (sa * (1.0 - sa))).astype(BF16)
            keep[lo(j)] = (dm * yb_ref[...].astype(F32) * (sb * (1.0 - sb))).astype(BF16)

        @pl.when(j >= nj)
        def _():
            dz_ref[...] = keep[jnp.maximum(j - nj, 0)]

    t_lo = pl.BlockSpec((tm, tg), lambda i, j: (i, lo(j)))
    dya, dyb, dz = _pallas(
        gate_bwd_body,
        name="mm_dgate",
        grid=(s // tm, 2 * nj),
        in_specs=[
            pl.BlockSpec((tm, d), lambda i, j: (i, 0)),
            pl.BlockSpec((tg, d), lambda i, j: (lo(j), 0)),
            pl.BlockSpec((tm, tg), lambda i, j: (i, ga0 + lo(j))),
            pl.BlockSpec((tm, tg), lambda i, j: (i, gb0 + lo(j))),
            t_lo,
            t_lo,
        ],
        out_specs=[t_lo, t_lo, pl.BlockSpec((tm, tg), lambda i, j: (i, ga0 + j))],
        out_shape=[jax.ShapeDtypeStruct((s, d), BF16), jax.ShapeDtypeStruct((s, d), BF16), jax.ShapeDtypeStruct((s, n_in), BF16)],
        scratch_shapes=[pltpu.VMEM((nj, tm, tg), BF16)],
        compiler_params=_params(_mm_vmem([((tm, d), BF16, 2), ((tg, d), BF16, 2), ((tm, tg), F32, 14), ((nj, tm, tg), BF16, 1)])),
    )(dx2b, g_o, z, z, y_a, y_b)
    reduce_b(red_ffn, dya)
    reduce_c(red_down, red_ffn["token"])

    gw_o = _mm(
        "mm_gw_o", (d // twr, d // twn, 1), [m_act, dx2b],
        [pl.BlockSpec((s, twr), lambda i, j, k: (0, i)), pl.BlockSpec((s, twn), lambda i, j, k: (0, j))],
        [jax.ShapeDtypeStruct((d, d), BF16)], [pl.BlockSpec((twr, twn), lambda i, j, k: (i, j))],
        [(0, 1, TN, 0)], 1, (twr, twn), 1, ep_store,
        _mm_vmem([((s, twr), BF16, 3), ((s, twn), BF16, 2), ((twr, twn), F32, 3)]),
        after=(red_down["token"],),
    )[0]
    red_o = reduce_a("w_o", [W_O], [gw_o])
    finish(red_down, red_o["token"])

    tb = _tile(w_sgu, (1024, 512))
    b_out = pl.BlockSpec((tm, tb), lambda i, j, k: (i, j))

    da, datt = _mm(
        "mm_dbranches", (s // tm, w_sgu // tb, 1), [dya, g_a, dyb, g_b],
        [pl.BlockSpec((tm, d), lambda i, j, k: (i, 0)), pl.BlockSpec((tb, d), lambda i, j, k: (j, 0)),
         pl.BlockSpec((tm, d), lambda i, j, k: (i, 0)), pl.BlockSpec((tb, d), lambda i, j, k: (j, 0))],
        [jax.ShapeDtypeStruct((s, w_sgu), BF16), jax.ShapeDtypeStruct((s, w_att), BF16)], [b_out, b_out],
        [(0, 1, NT, 0), (2, 3, NT, 1)], 2, (tm, tb), 1, ep_store,
        _mm_vmem([((tm, d), BF16, 4), ((tb, d), BF16, 4), ((tm, tb), F32, 6)]),        after=(red_o["token"],),
    )
    reduce_b(red_o, da)

    wb_tile = pl.BlockSpec((tb, twn), lambda i, j, k: (i, j))
    gw_a, gw_b = _mm(
        "mm_gw_branches", (w_sgu // tb, d // twn, 1), [a_act, dya, att, dyb],
        [pl.BlockSpec((s, tb), lambda i, j, k: (0, i)), pl.BlockSpec((s, twn), lambda i, j, k: (0, j)),
         pl.BlockSpec((s, tb), lambda i, j, k: (0, i)), pl.BlockSpec((s, twn), lambda i, j, k: (0, j))],
        [jax.ShapeDtypeStruct((w_sgu, d), BF16), jax.ShapeDtypeStruct((w_att, d), BF16)], [wb_tile, wb_tile],
        [(0, 1, TN, 0), (2, 3, TN, 1)], 2, (tb, twn), 1, ep_store,
        _mm_vmem([((s, tb), BF16, 5), ((s, twn), BF16, 4), ((tb, twn), F32, 6)]),
        after=(red_o["token"],),
    )
    red_mix = reduce_a("mix", [W_A, W_B], [gw_a, gw_b])

    dz, dws, dbs, dgain = _sgu_bwd(z, da, sgu_v_gain, ws_b, wst_b, b_col, w_sgu, dz, after=(red_mix["token"],))
    dz, dk_pad, dv_pad, dbias_tab, dsink = _attn_bwd(sink, z, k_pad, v_pad, bias_tab, datt, dz, grp, q_blk0)
    dz = _dkv_to_dz(dk_pad, dv_pad, dz, off_k // (2 * w_kv))
    drel = _relbias_bwd(dbias_tab, bucket)
    reduce_b(red_mix, dz)
    reduce_c(red_ffn, red_mix["token"])

    small_w = [norm_mix, sgu_v_gain, sgu_w_s, sgu_b_s, attn_sink, rel_bias, norm_ffn, norm_final]
    small_m = [m_norm_mix, m_sgu_v_gain, m_sgu_w_s, m_sgu_b_s, m_attn_sink, m_rel_bias, m_norm_ffn, m_norm_final]
    small_v = [v_norm_mix, v_sgu_v_gain, v_sgu_w_s, v_sgu_b_s, v_attn_sink, v_rel_bias, v_norm_ffn, v_norm_final]
    small_shapes = [w.shape for w in small_w]
    early = [dgain, dws, dbs, dsink[:, 0], drel[:, :REL_BUCKETS].T, dg_ffn, dg_final]
    p_early = _pack([g.reshape(shp) for g, shp in zip(early, small_shapes[1:])] + [loss_part[0, :1]])
    land = jnp.zeros((2 * N_CHIPS,) + p_early.shape, F32)
    sm_send, sm_recv, (p_early, land) = _split_start("small_send", [p_early, land], 2 * N_CHIPS - 1, _small_exchange_copies(False))

    tzn = _tile(n_in, (768, 640, 512))
    gw_in = _mm(
        "mm_gw_in", (d // twr, n_in // tzn, 1), [h1, dz],
        [pl.BlockSpec((s, twr), lambda i, j, k: (0, i)), pl.BlockSpec((s, tzn), lambda i, j, k: (0, j))],
        [jax.ShapeDtypeStruct((d, n_in), BF16)], [pl.BlockSpec((twr, tzn), lambda i, j, k: (i, j))],
        [(0, 1, TN, 0)], 1, (twr, tzn), 1, ep_store,
        _mm_vmem([((s, twr), BF16, 3), ((s, tzn), BF16, 2), ((twr, tzn), F32, 3)]),
        after=(red_ffn["token"], p_early),
    )[0]
    red_in = reduce_a("w_in", [W_IN], [gw_in])

    reduce_c(red_o, red_in["token"])
    reduce_c(red_mix, red_o["token"])
    reduce_b(red_in, finish(red_mix, finish(red_o, red_mix["token"])))

    dh1 = _mm(
        "mm_dh1", (s // tm, d // tn2, 1), [dz, g_in],
        [pl.BlockSpec((tm, n_in), lambda i, j, k: (i, 0)), pl.BlockSpec((tn2, n_in), lambda i, j, k: (j, 0))],
        [jax.ShapeDtypeStruct((s, d), F32)], [pl.BlockSpec((tm, tn2), lambda i, j, k: (i, j))],
        [(0, 1, NT, 0)], 1, (tm, tn2), 1, ep_store,
        _mm_vmem([((tm, n_in), BF16, 2), ((tn2, n_in), BF16, 2), ((tm, tn2), F32, 5)]),
        after=(red_in["token"],),
    )[0]

    grad_x, _, dg_mix = _rms_bwd("rms_mix_bwd", x2d, norm_mix, dh1, dx2)

    p_mix = _pack([dg_mix.reshape(small_shapes[0])])
    land_mix = jnp.zeros((2 * N_CHIPS,) + p_mix.shape, F32)
    mx_send, mx_recv, (p_mix, land_mix) = _split_start("mix_send", [p_mix, land_mix], 2 * N_CHIPS - 1, _small_exchange_copies(False))

    reduce_c(red_in, finish(red_ffn, p_mix))
    p_early, land = _split_wait("small_wait", [p_early, land], sm_send, sm_recv, _small_exchange_copies(True), red_in["token"])
    p_mix, land_mix = _split_wait("mix_wait", [p_mix, land_mix], mx_send, mx_recv, _small_exchange_copies(True), p_early)
    me_idx = 2 * q_idx + c_idx
    packed_g = jnp.concatenate([_small_sum("mix_sum", me_idx, p_mix, land_mix), _small_sum("small_sum", me_idx, p_early, land)], axis=0)
    g_small = _unpack(packed_g, small_shapes + [(1,)])
    loss = g_small[-1].reshape(())
    g_small = g_small[:-1]
    zero1 = jnp.zeros((1,), F32)
    pw, pg, pm, pv = _pack(small_w + [zero1]), _pack(g_small + [zero1]), _pack(small_m + [zero1]), _pack(small_v + [zero1])
    small_upd = _adamw("adamw_small", pw, pg, pm, pv)
    d_small, nm_small, nv_small = [_unpack(a, small_shapes) for a in small_upd[:3]]
    finish(red_in, small_upd[0])

    small_names = ["norm_mix", "sgu_v_gain", "sgu_w_s", "sgu_b_s", "attn_sink", "rel_bias", "norm_ffn", "norm_final"]
    table = {}
    for i, n in enumerate(names):
        table[n] = (grads_big[i][None], upd[i][0][None], upd[i][1][None], upd[i][2][None])
    for i, n in enumerate(small_names):
        table[n] = (g_small[i], d_small[i], nm_small[i], nv_small[i])
    order = ["w_in", "norm_mix", "sgu_v_gain", "sgu_w_s", "sgu_b_s", "w_a", "attn_sink", "rel_bias", "w_b", "w_o", "norm_ffn",
             "w_gate", "w_up", "w_down", "norm_final"]
    outs = [loss, grad_x.reshape(1, s, d)]
    for part in range(4):
        outs += [table[n][part] for n in order]
    return tuple(outs)
```

```python
import math

import jax
import jax.numpy as jnp
import numpy as np
from jax import lax
from jax.experimental import pallas as pl
from jax.experimental.pallas import tpu as pltpu

F32 = jnp.float32
BF16 = jnp.bfloat16
I32 = jnp.int32
MESH = pl.DeviceIdType.MESH

EPS = 1e-6
NEG = -1e30
BLK = 128
HEAD_DIM = 128
N_KV_HEADS = 2
REL_BUCKETS = 32
REL_MAX_DIST = 128
N_CHIPS = 4
ADAM_LR, ADAM_B1, ADAM_B2, ADAM_EPS, ADAM_WD, ADAM_STEP = 0.001, 0.9, 0.999, 1e-08, 0.01, 10

LANES = 128
MXU_COLS = 256
VMEM_CAP = 60 * 1024 * 1024

NN = (((1,), (0,)), ((), ()))
NT = (((1,), (1,)), ((), ()))
TN = (((0,), (0,)), ((), ()))
ANY = pl.BlockSpec(memory_space=pl.ANY)
HBM_SPEC = pl.BlockSpec(memory_space=pltpu.HBM)
SEM_SPEC = pl.BlockSpec(memory_space=pltpu.SEMAPHORE)
EFFECT = pltpu.SideEffectType.DATAFLOW_SIDE_EFFECTING


def _tile(n, cands):
    for t in cands:
        if n % t == 0:
            return t
    return n


PIN_BYTES = 64 * 1024


def _pin_hbm(a):
    big = hasattr(a, "dtype") and jnp.issubdtype(a.dtype, jnp.floating) and _nbytes(a.shape, a.dtype) >= PIN_BYTES
    return pltpu.with_memory_space_constraint(a, pltpu.HBM) if big else a


def _pallas(body, *, out_shape, **kw):
    def pin(o):
        big = isinstance(o, jax.ShapeDtypeStruct) and jnp.issubdtype(o.dtype, jnp.floating) and _nbytes(o.shape, o.dtype) >= PIN_BYTES
        return pltpu.HBM(o.shape, o.dtype) if big else o

    shapes = type(out_shape)(pin(o) for o in out_shape) if isinstance(out_shape, (list, tuple)) else pin(out_shape)
    call = pl.pallas_call(body, out_shape=shapes, **kw)
    return lambda *args: call(*[_pin_hbm(a) for a in args])


def _params(vmem_bytes=None, **kw):
    if vmem_bytes is not None:
        kw["vmem_limit_bytes"] = int(min(max(vmem_bytes, 32 * 1024 * 1024), VMEM_CAP))
    return pltpu.CompilerParams(**kw)


def _nbytes(shape, dtype):
    return int(np.prod(shape)) * jnp.dtype(dtype).itemsize


def _sigmoid(x):
    return 1.0 / (1.0 + jnp.exp(-x))


_GC = 0.7978845608028654
_GA = 0.044715


def _gelu(x):
    return 0.5 * x * (1.0 + jnp.tanh(_GC * (x + _GA * (x * x * x))))


def _gelu_grad(x):
    t = jnp.tanh(_GC * (x + _GA * (x * x * x)))
    return 0.5 * (1.0 + t) + 0.5 * x * (1.0 - t * t) * (_GC * (1.0 + 3.0 * _GA * (x * x)))


def _bf(v):
    return v if v.dtype == BF16 else v.astype(BF16)


def _mm(name, grid, ins, in_specs, out_shape, out_specs, pairs, n_acc, tile, nk, epilogue, vmem_bytes, after=(), col_chunks=1):
    assert nk == 1 and tile[1] % col_chunks == 0
    n_in, n_out = len(ins) + len(after), len(out_shape)
    width = tile[1] // col_chunks

    def body(*refs):
        in_refs, out_refs = refs[:n_in], refs[n_in : n_in + n_out]
        for ch in range(col_chunks):
            cs = slice(ch * width, (ch + 1) * width) if col_chunks > 1 else slice(None)
            vals = [None] * n_acc
            for a_i, b_i, dn, acc_i in pairs:
                rhs = in_refs[b_i][cs, :] if dn == NT else in_refs[b_i][:, cs]
                d = lax.dot_general(_bf(in_refs[a_i][...]), _bf(rhs), dn, preferred_element_type=F32)
                vals[acc_i] = d if vals[acc_i] is None else vals[acc_i] + d
            epilogue(in_refs, vals, out_refs, cs)

    return _pallas(
        body,
        name=name,
        grid=grid,
        in_specs=list(in_specs) + [ANY] * len(after),
        out_specs=out_specs,
        out_shape=out_shape,
        compiler_params=_params(vmem_bytes),
    )(*ins, *after)


def _put(ref, cs, v):
    ref[:, cs] = v.astype(ref.dtype)


def _gate_up_quarter(name, quarter, h2, g_gate, g_up, prev):
    s, d = h2.shape
    d_ff = g_gate.shape[1]
    wq = d_ff // N_CHIPS
    tm = _tile(s, (512, 256, 128))
    n_prev = 0 if prev is None else 3

    def body(q_ref, h_ref, wg_ref, wu_ref, *rest):
        del q_ref
        gt_ref, up_ref, f_ref = rest[n_prev:]
        hv = h_ref[...]
        gt = jnp.dot(hv, wg_ref[...], preferred_element_type=F32)
        up = jnp.dot(hv, wu_ref[...], preferred_element_type=F32)
        gt_ref[...] = gt.astype(BF16)
        up_ref[...] = up.astype(BF16)
        f_ref[...] = ((gt * _sigmoid(gt)) * up).astype(BF16)

    w_spec = pl.BlockSpec((d, wq), lambda i, q: (0, q[0]))
    o_spec = pl.BlockSpec((tm, wq), lambda i, q: (i, q[0]))
    out = jax.ShapeDtypeStruct((s, d_ff), BF16)
    return _pallas(
        body,
        name=name,
        grid_spec=pltpu.PrefetchScalarGridSpec(
            num_scalar_prefetch=1,
            grid=(s // tm,),
            in_specs=[pl.BlockSpec((tm, d), lambda i, q: (i, 0)), w_spec, w_spec] + [ANY] * n_prev,
            out_specs=[o_spec] * 3,
        ),
        out_shape=[out, out, out],
        input_output_aliases={4 + k: k for k in range(n_prev)},
        compiler_params=_params(_mm_vmem([((tm, d), BF16, 2), ((d, wq), BF16, 4), ((tm, wq), BF16, 6), ((tm, wq), F32, 3)])),
    )(quarter, h2, g_gate, g_up, *(prev or ()))


def _down_quarter(name, quarter, f_act, g_down, x_acc, in_place):
    s, d_ff = f_act.shape
    d = g_down.shape[1]
    kq = d_ff // N_CHIPS
    tm, tn = _tile(s, (1024, 512, 256, 128)), _tile(d, (1024, 512))

    def body(q_ref, f_ref, w_ref, x_ref, o_ref):
        del q_ref
        o_ref[...] = x_ref[...] + jnp.dot(f_ref[...], w_ref[...], preferred_element_type=F32)

    tile = pl.BlockSpec((tm, tn), lambda i, j, q: (i, j))
    return _pallas(
        body,
        name=name,
        grid_spec=pltpu.PrefetchScalarGridSpec(
            num_scalar_prefetch=1,
            grid=(s // tm, d // tn),
            in_specs=[pl.BlockSpec((tm, kq), lambda i, j, q: (i, q[0])), pl.BlockSpec((kq, tn), lambda i, j, q: (q[0], j)), tile],
            out_specs=tile,
        ),
        out_shape=jax.ShapeDtypeStruct(x_acc.shape, F32),
        input_output_aliases={3: 0} if in_place else {},
        compiler_params=_params(_mm_vmem([((tm, kq), BF16, 2), ((kq, tn), BF16, 2), ((tm, tn), F32, 6)])),
    )(quarter, f_act, g_down, x_acc)


def _mm_vmem(tiles):
    return sum(_nbytes(s, d) * c for s, d, c in tiles) + 4 * 1024 * 1024


def _rows8(v):
    r, d = v.shape
    return v.reshape(r // 8, 8, d).sum(axis=0)


def _rms_fwd(name, x, g, after=()):
    s, d = x.shape
    tm = _tile(s, (256, 128))

    def body(x_ref, g_ref, *rest):
        h_ref = rest[-1]
        xv = x_ref[...]
        r = lax.rsqrt(jnp.mean(xv * xv, axis=-1, keepdims=True) + EPS)
        h_ref[...] = ((xv * r) * g_ref[...]).astype(BF16)

    return _pallas(
        body,
        name=name,
        grid=(s // tm,),
        in_specs=[pl.BlockSpec((tm, d), lambda i: (i, 0)), pl.BlockSpec((1, d), lambda i: (0, 0))] + [ANY] * len(after),
        out_specs=pl.BlockSpec((tm, d), lambda i: (i, 0)),
        out_shape=jax.ShapeDtypeStruct((s, d), BF16),
    )(x, g, *after)


def _rms_bwd(name, x, g, dh, dres, after=()):
    s, d = x.shape
    tm = _tile(s, (256, 128))
    n = s // tm
    n_after = len(after)

    def body(x_ref, g_ref, dh_ref, dres_ref, *rest):
        dx_ref, dxb_ref, dg_ref, acc_ref = rest[n_after:]
        i = pl.program_id(0)
        xv = x_ref[...]
        r = lax.rsqrt(jnp.mean(xv * xv, axis=-1, keepdims=True) + EPS)
        xh = xv * r
        dhv = dh_ref[...]
        dxh = dhv * g_ref[...]
        dx = r * (dxh - xh * jnp.mean(dxh * xh, axis=-1, keepdims=True)) + dres_ref[...]
        dx_ref[...] = dx
        dxb_ref[...] = dx.astype(BF16)
        part = _rows8(dhv * xh)

        @pl.when(i == 0)
        def _():
            acc_ref[...] = part

        @pl.when(i > 0)
        def _():
            acc_ref[...] += part

        @pl.when(i == n - 1)
        def _():
            dg_ref[...] = jnp.sum(acc_ref[...], axis=0, keepdims=True)

    row = pl.BlockSpec((tm, d), lambda i: (i, 0))
    vec = pl.BlockSpec((1, d), lambda i: (0, 0))
    return _pallas(
        body,
        name=name,
        grid=(n,),
        in_specs=[row, vec, row, row] + [ANY] * n_after,
        out_specs=[row, row, vec],
        out_shape=[jax.ShapeDtypeStruct((s, d), F32), jax.ShapeDtypeStruct((s, d), BF16), jax.ShapeDtypeStruct((1, d), F32)],
        scratch_shapes=[pltpu.VMEM((8, d), F32)],
    )(x, g, dh, dres, *after)


def _head(x3, g, target):
    s, d = x3.shape
    tm = _tile(s, (256, 128))
    n = s // tm

    def body(x_ref, g_ref, t_ref, dx_ref, dxb_ref, dg_ref, loss_ref, acc_g, acc_l):
        i = pl.program_id(0)
        xv = x_ref[...]
        gv = g_ref[...]
        r = lax.rsqrt(jnp.mean(xv * xv, axis=-1, keepdims=True) + EPS)
        xh = xv * r
        e = xh * gv - t_ref[...]
        dy = e * (1.0 / d)
        dxh = dy * gv
        dx = r * (dxh - xh * jnp.mean(dxh * xh, axis=-1, keepdims=True))
        dx_ref[...] = dx
        dxb_ref[...] = dx.astype(BF16)
        pg = _rows8(dy * xh)
        plo = _rows8(e * e)

        @pl.when(i == 0)
        def _():
            acc_g[...] = pg
            acc_l[...] = plo

        @pl.when(i > 0)
        def _():
            acc_g[...] += pg
            acc_l[...] += plo

        @pl.when(i == n - 1)
        def _():
            dg_ref[...] = jnp.sum(acc_g[...], axis=0, keepdims=True)
            loss_ref[...] = jnp.full((1, LANES), (0.5 / d) * jnp.sum(acc_l[...]), F32)

    row = pl.BlockSpec((tm, d), lambda i: (i, 0))
    vec = pl.BlockSpec((1, d), lambda i: (0, 0))
    return _pallas(
        body,
        name="head",
        grid=(n,),
        in_specs=[row, vec, row],
        out_specs=[row, row, vec, pl.BlockSpec((1, LANES), lambda i: (0, 0))],
        out_shape=[
            jax.ShapeDtypeStruct((s, d), F32),
            jax.ShapeDtypeStruct((s, d), BF16),
            jax.ShapeDtypeStruct((1, d), F32),
            jax.ShapeDtypeStruct((1, LANES), F32),
        ],
        scratch_shapes=[pltpu.VMEM((8, d), F32), pltpu.VMEM((8, d), F32)],
    )(x3, g, target)


def _sgu_fwd(z, gain, ws_b, b_col, w_sgu):
    s = z.shape[0]
    groups = ws_b.shape[0]

    def body(zu_ref, zv_ref, gain_ref, ws_ref, b_ref, a_ref):
        vv = _gelu(zv_ref[...].astype(F32))
        r = lax.rsqrt(jnp.mean(vv * vv, axis=-1, keepdims=True) + EPS)
        vn = ((vv * r) * gain_ref[...]).astype(BF16)
        u = _gelu(zu_ref[...].astype(F32))
        for g in range(groups):
            sl = slice(g * BLK, (g + 1) * BLK)
            mixed = jnp.dot(ws_ref[g], vn[:, sl], preferred_element_type=F32) + b_ref[g]
            a_ref[:, sl] = (u[:, sl] * mixed).astype(BF16)

    return _pallas(
        body,
        name="sgu_fwd",
        grid=(s // BLK,),
        in_specs=[
            pl.BlockSpec((BLK, w_sgu), lambda c: (c, 0)),
            pl.BlockSpec((BLK, w_sgu), lambda c: (c, 1)),
            pl.BlockSpec((1, w_sgu), lambda c: (0, 0)),
            pl.BlockSpec((groups, BLK, BLK), lambda c: (0, 0, 0)),
            pl.BlockSpec((groups, BLK, 1), lambda c: (0, 0, 0)),
        ],
        out_specs=pl.BlockSpec((BLK, w_sgu), lambda c: (c, 0)),
        out_shape=jax.ShapeDtypeStruct((s, w_sgu), BF16),
    )(z, z, gain, ws_b, b_col)


def _sgu_bwd(z, da, gain, ws_b, wst_b, b_col, w_sgu, dz, after=()):
    s = z.shape[0]
    groups = ws_b.shape[0]
    n = s // BLK
    n_skip = 1 + len(after)

    def body(zu_ref, zv_ref, da_ref, gain_ref, ws_ref, wst_ref, b_ref, *rest):
        dz_ref, dws_ref, dbs_ref, dgain_ref, acc_gain = rest[n_skip:]
        c = pl.program_id(0)
        zu = zu_ref[...].astype(F32)
        zv = zv_ref[...].astype(F32)
        gain_v = gain_ref[...]
        vv = _gelu(zv)
        r = lax.rsqrt(jnp.mean(vv * vv, axis=-1, keepdims=True) + EPS)
        xh = vv * r
        vn = (xh * gain_v).astype(BF16)
        u = _gelu(zu)
        dav = da_ref[...].astype(F32)
        dmix = dav * u
        dmix_b = dmix.astype(BF16)
        dvn_parts = []
        for g in range(groups):
            sl = slice(g * BLK, (g + 1) * BLK)
            mixed = jnp.dot(ws_ref[g], vn[:, sl], preferred_element_type=F32) + b_ref[g]
            dz_ref[:, sl] = (dav[:, sl] * mixed * _gelu_grad(zu[:, sl])).astype(BF16)
            dvn_parts.append(jnp.dot(wst_ref[g], dmix_b[:, sl], preferred_element_type=F32))
            dws_g = lax.dot_general(dmix_b[:, sl], vn[:, sl], NT, preferred_element_type=F32)
            dbs_g = jnp.sum(dmix[:, sl], axis=1, keepdims=True)

            @pl.when(c == 0)
            def _():
                dws_ref[g] = dws_g
                dbs_ref[g] = dbs_g

            @pl.when(c > 0)
            def _():
                dws_ref[g] += dws_g
                dbs_ref[g] += dbs_g

        dvn = jnp.concatenate(dvn_parts, axis=1)
        dxh = dvn * gain_v
        dvv = r * (dxh - xh * jnp.mean(dxh * xh, axis=-1, keepdims=True))
        dz_ref[:, w_sgu:] = (dvv * _gelu_grad(zv)).astype(BF16)
        pg = _rows8(dvn * xh)

        @pl.when(c == 0)
        def _():
            acc_gain[...] = pg

        @pl.when(c > 0)
        def _():
            acc_gain[...] += pg

        @pl.when(c == n - 1)
        def _():
            dgain_ref[...] = jnp.sum(acc_gain[...], axis=0, keepdims=True)

    full3 = pl.BlockSpec((groups, BLK, BLK), lambda c: (0, 0, 0))
    col3 = pl.BlockSpec((groups, BLK, 1), lambda c: (0, 0, 0))
    vec = pl.BlockSpec((1, w_sgu), lambda c: (0, 0))
    return _pallas(
        body,
        name="sgu_bwd",
        grid=(n,),
        in_specs=[
            pl.BlockSpec((BLK, w_sgu), lambda c: (c, 0)),
            pl.BlockSpec((BLK, w_sgu), lambda c: (c, 1)),
            pl.BlockSpec((BLK, w_sgu), lambda c: (c, 0)),
            vec,
            full3,
            full3,
            col3,
            ANY,
        ]
        + [ANY] * len(after),
        out_specs=[pl.BlockSpec((BLK, 2 * w_sgu), lambda c: (c, 0)), full3, col3, vec],
        out_shape=[
            jax.ShapeDtypeStruct(dz.shape, BF16),
            jax.ShapeDtypeStruct((groups, BLK, BLK), F32),
            jax.ShapeDtypeStruct((groups, BLK, 1), F32),
            jax.ShapeDtypeStruct((1, w_sgu), F32),
        ],
        scratch_shapes=[pltpu.VMEM((8, w_sgu), F32)],
        input_output_aliases={7: 0},
    )(z, z, da, gain, ws_b, wst_b, b_col, dz, *after)


def _attn_softmax(sink_ref, q_ref, k_ref, v_ref, bias_ref, s_len, grp):
    kv = pl.program_id(0)
    n = pl.program_id(1)
    start = pl.multiple_of(n * BLK, BLK)
    kb = k_ref[pl.ds(start, 3 * BLK), :]
    vb = v_ref[pl.ds(start, 3 * BLK), :]
    qv = q_ref[...]
    qs = jnp.concatenate([qv[:, g * HEAD_DIM : (g + 1) * HEAD_DIM] for g in range(grp)], axis=0).astype(BF16)
    sc = lax.dot_general(qs, kb, NT, preferred_element_type=F32) * (HEAD_DIM**-0.5)
    sc = sc + bias_ref[...].reshape(grp * BLK, 3 * BLK)
    kpos = start + lax.broadcasted_iota(I32, (1, 3 * BLK), 1) - BLK
    sc = jnp.where((kpos >= 0) & (kpos < s_len), sc, NEG)
    sink = jnp.concatenate([jnp.full((BLK, 1), sink_ref[kv * grp + g], F32) for g in range(grp)], axis=0)
    m = jnp.maximum(jnp.max(sc, axis=-1, keepdims=True), sink)
    p = jnp.exp(sc - m)
    esink = jnp.exp(sink - m)
    den = jnp.sum(p, axis=-1, keepdims=True) + esink
    return start, qs, kb, vb, p / den, esink / den


def _attn_specs(s, grp, q_blk0):
    qw = grp * HEAD_DIM
    return [
        pl.BlockSpec(memory_space=pltpu.SMEM),
        pl.BlockSpec((BLK, qw), lambda kv, n: (n, q_blk0 + kv)),
        pl.BlockSpec((s + 2 * BLK, HEAD_DIM), lambda kv, n: (0, kv)),
        pl.BlockSpec((s + 2 * BLK, HEAD_DIM), lambda kv, n: (0, kv)),
        pl.BlockSpec((grp, BLK, 3 * BLK), lambda kv, n: (kv, 0, 0)),
    ]


def _attn_fwd(sink, z, k_pad, v_pad, bias_tab, grp, q_blk0):
    s = z.shape[0]
    qw = grp * HEAD_DIM

    def body(sink_ref, q_ref, k_ref, v_ref, bias_ref, o_ref):
        _, _, _, vb, pn, _ = _attn_softmax(sink_ref, q_ref, k_ref, v_ref, bias_ref, s, grp)
        o = jnp.dot(pn.astype(BF16), vb, preferred_element_type=F32)
        for g in range(grp):
            o_ref[:, g * HEAD_DIM : (g + 1) * HEAD_DIM] = o[g * BLK : (g + 1) * BLK].astype(BF16)

    return _pallas(
        body,
        name="attn_fwd",
        grid=(N_KV_HEADS, s // BLK),
        in_specs=_attn_specs(s, grp, q_blk0),
        out_specs=pl.BlockSpec((BLK, qw), lambda kv, n: (n, kv)),
        out_shape=jax.ShapeDtypeStruct((s, N_KV_HEADS * qw), BF16),
    )(sink, z, k_pad, v_pad, bias_tab)


def _attn_bwd(sink, z, k_pad, v_pad, bias_tab, dout, dz, grp, q_blk0):
    s = z.shape[0]
    qw = grp * HEAD_DIM
    nb = s // BLK
    heads = N_KV_HEADS * grp

    def body(sink_ref, q_ref, k_ref, v_ref, bias_ref, do_ref, dz_in, dq_ref, dk_ref, dv_ref, dbias_ref, dsink_ref, dk_acc, dv_acc):
        del dz_in
        kv = pl.program_id(0)
        n = pl.program_id(1)
        start, qs, kb, vb, pn, psink = _attn_softmax(sink_ref, q_ref, k_ref, v_ref, bias_ref, s, grp)
        dov = do_ref[...]
        dos = jnp.concatenate([dov[:, g * HEAD_DIM : (g + 1) * HEAD_DIM] for g in range(grp)], axis=0)
        dp = lax.dot_general(dos, vb, NT, preferred_element_type=F32)
        dvb = lax.dot_general(pn.astype(BF16), dos, TN, preferred_element_type=F32)
        delta = jnp.sum(pn * dp, axis=-1, keepdims=True)
        ds = pn * (dp - delta)
        dsb = (ds * (HEAD_DIM**-0.5)).astype(BF16)
        dq = jnp.dot(dsb, kb, preferred_element_type=F32)
        dkb = lax.dot_general(dsb, qs, TN, preferred_element_type=F32)
        for g in range(grp):
            dq_ref[:, g * HEAD_DIM : (g + 1) * HEAD_DIM] = dq[g * BLK : (g + 1) * BLK].astype(BF16)

        @pl.when(n == 0)
        def _():
            dk_acc[...] = jnp.zeros_like(dk_acc)
            dv_acc[...] = jnp.zeros_like(dv_acc)
            dbias_ref[...] = jnp.zeros_like(dbias_ref)

        @pl.when((n == 0) & (kv == 0))
        def _():
            dsink_ref[...] = jnp.zeros_like(dsink_ref)

        dk_acc[pl.ds(start, 3 * BLK), :] += dkb
        dv_acc[pl.ds(start, 3 * BLK), :] += dvb
        dbias_ref[...] += ds.reshape(grp, BLK, 3 * BLK)
        row = lax.broadcasted_iota(I32, (heads, LANES), 0)
        sd = psink * delta
        upd = jnp.zeros((heads, LANES), F32)
        for g in range(grp):
            upd = jnp.where(row == kv * grp + g, -jnp.sum(sd[g * BLK : (g + 1) * BLK]), upd)
        dsink_ref[...] += upd

        @pl.when(n == nb - 1)
        def _():
            dk_ref[...] = dk_acc[...]
            dv_ref[...] = dv_acc[...]

    pad_spec = pl.BlockSpec((s + 2 * BLK, HEAD_DIM), lambda kv, n: (0, kv))
    kvw = N_KV_HEADS * HEAD_DIM
    return _pallas(
        body,
        name="attn_bwd",
        grid=(N_KV_HEADS, nb),
        in_specs=_attn_specs(s, grp, q_blk0) + [pl.BlockSpec((BLK, qw), lambda kv, n: (n, kv)), ANY],
        out_specs=[
            pl.BlockSpec((BLK, qw), lambda kv, n: (n, q_blk0 + kv)),
            pad_spec,
            pad_spec,
            pl.BlockSpec((grp, BLK, 3 * BLK), lambda kv, n: (kv, 0, 0)),
            pl.BlockSpec((heads, LANES), lambda kv, n: (0, 0)),
        ],
        out_shape=[
            jax.ShapeDtypeStruct(dz.shape, BF16),
            jax.ShapeDtypeStruct((s + 2 * BLK, kvw), F32),
            jax.ShapeDtypeStruct((s + 2 * BLK, kvw), F32),
            jax.ShapeDtypeStruct((heads, BLK, 3 * BLK), F32),
            jax.ShapeDtypeStruct((heads, LANES), F32),
        ],
        scratch_shapes=[pltpu.VMEM((s + 2 * BLK, HEAD_DIM), F32), pltpu.VMEM((s + 2 * BLK, HEAD_DIM), F32)],
        input_output_aliases={6: 0},
    )(sink, z, k_pad, v_pad, bias_tab, dout, dz)


def _dkv_to_dz(dk_pad, dv_pad, dz, blk_idx):
    s = dz.shape[0]
    kvw = dk_pad.shape[1]

    def body(dk_ref, dv_ref, dz_in, out_ref):
        del dz_in
        out_ref[:, :kvw] = dk_ref[...].astype(BF16)
        out_ref[:, kvw:] = dv_ref[...].astype(BF16)

    src = pl.BlockSpec((BLK, kvw), lambda i: (i + 1, 0))
    return _pallas(
        body,
        name="dkv_to_dz",
        grid=(s // BLK,),
        in_specs=[src, src, ANY],
        out_specs=pl.BlockSpec((BLK, 2 * kvw), lambda i: (i, blk_idx)),
        out_shape=jax.ShapeDtypeStruct(dz.shape, BF16),
        input_output_aliases={2: 0},
    )(dk_pad, dv_pad, dz)


def _relbias_bwd(dbias_tab, bucket):
    heads = dbias_tab.shape[0]

    def body(dt_ref, bk_ref, out_ref):
        lane = lax.broadcasted_iota(I32, (1, LANES), 1)
        bk = bk_ref[...]
        rows = []
        for h in range(heads):
            dt = dt_ref[h]
            acc = jnp.zeros((1, LANES), F32)
            for b in range(REL_BUCKETS):
                acc = jnp.where(lane == b, jnp.sum(jnp.where(bk == b, dt, 0.0)), acc)
            rows.append(acc)
        out_ref[...] = jnp.concatenate(rows, axis=0)

    return _pallas(body, name="relbias_bwd", out_shape=jax.ShapeDtypeStruct((heads, LANES), F32))(dbias_tab, bucket)


def _t5_bucket(rel):
    nb = REL_BUCKETS // 2
    ret = jnp.where(rel > 0, nb, 0)
    n = jnp.abs(rel)
    max_exact = nb // 2
    nf = jnp.maximum(n, 1).astype(F32)
    large = max_exact + (jnp.log(nf / max_exact) / math.log(REL_MAX_DIST / max_exact) * (nb - max_exact)).astype(I32)
    large = jnp.minimum(large, nb - 1)
    return ret + jnp.where(n < max_exact, n, large)


def _band_tables(rel_bias):
    qi = jnp.arange(BLK)[:, None]
    kj = jnp.arange(3 * BLK)[None, :]
    rel = kj - BLK - qi
    bucket = _t5_bucket(rel).astype(I32)
    heads = rel_bias.shape[1]
    masked = jnp.where(jnp.abs(rel) <= BLK, bucket, -1)

    def body(rb_ref, bk_ref, out_ref):
        bk = bk_ref[...]
        for h in range(heads):
            tab = jnp.full(bk.shape, NEG, F32)
            for b in range(REL_BUCKETS):
                tab = jnp.where(bk == b, rb_ref[b, h], tab)
            out_ref[h] = tab

    bias_tab = _pallas(
        body,
        name="bias_table",
        in_specs=[pl.BlockSpec(memory_space=pltpu.SMEM), pl.BlockSpec(memory_space=pltpu.VMEM)],
        out_specs=pl.BlockSpec(memory_space=pltpu.VMEM),
        out_shape=jax.ShapeDtypeStruct((heads, BLK, 3 * BLK), F32),
    )(rel_bias.astype(F32), masked)
    return bias_tab, bucket


EW_BLOCK_ELEMS = 512 * 1024


def _ew_tiles(shape, elems=EW_BLOCK_ELEMS // 2):
    r, c = shape
    tn = c if c <= 2048 else _tile(c, (2048, 1920, 1536, 1408, 1024, 512))
    tm = _tile(r, [t for t in (1024, 512, 256, 128, 64, 32, 16, 8) if t * tn <= elems] or [8])
    return tm, tn


def _cast_into_full(name, qidx, w, kind, after=()):
    r, c = w.shape
    tm, tn = _ew_tiles(w.shape, EW_BLOCK_ELEMS)
    nbi, nbj = r // tm, c // tn
    if kind == "col":
        full, out_spec = (r, c * N_CHIPS), pl.BlockSpec((tm, tn), lambda i, j, q: (i, q[0] * nbj + j))
    else:
        full, out_spec = (r * N_CHIPS, c), pl.BlockSpec((tm, tn), lambda i, j, q: (q[0] * nbi + i, j))

    def body(q_ref, w_ref, *rest):
        del q_ref
        rest[-1][...] = w_ref[...].astype(BF16)

    return _pallas(
        body,
        name=name,
        grid_spec=pltpu.PrefetchScalarGridSpec(
            num_scalar_prefetch=1,
            grid=(nbi, nbj),
            in_specs=[pl.BlockSpec((tm, tn), lambda i, j, q: (i, j))] + [ANY] * len(after),
            out_specs=out_spec,
        ),
        out_shape=jax.ShapeDtypeStruct(full, BF16),
    )(qidx, w, *after)


def _adamw(name, w, g, m, v, after=()):
    tm, tn = _ew_tiles(w.shape, EW_BLOCK_ELEMS)
    if _nbytes(w.shape, F32) <= 1024 * 1024:
        tm, tn = w.shape
    spec = pl.BlockSpec((tm, tn), lambda i, j: (i, j))
    n_after = len(after)

    def body(w_ref, g_ref, m_ref, v_ref, *rest):
        d_ref, nm_ref, nv_ref, g_out_ref = rest[n_after:]
        gv = g_ref[...]
        g_out_ref[...] = gv
        nm = ADAM_B1 * m_ref[...] + (1.0 - ADAM_B1) * gv
        nv = ADAM_B2 * v_ref[...] + (1.0 - ADAM_B2) * (gv * gv)
        m_hat = nm / (1.0 - ADAM_B1**ADAM_STEP)
        v_hat = nv / (1.0 - ADAM_B2**ADAM_STEP)
        d_ref[...] = -ADAM_LR * (m_hat / (jnp.sqrt(v_hat) + ADAM_EPS) + ADAM_WD * w_ref[...])
        nm_ref[...] = nm
        nv_ref[...] = nv

    out = jax.ShapeDtypeStruct(w.shape, F32)
    return _pallas(
        body, name=name, grid=(w.shape[0] // tm, w.shape[1] // tn), in_specs=[spec] * 4 + [ANY] * n_after,
        out_specs=[spec] * 4, out_shape=[out, out, out, out],
        compiler_params=_params(_mm_vmem([((tm, tn), F32, 24)])),
    )(w, g, m, v, *after)


def _pair_add(name, cidx, g_full, r_sib, kind):
    hr, hc = r_sib.shape
    tm, tn = _ew_tiles((hr, hc), 2 * EW_BLOCK_ELEMS)
    nbi, nbj = hr // tm, hc // tn
    if kind == "col":
        g_spec = pl.BlockSpec((tm, tn), lambda i, j, c: (c[0] * nbi + i, j))
    else:
        g_spec = pl.BlockSpec((tm, tn), lambda i, j, c: (i, c[0] * nbj + j))
    spec = pl.BlockSpec((tm, tn), lambda i, j, c: (i, j))

    def body(c_ref, g_ref, r_ref, o_ref):
        del c_ref
        o_ref[...] = (g_ref[...].astype(F32) + r_ref[...].astype(F32)).astype(BF16)

    return _pallas(
        body,
        name=name,
        grid_spec=pltpu.PrefetchScalarGridSpec(num_scalar_prefetch=1, grid=(nbi, nbj), in_specs=[g_spec, spec], out_specs=spec),
        out_shape=jax.ShapeDtypeStruct((hr, hc), BF16),
        compiler_params=_params(_mm_vmem([((tm, tn), BF16, 6), ((tm, tn), F32, 3)])),
    )(cidx, g_full, r_sib)


def _chip_sum(name, qidx, c_half, r_ici, kind):
    _, pr, pc = r_ici.shape
    tm, tn = _ew_tiles((pr, pc), 2 * EW_BLOCK_ELEMS)
    nbi, nbj = pr // tm, pc // tn
    if kind == "col":
        own_spec = pl.BlockSpec((tm, tn), lambda i, j, q: (i, q[0] * nbj + j))
        full, out_spec = (2 * pr, pc), pl.BlockSpec((tm, tn), lambda i, j, q: (q[1] * nbi + i, j))
    else:
        own_spec = pl.BlockSpec((tm, tn), lambda i, j, q: (q[0] * nbi + i, j))
        full, out_spec = (pr, 2 * pc), pl.BlockSpec((tm, tn), lambda i, j, q: (i, q[1] * nbj + j))

    def body(q_ref, own_ref, r_ref, o_ref):
        q = q_ref[0]
        own = own_ref[...].astype(F32)
        recv = [r_ref[r].astype(F32) for r in range(3)]
        total = None
        for chip in range(N_CHIPS):
            d = chip ^ q
            term = jnp.where(d == 0, own, jnp.where(d == 2, recv[0], jnp.where(d == 1, recv[1], recv[2])))
            total = term if total is None else total + term
        o_ref[...] = total

    return _pallas(
        body,
        name=name,
        grid_spec=pltpu.PrefetchScalarGridSpec(
            num_scalar_prefetch=1,
            grid=(nbi, nbj),
            in_specs=[own_spec, pl.BlockSpec((3, tm, tn), lambda i, j, q: (0, i, j))],
            out_specs=out_spec,
        ),
        out_shape=jax.ShapeDtypeStruct(full, F32),
        compiler_params=_params(_mm_vmem([((tm, tn), BF16, 8), ((tm, tn), F32, 6)])),
    )(qidx, c_half, r_ici)


_REL_MASK = (2, 1, 3)


def _place():
    x, y, c = lax.axis_index("x"), lax.axis_index("y"), lax.axis_index("c")
    chips = [(1 - x, y), (x, 1 - y), (1 - x, 1 - y)]
    return x, y, c, 2 * x + y, chips


def _shard_view(ref, kind, chip):
    if kind == "col":
        w = ref.shape[1] // N_CHIPS
        return ref.at[:, pl.ds(pl.multiple_of(chip * w, LANES), w)]
    h = ref.shape[0] // N_CHIPS
    return ref.at[pl.ds(pl.multiple_of(chip * h, 16), h), :]


def _row_half(ref, half):
    h = ref.shape[0] // 2
    return ref.at[pl.ds(pl.multiple_of(half * h, 16), h), :]


def _pair_half(ref, kind, half):
    if kind == "col":
        return _row_half(ref, half)
    w = ref.shape[1] // 2
    return ref.at[:, pl.ds(pl.multiple_of(half * w, LANES), w)]


def _remote(src, dst, send_sem, recv_sem, dev):
    return pltpu.make_async_remote_copy(src_ref=src, dst_ref=dst, send_sem=send_sem, recv_sem=recv_sem, device_id=dev, device_id_type=MESH)


def _hbm(a):
    return pltpu.with_memory_space_constraint(a, pltpu.HBM)


def _gather_start(name, fulls, kinds):
    n_w = len(fulls)

    def body(*refs):
        g = refs[:n_w]
        send_sem, recv_sem = refs[n_w], refs[n_w + 1]
        token = refs[-1]
        _, _, c, q, chips = _place()
        for w in range(n_w):
            mine = _row_half(_shard_view(g[w], kinds[w], q), c)
            for r, chip in enumerate(chips):
                _remote(mine, mine, send_sem.at[3 * w + r], recv_sem.at[3 * w + r], (*chip, c)).start()
        token[...] = jnp.zeros_like(token)

    res = _pallas(
        body,
        name=name,
        out_shape=(
            pltpu.SemaphoreType.DMA((3 * n_w,)),
            pltpu.SemaphoreType.DMA((3 * n_w,)),
            *[pltpu.HBM(f.shape, f.dtype) for f in fulls],
            jax.ShapeDtypeStruct((8, LANES), F32),
        ),
        in_specs=[HBM_SPEC] * n_w,
        out_specs=(SEM_SPEC, SEM_SPEC, *[HBM_SPEC] * n_w, pl.BlockSpec(memory_space=pltpu.VMEM)),
        input_output_aliases={w: w + 2 for w in range(n_w)},
        compiler_params=pltpu.CompilerParams(has_side_effects=EFFECT),
    )(*[_hbm(f) for f in fulls])
    return res[0], res[1], list(res[2 : 2 + n_w]), res[-1]


def _gather_wait(name, fulls, kinds, w_ids, send_sem, recv_sem, after, rels=(0, 1, 2)):
    n = len(fulls)

    def body(*refs):
        g = refs[:n]
        s_sem, r_sem = refs[n], refs[n + 1]
        x, y, c, q, _ = _place()
        for i, w in enumerate(w_ids):
            mine = _row_half(_shard_view(g[i], kinds[i], q), c)
            for r in rels:
                landed = _row_half(_shard_view(g[i], kinds[i], q ^ _REL_MASK[r]), c)
                cp = _remote(mine, landed, s_sem.at[3 * w + r], r_sem.at[3 * w + r], (x, y, 1 - c))
                cp.wait_send()
                cp.wait_recv()

    res = _pallas(
        body,
        name=name,
        out_shape=[pltpu.HBM(f.shape, f.dtype) for f in fulls],
        in_specs=[HBM_SPEC] * n + [SEM_SPEC, SEM_SPEC, ANY],
        out_specs=[HBM_SPEC] * n,
        input_output_aliases={i: i for i in range(n)},
        compiler_params=pltpu.CompilerParams(has_side_effects=EFFECT),
    )(*fulls, send_sem, recv_sem, after)
    return list(res)


def _gather_forward(name, fulls, kinds, rels=(0, 1, 2)):
    n = len(fulls)

    def body(*refs):
        g = refs[n : 2 * n]
        send, recv = refs[2 * n :]
        x, y, c, q, _ = _place()
        sib = (x, y, 1 - c)
        cps = []
        for i in range(n):
            for r in rels:
                landed = _row_half(_shard_view(g[i], kinds[i], q ^ _REL_MASK[r]), c)
                cps.append(_remote(landed, landed, send.at[i, r], recv.at[i, r], sib))
        for cp in cps:
            cp.start()
        for i in range(n):
            for r in rels:
                other = _row_half(_shard_view(g[i], kinds[i], q ^ _REL_MASK[r]), 1 - c)
                _remote(other, other, send.at[i, r], recv.at[i, r], sib).wait_recv()
        for cp in cps:
            cp.wait_send()

    res = _pallas(
        body,
        name=name,
        in_specs=[ANY] * n,
        out_specs=[ANY] * n,
        out_shape=[jax.ShapeDtypeStruct(f.shape, f.dtype) for f in fulls],
        scratch_shapes=[pltpu.SemaphoreType.DMA((n, 3)), pltpu.SemaphoreType.DMA((n, 3))],
        input_output_aliases={i: i for i in range(n)},
    )(*fulls)
    return list(res)


def _split_start(name, bufs, n_sems, copies):
    n = len(bufs)

    def body(*refs):
        for cp in copies(refs[:n], refs[n], refs[n + 1]):
            cp.start()

    res = _pallas(
        body,
        name=name,
        out_shape=(
            pltpu.SemaphoreType.DMA((n_sems,)),
            pltpu.SemaphoreType.DMA((n_sems,)),
            *[pltpu.HBM(b.shape, b.dtype) for b in bufs],
        ),
        in_specs=[HBM_SPEC] * n,
        out_specs=(SEM_SPEC, SEM_SPEC, *[HBM_SPEC] * n),
        input_output_aliases={i: i + 2 for i in range(n)},
        compiler_params=pltpu.CompilerParams(has_side_effects=EFFECT),
    )(*[_hbm(b) for b in bufs])
    return res[0], res[1], list(res[2:])


def _split_wait(name, bufs, send_sem, recv_sem, copies, after):
    n = len(bufs)

    def body(*refs):
        for cp in copies(refs[:n], refs[n], refs[n + 1]):
            cp.wait_send()
            cp.wait_recv()

    res = _pallas(
        body,
        name=name,
        out_shape=[pltpu.HBM(b.shape, b.dtype) for b in bufs],
        in_specs=[HBM_SPEC] * n + [SEM_SPEC, SEM_SPEC, ANY],
        out_specs=[HBM_SPEC] * n,
        input_output_aliases={i: i for i in range(n)},
        compiler_params=pltpu.CompilerParams(has_side_effects=EFFECT),
    )(*bufs, send_sem, recv_sem, after)
    return list(res)


def _pair_exchange_copies(kinds):
    n = len(kinds)

    def copies(refs, send_sem, recv_sem):
        x, y, c, _, _ = _place()
        return [
            _remote(_pair_half(refs[w], kinds[w], 1 - c), refs[n + w], send_sem.at[w], recv_sem.at[w], (x, y, 1 - c))
            for w in range(n)
        ]

    return copies


def _pair_share_copies(kinds, waiting):
    def copies(refs, send_sem, recv_sem):
        x, y, c, _, _ = _place()
        out = []
        for w, kind in enumerate(kinds):
            mine = _pair_half(refs[w], kind, c)
            dst = _pair_half(refs[w], kind, 1 - c) if waiting else mine
            out.append(_remote(mine, dst, send_sem.at[w], recv_sem.at[w], (x, y, 1 - c)))
        return out

    return copies


def _piece_shape(half_shape, kind):
    r, c = half_shape
    return (3, r, c // N_CHIPS) if kind == "col" else (3, r // N_CHIPS, c)


def _chip_send_start(name, halves, kinds):
    n = len(halves)
    lands = [lax.empty(_piece_shape(h.shape, k), BF16) for h, k in zip(halves, kinds)]

    def body(*refs):
        h, land = refs[:n], refs[n : 2 * n]
        send_sem, recv_sem = refs[2 * n], refs[2 * n + 1]
        _, _, c, q, chips = _place()
        for i in range(n):
            for r, chip in enumerate(chips):
                piece = _shard_view(h[i], kinds[i], q ^ _REL_MASK[r])
                _remote(piece, land[i].at[r], send_sem.at[3 * i + r], recv_sem.at[3 * i + r], (*chip, c)).start()

    res = _pallas(
        body,
        name=name,
        out_shape=(
            pltpu.SemaphoreType.DMA((3 * n,)),
            pltpu.SemaphoreType.DMA((3 * n,)),
            *[pltpu.HBM(a.shape, a.dtype) for a in halves],
            *[pltpu.HBM(a.shape, a.dtype) for a in lands],
        ),
        in_specs=[HBM_SPEC] * (2 * n),
        out_specs=(SEM_SPEC, SEM_SPEC, *[HBM_SPEC] * (2 * n)),
        input_output_aliases={i: i + 2 for i in range(2 * n)},
        compiler_params=pltpu.CompilerParams(has_side_effects=EFFECT),
    )(*[_hbm(a) for a in halves], *[_hbm(a) for a in lands])
    return res[0], res[1], list(res[2 : 2 + n]), list(res[2 + n :])


def _chip_send_wait(name, halves, lands, kinds, send_sem, recv_sem, after):
    n = len(halves)

    def body(*refs):
        h, land = refs[:n], refs[n : 2 * n]
        s_sem, r_sem = refs[2 * n], refs[2 * n + 1]
        x, y, c, q, _ = _place()
        for i in range(n):
            for r in range(3):
                piece = _shard_view(h[i], kinds[i], q ^ _REL_MASK[r])
                cp = _remote(piece, land[i].at[r], s_sem.at[3 * i + r], r_sem.at[3 * i + r], (x, y, 1 - c))
                cp.wait_send()
                cp.wait_recv()

    res = _pallas(
        body,
        name=name,
        out_shape=[pltpu.HBM(a.shape, a.dtype) for a in halves] + [pltpu.HBM(a.shape, a.dtype) for a in lands],
        in_specs=[HBM_SPEC] * (2 * n) + [SEM_SPEC, SEM_SPEC, ANY],
        out_specs=[HBM_SPEC] * (2 * n),
        input_output_aliases={i: i for i in range(2 * n)},
        compiler_params=pltpu.CompilerParams(has_side_effects=EFFECT),
    )(*halves, *lands, send_sem, recv_sem, after)
    return list(res[:n]), list(res[n:])


def _small_all_reduce(p):
    rows = p.shape[0]
    n_dev = 2 * N_CHIPS

    def body(p_ref, o_ref, buf, loc_sem, send, recv):
        x, y, c, q, _ = _place()
        me = 2 * q + c
        own = pltpu.make_async_copy(p_ref, buf.at[me], loc_sem)
        own.start()
        cps = []
        for d in range(1, n_dev):
            dev = (x ^ ((d >> 2) & 1), y ^ ((d >> 1) & 1), c ^ (d & 1))
            cps.append(_remote(p_ref, buf.at[me], send.at[d - 1], recv.at[d - 1], dev))
        for cp in cps:
            cp.start()
        for d in range(1, n_dev):
            slot = buf.at[me ^ d]
            _remote(slot, slot, send.at[d - 1], recv.at[d - 1], (x, y, c)).wait_recv()
        own.wait()
        total = buf[0]
        for d in range(1, n_dev):
            total = total + buf[d]
        o_ref[...] = total
        for cp in cps:
            cp.wait_send()

    return _pallas(
        body,
        name="small_all_reduce",
        in_specs=[ANY],
        out_specs=pl.BlockSpec(memory_space=pltpu.VMEM),
        out_shape=jax.ShapeDtypeStruct(p.shape, F32),
        scratch_shapes=[
            pltpu.VMEM((n_dev, rows, LANES), F32),
            pltpu.SemaphoreType.DMA,
            pltpu.SemaphoreType.DMA((n_dev - 1,)),
            pltpu.SemaphoreType.DMA((n_dev - 1,)),
        ],
    )(p)


def _small_exchange_copies(waiting):
    def copies(refs, send_sem, recv_sem):
        p, land = refs
        x, y, c, q, _ = _place()
        me = 2 * q + c
        out = []
        for dd in range(1, 2 * N_CHIPS):
            dev = (x ^ ((dd >> 2) & 1), y ^ ((dd >> 1) & 1), c ^ (dd & 1))
            dst = land.at[me ^ dd] if waiting else land.at[me]
            out.append(_remote(p, dst, send_sem.at[dd - 1], recv_sem.at[dd - 1], dev))
        return out

    return copies


def _small_sum(name, me_idx, p, land):
    rows = p.shape[0]
    n_dev = 2 * N_CHIPS

    def body(me_ref, p_ref, land_ref, o_ref):
        me = me_ref[0]
        total = None
        for dev in range(n_dev):
            term = jnp.where(me == dev, p_ref[...], land_ref[dev])
            total = term if total is None else total + term
        o_ref[...] = total

    return _pallas(
        body,
        name=name,
        grid_spec=pltpu.PrefetchScalarGridSpec(
            num_scalar_prefetch=1,
            grid=(1,),
            in_specs=[pl.BlockSpec((rows, LANES), lambda i, m: (0, 0)), pl.BlockSpec((n_dev, rows, LANES), lambda i, m: (0, 0, 0))],
            out_specs=pl.BlockSpec((rows, LANES), lambda i, m: (0, 0)),
        ),
        out_shape=jax.ShapeDtypeStruct(p.shape, F32),
    )(me_idx, p, land)


def _pack(parts):
    rows = []
    for a in parts:
        flat = a.reshape(-1).astype(F32)
        n = flat.shape[0]
        padded = -(-n // (8 * LANES)) * (8 * LANES)
        rows.append(jnp.pad(flat, (0, padded - n)).reshape(-1, LANES))
    return jnp.concatenate(rows, axis=0)


def _unpack(packed, shapes):
    out, row = [], 0
    for shp in shapes:
        n = int(np.prod(shp))
        nrows = -(-n // (8 * LANES)) * 8
        out.append(packed[row : row + nrows].reshape(-1)[:n].reshape(shp))
        row += nrows
    return out


def kernel(x, w_in, norm_mix, sgu_v_gain, sgu_w_s, sgu_b_s, w_a_out, attn_sink, rel_bias, w_b_out, w_o, norm_ffn, w_gate, w_up, w_down, norm_final, loss_target, m_w_in, m_norm_mix, m_sgu_v_gain, m_sgu_w_s, m_sgu_b_s, m_w_a_out, m_attn_sink, m_rel_bias, m_w_b_out, m_w_o, m_norm_ffn, m_w_gate, m_w_up, m_w_down, m_norm_final, v_w_in, v_norm_mix, v_sgu_v_gain, v_sgu_w_s, v_sgu_b_s, v_w_a_out, v_attn_sink, v_rel_bias, v_w_b_out, v_w_o, v_norm_ffn, v_w_gate, v_w_up, v_w_down, v_norm_final):
    s, d = x.shape[1], x.shape[2]
    w_sgu = sgu_v_gain.shape[1]
    groups = sgu_w_s.shape[1]
    heads = attn_sink.shape[1]
    grp = heads // N_KV_HEADS
    w_att = heads * HEAD_DIM
    w_kv = N_KV_HEADS * HEAD_DIM
    d_ff = w_gate.shape[2] * N_CHIPS
    n_in = w_in.shape[2] * N_CHIPS
    off_q = 2 * w_sgu
    off_k = off_q + w_att
    off_g = off_k + 2 * w_kv
    assert n_in == off_g + 2 * d and groups * BLK == w_sgu and s % BLK == 0

    x2d = x.reshape(s, d)
    tgt = loss_target.reshape(s, d)
    c_idx = lax.axis_index("c").astype(I32).reshape(1)
    q_idx = (2 * lax.axis_index("x") + lax.axis_index("y")).astype(I32).reshape(1)
    qc_idx = jnp.concatenate([q_idx, c_idx])

    W_IN, W_A, W_B, W_O, W_GATE, W_UP, W_DOWN = range(7)
    names = ["w_in", "w_a", "w_b", "w_o", "w_gate", "w_up", "w_down"]
    kinds = ["col", "col", "col", "row", "col", "col", "row"]
    big_w = [w_in[0], w_a_out[0], w_b_out[0], w_o[0], w_gate[0], w_up[0], w_down[0]]
    big_m = [m_w_in[0], m_w_a_out[0], m_w_b_out[0], m_w_o[0], m_w_gate[0], m_w_up[0], m_w_down[0]]
    big_v = [v_w_in[0], v_w_a_out[0], v_w_b_out[0], v_w_o[0], v_w_gate[0], v_w_up[0], v_w_down[0]]
    full_in = _cast_into_full("cast_w_in", q_idx, big_w[W_IN], kinds[W_IN])
    in_send, in_recv, (full_in,), token = _gather_start("gather_start_in", [full_in], [kinds[W_IN]])
    rest = [_cast_into_full("cast_" + names[i], q_idx, big_w[i], kinds[i], after=(token,)) for i in range(1, 7)]
    ag_send, ag_recv, rest, token = _gather_start("gather_start_rest", rest, kinds[1:])
    fulls = [full_in] + rest

    def gathered(tag, ids, after):
        if ids == [W_IN]:
            sems, pos = (in_send, in_recv), [0]
        else:
            sems, pos = (ag_send, ag_recv), [i - 1 for i in ids]
        got = _gather_wait("gather_wait_" + tag, [fulls[i] for i in ids], [kinds[i] for i in ids], pos, *sems, after)
        return _gather_forward("gather_fwd_" + tag, got, [kinds[i] for i in ids])

    ws_b = sgu_w_s[0].astype(BF16)
    wst_b = jnp.swapaxes(sgu_w_s[0], 1, 2).astype(BF16)
    b_col = sgu_b_s[0].reshape(groups, BLK, 1)
    bias_tab, bucket = _band_tables(rel_bias)
    sink = attn_sink[0]

    tm = _tile(s, (1024, 512, 256, 128))

    h1 = _rms_fwd("rms_mix", x2d, norm_mix, after=(token,))
    (g_in,) = gathered("in", [W_IN], h1)

    tn = _tile(n_in, (768, 640, 512))
    z = _mm(
        "mm_z", (s // tm, n_in // tn, 1), [h1, g_in],
        [pl.BlockSpec((tm, d), lambda i, j, k: (i, 0)), pl.BlockSpec((d, tn), lambda i, j, k: (0, j))],
        [jax.ShapeDtypeStruct((s, n_in), BF16)], [pl.BlockSpec((tm, tn), lambda i, j, k: (i, j))],
        [(0, 1, NN, 0)], 1, (tm, tn), 1, lambda ins, vals, outs, cs: _put(outs[0], cs, vals[0]),
        _mm_vmem([((tm, d), BF16, 2), ((d, tn), BF16, 2), ((tm, tn), F32, 3)]),
    )[0]
    g_a, g_b, g_o = gathered("mix", [W_A, W_B, W_O], z)

    a_act = _sgu_fwd(z, sgu_v_gain, ws_b, b_col, w_sgu)

    kv_b = z[:, off_k:off_g]
    k_pad = jnp.pad(kv_b[:, :w_kv], ((BLK, BLK), (0, 0)))
    v_pad = jnp.pad(kv_b[:, w_kv:], ((BLK, BLK), (0, 0)))
    q_blk0 = off_q // (grp * HEAD_DIM)
    att = _attn_fwd(sink, z, k_pad, v_pad, bias_tab, grp, q_blk0)

    tg = _tile(d, (512,))
    ga0, gb0 = off_g // tg, (off_g + d) // tg

    def ep_gate(ins, vals, outs, cs):
        sa, sb = _sigmoid(ins[4][:, cs].astype(F32)), _sigmoid(ins[5][:, cs].astype(F32))
        _put(outs[0], cs, sa * vals[0] + sb * vals[1])
        _put(outs[1], cs, vals[0])
        _put(outs[2], cs, vals[1])

    t_out = pl.BlockSpec((tm, tg), lambda i, j, k: (i, j))
    m_act, y_a, y_b = _mm(
        "mm_branches", (s // tm, d // tg, 1), [a_act, g_a, att, g_b, z, z],
        [pl.BlockSpec((tm, w_sgu), lambda i, j, k: (i, 0)), pl.BlockSpec((w_sgu, tg), lambda i, j, k: (0, j)),
         pl.BlockSpec((tm, w_att), lambda i, j, k: (i, 0)), pl.BlockSpec((w_att, tg), lambda i, j, k: (0, j)),
         pl.BlockSpec((tm, tg), lambda i, j, k: (i, ga0 + j)), pl.BlockSpec((tm, tg), lambda i, j, k: (i, gb0 + j))],
        [jax.ShapeDtypeStruct((s, d), BF16)] * 3,
        [t_out, t_out, t_out], [(0, 1, NN, 0), (2, 3, NN, 1)], 2, (tm, tg), 1, ep_gate,
        _mm_vmem([((tm, w_sgu), BF16, 4), ((w_sgu, tg), BF16, 4), ((tm, tg), F32, 12)]),    )

    tn = _tile(d, (1024, 512))

    def ep_residual(ins, vals, outs, cs):
        _put(outs[0], cs, ins[2][:, cs] + vals[0])

    x2 = _mm(
        "mm_wo", (s // tm, d // tn, 1), [m_act, g_o, x2d],
        [pl.BlockSpec((tm, d), lambda i, j, k: (i, 0)), pl.BlockSpec((d, tn), lambda i, j, k: (0, j)),
         pl.BlockSpec((tm, tn), lambda i, j, k: (i, j))],
        [jax.ShapeDtypeStruct((s, d), F32)], [pl.BlockSpec((tm, tn), lambda i, j, k: (i, j))],
        [(0, 1, NN, 0)], 1, (tm, tn), 1, ep_residual,
        _mm_vmem([((tm, d), BF16, 2), ((d, tn), BF16, 2), ((tm, tn), F32, 5)]),    )[0]

    h2 = _rms_fwd("rms_ffn", x2, norm_ffn)

    tf = _tile(d_ff, (512,))
    f_out = pl.BlockSpec((tm, tf), lambda i, j, k: (i, j))
    tkf = _tile(d_ff, (1408, 1024, 512))

    def by_quarter(tag, ids, after, work):
        bufs, ks, pos = [fulls[i] for i in ids], [kinds[i] for i in ids], [i - 1 for i in ids]
        for step, rel in enumerate((None, 0, 1, 2)):
            quarter = q_idx
            if rel is not None:
                bufs = _gather_wait(f"gather_wait_{tag}{rel}", bufs, ks, pos, ag_send, ag_recv, after, rels=(rel,))
                bufs = _gather_forward(f"gather_fwd_{tag}{rel}", bufs, ks, rels=(rel,))
                quarter = q_idx ^ _REL_MASK[rel]
            after = work(step, quarter, bufs)
        return bufs

    acts = []

    def gate_up_step(step, quarter, bufs):
        acts[:] = _gate_up_quarter(f"mm_gate_up{step}", quarter, h2, bufs[0], bufs[1], acts or None)
        return acts[2]

    g_gate, g_up = by_quarter("ffn_in", [W_GATE, W_UP], h2, gate_up_step)
    gt, up, f_act = acts
    x_acc = [x2]

    def down_step(step, quarter, bufs):
        x_acc[0] = _down_quarter(f"mm_down{step}", quarter, f_act, bufs[0], x_acc[0], in_place=step > 0)
        return x_acc[0]

    (g_down,) = by_quarter("ffn_out", [W_DOWN], f_act, down_step)
    x3 = x_acc[0]

    dx3, dx3b, dg_final, loss_part = _head(x3, norm_final.reshape(1, d), tgt)

    def reduce_a(tag, ids, grads):
        ks = [kinds[i] for i in ids]
        lands = [lax.empty((g.shape[0] // 2, g.shape[1]) if k == "col" else (g.shape[0], g.shape[1] // 2), BF16)
                 for g, k in zip(grads, ks)]
        send, recv, bufs = _split_start("pair_send_" + tag, list(grads) + lands, len(ids), _pair_exchange_copies(ks))
        return {"tag": tag, "ids": ids, "ks": ks, "pair": (send, recv, bufs), "token": bufs[0]}

    def reduce_b(st, after):
        tag, ids, ks = st["tag"], st["ids"], st["ks"]
        send, recv, bufs = st["pair"]
        bufs = _split_wait("pair_wait_" + tag, bufs, send, recv, _pair_exchange_copies(ks), after)
        grads, from_sib = bufs[: len(ids)], bufs[len(ids) :]
        halves = [_pair_add("pair_add_" + names[i], c_idx, g, r, k) for i, g, r, k in zip(ids, grads, from_sib, ks)]
        st["chip"] = _chip_send_start("chip_send_" + tag, halves, ks)
        st["token"] = st["chip"][2][0]

    def reduce_c(st, after):
        tag, ids, ks = st["tag"], st["ids"], st["ks"]
        send, recv, halves, lands = st["chip"]
        halves, lands = _chip_send_wait("chip_wait_" + tag, halves, lands, ks, send, recv, after)
        pieces = [_chip_sum("chip_sum_" + names[i], qc_idx, h, r, k) for i, h, r, k in zip(ids, halves, lands, ks)]
        st["share"] = _split_start("share_send_" + tag, pieces, len(ids), _pair_share_copies(ks, False))
        st["token"] = st["share"][2][0]

    def reduce_d(st, after):
        send, recv, bufs = st["share"]
        return _split_wait("share_wait_" + st["tag"], bufs, send, recv, _pair_share_copies(st["ks"], True), after)

    grads_big, upd = [None] * 7, [None] * 7

    def finish(st, after):
        shared = reduce_d(st, after)
        after = shared[0]
        for i, g in zip(st["ids"], shared):
            upd[i] = _adamw("adamw_" + names[i], big_w[i], g, big_m[i], big_v[i], after=(after,))
            grads_big[i] = upd[i][3]
            after = upd[i][0]
        return after

    def ep_swiglu_bwd(ins, vals, outs, cs):
        df = vals[0]
        gtv, upv = ins[2][:, cs].astype(F32), ins[3][:, cs].astype(F32)
        sg = _sigmoid(gtv)
        _put(outs[0], cs, df * upv * (sg + gtv * sg * (1.0 - sg)))
        _put(outs[1], cs, df * (gtv * sg))

    dgt, dup = _mm(
        "mm_dswiglu", (s // tm, d_ff // tf, 1), [dx3b, g_down, gt, up],
        [pl.BlockSpec((tm, d), lambda i, j, k: (i, 0)), pl.BlockSpec((tf, d), lambda i, j, k: (j, 0)), f_out, f_out],
        [jax.ShapeDtypeStruct((s, d_ff), BF16), jax.ShapeDtypeStruct((s, d_ff), BF16)], [f_out, f_out],
        [(0, 1, NT, 0)], 1, (tm, tf), 1, ep_swiglu_bwd,
        _mm_vmem([((tm, d), BF16, 2), ((tf, d), BF16, 2), ((tm, tf), F32, 8)]),    )

    def ep_store(ins, vals, outs, cs):
        for o, v in zip(outs, vals):
            _put(o, cs, v)

    twn = _tile(d, (1024, 512))
    gw_down = _mm(
        "mm_gw_down", (d_ff // tkf, d // twn, 1), [f_act, dx3b],
        [pl.BlockSpec((s, tkf), lambda i, j, k: (0, i)), pl.BlockSpec((s, twn), lambda i, j, k: (0, j))],
        [jax.ShapeDtypeStruct((d_ff, d), BF16)], [pl.BlockSpec((tkf, twn), lambda i, j, k: (i, j))],
        [(0, 1, TN, 0)], 1, (tkf, twn), 1, ep_store,
        _mm_vmem([((s, tkf), BF16, 3), ((s, twn), BF16, 2), ((tkf, twn), F32, 3)]),
    )[0]
    red_down = reduce_a("down", [W_DOWN], [gw_down])

    tn2 = _tile(d, (256,))
    dh2_specs = [pl.BlockSpec((tm, d_ff), lambda i, j, k: (i, 0)), pl.BlockSpec((tn2, d_ff), lambda i, j, k: (j, 0))]
    dh2_tile = pl.BlockSpec((tm, tn2), lambda i, j, k: (i, j))
    dh2_vmem = _mm_vmem([((tm, d_ff), BF16, 2), ((tn2, d_ff), BF16, 2), ((tm, tn2), F32, 7)])
    dh2 = _mm(
        "mm_dh2_gate", (s // tm, d // tn2, 1), [dgt, g_gate], dh2_specs,
        [jax.ShapeDtypeStruct((s, d), F32)], [dh2_tile], [(0, 1, NT, 0)], 1, (tm, tn2), 1, ep_store, dh2_vmem,
        after=(red_down["token"],),
    )[0]
    dh2 = _mm(
        "mm_dh2_up", (s // tm, d // tn2, 1), [dup, g_up, dh2], dh2_specs + [dh2_tile],
        [jax.ShapeDtypeStruct((s, d), F32)], [dh2_tile], [(0, 1, NT, 0)], 1, (tm, tn2), 1, ep_residual, dh2_vmem,
    )[0]
    reduce_b(red_down, dh2)

    twr = _tile(d, (1024, 512))
    w_tile = pl.BlockSpec((twr, tf), lambda i, j, k: (i, j))
    gw_gate, gw_up = _mm(
        "mm_gw_gate_up", (d // twr, d_ff // tf, 1), [h2, dgt, dup],
        [pl.BlockSpec((s, twr), lambda i, j, k: (0, i)), pl.BlockSpec((s, tf), lambda i, j, k: (0, j)),
         pl.BlockSpec((s, tf), lambda i, j, k: (0, j))],
        [jax.ShapeDtypeStruct((d, d_ff), BF16), jax.ShapeDtypeStruct((d, d_ff), BF16)], [w_tile, w_tile],
        [(0, 1, TN, 0), (0, 2, TN, 1)], 2, (twr, tf), 1, ep_store,
        _mm_vmem([((s, twr), BF16, 3), ((s, tf), BF16, 4), ((twr, tf), F32, 6)]),
        after=(red_down["token"],),
    )
    red_ffn = reduce_a("ffn_in", [W_GATE, W_UP], [gw_gate, gw_up])

    dx2, dx2b, dg_ffn = _rms_bwd("rms_ffn_bwd", x2, norm_ffn, dh2, dx3, after=(red_ffn["token"],))

    nj = d // tg

    def lo(j):
        return jnp.minimum(j, nj - 1)

    def gate_bwd_body(dx_ref, wo_ref, ga_ref, gb_ref, ya_ref, yb_ref, dya_ref, dyb_ref, dz_ref, keep):
        j = pl.program_id(1)

        @pl.when(j < nj)
        def _():
            dm = lax.dot_general(dx_ref[...], wo_ref[...], NT, preferred_element_type=F32)
            sa, sb = _sigmoid(ga_ref[...].astype(F32)), _sigmoid(gb_ref[...].astype(F32))
            dya_ref[...] = (dm * sa).astype(BF16)
            dyb_ref[...] = (dm * sb).astype(BF16)
            dz_ref[...] = (dm * ya_ref[...].astype(F32) * (sa * (1.0 - sa))).astype(BF16)
            keep[lo(j)] = (dm * yb_ref[...].astype(F32) * (sb * (1.0 - sb))).astype(BF16)

        @pl.when(j >= nj)
        def _():
            dz_ref[...] = keep[jnp.maximum(j - nj, 0)]

    t_lo = pl.BlockSpec((tm, tg), lambda i, j: (i, lo(j)))
    dya, dyb, dz = _pallas(
        gate_bwd_body,
        name="mm_dgate",
        grid=(s // tm, 2 * nj),
        in_specs=[
            pl.BlockSpec((tm, d), lambda i, j: (i, 0)),
            pl.BlockSpec((tg, d), lambda i, j: (lo(j), 0)),
            pl.BlockSpec((tm, tg), lambda i, j: (i, ga0 + lo(j))),
            pl.BlockSpec((tm, tg), lambda i, j: (i, gb0 + lo(j))),
            t_lo,
            t_lo,
        ],
        out_specs=[t_lo, t_lo, pl.BlockSpec((tm, tg), lambda i, j: (i, ga0 + j))],
        out_shape=[jax.ShapeDtypeStruct((s, d), BF16), jax.ShapeDtypeStruct((s, d), BF16), jax.ShapeDtypeStruct((s, n_in), BF16)],
        scratch_shapes=[pltpu.VMEM((nj, tm, tg), BF16)],
        compiler_params=_params(_mm_vmem([((tm, d), BF16, 2), ((tg, d), BF16, 2), ((tm, tg), F32, 14), ((nj, tm, tg), BF16, 1)])),
    )(dx2b, g_o, z, z, y_a, y_b)
    reduce_b(red_ffn, dya)
    reduce_c(red_down, red_ffn["token"])

    gw_o = _mm(
        "mm_gw_o", (d // twr, d // twn, 1), [m_act, dx2b],
        [pl.BlockSpec((s, twr), lambda i, j, k: (0, i)), pl.BlockSpec((s, twn), lambda i, j, k: (0, j))],
        [jax.ShapeDtypeStruct((d, d), BF16)], [pl.BlockSpec((twr, twn), lambda i, j, k: (i, j))],
        [(0, 1, TN, 0)], 1, (twr, twn), 1, ep_store,
        _mm_vmem([((s, twr), BF16, 3), ((s, twn), BF16, 2), ((twr, twn), F32, 3)]),
        after=(red_down["token"],),
    )[0]
    red_o = reduce_a("w_o", [W_O], [gw_o])
    finish(red_down, red_o["token"])

    tb = _tile(w_sgu, (1024, 512))
    b_out = pl.BlockSpec((tm, tb), lambda i, j, k: (i, j))

    da, datt = _mm(
        "mm_dbranches", (s // tm, w_sgu // tb, 1), [dya, g_a, dyb, g_b],
        [pl.BlockSpec((tm, d), lambda i, j, k: (i, 0)), pl.BlockSpec((tb, d), lambda i, j, k: (j, 0)),
         pl.BlockSpec((tm, d), lambda i, j, k: (i, 0)), pl.BlockSpec((tb, d), lambda i, j, k: (j, 0))],
        [jax.ShapeDtypeStruct((s, w_sgu), BF16), jax.ShapeDtypeStruct((s, w_att), BF16)], [b_out, b_out],
        [(0, 1, NT, 0), (2, 3, NT, 1)], 2, (tm, tb), 1, ep_store,
        _mm_vmem([((tm, d), BF16, 4), ((tb, d), BF16, 4), ((tm, tb), F32, 6)]),        after=(red_o["token"],),
    )
    reduce_b(red_o, da)

    wb_tile = pl.BlockSpec((tb, twn), lambda i, j, k: (i, j))
    gw_a, gw_b = _mm(
        "mm_gw_branches", (w_sgu // tb, d // twn, 1), [a_act, dya, att, dyb],
        [pl.BlockSpec((s, tb), lambda i, j, k: (0, i)), pl.BlockSpec((s, twn), lambda i, j, k: (0, j)),
         pl.BlockSpec((s, tb), lambda i, j, k: (0, i)), pl.BlockSpec((s, twn), lambda i, j, k: (0, j))],
        [jax.ShapeDtypeStruct((w_sgu, d), BF16), jax.ShapeDtypeStruct((w_att, d), BF16)], [wb_tile, wb_tile],
        [(0, 1, TN, 0), (2, 3, TN, 1)], 2, (tb, twn), 1, ep_store,
        _mm_vmem([((s, tb), BF16, 5), ((s, twn), BF16, 4), ((tb, twn), F32, 6)]),
        after=(red_o["token"],),
    )
    red_mix = reduce_a("mix", [W_A, W_B], [gw_a, gw_b])

    dz, dws, dbs, dgain = _sgu_bwd(z, da, sgu_v_gain, ws_b, wst_b, b_col, w_sgu, dz, after=(red_mix["token"],))
    dz, dk_pad, dv_pad, dbias_tab, dsink = _attn_bwd(sink, z, k_pad, v_pad, bias_tab, datt, dz, grp, q_blk0)
    dz = _dkv_to_dz(dk_pad, dv_pad, dz, off_k // (2 * w_kv))
    drel = _relbias_bwd(dbias_tab, bucket)
    reduce_b(red_mix, dz)
    reduce_c(red_ffn, red_mix["token"])

    small_w = [norm_mix, sgu_v_gain, sgu_w_s, sgu_b_s, attn_sink, rel_bias, norm_ffn, norm_final]
    small_m = [m_norm_mix, m_sgu_v_gain, m_sgu_w_s, m_sgu_b_s, m_attn_sink, m_rel_bias, m_norm_ffn, m_norm_final]
    small_v = [v_norm_mix, v_sgu_v_gain, v_sgu_w_s, v_sgu_b_s, v_attn_sink, v_rel_bias, v_norm_ffn, v_norm_final]
    small_shapes = [w.shape for w in small_w]
    early = [dgain, dws, dbs, dsink[:, 0], drel[:, :REL_BUCKETS].T, dg_ffn, dg_final]
    p_early = _pack([g.reshape(shp) for g, shp in zip(early, small_shapes[1:])] + [loss_part[0, :1]])
    land = jnp.zeros((2 * N_CHIPS,) + p_early.shape, F32)
    sm_send, sm_recv, (p_early, land) = _split_start("small_send", [p_early, land], 2 * N_CHIPS - 1, _small_exchange_copies(False))

    tzn = _tile(n_in, (768, 640, 512))
    gw_in = _mm(
        "mm_gw_in", (d // twr, n_in // tzn, 1), [h1, dz],
        [pl.BlockSpec((s, twr), lambda i, j, k: (0, i)), pl.BlockSpec((s, tzn), lambda i, j, k: (0, j))],
        [jax.ShapeDtypeStruct((d, n_in), BF16)], [pl.BlockSpec((twr, tzn), lambda i, j, k: (i, j))],
        [(0, 1, TN, 0)], 1, (twr, tzn), 1, ep_store,
        _mm_vmem([((s, twr), BF16, 3), ((s, tzn), BF16, 2), ((twr, tzn), F32, 3)]),
        after=(red_ffn["token"], p_early),
    )[0]
    red_in = reduce_a("w_in", [W_IN], [gw_in])

    reduce_c(red_o, red_in["token"])
    reduce_c(red_mix, red_o["token"])
    reduce_b(red_in, finish(red_mix, finish(red_o, red_mix["token"])))

    dh1 = _mm(
        "mm_dh1", (s // tm, d // tn2, 1), [dz, g_in],
        [pl.BlockSpec((tm, n_in), lambda i, j, k: (i, 0)), pl.BlockSpec((tn2, n_in), lambda i, j, k: (j, 0))],
        [jax.ShapeDtypeStruct((s, d), F32)], [pl.BlockSpec((tm, tn2), lambda i, j, k: (i, j))],
        [(0, 1, NT, 0)], 1, (tm, tn2), 1, ep_store,
        _mm_vmem([((tm, n_in), BF16, 2), ((tn2, n_in), BF16, 2), ((tm, tn2), F32, 5)]),
        after=(red_in["token"],),
    )[0]

    grad_x, _, dg_mix = _rms_bwd("rms_mix_bwd", x2d, norm_mix, dh1, dx2)

    p_mix = _pack([dg_mix.reshape(small_shapes[0])])
    land_mix = jnp.zeros((2 * N_CHIPS,) + p_mix.shape, F32)
    mx_send, mx_recv, (p_mix, land_mix) = _split_start("mix_send", [p_mix, land_mix], 2 * N_CHIPS - 1, _small_exchange_copies(False))

    reduce_c(red_in, finish(red_ffn, p_mix))
    p_early, land = _split_wait("small_wait", [p_early, land], sm_send, sm_recv, _small_exchange_copies(True), red_in["token"])
    p_mix, land_mix = _split_wait("mix_wait", [p_mix, land_mix], mx_send, mx_recv, _small_exchange_copies(True), p_early)
    me_idx = 2 * q_idx + c_idx
    packed_g = jnp.concatenate([_small_sum("mix_sum", me_idx, p_mix, land_mix), _small_sum("small_sum", me_idx, p_early, land)], axis=0)
    g_small = _unpack(packed_g, small_shapes + [(1,)])
    loss = g_small[-1].reshape(())
    g_small = g_small[:-1]
    zero1 = jnp.zeros((1,), F32)
    pw, pg, pm, pv = _pack(small_w + [zero1]), _pack(g_small + [zero1]), _pack(small_m + [zero1]), _pack(small_v + [zero1])
    small_upd = _adamw("adamw_small", pw, pg, pm, pv)
    d_small, nm_small, nv_small = [_unpack(a, small_shapes) for a in small_upd[:3]]
    finish(red_in, small_upd[0])

    small_names = ["norm_mix", "sgu_v_gain", "sgu_w_s", "sgu_b_s", "attn_sink", "rel_bias", "norm_ffn", "norm_final"]
    table = {}
    for i, n in enumerate(names):
        table[n] = (grads_big[i][None], upd[i][0][None], upd[i][1][None], upd[i][2][None])
    for i, n in enumerate(small_names):
        table[n] = (g_small[i], d_small[i], nm_small[i], nv_small[i])
    order = ["w_in", "norm_mix", "sgu_v_gain", "sgu_w_s", "sgu_b_s", "w_a", "attn_sink", "rel_bias", "w_b", "w_o", "norm_ffn",
             "w_gate", "w_up", "w_down", "norm_final"]
    outs = [loss, grad_x.reshape(1, s, d)]
    for part in range(4):
        outs += [table[n][part] for n in order]
    return tuple(outs)
```

```python
import math

import jax
import jax.numpy as jnp
import numpy as np
from jax import lax
from jax.experimental import pallas as pl
from jax.experimental.pallas import tpu as pltpu

F32 = jnp.float32
BF16 = jnp.bfloat16
I32 = jnp.int32
MESH = pl.DeviceIdType.MESH

EPS = 1e-6
NEG = -1e30
BLK = 128
HEAD_DIM = 128
N_KV_HEADS = 2
REL_BUCKETS = 32
REL_MAX_DIST = 128
N_CHIPS = 4
ADAM_LR, ADAM_B1, ADAM_B2, ADAM_EPS, ADAM_WD, ADAM_STEP = 0.001, 0.9, 0.999, 1e-08, 0.01, 10

LANES = 128
MXU_COLS = 256
VMEM_CAP = 60 * 1024 * 1024

NN = (((1,), (0,)), ((), ()))
NT = (((1,), (1,)), ((), ()))
TN = (((0,), (0,)), ((), ()))
ANY = pl.BlockSpec(memory_space=pl.ANY)
HBM_SPEC = pl.BlockSpec(memory_space=pltpu.HBM)
SEM_SPEC = pl.BlockSpec(memory_space=pltpu.SEMAPHORE)
EFFECT = pltpu.SideEffectType.DATAFLOW_SIDE_EFFECTING


def _tile(n, cands):
    for t in cands:
        if n % t == 0:
            return t
    return n


PIN_BYTES = 64 * 1024


def _pin_hbm(a):
    big = hasattr(a, "dtype") and jnp.issubdtype(a.dtype, jnp.floating) and _nbytes(a.shape, a.dtype) >= PIN_BYTES
    return pltpu.with_memory_space_constraint(a, pltpu.HBM) if big else a


def _pallas(body, *, out_shape, **kw):
    def pin(o):
        big = isinstance(o, jax.ShapeDtypeStruct) and jnp.issubdtype(o.dtype, jnp.floating) and _nbytes(o.shape, o.dtype) >= PIN_BYTES
        return pltpu.HBM(o.shape, o.dtype) if big else o

    shapes = type(out_shape)(pin(o) for o in out_shape) if isinstance(out_shape, (list, tuple)) else pin(out_shape)
    call = pl.pallas_call(body, out_shape=shapes, **kw)
    return lambda *args: call(*[_pin_hbm(a) for a in args])


def _params(vmem_bytes=None, **kw):
    if vmem_bytes is not None:
        kw["vmem_limit_bytes"] = int(min(max(vmem_bytes, 32 * 1024 * 1024), VMEM_CAP))
    return pltpu.CompilerParams(**kw)


def _nbytes(shape, dtype):
    return int(np.prod(shape)) * jnp.dtype(dtype).itemsize


def _sigmoid(x):
    return 1.0 / (1.0 + jnp.exp(-x))


_GC = 0.7978845608028654
_GA = 0.044715


def _gelu(x):
    return 0.5 * x * (1.0 + jnp.tanh(_GC * (x + _GA * (x * x * x))))


def _gelu_grad(x):
    t = jnp.tanh(_GC * (x + _GA * (x * x * x)))
    return 0.5 * (1.0 + t) + 0.5 * x * (1.0 - t * t) * (_GC * (1.0 + 3.0 * _GA * (x * x)))


def _bf(v):
    return v if v.dtype == BF16 else v.astype(BF16)


def _mm(name, grid, ins, in_specs, out_shape, out_specs, pairs, n_acc, tile, nk, epilogue, vmem_bytes, after=(), col_chunks=1):
    assert nk == 1 and tile[1] % col_chunks == 0
    n_in, n_out = len(ins) + len(after), len(out_shape)
    width = tile[1] // col_chunks

    def body(*refs):
        in_refs, out_refs = refs[:n_in], refs[n_in : n_in + n_out]
        for ch in range(col_chunks):
            cs = slice(ch * width, (ch + 1) * width) if col_chunks > 1 else slice(None)
            vals = [None] * n_acc
            for a_i, b_i, dn, acc_i in pairs:
                rhs = in_refs[b_i][cs, :] if dn == NT else in_refs[b_i][:, cs]
                d = lax.dot_general(_bf(in_refs[a_i][...]), _bf(rhs), dn, preferred_element_type=F32)
                vals[acc_i] = d if vals[acc_i] is None else vals[acc_i] + d
            epilogue(in_refs, vals, out_refs, cs)

    return _pallas(
        body,
        name=name,
        grid=grid,
        in_specs=list(in_specs) + [ANY] * len(after),
        out_specs=out_specs,
        out_shape=out_shape,
        compiler_params=_params(vmem_bytes),
    )(*ins, *after)


def _put(ref, cs, v):
    ref[:, cs] = v.astype(ref.dtype)


def _gate_up_quarters(name, quarters, nq, h2, g_gate, g_up, prev):
    s, d = h2.shape
    d_ff = g_gate.shape[1]
    wq = d_ff // N_CHIPS
    tm = _tile(s, (512, 256, 128))
    n_prev = 0 if prev is None else 3

    def body(q_ref, h_ref, wg_ref, wu_ref, *rest):
        del q_ref
        gt_ref, up_ref, f_ref = rest[n_prev:]
        hv = h_ref[...]
        gt = jnp.dot(hv, wg_ref[...], preferred_element_type=F32)
        up = jnp.dot(hv, wu_ref[...], preferred_element_type=F32)
        gt_ref[...] = gt.astype(BF16)
        up_ref[...] = up.astype(BF16)
        f_ref[...] = ((gt * _sigmoid(gt)) * up).astype(BF16)

    w_spec = pl.BlockSpec((d, wq), lambda jq, i, q: (0, q[jq]))
    o_spec = pl.BlockSpec((tm, wq), lambda jq, i, q: (i, q[jq]))
    out = jax.ShapeDtypeStruct((s, d_ff), BF16)
    return _pallas(
        body,
        name=name,
        grid_spec=pltpu.PrefetchScalarGridSpec(
            num_scalar_prefetch=1,
            grid=(nq, s // tm),
            in_specs=[pl.BlockSpec((tm, d), lambda jq, i, q: (i, 0)), w_spec, w_spec] + [ANY] * n_prev,
            out_specs=[o_spec] * 3,
        ),
        out_shape=[out, out, out],
        input_output_aliases={4 + k: k for k in range(n_prev)},
        compiler_params=_params(_mm_vmem([((tm, d), BF16, 2), ((d, wq), BF16, 4), ((tm, wq), BF16, 6), ((tm, wq), F32, 3)])),
    )(quarters, h2, g_gate, g_up, *(prev or ()))


def _down_quarters(name, quarters, nq, f_act, g_down, x_acc, in_place):
    s, d_ff = f_act.shape
    d = g_down.shape[1]
    kq = d_ff // N_CHIPS
    tm, tn = _tile(s, (1024, 512, 256, 128)), _tile(d, (1024, 512))

    def body(q_ref, f_ref, w_ref, x_ref, o_ref):
        del q_ref
        part = jnp.dot(f_ref[...], w_ref[...], preferred_element_type=F32)
        kk = pl.program_id(2)

        @pl.when(kk == 0)
        def _():
            o_ref[...] = x_ref[...] + part

        @pl.when(kk > 0)
        def _():
            o_ref[...] += part

    tile = pl.BlockSpec((tm, tn), lambda i, j, kk, q: (i, j))
    return _pallas(
        body,
        name=name,
        grid_spec=pltpu.PrefetchScalarGridSpec(
            num_scalar_prefetch=1,
            grid=(s // tm, d // tn, nq),
            in_specs=[pl.BlockSpec((tm, kq), lambda i, j, kk, q: (i, q[kk])), pl.BlockSpec((kq, tn), lambda i, j, kk, q: (q[kk], j)), tile],
            out_specs=tile,
        ),
        out_shape=jax.ShapeDtypeStruct(x_acc.shape, F32),
        input_output_aliases={3: 0} if in_place else {},
        compiler_params=_params(_mm_vmem([((tm, kq), BF16, 2), ((kq, tn), BF16, 2), ((tm, tn), F32, 6)])),
    )(quarters, f_act, g_down, x_acc)


def _mm_vmem(tiles):
    return sum(_nbytes(s, d) * c for s, d, c in tiles) + 4 * 1024 * 1024


def _rows8(v):
    r, d = v.shape
    return v.reshape(r // 8, 8, d).sum(axis=0)


def _rms_fwd(name, x, g, after=()):
    s, d = x.shape
    tm = _tile(s, (256, 128))

    def body(x_ref, g_ref, *rest):
        h_ref = rest[-1]
        xv = x_ref[...]
        r = lax.rsqrt(jnp.mean(xv * xv, axis=-1, keepdims=True) + EPS)
        h_ref[...] = ((xv * r) * g_ref[...]).astype(BF16)

    return _pallas(
        body,
        name=name,
        grid=(s // tm,),
        in_specs=[pl.BlockSpec((tm, d), lambda i: (i, 0)), pl.BlockSpec((1, d), lambda i: (0, 0))] + [ANY] * len(after),
        out_specs=pl.BlockSpec((tm, d), lambda i: (i, 0)),
        out_shape=jax.ShapeDtypeStruct((s, d), BF16),
    )(x, g, *after)


def _rms_bwd(name, x, g, dh, dres, after=()):
    s, d = x.shape
    tm = _tile(s, (256, 128))
    n = s // tm
    n_after = len(after)

    def body(x_ref, g_ref, dh_ref, dres_ref, *rest):
        dx_ref, dxb_ref, dg_ref, acc_ref = rest[n_after:]
        i = pl.program_id(0)
        xv = x_ref[...]
        r = lax.rsqrt(jnp.mean(xv * xv, axis=-1, keepdims=True) + EPS)
        xh = xv * r
        dhv = dh_ref[...]
        dxh = dhv * g_ref[...]
        dx = r * (dxh - xh * jnp.mean(dxh * xh, axis=-1, keepdims=True)) + dres_ref[...]
        dx_ref[...] = dx
        dxb_ref[...] = dx.astype(BF16)
        part = _rows8(dhv * xh)

        @pl.when(i == 0)
        def _():
            acc_ref[...] = part

        @pl.when(i > 0)
        def _():
            acc_ref[...] += part

        @pl.when(i == n - 1)
        def _():
            dg_ref[...] = jnp.sum(acc_ref[...], axis=0, keepdims=True)

    row = pl.BlockSpec((tm, d), lambda i: (i, 0))
    vec = pl.BlockSpec((1, d), lambda i: (0, 0))
    return _pallas(
        body,
        name=name,
        grid=(n,),
        in_specs=[row, vec, row, row] + [ANY] * n_after,
        out_specs=[row, row, vec],
        out_shape=[jax.ShapeDtypeStruct((s, d), F32), jax.ShapeDtypeStruct((s, d), BF16), jax.ShapeDtypeStruct((1, d), F32)],
        scratch_shapes=[pltpu.VMEM((8, d), F32)],
    )(x, g, dh, dres, *after)


def _head(x3, g, target):
    s, d = x3.shape
    tm = _tile(s, (256, 128))
    n = s // tm

    def body(x_ref, g_ref, t_ref, dx_ref, dxb_ref, dg_ref, loss_ref, acc_g, acc_l):
        i = pl.program_id(0)
        xv = x_ref[...]
        gv = g_ref[...]
        r = lax.rsqrt(jnp.mean(xv * xv, axis=-1, keepdims=True) + EPS)
        xh = xv * r
        e = xh * gv - t_ref[...]
        dy = e * (1.0 / d)
        dxh = dy * gv
        dx = r * (dxh - xh * jnp.mean(dxh * xh, axis=-1, keepdims=True))
        dx_ref[...] = dx
        dxb_ref[...] = dx.astype(BF16)
        pg = _rows8(dy * xh)
        plo = _rows8(e * e)

        @pl.when(i == 0)
        def _():
            acc_g[...] = pg
            acc_l[...] = plo

        @pl.when(i > 0)
        def _():
            acc_g[...] += pg
            acc_l[...] += plo

        @pl.when(i == n - 1)
        def _():
            dg_ref[...] = jnp.sum(acc_g[...], axis=0, keepdims=True)
            loss_ref[...] = jnp.full((1, LANES), (0.5 / d) * jnp.sum(acc_l[...]), F32)

    row = pl.BlockSpec((tm, d), lambda i: (i, 0))
    vec = pl.BlockSpec((1, d), lambda i: (0, 0))
    return _pallas(
        body,
        name="head",
        grid=(n,),
        in_specs=[row, vec, row],
        out_specs=[row, row, vec, pl.BlockSpec((1, LANES), lambda i: (0, 0))],
        out_shape=[
            jax.ShapeDtypeStruct((s, d), F32),
            jax.ShapeDtypeStruct((s, d), BF16),
            jax.ShapeDtypeStruct((1, d), F32),
            jax.ShapeDtypeStruct((1, LANES), F32),
        ],
        scratch_shapes=[pltpu.VMEM((8, d), F32), pltpu.VMEM((8, d), F32)],
    )(x3, g, target)


def _sgu_fwd(z, gain, ws_b, b_col, w_sgu):
    s = z.shape[0]
    groups = ws_b.shape[0]

    def body(zu_ref, zv_ref, gain_ref, ws_ref, b_ref, a_ref):
        vv = _gelu(zv_ref[...].astype(F32))
        r = lax.rsqrt(jnp.mean(vv * vv, axis=-1, keepdims=True) + EPS)
        vn = ((vv * r) * gain_ref[...]).astype(BF16)
        u = _gelu(zu_ref[...].astype(F32))
        for g in range(groups):
            sl = slice(g * BLK, (g + 1) * BLK)
            mixed = jnp.dot(ws_ref[g], vn[:, sl], preferred_element_type=F32) + b_ref[g]
            a_ref[:, sl] = (u[:, sl] * mixed).astype(BF16)

    return _pallas(
        body,
        name="sgu_fwd",
        grid=(s // BLK,),
        in_specs=[
            pl.BlockSpec((BLK, w_sgu), lambda c: (c, 0)),
            pl.BlockSpec((BLK, w_sgu), lambda c: (c, 1)),
            pl.BlockSpec((1, w_sgu), lambda c: (0, 0)),
            pl.BlockSpec((groups, BLK, BLK), lambda c: (0, 0, 0)),
            pl.BlockSpec((groups, BLK, 1), lambda c: (0, 0, 0)),
        ],
        out_specs=pl.BlockSpec((BLK, w_sgu), lambda c: (c, 0)),
        out_shape=jax.ShapeDtypeStruct((s, w_sgu), BF16),
    )(z, z, gain, ws_b, b_col)


def _sgu_bwd(z, da, gain, ws_b, wst_b, b_col, w_sgu, dz, after=()):
    s = z.shape[0]
    groups = ws_b.shape[0]
    n = s // BLK
    n_skip = 1 + len(after)

    def body(zu_ref, zv_ref, da_ref, gain_ref, ws_ref, wst_ref, b_ref, *rest):
        dz_ref, dws_ref, dbs_ref, dgain_ref, acc_gain = rest[n_skip:]
        c = pl.program_id(0)
        zu = zu_ref[...].astype(F32)
        zv = zv_ref[...].astype(F32)
        gain_v = gain_ref[...]
        vv = _gelu(zv)
        r = lax.rsqrt(jnp.mean(vv * vv, axis=-1, keepdims=True) + EPS)
        xh = vv * r
        vn = (xh * gain_v).astype(BF16)
        u = _gelu(zu)
        dav = da_ref[...].astype(F32)
        dmix = dav * u
        dmix_b = dmix.astype(BF16)
        dvn_parts = []
        for g in range(groups):
            sl = slice(g * BLK, (g + 1) * BLK)
            mixed = jnp.dot(ws_ref[g], vn[:, sl], preferred_element_type=F32) + b_ref[g]
            dz_ref[:, sl] = (dav[:, sl] * mixed * _gelu_grad(zu[:, sl])).astype(BF16)
            dvn_parts.append(jnp.dot(wst_ref[g], dmix_b[:, sl], preferred_element_type=F32))
            dws_g = lax.dot_general(dmix_b[:, sl], vn[:, sl], NT, preferred_element_type=F32)
            dbs_g = jnp.sum(dmix[:, sl], axis=1, keepdims=True)

            @pl.when(c == 0)
            def _():
                dws_ref[g] = dws_g
                dbs_ref[g] = dbs_g

            @pl.when(c > 0)
            def _():
                dws_ref[g] += dws_g
                dbs_ref[g] += dbs_g

        dvn = jnp.concatenate(dvn_parts, axis=1)
        dxh = dvn * gain_v
        dvv = r * (dxh - xh * jnp.mean(dxh * xh, axis=-1, keepdims=True))
        dz_ref[:, w_sgu:] = (dvv * _gelu_grad(zv)).astype(BF16)
        pg = _rows8(dvn * xh)

        @pl.when(c == 0)
        def _():
            acc_gain[...] = pg

        @pl.when(c > 0)
        def _():
            acc_gain[...] += pg

        @pl.when(c == n - 1)
        def _():
            dgain_ref[...] = jnp.sum(acc_gain[...], axis=0, keepdims=True)

    full3 = pl.BlockSpec((groups, BLK, BLK), lambda c: (0, 0, 0))
    col3 = pl.BlockSpec((groups, BLK, 1), lambda c: (0, 0, 0))
    vec = pl.BlockSpec((1, w_sgu), lambda c: (0, 0))
    return _pallas(
        body,
        name="sgu_bwd",
        grid=(n,),
        in_specs=[
            pl.BlockSpec((BLK, w_sgu), lambda c: (c, 0)),
            pl.BlockSpec((BLK, w_sgu), lambda c: (c, 1)),
            pl.BlockSpec((BLK, w_sgu), lambda c: (c, 0)),
            vec,
            full3,
            full3,
            col3,
            ANY,
        ]
        + [ANY] * len(after),
        out_specs=[pl.BlockSpec((BLK, 2 * w_sgu), lambda c: (c, 0)), full3, col3, vec],
        out_shape=[
            jax.ShapeDtypeStruct(dz.shape, BF16),
            jax.ShapeDtypeStruct((groups, BLK, BLK), F32),
            jax.ShapeDtypeStruct((groups, BLK, 1), F32),
            jax.ShapeDtypeStruct((1, w_sgu), F32),
        ],
        scratch_shapes=[pltpu.VMEM((8, w_sgu), F32)],
        input_output_aliases={7: 0},
    )(z, z, da, gain, ws_b, wst_b, b_col, dz, *after)


def _attn_softmax(sink_ref, q_ref, k_ref, v_ref, bias_ref, s_len, grp):
    kv = pl.program_id(0)
    n = pl.program_id(1)
    start = pl.multiple_of(n * BLK, BLK)
    kb = k_ref[pl.ds(start, 3 * BLK), :]
    vb = v_ref[pl.ds(start, 3 * BLK), :]
    qv = q_ref[...]
    qs = jnp.concatenate([qv[:, g * HEAD_DIM : (g + 1) * HEAD_DIM] for g in range(grp)], axis=0).astype(BF16)
    sc = lax.dot_general(qs, kb, NT, preferred_element_type=F32) * (HEAD_DIM**-0.5)
    sc = sc + bias_ref[...].reshape(grp * BLK, 3 * BLK)
    kpos = start + lax.broadcasted_iota(I32, (1, 3 * BLK), 1) - BLK
    sc = jnp.where((kpos >= 0) & (kpos < s_len), sc, NEG)
    sink = jnp.concatenate([jnp.full((BLK, 1), sink_ref[kv * grp + g], F32) for g in range(grp)], axis=0)
    m = jnp.maximum(jnp.max(sc, axis=-1, keepdims=True), sink)
    p = jnp.exp(sc - m)
    esink = jnp.exp(sink - m)
    den = jnp.sum(p, axis=-1, keepdims=True) + esink
    return start, qs, kb, vb, p / den, esink / den


def _attn_specs(s, grp, q_blk0):
    qw = grp * HEAD_DIM
    return [
        pl.BlockSpec(memory_space=pltpu.SMEM),
        pl.BlockSpec((BLK, qw), lambda kv, n: (n, q_blk0 + kv)),
        pl.BlockSpec((s + 2 * BLK, HEAD_DIM), lambda kv, n: (0, kv)),
        pl.BlockSpec((s + 2 * BLK, HEAD_DIM), lambda kv, n: (0, kv)),
        pl.BlockSpec((grp, BLK, 3 * BLK), lambda kv, n: (kv, 0, 0)),
    ]


def _attn_fwd(sink, z, k_pad, v_pad, bias_tab, grp, q_blk0):
    s = z.shape[0]
    qw = grp * HEAD_DIM

    def body(sink_ref, q_ref, k_ref, v_ref, bias_ref, o_ref):
        _, _, _, vb, pn, _ = _attn_softmax(sink_ref, q_ref, k_ref, v_ref, bias_ref, s, grp)
        o = jnp.dot(pn.astype(BF16), vb, preferred_element_type=F32)
        for g in range(grp):
            o_ref[:, g * HEAD_DIM : (g + 1) * HEAD_DIM] = o[g * BLK : (g + 1) * BLK].astype(BF16)

    return _pallas(
        body,
        name="attn_fwd",
        grid=(N_KV_HEADS, s // BLK),
        in_specs=_attn_specs(s, grp, q_blk0),
        out_specs=pl.BlockSpec((BLK, qw), lambda kv, n: (n, kv)),
        out_shape=jax.ShapeDtypeStruct((s, N_KV_HEADS * qw), BF16),
    )(sink, z, k_pad, v_pad, bias_tab)


def _attn_bwd(sink, z, k_pad, v_pad, bias_tab, dout, dz, grp, q_blk0):
    s = z.shape[0]
    qw = grp * HEAD_DIM
    nb = s // BLK
    heads = N_KV_HEADS * grp

    def body(sink_ref, q_ref, k_ref, v_ref, bias_ref, do_ref, dz_in, dq_ref, dk_ref, dv_ref, dbias_ref, dsink_ref, dk_acc, dv_acc):
        del dz_in
        kv = pl.program_id(0)
        n = pl.program_id(1)
        start, qs, kb, vb, pn, psink = _attn_softmax(sink_ref, q_ref, k_ref, v_ref, bias_ref, s, grp)
        dov = do_ref[...]
        dos = jnp.concatenate([dov[:, g * HEAD_DIM : (g + 1) * HEAD_DIM] for g in range(grp)], axis=0)
        dp = lax.dot_general(dos, vb, NT, preferred_element_type=F32)
        dvb = lax.dot_general(pn.astype(BF16), dos, TN, preferred_element_type=F32)
        delta = jnp.sum(pn * dp, axis=-1, keepdims=True)
        ds = pn * (dp - delta)
        dsb = (ds * (HEAD_DIM**-0.5)).astype(BF16)
        dq = jnp.dot(dsb, kb, preferred_element_type=F32)
        dkb = lax.dot_general(dsb, qs, TN, preferred_element_type=F32)
        for g in range(grp):
            dq_ref[:, g * HEAD_DIM : (g + 1) * HEAD_DIM] = dq[g * BLK : (g + 1) * BLK].astype(BF16)

        @pl.when(n == 0)
        def _():
            dk_acc[...] = jnp.zeros_like(dk_acc)
            dv_acc[...] = jnp.zeros_like(dv_acc)
            dbias_ref[...] = jnp.zeros_like(dbias_ref)

        @pl.when((n == 0) & (kv == 0))
        def _():
            dsink_ref[...] = jnp.zeros_like(dsink_ref)

        dk_acc[pl.ds(start, 3 * BLK), :] += dkb
        dv_acc[pl.ds(start, 3 * BLK), :] += dvb
        dbias_ref[...] += ds.reshape(grp, BLK, 3 * BLK)
        row = lax.broadcasted_iota(I32, (heads, LANES), 0)
        sd = psink * delta
        upd = jnp.zeros((heads, LANES), F32)
        for g in range(grp):
            upd = jnp.where(row == kv * grp + g, -jnp.sum(sd[g * BLK : (g + 1) * BLK]), upd)
        dsink_ref[...] += upd

        @pl.when(n == nb - 1)
        def _():
            dk_ref[...] = dk_acc[...]
            dv_ref[...] = dv_acc[...]

    pad_spec = pl.BlockSpec((s + 2 * BLK, HEAD_DIM), lambda kv, n: (0, kv))
    kvw = N_KV_HEADS * HEAD_DIM
    return _pallas(
        body,
        name="attn_bwd",
        grid=(N_KV_HEADS, nb),
        in_specs=_attn_specs(s, grp, q_blk0) + [pl.BlockSpec((BLK, qw), lambda kv, n: (n, kv)), ANY],
        out_specs=[
            pl.BlockSpec((BLK, qw), lambda kv, n: (n, q_blk0 + kv)),
            pad_spec,
            pad_spec,
            pl.BlockSpec((grp, BLK, 3 * BLK), lambda kv, n: (kv, 0, 0)),
            pl.BlockSpec((heads, LANES), lambda kv, n: (0, 0)),
        ],
        out_shape=[
            jax.ShapeDtypeStruct(dz.shape, BF16),
            jax.ShapeDtypeStruct((s + 2 * BLK, kvw), F32),
            jax.ShapeDtypeStruct((s + 2 * BLK, kvw), F32),
            jax.ShapeDtypeStruct((heads, BLK, 3 * BLK), F32),
            jax.ShapeDtypeStruct((heads, LANES), F32),
        ],
        scratch_shapes=[pltpu.VMEM((s + 2 * BLK, HEAD_DIM), F32), pltpu.VMEM((s + 2 * BLK, HEAD_DIM), F32)],
        input_output_aliases={6: 0},
    )(sink, z, k_pad, v_pad, bias_tab, dout, dz)


def _dkv_to_dz(dk_pad, dv_pad, dz, blk_idx):
    s = dz.shape[0]
    kvw = dk_pad.shape[1]

    def body(dk_ref, dv_ref, dz_in, out_ref):
        del dz_in
        out_ref[:, :kvw] = dk_ref[...].astype(BF16)
        out_ref[:, kvw:] = dv_ref[...].astype(BF16)

    src = pl.BlockSpec((BLK, kvw), lambda i: (i + 1, 0))
    return _pallas(
        body,
        name="dkv_to_dz",
        grid=(s // BLK,),
        in_specs=[src, src, ANY],
        out_specs=pl.BlockSpec((BLK, 2 * kvw), lambda i: (i, blk_idx)),
        out_shape=jax.ShapeDtypeStruct(dz.shape, BF16),
        input_output_aliases={2: 0},
    )(dk_pad, dv_pad, dz)


def _relbias_bwd(dbias_tab, bucket):
    heads = dbias_tab.shape[0]

    def body(dt_ref, bk_ref, out_ref):
        lane = lax.broadcasted_iota(I32, (1, LANES), 1)
        bk = bk_ref[...]
        rows = []
        for h in range(heads):
            dt = dt_ref[h]
            acc = jnp.zeros((1, LANES), F32)
            for b in range(REL_BUCKETS):
                acc = jnp.where(lane == b, jnp.sum(jnp.where(bk == b, dt, 0.0)), acc)
            rows.append(acc)
        out_ref[...] = jnp.concatenate(rows, axis=0)

    return _pallas(body, name="relbias_bwd", out_shape=jax.ShapeDtypeStruct((heads, LANES), F32))(dbias_tab, bucket)


def _t5_bucket(rel):
    nb = REL_BUCKETS // 2
    ret = jnp.where(rel > 0, nb, 0)
    n = jnp.abs(rel)
    max_exact = nb // 2
    nf = jnp.maximum(n, 1).astype(F32)
    large = max_exact + (jnp.log(nf / max_exact) / math.log(REL_MAX_DIST / max_exact) * (nb - max_exact)).astype(I32)
    large = jnp.minimum(large, nb - 1)
    return ret + jnp.where(n < max_exact, n, large)


def _band_tables(rel_bias):
    qi = jnp.arange(BLK)[:, None]
    kj = jnp.arange(3 * BLK)[None, :]
    rel = kj - BLK - qi
    bucket = _t5_bucket(rel).astype(I32)
    heads = rel_bias.shape[1]
    masked = jnp.where(jnp.abs(rel) <= BLK, bucket, -1)

    def body(rb_ref, bk_ref, out_ref):
        bk = bk_ref[...]
        for h in range(heads):
            tab = jnp.full(bk.shape, NEG, F32)
            for b in range(REL_BUCKETS):
                tab = jnp.where(bk == b, rb_ref[b, h], tab)
            out_ref[h] = tab

    bias_tab = _pallas(
        body,
        name="bias_table",
        in_specs=[pl.BlockSpec(memory_space=pltpu.SMEM), pl.BlockSpec(memory_space=pltpu.VMEM)],
        out_specs=pl.BlockSpec(memory_space=pltpu.VMEM),
        out_shape=jax.ShapeDtypeStruct((heads, BLK, 3 * BLK), F32),
    )(rel_bias.astype(F32), masked)
    return bias_tab, bucket


EW_BLOCK_ELEMS = 512 * 1024


def _ew_tiles(shape, elems=EW_BLOCK_ELEMS // 2):
    r, c = shape
    tn = c if c <= 2048 else _tile(c, (2048, 1920, 1536, 1408, 1024, 512))
    tm = _tile(r, [t for t in (1024, 512, 256, 128, 64, 32, 16, 8) if t * tn <= elems] or [8])
    return tm, tn


def _cast_into_full(name, qidx, w, kind, after=()):
    r, c = w.shape
    tm, tn = _ew_tiles(w.shape, EW_BLOCK_ELEMS)
    nbi, nbj = r // tm, c // tn
    if kind == "col":
        full, out_spec = (r, c * N_CHIPS), pl.BlockSpec((tm, tn), lambda i, j, q: (i, q[0] * nbj + j))
    else:
        full, out_spec = (r * N_CHIPS, c), pl.BlockSpec((tm, tn), lambda i, j, q: (q[0] * nbi + i, j))

    def body(q_ref, w_ref, *rest):
        del q_ref
        rest[-1][...] = w_ref[...].astype(BF16)

    return _pallas(
        body,
        name=name,
        grid_spec=pltpu.PrefetchScalarGridSpec(
            num_scalar_prefetch=1,
            grid=(nbi, nbj),
            in_specs=[pl.BlockSpec((tm, tn), lambda i, j, q: (i, j))] + [ANY] * len(after),
            out_specs=out_spec,
        ),
        out_shape=jax.ShapeDtypeStruct(full, BF16),
    )(qidx, w, *after)


def _adamw(name, w, g, m, v, after=()):
    tm, tn = _ew_tiles(w.shape, EW_BLOCK_ELEMS)
    if _nbytes(w.shape, F32) <= 1024 * 1024:
        tm, tn = w.shape
    spec = pl.BlockSpec((tm, tn), lambda i, j: (i, j))
    n_after = len(after)

    def body(w_ref, g_ref, m_ref, v_ref, *rest):
        d_ref, nm_ref, nv_ref, g_out_ref = rest[n_after:]
        gv = g_ref[...]
        g_out_ref[...] = gv
        nm = ADAM_B1 * m_ref[...] + (1.0 - ADAM_B1) * gv
        nv = ADAM_B2 * v_ref[...] + (1.0 - ADAM_B2) * (gv * gv)
        m_hat = nm / (1.0 - ADAM_B1**ADAM_STEP)
        v_hat = nv / (1.0 - ADAM_B2**ADAM_STEP)
        d_ref[...] = -ADAM_LR * (m_hat / (jnp.sqrt(v_hat) + ADAM_EPS) + ADAM_WD * w_ref[...])
        nm_ref[...] = nm
        nv_ref[...] = nv

    out = jax.ShapeDtypeStruct(w.shape, F32)
    return _pallas(
        body, name=name, grid=(w.shape[0] // tm, w.shape[1] // tn), in_specs=[spec] * 4 + [ANY] * n_after,
        out_specs=[spec] * 4, out_shape=[out, out, out, out],
        compiler_params=_params(_mm_vmem([((tm, tn), F32, 24)])),
    )(w, g, m, v, *after)


def _pair_add(name, cidx, g_full, r_sib, kind):
    hr, hc = r_sib.shape
    tm, tn = _ew_tiles((hr, hc), 2 * EW_BLOCK_ELEMS)
    nbi, nbj = hr // tm, hc // tn
    if kind == "col":
        g_spec = pl.BlockSpec((tm, tn), lambda i, j, c: (c[0] * nbi + i, j))
    else:
        g_spec = pl.BlockSpec((tm, tn), lambda i, j, c: (i, c[0] * nbj + j))
    spec = pl.BlockSpec((tm, tn), lambda i, j, c: (i, j))

    def body(c_ref, g_ref, r_ref, o_ref):
        del c_ref
        o_ref[...] = (g_ref[...].astype(F32) + r_ref[...].astype(F32)).astype(BF16)

    return _pallas(
        body,
        name=name,
        grid_spec=pltpu.PrefetchScalarGridSpec(num_scalar_prefetch=1, grid=(nbi, nbj), in_specs=[g_spec, spec], out_specs=spec),
        out_shape=jax.ShapeDtypeStruct((hr, hc), BF16),
        compiler_params=_params(_mm_vmem([((tm, tn), BF16, 6), ((tm, tn), F32, 3)])),
    )(cidx, g_full, r_sib)


def _chip_sum(name, qidx, c_half, r_ici, kind):
    _, pr, pc = r_ici.shape
    tm, tn = _ew_tiles((pr, pc), 2 * EW_BLOCK_ELEMS)
    nbi, nbj = pr // tm, pc // tn
    if kind == "col":
        own_spec = pl.BlockSpec((tm, tn), lambda i, j, q: (i, q[0] * nbj + j))
        full, out_spec = (2 * pr, pc), pl.BlockSpec((tm, tn), lambda i, j, q: (q[1] * nbi + i, j))
    else:
        own_spec = pl.BlockSpec((tm, tn), lambda i, j, q: (q[0] * nbi + i, j))
        full, out_spec = (pr, 2 * pc), pl.BlockSpec((tm, tn), lambda i, j, q: (i, q[1] * nbj + j))

    def body(q_ref, own_ref, r_ref, o_ref):
        q = q_ref[0]
        own = own_ref[...].astype(F32)
        recv = [r_ref[r].astype(F32) for r in range(3)]
        total = None
        for chip in range(N_CHIPS):
            d = chip ^ q
            term = jnp.where(d == 0, own, jnp.where(d == 2, recv[0], jnp.where(d == 1, recv[1], recv[2])))
            total = term if total is None else total + term
        o_ref[...] = total

    return _pallas(
        body,
        name=name,
        grid_spec=pltpu.PrefetchScalarGridSpec(
            num_scalar_prefetch=1,
            grid=(nbi, nbj),
            in_specs=[own_spec, pl.BlockSpec((3, tm, tn), lambda i, j, q: (0, i, j))],
            out_specs=out_spec,
        ),
        out_shape=jax.ShapeDtypeStruct(full, F32),
        compiler_params=_params(_mm_vmem([((tm, tn), BF16, 8), ((tm, tn), F32, 6)])),
    )(qidx, c_half, r_ici)


_REL_MASK = (2, 1, 3)


def _place():
    x, y, c = lax.axis_index("x"), lax.axis_index("y"), lax.axis_index("c")
    chips = [(1 - x, y), (x, 1 - y), (1 - x, 1 - y)]
    return x, y, c, 2 * x + y, chips


def _shard_view(ref, kind, chip):
    if kind == "col":
        w = ref.shape[1] // N_CHIPS
        return ref.at[:, pl.ds(pl.multiple_of(chip * w, LANES), w)]
    h = ref.shape[0] // N_CHIPS
    return ref.at[pl.ds(pl.multiple_of(chip * h, 16), h), :]


def _row_half(ref, half):
    h = ref.shape[0] // 2
    return ref.at[pl.ds(pl.multiple_of(half * h, 16), h), :]


def _pair_half(ref, kind, half):
    if kind == "col":
        return _row_half(ref, half)
    w = ref.shape[1] // 2
    return ref.at[:, pl.ds(pl.multiple_of(half * w, LANES), w)]


def _remote(src, dst, send_sem, recv_sem, dev):
    return pltpu.make_async_remote_copy(src_ref=src, dst_ref=dst, send_sem=send_sem, recv_sem=recv_sem, device_id=dev, device_id_type=MESH)


def _hbm(a):
    return pltpu.with_memory_space_constraint(a, pltpu.HBM)


def _gather_start(name, fulls, kinds):
    n_w = len(fulls)

    def body(*refs):
        g = refs[:n_w]
        send_sem, recv_sem = refs[n_w], refs[n_w + 1]
        token = refs[-1]
        _, _, c, q, chips = _place()
        for w in range(n_w):
            mine = _row_half(_shard_view(g[w], kinds[w], q), c)
            for r, chip in enumerate(chips):
                _remote(mine, mine, send_sem.at[3 * w + r], recv_sem.at[3 * w + r], (*chip, c)).start()
        token[...] = jnp.zeros_like(token)

    res = _pallas(
        body,
        name=name,
        out_shape=(
            pltpu.SemaphoreType.DMA((3 * n_w,)),
            pltpu.SemaphoreType.DMA((3 * n_w,)),
            *[pltpu.HBM(f.shape, f.dtype) for f in fulls],
            jax.ShapeDtypeStruct((8, LANES), F32),
        ),
        in_specs=[HBM_SPEC] * n_w,
        out_specs=(SEM_SPEC, SEM_SPEC, *[HBM_SPEC] * n_w, pl.BlockSpec(memory_space=pltpu.VMEM)),
        input_output_aliases={w: w + 2 for w in range(n_w)},
        compiler_params=pltpu.CompilerParams(has_side_effects=EFFECT),
    )(*[_hbm(f) for f in fulls])
    return res[0], res[1], list(res[2 : 2 + n_w]), res[-1]


def _gather_wait(name, fulls, kinds, w_ids, send_sem, recv_sem, after, rels=(0, 1, 2)):
    n = len(fulls)

    def body(*refs):
        g = refs[:n]
        s_sem, r_sem = refs[n], refs[n + 1]
        x, y, c, q, _ = _place()
        for i, w in enumerate(w_ids):
            mine = _row_half(_shard_view(g[i], kinds[i], q), c)
            for r in rels:
                landed = _row_half(_shard_view(g[i], kinds[i], q ^ _REL_MASK[r]), c)
                cp = _remote(mine, landed, s_sem.at[3 * w + r], r_sem.at[3 * w + r], (x, y, 1 - c))
                cp.wait_send()
                cp.wait_recv()

    res = _pallas(
        body,
        name=name,
        out_shape=[pltpu.HBM(f.shape, f.dtype) for f in fulls],
        in_specs=[HBM_SPEC] * n + [SEM_SPEC, SEM_SPEC, ANY],
        out_specs=[HBM_SPEC] * n,
        input_output_aliases={i: i for i in range(n)},
        compiler_params=pltpu.CompilerParams(has_side_effects=EFFECT),
    )(*fulls, send_sem, recv_sem, after)
    return list(res)


def _gather_forward(name, fulls, kinds, rels=(0, 1, 2)):
    n = len(fulls)

    def body(*refs):
        g = refs[n : 2 * n]
        send, recv = refs[2 * n :]
        x, y, c, q, _ = _place()
        sib = (x, y, 1 - c)
        cps = []
        for i in range(n):
            for r in rels:
                landed = _row_half(_shard_view(g[i], kinds[i], q ^ _REL_MASK[r]), c)
                cps.append(_remote(landed, landed, send.at[i, r], recv.at[i, r], sib))
        for cp in cps:
            cp.start()
        for i in range(n):
            for r in rels:
                other = _row_half(_shard_view(g[i], kinds[i], q ^ _REL_MASK[r]), 1 - c)
                _remote(other, other, send.at[i, r], recv.at[i, r], sib).wait_recv()
        for cp in cps:
            cp.wait_send()

    res = _pallas(
        body,
        name=name,
        in_specs=[ANY] * n,
        out_specs=[ANY] * n,
        out_shape=[jax.ShapeDtypeStruct(f.shape, f.dtype) for f in fulls],
        scratch_shapes=[pltpu.SemaphoreType.DMA((n, 3)), pltpu.SemaphoreType.DMA((n, 3))],
        input_output_aliases={i: i for i in range(n)},
    )(*fulls)
    return list(res)


def _split_start(name, bufs, n_sems, copies):
    n = len(bufs)

    def body(*refs):
        for cp in copies(refs[:n], refs[n], refs[n + 1]):
            cp.start()

    res = _pallas(
        body,
        name=name,
        out_shape=(
            pltpu.SemaphoreType.DMA((n_sems,)),
            pltpu.SemaphoreType.DMA((n_sems,)),
            *[pltpu.HBM(b.shape, b.dtype) for b in bufs],
        ),
        in_specs=[HBM_SPEC] * n,
        out_specs=(SEM_SPEC, SEM_SPEC, *[HBM_SPEC] * n),
        input_output_aliases={i: i + 2 for i in range(n)},
        compiler_params=pltpu.CompilerParams(has_side_effects=EFFECT),
    )(*[_hbm(b) for b in bufs])
    return res[0], res[1], list(res[2:])


def _split_wait(name, bufs, send_sem, recv_sem, copies, after):
    n = len(bufs)

    def body(*refs):
        for cp in copies(refs[:n], refs[n], refs[n + 1]):
            cp.wait_send()
            cp.wait_recv()

    res = _pallas(
        body,
        name=name,
        out_shape=[pltpu.HBM(b.shape, b.dtype) for b in bufs],
        in_specs=[HBM_SPEC] * n + [SEM_SPEC, SEM_SPEC, ANY],
        out_specs=[HBM_SPEC] * n,
        input_output_aliases={i: i for i in range(n)},
        compiler_params=pltpu.CompilerParams(has_side_effects=EFFECT),
    )(*bufs, send_sem, recv_sem, after)
    return list(res)


def _pair_exchange_copies(kinds):
    n = len(kinds)

    def copies(refs, send_sem, recv_sem):
        x, y, c, _, _ = _place()
        return [
            _remote(_pair_half(refs[w], kinds[w], 1 - c), refs[n + w], send_sem.at[w], recv_sem.at[w], (x, y, 1 - c))
            for w in range(n)
        ]

    return copies


def _pair_share_copies(kinds, waiting):
    def copies(refs, send_sem, recv_sem):
        x, y, c, _, _ = _place()
        out = []
        for w, kind in enumerate(kinds):
            mine = _pair_half(refs[w], kind, c)
            dst = _pair_half(refs[w], kind, 1 - c) if waiting else mine
            out.append(_remote(mine, dst, send_sem.at[w], recv_sem.at[w], (x, y, 1 - c)))
        return out

    return copies


def _piece_shape(half_shape, kind):
    r, c = half_shape
    return (3, r, c // N_CHIPS) if kind == "col" else (3, r // N_CHIPS, c)


def _chip_send_start(name, halves, kinds):
    n = len(halves)
    lands = [lax.empty(_piece_shape(h.shape, k), BF16) for h, k in zip(halves, kinds)]

    def body(*refs):
        h, land = refs[:n], refs[n : 2 * n]
        send_sem, recv_sem = refs[2 * n], refs[2 * n + 1]
        _, _, c, q, chips = _place()
        for i in range(n):
            for r, chip in enumerate(chips):
                piece = _shard_view(h[i], kinds[i], q ^ _REL_MASK[r])
                _remote(piece, land[i].at[r], send_sem.at[3 * i + r], recv_sem.at[3 * i + r], (*chip, c)).start()

    res = _pallas(
        body,
        name=name,
        out_shape=(
            pltpu.SemaphoreType.DMA((3 * n,)),
            pltpu.SemaphoreType.DMA((3 * n,)),
            *[pltpu.HBM(a.shape, a.dtype) for a in halves],
            *[pltpu.HBM(a.shape, a.dtype) for a in lands],
        ),
        in_specs=[HBM_SPEC] * (2 * n),
        out_specs=(SEM_SPEC, SEM_SPEC, *[HBM_SPEC] * (2 * n)),
        input_output_aliases={i: i + 2 for i in range(2 * n)},
        compiler_params=pltpu.CompilerParams(has_side_effects=EFFECT),
    )(*[_hbm(a) for a in halves], *[_hbm(a) for a in lands])
    return res[0], res[1], list(res[2 : 2 + n]), list(res[2 + n :])


def _chip_send_wait(name, halves, lands, kinds, send_sem, recv_sem, after):
    n = len(halves)

    def body(*refs):
        h, land = refs[:n], refs[n : 2 * n]
        s_sem, r_sem = refs[2 * n], refs[2 * n + 1]
        x, y, c, q, _ = _place()
        for i in range(n):
            for r in range(3):
                piece = _shard_view(h[i], kinds[i], q ^ _REL_MASK[r])
                cp = _remote(piece, land[i].at[r], s_sem.at[3 * i + r], r_sem.at[3 * i + r], (x, y, 1 - c))
                cp.wait_send()
                cp.wait_recv()

    res = _pallas(
        body,
        name=name,
        out_shape=[pltpu.HBM(a.shape, a.dtype) for a in halves] + [pltpu.HBM(a.shape, a.dtype) for a in lands],
        in_specs=[HBM_SPEC] * (2 * n) + [SEM_SPEC, SEM_SPEC, ANY],
        out_specs=[HBM_SPEC] * (2 * n),
        input_output_aliases={i: i for i in range(2 * n)},
        compiler_params=pltpu.CompilerParams(has_side_effects=EFFECT),
    )(*halves, *lands, send_sem, recv_sem, after)
    return list(res[:n]), list(res[n:])


def _small_all_reduce(p):
    rows = p.shape[0]
    n_dev = 2 * N_CHIPS

    def body(p_ref, o_ref, buf, loc_sem, send, recv):
        x, y, c, q, _ = _place()
        me = 2 * q + c
        own = pltpu.make_async_copy(p_ref, buf.at[me], loc_sem)
        own.start()
        cps = []
        for d in range(1, n_dev):
            dev = (x ^ ((d >> 2) & 1), y ^ ((d >> 1) & 1), c ^ (d & 1))
            cps.append(_remote(p_ref, buf.at[me], send.at[d - 1], recv.at[d - 1], dev))
        for cp in cps:
            cp.start()
        for d in range(1, n_dev):
            slot = buf.at[me ^ d]
            _remote(slot, slot, send.at[d - 1], recv.at[d - 1], (x, y, c)).wait_recv()
        own.wait()
        total = buf[0]
        for d in range(1, n_dev):
            total = total + buf[d]
        o_ref[...] = total
        for cp in cps:
            cp.wait_send()

    return _pallas(
        body,
        name="small_all_reduce",
        in_specs=[ANY],
        out_specs=pl.BlockSpec(memory_space=pltpu.VMEM),
        out_shape=jax.ShapeDtypeStruct(p.shape, F32),
        scratch_shapes=[
            pltpu.VMEM((n_dev, rows, LANES), F32),
            pltpu.SemaphoreType.DMA,
            pltpu.SemaphoreType.DMA((n_dev - 1,)),
            pltpu.SemaphoreType.DMA((n_dev - 1,)),
        ],
    )(p)


def _small_exchange_copies(waiting):
    def copies(refs, send_sem, recv_sem):
        p, land = refs
        x, y, c, q, _ = _place()
        me = 2 * q + c
        out = []
        for dd in range(1, 2 * N_CHIPS):
            dev = (x ^ ((dd >> 2) & 1), y ^ ((dd >> 1) & 1), c ^ (dd & 1))
            dst = land.at[me ^ dd] if waiting else land.at[me]
            out.append(_remote(p, dst, send_sem.at[dd - 1], recv_sem.at[dd - 1], dev))
        return out

    return copies


def _small_sum(name, me_idx, p, land):
    rows = p.shape[0]
    n_dev = 2 * N_CHIPS

    def body(me_ref, p_ref, land_ref, o_ref):
        me = me_ref[0]
        total = None
        for dev in range(n_dev):
            term = jnp.where(me == dev, p_ref[...], land_ref[dev])
            total = term if total is None else total + term
        o_ref[...] = total

    return _pallas(
        body,
        name=name,
        grid_spec=pltpu.PrefetchScalarGridSpec(
            num_scalar_prefetch=1,
            grid=(1,),
            in_specs=[pl.BlockSpec((rows, LANES), lambda i, m: (0, 0)), pl.BlockSpec((n_dev, rows, LANES), lambda i, m: (0, 0, 0))],
            out_specs=pl.BlockSpec((rows, LANES), lambda i, m: (0, 0)),
        ),
        out_shape=jax.ShapeDtypeStruct(p.shape, F32),
    )(me_idx, p, land)


def _pack(parts):
    rows = []
    for a in parts:
        flat = a.reshape(-1).astype(F32)
        n = flat.shape[0]
        padded = -(-n // (8 * LANES)) * (8 * LANES)
        rows.append(jnp.pad(flat, (0, padded - n)).reshape(-1, LANES))
    return jnp.concatenate(rows, axis=0)


def _unpack(packed, shapes):
    out, row = [], 0
    for shp in shapes:
        n = int(np.prod(shp))
        nrows = -(-n // (8 * LANES)) * 8
        out.append(packed[row : row + nrows].reshape(-1)[:n].reshape(shp))
        row += nrows
    return out


def kernel(x, w_in, norm_mix, sgu_v_gain, sgu_w_s, sgu_b_s, w_a_out, attn_sink, rel_bias, w_b_out, w_o, norm_ffn, w_gate, w_up, w_down, norm_final, loss_target, m_w_in, m_norm_mix, m_sgu_v_gain, m_sgu_w_s, m_sgu_b_s, m_w_a_out, m_attn_sink, m_rel_bias, m_w_b_out, m_w_o, m_norm_ffn, m_w_gate, m_w_up, m_w_down, m_norm_final, v_w_in, v_norm_mix, v_sgu_v_gain, v_sgu_w_s, v_sgu_b_s, v_w_a_out, v_attn_sink, v_rel_bias, v_w_b_out, v_w_o, v_norm_ffn, v_w_gate, v_w_up, v_w_down, v_norm_final):
    s, d = x.shape[1], x.shape[2]
    w_sgu = sgu_v_gain.shape[1]
    groups = sgu_w_s.shape[1]
    heads = attn_sink.shape[1]
    grp = heads // N_KV_HEADS
    w_att = heads * HEAD_DIM
    w_kv = N_KV_HEADS * HEAD_DIM
    d_ff = w_gate.shape[2] * N_CHIPS
    n_in = w_in.shape[2] * N_CHIPS
    off_q = 2 * w_sgu
    off_k = off_q + w_att
    off_g = off_k + 2 * w_kv
    assert n_in == off_g + 2 * d and groups * BLK == w_sgu and s % BLK == 0

    x2d = x.reshape(s, d)
    tgt = loss_target.reshape(s, d)
    c_idx = lax.axis_index("c").astype(I32).reshape(1)
    q_idx = (2 * lax.axis_index("x") + lax.axis_index("y")).astype(I32).reshape(1)
    qc_idx = jnp.concatenate([q_idx, c_idx])

    W_IN, W_A, W_B, W_O, W_GATE, W_UP, W_DOWN = range(7)
    names = ["w_in", "w_a", "w_b", "w_o", "w_gate", "w_up", "w_down"]
    kinds = ["col", "col", "col", "row", "col", "col", "row"]
    big_w = [w_in[0], w_a_out[0], w_b_out[0], w_o[0], w_gate[0], w_up[0], w_down[0]]
    big_m = [m_w_in[0], m_w_a_out[0], m_w_b_out[0], m_w_o[0], m_w_gate[0], m_w_up[0], m_w_down[0]]
    big_v = [v_w_in[0], v_w_a_out[0], v_w_b_out[0], v_w_o[0], v_w_gate[0], v_w_up[0], v_w_down[0]]
    full_in = _cast_into_full("cast_w_in", q_idx, big_w[W_IN], kinds[W_IN])
    in_send, in_recv, (full_in,), token = _gather_start("gather_start_in", [full_in], [kinds[W_IN]])
    rest = [_cast_into_full("cast_" + names[i], q_idx, big_w[i], kinds[i], after=(token,)) for i in range(1, 7)]
    ag_send, ag_recv, rest, token = _gather_start("gather_start_rest", rest, kinds[1:])
    fulls = [full_in] + rest

    def gathered(tag, ids, after):
        if ids == [W_IN]:
            sems, pos = (in_send, in_recv), [0]
        else:
            sems, pos = (ag_send, ag_recv), [i - 1 for i in ids]
        got = _gather_wait("gather_wait_" + tag, [fulls[i] for i in ids], [kinds[i] for i in ids], pos, *sems, after)
        return _gather_forward("gather_fwd_" + tag, got, [kinds[i] for i in ids])

    ws_b = sgu_w_s[0].astype(BF16)
    wst_b = jnp.swapaxes(sgu_w_s[0], 1, 2).astype(BF16)
    b_col = sgu_b_s[0].reshape(groups, BLK, 1)
    bias_tab, bucket = _band_tables(rel_bias)
    sink = attn_sink[0]

    tm = _tile(s, (1024, 512, 256, 128))

    h1 = _rms_fwd("rms_mix", x2d, norm_mix, after=(token,))
    (g_in,) = gathered("in", [W_IN], h1)

    tn = _tile(n_in, (768, 640, 512))
    z = _mm(
        "mm_z", (s // tm, n_in // tn, 1), [h1, g_in],
        [pl.BlockSpec((tm, d), lambda i, j, k: (i, 0)), pl.BlockSpec((d, tn), lambda i, j, k: (0, j))],
        [jax.ShapeDtypeStruct((s, n_in), BF16)], [pl.BlockSpec((tm, tn), lambda i, j, k: (i, j))],
        [(0, 1, NN, 0)], 1, (tm, tn), 1, lambda ins, vals, outs, cs: _put(outs[0], cs, vals[0]),
        _mm_vmem([((tm, d), BF16, 2), ((d, tn), BF16, 2), ((tm, tn), F32, 3)]),
    )[0]
    g_a, g_b, g_o = gathered("mix", [W_A, W_B, W_O], z)

    a_act = _sgu_fwd(z, sgu_v_gain, ws_b, b_col, w_sgu)

    kv_b = z[:, off_k:off_g]
    k_pad = jnp.pad(kv_b[:, :w_kv], ((BLK, BLK), (0, 0)))
    v_pad = jnp.pad(kv_b[:, w_kv:], ((BLK, BLK), (0, 0)))
    q_blk0 = off_q // (grp * HEAD_DIM)
    att = _attn_fwd(sink, z, k_pad, v_pad, bias_tab, grp, q_blk0)

    tg = _tile(d, (512,))
    ga0, gb0 = off_g // tg, (off_g + d) // tg

    def ep_gate(ins, vals, outs, cs):
        sa, sb = _sigmoid(ins[4][:, cs].astype(F32)), _sigmoid(ins[5][:, cs].astype(F32))
        _put(outs[0], cs, sa * vals[0] + sb * vals[1])
        _put(outs[1], cs, vals[0])
        _put(outs[2], cs, vals[1])

    t_out = pl.BlockSpec((tm, tg), lambda i, j, k: (i, j))
    m_act, y_a, y_b = _mm(
        "mm_branches", (s // tm, d // tg, 1), [a_act, g_a, att, g_b, z, z],
        [pl.BlockSpec((tm, w_sgu), lambda i, j, k: (i, 0)), pl.BlockSpec((w_sgu, tg), lambda i, j, k: (0, j)),
         pl.BlockSpec((tm, w_att), lambda i, j, k: (i, 0)), pl.BlockSpec((w_att, tg), lambda i, j, k: (0, j)),
         pl.BlockSpec((tm, tg), lambda i, j, k: (i, ga0 + j)), pl.BlockSpec((tm, tg), lambda i, j, k: (i, gb0 + j))],
        [jax.ShapeDtypeStruct((s, d), BF16)] * 3,
        [t_out, t_out, t_out], [(0, 1, NN, 0), (2, 3, NN, 1)], 2, (tm, tg), 1, ep_gate,
        _mm_vmem([((tm, w_sgu), BF16, 4), ((w_sgu, tg), BF16, 4), ((tm, tg), F32, 12)]),    )

    tn = _tile(d, (1024, 512))

    def ep_residual(ins, vals, outs, cs):
        _put(outs[0], cs, ins[2][:, cs] + vals[0])

    x2 = _mm(
        "mm_wo", (s // tm, d // tn, 1), [m_act, g_o, x2d],
        [pl.BlockSpec((tm, d), lambda i, j, k: (i, 0)), pl.BlockSpec((d, tn), lambda i, j, k: (0, j)),
         pl.BlockSpec((tm, tn), lambda i, j, k: (i, j))],
        [jax.ShapeDtypeStruct((s, d), F32)], [pl.BlockSpec((tm, tn), lambda i, j, k: (i, j))],
        [(0, 1, NN, 0)], 1, (tm, tn), 1, ep_residual,
        _mm_vmem([((tm, d), BF16, 2), ((d, tn), BF16, 2), ((tm, tn), F32, 5)]),    )[0]

    h2 = _rms_fwd("rms_ffn", x2, norm_ffn)

    tf = _tile(d_ff, (512,))
    f_out = pl.BlockSpec((tm, tf), lambda i, j, k: (i, j))
    tkf = _tile(d_ff, (1408, 1024, 512))
    near = jnp.concatenate([q_idx, q_idx ^ _REL_MASK[0], q_idx ^ _REL_MASK[1]])
    far = q_idx ^ _REL_MASK[2]

    def in_two_parts(tag, ids, after, work):
        bufs, ks, pos = [fulls[i] for i in ids], [kinds[i] for i in ids], [i - 1 for i in ids]
        for part, (rels, quarters) in enumerate((((0, 1), near), ((2,), far))):
            bufs = _gather_wait(f"gather_wait_{tag}{part}", bufs, ks, pos, ag_send, ag_recv, after, rels=rels)
            bufs = _gather_forward(f"gather_fwd_{tag}{part}", bufs, ks, rels=rels)
            after = work(part, quarters, bufs)
        return bufs

    acts = []

    def gate_up_part(part, quarters, bufs):
        acts[:] = _gate_up_quarters(f"mm_gate_up{part}", quarters, quarters.shape[0], h2, bufs[0], bufs[1], acts or None)
        return acts[2]

    g_gate, g_up = in_two_parts("ffn_in", [W_GATE, W_UP], h2, gate_up_part)
    gt, up, f_act = acts
    x_acc = [x2]

    def down_part(part, quarters, bufs):
        x_acc[0] = _down_quarters(f"mm_down{part}", quarters, quarters.shape[0], f_act, bufs[0], x_acc[0], in_place=part > 0)
        return x_acc[0]

    (g_down,) = in_two_parts("ffn_out", [W_DOWN], f_act, down_part)
    x3 = x_acc[0]

    dx3, dx3b, dg_final, loss_part = _head(x3, norm_final.reshape(1, d), tgt)

    def reduce_a(tag, ids, grads):
        ks = [kinds[i] for i in ids]
        lands = [lax.empty((g.shape[0] // 2, g.shape[1]) if k == "col" else (g.shape[0], g.shape[1] // 2), BF16)
                 for g, k in zip(grads, ks)]
        send, recv, bufs = _split_start("pair_send_" + tag, list(grads) + lands, len(ids), _pair_exchange_copies(ks))
        return {"tag": tag, "ids": ids, "ks": ks, "pair": (send, recv, bufs), "token": bufs[0]}

    def reduce_b(st, after):
        tag, ids, ks = st["tag"], st["ids"], st["ks"]
        send, recv, bufs = st["pair"]
        bufs = _split_wait("pair_wait_" + tag, bufs, send, recv, _pair_exchange_copies(ks), after)
        grads, from_sib = bufs[: len(ids)], bufs[len(ids) :]
        halves = [_pair_add("pair_add_" + names[i], c_idx, g, r, k) for i, g, r, k in zip(ids, grads, from_sib, ks)]
        st["chip"] = _chip_send_start("chip_send_" + tag, halves, ks)
        st["token"] = st["chip"][2][0]

    def reduce_c(st, after):
        tag, ids, ks = st["tag"], st["ids"], st["ks"]
        send, recv, halves, lands = st["chip"]
        halves, lands = _chip_send_wait("chip_wait_" + tag, halves, lands, ks, send, recv, after)
        pieces = [_chip_sum("chip_sum_" + names[i], qc_idx, h, r, k) for i, h, r, k in zip(ids, halves, lands, ks)]
        st["share"] = _split_start("share_send_" + tag, pieces, len(ids), _pair_share_copies(ks, False))
        st["token"] = st["share"][2][0]

    def reduce_d(st, after):
        send, recv, bufs = st["share"]
        return _split_wait("share_wait_" + st["tag"], bufs, send, recv, _pair_share_copies(st["ks"], True), after)

    grads_big, upd = [None] * 7, [None] * 7

    def finish(st, after):
        shared = reduce_d(st, after)
        after = shared[0]
        for i, g in zip(st["ids"], shared):
            upd[i] = _adamw("adamw_" + names[i], big_w[i], g, big_m[i], big_v[i], after=(after,))
            grads_big[i] = upd[i][3]
            after = upd[i][0]
        return after

    def ep_swiglu_bwd(ins, vals, outs, cs):
        df = vals[0]
        gtv, upv = ins[2][:, cs].astype(F32), ins[3][:, cs].astype(F32)
        sg = _sigmoid(gtv)
        _put(outs[0], cs, df * upv * (sg + gtv * sg * (1.0 - sg)))
        _put(outs[1], cs, df * (gtv * sg))

    dgt, dup = _mm(
        "mm_dswiglu", (s // tm, d_ff // tf, 1), [dx3b, g_down, gt, up],
        [pl.BlockSpec((tm, d), lambda i, j, k: (i, 0)), pl.BlockSpec((tf, d), lambda i, j, k: (j, 0)), f_out, f_out],
        [jax.ShapeDtypeStruct((s, d_ff), BF16), jax.ShapeDtypeStruct((s, d_ff), BF16)], [f_out, f_out],
        [(0, 1, NT, 0)], 1, (tm, tf), 1, ep_swiglu_bwd,
        _mm_vmem([((tm, d), BF16, 2), ((tf, d), BF16, 2), ((tm, tf), F32, 8)]),    )

    def ep_store(ins, vals, outs, cs):
        for o, v in zip(outs, vals):
            _put(o, cs, v)

    twn = _tile(d, (1024, 512))
    gw_down = _mm(
        "mm_gw_down", (d_ff // tkf, d // twn, 1), [f_act, dx3b],
        [pl.BlockSpec((s, tkf), lambda i, j, k: (0, i)), pl.BlockSpec((s, twn), lambda i, j, k: (0, j))],
        [jax.ShapeDtypeStruct((d_ff, d), BF16)], [pl.BlockSpec((tkf, twn), lambda i, j, k: (i, j))],
        [(0, 1, TN, 0)], 1, (tkf, twn), 1, ep_store,
        _mm_vmem([((s, tkf), BF16, 3), ((s, twn), BF16, 2), ((tkf, twn), F32, 3)]),
    )[0]
    red_down = reduce_a("down", [W_DOWN], [gw_down])

    tn2 = _tile(d, (256,))
    dh2_specs = [pl.BlockSpec((tm, d_ff), lambda i, j, k: (i, 0)), pl.BlockSpec((tn2, d_ff), lambda i, j, k: (j, 0))]
    dh2_tile = pl.BlockSpec((tm, tn2), lambda i, j, k: (i, j))
    dh2_vmem = _mm_vmem([((tm, d_ff), BF16, 2), ((tn2, d_ff), BF16, 2), ((tm, tn2), F32, 7)])
    dh2 = _mm(
        "mm_dh2_gate", (s // tm, d // tn2, 1), [dgt, g_gate], dh2_specs,
        [jax.ShapeDtypeStruct((s, d), F32)], [dh2_tile], [(0, 1, NT, 0)], 1, (tm, tn2), 1, ep_store, dh2_vmem,
        after=(red_down["token"],),
    )[0]
    dh2 = _mm(
        "mm_dh2_up", (s // tm, d // tn2, 1), [dup, g_up, dh2], dh2_specs + [dh2_tile],
        [jax.ShapeDtypeStruct((s, d), F32)], [dh2_tile], [(0, 1, NT, 0)], 1, (tm, tn2), 1, ep_residual, dh2_vmem,
    )[0]
    reduce_b(red_down, dh2)

    twr = _tile(d, (1024, 512))
    w_tile = pl.BlockSpec((twr, tf), lambda i, j, k: (i, j))
    gw_gate, gw_up = _mm(
        "mm_gw_gate_up", (d // twr, d_ff // tf, 1), [h2, dgt, dup],
        [pl.BlockSpec((s, twr), lambda i, j, k: (0, i)), pl.BlockSpec((s, tf), lambda i, j, k: (0, j)),
         pl.BlockSpec((s, tf), lambda i, j, k: (0, j))],
        [jax.ShapeDtypeStruct((d, d_ff), BF16), jax.ShapeDtypeStruct((d, d_ff), BF16)], [w_tile, w_tile],
        [(0, 1, TN, 0), (0, 2, TN, 1)], 2, (twr, tf), 1, ep_store,
        _mm_vmem([((s, twr), BF16, 3), ((s, tf), BF16, 4), ((twr, tf), F32, 6)]),
        after=(red_down["token"],),
    )
    red_ffn = reduce_a("ffn_in", [W_GATE, W_UP], [gw_gate, gw_up])

    dx2, dx2b, dg_ffn = _rms_bwd("rms_ffn_bwd", x2, norm_ffn, dh2, dx3, after=(red_ffn["token"],))

    nj = d // tg

    def lo(j):
        return jnp.minimum(j, nj - 1)

    def gate_bwd_body(dx_ref, wo_ref, ga_ref, gb_ref, ya_ref, yb_ref, dya_ref, dyb_ref, dz_ref, keep):
        j = pl.program_id(1)

        @pl.when(j < nj)
        def _():
            dm = lax.dot_general(dx_ref[...], wo_ref[...], NT, preferred_element_type=F32)
            sa, sb = _sigmoid(ga_ref[...].astype(F32)), _sigmoid(gb_ref[...].astype(F32))
            dya_ref[...] = (dm * sa).astype(BF16)
            dyb_ref[...] = (dm * sb).astype(BF16)
            dz_ref[...] = (dm * ya_ref[...].astype(F32) * (sa * (1.0 - sa))).astype(BF16)
            keep[lo(j)] = (dm * yb_ref[...].astype(F32) * (sb * (1.0 - sb))).astype(BF16)

        @pl.when(j >= nj)
        def _():
            dz_ref[...] = keep[jnp.maximum(j - nj, 0)]

    t_lo = pl.BlockSpec((tm, tg), lambda i, j: (i, lo(j)))
    dya, dyb, dz = _pallas(
        gate_bwd_body,
        name="mm_dgate",
        grid=(s // tm, 2 * nj),
        in_specs=[
            pl.BlockSpec((tm, d), lambda i, j: (i, 0)),
            pl.BlockSpec((tg, d), lambda i, j: (lo(j), 0)),
            pl.BlockSpec((tm, tg), lambda i, j: (i, ga0 + lo(j))),
            pl.BlockSpec((tm, tg), lambda i, j: (i, gb0 + lo(j))),
            t_lo,
            t_lo,
        ],
        out_specs=[t_lo, t_lo, pl.BlockSpec((tm, tg), lambda i, j: (i, ga0 + j))],
        out_shape=[jax.ShapeDtypeStruct((s, d), BF16), jax.ShapeDtypeStruct((s, d), BF16), jax.ShapeDtypeStruct((s, n_in), BF16)],
        scratch_shapes=[pltpu.VMEM((nj, tm, tg), BF16)],
        compiler_params=_params(_mm_vmem([((tm, d), BF16, 2), ((tg, d), BF16, 2), ((tm, tg), F32, 14), ((nj, tm, tg), BF16, 1)])),
    )(dx2b, g_o, z, z, y_a, y_b)
    reduce_b(red_ffn, dya)
    reduce_c(red_down, red_ffn["token"])

    gw_o = _mm(
        "mm_gw_o", (d // twr, d // twn, 1), [m_act, dx2b],
        [pl.BlockSpec((s, twr), lambda i, j, k: (0, i)), pl.BlockSpec((s, twn), lambda i, j, k: (0, j))],
        [jax.ShapeDtypeStruct((d, d), BF16)], [pl.BlockSpec((twr, twn), lambda i, j, k: (i, j))],
        [(0, 1, TN, 0)], 1, (twr, twn), 1, ep_store,
        _mm_vmem([((s, twr), BF16, 3), ((s, twn), BF16, 2), ((twr, twn), F32, 3)]),
        after=(red_down["token"],),
    )[0]
    red_o = reduce_a("w_o", [W_O], [gw_o])
    finish(red_down, red_o["token"])

    tb = _tile(w_sgu, (1024, 512))
    b_out = pl.BlockSpec((tm, tb), lambda i, j, k: (i, j))

    da, datt = _mm(
        "mm_dbranches", (s // tm, w_sgu // tb, 1), [dya, g_a, dyb, g_b],
        [pl.BlockSpec((tm, d), lambda i, j, k: (i, 0)), pl.BlockSpec((tb, d), lambda i, j, k: (j, 0)),
         pl.BlockSpec((tm, d), lambda i, j, k: (i, 0)), pl.BlockSpec((tb, d), lambda i, j, k: (j, 0))],
        [jax.ShapeDtypeStruct((s, w_sgu), BF16), jax.ShapeDtypeStruct((s, w_att), BF16)], [b_out, b_out],
        [(0, 1, NT, 0), (2, 3, NT, 1)], 2, (tm, tb), 1, ep_store,
        _mm_vmem([((tm, d), BF16, 4), ((tb, d), BF16, 4), ((tm, tb), F32, 6)]),        after=(red_o["token"],),
    )
    reduce_b(red_o, da)

    wb_tile = pl.BlockSpec((tb, twn), lambda i, j, k: (i, j))
    gw_a, gw_b = _mm(
        "mm_gw_branches", (w_sgu // tb, d // twn, 1), [a_act, dya, att, dyb],
        [pl.BlockSpec((s, tb), lambda i, j, k: (0, i)), pl.BlockSpec((s, twn), lambda i, j, k: (0, j)),
         pl.BlockSpec((s, tb), lambda i, j, k: (0, i)), pl.BlockSpec((s, twn), lambda i, j, k: (0, j))],
        [jax.ShapeDtypeStruct((w_sgu, d), BF16), jax.ShapeDtypeStruct((w_att, d), BF16)], [wb_tile, wb_tile],
        [(0, 1, TN, 0), (2, 3, TN, 1)], 2, (tb, twn), 1, ep_store,
        _mm_vmem([((s, tb), BF16, 5), ((s, twn), BF16, 4), ((tb, twn), F32, 6)]),
        after=(red_o["token"],),
    )
    red_mix = reduce_a("mix", [W_A, W_B], [gw_a, gw_b])

    dz, dws, dbs, dgain = _sgu_bwd(z, da, sgu_v_gain, ws_b, wst_b, b_col, w_sgu, dz, after=(red_mix["token"],))
    dz, dk_pad, dv_pad, dbias_tab, dsink = _attn_bwd(sink, z, k_pad, v_pad, bias_tab, datt, dz, grp, q_blk0)
    dz = _dkv_to_dz(dk_pad, dv_pad, dz, off_k // (2 * w_kv))
    drel = _relbias_bwd(dbias_tab, bucket)
    reduce_b(red_mix, dz)
    reduce_c(red_ffn, red_mix["token"])

    small_w = [norm_mix, sgu_v_gain, sgu_w_s, sgu_b_s, attn_sink, rel_bias, norm_ffn, norm_final]
    small_m = [m_norm_mix, m_sgu_v_gain, m_sgu_w_s, m_sgu_b_s, m_attn_sink, m_rel_bias, m_norm_ffn, m_norm_final]
    small_v = [v_norm_mix, v_sgu_v_gain, v_sgu_w_s, v_sgu_b_s, v_attn_sink, v_rel_bias, v_norm_ffn, v_norm_final]
    small_shapes = [w.shape for w in small_w]
    early = [dgain, dws, dbs, dsink[:, 0], drel[:, :REL_BUCKETS].T, dg_ffn, dg_final]
    p_early = _pack([g.reshape(shp) for g, shp in zip(early, small_shapes[1:])] + [loss_part[0, :1]])
    land = jnp.zeros((2 * N_CHIPS,) + p_early.shape, F32)
    sm_send, sm_recv, (p_early, land) = _split_start("small_send", [p_early, land], 2 * N_CHIPS - 1, _small_exchange_copies(False))

    tzn = _tile(n_in, (768, 640, 512))
    gw_in = _mm(
        "mm_gw_in", (d // twr, n_in // tzn, 1), [h1, dz],
        [pl.BlockSpec((s, twr), lambda i, j, k: (0, i)), pl.BlockSpec((s, tzn), lambda i, j, k: (0, j))],
        [jax.ShapeDtypeStruct((d, n_in), BF16)], [pl.BlockSpec((twr, tzn), lambda i, j, k: (i, j))],
        [(0, 1, TN, 0)], 1, (twr, tzn), 1, ep_store,
        _mm_vmem([((s, twr), BF16, 3), ((s, tzn), BF16, 2), ((twr, tzn), F32, 3)]),
        after=(red_ffn["token"], p_early),
    )[0]
    red_in = reduce_a("w_in", [W_IN], [gw_in])

    reduce_c(red_o, red_in["token"])
    reduce_c(red_mix, red_o["token"])
    reduce_b(red_in, finish(red_mix, finish(red_o, red_mix["token"])))

    dh1 = _mm(
        "mm_dh1", (s // tm, d // tn2, 1), [dz, g_in],
        [pl.BlockSpec((tm, n_in), lambda i, j, k: (i, 0)), pl.BlockSpec((tn2, n_in), lambda i, j, k: (j, 0))],
        [jax.ShapeDtypeStruct((s, d), F32)], [pl.BlockSpec((tm, tn2), lambda i, j, k: (i, j))],
        [(0, 1, NT, 0)], 1, (tm, tn2), 1, ep_store,
        _mm_vmem([((tm, n_in), BF16, 2), ((tn2, n_in), BF16, 2), ((tm, tn2), F32, 5)]),
        after=(red_in["token"],),
    )[0]

    grad_x, _, dg_mix = _rms_bwd("rms_mix_bwd", x2d, norm_mix, dh1, dx2)

    p_mix = _pack([dg_mix.reshape(small_shapes[0])])
    land_mix = jnp.zeros((2 * N_CHIPS,) + p_mix.shape, F32)
    mx_send, mx_recv, (p_mix, land_mix) = _split_start("mix_send", [p_mix, land_mix], 2 * N_CHIPS - 1, _small_exchange_copies(False))

    reduce_c(red_in, finish(red_ffn, p_mix))
    p_early, land = _split_wait("small_wait", [p_early, land], sm_send, sm_recv, _small_exchange_copies(True), red_in["token"])
    p_mix, land_mix = _split_wait("mix_wait", [p_mix, land_mix], mx_send, mx_recv, _small_exchange_copies(True), p_early)
    me_idx = 2 * q_idx + c_idx
    packed_g = jnp.concatenate([_small_sum("mix_sum", me_idx, p_mix, land_mix), _small_sum("small_sum", me_idx, p_early, land)], axis=0)
    g_small = _unpack(packed_g, small_shapes + [(1,)])
    loss = g_small[-1].reshape(())
    g_small = g_small[:-1]
    zero1 = jnp.zeros((1,), F32)
    pw, pg, pm, pv = _pack(small_w + [zero1]), _pack(g_small + [zero1]), _pack(small_m + [zero1]), _pack(small_v + [zero1])
    small_upd = _adamw("adamw_small", pw, pg, pm, pv)
    d_small, nm_small, nv_small = [_unpack(a, small_shapes) for a in small_upd[:3]]
    finish(red_in, small_upd[0])

    small_names = ["norm_mix", "sgu_v_gain", "sgu_w_s", "sgu_b_s", "attn_sink", "rel_bias", "norm_ffn", "norm_final"]
    table = {}
    for i, n in enumerate(names):
        table[n] = (grads_big[i][None], upd[i][0][None], upd[i][1][None], upd[i][2][None])
    for i, n in enumerate(small_names):
        table[n] = (g_small[i], d_small[i], nm_small[i], nv_small[i])
    order = ["w_in", "norm_mix", "sgu_v_gain", "sgu_w_s", "sgu_b_s", "w_a", "attn_sink", "rel_bias", "w_b", "w_o", "norm_ffn",
             "w_gate", "w_up", "w_down", "norm_final"]
    outs = [loss, grad_x.reshape(1, s, d)]
    for part in range(4):
        outs += [table[n][part] for n in order]
    return tuple(outs)
```

```python
import math

import jax
import jax.numpy as jnp
import numpy as np
from jax import lax
from jax.experimental import pallas as pl
from jax.experimental.pallas import tpu as pltpu

F32 = jnp.float32
BF16 = jnp.bfloat16
I32 = jnp.int32
MESH = pl.DeviceIdType.MESH

EPS = 1e-6
NEG = -1e30
BLK = 128
HEAD_DIM = 128
N_KV_HEADS = 2
REL_BUCKETS = 32
REL_MAX_DIST = 128
N_CHIPS = 4
ADAM_LR, ADAM_B1, ADAM_B2, ADAM_EPS, ADAM_WD, ADAM_STEP = 0.001, 0.9, 0.999, 1e-08, 0.01, 10

LANES = 128
VMEM_CAP = 60 * 1024 * 1024

NN = (((1,), (0,)), ((), ()))
NT = (((1,), (1,)), ((), ()))
TN = (((0,), (0,)), ((), ()))
ANY = pl.BlockSpec(memory_space=pl.ANY)
HBM_SPEC = pl.BlockSpec(memory_space=pltpu.HBM)
SEM_SPEC = pl.BlockSpec(memory_space=pltpu.SEMAPHORE)
EFFECT = pltpu.SideEffectType.DATAFLOW_SIDE_EFFECTING


def _tile(n, cands):
    for t in cands:
        if n % t == 0:
            return t
    return n


PIN_BYTES = 64 * 1024


def _pin_hbm(a):
    big = hasattr(a, "dtype") and jnp.issubdtype(a.dtype, jnp.floating) and _nbytes(a.shape, a.dtype) >= PIN_BYTES
    return pltpu.with_memory_space_constraint(a, pltpu.HBM) if big else a


def _pallas(body, *, out_shape, **kw):
    def pin(o):
        big = isinstance(o, jax.ShapeDtypeStruct) and jnp.issubdtype(o.dtype, jnp.floating) and _nbytes(o.shape, o.dtype) >= PIN_BYTES
        return pltpu.HBM(o.shape, o.dtype) if big else o

    shapes = type(out_shape)(pin(o) for o in out_shape) if isinstance(out_shape, (list, tuple)) else pin(out_shape)
    call = pl.pallas_call(body, out_shape=shapes, **kw)
    return lambda *args: call(*[_pin_hbm(a) for a in args])


def _params(vmem_bytes=None, **kw):
    if vmem_bytes is not None:
        kw["vmem_limit_bytes"] = int(min(max(vmem_bytes, 32 * 1024 * 1024), VMEM_CAP))
    return pltpu.CompilerParams(**kw)


def _nbytes(shape, dtype):
    return int(np.prod(shape)) * jnp.dtype(dtype).itemsize


def _sigmoid(x):
    return 1.0 / (1.0 + jnp.exp(-x))


_GC = 0.7978845608028654
_GA = 0.044715


def _gelu(x):
    return 0.5 * x * (1.0 + jnp.tanh(_GC * (x + _GA * (x * x * x))))


def _gelu_grad(x):
    t = jnp.tanh(_GC * (x + _GA * (x * x * x)))
    return 0.5 * (1.0 + t) + 0.5 * x * (1.0 - t * t) * (_GC * (1.0 + 3.0 * _GA * (x * x)))


def _bf(v):
    return v if v.dtype == BF16 else v.astype(BF16)


def _mm(name, grid, ins, in_specs, out_shape, out_specs, pairs, n_acc, tile, nk, epilogue, vmem_bytes, after=()):
    assert nk == 1
    n_in, n_out = len(ins) + len(after), len(out_shape)

    def body(*refs):
        in_refs, out_refs = refs[:n_in], refs[n_in : n_in + n_out]
        vals = [None] * n_acc
        for a_i, b_i, dn, acc_i in pairs:
            d = lax.dot_general(_bf(in_refs[a_i][...]), _bf(in_refs[b_i][...]), dn, preferred_element_type=F32)
            vals[acc_i] = d if vals[acc_i] is None else vals[acc_i] + d
        epilogue(in_refs, vals, out_refs, slice(None))

    return _pallas(
        body,
        name=name,
        grid=grid,
        in_specs=list(in_specs) + [ANY] * len(after),
        out_specs=out_specs,
        out_shape=out_shape,
        compiler_params=_params(vmem_bytes),
    )(*ins, *after)


def _put(ref, cs, v):
    ref[:, cs] = v.astype(ref.dtype)


def _mm_vmem(tiles):
    return sum(_nbytes(s, d) * c for s, d, c in tiles) + 4 * 1024 * 1024


def _rows8(v):
    r, d = v.shape
    return v.reshape(r // 8, 8, d).sum(axis=0)


def _rms_fwd(name, x, g, after=()):
    s, d = x.shape
    tm = _tile(s, (256, 128))

    def body(x_ref, g_ref, *rest):
        h_ref = rest[-1]
        xv = x_ref[...]
        r = lax.rsqrt(jnp.mean(xv * xv, axis=-1, keepdims=True) + EPS)
        h_ref[...] = ((xv * r) * g_ref[...]).astype(BF16)

    return _pallas(
        body,
        name=name,
        grid=(s // tm,),
        in_specs=[pl.BlockSpec((tm, d), lambda i: (i, 0)), pl.BlockSpec((1, d), lambda i: (0, 0))] + [ANY] * len(after),
        out_specs=pl.BlockSpec((tm, d), lambda i: (i, 0)),
        out_shape=jax.ShapeDtypeStruct((s, d), BF16),
    )(x, g, *after)


def _rms_bwd(name, x, g, dh, dres, after=()):
    s, d = x.shape
    tm = _tile(s, (256, 128))
    n = s // tm
    n_after = len(after)

    def body(x_ref, g_ref, dh_ref, dres_ref, *rest):
        dx_ref, dxb_ref, dg_ref, acc_ref = rest[n_after:]
        i = pl.program_id(0)
        xv = x_ref[...]
        r = lax.rsqrt(jnp.mean(xv * xv, axis=-1, keepdims=True) + EPS)
        xh = xv * r
        dhv = dh_ref[...]
        dxh = dhv * g_ref[...]
        dx = r * (dxh - xh * jnp.mean(dxh * xh, axis=-1, keepdims=True)) + dres_ref[...]
        dx_ref[...] = dx
        dxb_ref[...] = dx.astype(BF16)
        part = _rows8(dhv * xh)

        @pl.when(i == 0)
        def _():
            acc_ref[...] = part

        @pl.when(i > 0)
        def _():
            acc_ref[...] += part

        @pl.when(i == n - 1)
        def _():
            dg_ref[...] = jnp.sum(acc_ref[...], axis=0, keepdims=True)

    row = pl.BlockSpec((tm, d), lambda i: (i, 0))
    vec = pl.BlockSpec((1, d), lambda i: (0, 0))
    return _pallas(
        body,
        name=name,
        grid=(n,),
        in_specs=[row, vec, row, row] + [ANY] * n_after,
        out_specs=[row, row, vec],
        out_shape=[jax.ShapeDtypeStruct((s, d), F32), jax.ShapeDtypeStruct((s, d), BF16), jax.ShapeDtypeStruct((1, d), F32)],
        scratch_shapes=[pltpu.VMEM((8, d), F32)],
    )(x, g, dh, dres, *after)


def _head(x3, g, target):
    s, d = x3.shape
    tm = _tile(s, (256, 128))
    n = s // tm

    def body(x_ref, g_ref, t_ref, dx_ref, dxb_ref, dg_ref, loss_ref, acc_g, acc_l):
        i = pl.program_id(0)
        xv = x_ref[...]
        gv = g_ref[...]
        r = lax.rsqrt(jnp.mean(xv * xv, axis=-1, keepdims=True) + EPS)
        xh = xv * r
        e = xh * gv - t_ref[...]
        dy = e * (1.0 / d)
        dxh = dy * gv
        dx = r * (dxh - xh * jnp.mean(dxh * xh, axis=-1, keepdims=True))
        dx_ref[...] = dx
        dxb_ref[...] = dx.astype(BF16)
        pg = _rows8(dy * xh)
        plo = _rows8(e * e)

        @pl.when(i == 0)
        def _():
            acc_g[...] = pg
            acc_l[...] = plo

        @pl.when(i > 0)
        def _():
            acc_g[...] += pg
            acc_l[...] += plo

        @pl.when(i == n - 1)
        def _():
            dg_ref[...] = jnp.sum(acc_g[...], axis=0, keepdims=True)
            loss_ref[...] = jnp.full((1, LANES), (0.5 / d) * jnp.sum(acc_l[...]), F32)

    row = pl.BlockSpec((tm, d), lambda i: (i, 0))
    vec = pl.BlockSpec((1, d), lambda i: (0, 0))
    return _pallas(
        body,
        name="head",
        grid=(n,),
        in_specs=[row, vec, row],
        out_specs=[row, row, vec, pl.BlockSpec((1, LANES), lambda i: (0, 0))],
        out_shape=[
            jax.ShapeDtypeStruct((s, d), F32),
            jax.ShapeDtypeStruct((s, d), BF16),
            jax.ShapeDtypeStruct((1, d), F32),
            jax.ShapeDtypeStruct((1, LANES), F32),
        ],
        scratch_shapes=[pltpu.VMEM((8, d), F32), pltpu.VMEM((8, d), F32)],
    )(x3, g, target)


def _sgu_fwd(z, gain, ws_b, b_col, w_sgu):
    s = z.shape[0]
    groups = ws_b.shape[0]

    def body(zu_ref, zv_ref, gain_ref, ws_ref, b_ref, a_ref):
        vv = _gelu(zv_ref[...].astype(F32))
        r = lax.rsqrt(jnp.mean(vv * vv, axis=-1, keepdims=True) + EPS)
        vn = ((vv * r) * gain_ref[...]).astype(BF16)
        u = _gelu(zu_ref[...].astype(F32))
        for g in range(groups):
            sl = slice(g * BLK, (g + 1) * BLK)
            mixed = jnp.dot(ws_ref[g], vn[:, sl], preferred_element_type=F32) + b_ref[g]
            a_ref[:, sl] = (u[:, sl] * mixed).astype(BF16)

    return _pallas(
        body,
        name="sgu_fwd",
        grid=(s // BLK,),
        in_specs=[
            pl.BlockSpec((BLK, w_sgu), lambda c: (c, 0)),
            pl.BlockSpec((BLK, w_sgu), lambda c: (c, 1)),
            pl.BlockSpec((1, w_sgu), lambda c: (0, 0)),
            pl.BlockSpec((groups, BLK, BLK), lambda c: (0, 0, 0)),
            pl.BlockSpec((groups, BLK, 1), lambda c: (0, 0, 0)),
        ],
        out_specs=pl.BlockSpec((BLK, w_sgu), lambda c: (c, 0)),
        out_shape=jax.ShapeDtypeStruct((s, w_sgu), BF16),
    )(z, z, gain, ws_b, b_col)


def _sgu_bwd(z, da, gain, ws_b, wst_b, b_col, w_sgu, dz, after=()):
    s = z.shape[0]
    groups = ws_b.shape[0]
    n = s // BLK
    n_skip = 1 + len(after)

    def body(zu_ref, zv_ref, da_ref, gain_ref, ws_ref, wst_ref, b_ref, *rest):
        dz_ref, dws_ref, dbs_ref, dgain_ref, acc_gain = rest[n_skip:]
        c = pl.program_id(0)
        zu = zu_ref[...].astype(F32)
        zv = zv_ref[...].astype(F32)
        gain_v = gain_ref[...]
        vv = _gelu(zv)
        r = lax.rsqrt(jnp.mean(vv * vv, axis=-1, keepdims=True) + EPS)
        xh = vv * r
        vn = (xh * gain_v).astype(BF16)
        u = _gelu(zu)
        dav = da_ref[...].astype(F32)
        dmix = dav * u
        dmix_b = dmix.astype(BF16)
        dvn_parts = []
        for g in range(groups):
            sl = slice(g * BLK, (g + 1) * BLK)
            mixed = jnp.dot(ws_ref[g], vn[:, sl], preferred_element_type=F32) + b_ref[g]
            dz_ref[:, sl] = (dav[:, sl] * mixed * _gelu_grad(zu[:, sl])).astype(BF16)
            dvn_parts.append(jnp.dot(wst_ref[g], dmix_b[:, sl], preferred_element_type=F32))
            dws_g = lax.dot_general(dmix_b[:, sl], vn[:, sl], NT, preferred_element_type=F32)
            dbs_g = jnp.sum(dmix[:, sl], axis=1, keepdims=True)

            @pl.when(c == 0)
            def _():
                dws_ref[g] = dws_g
                dbs_ref[g] = dbs_g

            @pl.when(c > 0)
            def _():
                dws_ref[g] += dws_g
                dbs_ref[g] += dbs_g

        dvn = jnp.concatenate(dvn_parts, axis=1)
        dxh = dvn * gain_v
        dvv = r * (dxh - xh * jnp.mean(dxh * xh, axis=-1, keepdims=True))
        dz_ref[:, w_sgu:] = (dvv * _gelu_grad(zv)).astype(BF16)
        pg = _rows8(dvn * xh)

        @pl.when(c == 0)
        def _():
            acc_gain[...] = pg

        @pl.when(c > 0)
        def _():
            acc_gain[...] += pg

        @pl.when(c == n - 1)
        def _():
            dgain_ref[...] = jnp.sum(acc_gain[...], axis=0, keepdims=True)

    full3 = pl.BlockSpec((groups, BLK, BLK), lambda c: (0, 0, 0))
    col3 = pl.BlockSpec((groups, BLK, 1), lambda c: (0, 0, 0))
    vec = pl.BlockSpec((1, w_sgu), lambda c: (0, 0))
    return _pallas(
        body,
        name="sgu_bwd",
        grid=(n,),
        in_specs=[
            pl.BlockSpec((BLK, w_sgu), lambda c: (c, 0)),
            pl.BlockSpec((BLK, w_sgu), lambda c: (c, 1)),
            pl.BlockSpec((BLK, w_sgu), lambda c: (c, 0)),
            vec,
            full3,
            full3,
            col3,
            ANY,
        ]
        + [ANY] * len(after),
        out_specs=[pl.BlockSpec((BLK, 2 * w_sgu), lambda c: (c, 0)), full3, col3, vec],
        out_shape=[
            jax.ShapeDtypeStruct(dz.shape, BF16),
            jax.ShapeDtypeStruct((groups, BLK, BLK), F32),
            jax.ShapeDtypeStruct((groups, BLK, 1), F32),
            jax.ShapeDtypeStruct((1, w_sgu), F32),
        ],
        scratch_shapes=[pltpu.VMEM((8, w_sgu), F32)],
        input_output_aliases={7: 0},
    )(z, z, da, gain, ws_b, wst_b, b_col, dz, *after)


def _attn_softmax(sink_ref, q_ref, k_ref, v_ref, bias_ref, s_len, grp):
    kv = pl.program_id(0)
    n = pl.program_id(1)
    start = pl.multiple_of(n * BLK, BLK)
    kb = k_ref[pl.ds(start, 3 * BLK), :]
    vb = v_ref[pl.ds(start, 3 * BLK), :]
    qv = q_ref[...]
    qs = jnp.concatenate([qv[:, g * HEAD_DIM : (g + 1) * HEAD_DIM] for g in range(grp)], axis=0).astype(BF16)
    sc = lax.dot_general(qs, kb, NT, preferred_element_type=F32) * (HEAD_DIM**-0.5)
    sc = sc + bias_ref[...].reshape(grp * BLK, 3 * BLK)
    kpos = start + lax.broadcasted_iota(I32, (1, 3 * BLK), 1) - BLK
    sc = jnp.where((kpos >= 0) & (kpos < s_len), sc, NEG)
    sink = jnp.concatenate([jnp.full((BLK, 1), sink_ref[kv * grp + g], F32) for g in range(grp)], axis=0)
    m = jnp.maximum(jnp.max(sc, axis=-1, keepdims=True), sink)
    p = jnp.exp(sc - m)
    esink = jnp.exp(sink - m)
    den = jnp.sum(p, axis=-1, keepdims=True) + esink
    return start, qs, kb, vb, p / den, esink / den


def _attn_specs(s, grp, q_blk0):
    qw = grp * HEAD_DIM
    return [
        pl.BlockSpec(memory_space=pltpu.SMEM),
        pl.BlockSpec((BLK, qw), lambda kv, n: (n, q_blk0 + kv)),
        pl.BlockSpec((s + 2 * BLK, HEAD_DIM), lambda kv, n: (0, kv)),
        pl.BlockSpec((s + 2 * BLK, HEAD_DIM), lambda kv, n: (0, kv)),
        pl.BlockSpec((grp, BLK, 3 * BLK), lambda kv, n: (kv, 0, 0)),
    ]


def _attn_fwd(sink, z, k_pad, v_pad, bias_tab, grp, q_blk0):
    s = z.shape[0]
    qw = grp * HEAD_DIM

    def body(sink_ref, q_ref, k_ref, v_ref, bias_ref, o_ref):
        _, _, _, vb, pn, _ = _attn_softmax(sink_ref, q_ref, k_ref, v_ref, bias_ref, s, grp)
        o = jnp.dot(pn.astype(BF16), vb, preferred_element_type=F32)
        for g in range(grp):
            o_ref[:, g * HEAD_DIM : (g + 1) * HEAD_DIM] = o[g * BLK : (g + 1) * BLK].astype(BF16)

    return _pallas(
        body,
        name="attn_fwd",
        grid=(N_KV_HEADS, s // BLK),
        in_specs=_attn_specs(s, grp, q_blk0),
        out_specs=pl.BlockSpec((BLK, qw), lambda kv, n: (n, kv)),
        out_shape=jax.ShapeDtypeStruct((s, N_KV_HEADS * qw), BF16),
    )(sink, z, k_pad, v_pad, bias_tab)


def _attn_bwd(sink, z, k_pad, v_pad, bias_tab, dout, dz, grp, q_blk0):
    s = z.shape[0]
    qw = grp * HEAD_DIM
    nb = s // BLK
    heads = N_KV_HEADS * grp

    def body(sink_ref, q_ref, k_ref, v_ref, bias_ref, do_ref, dz_in, dq_ref, dk_ref, dv_ref, dbias_ref, dsink_ref, dk_acc, dv_acc):
        del dz_in
        kv = pl.program_id(0)
        n = pl.program_id(1)
        start, qs, kb, vb, pn, psink = _attn_softmax(sink_ref, q_ref, k_ref, v_ref, bias_ref, s, grp)
        dov = do_ref[...]
        dos = jnp.concatenate([dov[:, g * HEAD_DIM : (g + 1) * HEAD_DIM] for g in range(grp)], axis=0)
        dp = lax.dot_general(dos, vb, NT, preferred_element_type=F32)
        dvb = lax.dot_general(pn.astype(BF16), dos, TN, preferred_element_type=F32)
        delta = jnp.sum(pn * dp, axis=-1, keepdims=True)
        ds = pn * (dp - delta)
        dsb = (ds * (HEAD_DIM**-0.5)).astype(BF16)
        dq = jnp.dot(dsb, kb, preferred_element_type=F32)
        dkb = lax.dot_general(dsb, qs, TN, preferred_element_type=F32)
        for g in range(grp):
            dq_ref[:, g * HEAD_DIM : (g + 1) * HEAD_DIM] = dq[g * BLK : (g + 1) * BLK].astype(BF16)

        @pl.when(n == 0)
        def _():
            dk_acc[...] = jnp.zeros_like(dk_acc)
            dv_acc[...] = jnp.zeros_like(dv_acc)
            dbias_ref[...] = jnp.zeros_like(dbias_ref)

        @pl.when((n == 0) & (kv == 0))
        def _():
            dsink_ref[...] = jnp.zeros_like(dsink_ref)

        dk_acc[pl.ds(start, 3 * BLK), :] += dkb
        dv_acc[pl.ds(start, 3 * BLK), :] += dvb
        dbias_ref[...] += ds.reshape(grp, BLK, 3 * BLK)
        row = lax.broadcasted_iota(I32, (heads, LANES), 0)
        sd = psink * delta
        upd = jnp.zeros((heads, LANES), F32)
        for g in range(grp):
            upd = jnp.where(row == kv * grp + g, -jnp.sum(sd[g * BLK : (g + 1) * BLK]), upd)
        dsink_ref[...] += upd

        @pl.when(n == nb - 1)
        def _():
            dk_ref[...] = dk_acc[...]
            dv_ref[...] = dv_acc[...]

    pad_spec = pl.BlockSpec((s + 2 * BLK, HEAD_DIM), lambda kv, n: (0, kv))
    kvw = N_KV_HEADS * HEAD_DIM
    return _pallas(
        body,
        name="attn_bwd",
        grid=(N_KV_HEADS, nb),
        in_specs=_attn_specs(s, grp, q_blk0) + [pl.BlockSpec((BLK, qw), lambda kv, n: (n, kv)), ANY],
        out_specs=[
            pl.BlockSpec((BLK, qw), lambda kv, n: (n, q_blk0 + kv)),
            pad_spec,
            pad_spec,
            pl.BlockSpec((grp, BLK, 3 * BLK), lambda kv, n: (kv, 0, 0)),
            pl.BlockSpec((heads, LANES), lambda kv, n: (0, 0)),
        ],
        out_shape=[
            jax.ShapeDtypeStruct(dz.shape, BF16),
            jax.ShapeDtypeStruct((s + 2 * BLK, kvw), F32),
            jax.ShapeDtypeStruct((s + 2 * BLK, kvw), F32),
            jax.ShapeDtypeStruct((heads, BLK, 3 * BLK), F32),
            jax.ShapeDtypeStruct((heads, LANES), F32),
        ],
        scratch_shapes=[pltpu.VMEM((s + 2 * BLK, HEAD_DIM), F32), pltpu.VMEM((s + 2 * BLK, HEAD_DIM), F32)],
        input_output_aliases={6: 0},
    )(sink, z, k_pad, v_pad, bias_tab, dout, dz)


def _dkv_to_dz(dk_pad, dv_pad, dz, blk_idx):
    s = dz.shape[0]
    kvw = dk_pad.shape[1]

    def body(dk_ref, dv_ref, dz_in, out_ref):
        del dz_in
        out_ref[:, :kvw] = dk_ref[...].astype(BF16)
        out_ref[:, kvw:] = dv_ref[...].astype(BF16)

    src = pl.BlockSpec((BLK, kvw), lambda i: (i + 1, 0))
    return _pallas(
        body,
        name="dkv_to_dz",
        grid=(s // BLK,),
        in_specs=[src, src, ANY],
        out_specs=pl.BlockSpec((BLK, 2 * kvw), lambda i: (i, blk_idx)),
        out_shape=jax.ShapeDtypeStruct(dz.shape, BF16),
        input_output_aliases={2: 0},
    )(dk_pad, dv_pad, dz)


def _relbias_bwd(dbias_tab, bucket):
    heads = dbias_tab.shape[0]

    def body(dt_ref, bk_ref, out_ref):
        lane = lax.broadcasted_iota(I32, (1, LANES), 1)
        bk = bk_ref[...]
        rows = []
        for h in range(heads):
            dt = dt_ref[h]
            acc = jnp.zeros((1, LANES), F32)
            for b in range(REL_BUCKETS):
                acc = jnp.where(lane == b, jnp.sum(jnp.where(bk == b, dt, 0.0)), acc)
            rows.append(acc)
        out_ref[...] = jnp.concatenate(rows, axis=0)

    return _pallas(body, name="relbias_bwd", out_shape=jax.ShapeDtypeStruct((heads, LANES), F32))(dbias_tab, bucket)


def _t5_bucket(rel):
    nb = REL_BUCKETS // 2
    ret = jnp.where(rel > 0, nb, 0)
    n = jnp.abs(rel)
    max_exact = nb // 2
    nf = jnp.maximum(n, 1).astype(F32)
    large = max_exact + (jnp.log(nf / max_exact) / math.log(REL_MAX_DIST / max_exact) * (nb - max_exact)).astype(I32)
    large = jnp.minimum(large, nb - 1)
    return ret + jnp.where(n < max_exact, n, large)


def _band_tables(rel_bias):
    qi = jnp.arange(BLK)[:, None]
    kj = jnp.arange(3 * BLK)[None, :]
    rel = kj - BLK - qi
    bucket = _t5_bucket(rel).astype(I32)
    heads = rel_bias.shape[1]
    masked = jnp.where(jnp.abs(rel) <= BLK, bucket, -1)

    def body(rb_ref, bk_ref, out_ref):
        bk = bk_ref[...]
        for h in range(heads):
            tab = jnp.full(bk.shape, NEG, F32)
            for b in range(REL_BUCKETS):
                tab = jnp.where(bk == b, rb_ref[b, h], tab)
            out_ref[h] = tab

    bias_tab = _pallas(
        body,
        name="bias_table",
        in_specs=[pl.BlockSpec(memory_space=pltpu.SMEM), pl.BlockSpec(memory_space=pltpu.VMEM)],
        out_specs=pl.BlockSpec(memory_space=pltpu.VMEM),
        out_shape=jax.ShapeDtypeStruct((heads, BLK, 3 * BLK), F32),
    )(rel_bias.astype(F32), masked)
    return bias_tab, bucket


EW_BLOCK_ELEMS = 512 * 1024


def _ew_tiles(shape, elems=EW_BLOCK_ELEMS // 2):
    r, c = shape
    tn = c if c <= 2048 else _tile(c, (2048, 1920, 1536, 1408, 1024, 512))
    tm = _tile(r, [t for t in (1024, 512, 256, 128, 64, 32, 16, 8) if t * tn <= elems] or [8])
    return tm, tn


def _cast_into_full(name, qidx, w, kind, after=()):
    r, c = w.shape
    tm, tn = _ew_tiles(w.shape, EW_BLOCK_ELEMS)
    nbi, nbj = r // tm, c // tn
    if kind == "col":
        full, out_spec = (r, c * N_CHIPS), pl.BlockSpec((tm, tn), lambda i, j, q: (i, q[0] * nbj + j))
    else:
        full, out_spec = (r * N_CHIPS, c), pl.BlockSpec((tm, tn), lambda i, j, q: (q[0] * nbi + i, j))

    def body(q_ref, w_ref, *rest):
        del q_ref
        rest[-1][...] = w_ref[...].astype(BF16)

    return _pallas(
        body,
        name=name,
        grid_spec=pltpu.PrefetchScalarGridSpec(
            num_scalar_prefetch=1,
            grid=(nbi, nbj),
            in_specs=[pl.BlockSpec((tm, tn), lambda i, j, q: (i, j))] + [ANY] * len(after),
            out_specs=out_spec,
        ),
        out_shape=jax.ShapeDtypeStruct(full, BF16),
    )(qidx, w, *after)


def _adamw(name, w, g, m, v, after=()):
    tm, tn = _ew_tiles(w.shape, EW_BLOCK_ELEMS)
    if _nbytes(w.shape, F32) <= 1024 * 1024:
        tm, tn = w.shape
    spec = pl.BlockSpec((tm, tn), lambda i, j: (i, j))
    n_after = len(after)

    def body(w_ref, g_ref, m_ref, v_ref, *rest):
        d_ref, nm_ref, nv_ref, g_out_ref = rest[n_after:]
        gv = g_ref[...]
        g_out_ref[...] = gv
        nm = ADAM_B1 * m_ref[...] + (1.0 - ADAM_B1) * gv
        nv = ADAM_B2 * v_ref[...] + (1.0 - ADAM_B2) * (gv * gv)
        m_hat = nm / (1.0 - ADAM_B1**ADAM_STEP)
        v_hat = nv / (1.0 - ADAM_B2**ADAM_STEP)
        d_ref[...] = -ADAM_LR * (m_hat / (jnp.sqrt(v_hat) + ADAM_EPS) + ADAM_WD * w_ref[...])
        nm_ref[...] = nm
        nv_ref[...] = nv

    out = jax.ShapeDtypeStruct(w.shape, F32)
    return _pallas(
        body, name=name, grid=(w.shape[0] // tm, w.shape[1] // tn), in_specs=[spec] * 4 + [ANY] * n_after,
        out_specs=[spec] * 4, out_shape=[out, out, out, out],
        compiler_params=_params(_mm_vmem([((tm, tn), F32, 24)])),
    )(w, g, m, v, *after)


def _pair_add(name, cidx, g_full, r_sib, kind):
    hr, hc = r_sib.shape
    tm, tn = _ew_tiles((hr, hc), 2 * EW_BLOCK_ELEMS)
    nbi, nbj = hr // tm, hc // tn
    if kind == "col":
        g_spec = pl.BlockSpec((tm, tn), lambda i, j, c: (c[0] * nbi + i, j))
    else:
        g_spec = pl.BlockSpec((tm, tn), lambda i, j, c: (i, c[0] * nbj + j))
    spec = pl.BlockSpec((tm, tn), lambda i, j, c: (i, j))

    def body(c_ref, g_ref, r_ref, o_ref):
        del c_ref
        o_ref[...] = (g_ref[...].astype(F32) + r_ref[...].astype(F32)).astype(BF16)

    return _pallas(
        body,
        name=name,
        grid_spec=pltpu.PrefetchScalarGridSpec(num_scalar_prefetch=1, grid=(nbi, nbj), in_specs=[g_spec, spec], out_specs=spec),
        out_shape=jax.ShapeDtypeStruct((hr, hc), BF16),
        compiler_params=_params(_mm_vmem([((tm, tn), BF16, 6), ((tm, tn), F32, 3)])),
    )(cidx, g_full, r_sib)


def _chip_sum(name, qidx, c_half, r_ici, kind):
    _, pr, pc = r_ici.shape
    tm, tn = _ew_tiles((pr, pc), 2 * EW_BLOCK_ELEMS)
    nbi, nbj = pr // tm, pc // tn
    if kind == "col":
        own_spec = pl.BlockSpec((tm, tn), lambda i, j, q: (i, q[0] * nbj + j))
        full, out_spec = (2 * pr, pc), pl.BlockSpec((tm, tn), lambda i, j, q: (q[1] * nbi + i, j))
    else:
        own_spec = pl.BlockSpec((tm, tn), lambda i, j, q: (q[0] * nbi + i, j))
        full, out_spec = (pr, 2 * pc), pl.BlockSpec((tm, tn), lambda i, j, q: (i, q[1] * nbj + j))

    def body(q_ref, own_ref, r_ref, o_ref):
        q = q_ref[0]
        own = own_ref[...].astype(F32)
        recv = [r_ref[r].astype(F32) for r in range(3)]
        total = None
        for chip in range(N_CHIPS):
            d = chip ^ q
            term = jnp.where(d == 0, own, jnp.where(d == 2, recv[0], jnp.where(d == 1, recv[1], recv[2])))
            total = term if total is None else total + term
        o_ref[...] = total

    return _pallas(
        body,
        name=name,
        grid_spec=pltpu.PrefetchScalarGridSpec(
            num_scalar_prefetch=1,
            grid=(nbi, nbj),
            in_specs=[own_spec, pl.BlockSpec((3, tm, tn), lambda i, j, q: (0, i, j))],
            out_specs=out_spec,
        ),
        out_shape=jax.ShapeDtypeStruct(full, F32),
        compiler_params=_params(_mm_vmem([((tm, tn), BF16, 8), ((tm, tn), F32, 6)])),
    )(qidx, c_half, r_ici)


_REL_MASK = (2, 1, 3)


def _place():
    x, y, c = lax.axis_index("x"), lax.axis_index("y"), lax.axis_index("c")
    chips = [(1 - x, y), (x, 1 - y), (1 - x, 1 - y)]
    return x, y, c, 2 * x + y, chips


def _shard_view(ref, kind, chip):
    if kind == "col":
        w = ref.shape[1] // N_CHIPS
        return ref.at[:, pl.ds(pl.multiple_of(chip * w, LANES), w)]
    h = ref.shape[0] // N_CHIPS
    return ref.at[pl.ds(pl.multiple_of(chip * h, 16), h), :]


def _row_half(ref, half):
    h = ref.shape[0] // 2
    return ref.at[pl.ds(pl.multiple_of(half * h, 16), h), :]


def _pair_half(ref, kind, half):
    if kind == "col":
        return _row_half(ref, half)
    w = ref.shape[1] // 2
    return ref.at[:, pl.ds(pl.multiple_of(half * w, LANES), w)]


def _remote(src, dst, send_sem, recv_sem, dev):
    return pltpu.make_async_remote_copy(src_ref=src, dst_ref=dst, send_sem=send_sem, recv_sem=recv_sem, device_id=dev, device_id_type=MESH)


def _hbm(a):
    return pltpu.with_memory_space_constraint(a, pltpu.HBM)


def _gather_start(name, fulls, kinds):
    n_w = len(fulls)

    def body(*refs):
        g = refs[:n_w]
        send_sem, recv_sem = refs[n_w], refs[n_w + 1]
        token = refs[-1]
        _, _, c, q, chips = _place()
        for w in range(n_w):
            mine = _row_half(_shard_view(g[w], kinds[w], q), c)
            for r, chip in enumerate(chips):
                _remote(mine, mine, send_sem.at[3 * w + r], recv_sem.at[3 * w + r], (*chip, c)).start()
        token[...] = jnp.zeros_like(token)

    res = _pallas(
        body,
        name=name,
        out_shape=(
            pltpu.SemaphoreType.DMA((3 * n_w,)),
            pltpu.SemaphoreType.DMA((3 * n_w,)),
            *[pltpu.HBM(f.shape, f.dtype) for f in fulls],
            jax.ShapeDtypeStruct((8, LANES), F32),
        ),
        in_specs=[HBM_SPEC] * n_w,
        out_specs=(SEM_SPEC, SEM_SPEC, *[HBM_SPEC] * n_w, pl.BlockSpec(memory_space=pltpu.VMEM)),
        input_output_aliases={w: w + 2 for w in range(n_w)},
        compiler_params=pltpu.CompilerParams(has_side_effects=EFFECT),
    )(*[_hbm(f) for f in fulls])
    return res[0], res[1], list(res[2 : 2 + n_w]), res[-1]


def _gather_wait(name, fulls, kinds, w_ids, send_sem, recv_sem, after):
    n = len(fulls)

    def body(*refs):
        g = refs[:n]
        s_sem, r_sem = refs[n], refs[n + 1]
        x, y, c, q, _ = _place()
        for i, w in enumerate(w_ids):
            mine = _row_half(_shard_view(g[i], kinds[i], q), c)
            for r in range(3):
                landed = _row_half(_shard_view(g[i], kinds[i], q ^ _REL_MASK[r]), c)
                cp = _remote(mine, landed, s_sem.at[3 * w + r], r_sem.at[3 * w + r], (x, y, 1 - c))
                cp.wait_send()
                cp.wait_recv()

    res = _pallas(
        body,
        name=name,
        out_shape=[pltpu.HBM(f.shape, f.dtype) for f in fulls],
        in_specs=[HBM_SPEC] * n + [SEM_SPEC, SEM_SPEC, ANY],
        out_specs=[HBM_SPEC] * n,
        input_output_aliases={i: i for i in range(n)},
        compiler_params=pltpu.CompilerParams(has_side_effects=EFFECT),
    )(*fulls, send_sem, recv_sem, after)
    return list(res)


def _gather_forward(name, fulls, kinds):
    n = len(fulls)

    def body(*refs):
        g = refs[n : 2 * n]
        send, recv = refs[2 * n :]
        x, y, c, q, _ = _place()
        sib = (x, y, 1 - c)
        cps = []
        for i in range(n):
            for r in range(3):
                landed = _row_half(_shard_view(g[i], kinds[i], q ^ _REL_MASK[r]), c)
                cps.append(_remote(landed, landed, send.at[i, r], recv.at[i, r], sib))
        for cp in cps:
            cp.start()
        for i in range(n):
            for r in range(3):
                other = _row_half(_shard_view(g[i], kinds[i], q ^ _REL_MASK[r]), 1 - c)
                _remote(other, other, send.at[i, r], recv.at[i, r], sib).wait_recv()
        for cp in cps:
            cp.wait_send()

    res = _pallas(
        body,
        name=name,
        in_specs=[ANY] * n,
        out_specs=[ANY] * n,
        out_shape=[jax.ShapeDtypeStruct(f.shape, f.dtype) for f in fulls],
        scratch_shapes=[pltpu.SemaphoreType.DMA((n, 3)), pltpu.SemaphoreType.DMA((n, 3))],
        input_output_aliases={i: i for i in range(n)},
    )(*fulls)
    return list(res)


def _split_start(name, bufs, n_sems, copies):
    n = len(bufs)

    def body(*refs):
        for cp in copies(refs[:n], refs[n], refs[n + 1]):
            cp.start()

    res = _pallas(
        body,
        name=name,
        out_shape=(
            pltpu.SemaphoreType.DMA((n_sems,)),
            pltpu.SemaphoreType.DMA((n_sems,)),
            *[pltpu.HBM(b.shape, b.dtype) for b in bufs],
        ),
        in_specs=[HBM_SPEC] * n,
        out_specs=(SEM_SPEC, SEM_SPEC, *[HBM_SPEC] * n),
        input_output_aliases={i: i + 2 for i in range(n)},
        compiler_params=pltpu.CompilerParams(has_side_effects=EFFECT),
    )(*[_hbm(b) for b in bufs])
    return res[0], res[1], list(res[2:])


def _split_wait(name, bufs, send_sem, recv_sem, copies, after):
    n = len(bufs)

    def body(*refs):
        for cp in copies(refs[:n], refs[n], refs[n + 1]):
            cp.wait_send()
            cp.wait_recv()

    res = _pallas(
        body,
        name=name,
        out_shape=[pltpu.HBM(b.shape, b.dtype) for b in bufs],
        in_specs=[HBM_SPEC] * n + [SEM_SPEC, SEM_SPEC, ANY],
        out_specs=[HBM_SPEC] * n,
        input_output_aliases={i: i for i in range(n)},
        compiler_params=pltpu.CompilerParams(has_side_effects=EFFECT),
    )(*bufs, send_sem, recv_sem, after)
    return list(res)


def _pair_exchange_copies(kinds):
    n = len(kinds)

    def copies(refs, send_sem, recv_sem):
        x, y, c, _, _ = _place()
        return [
            _remote(_pair_half(refs[w], kinds[w], 1 - c), refs[n + w], send_sem.at[w], recv_sem.at[w], (x, y, 1 - c))
            for w in range(n)
        ]

    return copies


def _pair_share_copies(kinds, waiting):
    def copies(refs, send_sem, recv_sem):
        x, y, c, _, _ = _place()
        out = []
        for w, kind in enumerate(kinds):
            mine = _pair_half(refs[w], kind, c)
            dst = _pair_half(refs[w], kind, 1 - c) if waiting else mine
            out.append(_remote(mine, dst, send_sem.at[w], recv_sem.at[w], (x, y, 1 - c)))
        return out

    return copies


def _piece_shape(half_shape, kind):
    r, c = half_shape
    return (3, r, c // N_CHIPS) if kind == "col" else (3, r // N_CHIPS, c)


def _chip_send_start(name, halves, kinds):
    n = len(halves)
    lands = [lax.empty(_piece_shape(h.shape, k), BF16) for h, k in zip(halves, kinds)]

    def body(*refs):
        h, land = refs[:n], refs[n : 2 * n]
        send_sem, recv_sem = refs[2 * n], refs[2 * n + 1]
        _, _, c, q, chips = _place()
        for i in range(n):
            for r, chip in enumerate(chips):
                piece = _shard_view(h[i], kinds[i], q ^ _REL_MASK[r])
                _remote(piece, land[i].at[r], send_sem.at[3 * i + r], recv_sem.at[3 * i + r], (*chip, c)).start()

    res = _pallas(
        body,
        name=name,
        out_shape=(
            pltpu.SemaphoreType.DMA((3 * n,)),
            pltpu.SemaphoreType.DMA((3 * n,)),
            *[pltpu.HBM(a.shape, a.dtype) for a in halves],
            *[pltpu.HBM(a.shape, a.dtype) for a in lands],
        ),
        in_specs=[HBM_SPEC] * (2 * n),
        out_specs=(SEM_SPEC, SEM_SPEC, *[HBM_SPEC] * (2 * n)),
        input_output_aliases={i: i + 2 for i in range(2 * n)},
        compiler_params=pltpu.CompilerParams(has_side_effects=EFFECT),
    )(*[_hbm(a) for a in halves], *[_hbm(a) for a in lands])
    return res[0], res[1], list(res[2 : 2 + n]), list(res[2 + n :])


def _chip_send_wait(name, halves, lands, kinds, send_sem, recv_sem, after):
    n = len(halves)

    def body(*refs):
        h, land = refs[:n], refs[n : 2 * n]
        s_sem, r_sem = refs[2 * n], refs[2 * n + 1]
        x, y, c, q, _ = _place()
        for i in range(n):
            for r in range(3):
                piece = _shard_view(h[i], kinds[i], q ^ _REL_MASK[r])
                cp = _remote(piece, land[i].at[r], s_sem.at[3 * i + r], r_sem.at[3 * i + r], (x, y, 1 - c))
                cp.wait_send()
                cp.wait_recv()

    res = _pallas(
        body,
        name=name,
        out_shape=[pltpu.HBM(a.shape, a.dtype) for a in halves] + [pltpu.HBM(a.shape, a.dtype) for a in lands],
        in_specs=[HBM_SPEC] * (2 * n) + [SEM_SPEC, SEM_SPEC, ANY],
        out_specs=[HBM_SPEC] * (2 * n),
        input_output_aliases={i: i for i in range(2 * n)},
        compiler_params=pltpu.CompilerParams(has_side_effects=EFFECT),
    )(*halves, *lands, send_sem, recv_sem, after)
    return list(res[:n]), list(res[n:])


def _small_exchange_copies(waiting):
    def copies(refs, send_sem, recv_sem):
        p, land = refs
        x, y, c, q, _ = _place()
        me = 2 * q + c
        out = []
        for dd in range(1, 2 * N_CHIPS):
            dev = (x ^ ((dd >> 2) & 1), y ^ ((dd >> 1) & 1), c ^ (dd & 1))
            dst = land.at[me ^ dd] if waiting else land.at[me]
            out.append(_remote(p, dst, send_sem.at[dd - 1], recv_sem.at[dd - 1], dev))
        return out

    return copies


def _small_sum(name, me_idx, p, land):
    rows = p.shape[0]
    n_dev = 2 * N_CHIPS

    def body(me_ref, p_ref, land_ref, o_ref):
        me = me_ref[0]
        total = None
        for dev in range(n_dev):
            term = jnp.where(me == dev, p_ref[...], land_ref[dev])
            total = term if total is None else total + term
        o_ref[...] = total

    return _pallas(
        body,
        name=name,
        grid_spec=pltpu.PrefetchScalarGridSpec(
            num_scalar_prefetch=1,
            grid=(1,),
            in_specs=[pl.BlockSpec((rows, LANES), lambda i, m: (0, 0)), pl.BlockSpec((n_dev, rows, LANES), lambda i, m: (0, 0, 0))],
            out_specs=pl.BlockSpec((rows, LANES), lambda i, m: (0, 0)),
        ),
        out_shape=jax.ShapeDtypeStruct(p.shape, F32),
    )(me_idx, p, land)


def _pack(parts):
    rows = []
    for a in parts:
        flat = a.reshape(-1).astype(F32)
        n = flat.shape[0]
        padded = -(-n // (8 * LANES)) * (8 * LANES)
        rows.append(jnp.pad(flat, (0, padded - n)).reshape(-1, LANES))
    return jnp.concatenate(rows, axis=0)


def _unpack(packed, shapes):
    out, row = [], 0
    for shp in shapes:
        n = int(np.prod(shp))
        nrows = -(-n // (8 * LANES)) * 8
        out.append(packed[row : row + nrows].reshape(-1)[:n].reshape(shp))
        row += nrows
    return out


def kernel(x, w_in, norm_mix, sgu_v_gain, sgu_w_s, sgu_b_s, w_a_out, attn_sink, rel_bias, w_b_out, w_o, norm_ffn, w_gate, w_up, w_down, norm_final, loss_target, m_w_in, m_norm_mix, m_sgu_v_gain, m_sgu_w_s, m_sgu_b_s, m_w_a_out, m_attn_sink, m_rel_bias, m_w_b_out, m_w_o, m_norm_ffn, m_w_gate, m_w_up, m_w_down, m_norm_final, v_w_in, v_norm_mix, v_sgu_v_gain, v_sgu_w_s, v_sgu_b_s, v_w_a_out, v_attn_sink, v_rel_bias, v_w_b_out, v_w_o, v_norm_ffn, v_w_gate, v_w_up, v_w_down, v_norm_final):
    s, d = x.shape[1], x.shape[2]
    w_sgu = sgu_v_gain.shape[1]
    groups = sgu_w_s.shape[1]
    heads = attn_sink.shape[1]
    grp = heads // N_KV_HEADS
    w_att = heads * HEAD_DIM
    w_kv = N_KV_HEADS * HEAD_DIM
    d_ff = w_gate.shape[2] * N_CHIPS
    n_in = w_in.shape[2] * N_CHIPS
    off_q = 2 * w_sgu
    off_k = off_q + w_att
    off_g = off_k + 2 * w_kv
    assert n_in == off_g + 2 * d and groups * BLK == w_sgu and s % BLK == 0

    x2d = x.reshape(s, d)
    tgt = loss_target.reshape(s, d)
    c_idx = lax.axis_index("c").astype(I32).reshape(1)
    q_idx = (2 * lax.axis_index("x") + lax.axis_index("y")).astype(I32).reshape(1)
    qc_idx = jnp.concatenate([q_idx, c_idx])

    W_IN, W_A, W_B, W_O, W_GATE, W_UP, W_DOWN = range(7)
    names = ["w_in", "w_a", "w_b", "w_o", "w_gate", "w_up", "w_down"]
    kinds = ["col", "col", "col", "row", "col", "col", "row"]
    big_w = [w_in[0], w_a_out[0], w_b_out[0], w_o[0], w_gate[0], w_up[0], w_down[0]]
    big_m = [m_w_in[0], m_w_a_out[0], m_w_b_out[0], m_w_o[0], m_w_gate[0], m_w_up[0], m_w_down[0]]
    big_v = [v_w_in[0], v_w_a_out[0], v_w_b_out[0], v_w_o[0], v_w_gate[0], v_w_up[0], v_w_down[0]]
    full_in = _cast_into_full("cast_w_in", q_idx, big_w[W_IN], kinds[W_IN])
    in_send, in_recv, (full_in,), token = _gather_start("gather_start_in", [full_in], [kinds[W_IN]])
    rest = [_cast_into_full("cast_" + names[i], q_idx, big_w[i], kinds[i], after=(token,)) for i in range(1, 7)]
    ag_send, ag_recv, rest, token = _gather_start("gather_start_rest", rest, kinds[1:])
    fulls = [full_in] + rest

    def gathered(tag, ids, after):
        if ids == [W_IN]:
            sems, pos = (in_send, in_recv), [0]
        else:
            sems, pos = (ag_send, ag_recv), [i - 1 for i in ids]
        got = _gather_wait("gather_wait_" + tag, [fulls[i] for i in ids], [kinds[i] for i in ids], pos, *sems, after)
        return _gather_forward("gather_fwd_" + tag, got, [kinds[i] for i in ids])

    ws_b = sgu_w_s[0].astype(BF16)
    wst_b = jnp.swapaxes(sgu_w_s[0], 1, 2).astype(BF16)
    b_col = sgu_b_s[0].reshape(groups, BLK, 1)
    bias_tab, bucket = _band_tables(rel_bias)
    sink = attn_sink[0]

    tm = _tile(s, (1024, 512, 256, 128))

    h1 = _rms_fwd("rms_mix", x2d, norm_mix, after=(token,))
    (g_in,) = gathered("in", [W_IN], h1)

    tn = _tile(n_in, (768, 640, 512))
    z = _mm(
        "mm_z", (s // tm, n_in // tn, 1), [h1, g_in],
        [pl.BlockSpec((tm, d), lambda i, j, k: (i, 0)), pl.BlockSpec((d, tn), lambda i, j, k: (0, j))],
        [jax.ShapeDtypeStruct((s, n_in), BF16)], [pl.BlockSpec((tm, tn), lambda i, j, k: (i, j))],
        [(0, 1, NN, 0)], 1, (tm, tn), 1, lambda ins, vals, outs, cs: _put(outs[0], cs, vals[0]),
        _mm_vmem([((tm, d), BF16, 2), ((d, tn), BF16, 2), ((tm, tn), F32, 3)]),
    )[0]
    g_a, g_b, g_o = gathered("mix", [W_A, W_B, W_O], z)

    a_act = _sgu_fwd(z, sgu_v_gain, ws_b, b_col, w_sgu)

    kv_b = z[:, off_k:off_g]
    k_pad = jnp.pad(kv_b[:, :w_kv], ((BLK, BLK), (0, 0)))
    v_pad = jnp.pad(kv_b[:, w_kv:], ((BLK, BLK), (0, 0)))
    q_blk0 = off_q // (grp * HEAD_DIM)
    att = _attn_fwd(sink, z, k_pad, v_pad, bias_tab, grp, q_blk0)

    tg = _tile(d, (512,))
    ga0, gb0 = off_g // tg, (off_g + d) // tg

    def ep_gate(ins, vals, outs, cs):
        sa, sb = _sigmoid(ins[4][:, cs].astype(F32)), _sigmoid(ins[5][:, cs].astype(F32))
        _put(outs[0], cs, sa * vals[0] + sb * vals[1])
        _put(outs[1], cs, vals[0])
        _put(outs[2], cs, vals[1])

    t_out = pl.BlockSpec((tm, tg), lambda i, j, k: (i, j))
    m_act, y_a, y_b = _mm(
        "mm_branches", (s // tm, d // tg, 1), [a_act, g_a, att, g_b, z, z],
        [pl.BlockSpec((tm, w_sgu), lambda i, j, k: (i, 0)), pl.BlockSpec((w_sgu, tg), lambda i, j, k: (0, j)),
         pl.BlockSpec((tm, w_att), lambda i, j, k: (i, 0)), pl.BlockSpec((w_att, tg), lambda i, j, k: (0, j)),
         pl.BlockSpec((tm, tg), lambda i, j, k: (i, ga0 + j)), pl.BlockSpec((tm, tg), lambda i, j, k: (i, gb0 + j))],
        [jax.ShapeDtypeStruct((s, d), BF16)] * 3,
        [t_out, t_out, t_out], [(0, 1, NN, 0), (2, 3, NN, 1)], 2, (tm, tg), 1, ep_gate,
        _mm_vmem([((tm, w_sgu), BF16, 4), ((w_sgu, tg), BF16, 4), ((tm, tg), F32, 12)]),    )

    tn = _tile(d, (1024, 512))

    def ep_residual(ins, vals, outs, cs):
        _put(outs[0], cs, ins[2][:, cs] + vals[0])

    x2 = _mm(
        "mm_wo", (s // tm, d // tn, 1), [m_act, g_o, x2d],
        [pl.BlockSpec((tm, d), lambda i, j, k: (i, 0)), pl.BlockSpec((d, tn), lambda i, j, k: (0, j)),
         pl.BlockSpec((tm, tn), lambda i, j, k: (i, j))],
        [jax.ShapeDtypeStruct((s, d), F32)], [pl.BlockSpec((tm, tn), lambda i, j, k: (i, j))],
        [(0, 1, NN, 0)], 1, (tm, tn), 1, ep_residual,
        _mm_vmem([((tm, d), BF16, 2), ((d, tn), BF16, 2), ((tm, tn), F32, 5)]),    )[0]

    h2 = _rms_fwd("rms_ffn", x2, norm_ffn)
    (g_gate,) = gathered("gate", [W_GATE], h2)
    (g_up,) = gathered("up", [W_UP], g_gate)

    tf = _tile(d_ff, (512,))

    def ep_swiglu(ins, vals, outs, cs):
        gt, up = vals
        _put(outs[0], cs, gt)
        _put(outs[1], cs, up)
        _put(outs[2], cs, (gt * _sigmoid(gt)) * up)

    f_out = pl.BlockSpec((tm, tf), lambda i, j, k: (i, j))
    gt, up, f_act = _mm(
        "mm_gate_up", (s // tm, d_ff // tf, 1), [h2, g_gate, g_up],
        [pl.BlockSpec((tm, d), lambda i, j, k: (i, 0)), pl.BlockSpec((d, tf), lambda i, j, k: (0, j)),
         pl.BlockSpec((d, tf), lambda i, j, k: (0, j))],
        [jax.ShapeDtypeStruct((s, d_ff), BF16)] * 3,
        [f_out, f_out, f_out], [(0, 1, NN, 0), (0, 2, NN, 1)], 2, (tm, tf), 1, ep_swiglu,
        _mm_vmem([((tm, d), BF16, 2), ((d, tf), BF16, 4), ((tm, tf), F32, 8)]),    )
    (g_down,) = gathered("ffn_out", [W_DOWN], f_act)

    tkf = _tile(d_ff, (1408, 1024, 512))
    tml, tnl = _tile(s, (512, 256, 128)), _tile(d, (512,))
    x3 = _mm(
        "mm_down", (s // tml, d // tnl, 1), [f_act, g_down, x2],
        [pl.BlockSpec((tml, d_ff), lambda i, j, k: (i, 0)), pl.BlockSpec((d_ff, tnl), lambda i, j, k: (0, j)),
         pl.BlockSpec((tml, tnl), lambda i, j, k: (i, j))],
        [jax.ShapeDtypeStruct((s, d), F32)], [pl.BlockSpec((tml, tnl), lambda i, j, k: (i, j))],
        [(0, 1, NN, 0)], 1, (tml, tnl), 1, ep_residual,
        _mm_vmem([((tml, d_ff), BF16, 2), ((d_ff, tnl), BF16, 2), ((tml, tnl), F32, 6)]),    )[0]

    dx3, dx3b, dg_final, loss_part = _head(x3, norm_final.reshape(1, d), tgt)

    def reduce_a(tag, ids, grads):
        ks = [kinds[i] for i in ids]
        lands = [lax.empty((g.shape[0] // 2, g.shape[1]) if k == "col" else (g.shape[0], g.shape[1] // 2), BF16)
                 for g, k in zip(grads, ks)]
        send, recv, bufs = _split_start("pair_send_" + tag, list(grads) + lands, len(ids), _pair_exchange_copies(ks))
        return {"tag": tag, "ids": ids, "ks": ks, "pair": (send, recv, bufs), "token": bufs[0]}

    def reduce_b(st, after):
        tag, ids, ks = st["tag"], st["ids"], st["ks"]
        send, recv, bufs = st["pair"]
        bufs = _split_wait("pair_wait_" + tag, bufs, send, recv, _pair_exchange_copies(ks), after)
        grads, from_sib = bufs[: len(ids)], bufs[len(ids) :]
        halves = [_pair_add("pair_add_" + names[i], c_idx, g, r, k) for i, g, r, k in zip(ids, grads, from_sib, ks)]
        st["chip"] = _chip_send_start("chip_send_" + tag, halves, ks)
        st["token"] = st["chip"][2][0]

    def reduce_c(st, after):
        tag, ids, ks = st["tag"], st["ids"], st["ks"]
        send, recv, halves, lands = st["chip"]
        halves, lands = _chip_send_wait("chip_wait_" + tag, halves, lands, ks, send, recv, after)
        pieces = [_chip_sum("chip_sum_" + names[i], qc_idx, h, r, k) for i, h, r, k in zip(ids, halves, lands, ks)]
        st["share"] = _split_start("share_send_" + tag, pieces, len(ids), _pair_share_copies(ks, False))
        st["token"] = st["share"][2][0]

    def reduce_d(st, after):
        send, recv, bufs = st["share"]
        return _split_wait("share_wait_" + st["tag"], bufs, send, recv, _pair_share_copies(st["ks"], True), after)

    grads_big, upd = [None] * 7, [None] * 7

    def finish(st, after):
        shared = reduce_d(st, after)
        after = shared[0]
        for i, g in zip(st["ids"], shared):
            upd[i] = _adamw("adamw_" + names[i], big_w[i], g, big_m[i], big_v[i], after=(after,))
            grads_big[i] = upd[i][3]
            after = upd[i][0]
        return after

    def ep_swiglu_bwd(ins, vals, outs, cs):
        df = vals[0]
        gtv, upv = ins[2][:, cs].astype(F32), ins[3][:, cs].astype(F32)
        sg = _sigmoid(gtv)
        _put(outs[0], cs, df * upv * (sg + gtv * sg * (1.0 - sg)))
        _put(outs[1], cs, df * (gtv * sg))

    dgt, dup = _mm(
        "mm_dswiglu", (s // tm, d_ff // tf, 1), [dx3b, g_down, gt, up],
        [pl.BlockSpec((tm, d), lambda i, j, k: (i, 0)), pl.BlockSpec((tf, d), lambda i, j, k: (j, 0)), f_out, f_out],
        [jax.ShapeDtypeStruct((s, d_ff), BF16), jax.ShapeDtypeStruct((s, d_ff), BF16)], [f_out, f_out],
        [(0, 1, NT, 0)], 1, (tm, tf), 1, ep_swiglu_bwd,
        _mm_vmem([((tm, d), BF16, 2), ((tf, d), BF16, 2), ((tm, tf), F32, 8)]),    )

    def ep_store(ins, vals, outs, cs):
        for o, v in zip(outs, vals):
            _put(o, cs, v)

    twn = _tile(d, (1024, 512))
    gw_down = _mm(
        "mm_gw_down", (d_ff // tkf, d // twn, 1), [f_act, dx3b],
        [pl.BlockSpec((s, tkf), lambda i, j, k: (0, i)), pl.BlockSpec((s, twn), lambda i, j, k: (0, j))],
        [jax.ShapeDtypeStruct((d_ff, d), BF16)], [pl.BlockSpec((tkf, twn), lambda i, j, k: (i, j))],
        [(0, 1, TN, 0)], 1, (tkf, twn), 1, ep_store,
        _mm_vmem([((s, tkf), BF16, 3), ((s, twn), BF16, 2), ((tkf, twn), F32, 3)]),
    )[0]
    red_down = reduce_a("down", [W_DOWN], [gw_down])

    tn2 = _tile(d, (256,))
    dh2_specs = [pl.BlockSpec((tm, d_ff), lambda i, j, k: (i, 0)), pl.BlockSpec((tn2, d_ff), lambda i, j, k: (j, 0))]
    dh2_tile = pl.BlockSpec((tm, tn2), lambda i, j, k: (i, j))
    dh2_vmem = _mm_vmem([((tm, d_ff), BF16, 2), ((tn2, d_ff), BF16, 2), ((tm, tn2), F32, 7)])
    dh2 = _mm(
        "mm_dh2_gate", (s // tm, d // tn2, 1), [dgt, g_gate], dh2_specs,
        [jax.ShapeDtypeStruct((s, d), F32)], [dh2_tile], [(0, 1, NT, 0)], 1, (tm, tn2), 1, ep_store, dh2_vmem,
        after=(red_down["token"],),
    )[0]
    dh2 = _mm(
        "mm_dh2_up", (s // tm, d // tn2, 1), [dup, g_up, dh2], dh2_specs + [dh2_tile],
        [jax.ShapeDtypeStruct((s, d), F32)], [dh2_tile], [(0, 1, NT, 0)], 1, (tm, tn2), 1, ep_residual, dh2_vmem,
    )[0]
    reduce_b(red_down, dh2)

    twr = _tile(d, (1024, 512))
    w_tile = pl.BlockSpec((twr, tf), lambda i, j, k: (i, j))
    gw_gate, gw_up = _mm(
        "mm_gw_gate_up", (d // twr, d_ff // tf, 1), [h2, dgt, dup],
        [pl.BlockSpec((s, twr), lambda i, j, k: (0, i)), pl.BlockSpec((s, tf), lambda i, j, k: (0, j)),
         pl.BlockSpec((s, tf), lambda i, j, k: (0, j))],
        [jax.ShapeDtypeStruct((d, d_ff), BF16), jax.ShapeDtypeStruct((d, d_ff), BF16)], [w_tile, w_tile],
        [(0, 1, TN, 0), (0, 2, TN, 1)], 2, (twr, tf), 1, ep_store,
        _mm_vmem([((s, twr), BF16, 3), ((s, tf), BF16, 4), ((twr, tf), F32, 6)]),
        after=(red_down["token"],),
    )
    red_ffn = reduce_a("ffn_in", [W_GATE, W_UP], [gw_gate, gw_up])

    dx2, dx2b, dg_ffn = _rms_bwd("rms_ffn_bwd", x2, norm_ffn, dh2, dx3, after=(red_ffn["token"],))

    nj = d // tg

    def lo(j):
        return jnp.minimum(j, nj - 1)

    def gate_bwd_body(dx_ref, wo_ref, ga_ref, gb_ref, ya_ref, yb_ref, dya_ref, dyb_ref, dz_ref, keep):
        j = pl.program_id(1)

        @pl.when(j < nj)
        def _():
            dm = lax.dot_general(dx_ref[...], wo_ref[...], NT, preferred_element_type=F32)
            sa, sb = _sigmoid(ga_ref[...].astype(F32)), _sigmoid(gb_ref[...].astype(F32))
            dya_ref[...] = (dm * sa).astype(BF16)
            dyb_ref[...] = (dm * sb).astype(BF16)
            dz_ref[...] = (dm * ya_ref[...].astype(F32) * (sa * (1.0 - sa))).astype(BF16)
            keep[lo(j)] = (dm * yb_ref[...].astype(F32) * (sb * (1.0 - sb))).astype(BF16)

        @pl.when(j >= nj)
        def _():
            dz_ref[...] = keep[jnp.maximum(j - nj, 0)]

    t_lo = pl.BlockSpec((tm, tg), lambda i, j: (i, lo(j)))
    dya, dyb, dz = _pallas(
        gate_bwd_body,
        name="mm_dgate",
        grid=(s // tm, 2 * nj),
        in_specs=[
            pl.BlockSpec((tm, d), lambda i, j: (i, 0)),
            pl.BlockSpec((tg, d), lambda i, j: (lo(j), 0)),
            pl.BlockSpec((tm, tg), lambda i, j: (i, ga0 + lo(j))),
            pl.BlockSpec((tm, tg), lambda i, j: (i, gb0 + lo(j))),
            t_lo,
            t_lo,
        ],
        out_specs=[t_lo, t_lo, pl.BlockSpec((tm, tg), lambda i, j: (i, ga0 + j))],
        out_shape=[jax.ShapeDtypeStruct((s, d), BF16), jax.ShapeDtypeStruct((s, d), BF16), jax.ShapeDtypeStruct((s, n_in), BF16)],
        scratch_shapes=[pltpu.VMEM((nj, tm, tg), BF16)],
        compiler_params=_params(_mm_vmem([((tm, d), BF16, 2), ((tg, d), BF16, 2), ((tm, tg), F32, 14), ((nj, tm, tg), BF16, 1)])),
    )(dx2b, g_o, z, z, y_a, y_b)
    reduce_b(red_ffn, dya)
    reduce_c(red_down, red_ffn["token"])

    gw_o = _mm(
        "mm_gw_o", (d // twr, d // twn, 1), [m_act, dx2b],
        [pl.BlockSpec((s, twr), lambda i, j, k: (0, i)), pl.BlockSpec((s, twn), lambda i, j, k: (0, j))],
        [jax.ShapeDtypeStruct((d, d), BF16)], [pl.BlockSpec((twr, twn), lambda i, j, k: (i, j))],
        [(0, 1, TN, 0)], 1, (twr, twn), 1, ep_store,
        _mm_vmem([((s, twr), BF16, 3), ((s, twn), BF16, 2), ((twr, twn), F32, 3)]),
        after=(red_down["token"],),
    )[0]
    after_down = finish(red_down, gw_o)

    tb = _tile(w_sgu, (1024, 512))
    b_out = pl.BlockSpec((tm, tb), lambda i, j, k: (i, j))

    da, datt = _mm(
        "mm_dbranches", (s // tm, w_sgu // tb, 1), [dya, g_a, dyb, g_b],
        [pl.BlockSpec((tm, d), lambda i, j, k: (i, 0)), pl.BlockSpec((tb, d), lambda i, j, k: (j, 0)),
         pl.BlockSpec((tm, d), lambda i, j, k: (i, 0)), pl.BlockSpec((tb, d), lambda i, j, k: (j, 0))],
        [jax.ShapeDtypeStruct((s, w_sgu), BF16), jax.ShapeDtypeStruct((s, w_att), BF16)], [b_out, b_out],
        [(0, 1, NT, 0), (2, 3, NT, 1)], 2, (tm, tb), 1, ep_store,
        _mm_vmem([((tm, d), BF16, 4), ((tb, d), BF16, 4), ((tm, tb), F32, 6)]),
        after=(after_down,),
    )

    wb_tile = pl.BlockSpec((tb, twn), lambda i, j, k: (i, j))
    gw_a, gw_b = _mm(
        "mm_gw_branches", (w_sgu // tb, d // twn, 1), [a_act, dya, att, dyb],
        [pl.BlockSpec((s, tb), lambda i, j, k: (0, i)), pl.BlockSpec((s, twn), lambda i, j, k: (0, j)),
         pl.BlockSpec((s, tb), lambda i, j, k: (0, i)), pl.BlockSpec((s, twn), lambda i, j, k: (0, j))],
        [jax.ShapeDtypeStruct((w_sgu, d), BF16), jax.ShapeDtypeStruct((w_att, d), BF16)], [wb_tile, wb_tile],
        [(0, 1, TN, 0), (2, 3, TN, 1)], 2, (tb, twn), 1, ep_store,
        _mm_vmem([((s, tb), BF16, 5), ((s, twn), BF16, 4), ((tb, twn), F32, 6)]),
        after=(da,),
    )
    red_mix = reduce_a("mix", [W_O, W_A, W_B], [gw_o, gw_a, gw_b])

    dz, dws, dbs, dgain = _sgu_bwd(z, da, sgu_v_gain, ws_b, wst_b, b_col, w_sgu, dz, after=(red_mix["token"],))
    dz, dk_pad, dv_pad, dbias_tab, dsink = _attn_bwd(sink, z, k_pad, v_pad, bias_tab, datt, dz, grp, q_blk0)
    dz = _dkv_to_dz(dk_pad, dv_pad, dz, off_k // (2 * w_kv))
    drel = _relbias_bwd(dbias_tab, bucket)
    reduce_b(red_mix, dz)
    reduce_c(red_ffn, red_mix["token"])

    small_w = [norm_mix, sgu_v_gain, sgu_w_s, sgu_b_s, attn_sink, rel_bias, norm_ffn, norm_final]
    small_m = [m_norm_mix, m_sgu_v_gain, m_sgu_w_s, m_sgu_b_s, m_attn_sink, m_rel_bias, m_norm_ffn, m_norm_final]
    small_v = [v_norm_mix, v_sgu_v_gain, v_sgu_w_s, v_sgu_b_s, v_attn_sink, v_rel_bias, v_norm_ffn, v_norm_final]
    small_shapes = [w.shape for w in small_w]
    early = [dgain, dws, dbs, dsink[:, 0], drel[:, :REL_BUCKETS].T, dg_ffn, dg_final]
    p_early = _pack([g.reshape(shp) for g, shp in zip(early, small_shapes[1:])] + [loss_part[0, :1]])
    land = jnp.zeros((2 * N_CHIPS,) + p_early.shape, F32)
    sm_send, sm_recv, (p_early, land) = _split_start("small_send", [p_early, land], 2 * N_CHIPS - 1, _small_exchange_copies(False))

    tzn = _tile(n_in, (768, 640, 512))
    gw_in = _mm(
        "mm_gw_in", (d // twr, n_in // tzn, 1), [h1, dz],
        [pl.BlockSpec((s, twr), lambda i, j, k: (0, i)), pl.BlockSpec((s, tzn), lambda i, j, k: (0, j))],
        [jax.ShapeDtypeStruct((d, n_in), BF16)], [pl.BlockSpec((twr, tzn), lambda i, j, k: (i, j))],
        [(0, 1, TN, 0)], 1, (twr, tzn), 1, ep_store,
        _mm_vmem([((s, twr), BF16, 3), ((s, tzn), BF16, 2), ((twr, tzn), F32, 3)]),
        after=(red_ffn["token"], p_early),
    )[0]
    red_in = reduce_a("w_in", [W_IN], [gw_in])

    reduce_c(red_mix, red_in["token"])
    reduce_b(red_in, finish(red_mix, red_in["token"]))

    dh1 = _mm(
        "mm_dh1", (s // tm, d // tn2, 1), [dz, g_in],
        [pl.BlockSpec((tm, n_in), lambda i, j, k: (i, 0)), pl.BlockSpec((tn2, n_in), lambda i, j, k: (j, 0))],
        [jax.ShapeDtypeStruct((s, d), F32)], [pl.BlockSpec((tm, tn2), lambda i, j, k: (i, j))],
        [(0, 1, NT, 0)], 1, (tm, tn2), 1, ep_store,
        _mm_vmem([((tm, n_in), BF16, 2), ((tn2, n_in), BF16, 2), ((tm, tn2), F32, 5)]),
        after=(red_in["token"],),
    )[0]

    grad_x, _, dg_mix = _rms_bwd("rms_mix_bwd", x2d, norm_mix, dh1, dx2)

    p_mix = _pack([dg_mix.reshape(small_shapes[0])])
    land_mix = jnp.zeros((2 * N_CHIPS,) + p_mix.shape, F32)
    mx_send, mx_recv, (p_mix, land_mix) = _split_start("mix_send", [p_mix, land_mix], 2 * N_CHIPS - 1, _small_exchange_copies(False))

    reduce_c(red_in, finish(red_ffn, p_mix))
    p_early, land = _split_wait("small_wait", [p_early, land], sm_send, sm_recv, _small_exchange_copies(True), red_in["token"])
    p_mix, land_mix = _split_wait("mix_wait", [p_mix, land_mix], mx_send, mx_recv, _small_exchange_copies(True), p_early)
    me_idx = 2 * q_idx + c_idx
    packed_g = jnp.concatenate([_small_sum("mix_sum", me_idx, p_mix, land_mix), _small_sum("small_sum", me_idx, p_early, land)], axis=0)
    g_small = _unpack(packed_g, small_shapes + [(1,)])
    loss = g_small[-1].reshape(())
    g_small = g_small[:-1]
    zero1 = jnp.zeros((1,), F32)
    pw, pg, pm, pv = _pack(small_w + [zero1]), _pack(g_small + [zero1]), _pack(small_m + [zero1]), _pack(small_v + [zero1])
    small_upd = _adamw("adamw_small", pw, pg, pm, pv)
    d_small, nm_small, nv_small = [_unpack(a, small_shapes) for a in small_upd[:3]]
    finish(red_in, small_upd[0])

    small_names = ["norm_mix", "sgu_v_gain", "sgu_w_s", "sgu_b_s", "attn_sink", "rel_bias", "norm_ffn", "norm_final"]
    table = {}
    for i, n in enumerate(names):
        table[n] = (grads_big[i][None], upd[i][0][None], upd[i][1][None], upd[i][2][None])
    for i, n in enumerate(small_names):
        table[n] = (g_small[i], d_small[i], nm_small[i], nv_small[i])
    order = ["w_in", "norm_mix", "sgu_v_gain", "sgu_w_s", "sgu_b_s", "w_a", "attn_sink", "rel_bias", "w_b", "w_o", "norm_ffn",
             "w_gate", "w_up", "w_down", "norm_final"]
    outs = [loss, grad_x.reshape(1, s, d)]
    for part in range(4):
        outs += [table[n][part] for n in order]
    return tuple(outs)
```

```python
import math

import jax
import jax.numpy as jnp
import numpy as np
from jax import lax
from jax.experimental import pallas as pl
from jax.experimental.pallas import tpu as pltpu

F32 = jnp.float32
BF16 = jnp.bfloat16
I32 = jnp.int32
MESH = pl.DeviceIdType.MESH

EPS = 1e-6
NEG = -1e30
BLK = 128
HEAD_DIM = 128
N_KV_HEADS = 2
REL_BUCKETS = 32
REL_MAX_DIST = 128
N_CHIPS = 4
ADAM_LR, ADAM_B1, ADAM_B2, ADAM_EPS, ADAM_WD, ADAM_STEP = 0.001, 0.9, 0.999, 1e-08, 0.01, 10

LANES = 128
VMEM_CAP = 60 * 1024 * 1024

NN = (((1,), (0,)), ((), ()))
NT = (((1,), (1,)), ((), ()))
TN = (((0,), (0,)), ((), ()))
ANY = pl.BlockSpec(memory_space=pl.ANY)
HBM_SPEC = pl.BlockSpec(memory_space=pltpu.HBM)
SEM_SPEC = pl.BlockSpec(memory_space=pltpu.SEMAPHORE)
EFFECT = pltpu.SideEffectType.DATAFLOW_SIDE_EFFECTING


def _tile(n, cands):
    for t in cands:
        if n % t == 0:
            return t
    return n


PIN_BYTES = 64 * 1024


def _pin_hbm(a):
    big = hasattr(a, "dtype") and jnp.issubdtype(a.dtype, jnp.floating) and _nbytes(a.shape, a.dtype) >= PIN_BYTES
    return pltpu.with_memory_space_constraint(a, pltpu.HBM) if big else a


def _pallas(body, *, out_shape, **kw):
    def pin(o):
        big = isinstance(o, jax.ShapeDtypeStruct) and jnp.issubdtype(o.dtype, jnp.floating) and _nbytes(o.shape, o.dtype) >= PIN_BYTES
        return pltpu.HBM(o.shape, o.dtype) if big else o

    shapes = type(out_shape)(pin(o) for o in out_shape) if isinstance(out_shape, (list, tuple)) else pin(out_shape)
    call = pl.pallas_call(body, out_shape=shapes, **kw)
    return lambda *args: call(*[_pin_hbm(a) for a in args])


def _params(vmem_bytes=None, **kw):
    if vmem_bytes is not None:
        kw["vmem_limit_bytes"] = int(min(max(vmem_bytes, 32 * 1024 * 1024), VMEM_CAP))
    return pltpu.CompilerParams(**kw)


def _nbytes(shape, dtype):
    return int(np.prod(shape)) * jnp.dtype(dtype).itemsize


def _sigmoid(x):
    return 1.0 / (1.0 + jnp.exp(-x))


_GC = 0.7978845608028654
_GA = 0.044715


def _gelu(x):
    return 0.5 * x * (1.0 + jnp.tanh(_GC * (x + _GA * (x * x * x))))


def _gelu_grad(x):
    t = jnp.tanh(_GC * (x + _GA * (x * x * x)))
    return 0.5 * (1.0 + t) + 0.5 * x * (1.0 - t * t) * (_GC * (1.0 + 3.0 * _GA * (x * x)))


def _bf(v):
    return v if v.dtype == BF16 else v.astype(BF16)


def _mm(name, grid, ins, in_specs, out_shape, out_specs, pairs, n_acc, tile, nk, epilogue, vmem_bytes, after=()):
    assert nk == 1
    n_in, n_out = len(ins) + len(after), len(out_shape)

    def body(*refs):
        in_refs, out_refs = refs[:n_in], refs[n_in : n_in + n_out]
        vals = [None] * n_acc
        for a_i, b_i, dn, acc_i in pairs:
            d = lax.dot_general(_bf(in_refs[a_i][...]), _bf(in_refs[b_i][...]), dn, preferred_element_type=F32)
            vals[acc_i] = d if vals[acc_i] is None else vals[acc_i] + d
        epilogue(in_refs, vals, out_refs, slice(None))

    return _pallas(
        body,
        name=name,
        grid=grid,
        in_specs=list(in_specs) + [ANY] * len(after),
        out_specs=out_specs,
        out_shape=out_shape,
        compiler_params=_params(vmem_bytes),
    )(*ins, *after)


def _put(ref, cs, v):
    ref[:, cs] = v.astype(ref.dtype)


def _mm_vmem(tiles):
    return sum(_nbytes(s, d) * c for s, d, c in tiles) + 4 * 1024 * 1024


def _rows8(v):
    r, d = v.shape
    return v.reshape(r // 8, 8, d).sum(axis=0)


def _rms_fwd(name, x, g, after=()):
    s, d = x.shape
    tm = _tile(s, (256, 128))

    def body(x_ref, g_ref, *rest):
        h_ref = rest[-1]
        xv = x_ref[...]
        r = lax.rsqrt(jnp.mean(xv * xv, axis=-1, keepdims=True) + EPS)
        h_ref[...] = ((xv * r) * g_ref[...]).astype(BF16)

    return _pallas(
        body,
        name=name,
        grid=(s // tm,),
        in_specs=[pl.BlockSpec((tm, d), lambda i: (i, 0)), pl.BlockSpec((1, d), lambda i: (0, 0))] + [ANY] * len(after),
        out_specs=pl.BlockSpec((tm, d), lambda i: (i, 0)),
        out_shape=jax.ShapeDtypeStruct((s, d), BF16),
    )(x, g, *after)


def _rms_bwd(name, x, g, dh, dres, after=()):
    s, d = x.shape
    tm = _tile(s, (256, 128))
    n = s // tm
    n_after = len(after)

    def body(x_ref, g_ref, dh_ref, dres_ref, *rest):
        dx_ref, dxb_ref, dg_ref, acc_ref = rest[n_after:]
        i = pl.program_id(0)
        xv = x_ref[...]
        r = lax.rsqrt(jnp.mean(xv * xv, axis=-1, keepdims=True) + EPS)
        xh = xv * r
        dhv = dh_ref[...]
        dxh = dhv * g_ref[...]
        dx = r * (dxh - xh * jnp.mean(dxh * xh, axis=-1, keepdims=True)) + dres_ref[...]
        dx_ref[...] = dx
        dxb_ref[...] = dx.astype(BF16)
        part = _rows8(dhv * xh)

        @pl.when(i == 0)
        def _():
            acc_ref[...] = part

        @pl.when(i > 0)
        def _():
            acc_ref[...] += part

        @pl.when(i == n - 1)
        def _():
            dg_ref[...] = jnp.sum(acc_ref[...], axis=0, keepdims=True)

    row = pl.BlockSpec((tm, d), lambda i: (i, 0))
    vec = pl.BlockSpec((1, d), lambda i: (0, 0))
    return _pallas(
        body,
        name=name,
        grid=(n,),
        in_specs=[row, vec, row, row] + [ANY] * n_after,
        out_specs=[row, row, vec],
        out_shape=[jax.ShapeDtypeStruct((s, d), F32), jax.ShapeDtypeStruct((s, d), BF16), jax.ShapeDtypeStruct((1, d), F32)],
        scratch_shapes=[pltpu.VMEM((8, d), F32)],
    )(x, g, dh, dres, *after)


def _head(x3, g, target):
    s, d = x3.shape
    tm = _tile(s, (256, 128))
    n = s // tm

    def body(x_ref, g_ref, t_ref, dx_ref, dxb_ref, dg_ref, loss_ref, acc_g, acc_l):
        i = pl.program_id(0)
        xv = x_ref[...]
        gv = g_ref[...]
        r = lax.rsqrt(jnp.mean(xv * xv, axis=-1, keepdims=True) + EPS)
        xh = xv * r
        e = xh * gv - t_ref[...]
        dy = e * (1.0 / d)
        dxh = dy * gv
        dx = r * (dxh - xh * jnp.mean(dxh * xh, axis=-1, keepdims=True))
        dx_ref[...] = dx
        dxb_ref[...] = dx.astype(BF16)
        pg = _rows8(dy * xh)
        plo = _rows8(e * e)

        @pl.when(i == 0)
        def _():
            acc_g[...] = pg
            acc_l[...] = plo

        @pl.when(i > 0)
        def _():
            acc_g[...] += pg
            acc_l[...] += plo

        @pl.when(i == n - 1)
        def _():
            dg_ref[...] = jnp.sum(acc_g[...], axis=0, keepdims=True)
            loss_ref[...] = jnp.full((1, LANES), (0.5 / d) * jnp.sum(acc_l[...]), F32)

    row = pl.BlockSpec((tm, d), lambda i: (i, 0))
    vec = pl.BlockSpec((1, d), lambda i: (0, 0))
    return _pallas(
        body,
        name="head",
        grid=(n,),
        in_specs=[row, vec, row],
        out_specs=[row, row, vec, pl.BlockSpec((1, LANES), lambda i: (0, 0))],
        out_shape=[
            jax.ShapeDtypeStruct((s, d), F32),
            jax.ShapeDtypeStruct((s, d), BF16),
            jax.ShapeDtypeStruct((1, d), F32),
            jax.ShapeDtypeStruct((1, LANES), F32),
        ],
        scratch_shapes=[pltpu.VMEM((8, d), F32), pltpu.VMEM((8, d), F32)],
    )(x3, g, target)


def _sgu_fwd(z, gain, ws_b, b_col, w_sgu):
    s = z.shape[0]
    groups = ws_b.shape[0]

    def body(zu_ref, zv_ref, gain_ref, ws_ref, b_ref, a_ref):
        vv = _gelu(zv_ref[...].astype(F32))
        r = lax.rsqrt(jnp.mean(vv * vv, axis=-1, keepdims=True) + EPS)
        vn = ((vv * r) * gain_ref[...]).astype(BF16)
        u = _gelu(zu_ref[...].astype(F32))
        for g in range(groups):
            sl = slice(g * BLK, (g + 1) * BLK)
            mixed = jnp.dot(ws_ref[g], vn[:, sl], preferred_element_type=F32) + b_ref[g]
            a_ref[:, sl] = (u[:, sl] * mixed).astype(BF16)

    return _pallas(
        body,
        name="sgu_fwd",
        grid=(s // BLK,),
        in_specs=[
            pl.BlockSpec((BLK, w_sgu), lambda c: (c, 0)),
            pl.BlockSpec((BLK, w_sgu), lambda c: (c, 1)),
            pl.BlockSpec((1, w_sgu), lambda c: (0, 0)),
            pl.BlockSpec((groups, BLK, BLK), lambda c: (0, 0, 0)),
            pl.BlockSpec((groups, BLK, 1), lambda c: (0, 0, 0)),
        ],
        out_specs=pl.BlockSpec((BLK, w_sgu), lambda c: (c, 0)),
        out_shape=jax.ShapeDtypeStruct((s, w_sgu), BF16),
    )(z, z, gain, ws_b, b_col)


def _sgu_bwd(z, da, gain, ws_b, wst_b, b_col, w_sgu, dz, after=()):
    s = z.shape[0]
    groups = ws_b.shape[0]
    n = s // BLK
    n_skip = 1 + len(after)

    def body(zu_ref, zv_ref, da_ref, gain_ref, ws_ref, wst_ref, b_ref, *rest):
        dz_ref, dws_ref, dbs_ref, dgain_ref, acc_gain = rest[n_skip:]
        c = pl.program_id(0)
        zu = zu_ref[...].astype(F32)
        zv = zv_ref[...].astype(F32)
        gain_v = gain_ref[...]
        vv = _gelu(zv)
        r = lax.rsqrt(jnp.mean(vv * vv, axis=-1, keepdims=True) + EPS)
        xh = vv * r
        vn = (xh * gain_v).astype(BF16)
        u = _gelu(zu)
        dav = da_ref[...].astype(F32)
        dmix = dav * u
        dmix_b = dmix.astype(BF16)
        dvn_parts = []
        for g in range(groups):
            sl = slice(g * BLK, (g + 1) * BLK)
            mixed = jnp.dot(ws_ref[g], vn[:, sl], preferred_element_type=F32) + b_ref[g]
            dz_ref[:, sl] = (dav[:, sl] * mixed * _gelu_grad(zu[:, sl])).astype(BF16)
            dvn_parts.append(jnp.dot(wst_ref[g], dmix_b[:, sl], preferred_element_type=F32))
            dws_g = lax.dot_general(dmix_b[:, sl], vn[:, sl], NT, preferred_element_type=F32)
            dbs_g = jnp.sum(dmix[:, sl], axis=1, keepdims=True)

            @pl.when(c == 0)
            def _():
                dws_ref[g] = dws_g
                dbs_ref[g] = dbs_g

            @pl.when(c > 0)
            def _():
                dws_ref[g] += dws_g
                dbs_ref[g] += dbs_g

        dvn = jnp.concatenate(dvn_parts, axis=1)
        dxh = dvn * gain_v
        dvv = r * (dxh - xh * jnp.mean(dxh * xh, axis=-1, keepdims=True))
        dz_ref[:, w_sgu:] = (dvv * _gelu_grad(zv)).astype(BF16)
        pg = _rows8(dvn * xh)

        @pl.when(c == 0)
        def _():
            acc_gain[...] = pg

        @pl.when(c > 0)
        def _():
            acc_gain[...] += pg

        @pl.when(c == n - 1)
        def _():
            dgain_ref[...] = jnp.sum(acc_gain[...], axis=0, keepdims=True)

    full3 = pl.BlockSpec((groups, BLK, BLK), lambda c: (0, 0, 0))
    col3 = pl.BlockSpec((groups, BLK, 1), lambda c: (0, 0, 0))
    vec = pl.BlockSpec((1, w_sgu), lambda c: (0, 0))
    return _pallas(
        body,
        name="sgu_bwd",
        grid=(n,),
        in_specs=[
            pl.BlockSpec((BLK, w_sgu), lambda c: (c, 0)),
            pl.BlockSpec((BLK, w_sgu), lambda c: (c, 1)),
            pl.BlockSpec((BLK, w_sgu), lambda c: (c, 0)),
            vec,
            full3,
            full3,
            col3,
            ANY,
        ]
        + [ANY] * len(after),
        out_specs=[pl.BlockSpec((BLK, 2 * w_sgu), lambda c: (c, 0)), full3, col3, vec],
        out_shape=[
            jax.ShapeDtypeStruct(dz.shape, BF16),
            jax.ShapeDtypeStruct((groups, BLK, BLK), F32),
            jax.ShapeDtypeStruct((groups, BLK, 1), F32),
            jax.ShapeDtypeStruct((1, w_sgu), F32),
        ],
        scratch_shapes=[pltpu.VMEM((8, w_sgu), F32)],
        input_output_aliases={7: 0},
    )(z, z, da, gain, ws_b, wst_b, b_col, dz, *after)


def _attn_softmax(sink_ref, q_ref, k_ref, v_ref, bias_ref, s_len, grp):
    kv = pl.program_id(0)
    n = pl.program_id(1)
    start = pl.multiple_of(n * BLK, BLK)
    kb = k_ref[pl.ds(start, 3 * BLK), :]
    vb = v_ref[pl.ds(start, 3 * BLK), :]
    qv = q_ref[...]
    qs = jnp.concatenate([qv[:, g * HEAD_DIM : (g + 1) * HEAD_DIM] for g in range(grp)], axis=0).astype(BF16)
    sc = lax.dot_general(qs, kb, NT, preferred_element_type=F32) * (HEAD_DIM**-0.5)
    sc = sc + bias_ref[...].reshape(grp * BLK, 3 * BLK)
    kpos = start + lax.broadcasted_iota(I32, (1, 3 * BLK), 1) - BLK
    sc = jnp.where((kpos >= 0) & (kpos < s_len), sc, NEG)
    sink = jnp.concatenate([jnp.full((BLK, 1), sink_ref[kv * grp + g], F32) for g in range(grp)], axis=0)
    m = jnp.maximum(jnp.max(sc, axis=-1, keepdims=True), sink)
    p = jnp.exp(sc - m)
    esink = jnp.exp(sink - m)
    den = jnp.sum(p, axis=-1, keepdims=True) + esink
    return start, qs, kb, vb, p / den, esink / den


def _attn_specs(s, grp, q_blk0):
    qw = grp * HEAD_DIM
    return [
        pl.BlockSpec(memory_space=pltpu.SMEM),
        pl.BlockSpec((BLK, qw), lambda kv, n: (n, q_blk0 + kv)),
        pl.BlockSpec((s + 2 * BLK, HEAD_DIM), lambda kv, n: (0, kv)),
        pl.BlockSpec((s + 2 * BLK, HEAD_DIM), lambda kv, n: (0, kv)),
        pl.BlockSpec((grp, BLK, 3 * BLK), lambda kv, n: (kv, 0, 0)),
    ]


def _attn_fwd(sink, z, k_pad, v_pad, bias_tab, grp, q_blk0):
    s = z.shape[0]
    qw = grp * HEAD_DIM

    def body(sink_ref, q_ref, k_ref, v_ref, bias_ref, o_ref):
        _, _, _, vb, pn, _ = _attn_softmax(sink_ref, q_ref, k_ref, v_ref, bias_ref, s, grp)
        o = jnp.dot(pn.astype(BF16), vb, preferred_element_type=F32)
        for g in range(grp):
            o_ref[:, g * HEAD_DIM : (g + 1) * HEAD_DIM] = o[g * BLK : (g + 1) * BLK].astype(BF16)

    return _pallas(
        body,
        name="attn_fwd",
        grid=(N_KV_HEADS, s // BLK),
        in_specs=_attn_specs(s, grp, q_blk0),
        out_specs=pl.BlockSpec((BLK, qw), lambda kv, n: (n, kv)),
        out_shape=jax.ShapeDtypeStruct((s, N_KV_HEADS * qw), BF16),
    )(sink, z, k_pad, v_pad, bias_tab)


def _attn_bwd(sink, z, k_pad, v_pad, bias_tab, dout, dz, grp, q_blk0):
    s = z.shape[0]
    qw = grp * HEAD_DIM
    nb = s // BLK
    heads = N_KV_HEADS * grp

    def body(sink_ref, q_ref, k_ref, v_ref, bias_ref, do_ref, dz_in, dq_ref, dk_ref, dv_ref, dbias_ref, dsink_ref, dk_acc, dv_acc):
        del dz_in
        kv = pl.program_id(0)
        n = pl.program_id(1)
        start, qs, kb, vb, pn, psink = _attn_softmax(sink_ref, q_ref, k_ref, v_ref, bias_ref, s, grp)
        dov = do_ref[...]
        dos = jnp.concatenate([dov[:, g * HEAD_DIM : (g + 1) * HEAD_DIM] for g in range(grp)], axis=0)
        dp = lax.dot_general(dos, vb, NT, preferred_element_type=F32)
        dvb = lax.dot_general(pn.astype(BF16), dos, TN, preferred_element_type=F32)
        delta = jnp.sum(pn * dp, axis=-1, keepdims=True)
        ds = pn * (dp - delta)
        dsb = (ds * (HEAD_DIM**-0.5)).astype(BF16)
        dq = jnp.dot(dsb, kb, preferred_element_type=F32)
        dkb = lax.dot_general(dsb, qs, TN, preferred_element_type=F32)
        for g in range(grp):
            dq_ref[:, g * HEAD_DIM : (g + 1) * HEAD_DIM] = dq[g * BLK : (g + 1) * BLK].astype(BF16)

        @pl.when(n == 0)
        def _():
            dk_acc[...] = jnp.zeros_like(dk_acc)
            dv_acc[...] = jnp.zeros_like(dv_acc)
            dbias_ref[...] = jnp.zeros_like(dbias_ref)

        @pl.when((n == 0) & (kv == 0))
        def _():
            dsink_ref[...] = jnp.zeros_like(dsink_ref)

        dk_acc[pl.ds(start, 3 * BLK), :] += dkb
        dv_acc[pl.ds(start, 3 * BLK), :] += dvb
        dbias_ref[...] += ds.reshape(grp, BLK, 3 * BLK)
        row = lax.broadcasted_iota(I32, (heads, LANES), 0)
        sd = psink * delta
        upd = jnp.zeros((heads, LANES), F32)
        for g in range(grp):
            upd = jnp.where(row == kv * grp + g, -jnp.sum(sd[g * BLK : (g + 1) * BLK]), upd)
        dsink_ref[...] += upd

        @pl.when(n == nb - 1)
        def _():
            dk_ref[...] = dk_acc[...]
            dv_ref[...] = dv_acc[...]

    pad_spec = pl.BlockSpec((s + 2 * BLK, HEAD_DIM), lambda kv, n: (0, kv))
    kvw = N_KV_HEADS * HEAD_DIM
    return _pallas(
        body,
        name="attn_bwd",
        grid=(N_KV_HEADS, nb),
        in_specs=_attn_specs(s, grp, q_blk0) + [pl.BlockSpec((BLK, qw), lambda kv, n: (n, kv)), ANY],
        out_specs=[
            pl.BlockSpec((BLK, qw), lambda kv, n: (n, q_blk0 + kv)),
            pad_spec,
            pad_spec,
            pl.BlockSpec((grp, BLK, 3 * BLK), lambda kv, n: (kv, 0, 0)),
            pl.BlockSpec((heads, LANES), lambda kv, n: (0, 0)),
        ],
        out_shape=[
            jax.ShapeDtypeStruct(dz.shape, BF16),
            jax.ShapeDtypeStruct((s + 2 * BLK, kvw), F32),
            jax.ShapeDtypeStruct((s + 2 * BLK, kvw), F32),
            jax.ShapeDtypeStruct((heads, BLK, 3 * BLK), F32),
            jax.ShapeDtypeStruct((heads, LANES), F32),
        ],
        scratch_shapes=[pltpu.VMEM((s + 2 * BLK, HEAD_DIM), F32), pltpu.VMEM((s + 2 * BLK, HEAD_DIM), F32)],
        input_output_aliases={6: 0},
    )(sink, z, k_pad, v_pad, bias_tab, dout, dz)


def _dkv_to_dz(dk_pad, dv_pad, dz, blk_idx):
    s = dz.shape[0]
    kvw = dk_pad.shape[1]

    def body(dk_ref, dv_ref, dz_in, out_ref):
        del dz_in
        out_ref[:, :kvw] = dk_ref[...].astype(BF16)
        out_ref[:, kvw:] = dv_ref[...].astype(BF16)

    src = pl.BlockSpec((BLK, kvw), lambda i: (i + 1, 0))
    return _pallas(
        body,
        name="dkv_to_dz",
        grid=(s // BLK,),
        in_specs=[src, src, ANY],
        out_specs=pl.BlockSpec((BLK, 2 * kvw), lambda i: (i, blk_idx)),
        out_shape=jax.ShapeDtypeStruct(dz.shape, BF16),
        input_output_aliases={2: 0},
    )(dk_pad, dv_pad, dz)


def _relbias_bwd(dbias_tab, bucket):
    heads = dbias_tab.shape[0]

    def body(dt_ref, bk_ref, out_ref):
        lane = lax.broadcasted_iota(I32, (1, LANES), 1)
        bk = bk_ref[...]
        rows = []
        for h in range(heads):
            dt = dt_ref[h]
            acc = jnp.zeros((1, LANES), F32)
            for b in range(REL_BUCKETS):
                acc = jnp.where(lane == b, jnp.sum(jnp.where(bk == b, dt, 0.0)), acc)
            rows.append(acc)
        out_ref[...] = jnp.concatenate(rows, axis=0)

    return _pallas(body, name="relbias_bwd", out_shape=jax.ShapeDtypeStruct((heads, LANES), F32))(dbias_tab, bucket)


def _t5_bucket(rel):
    nb = REL_BUCKETS // 2
    ret = jnp.where(rel > 0, nb, 0)
    n = jnp.abs(rel)
    max_exact = nb // 2
    nf = jnp.maximum(n, 1).astype(F32)
    large = max_exact + (jnp.log(nf / max_exact) / math.log(REL_MAX_DIST / max_exact) * (nb - max_exact)).astype(I32)
    large = jnp.minimum(large, nb - 1)
    return ret + jnp.where(n < max_exact, n, large)


def _band_tables(rel_bias):
    qi = jnp.arange(BLK)[:, None]
    kj = jnp.arange(3 * BLK)[None, :]
    rel = kj - BLK - qi
    bucket = _t5_bucket(rel).astype(I32)
    heads = rel_bias.shape[1]
    masked = jnp.where(jnp.abs(rel) <= BLK, bucket, -1)

    def body(rb_ref, bk_ref, out_ref):
        bk = bk_ref[...]
        for h in range(heads):
            tab = jnp.full(bk.shape, NEG, F32)
            for b in range(REL_BUCKETS):
                tab = jnp.where(bk == b, rb_ref[b, h], tab)
            out_ref[h] = tab

    bias_tab = _pallas(
        body,
        name="bias_table",
        in_specs=[pl.BlockSpec(memory_space=pltpu.SMEM), pl.BlockSpec(memory_space=pltpu.VMEM)],
        out_specs=pl.BlockSpec(memory_space=pltpu.VMEM),
        out_shape=jax.ShapeDtypeStruct((heads, BLK, 3 * BLK), F32),
    )(rel_bias.astype(F32), masked)
    return bias_tab, bucket


EW_BLOCK_ELEMS = 512 * 1024


def _ew_tiles(shape, elems=EW_BLOCK_ELEMS // 2):
    r, c = shape
    tn = c if c <= 2048 else _tile(c, (2048, 1920, 1536, 1408, 1024, 512))
    tm = _tile(r, [t for t in (1024, 512, 256, 128, 64, 32, 16, 8) if t * tn <= elems] or [8])
    return tm, tn


def _cast_into_full(name, qidx, w, kind, after=()):
    r, c = w.shape
    tm, tn = _ew_tiles(w.shape, EW_BLOCK_ELEMS)
    nbi, nbj = r // tm, c // tn
    if kind == "col":
        full, out_spec = (r, c * N_CHIPS), pl.BlockSpec((tm, tn), lambda i, j, q: (i, q[0] * nbj + j))
    else:
        full, out_spec = (r * N_CHIPS, c), pl.BlockSpec((tm, tn), lambda i, j, q: (q[0] * nbi + i, j))

    def body(q_ref, w_ref, *rest):
        del q_ref
        rest[-1][...] = w_ref[...].astype(BF16)

    return _pallas(
        body,
        name=name,
        grid_spec=pltpu.PrefetchScalarGridSpec(
            num_scalar_prefetch=1,
            grid=(nbi, nbj),
            in_specs=[pl.BlockSpec((tm, tn), lambda i, j, q: (i, j))] + [ANY] * len(after),
            out_specs=out_spec,
        ),
        out_shape=jax.ShapeDtypeStruct(full, BF16),
    )(qidx, w, *after)


def _adamw(name, w, g, m, v, after=()):
    tm, tn = _ew_tiles(w.shape, EW_BLOCK_ELEMS)
    if _nbytes(w.shape, F32) <= 1024 * 1024:
        tm, tn = w.shape
    spec = pl.BlockSpec((tm, tn), lambda i, j: (i, j))
    n_after = len(after)

    def body(w_ref, g_ref, m_ref, v_ref, *rest):
        d_ref, nm_ref, nv_ref, g_out_ref = rest[n_after:]
        gv = g_ref[...]
        g_out_ref[...] = gv
        nm = ADAM_B1 * m_ref[...] + (1.0 - ADAM_B1) * gv
        nv = ADAM_B2 * v_ref[...] + (1.0 - ADAM_B2) * (gv * gv)
        m_hat = nm / (1.0 - ADAM_B1**ADAM_STEP)
        v_hat = nv / (1.0 - ADAM_B2**ADAM_STEP)
        d_ref[...] = -ADAM_LR * (m_hat / (jnp.sqrt(v_hat) + ADAM_EPS) + ADAM_WD * w_ref[...])
        nm_ref[...] = nm
        nv_ref[...] = nv

    out = jax.ShapeDtypeStruct(w.shape, F32)
    return _pallas(
        body, name=name, grid=(w.shape[0] // tm, w.shape[1] // tn), in_specs=[spec] * 4 + [ANY] * n_after,
        out_specs=[spec] * 4, out_shape=[out, out, out, out],
        compiler_params=_params(_mm_vmem([((tm, tn), F32, 24)])),
    )(w, g, m, v, *after)


def _pair_add(name, cidx, g_full, r_sib, kind):
    hr, hc = r_sib.shape
    tm, tn = _ew_tiles((hr, hc), 2 * EW_BLOCK_ELEMS)
    nbi, nbj = hr // tm, hc // tn
    if kind == "col":
        g_spec = pl.BlockSpec((tm, tn), lambda i, j, c: (c[0] * nbi + i, j))
    else:
        g_spec = pl.BlockSpec((tm, tn), lambda i, j, c: (i, c[0] * nbj + j))
    spec = pl.BlockSpec((tm, tn), lambda i, j, c: (i, j))

    def body(c_ref, g_ref, r_ref, o_ref):
        del c_ref
        o_ref[...] = (g_ref[...].astype(F32) + r_ref[...].astype(F32)).astype(BF16)

    return _pallas(
        body,
        name=name,
        grid_spec=pltpu.PrefetchScalarGridSpec(num_scalar_prefetch=1, grid=(nbi, nbj), in_specs=[g_spec, spec], out_specs=spec),
        out_shape=jax.ShapeDtypeStruct((hr, hc), BF16),
        compiler_params=_params(_mm_vmem([((tm, tn), BF16, 6), ((tm, tn), F32, 3)])),
    )(cidx, g_full, r_sib)


def _chip_sum(name, qidx, c_half, r_ici, kind):
    _, pr, pc = r_ici.shape
    tm, tn = _ew_tiles((pr, pc), 2 * EW_BLOCK_ELEMS)
    nbi, nbj = pr // tm, pc // tn
    if kind == "col":
        own_spec = pl.BlockSpec((tm, tn), lambda i, j, q: (i, q[0] * nbj + j))
        full, out_spec = (2 * pr, pc), pl.BlockSpec((tm, tn), lambda i, j, q: (q[1] * nbi + i, j))
    else:
        own_spec = pl.BlockSpec((tm, tn), lambda i, j, q: (q[0] * nbi + i, j))
        full, out_spec = (pr, 2 * pc), pl.BlockSpec((tm, tn), lambda i, j, q: (i, q[1] * nbj + j))

    def body(q_ref, own_ref, r_ref, o_ref):
        q = q_ref[0]
        own = own_ref[...].astype(F32)
        recv = [r_ref[r].astype(F32) for r in range(3)]
        total = None
        for chip in range(N_CHIPS):
            d = chip ^ q
            term = jnp.where(d == 0, own, jnp.where(d == 2, recv[0], jnp.where(d == 1, recv[1], recv[2])))
            total = term if total is None else total + term
        o_ref[...] = total

    return _pallas(
        body,
        name=name,
        grid_spec=pltpu.PrefetchScalarGridSpec(
            num_scalar_prefetch=1,
            grid=(nbi, nbj),
            in_specs=[own_spec, pl.BlockSpec((3, tm, tn), lambda i, j, q: (0, i, j))],
            out_specs=out_spec,
        ),
        out_shape=jax.ShapeDtypeStruct(full, F32),
        compiler_params=_params(_mm_vmem([((tm, tn), BF16, 8), ((tm, tn), F32, 6)])),
    )(qidx, c_half, r_ici)


_REL_MASK = (2, 1, 3)


def _place():
    x, y, c = lax.axis_index("x"), lax.axis_index("y"), lax.axis_index("c")
    chips = [(1 - x, y), (x, 1 - y), (1 - x, 1 - y)]
    return x, y, c, 2 * x + y, chips


def _shard_view(ref, kind, chip):
    if kind == "col":
        w = ref.shape[1] // N_CHIPS
        return ref.at[:, pl.ds(pl.multiple_of(chip * w, LANES), w)]
    h = ref.shape[0] // N_CHIPS
    return ref.at[pl.ds(pl.multiple_of(chip * h, 16), h), :]


def _row_half(ref, half):
    h = ref.shape[0] // 2
    return ref.at[pl.ds(pl.multiple_of(half * h, 16), h), :]


def _pair_half(ref, kind, half):
    if kind == "col":
        return _row_half(ref, half)
    w = ref.shape[1] // 2
    return ref.at[:, pl.ds(pl.multiple_of(half * w, LANES), w)]


def _remote(src, dst, send_sem, recv_sem, dev):
    return pltpu.make_async_remote_copy(src_ref=src, dst_ref=dst, send_sem=send_sem, recv_sem=recv_sem, device_id=dev, device_id_type=MESH)


def _hbm(a):
    return pltpu.with_memory_space_constraint(a, pltpu.HBM)


def _gather_start(name, fulls, kinds, rels=(0, 1, 2), after=()):
    n_w = len(fulls)

    def body(*refs):
        g = refs[:n_w]
        send_sem, recv_sem = refs[n_w + len(after)], refs[n_w + len(after) + 1]
        token = refs[-1]
        _, _, c, q, chips = _place()
        for w in range(n_w):
            mine = _row_half(_shard_view(g[w], kinds[w], q), c)
            for r in rels:
                _remote(mine, mine, send_sem.at[3 * w + r], recv_sem.at[3 * w + r], (*chips[r], c)).start()
        token[...] = jnp.zeros_like(token)

    res = _pallas(
        body,
        name=name,
        out_shape=(
            pltpu.SemaphoreType.DMA((3 * n_w,)),
            pltpu.SemaphoreType.DMA((3 * n_w,)),
            *[pltpu.HBM(f.shape, f.dtype) for f in fulls],
            jax.ShapeDtypeStruct((8, LANES), F32),
        ),
        in_specs=[HBM_SPEC] * n_w + [ANY] * len(after),
        out_specs=(SEM_SPEC, SEM_SPEC, *[HBM_SPEC] * n_w, pl.BlockSpec(memory_space=pltpu.VMEM)),
        input_output_aliases={w: w + 2 for w in range(n_w)},
        compiler_params=pltpu.CompilerParams(has_side_effects=EFFECT),
    )(*[_hbm(f) for f in fulls], *after)
    return res[0], res[1], list(res[2 : 2 + n_w]), res[-1]


def _relay_copies(kind, waiting):
    def copies(refs, send_sem, recv_sem):
        _, _, c, q, chips = _place()
        out = []
        for k, (src_rel, dst_rel) in enumerate(((0, 1), (1, 0))):
            held = _row_half(_row_half(_shard_view(refs[0], kind, q ^ _REL_MASK[src_rel]), c), k)
            far = _row_half(_row_half(_shard_view(refs[0], kind, q ^ _REL_MASK[2]), c), k)
            out.append(_remote(held, far if waiting else held, send_sem.at[k], recv_sem.at[k], (*chips[dst_rel], c)))
        return out

    return copies


def _gather_wait(name, fulls, kinds, w_ids, send_sem, recv_sem, after, rels=(0, 1, 2)):
    n = len(fulls)

    def body(*refs):
        g = refs[:n]
        s_sem, r_sem = refs[n], refs[n + 1]
        x, y, c, q, _ = _place()
        for i, w in enumerate(w_ids):
            mine = _row_half(_shard_view(g[i], kinds[i], q), c)
            for r in rels:
                landed = _row_half(_shard_view(g[i], kinds[i], q ^ _REL_MASK[r]), c)
                cp = _remote(mine, landed, s_sem.at[3 * w + r], r_sem.at[3 * w + r], (x, y, 1 - c))
                cp.wait_send()
                cp.wait_recv()

    res = _pallas(
        body,
        name=name,
        out_shape=[pltpu.HBM(f.shape, f.dtype) for f in fulls],
        in_specs=[HBM_SPEC] * n + [SEM_SPEC, SEM_SPEC, ANY],
        out_specs=[HBM_SPEC] * n,
        input_output_aliases={i: i for i in range(n)},
        compiler_params=pltpu.CompilerParams(has_side_effects=EFFECT),
    )(*fulls, send_sem, recv_sem, after)
    return list(res)


def _gather_forward(name, fulls, kinds):
    n = len(fulls)

    def body(*refs):
        g = refs[n : 2 * n]
        send, recv = refs[2 * n :]
        x, y, c, q, _ = _place()
        sib = (x, y, 1 - c)
        cps = []
        for i in range(n):
            for r in range(3):
                landed = _row_half(_shard_view(g[i], kinds[i], q ^ _REL_MASK[r]), c)
                cps.append(_remote(landed, landed, send.at[i, r], recv.at[i, r], sib))
        for cp in cps:
            cp.start()
        for i in range(n):
            for r in range(3):
                other = _row_half(_shard_view(g[i], kinds[i], q ^ _REL_MASK[r]), 1 - c)
                _remote(other, other, send.at[i, r], recv.at[i, r], sib).wait_recv()
        for cp in cps:
            cp.wait_send()

    res = _pallas(
        body,
        name=name,
        in_specs=[ANY] * n,
        out_specs=[ANY] * n,
        out_shape=[jax.ShapeDtypeStruct(f.shape, f.dtype) for f in fulls],
        scratch_shapes=[pltpu.SemaphoreType.DMA((n, 3)), pltpu.SemaphoreType.DMA((n, 3))],
        input_output_aliases={i: i for i in range(n)},
    )(*fulls)
    return list(res)


def _split_start(name, bufs, n_sems, copies):
    n = len(bufs)

    def body(*refs):
        for cp in copies(refs[:n], refs[n], refs[n + 1]):
            cp.start()

    res = _pallas(
        body,
        name=name,
        out_shape=(
            pltpu.SemaphoreType.DMA((n_sems,)),
            pltpu.SemaphoreType.DMA((n_sems,)),
            *[pltpu.HBM(b.shape, b.dtype) for b in bufs],
        ),
        in_specs=[HBM_SPEC] * n,
        out_specs=(SEM_SPEC, SEM_SPEC, *[HBM_SPEC] * n),
        input_output_aliases={i: i + 2 for i in range(n)},
        compiler_params=pltpu.CompilerParams(has_side_effects=EFFECT),
    )(*[_hbm(b) for b in bufs])
    return res[0], res[1], list(res[2:])


def _split_wait(name, bufs, send_sem, recv_sem, copies, after):
    n = len(bufs)

    def body(*refs):
        for cp in copies(refs[:n], refs[n], refs[n + 1]):
            cp.wait_send()
            cp.wait_recv()

    res = _pallas(
        body,
        name=name,
        out_shape=[pltpu.HBM(b.shape, b.dtype) for b in bufs],
        in_specs=[HBM_SPEC] * n + [SEM_SPEC, SEM_SPEC, ANY],
        out_specs=[HBM_SPEC] * n,
        input_output_aliases={i: i for i in range(n)},
        compiler_params=pltpu.CompilerParams(has_side_effects=EFFECT),
    )(*bufs, send_sem, recv_sem, after)
    return list(res)


def _pair_exchange_copies(kinds):
    n = len(kinds)

    def copies(refs, send_sem, recv_sem):
        x, y, c, _, _ = _place()
        return [
            _remote(_pair_half(refs[w], kinds[w], 1 - c), refs[n + w], send_sem.at[w], recv_sem.at[w], (x, y, 1 - c))
            for w in range(n)
        ]

    return copies


def _pair_share_copies(kinds, waiting):
    def copies(refs, send_sem, recv_sem):
        x, y, c, _, _ = _place()
        out = []
        for w, kind in enumerate(kinds):
            mine = _pair_half(refs[w], kind, c)
            dst = _pair_half(refs[w], kind, 1 - c) if waiting else mine
            out.append(_remote(mine, dst, send_sem.at[w], recv_sem.at[w], (x, y, 1 - c)))
        return out

    return copies


def _piece_shape(half_shape, kind):
    r, c = half_shape
    return (3, r, c // N_CHIPS) if kind == "col" else (3, r // N_CHIPS, c)


def _chip_send_start(name, halves, kinds):
    n = len(halves)
    lands = [lax.empty(_piece_shape(h.shape, k), BF16) for h, k in zip(halves, kinds)]

    def body(*refs):
        h, land = refs[:n], refs[n : 2 * n]
        send_sem, recv_sem = refs[2 * n], refs[2 * n + 1]
        _, _, c, q, chips = _place()
        for i in range(n):
            for r, chip in enumerate(chips):
                piece = _shard_view(h[i], kinds[i], q ^ _REL_MASK[r])
                _remote(piece, land[i].at[r], send_sem.at[3 * i + r], recv_sem.at[3 * i + r], (*chip, c)).start()

    res = _pallas(
        body,
        name=name,
        out_shape=(
            pltpu.SemaphoreType.DMA((3 * n,)),
            pltpu.SemaphoreType.DMA((3 * n,)),
            *[pltpu.HBM(a.shape, a.dtype) for a in halves],
            *[pltpu.HBM(a.shape, a.dtype) for a in lands],
        ),
        in_specs=[HBM_SPEC] * (2 * n),
        out_specs=(SEM_SPEC, SEM_SPEC, *[HBM_SPEC] * (2 * n)),
        input_output_aliases={i: i + 2 for i in range(2 * n)},
        compiler_params=pltpu.CompilerParams(has_side_effects=EFFECT),
    )(*[_hbm(a) for a in halves], *[_hbm(a) for a in lands])
    return res[0], res[1], list(res[2 : 2 + n]), list(res[2 + n :])


def _chip_send_wait(name, halves, lands, kinds, send_sem, recv_sem, after):
    n = len(halves)

    def body(*refs):
        h, land = refs[:n], refs[n : 2 * n]
        s_sem, r_sem = refs[2 * n], refs[2 * n + 1]
        x, y, c, q, _ = _place()
        for i in range(n):
            for r in range(3):
                piece = _shard_view(h[i], kinds[i], q ^ _REL_MASK[r])
                cp = _remote(piece, land[i].at[r], s_sem.at[3 * i + r], r_sem.at[3 * i + r], (x, y, 1 - c))
                cp.wait_send()
                cp.wait_recv()

    res = _pallas(
        body,
        name=name,
        out_shape=[pltpu.HBM(a.shape, a.dtype) for a in halves] + [pltpu.HBM(a.shape, a.dtype) for a in lands],
        in_specs=[HBM_SPEC] * (2 * n) + [SEM_SPEC, SEM_SPEC, ANY],
        out_specs=[HBM_SPEC] * (2 * n),
        input_output_aliases={i: i for i in range(2 * n)},
        compiler_params=pltpu.CompilerParams(has_side_effects=EFFECT),
    )(*halves, *lands, send_sem, recv_sem, after)
    return list(res[:n]), list(res[n:])


def _small_exchange_copies(waiting):
    def copies(refs, send_sem, recv_sem):
        p, land = refs
        x, y, c, q, _ = _place()
        me = 2 * q + c
        out = []
        for dd in range(1, 2 * N_CHIPS):
            dev = (x ^ ((dd >> 2) & 1), y ^ ((dd >> 1) & 1), c ^ (dd & 1))
            dst = land.at[me ^ dd] if waiting else land.at[me]
            out.append(_remote(p, dst, send_sem.at[dd - 1], recv_sem.at[dd - 1], dev))
        return out

    return copies


def _small_sum(name, me_idx, p, land):
    rows = p.shape[0]
    n_dev = 2 * N_CHIPS

    def body(me_ref, p_ref, land_ref, o_ref):
        me = me_ref[0]
        total = None
        for dev in range(n_dev):
            term = jnp.where(me == dev, p_ref[...], land_ref[dev])
            total = term if total is None else total + term
        o_ref[...] = total

    return _pallas(
        body,
        name=name,
        grid_spec=pltpu.PrefetchScalarGridSpec(
            num_scalar_prefetch=1,
            grid=(1,),
            in_specs=[pl.BlockSpec((rows, LANES), lambda i, m: (0, 0)), pl.BlockSpec((n_dev, rows, LANES), lambda i, m: (0, 0, 0))],
            out_specs=pl.BlockSpec((rows, LANES), lambda i, m: (0, 0)),
        ),
        out_shape=jax.ShapeDtypeStruct(p.shape, F32),
    )(me_idx, p, land)


def _pack(parts):
    rows = []
    for a in parts:
        flat = a.reshape(-1).astype(F32)
        n = flat.shape[0]
        padded = -(-n // (8 * LANES)) * (8 * LANES)
        rows.append(jnp.pad(flat, (0, padded - n)).reshape(-1, LANES))
    return jnp.concatenate(rows, axis=0)


def _unpack(packed, shapes):
    out, row = [], 0
    for shp in shapes:
        n = int(np.prod(shp))
        nrows = -(-n // (8 * LANES)) * 8
        out.append(packed[row : row + nrows].reshape(-1)[:n].reshape(shp))
        row += nrows
    return out


def kernel(x, w_in, norm_mix, sgu_v_gain, sgu_w_s, sgu_b_s, w_a_out, attn_sink, rel_bias, w_b_out, w_o, norm_ffn, w_gate, w_up, w_down, norm_final, loss_target, m_w_in, m_norm_mix, m_sgu_v_gain, m_sgu_w_s, m_sgu_b_s, m_w_a_out, m_attn_sink, m_rel_bias, m_w_b_out, m_w_o, m_norm_ffn, m_w_gate, m_w_up, m_w_down, m_norm_final, v_w_in, v_norm_mix, v_sgu_v_gain, v_sgu_w_s, v_sgu_b_s, v_w_a_out, v_attn_sink, v_rel_bias, v_w_b_out, v_w_o, v_norm_ffn, v_w_gate, v_w_up, v_w_down, v_norm_final):
    s, d = x.shape[1], x.shape[2]
    w_sgu = sgu_v_gain.shape[1]
    groups = sgu_w_s.shape[1]
    heads = attn_sink.shape[1]
    grp = heads // N_KV_HEADS
    w_att = heads * HEAD_DIM
    w_kv = N_KV_HEADS * HEAD_DIM
    d_ff = w_gate.shape[2] * N_CHIPS
    n_in = w_in.shape[2] * N_CHIPS
    off_q = 2 * w_sgu
    off_k = off_q + w_att
    off_g = off_k + 2 * w_kv
    assert n_in == off_g + 2 * d and groups * BLK == w_sgu and s % BLK == 0

    x2d = x.reshape(s, d)
    tgt = loss_target.reshape(s, d)
    c_idx = lax.axis_index("c").astype(I32).reshape(1)
    q_idx = (2 * lax.axis_index("x") + lax.axis_index("y")).astype(I32).reshape(1)
    qc_idx = jnp.concatenate([q_idx, c_idx])

    W_IN, W_A, W_B, W_O, W_GATE, W_UP, W_DOWN = range(7)
    names = ["w_in", "w_a", "w_b", "w_o", "w_gate", "w_up", "w_down"]
    kinds = ["col", "col", "col", "row", "col", "col", "row"]
    big_w = [w_in[0], w_a_out[0], w_b_out[0], w_o[0], w_gate[0], w_up[0], w_down[0]]
    big_m = [m_w_in[0], m_w_a_out[0], m_w_b_out[0], m_w_o[0], m_w_gate[0], m_w_up[0], m_w_down[0]]
    big_v = [v_w_in[0], v_w_a_out[0], v_w_b_out[0], v_w_o[0], v_w_gate[0], v_w_up[0], v_w_down[0]]
    full_in = _cast_into_full("cast_w_in", q_idx, big_w[W_IN], kinds[W_IN])
    in_send, in_recv, (full_in,), token = _gather_start("gather_start_in", [full_in], [kinds[W_IN]], rels=(0, 1))
    rest = [_cast_into_full("cast_" + names[i], q_idx, big_w[i], kinds[i], after=(token,)) for i in range(1, 7)]
    h1 = _rms_fwd("rms_mix", x2d, norm_mix, after=(token,))
    (full_in,) = _gather_wait("gather_wait_in", [full_in], [kinds[W_IN]], [0], in_send, in_recv, h1, rels=(0, 1))
    relay_send, relay_recv, (full_in,) = _split_start("gather_relay_in", [full_in], 2, _relay_copies(kinds[W_IN], False))
    ag_send, ag_recv, rest, token = _gather_start("gather_start_rest", rest, kinds[1:], after=(full_in,))
    (full_in,) = _split_wait("gather_relay_wait_in", [full_in], relay_send, relay_recv, _relay_copies(kinds[W_IN], True), token)
    (g_in,) = _gather_forward("gather_fwd_in", [full_in], [kinds[W_IN]])
    fulls = [g_in] + rest

    def gathered(tag, ids, after):
        got = _gather_wait("gather_wait_" + tag, [fulls[i] for i in ids], [kinds[i] for i in ids], [i - 1 for i in ids],
                           ag_send, ag_recv, after)
        return _gather_forward("gather_fwd_" + tag, got, [kinds[i] for i in ids])

    ws_b = sgu_w_s[0].astype(BF16)
    wst_b = jnp.swapaxes(sgu_w_s[0], 1, 2).astype(BF16)
    b_col = sgu_b_s[0].reshape(groups, BLK, 1)
    bias_tab, bucket = _band_tables(rel_bias)
    sink = attn_sink[0]

    tm = _tile(s, (1024, 512, 256, 128))

    tn = _tile(n_in, (768, 640, 512))
    z = _mm(
        "mm_z", (s // tm, n_in // tn, 1), [h1, g_in],
        [pl.BlockSpec((tm, d), lambda i, j, k: (i, 0)), pl.BlockSpec((d, tn), lambda i, j, k: (0, j))],
        [jax.ShapeDtypeStruct((s, n_in), BF16)], [pl.BlockSpec((tm, tn), lambda i, j, k: (i, j))],
        [(0, 1, NN, 0)], 1, (tm, tn), 1, lambda ins, vals, outs, cs: _put(outs[0], cs, vals[0]),
        _mm_vmem([((tm, d), BF16, 2), ((d, tn), BF16, 2), ((tm, tn), F32, 3)]),
    )[0]
    g_a, g_b, g_o = gathered("mix", [W_A, W_B, W_O], z)

    a_act = _sgu_fwd(z, sgu_v_gain, ws_b, b_col, w_sgu)

    kv_b = z[:, off_k:off_g]
    k_pad = jnp.pad(kv_b[:, :w_kv], ((BLK, BLK), (0, 0)))
    v_pad = jnp.pad(kv_b[:, w_kv:], ((BLK, BLK), (0, 0)))
    q_blk0 = off_q // (grp * HEAD_DIM)
    att = _attn_fwd(sink, z, k_pad, v_pad, bias_tab, grp, q_blk0)

    tg = _tile(d, (512,))
    ga0, gb0 = off_g // tg, (off_g + d) // tg

    def ep_gate(ins, vals, outs, cs):
        sa, sb = _sigmoid(ins[4][:, cs].astype(F32)), _sigmoid(ins[5][:, cs].astype(F32))
        _put(outs[0], cs, sa * vals[0] + sb * vals[1])
        _put(outs[1], cs, vals[0])
        _put(outs[2], cs, vals[1])

    t_out = pl.BlockSpec((tm, tg), lambda i, j, k: (i, j))
    m_act, y_a, y_b = _mm(
        "mm_branches", (s // tm, d // tg, 1), [a_act, g_a, att, g_b, z, z],
        [pl.BlockSpec((tm, w_sgu), lambda i, j, k: (i, 0)), pl.BlockSpec((w_sgu, tg), lambda i, j, k: (0, j)),
         pl.BlockSpec((tm, w_att), lambda i, j, k: (i, 0)), pl.BlockSpec((w_att, tg), lambda i, j, k: (0, j)),
         pl.BlockSpec((tm, tg), lambda i, j, k: (i, ga0 + j)), pl.BlockSpec((tm, tg), lambda i, j, k: (i, gb0 + j))],
        [jax.ShapeDtypeStruct((s, d), BF16)] * 3,
        [t_out, t_out, t_out], [(0, 1, NN, 0), (2, 3, NN, 1)], 2, (tm, tg), 1, ep_gate,
        _mm_vmem([((tm, w_sgu), BF16, 4), ((w_sgu, tg), BF16, 4), ((tm, tg), F32, 12)]),    )

    tn = _tile(d, (1024, 512))

    def ep_residual(ins, vals, outs, cs):
        _put(outs[0], cs, ins[2][:, cs] + vals[0])

    x2 = _mm(
        "mm_wo", (s // tm, d // tn, 1), [m_act, g_o, x2d],
        [pl.BlockSpec((tm, d), lambda i, j, k: (i, 0)), pl.BlockSpec((d, tn), lambda i, j, k: (0, j)),
         pl.BlockSpec((tm, tn), lambda i, j, k: (i, j))],
        [jax.ShapeDtypeStruct((s, d), F32)], [pl.BlockSpec((tm, tn), lambda i, j, k: (i, j))],
        [(0, 1, NN, 0)], 1, (tm, tn), 1, ep_residual,
        _mm_vmem([((tm, d), BF16, 2), ((d, tn), BF16, 2), ((tm, tn), F32, 5)]),    )[0]

    h2 = _rms_fwd("rms_ffn", x2, norm_ffn)
    (g_gate,) = gathered("gate", [W_GATE], h2)
    (g_up,) = gathered("up", [W_UP], g_gate)

    tf = _tile(d_ff, (512,))

    def ep_swiglu(ins, vals, outs, cs):
        gt, up = vals
        _put(outs[0], cs, gt)
        _put(outs[1], cs, up)
        _put(outs[2], cs, (gt * _sigmoid(gt)) * up)

    f_out = pl.BlockSpec((tm, tf), lambda i, j, k: (i, j))
    gt, up, f_act = _mm(
        "mm_gate_up", (s // tm, d_ff // tf, 1), [h2, g_gate, g_up],
        [pl.BlockSpec((tm, d), lambda i, j, k: (i, 0)), pl.BlockSpec((d, tf), lambda i, j, k: (0, j)),
         pl.BlockSpec((d, tf), lambda i, j, k: (0, j))],
        [jax.ShapeDtypeStruct((s, d_ff), BF16)] * 3,
        [f_out, f_out, f_out], [(0, 1, NN, 0), (0, 2, NN, 1)], 2, (tm, tf), 1, ep_swiglu,
        _mm_vmem([((tm, d), BF16, 2), ((d, tf), BF16, 4), ((tm, tf), F32, 8)]),    )
    (g_down,) = gathered("ffn_out", [W_DOWN], f_act)

    tkf = _tile(d_ff, (1408, 1024, 512))
    tml, tnl = _tile(s, (512, 256, 128)), _tile(d, (512,))
    x3 = _mm(
        "mm_down", (s // tml, d // tnl, 1), [f_act, g_down, x2],
        [pl.BlockSpec((tml, d_ff), lambda i, j, k: (i, 0)), pl.BlockSpec((d_ff, tnl), lambda i, j, k: (0, j)),
         pl.BlockSpec((tml, tnl), lambda i, j, k: (i, j))],
        [jax.ShapeDtypeStruct((s, d), F32)], [pl.BlockSpec((tml, tnl), lambda i, j, k: (i, j))],
        [(0, 1, NN, 0)], 1, (tml, tnl), 1, ep_residual,
        _mm_vmem([((tml, d_ff), BF16, 2), ((d_ff, tnl), BF16, 2), ((tml, tnl), F32, 6)]),    )[0]

    dx3, dx3b, dg_final, loss_part = _head(x3, norm_final.reshape(1, d), tgt)

    def reduce_a(tag, ids, grads):
        ks = [kinds[i] for i in ids]
        lands = [lax.empty((g.shape[0] // 2, g.shape[1]) if k == "col" else (g.shape[0], g.shape[1] // 2), BF16)
                 for g, k in zip(grads, ks)]
        send, recv, bufs = _split_start("pair_send_" + tag, list(grads) + lands, len(ids), _pair_exchange_copies(ks))
        return {"tag": tag, "ids": ids, "ks": ks, "pair": (send, recv, bufs), "token": bufs[0]}

    def reduce_b(st, after):
        tag, ids, ks = st["tag"], st["ids"], st["ks"]
        send, recv, bufs = st["pair"]
        bufs = _split_wait("pair_wait_" + tag, bufs, send, recv, _pair_exchange_copies(ks), after)
        grads, from_sib = bufs[: len(ids)], bufs[len(ids) :]
        halves = [_pair_add("pair_add_" + names[i], c_idx, g, r, k) for i, g, r, k in zip(ids, grads, from_sib, ks)]
        st["chip"] = _chip_send_start("chip_send_" + tag, halves, ks)
        st["token"] = st["chip"][2][0]

    def reduce_c(st, after):
        tag, ids, ks = st["tag"], st["ids"], st["ks"]
        send, recv, halves, lands = st["chip"]
        halves, lands = _chip_send_wait("chip_wait_" + tag, halves, lands, ks, send, recv, after)
        pieces = [_chip_sum("chip_sum_" + names[i], qc_idx, h, r, k) for i, h, r, k in zip(ids, halves, lands, ks)]
        st["share"] = _split_start("share_send_" + tag, pieces, len(ids), _pair_share_copies(ks, False))
        st["token"] = st["share"][2][0]

    def reduce_d(st, after):
        send, recv, bufs = st["share"]
        return _split_wait("share_wait_" + st["tag"], bufs, send, recv, _pair_share_copies(st["ks"], True), after)

    grads_big, upd = [None] * 7, [None] * 7

    def finish(st, after):
        shared = reduce_d(st, after)
        after = shared[0]
        for i, g in zip(st["ids"], shared):
            upd[i] = _adamw("adamw_" + names[i], big_w[i], g, big_m[i], big_v[i], after=(after,))
            grads_big[i] = upd[i][3]
            after = upd[i][0]
        return after

    def ep_swiglu_bwd(ins, vals, outs, cs):
        df = vals[0]
        gtv, upv = ins[2][:, cs].astype(F32), ins[3][:, cs].astype(F32)
        sg = _sigmoid(gtv)
        _put(outs[0], cs, df * upv * (sg + gtv * sg * (1.0 - sg)))
        _put(outs[1], cs, df * (gtv * sg))

    dgt, dup = _mm(
        "mm_dswiglu", (s // tm, d_ff // tf, 1), [dx3b, g_down, gt, up],
        [pl.BlockSpec((tm, d), lambda i, j, k: (i, 0)), pl.BlockSpec((tf, d), lambda i, j, k: (j, 0)), f_out, f_out],
        [jax.ShapeDtypeStruct((s, d_ff), BF16), jax.ShapeDtypeStruct((s, d_ff), BF16)], [f_out, f_out],
        [(0, 1, NT, 0)], 1, (tm, tf), 1, ep_swiglu_bwd,
        _mm_vmem([((tm, d), BF16, 2), ((tf, d), BF16, 2), ((tm, tf), F32, 8)]),    )

    def ep_store(ins, vals, outs, cs):
        for o, v in zip(outs, vals):
            _put(o, cs, v)

    twn = _tile(d, (1024, 512))
    gw_down = _mm(
        "mm_gw_down", (d_ff // tkf, d // twn, 1), [f_act, dx3b],
        [pl.BlockSpec((s, tkf), lambda i, j, k: (0, i)), pl.BlockSpec((s, twn), lambda i, j, k: (0, j))],
        [jax.ShapeDtypeStruct((d_ff, d), BF16)], [pl.BlockSpec((tkf, twn), lambda i, j, k: (i, j))],
        [(0, 1, TN, 0)], 1, (tkf, twn), 1, ep_store,
        _mm_vmem([((s, tkf), BF16, 3), ((s, twn), BF16, 2), ((tkf, twn), F32, 3)]),
    )[0]
    red_down = reduce_a("down", [W_DOWN], [gw_down])

    tn2 = _tile(d, (256,))
    dh2_specs = [pl.BlockSpec((tm, d_ff), lambda i, j, k: (i, 0)), pl.BlockSpec((tn2, d_ff), lambda i, j, k: (j, 0))]
    dh2_tile = pl.BlockSpec((tm, tn2), lambda i, j, k: (i, j))
    dh2_vmem = _mm_vmem([((tm, d_ff), BF16, 2), ((tn2, d_ff), BF16, 2), ((tm, tn2), F32, 7)])
    dh2 = _mm(
        "mm_dh2_gate", (s // tm, d // tn2, 1), [dgt, g_gate], dh2_specs,
        [jax.ShapeDtypeStruct((s, d), F32)], [dh2_tile], [(0, 1, NT, 0)], 1, (tm, tn2), 1, ep_store, dh2_vmem,
        after=(red_down["token"],),
    )[0]
    dh2 = _mm(
        "mm_dh2_up", (s // tm, d // tn2, 1), [dup, g_up, dh2], dh2_specs + [dh2_tile],
        [jax.ShapeDtypeStruct((s, d), F32)], [dh2_tile], [(0, 1, NT, 0)], 1, (tm, tn2), 1, ep_residual, dh2_vmem,
    )[0]
    reduce_b(red_down, dh2)

    twr = _tile(d, (1024, 512))
    w_tile = pl.BlockSpec((twr, tf), lambda i, j, k: (i, j))
    gw_gate, gw_up = _mm(
        "mm_gw_gate_up", (d // twr, d_ff // tf, 1), [h2, dgt, dup],
        [pl.BlockSpec((s, twr), lambda i, j, k: (0, i)), pl.BlockSpec((s, tf), lambda i, j, k: (0, j)),
         pl.BlockSpec((s, tf), lambda i, j, k: (0, j))],
        [jax.ShapeDtypeStruct((d, d_ff), BF16), jax.ShapeDtypeStruct((d, d_ff), BF16)], [w_tile, w_tile],
        [(0, 1, TN, 0), (0, 2, TN, 1)], 2, (twr, tf), 1, ep_store,
        _mm_vmem([((s, twr), BF16, 3), ((s, tf), BF16, 4), ((twr, tf), F32, 6)]),
        after=(red_down["token"],),
    )
    red_ffn = reduce_a("ffn_in", [W_GATE, W_UP], [gw_gate, gw_up])

    dx2, dx2b, dg_ffn = _rms_bwd("rms_ffn_bwd", x2, norm_ffn, dh2, dx3, after=(red_ffn["token"],))

    nj = d // tg

    def lo(j):
        return jnp.minimum(j, nj - 1)

    def gate_bwd_body(dx_ref, wo_ref, ga_ref, gb_ref, ya_ref, yb_ref, dya_ref, dyb_ref, dz_ref, keep):
        j = pl.program_id(1)

        @pl.when(j < nj)
        def _():
            dm = lax.dot_general(dx_ref[...], wo_ref[...], NT, preferred_element_type=F32)
            sa, sb = _sigmoid(ga_ref[...].astype(F32)), _sigmoid(gb_ref[...].astype(F32))
            dya_ref[...] = (dm * sa).astype(BF16)
            dyb_ref[...] = (dm * sb).astype(BF16)
            dz_ref[...] = (dm * ya_ref[...].astype(F32) * (sa * (1.0 - sa))).astype(BF16)
            keep[lo(j)] = (dm * yb_ref[...].astype(F32) * (sb * (1.0 - sb))).astype(BF16)

        @pl.when(j >= nj)
        def _():
            dz_ref[...] = keep[jnp.maximum(j - nj, 0)]

    t_lo = pl.BlockSpec((tm, tg), lambda i, j: (i, lo(j)))
    dya, dyb, dz = _pallas(
        gate_bwd_body,
        name="mm_dgate",
        grid=(s // tm, 2 * nj),
        in_specs=[
            pl.BlockSpec((tm, d), lambda i, j: (i, 0)),
            pl.BlockSpec((tg, d), lambda i, j: (lo(j), 0)),
            pl.BlockSpec((tm, tg), lambda i, j: (i, ga0 + lo(j))),
            pl.BlockSpec((tm, tg), lambda i, j: (i, gb0 + lo(j))),
            t_lo,
            t_lo,
        ],
        out_specs=[t_lo, t_lo, pl.BlockSpec((tm, tg), lambda i, j: (i, ga0 + j))],
        out_shape=[jax.ShapeDtypeStruct((s, d), BF16), jax.ShapeDtypeStruct((s, d), BF16), jax.ShapeDtypeStruct((s, n_in), BF16)],
        scratch_shapes=[pltpu.VMEM((nj, tm, tg), BF16)],
        compiler_params=_params(_mm_vmem([((tm, d), BF16, 2), ((tg, d), BF16, 2), ((tm, tg), F32, 14), ((nj, tm, tg), BF16, 1)])),
    )(dx2b, g_o, z, z, y_a, y_b)
    reduce_b(red_ffn, dya)
    reduce_c(red_down, red_ffn["token"])

    gw_o = _mm(
        "mm_gw_o", (d // twr, d // twn, 1), [m_act, dx2b],
        [pl.BlockSpec((s, twr), lambda i, j, k: (0, i)), pl.BlockSpec((s, twn), lambda i, j, k: (0, j))],
        [jax.ShapeDtypeStruct((d, d), BF16)], [pl.BlockSpec((twr, twn), lambda i, j, k: (i, j))],
        [(0, 1, TN, 0)], 1, (twr, twn), 1, ep_store,
        _mm_vmem([((s, twr), BF16, 3), ((s, twn), BF16, 2), ((twr, twn), F32, 3)]),
        after=(red_down["token"],),
    )[0]
    after_down = finish(red_down, gw_o)

    tb = _tile(w_sgu, (1024, 512))
    b_out = pl.BlockSpec((tm, tb), lambda i, j, k: (i, j))

    da, datt = _mm(
        "mm_dbranches", (s // tm, w_sgu // tb, 1), [dya, g_a, dyb, g_b],
        [pl.BlockSpec((tm, d), lambda i, j, k: (i, 0)), pl.BlockSpec((tb, d), lambda i, j, k: (j, 0)),
         pl.BlockSpec((tm, d), lambda i, j, k: (i, 0)), pl.BlockSpec((tb, d), lambda i, j, k: (j, 0))],
        [jax.ShapeDtypeStruct((s, w_sgu), BF16), jax.ShapeDtypeStruct((s, w_att), BF16)], [b_out, b_out],
        [(0, 1, NT, 0), (2, 3, NT, 1)], 2, (tm, tb), 1, ep_store,
        _mm_vmem([((tm, d), BF16, 4), ((tb, d), BF16, 4), ((tm, tb), F32, 6)]),
        after=(after_down,),
    )

    wb_tile = pl.BlockSpec((tb, twn), lambda i, j, k: (i, j))
    gw_a, gw_b = _mm(
        "mm_gw_branches", (w_sgu // tb, d // twn, 1), [a_act, dya, att, dyb],
        [pl.BlockSpec((s, tb), lambda i, j, k: (0, i)), pl.BlockSpec((s, twn), lambda i, j, k: (0, j)),
         pl.BlockSpec((s, tb), lambda i, j, k: (0, i)), pl.BlockSpec((s, twn), lambda i, j, k: (0, j))],
        [jax.ShapeDtypeStruct((w_sgu, d), BF16), jax.ShapeDtypeStruct((w_att, d), BF16)], [wb_tile, wb_tile],
        [(0, 1, TN, 0), (2, 3, TN, 1)], 2, (tb, twn), 1, ep_store,
        _mm_vmem([((s, tb), BF16, 5), ((s, twn), BF16, 4), ((tb, twn), F32, 6)]),
        after=(da,),
    )
    red_mix = reduce_a("mix", [W_O, W_A, W_B], [gw_o, gw_a, gw_b])

    dz, dws, dbs, dgain = _sgu_bwd(z, da, sgu_v_gain, ws_b, wst_b, b_col, w_sgu, dz, after=(red_mix["token"],))
    dz, dk_pad, dv_pad, dbias_tab, dsink = _attn_bwd(sink, z, k_pad, v_pad, bias_tab, datt, dz, grp, q_blk0)
    dz = _dkv_to_dz(dk_pad, dv_pad, dz, off_k // (2 * w_kv))
    drel = _relbias_bwd(dbias_tab, bucket)
    reduce_b(red_mix, dz)
    reduce_c(red_ffn, red_mix["token"])

    small_w = [norm_mix, sgu_v_gain, sgu_w_s, sgu_b_s, attn_sink, rel_bias, norm_ffn, norm_final]
    small_m = [m_norm_mix, m_sgu_v_gain, m_sgu_w_s, m_sgu_b_s, m_attn_sink, m_rel_bias, m_norm_ffn, m_norm_final]
    small_v = [v_norm_mix, v_sgu_v_gain, v_sgu_w_s, v_sgu_b_s, v_attn_sink, v_rel_bias, v_norm_ffn, v_norm_final]
    small_shapes = [w.shape for w in small_w]
    early = [dgain, dws, dbs, dsink[:, 0], drel[:, :REL_BUCKETS].T, dg_ffn, dg_final]
    p_early = _pack([g.reshape(shp) for g, shp in zip(early, small_shapes[1:])] + [loss_part[0, :1]])
    land = jnp.zeros((2 * N_CHIPS,) + p_early.shape, F32)
    sm_send, sm_recv, (p_early, land) = _split_start("small_send", [p_early, land], 2 * N_CHIPS - 1, _small_exchange_copies(False))

    tzn = _tile(n_in, (768, 640, 512))
    gw_in = _mm(
        "mm_gw_in", (d // twr, n_in // tzn, 1), [h1, dz],
        [pl.BlockSpec((s, twr), lambda i, j, k: (0, i)), pl.BlockSpec((s, tzn), lambda i, j, k: (0, j))],
        [jax.ShapeDtypeStruct((d, n_in), BF16)], [pl.BlockSpec((twr, tzn), lambda i, j, k: (i, j))],
        [(0, 1, TN, 0)], 1, (twr, tzn), 1, ep_store,
        _mm_vmem([((s, twr), BF16, 3), ((s, tzn), BF16, 2), ((twr, tzn), F32, 3)]),
        after=(red_ffn["token"], p_early),
    )[0]
    red_in = reduce_a("w_in", [W_IN], [gw_in])

    reduce_c(red_mix, red_in["token"])
    reduce_b(red_in, finish(red_mix, red_in["token"]))

    dh1 = _mm(
        "mm_dh1", (s // tm, d // tn2, 1), [dz, g_in],
        [pl.BlockSpec((tm, n_in), lambda i, j, k: (i, 0)), pl.BlockSpec((tn2, n_in), lambda i, j, k: (j, 0))],
        [jax.ShapeDtypeStruct((s, d), F32)], [pl.BlockSpec((tm, tn2), lambda i, j, k: (i, j))],
        [(0, 1, NT, 0)], 1, (tm, tn2), 1, ep_store,
        _mm_vmem([((tm, n_in), BF16, 2), ((tn2, n_in), BF16, 2), ((tm, tn2), F32, 5)]),
        after=(red_in["token"],),
    )[0]

    grad_x, _, dg_mix = _rms_bwd("rms_mix_bwd", x2d, norm_mix, dh1, dx2)

    p_mix = _pack([dg_mix.reshape(small_shapes[0])])
    land_mix = jnp.zeros((2 * N_CHIPS,) + p_mix.shape, F32)
    mx_send, mx_recv, (p_mix, land_mix) = _split_start("mix_send", [p_mix, land_mix], 2 * N_CHIPS - 1, _small_exchange_copies(False))

    reduce_c(red_in, finish(red_ffn, p_mix))
    p_early, land = _split_wait("small_wait", [p_early, land], sm_send, sm_recv, _small_exchange_copies(True), red_in["token"])
    p_mix, land_mix = _split_wait("mix_wait", [p_mix, land_mix], mx_send, mx_recv, _small_exchange_copies(True), p_early)
    me_idx = 2 * q_idx + c_idx
    packed_g = jnp.concatenate([_small_sum("mix_sum", me_idx, p_mix, land_mix), _small_sum("small_sum", me_idx, p_early, land)], axis=0)
    g_small = _unpack(packed_g, small_shapes + [(1,)])
    loss = g_small[-1].reshape(())
    g_small = g_small[:-1]
    zero1 = jnp.zeros((1,), F32)
    pw, pg, pm, pv = _pack(small_w + [zero1]), _pack(g_small + [zero1]), _pack(small_m + [zero1]), _pack(small_v + [zero1])
    small_upd = _adamw("adamw_small", pw, pg, pm, pv)
    d_small, nm_small, nv_small = [_unpack(a, small_shapes) for a in small_upd[:3]]
    finish(red_in, small_upd[0])

    small_names = ["norm_mix", "sgu_v_gain", "sgu_w_s", "sgu_b_s", "attn_sink", "rel_bias", "norm_ffn", "norm_final"]
    table = {}
    for i, n in enumerate(names):
        table[n] = (grads_big[i][None], upd[i][0][None], upd[i][1][None], upd[i][2][None])
    for i, n in enumerate(small_names):
        table[n] = (g_small[i], d_small[i], nm_small[i], nv_small[i])
    order = ["w_in", "norm_mix", "sgu_v_gain", "sgu_w_s", "sgu_b_s", "w_a", "attn_sink", "rel_bias", "w_b", "w_o", "norm_ffn",
             "w_gate", "w_up", "w_down", "norm_final"]
    outs = [loss, grad_x.reshape(1, s, d)]
    for part in range(4):
        outs += [table[n][part] for n in order]
    return tuple(outs)
```

```python
import math

import jax
import jax.numpy as jnp
import numpy as np
from jax import lax
from jax.experimental import pallas as pl
from jax.experimental.pallas import tpu as pltpu

F32 = jnp.float32
BF16 = jnp.bfloat16
I32 = jnp.int32
MESH = pl.DeviceIdType.MESH

EPS = 1e-6
NEG = -1e30
BLK = 128
HEAD_DIM = 128
N_KV_HEADS = 2
REL_BUCKETS = 32
REL_MAX_DIST = 128
N_CHIPS = 4
ADAM_LR, ADAM_B1, ADAM_B2, ADAM_EPS, ADAM_WD, ADAM_STEP = 0.001, 0.9, 0.999, 1e-08, 0.01, 10

LANES = 128
VMEM_CAP = 60 * 1024 * 1024

NN = (((1,), (0,)), ((), ()))
NT = (((1,), (1,)), ((), ()))
TN = (((0,), (0,)), ((), ()))
ANY = pl.BlockSpec(memory_space=pl.ANY)
HBM_SPEC = pl.BlockSpec(memory_space=pltpu.HBM)
SEM_SPEC = pl.BlockSpec(memory_space=pltpu.SEMAPHORE)
EFFECT = pltpu.SideEffectType.DATAFLOW_SIDE_EFFECTING


def _tile(n, cands):
    for t in cands:
        if n % t == 0:
            return t
    return n


PIN_BYTES = 64 * 1024


def _pin_hbm(a):
    big = hasattr(a, "dtype") and jnp.issubdtype(a.dtype, jnp.floating) and _nbytes(a.shape, a.dtype) >= PIN_BYTES
    return pltpu.with_memory_space_constraint(a, pltpu.HBM) if big else a


def _pallas(body, *, out_shape, **kw):
    def pin(o):
        big = isinstance(o, jax.ShapeDtypeStruct) and jnp.issubdtype(o.dtype, jnp.floating) and _nbytes(o.shape, o.dtype) >= PIN_BYTES
        return pltpu.HBM(o.shape, o.dtype) if big else o

    shapes = type(out_shape)(pin(o) for o in out_shape) if isinstance(out_shape, (list, tuple)) else pin(out_shape)
    call = pl.pallas_call(body, out_shape=shapes, **kw)
    return lambda *args: call(*[_pin_hbm(a) for a in args])


def _params(vmem_bytes=None, **kw):
    if vmem_bytes is not None:
        kw["vmem_limit_bytes"] = int(min(max(vmem_bytes, 32 * 1024 * 1024), VMEM_CAP))
    return pltpu.CompilerParams(**kw)


def _nbytes(shape, dtype):
    return int(np.prod(shape)) * jnp.dtype(dtype).itemsize


def _sigmoid(x):
    return 1.0 / (1.0 + jnp.exp(-x))


_GC = 0.7978845608028654
_GA = 0.044715


def _gelu(x):
    return 0.5 * x * (1.0 + jnp.tanh(_GC * (x + _GA * (x * x * x))))


def _gelu_grad(x):
    t = jnp.tanh(_GC * (x + _GA * (x * x * x)))
    return 0.5 * (1.0 + t) + 0.5 * x * (1.0 - t * t) * (_GC * (1.0 + 3.0 * _GA * (x * x)))


def _bf(v):
    return v if v.dtype == BF16 else v.astype(BF16)


def _mm(name, grid, ins, in_specs, out_shape, out_specs, pairs, n_acc, tile, nk, epilogue, vmem_bytes, after=()):
    assert nk == 1
    n_in, n_out = len(ins) + len(after), len(out_shape)

    def body(*refs):
        in_refs, out_refs = refs[:n_in], refs[n_in : n_in + n_out]
        vals = [None] * n_acc
        for a_i, b_i, dn, acc_i in pairs:
            d = lax.dot_general(_bf(in_refs[a_i][...]), _bf(in_refs[b_i][...]), dn, preferred_element_type=F32)
            vals[acc_i] = d if vals[acc_i] is None else vals[acc_i] + d
        epilogue(in_refs, vals, out_refs, slice(None))

    return _pallas(
        body,
        name=name,
        grid=grid,
        in_specs=list(in_specs) + [ANY] * len(after),
        out_specs=out_specs,
        out_shape=out_shape,
        compiler_params=_params(vmem_bytes),
    )(*ins, *after)


def _put(ref, cs, v):
    ref[:, cs] = v.astype(ref.dtype)


def _mm_vmem(tiles):
    return sum(_nbytes(s, d) * c for s, d, c in tiles) + 4 * 1024 * 1024


def _rows8(v):
    r, d = v.shape
    return v.reshape(r // 8, 8, d).sum(axis=0)


def _rms_fwd(name, x, g, after=()):
    s, d = x.shape
    tm = _tile(s, (256, 128))

    def body(x_ref, g_ref, *rest):
        h_ref = rest[-1]
        xv = x_ref[...]
        r = lax.rsqrt(jnp.mean(xv * xv, axis=-1, keepdims=True) + EPS)
        h_ref[...] = ((xv * r) * g_ref[...]).astype(BF16)

    return _pallas(
        body,
        name=name,
        grid=(s // tm,),
        in_specs=[pl.BlockSpec((tm, d), lambda i: (i, 0)), pl.BlockSpec((1, d), lambda i: (0, 0))] + [ANY] * len(after),
        out_specs=pl.BlockSpec((tm, d), lambda i: (i, 0)),
        out_shape=jax.ShapeDtypeStruct((s, d), BF16),
    )(x, g, *after)


def _rms_bwd(name, x, g, dh, dres, after=()):
    s, d = x.shape
    tm = _tile(s, (256, 128))
    n = s // tm
    n_after = len(after)

    def body(x_ref, g_ref, dh_ref, dres_ref, *rest):
        dx_ref, dxb_ref, dg_ref, acc_ref = rest[n_after:]
        i = pl.program_id(0)
        xv = x_ref[...]
        r = lax.rsqrt(jnp.mean(xv * xv, axis=-1, keepdims=True) + EPS)
        xh = xv * r
        dhv = dh_ref[...]
        dxh = dhv * g_ref[...]
        dx = r * (dxh - xh * jnp.mean(dxh * xh, axis=-1, keepdims=True)) + dres_ref[...]
        dx_ref[...] = dx
        dxb_ref[...] = dx.astype(BF16)
        part = _rows8(dhv * xh)

        @pl.when(i == 0)
        def _():
            acc_ref[...] = part

        @pl.when(i > 0)
        def _():
            acc_ref[...] += part

        @pl.when(i == n - 1)
        def _():
            dg_ref[...] = jnp.sum(acc_ref[...], axis=0, keepdims=True)

    row = pl.BlockSpec((tm, d), lambda i: (i, 0))
    vec = pl.BlockSpec((1, d), lambda i: (0, 0))
    return _pallas(
        body,
        name=name,
        grid=(n,),
        in_specs=[row, vec, row, row] + [ANY] * n_after,
        out_specs=[row, row, vec],
        out_shape=[jax.ShapeDtypeStruct((s, d), F32), jax.ShapeDtypeStruct((s, d), BF16), jax.ShapeDtypeStruct((1, d), F32)],
        scratch_shapes=[pltpu.VMEM((8, d), F32)],
    )(x, g, dh, dres, *after)


def _head(x3, g, target):
    s, d = x3.shape
    tm = _tile(s, (256, 128))
    n = s // tm

    def body(x_ref, g_ref, t_ref, dx_ref, dxb_ref, dg_ref, loss_ref, acc_g, acc_l):
        i = pl.program_id(0)
        xv = x_ref[...]
        gv = g_ref[...]
        r = lax.rsqrt(jnp.mean(xv * xv, axis=-1, keepdims=True) + EPS)
        xh = xv * r
        e = xh * gv - t_ref[...]
        dy = e * (1.0 / d)
        dxh = dy * gv
        dx = r * (dxh - xh * jnp.mean(dxh * xh, axis=-1, keepdims=True))
        dx_ref[...] = dx
        dxb_ref[...] = dx.astype(BF16)
        pg = _rows8(dy * xh)
        plo = _rows8(e * e)

        @pl.when(i == 0)
        def _():
            acc_g[...] = pg
            acc_l[...] = plo

        @pl.when(i > 0)
        def _():
            acc_g[...] += pg
            acc_l[...] += plo

        @pl.when(i == n - 1)
        def _():
            dg_ref[...] = jnp.sum(acc_g[...], axis=0, keepdims=True)
            loss_ref[...] = jnp.full((1, LANES), (0.5 / d) * jnp.sum(acc_l[...]), F32)

    row = pl.BlockSpec((tm, d), lambda i: (i, 0))
    vec = pl.BlockSpec((1, d), lambda i: (0, 0))
    return _pallas(
        body,
        name="head",
        grid=(n,),
        in_specs=[row, vec, row],
        out_specs=[row, row, vec, pl.BlockSpec((1, LANES), lambda i: (0, 0))],
        out_shape=[
            jax.ShapeDtypeStruct((s, d), F32),
            jax.ShapeDtypeStruct((s, d), BF16),
            jax.ShapeDtypeStruct((1, d), F32),
            jax.ShapeDtypeStruct((1, LANES), F32),
        ],
        scratch_shapes=[pltpu.VMEM((8, d), F32), pltpu.VMEM((8, d), F32)],
    )(x3, g, target)


def _sgu_fwd(z, gain, ws_b, b_col, w_sgu):
    s = z.shape[0]
    groups = ws_b.shape[0]

    def body(zu_ref, zv_ref, gain_ref, ws_ref, b_ref, a_ref):
        vv = _gelu(zv_ref[...].astype(F32))
        r = lax.rsqrt(jnp.mean(vv * vv, axis=-1, keepdims=True) + EPS)
        vn = ((vv * r) * gain_ref[...]).astype(BF16)
        u = _gelu(zu_ref[...].astype(F32))
        for g in range(groups):
            sl = slice(g * BLK, (g + 1) * BLK)
            mixed = jnp.dot(ws_ref[g], vn[:, sl], preferred_element_type=F32) + b_ref[g]
            a_ref[:, sl] = (u[:, sl] * mixed).astype(BF16)

    return _pallas(
        body,
        name="sgu_fwd",
        grid=(s // BLK,),
        in_specs=[
            pl.BlockSpec((BLK, w_sgu), lambda c: (c, 0)),
            pl.BlockSpec((BLK, w_sgu), lambda c: (c, 1)),
            pl.BlockSpec((1, w_sgu), lambda c: (0, 0)),
            pl.BlockSpec((groups, BLK, BLK), lambda c: (0, 0, 0)),
            pl.BlockSpec((groups, BLK, 1), lambda c: (0, 0, 0)),
        ],
        out_specs=pl.BlockSpec((BLK, w_sgu), lambda c: (c, 0)),
        out_shape=jax.ShapeDtypeStruct((s, w_sgu), BF16),
    )(z, z, gain, ws_b, b_col)


def _sgu_bwd(z, da, gain, ws_b, wst_b, b_col, w_sgu, dz, after=()):
    s = z.shape[0]
    groups = ws_b.shape[0]
    n = s // BLK
    n_skip = 1 + len(after)

    def body(zu_ref, zv_ref, da_ref, gain_ref, ws_ref, wst_ref, b_ref, *rest):
        dz_ref, dws_ref, dbs_ref, dgain_ref, acc_gain = rest[n_skip:]
        c = pl.program_id(0)
        zu = zu_ref[...].astype(F32)
        zv = zv_ref[...].astype(F32)
        gain_v = gain_ref[...]
        vv = _gelu(zv)
        r = lax.rsqrt(jnp.mean(vv * vv, axis=-1, keepdims=True) + EPS)
        xh = vv * r
        vn = (xh * gain_v).astype(BF16)
        u = _gelu(zu)
        dav = da_ref[...].astype(F32)
        dmix = dav * u
        dmix_b = dmix.astype(BF16)
        dvn_parts = []
        for g in range(groups):
            sl = slice(g * BLK, (g + 1) * BLK)
            mixed = jnp.dot(ws_ref[g], vn[:, sl], preferred_element_type=F32) + b_ref[g]
            dz_ref[:, sl] = (dav[:, sl] * mixed * _gelu_grad(zu[:, sl])).astype(BF16)
            dvn_parts.append(jnp.dot(wst_ref[g], dmix_b[:, sl], preferred_element_type=F32))
            dws_g = lax.dot_general(dmix_b[:, sl], vn[:, sl], NT, preferred_element_type=F32)
            dbs_g = jnp.sum(dmix[:, sl], axis=1, keepdims=True)

            @pl.when(c == 0)
            def _():
                dws_ref[g] = dws_g
                dbs_ref[g] = dbs_g

            @pl.when(c > 0)
            def _():
                dws_ref[g] += dws_g
                dbs_ref[g] += dbs_g

        dvn = jnp.concatenate(dvn_parts, axis=1)
        dxh = dvn * gain_v
        dvv = r * (dxh - xh * jnp.mean(dxh * xh, axis=-1, keepdims=True))
        dz_ref[:, w_sgu:] = (dvv * _gelu_grad(zv)).astype(BF16)
        pg = _rows8(dvn * xh)

        @pl.when(c == 0)
        def _():
            acc_gain[...] = pg

        @pl.when(c > 0)
        def _():
            acc_gain[...] += pg

        @pl.when(c == n - 1)
        def _():
            dgain_ref[...] = jnp.sum(acc_gain[...], axis=0, keepdims=True)

    full3 = pl.BlockSpec((groups, BLK, BLK), lambda c: (0, 0, 0))
    col3 = pl.BlockSpec((groups, BLK, 1), lambda c: (0, 0, 0))
    vec = pl.BlockSpec((1, w_sgu), lambda c: (0, 0))
    return _pallas(
        body,
        name="sgu_bwd",
        grid=(n,),
        in_specs=[
            pl.BlockSpec((BLK, w_sgu), lambda c: (c, 0)),
            pl.BlockSpec((BLK, w_sgu), lambda c: (c, 1)),
            pl.BlockSpec((BLK, w_sgu), lambda c: (c, 0)),
            vec,
            full3,
            full3,
            col3,
            ANY,
        ]
        + [ANY] * len(after),
        out_specs=[pl.BlockSpec((BLK, 2 * w_sgu), lambda c: (c, 0)), full3, col3, vec],
        out_shape=[
            jax.ShapeDtypeStruct(dz.shape, BF16),
            jax.ShapeDtypeStruct((groups, BLK, BLK), F32),
            jax.ShapeDtypeStruct((groups, BLK, 1), F32),
            jax.ShapeDtypeStruct((1, w_sgu), F32),
        ],
        scratch_shapes=[pltpu.VMEM((8, w_sgu), F32)],
        input_output_aliases={7: 0},
    )(z, z, da, gain, ws_b, wst_b, b_col, dz, *after)


def _attn_softmax(sink_ref, q_ref, k_ref, v_ref, bias_ref, s_len, grp):
    kv = pl.program_id(0)
    n = pl.program_id(1)
    start = pl.multiple_of(n * BLK, BLK)
    kb = k_ref[pl.ds(start, 3 * BLK), :]
    vb = v_ref[pl.ds(start, 3 * BLK), :]
    qv = q_ref[...]
    qs = jnp.concatenate([qv[:, g * HEAD_DIM : (g + 1) * HEAD_DIM] for g in range(grp)], axis=0).astype(BF16)
    sc = lax.dot_general(qs, kb, NT, preferred_element_type=F32) * (HEAD_DIM**-0.5)
    sc = sc + bias_ref[...].reshape(grp * BLK, 3 * BLK)
    kpos = start + lax.broadcasted_iota(I32, (1, 3 * BLK), 1) - BLK
    sc = jnp.where((kpos >= 0) & (kpos < s_len), sc, NEG)
    sink = jnp.concatenate([jnp.full((BLK, 1), sink_ref[kv * grp + g], F32) for g in range(grp)], axis=0)
    m = jnp.maximum(jnp.max(sc, axis=-1, keepdims=True), sink)
    p = jnp.exp(sc - m)
    esink = jnp.exp(sink - m)
    den = jnp.sum(p, axis=-1, keepdims=True) + esink
    return start, qs, kb, vb, p / den, esink / den


def _attn_specs(s, grp, q_blk0):
    qw = grp * HEAD_DIM
    return [
        pl.BlockSpec(memory_space=pltpu.SMEM),
        pl.BlockSpec((BLK, qw), lambda kv, n: (n, q_blk0 + kv)),
        pl.BlockSpec((s + 2 * BLK, HEAD_DIM), lambda kv, n: (0, kv)),
        pl.BlockSpec((s + 2 * BLK, HEAD_DIM), lambda kv, n: (0, kv)),
        pl.BlockSpec((grp, BLK, 3 * BLK), lambda kv, n: (kv, 0, 0)),
    ]


def _attn_fwd(sink, z, k_pad, v_pad, bias_tab, grp, q_blk0):
    s = z.shape[0]
    qw = grp * HEAD_DIM

    def body(sink_ref, q_ref, k_ref, v_ref, bias_ref, o_ref):
        _, _, _, vb, pn, _ = _attn_softmax(sink_ref, q_ref, k_ref, v_ref, bias_ref, s, grp)
        o = jnp.dot(pn.astype(BF16), vb, preferred_element_type=F32)
        for g in range(grp):
            o_ref[:, g * HEAD_DIM : (g + 1) * HEAD_DIM] = o[g * BLK : (g + 1) * BLK].astype(BF16)

    return _pallas(
        body,
        name="attn_fwd",
        grid=(N_KV_HEADS, s // BLK),
        in_specs=_attn_specs(s, grp, q_blk0),
        out_specs=pl.BlockSpec((BLK, qw), lambda kv, n: (n, kv)),
        out_shape=jax.ShapeDtypeStruct((s, N_KV_HEADS * qw), BF16),
    )(sink, z, k_pad, v_pad, bias_tab)


def _attn_bwd(sink, z, k_pad, v_pad, bias_tab, dout, dz, grp, q_blk0):
    s = z.shape[0]
    qw = grp * HEAD_DIM
    nb = s // BLK
    heads = N_KV_HEADS * grp

    def body(sink_ref, q_ref, k_ref, v_ref, bias_ref, do_ref, dz_in, dq_ref, dk_ref, dv_ref, dbias_ref, dsink_ref, dk_acc, dv_acc):
        del dz_in
        kv = pl.program_id(0)
        n = pl.program_id(1)
        start, qs, kb, vb, pn, psink = _attn_softmax(sink_ref, q_ref, k_ref, v_ref, bias_ref, s, grp)
        dov = do_ref[...]
        dos = jnp.concatenate([dov[:, g * HEAD_DIM : (g + 1) * HEAD_DIM] for g in range(grp)], axis=0)
        dp = lax.dot_general(dos, vb, NT, preferred_element_type=F32)
        dvb = lax.dot_general(pn.astype(BF16), dos, TN, preferred_element_type=F32)
        delta = jnp.sum(pn * dp, axis=-1, keepdims=True)
        ds = pn * (dp - delta)
        dsb = (ds * (HEAD_DIM**-0.5)).astype(BF16)
        dq = jnp.dot(dsb, kb, preferred_element_type=F32)
        dkb = lax.dot_general(dsb, qs, TN, preferred_element_type=F32)
        for g in range(grp):
            dq_ref[:, g * HEAD_DIM : (g + 1) * HEAD_DIM] = dq[g * BLK : (g + 1) * BLK].astype(BF16)

        @pl.when(n == 0)
        def _():
            dk_acc[...] = jnp.zeros_like(dk_acc)
            dv_acc[...] = jnp.zeros_like(dv_acc)
            dbias_ref[...] = jnp.zeros_like(dbias_ref)

        @pl.when((n == 0) & (kv == 0))
        def _():
            dsink_ref[...] = jnp.zeros_like(dsink_ref)

        dk_acc[pl.ds(start, 3 * BLK), :] += dkb
        dv_acc[pl.ds(start, 3 * BLK), :] += dvb
        dbias_ref[...] += ds.reshape(grp, BLK, 3 * BLK)
        row = lax.broadcasted_iota(I32, (heads, LANES), 0)
        sd = psink * delta
        upd = jnp.zeros((heads, LANES), F32)
        for g in range(grp):
            upd = jnp.where(row == kv * grp + g, -jnp.sum(sd[g * BLK : (g + 1) * BLK]), upd)
        dsink_ref[...] += upd

        @pl.when(n == nb - 1)
        def _():
            dk_ref[...] = dk_acc[...]
            dv_ref[...] = dv_acc[...]

    pad_spec = pl.BlockSpec((s + 2 * BLK, HEAD_DIM), lambda kv, n: (0, kv))
    kvw = N_KV_HEADS * HEAD_DIM
    return _pallas(
        body,
        name="attn_bwd",
        grid=(N_KV_HEADS, nb),
        in_specs=_attn_specs(s, grp, q_blk0) + [pl.BlockSpec((BLK, qw), lambda kv, n: (n, kv)), ANY],
        out_specs=[
            pl.BlockSpec((BLK, qw), lambda kv, n: (n, q_blk0 + kv)),
            pad_spec,
            pad_spec,
            pl.BlockSpec((grp, BLK, 3 * BLK), lambda kv, n: (kv, 0, 0)),
            pl.BlockSpec((heads, LANES), lambda kv, n: (0, 0)),
        ],
        out_shape=[
            jax.ShapeDtypeStruct(dz.shape, BF16),
            jax.ShapeDtypeStruct((s + 2 * BLK, kvw), F32),
            jax.ShapeDtypeStruct((s + 2 * BLK, kvw), F32),
            jax.ShapeDtypeStruct((heads, BLK, 3 * BLK), F32),
            jax.ShapeDtypeStruct((heads, LANES), F32),
        ],
        scratch_shapes=[pltpu.VMEM((s + 2 * BLK, HEAD_DIM), F32), pltpu.VMEM((s + 2 * BLK, HEAD_DIM), F32)],
        input_output_aliases={6: 0},
    )(sink, z, k_pad, v_pad, bias_tab, dout, dz)


def _dkv_to_dz(dk_pad, dv_pad, dz, blk_idx):
    s = dz.shape[0]
    kvw = dk_pad.shape[1]

    def body(dk_ref, dv_ref, dz_in, out_ref):
        del dz_in
        out_ref[:, :kvw] = dk_ref[...].astype(BF16)
        out_ref[:, kvw:] = dv_ref[...].astype(BF16)

    src = pl.BlockSpec((BLK, kvw), lambda i: (i + 1, 0))
    return _pallas(
        body,
        name="dkv_to_dz",
        grid=(s // BLK,),
        in_specs=[src, src, ANY],
        out_specs=pl.BlockSpec((BLK, 2 * kvw), lambda i: (i, blk_idx)),
        out_shape=jax.ShapeDtypeStruct(dz.shape, BF16),
        input_output_aliases={2: 0},
    )(dk_pad, dv_pad, dz)


def _relbias_bwd(dbias_tab, bucket):
    heads = dbias_tab.shape[0]

    def body(dt_ref, bk_ref, out_ref):
        lane = lax.broadcasted_iota(I32, (1, LANES), 1)
        bk = bk_ref[...]
        rows = []
        for h in range(heads):
            dt = dt_ref[h]
            acc = jnp.zeros((1, LANES), F32)
            for b in range(REL_BUCKETS):
                acc = jnp.where(lane == b, jnp.sum(jnp.where(bk == b, dt, 0.0)), acc)
            rows.append(acc)
        out_ref[...] = jnp.concatenate(rows, axis=0)

    return _pallas(body, name="relbias_bwd", out_shape=jax.ShapeDtypeStruct((heads, LANES), F32))(dbias_tab, bucket)


def _t5_bucket(rel):
    nb = REL_BUCKETS // 2
    ret = jnp.where(rel > 0, nb, 0)
    n = jnp.abs(rel)
    max_exact = nb // 2
    nf = jnp.maximum(n, 1).astype(F32)
    large = max_exact + (jnp.log(nf / max_exact) / math.log(REL_MAX_DIST / max_exact) * (nb - max_exact)).astype(I32)
    large = jnp.minimum(large, nb - 1)
    return ret + jnp.where(n < max_exact, n, large)


def _band_tables(rel_bias):
    qi = jnp.arange(BLK)[:, None]
    kj = jnp.arange(3 * BLK)[None, :]
    rel = kj - BLK - qi
    bucket = _t5_bucket(rel).astype(I32)
    heads = rel_bias.shape[1]
    masked = jnp.where(jnp.abs(rel) <= BLK, bucket, -1)

    def body(rb_ref, bk_ref, out_ref):
        bk = bk_ref[...]
        for h in range(heads):
            tab = jnp.full(bk.shape, NEG, F32)
            for b in range(REL_BUCKETS):
                tab = jnp.where(bk == b, rb_ref[b, h], tab)
            out_ref[h] = tab

    bias_tab = _pallas(
        body,
        name="bias_table",
        in_specs=[pl.BlockSpec(memory_space=pltpu.SMEM), pl.BlockSpec(memory_space=pltpu.VMEM)],
        out_specs=pl.BlockSpec(memory_space=pltpu.VMEM),
        out_shape=jax.ShapeDtypeStruct((heads, BLK, 3 * BLK), F32),
    )(rel_bias.astype(F32), masked)
    return bias_tab, bucket


EW_BLOCK_ELEMS = 512 * 1024


def _ew_tiles(shape, elems=EW_BLOCK_ELEMS // 2):
    r, c = shape
    tn = c if c <= 2048 else _tile(c, (2048, 1920, 1536, 1408, 1024, 512))
    tm = _tile(r, [t for t in (1024, 512, 256, 128, 64, 32, 16, 8) if t * tn <= elems] or [8])
    return tm, tn


def _cast_into_full(name, qidx, w, kind, after=()):
    r, c = w.shape
    tm, tn = _ew_tiles(w.shape, EW_BLOCK_ELEMS)
    nbi, nbj = r // tm, c // tn
    if kind == "col":
        full, out_spec = (r, c * N_CHIPS), pl.BlockSpec((tm, tn), lambda i, j, q: (i, q[0] * nbj + j))
    else:
        full, out_spec = (r * N_CHIPS, c), pl.BlockSpec((tm, tn), lambda i, j, q: (q[0] * nbi + i, j))

    def body(q_ref, w_ref, *rest):
        del q_ref
        rest[-1][...] = w_ref[...].astype(BF16)

    return _pallas(
        body,
        name=name,
        grid_spec=pltpu.PrefetchScalarGridSpec(
            num_scalar_prefetch=1,
            grid=(nbi, nbj),
            in_specs=[pl.BlockSpec((tm, tn), lambda i, j, q: (i, j))] + [ANY] * len(after),
            out_specs=out_spec,
        ),
        out_shape=jax.ShapeDtypeStruct(full, BF16),
    )(qidx, w, *after)


def _adamw(name, w, g, m, v, after=()):
    tm, tn = _ew_tiles(w.shape, EW_BLOCK_ELEMS)
    if _nbytes(w.shape, F32) <= 1024 * 1024:
        tm, tn = w.shape
    spec = pl.BlockSpec((tm, tn), lambda i, j: (i, j))
    n_after = len(after)

    def body(w_ref, g_ref, m_ref, v_ref, *rest):
        d_ref, nm_ref, nv_ref, g_out_ref = rest[n_after:]
        gv = g_ref[...]
        g_out_ref[...] = gv
        nm = ADAM_B1 * m_ref[...] + (1.0 - ADAM_B1) * gv
        nv = ADAM_B2 * v_ref[...] + (1.0 - ADAM_B2) * (gv * gv)
        m_hat = nm / (1.0 - ADAM_B1**ADAM_STEP)
        v_hat = nv / (1.0 - ADAM_B2**ADAM_STEP)
        d_ref[...] = -ADAM_LR * (m_hat / (jnp.sqrt(v_hat) + ADAM_EPS) + ADAM_WD * w_ref[...])
        nm_ref[...] = nm
        nv_ref[...] = nv

    out = jax.ShapeDtypeStruct(w.shape, F32)
    return _pallas(
        body, name=name, grid=(w.shape[0] // tm, w.shape[1] // tn), in_specs=[spec] * 4 + [ANY] * n_after,
        out_specs=[spec] * 4, out_shape=[out, out, out, out],
        compiler_params=_params(_mm_vmem([((tm, tn), F32, 24)])),
    )(w, g, m, v, *after)


def _pair_add(name, cidx, g_full, r_sib, kind):
    hr, hc = r_sib.shape
    tm, tn = _ew_tiles((hr, hc), 2 * EW_BLOCK_ELEMS)
    nbi, nbj = hr // tm, hc // tn
    if kind == "col":
        g_spec = pl.BlockSpec((tm, tn), lambda i, j, c: (c[0] * nbi + i, j))
    else:
        g_spec = pl.BlockSpec((tm, tn), lambda i, j, c: (i, c[0] * nbj + j))
    spec = pl.BlockSpec((tm, tn), lambda i, j, c: (i, j))

    def body(c_ref, g_ref, r_ref, o_ref):
        del c_ref
        o_ref[...] = (g_ref[...].astype(F32) + r_ref[...].astype(F32)).astype(BF16)

    return _pallas(
        body,
        name=name,
        grid_spec=pltpu.PrefetchScalarGridSpec(num_scalar_prefetch=1, grid=(nbi, nbj), in_specs=[g_spec, spec], out_specs=spec),
        out_shape=jax.ShapeDtypeStruct((hr, hc), BF16),
        compiler_params=_params(_mm_vmem([((tm, tn), BF16, 6), ((tm, tn), F32, 3)])),
    )(cidx, g_full, r_sib)


def _chip_sum(name, qidx, c_half, r_ici, kind):
    _, pr, pc = r_ici.shape
    tm, tn = _ew_tiles((pr, pc), 2 * EW_BLOCK_ELEMS)
    nbi, nbj = pr // tm, pc // tn
    if kind == "col":
        own_spec = pl.BlockSpec((tm, tn), lambda i, j, q: (i, q[0] * nbj + j))
        full, out_spec = (2 * pr, pc), pl.BlockSpec((tm, tn), lambda i, j, q: (q[1] * nbi + i, j))
    else:
        own_spec = pl.BlockSpec((tm, tn), lambda i, j, q: (q[0] * nbi + i, j))
        full, out_spec = (pr, 2 * pc), pl.BlockSpec((tm, tn), lambda i, j, q: (i, q[1] * nbj + j))

    def body(q_ref, own_ref, r_ref, o_ref):
        q = q_ref[0]
        own = own_ref[...].astype(F32)
        recv = [r_ref[r].astype(F32) for r in range(3)]
        total = None
        for chip in range(N_CHIPS):
            d = chip ^ q
            term = jnp.where(d == 0, own, jnp.where(d == 2, recv[0], jnp.where(d == 1, recv[1], recv[2])))
            total = term if total is None else total + term
        o_ref[...] = total

    return _pallas(
        body,
        name=name,
        grid_spec=pltpu.PrefetchScalarGridSpec(
            num_scalar_prefetch=1,
            grid=(nbi, nbj),
            in_specs=[own_spec, pl.BlockSpec((3, tm, tn), lambda i, j, q: (0, i, j))],
            out_specs=out_spec,
        ),
        out_shape=jax.ShapeDtypeStruct(full, F32),
        compiler_params=_params(_mm_vmem([((tm, tn), BF16, 8), ((tm, tn), F32, 6)])),
    )(qidx, c_half, r_ici)


_REL_MASK = (2, 1, 3)


def _place():
    x, y, c = lax.axis_index("x"), lax.axis_index("y"), lax.axis_index("c")
    chips = [(1 - x, y), (x, 1 - y), (1 - x, 1 - y)]
    return x, y, c, 2 * x + y, chips


def _shard_view(ref, kind, chip):
    if kind == "col":
        w = ref.shape[1] // N_CHIPS
        return ref.at[:, pl.ds(pl.multiple_of(chip * w, LANES), w)]
    h = ref.shape[0] // N_CHIPS
    return ref.at[pl.ds(pl.multiple_of(chip * h, 16), h), :]


def _row_half(ref, half):
    h = ref.shape[0] // 2
    return ref.at[pl.ds(pl.multiple_of(half * h, 16), h), :]


def _pair_half(ref, kind, half):
    if kind == "col":
        return _row_half(ref, half)
    w = ref.shape[1] // 2
    return ref.at[:, pl.ds(pl.multiple_of(half * w, LANES), w)]


def _remote(src, dst, send_sem, recv_sem, dev):
    return pltpu.make_async_remote_copy(src_ref=src, dst_ref=dst, send_sem=send_sem, recv_sem=recv_sem, device_id=dev, device_id_type=MESH)


def _hbm(a):
    return pltpu.with_memory_space_constraint(a, pltpu.HBM)


def _gather_start(name, fulls, kinds, rels=(0, 1, 2), after=()):
    n_w = len(fulls)

    def body(*refs):
        g = refs[:n_w]
        send_sem, recv_sem = refs[n_w + len(after)], refs[n_w + len(after) + 1]
        token = refs[-1]
        _, _, c, q, chips = _place()
        for w in range(n_w):
            mine = _row_half(_shard_view(g[w], kinds[w], q), c)
            for r in rels if isinstance(rels, tuple) else rels[w]:
                _remote(mine, mine, send_sem.at[3 * w + r], recv_sem.at[3 * w + r], (*chips[r], c)).start()
        token[...] = jnp.zeros_like(token)

    res = _pallas(
        body,
        name=name,
        out_shape=(
            pltpu.SemaphoreType.DMA((3 * n_w,)),
            pltpu.SemaphoreType.DMA((3 * n_w,)),
            *[pltpu.HBM(f.shape, f.dtype) for f in fulls],
            jax.ShapeDtypeStruct((8, LANES), F32),
        ),
        in_specs=[HBM_SPEC] * n_w + [ANY] * len(after),
        out_specs=(SEM_SPEC, SEM_SPEC, *[HBM_SPEC] * n_w, pl.BlockSpec(memory_space=pltpu.VMEM)),
        input_output_aliases={w: w + 2 for w in range(n_w)},
        compiler_params=pltpu.CompilerParams(has_side_effects=EFFECT),
    )(*[_hbm(f) for f in fulls], *after)
    return res[0], res[1], list(res[2 : 2 + n_w]), res[-1]


def _relay_copies(kinds, waiting):
    def copies(refs, send_sem, recv_sem):
        _, _, c, q, chips = _place()
        out = []
        for i, kind in enumerate(kinds):
            for k, (src_rel, dst_rel) in enumerate(((0, 1), (1, 0))):
                held = _row_half(_row_half(_shard_view(refs[i], kind, q ^ _REL_MASK[src_rel]), c), k)
                far = _row_half(_row_half(_shard_view(refs[i], kind, q ^ _REL_MASK[2]), c), k)
                dst = far if waiting else held
                out.append(_remote(held, dst, send_sem.at[2 * i + k], recv_sem.at[2 * i + k], (*chips[dst_rel], c)))
        return out

    return copies


def _gather_wait(name, fulls, kinds, w_ids, send_sem, recv_sem, after, rels=(0, 1, 2)):
    n = len(fulls)

    def body(*refs):
        g = refs[:n]
        s_sem, r_sem = refs[n], refs[n + 1]
        x, y, c, q, _ = _place()
        for i, w in enumerate(w_ids):
            mine = _row_half(_shard_view(g[i], kinds[i], q), c)
            for r in rels:
                landed = _row_half(_shard_view(g[i], kinds[i], q ^ _REL_MASK[r]), c)
                cp = _remote(mine, landed, s_sem.at[3 * w + r], r_sem.at[3 * w + r], (x, y, 1 - c))
                cp.wait_send()
                cp.wait_recv()

    res = _pallas(
        body,
        name=name,
        out_shape=[pltpu.HBM(f.shape, f.dtype) for f in fulls],
        in_specs=[HBM_SPEC] * n + [SEM_SPEC, SEM_SPEC, ANY],
        out_specs=[HBM_SPEC] * n,
        input_output_aliases={i: i for i in range(n)},
        compiler_params=pltpu.CompilerParams(has_side_effects=EFFECT),
    )(*fulls, send_sem, recv_sem, after)
    return list(res)


def _gather_forward(name, fulls, kinds):
    n = len(fulls)

    def body(*refs):
        g = refs[n : 2 * n]
        send, recv = refs[2 * n :]
        x, y, c, q, _ = _place()
        sib = (x, y, 1 - c)
        cps = []
        for i in range(n):
            for r in range(3):
                landed = _row_half(_shard_view(g[i], kinds[i], q ^ _REL_MASK[r]), c)
                cps.append(_remote(landed, landed, send.at[i, r], recv.at[i, r], sib))
        for cp in cps:
            cp.start()
        for i in range(n):
            for r in range(3):
                other = _row_half(_shard_view(g[i], kinds[i], q ^ _REL_MASK[r]), 1 - c)
                _remote(other, other, send.at[i, r], recv.at[i, r], sib).wait_recv()
        for cp in cps:
            cp.wait_send()

    res = _pallas(
        body,
        name=name,
        in_specs=[ANY] * n,
        out_specs=[ANY] * n,
        out_shape=[jax.ShapeDtypeStruct(f.shape, f.dtype) for f in fulls],
        scratch_shapes=[pltpu.SemaphoreType.DMA((n, 3)), pltpu.SemaphoreType.DMA((n, 3))],
        input_output_aliases={i: i for i in range(n)},
    )(*fulls)
    return list(res)


def _split_start(name, bufs, n_sems, copies):
    n = len(bufs)

    def body(*refs):
        for cp in copies(refs[:n], refs[n], refs[n + 1]):
            cp.start()

    res = _pallas(
        body,
        name=name,
        out_shape=(
            pltpu.SemaphoreType.DMA((n_sems,)),
            pltpu.SemaphoreType.DMA((n_sems,)),
            *[pltpu.HBM(b.shape, b.dtype) for b in bufs],
        ),
        in_specs=[HBM_SPEC] * n,
        out_specs=(SEM_SPEC, SEM_SPEC, *[HBM_SPEC] * n),
        input_output_aliases={i: i + 2 for i in range(n)},
        compiler_params=pltpu.CompilerParams(has_side_effects=EFFECT),
    )(*[_hbm(b) for b in bufs])
    return res[0], res[1], list(res[2:])


def _split_wait(name, bufs, send_sem, recv_sem, copies, after):
    n = len(bufs)

    def body(*refs):
        for cp in copies(refs[:n], refs[n], refs[n + 1]):
            cp.wait_send()
            cp.wait_recv()

    res = _pallas(
        body,
        name=name,
        out_shape=[pltpu.HBM(b.shape, b.dtype) for b in bufs],
        in_specs=[HBM_SPEC] * n + [SEM_SPEC, SEM_SPEC, ANY],
        out_specs=[HBM_SPEC] * n,
        input_output_aliases={i: i for i in range(n)},
        compiler_params=pltpu.CompilerParams(has_side_effects=EFFECT),
    )(*bufs, send_sem, recv_sem, after)
    return list(res)


def _pair_exchange_copies(kinds):
    n = len(kinds)

    def copies(refs, send_sem, recv_sem):
        x, y, c, _, _ = _place()
        return [
            _remote(_pair_half(refs[w], kinds[w], 1 - c), refs[n + w], send_sem.at[w], recv_sem.at[w], (x, y, 1 - c))
            for w in range(n)
        ]

    return copies


def _pair_share_copies(kinds, waiting):
    def copies(refs, send_sem, recv_sem):
        x, y, c, _, _ = _place()
        out = []
        for w, kind in enumerate(kinds):
            mine = _pair_half(refs[w], kind, c)
            dst = _pair_half(refs[w], kind, 1 - c) if waiting else mine
            out.append(_remote(mine, dst, send_sem.at[w], recv_sem.at[w], (x, y, 1 - c)))
        return out

    return copies


def _piece_shape(half_shape, kind):
    r, c = half_shape
    return (3, r, c // N_CHIPS) if kind == "col" else (3, r // N_CHIPS, c)


def _chip_send_start(name, halves, kinds):
    n = len(halves)
    lands = [lax.empty(_piece_shape(h.shape, k), BF16) for h, k in zip(halves, kinds)]

    def body(*refs):
        h, land = refs[:n], refs[n : 2 * n]
        send_sem, recv_sem = refs[2 * n], refs[2 * n + 1]
        _, _, c, q, chips = _place()
        for i in range(n):
            for r, chip in enumerate(chips):
                piece = _shard_view(h[i], kinds[i], q ^ _REL_MASK[r])
                _remote(piece, land[i].at[r], send_sem.at[3 * i + r], recv_sem.at[3 * i + r], (*chip, c)).start()

    res = _pallas(
        body,
        name=name,
        out_shape=(
            pltpu.SemaphoreType.DMA((3 * n,)),
            pltpu.SemaphoreType.DMA((3 * n,)),
            *[pltpu.HBM(a.shape, a.dtype) for a in halves],
            *[pltpu.HBM(a.shape, a.dtype) for a in lands],
        ),
        in_specs=[HBM_SPEC] * (2 * n),
        out_specs=(SEM_SPEC, SEM_SPEC, *[HBM_SPEC] * (2 * n)),
        input_output_aliases={i: i + 2 for i in range(2 * n)},
        compiler_params=pltpu.CompilerParams(has_side_effects=EFFECT),
    )(*[_hbm(a) for a in halves], *[_hbm(a) for a in lands])
    return res[0], res[1], list(res[2 : 2 + n]), list(res[2 + n :])


def _chip_send_wait(name, halves, lands, kinds, send_sem, recv_sem, after):
    n = len(halves)

    def body(*refs):
        h, land = refs[:n], refs[n : 2 * n]
        s_sem, r_sem = refs[2 * n], refs[2 * n + 1]
        x, y, c, q, _ = _place()
        for i in range(n):
            for r in range(3):
                piece = _shard_view(h[i], kinds[i], q ^ _REL_MASK[r])
                cp = _remote(piece, land[i].at[r], s_sem.at[3 * i + r], r_sem.at[3 * i + r], (x, y, 1 - c))
                cp.wait_send()
                cp.wait_recv()

    res = _pallas(
        body,
        name=name,
        out_shape=[pltpu.HBM(a.shape, a.dtype) for a in halves] + [pltpu.HBM(a.shape, a.dtype) for a in lands],
        in_specs=[HBM_SPEC] * (2 * n) + [SEM_SPEC, SEM_SPEC, ANY],
        out_specs=[HBM_SPEC] * (2 * n),
        input_output_aliases={i: i for i in range(2 * n)},
        compiler_params=pltpu.CompilerParams(has_side_effects=EFFECT),
    )(*halves, *lands, send_sem, recv_sem, after)
    return list(res[:n]), list(res[n:])


def _small_exchange_copies(waiting):
    def copies(refs, send_sem, recv_sem):
        p, land = refs
        x, y, c, q, _ = _place()
        me = 2 * q + c
        out = []
        for dd in range(1, 2 * N_CHIPS):
            dev = (x ^ ((dd >> 2) & 1), y ^ ((dd >> 1) & 1), c ^ (dd & 1))
            dst = land.at[me ^ dd] if waiting else land.at[me]
            out.append(_remote(p, dst, send_sem.at[dd - 1], recv_sem.at[dd - 1], dev))
        return out

    return copies


def _small_sum(name, me_idx, p, land):
    rows = p.shape[0]
    n_dev = 2 * N_CHIPS

    def body(me_ref, p_ref, land_ref, o_ref):
        me = me_ref[0]
        total = None
        for dev in range(n_dev):
            term = jnp.where(me == dev, p_ref[...], land_ref[dev])
            total = term if total is None else total + term
        o_ref[...] = total

    return _pallas(
        body,
        name=name,
        grid_spec=pltpu.PrefetchScalarGridSpec(
            num_scalar_prefetch=1,
            grid=(1,),
            in_specs=[pl.BlockSpec((rows, LANES), lambda i, m: (0, 0)), pl.BlockSpec((n_dev, rows, LANES), lambda i, m: (0, 0, 0))],
            out_specs=pl.BlockSpec((rows, LANES), lambda i, m: (0, 0)),
        ),
        out_shape=jax.ShapeDtypeStruct(p.shape, F32),
    )(me_idx, p, land)


def _pack(parts):
    rows = []
    for a in parts:
        flat = a.reshape(-1).astype(F32)
        n = flat.shape[0]
        padded = -(-n // (8 * LANES)) * (8 * LANES)
        rows.append(jnp.pad(flat, (0, padded - n)).reshape(-1, LANES))
    return jnp.concatenate(rows, axis=0)


def _unpack(packed, shapes):
    out, row = [], 0
    for shp in shapes:
        n = int(np.prod(shp))
        nrows = -(-n // (8 * LANES)) * 8
        out.append(packed[row : row + nrows].reshape(-1)[:n].reshape(shp))
        row += nrows
    return out


def kernel(x, w_in, norm_mix, sgu_v_gain, sgu_w_s, sgu_b_s, w_a_out, attn_sink, rel_bias, w_b_out, w_o, norm_ffn, w_gate, w_up, w_down, norm_final, loss_target, m_w_in, m_norm_mix, m_sgu_v_gain, m_sgu_w_s, m_sgu_b_s, m_w_a_out, m_attn_sink, m_rel_bias, m_w_b_out, m_w_o, m_norm_ffn, m_w_gate, m_w_up, m_w_down, m_norm_final, v_w_in, v_norm_mix, v_sgu_v_gain, v_sgu_w_s, v_sgu_b_s, v_w_a_out, v_attn_sink, v_rel_bias, v_w_b_out, v_w_o, v_norm_ffn, v_w_gate, v_w_up, v_w_down, v_norm_final):
    s, d = x.shape[1], x.shape[2]
    w_sgu = sgu_v_gain.shape[1]
    groups = sgu_w_s.shape[1]
    heads = attn_sink.shape[1]
    grp = heads // N_KV_HEADS
    w_att = heads * HEAD_DIM
    w_kv = N_KV_HEADS * HEAD_DIM
    d_ff = w_gate.shape[2] * N_CHIPS
    n_in = w_in.shape[2] * N_CHIPS
    off_q = 2 * w_sgu
    off_k = off_q + w_att
    off_g = off_k + 2 * w_kv
    assert n_in == off_g + 2 * d and groups * BLK == w_sgu and s % BLK == 0

    x2d = x.reshape(s, d)
    tgt = loss_target.reshape(s, d)
    c_idx = lax.axis_index("c").astype(I32).reshape(1)
    q_idx = (2 * lax.axis_index("x") + lax.axis_index("y")).astype(I32).reshape(1)
    qc_idx = jnp.concatenate([q_idx, c_idx])

    W_IN, W_A, W_B, W_O, W_GATE, W_UP, W_DOWN = range(7)
    names = ["w_in", "w_a", "w_b", "w_o", "w_gate", "w_up", "w_down"]
    kinds = ["col", "col", "col", "row", "col", "col", "row"]
    big_w = [w_in[0], w_a_out[0], w_b_out[0], w_o[0], w_gate[0], w_up[0], w_down[0]]
    big_m = [m_w_in[0], m_w_a_out[0], m_w_b_out[0], m_w_o[0], m_w_gate[0], m_w_up[0], m_w_down[0]]
    big_v = [v_w_in[0], v_w_a_out[0], v_w_b_out[0], v_w_o[0], v_w_gate[0], v_w_up[0], v_w_down[0]]
    full_in = _cast_into_full("cast_w_in", q_idx, big_w[W_IN], kinds[W_IN])
    in_send, in_recv, (full_in,), token = _gather_start("gather_start_in", [full_in], [kinds[W_IN]], rels=(0, 1))
    rest = [_cast_into_full("cast_" + names[i], q_idx, big_w[i], kinds[i], after=(token,)) for i in range(1, 7)]
    h1 = _rms_fwd("rms_mix", x2d, norm_mix, after=(token,))
    (full_in,) = _gather_wait("gather_wait_in", [full_in], [kinds[W_IN]], [0], in_send, in_recv, h1, rels=(0, 1))
    relay_send, relay_recv, (full_in,) = _split_start("gather_relay_in", [full_in], 2, _relay_copies([kinds[W_IN]], False))
    relayed = (W_GATE, W_UP)
    rest_rels = [(0, 1) if i in relayed else (0, 1, 2) for i in range(1, 7)]
    ag_send, ag_recv, rest, token = _gather_start("gather_start_rest", rest, kinds[1:], rels=rest_rels, after=(full_in,))
    (full_in,) = _split_wait("gather_relay_wait_in", [full_in], relay_send, relay_recv, _relay_copies([kinds[W_IN]], True), token)
    (g_in,) = _gather_forward("gather_fwd_in", [full_in], [kinds[W_IN]])
    fulls = [g_in] + rest

    def gathered(tag, ids, after):
        got = _gather_wait("gather_wait_" + tag, [fulls[i] for i in ids], [kinds[i] for i in ids], [i - 1 for i in ids],
                           ag_send, ag_recv, after)
        return _gather_forward("gather_fwd_" + tag, got, [kinds[i] for i in ids])

    ws_b = sgu_w_s[0].astype(BF16)
    wst_b = jnp.swapaxes(sgu_w_s[0], 1, 2).astype(BF16)
    b_col = sgu_b_s[0].reshape(groups, BLK, 1)
    bias_tab, bucket = _band_tables(rel_bias)
    sink = attn_sink[0]

    tm = _tile(s, (1024, 512, 256, 128))

    tn = _tile(n_in, (768, 640, 512))
    z = _mm(
        "mm_z", (s // tm, n_in // tn, 1), [h1, g_in],
        [pl.BlockSpec((tm, d), lambda i, j, k: (i, 0)), pl.BlockSpec((d, tn), lambda i, j, k: (0, j))],
        [jax.ShapeDtypeStruct((s, n_in), BF16)], [pl.BlockSpec((tm, tn), lambda i, j, k: (i, j))],
        [(0, 1, NN, 0)], 1, (tm, tn), 1, lambda ins, vals, outs, cs: _put(outs[0], cs, vals[0]),
        _mm_vmem([((tm, d), BF16, 2), ((d, tn), BF16, 2), ((tm, tn), F32, 3)]),
    )[0]
    g_a, g_b, g_o = gathered("mix", [W_A, W_B, W_O], z)

    a_act = _sgu_fwd(z, sgu_v_gain, ws_b, b_col, w_sgu)

    kv_b = z[:, off_k:off_g]
    k_pad = jnp.pad(kv_b[:, :w_kv], ((BLK, BLK), (0, 0)))
    v_pad = jnp.pad(kv_b[:, w_kv:], ((BLK, BLK), (0, 0)))
    q_blk0 = off_q // (grp * HEAD_DIM)
    att = _attn_fwd(sink, z, k_pad, v_pad, bias_tab, grp, q_blk0)

    tg = _tile(d, (512,))
    ga0, gb0 = off_g // tg, (off_g + d) // tg

    def ep_gate(ins, vals, outs, cs):
        sa, sb = _sigmoid(ins[4][:, cs].astype(F32)), _sigmoid(ins[5][:, cs].astype(F32))
        _put(outs[0], cs, sa * vals[0] + sb * vals[1])
        _put(outs[1], cs, vals[0])
        _put(outs[2], cs, vals[1])

    t_out = pl.BlockSpec((tm, tg), lambda i, j, k: (i, j))
    m_act, y_a, y_b = _mm(
        "mm_branches", (s // tm, d // tg, 1), [a_act, g_a, att, g_b, z, z],
        [pl.BlockSpec((tm, w_sgu), lambda i, j, k: (i, 0)), pl.BlockSpec((w_sgu, tg), lambda i, j, k: (0, j)),
         pl.BlockSpec((tm, w_att), lambda i, j, k: (i, 0)), pl.BlockSpec((w_att, tg), lambda i, j, k: (0, j)),
         pl.BlockSpec((tm, tg), lambda i, j, k: (i, ga0 + j)), pl.BlockSpec((tm, tg), lambda i, j, k: (i, gb0 + j))],
        [jax.ShapeDtypeStruct((s, d), BF16)] * 3,
        [t_out, t_out, t_out], [(0, 1, NN, 0), (2, 3, NN, 1)], 2, (tm, tg), 1, ep_gate,
        _mm_vmem([((tm, w_sgu), BF16, 4), ((w_sgu, tg), BF16, 4), ((tm, tg), F32, 12)]),    )

    tn = _tile(d, (1024, 512))

    def ep_residual(ins, vals, outs, cs):
        _put(outs[0], cs, ins[2][:, cs] + vals[0])

    ffn_k = [kinds[i] for i in relayed]
    ffn_bufs = _gather_wait("gather_wait_ffn_in", [fulls[i] for i in relayed], ffn_k, [i - 1 for i in relayed], ag_send, ag_recv,
                            m_act, rels=(0, 1))
    ffn_send, ffn_recv, ffn_bufs = _split_start("gather_relay_ffn_in", ffn_bufs, 2 * len(relayed), _relay_copies(ffn_k, False))

    x2 = _mm(
        "mm_wo", (s // tm, d // tn, 1), [m_act, g_o, x2d],
        [pl.BlockSpec((tm, d), lambda i, j, k: (i, 0)), pl.BlockSpec((d, tn), lambda i, j, k: (0, j)),
         pl.BlockSpec((tm, tn), lambda i, j, k: (i, j))],
        [jax.ShapeDtypeStruct((s, d), F32)], [pl.BlockSpec((tm, tn), lambda i, j, k: (i, j))],
        [(0, 1, NN, 0)], 1, (tm, tn), 1, ep_residual,
        _mm_vmem([((tm, d), BF16, 2), ((d, tn), BF16, 2), ((tm, tn), F32, 5)]),
        after=(ffn_bufs[0],),
    )[0]

    h2 = _rms_fwd("rms_ffn", x2, norm_ffn)
    ffn_bufs = _split_wait("gather_relay_wait_ffn_in", ffn_bufs, ffn_send, ffn_recv, _relay_copies(ffn_k, True), h2)
    g_gate, g_up = _gather_forward("gather_fwd_ffn_in", ffn_bufs, ffn_k)

    tf = _tile(d_ff, (512,))

    def ep_swiglu(ins, vals, outs, cs):
        gt, up = vals
        _put(outs[0], cs, gt)
        _put(outs[1], cs, up)
        _put(outs[2], cs, (gt * _sigmoid(gt)) * up)

    f_out = pl.BlockSpec((tm, tf), lambda i, j, k: (i, j))
    gt, up, f_act = _mm(
        "mm_gate_up", (s // tm, d_ff // tf, 1), [h2, g_gate, g_up],
        [pl.BlockSpec((tm, d), lambda i, j, k: (i, 0)), pl.BlockSpec((d, tf), lambda i, j, k: (0, j)),
         pl.BlockSpec((d, tf), lambda i, j, k: (0, j))],
        [jax.ShapeDtypeStruct((s, d_ff), BF16)] * 3,
        [f_out, f_out, f_out], [(0, 1, NN, 0), (0, 2, NN, 1)], 2, (tm, tf), 1, ep_swiglu,
        _mm_vmem([((tm, d), BF16, 2), ((d, tf), BF16, 4), ((tm, tf), F32, 8)]),    )
    (g_down,) = gathered("ffn_out", [W_DOWN], f_act)

    tkf = _tile(d_ff, (1408, 1024, 512))
    tml, tnl = _tile(s, (512, 256, 128)), _tile(d, (512,))
    x3 = _mm(
        "mm_down", (s // tml, d // tnl, 1), [f_act, g_down, x2],
        [pl.BlockSpec((tml, d_ff), lambda i, j, k: (i, 0)), pl.BlockSpec((d_ff, tnl), lambda i, j, k: (0, j)),
         pl.BlockSpec((tml, tnl), lambda i, j, k: (i, j))],
        [jax.ShapeDtypeStruct((s, d), F32)], [pl.BlockSpec((tml, tnl), lambda i, j, k: (i, j))],
        [(0, 1, NN, 0)], 1, (tml, tnl), 1, ep_residual,
        _mm_vmem([((tml, d_ff), BF16, 2), ((d_ff, tnl), BF16, 2), ((tml, tnl), F32, 6)]),    )[0]

    dx3, dx3b, dg_final, loss_part = _head(x3, norm_final.reshape(1, d), tgt)

    def reduce_a(tag, ids, grads):
        ks = [kinds[i] for i in ids]
        lands = [lax.empty((g.shape[0] // 2, g.shape[1]) if k == "col" else (g.shape[0], g.shape[1] // 2), BF16)
                 for g, k in zip(grads, ks)]
        send, recv, bufs = _split_start("pair_send_" + tag, list(grads) + lands, len(ids), _pair_exchange_copies(ks))
        return {"tag": tag, "ids": ids, "ks": ks, "pair": (send, recv, bufs), "token": bufs[0]}

    def reduce_b(st, after):
        tag, ids, ks = st["tag"], st["ids"], st["ks"]
        send, recv, bufs = st["pair"]
        bufs = _split_wait("pair_wait_" + tag, bufs, send, recv, _pair_exchange_copies(ks), after)
        grads, from_sib = bufs[: len(ids)], bufs[len(ids) :]
        halves = [_pair_add("pair_add_" + names[i], c_idx, g, r, k) for i, g, r, k in zip(ids, grads, from_sib, ks)]
        st["chip"] = _chip_send_start("chip_send_" + tag, halves, ks)
        st["token"] = st["chip"][2][0]

    def reduce_c(st, after):
        tag, ids, ks = st["tag"], st["ids"], st["ks"]
        send, recv, halves, lands = st["chip"]
        halves, lands = _chip_send_wait("chip_wait_" + tag, halves, lands, ks, send, recv, after)
        pieces = [_chip_sum("chip_sum_" + names[i], qc_idx, h, r, k) for i, h, r, k in zip(ids, halves, lands, ks)]
        st["share"] = _split_start("share_send_" + tag, pieces, len(ids), _pair_share_copies(ks, False))
        st["token"] = st["share"][2][0]

    def reduce_d(st, after):
        send, recv, bufs = st["share"]
        return _split_wait("share_wait_" + st["tag"], bufs, send, recv, _pair_share_copies(st["ks"], True), after)

    grads_big, upd = [None] * 7, [None] * 7

    def finish(st, after):
        shared = reduce_d(st, after)
        after = shared[0]
        for i, g in zip(st["ids"], shared):
            upd[i] = _adamw("adamw_" + names[i], big_w[i], g, big_m[i], big_v[i], after=(after,))
            grads_big[i] = upd[i][3]
            after = upd[i][0]
        return after

    def ep_swiglu_bwd(ins, vals, outs, cs):
        df = vals[0]
        gtv, upv = ins[2][:, cs].astype(F32), ins[3][:, cs].astype(F32)
        sg = _sigmoid(gtv)
        _put(outs[0], cs, df * upv * (sg + gtv * sg * (1.0 - sg)))
        _put(outs[1], cs, df * (gtv * sg))

    dgt, dup = _mm(
        "mm_dswiglu", (s // tm, d_ff // tf, 1), [dx3b, g_down, gt, up],
        [pl.BlockSpec((tm, d), lambda i, j, k: (i, 0)), pl.BlockSpec((tf, d), lambda i, j, k: (j, 0)), f_out, f_out],
        [jax.ShapeDtypeStruct((s, d_ff), BF16), jax.ShapeDtypeStruct((s, d_ff), BF16)], [f_out, f_out],
        [(0, 1, NT, 0)], 1, (tm, tf), 1, ep_swiglu_bwd,
        _mm_vmem([((tm, d), BF16, 2), ((tf, d), BF16, 2), ((tm, tf), F32, 8)]),    )

    def ep_store(ins, vals, outs, cs):
        for o, v in zip(outs, vals):
            _put(o, cs, v)

    twn = _tile(d, (1024, 512))
    gw_down = _mm(
        "mm_gw_down", (d_ff // tkf, d // twn, 1), [f_act, dx3b],
        [pl.BlockSpec((s, tkf), lambda i, j, k: (0, i)), pl.BlockSpec((s, twn), lambda i, j, k: (0, j))],
        [jax.ShapeDtypeStruct((d_ff, d), BF16)], [pl.BlockSpec((tkf, twn), lambda i, j, k: (i, j))],
        [(0, 1, TN, 0)], 1, (tkf, twn), 1, ep_store,
        _mm_vmem([((s, tkf), BF16, 3), ((s, twn), BF16, 2), ((tkf, twn), F32, 3)]),
    )[0]
    red_down = reduce_a("down", [W_DOWN], [gw_down])

    tn2 = _tile(d, (256,))
    dh2_specs = [pl.BlockSpec((tm, d_ff), lambda i, j, k: (i, 0)), pl.BlockSpec((tn2, d_ff), lambda i, j, k: (j, 0))]
    dh2_tile = pl.BlockSpec((tm, tn2), lambda i, j, k: (i, j))
    dh2_vmem = _mm_vmem([((tm, d_ff), BF16, 2), ((tn2, d_ff), BF16, 2), ((tm, tn2), F32, 7)])
    dh2 = _mm(
        "mm_dh2_gate", (s // tm, d // tn2, 1), [dgt, g_gate], dh2_specs,
        [jax.ShapeDtypeStruct((s, d), F32)], [dh2_tile], [(0, 1, NT, 0)], 1, (tm, tn2), 1, ep_store, dh2_vmem,
        after=(red_down["token"],),
    )[0]
    dh2 = _mm(
        "mm_dh2_up", (s // tm, d // tn2, 1), [dup, g_up, dh2], dh2_specs + [dh2_tile],
        [jax.ShapeDtypeStruct((s, d), F32)], [dh2_tile], [(0, 1, NT, 0)], 1, (tm, tn2), 1, ep_residual, dh2_vmem,
    )[0]
    reduce_b(red_down, dh2)

    twr = _tile(d, (1024, 512))
    w_tile = pl.BlockSpec((twr, tf), lambda i, j, k: (i, j))
    gw_gate, gw_up = _mm(
        "mm_gw_gate_up", (d // twr, d_ff // tf, 1), [h2, dgt, dup],
        [pl.BlockSpec((s, twr), lambda i, j, k: (0, i)), pl.BlockSpec((s, tf), lambda i, j, k: (0, j)),
         pl.BlockSpec((s, tf), lambda i, j, k: (0, j))],
        [jax.ShapeDtypeStruct((d, d_ff), BF16), jax.ShapeDtypeStruct((d, d_ff), BF16)], [w_tile, w_tile],
        [(0, 1, TN, 0), (0, 2, TN, 1)], 2, (twr, tf), 1, ep_store,
        _mm_vmem([((s, twr), BF16, 3), ((s, tf), BF16, 4), ((twr, tf), F32, 6)]),
        after=(red_down["token"],),
    )
    red_ffn = reduce_a("ffn_in", [W_GATE, W_UP], [gw_gate, gw_up])

    dx2, dx2b, dg_ffn = _rms_bwd("rms_ffn_bwd", x2, norm_ffn, dh2, dx3, after=(red_ffn["token"],))

    nj = d // tg

    def lo(j):
        return jnp.minimum(j, nj - 1)

    def gate_bwd_body(dx_ref, wo_ref, ga_ref, gb_ref, ya_ref, yb_ref, dya_ref, dyb_ref, dz_ref, keep):
        j = pl.program_id(1)

        @pl.when(j < nj)
        def _():
            dm = lax.dot_general(dx_ref[...], wo_ref[...], NT, preferred_element_type=F32)
            sa, sb = _sigmoid(ga_ref[...].astype(F32)), _sigmoid(gb_ref[...].astype(F32))
            dya_ref[...] = (dm * sa).astype(BF16)
            dyb_ref[...] = (dm * sb).astype(BF16)
            dz_ref[...] = (dm * ya_ref[...].astype(F32) * (sa * (1.0 - sa))).astype(BF16)
            keep[lo(j)] = (dm * yb_ref[...].astype(F32) * (sb * (1.0 - sb))).astype(BF16)

        @pl.when(j >= nj)
        def _():
            dz_ref[...] = keep[jnp.maximum(j - nj, 0)]

    t_lo = pl.BlockSpec((tm, tg), lambda i, j: (i, lo(j)))
    dya, dyb, dz = _pallas(
        gate_bwd_body,
        name="mm_dgate",
        grid=(s // tm, 2 * nj),
        in_specs=[
            pl.BlockSpec((tm, d), lambda i, j: (i, 0)),
            pl.BlockSpec((tg, d), lambda i, j: (lo(j), 0)),
            pl.BlockSpec((tm, tg), lambda i, j: (i, ga0 + lo(j))),
            pl.BlockSpec((tm, tg), lambda i, j: (i, gb0 + lo(j))),
            t_lo,
            t_lo,
        ],
        out_specs=[t_lo, t_lo, pl.BlockSpec((tm, tg), lambda i, j: (i, ga0 + j))],
        out_shape=[jax.ShapeDtypeStruct((s, d), BF16), jax.ShapeDtypeStruct((s, d), BF16), jax.ShapeDtypeStruct((s, n_in), BF16)],
        scratch_shapes=[pltpu.VMEM((nj, tm, tg), BF16)],
        compiler_params=_params(_mm_vmem([((tm, d), BF16, 2), ((tg, d), BF16, 2), ((tm, tg), F32, 14), ((nj, tm, tg), BF16, 1)])),
    )(dx2b, g_o, z, z, y_a, y_b)
    reduce_b(red_ffn, dya)
    reduce_c(red_down, red_ffn["token"])

    gw_o = _mm(
        "mm_gw_o", (d // twr, d // twn, 1), [m_act, dx2b],
        [pl.BlockSpec((s, twr), lambda i, j, k: (0, i)), pl.BlockSpec((s, twn), lambda i, j, k: (0, j))],
        [jax.ShapeDtypeStruct((d, d), BF16)], [pl.BlockSpec((twr, twn), lambda i, j, k: (i, j))],
        [(0, 1, TN, 0)], 1, (twr, twn), 1, ep_store,
        _mm_vmem([((s, twr), BF16, 3), ((s, twn), BF16, 2), ((twr, twn), F32, 3)]),
        after=(red_down["token"],),
    )[0]
    after_down = finish(red_down, gw_o)

    tb = _tile(w_sgu, (1024, 512))
    b_out = pl.BlockSpec((tm, tb), lambda i, j, k: (i, j))

    da, datt = _mm(
        "mm_dbranches", (s // tm, w_sgu // tb, 1), [dya, g_a, dyb, g_b],
        [pl.BlockSpec((tm, d), lambda i, j, k: (i, 0)), pl.BlockSpec((tb, d), lambda i, j, k: (j, 0)),
         pl.BlockSpec((tm, d), lambda i, j, k: (i, 0)), pl.BlockSpec((tb, d), lambda i, j, k: (j, 0))],
        [jax.ShapeDtypeStruct((s, w_sgu), BF16), jax.ShapeDtypeStruct((s, w_att), BF16)], [b_out, b_out],
        [(0, 1, NT, 0), (2, 3, NT, 1)], 2, (tm, tb), 1, ep_store,
        _mm_vmem([((tm, d), BF16, 4), ((tb, d), BF16, 4), ((tm, tb), F32, 6)]),
        after=(after_down,),
    )

    wb_tile = pl.BlockSpec((tb, twn), lambda i, j, k: (i, j))
    gw_a, gw_b = _mm(
        "mm_gw_branches", (w_sgu // tb, d // twn, 1), [a_act, dya, att, dyb],
        [pl.BlockSpec((s, tb), lambda i, j, k: (0, i)), pl.BlockSpec((s, twn), lambda i, j, k: (0, j)),
         pl.BlockSpec((s, tb), lambda i, j, k: (0, i)), pl.BlockSpec((s, twn), lambda i, j, k: (0, j))],
        [jax.ShapeDtypeStruct((w_sgu, d), BF16), jax.ShapeDtypeStruct((w_att, d), BF16)], [wb_tile, wb_tile],
        [(0, 1, TN, 0), (2, 3, TN, 1)], 2, (tb, twn), 1, ep_store,
        _mm_vmem([((s, tb), BF16, 5), ((s, twn), BF16, 4), ((tb, twn), F32, 6)]),
        after=(da,),
    )
    red_mix = reduce_a("mix", [W_O, W_A, W_B], [gw_o, gw_a, gw_b])

    dz, dws, dbs, dgain = _sgu_bwd(z, da, sgu_v_gain, ws_b, wst_b, b_col, w_sgu, dz, after=(red_mix["token"],))
    dz, dk_pad, dv_pad, dbias_tab, dsink = _attn_bwd(sink, z, k_pad, v_pad, bias_tab, datt, dz, grp, q_blk0)
    dz = _dkv_to_dz(dk_pad, dv_pad, dz, off_k // (2 * w_kv))
    drel = _relbias_bwd(dbias_tab, bucket)
    reduce_b(red_mix, dz)
    reduce_c(red_ffn, red_mix["token"])

    small_w = [norm_mix, sgu_v_gain, sgu_w_s, sgu_b_s, attn_sink, rel_bias, norm_ffn, norm_final]
    small_m = [m_norm_mix, m_sgu_v_gain, m_sgu_w_s, m_sgu_b_s, m_attn_sink, m_rel_bias, m_norm_ffn, m_norm_final]
    small_v = [v_norm_mix, v_sgu_v_gain, v_sgu_w_s, v_sgu_b_s, v_attn_sink, v_rel_bias, v_norm_ffn, v_norm_final]
    small_shapes = [w.shape for w in small_w]
    early = [dgain, dws, dbs, dsink[:, 0], drel[:, :REL_BUCKETS].T, dg_ffn, dg_final]
    p_early = _pack([g.reshape(shp) for g, shp in zip(early, small_shapes[1:])] + [loss_part[0, :1]])
    land = jnp.zeros((2 * N_CHIPS,) + p_early.shape, F32)
    sm_send, sm_recv, (p_early, land) = _split_start("small_send", [p_early, land], 2 * N_CHIPS - 1, _small_exchange_copies(False))

    tzn = _tile(n_in, (768, 640, 512))
    gw_in = _mm(
        "mm_gw_in", (d // twr, n_in // tzn, 1), [h1, dz],
        [pl.BlockSpec((s, twr), lambda i, j, k: (0, i)), pl.BlockSpec((s, tzn), lambda i, j, k: (0, j))],
        [jax.ShapeDtypeStruct((d, n_in), BF16)], [pl.BlockSpec((twr, tzn), lambda i, j, k: (i, j))],
        [(0, 1, TN, 0)], 1, (twr, tzn), 1, ep_store,
        _mm_vmem([((s, twr), BF16, 3), ((s, tzn), BF16, 2), ((twr, tzn), F32, 3)]),
        after=(red_ffn["token"], p_early),
    )[0]
    red_in = reduce_a("w_in", [W_IN], [gw_in])

    reduce_c(red_mix, red_in["token"])
    reduce_b(red_in, finish(red_mix, red_in["token"]))

    dh1 = _mm(
        "mm_dh1", (s // tm, d // tn2, 1), [dz, g_in],
        [pl.BlockSpec((tm, n_in), lambda i, j, k: (i, 0)), pl.BlockSpec((tn2, n_in), lambda i, j, k: (j, 0))],
        [jax.ShapeDtypeStruct((s, d), F32)], [pl.BlockSpec((tm, tn2), lambda i, j, k: (i, j))],
        [(0, 1, NT, 0)], 1, (tm, tn2), 1, ep_store,
        _mm_vmem([((tm, n_in), BF16, 2), ((tn2, n_in), BF16, 2), ((tm, tn2), F32, 5)]),
        after=(red_in["token"],),
    )[0]

    grad_x, _, dg_mix = _rms_bwd("rms_mix_bwd", x2d, norm_mix, dh1, dx2)

    p_mix = _pack([dg_mix.reshape(small_shapes[0])])
    land_mix = jnp.zeros((2 * N_CHIPS,) + p_mix.shape, F32)
    mx_send, mx_recv, (p_mix, land_mix) = _split_start("mix_send", [p_mix, land_mix], 2 * N_CHIPS - 1, _small_exchange_copies(False))

    reduce_c(red_in, finish(red_ffn, p_mix))
    p_early, land = _split_wait("small_wait", [p_early, land], sm_send, sm_recv, _small_exchange_copies(True), red_in["token"])
    p_mix, land_mix = _split_wait("mix_wait", [p_mix, land_mix], mx_send, mx_recv, _small_exchange_copies(True), p_early)
    me_idx = 2 * q_idx + c_idx
    packed_g = jnp.concatenate([_small_sum("mix_sum", me_idx, p_mix, land_mix), _small_sum("small_sum", me_idx, p_early, land)], axis=0)
    g_small = _unpack(packed_g, small_shapes + [(1,)])
    loss = g_small[-1].reshape(())
    g_small = g_small[:-1]
    zero1 = jnp.zeros((1,), F32)
    pw, pg, pm, pv = _pack(small_w + [zero1]), _pack(g_small + [zero1]), _pack(small_m + [zero1]), _pack(small_v + [zero1])
    small_upd = _adamw("adamw_small", pw, pg, pm, pv)
    d_small, nm_small, nv_small = [_unpack(a, small_shapes) for a in small_upd[:3]]
    finish(red_in, small_upd[0])

    small_names = ["norm_mix", "sgu_v_gain", "sgu_w_s", "sgu_b_s", "attn_sink", "rel_bias", "norm_ffn", "norm_final"]
    table = {}
    for i, n in enumerate(names):
        table[n] = (grads_big[i][None], upd[i][0][None], upd[i][1][None], upd[i][2][None])
    for i, n in enumerate(small_names):
        table[n] = (g_small[i], d_small[i], nm_small[i], nv_small[i])
    order = ["w_in", "norm_mix", "sgu_v_gain", "sgu_w_s", "sgu_b_s", "w_a", "attn_sink", "rel_bias", "w_b", "w_o", "norm_ffn",
             "w_gate", "w_up", "w_down", "norm_final"]
    outs = [loss, grad_x.reshape(1, s, d)]
    for part in range(4):
        outs += [table[n][part] for n in order]
    return tuple(outs)
```

```python
import math

import jax
import jax.numpy as jnp
import numpy as np
from jax import lax
from jax.experimental import pallas as pl
from jax.experimental.pallas import tpu as pltpu

F32 = jnp.float32
BF16 = jnp.bfloat16
I32 = jnp.int32
MESH = pl.DeviceIdType.MESH

EPS = 1e-6
NEG = -1e30
BLK = 128
HEAD_DIM = 128
N_KV_HEADS = 2
REL_BUCKETS = 32
REL_MAX_DIST = 128
N_CHIPS = 4
ADAM_LR, ADAM_B1, ADAM_B2, ADAM_EPS, ADAM_WD, ADAM_STEP = 0.001, 0.9, 0.999, 1e-08, 0.01, 10

LANES = 128
VMEM_CAP = 60 * 1024 * 1024

NN = (((1,), (0,)), ((), ()))
NT = (((1,), (1,)), ((), ()))
TN = (((0,), (0,)), ((), ()))
ANY = pl.BlockSpec(memory_space=pl.ANY)
HBM_SPEC = pl.BlockSpec(memory_space=pltpu.HBM)
SEM_SPEC = pl.BlockSpec(memory_space=pltpu.SEMAPHORE)
EFFECT = pltpu.SideEffectType.DATAFLOW_SIDE_EFFECTING


def _tile(n, cands):
    for t in cands:
        if n % t == 0:
            return t
    return n


PIN_BYTES = 64 * 1024


def _pin_hbm(a):
    big = hasattr(a, "dtype") and jnp.issubdtype(a.dtype, jnp.floating) and _nbytes(a.shape, a.dtype) >= PIN_BYTES
    return pltpu.with_memory_space_constraint(a, pltpu.HBM) if big else a


def _pallas(body, *, out_shape, **kw):
    def pin(o):
        big = isinstance(o, jax.ShapeDtypeStruct) and jnp.issubdtype(o.dtype, jnp.floating) and _nbytes(o.shape, o.dtype) >= PIN_BYTES
        return pltpu.HBM(o.shape, o.dtype) if big else o

    shapes = type(out_shape)(pin(o) for o in out_shape) if isinstance(out_shape, (list, tuple)) else pin(out_shape)
    call = pl.pallas_call(body, out_shape=shapes, **kw)
    return lambda *args: call(*[_pin_hbm(a) for a in args])


def _params(vmem_bytes=None, **kw):
    if vmem_bytes is not None:
        kw["vmem_limit_bytes"] = int(min(max(vmem_bytes, 32 * 1024 * 1024), VMEM_CAP))
    return pltpu.CompilerParams(**kw)


def _nbytes(shape, dtype):
    return int(np.prod(shape)) * jnp.dtype(dtype).itemsize


def _sigmoid(x):
    return 1.0 / (1.0 + jnp.exp(-x))


_GC = 0.7978845608028654
_GA = 0.044715


def _gelu(x):
    return 0.5 * x * (1.0 + jnp.tanh(_GC * (x + _GA * (x * x * x))))


def _gelu_grad(x):
    t = jnp.tanh(_GC * (x + _GA * (x * x * x)))
    return 0.5 * (1.0 + t) + 0.5 * x * (1.0 - t * t) * (_GC * (1.0 + 3.0 * _GA * (x * x)))


def _bf(v):
    return v if v.dtype == BF16 else v.astype(BF16)


def _mm(name, grid, ins, in_specs, out_shape, out_specs, pairs, n_acc, tile, nk, epilogue, vmem_bytes, after=()):
    assert nk == 1
    n_in, n_out = len(ins) + len(after), len(out_shape)

    def body(*refs):
        in_refs, out_refs = refs[:n_in], refs[n_in : n_in + n_out]
        vals = [None] * n_acc
        for a_i, b_i, dn, acc_i in pairs:
            d = lax.dot_general(_bf(in_refs[a_i][...]), _bf(in_refs[b_i][...]), dn, preferred_element_type=F32)
            vals[acc_i] = d if vals[acc_i] is None else vals[acc_i] + d
        epilogue(in_refs, vals, out_refs, slice(None))

    return _pallas(
        body,
        name=name,
        grid=grid,
        in_specs=list(in_specs) + [ANY] * len(after),
        out_specs=out_specs,
        out_shape=out_shape,
        compiler_params=_params(vmem_bytes),
    )(*ins, *after)


def _put(ref, cs, v):
    ref[:, cs] = v.astype(ref.dtype)


def _mm_vmem(tiles):
    return sum(_nbytes(s, d) * c for s, d, c in tiles) + 4 * 1024 * 1024


def _rows8(v):
    r, d = v.shape
    return v.reshape(r // 8, 8, d).sum(axis=0)


def _rms_fwd(name, x, g, after=()):
    s, d = x.shape
    tm = _tile(s, (256, 128))

    def body(x_ref, g_ref, *rest):
        h_ref = rest[-1]
        xv = x_ref[...]
        r = lax.rsqrt(jnp.mean(xv * xv, axis=-1, keepdims=True) + EPS)
        h_ref[...] = ((xv * r) * g_ref[...]).astype(BF16)

    return _pallas(
        body,
        name=name,
        grid=(s // tm,),
        in_specs=[pl.BlockSpec((tm, d), lambda i: (i, 0)), pl.BlockSpec((1, d), lambda i: (0, 0))] + [ANY] * len(after),
        out_specs=pl.BlockSpec((tm, d), lambda i: (i, 0)),
        out_shape=jax.ShapeDtypeStruct((s, d), BF16),
    )(x, g, *after)


def _rms_bwd(name, x, g, dh, dres, after=()):
    s, d = x.shape
    tm = _tile(s, (256, 128))
    n = s // tm
    n_after = len(after)

    def body(x_ref, g_ref, dh_ref, dres_ref, *rest):
        dx_ref, dxb_ref, dg_ref, acc_ref = rest[n_after:]
        i = pl.program_id(0)
        xv = x_ref[...]
        r = lax.rsqrt(jnp.mean(xv * xv, axis=-1, keepdims=True) + EPS)
        xh = xv * r
        dhv = dh_ref[...]
        dxh = dhv * g_ref[...]
        dx = r * (dxh - xh * jnp.mean(dxh * xh, axis=-1, keepdims=True)) + dres_ref[...]
        dx_ref[...] = dx
        dxb_ref[...] = dx.astype(BF16)
        part = _rows8(dhv * xh)

        @pl.when(i == 0)
        def _():
            acc_ref[...] = part

        @pl.when(i > 0)
        def _():
            acc_ref[...] += part

        @pl.when(i == n - 1)
        def _():
            dg_ref[...] = jnp.sum(acc_ref[...], axis=0, keepdims=True)

    row = pl.BlockSpec((tm, d), lambda i: (i, 0))
    vec = pl.BlockSpec((1, d), lambda i: (0, 0))
    return _pallas(
        body,
        name=name,
        grid=(n,),
        in_specs=[row, vec, row, row] + [ANY] * n_after,
        out_specs=[row, row, vec],
        out_shape=[jax.ShapeDtypeStruct((s, d), F32), jax.ShapeDtypeStruct((s, d), BF16), jax.ShapeDtypeStruct((1, d), F32)],
        scratch_shapes=[pltpu.VMEM((8, d), F32)],
    )(x, g, dh, dres, *after)


def _head(x3, g, target):
    s, d = x3.shape
    tm = _tile(s, (256, 128))
    n = s // tm

    def body(x_ref, g_ref, t_ref, dx_ref, dxb_ref, dg_ref, loss_ref, acc_g, acc_l):
        i = pl.program_id(0)
        xv = x_ref[...]
        gv = g_ref[...]
        r = lax.rsqrt(jnp.mean(xv * xv, axis=-1, keepdims=True) + EPS)
        xh = xv * r
        e = xh * gv - t_ref[...]
        dy = e * (1.0 / d)
        dxh = dy * gv
        dx = r * (dxh - xh * jnp.mean(dxh * xh, axis=-1, keepdims=True))
        dx_ref[...] = dx
        dxb_ref[...] = dx.astype(BF16)
        pg = _rows8(dy * xh)
        plo = _rows8(e * e)

        @pl.when(i == 0)
        def _():
            acc_g[...] = pg
            acc_l[...] = plo

        @pl.when(i > 0)
        def _():
            acc_g[...] += pg
            acc_l[...] += plo

        @pl.when(i == n - 1)
        def _():
            dg_ref[...] = jnp.sum(acc_g[...], axis=0, keepdims=True)
            loss_ref[...] = jnp.full((1, LANES), (0.5 / d) * jnp.sum(acc_l[...]), F32)

    row = pl.BlockSpec((tm, d), lambda i: (i, 0))
    vec = pl.BlockSpec((1, d), lambda i: (0, 0))
    return _pallas(
        body,
        name="head",
        grid=(n,),
        in_specs=[row, vec, row],
        out_specs=[row, row, vec, pl.BlockSpec((1, LANES), lambda i: (0, 0))],
        out_shape=[
            jax.ShapeDtypeStruct((s, d), F32),
            jax.ShapeDtypeStruct((s, d), BF16),
            jax.ShapeDtypeStruct((1, d), F32),
            jax.ShapeDtypeStruct((1, LANES), F32),
        ],
        scratch_shapes=[pltpu.VMEM((8, d), F32), pltpu.VMEM((8, d), F32)],
    )(x3, g, target)


def _sgu_fwd(z, gain, ws_b, b_col, w_sgu):
    s = z.shape[0]
    groups = ws_b.shape[0]

    def body(zu_ref, zv_ref, gain_ref, ws_ref, b_ref, a_ref):
        vv = _gelu(zv_ref[...].astype(F32))
        r = lax.rsqrt(jnp.mean(vv * vv, axis=-1, keepdims=True) + EPS)
        vn = ((vv * r) * gain_ref[...]).astype(BF16)
        u = _gelu(zu_ref[...].astype(F32))
        for g in range(groups):
            sl = slice(g * BLK, (g + 1) * BLK)
            mixed = jnp.dot(ws_ref[g], vn[:, sl], preferred_element_type=F32) + b_ref[g]
            a_ref[:, sl] = (u[:, sl] * mixed).astype(BF16)

    return _pallas(
        body,
        name="sgu_fwd",
        grid=(s // BLK,),
        in_specs=[
            pl.BlockSpec((BLK, w_sgu), lambda c: (c, 0)),
            pl.BlockSpec((BLK, w_sgu), lambda c: (c, 1)),
            pl.BlockSpec((1, w_sgu), lambda c: (0, 0)),
            pl.BlockSpec((groups, BLK, BLK), lambda c: (0, 0, 0)),
            pl.BlockSpec((groups, BLK, 1), lambda c: (0, 0, 0)),
        ],
        out_specs=pl.BlockSpec((BLK, w_sgu), lambda c: (c, 0)),
        out_shape=jax.ShapeDtypeStruct((s, w_sgu), BF16),
    )(z, z, gain, ws_b, b_col)


def _sgu_bwd(z, da, gain, ws_b, wst_b, b_col, w_sgu, dz, after=()):
    s = z.shape[0]
    groups = ws_b.shape[0]
    n = s // BLK
    n_skip = 1 + len(after)

    def body(zu_ref, zv_ref, da_ref, gain_ref, ws_ref, wst_ref, b_ref, *rest):
        dz_ref, dws_ref, dbs_ref, dgain_ref, acc_gain = rest[n_skip:]
        c = pl.program_id(0)
        zu = zu_ref[...].astype(F32)
        zv = zv_ref[...].astype(F32)
        gain_v = gain_ref[...]
        vv = _gelu(zv)
        r = lax.rsqrt(jnp.mean(vv * vv, axis=-1, keepdims=True) + EPS)
        xh = vv * r
        vn = (xh * gain_v).astype(BF16)
        u = _gelu(zu)
        dav = da_ref[...].astype(F32)
        dmix = dav * u
        dmix_b = dmix.astype(BF16)
        dvn_parts = []
        for g in range(groups):
            sl = slice(g * BLK, (g + 1) * BLK)
            mixed = jnp.dot(ws_ref[g], vn[:, sl], preferred_element_type=F32) + b_ref[g]
            dz_ref[:, sl] = (dav[:, sl] * mixed * _gelu_grad(zu[:, sl])).astype(BF16)
            dvn_parts.append(jnp.dot(wst_ref[g], dmix_b[:, sl], preferred_element_type=F32))
            dws_g = lax.dot_general(dmix_b[:, sl], vn[:, sl], NT, preferred_element_type=F32)
            dbs_g = jnp.sum(dmix[:, sl], axis=1, keepdims=True)

            @pl.when(c == 0)
            def _():
                dws_ref[g] = dws_g
                dbs_ref[g] = dbs_g

            @pl.when(c > 0)
            def _():
                dws_ref[g] += dws_g
                dbs_ref[g] += dbs_g

        dvn = jnp.concatenate(dvn_parts, axis=1)
        dxh = dvn * gain_v
        dvv = r * (dxh - xh * jnp.mean(dxh * xh, axis=-1, keepdims=True))
        dz_ref[:, w_sgu:] = (dvv * _gelu_grad(zv)).astype(BF16)
        pg = _rows8(dvn * xh)

        @pl.when(c == 0)
        def _():
            acc_gain[...] = pg

        @pl.when(c > 0)
        def _():
            acc_gain[...] += pg

        @pl.when(c == n - 1)
        def _():
            dgain_ref[...] = jnp.sum(acc_gain[...], axis=0, keepdims=True)

    full3 = pl.BlockSpec((groups, BLK, BLK), lambda c: (0, 0, 0))
    col3 = pl.BlockSpec((groups, BLK, 1), lambda c: (0, 0, 0))
    vec = pl.BlockSpec((1, w_sgu), lambda c: (0, 0))
    return _pallas(
        body,
        name="sgu_bwd",
        grid=(n,),
        in_specs=[
            pl.BlockSpec((BLK, w_sgu), lambda c: (c, 0)),
            pl.BlockSpec((BLK, w_sgu), lambda c: (c, 1)),
            pl.BlockSpec((BLK, w_sgu), lambda c: (c, 0)),
            vec,
            full3,
            full3,
            col3,
            ANY,
        ]
        + [ANY] * len(after),
        out_specs=[pl.BlockSpec((BLK, 2 * w_sgu), lambda c: (c, 0)), full3, col3, vec],
        out_shape=[
            jax.ShapeDtypeStruct(dz.shape, BF16),
            jax.ShapeDtypeStruct((groups, BLK, BLK), F32),
            jax.ShapeDtypeStruct((groups, BLK, 1), F32),
            jax.ShapeDtypeStruct((1, w_sgu), F32),
        ],
        scratch_shapes=[pltpu.VMEM((8, w_sgu), F32)],
        input_output_aliases={7: 0},
    )(z, z, da, gain, ws_b, wst_b, b_col, dz, *after)


def _attn_softmax(sink_ref, q_ref, k_ref, v_ref, bias_ref, s_len, grp):
    kv = pl.program_id(0)
    n = pl.program_id(1)
    start = pl.multiple_of(n * BLK, BLK)
    kb = k_ref[pl.ds(start, 3 * BLK), :]
    vb = v_ref[pl.ds(start, 3 * BLK), :]
    qv = q_ref[...]
    qs = jnp.concatenate([qv[:, g * HEAD_DIM : (g + 1) * HEAD_DIM] for g in range(grp)], axis=0).astype(BF16)
    sc = lax.dot_general(qs, kb, NT, preferred_element_type=F32) * (HEAD_DIM**-0.5)
    sc = sc + bias_ref[...].reshape(grp * BLK, 3 * BLK)
    kpos = start + lax.broadcasted_iota(I32, (1, 3 * BLK), 1) - BLK
    sc = jnp.where((kpos >= 0) & (kpos < s_len), sc, NEG)
    sink = jnp.concatenate([jnp.full((BLK, 1), sink_ref[kv * grp + g], F32) for g in range(grp)], axis=0)
    m = jnp.maximum(jnp.max(sc, axis=-1, keepdims=True), sink)
    p = jnp.exp(sc - m)
    esink = jnp.exp(sink - m)
    den = jnp.sum(p, axis=-1, keepdims=True) + esink
    return start, qs, kb, vb, p / den, esink / den


def _attn_specs(s, grp, q_blk0):
    qw = grp * HEAD_DIM
    return [
        pl.BlockSpec(memory_space=pltpu.SMEM),
        pl.BlockSpec((BLK, qw), lambda kv, n: (n, q_blk0 + kv)),
        pl.BlockSpec((s + 2 * BLK, HEAD_DIM), lambda kv, n: (0, kv)),
        pl.BlockSpec((s + 2 * BLK, HEAD_DIM), lambda kv, n: (0, kv)),
        pl.BlockSpec((grp, BLK, 3 * BLK), lambda kv, n: (kv, 0, 0)),
    ]


def _attn_fwd(sink, z, k_pad, v_pad, bias_tab, grp, q_blk0):
    s = z.shape[0]
    qw = grp * HEAD_DIM

    def body(sink_ref, q_ref, k_ref, v_ref, bias_ref, o_ref):
        _, _, _, vb, pn, _ = _attn_softmax(sink_ref, q_ref, k_ref, v_ref, bias_ref, s, grp)
        o = jnp.dot(pn.astype(BF16), vb, preferred_element_type=F32)
        for g in range(grp):
            o_ref[:, g * HEAD_DIM : (g + 1) * HEAD_DIM] = o[g * BLK : (g + 1) * BLK].astype(BF16)

    return _pallas(
        body,
        name="attn_fwd",
        grid=(N_KV_HEADS, s // BLK),
        in_specs=_attn_specs(s, grp, q_blk0),
        out_specs=pl.BlockSpec((BLK, qw), lambda kv, n: (n, kv)),
        out_shape=jax.ShapeDtypeStruct((s, N_KV_HEADS * qw), BF16),
    )(sink, z, k_pad, v_pad, bias_tab)


def _attn_bwd(sink, z, k_pad, v_pad, bias_tab, dout, dz, grp, q_blk0):
    s = z.shape[0]
    qw = grp * HEAD_DIM
    nb = s // BLK
    heads = N_KV_HEADS * grp

    def body(sink_ref, q_ref, k_ref, v_ref, bias_ref, do_ref, dz_in, dq_ref, dk_ref, dv_ref, dbias_ref, dsink_ref, dk_acc, dv_acc):
        del dz_in
        kv = pl.program_id(0)
        n = pl.program_id(1)
        start, qs, kb, vb, pn, psink = _attn_softmax(sink_ref, q_ref, k_ref, v_ref, bias_ref, s, grp)
        dov = do_ref[...]
        dos = jnp.concatenate([dov[:, g * HEAD_DIM : (g + 1) * HEAD_DIM] for g in range(grp)], axis=0)
        dp = lax.dot_general(dos, vb, NT, preferred_element_type=F32)
        dvb = lax.dot_general(pn.astype(BF16), dos, TN, preferred_element_type=F32)
        delta = jnp.sum(pn * dp, axis=-1, keepdims=True)
        ds = pn * (dp - delta)
        dsb = (ds * (HEAD_DIM**-0.5)).astype(BF16)
        dq = jnp.dot(dsb, kb, preferred_element_type=F32)
        dkb = lax.dot_general(dsb, qs, TN, preferred_element_type=F32)
        for g in range(grp):
            dq_ref[:, g * HEAD_DIM : (g + 1) * HEAD_DIM] = dq[g * BLK : (g + 1) * BLK].astype(BF16)

        @pl.when(n == 0)
        def _():
            dk_acc[...] = jnp.zeros_like(dk_acc)
            dv_acc[...] = jnp.zeros_like(dv_acc)
            dbias_ref[...] = jnp.zeros_like(dbias_ref)

        @pl.when((n == 0) & (kv == 0))
        def _():
            dsink_ref[...] = jnp.zeros_like(dsink_ref)

        dk_acc[pl.ds(start, 3 * BLK), :] += dkb
        dv_acc[pl.ds(start, 3 * BLK), :] += dvb
        dbias_ref[...] += ds.reshape(grp, BLK, 3 * BLK)
        row = lax.broadcasted_iota(I32, (heads, LANES), 0)
        sd = psink * delta
        upd = jnp.zeros((heads, LANES), F32)
        for g in range(grp):
            upd = jnp.where(row == kv * grp + g, -jnp.sum(sd[g * BLK : (g + 1) * BLK]), upd)
        dsink_ref[...] += upd

        @pl.when(n == nb - 1)
        def _():
            dk_ref[...] = dk_acc[...]
            dv_ref[...] = dv_acc[...]

    pad_spec = pl.BlockSpec((s + 2 * BLK, HEAD_DIM), lambda kv, n: (0, kv))
    kvw = N_KV_HEADS * HEAD_DIM
    return _pallas(
        body,
        name="attn_bwd",
        grid=(N_KV_HEADS, nb),
        in_specs=_attn_specs(s, grp, q_blk0) + [pl.BlockSpec((BLK, qw), lambda kv, n: (n, kv)), ANY],
        out_specs=[
            pl.BlockSpec((BLK, qw), lambda kv, n: (n, q_blk0 + kv)),
            pad_spec,
            pad_spec,
            pl.BlockSpec((grp, BLK, 3 * BLK), lambda kv, n: (kv, 0, 0)),
            pl.BlockSpec((heads, LANES), lambda kv, n: (0, 0)),
        ],
        out_shape=[
            jax.ShapeDtypeStruct(dz.shape, BF16),
            jax.ShapeDtypeStruct((s + 2 * BLK, kvw), F32),
            jax.ShapeDtypeStruct((s + 2 * BLK, kvw), F32),
            jax.ShapeDtypeStruct((heads, BLK, 3 * BLK), F32),
            jax.ShapeDtypeStruct((heads, LANES), F32),
        ],
        scratch_shapes=[pltpu.VMEM((s + 2 * BLK, HEAD_DIM), F32), pltpu.VMEM((s + 2 * BLK, HEAD_DIM), F32)],
        input_output_aliases={6: 0},
    )(sink, z, k_pad, v_pad, bias_tab, dout, dz)


def _dkv_to_dz(dk_pad, dv_pad, dz, blk_idx):
    s = dz.shape[0]
    kvw = dk_pad.shape[1]

    def body(dk_ref, dv_ref, dz_in, out_ref):
        del dz_in
        out_ref[:, :kvw] = dk_ref[...].astype(BF16)
        out_ref[:, kvw:] = dv_ref[...].astype(BF16)

    src = pl.BlockSpec((BLK, kvw), lambda i: (i + 1, 0))
    return _pallas(
        body,
        name="dkv_to_dz",
        grid=(s // BLK,),
        in_specs=[src, src, ANY],
        out_specs=pl.BlockSpec((BLK, 2 * kvw), lambda i: (i, blk_idx)),
        out_shape=jax.ShapeDtypeStruct(dz.shape, BF16),
        input_output_aliases={2: 0},
    )(dk_pad, dv_pad, dz)


def _relbias_bwd(dbias_tab, bucket):
    heads = dbias_tab.shape[0]

    def body(dt_ref, bk_ref, out_ref):
        lane = lax.broadcasted_iota(I32, (1, LANES), 1)
        bk = bk_ref[...]
        rows = []
        for h in range(heads):
            dt = dt_ref[h]
            acc = jnp.zeros((1, LANES), F32)
            for b in range(REL_BUCKETS):
                acc = jnp.where(lane == b, jnp.sum(jnp.where(bk == b, dt, 0.0)), acc)
            rows.append(acc)
        out_ref[...] = jnp.concatenate(rows, axis=0)

    return _pallas(body, name="relbias_bwd", out_shape=jax.ShapeDtypeStruct((heads, LANES), F32))(dbias_tab, bucket)


def _t5_bucket(rel):
    nb = REL_BUCKETS // 2
    ret = jnp.where(rel > 0, nb, 0)
    n = jnp.abs(rel)
    max_exact = nb // 2
    nf = jnp.maximum(n, 1).astype(F32)
    large = max_exact + (jnp.log(nf / max_exact) / math.log(REL_MAX_DIST / max_exact) * (nb - max_exact)).astype(I32)
    large = jnp.minimum(large, nb - 1)
    return ret + jnp.where(n < max_exact, n, large)


def _band_tables(rel_bias):
    qi = jnp.arange(BLK)[:, None]
    kj = jnp.arange(3 * BLK)[None, :]
    rel = kj - BLK - qi
    bucket = _t5_bucket(rel).astype(I32)
    heads = rel_bias.shape[1]
    masked = jnp.where(jnp.abs(rel) <= BLK, bucket, -1)

    def body(rb_ref, bk_ref, out_ref):
        bk = bk_ref[...]
        for h in range(heads):
            tab = jnp.full(bk.shape, NEG, F32)
            for b in range(REL_BUCKETS):
                tab = jnp.where(bk == b, rb_ref[b, h], tab)
            out_ref[h] = tab

    bias_tab = _pallas(
        body,
        name="bias_table",
        in_specs=[pl.BlockSpec(memory_space=pltpu.SMEM), pl.BlockSpec(memory_space=pltpu.VMEM)],
        out_specs=pl.BlockSpec(memory_space=pltpu.VMEM),
        out_shape=jax.ShapeDtypeStruct((heads, BLK, 3 * BLK), F32),
    )(rel_bias.astype(F32), masked)
    return bias_tab, bucket


EW_BLOCK_ELEMS = 512 * 1024


def _ew_tiles(shape, elems=EW_BLOCK_ELEMS // 2):
    r, c = shape
    tn = c if c <= 2048 else _tile(c, (2048, 1920, 1536, 1408, 1024, 512))
    tm = _tile(r, [t for t in (1024, 512, 256, 128, 64, 32, 16, 8) if t * tn <= elems] or [8])
    return tm, tn


def _cast_into_full(name, qidx, w, kind, after=()):
    r, c = w.shape
    tm, tn = _ew_tiles(w.shape, EW_BLOCK_ELEMS)
    nbi, nbj = r // tm, c // tn
    if kind == "col":
        full, out_spec = (r, c * N_CHIPS), pl.BlockSpec((tm, tn), lambda i, j, q: (i, q[0] * nbj + j))
    else:
        full, out_spec = (r * N_CHIPS, c), pl.BlockSpec((tm, tn), lambda i, j, q: (q[0] * nbi + i, j))

    def body(q_ref, w_ref, *rest):
        del q_ref
        rest[-1][...] = w_ref[...].astype(BF16)

    return _pallas(
        body,
        name=name,
        grid_spec=pltpu.PrefetchScalarGridSpec(
            num_scalar_prefetch=1,
            grid=(nbi, nbj),
            in_specs=[pl.BlockSpec((tm, tn), lambda i, j, q: (i, j))] + [ANY] * len(after),
            out_specs=out_spec,
        ),
        out_shape=jax.ShapeDtypeStruct(full, BF16),
    )(qidx, w, *after)


def _adamw(name, w, g, m, v, after=()):
    tm, tn = _ew_tiles(w.shape, EW_BLOCK_ELEMS)
    if _nbytes(w.shape, F32) <= 1024 * 1024:
        tm, tn = w.shape
    spec = pl.BlockSpec((tm, tn), lambda i, j: (i, j))
    n_after = len(after)

    def body(w_ref, g_ref, m_ref, v_ref, *rest):
        d_ref, nm_ref, nv_ref, g_out_ref = rest[n_after:]
        gv = g_ref[...]
        g_out_ref[...] = gv
        nm = ADAM_B1 * m_ref[...] + (1.0 - ADAM_B1) * gv
        nv = ADAM_B2 * v_ref[...] + (1.0 - ADAM_B2) * (gv * gv)
        m_hat = nm / (1.0 - ADAM_B1**ADAM_STEP)
        v_hat = nv / (1.0 - ADAM_B2**ADAM_STEP)
        d_ref[...] = -ADAM_LR * (m_hat / (jnp.sqrt(v_hat) + ADAM_EPS) + ADAM_WD * w_ref[...])
        nm_ref[...] = nm
        nv_ref[...] = nv

    out = jax.ShapeDtypeStruct(w.shape, F32)
    return _pallas(
        body, name=name, grid=(w.shape[0] // tm, w.shape[1] // tn), in_specs=[spec] * 4 + [ANY] * n_after,
        out_specs=[spec] * 4, out_shape=[out, out, out, out],
        compiler_params=_params(_mm_vmem([((tm, tn), F32, 24)])),
    )(w, g, m, v, *after)


def _pair_add(name, cidx, g_full, r_sib, kind):
    hr, hc = r_sib.shape
    tm, tn = _ew_tiles((hr, hc), 2 * EW_BLOCK_ELEMS)
    nbi, nbj = hr // tm, hc // tn
    if kind == "col":
        g_spec = pl.BlockSpec((tm, tn), lambda i, j, c: (c[0] * nbi + i, j))
    else:
        g_spec = pl.BlockSpec((tm, tn), lambda i, j, c: (i, c[0] * nbj + j))
    spec = pl.BlockSpec((tm, tn), lambda i, j, c: (i, j))

    def body(c_ref, g_ref, r_ref, o_ref):
        del c_ref
        o_ref[...] = (g_ref[...].astype(F32) + r_ref[...].astype(F32)).astype(BF16)

    return _pallas(
        body,
        name=name,
        grid_spec=pltpu.PrefetchScalarGridSpec(num_scalar_prefetch=1, grid=(nbi, nbj), in_specs=[g_spec, spec], out_specs=spec),
        out_shape=jax.ShapeDtypeStruct((hr, hc), BF16),
        compiler_params=_params(_mm_vmem([((tm, tn), BF16, 6), ((tm, tn), F32, 3)])),
    )(cidx, g_full, r_sib)


def _chip_sum(name, qidx, c_half, r_ici, kind):
    _, pr, pc = r_ici.shape
    tm, tn = _ew_tiles((pr, pc), 2 * EW_BLOCK_ELEMS)
    nbi, nbj = pr // tm, pc // tn
    if kind == "col":
        own_spec = pl.BlockSpec((tm, tn), lambda i, j, q: (i, q[0] * nbj + j))
        full, out_spec = (2 * pr, pc), pl.BlockSpec((tm, tn), lambda i, j, q: (q[1] * nbi + i, j))
    else:
        own_spec = pl.BlockSpec((tm, tn), lambda i, j, q: (q[0] * nbi + i, j))
        full, out_spec = (pr, 2 * pc), pl.BlockSpec((tm, tn), lambda i, j, q: (i, q[1] * nbj + j))

    def body(q_ref, own_ref, r_ref, o_ref):
        q = q_ref[0]
        own = own_ref[...].astype(F32)
        recv = [r_ref[r].astype(F32) for r in range(3)]
        total = None
        for chip in range(N_CHIPS):
            d = chip ^ q
            term = jnp.where(d == 0, own, jnp.where(d == 2, recv[0], jnp.where(d == 1, recv[1], recv[2])))
            total = term if total is None else total + term
        o_ref[...] = total

    return _pallas(
        body,
        name=name,
        grid_spec=pltpu.PrefetchScalarGridSpec(
            num_scalar_prefetch=1,
            grid=(nbi, nbj),
            in_specs=[own_spec, pl.BlockSpec((3, tm, tn), lambda i, j, q: (0, i, j))],
            out_specs=out_spec,
        ),
        out_shape=jax.ShapeDtypeStruct(full, F32),
        compiler_params=_params(_mm_vmem([((tm, tn), BF16, 8), ((tm, tn), F32, 6)])),
    )(qidx, c_half, r_ici)


_REL_MASK = (2, 1, 3)


def _place():
    x, y, c = lax.axis_index("x"), lax.axis_index("y"), lax.axis_index("c")
    chips = [(1 - x, y), (x, 1 - y), (1 - x, 1 - y)]
    return x, y, c, 2 * x + y, chips


def _shard_view(ref, kind, chip):
    if kind == "col":
        w = ref.shape[1] // N_CHIPS
        return ref.at[:, pl.ds(pl.multiple_of(chip * w, LANES), w)]
    h = ref.shape[0] // N_CHIPS
    return ref.at[pl.ds(pl.multiple_of(chip * h, 16), h), :]


def _row_half(ref, half):
    h = ref.shape[0] // 2
    return ref.at[pl.ds(pl.multiple_of(half * h, 16), h), :]


def _pair_half(ref, kind, half):
    if kind == "col":
        return _row_half(ref, half)
    w = ref.shape[1] // 2
    return ref.at[:, pl.ds(pl.multiple_of(half * w, LANES), w)]


def _remote(src, dst, send_sem, recv_sem, dev):
    return pltpu.make_async_remote_copy(src_ref=src, dst_ref=dst, send_sem=send_sem, recv_sem=recv_sem, device_id=dev, device_id_type=MESH)


def _hbm(a):
    return pltpu.with_memory_space_constraint(a, pltpu.HBM)


def _gather_start(name, fulls, kinds, rels=(0, 1, 2), after=()):
    n_w = len(fulls)

    def body(*refs):
        g = refs[:n_w]
        send_sem, recv_sem = refs[n_w + len(after)], refs[n_w + len(after) + 1]
        token = refs[-1]
        _, _, c, q, chips = _place()
        for w in range(n_w):
            mine = _row_half(_shard_view(g[w], kinds[w], q), c)
            for r in rels if isinstance(rels, tuple) else rels[w]:
                _remote(mine, mine, send_sem.at[3 * w + r], recv_sem.at[3 * w + r], (*chips[r], c)).start()
        token[...] = jnp.zeros_like(token)

    res = _pallas(
        body,
        name=name,
        out_shape=(
            pltpu.SemaphoreType.DMA((3 * n_w,)),
            pltpu.SemaphoreType.DMA((3 * n_w,)),
            *[pltpu.HBM(f.shape, f.dtype) for f in fulls],
            jax.ShapeDtypeStruct((8, LANES), F32),
        ),
        in_specs=[HBM_SPEC] * n_w + [ANY] * len(after),
        out_specs=(SEM_SPEC, SEM_SPEC, *[HBM_SPEC] * n_w, pl.BlockSpec(memory_space=pltpu.VMEM)),
        input_output_aliases={w: w + 2 for w in range(n_w)},
        compiler_params=pltpu.CompilerParams(has_side_effects=EFFECT),
    )(*[_hbm(f) for f in fulls], *after)
    return res[0], res[1], list(res[2 : 2 + n_w]), res[-1]


def _relay_copies(kinds, waiting):
    def copies(refs, send_sem, recv_sem):
        _, _, c, q, chips = _place()
        out = []
        for i, kind in enumerate(kinds):
            for k, (src_rel, dst_rel) in enumerate(((0, 1), (1, 0))):
                held = _row_half(_row_half(_shard_view(refs[i], kind, q ^ _REL_MASK[src_rel]), c), k)
                far = _row_half(_row_half(_shard_view(refs[i], kind, q ^ _REL_MASK[2]), c), k)
                dst = far if waiting else held
                out.append(_remote(held, dst, send_sem.at[2 * i + k], recv_sem.at[2 * i + k], (*chips[dst_rel], c)))
        return out

    return copies


def _gather_wait(name, fulls, kinds, w_ids, send_sem, recv_sem, after, rels=(0, 1, 2)):
    n = len(fulls)

    def body(*refs):
        g = refs[:n]
        s_sem, r_sem = refs[n], refs[n + 1]
        x, y, c, q, _ = _place()
        for i, w in enumerate(w_ids):
            mine = _row_half(_shard_view(g[i], kinds[i], q), c)
            for r in rels:
                landed = _row_half(_shard_view(g[i], kinds[i], q ^ _REL_MASK[r]), c)
                cp = _remote(mine, landed, s_sem.at[3 * w + r], r_sem.at[3 * w + r], (x, y, 1 - c))
                cp.wait_send()
                cp.wait_recv()

    res = _pallas(
        body,
        name=name,
        out_shape=[pltpu.HBM(f.shape, f.dtype) for f in fulls],
        in_specs=[HBM_SPEC] * n + [SEM_SPEC, SEM_SPEC, ANY],
        out_specs=[HBM_SPEC] * n,
        input_output_aliases={i: i for i in range(n)},
        compiler_params=pltpu.CompilerParams(has_side_effects=EFFECT),
    )(*fulls, send_sem, recv_sem, after)
    return list(res)


def _gather_forward(name, fulls, kinds):
    n = len(fulls)

    def body(*refs):
        g = refs[n : 2 * n]
        send, recv = refs[2 * n :]
        x, y, c, q, _ = _place()
        sib = (x, y, 1 - c)
        cps = []
        for i in range(n):
            for r in range(3):
                landed = _row_half(_shard_view(g[i], kinds[i], q ^ _REL_MASK[r]), c)
                cps.append(_remote(landed, landed, send.at[i, r], recv.at[i, r], sib))
        for cp in cps:
            cp.start()
        for i in range(n):
            for r in range(3):
                other = _row_half(_shard_view(g[i], kinds[i], q ^ _REL_MASK[r]), 1 - c)
                _remote(other, other, send.at[i, r], recv.at[i, r], sib).wait_recv()
        for cp in cps:
            cp.wait_send()

    res = _pallas(
        body,
        name=name,
        in_specs=[ANY] * n,
        out_specs=[ANY] * n,
        out_shape=[jax.ShapeDtypeStruct(f.shape, f.dtype) for f in fulls],
        scratch_shapes=[pltpu.SemaphoreType.DMA((n, 3)), pltpu.SemaphoreType.DMA((n, 3))],
        input_output_aliases={i: i for i in range(n)},
    )(*fulls)
    return list(res)


def _split_start(name, bufs, n_sems, copies):
    n = len(bufs)

    def body(*refs):
        for cp in copies(refs[:n], refs[n], refs[n + 1]):
            cp.start()

    res = _pallas(
        body,
        name=name,
        out_shape=(
            pltpu.SemaphoreType.DMA((n_sems,)),
            pltpu.SemaphoreType.DMA((n_sems,)),
            *[pltpu.HBM(b.shape, b.dtype) for b in bufs],
        ),
        in_specs=[HBM_SPEC] * n,
        out_specs=(SEM_SPEC, SEM_SPEC, *[HBM_SPEC] * n),
        input_output_aliases={i: i + 2 for i in range(n)},
        compiler_params=pltpu.CompilerParams(has_side_effects=EFFECT),
    )(*[_hbm(b) for b in bufs])
    return res[0], res[1], list(res[2:])


def _split_wait(name, bufs, send_sem, recv_sem, copies, after):
    n = len(bufs)

    def body(*refs):
        for cp in copies(refs[:n], refs[n], refs[n + 1]):
            cp.wait_send()
            cp.wait_recv()

    res = _pallas(
        body,
        name=name,
        out_shape=[pltpu.HBM(b.shape, b.dtype) for b in bufs],
        in_specs=[HBM_SPEC] * n + [SEM_SPEC, SEM_SPEC, ANY],
        out_specs=[HBM_SPEC] * n,
        input_output_aliases={i: i for i in range(n)},
        compiler_params=pltpu.CompilerParams(has_side_effects=EFFECT),
    )(*bufs, send_sem, recv_sem, after)
    return list(res)


def _pair_exchange_copies(kinds):
    n = len(kinds)

    def copies(refs, send_sem, recv_sem):
        x, y, c, _, _ = _place()
        return [
            _remote(_pair_half(refs[w], kinds[w], 1 - c), refs[n + w], send_sem.at[w], recv_sem.at[w], (x, y, 1 - c))
            for w in range(n)
        ]

    return copies


def _pair_share_copies(kinds, waiting):
    def copies(refs, send_sem, recv_sem):
        x, y, c, _, _ = _place()
        out = []
        for w, kind in enumerate(kinds):
            mine = _pair_half(refs[w], kind, c)
            dst = _pair_half(refs[w], kind, 1 - c) if waiting else mine
            out.append(_remote(mine, dst, send_sem.at[w], recv_sem.at[w], (x, y, 1 - c)))
        return out

    return copies


def _piece_shape(half_shape, kind):
    r, c = half_shape
    return (3, r, c // N_CHIPS) if kind == "col" else (3, r // N_CHIPS, c)


def _chip_send_start(name, halves, kinds):
    n = len(halves)
    lands = [lax.empty(_piece_shape(h.shape, k), BF16) for h, k in zip(halves, kinds)]

    def body(*refs):
        h, land = refs[:n], refs[n : 2 * n]
        send_sem, recv_sem = refs[2 * n], refs[2 * n + 1]
        _, _, c, q, chips = _place()
        for i in range(n):
            for r, chip in enumerate(chips):
                piece = _shard_view(h[i], kinds[i], q ^ _REL_MASK[r])
                _remote(piece, land[i].at[r], send_sem.at[3 * i + r], recv_sem.at[3 * i + r], (*chip, c)).start()

    res = _pallas(
        body,
        name=name,
        out_shape=(
            pltpu.SemaphoreType.DMA((3 * n,)),
            pltpu.SemaphoreType.DMA((3 * n,)),
            *[pltpu.HBM(a.shape, a.dtype) for a in halves],
            *[pltpu.HBM(a.shape, a.dtype) for a in lands],
        ),
        in_specs=[HBM_SPEC] * (2 * n),
        out_specs=(SEM_SPEC, SEM_SPEC, *[HBM_SPEC] * (2 * n)),
        input_output_aliases={i: i + 2 for i in range(2 * n)},
        compiler_params=pltpu.CompilerParams(has_side_effects=EFFECT),
    )(*[_hbm(a) for a in halves], *[_hbm(a) for a in lands])
    return res[0], res[1], list(res[2 : 2 + n]), list(res[2 + n :])


def _chip_send_wait(name, halves, lands, kinds, send_sem, recv_sem, after):
    n = len(halves)

    def body(*refs):
        h, land = refs[:n], refs[n : 2 * n]
        s_sem, r_sem = refs[2 * n], refs[2 * n + 1]
        x, y, c, q, _ = _place()
        for i in range(n):
            for r in range(3):
                piece = _shard_view(h[i], kinds[i], q ^ _REL_MASK[r])
                cp = _remote(piece, land[i].at[r], s_sem.at[3 * i + r], r_sem.at[3 * i + r], (x, y, 1 - c))
                cp.wait_send()
                cp.wait_recv()

    res = _pallas(
        body,
        name=name,
        out_shape=[pltpu.HBM(a.shape, a.dtype) for a in halves] + [pltpu.HBM(a.shape, a.dtype) for a in lands],
        in_specs=[HBM_SPEC] * (2 * n) + [SEM_SPEC, SEM_SPEC, ANY],
        out_specs=[HBM_SPEC] * (2 * n),
        input_output_aliases={i: i for i in range(2 * n)},
        compiler_params=pltpu.CompilerParams(has_side_effects=EFFECT),
    )(*halves, *lands, send_sem, recv_sem, after)
    return list(res[:n]), list(res[n:])


def _small_exchange_copies(waiting):
    def copies(refs, send_sem, recv_sem):
        p, land = refs
        x, y, c, q, _ = _place()
        me = 2 * q + c
        out = []
        for dd in range(1, 2 * N_CHIPS):
            dev = (x ^ ((dd >> 2) & 1), y ^ ((dd >> 1) & 1), c ^ (dd & 1))
            dst = land.at[me ^ dd] if waiting else land.at[me]
            out.append(_remote(p, dst, send_sem.at[dd - 1], recv_sem.at[dd - 1], dev))
        return out

    return copies


def _small_sum(name, me_idx, p, land):
    rows = p.shape[0]
    n_dev = 2 * N_CHIPS

    def body(me_ref, p_ref, land_ref, o_ref):
        me = me_ref[0]
        total = None
        for dev in range(n_dev):
            term = jnp.where(me == dev, p_ref[...], land_ref[dev])
            total = term if total is None else total + term
        o_ref[...] = total

    return _pallas(
        body,
        name=name,
        grid_spec=pltpu.PrefetchScalarGridSpec(
            num_scalar_prefetch=1,
            grid=(1,),
            in_specs=[pl.BlockSpec((rows, LANES), lambda i, m: (0, 0)), pl.BlockSpec((n_dev, rows, LANES), lambda i, m: (0, 0, 0))],
            out_specs=pl.BlockSpec((rows, LANES), lambda i, m: (0, 0)),
        ),
        out_shape=jax.ShapeDtypeStruct(p.shape, F32),
    )(me_idx, p, land)


def _pack(parts):
    rows = []
    for a in parts:
        flat = a.reshape(-1).astype(F32)
        n = flat.shape[0]
        padded = -(-n // (8 * LANES)) * (8 * LANES)
        rows.append(jnp.pad(flat, (0, padded - n)).reshape(-1, LANES))
    return jnp.concatenate(rows, axis=0)


def _unpack(packed, shapes):
    out, row = [], 0
    for shp in shapes:
        n = int(np.prod(shp))
        nrows = -(-n // (8 * LANES)) * 8
        out.append(packed[row : row + nrows].reshape(-1)[:n].reshape(shp))
        row += nrows
    return out


def kernel(x, w_in, norm_mix, sgu_v_gain, sgu_w_s, sgu_b_s, w_a_out, attn_sink, rel_bias, w_b_out, w_o, norm_ffn, w_gate, w_up, w_down, norm_final, loss_target, m_w_in, m_norm_mix, m_sgu_v_gain, m_sgu_w_s, m_sgu_b_s, m_w_a_out, m_attn_sink, m_rel_bias, m_w_b_out, m_w_o, m_norm_ffn, m_w_gate, m_w_up, m_w_down, m_norm_final, v_w_in, v_norm_mix, v_sgu_v_gain, v_sgu_w_s, v_sgu_b_s, v_w_a_out, v_attn_sink, v_rel_bias, v_w_b_out, v_w_o, v_norm_ffn, v_w_gate, v_w_up, v_w_down, v_norm_final):
    s, d = x.shape[1], x.shape[2]
    w_sgu = sgu_v_gain.shape[1]
    groups = sgu_w_s.shape[1]
    heads = attn_sink.shape[1]
    grp = heads // N_KV_HEADS
    w_att = heads * HEAD_DIM
    w_kv = N_KV_HEADS * HEAD_DIM
    d_ff = w_gate.shape[2] * N_CHIPS
    n_in = w_in.shape[2] * N_CHIPS
    off_q = 2 * w_sgu
    off_k = off_q + w_att
    off_g = off_k + 2 * w_kv
    assert n_in == off_g + 2 * d and groups * BLK == w_sgu and s % BLK == 0

    x2d = x.reshape(s, d)
    tgt = loss_target.reshape(s, d)
    c_idx = lax.axis_index("c").astype(I32).reshape(1)
    q_idx = (2 * lax.axis_index("x") + lax.axis_index("y")).astype(I32).reshape(1)
    qc_idx = jnp.concatenate([q_idx, c_idx])

    W_IN, W_A, W_B, W_O, W_GATE, W_UP, W_DOWN = range(7)
    names = ["w_in", "w_a", "w_b", "w_o", "w_gate", "w_up", "w_down"]
    kinds = ["col", "col", "col", "row", "col", "col", "row"]
    big_w = [w_in[0], w_a_out[0], w_b_out[0], w_o[0], w_gate[0], w_up[0], w_down[0]]
    big_m = [m_w_in[0], m_w_a_out[0], m_w_b_out[0], m_w_o[0], m_w_gate[0], m_w_up[0], m_w_down[0]]
    big_v = [v_w_in[0], v_w_a_out[0], v_w_b_out[0], v_w_o[0], v_w_gate[0], v_w_up[0], v_w_down[0]]
    full_in = _cast_into_full("cast_w_in", q_idx, big_w[W_IN], kinds[W_IN])
    in_send, in_recv, (full_in,), token = _gather_start("gather_start_in", [full_in], [kinds[W_IN]], rels=(0, 1))
    rest = [_cast_into_full("cast_" + names[i], q_idx, big_w[i], kinds[i], after=(token,)) for i in range(1, 7)]
    h1 = _rms_fwd("rms_mix", x2d, norm_mix, after=(token,))
    (full_in,) = _gather_wait("gather_wait_in", [full_in], [kinds[W_IN]], [0], in_send, in_recv, h1, rels=(0, 1))
    relay_send, relay_recv, (full_in,) = _split_start("gather_relay_in", [full_in], 2, _relay_copies([kinds[W_IN]], False))
    relayed = (W_GATE, W_UP)
    full_down = rest.pop()
    rest_rels = [(0, 1) if i in relayed else (0, 1, 2) for i in range(1, 6)]
    ag_send, ag_recv, rest, token = _gather_start("gather_start_rest", rest, kinds[1:6], rels=rest_rels, after=(full_in,))
    (full_in,) = _split_wait("gather_relay_wait_in", [full_in], relay_send, relay_recv, _relay_copies([kinds[W_IN]], True), token)
    (g_in,) = _gather_forward("gather_fwd_in", [full_in], [kinds[W_IN]])
    fulls = [g_in] + rest

    def gathered(tag, ids, after):
        got = _gather_wait("gather_wait_" + tag, [fulls[i] for i in ids], [kinds[i] for i in ids], [i - 1 for i in ids],
                           ag_send, ag_recv, after)
        return _gather_forward("gather_fwd_" + tag, got, [kinds[i] for i in ids])

    ws_b = sgu_w_s[0].astype(BF16)
    wst_b = jnp.swapaxes(sgu_w_s[0], 1, 2).astype(BF16)
    b_col = sgu_b_s[0].reshape(groups, BLK, 1)
    bias_tab, bucket = _band_tables(rel_bias)
    sink = attn_sink[0]

    tm = _tile(s, (1024, 512, 256, 128))

    tn = _tile(n_in, (768, 640, 512))
    z = _mm(
        "mm_z", (s // tm, n_in // tn, 1), [h1, g_in],
        [pl.BlockSpec((tm, d), lambda i, j, k: (i, 0)), pl.BlockSpec((d, tn), lambda i, j, k: (0, j))],
        [jax.ShapeDtypeStruct((s, n_in), BF16)], [pl.BlockSpec((tm, tn), lambda i, j, k: (i, j))],
        [(0, 1, NN, 0)], 1, (tm, tn), 1, lambda ins, vals, outs, cs: _put(outs[0], cs, vals[0]),
        _mm_vmem([((tm, d), BF16, 2), ((d, tn), BF16, 2), ((tm, tn), F32, 3)]),
    )[0]
    g_a, g_b, g_o = gathered("mix", [W_A, W_B, W_O], z)

    a_act = _sgu_fwd(z, sgu_v_gain, ws_b, b_col, w_sgu)

    kv_b = z[:, off_k:off_g]
    k_pad = jnp.pad(kv_b[:, :w_kv], ((BLK, BLK), (0, 0)))
    v_pad = jnp.pad(kv_b[:, w_kv:], ((BLK, BLK), (0, 0)))
    q_blk0 = off_q // (grp * HEAD_DIM)
    att = _attn_fwd(sink, z, k_pad, v_pad, bias_tab, grp, q_blk0)

    tg = _tile(d, (512,))
    ga0, gb0 = off_g // tg, (off_g + d) // tg

    def ep_gate(ins, vals, outs, cs):
        sa, sb = _sigmoid(ins[4][:, cs].astype(F32)), _sigmoid(ins[5][:, cs].astype(F32))
        _put(outs[0], cs, sa * vals[0] + sb * vals[1])
        _put(outs[1], cs, vals[0])
        _put(outs[2], cs, vals[1])

    t_out = pl.BlockSpec((tm, tg), lambda i, j, k: (i, j))
    m_act, y_a, y_b = _mm(
        "mm_branches", (s // tm, d // tg, 1), [a_act, g_a, att, g_b, z, z],
        [pl.BlockSpec((tm, w_sgu), lambda i, j, k: (i, 0)), pl.BlockSpec((w_sgu, tg), lambda i, j, k: (0, j)),
         pl.BlockSpec((tm, w_att), lambda i, j, k: (i, 0)), pl.BlockSpec((w_att, tg), lambda i, j, k: (0, j)),
         pl.BlockSpec((tm, tg), lambda i, j, k: (i, ga0 + j)), pl.BlockSpec((tm, tg), lambda i, j, k: (i, gb0 + j))],
        [jax.ShapeDtypeStruct((s, d), BF16)] * 3,
        [t_out, t_out, t_out], [(0, 1, NN, 0), (2, 3, NN, 1)], 2, (tm, tg), 1, ep_gate,
        _mm_vmem([((tm, w_sgu), BF16, 4), ((w_sgu, tg), BF16, 4), ((tm, tg), F32, 12)]),    )

    tn = _tile(d, (1024, 512))

    def ep_residual(ins, vals, outs, cs):
        _put(outs[0], cs, ins[2][:, cs] + vals[0])

    ffn_k = [kinds[i] for i in relayed]
    ffn_bufs = _gather_wait("gather_wait_ffn_in", [fulls[i] for i in relayed], ffn_k, [i - 1 for i in relayed], ag_send, ag_recv,
                            m_act, rels=(0, 1))
    ffn_send, ffn_recv, ffn_bufs = _split_start("gather_relay_ffn_in", ffn_bufs, 2 * len(relayed), _relay_copies(ffn_k, False))
    down_send, down_recv, (full_down,), _ = _gather_start("gather_start_down", [full_down], [kinds[W_DOWN]], after=(ffn_bufs[0],))

    x2 = _mm(
        "mm_wo", (s // tm, d // tn, 1), [m_act, g_o, x2d],
        [pl.BlockSpec((tm, d), lambda i, j, k: (i, 0)), pl.BlockSpec((d, tn), lambda i, j, k: (0, j)),
         pl.BlockSpec((tm, tn), lambda i, j, k: (i, j))],
        [jax.ShapeDtypeStruct((s, d), F32)], [pl.BlockSpec((tm, tn), lambda i, j, k: (i, j))],
        [(0, 1, NN, 0)], 1, (tm, tn), 1, ep_residual,
        _mm_vmem([((tm, d), BF16, 2), ((d, tn), BF16, 2), ((tm, tn), F32, 5)]),
        after=(ffn_bufs[0],),
    )[0]

    h2 = _rms_fwd("rms_ffn", x2, norm_ffn)
    ffn_bufs = _split_wait("gather_relay_wait_ffn_in", ffn_bufs, ffn_send, ffn_recv, _relay_copies(ffn_k, True), h2)
    g_gate, g_up = _gather_forward("gather_fwd_ffn_in", ffn_bufs, ffn_k)

    tf = _tile(d_ff, (512,))

    def ep_swiglu(ins, vals, outs, cs):
        gt, up = vals
        _put(outs[0], cs, gt)
        _put(outs[1], cs, up)
        _put(outs[2], cs, (gt * _sigmoid(gt)) * up)

    f_out = pl.BlockSpec((tm, tf), lambda i, j, k: (i, j))
    gt, up, f_act = _mm(
        "mm_gate_up", (s // tm, d_ff // tf, 1), [h2, g_gate, g_up],
        [pl.BlockSpec((tm, d), lambda i, j, k: (i, 0)), pl.BlockSpec((d, tf), lambda i, j, k: (0, j)),
         pl.BlockSpec((d, tf), lambda i, j, k: (0, j))],
        [jax.ShapeDtypeStruct((s, d_ff), BF16)] * 3,
        [f_out, f_out, f_out], [(0, 1, NN, 0), (0, 2, NN, 1)], 2, (tm, tf), 1, ep_swiglu,
        _mm_vmem([((tm, d), BF16, 2), ((d, tf), BF16, 4), ((tm, tf), F32, 8)]),    )
    (g_down,) = _gather_forward(
        "gather_fwd_ffn_out",
        _gather_wait("gather_wait_ffn_out", [full_down], [kinds[W_DOWN]], [0], down_send, down_recv, f_act),
        [kinds[W_DOWN]],
    )

    tkf = _tile(d_ff, (1408, 1024, 512))
    tml, tnl = _tile(s, (512, 256, 128)), _tile(d, (512,))
    x3 = _mm(
        "mm_down", (s // tml, d // tnl, 1), [f_act, g_down, x2],
        [pl.BlockSpec((tml, d_ff), lambda i, j, k: (i, 0)), pl.BlockSpec((d_ff, tnl), lambda i, j, k: (0, j)),
         pl.BlockSpec((tml, tnl), lambda i, j, k: (i, j))],
        [jax.ShapeDtypeStruct((s, d), F32)], [pl.BlockSpec((tml, tnl), lambda i, j, k: (i, j))],
        [(0, 1, NN, 0)], 1, (tml, tnl), 1, ep_residual,
        _mm_vmem([((tml, d_ff), BF16, 2), ((d_ff, tnl), BF16, 2), ((tml, tnl), F32, 6)]),    )[0]

    dx3, dx3b, dg_final, loss_part = _head(x3, norm_final.reshape(1, d), tgt)

    def reduce_a(tag, ids, grads):
        ks = [kinds[i] for i in ids]
        lands = [lax.empty((g.shape[0] // 2, g.shape[1]) if k == "col" else (g.shape[0], g.shape[1] // 2), BF16)
                 for g, k in zip(grads, ks)]
        send, recv, bufs = _split_start("pair_send_" + tag, list(grads) + lands, len(ids), _pair_exchange_copies(ks))
        return {"tag": tag, "ids": ids, "ks": ks, "pair": (send, recv, bufs), "token": bufs[0]}

    def reduce_b(st, after):
        tag, ids, ks = st["tag"], st["ids"], st["ks"]
        send, recv, bufs = st["pair"]
        bufs = _split_wait("pair_wait_" + tag, bufs, send, recv, _pair_exchange_copies(ks), after)
        grads, from_sib = bufs[: len(ids)], bufs[len(ids) :]
        halves = [_pair_add("pair_add_" + names[i], c_idx, g, r, k) for i, g, r, k in zip(ids, grads, from_sib, ks)]
        st["chip"] = _chip_send_start("chip_send_" + tag, halves, ks)
        st["token"] = st["chip"][2][0]

    def reduce_c(st, after):
        tag, ids, ks = st["tag"], st["ids"], st["ks"]
        send, recv, halves, lands = st["chip"]
        halves, lands = _chip_send_wait("chip_wait_" + tag, halves, lands, ks, send, recv, after)
        pieces = [_chip_sum("chip_sum_" + names[i], qc_idx, h, r, k) for i, h, r, k in zip(ids, halves, lands, ks)]
        st["share"] = _split_start("share_send_" + tag, pieces, len(ids), _pair_share_copies(ks, False))
        st["token"] = st["share"][2][0]

    def reduce_d(st, after):
        send, recv, bufs = st["share"]
        return _split_wait("share_wait_" + st["tag"], bufs, send, recv, _pair_share_copies(st["ks"], True), after)

    grads_big, upd = [None] * 7, [None] * 7

    def finish(st, after):
        shared = reduce_d(st, after)
        after = shared[0]
        for i, g in zip(st["ids"], shared):
            upd[i] = _adamw("adamw_" + names[i], big_w[i], g, big_m[i], big_v[i], after=(after,))
            grads_big[i] = upd[i][3]
            after = upd[i][0]
        return after

    def ep_swiglu_bwd(ins, vals, outs, cs):
        df = vals[0]
        gtv, upv = ins[2][:, cs].astype(F32), ins[3][:, cs].astype(F32)
        sg = _sigmoid(gtv)
        _put(outs[0], cs, df * upv * (sg + gtv * sg * (1.0 - sg)))
        _put(outs[1], cs, df * (gtv * sg))

    dgt, dup = _mm(
        "mm_dswiglu", (s // tm, d_ff // tf, 1), [dx3b, g_down, gt, up],
        [pl.BlockSpec((tm, d), lambda i, j, k: (i, 0)), pl.BlockSpec((tf, d), lambda i, j, k: (j, 0)), f_out, f_out],
        [jax.ShapeDtypeStruct((s, d_ff), BF16), jax.ShapeDtypeStruct((s, d_ff), BF16)], [f_out, f_out],
        [(0, 1, NT, 0)], 1, (tm, tf), 1, ep_swiglu_bwd,
        _mm_vmem([((tm, d), BF16, 2), ((tf, d), BF16, 2), ((tm, tf), F32, 8)]),    )

    def ep_store(ins, vals, outs, cs):
        for o, v in zip(outs, vals):
            _put(o, cs, v)

    twn = _tile(d, (1024, 512))
    gw_down = _mm(
        "mm_gw_down", (d_ff // tkf, d // twn, 1), [f_act, dx3b],
        [pl.BlockSpec((s, tkf), lambda i, j, k: (0, i)), pl.BlockSpec((s, twn), lambda i, j, k: (0, j))],
        [jax.ShapeDtypeStruct((d_ff, d), BF16)], [pl.BlockSpec((tkf, twn), lambda i, j, k: (i, j))],
        [(0, 1, TN, 0)], 1, (tkf, twn), 1, ep_store,
        _mm_vmem([((s, tkf), BF16, 3), ((s, twn), BF16, 2), ((tkf, twn), F32, 3)]),
    )[0]
    red_down = reduce_a("down", [W_DOWN], [gw_down])

    tn2 = _tile(d, (256,))
    dh2_specs = [pl.BlockSpec((tm, d_ff), lambda i, j, k: (i, 0)), pl.BlockSpec((tn2, d_ff), lambda i, j, k: (j, 0))]
    dh2_tile = pl.BlockSpec((tm, tn2), lambda i, j, k: (i, j))
    dh2_vmem = _mm_vmem([((tm, d_ff), BF16, 2), ((tn2, d_ff), BF16, 2), ((tm, tn2), F32, 7)])
    dh2 = _mm(
        "mm_dh2_gate", (s // tm, d // tn2, 1), [dgt, g_gate], dh2_specs,
        [jax.ShapeDtypeStruct((s, d), F32)], [dh2_tile], [(0, 1, NT, 0)], 1, (tm, tn2), 1, ep_store, dh2_vmem,
        after=(red_down["token"],),
    )[0]
    dh2 = _mm(
        "mm_dh2_up", (s // tm, d // tn2, 1), [dup, g_up, dh2], dh2_specs + [dh2_tile],
        [jax.ShapeDtypeStruct((s, d), F32)], [dh2_tile], [(0, 1, NT, 0)], 1, (tm, tn2), 1, ep_residual, dh2_vmem,
    )[0]
    reduce_b(red_down, dh2)

    twr = _tile(d, (1024, 512))
    w_tile = pl.BlockSpec((twr, tf), lambda i, j, k: (i, j))
    gw_gate, gw_up = _mm(
        "mm_gw_gate_up", (d // twr, d_ff // tf, 1), [h2, dgt, dup],
        [pl.BlockSpec((s, twr), lambda i, j, k: (0, i)), pl.BlockSpec((s, tf), lambda i, j, k: (0, j)),
         pl.BlockSpec((s, tf), lambda i, j, k: (0, j))],
        [jax.ShapeDtypeStruct((d, d_ff), BF16), jax.ShapeDtypeStruct((d, d_ff), BF16)], [w_tile, w_tile],
        [(0, 1, TN, 0), (0, 2, TN, 1)], 2, (twr, tf), 1, ep_store,
        _mm_vmem([((s, twr), BF16, 3), ((s, tf), BF16, 4), ((twr, tf), F32, 6)]),
        after=(red_down["token"],),
    )
    red_ffn = reduce_a("ffn_in", [W_GATE, W_UP], [gw_gate, gw_up])

    dx2, dx2b, dg_ffn = _rms_bwd("rms_ffn_bwd", x2, norm_ffn, dh2, dx3, after=(red_ffn["token"],))

    nj = d // tg

    def lo(j):
        return jnp.minimum(j, nj - 1)

    def gate_bwd_body(dx_ref, wo_ref, ga_ref, gb_ref, ya_ref, yb_ref, dya_ref, dyb_ref, dz_ref, keep):
        j = pl.program_id(1)

        @pl.when(j < nj)
        def _():
            dm = lax.dot_general(dx_ref[...], wo_ref[...], NT, preferred_element_type=F32)
            sa, sb = _sigmoid(ga_ref[...].astype(F32)), _sigmoid(gb_ref[...].astype(F32))
            dya_ref[...] = (dm * sa).astype(BF16)
            dyb_ref[...] = (dm * sb).astype(BF16)
            dz_ref[...] = (dm * ya_ref[...].astype(F32) * (sa * (1.0 - sa))).astype(BF16)
            keep[lo(j)] = (dm * yb_ref[...].astype(F32) * (sb * (1.0 - sb))).astype(BF16)

        @pl.when(j >= nj)
        def _():
            dz_ref[...] = keep[jnp.maximum(j - nj, 0)]

    t_lo = pl.BlockSpec((tm, tg), lambda i, j: (i, lo(j)))
    dya, dyb, dz = _pallas(
        gate_bwd_body,
        name="mm_dgate",
        grid=(s // tm, 2 * nj),
        in_specs=[
            pl.BlockSpec((tm, d), lambda i, j: (i, 0)),
            pl.BlockSpec((tg, d), lambda i, j: (lo(j), 0)),
            pl.BlockSpec((tm, tg), lambda i, j: (i, ga0 + lo(j))),
            pl.BlockSpec((tm, tg), lambda i, j: (i, gb0 + lo(j))),
            t_lo,
            t_lo,
        ],
        out_specs=[t_lo, t_lo, pl.BlockSpec((tm, tg), lambda i, j: (i, ga0 + j))],
        out_shape=[jax.ShapeDtypeStruct((s, d), BF16), jax.ShapeDtypeStruct((s, d), BF16), jax.ShapeDtypeStruct((s, n_in), BF16)],
        scratch_shapes=[pltpu.VMEM((nj, tm, tg), BF16)],
        compiler_params=_params(_mm_vmem([((tm, d), BF16, 2), ((tg, d), BF16, 2), ((tm, tg), F32, 14), ((nj, tm, tg), BF16, 1)])),
    )(dx2b, g_o, z, z, y_a, y_b)
    reduce_b(red_ffn, dya)
    reduce_c(red_down, red_ffn["token"])

    gw_o = _mm(
        "mm_gw_o", (d // twr, d // twn, 1), [m_act, dx2b],
        [pl.BlockSpec((s, twr), lambda i, j, k: (0, i)), pl.BlockSpec((s, twn), lambda i, j, k: (0, j))],
        [jax.ShapeDtypeStruct((d, d), BF16)], [pl.BlockSpec((twr, twn), lambda i, j, k: (i, j))],
        [(0, 1, TN, 0)], 1, (twr, twn), 1, ep_store,
        _mm_vmem([((s, twr), BF16, 3), ((s, twn), BF16, 2), ((twr, twn), F32, 3)]),
        after=(red_down["token"],),
    )[0]
    after_down = finish(red_down, gw_o)

    tb = _tile(w_sgu, (1024, 512))
    b_out = pl.BlockSpec((tm, tb), lambda i, j, k: (i, j))

    da, datt = _mm(
        "mm_dbranches", (s // tm, w_sgu // tb, 1), [dya, g_a, dyb, g_b],
        [pl.BlockSpec((tm, d), lambda i, j, k: (i, 0)), pl.BlockSpec((tb, d), lambda i, j, k: (j, 0)),
         pl.BlockSpec((tm, d), lambda i, j, k: (i, 0)), pl.BlockSpec((tb, d), lambda i, j, k: (j, 0))],
        [jax.ShapeDtypeStruct((s, w_sgu), BF16), jax.ShapeDtypeStruct((s, w_att), BF16)], [b_out, b_out],
        [(0, 1, NT, 0), (2, 3, NT, 1)], 2, (tm, tb), 1, ep_store,
        _mm_vmem([((tm, d), BF16, 4), ((tb, d), BF16, 4), ((tm, tb), F32, 6)]),
        after=(after_down,),
    )

    wb_tile = pl.BlockSpec((tb, twn), lambda i, j, k: (i, j))
    gw_a, gw_b = _mm(
        "mm_gw_branches", (w_sgu // tb, d // twn, 1), [a_act, dya, att, dyb],
        [pl.BlockSpec((s, tb), lambda i, j, k: (0, i)), pl.BlockSpec((s, twn), lambda i, j, k: (0, j)),
         pl.BlockSpec((s, tb), lambda i, j, k: (0, i)), pl.BlockSpec((s, twn), lambda i, j, k: (0, j))],
        [jax.ShapeDtypeStruct((w_sgu, d), BF16), jax.ShapeDtypeStruct((w_att, d), BF16)], [wb_tile, wb_tile],
        [(0, 1, TN, 0), (2, 3, TN, 1)], 2, (tb, twn), 1, ep_store,
        _mm_vmem([((s, tb), BF16, 5), ((s, twn), BF16, 4), ((tb, twn), F32, 6)]),
        after=(da,),
    )
    red_mix = reduce_a("mix", [W_O, W_A, W_B], [gw_o, gw_a, gw_b])

    dz, dws, dbs, dgain = _sgu_bwd(z, da, sgu_v_gain, ws_b, wst_b, b_col, w_sgu, dz, after=(red_mix["token"],))
    dz, dk_pad, dv_pad, dbias_tab, dsink = _attn_bwd(sink, z, k_pad, v_pad, bias_tab, datt, dz, grp, q_blk0)
    dz = _dkv_to_dz(dk_pad, dv_pad, dz, off_k // (2 * w_kv))
    drel = _relbias_bwd(dbias_tab, bucket)
    reduce_b(red_mix, dz)
    reduce_c(red_ffn, red_mix["token"])

    small_w = [norm_mix, sgu_v_gain, sgu_w_s, sgu_b_s, attn_sink, rel_bias, norm_ffn, norm_final]
    small_m = [m_norm_mix, m_sgu_v_gain, m_sgu_w_s, m_sgu_b_s, m_attn_sink, m_rel_bias, m_norm_ffn, m_norm_final]
    small_v = [v_norm_mix, v_sgu_v_gain, v_sgu_w_s, v_sgu_b_s, v_attn_sink, v_rel_bias, v_norm_ffn, v_norm_final]
    small_shapes = [w.shape for w in small_w]
    early = [dgain, dws, dbs, dsink[:, 0], drel[:, :REL_BUCKETS].T, dg_ffn, dg_final]
    p_early = _pack([g.reshape(shp) for g, shp in zip(early, small_shapes[1:])] + [loss_part[0, :1]])
    land = jnp.zeros((2 * N_CHIPS,) + p_early.shape, F32)
    sm_send, sm_recv, (p_early, land) = _split_start("small_send", [p_early, land], 2 * N_CHIPS - 1, _small_exchange_copies(False))

    tzn = _tile(n_in, (768, 640, 512))
    gw_in = _mm(
        "mm_gw_in", (d // twr, n_in // tzn, 1), [h1, dz],
        [pl.BlockSpec((s, twr), lambda i, j, k: (0, i)), pl.BlockSpec((s, tzn), lambda i, j, k: (0, j))],
        [jax.ShapeDtypeStruct((d, n_in), BF16)], [pl.BlockSpec((twr, tzn), lambda i, j, k: (i, j))],
        [(0, 1, TN, 0)], 1, (twr, tzn), 1, ep_store,
        _mm_vmem([((s, twr), BF16, 3), ((s, tzn), BF16, 2), ((twr, tzn), F32, 3)]),
        after=(red_ffn["token"], p_early),
    )[0]
    red_in = reduce_a("w_in", [W_IN], [gw_in])

    reduce_c(red_mix, red_in["token"])
    reduce_b(red_in, finish(red_mix, red_in["token"]))

    dh1 = _mm(
        "mm_dh1", (s // tm, d // tn2, 1), [dz, g_in],
        [pl.BlockSpec((tm, n_in), lambda i, j, k: (i, 0)), pl.BlockSpec((tn2, n_in), lambda i, j, k: (j, 0))],
        [jax.ShapeDtypeStruct((s, d), F32)], [pl.BlockSpec((tm, tn2), lambda i, j, k: (i, j))],
        [(0, 1, NT, 0)], 1, (tm, tn2), 1, ep_store,
        _mm_vmem([((tm, n_in), BF16, 2), ((tn2, n_in), BF16, 2), ((tm, tn2), F32, 5)]),
        after=(red_in["token"],),
    )[0]

    grad_x, _, dg_mix = _rms_bwd("rms_mix_bwd", x2d, norm_mix, dh1, dx2)

    p_mix = _pack([dg_mix.reshape(small_shapes[0])])
    land_mix = jnp.zeros((2 * N_CHIPS,) + p_mix.shape, F32)
    mx_send, mx_recv, (p_mix, land_mix) = _split_start("mix_send", [p_mix, land_mix], 2 * N_CHIPS - 1, _small_exchange_copies(False))

    reduce_c(red_in, finish(red_ffn, p_mix))
    p_early, land = _split_wait("small_wait", [p_early, land], sm_send, sm_recv, _small_exchange_copies(True), red_in["token"])
    p_mix, land_mix = _split_wait("mix_wait", [p_mix, land_mix], mx_send, mx_recv, _small_exchange_copies(True), p_early)
    me_idx = 2 * q_idx + c_idx
    packed_g = jnp.concatenate([_small_sum("mix_sum", me_idx, p_mix, land_mix), _small_sum("small_sum", me_idx, p_early, land)], axis=0)
    g_small = _unpack(packed_g, small_shapes + [(1,)])
    loss = g_small[-1].reshape(())
    g_small = g_small[:-1]
    zero1 = jnp.zeros((1,), F32)
    pw, pg, pm, pv = _pack(small_w + [zero1]), _pack(g_small + [zero1]), _pack(small_m + [zero1]), _pack(small_v + [zero1])
    small_upd = _adamw("adamw_small", pw, pg, pm, pv)
    d_small, nm_small, nv_small = [_unpack(a, small_shapes) for a in small_upd[:3]]
    finish(red_in, small_upd[0])

    small_names = ["norm_mix", "sgu_v_gain", "sgu_w_s", "sgu_b_s", "attn_sink", "rel_bias", "norm_ffn", "norm_final"]
    table = {}
    for i, n in enumerate(names):
        table[n] = (grads_big[i][None], upd[i][0][None], upd[i][1][None], upd[i][2][None])
    for i, n in enumerate(small_names):
        table[n] = (g_small[i], d_small[i], nm_small[i], nv_small[i])
    order = ["w_in", "norm_mix", "sgu_v_gain", "sgu_w_s", "sgu_b_s", "w_a", "attn_sink", "rel_bias", "w_b", "w_o", "norm_ffn",
             "w_gate", "w_up", "w_down", "norm_final"]
    outs = [loss, grad_x.reshape(1, s, d)]
    for part in range(4):
        outs += [table[n][part] for n in order]
    return tuple(outs)
```

```python
import math

import jax
import jax.numpy as jnp
import numpy as np
from jax import lax
from jax.experimental import pallas as pl
from jax.experimental.pallas import tpu as pltpu

F32 = jnp.float32
BF16 = jnp.bfloat16
I32 = jnp.int32
MESH = pl.DeviceIdType.MESH

EPS = 1e-6
NEG = -1e30
BLK = 128
HEAD_DIM = 128
N_KV_HEADS = 2
REL_BUCKETS = 32
REL_MAX_DIST = 128
N_CHIPS = 4
ADAM_LR, ADAM_B1, ADAM_B2, ADAM_EPS, ADAM_WD, ADAM_STEP = 0.001, 0.9, 0.999, 1e-08, 0.01, 10

LANES = 128
VMEM_CAP = 60 * 1024 * 1024

NN = (((1,), (0,)), ((), ()))
NT = (((1,), (1,)), ((), ()))
TN = (((0,), (0,)), ((), ()))
ANY = pl.BlockSpec(memory_space=pl.ANY)
HBM_SPEC = pl.BlockSpec(memory_space=pltpu.HBM)
SEM_SPEC = pl.BlockSpec(memory_space=pltpu.SEMAPHORE)
EFFECT = pltpu.SideEffectType.DATAFLOW_SIDE_EFFECTING


def _tile(n, cands):
    for t in cands:
        if n % t == 0:
            return t
    return n


PIN_BYTES = 64 * 1024


def _pin_hbm(a):
    big = hasattr(a, "dtype") and jnp.issubdtype(a.dtype, jnp.floating) and _nbytes(a.shape, a.dtype) >= PIN_BYTES
    return pltpu.with_memory_space_constraint(a, pltpu.HBM) if big else a


def _pallas(body, *, out_shape, **kw):
    def pin(o):
        big = isinstance(o, jax.ShapeDtypeStruct) and jnp.issubdtype(o.dtype, jnp.floating) and _nbytes(o.shape, o.dtype) >= PIN_BYTES
        return pltpu.HBM(o.shape, o.dtype) if big else o

    shapes = type(out_shape)(pin(o) for o in out_shape) if isinstance(out_shape, (list, tuple)) else pin(out_shape)
    call = pl.pallas_call(body, out_shape=shapes, **kw)
    return lambda *args: call(*[_pin_hbm(a) for a in args])


def _params(vmem_bytes=None, **kw):
    if vmem_bytes is not None:
        kw["vmem_limit_bytes"] = int(min(max(vmem_bytes, 32 * 1024 * 1024), VMEM_CAP))
    return pltpu.CompilerParams(**kw)


def _nbytes(shape, dtype):
    return int(np.prod(shape)) * jnp.dtype(dtype).itemsize


def _sigmoid(x):
    return 1.0 / (1.0 + jnp.exp(-x))


_GC = 0.7978845608028654
_GA = 0.044715


def _gelu(x):
    return 0.5 * x * (1.0 + jnp.tanh(_GC * (x + _GA * (x * x * x))))


def _gelu_grad(x):
    t = jnp.tanh(_GC * (x + _GA * (x * x * x)))
    return 0.5 * (1.0 + t) + 0.5 * x * (1.0 - t * t) * (_GC * (1.0 + 3.0 * _GA * (x * x)))


def _bf(v):
    return v if v.dtype == BF16 else v.astype(BF16)


def _mm(name, grid, ins, in_specs, out_shape, out_specs, pairs, n_acc, tile, nk, epilogue, vmem_bytes, after=()):
    assert nk == 1
    n_in, n_out = len(ins) + len(after), len(out_shape)

    def body(*refs):
        in_refs, out_refs = refs[:n_in], refs[n_in : n_in + n_out]
        vals = [None] * n_acc
        for a_i, b_i, dn, acc_i in pairs:
            d = lax.dot_general(_bf(in_refs[a_i][...]), _bf(in_refs[b_i][...]), dn, preferred_element_type=F32)
            vals[acc_i] = d if vals[acc_i] is None else vals[acc_i] + d
        epilogue(in_refs, vals, out_refs, slice(None))

    return _pallas(
        body,
        name=name,
        grid=grid,
        in_specs=list(in_specs) + [ANY] * len(after),
        out_specs=out_specs,
        out_shape=out_shape,
        compiler_params=_params(vmem_bytes),
    )(*ins, *after)


def _put(ref, cs, v):
    ref[:, cs] = v.astype(ref.dtype)


def _mm_vmem(tiles):
    return sum(_nbytes(s, d) * c for s, d, c in tiles) + 4 * 1024 * 1024


def _rows8(v):
    r, d = v.shape
    return v.reshape(r // 8, 8, d).sum(axis=0)


def _rms_fwd(name, x, g, after=()):
    s, d = x.shape
    tm = _tile(s, (256, 128))

    def body(x_ref, g_ref, *rest):
        h_ref = rest[-1]
        xv = x_ref[...]
        r = lax.rsqrt(jnp.mean(xv * xv, axis=-1, keepdims=True) + EPS)
        h_ref[...] = ((xv * r) * g_ref[...]).astype(BF16)

    return _pallas(
        body,
        name=name,
        grid=(s // tm,),
        in_specs=[pl.BlockSpec((tm, d), lambda i: (i, 0)), pl.BlockSpec((1, d), lambda i: (0, 0))] + [ANY] * len(after),
        out_specs=pl.BlockSpec((tm, d), lambda i: (i, 0)),
        out_shape=jax.ShapeDtypeStruct((s, d), BF16),
    )(x, g, *after)


def _rms_bwd(name, x, g, dh, dres, after=()):
    s, d = x.shape
    tm = _tile(s, (256, 128))
    n = s // tm
    n_after = len(after)

    def body(x_ref, g_ref, dh_ref, dres_ref, *rest):
        dx_ref, dxb_ref, dg_ref, acc_ref = rest[n_after:]
        i = pl.program_id(0)
        xv = x_ref[...]
        r = lax.rsqrt(jnp.mean(xv * xv, axis=-1, keepdims=True) + EPS)
        xh = xv * r
        dhv = dh_ref[...]
        dxh = dhv * g_ref[...]
        dx = r * (dxh - xh * jnp.mean(dxh * xh, axis=-1, keepdims=True)) + dres_ref[...]
        dx_ref[...] = dx
        dxb_ref[...] = dx.astype(BF16)
        part = _rows8(dhv * xh)

        @pl.when(i == 0)
        def _():
            acc_ref[...] = part

        @pl.when(i > 0)
        def _():
            acc_ref[...] += part

        @pl.when(i == n - 1)
        def _():
            dg_ref[...] = jnp.sum(acc_ref[...], axis=0, keepdims=True)

    row = pl.BlockSpec((tm, d), lambda i: (i, 0))
    vec = pl.BlockSpec((1, d), lambda i: (0, 0))
    return _pallas(
        body,
        name=name,
        grid=(n,),
        in_specs=[row, vec, row, row] + [ANY] * n_after,
        out_specs=[row, row, vec],
        out_shape=[jax.ShapeDtypeStruct((s, d), F32), jax.ShapeDtypeStruct((s, d), BF16), jax.ShapeDtypeStruct((1, d), F32)],
        scratch_shapes=[pltpu.VMEM((8, d), F32)],
    )(x, g, dh, dres, *after)


def _head(x3, g, target):
    s, d = x3.shape
    tm = _tile(s, (256, 128))
    n = s // tm

    def body(x_ref, g_ref, t_ref, dx_ref, dxb_ref, dg_ref, loss_ref, acc_g, acc_l):
        i = pl.program_id(0)
        xv = x_ref[...]
        gv = g_ref[...]
        r = lax.rsqrt(jnp.mean(xv * xv, axis=-1, keepdims=True) + EPS)
        xh = xv * r
        e = xh * gv - t_ref[...]
        dy = e * (1.0 / d)
        dxh = dy * gv
        dx = r * (dxh - xh * jnp.mean(dxh * xh, axis=-1, keepdims=True))
        dx_ref[...] = dx
        dxb_ref[...] = dx.astype(BF16)
        pg = _rows8(dy * xh)
        plo = _rows8(e * e)

        @pl.when(i == 0)
        def _():
            acc_g[...] = pg
            acc_l[...] = plo

        @pl.when(i > 0)
        def _():
            acc_g[...] += pg
            acc_l[...] += plo

        @pl.when(i == n - 1)
        def _():
            dg_ref[...] = jnp.sum(acc_g[...], axis=0, keepdims=True)
            loss_ref[...] = jnp.full((1, LANES), (0.5 / d) * jnp.sum(acc_l[...]), F32)

    row = pl.BlockSpec((tm, d), lambda i: (i, 0))
    vec = pl.BlockSpec((1, d), lambda i: (0, 0))
    return _pallas(
        body,
        name="head",
        grid=(n,),
        in_specs=[row, vec, row],
        out_specs=[row, row, vec, pl.BlockSpec((1, LANES), lambda i: (0, 0))],
        out_shape=[
            jax.ShapeDtypeStruct((s, d), F32),
            jax.ShapeDtypeStruct((s, d), BF16),
            jax.ShapeDtypeStruct((1, d), F32),
            jax.ShapeDtypeStruct((1, LANES), F32),
        ],
        scratch_shapes=[pltpu.VMEM((8, d), F32), pltpu.VMEM((8, d), F32)],
    )(x3, g, target)


def _sgu_fwd(z, gain, ws_b, b_col, w_sgu, after=()):
    s = z.shape[0]
    groups = ws_b.shape[0]

    def body(zu_ref, zv_ref, gain_ref, ws_ref, b_ref, *rest):
        a_ref = rest[-1]
        vv = _gelu(zv_ref[...].astype(F32))
        r = lax.rsqrt(jnp.mean(vv * vv, axis=-1, keepdims=True) + EPS)
        vn = ((vv * r) * gain_ref[...]).astype(BF16)
        u = _gelu(zu_ref[...].astype(F32))
        for g in range(groups):
            sl = slice(g * BLK, (g + 1) * BLK)
            mixed = jnp.dot(ws_ref[g], vn[:, sl], preferred_element_type=F32) + b_ref[g]
            a_ref[:, sl] = (u[:, sl] * mixed).astype(BF16)

    return _pallas(
        body,
        name="sgu_fwd",
        grid=(s // BLK,),
        in_specs=[
            pl.BlockSpec((BLK, w_sgu), lambda c: (c, 0)),
            pl.BlockSpec((BLK, w_sgu), lambda c: (c, 1)),
            pl.BlockSpec((1, w_sgu), lambda c: (0, 0)),
            pl.BlockSpec((groups, BLK, BLK), lambda c: (0, 0, 0)),
            pl.BlockSpec((groups, BLK, 1), lambda c: (0, 0, 0)),
        ]
        + [ANY] * len(after),
        out_specs=pl.BlockSpec((BLK, w_sgu), lambda c: (c, 0)),
        out_shape=jax.ShapeDtypeStruct((s, w_sgu), BF16),
    )(z, z, gain, ws_b, b_col, *after)


def _sgu_bwd(z, da, gain, ws_b, wst_b, b_col, w_sgu, dz, after=()):
    s = z.shape[0]
    groups = ws_b.shape[0]
    n = s // BLK
    n_skip = 1 + len(after)

    def body(zu_ref, zv_ref, da_ref, gain_ref, ws_ref, wst_ref, b_ref, *rest):
        dz_ref, dws_ref, dbs_ref, dgain_ref, acc_gain = rest[n_skip:]
        c = pl.program_id(0)
        zu = zu_ref[...].astype(F32)
        zv = zv_ref[...].astype(F32)
        gain_v = gain_ref[...]
        vv = _gelu(zv)
        r = lax.rsqrt(jnp.mean(vv * vv, axis=-1, keepdims=True) + EPS)
        xh = vv * r
        vn = (xh * gain_v).astype(BF16)
        u = _gelu(zu)
        dav = da_ref[...].astype(F32)
        dmix = dav * u
        dmix_b = dmix.astype(BF16)
        dvn_parts = []
        for g in range(groups):
            sl = slice(g * BLK, (g + 1) * BLK)
            mixed = jnp.dot(ws_ref[g], vn[:, sl], preferred_element_type=F32) + b_ref[g]
            dz_ref[:, sl] = (dav[:, sl] * mixed * _gelu_grad(zu[:, sl])).astype(BF16)
            dvn_parts.append(jnp.dot(wst_ref[g], dmix_b[:, sl], preferred_element_type=F32))
            dws_g = lax.dot_general(dmix_b[:, sl], vn[:, sl], NT, preferred_element_type=F32)
            dbs_g = jnp.sum(dmix[:, sl], axis=1, keepdims=True)

            @pl.when(c == 0)
            def _():
                dws_ref[g] = dws_g
                dbs_ref[g] = dbs_g

            @pl.when(c > 0)
            def _():
                dws_ref[g] += dws_g
                dbs_ref[g] += dbs_g

        dvn = jnp.concatenate(dvn_parts, axis=1)
        dxh = dvn * gain_v
        dvv = r * (dxh - xh * jnp.mean(dxh * xh, axis=-1, keepdims=True))
        dz_ref[:, w_sgu:] = (dvv * _gelu_grad(zv)).astype(BF16)
        pg = _rows8(dvn * xh)

        @pl.when(c == 0)
        def _():
            acc_gain[...] = pg

        @pl.when(c > 0)
        def _():
            acc_gain[...] += pg

        @pl.when(c == n - 1)
        def _():
            dgain_ref[...] = jnp.sum(acc_gain[...], axis=0, keepdims=True)

    full3 = pl.BlockSpec((groups, BLK, BLK), lambda c: (0, 0, 0))
    col3 = pl.BlockSpec((groups, BLK, 1), lambda c: (0, 0, 0))
    vec = pl.BlockSpec((1, w_sgu), lambda c: (0, 0))
    return _pallas(
        body,
        name="sgu_bwd",
        grid=(n,),
        in_specs=[
            pl.BlockSpec((BLK, w_sgu), lambda c: (c, 0)),
            pl.BlockSpec((BLK, w_sgu), lambda c: (c, 1)),
            pl.BlockSpec((BLK, w_sgu), lambda c: (c, 0)),
            vec,
            full3,
            full3,
            col3,
            ANY,
        ]
        + [ANY] * len(after),
        out_specs=[pl.BlockSpec((BLK, 2 * w_sgu), lambda c: (c, 0)), full3, col3, vec],
        out_shape=[
            jax.ShapeDtypeStruct(dz.shape, BF16),
            jax.ShapeDtypeStruct((groups, BLK, BLK), F32),
            jax.ShapeDtypeStruct((groups, BLK, 1), F32),
            jax.ShapeDtypeStruct((1, w_sgu), F32),
        ],
        scratch_shapes=[pltpu.VMEM((8, w_sgu), F32)],
        input_output_aliases={7: 0},
    )(z, z, da, gain, ws_b, wst_b, b_col, dz, *after)


def _attn_softmax(sink_ref, q_ref, k_ref, v_ref, bias_ref, s_len, grp):
    kv = pl.program_id(0)
    n = pl.program_id(1)
    start = pl.multiple_of(n * BLK, BLK)
    kb = k_ref[pl.ds(start, 3 * BLK), :]
    vb = v_ref[pl.ds(start, 3 * BLK), :]
    qv = q_ref[...]
    qs = jnp.concatenate([qv[:, g * HEAD_DIM : (g + 1) * HEAD_DIM] for g in range(grp)], axis=0).astype(BF16)
    sc = lax.dot_general(qs, kb, NT, preferred_element_type=F32) * (HEAD_DIM**-0.5)
    sc = sc + bias_ref[...].reshape(grp * BLK, 3 * BLK)
    kpos = start + lax.broadcasted_iota(I32, (1, 3 * BLK), 1) - BLK
    sc = jnp.where((kpos >= 0) & (kpos < s_len), sc, NEG)
    sink = jnp.concatenate([jnp.full((BLK, 1), sink_ref[kv * grp + g], F32) for g in range(grp)], axis=0)
    m = jnp.maximum(jnp.max(sc, axis=-1, keepdims=True), sink)
    p = jnp.exp(sc - m)
    esink = jnp.exp(sink - m)
    den = jnp.sum(p, axis=-1, keepdims=True) + esink
    return start, qs, kb, vb, p / den, esink / den


def _attn_specs(s, grp, q_blk0):
    qw = grp * HEAD_DIM
    return [
        pl.BlockSpec(memory_space=pltpu.SMEM),
        pl.BlockSpec((BLK, qw), lambda kv, n: (n, q_blk0 + kv)),
        pl.BlockSpec((s + 2 * BLK, HEAD_DIM), lambda kv, n: (0, kv)),
        pl.BlockSpec((s + 2 * BLK, HEAD_DIM), lambda kv, n: (0, kv)),
        pl.BlockSpec((grp, BLK, 3 * BLK), lambda kv, n: (kv, 0, 0)),
    ]


def _attn_fwd(sink, z, k_pad, v_pad, bias_tab, grp, q_blk0):
    s = z.shape[0]
    qw = grp * HEAD_DIM

    def body(sink_ref, q_ref, k_ref, v_ref, bias_ref, o_ref):
        _, _, _, vb, pn, _ = _attn_softmax(sink_ref, q_ref, k_ref, v_ref, bias_ref, s, grp)
        o = jnp.dot(pn.astype(BF16), vb, preferred_element_type=F32)
        for g in range(grp):
            o_ref[:, g * HEAD_DIM : (g + 1) * HEAD_DIM] = o[g * BLK : (g + 1) * BLK].astype(BF16)

    return _pallas(
        body,
        name="attn_fwd",
        grid=(N_KV_HEADS, s // BLK),
        in_specs=_attn_specs(s, grp, q_blk0),
        out_specs=pl.BlockSpec((BLK, qw), lambda kv, n: (n, kv)),
        out_shape=jax.ShapeDtypeStruct((s, N_KV_HEADS * qw), BF16),
    )(sink, z, k_pad, v_pad, bias_tab)


def _attn_bwd(sink, z, k_pad, v_pad, bias_tab, dout, dz, grp, q_blk0):
    s = z.shape[0]
    qw = grp * HEAD_DIM
    nb = s // BLK
    heads = N_KV_HEADS * grp

    def body(sink_ref, q_ref, k_ref, v_ref, bias_ref, do_ref, dz_in, dq_ref, dk_ref, dv_ref, dbias_ref, dsink_ref, dk_acc, dv_acc):
        del dz_in
        kv = pl.program_id(0)
        n = pl.program_id(1)
        start, qs, kb, vb, pn, psink = _attn_softmax(sink_ref, q_ref, k_ref, v_ref, bias_ref, s, grp)
        dov = do_ref[...]
        dos = jnp.concatenate([dov[:, g * HEAD_DIM : (g + 1) * HEAD_DIM] for g in range(grp)], axis=0)
        dp = lax.dot_general(dos, vb, NT, preferred_element_type=F32)
        dvb = lax.dot_general(pn.astype(BF16), dos, TN, preferred_element_type=F32)
        delta = jnp.sum(pn * dp, axis=-1, keepdims=True)
        ds = pn * (dp - delta)
        dsb = (ds * (HEAD_DIM**-0.5)).astype(BF16)
        dq = jnp.dot(dsb, kb, preferred_element_type=F32)
        dkb = lax.dot_general(dsb, qs, TN, preferred_element_type=F32)
        for g in range(grp):
            dq_ref[:, g * HEAD_DIM : (g + 1) * HEAD_DIM] = dq[g * BLK : (g + 1) * BLK].astype(BF16)

        @pl.when(n == 0)
        def _():
            dk_acc[...] = jnp.zeros_like(dk_acc)
            dv_acc[...] = jnp.zeros_like(dv_acc)
            dbias_ref[...] = jnp.zeros_like(dbias_ref)

        @pl.when((n == 0) & (kv == 0))
        def _():
            dsink_ref[...] = jnp.zeros_like(dsink_ref)

        dk_acc[pl.ds(start, 3 * BLK), :] += dkb
        dv_acc[pl.ds(start, 3 * BLK), :] += dvb
        dbias_ref[...] += ds.reshape(grp, BLK, 3 * BLK)
        row = lax.broadcasted_iota(I32, (heads, LANES), 0)
        sd = psink * delta
        upd = jnp.zeros((heads, LANES), F32)
        for g in range(grp):
            upd = jnp.where(row == kv * grp + g, -jnp.sum(sd[g * BLK : (g + 1) * BLK]), upd)
        dsink_ref[...] += upd

        @pl.when(n == nb - 1)
        def _():
            dk_ref[...] = dk_acc[...]
            dv_ref[...] = dv_acc[...]

    pad_spec = pl.BlockSpec((s + 2 * BLK, HEAD_DIM), lambda kv, n: (0, kv))
    kvw = N_KV_HEADS * HEAD_DIM
    return _pallas(
        body,
        name="attn_bwd",
        grid=(N_KV_HEADS, nb),
        in_specs=_attn_specs(s, grp, q_blk0) + [pl.BlockSpec((BLK, qw), lambda kv, n: (n, kv)), ANY],
        out_specs=[
            pl.BlockSpec((BLK, qw), lambda kv, n: (n, q_blk0 + kv)),
            pad_spec,
            pad_spec,
            pl.BlockSpec((grp, BLK, 3 * BLK), lambda kv, n: (kv, 0, 0)),
            pl.BlockSpec((heads, LANES), lambda kv, n: (0, 0)),
        ],
        out_shape=[
            jax.ShapeDtypeStruct(dz.shape, BF16),
            jax.ShapeDtypeStruct((s + 2 * BLK, kvw), F32),
            jax.ShapeDtypeStruct((s + 2 * BLK, kvw), F32),
            jax.ShapeDtypeStruct((heads, BLK, 3 * BLK), F32),
            jax.ShapeDtypeStruct((heads, LANES), F32),
        ],
        scratch_shapes=[pltpu.VMEM((s + 2 * BLK, HEAD_DIM), F32), pltpu.VMEM((s + 2 * BLK, HEAD_DIM), F32)],
        input_output_aliases={6: 0},
    )(sink, z, k_pad, v_pad, bias_tab, dout, dz)


def _dkv_to_dz(dk_pad, dv_pad, dz, blk_idx):
    s = dz.shape[0]
    kvw = dk_pad.shape[1]

    def body(dk_ref, dv_ref, dz_in, out_ref):
        del dz_in
        out_ref[:, :kvw] = dk_ref[...].astype(BF16)
        out_ref[:, kvw:] = dv_ref[...].astype(BF16)

    src = pl.BlockSpec((BLK, kvw), lambda i: (i + 1, 0))
    return _pallas(
        body,
        name="dkv_to_dz",
        grid=(s // BLK,),
        in_specs=[src, src, ANY],
        out_specs=pl.BlockSpec((BLK, 2 * kvw), lambda i: (i, blk_idx)),
        out_shape=jax.ShapeDtypeStruct(dz.shape, BF16),
        input_output_aliases={2: 0},
    )(dk_pad, dv_pad, dz)


def _relbias_bwd(dbias_tab, bucket):
    heads = dbias_tab.shape[0]

    def body(dt_ref, bk_ref, out_ref):
        lane = lax.broadcasted_iota(I32, (1, LANES), 1)
        bk = bk_ref[...]
        rows = []
        for h in range(heads):
            dt = dt_ref[h]
            acc = jnp.zeros((1, LANES), F32)
            for b in range(REL_BUCKETS):
                acc = jnp.where(lane == b, jnp.sum(jnp.where(bk == b, dt, 0.0)), acc)
            rows.append(acc)
        out_ref[...] = jnp.concatenate(rows, axis=0)

    return _pallas(body, name="relbias_bwd", out_shape=jax.ShapeDtypeStruct((heads, LANES), F32))(dbias_tab, bucket)


def _t5_bucket(rel):
    nb = REL_BUCKETS // 2
    ret = jnp.where(rel > 0, nb, 0)
    n = jnp.abs(rel)
    max_exact = nb // 2
    nf = jnp.maximum(n, 1).astype(F32)
    large = max_exact + (jnp.log(nf / max_exact) / math.log(REL_MAX_DIST / max_exact) * (nb - max_exact)).astype(I32)
    large = jnp.minimum(large, nb - 1)
    return ret + jnp.where(n < max_exact, n, large)


def _band_tables(rel_bias):
    qi = jnp.arange(BLK)[:, None]
    kj = jnp.arange(3 * BLK)[None, :]
    rel = kj - BLK - qi
    bucket = _t5_bucket(rel).astype(I32)
    heads = rel_bias.shape[1]
    masked = jnp.where(jnp.abs(rel) <= BLK, bucket, -1)

    def body(rb_ref, bk_ref, out_ref):
        bk = bk_ref[...]
        for h in range(heads):
            tab = jnp.full(bk.shape, NEG, F32)
            for b in range(REL_BUCKETS):
                tab = jnp.where(bk == b, rb_ref[b, h], tab)
            out_ref[h] = tab

    bias_tab = _pallas(
        body,
        name="bias_table",
        in_specs=[pl.BlockSpec(memory_space=pltpu.SMEM), pl.BlockSpec(memory_space=pltpu.VMEM)],
        out_specs=pl.BlockSpec(memory_space=pltpu.VMEM),
        out_shape=jax.ShapeDtypeStruct((heads, BLK, 3 * BLK), F32),
    )(rel_bias.astype(F32), masked)
    return bias_tab, bucket


EW_BLOCK_ELEMS = 512 * 1024


def _ew_tiles(shape, elems=EW_BLOCK_ELEMS // 2):
    r, c = shape
    tn = c if c <= 2048 else _tile(c, (2048, 1920, 1536, 1408, 1024, 512))
    tm = _tile(r, [t for t in (1024, 512, 256, 128, 64, 32, 16, 8) if t * tn <= elems] or [8])
    return tm, tn


def _cast_into_full(name, qidx, w, kind, after=()):
    r, c = w.shape
    tm, tn = _ew_tiles(w.shape, EW_BLOCK_ELEMS)
    nbi, nbj = r // tm, c // tn
    if kind == "col":
        full, out_spec = (r, c * N_CHIPS), pl.BlockSpec((tm, tn), lambda i, j, q: (i, q[0] * nbj + j))
    else:
        full, out_spec = (r * N_CHIPS, c), pl.BlockSpec((tm, tn), lambda i, j, q: (q[0] * nbi + i, j))

    def body(q_ref, w_ref, *rest):
        del q_ref
        rest[-1][...] = w_ref[...].astype(BF16)

    return _pallas(
        body,
        name=name,
        grid_spec=pltpu.PrefetchScalarGridSpec(
            num_scalar_prefetch=1,
            grid=(nbi, nbj),
            in_specs=[pl.BlockSpec((tm, tn), lambda i, j, q: (i, j))] + [ANY] * len(after),
            out_specs=out_spec,
        ),
        out_shape=jax.ShapeDtypeStruct(full, BF16),
    )(qidx, w, *after)


def _adamw(name, w, g, m, v, after=()):
    tm, tn = _ew_tiles(w.shape, EW_BLOCK_ELEMS)
    if _nbytes(w.shape, F32) <= 1024 * 1024:
        tm, tn = w.shape
    spec = pl.BlockSpec((tm, tn), lambda i, j: (i, j))
    n_after = len(after)

    def body(w_ref, g_ref, m_ref, v_ref, *rest):
        d_ref, nm_ref, nv_ref, g_out_ref = rest[n_after:]
        gv = g_ref[...]
        g_out_ref[...] = gv
        nm = ADAM_B1 * m_ref[...] + (1.0 - ADAM_B1) * gv
        nv = ADAM_B2 * v_ref[...] + (1.0 - ADAM_B2) * (gv * gv)
        m_hat = nm / (1.0 - ADAM_B1**ADAM_STEP)
        v_hat = nv / (1.0 - ADAM_B2**ADAM_STEP)
        d_ref[...] = -ADAM_LR * (m_hat / (jnp.sqrt(v_hat) + ADAM_EPS) + ADAM_WD * w_ref[...])
        nm_ref[...] = nm
        nv_ref[...] = nv

    out = jax.ShapeDtypeStruct(w.shape, F32)
    return _pallas(
        body, name=name, grid=(w.shape[0] // tm, w.shape[1] // tn), in_specs=[spec] * 4 + [ANY] * n_after,
        out_specs=[spec] * 4, out_shape=[out, out, out, out],
        compiler_params=_params(_mm_vmem([((tm, tn), F32, 24)])),
    )(w, g, m, v, *after)


def _pair_add(name, cidx, g_full, r_sib, kind):
    hr, hc = r_sib.shape
    tm, tn = _ew_tiles((hr, hc), 2 * EW_BLOCK_ELEMS)
    nbi, nbj = hr // tm, hc // tn
    if kind == "col":
        g_spec = pl.BlockSpec((tm, tn), lambda i, j, c: (c[0] * nbi + i, j))
    else:
        g_spec = pl.BlockSpec((tm, tn), lambda i, j, c: (i, c[0] * nbj + j))
    spec = pl.BlockSpec((tm, tn), lambda i, j, c: (i, j))

    def body(c_ref, g_ref, r_ref, o_ref):
        del c_ref
        o_ref[...] = (g_ref[...].astype(F32) + r_ref[...].astype(F32)).astype(BF16)

    return _pallas(
        body,
        name=name,
        grid_spec=pltpu.PrefetchScalarGridSpec(num_scalar_prefetch=1, grid=(nbi, nbj), in_specs=[g_spec, spec], out_specs=spec),
        out_shape=jax.ShapeDtypeStruct((hr, hc), BF16),
        compiler_params=_params(_mm_vmem([((tm, tn), BF16, 6), ((tm, tn), F32, 3)])),
    )(cidx, g_full, r_sib)


def _chip_sum(name, qidx, c_half, r_ici, kind):
    _, pr, pc = r_ici.shape
    tm, tn = _ew_tiles((pr, pc), 2 * EW_BLOCK_ELEMS)
    nbi, nbj = pr // tm, pc // tn
    if kind == "col":
        own_spec = pl.BlockSpec((tm, tn), lambda i, j, q: (i, q[0] * nbj + j))
        full, out_spec = (2 * pr, pc), pl.BlockSpec((tm, tn), lambda i, j, q: (q[1] * nbi + i, j))
    else:
        own_spec = pl.BlockSpec((tm, tn), lambda i, j, q: (q[0] * nbi + i, j))
        full, out_spec = (pr, 2 * pc), pl.BlockSpec((tm, tn), lambda i, j, q: (i, q[1] * nbj + j))

    def body(q_ref, own_ref, r_ref, o_ref):
        q = q_ref[0]
        own = own_ref[...].astype(F32)
        recv = [r_ref[r].astype(F32) for r in range(3)]
        total = None
        for chip in range(N_CHIPS):
            d = chip ^ q
            term = jnp.where(d == 0, own, jnp.where(d == 2, recv[0], jnp.where(d == 1, recv[1], recv[2])))
            total = term if total is None else total + term
        o_ref[...] = total

    return _pallas(
        body,
        name=name,
        grid_spec=pltpu.PrefetchScalarGridSpec(
            num_scalar_prefetch=1,
            grid=(nbi, nbj),
            in_specs=[own_spec, pl.BlockSpec((3, tm, tn), lambda i, j, q: (0, i, j))],
            out_specs=out_spec,
        ),
        out_shape=jax.ShapeDtypeStruct(full, F32),
        compiler_params=_params(_mm_vmem([((tm, tn), BF16, 8), ((tm, tn), F32, 6)])),
    )(qidx, c_half, r_ici)


_REL_MASK = (2, 1, 3)


def _place():
    x, y, c = lax.axis_index("x"), lax.axis_index("y"), lax.axis_index("c")
    chips = [(1 - x, y), (x, 1 - y), (1 - x, 1 - y)]
    return x, y, c, 2 * x + y, chips


def _shard_view(ref, kind, chip):
    if kind == "col":
        w = ref.shape[1] // N_CHIPS
        return ref.at[:, pl.ds(pl.multiple_of(chip * w, LANES), w)]
    h = ref.shape[0] // N_CHIPS
    return ref.at[pl.ds(pl.multiple_of(chip * h, 16), h), :]


def _row_half(ref, half):
    h = ref.shape[0] // 2
    return ref.at[pl.ds(pl.multiple_of(half * h, 16), h), :]


def _pair_half(ref, kind, half):
    if kind == "col":
        return _row_half(ref, half)
    w = ref.shape[1] // 2
    return ref.at[:, pl.ds(pl.multiple_of(half * w, LANES), w)]


def _remote(src, dst, send_sem, recv_sem, dev):
    return pltpu.make_async_remote_copy(src_ref=src, dst_ref=dst, send_sem=send_sem, recv_sem=recv_sem, device_id=dev, device_id_type=MESH)


def _hbm(a):
    return pltpu.with_memory_space_constraint(a, pltpu.HBM)


def _gather_start(name, fulls, kinds, rels=(0, 1, 2), after=()):
    n_w = len(fulls)

    def body(*refs):
        g = refs[:n_w]
        send_sem, recv_sem = refs[n_w + len(after)], refs[n_w + len(after) + 1]
        token = refs[-1]
        _, _, c, q, chips = _place()
        for w in range(n_w):
            mine = _row_half(_shard_view(g[w], kinds[w], q), c)
            for r in rels if isinstance(rels, tuple) else rels[w]:
                _remote(mine, mine, send_sem.at[3 * w + r], recv_sem.at[3 * w + r], (*chips[r], c)).start()
        token[...] = jnp.zeros_like(token)

    res = _pallas(
        body,
        name=name,
        out_shape=(
            pltpu.SemaphoreType.DMA((3 * n_w,)),
            pltpu.SemaphoreType.DMA((3 * n_w,)),
            *[pltpu.HBM(f.shape, f.dtype) for f in fulls],
            jax.ShapeDtypeStruct((8, LANES), F32),
        ),
        in_specs=[HBM_SPEC] * n_w + [ANY] * len(after),
        out_specs=(SEM_SPEC, SEM_SPEC, *[HBM_SPEC] * n_w, pl.BlockSpec(memory_space=pltpu.VMEM)),
        input_output_aliases={w: w + 2 for w in range(n_w)},
        compiler_params=pltpu.CompilerParams(has_side_effects=EFFECT),
    )(*[_hbm(f) for f in fulls], *after)
    return res[0], res[1], list(res[2 : 2 + n_w]), res[-1]


def _relay_copies(kinds, waiting):
    def copies(refs, send_sem, recv_sem):
        _, _, c, q, chips = _place()
        out = []
        for i, kind in enumerate(kinds):
            for k, (src_rel, dst_rel) in enumerate(((0, 1), (1, 0))):
                held = _row_half(_row_half(_shard_view(refs[i], kind, q ^ _REL_MASK[src_rel]), c), k)
                far = _row_half(_row_half(_shard_view(refs[i], kind, q ^ _REL_MASK[2]), c), k)
                dst = far if waiting else held
                out.append(_remote(held, dst, send_sem.at[2 * i + k], recv_sem.at[2 * i + k], (*chips[dst_rel], c)))
        return out

    return copies


def _gather_wait(name, fulls, kinds, w_ids, send_sem, recv_sem, after, rels=(0, 1, 2)):
    n = len(fulls)

    def body(*refs):
        g = refs[:n]
        s_sem, r_sem = refs[n], refs[n + 1]
        x, y, c, q, _ = _place()
        for i, w in enumerate(w_ids):
            mine = _row_half(_shard_view(g[i], kinds[i], q), c)
            for r in rels:
                landed = _row_half(_shard_view(g[i], kinds[i], q ^ _REL_MASK[r]), c)
                cp = _remote(mine, landed, s_sem.at[3 * w + r], r_sem.at[3 * w + r], (x, y, 1 - c))
                cp.wait_send()
                cp.wait_recv()

    res = _pallas(
        body,
        name=name,
        out_shape=[pltpu.HBM(f.shape, f.dtype) for f in fulls],
        in_specs=[HBM_SPEC] * n + [SEM_SPEC, SEM_SPEC, ANY],
        out_specs=[HBM_SPEC] * n,
        input_output_aliases={i: i for i in range(n)},
        compiler_params=pltpu.CompilerParams(has_side_effects=EFFECT),
    )(*fulls, send_sem, recv_sem, after)
    return list(res)


def _gather_forward(name, fulls, kinds):
    n = len(fulls)

    def body(*refs):
        g = refs[n : 2 * n]
        send, recv = refs[2 * n :]
        x, y, c, q, _ = _place()
        sib = (x, y, 1 - c)
        cps = []
        for i in range(n):
            for r in range(3):
                landed = _row_half(_shard_view(g[i], kinds[i], q ^ _REL_MASK[r]), c)
                cps.append(_remote(landed, landed, send.at[i, r], recv.at[i, r], sib))
        for cp in cps:
            cp.start()
        for i in range(n):
            for r in range(3):
                other = _row_half(_shard_view(g[i], kinds[i], q ^ _REL_MASK[r]), 1 - c)
                _remote(other, other, send.at[i, r], recv.at[i, r], sib).wait_recv()
        for cp in cps:
            cp.wait_send()

    res = _pallas(
        body,
        name=name,
        in_specs=[ANY] * n,
        out_specs=[ANY] * n,
        out_shape=[jax.ShapeDtypeStruct(f.shape, f.dtype) for f in fulls],
        scratch_shapes=[pltpu.SemaphoreType.DMA((n, 3)), pltpu.SemaphoreType.DMA((n, 3))],
        input_output_aliases={i: i for i in range(n)},
    )(*fulls)
    return list(res)


def _split_start(name, bufs, n_sems, copies):
    n = len(bufs)

    def body(*refs):
        for cp in copies(refs[:n], refs[n], refs[n + 1]):
            cp.start()

    res = _pallas(
        body,
        name=name,
        out_shape=(
            pltpu.SemaphoreType.DMA((n_sems,)),
            pltpu.SemaphoreType.DMA((n_sems,)),
            *[pltpu.HBM(b.shape, b.dtype) for b in bufs],
        ),
        in_specs=[HBM_SPEC] * n,
        out_specs=(SEM_SPEC, SEM_SPEC, *[HBM_SPEC] * n),
        input_output_aliases={i: i + 2 for i in range(n)},
        compiler_params=pltpu.CompilerParams(has_side_effects=EFFECT),
    )(*[_hbm(b) for b in bufs])
    return res[0], res[1], list(res[2:])


def _split_wait(name, bufs, send_sem, recv_sem, copies, after):
    n = len(bufs)

    def body(*refs):
        for cp in copies(refs[:n], refs[n], refs[n + 1]):
            cp.wait_send()
            cp.wait_recv()

    res = _pallas(
        body,
        name=name,
        out_shape=[pltpu.HBM(b.shape, b.dtype) for b in bufs],
        in_specs=[HBM_SPEC] * n + [SEM_SPEC, SEM_SPEC, ANY],
        out_specs=[HBM_SPEC] * n,
        input_output_aliases={i: i for i in range(n)},
        compiler_params=pltpu.CompilerParams(has_side_effects=EFFECT),
    )(*bufs, send_sem, recv_sem, after)
    return list(res)


def _pair_exchange_copies(kinds):
    n = len(kinds)

    def copies(refs, send_sem, recv_sem):
        x, y, c, _, _ = _place()
        return [
            _remote(_pair_half(refs[w], kinds[w], 1 - c), refs[n + w], send_sem.at[w], recv_sem.at[w], (x, y, 1 - c))
            for w in range(n)
        ]

    return copies


def _pair_share_copies(kinds, waiting):
    def copies(refs, send_sem, recv_sem):
        x, y, c, _, _ = _place()
        out = []
        for w, kind in enumerate(kinds):
            mine = _pair_half(refs[w], kind, c)
            dst = _pair_half(refs[w], kind, 1 - c) if waiting else mine
            out.append(_remote(mine, dst, send_sem.at[w], recv_sem.at[w], (x, y, 1 - c)))
        return out

    return copies


def _piece_shape(half_shape, kind):
    r, c = half_shape
    return (3, r, c // N_CHIPS) if kind == "col" else (3, r // N_CHIPS, c)


def _chip_send_start(name, halves, kinds):
    n = len(halves)
    lands = [lax.empty(_piece_shape(h.shape, k), BF16) for h, k in zip(halves, kinds)]

    def body(*refs):
        h, land = refs[:n], refs[n : 2 * n]
        send_sem, recv_sem = refs[2 * n], refs[2 * n + 1]
        _, _, c, q, chips = _place()
        for i in range(n):
            for r, chip in enumerate(chips):
                piece = _shard_view(h[i], kinds[i], q ^ _REL_MASK[r])
                _remote(piece, land[i].at[r], send_sem.at[3 * i + r], recv_sem.at[3 * i + r], (*chip, c)).start()

    res = _pallas(
        body,
        name=name,
        out_shape=(
            pltpu.SemaphoreType.DMA((3 * n,)),
            pltpu.SemaphoreType.DMA((3 * n,)),
            *[pltpu.HBM(a.shape, a.dtype) for a in halves],
            *[pltpu.HBM(a.shape, a.dtype) for a in lands],
        ),
        in_specs=[HBM_SPEC] * (2 * n),
        out_specs=(SEM_SPEC, SEM_SPEC, *[HBM_SPEC] * (2 * n)),
        input_output_aliases={i: i + 2 for i in range(2 * n)},
        compiler_params=pltpu.CompilerParams(has_side_effects=EFFECT),
    )(*[_hbm(a) for a in halves], *[_hbm(a) for a in lands])
    return res[0], res[1], list(res[2 : 2 + n]), list(res[2 + n :])


def _chip_send_wait(name, halves, lands, kinds, send_sem, recv_sem, after):
    n = len(halves)

    def body(*refs):
        h, land = refs[:n], refs[n : 2 * n]
        s_sem, r_sem = refs[2 * n], refs[2 * n + 1]
        x, y, c, q, _ = _place()
        for i in range(n):
            for r in range(3):
                piece = _shard_view(h[i], kinds[i], q ^ _REL_MASK[r])
                cp = _remote(piece, land[i].at[r], s_sem.at[3 * i + r], r_sem.at[3 * i + r], (x, y, 1 - c))
                cp.wait_send()
                cp.wait_recv()

    res = _pallas(
        body,
        name=name,
        out_shape=[pltpu.HBM(a.shape, a.dtype) for a in halves] + [pltpu.HBM(a.shape, a.dtype) for a in lands],
        in_specs=[HBM_SPEC] * (2 * n) + [SEM_SPEC, SEM_SPEC, ANY],
        out_specs=[HBM_SPEC] * (2 * n),
        input_output_aliases={i: i for i in range(2 * n)},
        compiler_params=pltpu.CompilerParams(has_side_effects=EFFECT),
    )(*halves, *lands, send_sem, recv_sem, after)
    return list(res[:n]), list(res[n:])


def _small_exchange_copies(waiting):
    def copies(refs, send_sem, recv_sem):
        p, land = refs
        x, y, c, q, _ = _place()
        me = 2 * q + c
        out = []
        for dd in range(1, 2 * N_CHIPS):
            dev = (x ^ ((dd >> 2) & 1), y ^ ((dd >> 1) & 1), c ^ (dd & 1))
            dst = land.at[me ^ dd] if waiting else land.at[me]
            out.append(_remote(p, dst, send_sem.at[dd - 1], recv_sem.at[dd - 1], dev))
        return out

    return copies


def _small_sum(name, me_idx, p, land):
    rows = p.shape[0]
    n_dev = 2 * N_CHIPS

    def body(me_ref, p_ref, land_ref, o_ref):
        me = me_ref[0]
        total = None
        for dev in range(n_dev):
            term = jnp.where(me == dev, p_ref[...], land_ref[dev])
            total = term if total is None else total + term
        o_ref[...] = total

    return _pallas(
        body,
        name=name,
        grid_spec=pltpu.PrefetchScalarGridSpec(
            num_scalar_prefetch=1,
            grid=(1,),
            in_specs=[pl.BlockSpec((rows, LANES), lambda i, m: (0, 0)), pl.BlockSpec((n_dev, rows, LANES), lambda i, m: (0, 0, 0))],
            out_specs=pl.BlockSpec((rows, LANES), lambda i, m: (0, 0)),
        ),
        out_shape=jax.ShapeDtypeStruct(p.shape, F32),
    )(me_idx, p, land)


def _pack(parts):
    rows = []
    for a in parts:
        flat = a.reshape(-1).astype(F32)
        n = flat.shape[0]
        padded = -(-n // (8 * LANES)) * (8 * LANES)
        rows.append(jnp.pad(flat, (0, padded - n)).reshape(-1, LANES))
    return jnp.concatenate(rows, axis=0)


def _unpack(packed, shapes):
    out, row = [], 0
    for shp in shapes:
        n = int(np.prod(shp))
        nrows = -(-n // (8 * LANES)) * 8
        out.append(packed[row : row + nrows].reshape(-1)[:n].reshape(shp))
        row += nrows
    return out


def kernel(x, w_in, norm_mix, sgu_v_gain, sgu_w_s, sgu_b_s, w_a_out, attn_sink, rel_bias, w_b_out, w_o, norm_ffn, w_gate, w_up, w_down, norm_final, loss_target, m_w_in, m_norm_mix, m_sgu_v_gain, m_sgu_w_s, m_sgu_b_s, m_w_a_out, m_attn_sink, m_rel_bias, m_w_b_out, m_w_o, m_norm_ffn, m_w_gate, m_w_up, m_w_down, m_norm_final, v_w_in, v_norm_mix, v_sgu_v_gain, v_sgu_w_s, v_sgu_b_s, v_w_a_out, v_attn_sink, v_rel_bias, v_w_b_out, v_w_o, v_norm_ffn, v_w_gate, v_w_up, v_w_down, v_norm_final):
    s, d = x.shape[1], x.shape[2]
    w_sgu = sgu_v_gain.shape[1]
    groups = sgu_w_s.shape[1]
    heads = attn_sink.shape[1]
    grp = heads // N_KV_HEADS
    w_att = heads * HEAD_DIM
    w_kv = N_KV_HEADS * HEAD_DIM
    d_ff = w_gate.shape[2] * N_CHIPS
    n_in = w_in.shape[2] * N_CHIPS
    off_q = 2 * w_sgu
    off_k = off_q + w_att
    off_g = off_k + 2 * w_kv
    assert n_in == off_g + 2 * d and groups * BLK == w_sgu and s % BLK == 0

    x2d = x.reshape(s, d)
    tgt = loss_target.reshape(s, d)
    c_idx = lax.axis_index("c").astype(I32).reshape(1)
    q_idx = (2 * lax.axis_index("x") + lax.axis_index("y")).astype(I32).reshape(1)
    qc_idx = jnp.concatenate([q_idx, c_idx])

    W_IN, W_A, W_B, W_O, W_GATE, W_UP, W_DOWN = range(7)
    names = ["w_in", "w_a", "w_b", "w_o", "w_gate", "w_up", "w_down"]
    kinds = ["col", "col", "col", "row", "col", "col", "row"]
    big_w = [w_in[0], w_a_out[0], w_b_out[0], w_o[0], w_gate[0], w_up[0], w_down[0]]
    big_m = [m_w_in[0], m_w_a_out[0], m_w_b_out[0], m_w_o[0], m_w_gate[0], m_w_up[0], m_w_down[0]]
    big_v = [v_w_in[0], v_w_a_out[0], v_w_b_out[0], v_w_o[0], v_w_gate[0], v_w_up[0], v_w_down[0]]
    full_in = _cast_into_full("cast_w_in", q_idx, big_w[W_IN], kinds[W_IN])
    in_send, in_recv, (full_in,), token = _gather_start("gather_start_in", [full_in], [kinds[W_IN]], rels=(0, 1))
    rest = [_cast_into_full("cast_" + names[i], q_idx, big_w[i], kinds[i], after=(token,)) for i in range(1, 7)]
    h1 = _rms_fwd("rms_mix", x2d, norm_mix, after=(token,))
    (full_in,) = _gather_wait("gather_wait_in", [full_in], [kinds[W_IN]], [0], in_send, in_recv, h1, rels=(0, 1))
    relay_send, relay_recv, (full_in,) = _split_start("gather_relay_in", [full_in], 2, _relay_copies([kinds[W_IN]], False))
    full_down = rest.pop()
    ag_send, ag_recv, rest, token = _gather_start("gather_start_rest", rest, kinds[1:6], rels=(0, 1), after=(full_in,))
    (full_in,) = _split_wait("gather_relay_wait_in", [full_in], relay_send, relay_recv, _relay_copies([kinds[W_IN]], True), token)
    (g_in,) = _gather_forward("gather_fwd_in", [full_in], [kinds[W_IN]])
    fulls = [g_in] + rest

    def relay_begin(tag, ids, after):
        ks = [kinds[i] for i in ids]
        bufs = _gather_wait("gather_wait_" + tag, [fulls[i] for i in ids], ks, [i - 1 for i in ids], ag_send, ag_recv, after,
                            rels=(0, 1))
        send, recv, bufs = _split_start("gather_relay_" + tag, bufs, 2 * len(ids), _relay_copies(ks, False))
        return tag, ks, send, recv, bufs

    def relay_end(state, after):
        tag, ks, send, recv, bufs = state
        bufs = _split_wait("gather_relay_wait_" + tag, bufs, send, recv, _relay_copies(ks, True), after)
        return _gather_forward("gather_fwd_" + tag, bufs, ks)

    ws_b = sgu_w_s[0].astype(BF16)
    wst_b = jnp.swapaxes(sgu_w_s[0], 1, 2).astype(BF16)
    b_col = sgu_b_s[0].reshape(groups, BLK, 1)
    bias_tab, bucket = _band_tables(rel_bias)
    sink = attn_sink[0]

    tm = _tile(s, (1024, 512, 256, 128))

    tn = _tile(n_in, (768, 640, 512))
    z = _mm(
        "mm_z", (s // tm, n_in // tn, 1), [h1, g_in],
        [pl.BlockSpec((tm, d), lambda i, j, k: (i, 0)), pl.BlockSpec((d, tn), lambda i, j, k: (0, j))],
        [jax.ShapeDtypeStruct((s, n_in), BF16)], [pl.BlockSpec((tm, tn), lambda i, j, k: (i, j))],
        [(0, 1, NN, 0)], 1, (tm, tn), 1, lambda ins, vals, outs, cs: _put(outs[0], cs, vals[0]),
        _mm_vmem([((tm, d), BF16, 2), ((d, tn), BF16, 2), ((tm, tn), F32, 3)]),
    )[0]
    mix_relay = relay_begin("mix", [W_A, W_B, W_O], z)

    a_act = _sgu_fwd(z, sgu_v_gain, ws_b, b_col, w_sgu, after=(mix_relay[4][0],))

    kv_b = z[:, off_k:off_g]
    k_pad = jnp.pad(kv_b[:, :w_kv], ((BLK, BLK), (0, 0)))
    v_pad = jnp.pad(kv_b[:, w_kv:], ((BLK, BLK), (0, 0)))
    q_blk0 = off_q // (grp * HEAD_DIM)
    att = _attn_fwd(sink, z, k_pad, v_pad, bias_tab, grp, q_blk0)

    ffn_relay = relay_begin("ffn_in", [W_GATE, W_UP], att)
    down_send, down_recv, (full_down,), token = _gather_start(
        "gather_start_down", [full_down], [kinds[W_DOWN]], after=(a_act, ffn_relay[4][0])
    )
    g_a, g_b, g_o = relay_end(mix_relay, token)

    tg = _tile(d, (512,))
    ga0, gb0 = off_g // tg, (off_g + d) // tg

    def ep_gate(ins, vals, outs, cs):
        sa, sb = _sigmoid(ins[4][:, cs].astype(F32)), _sigmoid(ins[5][:, cs].astype(F32))
        _put(outs[0], cs, sa * vals[0] + sb * vals[1])
        _put(outs[1], cs, vals[0])
        _put(outs[2], cs, vals[1])

    t_out = pl.BlockSpec((tm, tg), lambda i, j, k: (i, j))
    m_act, y_a, y_b = _mm(
        "mm_branches", (s // tm, d // tg, 1), [a_act, g_a, att, g_b, z, z],
        [pl.BlockSpec((tm, w_sgu), lambda i, j, k: (i, 0)), pl.BlockSpec((w_sgu, tg), lambda i, j, k: (0, j)),
         pl.BlockSpec((tm, w_att), lambda i, j, k: (i, 0)), pl.BlockSpec((w_att, tg), lambda i, j, k: (0, j)),
         pl.BlockSpec((tm, tg), lambda i, j, k: (i, ga0 + j)), pl.BlockSpec((tm, tg), lambda i, j, k: (i, gb0 + j))],
        [jax.ShapeDtypeStruct((s, d), BF16)] * 3,
        [t_out, t_out, t_out], [(0, 1, NN, 0), (2, 3, NN, 1)], 2, (tm, tg), 1, ep_gate,
        _mm_vmem([((tm, w_sgu), BF16, 4), ((w_sgu, tg), BF16, 4), ((tm, tg), F32, 12)]),    )

    tn = _tile(d, (1024, 512))

    def ep_residual(ins, vals, outs, cs):
        _put(outs[0], cs, ins[2][:, cs] + vals[0])

    x2 = _mm(
        "mm_wo", (s // tm, d // tn, 1), [m_act, g_o, x2d],
        [pl.BlockSpec((tm, d), lambda i, j, k: (i, 0)), pl.BlockSpec((d, tn), lambda i, j, k: (0, j)),
         pl.BlockSpec((tm, tn), lambda i, j, k: (i, j))],
        [jax.ShapeDtypeStruct((s, d), F32)], [pl.BlockSpec((tm, tn), lambda i, j, k: (i, j))],
        [(0, 1, NN, 0)], 1, (tm, tn), 1, ep_residual,
        _mm_vmem([((tm, d), BF16, 2), ((d, tn), BF16, 2), ((tm, tn), F32, 5)]),
    )[0]

    h2 = _rms_fwd("rms_ffn", x2, norm_ffn)
    g_gate, g_up = relay_end(ffn_relay, h2)

    tf = _tile(d_ff, (512,))

    def ep_swiglu(ins, vals, outs, cs):
        gt, up = vals
        _put(outs[0], cs, gt)
        _put(outs[1], cs, up)
        _put(outs[2], cs, (gt * _sigmoid(gt)) * up)

    f_out = pl.BlockSpec((tm, tf), lambda i, j, k: (i, j))
    gt, up, f_act = _mm(
        "mm_gate_up", (s // tm, d_ff // tf, 1), [h2, g_gate, g_up],
        [pl.BlockSpec((tm, d), lambda i, j, k: (i, 0)), pl.BlockSpec((d, tf), lambda i, j, k: (0, j)),
         pl.BlockSpec((d, tf), lambda i, j, k: (0, j))],
        [jax.ShapeDtypeStruct((s, d_ff), BF16)] * 3,
        [f_out, f_out, f_out], [(0, 1, NN, 0), (0, 2, NN, 1)], 2, (tm, tf), 1, ep_swiglu,
        _mm_vmem([((tm, d), BF16, 2), ((d, tf), BF16, 4), ((tm, tf), F32, 8)]),    )
    (g_down,) = _gather_forward(
        "gather_fwd_ffn_out",
        _gather_wait("gather_wait_ffn_out", [full_down], [kinds[W_DOWN]], [0], down_send, down_recv, f_act),
        [kinds[W_DOWN]],
    )

    tkf = _tile(d_ff, (1408, 1024, 512))
    tml, tnl = _tile(s, (512, 256, 128)), _tile(d, (512,))
    x3 = _mm(
        "mm_down", (s // tml, d // tnl, 1), [f_act, g_down, x2],
        [pl.BlockSpec((tml, d_ff), lambda i, j, k: (i, 0)), pl.BlockSpec((d_ff, tnl), lambda i, j, k: (0, j)),
         pl.BlockSpec((tml, tnl), lambda i, j, k: (i, j))],
        [jax.ShapeDtypeStruct((s, d), F32)], [pl.BlockSpec((tml, tnl), lambda i, j, k: (i, j))],
        [(0, 1, NN, 0)], 1, (tml, tnl), 1, ep_residual,
        _mm_vmem([((tml, d_ff), BF16, 2), ((d_ff, tnl), BF16, 2), ((tml, tnl), F32, 6)]),    )[0]

    dx3, dx3b, dg_final, loss_part = _head(x3, norm_final.reshape(1, d), tgt)

    def reduce_a(tag, ids, grads):
        ks = [kinds[i] for i in ids]
        lands = [lax.empty((g.shape[0] // 2, g.shape[1]) if k == "col" else (g.shape[0], g.shape[1] // 2), BF16)
                 for g, k in zip(grads, ks)]
        send, recv, bufs = _split_start("pair_send_" + tag, list(grads) + lands, len(ids), _pair_exchange_copies(ks))
        return {"tag": tag, "ids": ids, "ks": ks, "pair": (send, recv, bufs), "token": bufs[0]}

    def reduce_b(st, after):
        tag, ids, ks = st["tag"], st["ids"], st["ks"]
        send, recv, bufs = st["pair"]
        bufs = _split_wait("pair_wait_" + tag, bufs, send, recv, _pair_exchange_copies(ks), after)
        grads, from_sib = bufs[: len(ids)], bufs[len(ids) :]
        halves = [_pair_add("pair_add_" + names[i], c_idx, g, r, k) for i, g, r, k in zip(ids, grads, from_sib, ks)]
        st["chip"] = _chip_send_start("chip_send_" + tag, halves, ks)
        st["token"] = st["chip"][2][0]

    def reduce_c(st, after):
        tag, ids, ks = st["tag"], st["ids"], st["ks"]
        send, recv, halves, lands = st["chip"]
        halves, lands = _chip_send_wait("chip_wait_" + tag, halves, lands, ks, send, recv, after)
        pieces = [_chip_sum("chip_sum_" + names[i], qc_idx, h, r, k) for i, h, r, k in zip(ids, halves, lands, ks)]
        st["share"] = _split_start("share_send_" + tag, pieces, len(ids), _pair_share_copies(ks, False))
        st["token"] = st["share"][2][0]

    def reduce_d(st, after):
        send, recv, bufs = st["share"]
        return _split_wait("share_wait_" + st["tag"], bufs, send, recv, _pair_share_copies(st["ks"], True), after)

    grads_big, upd = [None] * 7, [None] * 7

    def finish(st, after):
        shared = reduce_d(st, after)
        after = shared[0]
        for i, g in zip(st["ids"], shared):
            upd[i] = _adamw("adamw_" + names[i], big_w[i], g, big_m[i], big_v[i], after=(after,))
            grads_big[i] = upd[i][3]
            after = upd[i][0]
        return after

    def ep_swiglu_bwd(ins, vals, outs, cs):
        df = vals[0]
        gtv, upv = ins[2][:, cs].astype(F32), ins[3][:, cs].astype(F32)
        sg = _sigmoid(gtv)
        _put(outs[0], cs, df * upv * (sg + gtv * sg * (1.0 - sg)))
        _put(outs[1], cs, df * (gtv * sg))

    dgt, dup = _mm(
        "mm_dswiglu", (s // tm, d_ff // tf, 1), [dx3b, g_down, gt, up],
        [pl.BlockSpec((tm, d), lambda i, j, k: (i, 0)), pl.BlockSpec((tf, d), lambda i, j, k: (j, 0)), f_out, f_out],
        [jax.ShapeDtypeStruct((s, d_ff), BF16), jax.ShapeDtypeStruct((s, d_ff), BF16)], [f_out, f_out],
        [(0, 1, NT, 0)], 1, (tm, tf), 1, ep_swiglu_bwd,
        _mm_vmem([((tm, d), BF16, 2), ((tf, d), BF16, 2), ((tm, tf), F32, 8)]),    )

    def ep_store(ins, vals, outs, cs):
        for o, v in zip(outs, vals):
            _put(o, cs, v)

    twn = _tile(d, (1024, 512))
    gw_down = _mm(
        "mm_gw_down", (d_ff // tkf, d // twn, 1), [f_act, dx3b],
        [pl.BlockSpec((s, tkf), lambda i, j, k: (0, i)), pl.BlockSpec((s, twn), lambda i, j, k: (0, j))],
        [jax.ShapeDtypeStruct((d_ff, d), BF16)], [pl.BlockSpec((tkf, twn), lambda i, j, k: (i, j))],
        [(0, 1, TN, 0)], 1, (tkf, twn), 1, ep_store,
        _mm_vmem([((s, tkf), BF16, 3), ((s, twn), BF16, 2), ((tkf, twn), F32, 3)]),
    )[0]
    red_down = reduce_a("down", [W_DOWN], [gw_down])

    tn2 = _tile(d, (256,))
    dh2_specs = [pl.BlockSpec((tm, d_ff), lambda i, j, k: (i, 0)), pl.BlockSpec((tn2, d_ff), lambda i, j, k: (j, 0))]
    dh2_tile = pl.BlockSpec((tm, tn2), lambda i, j, k: (i, j))
    dh2_vmem = _mm_vmem([((tm, d_ff), BF16, 2), ((tn2, d_ff), BF16, 2), ((tm, tn2), F32, 7)])
    dh2 = _mm(
        "mm_dh2_gate", (s // tm, d // tn2, 1), [dgt, g_gate], dh2_specs,
        [jax.ShapeDtypeStruct((s, d), F32)], [dh2_tile], [(0, 1, NT, 0)], 1, (tm, tn2), 1, ep_store, dh2_vmem,
        after=(red_down["token"],),
    )[0]
    dh2 = _mm(
        "mm_dh2_up", (s // tm, d // tn2, 1), [dup, g_up, dh2], dh2_specs + [dh2_tile],
        [jax.ShapeDtypeStruct((s, d), F32)], [dh2_tile], [(0, 1, NT, 0)], 1, (tm, tn2), 1, ep_residual, dh2_vmem,
    )[0]
    reduce_b(red_down, dh2)

    twr = _tile(d, (1024, 512))
    w_tile = pl.BlockSpec((twr, tf), lambda i, j, k: (i, j))
    gw_gate, gw_up = _mm(
        "mm_gw_gate_up", (d // twr, d_ff // tf, 1), [h2, dgt, dup],
        [pl.BlockSpec((s, twr), lambda i, j, k: (0, i)), pl.BlockSpec((s, tf), lambda i, j, k: (0, j)),
         pl.BlockSpec((s, tf), lambda i, j, k: (0, j))],
        [jax.ShapeDtypeStruct((d, d_ff), BF16), jax.ShapeDtypeStruct((d, d_ff), BF16)], [w_tile, w_tile],
        [(0, 1, TN, 0), (0, 2, TN, 1)], 2, (twr, tf), 1, ep_store,
        _mm_vmem([((s, twr), BF16, 3), ((s, tf), BF16, 4), ((twr, tf), F32, 6)]),
        after=(red_down["token"],),
    )
    red_ffn = reduce_a("ffn_in", [W_GATE, W_UP], [gw_gate, gw_up])

    dx2, dx2b, dg_ffn = _rms_bwd("rms_ffn_bwd", x2, norm_ffn, dh2, dx3, after=(red_ffn["token"],))

    nj = d // tg

    def lo(j):
        return jnp.minimum(j, nj - 1)

    def gate_bwd_body(dx_ref, wo_ref, ga_ref, gb_ref, ya_ref, yb_ref, dya_ref, dyb_ref, dz_ref, keep):
        j = pl.program_id(1)

        @pl.when(j < nj)
        def _():
            dm = lax.dot_general(dx_ref[...], wo_ref[...], NT, preferred_element_type=F32)
            sa, sb = _sigmoid(ga_ref[...].astype(F32)), _sigmoid(gb_ref[...].astype(F32))
            dya_ref[...] = (dm * sa).astype(BF16)
            dyb_ref[...] = (dm * sb).astype(BF16)
            dz_ref[...] = (dm * ya_ref[...].astype(F32) * (sa * (1.0 - sa))).astype(BF16)
            keep[lo(j)] = (dm * yb_ref[...].astype(F32) * (sb * (1.0 - sb))).astype(BF16)

        @pl.when(j >= nj)
        def _():
            dz_ref[...] = keep[jnp.maximum(j - nj, 0)]

    t_lo = pl.BlockSpec((tm, tg), lambda i, j: (i, lo(j)))
    dya, dyb, dz = _pallas(
        gate_bwd_body,
        name="mm_dgate",
        grid=(s // tm, 2 * nj),
        in_specs=[
            pl.BlockSpec((tm, d), lambda i, j: (i, 0)),
            pl.BlockSpec((tg, d), lambda i, j: (lo(j), 0)),
            pl.BlockSpec((tm, tg), lambda i, j: (i, ga0 + lo(j))),
            pl.BlockSpec((tm, tg), lambda i, j: (i, gb0 + lo(j))),
            t_lo,
            t_lo,
        ],
        out_specs=[t_lo, t_lo, pl.BlockSpec((tm, tg), lambda i, j: (i, ga0 + j))],
        out_shape=[jax.ShapeDtypeStruct((s, d), BF16), jax.ShapeDtypeStruct((s, d), BF16), jax.ShapeDtypeStruct((s, n_in), BF16)],
        scratch_shapes=[pltpu.VMEM((nj, tm, tg), BF16)],
        compiler_params=_params(_mm_vmem([((tm, d), BF16, 2), ((tg, d), BF16, 2), ((tm, tg), F32, 14), ((nj, tm, tg), BF16, 1)])),
    )(dx2b, g_o, z, z, y_a, y_b)
    reduce_b(red_ffn, dya)
    reduce_c(red_down, red_ffn["token"])

    gw_o = _mm(
        "mm_gw_o", (d // twr, d // twn, 1), [m_act, dx2b],
        [pl.BlockSpec((s, twr), lambda i, j, k: (0, i)), pl.BlockSpec((s, twn), lambda i, j, k: (0, j))],
        [jax.ShapeDtypeStruct((d, d), BF16)], [pl.BlockSpec((twr, twn), lambda i, j, k: (i, j))],
        [(0, 1, TN, 0)], 1, (twr, twn), 1, ep_store,
        _mm_vmem([((s, twr), BF16, 3), ((s, twn), BF16, 2), ((twr, twn), F32, 3)]),
        after=(red_down["token"],),
    )[0]
    after_down = finish(red_down, gw_o)

    tb = _tile(w_sgu, (1024, 512))
    b_out = pl.BlockSpec((tm, tb), lambda i, j, k: (i, j))

    da, datt = _mm(
        "mm_dbranches", (s // tm, w_sgu // tb, 1), [dya, g_a, dyb, g_b],
        [pl.BlockSpec((tm, d), lambda i, j, k: (i, 0)), pl.BlockSpec((tb, d), lambda i, j, k: (j, 0)),
         pl.BlockSpec((tm, d), lambda i, j, k: (i, 0)), pl.BlockSpec((tb, d), lambda i, j, k: (j, 0))],
        [jax.ShapeDtypeStruct((s, w_sgu), BF16), jax.ShapeDtypeStruct((s, w_att), BF16)], [b_out, b_out],
        [(0, 1, NT, 0), (2, 3, NT, 1)], 2, (tm, tb), 1, ep_store,
        _mm_vmem([((tm, d), BF16, 4), ((tb, d), BF16, 4), ((tm, tb), F32, 6)]),
        after=(after_down,),
    )

    wb_tile = pl.BlockSpec((tb, twn), lambda i, j, k: (i, j))
    gw_a, gw_b = _mm(
        "mm_gw_branches", (w_sgu // tb, d // twn, 1), [a_act, dya, att, dyb],
        [pl.BlockSpec((s, tb), lambda i, j, k: (0, i)), pl.BlockSpec((s, twn), lambda i, j, k: (0, j)),
         pl.BlockSpec((s, tb), lambda i, j, k: (0, i)), pl.BlockSpec((s, twn), lambda i, j, k: (0, j))],
        [jax.ShapeDtypeStruct((w_sgu, d), BF16), jax.ShapeDtypeStruct((w_att, d), BF16)], [wb_tile, wb_tile],
        [(0, 1, TN, 0), (2, 3, TN, 1)], 2, (tb, twn), 1, ep_store,
        _mm_vmem([((s, tb), BF16, 5), ((s, twn), BF16, 4), ((tb, twn), F32, 6)]),
        after=(da,),
    )
    red_mix = reduce_a("mix", [W_O, W_A, W_B], [gw_o, gw_a, gw_b])

    dz, dws, dbs, dgain = _sgu_bwd(z, da, sgu_v_gain, ws_b, wst_b, b_col, w_sgu, dz, after=(red_mix["token"],))
    dz, dk_pad, dv_pad, dbias_tab, dsink = _attn_bwd(sink, z, k_pad, v_pad, bias_tab, datt, dz, grp, q_blk0)
    dz = _dkv_to_dz(dk_pad, dv_pad, dz, off_k // (2 * w_kv))
    drel = _relbias_bwd(dbias_tab, bucket)
    reduce_b(red_mix, dz)
    reduce_c(red_ffn, red_mix["token"])

    small_w = [norm_mix, sgu_v_gain, sgu_w_s, sgu_b_s, attn_sink, rel_bias, norm_ffn, norm_final]
    small_m = [m_norm_mix, m_sgu_v_gain, m_sgu_w_s, m_sgu_b_s, m_attn_sink, m_rel_bias, m_norm_ffn, m_norm_final]
    small_v = [v_norm_mix, v_sgu_v_gain, v_sgu_w_s, v_sgu_b_s, v_attn_sink, v_rel_bias, v_norm_ffn, v_norm_final]
    small_shapes = [w.shape for w in small_w]
    early = [dgain, dws, dbs, dsink[:, 0], drel[:, :REL_BUCKETS].T, dg_ffn, dg_final]
    p_early = _pack([g.reshape(shp) for g, shp in zip(early, small_shapes[1:])] + [loss_part[0, :1]])
    land = jnp.zeros((2 * N_CHIPS,) + p_early.shape, F32)
    sm_send, sm_recv, (p_early, land) = _split_start("small_send", [p_early, land], 2 * N_CHIPS - 1, _small_exchange_copies(False))

    tzn = _tile(n_in, (768, 640, 512))
    gw_in = _mm(
        "mm_gw_in", (d // twr, n_in // tzn, 1), [h1, dz],
        [pl.BlockSpec((s, twr), lambda i, j, k: (0, i)), pl.BlockSpec((s, tzn), lambda i, j, k: (0, j))],
        [jax.ShapeDtypeStruct((d, n_in), BF16)], [pl.BlockSpec((twr, tzn), lambda i, j, k: (i, j))],
        [(0, 1, TN, 0)], 1, (twr, tzn), 1, ep_store,
        _mm_vmem([((s, twr), BF16, 3), ((s, tzn), BF16, 2), ((twr, tzn), F32, 3)]),
        after=(red_ffn["token"], p_early),
    )[0]
    red_in = reduce_a("w_in", [W_IN], [gw_in])

    reduce_c(red_mix, red_in["token"])
    reduce_b(red_in, finish(red_mix, red_in["token"]))

    dh1 = _mm(
        "mm_dh1", (s // tm, d // tn2, 1), [dz, g_in],
        [pl.BlockSpec((tm, n_in), lambda i, j, k: (i, 0)), pl.BlockSpec((tn2, n_in), lambda i, j, k: (j, 0))],
        [jax.ShapeDtypeStruct((s, d), F32)], [pl.BlockSpec((tm, tn2), lambda i, j, k: (i, j))],
        [(0, 1, NT, 0)], 1, (tm, tn2), 1, ep_store,
        _mm_vmem([((tm, n_in), BF16, 2), ((tn2, n_in), BF16, 2), ((tm, tn2), F32, 5)]),
        after=(red_in["token"],),
    )[0]

    grad_x, _, dg_mix = _rms_bwd("rms_mix_bwd", x2d, norm_mix, dh1, dx2)

    p_mix = _pack([dg_mix.reshape(small_shapes[0])])
    land_mix = jnp.zeros((2 * N_CHIPS,) + p_mix.shape, F32)
    mx_send, mx_recv, (p_mix, land_mix) = _split_start("mix_send", [p_mix, land_mix], 2 * N_CHIPS - 1, _small_exchange_copies(False))

    reduce_c(red_in, finish(red_ffn, p_mix))
    p_early, land = _split_wait("small_wait", [p_early, land], sm_send, sm_recv, _small_exchange_copies(True), red_in["token"])
    p_mix, land_mix = _split_wait("mix_wait", [p_mix, land_mix], mx_send, mx_recv, _small_exchange_copies(True), p_early)
    me_idx = 2 * q_idx + c_idx
    packed_g = jnp.concatenate([_small_sum("mix_sum", me_idx, p_mix, land_mix), _small_sum("small_sum", me_idx, p_early, land)], axis=0)
    g_small = _unpack(packed_g, small_shapes + [(1,)])
    loss = g_small[-1].reshape(())
    g_small = g_small[:-1]
    zero1 = jnp.zeros((1,), F32)
    pw, pg, pm, pv = _pack(small_w + [zero1]), _pack(g_small + [zero1]), _pack(small_m + [zero1]), _pack(small_v + [zero1])
    small_upd = _adamw("adamw_small", pw, pg, pm, pv)
    d_small, nm_small, nv_small = [_unpack(a, small_shapes) for a in small_upd[:3]]
    finish(red_in, small_upd[0])

    small_names = ["norm_mix", "sgu_v_gain", "sgu_w_s", "sgu_b_s", "attn_sink", "rel_bias", "norm_ffn", "norm_final"]
    table = {}
    for i, n in enumerate(names):
        table[n] = (grads_big[i][None], upd[i][0][None], upd[i][1][None], upd[i][2][None])
    for i, n in enumerate(small_names):
        table[n] = (g_small[i], d_small[i], nm_small[i], nv_small[i])
    order = ["w_in", "norm_mix", "sgu_v_gain", "sgu_w_s", "sgu_b_s", "w_a", "attn_sink", "rel_bias", "w_b", "w_o", "norm_ffn",
             "w_gate", "w_up", "w_down", "norm_final"]
    outs = [loss, grad_x.reshape(1, s, d)]
    for part in range(4):
        outs += [table[n][part] for n in order]
    return tuple(outs)
```

```python
import math

import jax
import jax.numpy as jnp
import numpy as np
from jax import lax
from jax.experimental import pallas as pl
from jax.experimental.pallas import tpu as pltpu

F32 = jnp.float32
BF16 = jnp.bfloat16
I32 = jnp.int32
MESH = pl.DeviceIdType.MESH

EPS = 1e-6
NEG = -1e30
BLK = 128
HEAD_DIM = 128
N_KV_HEADS = 2
REL_BUCKETS = 32
REL_MAX_DIST = 128
N_CHIPS = 4
ADAM_LR, ADAM_B1, ADAM_B2, ADAM_EPS, ADAM_WD, ADAM_STEP = 0.001, 0.9, 0.999, 1e-08, 0.01, 10

LANES = 128
VMEM_CAP = 60 * 1024 * 1024

NN = (((1,), (0,)), ((), ()))
NT = (((1,), (1,)), ((), ()))
TN = (((0,), (0,)), ((), ()))
ANY = pl.BlockSpec(memory_space=pl.ANY)
HBM_SPEC = pl.BlockSpec(memory_space=pltpu.HBM)
SEM_SPEC = pl.BlockSpec(memory_space=pltpu.SEMAPHORE)
EFFECT = pltpu.SideEffectType.DATAFLOW_SIDE_EFFECTING


def _tile(n, cands):
    for t in cands:
        if n % t == 0:
            return t
    return n


PIN_BYTES = 64 * 1024


def _pin_hbm(a):
    big = hasattr(a, "dtype") and jnp.issubdtype(a.dtype, jnp.floating) and _nbytes(a.shape, a.dtype) >= PIN_BYTES
    return pltpu.with_memory_space_constraint(a, pltpu.HBM) if big else a


def _pallas(body, *, out_shape, **kw):
    def pin(o):
        big = isinstance(o, jax.ShapeDtypeStruct) and jnp.issubdtype(o.dtype, jnp.floating) and _nbytes(o.shape, o.dtype) >= PIN_BYTES
        return pltpu.HBM(o.shape, o.dtype) if big else o

    shapes = type(out_shape)(pin(o) for o in out_shape) if isinstance(out_shape, (list, tuple)) else pin(out_shape)
    call = pl.pallas_call(body, out_shape=shapes, **kw)
    return lambda *args: call(*[_pin_hbm(a) for a in args])


def _params(vmem_bytes=None, **kw):
    if vmem_bytes is not None:
        kw["vmem_limit_bytes"] = int(min(max(vmem_bytes, 32 * 1024 * 1024), VMEM_CAP))
    return pltpu.CompilerParams(**kw)


def _nbytes(shape, dtype):
    return int(np.prod(shape)) * jnp.dtype(dtype).itemsize


def _sigmoid(x):
    return 1.0 / (1.0 + jnp.exp(-x))


_GC = 0.7978845608028654
_GA = 0.044715


def _gelu(x):
    return 0.5 * x * (1.0 + jnp.tanh(_GC * (x + _GA * (x * x * x))))


def _gelu_grad(x):
    t = jnp.tanh(_GC * (x + _GA * (x * x * x)))
    return 0.5 * (1.0 + t) + 0.5 * x * (1.0 - t * t) * (_GC * (1.0 + 3.0 * _GA * (x * x)))


def _bf(v):
    return v if v.dtype == BF16 else v.astype(BF16)


def _mm(name, grid, ins, in_specs, out_shape, out_specs, pairs, n_acc, tile, nk, epilogue, vmem_bytes, after=()):
    assert nk == 1
    n_in, n_out = len(ins) + len(after), len(out_shape)

    def body(*refs):
        in_refs, out_refs = refs[:n_in], refs[n_in : n_in + n_out]
        vals = [None] * n_acc
        for a_i, b_i, dn, acc_i in pairs:
            d = lax.dot_general(_bf(in_refs[a_i][...]), _bf(in_refs[b_i][...]), dn, preferred_element_type=F32)
            vals[acc_i] = d if vals[acc_i] is None else vals[acc_i] + d
        epilogue(in_refs, vals, out_refs, slice(None))

    return _pallas(
        body,
        name=name,
        grid=grid,
        in_specs=list(in_specs) + [ANY] * len(after),
        out_specs=out_specs,
        out_shape=out_shape,
        compiler_params=_params(vmem_bytes),
    )(*ins, *after)


def _put(ref, cs, v):
    ref[:, cs] = v.astype(ref.dtype)


def _mm_vmem(tiles):
    return sum(_nbytes(s, d) * c for s, d, c in tiles) + 4 * 1024 * 1024


def _rows8(v):
    r, d = v.shape
    return v.reshape(r // 8, 8, d).sum(axis=0)


def _rms_fwd(name, x, g, after=()):
    s, d = x.shape
    tm = _tile(s, (256, 128))

    def body(x_ref, g_ref, *rest):
        h_ref = rest[-1]
        xv = x_ref[...]
        r = lax.rsqrt(jnp.mean(xv * xv, axis=-1, keepdims=True) + EPS)
        h_ref[...] = ((xv * r) * g_ref[...]).astype(BF16)

    return _pallas(
        body,
        name=name,
        grid=(s // tm,),
        in_specs=[pl.BlockSpec((tm, d), lambda i: (i, 0)), pl.BlockSpec((1, d), lambda i: (0, 0))] + [ANY] * len(after),
        out_specs=pl.BlockSpec((tm, d), lambda i: (i, 0)),
        out_shape=jax.ShapeDtypeStruct((s, d), BF16),
    )(x, g, *after)


def _rms_bwd(name, x, g, dh, dres, after=()):
    s, d = x.shape
    tm = _tile(s, (256, 128))
    n = s // tm
    n_after = len(after)

    def body(x_ref, g_ref, dh_ref, dres_ref, *rest):
        dx_ref, dxb_ref, dg_ref, acc_ref = rest[n_after:]
        i = pl.program_id(0)
        xv = x_ref[...]
        r = lax.rsqrt(jnp.mean(xv * xv, axis=-1, keepdims=True) + EPS)
        xh = xv * r
        dhv = dh_ref[...]
        dxh = dhv * g_ref[...]
        dx = r * (dxh - xh * jnp.mean(dxh * xh, axis=-1, keepdims=True)) + dres_ref[...]
        dx_ref[...] = dx
        dxb_ref[...] = dx.astype(BF16)
        part = _rows8(dhv * xh)

        @pl.when(i == 0)
        def _():
            acc_ref[...] = part

        @pl.when(i > 0)
        def _():
            acc_ref[...] += part

        @pl.when(i == n - 1)
        def _():
            dg_ref[...] = jnp.sum(acc_ref[...], axis=0, keepdims=True)

    row = pl.BlockSpec((tm, d), lambda i: (i, 0))
    vec = pl.BlockSpec((1, d), lambda i: (0, 0))
    return _pallas(
        body,
        name=name,
        grid=(n,),
        in_specs=[row, vec, row, row] + [ANY] * n_after,
        out_specs=[row, row, vec],
        out_shape=[jax.ShapeDtypeStruct((s, d), F32), jax.ShapeDtypeStruct((s, d), BF16), jax.ShapeDtypeStruct((1, d), F32)],
        scratch_shapes=[pltpu.VMEM((8, d), F32)],
    )(x, g, dh, dres, *after)


def _head(x3, g, target):
    s, d = x3.shape
    tm = _tile(s, (256, 128))
    n = s // tm

    def body(x_ref, g_ref, t_ref, dx_ref, dxb_ref, dg_ref, loss_ref, acc_g, acc_l):
        i = pl.program_id(0)
        xv = x_ref[...]
        gv = g_ref[...]
        r = lax.rsqrt(jnp.mean(xv * xv, axis=-1, keepdims=True) + EPS)
        xh = xv * r
        e = xh * gv - t_ref[...]
        dy = e * (1.0 / d)
        dxh = dy * gv
        dx = r * (dxh - xh * jnp.mean(dxh * xh, axis=-1, keepdims=True))
        dx_ref[...] = dx
        dxb_ref[...] = dx.astype(BF16)
        pg = _rows8(dy * xh)
        plo = _rows8(e * e)

        @pl.when(i == 0)
        def _():
            acc_g[...] = pg
            acc_l[...] = plo

        @pl.when(i > 0)
        def _():
            acc_g[...] += pg
            acc_l[...] += plo

        @pl.when(i == n - 1)
        def _():
            dg_ref[...] = jnp.sum(acc_g[...], axis=0, keepdims=True)
            loss_ref[...] = jnp.full((1, LANES), (0.5 / d) * jnp.sum(acc_l[...]), F32)

    row = pl.BlockSpec((tm, d), lambda i: (i, 0))
    vec = pl.BlockSpec((1, d), lambda i: (0, 0))
    return _pallas(
        body,
        name="head",
        grid=(n,),
        in_specs=[row, vec, row],
        out_specs=[row, row, vec, pl.BlockSpec((1, LANES), lambda i: (0, 0))],
        out_shape=[
            jax.ShapeDtypeStruct((s, d), F32),
            jax.ShapeDtypeStruct((s, d), BF16),
            jax.ShapeDtypeStruct((1, d), F32),
            jax.ShapeDtypeStruct((1, LANES), F32),
        ],
        scratch_shapes=[pltpu.VMEM((8, d), F32), pltpu.VMEM((8, d), F32)],
    )(x3, g, target)


def _sgu_fwd(z, gain, ws_b, b_col, w_sgu, after=()):
    s = z.shape[0]
    groups = ws_b.shape[0]

    def body(zu_ref, zv_ref, gain_ref, ws_ref, b_ref, *rest):
        a_ref = rest[-1]
        vv = _gelu(zv_ref[...].astype(F32))
        r = lax.rsqrt(jnp.mean(vv * vv, axis=-1, keepdims=True) + EPS)
        vn = ((vv * r) * gain_ref[...]).astype(BF16)
        u = _gelu(zu_ref[...].astype(F32))
        for g in range(groups):
            sl = slice(g * BLK, (g + 1) * BLK)
            mixed = jnp.dot(ws_ref[g], vn[:, sl], preferred_element_type=F32) + b_ref[g]
            a_ref[:, sl] = (u[:, sl] * mixed).astype(BF16)

    return _pallas(
        body,
        name="sgu_fwd",
        grid=(s // BLK,),
        in_specs=[
            pl.BlockSpec((BLK, w_sgu), lambda c: (c, 0)),
            pl.BlockSpec((BLK, w_sgu), lambda c: (c, 1)),
            pl.BlockSpec((1, w_sgu), lambda c: (0, 0)),
            pl.BlockSpec((groups, BLK, BLK), lambda c: (0, 0, 0)),
            pl.BlockSpec((groups, BLK, 1), lambda c: (0, 0, 0)),
        ]
        + [ANY] * len(after),
        out_specs=pl.BlockSpec((BLK, w_sgu), lambda c: (c, 0)),
        out_shape=jax.ShapeDtypeStruct((s, w_sgu), BF16),
    )(z, z, gain, ws_b, b_col, *after)


def _sgu_bwd(z, da, gain, ws_b, wst_b, b_col, w_sgu, dz, after=()):
    s = z.shape[0]
    groups = ws_b.shape[0]
    n = s // BLK
    n_skip = 1 + len(after)

    def body(zu_ref, zv_ref, da_ref, gain_ref, ws_ref, wst_ref, b_ref, *rest):
        dz_ref, dws_ref, dbs_ref, dgain_ref, acc_gain = rest[n_skip:]
        c = pl.program_id(0)
        zu = zu_ref[...].astype(F32)
        zv = zv_ref[...].astype(F32)
        gain_v = gain_ref[...]
        vv = _gelu(zv)
        r = lax.rsqrt(jnp.mean(vv * vv, axis=-1, keepdims=True) + EPS)
        xh = vv * r
        vn = (xh * gain_v).astype(BF16)
        u = _gelu(zu)
        dav = da_ref[...].astype(F32)
        dmix = dav * u
        dmix_b = dmix.astype(BF16)
        dvn_parts = []
        for g in range(groups):
            sl = slice(g * BLK, (g + 1) * BLK)
            mixed = jnp.dot(ws_ref[g], vn[:, sl], preferred_element_type=F32) + b_ref[g]
            dz_ref[:, sl] = (dav[:, sl] * mixed * _gelu_grad(zu[:, sl])).astype(BF16)
            dvn_parts.append(jnp.dot(wst_ref[g], dmix_b[:, sl], preferred_element_type=F32))
            dws_g = lax.dot_general(dmix_b[:, sl], vn[:, sl], NT, preferred_element_type=F32)
            dbs_g = jnp.sum(dmix[:, sl], axis=1, keepdims=True)

            @pl.when(c == 0)
            def _():
                dws_ref[g] = dws_g
                dbs_ref[g] = dbs_g

            @pl.when(c > 0)
            def _():
                dws_ref[g] += dws_g
                dbs_ref[g] += dbs_g

        dvn = jnp.concatenate(dvn_parts, axis=1)
        dxh = dvn * gain_v
        dvv = r * (dxh - xh * jnp.mean(dxh * xh, axis=-1, keepdims=True))
        dz_ref[:, w_sgu:] = (dvv * _gelu_grad(zv)).astype(BF16)
        pg = _rows8(dvn * xh)

        @pl.when(c == 0)
        def _():
            acc_gain[...] = pg

        @pl.when(c > 0)
        def _():
            acc_gain[...] += pg

        @pl.when(c == n - 1)
        def _():
            dgain_ref[...] = jnp.sum(acc_gain[...], axis=0, keepdims=True)

    full3 = pl.BlockSpec((groups, BLK, BLK), lambda c: (0, 0, 0))
    col3 = pl.BlockSpec((groups, BLK, 1), lambda c: (0, 0, 0))
    vec = pl.BlockSpec((1, w_sgu), lambda c: (0, 0))
    return _pallas(
        body,
        name="sgu_bwd",
        grid=(n,),
        in_specs=[
            pl.BlockSpec((BLK, w_sgu), lambda c: (c, 0)),
            pl.BlockSpec((BLK, w_sgu), lambda c: (c, 1)),
            pl.BlockSpec((BLK, w_sgu), lambda c: (c, 0)),
            vec,
            full3,
            full3,
            col3,
            ANY,
        ]
        + [ANY] * len(after),
        out_specs=[pl.BlockSpec((BLK, 2 * w_sgu), lambda c: (c, 0)), full3, col3, vec],
        out_shape=[
            jax.ShapeDtypeStruct(dz.shape, BF16),
            jax.ShapeDtypeStruct((groups, BLK, BLK), F32),
            jax.ShapeDtypeStruct((groups, BLK, 1), F32),
            jax.ShapeDtypeStruct((1, w_sgu), F32),
        ],
        scratch_shapes=[pltpu.VMEM((8, w_sgu), F32)],
        input_output_aliases={7: 0},
    )(z, z, da, gain, ws_b, wst_b, b_col, dz, *after)


def _attn_softmax(sink_ref, q_ref, k_ref, v_ref, bias_ref, s_len, grp):
    kv = pl.program_id(0)
    n = pl.program_id(1)
    start = pl.multiple_of(n * BLK, BLK)
    kb = k_ref[pl.ds(start, 3 * BLK), :]
    vb = v_ref[pl.ds(start, 3 * BLK), :]
    qv = q_ref[...]
    qs = jnp.concatenate([qv[:, g * HEAD_DIM : (g + 1) * HEAD_DIM] for g in range(grp)], axis=0).astype(BF16)
    sc = lax.dot_general(qs, kb, NT, preferred_element_type=F32) * (HEAD_DIM**-0.5)
    sc = sc + bias_ref[...].reshape(grp * BLK, 3 * BLK)
    kpos = start + lax.broadcasted_iota(I32, (1, 3 * BLK), 1) - BLK
    sc = jnp.where((kpos >= 0) & (kpos < s_len), sc, NEG)
    sink = jnp.concatenate([jnp.full((BLK, 1), sink_ref[kv * grp + g], F32) for g in range(grp)], axis=0)
    m = jnp.maximum(jnp.max(sc, axis=-1, keepdims=True), sink)
    p = jnp.exp(sc - m)
    esink = jnp.exp(sink - m)
    den = jnp.sum(p, axis=-1, keepdims=True) + esink
    return start, qs, kb, vb, p / den, esink / den


def _attn_specs(s, grp, q_blk0):
    qw = grp * HEAD_DIM
    return [
        pl.BlockSpec(memory_space=pltpu.SMEM),
        pl.BlockSpec((BLK, qw), lambda kv, n: (n, q_blk0 + kv)),
        pl.BlockSpec((s + 2 * BLK, HEAD_DIM), lambda kv, n: (0, kv)),
        pl.BlockSpec((s + 2 * BLK, HEAD_DIM), lambda kv, n: (0, kv)),
        pl.BlockSpec((grp, BLK, 3 * BLK), lambda kv, n: (kv, 0, 0)),
    ]


def _attn_fwd(sink, z, k_pad, v_pad, bias_tab, grp, q_blk0):
    s = z.shape[0]
    qw = grp * HEAD_DIM

    def body(sink_ref, q_ref, k_ref, v_ref, bias_ref, o_ref):
        _, _, _, vb, pn, _ = _attn_softmax(sink_ref, q_ref, k_ref, v_ref, bias_ref, s, grp)
        o = jnp.dot(pn.astype(BF16), vb, preferred_element_type=F32)
        for g in range(grp):
            o_ref[:, g * HEAD_DIM : (g + 1) * HEAD_DIM] = o[g * BLK : (g + 1) * BLK].astype(BF16)

    return _pallas(
        body,
        name="attn_fwd",
        grid=(N_KV_HEADS, s // BLK),
        in_specs=_attn_specs(s, grp, q_blk0),
        out_specs=pl.BlockSpec((BLK, qw), lambda kv, n: (n, kv)),
        out_shape=jax.ShapeDtypeStruct((s, N_KV_HEADS * qw), BF16),
    )(sink, z, k_pad, v_pad, bias_tab)


def _attn_bwd(sink, z, k_pad, v_pad, bias_tab, dout, dz, grp, q_blk0):
    s = z.shape[0]
    qw = grp * HEAD_DIM
    nb = s // BLK
    heads = N_KV_HEADS * grp

    def body(sink_ref, q_ref, k_ref, v_ref, bias_ref, do_ref, dz_in, dq_ref, dk_ref, dv_ref, dbias_ref, dsink_ref, dk_acc, dv_acc):
        del dz_in
        kv = pl.program_id(0)
        n = pl.program_id(1)
        start, qs, kb, vb, pn, psink = _attn_softmax(sink_ref, q_ref, k_ref, v_ref, bias_ref, s, grp)
        dov = do_ref[...]
        dos = jnp.concatenate([dov[:, g * HEAD_DIM : (g + 1) * HEAD_DIM] for g in range(grp)], axis=0)
        dp = lax.dot_general(dos, vb, NT, preferred_element_type=F32)
        dvb = lax.dot_general(pn.astype(BF16), dos, TN, preferred_element_type=F32)
        delta = jnp.sum(pn * dp, axis=-1, keepdims=True)
        ds = pn * (dp - delta)
        dsb = (ds * (HEAD_DIM**-0.5)).astype(BF16)
        dq = jnp.dot(dsb, kb, preferred_element_type=F32)
        dkb = lax.dot_general(dsb, qs, TN, preferred_element_type=F32)
        for g in range(grp):
            dq_ref[:, g * HEAD_DIM : (g + 1) * HEAD_DIM] = dq[g * BLK : (g + 1) * BLK].astype(BF16)

        @pl.when(n == 0)
        def _():
            dk_acc[...] = jnp.zeros_like(dk_acc)
            dv_acc[...] = jnp.zeros_like(dv_acc)
            dbias_ref[...] = jnp.zeros_like(dbias_ref)

        @pl.when((n == 0) & (kv == 0))
        def _():
            dsink_ref[...] = jnp.zeros_like(dsink_ref)

        dk_acc[pl.ds(start, 3 * BLK), :] += dkb
        dv_acc[pl.ds(start, 3 * BLK), :] += dvb
        dbias_ref[...] += ds.reshape(grp, BLK, 3 * BLK)
        row = lax.broadcasted_iota(I32, (heads, LANES), 0)
        sd = psink * delta
        upd = jnp.zeros((heads, LANES), F32)
        for g in range(grp):
            upd = jnp.where(row == kv * grp + g, -jnp.sum(sd[g * BLK : (g + 1) * BLK]), upd)
        dsink_ref[...] += upd

        @pl.when(n == nb - 1)
        def _():
            dk_ref[...] = dk_acc[...]
            dv_ref[...] = dv_acc[...]

    pad_spec = pl.BlockSpec((s + 2 * BLK, HEAD_DIM), lambda kv, n: (0, kv))
    kvw = N_KV_HEADS * HEAD_DIM
    return _pallas(
        body,
        name="attn_bwd",
        grid=(N_KV_HEADS, nb),
        in_specs=_attn_specs(s, grp, q_blk0) + [pl.BlockSpec((BLK, qw), lambda kv, n: (n, kv)), ANY],
        out_specs=[
            pl.BlockSpec((BLK, qw), lambda kv, n: (n, q_blk0 + kv)),
            pad_spec,
            pad_spec,
            pl.BlockSpec((grp, BLK, 3 * BLK), lambda kv, n: (kv, 0, 0)),
            pl.BlockSpec((heads, LANES), lambda kv, n: (0, 0)),
        ],
        out_shape=[
            jax.ShapeDtypeStruct(dz.shape, BF16),
            jax.ShapeDtypeStruct((s + 2 * BLK, kvw), F32),
            jax.ShapeDtypeStruct((s + 2 * BLK, kvw), F32),
            jax.ShapeDtypeStruct((heads, BLK, 3 * BLK), F32),
            jax.ShapeDtypeStruct((heads, LANES), F32),
        ],
        scratch_shapes=[pltpu.VMEM((s + 2 * BLK, HEAD_DIM), F32), pltpu.VMEM((s + 2 * BLK, HEAD_DIM), F32)],
        input_output_aliases={6: 0},
    )(sink, z, k_pad, v_pad, bias_tab, dout, dz)


def _dkv_to_dz(dk_pad, dv_pad, dz, blk_idx):
    s = dz.shape[0]
    kvw = dk_pad.shape[1]

    def body(dk_ref, dv_ref, dz_in, out_ref):
        del dz_in
        out_ref[:, :kvw] = dk_ref[...].astype(BF16)
        out_ref[:, kvw:] = dv_ref[...].astype(BF16)

    src = pl.BlockSpec((BLK, kvw), lambda i: (i + 1, 0))
    return _pallas(
        body,
        name="dkv_to_dz",
        grid=(s // BLK,),
        in_specs=[src, src, ANY],
        out_specs=pl.BlockSpec((BLK, 2 * kvw), lambda i: (i, blk_idx)),
        out_shape=jax.ShapeDtypeStruct(dz.shape, BF16),
        input_output_aliases={2: 0},
    )(dk_pad, dv_pad, dz)


def _relbias_bwd(dbias_tab, bucket):
    heads = dbias_tab.shape[0]

    def body(dt_ref, bk_ref, out_ref):
        lane = lax.broadcasted_iota(I32, (1, LANES), 1)
        bk = bk_ref[...]
        rows = []
        for h in range(heads):
            dt = dt_ref[h]
            acc = jnp.zeros((1, LANES), F32)
            for b in range(REL_BUCKETS):
                acc = jnp.where(lane == b, jnp.sum(jnp.where(bk == b, dt, 0.0)), acc)
            rows.append(acc)
        out_ref[...] = jnp.concatenate(rows, axis=0)

    return _pallas(body, name="relbias_bwd", out_shape=jax.ShapeDtypeStruct((heads, LANES), F32))(dbias_tab, bucket)


def _t5_bucket(rel):
    nb = REL_BUCKETS // 2
    ret = jnp.where(rel > 0, nb, 0)
    n = jnp.abs(rel)
    max_exact = nb // 2
    nf = jnp.maximum(n, 1).astype(F32)
    large = max_exact + (jnp.log(nf / max_exact) / math.log(REL_MAX_DIST / max_exact) * (nb - max_exact)).astype(I32)
    large = jnp.minimum(large, nb - 1)
    return ret + jnp.where(n < max_exact, n, large)


def _band_tables(rel_bias):
    qi = jnp.arange(BLK)[:, None]
    kj = jnp.arange(3 * BLK)[None, :]
    rel = kj - BLK - qi
    bucket = _t5_bucket(rel).astype(I32)
    heads = rel_bias.shape[1]
    masked = jnp.where(jnp.abs(rel) <= BLK, bucket, -1)

    def body(rb_ref, bk_ref, out_ref):
        bk = bk_ref[...]
        for h in range(heads):
            tab = jnp.full(bk.shape, NEG, F32)
            for b in range(REL_BUCKETS):
                tab = jnp.where(bk == b, rb_ref[b, h], tab)
            out_ref[h] = tab

    bias_tab = _pallas(
        body,
        name="bias_table",
        in_specs=[pl.BlockSpec(memory_space=pltpu.SMEM), pl.BlockSpec(memory_space=pltpu.VMEM)],
        out_specs=pl.BlockSpec(memory_space=pltpu.VMEM),
        out_shape=jax.ShapeDtypeStruct((heads, BLK, 3 * BLK), F32),
    )(rel_bias.astype(F32), masked)
    return bias_tab, bucket


EW_BLOCK_ELEMS = 512 * 1024


def _ew_tiles(shape, elems=EW_BLOCK_ELEMS // 2):
    r, c = shape
    tn = c if c <= 2048 else _tile(c, (2048, 1920, 1536, 1408, 1024, 512))
    tm = _tile(r, [t for t in (1024, 512, 256, 128, 64, 32, 16, 8) if t * tn <= elems] or [8])
    return tm, tn


def _cast_into_full(name, qidx, w, kind, after=()):
    r, c = w.shape
    tm, tn = _ew_tiles(w.shape, EW_BLOCK_ELEMS)
    nbi, nbj = r // tm, c // tn
    if kind == "col":
        full, out_spec = (r, c * N_CHIPS), pl.BlockSpec((tm, tn), lambda i, j, q: (i, q[0] * nbj + j))
    else:
        full, out_spec = (r * N_CHIPS, c), pl.BlockSpec((tm, tn), lambda i, j, q: (q[0] * nbi + i, j))

    def body(q_ref, w_ref, *rest):
        del q_ref
        rest[-1][...] = w_ref[...].astype(BF16)

    return _pallas(
        body,
        name=name,
        grid_spec=pltpu.PrefetchScalarGridSpec(
            num_scalar_prefetch=1,
            grid=(nbi, nbj),
            in_specs=[pl.BlockSpec((tm, tn), lambda i, j, q: (i, j))] + [ANY] * len(after),
            out_specs=out_spec,
        ),
        out_shape=jax.ShapeDtypeStruct(full, BF16),
    )(qidx, w, *after)


def _adamw(name, w, g, m, v, after=()):
    tm, tn = _ew_tiles(w.shape, EW_BLOCK_ELEMS)
    if _nbytes(w.shape, F32) <= 1024 * 1024:
        tm, tn = w.shape
    spec = pl.BlockSpec((tm, tn), lambda i, j: (i, j))
    n_after = len(after)

    def body(w_ref, g_ref, m_ref, v_ref, *rest):
        d_ref, nm_ref, nv_ref, g_out_ref = rest[n_after:]
        gv = g_ref[...]
        g_out_ref[...] = gv
        nm = ADAM_B1 * m_ref[...] + (1.0 - ADAM_B1) * gv
        nv = ADAM_B2 * v_ref[...] + (1.0 - ADAM_B2) * (gv * gv)
        m_hat = nm / (1.0 - ADAM_B1**ADAM_STEP)
        v_hat = nv / (1.0 - ADAM_B2**ADAM_STEP)
        d_ref[...] = -ADAM_LR * (m_hat / (jnp.sqrt(v_hat) + ADAM_EPS) + ADAM_WD * w_ref[...])
        nm_ref[...] = nm
        nv_ref[...] = nv

    out = jax.ShapeDtypeStruct(w.shape, F32)
    return _pallas(
        body, name=name, grid=(w.shape[0] // tm, w.shape[1] // tn), in_specs=[spec] * 4 + [ANY] * n_after,
        out_specs=[spec] * 4, out_shape=[out, out, out, out],
        compiler_params=_params(_mm_vmem([((tm, tn), F32, 24)])),
    )(w, g, m, v, *after)


def _pair_add(name, cidx, g_full, r_sib, kind):
    hr, hc = r_sib.shape
    tm, tn = _ew_tiles((hr, hc), 2 * EW_BLOCK_ELEMS)
    nbi, nbj = hr // tm, hc // tn
    if kind == "col":
        g_spec = pl.BlockSpec((tm, tn), lambda i, j, c: (c[0] * nbi + i, j))
    else:
        g_spec = pl.BlockSpec((tm, tn), lambda i, j, c: (i, c[0] * nbj + j))
    spec = pl.BlockSpec((tm, tn), lambda i, j, c: (i, j))

    def body(c_ref, g_ref, r_ref, o_ref):
        del c_ref
        o_ref[...] = (g_ref[...].astype(F32) + r_ref[...].astype(F32)).astype(BF16)

    return _pallas(
        body,
        name=name,
        grid_spec=pltpu.PrefetchScalarGridSpec(num_scalar_prefetch=1, grid=(nbi, nbj), in_specs=[g_spec, spec], out_specs=spec),
        out_shape=jax.ShapeDtypeStruct((hr, hc), BF16),
        compiler_params=_params(_mm_vmem([((tm, tn), BF16, 6), ((tm, tn), F32, 3)])),
    )(cidx, g_full, r_sib)


def _chip_sum(name, qidx, c_half, r_ici, kind):
    _, pr, pc = r_ici.shape
    tm, tn = _ew_tiles((pr, pc), 2 * EW_BLOCK_ELEMS)
    nbi, nbj = pr // tm, pc // tn
    if kind == "col":
        own_spec = pl.BlockSpec((tm, tn), lambda i, j, q: (i, q[0] * nbj + j))
        full, out_spec = (2 * pr, pc), pl.BlockSpec((tm, tn), lambda i, j, q: (q[1] * nbi + i, j))
    else:
        own_spec = pl.BlockSpec((tm, tn), lambda i, j, q: (q[0] * nbi + i, j))
        full, out_spec = (pr, 2 * pc), pl.BlockSpec((tm, tn), lambda i, j, q: (i, q[1] * nbj + j))

    def body(q_ref, own_ref, r_ref, o_ref):
        q = q_ref[0]
        own = own_ref[...].astype(F32)
        recv = [r_ref[r].astype(F32) for r in range(3)]
        total = None
        for chip in range(N_CHIPS):
            d = chip ^ q
            term = jnp.where(d == 0, own, jnp.where(d == 2, recv[0], jnp.where(d == 1, recv[1], recv[2])))
            total = term if total is None else total + term
        o_ref[...] = total

    return _pallas(
        body,
        name=name,
        grid_spec=pltpu.PrefetchScalarGridSpec(
            num_scalar_prefetch=1,
            grid=(nbi, nbj),
            in_specs=[own_spec, pl.BlockSpec((3, tm, tn), lambda i, j, q: (0, i, j))],
            out_specs=out_spec,
        ),
        out_shape=jax.ShapeDtypeStruct(full, F32),
        compiler_params=_params(_mm_vmem([((tm, tn), BF16, 8), ((tm, tn), F32, 6)])),
    )(qidx, c_half, r_ici)


_REL_MASK = (2, 1, 3)


def _place():
    x, y, c = lax.axis_index("x"), lax.axis_index("y"), lax.axis_index("c")
    chips = [(1 - x, y), (x, 1 - y), (1 - x, 1 - y)]
    return x, y, c, 2 * x + y, chips


def _shard_view(ref, kind, chip):
    if kind == "col":
        w = ref.shape[1] // N_CHIPS
        return ref.at[:, pl.ds(pl.multiple_of(chip * w, LANES), w)]
    h = ref.shape[0] // N_CHIPS
    return ref.at[pl.ds(pl.multiple_of(chip * h, 16), h), :]


def _row_half(ref, half):
    h = ref.shape[0] // 2
    return ref.at[pl.ds(pl.multiple_of(half * h, 16), h), :]


def _pair_half(ref, kind, half):
    if kind == "col":
        return _row_half(ref, half)
    w = ref.shape[1] // 2
    return ref.at[:, pl.ds(pl.multiple_of(half * w, LANES), w)]


def _remote(src, dst, send_sem, recv_sem, dev):
    return pltpu.make_async_remote_copy(src_ref=src, dst_ref=dst, send_sem=send_sem, recv_sem=recv_sem, device_id=dev, device_id_type=MESH)


def _hbm(a):
    return pltpu.with_memory_space_constraint(a, pltpu.HBM)


def _gather_start(name, fulls, kinds, rels=(0, 1, 2), after=()):
    n_w = len(fulls)

    def body(*refs):
        g = refs[:n_w]
        send_sem, recv_sem = refs[n_w + len(after)], refs[n_w + len(after) + 1]
        token = refs[-1]
        _, _, c, q, chips = _place()
        for w in range(n_w):
            mine = _row_half(_shard_view(g[w], kinds[w], q), c)
            for r in rels if isinstance(rels, tuple) else rels[w]:
                _remote(mine, mine, send_sem.at[3 * w + r], recv_sem.at[3 * w + r], (*chips[r], c)).start()
        token[...] = jnp.zeros_like(token)

    res = _pallas(
        body,
        name=name,
        out_shape=(
            pltpu.SemaphoreType.DMA((3 * n_w,)),
            pltpu.SemaphoreType.DMA((3 * n_w,)),
            *[pltpu.HBM(f.shape, f.dtype) for f in fulls],
            jax.ShapeDtypeStruct((8, LANES), F32),
        ),
        in_specs=[HBM_SPEC] * n_w + [ANY] * len(after),
        out_specs=(SEM_SPEC, SEM_SPEC, *[HBM_SPEC] * n_w, pl.BlockSpec(memory_space=pltpu.VMEM)),
        input_output_aliases={w: w + 2 for w in range(n_w)},
        compiler_params=pltpu.CompilerParams(has_side_effects=EFFECT),
    )(*[_hbm(f) for f in fulls], *after)
    return res[0], res[1], list(res[2 : 2 + n_w]), res[-1]


def _relay_copies(kinds, waiting):
    def copies(refs, send_sem, recv_sem):
        _, _, c, q, chips = _place()
        out = []
        for i, kind in enumerate(kinds):
            for k, (src_rel, dst_rel) in enumerate(((0, 1), (1, 0))):
                held = _row_half(_row_half(_shard_view(refs[i], kind, q ^ _REL_MASK[src_rel]), c), k)
                far = _row_half(_row_half(_shard_view(refs[i], kind, q ^ _REL_MASK[2]), c), k)
                dst = far if waiting else held
                out.append(_remote(held, dst, send_sem.at[2 * i + k], recv_sem.at[2 * i + k], (*chips[dst_rel], c)))
        return out

    return copies


def _forward_copies(kinds, waiting):
    def copies(refs, send_sem, recv_sem):
        x, y, c, q, _ = _place()
        out = []
        for i, kind in enumerate(kinds):
            for r in range(3):
                quarter = _shard_view(refs[i], kind, q ^ _REL_MASK[r])
                landed = _row_half(quarter, c)
                dst = _row_half(quarter, 1 - c) if waiting else landed
                out.append(_remote(landed, dst, send_sem.at[3 * i + r], recv_sem.at[3 * i + r], (x, y, 1 - c)))
        return out

    return copies


def _gather_wait(name, fulls, kinds, w_ids, send_sem, recv_sem, after, rels=(0, 1, 2)):
    n = len(fulls)

    def body(*refs):
        g = refs[:n]
        s_sem, r_sem = refs[n], refs[n + 1]
        x, y, c, q, _ = _place()
        for i, w in enumerate(w_ids):
            mine = _row_half(_shard_view(g[i], kinds[i], q), c)
            for r in rels:
                landed = _row_half(_shard_view(g[i], kinds[i], q ^ _REL_MASK[r]), c)
                cp = _remote(mine, landed, s_sem.at[3 * w + r], r_sem.at[3 * w + r], (x, y, 1 - c))
                cp.wait_send()
                cp.wait_recv()

    res = _pallas(
        body,
        name=name,
        out_shape=[pltpu.HBM(f.shape, f.dtype) for f in fulls],
        in_specs=[HBM_SPEC] * n + [SEM_SPEC, SEM_SPEC, ANY],
        out_specs=[HBM_SPEC] * n,
        input_output_aliases={i: i for i in range(n)},
        compiler_params=pltpu.CompilerParams(has_side_effects=EFFECT),
    )(*fulls, send_sem, recv_sem, after)
    return list(res)


def _gather_forward(name, fulls, kinds):
    n = len(fulls)

    def body(*refs):
        g = refs[n : 2 * n]
        send, recv = refs[2 * n :]
        x, y, c, q, _ = _place()
        sib = (x, y, 1 - c)
        cps = []
        for i in range(n):
            for r in range(3):
                landed = _row_half(_shard_view(g[i], kinds[i], q ^ _REL_MASK[r]), c)
                cps.append(_remote(landed, landed, send.at[i, r], recv.at[i, r], sib))
        for cp in cps:
            cp.start()
        for i in range(n):
            for r in range(3):
                other = _row_half(_shard_view(g[i], kinds[i], q ^ _REL_MASK[r]), 1 - c)
                _remote(other, other, send.at[i, r], recv.at[i, r], sib).wait_recv()
        for cp in cps:
            cp.wait_send()

    res = _pallas(
        body,
        name=name,
        in_specs=[ANY] * n,
        out_specs=[ANY] * n,
        out_shape=[jax.ShapeDtypeStruct(f.shape, f.dtype) for f in fulls],
        scratch_shapes=[pltpu.SemaphoreType.DMA((n, 3)), pltpu.SemaphoreType.DMA((n, 3))],
        input_output_aliases={i: i for i in range(n)},
    )(*fulls)
    return list(res)


def _split_start(name, bufs, n_sems, copies):
    n = len(bufs)

    def body(*refs):
        for cp in copies(refs[:n], refs[n], refs[n + 1]):
            cp.start()

    res = _pallas(
        body,
        name=name,
        out_shape=(
            pltpu.SemaphoreType.DMA((n_sems,)),
            pltpu.SemaphoreType.DMA((n_sems,)),
            *[pltpu.HBM(b.shape, b.dtype) for b in bufs],
        ),
        in_specs=[HBM_SPEC] * n,
        out_specs=(SEM_SPEC, SEM_SPEC, *[HBM_SPEC] * n),
        input_output_aliases={i: i + 2 for i in range(n)},
        compiler_params=pltpu.CompilerParams(has_side_effects=EFFECT),
    )(*[_hbm(b) for b in bufs])
    return res[0], res[1], list(res[2:])


def _split_wait(name, bufs, send_sem, recv_sem, copies, after):
    n = len(bufs)

    def body(*refs):
        for cp in copies(refs[:n], refs[n], refs[n + 1]):
            cp.wait_send()
            cp.wait_recv()

    res = _pallas(
        body,
        name=name,
        out_shape=[pltpu.HBM(b.shape, b.dtype) for b in bufs],
        in_specs=[HBM_SPEC] * n + [SEM_SPEC, SEM_SPEC, ANY],
        out_specs=[HBM_SPEC] * n,
        input_output_aliases={i: i for i in range(n)},
        compiler_params=pltpu.CompilerParams(has_side_effects=EFFECT),
    )(*bufs, send_sem, recv_sem, after)
    return list(res)


def _pair_exchange_copies(kinds):
    n = len(kinds)

    def copies(refs, send_sem, recv_sem):
        x, y, c, _, _ = _place()
        return [
            _remote(_pair_half(refs[w], kinds[w], 1 - c), refs[n + w], send_sem.at[w], recv_sem.at[w], (x, y, 1 - c))
            for w in range(n)
        ]

    return copies


def _pair_share_copies(kinds, waiting):
    def copies(refs, send_sem, recv_sem):
        x, y, c, _, _ = _place()
        out = []
        for w, kind in enumerate(kinds):
            mine = _pair_half(refs[w], kind, c)
            dst = _pair_half(refs[w], kind, 1 - c) if waiting else mine
            out.append(_remote(mine, dst, send_sem.at[w], recv_sem.at[w], (x, y, 1 - c)))
        return out

    return copies


def _piece_shape(half_shape, kind):
    r, c = half_shape
    return (3, r, c // N_CHIPS) if kind == "col" else (3, r // N_CHIPS, c)


def _chip_send_start(name, halves, kinds):
    n = len(halves)
    lands = [lax.empty(_piece_shape(h.shape, k), BF16) for h, k in zip(halves, kinds)]

    def body(*refs):
        h, land = refs[:n], refs[n : 2 * n]
        send_sem, recv_sem = refs[2 * n], refs[2 * n + 1]
        _, _, c, q, chips = _place()
        for i in range(n):
            for r, chip in enumerate(chips):
                piece = _shard_view(h[i], kinds[i], q ^ _REL_MASK[r])
                _remote(piece, land[i].at[r], send_sem.at[3 * i + r], recv_sem.at[3 * i + r], (*chip, c)).start()

    res = _pallas(
        body,
        name=name,
        out_shape=(
            pltpu.SemaphoreType.DMA((3 * n,)),
            pltpu.SemaphoreType.DMA((3 * n,)),
            *[pltpu.HBM(a.shape, a.dtype) for a in halves],
            *[pltpu.HBM(a.shape, a.dtype) for a in lands],
        ),
        in_specs=[HBM_SPEC] * (2 * n),
        out_specs=(SEM_SPEC, SEM_SPEC, *[HBM_SPEC] * (2 * n)),
        input_output_aliases={i: i + 2 for i in range(2 * n)},
        compiler_params=pltpu.CompilerParams(has_side_effects=EFFECT),
    )(*[_hbm(a) for a in halves], *[_hbm(a) for a in lands])
    return res[0], res[1], list(res[2 : 2 + n]), list(res[2 + n :])


def _chip_send_wait(name, halves, lands, kinds, send_sem, recv_sem, after):
    n = len(halves)

    def body(*refs):
        h, land = refs[:n], refs[n : 2 * n]
        s_sem, r_sem = refs[2 * n], refs[2 * n + 1]
        x, y, c, q, _ = _place()
        for i in range(n):
            for r in range(3):
                piece = _shard_view(h[i], kinds[i], q ^ _REL_MASK[r])
                cp = _remote(piece, land[i].at[r], s_sem.at[3 * i + r], r_sem.at[3 * i + r], (x, y, 1 - c))
                cp.wait_send()
                cp.wait_recv()

    res = _pallas(
        body,
        name=name,
        out_shape=[pltpu.HBM(a.shape, a.dtype) for a in halves] + [pltpu.HBM(a.shape, a.dtype) for a in lands],
        in_specs=[HBM_SPEC] * (2 * n) + [SEM_SPEC, SEM_SPEC, ANY],
        out_specs=[HBM_SPEC] * (2 * n),
        input_output_aliases={i: i for i in range(2 * n)},
        compiler_params=pltpu.CompilerParams(has_side_effects=EFFECT),
    )(*halves, *lands, send_sem, recv_sem, after)
    return list(res[:n]), list(res[n:])


def _small_exchange_copies(waiting):
    def copies(refs, send_sem, recv_sem):
        p, land = refs
        x, y, c, q, _ = _place()
        me = 2 * q + c
        out = []
        for dd in range(1, 2 * N_CHIPS):
            dev = (x ^ ((dd >> 2) & 1), y ^ ((dd >> 1) & 1), c ^ (dd & 1))
            dst = land.at[me ^ dd] if waiting else land.at[me]
            out.append(_remote(p, dst, send_sem.at[dd - 1], recv_sem.at[dd - 1], dev))
        return out

    return copies


def _small_sum(name, me_idx, p, land):
    rows = p.shape[0]
    n_dev = 2 * N_CHIPS

    def body(me_ref, p_ref, land_ref, o_ref):
        me = me_ref[0]
        total = None
        for dev in range(n_dev):
            term = jnp.where(me == dev, p_ref[...], land_ref[dev])
            total = term if total is None else total + term
        o_ref[...] = total

    return _pallas(
        body,
        name=name,
        grid_spec=pltpu.PrefetchScalarGridSpec(
            num_scalar_prefetch=1,
            grid=(1,),
            in_specs=[pl.BlockSpec((rows, LANES), lambda i, m: (0, 0)), pl.BlockSpec((n_dev, rows, LANES), lambda i, m: (0, 0, 0))],
            out_specs=pl.BlockSpec((rows, LANES), lambda i, m: (0, 0)),
        ),
        out_shape=jax.ShapeDtypeStruct(p.shape, F32),
    )(me_idx, p, land)


def _pack(parts):
    rows = []
    for a in parts:
        flat = a.reshape(-1).astype(F32)
        n = flat.shape[0]
        padded = -(-n // (8 * LANES)) * (8 * LANES)
        rows.append(jnp.pad(flat, (0, padded - n)).reshape(-1, LANES))
    return jnp.concatenate(rows, axis=0)


def _unpack(packed, shapes):
    out, row = [], 0
    for shp in shapes:
        n = int(np.prod(shp))
        nrows = -(-n // (8 * LANES)) * 8
        out.append(packed[row : row + nrows].reshape(-1)[:n].reshape(shp))
        row += nrows
    return out


def kernel(x, w_in, norm_mix, sgu_v_gain, sgu_w_s, sgu_b_s, w_a_out, attn_sink, rel_bias, w_b_out, w_o, norm_ffn, w_gate, w_up, w_down, norm_final, loss_target, m_w_in, m_norm_mix, m_sgu_v_gain, m_sgu_w_s, m_sgu_b_s, m_w_a_out, m_attn_sink, m_rel_bias, m_w_b_out, m_w_o, m_norm_ffn, m_w_gate, m_w_up, m_w_down, m_norm_final, v_w_in, v_norm_mix, v_sgu_v_gain, v_sgu_w_s, v_sgu_b_s, v_w_a_out, v_attn_sink, v_rel_bias, v_w_b_out, v_w_o, v_norm_ffn, v_w_gate, v_w_up, v_w_down, v_norm_final):
    s, d = x.shape[1], x.shape[2]
    w_sgu = sgu_v_gain.shape[1]
    groups = sgu_w_s.shape[1]
    heads = attn_sink.shape[1]
    grp = heads // N_KV_HEADS
    w_att = heads * HEAD_DIM
    w_kv = N_KV_HEADS * HEAD_DIM
    d_ff = w_gate.shape[2] * N_CHIPS
    n_in = w_in.shape[2] * N_CHIPS
    off_q = 2 * w_sgu
    off_k = off_q + w_att
    off_g = off_k + 2 * w_kv
    assert n_in == off_g + 2 * d and groups * BLK == w_sgu and s % BLK == 0

    x2d = x.reshape(s, d)
    tgt = loss_target.reshape(s, d)
    c_idx = lax.axis_index("c").astype(I32).reshape(1)
    q_idx = (2 * lax.axis_index("x") + lax.axis_index("y")).astype(I32).reshape(1)
    qc_idx = jnp.concatenate([q_idx, c_idx])

    W_IN, W_A, W_B, W_O, W_GATE, W_UP, W_DOWN = range(7)
    names = ["w_in", "w_a", "w_b", "w_o", "w_gate", "w_up", "w_down"]
    kinds = ["col", "col", "col", "row", "col", "col", "row"]
    big_w = [w_in[0], w_a_out[0], w_b_out[0], w_o[0], w_gate[0], w_up[0], w_down[0]]
    big_m = [m_w_in[0], m_w_a_out[0], m_w_b_out[0], m_w_o[0], m_w_gate[0], m_w_up[0], m_w_down[0]]
    big_v = [v_w_in[0], v_w_a_out[0], v_w_b_out[0], v_w_o[0], v_w_gate[0], v_w_up[0], v_w_down[0]]
    full_in = _cast_into_full("cast_w_in", q_idx, big_w[W_IN], kinds[W_IN])
    in_send, in_recv, (full_in,), token = _gather_start("gather_start_in", [full_in], [kinds[W_IN]], rels=(0, 1))
    rest = [_cast_into_full("cast_" + names[i], q_idx, big_w[i], kinds[i], after=(token,)) for i in range(1, 7)]
    h1 = _rms_fwd("rms_mix", x2d, norm_mix, after=(token,))
    (full_in,) = _gather_wait("gather_wait_in", [full_in], [kinds[W_IN]], [0], in_send, in_recv, h1, rels=(0, 1))
    relay_send, relay_recv, (full_in,) = _split_start("gather_relay_in", [full_in], 2, _relay_copies([kinds[W_IN]], False))
    full_down = rest.pop()
    ag_send, ag_recv, rest, token = _gather_start("gather_start_rest", rest, kinds[1:6], rels=(0, 1), after=(full_in,))
    (full_in,) = _split_wait("gather_relay_wait_in", [full_in], relay_send, relay_recv, _relay_copies([kinds[W_IN]], True), token)
    (g_in,) = _gather_forward("gather_fwd_in", [full_in], [kinds[W_IN]])
    fulls = [g_in] + rest

    def relay_begin(tag, ids, after):
        ks = [kinds[i] for i in ids]
        bufs = _gather_wait("gather_wait_" + tag, [fulls[i] for i in ids], ks, [i - 1 for i in ids], ag_send, ag_recv, after,
                            rels=(0, 1))
        send, recv, bufs = _split_start("gather_relay_" + tag, bufs, 2 * len(ids), _relay_copies(ks, False))
        return tag, ks, send, recv, bufs

    def relay_end(state, after):
        tag, ks, send, recv, bufs = state
        bufs = _split_wait("gather_relay_wait_" + tag, bufs, send, recv, _relay_copies(ks, True), after)
        return _gather_forward("gather_fwd_" + tag, bufs, ks)

    def relay_end_async(state, after):
        tag, ks, send, recv, bufs = state
        bufs = _split_wait("gather_relay_wait_" + tag, bufs, send, recv, _relay_copies(ks, True), after)
        send, recv, bufs = _split_start("gather_fwd_start_" + tag, bufs, 3 * len(ks), _forward_copies(ks, False))
        return tag, ks, send, recv, bufs

    def forwarded(state, after):
        tag, ks, send, recv, bufs = state
        return _split_wait("gather_fwd_wait_" + tag, bufs, send, recv, _forward_copies(ks, True), after)

    ws_b = sgu_w_s[0].astype(BF16)
    wst_b = jnp.swapaxes(sgu_w_s[0], 1, 2).astype(BF16)
    b_col = sgu_b_s[0].reshape(groups, BLK, 1)
    bias_tab, bucket = _band_tables(rel_bias)
    sink = attn_sink[0]

    tm = _tile(s, (1024, 512, 256, 128))

    tn = _tile(n_in, (768, 640, 512))
    z = _mm(
        "mm_z", (s // tm, n_in // tn, 1), [h1, g_in],
        [pl.BlockSpec((tm, d), lambda i, j, k: (i, 0)), pl.BlockSpec((d, tn), lambda i, j, k: (0, j))],
        [jax.ShapeDtypeStruct((s, n_in), BF16)], [pl.BlockSpec((tm, tn), lambda i, j, k: (i, j))],
        [(0, 1, NN, 0)], 1, (tm, tn), 1, lambda ins, vals, outs, cs: _put(outs[0], cs, vals[0]),
        _mm_vmem([((tm, d), BF16, 2), ((d, tn), BF16, 2), ((tm, tn), F32, 3)]),
    )[0]
    mix_relay = relay_begin("mix", [W_A, W_B, W_O], z)

    a_act = _sgu_fwd(z, sgu_v_gain, ws_b, b_col, w_sgu, after=(mix_relay[4][0],))

    kv_b = z[:, off_k:off_g]
    k_pad = jnp.pad(kv_b[:, :w_kv], ((BLK, BLK), (0, 0)))
    v_pad = jnp.pad(kv_b[:, w_kv:], ((BLK, BLK), (0, 0)))
    q_blk0 = off_q // (grp * HEAD_DIM)
    att = _attn_fwd(sink, z, k_pad, v_pad, bias_tab, grp, q_blk0)

    gate_relay = relay_begin("gate", [W_GATE], att)
    up_relay = relay_begin("up", [W_UP], gate_relay[4][0])
    down_send, down_recv, (full_down,), token = _gather_start(
        "gather_start_down", [full_down], [kinds[W_DOWN]], after=(a_act, up_relay[4][0])
    )
    g_a, g_b, g_o = relay_end(mix_relay, token)

    tg = _tile(d, (512,))
    ga0, gb0 = off_g // tg, (off_g + d) // tg

    def ep_gate(ins, vals, outs, cs):
        sa, sb = _sigmoid(ins[4][:, cs].astype(F32)), _sigmoid(ins[5][:, cs].astype(F32))
        _put(outs[0], cs, sa * vals[0] + sb * vals[1])
        _put(outs[1], cs, vals[0])
        _put(outs[2], cs, vals[1])

    t_out = pl.BlockSpec((tm, tg), lambda i, j, k: (i, j))
    m_act, y_a, y_b = _mm(
        "mm_branches", (s // tm, d // tg, 1), [a_act, g_a, att, g_b, z, z],
        [pl.BlockSpec((tm, w_sgu), lambda i, j, k: (i, 0)), pl.BlockSpec((w_sgu, tg), lambda i, j, k: (0, j)),
         pl.BlockSpec((tm, w_att), lambda i, j, k: (i, 0)), pl.BlockSpec((w_att, tg), lambda i, j, k: (0, j)),
         pl.BlockSpec((tm, tg), lambda i, j, k: (i, ga0 + j)), pl.BlockSpec((tm, tg), lambda i, j, k: (i, gb0 + j))],
        [jax.ShapeDtypeStruct((s, d), BF16)] * 3,
        [t_out, t_out, t_out], [(0, 1, NN, 0), (2, 3, NN, 1)], 2, (tm, tg), 1, ep_gate,
        _mm_vmem([((tm, w_sgu), BF16, 4), ((w_sgu, tg), BF16, 4), ((tm, tg), F32, 12)]),    )

    tn = _tile(d, (1024, 512))

    def ep_residual(ins, vals, outs, cs):
        _put(outs[0], cs, ins[2][:, cs] + vals[0])

    gate_fwd = relay_end_async(gate_relay, m_act)
    x2 = _mm(
        "mm_wo", (s // tm, d // tn, 1), [m_act, g_o, x2d],
        [pl.BlockSpec((tm, d), lambda i, j, k: (i, 0)), pl.BlockSpec((d, tn), lambda i, j, k: (0, j)),
         pl.BlockSpec((tm, tn), lambda i, j, k: (i, j))],
        [jax.ShapeDtypeStruct((s, d), F32)], [pl.BlockSpec((tm, tn), lambda i, j, k: (i, j))],
        [(0, 1, NN, 0)], 1, (tm, tn), 1, ep_residual,
        _mm_vmem([((tm, d), BF16, 2), ((d, tn), BF16, 2), ((tm, tn), F32, 5)]),
        after=(gate_fwd[4][0],),
    )[0]
    (g_gate,) = forwarded(gate_fwd, x2)
    up_fwd = relay_end_async(up_relay, g_gate)
    h2 = _rms_fwd("rms_ffn", x2, norm_ffn, after=(up_fwd[4][0],))
    (g_up,) = forwarded(up_fwd, h2)

    tf = _tile(d_ff, (512,))

    def ep_swiglu(ins, vals, outs, cs):
        gt, up = vals
        _put(outs[0], cs, gt)
        _put(outs[1], cs, up)
        _put(outs[2], cs, (gt * _sigmoid(gt)) * up)

    f_out = pl.BlockSpec((tm, tf), lambda i, j, k: (i, j))
    gt, up, f_act = _mm(
        "mm_gate_up", (s // tm, d_ff // tf, 1), [h2, g_gate, g_up],
        [pl.BlockSpec((tm, d), lambda i, j, k: (i, 0)), pl.BlockSpec((d, tf), lambda i, j, k: (0, j)),
         pl.BlockSpec((d, tf), lambda i, j, k: (0, j))],
        [jax.ShapeDtypeStruct((s, d_ff), BF16)] * 3,
        [f_out, f_out, f_out], [(0, 1, NN, 0), (0, 2, NN, 1)], 2, (tm, tf), 1, ep_swiglu,
        _mm_vmem([((tm, d), BF16, 2), ((d, tf), BF16, 4), ((tm, tf), F32, 8)]),    )
    (g_down,) = _gather_forward(
        "gather_fwd_ffn_out",
        _gather_wait("gather_wait_ffn_out", [full_down], [kinds[W_DOWN]], [0], down_send, down_recv, f_act),
        [kinds[W_DOWN]],
    )

    tkf = _tile(d_ff, (1408, 1024, 512))
    tml, tnl = _tile(s, (512, 256, 128)), _tile(d, (512,))
    x3 = _mm(
        "mm_down", (s // tml, d // tnl, 1), [f_act, g_down, x2],
        [pl.BlockSpec((tml, d_ff), lambda i, j, k: (i, 0)), pl.BlockSpec((d_ff, tnl), lambda i, j, k: (0, j)),
         pl.BlockSpec((tml, tnl), lambda i, j, k: (i, j))],
        [jax.ShapeDtypeStruct((s, d), F32)], [pl.BlockSpec((tml, tnl), lambda i, j, k: (i, j))],
        [(0, 1, NN, 0)], 1, (tml, tnl), 1, ep_residual,
        _mm_vmem([((tml, d_ff), BF16, 2), ((d_ff, tnl), BF16, 2), ((tml, tnl), F32, 6)]),    )[0]

    dx3, dx3b, dg_final, loss_part = _head(x3, norm_final.reshape(1, d), tgt)

    def reduce_a(tag, ids, grads):
        ks = [kinds[i] for i in ids]
        lands = [lax.empty((g.shape[0] // 2, g.shape[1]) if k == "col" else (g.shape[0], g.shape[1] // 2), BF16)
                 for g, k in zip(grads, ks)]
        send, recv, bufs = _split_start("pair_send_" + tag, list(grads) + lands, len(ids), _pair_exchange_copies(ks))
        return {"tag": tag, "ids": ids, "ks": ks, "pair": (send, recv, bufs), "token": bufs[0]}

    def reduce_b(st, after):
        tag, ids, ks = st["tag"], st["ids"], st["ks"]
        send, recv, bufs = st["pair"]
        bufs = _split_wait("pair_wait_" + tag, bufs, send, recv, _pair_exchange_copies(ks), after)
        grads, from_sib = bufs[: len(ids)], bufs[len(ids) :]
        halves = [_pair_add("pair_add_" + names[i], c_idx, g, r, k) for i, g, r, k in zip(ids, grads, from_sib, ks)]
        st["chip"] = _chip_send_start("chip_send_" + tag, halves, ks)
        st["token"] = st["chip"][2][0]

    def reduce_c(st, after):
        tag, ids, ks = st["tag"], st["ids"], st["ks"]
        send, recv, halves, lands = st["chip"]
        halves, lands = _chip_send_wait("chip_wait_" + tag, halves, lands, ks, send, recv, after)
        pieces = [_chip_sum("chip_sum_" + names[i], qc_idx, h, r, k) for i, h, r, k in zip(ids, halves, lands, ks)]
        st["share"] = _split_start("share_send_" + tag, pieces, len(ids), _pair_share_copies(ks, False))
        st["token"] = st["share"][2][0]

    def reduce_d(st, after):
        send, recv, bufs = st["share"]
        return _split_wait("share_wait_" + st["tag"], bufs, send, recv, _pair_share_copies(st["ks"], True), after)

    grads_big, upd = [None] * 7, [None] * 7

    def finish(st, after):
        shared = reduce_d(st, after)
        after = shared[0]
        for i, g in zip(st["ids"], shared):
            upd[i] = _adamw("adamw_" + names[i], big_w[i], g, big_m[i], big_v[i], after=(after,))
            grads_big[i] = upd[i][3]
            after = upd[i][0]
        return after

    def ep_swiglu_bwd(ins, vals, outs, cs):
        df = vals[0]
        gtv, upv = ins[2][:, cs].astype(F32), ins[3][:, cs].astype(F32)
        sg = _sigmoid(gtv)
        _put(outs[0], cs, df * upv * (sg + gtv * sg * (1.0 - sg)))
        _put(outs[1], cs, df * (gtv * sg))

    dgt, dup = _mm(
        "mm_dswiglu", (s // tm, d_ff // tf, 1), [dx3b, g_down, gt, up],
        [pl.BlockSpec((tm, d), lambda i, j, k: (i, 0)), pl.BlockSpec((tf, d), lambda i, j, k: (j, 0)), f_out, f_out],
        [jax.ShapeDtypeStruct((s, d_ff), BF16), jax.ShapeDtypeStruct((s, d_ff), BF16)], [f_out, f_out],
        [(0, 1, NT, 0)], 1, (tm, tf), 1, ep_swiglu_bwd,
        _mm_vmem([((tm, d), BF16, 2), ((tf, d), BF16, 2), ((tm, tf), F32, 8)]),    )

    def ep_store(ins, vals, outs, cs):
        for o, v in zip(outs, vals):
            _put(o, cs, v)

    twn = _tile(d, (1024, 512))
    gw_down = _mm(
        "mm_gw_down", (d_ff // tkf, d // twn, 1), [f_act, dx3b],
        [pl.BlockSpec((s, tkf), lambda i, j, k: (0, i)), pl.BlockSpec((s, twn), lambda i, j, k: (0, j))],
        [jax.ShapeDtypeStruct((d_ff, d), BF16)], [pl.BlockSpec((tkf, twn), lambda i, j, k: (i, j))],
        [(0, 1, TN, 0)], 1, (tkf, twn), 1, ep_store,
        _mm_vmem([((s, tkf), BF16, 3), ((s, twn), BF16, 2), ((tkf, twn), F32, 3)]),
    )[0]
    red_down = reduce_a("down", [W_DOWN], [gw_down])

    tn2 = _tile(d, (256,))
    dh2_specs = [pl.BlockSpec((tm, d_ff), lambda i, j, k: (i, 0)), pl.BlockSpec((tn2, d_ff), lambda i, j, k: (j, 0))]
    dh2_tile = pl.BlockSpec((tm, tn2), lambda i, j, k: (i, j))
    dh2_vmem = _mm_vmem([((tm, d_ff), BF16, 2), ((tn2, d_ff), BF16, 2), ((tm, tn2), F32, 7)])
    dh2 = _mm(
        "mm_dh2_gate", (s // tm, d // tn2, 1), [dgt, g_gate], dh2_specs,
        [jax.ShapeDtypeStruct((s, d), F32)], [dh2_tile], [(0, 1, NT, 0)], 1, (tm, tn2), 1, ep_store, dh2_vmem,
        after=(red_down["token"],),
    )[0]
    dh2 = _mm(
        "mm_dh2_up", (s // tm, d // tn2, 1), [dup, g_up, dh2], dh2_specs + [dh2_tile],
        [jax.ShapeDtypeStruct((s, d), F32)], [dh2_tile], [(0, 1, NT, 0)], 1, (tm, tn2), 1, ep_residual, dh2_vmem,
    )[0]
    reduce_b(red_down, dh2)

    twr = _tile(d, (1024, 512))
    w_tile = pl.BlockSpec((twr, tf), lambda i, j, k: (i, j))
    gw_gate, gw_up = _mm(
        "mm_gw_gate_up", (d // twr, d_ff // tf, 1), [h2, dgt, dup],
        [pl.BlockSpec((s, twr), lambda i, j, k: (0, i)), pl.BlockSpec((s, tf), lambda i, j, k: (0, j)),
         pl.BlockSpec((s, tf), lambda i, j, k: (0, j))],
        [jax.ShapeDtypeStruct((d, d_ff), BF16), jax.ShapeDtypeStruct((d, d_ff), BF16)], [w_tile, w_tile],
        [(0, 1, TN, 0), (0, 2, TN, 1)], 2, (twr, tf), 1, ep_store,
        _mm_vmem([((s, twr), BF16, 3), ((s, tf), BF16, 4), ((twr, tf), F32, 6)]),
        after=(red_down["token"],),
    )
    red_ffn = reduce_a("ffn_in", [W_GATE, W_UP], [gw_gate, gw_up])

    dx2, dx2b, dg_ffn = _rms_bwd("rms_ffn_bwd", x2, norm_ffn, dh2, dx3, after=(red_ffn["token"],))

    nj = d // tg

    def lo(j):
        return jnp.minimum(j, nj - 1)

    def gate_bwd_body(dx_ref, wo_ref, ga_ref, gb_ref, ya_ref, yb_ref, dya_ref, dyb_ref, dz_ref, keep):
        j = pl.program_id(1)

        @pl.when(j < nj)
        def _():
            dm = lax.dot_general(dx_ref[...], wo_ref[...], NT, preferred_element_type=F32)
            sa, sb = _sigmoid(ga_ref[...].astype(F32)), _sigmoid(gb_ref[...].astype(F32))
            dya_ref[...] = (dm * sa).astype(BF16)
            dyb_ref[...] = (dm * sb).astype(BF16)
            dz_ref[...] = (dm * ya_ref[...].astype(F32) * (sa * (1.0 - sa))).astype(BF16)
            keep[lo(j)] = (dm * yb_ref[...].astype(F32) * (sb * (1.0 - sb))).astype(BF16)

        @pl.when(j >= nj)
        def _():
            dz_ref[...] = keep[jnp.maximum(j - nj, 0)]

    t_lo = pl.BlockSpec((tm, tg), lambda i, j: (i, lo(j)))
    dya, dyb, dz = _pallas(
        gate_bwd_body,
        name="mm_dgate",
        grid=(s // tm, 2 * nj),
        in_specs=[
            pl.BlockSpec((tm, d), lambda i, j: (i, 0)),
            pl.BlockSpec((tg, d), lambda i, j: (lo(j), 0)),
            pl.BlockSpec((tm, tg), lambda i, j: (i, ga0 + lo(j))),
            pl.BlockSpec((tm, tg), lambda i, j: (i, gb0 + lo(j))),
            t_lo,
            t_lo,
        ],
        out_specs=[t_lo, t_lo, pl.BlockSpec((tm, tg), lambda i, j: (i, ga0 + j))],
        out_shape=[jax.ShapeDtypeStruct((s, d), BF16), jax.ShapeDtypeStruct((s, d), BF16), jax.ShapeDtypeStruct((s, n_in), BF16)],
        scratch_shapes=[pltpu.VMEM((nj, tm, tg), BF16)],
        compiler_params=_params(_mm_vmem([((tm, d), BF16, 2), ((tg, d), BF16, 2), ((tm, tg), F32, 14), ((nj, tm, tg), BF16, 1)])),
    )(dx2b, g_o, z, z, y_a, y_b)
    reduce_b(red_ffn, dya)
    reduce_c(red_down, red_ffn["token"])

    gw_o = _mm(
        "mm_gw_o", (d // twr, d // twn, 1), [m_act, dx2b],
        [pl.BlockSpec((s, twr), lambda i, j, k: (0, i)), pl.BlockSpec((s, twn), lambda i, j, k: (0, j))],
        [jax.ShapeDtypeStruct((d, d), BF16)], [pl.BlockSpec((twr, twn), lambda i, j, k: (i, j))],
        [(0, 1, TN, 0)], 1, (twr, twn), 1, ep_store,
        _mm_vmem([((s, twr), BF16, 3), ((s, twn), BF16, 2), ((twr, twn), F32, 3)]),
        after=(red_down["token"],),
    )[0]
    after_down = finish(red_down, gw_o)

    tb = _tile(w_sgu, (1024, 512))
    b_out = pl.BlockSpec((tm, tb), lambda i, j, k: (i, j))

    da, datt = _mm(
        "mm_dbranches", (s // tm, w_sgu // tb, 1), [dya, g_a, dyb, g_b],
        [pl.BlockSpec((tm, d), lambda i, j, k: (i, 0)), pl.BlockSpec((tb, d), lambda i, j, k: (j, 0)),
         pl.BlockSpec((tm, d), lambda i, j, k: (i, 0)), pl.BlockSpec((tb, d), lambda i, j, k: (j, 0))],
        [jax.ShapeDtypeStruct((s, w_sgu), BF16), jax.ShapeDtypeStruct((s, w_att), BF16)], [b_out, b_out],
        [(0, 1, NT, 0), (2, 3, NT, 1)], 2, (tm, tb), 1, ep_store,
        _mm_vmem([((tm, d), BF16, 4), ((tb, d), BF16, 4), ((tm, tb), F32, 6)]),
        after=(after_down,),
    )

    wb_tile = pl.BlockSpec((tb, twn), lambda i, j, k: (i, j))
    gw_a, gw_b = _mm(
        "mm_gw_branches", (w_sgu // tb, d // twn, 1), [a_act, dya, att, dyb],
        [pl.BlockSpec((s, tb), lambda i, j, k: (0, i)), pl.BlockSpec((s, twn), lambda i, j, k: (0, j)),
         pl.BlockSpec((s, tb), lambda i, j, k: (0, i)), pl.BlockSpec((s, twn), lambda i, j, k: (0, j))],
        [jax.ShapeDtypeStruct((w_sgu, d), BF16), jax.ShapeDtypeStruct((w_att, d), BF16)], [wb_tile, wb_tile],
        [(0, 1, TN, 0), (2, 3, TN, 1)], 2, (tb, twn), 1, ep_store,
        _mm_vmem([((s, tb), BF16, 5), ((s, twn), BF16, 4), ((tb, twn), F32, 6)]),
        after=(da,),
    )
    red_mix = reduce_a("mix", [W_O, W_A, W_B], [gw_o, gw_a, gw_b])

    dz, dws, dbs, dgain = _sgu_bwd(z, da, sgu_v_gain, ws_b, wst_b, b_col, w_sgu, dz, after=(red_mix["token"],))
    dz, dk_pad, dv_pad, dbias_tab, dsink = _attn_bwd(sink, z, k_pad, v_pad, bias_tab, datt, dz, grp, q_blk0)
    dz = _dkv_to_dz(dk_pad, dv_pad, dz, off_k // (2 * w_kv))
    drel = _relbias_bwd(dbias_tab, bucket)
    reduce_b(red_mix, dz)
    reduce_c(red_ffn, red_mix["token"])

    small_w = [norm_mix, sgu_v_gain, sgu_w_s, sgu_b_s, attn_sink, rel_bias, norm_ffn, norm_final]
    small_m = [m_norm_mix, m_sgu_v_gain, m_sgu_w_s, m_sgu_b_s, m_attn_sink, m_rel_bias, m_norm_ffn, m_norm_final]
    small_v = [v_norm_mix, v_sgu_v_gain, v_sgu_w_s, v_sgu_b_s, v_attn_sink, v_rel_bias, v_norm_ffn, v_norm_final]
    small_shapes = [w.shape for w in small_w]
    early = [dgain, dws, dbs, dsink[:, 0], drel[:, :REL_BUCKETS].T, dg_ffn, dg_final]
    p_early = _pack([g.reshape(shp) for g, shp in zip(early, small_shapes[1:])] + [loss_part[0, :1]])
    land = jnp.zeros((2 * N_CHIPS,) + p_early.shape, F32)
    sm_send, sm_recv, (p_early, land) = _split_start("small_send", [p_early, land], 2 * N_CHIPS - 1, _small_exchange_copies(False))

    tzn = _tile(n_in, (768, 640, 512))
    gw_in = _mm(
        "mm_gw_in", (d // twr, n_in // tzn, 1), [h1, dz],
        [pl.BlockSpec((s, twr), lambda i, j, k: (0, i)), pl.BlockSpec((s, tzn), lambda i, j, k: (0, j))],
        [jax.ShapeDtypeStruct((d, n_in), BF16)], [pl.BlockSpec((twr, tzn), lambda i, j, k: (i, j))],
        [(0, 1, TN, 0)], 1, (twr, tzn), 1, ep_store,
        _mm_vmem([((s, twr), BF16, 3), ((s, tzn), BF16, 2), ((twr, tzn), F32, 3)]),
        after=(red_ffn["token"], p_early),
    )[0]
    red_in = reduce_a("w_in", [W_IN], [gw_in])

    reduce_c(red_mix, red_in["token"])
    reduce_b(red_in, finish(red_mix, red_in["token"]))

    dh1 = _mm(
        "mm_dh1", (s // tm, d // tn2, 1), [dz, g_in],
        [pl.BlockSpec((tm, n_in), lambda i, j, k: (i, 0)), pl.BlockSpec((tn2, n_in), lambda i, j, k: (j, 0))],
        [jax.ShapeDtypeStruct((s, d), F32)], [pl.BlockSpec((tm, tn2), lambda i, j, k: (i, j))],
        [(0, 1, NT, 0)], 1, (tm, tn2), 1, ep_store,
        _mm_vmem([((tm, n_in), BF16, 2), ((tn2, n_in), BF16, 2), ((tm, tn2), F32, 5)]),
        after=(red_in["token"],),
    )[0]

    grad_x, _, dg_mix = _rms_bwd("rms_mix_bwd", x2d, norm_mix, dh1, dx2)

    p_mix = _pack([dg_mix.reshape(small_shapes[0])])
    land_mix = jnp.zeros((2 * N_CHIPS,) + p_mix.shape, F32)
    mx_send, mx_recv, (p_mix, land_mix) = _split_start("mix_send", [p_mix, land_mix], 2 * N_CHIPS - 1, _small_exchange_copies(False))

    reduce_c(red_in, finish(red_ffn, p_mix))
    p_early, land = _split_wait("small_wait", [p_early, land], sm_send, sm_recv, _small_exchange_copies(True), red_in["token"])
    p_mix, land_mix = _split_wait("mix_wait", [p_mix, land_mix], mx_send, mx_recv, _small_exchange_copies(True), p_early)
    me_idx = 2 * q_idx + c_idx
    packed_g = jnp.concatenate([_small_sum("mix_sum", me_idx, p_mix, land_mix), _small_sum("small_sum", me_idx, p_early, land)], axis=0)
    g_small = _unpack(packed_g, small_shapes + [(1,)])
    loss = g_small[-1].reshape(())
    g_small = g_small[:-1]
    zero1 = jnp.zeros((1,), F32)
    pw, pg, pm, pv = _pack(small_w + [zero1]), _pack(g_small + [zero1]), _pack(small_m + [zero1]), _pack(small_v + [zero1])
    small_upd = _adamw("adamw_small", pw, pg, pm, pv)
    d_small, nm_small, nv_small = [_unpack(a, small_shapes) for a in small_upd[:3]]
    finish(red_in, small_upd[0])

    small_names = ["norm_mix", "sgu_v_gain", "sgu_w_s", "sgu_b_s", "attn_sink", "rel_bias", "norm_ffn", "norm_final"]
    table = {}
    for i, n in enumerate(names):
        table[n] = (grads_big[i][None], upd[i][0][None], upd[i][1][None], upd[i][2][None])
    for i, n in enumerate(small_names):
        table[n] = (g_small[i], d_small[i], nm_small[i], nv_small[i])
    order = ["w_in", "norm_mix", "sgu_v_gain", "sgu_w_s", "sgu_b_s", "w_a", "attn_sink", "rel_bias", "w_b", "w_o", "norm_ffn",
             "w_gate", "w_up", "w_down", "norm_final"]
    outs = [loss, grad_x.reshape(1, s, d)]
    for part in range(4):
        outs += [table[n][part] for n in order]
    return tuple(outs)
```

```python
import math

import jax
import jax.numpy as jnp
import numpy as np
from jax import lax
from jax.experimental import pallas as pl
from jax.experimental.pallas import tpu as pltpu

F32 = jnp.float32
BF16 = jnp.bfloat16
I32 = jnp.int32
MESH = pl.DeviceIdType.MESH

EPS = 1e-6
NEG = -1e30
BLK = 128
HEAD_DIM = 128
N_KV_HEADS = 2
REL_BUCKETS = 32
REL_MAX_DIST = 128
N_CHIPS = 4
ADAM_LR, ADAM_B1, ADAM_B2, ADAM_EPS, ADAM_WD, ADAM_STEP = 0.001, 0.9, 0.999, 1e-08, 0.01, 10

LANES = 128
VMEM_CAP = 60 * 1024 * 1024

NN = (((1,), (0,)), ((), ()))
NT = (((1,), (1,)), ((), ()))
TN = (((0,), (0,)), ((), ()))
ANY = pl.BlockSpec(memory_space=pl.ANY)
HBM_SPEC = pl.BlockSpec(memory_space=pltpu.HBM)
SEM_SPEC = pl.BlockSpec(memory_space=pltpu.SEMAPHORE)
EFFECT = pltpu.SideEffectType.DATAFLOW_SIDE_EFFECTING


def _tile(n, cands):
    for t in cands:
        if n % t == 0:
            return t
    return n


PIN_BYTES = 64 * 1024


def _pin_hbm(a):
    big = hasattr(a, "dtype") and jnp.issubdtype(a.dtype, jnp.floating) and _nbytes(a.shape, a.dtype) >= PIN_BYTES
    return pltpu.with_memory_space_constraint(a, pltpu.HBM) if big else a


def _pallas(body, *, out_shape, **kw):
    def pin(o):
        big = isinstance(o, jax.ShapeDtypeStruct) and jnp.issubdtype(o.dtype, jnp.floating) and _nbytes(o.shape, o.dtype) >= PIN_BYTES
        return pltpu.HBM(o.shape, o.dtype) if big else o

    shapes = type(out_shape)(pin(o) for o in out_shape) if isinstance(out_shape, (list, tuple)) else pin(out_shape)
    call = pl.pallas_call(body, out_shape=shapes, **kw)
    return lambda *args: call(*[_pin_hbm(a) for a in args])


def _params(vmem_bytes=None, **kw):
    if vmem_bytes is not None:
        kw["vmem_limit_bytes"] = int(min(max(vmem_bytes, 32 * 1024 * 1024), VMEM_CAP))
    return pltpu.CompilerParams(**kw)


def _nbytes(shape, dtype):
    return int(np.prod(shape)) * jnp.dtype(dtype).itemsize


def _sigmoid(x):
    return 1.0 / (1.0 + jnp.exp(-x))


_GC = 0.7978845608028654
_GA = 0.044715


def _gelu(x):
    return 0.5 * x * (1.0 + jnp.tanh(_GC * (x + _GA * (x * x * x))))


def _gelu_grad(x):
    t = jnp.tanh(_GC * (x + _GA * (x * x * x)))
    return 0.5 * (1.0 + t) + 0.5 * x * (1.0 - t * t) * (_GC * (1.0 + 3.0 * _GA * (x * x)))


def _bf(v):
    return v if v.dtype == BF16 else v.astype(BF16)


def _mm(name, grid, ins, in_specs, out_shape, out_specs, pairs, n_acc, tile, nk, epilogue, vmem_bytes, after=()):
    assert nk == 1
    n_in, n_out = len(ins) + len(after), len(out_shape)

    def body(*refs):
        in_refs, out_refs = refs[:n_in], refs[n_in : n_in + n_out]
        vals = [None] * n_acc
        for a_i, b_i, dn, acc_i in pairs:
            d = lax.dot_general(_bf(in_refs[a_i][...]), _bf(in_refs[b_i][...]), dn, preferred_element_type=F32)
            vals[acc_i] = d if vals[acc_i] is None else vals[acc_i] + d
        epilogue(in_refs, vals, out_refs, slice(None))

    return _pallas(
        body,
        name=name,
        grid=grid,
        in_specs=list(in_specs) + [ANY] * len(after),
        out_specs=out_specs,
        out_shape=out_shape,
        compiler_params=_params(vmem_bytes),
    )(*ins, *after)


def _put(ref, cs, v):
    ref[:, cs] = v.astype(ref.dtype)


def _mm_quarters(name, quarters, nq, h1, g_in, prev):
    s, d = h1.shape
    n_in = g_in.shape[1]
    wq = n_in // N_CHIPS
    tm, tn = _tile(s, (1024, 512, 256, 128)), _tile(wq, (640, 512, 384, 256, 128))
    per = wq // tn
    n_prev = 0 if prev is None else 1

    def body(q_ref, h_ref, w_ref, *rest):
        del q_ref
        rest[-1][...] = jnp.dot(h_ref[...], w_ref[...], preferred_element_type=F32).astype(BF16)

    def col(a, b, q):
        return q[a] * per + b

    return _pallas(
        body,
        name=name,
        grid_spec=pltpu.PrefetchScalarGridSpec(
            num_scalar_prefetch=1,
            grid=(s // tm, nq, per),
            in_specs=[pl.BlockSpec((tm, d), lambda i, a, b, q: (i, 0)), pl.BlockSpec((d, tn), lambda i, a, b, q: (0, col(a, b, q)))]
            + [ANY] * n_prev,
            out_specs=pl.BlockSpec((tm, tn), lambda i, a, b, q: (i, col(a, b, q))),
        ),
        out_shape=jax.ShapeDtypeStruct((s, n_in), BF16),
        input_output_aliases={3: 0} if n_prev else {},
        compiler_params=_params(_mm_vmem([((tm, d), BF16, 2), ((d, tn), BF16, 2), ((tm, tn), F32, 3)])),
    )(quarters, h1, g_in, *((prev,) if n_prev else ()))


def _mm_vmem(tiles):
    return sum(_nbytes(s, d) * c for s, d, c in tiles) + 4 * 1024 * 1024


def _rows8(v):
    r, d = v.shape
    return v.reshape(r // 8, 8, d).sum(axis=0)


def _rms_fwd(name, x, g, after=()):
    s, d = x.shape
    tm = _tile(s, (256, 128))

    def body(x_ref, g_ref, *rest):
        h_ref = rest[-1]
        xv = x_ref[...]
        r = lax.rsqrt(jnp.mean(xv * xv, axis=-1, keepdims=True) + EPS)
        h_ref[...] = ((xv * r) * g_ref[...]).astype(BF16)

    return _pallas(
        body,
        name=name,
        grid=(s // tm,),
        in_specs=[pl.BlockSpec((tm, d), lambda i: (i, 0)), pl.BlockSpec((1, d), lambda i: (0, 0))] + [ANY] * len(after),
        out_specs=pl.BlockSpec((tm, d), lambda i: (i, 0)),
        out_shape=jax.ShapeDtypeStruct((s, d), BF16),
    )(x, g, *after)


def _rms_bwd(name, x, g, dh, dres, after=()):
    s, d = x.shape
    tm = _tile(s, (256, 128))
    n = s // tm
    n_after = len(after)

    def body(x_ref, g_ref, dh_ref, dres_ref, *rest):
        dx_ref, dxb_ref, dg_ref, acc_ref = rest[n_after:]
        i = pl.program_id(0)
        xv = x_ref[...]
        r = lax.rsqrt(jnp.mean(xv * xv, axis=-1, keepdims=True) + EPS)
        xh = xv * r
        dhv = dh_ref[...]
        dxh = dhv * g_ref[...]
        dx = r * (dxh - xh * jnp.mean(dxh * xh, axis=-1, keepdims=True)) + dres_ref[...]
        dx_ref[...] = dx
        dxb_ref[...] = dx.astype(BF16)
        part = _rows8(dhv * xh)

        @pl.when(i == 0)
        def _():
            acc_ref[...] = part

        @pl.when(i > 0)
        def _():
            acc_ref[...] += part

        @pl.when(i == n - 1)
        def _():
            dg_ref[...] = jnp.sum(acc_ref[...], axis=0, keepdims=True)

    row = pl.BlockSpec((tm, d), lambda i: (i, 0))
    vec = pl.BlockSpec((1, d), lambda i: (0, 0))
    return _pallas(
        body,
        name=name,
        grid=(n,),
        in_specs=[row, vec, row, row] + [ANY] * n_after,
        out_specs=[row, row, vec],
        out_shape=[jax.ShapeDtypeStruct((s, d), F32), jax.ShapeDtypeStruct((s, d), BF16), jax.ShapeDtypeStruct((1, d), F32)],
        scratch_shapes=[pltpu.VMEM((8, d), F32)],
    )(x, g, dh, dres, *after)


def _head(x3, g, target):
    s, d = x3.shape
    tm = _tile(s, (256, 128))
    n = s // tm

    def body(x_ref, g_ref, t_ref, dx_ref, dxb_ref, dg_ref, loss_ref, acc_g, acc_l):
        i = pl.program_id(0)
        xv = x_ref[...]
        gv = g_ref[...]
        r = lax.rsqrt(jnp.mean(xv * xv, axis=-1, keepdims=True) + EPS)
        xh = xv * r
        e = xh * gv - t_ref[...]
        dy = e * (1.0 / d)
        dxh = dy * gv
        dx = r * (dxh - xh * jnp.mean(dxh * xh, axis=-1, keepdims=True))
        dx_ref[...] = dx
        dxb_ref[...] = dx.astype(BF16)
        pg = _rows8(dy * xh)
        plo = _rows8(e * e)

        @pl.when(i == 0)
        def _():
            acc_g[...] = pg
            acc_l[...] = plo

        @pl.when(i > 0)
        def _():
            acc_g[...] += pg
            acc_l[...] += plo

        @pl.when(i == n - 1)
        def _():
            dg_ref[...] = jnp.sum(acc_g[...], axis=0, keepdims=True)
            loss_ref[...] = jnp.full((1, LANES), (0.5 / d) * jnp.sum(acc_l[...]), F32)

    row = pl.BlockSpec((tm, d), lambda i: (i, 0))
    vec = pl.BlockSpec((1, d), lambda i: (0, 0))
    return _pallas(
        body,
        name="head",
        grid=(n,),
        in_specs=[row, vec, row],
        out_specs=[row, row, vec, pl.BlockSpec((1, LANES), lambda i: (0, 0))],
        out_shape=[
            jax.ShapeDtypeStruct((s, d), F32),
            jax.ShapeDtypeStruct((s, d), BF16),
            jax.ShapeDtypeStruct((1, d), F32),
            jax.ShapeDtypeStruct((1, LANES), F32),
        ],
        scratch_shapes=[pltpu.VMEM((8, d), F32), pltpu.VMEM((8, d), F32)],
    )(x3, g, target)


def _sgu_fwd(z, gain, ws_b, b_col, w_sgu, after=()):
    s = z.shape[0]
    groups = ws_b.shape[0]

    def body(zu_ref, zv_ref, gain_ref, ws_ref, b_ref, *rest):
        a_ref = rest[-1]
        vv = _gelu(zv_ref[...].astype(F32))
        r = lax.rsqrt(jnp.mean(vv * vv, axis=-1, keepdims=True) + EPS)
        vn = ((vv * r) * gain_ref[...]).astype(BF16)
        u = _gelu(zu_ref[...].astype(F32))
        for g in range(groups):
            sl = slice(g * BLK, (g + 1) * BLK)
            mixed = jnp.dot(ws_ref[g], vn[:, sl], preferred_element_type=F32) + b_ref[g]
            a_ref[:, sl] = (u[:, sl] * mixed).astype(BF16)

    return _pallas(
        body,
        name="sgu_fwd",
        grid=(s // BLK,),
        in_specs=[
            pl.BlockSpec((BLK, w_sgu), lambda c: (c, 0)),
            pl.BlockSpec((BLK, w_sgu), lambda c: (c, 1)),
            pl.BlockSpec((1, w_sgu), lambda c: (0, 0)),
            pl.BlockSpec((groups, BLK, BLK), lambda c: (0, 0, 0)),
            pl.BlockSpec((groups, BLK, 1), lambda c: (0, 0, 0)),
        ]
        + [ANY] * len(after),
        out_specs=pl.BlockSpec((BLK, w_sgu), lambda c: (c, 0)),
        out_shape=jax.ShapeDtypeStruct((s, w_sgu), BF16),
    )(z, z, gain, ws_b, b_col, *after)


def _sgu_bwd(z, da, gain, ws_b, wst_b, b_col, w_sgu, dz, after=()):
    s = z.shape[0]
    groups = ws_b.shape[0]
    n = s // BLK
    n_skip = 1 + len(after)

    def body(zu_ref, zv_ref, da_ref, gain_ref, ws_ref, wst_ref, b_ref, *rest):
        dz_ref, dws_ref, dbs_ref, dgain_ref, acc_gain = rest[n_skip:]
        c = pl.program_id(0)
        zu = zu_ref[...].astype(F32)
        zv = zv_ref[...].astype(F32)
        gain_v = gain_ref[...]
        vv = _gelu(zv)
        r = lax.rsqrt(jnp.mean(vv * vv, axis=-1, keepdims=True) + EPS)
        xh = vv * r
        vn = (xh * gain_v).astype(BF16)
        u = _gelu(zu)
        dav = da_ref[...].astype(F32)
        dmix = dav * u
        dmix_b = dmix.astype(BF16)
        dvn_parts = []
        for g in range(groups):
            sl = slice(g * BLK, (g + 1) * BLK)
            mixed = jnp.dot(ws_ref[g], vn[:, sl], preferred_element_type=F32) + b_ref[g]
            dz_ref[:, sl] = (dav[:, sl] * mixed * _gelu_grad(zu[:, sl])).astype(BF16)
            dvn_parts.append(jnp.dot(wst_ref[g], dmix_b[:, sl], preferred_element_type=F32))
            dws_g = lax.dot_general(dmix_b[:, sl], vn[:, sl], NT, preferred_element_type=F32)
            dbs_g = jnp.sum(dmix[:, sl], axis=1, keepdims=True)

            @pl.when(c == 0)
            def _():
                dws_ref[g] = dws_g
                dbs_ref[g] = dbs_g

            @pl.when(c > 0)
            def _():
                dws_ref[g] += dws_g
                dbs_ref[g] += dbs_g

        dvn = jnp.concatenate(dvn_parts, axis=1)
        dxh = dvn * gain_v
        dvv = r * (dxh - xh * jnp.mean(dxh * xh, axis=-1, keepdims=True))
        dz_ref[:, w_sgu:] = (dvv * _gelu_grad(zv)).astype(BF16)
        pg = _rows8(dvn * xh)

        @pl.when(c == 0)
        def _():
            acc_gain[...] = pg

        @pl.when(c > 0)
        def _():
            acc_gain[...] += pg

        @pl.when(c == n - 1)
        def _():
            dgain_ref[...] = jnp.sum(acc_gain[...], axis=0, keepdims=True)

    full3 = pl.BlockSpec((groups, BLK, BLK), lambda c: (0, 0, 0))
    col3 = pl.BlockSpec((groups, BLK, 1), lambda c: (0, 0, 0))
    vec = pl.BlockSpec((1, w_sgu), lambda c: (0, 0))
    return _pallas(
        body,
        name="sgu_bwd",
        grid=(n,),
        in_specs=[
            pl.BlockSpec((BLK, w_sgu), lambda c: (c, 0)),
            pl.BlockSpec((BLK, w_sgu), lambda c: (c, 1)),
            pl.BlockSpec((BLK, w_sgu), lambda c: (c, 0)),
            vec,
            full3,
            full3,
            col3,
            ANY,
        ]
        + [ANY] * len(after),
        out_specs=[pl.BlockSpec((BLK, 2 * w_sgu), lambda c: (c, 0)), full3, col3, vec],
        out_shape=[
            jax.ShapeDtypeStruct(dz.shape, BF16),
            jax.ShapeDtypeStruct((groups, BLK, BLK), F32),
            jax.ShapeDtypeStruct((groups, BLK, 1), F32),
            jax.ShapeDtypeStruct((1, w_sgu), F32),
        ],
        scratch_shapes=[pltpu.VMEM((8, w_sgu), F32)],
        input_output_aliases={7: 0},
    )(z, z, da, gain, ws_b, wst_b, b_col, dz, *after)


def _attn_softmax(sink_ref, q_ref, k_ref, v_ref, bias_ref, s_len, grp):
    kv = pl.program_id(0)
    n = pl.program_id(1)
    start = pl.multiple_of(n * BLK, BLK)
    kb = k_ref[pl.ds(start, 3 * BLK), :]
    vb = v_ref[pl.ds(start, 3 * BLK), :]
    qv = q_ref[...]
    qs = jnp.concatenate([qv[:, g * HEAD_DIM : (g + 1) * HEAD_DIM] for g in range(grp)], axis=0).astype(BF16)
    sc = lax.dot_general(qs, kb, NT, preferred_element_type=F32) * (HEAD_DIM**-0.5)
    sc = sc + bias_ref[...].reshape(grp * BLK, 3 * BLK)
    kpos = start + lax.broadcasted_iota(I32, (1, 3 * BLK), 1) - BLK
    sc = jnp.where((kpos >= 0) & (kpos < s_len), sc, NEG)
    sink = jnp.concatenate([jnp.full((BLK, 1), sink_ref[kv * grp + g], F32) for g in range(grp)], axis=0)
    m = jnp.maximum(jnp.max(sc, axis=-1, keepdims=True), sink)
    p = jnp.exp(sc - m)
    esink = jnp.exp(sink - m)
    den = jnp.sum(p, axis=-1, keepdims=True) + esink
    return start, qs, kb, vb, p / den, esink / den


def _attn_specs(s, grp, q_blk0):
    qw = grp * HEAD_DIM
    return [
        pl.BlockSpec(memory_space=pltpu.SMEM),
        pl.BlockSpec((BLK, qw), lambda kv, n: (n, q_blk0 + kv)),
        pl.BlockSpec((s + 2 * BLK, HEAD_DIM), lambda kv, n: (0, kv)),
        pl.BlockSpec((s + 2 * BLK, HEAD_DIM), lambda kv, n: (0, kv)),
        pl.BlockSpec((grp, BLK, 3 * BLK), lambda kv, n: (kv, 0, 0)),
    ]


def _attn_fwd(sink, z, k_pad, v_pad, bias_tab, grp, q_blk0):
    s = z.shape[0]
    qw = grp * HEAD_DIM

    def body(sink_ref, q_ref, k_ref, v_ref, bias_ref, o_ref):
        _, _, _, vb, pn, _ = _attn_softmax(sink_ref, q_ref, k_ref, v_ref, bias_ref, s, grp)
        o = jnp.dot(pn.astype(BF16), vb, preferred_element_type=F32)
        for g in range(grp):
            o_ref[:, g * HEAD_DIM : (g + 1) * HEAD_DIM] = o[g * BLK : (g + 1) * BLK].astype(BF16)

    return _pallas(
        body,
        name="attn_fwd",
        grid=(N_KV_HEADS, s // BLK),
        in_specs=_attn_specs(s, grp, q_blk0),
        out_specs=pl.BlockSpec((BLK, qw), lambda kv, n: (n, kv)),
        out_shape=jax.ShapeDtypeStruct((s, N_KV_HEADS * qw), BF16),
    )(sink, z, k_pad, v_pad, bias_tab)


def _attn_bwd(sink, z, k_pad, v_pad, bias_tab, dout, dz, grp, q_blk0):
    s = z.shape[0]
    qw = grp * HEAD_DIM
    nb = s // BLK
    heads = N_KV_HEADS * grp

    def body(sink_ref, q_ref, k_ref, v_ref, bias_ref, do_ref, dz_in, dq_ref, dk_ref, dv_ref, dbias_ref, dsink_ref, dk_acc, dv_acc):
        del dz_in
        kv = pl.program_id(0)
        n = pl.program_id(1)
        start, qs, kb, vb, pn, psink = _attn_softmax(sink_ref, q_ref, k_ref, v_ref, bias_ref, s, grp)
        dov = do_ref[...]
        dos = jnp.concatenate([dov[:, g * HEAD_DIM : (g + 1) * HEAD_DIM] for g in range(grp)], axis=0)
        dp = lax.dot_general(dos, vb, NT, preferred_element_type=F32)
        dvb = lax.dot_general(pn.astype(BF16), dos, TN, preferred_element_type=F32)
        delta = jnp.sum(pn * dp, axis=-1, keepdims=True)
        ds = pn * (dp - delta)
        dsb = (ds * (HEAD_DIM**-0.5)).astype(BF16)
        dq = jnp.dot(dsb, kb, preferred_element_type=F32)
        dkb = lax.dot_general(dsb, qs, TN, preferred_element_type=F32)
        for g in range(grp):
            dq_ref[:, g * HEAD_DIM : (g + 1) * HEAD_DIM] = dq[g * BLK : (g + 1) * BLK].astype(BF16)

        @pl.when(n == 0)
        def _():
            dk_acc[...] = jnp.zeros_like(dk_acc)
            dv_acc[...] = jnp.zeros_like(dv_acc)
            dbias_ref[...] = jnp.zeros_like(dbias_ref)

        @pl.when((n == 0) & (kv == 0))
        def _():
            dsink_ref[...] = jnp.zeros_like(dsink_ref)

        dk_acc[pl.ds(start, 3 * BLK), :] += dkb
        dv_acc[pl.ds(start, 3 * BLK), :] += dvb
        dbias_ref[...] += ds.reshape(grp, BLK, 3 * BLK)
        row = lax.broadcasted_iota(I32, (heads, LANES), 0)
        sd = psink * delta
        upd = jnp.zeros((heads, LANES), F32)
        for g in range(grp):
            upd = jnp.where(row == kv * grp + g, -jnp.sum(sd[g * BLK : (g + 1) * BLK]), upd)
        dsink_ref[...] += upd

        @pl.when(n == nb - 1)
        def _():
            dk_ref[...] = dk_acc[...]
            dv_ref[...] = dv_acc[...]

    pad_spec = pl.BlockSpec((s + 2 * BLK, HEAD_DIM), lambda kv, n: (0, kv))
    kvw = N_KV_HEADS * HEAD_DIM
    return _pallas(
        body,
        name="attn_bwd",
        grid=(N_KV_HEADS, nb),
        in_specs=_attn_specs(s, grp, q_blk0) + [pl.BlockSpec((BLK, qw), lambda kv, n: (n, kv)), ANY],
        out_specs=[
            pl.BlockSpec((BLK, qw), lambda kv, n: (n, q_blk0 + kv)),
            pad_spec,
            pad_spec,
            pl.BlockSpec((grp, BLK, 3 * BLK), lambda kv, n: (kv, 0, 0)),
            pl.BlockSpec((heads, LANES), lambda kv, n: (0, 0)),
        ],
        out_shape=[
            jax.ShapeDtypeStruct(dz.shape, BF16),
            jax.ShapeDtypeStruct((s + 2 * BLK, kvw), F32),
            jax.ShapeDtypeStruct((s + 2 * BLK, kvw), F32),
            jax.ShapeDtypeStruct((heads, BLK, 3 * BLK), F32),
            jax.ShapeDtypeStruct((heads, LANES), F32),
        ],
        scratch_shapes=[pltpu.VMEM((s + 2 * BLK, HEAD_DIM), F32), pltpu.VMEM((s + 2 * BLK, HEAD_DIM), F32)],
        input_output_aliases={6: 0},
    )(sink, z, k_pad, v_pad, bias_tab, dout, dz)


def _dkv_to_dz(dk_pad, dv_pad, dz, blk_idx):
    s = dz.shape[0]
    kvw = dk_pad.shape[1]

    def body(dk_ref, dv_ref, dz_in, out_ref):
        del dz_in
        out_ref[:, :kvw] = dk_ref[...].astype(BF16)
        out_ref[:, kvw:] = dv_ref[...].astype(BF16)

    src = pl.BlockSpec((BLK, kvw), lambda i: (i + 1, 0))
    return _pallas(
        body,
        name="dkv_to_dz",
        grid=(s // BLK,),
        in_specs=[src, src, ANY],
        out_specs=pl.BlockSpec((BLK, 2 * kvw), lambda i: (i, blk_idx)),
        out_shape=jax.ShapeDtypeStruct(dz.shape, BF16),
        input_output_aliases={2: 0},
    )(dk_pad, dv_pad, dz)


def _relbias_bwd(dbias_tab, bucket):
    heads = dbias_tab.shape[0]

    def body(dt_ref, bk_ref, out_ref):
        lane = lax.broadcasted_iota(I32, (1, LANES), 1)
        bk = bk_ref[...]
        rows = []
        for h in range(heads):
            dt = dt_ref[h]
            acc = jnp.zeros((1, LANES), F32)
            for b in range(REL_BUCKETS):
                acc = jnp.where(lane == b, jnp.sum(jnp.where(bk == b, dt, 0.0)), acc)
            rows.append(acc)
        out_ref[...] = jnp.concatenate(rows, axis=0)

    return _pallas(body, name="relbias_bwd", out_shape=jax.ShapeDtypeStruct((heads, LANES), F32))(dbias_tab, bucket)


def _t5_bucket(rel):
    nb = REL_BUCKETS // 2
    ret = jnp.where(rel > 0, nb, 0)
    n = jnp.abs(rel)
    max_exact = nb // 2
    nf = jnp.maximum(n, 1).astype(F32)
    large = max_exact + (jnp.log(nf / max_exact) / math.log(REL_MAX_DIST / max_exact) * (nb - max_exact)).astype(I32)
    large = jnp.minimum(large, nb - 1)
    return ret + jnp.where(n < max_exact, n, large)


def _band_tables(rel_bias):
    qi = jnp.arange(BLK)[:, None]
    kj = jnp.arange(3 * BLK)[None, :]
    rel = kj - BLK - qi
    bucket = _t5_bucket(rel).astype(I32)
    heads = rel_bias.shape[1]
    masked = jnp.where(jnp.abs(rel) <= BLK, bucket, -1)

    def body(rb_ref, bk_ref, out_ref):
        bk = bk_ref[...]
        for h in range(heads):
            tab = jnp.full(bk.shape, NEG, F32)
            for b in range(REL_BUCKETS):
                tab = jnp.where(bk == b, rb_ref[b, h], tab)
            out_ref[h] = tab

    bias_tab = _pallas(
        body,
        name="bias_table",
        in_specs=[pl.BlockSpec(memory_space=pltpu.SMEM), pl.BlockSpec(memory_space=pltpu.VMEM)],
        out_specs=pl.BlockSpec(memory_space=pltpu.VMEM),
        out_shape=jax.ShapeDtypeStruct((heads, BLK, 3 * BLK), F32),
    )(rel_bias.astype(F32), masked)
    return bias_tab, bucket


EW_BLOCK_ELEMS = 512 * 1024


def _ew_tiles(shape, elems=EW_BLOCK_ELEMS // 2):
    r, c = shape
    tn = c if c <= 2048 else _tile(c, (2048, 1920, 1536, 1408, 1024, 512))
    tm = _tile(r, [t for t in (1024, 512, 256, 128, 64, 32, 16, 8) if t * tn <= elems] or [8])
    return tm, tn


def _cast_into_full(name, qidx, w, kind, after=()):
    r, c = w.shape
    tm, tn = _ew_tiles(w.shape, EW_BLOCK_ELEMS)
    nbi, nbj = r // tm, c // tn
    if kind == "col":
        full, out_spec = (r, c * N_CHIPS), pl.BlockSpec((tm, tn), lambda i, j, q: (i, q[0] * nbj + j))
    else:
        full, out_spec = (r * N_CHIPS, c), pl.BlockSpec((tm, tn), lambda i, j, q: (q[0] * nbi + i, j))

    def body(q_ref, w_ref, *rest):
        del q_ref
        rest[-1][...] = w_ref[...].astype(BF16)

    return _pallas(
        body,
        name=name,
        grid_spec=pltpu.PrefetchScalarGridSpec(
            num_scalar_prefetch=1,
            grid=(nbi, nbj),
            in_specs=[pl.BlockSpec((tm, tn), lambda i, j, q: (i, j))] + [ANY] * len(after),
            out_specs=out_spec,
        ),
        out_shape=jax.ShapeDtypeStruct(full, BF16),
    )(qidx, w, *after)


def _adamw(name, w, g, m, v, after=()):
    tm, tn = _ew_tiles(w.shape, EW_BLOCK_ELEMS)
    if _nbytes(w.shape, F32) <= 1024 * 1024:
        tm, tn = w.shape
    spec = pl.BlockSpec((tm, tn), lambda i, j: (i, j))
    n_after = len(after)

    def body(w_ref, g_ref, m_ref, v_ref, *rest):
        d_ref, nm_ref, nv_ref, g_out_ref = rest[n_after:]
        gv = g_ref[...]
        g_out_ref[...] = gv
        nm = ADAM_B1 * m_ref[...] + (1.0 - ADAM_B1) * gv
        nv = ADAM_B2 * v_ref[...] + (1.0 - ADAM_B2) * (gv * gv)
        m_hat = nm / (1.0 - ADAM_B1**ADAM_STEP)
        v_hat = nv / (1.0 - ADAM_B2**ADAM_STEP)
        d_ref[...] = -ADAM_LR * (m_hat / (jnp.sqrt(v_hat) + ADAM_EPS) + ADAM_WD * w_ref[...])
        nm_ref[...] = nm
        nv_ref[...] = nv

    out = jax.ShapeDtypeStruct(w.shape, F32)
    return _pallas(
        body, name=name, grid=(w.shape[0] // tm, w.shape[1] // tn), in_specs=[spec] * 4 + [ANY] * n_after,
        out_specs=[spec] * 4, out_shape=[out, out, out, out],
        compiler_params=_params(_mm_vmem([((tm, tn), F32, 24)])),
    )(w, g, m, v, *after)


def _pair_add(name, cidx, g_full, r_sib, kind):
    hr, hc = r_sib.shape
    tm, tn = _ew_tiles((hr, hc), 2 * EW_BLOCK_ELEMS)
    nbi, nbj = hr // tm, hc // tn
    if kind == "col":
        g_spec = pl.BlockSpec((tm, tn), lambda i, j, c: (c[0] * nbi + i, j))
    else:
        g_spec = pl.BlockSpec((tm, tn), lambda i, j, c: (i, c[0] * nbj + j))
    spec = pl.BlockSpec((tm, tn), lambda i, j, c: (i, j))

    def body(c_ref, g_ref, r_ref, o_ref):
        del c_ref
        o_ref[...] = (g_ref[...].astype(F32) + r_ref[...].astype(F32)).astype(BF16)

    return _pallas(
        body,
        name=name,
        grid_spec=pltpu.PrefetchScalarGridSpec(num_scalar_prefetch=1, grid=(nbi, nbj), in_specs=[g_spec, spec], out_specs=spec),
        out_shape=jax.ShapeDtypeStruct((hr, hc), BF16),
        compiler_params=_params(_mm_vmem([((tm, tn), BF16, 6), ((tm, tn), F32, 3)])),
    )(cidx, g_full, r_sib)


def _chip_sum(name, qidx, c_half, r_ici, kind):
    _, pr, pc = r_ici.shape
    tm, tn = _ew_tiles((pr, pc), 2 * EW_BLOCK_ELEMS)
    nbi, nbj = pr // tm, pc // tn
    if kind == "col":
        own_spec = pl.BlockSpec((tm, tn), lambda i, j, q: (i, q[0] * nbj + j))
        full, out_spec = (2 * pr, pc), pl.BlockSpec((tm, tn), lambda i, j, q: (q[1] * nbi + i, j))
    else:
        own_spec = pl.BlockSpec((tm, tn), lambda i, j, q: (q[0] * nbi + i, j))
        full, out_spec = (pr, 2 * pc), pl.BlockSpec((tm, tn), lambda i, j, q: (i, q[1] * nbj + j))

    def body(q_ref, own_ref, r_ref, o_ref):
        q = q_ref[0]
        own = own_ref[...].astype(F32)
        recv = [r_ref[r].astype(F32) for r in range(3)]
        total = None
        for chip in range(N_CHIPS):
            d = chip ^ q
            term = jnp.where(d == 0, own, jnp.where(d == 2, recv[0], jnp.where(d == 1, recv[1], recv[2])))
            total = term if total is None else total + term
        o_ref[...] = total

    return _pallas(
        body,
        name=name,
        grid_spec=pltpu.PrefetchScalarGridSpec(
            num_scalar_prefetch=1,
            grid=(nbi, nbj),
            in_specs=[own_spec, pl.BlockSpec((3, tm, tn), lambda i, j, q: (0, i, j))],
            out_specs=out_spec,
        ),
        out_shape=jax.ShapeDtypeStruct(full, F32),
        compiler_params=_params(_mm_vmem([((tm, tn), BF16, 8), ((tm, tn), F32, 6)])),
    )(qidx, c_half, r_ici)


_REL_MASK = (2, 1, 3)


def _place():
    x, y, c = lax.axis_index("x"), lax.axis_index("y"), lax.axis_index("c")
    chips = [(1 - x, y), (x, 1 - y), (1 - x, 1 - y)]
    return x, y, c, 2 * x + y, chips


def _shard_view(ref, kind, chip):
    if kind == "col":
        w = ref.shape[1] // N_CHIPS
        return ref.at[:, pl.ds(pl.multiple_of(chip * w, LANES), w)]
    h = ref.shape[0] // N_CHIPS
    return ref.at[pl.ds(pl.multiple_of(chip * h, 16), h), :]


def _row_half(ref, half):
    h = ref.shape[0] // 2
    return ref.at[pl.ds(pl.multiple_of(half * h, 16), h), :]


def _pair_half(ref, kind, half):
    if kind == "col":
        return _row_half(ref, half)
    w = ref.shape[1] // 2
    return ref.at[:, pl.ds(pl.multiple_of(half * w, LANES), w)]


def _remote(src, dst, send_sem, recv_sem, dev):
    return pltpu.make_async_remote_copy(src_ref=src, dst_ref=dst, send_sem=send_sem, recv_sem=recv_sem, device_id=dev, device_id_type=MESH)


def _hbm(a):
    return pltpu.with_memory_space_constraint(a, pltpu.HBM)


def _gather_start(name, fulls, kinds, rels=(0, 1, 2), after=()):
    n_w = len(fulls)

    def body(*refs):
        g = refs[:n_w]
        send_sem, recv_sem = refs[n_w + len(after)], refs[n_w + len(after) + 1]
        token = refs[-1]
        _, _, c, q, chips = _place()
        for w in range(n_w):
            mine = _row_half(_shard_view(g[w], kinds[w], q), c)
            for r in rels if isinstance(rels, tuple) else rels[w]:
                _remote(mine, mine, send_sem.at[3 * w + r], recv_sem.at[3 * w + r], (*chips[r], c)).start()
        token[...] = jnp.zeros_like(token)

    res = _pallas(
        body,
        name=name,
        out_shape=(
            pltpu.SemaphoreType.DMA((3 * n_w,)),
            pltpu.SemaphoreType.DMA((3 * n_w,)),
            *[pltpu.HBM(f.shape, f.dtype) for f in fulls],
            jax.ShapeDtypeStruct((8, LANES), F32),
        ),
        in_specs=[HBM_SPEC] * n_w + [ANY] * len(after),
        out_specs=(SEM_SPEC, SEM_SPEC, *[HBM_SPEC] * n_w, pl.BlockSpec(memory_space=pltpu.VMEM)),
        input_output_aliases={w: w + 2 for w in range(n_w)},
        compiler_params=pltpu.CompilerParams(has_side_effects=EFFECT),
    )(*[_hbm(f) for f in fulls], *after)
    return res[0], res[1], list(res[2 : 2 + n_w]), res[-1]


def _relay_copies(kinds, waiting):
    def copies(refs, send_sem, recv_sem):
        _, _, c, q, chips = _place()
        out = []
        for i, kind in enumerate(kinds):
            for k, (src_rel, dst_rel) in enumerate(((0, 1), (1, 0))):
                held = _row_half(_row_half(_shard_view(refs[i], kind, q ^ _REL_MASK[src_rel]), c), k)
                far = _row_half(_row_half(_shard_view(refs[i], kind, q ^ _REL_MASK[2]), c), k)
                dst = far if waiting else held
                out.append(_remote(held, dst, send_sem.at[2 * i + k], recv_sem.at[2 * i + k], (*chips[dst_rel], c)))
        return out

    return copies


def _forward_copies(kinds, waiting):
    def copies(refs, send_sem, recv_sem):
        x, y, c, q, _ = _place()
        out = []
        for i, kind in enumerate(kinds):
            for r in range(3):
                quarter = _shard_view(refs[i], kind, q ^ _REL_MASK[r])
                landed = _row_half(quarter, c)
                dst = _row_half(quarter, 1 - c) if waiting else landed
                out.append(_remote(landed, dst, send_sem.at[3 * i + r], recv_sem.at[3 * i + r], (x, y, 1 - c)))
        return out

    return copies


def _gather_wait(name, fulls, kinds, w_ids, send_sem, recv_sem, after, rels=(0, 1, 2)):
    n = len(fulls)

    def body(*refs):
        g = refs[:n]
        s_sem, r_sem = refs[n], refs[n + 1]
        x, y, c, q, _ = _place()
        for i, w in enumerate(w_ids):
            mine = _row_half(_shard_view(g[i], kinds[i], q), c)
            for r in rels:
                landed = _row_half(_shard_view(g[i], kinds[i], q ^ _REL_MASK[r]), c)
                cp = _remote(mine, landed, s_sem.at[3 * w + r], r_sem.at[3 * w + r], (x, y, 1 - c))
                cp.wait_send()
                cp.wait_recv()

    res = _pallas(
        body,
        name=name,
        out_shape=[pltpu.HBM(f.shape, f.dtype) for f in fulls],
        in_specs=[HBM_SPEC] * n + [SEM_SPEC, SEM_SPEC, ANY],
        out_specs=[HBM_SPEC] * n,
        input_output_aliases={i: i for i in range(n)},
        compiler_params=pltpu.CompilerParams(has_side_effects=EFFECT),
    )(*fulls, send_sem, recv_sem, after)
    return list(res)


def _gather_forward(name, fulls, kinds):
    n = len(fulls)

    def body(*refs):
        g = refs[n : 2 * n]
        send, recv = refs[2 * n :]
        x, y, c, q, _ = _place()
        sib = (x, y, 1 - c)
        cps = []
        for i in range(n):
            for r in range(3):
                landed = _row_half(_shard_view(g[i], kinds[i], q ^ _REL_MASK[r]), c)
                cps.append(_remote(landed, landed, send.at[i, r], recv.at[i, r], sib))
        for cp in cps:
            cp.start()
        for i in range(n):
            for r in range(3):
                other = _row_half(_shard_view(g[i], kinds[i], q ^ _REL_MASK[r]), 1 - c)
                _remote(other, other, send.at[i, r], recv.at[i, r], sib).wait_recv()
        for cp in cps:
            cp.wait_send()

    res = _pallas(
        body,
        name=name,
        in_specs=[ANY] * n,
        out_specs=[ANY] * n,
        out_shape=[jax.ShapeDtypeStruct(f.shape, f.dtype) for f in fulls],
        scratch_shapes=[pltpu.SemaphoreType.DMA((n, 3)), pltpu.SemaphoreType.DMA((n, 3))],
        input_output_aliases={i: i for i in range(n)},
    )(*fulls)
    return list(res)


def _split_start(name, bufs, n_sems, copies):
    n = len(bufs)

    def body(*refs):
        for cp in copies(refs[:n], refs[n], refs[n + 1]):
            cp.start()

    res = _pallas(
        body,
        name=name,
        out_shape=(
            pltpu.SemaphoreType.DMA((n_sems,)),
            pltpu.SemaphoreType.DMA((n_sems,)),
            *[pltpu.HBM(b.shape, b.dtype) for b in bufs],
        ),
        in_specs=[HBM_SPEC] * n,
        out_specs=(SEM_SPEC, SEM_SPEC, *[HBM_SPEC] * n),
        input_output_aliases={i: i + 2 for i in range(n)},
        compiler_params=pltpu.CompilerParams(has_side_effects=EFFECT),
    )(*[_hbm(b) for b in bufs])
    return res[0], res[1], list(res[2:])


def _split_wait(name, bufs, send_sem, recv_sem, copies, after):
    n = len(bufs)

    def body(*refs):
        for cp in copies(refs[:n], refs[n], refs[n + 1]):
            cp.wait_send()
            cp.wait_recv()

    res = _pallas(
        body,
        name=name,
        out_shape=[pltpu.HBM(b.shape, b.dtype) for b in bufs],
        in_specs=[HBM_SPEC] * n + [SEM_SPEC, SEM_SPEC, ANY],
        out_specs=[HBM_SPEC] * n,
        input_output_aliases={i: i for i in range(n)},
        compiler_params=pltpu.CompilerParams(has_side_effects=EFFECT),
    )(*bufs, send_sem, recv_sem, after)
    return list(res)


def _pair_exchange_copies(kinds):
    n = len(kinds)

    def copies(refs, send_sem, recv_sem):
        x, y, c, _, _ = _place()
        return [
            _remote(_pair_half(refs[w], kinds[w], 1 - c), refs[n + w], send_sem.at[w], recv_sem.at[w], (x, y, 1 - c))
            for w in range(n)
        ]

    return copies


def _pair_share_copies(kinds, waiting):
    def copies(refs, send_sem, recv_sem):
        x, y, c, _, _ = _place()
        out = []
        for w, kind in enumerate(kinds):
            mine = _pair_half(refs[w], kind, c)
            dst = _pair_half(refs[w], kind, 1 - c) if waiting else mine
            out.append(_remote(mine, dst, send_sem.at[w], recv_sem.at[w], (x, y, 1 - c)))
        return out

    return copies


def _piece_shape(half_shape, kind):
    r, c = half_shape
    return (3, r, c // N_CHIPS) if kind == "col" else (3, r // N_CHIPS, c)


def _chip_send_start(name, halves, kinds):
    n = len(halves)
    lands = [lax.empty(_piece_shape(h.shape, k), BF16) for h, k in zip(halves, kinds)]

    def body(*refs):
        h, land = refs[:n], refs[n : 2 * n]
        send_sem, recv_sem = refs[2 * n], refs[2 * n + 1]
        _, _, c, q, chips = _place()
        for i in range(n):
            for r, chip in enumerate(chips):
                piece = _shard_view(h[i], kinds[i], q ^ _REL_MASK[r])
                _remote(piece, land[i].at[r], send_sem.at[3 * i + r], recv_sem.at[3 * i + r], (*chip, c)).start()

    res = _pallas(
        body,
        name=name,
        out_shape=(
            pltpu.SemaphoreType.DMA((3 * n,)),
            pltpu.SemaphoreType.DMA((3 * n,)),
            *[pltpu.HBM(a.shape, a.dtype) for a in halves],
            *[pltpu.HBM(a.shape, a.dtype) for a in lands],
        ),
        in_specs=[HBM_SPEC] * (2 * n),
        out_specs=(SEM_SPEC, SEM_SPEC, *[HBM_SPEC] * (2 * n)),
        input_output_aliases={i: i + 2 for i in range(2 * n)},
        compiler_params=pltpu.CompilerParams(has_side_effects=EFFECT),
    )(*[_hbm(a) for a in halves], *[_hbm(a) for a in lands])
    return res[0], res[1], list(res[2 : 2 + n]), list(res[2 + n :])


def _chip_send_wait(name, halves, lands, kinds, send_sem, recv_sem, after):
    n = len(halves)

    def body(*refs):
        h, land = refs[:n], refs[n : 2 * n]
        s_sem, r_sem = refs[2 * n], refs[2 * n + 1]
        x, y, c, q, _ = _place()
        for i in range(n):
            for r in range(3):
                piece = _shard_view(h[i], kinds[i], q ^ _REL_MASK[r])
                cp = _remote(piece, land[i].at[r], s_sem.at[3 * i + r], r_sem.at[3 * i + r], (x, y, 1 - c))
                cp.wait_send()
                cp.wait_recv()

    res = _pallas(
        body,
        name=name,
        out_shape=[pltpu.HBM(a.shape, a.dtype) for a in halves] + [pltpu.HBM(a.shape, a.dtype) for a in lands],
        in_specs=[HBM_SPEC] * (2 * n) + [SEM_SPEC, SEM_SPEC, ANY],
        out_specs=[HBM_SPEC] * (2 * n),
        input_output_aliases={i: i for i in range(2 * n)},
        compiler_params=pltpu.CompilerParams(has_side_effects=EFFECT),
    )(*halves, *lands, send_sem, recv_sem, after)
    return list(res[:n]), list(res[n:])


def _small_exchange_copies(waiting):
    def copies(refs, send_sem, recv_sem):
        p, land = refs
        x, y, c, q, _ = _place()
        me = 2 * q + c
        out = []
        for dd in range(1, 2 * N_CHIPS):
            dev = (x ^ ((dd >> 2) & 1), y ^ ((dd >> 1) & 1), c ^ (dd & 1))
            dst = land.at[me ^ dd] if waiting else land.at[me]
            out.append(_remote(p, dst, send_sem.at[dd - 1], recv_sem.at[dd - 1], dev))
        return out

    return copies


def _small_sum(name, me_idx, p, land):
    rows = p.shape[0]
    n_dev = 2 * N_CHIPS

    def body(me_ref, p_ref, land_ref, o_ref):
        me = me_ref[0]
        total = None
        for dev in range(n_dev):
            term = jnp.where(me == dev, p_ref[...], land_ref[dev])
            total = term if total is None else total + term
        o_ref[...] = total

    return _pallas(
        body,
        name=name,
        grid_spec=pltpu.PrefetchScalarGridSpec(
            num_scalar_prefetch=1,
            grid=(1,),
            in_specs=[pl.BlockSpec((rows, LANES), lambda i, m: (0, 0)), pl.BlockSpec((n_dev, rows, LANES), lambda i, m: (0, 0, 0))],
            out_specs=pl.BlockSpec((rows, LANES), lambda i, m: (0, 0)),
        ),
        out_shape=jax.ShapeDtypeStruct(p.shape, F32),
    )(me_idx, p, land)


def _pack(parts):
    rows = []
    for a in parts:
        flat = a.reshape(-1).astype(F32)
        n = flat.shape[0]
        padded = -(-n // (8 * LANES)) * (8 * LANES)
        rows.append(jnp.pad(flat, (0, padded - n)).reshape(-1, LANES))
    return jnp.concatenate(rows, axis=0)


def _unpack(packed, shapes):
    out, row = [], 0
    for shp in shapes:
        n = int(np.prod(shp))
        nrows = -(-n // (8 * LANES)) * 8
        out.append(packed[row : row + nrows].reshape(-1)[:n].reshape(shp))
        row += nrows
    return out


def kernel(x, w_in, norm_mix, sgu_v_gain, sgu_w_s, sgu_b_s, w_a_out, attn_sink, rel_bias, w_b_out, w_o, norm_ffn, w_gate, w_up, w_down, norm_final, loss_target, m_w_in, m_norm_mix, m_sgu_v_gain, m_sgu_w_s, m_sgu_b_s, m_w_a_out, m_attn_sink, m_rel_bias, m_w_b_out, m_w_o, m_norm_ffn, m_w_gate, m_w_up, m_w_down, m_norm_final, v_w_in, v_norm_mix, v_sgu_v_gain, v_sgu_w_s, v_sgu_b_s, v_w_a_out, v_attn_sink, v_rel_bias, v_w_b_out, v_w_o, v_norm_ffn, v_w_gate, v_w_up, v_w_down, v_norm_final):
    s, d = x.shape[1], x.shape[2]
    w_sgu = sgu_v_gain.shape[1]
    groups = sgu_w_s.shape[1]
    heads = attn_sink.shape[1]
    grp = heads // N_KV_HEADS
    w_att = heads * HEAD_DIM
    w_kv = N_KV_HEADS * HEAD_DIM
    d_ff = w_gate.shape[2] * N_CHIPS
    n_in = w_in.shape[2] * N_CHIPS
    off_q = 2 * w_sgu
    off_k = off_q + w_att
    off_g = off_k + 2 * w_kv
    assert n_in == off_g + 2 * d and groups * BLK == w_sgu and s % BLK == 0

    x2d = x.reshape(s, d)
    tgt = loss_target.reshape(s, d)
    c_idx = lax.axis_index("c").astype(I32).reshape(1)
    q_idx = (2 * lax.axis_index("x") + lax.axis_index("y")).astype(I32).reshape(1)
    qc_idx = jnp.concatenate([q_idx, c_idx])

    W_IN, W_A, W_B, W_O, W_GATE, W_UP, W_DOWN = range(7)
    names = ["w_in", "w_a", "w_b", "w_o", "w_gate", "w_up", "w_down"]
    kinds = ["col", "col", "col", "row", "col", "col", "row"]
    big_w = [w_in[0], w_a_out[0], w_b_out[0], w_o[0], w_gate[0], w_up[0], w_down[0]]
    big_m = [m_w_in[0], m_w_a_out[0], m_w_b_out[0], m_w_o[0], m_w_gate[0], m_w_up[0], m_w_down[0]]
    big_v = [v_w_in[0], v_w_a_out[0], v_w_b_out[0], v_w_o[0], v_w_gate[0], v_w_up[0], v_w_down[0]]
    full_in = _cast_into_full("cast_w_in", q_idx, big_w[W_IN], kinds[W_IN])
    in_send, in_recv, (full_in,), token = _gather_start("gather_start_in", [full_in], [kinds[W_IN]], rels=(0, 1))
    rest = [_cast_into_full("cast_" + names[i], q_idx, big_w[i], kinds[i], after=(token,)) for i in range(1, 7)]
    h1 = _rms_fwd("rms_mix", x2d, norm_mix, after=(token,))
    (full_in,) = _gather_wait("gather_wait_in", [full_in], [kinds[W_IN]], [0], in_send, in_recv, h1, rels=(0, 1))
    relay_send, relay_recv, (full_in,) = _split_start("gather_relay_in", [full_in], 2, _relay_copies([kinds[W_IN]], False))
    full_down = rest.pop()
    ag_send, ag_recv, rest, token = _gather_start("gather_start_rest", rest, kinds[1:6], rels=(0, 1), after=(full_in,))
    (full_in,) = _split_wait("gather_relay_wait_in", [full_in], relay_send, relay_recv, _relay_copies([kinds[W_IN]], True), token)
    in_fwd = _split_start("gather_fwd_start_in", [full_in], 3, _forward_copies([kinds[W_IN]], False))
    fulls = [None] + rest

    def relay_begin(tag, ids, after):
        ks = [kinds[i] for i in ids]
        bufs = _gather_wait("gather_wait_" + tag, [fulls[i] for i in ids], ks, [i - 1 for i in ids], ag_send, ag_recv, after,
                            rels=(0, 1))
        send, recv, bufs = _split_start("gather_relay_" + tag, bufs, 2 * len(ids), _relay_copies(ks, False))
        return tag, ks, send, recv, bufs

    def relay_end(state, after):
        tag, ks, send, recv, bufs = state
        bufs = _split_wait("gather_relay_wait_" + tag, bufs, send, recv, _relay_copies(ks, True), after)
        return _gather_forward("gather_fwd_" + tag, bufs, ks)

    def relay_end_async(state, after):
        tag, ks, send, recv, bufs = state
        bufs = _split_wait("gather_relay_wait_" + tag, bufs, send, recv, _relay_copies(ks, True), after)
        send, recv, bufs = _split_start("gather_fwd_start_" + tag, bufs, 3 * len(ks), _forward_copies(ks, False))
        return tag, ks, send, recv, bufs

    def forwarded(state, after):
        tag, ks, send, recv, bufs = state
        return _split_wait("gather_fwd_wait_" + tag, bufs, send, recv, _forward_copies(ks, True), after)

    ws_b = sgu_w_s[0].astype(BF16)
    wst_b = jnp.swapaxes(sgu_w_s[0], 1, 2).astype(BF16)
    b_col = sgu_b_s[0].reshape(groups, BLK, 1)
    bias_tab, bucket = _band_tables(rel_bias)
    sink = attn_sink[0]

    tm = _tile(s, (1024, 512, 256, 128))

    z = _mm_quarters("mm_z_own", q_idx, 1, h1, in_fwd[2][0], None)
    (g_in,) = _split_wait("gather_fwd_wait_in", in_fwd[2], in_fwd[0], in_fwd[1], _forward_copies([kinds[W_IN]], True), z)
    z = _mm_quarters("mm_z_rest", jnp.concatenate([q_idx ^ mask for mask in _REL_MASK]), 3, h1, g_in, z)
    mix_relay = relay_begin("mix", [W_A, W_B, W_O], z)

    a_act = _sgu_fwd(z, sgu_v_gain, ws_b, b_col, w_sgu, after=(mix_relay[4][0],))

    kv_b = z[:, off_k:off_g]
    k_pad = jnp.pad(kv_b[:, :w_kv], ((BLK, BLK), (0, 0)))
    v_pad = jnp.pad(kv_b[:, w_kv:], ((BLK, BLK), (0, 0)))
    q_blk0 = off_q // (grp * HEAD_DIM)
    att = _attn_fwd(sink, z, k_pad, v_pad, bias_tab, grp, q_blk0)

    gate_relay = relay_begin("gate", [W_GATE], att)
    up_relay = relay_begin("up", [W_UP], gate_relay[4][0])
    down_send, down_recv, (full_down,), token = _gather_start(
        "gather_start_down", [full_down], [kinds[W_DOWN]], after=(a_act, up_relay[4][0])
    )
    g_a, g_b, g_o = relay_end(mix_relay, token)

    tg = _tile(d, (512,))
    ga0, gb0 = off_g // tg, (off_g + d) // tg

    def ep_gate(ins, vals, outs, cs):
        sa, sb = _sigmoid(ins[4][:, cs].astype(F32)), _sigmoid(ins[5][:, cs].astype(F32))
        _put(outs[0], cs, sa * vals[0] + sb * vals[1])
        _put(outs[1], cs, vals[0])
        _put(outs[2], cs, vals[1])

    t_out = pl.BlockSpec((tm, tg), lambda i, j, k: (i, j))
    m_act, y_a, y_b = _mm(
        "mm_branches", (s // tm, d // tg, 1), [a_act, g_a, att, g_b, z, z],
        [pl.BlockSpec((tm, w_sgu), lambda i, j, k: (i, 0)), pl.BlockSpec((w_sgu, tg), lambda i, j, k: (0, j)),
         pl.BlockSpec((tm, w_att), lambda i, j, k: (i, 0)), pl.BlockSpec((w_att, tg), lambda i, j, k: (0, j)),
         pl.BlockSpec((tm, tg), lambda i, j, k: (i, ga0 + j)), pl.BlockSpec((tm, tg), lambda i, j, k: (i, gb0 + j))],
        [jax.ShapeDtypeStruct((s, d), BF16)] * 3,
        [t_out, t_out, t_out], [(0, 1, NN, 0), (2, 3, NN, 1)], 2, (tm, tg), 1, ep_gate,
        _mm_vmem([((tm, w_sgu), BF16, 4), ((w_sgu, tg), BF16, 4), ((tm, tg), F32, 12)]),    )

    tn = _tile(d, (1024, 512))

    def ep_residual(ins, vals, outs, cs):
        _put(outs[0], cs, ins[2][:, cs] + vals[0])

    gate_fwd = relay_end_async(gate_relay, m_act)
    x2 = _mm(
        "mm_wo", (s // tm, d // tn, 1), [m_act, g_o, x2d],
        [pl.BlockSpec((tm, d), lambda i, j, k: (i, 0)), pl.BlockSpec((d, tn), lambda i, j, k: (0, j)),
         pl.BlockSpec((tm, tn), lambda i, j, k: (i, j))],
        [jax.ShapeDtypeStruct((s, d), F32)], [pl.BlockSpec((tm, tn), lambda i, j, k: (i, j))],
        [(0, 1, NN, 0)], 1, (tm, tn), 1, ep_residual,
        _mm_vmem([((tm, d), BF16, 2), ((d, tn), BF16, 2), ((tm, tn), F32, 5)]),
        after=(gate_fwd[4][0],),
    )[0]
    (g_gate,) = forwarded(gate_fwd, x2)
    up_fwd = relay_end_async(up_relay, g_gate)
    h2 = _rms_fwd("rms_ffn", x2, norm_ffn, after=(up_fwd[4][0],))
    (g_up,) = forwarded(up_fwd, h2)

    tf = _tile(d_ff, (512,))

    def ep_swiglu(ins, vals, outs, cs):
        gt, up = vals
        _put(outs[0], cs, gt)
        _put(outs[1], cs, up)
        _put(outs[2], cs, (gt * _sigmoid(gt)) * up)

    f_out = pl.BlockSpec((tm, tf), lambda i, j, k: (i, j))
    gt, up, f_act = _mm(
        "mm_gate_up", (s // tm, d_ff // tf, 1), [h2, g_gate, g_up],
        [pl.BlockSpec((tm, d), lambda i, j, k: (i, 0)), pl.BlockSpec((d, tf), lambda i, j, k: (0, j)),
         pl.BlockSpec((d, tf), lambda i, j, k: (0, j))],
        [jax.ShapeDtypeStruct((s, d_ff), BF16)] * 3,
        [f_out, f_out, f_out], [(0, 1, NN, 0), (0, 2, NN, 1)], 2, (tm, tf), 1, ep_swiglu,
        _mm_vmem([((tm, d), BF16, 2), ((d, tf), BF16, 4), ((tm, tf), F32, 8)]),    )
    (g_down,) = _gather_forward(
        "gather_fwd_ffn_out",
        _gather_wait("gather_wait_ffn_out", [full_down], [kinds[W_DOWN]], [0], down_send, down_recv, f_act),
        [kinds[W_DOWN]],
    )

    tkf = _tile(d_ff, (1408, 1024, 512))
    tml, tnl = _tile(s, (512, 256, 128)), _tile(d, (512,))
    x3 = _mm(
        "mm_down", (s // tml, d // tnl, 1), [f_act, g_down, x2],
        [pl.BlockSpec((tml, d_ff), lambda i, j, k: (i, 0)), pl.BlockSpec((d_ff, tnl), lambda i, j, k: (0, j)),
         pl.BlockSpec((tml, tnl), lambda i, j, k: (i, j))],
        [jax.ShapeDtypeStruct((s, d), F32)], [pl.BlockSpec((tml, tnl), lambda i, j, k: (i, j))],
        [(0, 1, NN, 0)], 1, (tml, tnl), 1, ep_residual,
        _mm_vmem([((tml, d_ff), BF16, 2), ((d_ff, tnl), BF16, 2), ((tml, tnl), F32, 6)]),    )[0]

    dx3, dx3b, dg_final, loss_part = _head(x3, norm_final.reshape(1, d), tgt)

    def reduce_a(tag, ids, grads):
        ks = [kinds[i] for i in ids]
        lands = [lax.empty((g.shape[0] // 2, g.shape[1]) if k == "col" else (g.shape[0], g.shape[1] // 2), BF16)
                 for g, k in zip(grads, ks)]
        send, recv, bufs = _split_start("pair_send_" + tag, list(grads) + lands, len(ids), _pair_exchange_copies(ks))
        return {"tag": tag, "ids": ids, "ks": ks, "pair": (send, recv, bufs), "token": bufs[0]}

    def reduce_b(st, after):
        tag, ids, ks = st["tag"], st["ids"], st["ks"]
        send, recv, bufs = st["pair"]
        bufs = _split_wait("pair_wait_" + tag, bufs, send, recv, _pair_exchange_copies(ks), after)
        grads, from_sib = bufs[: len(ids)], bufs[len(ids) :]
        halves = [_pair_add("pair_add_" + names[i], c_idx, g, r, k) for i, g, r, k in zip(ids, grads, from_sib, ks)]
        st["chip"] = _chip_send_start("chip_send_" + tag, halves, ks)
        st["token"] = st["chip"][2][0]

    def reduce_c(st, after):
        tag, ids, ks = st["tag"], st["ids"], st["ks"]
        send, recv, halves, lands = st["chip"]
        halves, lands = _chip_send_wait("chip_wait_" + tag, halves, lands, ks, send, recv, after)
        pieces = [_chip_sum("chip_sum_" + names[i], qc_idx, h, r, k) for i, h, r, k in zip(ids, halves, lands, ks)]
        st["share"] = _split_start("share_send_" + tag, pieces, len(ids), _pair_share_copies(ks, False))
        st["token"] = st["share"][2][0]

    def reduce_d(st, after):
        send, recv, bufs = st["share"]
        return _split_wait("share_wait_" + st["tag"], bufs, send, recv, _pair_share_copies(st["ks"], True), after)

    grads_big, upd = [None] * 7, [None] * 7

    def finish(st, after):
        shared = reduce_d(st, after)
        after = shared[0]
        for i, g in zip(st["ids"], shared):
            upd[i] = _adamw("adamw_" + names[i], big_w[i], g, big_m[i], big_v[i], after=(after,))
            grads_big[i] = upd[i][3]
            after = upd[i][0]
        return after

    def ep_swiglu_bwd(ins, vals, outs, cs):
        df = vals[0]
        gtv, upv = ins[2][:, cs].astype(F32), ins[3][:, cs].astype(F32)
        sg = _sigmoid(gtv)
        _put(outs[0], cs, df * upv * (sg + gtv * sg * (1.0 - sg)))
        _put(outs[1], cs, df * (gtv * sg))

    dgt, dup = _mm(
        "mm_dswiglu", (s // tm, d_ff // tf, 1), [dx3b, g_down, gt, up],
        [pl.BlockSpec((tm, d), lambda i, j, k: (i, 0)), pl.BlockSpec((tf, d), lambda i, j, k: (j, 0)), f_out, f_out],
        [jax.ShapeDtypeStruct((s, d_ff), BF16), jax.ShapeDtypeStruct((s, d_ff), BF16)], [f_out, f_out],
        [(0, 1, NT, 0)], 1, (tm, tf), 1, ep_swiglu_bwd,
        _mm_vmem([((tm, d), BF16, 2), ((tf, d), BF16, 2), ((tm, tf), F32, 8)]),    )

    def ep_store(ins, vals, outs, cs):
        for o, v in zip(outs, vals):
            _put(o, cs, v)

    twn = _tile(d, (1024, 512))
    gw_down = _mm(
        "mm_gw_down", (d_ff // tkf, d // twn, 1), [f_act, dx3b],
        [pl.BlockSpec((s, tkf), lambda i, j, k: (0, i)), pl.BlockSpec((s, twn), lambda i, j, k: (0, j))],
        [jax.ShapeDtypeStruct((d_ff, d), BF16)], [pl.BlockSpec((tkf, twn), lambda i, j, k: (i, j))],
        [(0, 1, TN, 0)], 1, (tkf, twn), 1, ep_store,
        _mm_vmem([((s, tkf), BF16, 3), ((s, twn), BF16, 2), ((tkf, twn), F32, 3)]),
    )[0]
    red_down = reduce_a("down", [W_DOWN], [gw_down])

    tn2 = _tile(d, (256,))
    dh2_specs = [pl.BlockSpec((tm, d_ff), lambda i, j, k: (i, 0)), pl.BlockSpec((tn2, d_ff), lambda i, j, k: (j, 0))]
    dh2_tile = pl.BlockSpec((tm, tn2), lambda i, j, k: (i, j))
    dh2_vmem = _mm_vmem([((tm, d_ff), BF16, 2), ((tn2, d_ff), BF16, 2), ((tm, tn2), F32, 7)])
    dh2 = _mm(
        "mm_dh2_gate", (s // tm, d // tn2, 1), [dgt, g_gate], dh2_specs,
        [jax.ShapeDtypeStruct((s, d), F32)], [dh2_tile], [(0, 1, NT, 0)], 1, (tm, tn2), 1, ep_store, dh2_vmem,
        after=(red_down["token"],),
    )[0]
    dh2 = _mm(
        "mm_dh2_up", (s // tm, d // tn2, 1), [dup, g_up, dh2], dh2_specs + [dh2_tile],
        [jax.ShapeDtypeStruct((s, d), F32)], [dh2_tile], [(0, 1, NT, 0)], 1, (tm, tn2), 1, ep_residual, dh2_vmem,
    )[0]
    reduce_b(red_down, dh2)

    twr = _tile(d, (1024, 512))
    w_tile = pl.BlockSpec((twr, tf), lambda i, j, k: (i, j))
    gw_gate, gw_up = _mm(
        "mm_gw_gate_up", (d // twr, d_ff // tf, 1), [h2, dgt, dup],
        [pl.BlockSpec((s, twr), lambda i, j, k: (0, i)), pl.BlockSpec((s, tf), lambda i, j, k: (0, j)),
         pl.BlockSpec((s, tf), lambda i, j, k: (0, j))],
        [jax.ShapeDtypeStruct((d, d_ff), BF16), jax.ShapeDtypeStruct((d, d_ff), BF16)], [w_tile, w_tile],
        [(0, 1, TN, 0), (0, 2, TN, 1)], 2, (twr, tf), 1, ep_store,
        _mm_vmem([((s, twr), BF16, 3), ((s, tf), BF16, 4), ((twr, tf), F32, 6)]),
        after=(red_down["token"],),
    )
    red_ffn = reduce_a("ffn_in", [W_GATE, W_UP], [gw_gate, gw_up])

    dx2, dx2b, dg_ffn = _rms_bwd("rms_ffn_bwd", x2, norm_ffn, dh2, dx3, after=(red_ffn["token"],))

    nj = d // tg

    def lo(j):
        return jnp.minimum(j, nj - 1)

    def gate_bwd_body(dx_ref, wo_ref, ga_ref, gb_ref, ya_ref, yb_ref, dya_ref, dyb_ref, dz_ref, keep):
        j = pl.program_id(1)

        @pl.when(j < nj)
        def _():
            dm = lax.dot_general(dx_ref[...], wo_ref[...], NT, preferred_element_type=F32)
            sa, sb = _sigmoid(ga_ref[...].astype(F32)), _sigmoid(gb_ref[...].astype(F32))
            dya_ref[...] = (dm * sa).astype(BF16)
            dyb_ref[...] = (dm * sb).astype(BF16)
            dz_ref[...] = (dm * ya_ref[...].astype(F32) * (sa * (1.0 - sa))).astype(BF16)
            keep[lo(j)] = (dm * yb_ref[...].astype(F32) * (sb * (1.0 - sb))).astype(BF16)

        @pl.when(j >= nj)
        def _():
            dz_ref[...] = keep[jnp.maximum(j - nj, 0)]

    t_lo = pl.BlockSpec((tm, tg), lambda i, j: (i, lo(j)))
    dya, dyb, dz = _pallas(
        gate_bwd_body,
        name="mm_dgate",
        grid=(s // tm, 2 * nj),
        in_specs=[
            pl.BlockSpec((tm, d), lambda i, j: (i, 0)),
            pl.BlockSpec((tg, d), lambda i, j: (lo(j), 0)),
            pl.BlockSpec((tm, tg), lambda i, j: (i, ga0 + lo(j))),
            pl.BlockSpec((tm, tg), lambda i, j: (i, gb0 + lo(j))),
            t_lo,
            t_lo,
        ],
        out_specs=[t_lo, t_lo, pl.BlockSpec((tm, tg), lambda i, j: (i, ga0 + j))],
        out_shape=[jax.ShapeDtypeStruct((s, d), BF16), jax.ShapeDtypeStruct((s, d), BF16), jax.ShapeDtypeStruct((s, n_in), BF16)],
        scratch_shapes=[pltpu.VMEM((nj, tm, tg), BF16)],
        compiler_params=_params(_mm_vmem([((tm, d), BF16, 2), ((tg, d), BF16, 2), ((tm, tg), F32, 14), ((nj, tm, tg), BF16, 1)])),
    )(dx2b, g_o, z, z, y_a, y_b)
    reduce_b(red_ffn, dya)
    reduce_c(red_down, red_ffn["token"])

    gw_o = _mm(
        "mm_gw_o", (d // twr, d // twn, 1), [m_act, dx2b],
        [pl.BlockSpec((s, twr), lambda i, j, k: (0, i)), pl.BlockSpec((s, twn), lambda i, j, k: (0, j))],
        [jax.ShapeDtypeStruct((d, d), BF16)], [pl.BlockSpec((twr, twn), lambda i, j, k: (i, j))],
        [(0, 1, TN, 0)], 1, (twr, twn), 1, ep_store,
        _mm_vmem([((s, twr), BF16, 3), ((s, twn), BF16, 2), ((twr, twn), F32, 3)]),
        after=(red_down["token"],),
    )[0]
    after_down = finish(red_down, gw_o)

    tb = _tile(w_sgu, (1024, 512))
    b_out = pl.BlockSpec((tm, tb), lambda i, j, k: (i, j))

    da, datt = _mm(
        "mm_dbranches", (s // tm, w_sgu // tb, 1), [dya, g_a, dyb, g_b],
        [pl.BlockSpec((tm, d), lambda i, j, k: (i, 0)), pl.BlockSpec((tb, d), lambda i, j, k: (j, 0)),
         pl.BlockSpec((tm, d), lambda i, j, k: (i, 0)), pl.BlockSpec((tb, d), lambda i, j, k: (j, 0))],
        [jax.ShapeDtypeStruct((s, w_sgu), BF16), jax.ShapeDtypeStruct((s, w_att), BF16)], [b_out, b_out],
        [(0, 1, NT, 0), (2, 3, NT, 1)], 2, (tm, tb), 1, ep_store,
        _mm_vmem([((tm, d), BF16, 4), ((tb, d), BF16, 4), ((tm, tb), F32, 6)]),
        after=(after_down,),
    )

    wb_tile = pl.BlockSpec((tb, twn), lambda i, j, k: (i, j))
    gw_a, gw_b = _mm(
        "mm_gw_branches", (w_sgu // tb, d // twn, 1), [a_act, dya, att, dyb],
        [pl.BlockSpec((s, tb), lambda i, j, k: (0, i)), pl.BlockSpec((s, twn), lambda i, j, k: (0, j)),
         pl.BlockSpec((s, tb), lambda i, j, k: (0, i)), pl.BlockSpec((s, twn), lambda i, j, k: (0, j))],
        [jax.ShapeDtypeStruct((w_sgu, d), BF16), jax.ShapeDtypeStruct((w_att, d), BF16)], [wb_tile, wb_tile],
        [(0, 1, TN, 0), (2, 3, TN, 1)], 2, (tb, twn), 1, ep_store,
        _mm_vmem([((s, tb), BF16, 5), ((s, twn), BF16, 4), ((tb, twn), F32, 6)]),
        after=(da,),
    )
    red_mix = reduce_a("mix", [W_O, W_A, W_B], [gw_o, gw_a, gw_b])

    dz, dws, dbs, dgain = _sgu_bwd(z, da, sgu_v_gain, ws_b, wst_b, b_col, w_sgu, dz, after=(red_mix["token"],))
    reduce_b(red_mix, dgain)
    dz, dk_pad, dv_pad, dbias_tab, dsink = _attn_bwd(sink, z, k_pad, v_pad, bias_tab, datt, dz, grp, q_blk0)
    dz = _dkv_to_dz(dk_pad, dv_pad, dz, off_k // (2 * w_kv))
    drel = _relbias_bwd(dbias_tab, bucket)
    reduce_c(red_ffn, dz)

    small_w = [norm_mix, sgu_v_gain, sgu_w_s, sgu_b_s, attn_sink, rel_bias, norm_ffn, norm_final]
    small_m = [m_norm_mix, m_sgu_v_gain, m_sgu_w_s, m_sgu_b_s, m_attn_sink, m_rel_bias, m_norm_ffn, m_norm_final]
    small_v = [v_norm_mix, v_sgu_v_gain, v_sgu_w_s, v_sgu_b_s, v_attn_sink, v_rel_bias, v_norm_ffn, v_norm_final]
    small_shapes = [w.shape for w in small_w]
    early = [dgain, dws, dbs, dsink[:, 0], drel[:, :REL_BUCKETS].T, dg_ffn, dg_final]
    p_early = _pack([g.reshape(shp) for g, shp in zip(early, small_shapes[1:])] + [loss_part[0, :1]])
    land = jnp.zeros((2 * N_CHIPS,) + p_early.shape, F32)
    sm_send, sm_recv, (p_early, land) = _split_start("small_send", [p_early, land], 2 * N_CHIPS - 1, _small_exchange_copies(False))

    tzn = _tile(n_in, (768, 640, 512))
    gw_in = _mm(
        "mm_gw_in", (d // twr, n_in // tzn, 1), [h1, dz],
        [pl.BlockSpec((s, twr), lambda i, j, k: (0, i)), pl.BlockSpec((s, tzn), lambda i, j, k: (0, j))],
        [jax.ShapeDtypeStruct((d, n_in), BF16)], [pl.BlockSpec((twr, tzn), lambda i, j, k: (i, j))],
        [(0, 1, TN, 0)], 1, (twr, tzn), 1, ep_store,
        _mm_vmem([((s, twr), BF16, 3), ((s, tzn), BF16, 2), ((twr, tzn), F32, 3)]),
        after=(red_ffn["token"], p_early),
    )[0]
    red_in = reduce_a("w_in", [W_IN], [gw_in])

    reduce_c(red_mix, red_in["token"])
    reduce_b(red_in, finish(red_mix, red_in["token"]))

    dh1 = _mm(
        "mm_dh1", (s // tm, d // tn2, 1), [dz, g_in],
        [pl.BlockSpec((tm, n_in), lambda i, j, k: (i, 0)), pl.BlockSpec((tn2, n_in), lambda i, j, k: (j, 0))],
        [jax.ShapeDtypeStruct((s, d), F32)], [pl.BlockSpec((tm, tn2), lambda i, j, k: (i, j))],
        [(0, 1, NT, 0)], 1, (tm, tn2), 1, ep_store,
        _mm_vmem([((tm, n_in), BF16, 2), ((tn2, n_in), BF16, 2), ((tm, tn2), F32, 5)]),
        after=(red_in["token"],),
    )[0]

    grad_x, _, dg_mix = _rms_bwd("rms_mix_bwd", x2d, norm_mix, dh1, dx2)

    p_mix = _pack([dg_mix.reshape(small_shapes[0])])
    land_mix = jnp.zeros((2 * N_CHIPS,) + p_mix.shape, F32)
    mx_send, mx_recv, (p_mix, land_mix) = _split_start("mix_send", [p_mix, land_mix], 2 * N_CHIPS - 1, _small_exchange_copies(False))

    reduce_c(red_in, finish(red_ffn, p_mix))
    p_early, land = _split_wait("small_wait", [p_early, land], sm_send, sm_recv, _small_exchange_copies(True), red_in["token"])
    p_mix, land_mix = _split_wait("mix_wait", [p_mix, land_mix], mx_send, mx_recv, _small_exchange_copies(True), p_early)
    me_idx = 2 * q_idx + c_idx
    packed_g = jnp.concatenate([_small_sum("mix_sum", me_idx, p_mix, land_mix), _small_sum("small_sum", me_idx, p_early, land)], axis=0)
    g_small = _unpack(packed_g, small_shapes + [(1,)])
    loss = g_small[-1].reshape(())
    g_small = g_small[:-1]
    zero1 = jnp.zeros((1,), F32)
    pw, pg, pm, pv = _pack(small_w + [zero1]), _pack(g_small + [zero1]), _pack(small_m + [zero1]), _pack(small_v + [zero1])
    small_upd = _adamw("adamw_small", pw, pg, pm, pv)
    d_small, nm_small, nv_small = [_unpack(a, small_shapes) for a in small_upd[:3]]
    finish(red_in, small_upd[0])

    small_names = ["norm_mix", "sgu_v_gain", "sgu_w_s", "sgu_b_s", "attn_sink", "rel_bias", "norm_ffn", "norm_final"]
    table = {}
    for i, n in enumerate(names):
        table[n] = (grads_big[i][None], upd[i][0][None], upd[i][1][None], upd[i][2][None])
    for i, n in enumerate(small_names):
        table[n] = (g_small[i], d_small[i], nm_small[i], nv_small[i])
    order = ["w_in", "norm_mix", "sgu_v_gain", "sgu_w_s", "sgu_b_s", "w_a", "attn_sink", "rel_bias", "w_b", "w_o", "norm_ffn",
             "w_gate", "w_up", "w_down", "norm_final"]
    outs = [loss, grad_x.reshape(1, s, d)]
    for part in range(4):
        outs += [table[n][part] for n in order]
    return tuple(outs)
```

```python
import math

import jax
import jax.numpy as jnp
import numpy as np
from jax import lax
from jax.experimental import pallas as pl
from jax.experimental.pallas import tpu as pltpu

F32 = jnp.float32
BF16 = jnp.bfloat16
I32 = jnp.int32
MESH = pl.DeviceIdType.MESH

EPS = 1e-6
NEG = -1e30
BLK = 128
HEAD_DIM = 128
N_KV_HEADS = 2
REL_BUCKETS = 32
REL_MAX_DIST = 128
N_CHIPS = 4
ADAM_LR, ADAM_B1, ADAM_B2, ADAM_EPS, ADAM_WD, ADAM_STEP = 0.001, 0.9, 0.999, 1e-08, 0.01, 10

LANES = 128
VMEM_CAP = 60 * 1024 * 1024

NN = (((1,), (0,)), ((), ()))
NT = (((1,), (1,)), ((), ()))
TN = (((0,), (0,)), ((), ()))
ANY = pl.BlockSpec(memory_space=pl.ANY)
HBM_SPEC = pl.BlockSpec(memory_space=pltpu.HBM)
SEM_SPEC = pl.BlockSpec(memory_space=pltpu.SEMAPHORE)
EFFECT = pltpu.SideEffectType.DATAFLOW_SIDE_EFFECTING


def _tile(n, cands):
    for t in cands:
        if n % t == 0:
            return t
    return n


PIN_BYTES = 64 * 1024


def _pin_hbm(a):
    big = hasattr(a, "dtype") and jnp.issubdtype(a.dtype, jnp.floating) and _nbytes(a.shape, a.dtype) >= PIN_BYTES
    return pltpu.with_memory_space_constraint(a, pltpu.HBM) if big else a


def _pallas(body, *, out_shape, **kw):
    def pin(o):
        big = isinstance(o, jax.ShapeDtypeStruct) and jnp.issubdtype(o.dtype, jnp.floating) and _nbytes(o.shape, o.dtype) >= PIN_BYTES
        return pltpu.HBM(o.shape, o.dtype) if big else o

    shapes = type(out_shape)(pin(o) for o in out_shape) if isinstance(out_shape, (list, tuple)) else pin(out_shape)
    call = pl.pallas_call(body, out_shape=shapes, **kw)
    return lambda *args: call(*[_pin_hbm(a) for a in args])


def _params(vmem_bytes=None, **kw):
    if vmem_bytes is not None:
        kw["vmem_limit_bytes"] = int(min(max(vmem_bytes, 32 * 1024 * 1024), VMEM_CAP))
    return pltpu.CompilerParams(**kw)


def _nbytes(shape, dtype):
    return int(np.prod(shape)) * jnp.dtype(dtype).itemsize


def _sigmoid(x):
    return 1.0 / (1.0 + jnp.exp(-x))


_GC = 0.7978845608028654
_GA = 0.044715


def _gelu(x):
    return 0.5 * x * (1.0 + jnp.tanh(_GC * (x + _GA * (x * x * x))))


def _gelu_grad(x):
    t = jnp.tanh(_GC * (x + _GA * (x * x * x)))
    return 0.5 * (1.0 + t) + 0.5 * x * (1.0 - t * t) * (_GC * (1.0 + 3.0 * _GA * (x * x)))


def _bf(v):
    return v if v.dtype == BF16 else v.astype(BF16)


def _mm(name, grid, ins, in_specs, out_shape, out_specs, pairs, n_acc, tile, nk, epilogue, vmem_bytes, after=()):
    assert nk == 1
    n_in, n_out = len(ins) + len(after), len(out_shape)

    def body(*refs):
        in_refs, out_refs = refs[:n_in], refs[n_in : n_in + n_out]
        vals = [None] * n_acc
        for a_i, b_i, dn, acc_i in pairs:
            d = lax.dot_general(_bf(in_refs[a_i][...]), _bf(in_refs[b_i][...]), dn, preferred_element_type=F32)
            vals[acc_i] = d if vals[acc_i] is None else vals[acc_i] + d
        epilogue(in_refs, vals, out_refs, slice(None))

    return _pallas(
        body,
        name=name,
        grid=grid,
        in_specs=list(in_specs) + [ANY] * len(after),
        out_specs=out_specs,
        out_shape=out_shape,
        compiler_params=_params(vmem_bytes),
    )(*ins, *after)


def _put(ref, cs, v):
    ref[:, cs] = v.astype(ref.dtype)


def _mm_vmem(tiles):
    return sum(_nbytes(s, d) * c for s, d, c in tiles) + 4 * 1024 * 1024


def _rows8(v):
    r, d = v.shape
    return v.reshape(r // 8, 8, d).sum(axis=0)


def _rms_fwd(name, x, g, after=()):
    s, d = x.shape
    tm = _tile(s, (256, 128))

    def body(x_ref, g_ref, *rest):
        h_ref = rest[-1]
        xv = x_ref[...]
        r = lax.rsqrt(jnp.mean(xv * xv, axis=-1, keepdims=True) + EPS)
        h_ref[...] = ((xv * r) * g_ref[...]).astype(BF16)

    return _pallas(
        body,
        name=name,
        grid=(s // tm,),
        in_specs=[pl.BlockSpec((tm, d), lambda i: (i, 0)), pl.BlockSpec((1, d), lambda i: (0, 0))] + [ANY] * len(after),
        out_specs=pl.BlockSpec((tm, d), lambda i: (i, 0)),
        out_shape=jax.ShapeDtypeStruct((s, d), BF16),
    )(x, g, *after)


def _rms_bwd(name, x, g, dh, dres, after=()):
    s, d = x.shape
    tm = _tile(s, (256, 128))
    n = s // tm
    n_after = len(after)

    def body(x_ref, g_ref, dh_ref, dres_ref, *rest):
        dx_ref, dxb_ref, dg_ref, acc_ref = rest[n_after:]
        i = pl.program_id(0)
        xv = x_ref[...]
        r = lax.rsqrt(jnp.mean(xv * xv, axis=-1, keepdims=True) + EPS)
        xh = xv * r
        dhv = dh_ref[...]
        dxh = dhv * g_ref[...]
        dx = r * (dxh - xh * jnp.mean(dxh * xh, axis=-1, keepdims=True)) + dres_ref[...]
        dx_ref[...] = dx
        dxb_ref[...] = dx.astype(BF16)
        part = _rows8(dhv * xh)

        @pl.when(i == 0)
        def _():
            acc_ref[...] = part

        @pl.when(i > 0)
        def _():
            acc_ref[...] += part

        @pl.when(i == n - 1)
        def _():
            dg_ref[...] = jnp.sum(acc_ref[...], axis=0, keepdims=True)

    row = pl.BlockSpec((tm, d), lambda i: (i, 0))
    vec = pl.BlockSpec((1, d), lambda i: (0, 0))
    return _pallas(
        body,
        name=name,
        grid=(n,),
        in_specs=[row, vec, row, row] + [ANY] * n_after,
        out_specs=[row, row, vec],
        out_shape=[jax.ShapeDtypeStruct((s, d), F32), jax.ShapeDtypeStruct((s, d), BF16), jax.ShapeDtypeStruct((1, d), F32)],
        scratch_shapes=[pltpu.VMEM((8, d), F32)],
    )(x, g, dh, dres, *after)


def _head(x3, g, target):
    s, d = x3.shape
    tm = _tile(s, (256, 128))
    n = s // tm

    def body(x_ref, g_ref, t_ref, dx_ref, dxb_ref, dg_ref, loss_ref, acc_g, acc_l):
        i = pl.program_id(0)
        xv = x_ref[...]
        gv = g_ref[...]
        r = lax.rsqrt(jnp.mean(xv * xv, axis=-1, keepdims=True) + EPS)
        xh = xv * r
        e = xh * gv - t_ref[...]
        dy = e * (1.0 / d)
        dxh = dy * gv
        dx = r * (dxh - xh * jnp.mean(dxh * xh, axis=-1, keepdims=True))
        dx_ref[...] = dx
        dxb_ref[...] = dx.astype(BF16)
        pg = _rows8(dy * xh)
        plo = _rows8(e * e)

        @pl.when(i == 0)
        def _():
            acc_g[...] = pg
            acc_l[...] = plo

        @pl.when(i > 0)
        def _():
            acc_g[...] += pg
            acc_l[...] += plo

        @pl.when(i == n - 1)
        def _():
            dg_ref[...] = jnp.sum(acc_g[...], axis=0, keepdims=True)
            loss_ref[...] = jnp.full((1, LANES), (0.5 / d) * jnp.sum(acc_l[...]), F32)

    row = pl.BlockSpec((tm, d), lambda i: (i, 0))
    vec = pl.BlockSpec((1, d), lambda i: (0, 0))
    return _pallas(
        body,
        name="head",
        grid=(n,),
        in_specs=[row, vec, row],
        out_specs=[row, row, vec, pl.BlockSpec((1, LANES), lambda i: (0, 0))],
        out_shape=[
            jax.ShapeDtypeStruct((s, d), F32),
            jax.ShapeDtypeStruct((s, d), BF16),
            jax.ShapeDtypeStruct((1, d), F32),
            jax.ShapeDtypeStruct((1, LANES), F32),
        ],
        scratch_shapes=[pltpu.VMEM((8, d), F32), pltpu.VMEM((8, d), F32)],
    )(x3, g, target)


def _sgu_fwd(z, gain, ws_b, b_col, w_sgu, after=()):
    s = z.shape[0]
    groups = ws_b.shape[0]

    def body(zu_ref, zv_ref, gain_ref, ws_ref, b_ref, *rest):
        a_ref = rest[-1]
        vv = _gelu(zv_ref[...].astype(F32))
        r = lax.rsqrt(jnp.mean(vv * vv, axis=-1, keepdims=True) + EPS)
        vn = ((vv * r) * gain_ref[...]).astype(BF16)
        u = _gelu(zu_ref[...].astype(F32))
        for g in range(groups):
            sl = slice(g * BLK, (g + 1) * BLK)
            mixed = jnp.dot(ws_ref[g], vn[:, sl], preferred_element_type=F32) + b_ref[g]
            a_ref[:, sl] = (u[:, sl] * mixed).astype(BF16)

    return _pallas(
        body,
        name="sgu_fwd",
        grid=(s // BLK,),
        in_specs=[
            pl.BlockSpec((BLK, w_sgu), lambda c: (c, 0)),
            pl.BlockSpec((BLK, w_sgu), lambda c: (c, 1)),
            pl.BlockSpec((1, w_sgu), lambda c: (0, 0)),
            pl.BlockSpec((groups, BLK, BLK), lambda c: (0, 0, 0)),
            pl.BlockSpec((groups, BLK, 1), lambda c: (0, 0, 0)),
        ]
        + [ANY] * len(after),
        out_specs=pl.BlockSpec((BLK, w_sgu), lambda c: (c, 0)),
        out_shape=jax.ShapeDtypeStruct((s, w_sgu), BF16),
    )(z, z, gain, ws_b, b_col, *after)


def _sgu_bwd(z, da, gain, ws_b, wst_b, b_col, w_sgu, dz, after=()):
    s = z.shape[0]
    groups = ws_b.shape[0]
    n = s // BLK
    n_skip = 1 + len(after)

    def body(zu_ref, zv_ref, da_ref, gain_ref, ws_ref, wst_ref, b_ref, *rest):
        dz_ref, dws_ref, dbs_ref, dgain_ref, acc_gain, vv_s, gv_s, dxh_s = rest[n_skip:]
        c = pl.program_id(0)
        cols = [slice(g * BLK, (g + 1) * BLK) for g in range(groups)]

        ss = jnp.zeros((BLK, 1), F32)
        for sl in cols:
            zv = zv_ref[:, sl].astype(F32)
            vv = _gelu(zv)
            vv_s[:, sl] = vv
            gv_s[:, sl] = _gelu_grad(zv)
            ss = ss + jnp.sum(vv * vv, axis=-1, keepdims=True)
        r = lax.rsqrt(ss * (1.0 / w_sgu) + EPS)

        dot_dx = jnp.zeros((BLK, 1), F32)
        for g, sl in enumerate(cols):
            gain_g = gain_ref[:, sl]
            xh = vv_s[:, sl] * r
            vn = (xh * gain_g).astype(BF16)
            zu = zu_ref[:, sl].astype(F32)
            dav = da_ref[:, sl].astype(F32)
            dmix = dav * _gelu(zu)
            dmix_b = dmix.astype(BF16)
            mixed = jnp.dot(ws_ref[g], vn, preferred_element_type=F32) + b_ref[g]
            dz_ref[:, sl] = (dav * mixed * _gelu_grad(zu)).astype(BF16)
            dvn = jnp.dot(wst_ref[g], dmix_b, preferred_element_type=F32)
            dws_g = lax.dot_general(dmix_b, vn, NT, preferred_element_type=F32)
            dbs_g = jnp.sum(dmix, axis=1, keepdims=True)
            pg = _rows8(dvn * xh)

            @pl.when(c == 0)
            def _():
                dws_ref[g] = dws_g
                dbs_ref[g] = dbs_g
                acc_gain[:, sl] = pg

            @pl.when(c > 0)
            def _():
                dws_ref[g] += dws_g
                dbs_ref[g] += dbs_g
                acc_gain[:, sl] += pg

            dxh = dvn * gain_g
            dxh_s[:, sl] = dxh
            dot_dx = dot_dx + jnp.sum(dxh * xh, axis=-1, keepdims=True)

        mean_dx = dot_dx * (1.0 / w_sgu)
        for g, sl in enumerate(cols):
            dvv = r * (dxh_s[:, sl] - (vv_s[:, sl] * r) * mean_dx)
            dz_ref[:, w_sgu + g * BLK : w_sgu + (g + 1) * BLK] = (dvv * gv_s[:, sl]).astype(BF16)

        @pl.when(c == n - 1)
        def _():
            dgain_ref[...] = jnp.sum(acc_gain[...], axis=0, keepdims=True)

    full3 = pl.BlockSpec((groups, BLK, BLK), lambda c: (0, 0, 0))
    col3 = pl.BlockSpec((groups, BLK, 1), lambda c: (0, 0, 0))
    vec = pl.BlockSpec((1, w_sgu), lambda c: (0, 0))
    return _pallas(
        body,
        name="sgu_bwd",
        grid=(n,),
        in_specs=[
            pl.BlockSpec((BLK, w_sgu), lambda c: (c, 0)),
            pl.BlockSpec((BLK, w_sgu), lambda c: (c, 1)),
            pl.BlockSpec((BLK, w_sgu), lambda c: (c, 0)),
            vec,
            full3,
            full3,
            col3,
            ANY,
        ]
        + [ANY] * len(after),
        out_specs=[pl.BlockSpec((BLK, 2 * w_sgu), lambda c: (c, 0)), full3, col3, vec],
        out_shape=[
            jax.ShapeDtypeStruct(dz.shape, BF16),
            jax.ShapeDtypeStruct((groups, BLK, BLK), F32),
            jax.ShapeDtypeStruct((groups, BLK, 1), F32),
            jax.ShapeDtypeStruct((1, w_sgu), F32),
        ],
        scratch_shapes=[pltpu.VMEM((8, w_sgu), F32)] + [pltpu.VMEM((BLK, w_sgu), F32)] * 3,
        input_output_aliases={7: 0},
    )(z, z, da, gain, ws_b, wst_b, b_col, dz, *after)


def _attn_softmax(sink_ref, q_ref, k_ref, v_ref, bias_ref, s_len, grp):
    kv = pl.program_id(0)
    n = pl.program_id(1)
    start = pl.multiple_of(n * BLK, BLK)
    kb = k_ref[pl.ds(start, 3 * BLK), :]
    vb = v_ref[pl.ds(start, 3 * BLK), :]
    qv = q_ref[...]
    qs = jnp.concatenate([qv[:, g * HEAD_DIM : (g + 1) * HEAD_DIM] for g in range(grp)], axis=0).astype(BF16)
    sc = lax.dot_general(qs, kb, NT, preferred_element_type=F32) * (HEAD_DIM**-0.5)
    sc = sc + bias_ref[...].reshape(grp * BLK, 3 * BLK)
    kpos = start + lax.broadcasted_iota(I32, (1, 3 * BLK), 1) - BLK
    sc = jnp.where((kpos >= 0) & (kpos < s_len), sc, NEG)
    sink = jnp.concatenate([jnp.full((BLK, 1), sink_ref[kv * grp + g], F32) for g in range(grp)], axis=0)
    m = jnp.maximum(jnp.max(sc, axis=-1, keepdims=True), sink)
    p = jnp.exp(sc - m)
    esink = jnp.exp(sink - m)
    den = jnp.sum(p, axis=-1, keepdims=True) + esink
    return start, qs, kb, vb, p / den, esink / den


def _attn_specs(s, grp, q_blk0):
    qw = grp * HEAD_DIM
    return [
        pl.BlockSpec(memory_space=pltpu.SMEM),
        pl.BlockSpec((BLK, qw), lambda kv, n: (n, q_blk0 + kv)),
        pl.BlockSpec((s + 2 * BLK, HEAD_DIM), lambda kv, n: (0, kv)),
        pl.BlockSpec((s + 2 * BLK, HEAD_DIM), lambda kv, n: (0, kv)),
        pl.BlockSpec((grp, BLK, 3 * BLK), lambda kv, n: (kv, 0, 0)),
    ]


def _attn_fwd(sink, z, k_pad, v_pad, bias_tab, grp, q_blk0):
    s = z.shape[0]
    qw = grp * HEAD_DIM

    def body(sink_ref, q_ref, k_ref, v_ref, bias_ref, o_ref):
        _, _, _, vb, pn, _ = _attn_softmax(sink_ref, q_ref, k_ref, v_ref, bias_ref, s, grp)
        o = jnp.dot(pn.astype(BF16), vb, preferred_element_type=F32)
        for g in range(grp):
            o_ref[:, g * HEAD_DIM : (g + 1) * HEAD_DIM] = o[g * BLK : (g + 1) * BLK].astype(BF16)

    return _pallas(
        body,
        name="attn_fwd",
        grid=(N_KV_HEADS, s // BLK),
        in_specs=_attn_specs(s, grp, q_blk0),
        out_specs=pl.BlockSpec((BLK, qw), lambda kv, n: (n, kv)),
        out_shape=jax.ShapeDtypeStruct((s, N_KV_HEADS * qw), BF16),
    )(sink, z, k_pad, v_pad, bias_tab)


def _attn_bwd(sink, z, k_pad, v_pad, bias_tab, dout, dz, grp, q_blk0):
    s = z.shape[0]
    qw = grp * HEAD_DIM
    nb = s // BLK
    heads = N_KV_HEADS * grp

    def body(sink_ref, q_ref, k_ref, v_ref, bias_ref, do_ref, dz_in, dq_ref, dk_ref, dv_ref, dbias_ref, dsink_ref, dk_acc, dv_acc):
        del dz_in
        kv = pl.program_id(0)
        n = pl.program_id(1)
        start, qs, kb, vb, pn, psink = _attn_softmax(sink_ref, q_ref, k_ref, v_ref, bias_ref, s, grp)
        dov = do_ref[...]
        dos = jnp.concatenate([dov[:, g * HEAD_DIM : (g + 1) * HEAD_DIM] for g in range(grp)], axis=0)
        dp = lax.dot_general(dos, vb, NT, preferred_element_type=F32)
        dvb = lax.dot_general(pn.astype(BF16), dos, TN, preferred_element_type=F32)
        delta = jnp.sum(pn * dp, axis=-1, keepdims=True)
        ds = pn * (dp - delta)
        dsb = (ds * (HEAD_DIM**-0.5)).astype(BF16)
        dq = jnp.dot(dsb, kb, preferred_element_type=F32)
        dkb = lax.dot_general(dsb, qs, TN, preferred_element_type=F32)
        for g in range(grp):
            dq_ref[:, g * HEAD_DIM : (g + 1) * HEAD_DIM] = dq[g * BLK : (g + 1) * BLK].astype(BF16)

        @pl.when(n == 0)
        def _():
            dk_acc[...] = jnp.zeros_like(dk_acc)
            dv_acc[...] = jnp.zeros_like(dv_acc)
            dbias_ref[...] = jnp.zeros_like(dbias_ref)

        @pl.when((n == 0) & (kv == 0))
        def _():
            dsink_ref[...] = jnp.zeros_like(dsink_ref)

        dk_acc[pl.ds(start, 3 * BLK), :] += dkb
        dv_acc[pl.ds(start, 3 * BLK), :] += dvb
        dbias_ref[...] += ds.reshape(grp, BLK, 3 * BLK)
        row = lax.broadcasted_iota(I32, (heads, LANES), 0)
        sd = psink * delta
        upd = jnp.zeros((heads, LANES), F32)
        for g in range(grp):
            upd = jnp.where(row == kv * grp + g, -jnp.sum(sd[g * BLK : (g + 1) * BLK]), upd)
        dsink_ref[...] += upd

        @pl.when(n == nb - 1)
        def _():
            dk_ref[...] = dk_acc[...]
            dv_ref[...] = dv_acc[...]

    pad_spec = pl.BlockSpec((s + 2 * BLK, HEAD_DIM), lambda kv, n: (0, kv))
    kvw = N_KV_HEADS * HEAD_DIM
    return _pallas(
        body,
        name="attn_bwd",
        grid=(N_KV_HEADS, nb),
        in_specs=_attn_specs(s, grp, q_blk0) + [pl.BlockSpec((BLK, qw), lambda kv, n: (n, kv)), ANY],
        out_specs=[
            pl.BlockSpec((BLK, qw), lambda kv, n: (n, q_blk0 + kv)),
            pad_spec,
            pad_spec,
            pl.BlockSpec((grp, BLK, 3 * BLK), lambda kv, n: (kv, 0, 0)),
            pl.BlockSpec((heads, LANES), lambda kv, n: (0, 0)),
        ],
        out_shape=[
            jax.ShapeDtypeStruct(dz.shape, BF16),
            jax.ShapeDtypeStruct((s + 2 * BLK, kvw), F32),
            jax.ShapeDtypeStruct((s + 2 * BLK, kvw), F32),
            jax.ShapeDtypeStruct((heads, BLK, 3 * BLK), F32),
            jax.ShapeDtypeStruct((heads, LANES), F32),
        ],
        scratch_shapes=[pltpu.VMEM((s + 2 * BLK, HEAD_DIM), F32), pltpu.VMEM((s + 2 * BLK, HEAD_DIM), F32)],
        input_output_aliases={6: 0},
    )(sink, z, k_pad, v_pad, bias_tab, dout, dz)


def _dkv_to_dz(dk_pad, dv_pad, dz, blk_idx):
    s = dz.shape[0]
    kvw = dk_pad.shape[1]

    def body(dk_ref, dv_ref, dz_in, out_ref):
        del dz_in
        out_ref[:, :kvw] = dk_ref[...].astype(BF16)
        out_ref[:, kvw:] = dv_ref[...].astype(BF16)

    src = pl.BlockSpec((BLK, kvw), lambda i: (i + 1, 0))
    return _pallas(
        body,
        name="dkv_to_dz",
        grid=(s // BLK,),
        in_specs=[src, src, ANY],
        out_specs=pl.BlockSpec((BLK, 2 * kvw), lambda i: (i, blk_idx)),
        out_shape=jax.ShapeDtypeStruct(dz.shape, BF16),
        input_output_aliases={2: 0},
    )(dk_pad, dv_pad, dz)


def _relbias_bwd(dbias_tab, bucket):
    heads = dbias_tab.shape[0]

    def body(dt_ref, bk_ref, out_ref):
        lane = lax.broadcasted_iota(I32, (1, LANES), 1)
        bk = bk_ref[...]
        rows = []
        for h in range(heads):
            dt = dt_ref[h]
            acc = jnp.zeros((1, LANES), F32)
            for b in range(REL_BUCKETS):
                acc = jnp.where(lane == b, jnp.sum(jnp.where(bk == b, dt, 0.0)), acc)
            rows.append(acc)
        out_ref[...] = jnp.concatenate(rows, axis=0)

    return _pallas(body, name="relbias_bwd", out_shape=jax.ShapeDtypeStruct((heads, LANES), F32))(dbias_tab, bucket)


def _t5_bucket(rel):
    nb = REL_BUCKETS // 2
    ret = jnp.where(rel > 0, nb, 0)
    n = jnp.abs(rel)
    max_exact = nb // 2
    nf = jnp.maximum(n, 1).astype(F32)
    large = max_exact + (jnp.log(nf / max_exact) / math.log(REL_MAX_DIST / max_exact) * (nb - max_exact)).astype(I32)
    large = jnp.minimum(large, nb - 1)
    return ret + jnp.where(n < max_exact, n, large)


def _band_tables(rel_bias):
    qi = jnp.arange(BLK)[:, None]
    kj = jnp.arange(3 * BLK)[None, :]
    rel = kj - BLK - qi
    bucket = _t5_bucket(rel).astype(I32)
    heads = rel_bias.shape[1]
    masked = jnp.where(jnp.abs(rel) <= BLK, bucket, -1)

    def body(rb_ref, bk_ref, out_ref):
        bk = bk_ref[...]
        for h in range(heads):
            tab = jnp.full(bk.shape, NEG, F32)
            for b in range(REL_BUCKETS):
                tab = jnp.where(bk == b, rb_ref[b, h], tab)
            out_ref[h] = tab

    bias_tab = _pallas(
        body,
        name="bias_table",
        in_specs=[pl.BlockSpec(memory_space=pltpu.SMEM), pl.BlockSpec(memory_space=pltpu.VMEM)],
        out_specs=pl.BlockSpec(memory_space=pltpu.VMEM),
        out_shape=jax.ShapeDtypeStruct((heads, BLK, 3 * BLK), F32),
    )(rel_bias.astype(F32), masked)
    return bias_tab, bucket


EW_BLOCK_ELEMS = 512 * 1024


def _ew_tiles(shape, elems=EW_BLOCK_ELEMS // 2):
    r, c = shape
    tn = c if c <= 2048 else _tile(c, (2048, 1920, 1536, 1408, 1024, 512))
    tm = _tile(r, [t for t in (1024, 512, 256, 128, 64, 32, 16, 8) if t * tn <= elems] or [8])
    return tm, tn


def _cast_into_full(name, qidx, w, kind, after=()):
    r, c = w.shape
    tm, tn = _ew_tiles(w.shape, EW_BLOCK_ELEMS)
    nbi, nbj = r // tm, c // tn
    if kind == "col":
        full, out_spec = (r, c * N_CHIPS), pl.BlockSpec((tm, tn), lambda i, j, q: (i, q[0] * nbj + j))
    else:
        full, out_spec = (r * N_CHIPS, c), pl.BlockSpec((tm, tn), lambda i, j, q: (q[0] * nbi + i, j))

    def body(q_ref, w_ref, *rest):
        del q_ref
        rest[-1][...] = w_ref[...].astype(BF16)

    return _pallas(
        body,
        name=name,
        grid_spec=pltpu.PrefetchScalarGridSpec(
            num_scalar_prefetch=1,
            grid=(nbi, nbj),
            in_specs=[pl.BlockSpec((tm, tn), lambda i, j, q: (i, j))] + [ANY] * len(after),
            out_specs=out_spec,
        ),
        out_shape=jax.ShapeDtypeStruct(full, BF16),
    )(qidx, w, *after)


def _adamw(name, w, g, m, v, after=()):
    tm, tn = _ew_tiles(w.shape, EW_BLOCK_ELEMS)
    if _nbytes(w.shape, F32) <= 1024 * 1024:
        tm, tn = w.shape
    spec = pl.BlockSpec((tm, tn), lambda i, j: (i, j))
    n_after = len(after)

    def body(w_ref, g_ref, m_ref, v_ref, *rest):
        d_ref, nm_ref, nv_ref, g_out_ref = rest[n_after:]
        gv = g_ref[...]
        g_out_ref[...] = gv
        nm = ADAM_B1 * m_ref[...] + (1.0 - ADAM_B1) * gv
        nv = ADAM_B2 * v_ref[...] + (1.0 - ADAM_B2) * (gv * gv)
        m_hat = nm / (1.0 - ADAM_B1**ADAM_STEP)
        v_hat = nv / (1.0 - ADAM_B2**ADAM_STEP)
        d_ref[...] = -ADAM_LR * (m_hat / (jnp.sqrt(v_hat) + ADAM_EPS) + ADAM_WD * w_ref[...])
        nm_ref[...] = nm
        nv_ref[...] = nv

    out = jax.ShapeDtypeStruct(w.shape, F32)
    return _pallas(
        body, name=name, grid=(w.shape[0] // tm, w.shape[1] // tn), in_specs=[spec] * 4 + [ANY] * n_after,
        out_specs=[spec] * 4, out_shape=[out, out, out, out],
        compiler_params=_params(_mm_vmem([((tm, tn), F32, 24)])),
    )(w, g, m, v, *after)


def _pair_add(name, cidx, g_full, r_sib, kind):
    hr, hc = r_sib.shape
    tm, tn = _ew_tiles((hr, hc), 2 * EW_BLOCK_ELEMS)
    nbi, nbj = hr // tm, hc // tn
    if kind == "col":
        g_spec = pl.BlockSpec((tm, tn), lambda i, j, c: (c[0] * nbi + i, j))
    else:
        g_spec = pl.BlockSpec((tm, tn), lambda i, j, c: (i, c[0] * nbj + j))
    spec = pl.BlockSpec((tm, tn), lambda i, j, c: (i, j))

    def body(c_ref, g_ref, r_ref, o_ref):
        del c_ref
        o_ref[...] = (g_ref[...].astype(F32) + r_ref[...].astype(F32)).astype(BF16)

    return _pallas(
        body,
        name=name,
        grid_spec=pltpu.PrefetchScalarGridSpec(num_scalar_prefetch=1, grid=(nbi, nbj), in_specs=[g_spec, spec], out_specs=spec),
        out_shape=jax.ShapeDtypeStruct((hr, hc), BF16),
        compiler_params=_params(_mm_vmem([((tm, tn), BF16, 6), ((tm, tn), F32, 3)])),
    )(cidx, g_full, r_sib)


def _chip_sum(name, qidx, c_half, r_ici, kind):
    _, pr, pc = r_ici.shape
    tm, tn = _ew_tiles((pr, pc), 2 * EW_BLOCK_ELEMS)
    nbi, nbj = pr // tm, pc // tn
    if kind == "col":
        own_spec = pl.BlockSpec((tm, tn), lambda i, j, q: (i, q[0] * nbj + j))
        full, out_spec = (2 * pr, pc), pl.BlockSpec((tm, tn), lambda i, j, q: (q[1] * nbi + i, j))
    else:
        own_spec = pl.BlockSpec((tm, tn), lambda i, j, q: (q[0] * nbi + i, j))
        full, out_spec = (pr, 2 * pc), pl.BlockSpec((tm, tn), lambda i, j, q: (i, q[1] * nbj + j))

    def body(q_ref, own_ref, r_ref, o_ref):
        q = q_ref[0]
        own = own_ref[...].astype(F32)
        recv = [r_ref[r].astype(F32) for r in range(3)]
        total = None
        for chip in range(N_CHIPS):
            d = chip ^ q
            term = jnp.where(d == 0, own, jnp.where(d == 2, recv[0], jnp.where(d == 1, recv[1], recv[2])))
            total = term if total is None else total + term
        o_ref[...] = total

    return _pallas(
        body,
        name=name,
        grid_spec=pltpu.PrefetchScalarGridSpec(
            num_scalar_prefetch=1,
            grid=(nbi, nbj),
            in_specs=[own_spec, pl.BlockSpec((3, tm, tn), lambda i, j, q: (0, i, j))],
            out_specs=out_spec,
        ),
        out_shape=jax.ShapeDtypeStruct(full, F32),
        compiler_params=_params(_mm_vmem([((tm, tn), BF16, 8), ((tm, tn), F32, 6)])),
    )(qidx, c_half, r_ici)


_REL_MASK = (2, 1, 3)


def _place():
    x, y, c = lax.axis_index("x"), lax.axis_index("y"), lax.axis_index("c")
    chips = [(1 - x, y), (x, 1 - y), (1 - x, 1 - y)]
    return x, y, c, 2 * x + y, chips


def _shard_view(ref, kind, chip):
    if kind == "col":
        w = ref.shape[1] // N_CHIPS
        return ref.at[:, pl.ds(pl.multiple_of(chip * w, LANES), w)]
    h = ref.shape[0] // N_CHIPS
    return ref.at[pl.ds(pl.multiple_of(chip * h, 16), h), :]


def _row_half(ref, half):
    h = ref.shape[0] // 2
    return ref.at[pl.ds(pl.multiple_of(half * h, 16), h), :]


def _pair_half(ref, kind, half):
    if kind == "col":
        return _row_half(ref, half)
    w = ref.shape[1] // 2
    return ref.at[:, pl.ds(pl.multiple_of(half * w, LANES), w)]


def _remote(src, dst, send_sem, recv_sem, dev):
    return pltpu.make_async_remote_copy(src_ref=src, dst_ref=dst, send_sem=send_sem, recv_sem=recv_sem, device_id=dev, device_id_type=MESH)


def _hbm(a):
    return pltpu.with_memory_space_constraint(a, pltpu.HBM)


def _gather_start(name, fulls, kinds, rels=(0, 1, 2), after=()):
    n_w = len(fulls)

    def body(*refs):
        g = refs[:n_w]
        send_sem, recv_sem = refs[n_w + len(after)], refs[n_w + len(after) + 1]
        token = refs[-1]
        _, _, c, q, chips = _place()
        for w in range(n_w):
            mine = _row_half(_shard_view(g[w], kinds[w], q), c)
            for r in rels if isinstance(rels, tuple) else rels[w]:
                _remote(mine, mine, send_sem.at[3 * w + r], recv_sem.at[3 * w + r], (*chips[r], c)).start()
        token[...] = jnp.zeros_like(token)

    res = _pallas(
        body,
        name=name,
        out_shape=(
            pltpu.SemaphoreType.DMA((3 * n_w,)),
            pltpu.SemaphoreType.DMA((3 * n_w,)),
            *[pltpu.HBM(f.shape, f.dtype) for f in fulls],
            jax.ShapeDtypeStruct((8, LANES), F32),
        ),
        in_specs=[HBM_SPEC] * n_w + [ANY] * len(after),
        out_specs=(SEM_SPEC, SEM_SPEC, *[HBM_SPEC] * n_w, pl.BlockSpec(memory_space=pltpu.VMEM)),
        input_output_aliases={w: w + 2 for w in range(n_w)},
        compiler_params=pltpu.CompilerParams(has_side_effects=EFFECT),
    )(*[_hbm(f) for f in fulls], *after)
    return res[0], res[1], list(res[2 : 2 + n_w]), res[-1]


def _relay_copies(kinds, waiting):
    def copies(refs, send_sem, recv_sem):
        _, _, c, q, chips = _place()
        out = []
        for i, kind in enumerate(kinds):
            for k, (src_rel, dst_rel) in enumerate(((0, 1), (1, 0))):
                held = _row_half(_row_half(_shard_view(refs[i], kind, q ^ _REL_MASK[src_rel]), c), k)
                far = _row_half(_row_half(_shard_view(refs[i], kind, q ^ _REL_MASK[2]), c), k)
                dst = far if waiting else held
                out.append(_remote(held, dst, send_sem.at[2 * i + k], recv_sem.at[2 * i + k], (*chips[dst_rel], c)))
        return out

    return copies


def _forward_copies(kinds, waiting):
    def copies(refs, send_sem, recv_sem):
        x, y, c, q, _ = _place()
        out = []
        for i, kind in enumerate(kinds):
            for r in range(3):
                quarter = _shard_view(refs[i], kind, q ^ _REL_MASK[r])
                landed = _row_half(quarter, c)
                dst = _row_half(quarter, 1 - c) if waiting else landed
                out.append(_remote(landed, dst, send_sem.at[3 * i + r], recv_sem.at[3 * i + r], (x, y, 1 - c)))
        return out

    return copies


def _gather_wait(name, fulls, kinds, w_ids, send_sem, recv_sem, after, rels=(0, 1, 2)):
    n = len(fulls)

    def body(*refs):
        g = refs[:n]
        s_sem, r_sem = refs[n], refs[n + 1]
        x, y, c, q, _ = _place()
        for i, w in enumerate(w_ids):
            mine = _row_half(_shard_view(g[i], kinds[i], q), c)
            for r in rels:
                landed = _row_half(_shard_view(g[i], kinds[i], q ^ _REL_MASK[r]), c)
                cp = _remote(mine, landed, s_sem.at[3 * w + r], r_sem.at[3 * w + r], (x, y, 1 - c))
                cp.wait_send()
                cp.wait_recv()

    res = _pallas(
        body,
        name=name,
        out_shape=[pltpu.HBM(f.shape, f.dtype) for f in fulls],
        in_specs=[HBM_SPEC] * n + [SEM_SPEC, SEM_SPEC, ANY],
        out_specs=[HBM_SPEC] * n,
        input_output_aliases={i: i for i in range(n)},
        compiler_params=pltpu.CompilerParams(has_side_effects=EFFECT),
    )(*fulls, send_sem, recv_sem, after)
    return list(res)


def _gather_forward(name, fulls, kinds):
    n = len(fulls)

    def body(*refs):
        g = refs[n : 2 * n]
        send, recv = refs[2 * n :]
        x, y, c, q, _ = _place()
        sib = (x, y, 1 - c)
        cps = []
        for i in range(n):
            for r in range(3):
                landed = _row_half(_shard_view(g[i], kinds[i], q ^ _REL_MASK[r]), c)
                cps.append(_remote(landed, landed, send.at[i, r], recv.at[i, r], sib))
        for cp in cps:
            cp.start()
        for i in range(n):
            for r in range(3):
                other = _row_half(_shard_view(g[i], kinds[i], q ^ _REL_MASK[r]), 1 - c)
                _remote(other, other, send.at[i, r], recv.at[i, r], sib).wait_recv()
        for cp in cps:
            cp.wait_send()

    res = _pallas(
        body,
        name=name,
        in_specs=[ANY] * n,
        out_specs=[ANY] * n,
        out_shape=[jax.ShapeDtypeStruct(f.shape, f.dtype) for f in fulls],
        scratch_shapes=[pltpu.SemaphoreType.DMA((n, 3)), pltpu.SemaphoreType.DMA((n, 3))],
        input_output_aliases={i: i for i in range(n)},
    )(*fulls)
    return list(res)


def _split_start(name, bufs, n_sems, copies):
    n = len(bufs)

    def body(*refs):
        for cp in copies(refs[:n], refs[n], refs[n + 1]):
            cp.start()

    res = _pallas(
        body,
        name=name,
        out_shape=(
            pltpu.SemaphoreType.DMA((n_sems,)),
            pltpu.SemaphoreType.DMA((n_sems,)),
            *[pltpu.HBM(b.shape, b.dtype) for b in bufs],
        ),
        in_specs=[HBM_SPEC] * n,
        out_specs=(SEM_SPEC, SEM_SPEC, *[HBM_SPEC] * n),
        input_output_aliases={i: i + 2 for i in range(n)},
        compiler_params=pltpu.CompilerParams(has_side_effects=EFFECT),
    )(*[_hbm(b) for b in bufs])
    return res[0], res[1], list(res[2:])


def _split_wait(name, bufs, send_sem, recv_sem, copies, after):
    n = len(bufs)

    def body(*refs):
        for cp in copies(refs[:n], refs[n], refs[n + 1]):
            cp.wait_send()
            cp.wait_recv()

    res = _pallas(
        body,
        name=name,
        out_shape=[pltpu.HBM(b.shape, b.dtype) for b in bufs],
        in_specs=[HBM_SPEC] * n + [SEM_SPEC, SEM_SPEC, ANY],
        out_specs=[HBM_SPEC] * n,
        input_output_aliases={i: i for i in range(n)},
        compiler_params=pltpu.CompilerParams(has_side_effects=EFFECT),
    )(*bufs, send_sem, recv_sem, after)
    return list(res)


def _pair_exchange_copies(kinds):
    n = len(kinds)

    def copies(refs, send_sem, recv_sem):
        x, y, c, _, _ = _place()
        return [
            _remote(_pair_half(refs[w], kinds[w], 1 - c), refs[n + w], send_sem.at[w], recv_sem.at[w], (x, y, 1 - c))
            for w in range(n)
        ]

    return copies


def _pair_share_copies(kinds, waiting):
    def copies(refs, send_sem, recv_sem):
        x, y, c, _, _ = _place()
        out = []
        for w, kind in enumerate(kinds):
            mine = _pair_half(refs[w], kind, c)
            dst = _pair_half(refs[w], kind, 1 - c) if waiting else mine
            out.append(_remote(mine, dst, send_sem.at[w], recv_sem.at[w], (x, y, 1 - c)))
        return out

    return copies


def _piece_shape(half_shape, kind):
    r, c = half_shape
    return (3, r, c // N_CHIPS) if kind == "col" else (3, r // N_CHIPS, c)


def _chip_send_start(name, halves, kinds):
    n = len(halves)
    lands = [lax.empty(_piece_shape(h.shape, k), BF16) for h, k in zip(halves, kinds)]

    def body(*refs):
        h, land = refs[:n], refs[n : 2 * n]
        send_sem, recv_sem = refs[2 * n], refs[2 * n + 1]
        _, _, c, q, chips = _place()
        for i in range(n):
            for r, chip in enumerate(chips):
                piece = _shard_view(h[i], kinds[i], q ^ _REL_MASK[r])
                _remote(piece, land[i].at[r], send_sem.at[3 * i + r], recv_sem.at[3 * i + r], (*chip, c)).start()

    res = _pallas(
        body,
        name=name,
        out_shape=(
            pltpu.SemaphoreType.DMA((3 * n,)),
            pltpu.SemaphoreType.DMA((3 * n,)),
            *[pltpu.HBM(a.shape, a.dtype) for a in halves],
            *[pltpu.HBM(a.shape, a.dtype) for a in lands],
        ),
        in_specs=[HBM_SPEC] * (2 * n),
        out_specs=(SEM_SPEC, SEM_SPEC, *[HBM_SPEC] * (2 * n)),
        input_output_aliases={i: i + 2 for i in range(2 * n)},
        compiler_params=pltpu.CompilerParams(has_side_effects=EFFECT),
    )(*[_hbm(a) for a in halves], *[_hbm(a) for a in lands])
    return res[0], res[1], list(res[2 : 2 + n]), list(res[2 + n :])


def _chip_send_wait(name, halves, lands, kinds, send_sem, recv_sem, after):
    n = len(halves)

    def body(*refs):
        h, land = refs[:n], refs[n : 2 * n]
        s_sem, r_sem = refs[2 * n], refs[2 * n + 1]
        x, y, c, q, _ = _place()
        for i in range(n):
            for r in range(3):
                piece = _shard_view(h[i], kinds[i], q ^ _REL_MASK[r])
                cp = _remote(piece, land[i].at[r], s_sem.at[3 * i + r], r_sem.at[3 * i + r], (x, y, 1 - c))
                cp.wait_send()
                cp.wait_recv()

    res = _pallas(
        body,
        name=name,
        out_shape=[pltpu.HBM(a.shape, a.dtype) for a in halves] + [pltpu.HBM(a.shape, a.dtype) for a in lands],
        in_specs=[HBM_SPEC] * (2 * n) + [SEM_SPEC, SEM_SPEC, ANY],
        out_specs=[HBM_SPEC] * (2 * n),
        input_output_aliases={i: i for i in range(2 * n)},
        compiler_params=pltpu.CompilerParams(has_side_effects=EFFECT),
    )(*halves, *lands, send_sem, recv_sem, after)
    return list(res[:n]), list(res[n:])


def _small_exchange_copies(waiting):
    def copies(refs, send_sem, recv_sem):
        p, land = refs
        x, y, c, q, _ = _place()
        me = 2 * q + c
        out = []
        for dd in range(1, 2 * N_CHIPS):
            dev = (x ^ ((dd >> 2) & 1), y ^ ((dd >> 1) & 1), c ^ (dd & 1))
            dst = land.at[me ^ dd] if waiting else land.at[me]
            out.append(_remote(p, dst, send_sem.at[dd - 1], recv_sem.at[dd - 1], dev))
        return out

    return copies


def _small_sum(name, me_idx, p, land):
    rows = p.shape[0]
    n_dev = 2 * N_CHIPS

    def body(me_ref, p_ref, land_ref, o_ref):
        me = me_ref[0]
        total = None
        for dev in range(n_dev):
            term = jnp.where(me == dev, p_ref[...], land_ref[dev])
            total = term if total is None else total + term
        o_ref[...] = total

    return _pallas(
        body,
        name=name,
        grid_spec=pltpu.PrefetchScalarGridSpec(
            num_scalar_prefetch=1,
            grid=(1,),
            in_specs=[pl.BlockSpec((rows, LANES), lambda i, m: (0, 0)), pl.BlockSpec((n_dev, rows, LANES), lambda i, m: (0, 0, 0))],
            out_specs=pl.BlockSpec((rows, LANES), lambda i, m: (0, 0)),
        ),
        out_shape=jax.ShapeDtypeStruct(p.shape, F32),
    )(me_idx, p, land)


def _pack(parts):
    rows = []
    for a in parts:
        flat = a.reshape(-1).astype(F32)
        n = flat.shape[0]
        padded = -(-n // (8 * LANES)) * (8 * LANES)
        rows.append(jnp.pad(flat, (0, padded - n)).reshape(-1, LANES))
    return jnp.concatenate(rows, axis=0)


def _unpack(packed, shapes):
    out, row = [], 0
    for shp in shapes:
        n = int(np.prod(shp))
        nrows = -(-n // (8 * LANES)) * 8
        out.append(packed[row : row + nrows].reshape(-1)[:n].reshape(shp))
        row += nrows
    return out


def kernel(x, w_in, norm_mix, sgu_v_gain, sgu_w_s, sgu_b_s, w_a_out, attn_sink, rel_bias, w_b_out, w_o, norm_ffn, w_gate, w_up, w_down, norm_final, loss_target, m_w_in, m_norm_mix, m_sgu_v_gain, m_sgu_w_s, m_sgu_b_s, m_w_a_out, m_attn_sink, m_rel_bias, m_w_b_out, m_w_o, m_norm_ffn, m_w_gate, m_w_up, m_w_down, m_norm_final, v_w_in, v_norm_mix, v_sgu_v_gain, v_sgu_w_s, v_sgu_b_s, v_w_a_out, v_attn_sink, v_rel_bias, v_w_b_out, v_w_o, v_norm_ffn, v_w_gate, v_w_up, v_w_down, v_norm_final):
    s, d = x.shape[1], x.shape[2]
    w_sgu = sgu_v_gain.shape[1]
    groups = sgu_w_s.shape[1]
    heads = attn_sink.shape[1]
    grp = heads // N_KV_HEADS
    w_att = heads * HEAD_DIM
    w_kv = N_KV_HEADS * HEAD_DIM
    d_ff = w_gate.shape[2] * N_CHIPS
    n_in = w_in.shape[2] * N_CHIPS
    off_q = 2 * w_sgu
    off_k = off_q + w_att
    off_g = off_k + 2 * w_kv
    assert n_in == off_g + 2 * d and groups * BLK == w_sgu and s % BLK == 0

    x2d = x.reshape(s, d)
    tgt = loss_target.reshape(s, d)
    c_idx = lax.axis_index("c").astype(I32).reshape(1)
    q_idx = (2 * lax.axis_index("x") + lax.axis_index("y")).astype(I32).reshape(1)
    qc_idx = jnp.concatenate([q_idx, c_idx])

    W_IN, W_A, W_B, W_O, W_GATE, W_UP, W_DOWN = range(7)
    names = ["w_in", "w_a", "w_b", "w_o", "w_gate", "w_up", "w_down"]
    kinds = ["col", "col", "col", "row", "col", "col", "row"]
    big_w = [w_in[0], w_a_out[0], w_b_out[0], w_o[0], w_gate[0], w_up[0], w_down[0]]
    big_m = [m_w_in[0], m_w_a_out[0], m_w_b_out[0], m_w_o[0], m_w_gate[0], m_w_up[0], m_w_down[0]]
    big_v = [v_w_in[0], v_w_a_out[0], v_w_b_out[0], v_w_o[0], v_w_gate[0], v_w_up[0], v_w_down[0]]
    full_in = _cast_into_full("cast_w_in", q_idx, big_w[W_IN], kinds[W_IN])
    in_send, in_recv, (full_in,), token = _gather_start("gather_start_in", [full_in], [kinds[W_IN]], rels=(0, 1))
    rest = [_cast_into_full("cast_" + names[i], q_idx, big_w[i], kinds[i], after=(token,)) for i in range(1, 7)]
    h1 = _rms_fwd("rms_mix", x2d, norm_mix, after=(token,))
    (full_in,) = _gather_wait("gather_wait_in", [full_in], [kinds[W_IN]], [0], in_send, in_recv, h1, rels=(0, 1))
    relay_send, relay_recv, (full_in,) = _split_start("gather_relay_in", [full_in], 2, _relay_copies([kinds[W_IN]], False))
    full_down = rest.pop()
    ag_send, ag_recv, rest, token = _gather_start("gather_start_rest", rest, kinds[1:6], rels=(0, 1), after=(full_in,))
    (full_in,) = _split_wait("gather_relay_wait_in", [full_in], relay_send, relay_recv, _relay_copies([kinds[W_IN]], True), token)
    (g_in,) = _gather_forward("gather_fwd_in", [full_in], [kinds[W_IN]])
    fulls = [g_in] + rest

    def relay_begin(tag, ids, after):
        ks = [kinds[i] for i in ids]
        bufs = _gather_wait("gather_wait_" + tag, [fulls[i] for i in ids], ks, [i - 1 for i in ids], ag_send, ag_recv, after,
                            rels=(0, 1))
        send, recv, bufs = _split_start("gather_relay_" + tag, bufs, 2 * len(ids), _relay_copies(ks, False))
        return tag, ks, send, recv, bufs

    def relay_end(state, after):
        tag, ks, send, recv, bufs = state
        bufs = _split_wait("gather_relay_wait_" + tag, bufs, send, recv, _relay_copies(ks, True), after)
        return _gather_forward("gather_fwd_" + tag, bufs, ks)

    def relay_end_async(state, after):
        tag, ks, send, recv, bufs = state
        bufs = _split_wait("gather_relay_wait_" + tag, bufs, send, recv, _relay_copies(ks, True), after)
        send, recv, bufs = _split_start("gather_fwd_start_" + tag, bufs, 3 * len(ks), _forward_copies(ks, False))
        return tag, ks, send, recv, bufs

    def forwarded(state, after):
        tag, ks, send, recv, bufs = state
        return _split_wait("gather_fwd_wait_" + tag, bufs, send, recv, _forward_copies(ks, True), after)

    ws_b = sgu_w_s[0].astype(BF16)
    wst_b = jnp.swapaxes(sgu_w_s[0], 1, 2).astype(BF16)
    b_col = sgu_b_s[0].reshape(groups, BLK, 1)
    bias_tab, bucket = _band_tables(rel_bias)
    sink = attn_sink[0]

    tm = _tile(s, (1024, 512, 256, 128))

    tn = _tile(n_in, (768, 640, 512))
    z = _mm(
        "mm_z", (s // tm, n_in // tn, 1), [h1, g_in],
        [pl.BlockSpec((tm, d), lambda i, j, k: (i, 0)), pl.BlockSpec((d, tn), lambda i, j, k: (0, j))],
        [jax.ShapeDtypeStruct((s, n_in), BF16)], [pl.BlockSpec((tm, tn), lambda i, j, k: (i, j))],
        [(0, 1, NN, 0)], 1, (tm, tn), 1, lambda ins, vals, outs, cs: _put(outs[0], cs, vals[0]),
        _mm_vmem([((tm, d), BF16, 2), ((d, tn), BF16, 2), ((tm, tn), F32, 3)]),
    )[0]
    mix_relay = relay_begin("mix", [W_A, W_B, W_O], z)

    a_act = _sgu_fwd(z, sgu_v_gain, ws_b, b_col, w_sgu, after=(mix_relay[4][0],))

    kv_b = z[:, off_k:off_g]
    k_pad = jnp.pad(kv_b[:, :w_kv], ((BLK, BLK), (0, 0)))
    v_pad = jnp.pad(kv_b[:, w_kv:], ((BLK, BLK), (0, 0)))
    q_blk0 = off_q // (grp * HEAD_DIM)
    att = _attn_fwd(sink, z, k_pad, v_pad, bias_tab, grp, q_blk0)

    gate_relay = relay_begin("gate", [W_GATE], att)
    up_relay = relay_begin("up", [W_UP], gate_relay[4][0])
    down_send, down_recv, (full_down,), token = _gather_start(
        "gather_start_down", [full_down], [kinds[W_DOWN]], after=(a_act, up_relay[4][0])
    )
    g_a, g_b, g_o = relay_end(mix_relay, token)

    tg = _tile(d, (512,))
    ga0, gb0 = off_g // tg, (off_g + d) // tg

    def ep_gate(ins, vals, outs, cs):
        sa, sb = _sigmoid(ins[4][:, cs].astype(F32)), _sigmoid(ins[5][:, cs].astype(F32))
        _put(outs[0], cs, sa * vals[0] + sb * vals[1])
        _put(outs[1], cs, vals[0])
        _put(outs[2], cs, vals[1])

    t_out = pl.BlockSpec((tm, tg), lambda i, j, k: (i, j))
    m_act, y_a, y_b = _mm(
        "mm_branches", (s // tm, d // tg, 1), [a_act, g_a, att, g_b, z, z],
        [pl.BlockSpec((tm, w_sgu), lambda i, j, k: (i, 0)), pl.BlockSpec((w_sgu, tg), lambda i, j, k: (0, j)),
         pl.BlockSpec((tm, w_att), lambda i, j, k: (i, 0)), pl.BlockSpec((w_att, tg), lambda i, j, k: (0, j)),
         pl.BlockSpec((tm, tg), lambda i, j, k: (i, ga0 + j)), pl.BlockSpec((tm, tg), lambda i, j, k: (i, gb0 + j))],
        [jax.ShapeDtypeStruct((s, d), BF16)] * 3,
        [t_out, t_out, t_out], [(0, 1, NN, 0), (2, 3, NN, 1)], 2, (tm, tg), 1, ep_gate,
        _mm_vmem([((tm, w_sgu), BF16, 4), ((w_sgu, tg), BF16, 4), ((tm, tg), F32, 12)]),    )

    tn = _tile(d, (1024, 512))

    def ep_residual(ins, vals, outs, cs):
        _put(outs[0], cs, ins[2][:, cs] + vals[0])

    gate_fwd = relay_end_async(gate_relay, m_act)
    x2 = _mm(
        "mm_wo", (s // tm, d // tn, 1), [m_act, g_o, x2d],
        [pl.BlockSpec((tm, d), lambda i, j, k: (i, 0)), pl.BlockSpec((d, tn), lambda i, j, k: (0, j)),
         pl.BlockSpec((tm, tn), lambda i, j, k: (i, j))],
        [jax.ShapeDtypeStruct((s, d), F32)], [pl.BlockSpec((tm, tn), lambda i, j, k: (i, j))],
        [(0, 1, NN, 0)], 1, (tm, tn), 1, ep_residual,
        _mm_vmem([((tm, d), BF16, 2), ((d, tn), BF16, 2), ((tm, tn), F32, 5)]),
        after=(gate_fwd[4][0],),
    )[0]
    (g_gate,) = forwarded(gate_fwd, x2)
    up_fwd = relay_end_async(up_relay, g_gate)
    h2 = _rms_fwd("rms_ffn", x2, norm_ffn, after=(up_fwd[4][0],))
    (g_up,) = forwarded(up_fwd, h2)

    tf = _tile(d_ff, (512,))

    def ep_swiglu(ins, vals, outs, cs):
        gt, up = vals
        _put(outs[0], cs, gt)
        _put(outs[1], cs, up)
        _put(outs[2], cs, (gt * _sigmoid(gt)) * up)

    f_out = pl.BlockSpec((tm, tf), lambda i, j, k: (i, j))
    gt, up, f_act = _mm(
        "mm_gate_up", (s // tm, d_ff // tf, 1), [h2, g_gate, g_up],
        [pl.BlockSpec((tm, d), lambda i, j, k: (i, 0)), pl.BlockSpec((d, tf), lambda i, j, k: (0, j)),
         pl.BlockSpec((d, tf), lambda i, j, k: (0, j))],
        [jax.ShapeDtypeStruct((s, d_ff), BF16)] * 3,
        [f_out, f_out, f_out], [(0, 1, NN, 0), (0, 2, NN, 1)], 2, (tm, tf), 1, ep_swiglu,
        _mm_vmem([((tm, d), BF16, 2), ((d, tf), BF16, 4), ((tm, tf), F32, 8)]),    )
    (g_down,) = _gather_forward(
        "gather_fwd_ffn_out",
        _gather_wait("gather_wait_ffn_out", [full_down], [kinds[W_DOWN]], [0], down_send, down_recv, f_act),
        [kinds[W_DOWN]],
    )

    tkf = _tile(d_ff, (1408, 1024, 512))
    tml, tnl = _tile(s, (512, 256, 128)), _tile(d, (512,))
    x3 = _mm(
        "mm_down", (s // tml, d // tnl, 1), [f_act, g_down, x2],
        [pl.BlockSpec((tml, d_ff), lambda i, j, k: (i, 0)), pl.BlockSpec((d_ff, tnl), lambda i, j, k: (0, j)),
         pl.BlockSpec((tml, tnl), lambda i, j, k: (i, j))],
        [jax.ShapeDtypeStruct((s, d), F32)], [pl.BlockSpec((tml, tnl), lambda i, j, k: (i, j))],
        [(0, 1, NN, 0)], 1, (tml, tnl), 1, ep_residual,
        _mm_vmem([((tml, d_ff), BF16, 2), ((d_ff, tnl), BF16, 2), ((tml, tnl), F32, 6)]),    )[0]

    dx3, dx3b, dg_final, loss_part = _head(x3, norm_final.reshape(1, d), tgt)

    def reduce_a(tag, ids, grads):
        ks = [kinds[i] for i in ids]
        lands = [lax.empty((g.shape[0] // 2, g.shape[1]) if k == "col" else (g.shape[0], g.shape[1] // 2), BF16)
                 for g, k in zip(grads, ks)]
        send, recv, bufs = _split_start("pair_send_" + tag, list(grads) + lands, len(ids), _pair_exchange_copies(ks))
        return {"tag": tag, "ids": ids, "ks": ks, "pair": (send, recv, bufs), "token": bufs[0]}

    def reduce_b(st, after):
        tag, ids, ks = st["tag"], st["ids"], st["ks"]
        send, recv, bufs = st["pair"]
        bufs = _split_wait("pair_wait_" + tag, bufs, send, recv, _pair_exchange_copies(ks), after)
        grads, from_sib = bufs[: len(ids)], bufs[len(ids) :]
        halves = [_pair_add("pair_add_" + names[i], c_idx, g, r, k) for i, g, r, k in zip(ids, grads, from_sib, ks)]
        st["chip"] = _chip_send_start("chip_send_" + tag, halves, ks)
        st["token"] = st["chip"][2][0]

    def reduce_c(st, after):
        tag, ids, ks = st["tag"], st["ids"], st["ks"]
        send, recv, halves, lands = st["chip"]
        halves, lands = _chip_send_wait("chip_wait_" + tag, halves, lands, ks, send, recv, after)
        pieces = [_chip_sum("chip_sum_" + names[i], qc_idx, h, r, k) for i, h, r, k in zip(ids, halves, lands, ks)]
        st["share"] = _split_start("share_send_" + tag, pieces, len(ids), _pair_share_copies(ks, False))
        st["token"] = st["share"][2][0]

    def reduce_d(st, after):
        send, recv, bufs = st["share"]
        return _split_wait("share_wait_" + st["tag"], bufs, send, recv, _pair_share_copies(st["ks"], True), after)

    grads_big, upd = [None] * 7, [None] * 7

    def finish(st, after):
        shared = reduce_d(st, after)
        after = shared[0]
        for i, g in zip(st["ids"], shared):
            upd[i] = _adamw("adamw_" + names[i], big_w[i], g, big_m[i], big_v[i], after=(after,))
            grads_big[i] = upd[i][3]
            after = upd[i][0]
        return after

    def ep_swiglu_bwd(ins, vals, outs, cs):
        df = vals[0]
        gtv, upv = ins[2][:, cs].astype(F32), ins[3][:, cs].astype(F32)
        sg = _sigmoid(gtv)
        _put(outs[0], cs, df * upv * (sg + gtv * sg * (1.0 - sg)))
        _put(outs[1], cs, df * (gtv * sg))

    dgt, dup = _mm(
        "mm_dswiglu", (s // tm, d_ff // tf, 1), [dx3b, g_down, gt, up],
        [pl.BlockSpec((tm, d), lambda i, j, k: (i, 0)), pl.BlockSpec((tf, d), lambda i, j, k: (j, 0)), f_out, f_out],
        [jax.ShapeDtypeStruct((s, d_ff), BF16), jax.ShapeDtypeStruct((s, d_ff), BF16)], [f_out, f_out],
        [(0, 1, NT, 0)], 1, (tm, tf), 1, ep_swiglu_bwd,
        _mm_vmem([((tm, d), BF16, 2), ((tf, d), BF16, 2), ((tm, tf), F32, 8)]),    )

    def ep_store(ins, vals, outs, cs):
        for o, v in zip(outs, vals):
            _put(o, cs, v)

    twn = _tile(d, (1024, 512))
    gw_down = _mm(
        "mm_gw_down", (d_ff // tkf, d // twn, 1), [f_act, dx3b],
        [pl.BlockSpec((s, tkf), lambda i, j, k: (0, i)), pl.BlockSpec((s, twn), lambda i, j, k: (0, j))],
        [jax.ShapeDtypeStruct((d_ff, d), BF16)], [pl.BlockSpec((tkf, twn), lambda i, j, k: (i, j))],
        [(0, 1, TN, 0)], 1, (tkf, twn), 1, ep_store,
        _mm_vmem([((s, tkf), BF16, 3), ((s, twn), BF16, 2), ((tkf, twn), F32, 3)]),
    )[0]
    red_down = reduce_a("down", [W_DOWN], [gw_down])

    tn2 = _tile(d, (256,))
    dh2_specs = [pl.BlockSpec((tm, d_ff), lambda i, j, k: (i, 0)), pl.BlockSpec((tn2, d_ff), lambda i, j, k: (j, 0))]
    dh2_tile = pl.BlockSpec((tm, tn2), lambda i, j, k: (i, j))
    dh2_vmem = _mm_vmem([((tm, d_ff), BF16, 2), ((tn2, d_ff), BF16, 2), ((tm, tn2), F32, 7)])
    dh2 = _mm(
        "mm_dh2_gate", (s // tm, d // tn2, 1), [dgt, g_gate], dh2_specs,
        [jax.ShapeDtypeStruct((s, d), F32)], [dh2_tile], [(0, 1, NT, 0)], 1, (tm, tn2), 1, ep_store, dh2_vmem,
        after=(red_down["token"],),
    )[0]
    dh2 = _mm(
        "mm_dh2_up", (s // tm, d // tn2, 1), [dup, g_up, dh2], dh2_specs + [dh2_tile],
        [jax.ShapeDtypeStruct((s, d), F32)], [dh2_tile], [(0, 1, NT, 0)], 1, (tm, tn2), 1, ep_residual, dh2_vmem,
    )[0]
    reduce_b(red_down, dh2)

    twr = _tile(d, (1024, 512))
    w_tile = pl.BlockSpec((twr, tf), lambda i, j, k: (i, j))
    gw_gate, gw_up = _mm(
        "mm_gw_gate_up", (d // twr, d_ff // tf, 1), [h2, dgt, dup],
        [pl.BlockSpec((s, twr), lambda i, j, k: (0, i)), pl.BlockSpec((s, tf), lambda i, j, k: (0, j)),
         pl.BlockSpec((s, tf), lambda i, j, k: (0, j))],
        [jax.ShapeDtypeStruct((d, d_ff), BF16), jax.ShapeDtypeStruct((d, d_ff), BF16)], [w_tile, w_tile],
        [(0, 1, TN, 0), (0, 2, TN, 1)], 2, (twr, tf), 1, ep_store,
        _mm_vmem([((s, twr), BF16, 3), ((s, tf), BF16, 4), ((twr, tf), F32, 6)]),
        after=(red_down["token"],),
    )
    red_ffn = reduce_a("ffn_in", [W_GATE, W_UP], [gw_gate, gw_up])

    dx2, dx2b, dg_ffn = _rms_bwd("rms_ffn_bwd", x2, norm_ffn, dh2, dx3, after=(red_ffn["token"],))

    nj = d // tg

    def lo(j):
        return jnp.minimum(j, nj - 1)

    def gate_bwd_body(dx_ref, wo_ref, ga_ref, gb_ref, ya_ref, yb_ref, dya_ref, dyb_ref, dz_ref, keep):
        j = pl.program_id(1)

        @pl.when(j < nj)
        def _():
            dm = lax.dot_general(dx_ref[...], wo_ref[...], NT, preferred_element_type=F32)
            sa, sb = _sigmoid(ga_ref[...].astype(F32)), _sigmoid(gb_ref[...].astype(F32))
            dya_ref[...] = (dm * sa).astype(BF16)
            dyb_ref[...] = (dm * sb).astype(BF16)
            dz_ref[...] = (dm * ya_ref[...].astype(F32) * (sa * (1.0 - sa))).astype(BF16)
            keep[lo(j)] = (dm * yb_ref[...].astype(F32) * (sb * (1.0 - sb))).astype(BF16)

        @pl.when(j >= nj)
        def _():
            dz_ref[...] = keep[jnp.maximum(j - nj, 0)]

    t_lo = pl.BlockSpec((tm, tg), lambda i, j: (i, lo(j)))
    dya, dyb, dz = _pallas(
        gate_bwd_body,
        name="mm_dgate",
        grid=(s // tm, 2 * nj),
        in_specs=[
            pl.BlockSpec((tm, d), lambda i, j: (i, 0)),
            pl.BlockSpec((tg, d), lambda i, j: (lo(j), 0)),
            pl.BlockSpec((tm, tg), lambda i, j: (i, ga0 + lo(j))),
            pl.BlockSpec((tm, tg), lambda i, j: (i, gb0 + lo(j))),
            t_lo,
            t_lo,
        ],
        out_specs=[t_lo, t_lo, pl.BlockSpec((tm, tg), lambda i, j: (i, ga0 + j))],
        out_shape=[jax.ShapeDtypeStruct((s, d), BF16), jax.ShapeDtypeStruct((s, d), BF16), jax.ShapeDtypeStruct((s, n_in), BF16)],
        scratch_shapes=[pltpu.VMEM((nj, tm, tg), BF16)],
        compiler_params=_params(_mm_vmem([((tm, d), BF16, 2), ((tg, d), BF16, 2), ((tm, tg), F32, 14), ((nj, tm, tg), BF16, 1)])),
    )(dx2b, g_o, z, z, y_a, y_b)
    reduce_b(red_ffn, dya)
    reduce_c(red_down, red_ffn["token"])

    gw_o = _mm(
        "mm_gw_o", (d // twr, d // twn, 1), [m_act, dx2b],
        [pl.BlockSpec((s, twr), lambda i, j, k: (0, i)), pl.BlockSpec((s, twn), lambda i, j, k: (0, j))],
        [jax.ShapeDtypeStruct((d, d), BF16)], [pl.BlockSpec((twr, twn), lambda i, j, k: (i, j))],
        [(0, 1, TN, 0)], 1, (twr, twn), 1, ep_store,
        _mm_vmem([((s, twr), BF16, 3), ((s, twn), BF16, 2), ((twr, twn), F32, 3)]),
        after=(red_down["token"],),
    )[0]
    after_down = finish(red_down, gw_o)

    tb = _tile(w_sgu, (1024, 512))
    b_out = pl.BlockSpec((tm, tb), lambda i, j, k: (i, j))

    da, datt = _mm(
        "mm_dbranches", (s // tm, w_sgu // tb, 1), [dya, g_a, dyb, g_b],
        [pl.BlockSpec((tm, d), lambda i, j, k: (i, 0)), pl.BlockSpec((tb, d), lambda i, j, k: (j, 0)),
         pl.BlockSpec((tm, d), lambda i, j, k: (i, 0)), pl.BlockSpec((tb, d), lambda i, j, k: (j, 0))],
        [jax.ShapeDtypeStruct((s, w_sgu), BF16), jax.ShapeDtypeStruct((s, w_att), BF16)], [b_out, b_out],
        [(0, 1, NT, 0), (2, 3, NT, 1)], 2, (tm, tb), 1, ep_store,
        _mm_vmem([((tm, d), BF16, 4), ((tb, d), BF16, 4), ((tm, tb), F32, 6)]),
        after=(after_down,),
    )

    wb_tile = pl.BlockSpec((tb, twn), lambda i, j, k: (i, j))
    gw_a, gw_b = _mm(
        "mm_gw_branches", (w_sgu // tb, d // twn, 1), [a_act, dya, att, dyb],
        [pl.BlockSpec((s, tb), lambda i, j, k: (0, i)), pl.BlockSpec((s, twn), lambda i, j, k: (0, j)),
         pl.BlockSpec((s, tb), lambda i, j, k: (0, i)), pl.BlockSpec((s, twn), lambda i, j, k: (0, j))],
        [jax.ShapeDtypeStruct((w_sgu, d), BF16), jax.ShapeDtypeStruct((w_att, d), BF16)], [wb_tile, wb_tile],
        [(0, 1, TN, 0), (2, 3, TN, 1)], 2, (tb, twn), 1, ep_store,
        _mm_vmem([((s, tb), BF16, 5), ((s, twn), BF16, 4), ((tb, twn), F32, 6)]),
        after=(da,),
    )
    red_mix = reduce_a("mix", [W_O, W_A, W_B], [gw_o, gw_a, gw_b])

    dz, dws, dbs, dgain = _sgu_bwd(z, da, sgu_v_gain, ws_b, wst_b, b_col, w_sgu, dz, after=(red_mix["token"],))
    dz, dk_pad, dv_pad, dbias_tab, dsink = _attn_bwd(sink, z, k_pad, v_pad, bias_tab, datt, dz, grp, q_blk0)
    dz = _dkv_to_dz(dk_pad, dv_pad, dz, off_k // (2 * w_kv))
    drel = _relbias_bwd(dbias_tab, bucket)
    reduce_b(red_mix, dz)
    reduce_c(red_ffn, red_mix["token"])

    small_w = [norm_mix, sgu_v_gain, sgu_w_s, sgu_b_s, attn_sink, rel_bias, norm_ffn, norm_final]
    small_m = [m_norm_mix, m_sgu_v_gain, m_sgu_w_s, m_sgu_b_s, m_attn_sink, m_rel_bias, m_norm_ffn, m_norm_final]
    small_v = [v_norm_mix, v_sgu_v_gain, v_sgu_w_s, v_sgu_b_s, v_attn_sink, v_rel_bias, v_norm_ffn, v_norm_final]
    small_shapes = [w.shape for w in small_w]
    early = [dgain, dws, dbs, dsink[:, 0], drel[:, :REL_BUCKETS].T, dg_ffn, dg_final]
    p_early = _pack([g.reshape(shp) for g, shp in zip(early, small_shapes[1:])] + [loss_part[0, :1]])
    land = jnp.zeros((2 * N_CHIPS,) + p_early.shape, F32)
    sm_send, sm_recv, (p_early, land) = _split_start("small_send", [p_early, land], 2 * N_CHIPS - 1, _small_exchange_copies(False))

    tzn = _tile(n_in, (768, 640, 512))
    gw_in = _mm(
        "mm_gw_in", (d // twr, n_in // tzn, 1), [h1, dz],
        [pl.BlockSpec((s, twr), lambda i, j, k: (0, i)), pl.BlockSpec((s, tzn), lambda i, j, k: (0, j))],
        [jax.ShapeDtypeStruct((d, n_in), BF16)], [pl.BlockSpec((twr, tzn), lambda i, j, k: (i, j))],
        [(0, 1, TN, 0)], 1, (twr, tzn), 1, ep_store,
        _mm_vmem([((s, twr), BF16, 3), ((s, tzn), BF16, 2), ((twr, tzn), F32, 3)]),
        after=(red_ffn["token"], p_early),
    )[0]
    red_in = reduce_a("w_in", [W_IN], [gw_in])

    reduce_c(red_mix, red_in["token"])
    reduce_b(red_in, finish(red_mix, red_in["token"]))

    dh1 = _mm(
        "mm_dh1", (s // tm, d // tn2, 1), [dz, g_in],
        [pl.BlockSpec((tm, n_in), lambda i, j, k: (i, 0)), pl.BlockSpec((tn2, n_in), lambda i, j, k: (j, 0))],
        [jax.ShapeDtypeStruct((s, d), F32)], [pl.BlockSpec((tm, tn2), lambda i, j, k: (i, j))],
        [(0, 1, NT, 0)], 1, (tm, tn2), 1, ep_store,
        _mm_vmem([((tm, n_in), BF16, 2), ((tn2, n_in), BF16, 2), ((tm, tn2), F32, 5)]),
        after=(red_in["token"],),
    )[0]

    grad_x, _, dg_mix = _rms_bwd("rms_mix_bwd", x2d, norm_mix, dh1, dx2)

    p_mix = _pack([dg_mix.reshape(small_shapes[0])])
    land_mix = jnp.zeros((2 * N_CHIPS,) + p_mix.shape, F32)
    mx_send, mx_recv, (p_mix, land_mix) = _split_start("mix_send", [p_mix, land_mix], 2 * N_CHIPS - 1, _small_exchange_copies(False))

    reduce_c(red_in, finish(red_ffn, p_mix))
    p_early, land = _split_wait("small_wait", [p_early, land], sm_send, sm_recv, _small_exchange_copies(True), red_in["token"])
    p_mix, land_mix = _split_wait("mix_wait", [p_mix, land_mix], mx_send, mx_recv, _small_exchange_copies(True), p_early)
    me_idx = 2 * q_idx + c_idx
    packed_g = jnp.concatenate([_small_sum("mix_sum", me_idx, p_mix, land_mix), _small_sum("small_sum", me_idx, p_early, land)], axis=0)
    g_small = _unpack(packed_g, small_shapes + [(1,)])
    loss = g_small[-1].reshape(())
    g_small = g_small[:-1]
    zero1 = jnp.zeros((1,), F32)
    pw, pg, pm, pv = _pack(small_w + [zero1]), _pack(g_small + [zero1]), _pack(small_m + [zero1]), _pack(small_v + [zero1])
    small_upd = _adamw("adamw_small", pw, pg, pm, pv)
    d_small, nm_small, nv_small = [_unpack(a, small_shapes) for a in small_upd[:3]]
    finish(red_in, small_upd[0])

    small_names = ["norm_mix", "sgu_v_gain", "sgu_w_s", "sgu_b_s", "attn_sink", "rel_bias", "norm_ffn", "norm_final"]
    table = {}
    for i, n in enumerate(names):
        table[n] = (grads_big[i][None], upd[i][0][None], upd[i][1][None], upd[i][2][None])
    for i, n in enumerate(small_names):
        table[n] = (g_small[i], d_small[i], nm_small[i], nv_small[i])
    order = ["w_in", "norm_mix", "sgu_v_gain", "sgu_w_s", "sgu_b_s", "w_a", "attn_sink", "rel_bias", "w_b", "w_o", "norm_ffn",
             "w_gate", "w_up", "w_down", "norm_final"]
    outs = [loss, grad_x.reshape(1, s, d)]
    for part in range(4):
        outs += [table[n][part] for n in order]
    return tuple(outs)
```

```python
import math

import jax
import jax.numpy as jnp
import numpy as np
from jax import lax
from jax.experimental import pallas as pl
from jax.experimental.pallas import tpu as pltpu

F32 = jnp.float32
BF16 = jnp.bfloat16
I32 = jnp.int32
MESH = pl.DeviceIdType.MESH

EPS = 1e-6
NEG = -1e30
BLK = 128
HEAD_DIM = 128
N_KV_HEADS = 2
REL_BUCKETS = 32
REL_MAX_DIST = 128
N_CHIPS = 4
ADAM_LR, ADAM_B1, ADAM_B2, ADAM_EPS, ADAM_WD, ADAM_STEP = 0.001, 0.9, 0.999, 1e-08, 0.01, 10

LANES = 128
VMEM_CAP = 60 * 1024 * 1024

NN = (((1,), (0,)), ((), ()))
NT = (((1,), (1,)), ((), ()))
TN = (((0,), (0,)), ((), ()))
ANY = pl.BlockSpec(memory_space=pl.ANY)
HBM_SPEC = pl.BlockSpec(memory_space=pltpu.HBM)
SEM_SPEC = pl.BlockSpec(memory_space=pltpu.SEMAPHORE)
EFFECT = pltpu.SideEffectType.DATAFLOW_SIDE_EFFECTING


def _tile(n, cands):
    for t in cands:
        if n % t == 0:
            return t
    return n


PIN_BYTES = 64 * 1024


def _pin_hbm(a):
    big = hasattr(a, "dtype") and jnp.issubdtype(a.dtype, jnp.floating) and _nbytes(a.shape, a.dtype) >= PIN_BYTES
    return pltpu.with_memory_space_constraint(a, pltpu.HBM) if big else a


def _pallas(body, *, out_shape, **kw):
    def pin(o):
        big = isinstance(o, jax.ShapeDtypeStruct) and jnp.issubdtype(o.dtype, jnp.floating) and _nbytes(o.shape, o.dtype) >= PIN_BYTES
        return pltpu.HBM(o.shape, o.dtype) if big else o

    shapes = type(out_shape)(pin(o) for o in out_shape) if isinstance(out_shape, (list, tuple)) else pin(out_shape)
    call = pl.pallas_call(body, out_shape=shapes, **kw)
    return lambda *args: call(*[_pin_hbm(a) for a in args])


def _params(vmem_bytes=None, **kw):
    if vmem_bytes is not None:
        kw["vmem_limit_bytes"] = int(min(max(vmem_bytes, 32 * 1024 * 1024), VMEM_CAP))
    return pltpu.CompilerParams(**kw)


def _nbytes(shape, dtype):
    return int(np.prod(shape)) * jnp.dtype(dtype).itemsize


def _sigmoid(x):
    return 1.0 / (1.0 + jnp.exp(-x))


_GC = 0.7978845608028654
_GA = 0.044715


def _gelu(x):
    return 0.5 * x * (1.0 + jnp.tanh(_GC * (x + _GA * (x * x * x))))


def _gelu_grad(x):
    t = jnp.tanh(_GC * (x + _GA * (x * x * x)))
    return 0.5 * (1.0 + t) + 0.5 * x * (1.0 - t * t) * (_GC * (1.0 + 3.0 * _GA * (x * x)))


def _bf(v):
    return v if v.dtype == BF16 else v.astype(BF16)


def _mm(name, grid, ins, in_specs, out_shape, out_specs, pairs, n_acc, tile, nk, epilogue, vmem_bytes, after=()):
    assert nk == 1
    n_in, n_out = len(ins) + len(after), len(out_shape)

    def body(*refs):
        in_refs, out_refs = refs[:n_in], refs[n_in : n_in + n_out]
        vals = [None] * n_acc
        for a_i, b_i, dn, acc_i in pairs:
            d = lax.dot_general(_bf(in_refs[a_i][...]), _bf(in_refs[b_i][...]), dn, preferred_element_type=F32)
            vals[acc_i] = d if vals[acc_i] is None else vals[acc_i] + d
        epilogue(in_refs, vals, out_refs, slice(None))

    return _pallas(
        body,
        name=name,
        grid=grid,
        in_specs=list(in_specs) + [ANY] * len(after),
        out_specs=out_specs,
        out_shape=out_shape,
        compiler_params=_params(vmem_bytes),
    )(*ins, *after)


def _put(ref, cs, v):
    ref[:, cs] = v.astype(ref.dtype)


def _mm_vmem(tiles):
    return sum(_nbytes(s, d) * c for s, d, c in tiles) + 4 * 1024 * 1024


def _rows8(v):
    r, d = v.shape
    return v.reshape(r // 8, 8, d).sum(axis=0)


def _rms_fwd(name, x, g, after=()):
    s, d = x.shape
    tm = _tile(s, (256, 128))

    def body(x_ref, g_ref, *rest):
        h_ref = rest[-1]
        xv = x_ref[...]
        r = lax.rsqrt(jnp.mean(xv * xv, axis=-1, keepdims=True) + EPS)
        h_ref[...] = ((xv * r) * g_ref[...]).astype(BF16)

    return _pallas(
        body,
        name=name,
        grid=(s // tm,),
        in_specs=[pl.BlockSpec((tm, d), lambda i: (i, 0)), pl.BlockSpec((1, d), lambda i: (0, 0))] + [ANY] * len(after),
        out_specs=pl.BlockSpec((tm, d), lambda i: (i, 0)),
        out_shape=jax.ShapeDtypeStruct((s, d), BF16),
    )(x, g, *after)


def _rms_bwd(name, x, g, dh, dres, after=()):
    s, d = x.shape
    tm = _tile(s, (256, 128))
    n = s // tm
    n_after = len(after)

    def body(x_ref, g_ref, dh_ref, dres_ref, *rest):
        dx_ref, dxb_ref, dg_ref, acc_ref = rest[n_after:]
        i = pl.program_id(0)
        xv = x_ref[...]
        r = lax.rsqrt(jnp.mean(xv * xv, axis=-1, keepdims=True) + EPS)
        xh = xv * r
        dhv = dh_ref[...]
        dxh = dhv * g_ref[...]
        dx = r * (dxh - xh * jnp.mean(dxh * xh, axis=-1, keepdims=True)) + dres_ref[...]
        dx_ref[...] = dx
        dxb_ref[...] = dx.astype(BF16)
        part = _rows8(dhv * xh)

        @pl.when(i == 0)
        def _():
            acc_ref[...] = part

        @pl.when(i > 0)
        def _():
            acc_ref[...] += part

        @pl.when(i == n - 1)
        def _():
            dg_ref[...] = jnp.sum(acc_ref[...], axis=0, keepdims=True)

    row = pl.BlockSpec((tm, d), lambda i: (i, 0))
    vec = pl.BlockSpec((1, d), lambda i: (0, 0))
    return _pallas(
        body,
        name=name,
        grid=(n,),
        in_specs=[row, vec, row, row] + [ANY] * n_after,
        out_specs=[row, row, vec],
        out_shape=[jax.ShapeDtypeStruct((s, d), F32), jax.ShapeDtypeStruct((s, d), BF16), jax.ShapeDtypeStruct((1, d), F32)],
        scratch_shapes=[pltpu.VMEM((8, d), F32)],
    )(x, g, dh, dres, *after)


def _head(x3, g, target):
    s, d = x3.shape
    tm = _tile(s, (256, 128))
    n = s // tm

    def body(x_ref, g_ref, t_ref, dx_ref, dxb_ref, dg_ref, loss_ref, acc_g, acc_l):
        i = pl.program_id(0)
        xv = x_ref[...]
        gv = g_ref[...]
        r = lax.rsqrt(jnp.mean(xv * xv, axis=-1, keepdims=True) + EPS)
        xh = xv * r
        e = xh * gv - t_ref[...]
        dy = e * (1.0 / d)
        dxh = dy * gv
        dx = r * (dxh - xh * jnp.mean(dxh * xh, axis=-1, keepdims=True))
        dx_ref[...] = dx
        dxb_ref[...] = dx.astype(BF16)
        pg = _rows8(dy * xh)
        plo = _rows8(e * e)

        @pl.when(i == 0)
        def _():
            acc_g[...] = pg
            acc_l[...] = plo

        @pl.when(i > 0)
        def _():
            acc_g[...] += pg
            acc_l[...] += plo

        @pl.when(i == n - 1)
        def _():
            dg_ref[...] = jnp.sum(acc_g[...], axis=0, keepdims=True)
            loss_ref[...] = jnp.full((1, LANES), (0.5 / d) * jnp.sum(acc_l[...]), F32)

    row = pl.BlockSpec((tm, d), lambda i: (i, 0))
    vec = pl.BlockSpec((1, d), lambda i: (0, 0))
    return _pallas(
        body,
        name="head",
        grid=(n,),
        in_specs=[row, vec, row],
        out_specs=[row, row, vec, pl.BlockSpec((1, LANES), lambda i: (0, 0))],
        out_shape=[
            jax.ShapeDtypeStruct((s, d), F32),
            jax.ShapeDtypeStruct((s, d), BF16),
            jax.ShapeDtypeStruct((1, d), F32),
            jax.ShapeDtypeStruct((1, LANES), F32),
        ],
        scratch_shapes=[pltpu.VMEM((8, d), F32), pltpu.VMEM((8, d), F32)],
    )(x3, g, target)


def _sgu_fwd(z, gain, ws_b, b_col, w_sgu, after=()):
    s = z.shape[0]
    groups = ws_b.shape[0]

    def body(zu_ref, zv_ref, gain_ref, ws_ref, b_ref, *rest):
        a_ref = rest[-1]
        vv = _gelu(zv_ref[...].astype(F32))
        r = lax.rsqrt(jnp.mean(vv * vv, axis=-1, keepdims=True) + EPS)
        vn = ((vv * r) * gain_ref[...]).astype(BF16)
        u = _gelu(zu_ref[...].astype(F32))
        for g in range(groups):
            sl = slice(g * BLK, (g + 1) * BLK)
            mixed = jnp.dot(ws_ref[g], vn[:, sl], preferred_element_type=F32) + b_ref[g]
            a_ref[:, sl] = (u[:, sl] * mixed).astype(BF16)

    return _pallas(
        body,
        name="sgu_fwd",
        grid=(s // BLK,),
        in_specs=[
            pl.BlockSpec((BLK, w_sgu), lambda c: (c, 0)),
            pl.BlockSpec((BLK, w_sgu), lambda c: (c, 1)),
            pl.BlockSpec((1, w_sgu), lambda c: (0, 0)),
            pl.BlockSpec((groups, BLK, BLK), lambda c: (0, 0, 0)),
            pl.BlockSpec((groups, BLK, 1), lambda c: (0, 0, 0)),
        ]
        + [ANY] * len(after),
        out_specs=pl.BlockSpec((BLK, w_sgu), lambda c: (c, 0)),
        out_shape=jax.ShapeDtypeStruct((s, w_sgu), BF16),
    )(z, z, gain, ws_b, b_col, *after)


def _sgu_bwd(z, da, gain, ws_b, wst_b, b_col, w_sgu, dz, after=()):
    s = z.shape[0]
    groups = ws_b.shape[0]
    n = s // BLK
    n_skip = 1 + len(after)

    def body(zu_ref, zv_ref, da_ref, gain_ref, ws_ref, wst_ref, b_ref, *rest):
        dz_ref, dws_ref, dbs_ref, dgain_ref, acc_gain, vv_s, gv_s, dxh_s = rest[n_skip:]
        c = pl.program_id(0)
        cols = [slice(g * BLK, (g + 1) * BLK) for g in range(groups)]

        ss = jnp.zeros((BLK, 1), F32)
        for sl in cols:
            zv = zv_ref[:, sl].astype(F32)
            vv = _gelu(zv)
            vv_s[:, sl] = vv
            gv_s[:, sl] = _gelu_grad(zv)
            ss = ss + jnp.sum(vv * vv, axis=-1, keepdims=True)
        r = lax.rsqrt(ss * (1.0 / w_sgu) + EPS)

        dot_dx = jnp.zeros((BLK, 1), F32)
        for g, sl in enumerate(cols):
            gain_g = gain_ref[:, sl]
            xh = vv_s[:, sl] * r
            vn = (xh * gain_g).astype(BF16)
            zu = zu_ref[:, sl].astype(F32)
            dav = da_ref[:, sl].astype(F32)
            dmix = dav * _gelu(zu)
            dmix_b = dmix.astype(BF16)
            mixed = jnp.dot(ws_ref[g], vn, preferred_element_type=F32) + b_ref[g]
            dz_ref[:, sl] = (dav * mixed * _gelu_grad(zu)).astype(BF16)
            dvn = jnp.dot(wst_ref[g], dmix_b, preferred_element_type=F32)
            dws_g = lax.dot_general(dmix_b, vn, NT, preferred_element_type=F32)
            dbs_g = jnp.sum(dmix, axis=1, keepdims=True)
            pg = _rows8(dvn * xh)

            @pl.when(c == 0)
            def _():
                dws_ref[g] = dws_g
                dbs_ref[g] = dbs_g
                acc_gain[:, sl] = pg

            @pl.when(c > 0)
            def _():
                dws_ref[g] += dws_g
                dbs_ref[g] += dbs_g
                acc_gain[:, sl] += pg

            dxh = dvn * gain_g
            dxh_s[:, sl] = dxh
            dot_dx = dot_dx + jnp.sum(dxh * xh, axis=-1, keepdims=True)

        mean_dx = dot_dx * (1.0 / w_sgu)
        for g, sl in enumerate(cols):
            dvv = r * (dxh_s[:, sl] - (vv_s[:, sl] * r) * mean_dx)
            dz_ref[:, w_sgu + g * BLK : w_sgu + (g + 1) * BLK] = (dvv * gv_s[:, sl]).astype(BF16)

        @pl.when(c == n - 1)
        def _():
            dgain_ref[...] = jnp.sum(acc_gain[...], axis=0, keepdims=True)

    full3 = pl.BlockSpec((groups, BLK, BLK), lambda c: (0, 0, 0))
    col3 = pl.BlockSpec((groups, BLK, 1), lambda c: (0, 0, 0))
    vec = pl.BlockSpec((1, w_sgu), lambda c: (0, 0))
    return _pallas(
        body,
        name="sgu_bwd",
        grid=(n,),
        in_specs=[
            pl.BlockSpec((BLK, w_sgu), lambda c: (c, 0)),
            pl.BlockSpec((BLK, w_sgu), lambda c: (c, 1)),
            pl.BlockSpec((BLK, w_sgu), lambda c: (c, 0)),
            vec,
            full3,
            full3,
            col3,
            ANY,
        ]
        + [ANY] * len(after),
        out_specs=[pl.BlockSpec((BLK, 2 * w_sgu), lambda c: (c, 0)), full3, col3, vec],
        out_shape=[
            jax.ShapeDtypeStruct(dz.shape, BF16),
            jax.ShapeDtypeStruct((groups, BLK, BLK), F32),
            jax.ShapeDtypeStruct((groups, BLK, 1), F32),
            jax.ShapeDtypeStruct((1, w_sgu), F32),
        ],
        scratch_shapes=[pltpu.VMEM((8, w_sgu), F32)] + [pltpu.VMEM((BLK, w_sgu), F32)] * 3,
        input_output_aliases={7: 0},
    )(z, z, da, gain, ws_b, wst_b, b_col, dz, *after)


def _attn_softmax(sink_ref, q_ref, k_ref, v_ref, bias_ref, s_len, grp):
    kv = pl.program_id(0)
    n = pl.program_id(1)
    start = pl.multiple_of(n * BLK, BLK)
    kb = k_ref[pl.ds(start, 3 * BLK), :]
    vb = v_ref[pl.ds(start, 3 * BLK), :]
    qv = q_ref[...]
    qs = jnp.concatenate([qv[:, g * HEAD_DIM : (g + 1) * HEAD_DIM] for g in range(grp)], axis=0).astype(BF16)
    sc = lax.dot_general(qs, kb, NT, preferred_element_type=F32) * (HEAD_DIM**-0.5)
    sc = sc + bias_ref[...].reshape(grp * BLK, 3 * BLK)
    kpos = start + lax.broadcasted_iota(I32, (1, 3 * BLK), 1) - BLK
    sc = jnp.where((kpos >= 0) & (kpos < s_len), sc, NEG)
    sink = jnp.concatenate([jnp.full((BLK, 1), sink_ref[kv * grp + g], F32) for g in range(grp)], axis=0)
    m = jnp.maximum(jnp.max(sc, axis=-1, keepdims=True), sink)
    p = jnp.exp(sc - m)
    esink = jnp.exp(sink - m)
    den = jnp.sum(p, axis=-1, keepdims=True) + esink
    return start, qs, kb, vb, p / den, esink / den


def _attn_specs(s, grp, q_blk0):
    qw = grp * HEAD_DIM
    return [
        pl.BlockSpec(memory_space=pltpu.SMEM),
        pl.BlockSpec((BLK, qw), lambda kv, n: (n, q_blk0 + kv)),
        pl.BlockSpec((s + 2 * BLK, HEAD_DIM), lambda kv, n: (0, kv)),
        pl.BlockSpec((s + 2 * BLK, HEAD_DIM), lambda kv, n: (0, kv)),
        pl.BlockSpec((grp, BLK, 3 * BLK), lambda kv, n: (kv, 0, 0)),
    ]


def _attn_fwd(sink, z, k_pad, v_pad, bias_tab, grp, q_blk0):
    s = z.shape[0]
    qw = grp * HEAD_DIM

    def body(sink_ref, q_ref, k_ref, v_ref, bias_ref, o_ref):
        _, _, _, vb, pn, _ = _attn_softmax(sink_ref, q_ref, k_ref, v_ref, bias_ref, s, grp)
        o = jnp.dot(pn.astype(BF16), vb, preferred_element_type=F32)
        for g in range(grp):
            o_ref[:, g * HEAD_DIM : (g + 1) * HEAD_DIM] = o[g * BLK : (g + 1) * BLK].astype(BF16)

    return _pallas(
        body,
        name="attn_fwd",
        grid=(N_KV_HEADS, s // BLK),
        in_specs=_attn_specs(s, grp, q_blk0),
        out_specs=pl.BlockSpec((BLK, qw), lambda kv, n: (n, kv)),
        out_shape=jax.ShapeDtypeStruct((s, N_KV_HEADS * qw), BF16),
    )(sink, z, k_pad, v_pad, bias_tab)


def _attn_bwd(sink, z, k_pad, v_pad, bias_tab, dout, dz, grp, q_blk0):
    s = z.shape[0]
    qw = grp * HEAD_DIM
    nb = s // BLK
    heads = N_KV_HEADS * grp

    def body(sink_ref, q_ref, k_ref, v_ref, bias_ref, do_ref, dz_in, dq_ref, dk_ref, dv_ref, dbias_ref, dsink_ref, dk_acc, dv_acc):
        del dz_in
        kv = pl.program_id(0)
        n = pl.program_id(1)
        start, qs, kb, vb, pn, psink = _attn_softmax(sink_ref, q_ref, k_ref, v_ref, bias_ref, s, grp)
        dov = do_ref[...]
        dos = jnp.concatenate([dov[:, g * HEAD_DIM : (g + 1) * HEAD_DIM] for g in range(grp)], axis=0)
        dp = lax.dot_general(dos, vb, NT, preferred_element_type=F32)
        dvb = lax.dot_general(pn.astype(BF16), dos, TN, preferred_element_type=F32)
        delta = jnp.sum(pn * dp, axis=-1, keepdims=True)
        ds = pn * (dp - delta)
        dsb = (ds * (HEAD_DIM**-0.5)).astype(BF16)
        dq = jnp.dot(dsb, kb, preferred_element_type=F32)
        dkb = lax.dot_general(dsb, qs, TN, preferred_element_type=F32)
        for g in range(grp):
            dq_ref[:, g * HEAD_DIM : (g + 1) * HEAD_DIM] = dq[g * BLK : (g + 1) * BLK].astype(BF16)

        @pl.when(n == 0)
        def _():
            dk_acc[...] = jnp.zeros_like(dk_acc)
            dv_acc[...] = jnp.zeros_like(dv_acc)
            dbias_ref[...] = jnp.zeros_like(dbias_ref)

        @pl.when((n == 0) & (kv == 0))
        def _():
            dsink_ref[...] = jnp.zeros_like(dsink_ref)

        dk_acc[pl.ds(start, 3 * BLK), :] += dkb
        dv_acc[pl.ds(start, 3 * BLK), :] += dvb
        dbias_ref[...] += ds.reshape(grp, BLK, 3 * BLK)
        row = lax.broadcasted_iota(I32, (heads, LANES), 0)
        sd = psink * delta
        upd = jnp.zeros((heads, LANES), F32)
        for g in range(grp):
            upd = jnp.where(row == kv * grp + g, -jnp.sum(sd[g * BLK : (g + 1) * BLK]), upd)
        dsink_ref[...] += upd

        @pl.when(n == nb - 1)
        def _():
            dk_ref[...] = dk_acc[...]
            dv_ref[...] = dv_acc[...]

    pad_spec = pl.BlockSpec((s + 2 * BLK, HEAD_DIM), lambda kv, n: (0, kv))
    kvw = N_KV_HEADS * HEAD_DIM
    return _pallas(
        body,
        name="attn_bwd",
        grid=(N_KV_HEADS, nb),
        in_specs=_attn_specs(s, grp, q_blk0) + [pl.BlockSpec((BLK, qw), lambda kv, n: (n, kv)), ANY],
        out_specs=[
            pl.BlockSpec((BLK, qw), lambda kv, n: (n, q_blk0 + kv)),
            pad_spec,
            pad_spec,
            pl.BlockSpec((grp, BLK, 3 * BLK), lambda kv, n: (kv, 0, 0)),
            pl.BlockSpec((heads, LANES), lambda kv, n: (0, 0)),
        ],
        out_shape=[
            jax.ShapeDtypeStruct(dz.shape, BF16),
            jax.ShapeDtypeStruct((s + 2 * BLK, kvw), F32),
            jax.ShapeDtypeStruct((s + 2 * BLK, kvw), F32),
            jax.ShapeDtypeStruct((heads, BLK, 3 * BLK), F32),
            jax.ShapeDtypeStruct((heads, LANES), F32),
        ],
        scratch_shapes=[pltpu.VMEM((s + 2 * BLK, HEAD_DIM), F32), pltpu.VMEM((s + 2 * BLK, HEAD_DIM), F32)],
        input_output_aliases={6: 0},
    )(sink, z, k_pad, v_pad, bias_tab, dout, dz)


def _dkv_to_dz(dk_pad, dv_pad, dz, blk_idx):
    s = dz.shape[0]
    kvw = dk_pad.shape[1]

    def body(dk_ref, dv_ref, dz_in, out_ref):
        del dz_in
        out_ref[:, :kvw] = dk_ref[...].astype(BF16)
        out_ref[:, kvw:] = dv_ref[...].astype(BF16)

    src = pl.BlockSpec((BLK, kvw), lambda i: (i + 1, 0))
    return _pallas(
        body,
        name="dkv_to_dz",
        grid=(s // BLK,),
        in_specs=[src, src, ANY],
        out_specs=pl.BlockSpec((BLK, 2 * kvw), lambda i: (i, blk_idx)),
        out_shape=jax.ShapeDtypeStruct(dz.shape, BF16),
        input_output_aliases={2: 0},
    )(dk_pad, dv_pad, dz)


def _relbias_bwd(dbias_tab, bucket):
    heads = dbias_tab.shape[0]

    def body(dt_ref, bk_ref, out_ref):
        lane = lax.broadcasted_iota(I32, (1, LANES), 1)
        bk = bk_ref[...]
        rows = []
        for h in range(heads):
            dt = dt_ref[h]
            acc = jnp.zeros((1, LANES), F32)
            for b in range(REL_BUCKETS):
                acc = jnp.where(lane == b, jnp.sum(jnp.where(bk == b, dt, 0.0)), acc)
            rows.append(acc)
        out_ref[...] = jnp.concatenate(rows, axis=0)

    return _pallas(body, name="relbias_bwd", out_shape=jax.ShapeDtypeStruct((heads, LANES), F32))(dbias_tab, bucket)


def _t5_bucket(rel):
    nb = REL_BUCKETS // 2
    ret = jnp.where(rel > 0, nb, 0)
    n = jnp.abs(rel)
    max_exact = nb // 2
    nf = jnp.maximum(n, 1).astype(F32)
    large = max_exact + (jnp.log(nf / max_exact) / math.log(REL_MAX_DIST / max_exact) * (nb - max_exact)).astype(I32)
    large = jnp.minimum(large, nb - 1)
    return ret + jnp.where(n < max_exact, n, large)


def _band_tables(rel_bias):
    qi = jnp.arange(BLK)[:, None]
    kj = jnp.arange(3 * BLK)[None, :]
    rel = kj - BLK - qi
    bucket = _t5_bucket(rel).astype(I32)
    heads = rel_bias.shape[1]
    masked = jnp.where(jnp.abs(rel) <= BLK, bucket, -1)

    def body(rb_ref, bk_ref, out_ref):
        bk = bk_ref[...]
        for h in range(heads):
            tab = jnp.full(bk.shape, NEG, F32)
            for b in range(REL_BUCKETS):
                tab = jnp.where(bk == b, rb_ref[b, h], tab)
            out_ref[h] = tab

    bias_tab = _pallas(
        body,
        name="bias_table",
        in_specs=[pl.BlockSpec(memory_space=pltpu.SMEM), pl.BlockSpec(memory_space=pltpu.VMEM)],
        out_specs=pl.BlockSpec(memory_space=pltpu.VMEM),
        out_shape=jax.ShapeDtypeStruct((heads, BLK, 3 * BLK), F32),
    )(rel_bias.astype(F32), masked)
    return bias_tab, bucket


EW_BLOCK_ELEMS = 512 * 1024


def _ew_tiles(shape, elems=EW_BLOCK_ELEMS // 2):
    r, c = shape
    tn = c if c <= 2048 else _tile(c, (2048, 1920, 1536, 1408, 1024, 512))
    tm = _tile(r, [t for t in (1024, 512, 256, 128, 64, 32, 16, 8) if t * tn <= elems] or [8])
    return tm, tn


def _cast_into_full(name, qidx, w, kind, after=()):
    r, c = w.shape
    tm, tn = _ew_tiles(w.shape, EW_BLOCK_ELEMS)
    nbi, nbj = r // tm, c // tn
    if kind == "col":
        full, out_spec = (r, c * N_CHIPS), pl.BlockSpec((tm, tn), lambda i, j, q: (i, q[0] * nbj + j))
    else:
        full, out_spec = (r * N_CHIPS, c), pl.BlockSpec((tm, tn), lambda i, j, q: (q[0] * nbi + i, j))

    def body(q_ref, w_ref, *rest):
        del q_ref
        rest[-1][...] = w_ref[...].astype(BF16)

    return _pallas(
        body,
        name=name,
        grid_spec=pltpu.PrefetchScalarGridSpec(
            num_scalar_prefetch=1,
            grid=(nbi, nbj),
            in_specs=[pl.BlockSpec((tm, tn), lambda i, j, q: (i, j))] + [ANY] * len(after),
            out_specs=out_spec,
        ),
        out_shape=jax.ShapeDtypeStruct(full, BF16),
    )(qidx, w, *after)


def _adamw(name, w, g, m, v, after=()):
    tm, tn = _ew_tiles(w.shape, EW_BLOCK_ELEMS)
    if _nbytes(w.shape, F32) <= 1024 * 1024:
        tm, tn = w.shape
    spec = pl.BlockSpec((tm, tn), lambda i, j: (i, j))
    n_after = len(after)

    def body(w_ref, g_ref, m_ref, v_ref, *rest):
        d_ref, nm_ref, nv_ref, g_out_ref = rest[n_after:]
        gv = g_ref[...]
        g_out_ref[...] = gv
        nm = ADAM_B1 * m_ref[...] + (1.0 - ADAM_B1) * gv
        nv = ADAM_B2 * v_ref[...] + (1.0 - ADAM_B2) * (gv * gv)
        m_hat = nm / (1.0 - ADAM_B1**ADAM_STEP)
        v_hat = nv / (1.0 - ADAM_B2**ADAM_STEP)
        d_ref[...] = -ADAM_LR * (m_hat / (jnp.sqrt(v_hat) + ADAM_EPS) + ADAM_WD * w_ref[...])
        nm_ref[...] = nm
        nv_ref[...] = nv

    out = jax.ShapeDtypeStruct(w.shape, F32)
    return _pallas(
        body, name=name, grid=(w.shape[0] // tm, w.shape[1] // tn), in_specs=[spec] * 4 + [ANY] * n_after,
        out_specs=[spec] * 4, out_shape=[out, out, out, out],
        compiler_params=_params(_mm_vmem([((tm, tn), F32, 24)])),
    )(w, g, m, v, *after)


def _pair_add(name, cidx, g_full, r_sib, kind):
    hr, hc = r_sib.shape
    tm, tn = _ew_tiles((hr, hc), 2 * EW_BLOCK_ELEMS)
    nbi, nbj = hr // tm, hc // tn
    if kind == "col":
        g_spec = pl.BlockSpec((tm, tn), lambda i, j, c: (c[0] * nbi + i, j))
    else:
        g_spec = pl.BlockSpec((tm, tn), lambda i, j, c: (i, c[0] * nbj + j))
    spec = pl.BlockSpec((tm, tn), lambda i, j, c: (i, j))

    def body(c_ref, g_ref, r_ref, o_ref):
        del c_ref
        o_ref[...] = (g_ref[...].astype(F32) + r_ref[...].astype(F32)).astype(BF16)

    return _pallas(
        body,
        name=name,
        grid_spec=pltpu.PrefetchScalarGridSpec(num_scalar_prefetch=1, grid=(nbi, nbj), in_specs=[g_spec, spec], out_specs=spec),
        out_shape=jax.ShapeDtypeStruct((hr, hc), BF16),
        compiler_params=_params(_mm_vmem([((tm, tn), BF16, 6), ((tm, tn), F32, 3)])),
    )(cidx, g_full, r_sib)


def _chip_sum(name, qidx, c_half, r_ici, kind):
    _, pr, pc = r_ici.shape
    tm, tn = _ew_tiles((pr, pc), 2 * EW_BLOCK_ELEMS)
    nbi, nbj = pr // tm, pc // tn
    if kind == "col":
        own_spec = pl.BlockSpec((tm, tn), lambda i, j, q: (i, q[0] * nbj + j))
        full, out_spec = (2 * pr, pc), pl.BlockSpec((tm, tn), lambda i, j, q: (q[1] * nbi + i, j))
    else:
        own_spec = pl.BlockSpec((tm, tn), lambda i, j, q: (q[0] * nbi + i, j))
        full, out_spec = (pr, 2 * pc), pl.BlockSpec((tm, tn), lambda i, j, q: (i, q[1] * nbj + j))

    def body(q_ref, own_ref, r_ref, o_ref):
        q = q_ref[0]
        own = own_ref[...].astype(F32)
        recv = [r_ref[r].astype(F32) for r in range(3)]
        total = None
        for chip in range(N_CHIPS):
            d = chip ^ q
            term = jnp.where(d == 0, own, jnp.where(d == 2, recv[0], jnp.where(d == 1, recv[1], recv[2])))
            total = term if total is None else total + term
        o_ref[...] = total

    return _pallas(
        body,
        name=name,
        grid_spec=pltpu.PrefetchScalarGridSpec(
            num_scalar_prefetch=1,
            grid=(nbi, nbj),
            in_specs=[own_spec, pl.BlockSpec((3, tm, tn), lambda i, j, q: (0, i, j))],
            out_specs=out_spec,
        ),
        out_shape=jax.ShapeDtypeStruct(full, F32),
        compiler_params=_params(_mm_vmem([((tm, tn), BF16, 8), ((tm, tn), F32, 6)])),
    )(qidx, c_half, r_ici)


_REL_MASK = (2, 1, 3)


def _place():
    x, y, c = lax.axis_index("x"), lax.axis_index("y"), lax.axis_index("c")
    chips = [(1 - x, y), (x, 1 - y), (1 - x, 1 - y)]
    return x, y, c, 2 * x + y, chips


def _shard_view(ref, kind, chip):
    if kind == "col":
        w = ref.shape[1] // N_CHIPS
        return ref.at[:, pl.ds(pl.multiple_of(chip * w, LANES), w)]
    h = ref.shape[0] // N_CHIPS
    return ref.at[pl.ds(pl.multiple_of(chip * h, 16), h), :]


def _row_half(ref, half):
    h = ref.shape[0] // 2
    return ref.at[pl.ds(pl.multiple_of(half * h, 16), h), :]


def _pair_half(ref, kind, half):
    if kind == "col":
        return _row_half(ref, half)
    w = ref.shape[1] // 2
    return ref.at[:, pl.ds(pl.multiple_of(half * w, LANES), w)]


def _remote(src, dst, send_sem, recv_sem, dev):
    return pltpu.make_async_remote_copy(src_ref=src, dst_ref=dst, send_sem=send_sem, recv_sem=recv_sem, device_id=dev, device_id_type=MESH)


def _hbm(a):
    return pltpu.with_memory_space_constraint(a, pltpu.HBM)


def _gather_start(name, fulls, kinds, rels=(0, 1, 2), after=()):
    n_w = len(fulls)

    def body(*refs):
        g = refs[:n_w]
        send_sem, recv_sem = refs[n_w + len(after)], refs[n_w + len(after) + 1]
        token = refs[-1]
        _, _, c, q, chips = _place()
        for w in range(n_w):
            mine = _row_half(_shard_view(g[w], kinds[w], q), c)
            for r in rels if isinstance(rels, tuple) else rels[w]:
                _remote(mine, mine, send_sem.at[3 * w + r], recv_sem.at[3 * w + r], (*chips[r], c)).start()
        token[...] = jnp.zeros_like(token)

    res = _pallas(
        body,
        name=name,
        out_shape=(
            pltpu.SemaphoreType.DMA((3 * n_w,)),
            pltpu.SemaphoreType.DMA((3 * n_w,)),
            *[pltpu.HBM(f.shape, f.dtype) for f in fulls],
            jax.ShapeDtypeStruct((8, LANES), F32),
        ),
        in_specs=[HBM_SPEC] * n_w + [ANY] * len(after),
        out_specs=(SEM_SPEC, SEM_SPEC, *[HBM_SPEC] * n_w, pl.BlockSpec(memory_space=pltpu.VMEM)),
        input_output_aliases={w: w + 2 for w in range(n_w)},
        compiler_params=pltpu.CompilerParams(has_side_effects=EFFECT),
    )(*[_hbm(f) for f in fulls], *after)
    return res[0], res[1], list(res[2 : 2 + n_w]), res[-1]


def _relay_copies(kinds, waiting):
    def copies(refs, send_sem, recv_sem):
        _, _, c, q, chips = _place()
        out = []
        for i, kind in enumerate(kinds):
            for k, (src_rel, dst_rel) in enumerate(((0, 1), (1, 0))):
                held = _row_half(_row_half(_shard_view(refs[i], kind, q ^ _REL_MASK[src_rel]), c), k)
                far = _row_half(_row_half(_shard_view(refs[i], kind, q ^ _REL_MASK[2]), c), k)
                dst = far if waiting else held
                out.append(_remote(held, dst, send_sem.at[2 * i + k], recv_sem.at[2 * i + k], (*chips[dst_rel], c)))
        return out

    return copies


def _forward_copies(kinds, waiting):
    def copies(refs, send_sem, recv_sem):
        x, y, c, q, _ = _place()
        out = []
        for i, kind in enumerate(kinds):
            for r in range(3):
                quarter = _shard_view(refs[i], kind, q ^ _REL_MASK[r])
                landed = _row_half(quarter, c)
                dst = _row_half(quarter, 1 - c) if waiting else landed
                out.append(_remote(landed, dst, send_sem.at[3 * i + r], recv_sem.at[3 * i + r], (x, y, 1 - c)))
        return out

    return copies


def _gather_wait(name, fulls, kinds, w_ids, send_sem, recv_sem, after, rels=(0, 1, 2)):
    n = len(fulls)

    def body(*refs):
        g = refs[:n]
        s_sem, r_sem = refs[n], refs[n + 1]
        x, y, c, q, _ = _place()
        for i, w in enumerate(w_ids):
            mine = _row_half(_shard_view(g[i], kinds[i], q), c)
            for r in rels:
                landed = _row_half(_shard_view(g[i], kinds[i], q ^ _REL_MASK[r]), c)
                cp = _remote(mine, landed, s_sem.at[3 * w + r], r_sem.at[3 * w + r], (x, y, 1 - c))
                cp.wait_send()
                cp.wait_recv()

    res = _pallas(
        body,
        name=name,
        out_shape=[pltpu.HBM(f.shape, f.dtype) for f in fulls],
        in_specs=[HBM_SPEC] * n + [SEM_SPEC, SEM_SPEC, ANY],
        out_specs=[HBM_SPEC] * n,
        input_output_aliases={i: i for i in range(n)},
        compiler_params=pltpu.CompilerParams(has_side_effects=EFFECT),
    )(*fulls, send_sem, recv_sem, after)
    return list(res)


def _gather_forward(name, fulls, kinds):
    n = len(fulls)

    def body(*refs):
        g = refs[n : 2 * n]
        send, recv = refs[2 * n :]
        x, y, c, q, _ = _place()
        sib = (x, y, 1 - c)
        cps = []
        for i in range(n):
            for r in range(3):
                landed = _row_half(_shard_view(g[i], kinds[i], q ^ _REL_MASK[r]), c)
                cps.append(_remote(landed, landed, send.at[i, r], recv.at[i, r], sib))
        for cp in cps:
            cp.start()
        for i in range(n):
            for r in range(3):
                other = _row_half(_shard_view(g[i], kinds[i], q ^ _REL_MASK[r]), 1 - c)
                _remote(other, other, send.at[i, r], recv.at[i, r], sib).wait_recv()
        for cp in cps:
            cp.wait_send()

    res = _pallas(
        body,
        name=name,
        in_specs=[ANY] * n,
        out_specs=[ANY] * n,
        out_shape=[jax.ShapeDtypeStruct(f.shape, f.dtype) for f in fulls],
        scratch_shapes=[pltpu.SemaphoreType.DMA((n, 3)), pltpu.SemaphoreType.DMA((n, 3))],
        input_output_aliases={i: i for i in range(n)},
    )(*fulls)
    return list(res)


SIBLING_BARRIER_ID = 1


def _split_start(name, bufs, n_sems, copies, sibling_only=False):
    n = len(bufs)

    def body(*refs):
        if sibling_only:
            barrier = pltpu.get_barrier_semaphore()
            sib = (lax.axis_index("x"), lax.axis_index("y"), 1 - lax.axis_index("c"))
            pl.semaphore_signal(barrier, inc=1, device_id=sib, device_id_type=MESH)
            pl.semaphore_wait(barrier, 1)
        for cp in copies(refs[:n], refs[n], refs[n + 1]):
            cp.start()

    extra = {"collective_id": SIBLING_BARRIER_ID} if sibling_only else {}

    res = _pallas(
        body,
        name=name,
        out_shape=(
            pltpu.SemaphoreType.DMA((n_sems,)),
            pltpu.SemaphoreType.DMA((n_sems,)),
            *[pltpu.HBM(b.shape, b.dtype) for b in bufs],
        ),
        in_specs=[HBM_SPEC] * n,
        out_specs=(SEM_SPEC, SEM_SPEC, *[HBM_SPEC] * n),
        input_output_aliases={i: i + 2 for i in range(n)},
        compiler_params=pltpu.CompilerParams(has_side_effects=EFFECT, **extra),
    )(*[_hbm(b) for b in bufs])
    return res[0], res[1], list(res[2:])


def _split_wait(name, bufs, send_sem, recv_sem, copies, after):
    n = len(bufs)

    def body(*refs):
        for cp in copies(refs[:n], refs[n], refs[n + 1]):
            cp.wait_send()
            cp.wait_recv()

    res = _pallas(
        body,
        name=name,
        out_shape=[pltpu.HBM(b.shape, b.dtype) for b in bufs],
        in_specs=[HBM_SPEC] * n + [SEM_SPEC, SEM_SPEC, ANY],
        out_specs=[HBM_SPEC] * n,
        input_output_aliases={i: i for i in range(n)},
        compiler_params=pltpu.CompilerParams(has_side_effects=EFFECT),
    )(*bufs, send_sem, recv_sem, after)
    return list(res)


def _pair_exchange_copies(kinds):
    n = len(kinds)

    def copies(refs, send_sem, recv_sem):
        x, y, c, _, _ = _place()
        return [
            _remote(_pair_half(refs[w], kinds[w], 1 - c), refs[n + w], send_sem.at[w], recv_sem.at[w], (x, y, 1 - c))
            for w in range(n)
        ]

    return copies


def _pair_share_copies(kinds, waiting):
    def copies(refs, send_sem, recv_sem):
        x, y, c, _, _ = _place()
        out = []
        for w, kind in enumerate(kinds):
            mine = _pair_half(refs[w], kind, c)
            dst = _pair_half(refs[w], kind, 1 - c) if waiting else mine
            out.append(_remote(mine, dst, send_sem.at[w], recv_sem.at[w], (x, y, 1 - c)))
        return out

    return copies


def _piece_shape(half_shape, kind):
    r, c = half_shape
    return (3, r, c // N_CHIPS) if kind == "col" else (3, r // N_CHIPS, c)


def _chip_send_start(name, halves, kinds):
    n = len(halves)
    lands = [lax.empty(_piece_shape(h.shape, k), BF16) for h, k in zip(halves, kinds)]

    def body(*refs):
        h, land = refs[:n], refs[n : 2 * n]
        send_sem, recv_sem = refs[2 * n], refs[2 * n + 1]
        _, _, c, q, chips = _place()
        for i in range(n):
            for r, chip in enumerate(chips):
                piece = _shard_view(h[i], kinds[i], q ^ _REL_MASK[r])
                _remote(piece, land[i].at[r], send_sem.at[3 * i + r], recv_sem.at[3 * i + r], (*chip, c)).start()

    res = _pallas(
        body,
        name=name,
        out_shape=(
            pltpu.SemaphoreType.DMA((3 * n,)),
            pltpu.SemaphoreType.DMA((3 * n,)),
            *[pltpu.HBM(a.shape, a.dtype) for a in halves],
            *[pltpu.HBM(a.shape, a.dtype) for a in lands],
        ),
        in_specs=[HBM_SPEC] * (2 * n),
        out_specs=(SEM_SPEC, SEM_SPEC, *[HBM_SPEC] * (2 * n)),
        input_output_aliases={i: i + 2 for i in range(2 * n)},
        compiler_params=pltpu.CompilerParams(has_side_effects=EFFECT),
    )(*[_hbm(a) for a in halves], *[_hbm(a) for a in lands])
    return res[0], res[1], list(res[2 : 2 + n]), list(res[2 + n :])


def _chip_send_wait(name, halves, lands, kinds, send_sem, recv_sem, after):
    n = len(halves)

    def body(*refs):
        h, land = refs[:n], refs[n : 2 * n]
        s_sem, r_sem = refs[2 * n], refs[2 * n + 1]
        x, y, c, q, _ = _place()
        for i in range(n):
            for r in range(3):
                piece = _shard_view(h[i], kinds[i], q ^ _REL_MASK[r])
                cp = _remote(piece, land[i].at[r], s_sem.at[3 * i + r], r_sem.at[3 * i + r], (x, y, 1 - c))
                cp.wait_send()
                cp.wait_recv()

    res = _pallas(
        body,
        name=name,
        out_shape=[pltpu.HBM(a.shape, a.dtype) for a in halves] + [pltpu.HBM(a.shape, a.dtype) for a in lands],
        in_specs=[HBM_SPEC] * (2 * n) + [SEM_SPEC, SEM_SPEC, ANY],
        out_specs=[HBM_SPEC] * (2 * n),
        input_output_aliases={i: i for i in range(2 * n)},
        compiler_params=pltpu.CompilerParams(has_side_effects=EFFECT),
    )(*halves, *lands, send_sem, recv_sem, after)
    return list(res[:n]), list(res[n:])


def _small_exchange_copies(waiting):
    def copies(refs, send_sem, recv_sem):
        p, land = refs
        x, y, c, q, _ = _place()
        me = 2 * q + c
        out = []
        for dd in range(1, 2 * N_CHIPS):
            dev = (x ^ ((dd >> 2) & 1), y ^ ((dd >> 1) & 1), c ^ (dd & 1))
            dst = land.at[me ^ dd] if waiting else land.at[me]
            out.append(_remote(p, dst, send_sem.at[dd - 1], recv_sem.at[dd - 1], dev))
        return out

    return copies


def _small_sum(name, me_idx, p, land):
    rows = p.shape[0]
    n_dev = 2 * N_CHIPS

    def body(me_ref, p_ref, land_ref, o_ref):
        me = me_ref[0]
        total = None
        for dev in range(n_dev):
            term = jnp.where(me == dev, p_ref[...], land_ref[dev])
            total = term if total is None else total + term
        o_ref[...] = total

    return _pallas(
        body,
        name=name,
        grid_spec=pltpu.PrefetchScalarGridSpec(
            num_scalar_prefetch=1,
            grid=(1,),
            in_specs=[pl.BlockSpec((rows, LANES), lambda i, m: (0, 0)), pl.BlockSpec((n_dev, rows, LANES), lambda i, m: (0, 0, 0))],
            out_specs=pl.BlockSpec((rows, LANES), lambda i, m: (0, 0)),
        ),
        out_shape=jax.ShapeDtypeStruct(p.shape, F32),
    )(me_idx, p, land)


def _pack(parts):
    rows = []
    for a in parts:
        flat = a.reshape(-1).astype(F32)
        n = flat.shape[0]
        padded = -(-n // (8 * LANES)) * (8 * LANES)
        rows.append(jnp.pad(flat, (0, padded - n)).reshape(-1, LANES))
    return jnp.concatenate(rows, axis=0)


def _unpack(packed, shapes):
    out, row = [], 0
    for shp in shapes:
        n = int(np.prod(shp))
        nrows = -(-n // (8 * LANES)) * 8
        out.append(packed[row : row + nrows].reshape(-1)[:n].reshape(shp))
        row += nrows
    return out


def kernel(x, w_in, norm_mix, sgu_v_gain, sgu_w_s, sgu_b_s, w_a_out, attn_sink, rel_bias, w_b_out, w_o, norm_ffn, w_gate, w_up, w_down, norm_final, loss_target, m_w_in, m_norm_mix, m_sgu_v_gain, m_sgu_w_s, m_sgu_b_s, m_w_a_out, m_attn_sink, m_rel_bias, m_w_b_out, m_w_o, m_norm_ffn, m_w_gate, m_w_up, m_w_down, m_norm_final, v_w_in, v_norm_mix, v_sgu_v_gain, v_sgu_w_s, v_sgu_b_s, v_w_a_out, v_attn_sink, v_rel_bias, v_w_b_out, v_w_o, v_norm_ffn, v_w_gate, v_w_up, v_w_down, v_norm_final):
    s, d = x.shape[1], x.shape[2]
    w_sgu = sgu_v_gain.shape[1]
    groups = sgu_w_s.shape[1]
    heads = attn_sink.shape[1]
    grp = heads // N_KV_HEADS
    w_att = heads * HEAD_DIM
    w_kv = N_KV_HEADS * HEAD_DIM
    d_ff = w_gate.shape[2] * N_CHIPS
    n_in = w_in.shape[2] * N_CHIPS
    off_q = 2 * w_sgu
    off_k = off_q + w_att
    off_g = off_k + 2 * w_kv
    assert n_in == off_g + 2 * d and groups * BLK == w_sgu and s % BLK == 0

    x2d = x.reshape(s, d)
    tgt = loss_target.reshape(s, d)
    c_idx = lax.axis_index("c").astype(I32).reshape(1)
    q_idx = (2 * lax.axis_index("x") + lax.axis_index("y")).astype(I32).reshape(1)
    qc_idx = jnp.concatenate([q_idx, c_idx])

    W_IN, W_A, W_B, W_O, W_GATE, W_UP, W_DOWN = range(7)
    names = ["w_in", "w_a", "w_b", "w_o", "w_gate", "w_up", "w_down"]
    kinds = ["col", "col", "col", "row", "col", "col", "row"]
    big_w = [w_in[0], w_a_out[0], w_b_out[0], w_o[0], w_gate[0], w_up[0], w_down[0]]
    big_m = [m_w_in[0], m_w_a_out[0], m_w_b_out[0], m_w_o[0], m_w_gate[0], m_w_up[0], m_w_down[0]]
    big_v = [v_w_in[0], v_w_a_out[0], v_w_b_out[0], v_w_o[0], v_w_gate[0], v_w_up[0], v_w_down[0]]
    full_in = _cast_into_full("cast_w_in", q_idx, big_w[W_IN], kinds[W_IN])
    in_send, in_recv, (full_in,), token = _gather_start("gather_start_in", [full_in], [kinds[W_IN]], rels=(0, 1))
    rest = [_cast_into_full("cast_" + names[i], q_idx, big_w[i], kinds[i], after=(token,)) for i in range(1, 7)]
    h1 = _rms_fwd("rms_mix", x2d, norm_mix, after=(token,))
    (full_in,) = _gather_wait("gather_wait_in", [full_in], [kinds[W_IN]], [0], in_send, in_recv, h1, rels=(0, 1))
    relay_send, relay_recv, (full_in,) = _split_start("gather_relay_in", [full_in], 2, _relay_copies([kinds[W_IN]], False))
    full_down = rest.pop()
    ag_send, ag_recv, rest, token = _gather_start("gather_start_rest", rest, kinds[1:6], rels=(0, 1), after=(full_in,))
    (full_in,) = _split_wait("gather_relay_wait_in", [full_in], relay_send, relay_recv, _relay_copies([kinds[W_IN]], True), token)
    (g_in,) = _gather_forward("gather_fwd_in", [full_in], [kinds[W_IN]])
    fulls = [g_in] + rest

    def relay_begin(tag, ids, after):
        ks = [kinds[i] for i in ids]
        bufs = _gather_wait("gather_wait_" + tag, [fulls[i] for i in ids], ks, [i - 1 for i in ids], ag_send, ag_recv, after,
                            rels=(0, 1))
        send, recv, bufs = _split_start("gather_relay_" + tag, bufs, 2 * len(ids), _relay_copies(ks, False))
        return tag, ks, send, recv, bufs

    def relay_end(state, after):
        tag, ks, send, recv, bufs = state
        bufs = _split_wait("gather_relay_wait_" + tag, bufs, send, recv, _relay_copies(ks, True), after)
        return _gather_forward("gather_fwd_" + tag, bufs, ks)

    def relay_end_async(state, after):
        tag, ks, send, recv, bufs = state
        bufs = _split_wait("gather_relay_wait_" + tag, bufs, send, recv, _relay_copies(ks, True), after)
        send, recv, bufs = _split_start("gather_fwd_start_" + tag, bufs, 3 * len(ks), _forward_copies(ks, False), sibling_only=True)
        return tag, ks, send, recv, bufs

    def forwarded(state, after):
        tag, ks, send, recv, bufs = state
        return _split_wait("gather_fwd_wait_" + tag, bufs, send, recv, _forward_copies(ks, True), after)

    ws_b = sgu_w_s[0].astype(BF16)
    wst_b = jnp.swapaxes(sgu_w_s[0], 1, 2).astype(BF16)
    b_col = sgu_b_s[0].reshape(groups, BLK, 1)
    bias_tab, bucket = _band_tables(rel_bias)
    sink = attn_sink[0]

    tm = _tile(s, (1024, 512, 256, 128))

    tn = _tile(n_in, (768, 640, 512))
    z = _mm(
        "mm_z", (s // tm, n_in // tn, 1), [h1, g_in],
        [pl.BlockSpec((tm, d), lambda i, j, k: (i, 0)), pl.BlockSpec((d, tn), lambda i, j, k: (0, j))],
        [jax.ShapeDtypeStruct((s, n_in), BF16)], [pl.BlockSpec((tm, tn), lambda i, j, k: (i, j))],
        [(0, 1, NN, 0)], 1, (tm, tn), 1, lambda ins, vals, outs, cs: _put(outs[0], cs, vals[0]),
        _mm_vmem([((tm, d), BF16, 2), ((d, tn), BF16, 2), ((tm, tn), F32, 3)]),
    )[0]
    mix_relay = relay_begin("mix", [W_A, W_B, W_O], z)

    a_act = _sgu_fwd(z, sgu_v_gain, ws_b, b_col, w_sgu, after=(mix_relay[4][0],))

    kv_b = z[:, off_k:off_g]
    k_pad = jnp.pad(kv_b[:, :w_kv], ((BLK, BLK), (0, 0)))
    v_pad = jnp.pad(kv_b[:, w_kv:], ((BLK, BLK), (0, 0)))
    q_blk0 = off_q // (grp * HEAD_DIM)
    att = _attn_fwd(sink, z, k_pad, v_pad, bias_tab, grp, q_blk0)

    gate_relay = relay_begin("gate", [W_GATE], att)
    up_relay = relay_begin("up", [W_UP], gate_relay[4][0])
    down_send, down_recv, (full_down,), token = _gather_start(
        "gather_start_down", [full_down], [kinds[W_DOWN]], after=(a_act, up_relay[4][0])
    )
    g_a, g_b, g_o = relay_end(mix_relay, token)

    tg = _tile(d, (512,))
    ga0, gb0 = off_g // tg, (off_g + d) // tg

    def ep_gate(ins, vals, outs, cs):
        sa, sb = _sigmoid(ins[4][:, cs].astype(F32)), _sigmoid(ins[5][:, cs].astype(F32))
        _put(outs[0], cs, sa * vals[0] + sb * vals[1])
        _put(outs[1], cs, vals[0])
        _put(outs[2], cs, vals[1])

    t_out = pl.BlockSpec((tm, tg), lambda i, j, k: (i, j))
    m_act, y_a, y_b = _mm(
        "mm_branches", (s // tm, d // tg, 1), [a_act, g_a, att, g_b, z, z],
        [pl.BlockSpec((tm, w_sgu), lambda i, j, k: (i, 0)), pl.BlockSpec((w_sgu, tg), lambda i, j, k: (0, j)),
         pl.BlockSpec((tm, w_att), lambda i, j, k: (i, 0)), pl.BlockSpec((w_att, tg), lambda i, j, k: (0, j)),
         pl.BlockSpec((tm, tg), lambda i, j, k: (i, ga0 + j)), pl.BlockSpec((tm, tg), lambda i, j, k: (i, gb0 + j))],
        [jax.ShapeDtypeStruct((s, d), BF16)] * 3,
        [t_out, t_out, t_out], [(0, 1, NN, 0), (2, 3, NN, 1)], 2, (tm, tg), 1, ep_gate,
        _mm_vmem([((tm, w_sgu), BF16, 4), ((w_sgu, tg), BF16, 4), ((tm, tg), F32, 12)]),    )

    tn = _tile(d, (1024, 512))

    def ep_residual(ins, vals, outs, cs):
        _put(outs[0], cs, ins[2][:, cs] + vals[0])

    gate_fwd = relay_end_async(gate_relay, m_act)
    x2 = _mm(
        "mm_wo", (s // tm, d // tn, 1), [m_act, g_o, x2d],
        [pl.BlockSpec((tm, d), lambda i, j, k: (i, 0)), pl.BlockSpec((d, tn), lambda i, j, k: (0, j)),
         pl.BlockSpec((tm, tn), lambda i, j, k: (i, j))],
        [jax.ShapeDtypeStruct((s, d), F32)], [pl.BlockSpec((tm, tn), lambda i, j, k: (i, j))],
        [(0, 1, NN, 0)], 1, (tm, tn), 1, ep_residual,
        _mm_vmem([((tm, d), BF16, 2), ((d, tn), BF16, 2), ((tm, tn), F32, 5)]),
        after=(gate_fwd[4][0],),
    )[0]
    (g_gate,) = forwarded(gate_fwd, x2)
    up_fwd = relay_end_async(up_relay, g_gate)
    h2 = _rms_fwd("rms_ffn", x2, norm_ffn, after=(up_fwd[4][0],))
    (g_up,) = forwarded(up_fwd, h2)

    tf = _tile(d_ff, (512,))

    def ep_swiglu(ins, vals, outs, cs):
        gt, up = vals
        _put(outs[0], cs, gt)
        _put(outs[1], cs, up)
        _put(outs[2], cs, (gt * _sigmoid(gt)) * up)

    f_out = pl.BlockSpec((tm, tf), lambda i, j, k: (i, j))
    gt, up, f_act = _mm(
        "mm_gate_up", (s // tm, d_ff // tf, 1), [h2, g_gate, g_up],
        [pl.BlockSpec((tm, d), lambda i, j, k: (i, 0)), pl.BlockSpec((d, tf), lambda i, j, k: (0, j)),
         pl.BlockSpec((d, tf), lambda i, j, k: (0, j))],
        [jax.ShapeDtypeStruct((s, d_ff), BF16)] * 3,
        [f_out, f_out, f_out], [(0, 1, NN, 0), (0, 2, NN, 1)], 2, (tm, tf), 1, ep_swiglu,
        _mm_vmem([((tm, d), BF16, 2), ((d, tf), BF16, 4), ((tm, tf), F32, 8)]),    )
    (g_down,) = _gather_forward(
        "gather_fwd_ffn_out",
        _gather_wait("gather_wait_ffn_out", [full_down], [kinds[W_DOWN]], [0], down_send, down_recv, f_act),
        [kinds[W_DOWN]],
    )

    tkf = _tile(d_ff, (1408, 1024, 512))
    tml, tnl = _tile(s, (512, 256, 128)), _tile(d, (512,))
    x3 = _mm(
        "mm_down", (s // tml, d // tnl, 1), [f_act, g_down, x2],
        [pl.BlockSpec((tml, d_ff), lambda i, j, k: (i, 0)), pl.BlockSpec((d_ff, tnl), lambda i, j, k: (0, j)),
         pl.BlockSpec((tml, tnl), lambda i, j, k: (i, j))],
        [jax.ShapeDtypeStruct((s, d), F32)], [pl.BlockSpec((tml, tnl), lambda i, j, k: (i, j))],
        [(0, 1, NN, 0)], 1, (tml, tnl), 1, ep_residual,
        _mm_vmem([((tml, d_ff), BF16, 2), ((d_ff, tnl), BF16, 2), ((tml, tnl), F32, 6)]),    )[0]

    dx3, dx3b, dg_final, loss_part = _head(x3, norm_final.reshape(1, d), tgt)

    def reduce_a(tag, ids, grads):
        ks = [kinds[i] for i in ids]
        lands = [lax.empty((g.shape[0] // 2, g.shape[1]) if k == "col" else (g.shape[0], g.shape[1] // 2), BF16)
                 for g, k in zip(grads, ks)]
        send, recv, bufs = _split_start("pair_send_" + tag, list(grads) + lands, len(ids), _pair_exchange_copies(ks), sibling_only=True)
        return {"tag": tag, "ids": ids, "ks": ks, "pair": (send, recv, bufs), "token": bufs[0]}

    def reduce_b(st, after):
        tag, ids, ks = st["tag"], st["ids"], st["ks"]
        send, recv, bufs = st["pair"]
        bufs = _split_wait("pair_wait_" + tag, bufs, send, recv, _pair_exchange_copies(ks), after)
        grads, from_sib = bufs[: len(ids)], bufs[len(ids) :]
        halves = [_pair_add("pair_add_" + names[i], c_idx, g, r, k) for i, g, r, k in zip(ids, grads, from_sib, ks)]
        st["chip"] = _chip_send_start("chip_send_" + tag, halves, ks)
        st["token"] = st["chip"][2][0]

    def reduce_c(st, after):
        tag, ids, ks = st["tag"], st["ids"], st["ks"]
        send, recv, halves, lands = st["chip"]
        halves, lands = _chip_send_wait("chip_wait_" + tag, halves, lands, ks, send, recv, after)
        pieces = [_chip_sum("chip_sum_" + names[i], qc_idx, h, r, k) for i, h, r, k in zip(ids, halves, lands, ks)]
        st["share"] = _split_start("share_send_" + tag, pieces, len(ids), _pair_share_copies(ks, False), sibling_only=True)
        st["token"] = st["share"][2][0]

    def reduce_d(st, after):
        send, recv, bufs = st["share"]
        return _split_wait("share_wait_" + st["tag"], bufs, send, recv, _pair_share_copies(st["ks"], True), after)

    grads_big, upd = [None] * 7, [None] * 7

    def finish(st, after):
        shared = reduce_d(st, after)
        after = shared[0]
        for i, g in zip(st["ids"], shared):
            upd[i] = _adamw("adamw_" + names[i], big_w[i], g, big_m[i], big_v[i], after=(after,))
            grads_big[i] = upd[i][3]
            after = upd[i][0]
        return after

    def ep_swiglu_bwd(ins, vals, outs, cs):
        df = vals[0]
        gtv, upv = ins[2][:, cs].astype(F32), ins[3][:, cs].astype(F32)
        sg = _sigmoid(gtv)
        _put(outs[0], cs, df * upv * (sg + gtv * sg * (1.0 - sg)))
        _put(outs[1], cs, df * (gtv * sg))

    dgt, dup = _mm(
        "mm_dswiglu", (s // tm, d_ff // tf, 1), [dx3b, g_down, gt, up],
        [pl.BlockSpec((tm, d), lambda i, j, k: (i, 0)), pl.BlockSpec((tf, d), lambda i, j, k: (j, 0)), f_out, f_out],
        [jax.ShapeDtypeStruct((s, d_ff), BF16), jax.ShapeDtypeStruct((s, d_ff), BF16)], [f_out, f_out],
        [(0, 1, NT, 0)], 1, (tm, tf), 1, ep_swiglu_bwd,
        _mm_vmem([((tm, d), BF16, 2), ((tf, d), BF16, 2), ((tm, tf), F32, 8)]),    )

    def ep_store(ins, vals, outs, cs):
        for o, v in zip(outs, vals):
            _put(o, cs, v)

    twn = _tile(d, (1024, 512))
    gw_down = _mm(
        "mm_gw_down", (d_ff // tkf, d // twn, 1), [f_act, dx3b],
        [pl.BlockSpec((s, tkf), lambda i, j, k: (0, i)), pl.BlockSpec((s, twn), lambda i, j, k: (0, j))],
        [jax.ShapeDtypeStruct((d_ff, d), BF16)], [pl.BlockSpec((tkf, twn), lambda i, j, k: (i, j))],
        [(0, 1, TN, 0)], 1, (tkf, twn), 1, ep_store,
        _mm_vmem([((s, tkf), BF16, 3), ((s, twn), BF16, 2), ((tkf, twn), F32, 3)]),
    )[0]
    red_down = reduce_a("down", [W_DOWN], [gw_down])

    tn2 = _tile(d, (256,))
    dh2_specs = [pl.BlockSpec((tm, d_ff), lambda i, j, k: (i, 0)), pl.BlockSpec((tn2, d_ff), lambda i, j, k: (j, 0))]
    dh2_tile = pl.BlockSpec((tm, tn2), lambda i, j, k: (i, j))
    dh2_vmem = _mm_vmem([((tm, d_ff), BF16, 2), ((tn2, d_ff), BF16, 2), ((tm, tn2), F32, 7)])
    dh2 = _mm(
        "mm_dh2_gate", (s // tm, d // tn2, 1), [dgt, g_gate], dh2_specs,
        [jax.ShapeDtypeStruct((s, d), F32)], [dh2_tile], [(0, 1, NT, 0)], 1, (tm, tn2), 1, ep_store, dh2_vmem,
        after=(red_down["token"],),
    )[0]
    dh2 = _mm(
        "mm_dh2_up", (s // tm, d // tn2, 1), [dup, g_up, dh2], dh2_specs + [dh2_tile],
        [jax.ShapeDtypeStruct((s, d), F32)], [dh2_tile], [(0, 1, NT, 0)], 1, (tm, tn2), 1, ep_residual, dh2_vmem,
    )[0]
    reduce_b(red_down, dh2)

    twr = _tile(d, (1024, 512))
    w_tile = pl.BlockSpec((twr, tf), lambda i, j, k: (i, j))
    gw_gate, gw_up = _mm(
        "mm_gw_gate_up", (d // twr, d_ff // tf, 1), [h2, dgt, dup],
        [pl.BlockSpec((s, twr), lambda i, j, k: (0, i)), pl.BlockSpec((s, tf), lambda i, j, k: (0, j)),
         pl.BlockSpec((s, tf), lambda i, j, k: (0, j))],
        [jax.ShapeDtypeStruct((d, d_ff), BF16), jax.ShapeDtypeStruct((d, d_ff), BF16)], [w_tile, w_tile],
        [(0, 1, TN, 0), (0, 2, TN, 1)], 2, (twr, tf), 1, ep_store,
        _mm_vmem([((s, twr), BF16, 3), ((s, tf), BF16, 4), ((twr, tf), F32, 6)]),
        after=(red_down["token"],),
    )
    red_ffn = reduce_a("ffn_in", [W_GATE, W_UP], [gw_gate, gw_up])

    dx2, dx2b, dg_ffn = _rms_bwd("rms_ffn_bwd", x2, norm_ffn, dh2, dx3, after=(red_ffn["token"],))

    nj = d // tg

    def lo(j):
        return jnp.minimum(j, nj - 1)

    def gate_bwd_body(dx_ref, wo_ref, ga_ref, gb_ref, ya_ref, yb_ref, dya_ref, dyb_ref, dz_ref, keep):
        j = pl.program_id(1)

        @pl.when(j < nj)
        def _():
            dm = lax.dot_general(dx_ref[...], wo_ref[...], NT, preferred_element_type=F32)
            sa, sb = _sigmoid(ga_ref[...].astype(F32)), _sigmoid(gb_ref[...].astype(F32))
            dya_ref[...] = (dm * sa).astype(BF16)
            dyb_ref[...] = (dm * sb).astype(BF16)
            dz_ref[...] = (dm * ya_ref[...].astype(F32) * (sa * (1.0 - sa))).astype(BF16)
            keep[lo(j)] = (dm * yb_ref[...].astype(F32) * (sb * (1.0 - sb))).astype(BF16)

        @pl.when(j >= nj)
        def _():
            dz_ref[...] = keep[jnp.maximum(j - nj, 0)]

    t_lo = pl.BlockSpec((tm, tg), lambda i, j: (i, lo(j)))
    dya, dyb, dz = _pallas(
        gate_bwd_body,
        name="mm_dgate",
        grid=(s // tm, 2 * nj),
        in_specs=[
            pl.BlockSpec((tm, d), lambda i, j: (i, 0)),
            pl.BlockSpec((tg, d), lambda i, j: (lo(j), 0)),
            pl.BlockSpec((tm, tg), lambda i, j: (i, ga0 + lo(j))),
            pl.BlockSpec((tm, tg), lambda i, j: (i, gb0 + lo(j))),
            t_lo,
            t_lo,
        ],
        out_specs=[t_lo, t_lo, pl.BlockSpec((tm, tg), lambda i, j: (i, ga0 + j))],
        out_shape=[jax.ShapeDtypeStruct((s, d), BF16), jax.ShapeDtypeStruct((s, d), BF16), jax.ShapeDtypeStruct((s, n_in), BF16)],
        scratch_shapes=[pltpu.VMEM((nj, tm, tg), BF16)],
        compiler_params=_params(_mm_vmem([((tm, d), BF16, 2), ((tg, d), BF16, 2), ((tm, tg), F32, 14), ((nj, tm, tg), BF16, 1)])),
    )(dx2b, g_o, z, z, y_a, y_b)
    reduce_b(red_ffn, dya)
    reduce_c(red_down, red_ffn["token"])

    gw_o = _mm(
        "mm_gw_o", (d // twr, d // twn, 1), [m_act, dx2b],
        [pl.BlockSpec((s, twr), lambda i, j, k: (0, i)), pl.BlockSpec((s, twn), lambda i, j, k: (0, j))],
        [jax.ShapeDtypeStruct((d, d), BF16)], [pl.BlockSpec((twr, twn), lambda i, j, k: (i, j))],
        [(0, 1, TN, 0)], 1, (twr, twn), 1, ep_store,
        _mm_vmem([((s, twr), BF16, 3), ((s, twn), BF16, 2), ((twr, twn), F32, 3)]),
        after=(red_down["token"],),
    )[0]
    after_down = finish(red_down, gw_o)

    tb = _tile(w_sgu, (1024, 512))
    b_out = pl.BlockSpec((tm, tb), lambda i, j, k: (i, j))

    da, datt = _mm(
        "mm_dbranches", (s // tm, w_sgu // tb, 1), [dya, g_a, dyb, g_b],
        [pl.BlockSpec((tm, d), lambda i, j, k: (i, 0)), pl.BlockSpec((tb, d), lambda i, j, k: (j, 0)),
         pl.BlockSpec((tm, d), lambda i, j, k: (i, 0)), pl.BlockSpec((tb, d), lambda i, j, k: (j, 0))],
        [jax.ShapeDtypeStruct((s, w_sgu), BF16), jax.ShapeDtypeStruct((s, w_att), BF16)], [b_out, b_out],
        [(0, 1, NT, 0), (2, 3, NT, 1)], 2, (tm, tb), 1, ep_store,
        _mm_vmem([((tm, d), BF16, 4), ((tb, d), BF16, 4), ((tm, tb), F32, 6)]),
        after=(after_down,),
    )

    wb_tile = pl.BlockSpec((tb, twn), lambda i, j, k: (i, j))
    gw_a, gw_b = _mm(
        "mm_gw_branches", (w_sgu // tb, d // twn, 1), [a_act, dya, att, dyb],
        [pl.BlockSpec((s, tb), lambda i, j, k: (0, i)), pl.BlockSpec((s, twn), lambda i, j, k: (0, j)),
         pl.BlockSpec((s, tb), lambda i, j, k: (0, i)), pl.BlockSpec((s, twn), lambda i, j, k: (0, j))],
        [jax.ShapeDtypeStruct((w_sgu, d), BF16), jax.ShapeDtypeStruct((w_att, d), BF16)], [wb_tile, wb_tile],
        [(0, 1, TN, 0), (2, 3, TN, 1)], 2, (tb, twn), 1, ep_store,
        _mm_vmem([((s, tb), BF16, 5), ((s, twn), BF16, 4), ((tb, twn), F32, 6)]),
        after=(da,),
    )
    red_mix = reduce_a("mix", [W_O, W_A, W_B], [gw_o, gw_a, gw_b])

    dz, dws, dbs, dgain = _sgu_bwd(z, da, sgu_v_gain, ws_b, wst_b, b_col, w_sgu, dz, after=(red_mix["token"],))
    dz, dk_pad, dv_pad, dbias_tab, dsink = _attn_bwd(sink, z, k_pad, v_pad, bias_tab, datt, dz, grp, q_blk0)
    dz = _dkv_to_dz(dk_pad, dv_pad, dz, off_k // (2 * w_kv))
    drel = _relbias_bwd(dbias_tab, bucket)
    reduce_b(red_mix, dz)
    reduce_c(red_ffn, red_mix["token"])

    small_w = [norm_mix, sgu_v_gain, sgu_w_s, sgu_b_s, attn_sink, rel_bias, norm_ffn, norm_final]
    small_m = [m_norm_mix, m_sgu_v_gain, m_sgu_w_s, m_sgu_b_s, m_attn_sink, m_rel_bias, m_norm_ffn, m_norm_final]
    small_v = [v_norm_mix, v_sgu_v_gain, v_sgu_w_s, v_sgu_b_s, v_attn_sink, v_rel_bias, v_norm_ffn, v_norm_final]
    small_shapes = [w.shape for w in small_w]
    early = [dgain, dws, dbs, dsink[:, 0], drel[:, :REL_BUCKETS].T, dg_ffn, dg_final]
    p_early = _pack([g.reshape(shp) for g, shp in zip(early, small_shapes[1:])] + [loss_part[0, :1]])
    land = jnp.zeros((2 * N_CHIPS,) + p_early.shape, F32)
    sm_send, sm_recv, (p_early, land) = _split_start("small_send", [p_early, land], 2 * N_CHIPS - 1, _small_exchange_copies(False))

    tzn = _tile(n_in, (768, 640, 512))
    gw_in = _mm(
        "mm_gw_in", (d // twr, n_in // tzn, 1), [h1, dz],
        [pl.BlockSpec((s, twr), lambda i, j, k: (0, i)), pl.BlockSpec((s, tzn), lambda i, j, k: (0, j))],
        [jax.ShapeDtypeStruct((d, n_in), BF16)], [pl.BlockSpec((twr, tzn), lambda i, j, k: (i, j))],
        [(0, 1, TN, 0)], 1, (twr, tzn), 1, ep_store,
        _mm_vmem([((s, twr), BF16, 3), ((s, tzn), BF16, 2), ((twr, tzn), F32, 3)]),
        after=(red_ffn["token"], p_early),
    )[0]
    red_in = reduce_a("w_in", [W_IN], [gw_in])

    reduce_c(red_mix, red_in["token"])
    reduce_b(red_in, finish(red_mix, red_in["token"]))

    dh1 = _mm(
        "mm_dh1", (s // tm, d // tn2, 1), [dz, g_in],
        [pl.BlockSpec((tm, n_in), lambda i, j, k: (i, 0)), pl.BlockSpec((tn2, n_in), lambda i, j, k: (j, 0))],
        [jax.ShapeDtypeStruct((s, d), F32)], [pl.BlockSpec((tm, tn2), lambda i, j, k: (i, j))],
        [(0, 1, NT, 0)], 1, (tm, tn2), 1, ep_store,
        _mm_vmem([((tm, n_in), BF16, 2), ((tn2, n_in), BF16, 2), ((tm, tn2), F32, 5)]),
        after=(red_in["token"],),
    )[0]

    grad_x, _, dg_mix = _rms_bwd("rms_mix_bwd", x2d, norm_mix, dh1, dx2)

    p_mix = _pack([dg_mix.reshape(small_shapes[0])])
    land_mix = jnp.zeros((2 * N_CHIPS,) + p_mix.shape, F32)
    mx_send, mx_recv, (p_mix, land_mix) = _split_start("mix_send", [p_mix, land_mix], 2 * N_CHIPS - 1, _small_exchange_copies(False))

    reduce_c(red_in, finish(red_ffn, p_mix))
    p_early, land = _split_wait("small_wait", [p_early, land], sm_send, sm_recv, _small_exchange_copies(True), red_in["token"])
    p_mix, land_mix = _split_wait("mix_wait", [p_mix, land_mix], mx_send, mx_recv, _small_exchange_copies(True), p_early)
    me_idx = 2 * q_idx + c_idx
    packed_g = jnp.concatenate([_small_sum("mix_sum", me_idx, p_mix, land_mix), _small_sum("small_sum", me_idx, p_early, land)], axis=0)
    g_small = _unpack(packed_g, small_shapes + [(1,)])
    loss = g_small[-1].reshape(())
    g_small = g_small[:-1]
    zero1 = jnp.zeros((1,), F32)
    pw, pg, pm, pv = _pack(small_w + [zero1]), _pack(g_small + [zero1]), _pack(small_m + [zero1]), _pack(small_v + [zero1])
    small_upd = _adamw("adamw_small", pw, pg, pm, pv)
    d_small, nm_small, nv_small = [_unpack(a, small_shapes) for a in small_upd[:3]]
    finish(red_in, small_upd[0])

    small_names = ["norm_mix", "sgu_v_gain", "sgu_w_s", "sgu_b_s", "attn_sink", "rel_bias", "norm_ffn", "norm_final"]
    table = {}
    for i, n in enumerate(names):
        table[n] = (grads_big[i][None], upd[i][0][None], upd[i][1][None], upd[i][2][None])
    for i, n in enumerate(small_names):
        table[n] = (g_small[i], d_small[i], nm_small[i], nv_small[i])
    order = ["w_in", "norm_mix", "sgu_v_gain", "sgu_w_s", "sgu_b_s", "w_a", "attn_sink", "rel_bias", "w_b", "w_o", "norm_ffn",
             "w_gate", "w_up", "w_down", "norm_final"]
    outs = [loss, grad_x.reshape(1, s, d)]
    for part in range(4):
        outs += [table[n][part] for n in order]
    return tuple(outs)
```

```python
import math

import jax
import jax.numpy as jnp
import numpy as np
from jax import lax
from jax.experimental import pallas as pl
from jax.experimental.pallas import tpu as pltpu

F32 = jnp.float32
BF16 = jnp.bfloat16
I32 = jnp.int32
MESH = pl.DeviceIdType.MESH

EPS = 1e-6
NEG = -1e30
BLK = 128
HEAD_DIM = 128
N_KV_HEADS = 2
REL_BUCKETS = 32
REL_MAX_DIST = 128
N_CHIPS = 4
ADAM_LR, ADAM_B1, ADAM_B2, ADAM_EPS, ADAM_WD, ADAM_STEP = 0.001, 0.9, 0.999, 1e-08, 0.01, 10

LANES = 128
VMEM_CAP = 60 * 1024 * 1024

NN = (((1,), (0,)), ((), ()))
NT = (((1,), (1,)), ((), ()))
TN = (((0,), (0,)), ((), ()))
ANY = pl.BlockSpec(memory_space=pl.ANY)
HBM_SPEC = pl.BlockSpec(memory_space=pltpu.HBM)
SEM_SPEC = pl.BlockSpec(memory_space=pltpu.SEMAPHORE)
EFFECT = pltpu.SideEffectType.DATAFLOW_SIDE_EFFECTING


def _tile(n, cands):
    for t in cands:
        if n % t == 0:
            return t
    return n


PIN_BYTES = 64 * 1024


def _pin_hbm(a):
    big = hasattr(a, "dtype") and jnp.issubdtype(a.dtype, jnp.floating) and _nbytes(a.shape, a.dtype) >= PIN_BYTES
    return pltpu.with_memory_space_constraint(a, pltpu.HBM) if big else a


def _pallas(body, *, out_shape, **kw):
    def pin(o):
        big = isinstance(o, jax.ShapeDtypeStruct) and jnp.issubdtype(o.dtype, jnp.floating) and _nbytes(o.shape, o.dtype) >= PIN_BYTES
        return pltpu.HBM(o.shape, o.dtype) if big else o

    shapes = type(out_shape)(pin(o) for o in out_shape) if isinstance(out_shape, (list, tuple)) else pin(out_shape)
    call = pl.pallas_call(body, out_shape=shapes, **kw)
    return lambda *args: call(*[_pin_hbm(a) for a in args])


def _params(vmem_bytes=None, **kw):
    if vmem_bytes is not None:
        kw["vmem_limit_bytes"] = int(min(max(vmem_bytes, 32 * 1024 * 1024), VMEM_CAP))
    return pltpu.CompilerParams(**kw)


def _nbytes(shape, dtype):
    return int(np.prod(shape)) * jnp.dtype(dtype).itemsize


def _sigmoid(x):
    return 1.0 / (1.0 + jnp.exp(-x))


_GC = 0.7978845608028654
_GA = 0.044715


def _gelu(x):
    return 0.5 * x * (1.0 + jnp.tanh(_GC * (x + _GA * (x * x * x))))


def _gelu_grad(x):
    t = jnp.tanh(_GC * (x + _GA * (x * x * x)))
    return 0.5 * (1.0 + t) + 0.5 * x * (1.0 - t * t) * (_GC * (1.0 + 3.0 * _GA * (x * x)))


def _bf(v):
    return v if v.dtype == BF16 else v.astype(BF16)


def _mm(name, grid, ins, in_specs, out_shape, out_specs, pairs, n_acc, tile, nk, epilogue, vmem_bytes, after=()):
    assert nk == 1
    n_in, n_out = len(ins) + len(after), len(out_shape)

    def body(*refs):
        in_refs, out_refs = refs[:n_in], refs[n_in : n_in + n_out]
        vals = [None] * n_acc
        for a_i, b_i, dn, acc_i in pairs:
            d = lax.dot_general(_bf(in_refs[a_i][...]), _bf(in_refs[b_i][...]), dn, preferred_element_type=F32)
            vals[acc_i] = d if vals[acc_i] is None else vals[acc_i] + d
        epilogue(in_refs, vals, out_refs, slice(None))

    return _pallas(
        body,
        name=name,
        grid=grid,
        in_specs=list(in_specs) + [ANY] * len(after),
        out_specs=out_specs,
        out_shape=out_shape,
        compiler_params=_params(vmem_bytes),
    )(*ins, *after)


def _put(ref, cs, v):
    ref[:, cs] = v.astype(ref.dtype)


def _mm_vmem(tiles):
    return sum(_nbytes(s, d) * c for s, d, c in tiles) + 4 * 1024 * 1024


def _rows8(v):
    r, d = v.shape
    return v.reshape(r // 8, 8, d).sum(axis=0)


def _rms_fwd(name, x, g, after=()):
    s, d = x.shape
    tm = _tile(s, (256, 128))

    def body(x_ref, g_ref, *rest):
        h_ref = rest[-1]
        xv = x_ref[...]
        r = lax.rsqrt(jnp.mean(xv * xv, axis=-1, keepdims=True) + EPS)
        h_ref[...] = ((xv * r) * g_ref[...]).astype(BF16)

    return _pallas(
        body,
        name=name,
        grid=(s // tm,),
        in_specs=[pl.BlockSpec((tm, d), lambda i: (i, 0)), pl.BlockSpec((1, d), lambda i: (0, 0))] + [ANY] * len(after),
        out_specs=pl.BlockSpec((tm, d), lambda i: (i, 0)),
        out_shape=jax.ShapeDtypeStruct((s, d), BF16),
    )(x, g, *after)


def _rms_bwd(name, x, g, dh, dres, after=()):
    s, d = x.shape
    tm = _tile(s, (256, 128))
    n = s // tm
    n_after = len(after)

    def body(x_ref, g_ref, dh_ref, dres_ref, *rest):
        dx_ref, dxb_ref, dg_ref, acc_ref = rest[n_after:]
        i = pl.program_id(0)
        xv = x_ref[...]
        r = lax.rsqrt(jnp.mean(xv * xv, axis=-1, keepdims=True) + EPS)
        xh = xv * r
        dhv = dh_ref[...]
        dxh = dhv * g_ref[...]
        dx = r * (dxh - xh * jnp.mean(dxh * xh, axis=-1, keepdims=True)) + dres_ref[...]
        dx_ref[...] = dx
        dxb_ref[...] = dx.astype(BF16)
        part = _rows8(dhv * xh)

        @pl.when(i == 0)
        def _():
            acc_ref[...] = part

        @pl.when(i > 0)
        def _():
            acc_ref[...] += part

        @pl.when(i == n - 1)
        def _():
            dg_ref[...] = jnp.sum(acc_ref[...], axis=0, keepdims=True)

    row = pl.BlockSpec((tm, d), lambda i: (i, 0))
    vec = pl.BlockSpec((1, d), lambda i: (0, 0))
    return _pallas(
        body,
        name=name,
        grid=(n,),
        in_specs=[row, vec, row, row] + [ANY] * n_after,
        out_specs=[row, row, vec],
        out_shape=[jax.ShapeDtypeStruct((s, d), F32), jax.ShapeDtypeStruct((s, d), BF16), jax.ShapeDtypeStruct((1, d), F32)],
        scratch_shapes=[pltpu.VMEM((8, d), F32)],
    )(x, g, dh, dres, *after)


def _head(x3, g, target):
    s, d = x3.shape
    tm = _tile(s, (256, 128))
    n = s // tm

    def body(x_ref, g_ref, t_ref, dx_ref, dxb_ref, dg_ref, loss_ref, acc_g, acc_l):
        i = pl.program_id(0)
        xv = x_ref[...]
        gv = g_ref[...]
        r = lax.rsqrt(jnp.mean(xv * xv, axis=-1, keepdims=True) + EPS)
        xh = xv * r
        e = xh * gv - t_ref[...]
        dy = e * (1.0 / d)
        dxh = dy * gv
        dx = r * (dxh - xh * jnp.mean(dxh * xh, axis=-1, keepdims=True))
        dx_ref[...] = dx
        dxb_ref[...] = dx.astype(BF16)
        pg = _rows8(dy * xh)
        plo = _rows8(e * e)

        @pl.when(i == 0)
        def _():
            acc_g[...] = pg
            acc_l[...] = plo

        @pl.when(i > 0)
        def _():
            acc_g[...] += pg
            acc_l[...] += plo

        @pl.when(i == n - 1)
        def _():
            dg_ref[...] = jnp.sum(acc_g[...], axis=0, keepdims=True)
            loss_ref[...] = jnp.full((1, LANES), (0.5 / d) * jnp.sum(acc_l[...]), F32)

    row = pl.BlockSpec((tm, d), lambda i: (i, 0))
    vec = pl.BlockSpec((1, d), lambda i: (0, 0))
    return _pallas(
        body,
        name="head",
        grid=(n,),
        in_specs=[row, vec, row],
        out_specs=[row, row, vec, pl.BlockSpec((1, LANES), lambda i: (0, 0))],
        out_shape=[
            jax.ShapeDtypeStruct((s, d), F32),
            jax.ShapeDtypeStruct((s, d), BF16),
            jax.ShapeDtypeStruct((1, d), F32),
            jax.ShapeDtypeStruct((1, LANES), F32),
        ],
        scratch_shapes=[pltpu.VMEM((8, d), F32), pltpu.VMEM((8, d), F32)],
    )(x3, g, target)


def _sgu_fwd(z, gain, ws_b, b_col, w_sgu, after=()):
    s = z.shape[0]
    groups = ws_b.shape[0]

    def body(zu_ref, zv_ref, gain_ref, ws_ref, b_ref, *rest):
        a_ref = rest[-1]
        vv = _gelu(zv_ref[...].astype(F32))
        r = lax.rsqrt(jnp.mean(vv * vv, axis=-1, keepdims=True) + EPS)
        vn = ((vv * r) * gain_ref[...]).astype(BF16)
        u = _gelu(zu_ref[...].astype(F32))
        for g in range(groups):
            sl = slice(g * BLK, (g + 1) * BLK)
            mixed = jnp.dot(ws_ref[g], vn[:, sl], preferred_element_type=F32) + b_ref[g]
            a_ref[:, sl] = (u[:, sl] * mixed).astype(BF16)

    return _pallas(
        body,
        name="sgu_fwd",
        grid=(s // BLK,),
        in_specs=[
            pl.BlockSpec((BLK, w_sgu), lambda c: (c, 0)),
            pl.BlockSpec((BLK, w_sgu), lambda c: (c, 1)),
            pl.BlockSpec((1, w_sgu), lambda c: (0, 0)),
            pl.BlockSpec((groups, BLK, BLK), lambda c: (0, 0, 0)),
            pl.BlockSpec((groups, BLK, 1), lambda c: (0, 0, 0)),
        ]
        + [ANY] * len(after),
        out_specs=pl.BlockSpec((BLK, w_sgu), lambda c: (c, 0)),
        out_shape=jax.ShapeDtypeStruct((s, w_sgu), BF16),
    )(z, z, gain, ws_b, b_col, *after)


def _sgu_bwd(z, da, gain, ws_b, wst_b, b_col, w_sgu, dz, after=()):
    s = z.shape[0]
    groups = ws_b.shape[0]
    n = s // BLK
    n_skip = 1 + len(after)

    def body(zu_ref, zv_ref, da_ref, gain_ref, ws_ref, wst_ref, b_ref, *rest):
        dz_ref, dws_ref, dbs_ref, dgain_ref, acc_gain, vv_s, gv_s, dxh_s = rest[n_skip:]
        c = pl.program_id(0)
        cols = [slice(g * BLK, (g + 1) * BLK) for g in range(groups)]

        ss = jnp.zeros((BLK, 1), F32)
        for sl in cols:
            zv = zv_ref[:, sl].astype(F32)
            vv = _gelu(zv)
            vv_s[:, sl] = vv
            gv_s[:, sl] = _gelu_grad(zv)
            ss = ss + jnp.sum(vv * vv, axis=-1, keepdims=True)
        r = lax.rsqrt(ss * (1.0 / w_sgu) + EPS)

        dot_dx = jnp.zeros((BLK, 1), F32)
        for g, sl in enumerate(cols):
            gain_g = gain_ref[:, sl]
            xh = vv_s[:, sl] * r
            vn = (xh * gain_g).astype(BF16)
            zu = zu_ref[:, sl].astype(F32)
            dav = da_ref[:, sl].astype(F32)
            dmix = dav * _gelu(zu)
            dmix_b = dmix.astype(BF16)
            mixed = jnp.dot(ws_ref[g], vn, preferred_element_type=F32) + b_ref[g]
            dz_ref[:, sl] = (dav * mixed * _gelu_grad(zu)).astype(BF16)
            dvn = jnp.dot(wst_ref[g], dmix_b, preferred_element_type=F32)
            dws_g = lax.dot_general(dmix_b, vn, NT, preferred_element_type=F32)
            dbs_g = jnp.sum(dmix, axis=1, keepdims=True)
            pg = _rows8(dvn * xh)

            @pl.when(c == 0)
            def _():
                dws_ref[g] = dws_g
                dbs_ref[g] = dbs_g
                acc_gain[:, sl] = pg

            @pl.when(c > 0)
            def _():
                dws_ref[g] += dws_g
                dbs_ref[g] += dbs_g
                acc_gain[:, sl] += pg

            dxh = dvn * gain_g
            dxh_s[:, sl] = dxh
            dot_dx = dot_dx + jnp.sum(dxh * xh, axis=-1, keepdims=True)

        mean_dx = dot_dx * (1.0 / w_sgu)
        for g, sl in enumerate(cols):
            dvv = r * (dxh_s[:, sl] - (vv_s[:, sl] * r) * mean_dx)
            dz_ref[:, w_sgu + g * BLK : w_sgu + (g + 1) * BLK] = (dvv * gv_s[:, sl]).astype(BF16)

        @pl.when(c == n - 1)
        def _():
            dgain_ref[...] = jnp.sum(acc_gain[...], axis=0, keepdims=True)

    full3 = pl.BlockSpec((groups, BLK, BLK), lambda c: (0, 0, 0))
    col3 = pl.BlockSpec((groups, BLK, 1), lambda c: (0, 0, 0))
    vec = pl.BlockSpec((1, w_sgu), lambda c: (0, 0))
    return _pallas(
        body,
        name="sgu_bwd",
        grid=(n,),
        in_specs=[
            pl.BlockSpec((BLK, w_sgu), lambda c: (c, 0)),
            pl.BlockSpec((BLK, w_sgu), lambda c: (c, 1)),
            pl.BlockSpec((BLK, w_sgu), lambda c: (c, 0)),
            vec,
            full3,
            full3,
            col3,
            ANY,
        ]
        + [ANY] * len(after),
        out_specs=[pl.BlockSpec((BLK, 2 * w_sgu), lambda c: (c, 0)), full3, col3, vec],
        out_shape=[
            jax.ShapeDtypeStruct(dz.shape, BF16),
            jax.ShapeDtypeStruct((groups, BLK, BLK), F32),
            jax.ShapeDtypeStruct((groups, BLK, 1), F32),
            jax.ShapeDtypeStruct((1, w_sgu), F32),
        ],
        scratch_shapes=[pltpu.VMEM((8, w_sgu), F32)] + [pltpu.VMEM((BLK, w_sgu), F32)] * 3,
        input_output_aliases={7: 0},
    )(z, z, da, gain, ws_b, wst_b, b_col, dz, *after)


def _attn_softmax(sink_ref, q_ref, k_ref, v_ref, bias_ref, s_len, grp):
    kv = pl.program_id(0)
    n = pl.program_id(1)
    start = pl.multiple_of(n * BLK, BLK)
    kb = k_ref[pl.ds(start, 3 * BLK), :]
    vb = v_ref[pl.ds(start, 3 * BLK), :]
    qv = q_ref[...]
    qs = jnp.concatenate([qv[:, g * HEAD_DIM : (g + 1) * HEAD_DIM] for g in range(grp)], axis=0).astype(BF16)
    sc = lax.dot_general(qs, kb, NT, preferred_element_type=F32) * (HEAD_DIM**-0.5)
    sc = sc + bias_ref[...].reshape(grp * BLK, 3 * BLK)
    kpos = start + lax.broadcasted_iota(I32, (1, 3 * BLK), 1) - BLK
    sc = jnp.where((kpos >= 0) & (kpos < s_len), sc, NEG)
    sink = jnp.concatenate([jnp.full((BLK, 1), sink_ref[kv * grp + g], F32) for g in range(grp)], axis=0)
    m = jnp.maximum(jnp.max(sc, axis=-1, keepdims=True), sink)
    p = jnp.exp(sc - m)
    esink = jnp.exp(sink - m)
    den = jnp.sum(p, axis=-1, keepdims=True) + esink
    return start, qs, kb, vb, p / den, esink / den


def _attn_specs(s, grp, q_blk0):
    qw = grp * HEAD_DIM
    return [
        pl.BlockSpec(memory_space=pltpu.SMEM),
        pl.BlockSpec((BLK, qw), lambda kv, n: (n, q_blk0 + kv)),
        pl.BlockSpec((s + 2 * BLK, HEAD_DIM), lambda kv, n: (0, kv)),
        pl.BlockSpec((s + 2 * BLK, HEAD_DIM), lambda kv, n: (0, kv)),
        pl.BlockSpec((grp, BLK, 3 * BLK), lambda kv, n: (kv, 0, 0)),
    ]


def _attn_fwd(sink, z, k_pad, v_pad, bias_tab, grp, q_blk0):
    s = z.shape[0]
    qw = grp * HEAD_DIM

    def body(sink_ref, q_ref, k_ref, v_ref, bias_ref, o_ref):
        _, _, _, vb, pn, _ = _attn_softmax(sink_ref, q_ref, k_ref, v_ref, bias_ref, s, grp)
        o = jnp.dot(pn.astype(BF16), vb, preferred_element_type=F32)
        for g in range(grp):
            o_ref[:, g * HEAD_DIM : (g + 1) * HEAD_DIM] = o[g * BLK : (g + 1) * BLK].astype(BF16)

    return _pallas(
        body,
        name="attn_fwd",
        grid=(N_KV_HEADS, s // BLK),
        in_specs=_attn_specs(s, grp, q_blk0),
        out_specs=pl.BlockSpec((BLK, qw), lambda kv, n: (n, kv)),
        out_shape=jax.ShapeDtypeStruct((s, N_KV_HEADS * qw), BF16),
    )(sink, z, k_pad, v_pad, bias_tab)


def _attn_bwd(sink, z, k_pad, v_pad, bias_tab, dout, dz, grp, q_blk0):
    s = z.shape[0]
    qw = grp * HEAD_DIM
    nb = s // BLK
    heads = N_KV_HEADS * grp

    def body(sink_ref, q_ref, k_ref, v_ref, bias_ref, do_ref, dz_in, dq_ref, dk_ref, dv_ref, dbias_ref, dsink_ref, dk_acc, dv_acc):
        del dz_in
        kv = pl.program_id(0)
        n = pl.program_id(1)
        start, qs, kb, vb, pn, psink = _attn_softmax(sink_ref, q_ref, k_ref, v_ref, bias_ref, s, grp)
        dov = do_ref[...]
        dos = jnp.concatenate([dov[:, g * HEAD_DIM : (g + 1) * HEAD_DIM] for g in range(grp)], axis=0)
        dp = lax.dot_general(dos, vb, NT, preferred_element_type=F32)
        dvb = lax.dot_general(pn.astype(BF16), dos, TN, preferred_element_type=F32)
        delta = jnp.sum(pn * dp, axis=-1, keepdims=True)
        ds = pn * (dp - delta)
        dsb = (ds * (HEAD_DIM**-0.5)).astype(BF16)
        dq = jnp.dot(dsb, kb, preferred_element_type=F32)
        dkb = lax.dot_general(dsb, qs, TN, preferred_element_type=F32)
        for g in range(grp):
            dq_ref[:, g * HEAD_DIM : (g + 1) * HEAD_DIM] = dq[g * BLK : (g + 1) * BLK].astype(BF16)

        @pl.when(n == 0)
        def _():
            dk_acc[...] = jnp.zeros_like(dk_acc)
            dv_acc[...] = jnp.zeros_like(dv_acc)
            dbias_ref[...] = jnp.zeros_like(dbias_ref)

        @pl.when((n == 0) & (kv == 0))
        def _():
            dsink_ref[...] = jnp.zeros_like(dsink_ref)

        dk_acc[pl.ds(start, 3 * BLK), :] += dkb
        dv_acc[pl.ds(start, 3 * BLK), :] += dvb
        dbias_ref[...] += ds.reshape(grp, BLK, 3 * BLK)
        row = lax.broadcasted_iota(I32, (heads, LANES), 0)
        sd = psink * delta
        upd = jnp.zeros((heads, LANES), F32)
        for g in range(grp):
            upd = jnp.where(row == kv * grp + g, -jnp.sum(sd[g * BLK : (g + 1) * BLK]), upd)
        dsink_ref[...] += upd

        @pl.when(n == nb - 1)
        def _():
            dk_ref[...] = dk_acc[...]
            dv_ref[...] = dv_acc[...]

    pad_spec = pl.BlockSpec((s + 2 * BLK, HEAD_DIM), lambda kv, n: (0, kv))
    kvw = N_KV_HEADS * HEAD_DIM
    return _pallas(
        body,
        name="attn_bwd",
        grid=(N_KV_HEADS, nb),
        in_specs=_attn_specs(s, grp, q_blk0) + [pl.BlockSpec((BLK, qw), lambda kv, n: (n, kv)), ANY],
        out_specs=[
            pl.BlockSpec((BLK, qw), lambda kv, n: (n, q_blk0 + kv)),
            pad_spec,
            pad_spec,
            pl.BlockSpec((grp, BLK, 3 * BLK), lambda kv, n: (kv, 0, 0)),
            pl.BlockSpec((heads, LANES), lambda kv, n: (0, 0)),
        ],
        out_shape=[
            jax.ShapeDtypeStruct(dz.shape, BF16),
            jax.ShapeDtypeStruct((s + 2 * BLK, kvw), F32),
            jax.ShapeDtypeStruct((s + 2 * BLK, kvw), F32),
            jax.ShapeDtypeStruct((heads, BLK, 3 * BLK), F32),
            jax.ShapeDtypeStruct((heads, LANES), F32),
        ],
        scratch_shapes=[pltpu.VMEM((s + 2 * BLK, HEAD_DIM), F32), pltpu.VMEM((s + 2 * BLK, HEAD_DIM), F32)],
        input_output_aliases={6: 0},
    )(sink, z, k_pad, v_pad, bias_tab, dout, dz)


def _dkv_to_dz(dk_pad, dv_pad, dz, blk_idx):
    s = dz.shape[0]
    kvw = dk_pad.shape[1]

    def body(dk_ref, dv_ref, dz_in, out_ref):
        del dz_in
        out_ref[:, :kvw] = dk_ref[...].astype(BF16)
        out_ref[:, kvw:] = dv_ref[...].astype(BF16)

    src = pl.BlockSpec((BLK, kvw), lambda i: (i + 1, 0))
    return _pallas(
        body,
        name="dkv_to_dz",
        grid=(s // BLK,),
        in_specs=[src, src, ANY],
        out_specs=pl.BlockSpec((BLK, 2 * kvw), lambda i: (i, blk_idx)),
        out_shape=jax.ShapeDtypeStruct(dz.shape, BF16),
        input_output_aliases={2: 0},
    )(dk_pad, dv_pad, dz)


def _relbias_bwd(dbias_tab, bucket):
    heads = dbias_tab.shape[0]

    def body(dt_ref, bk_ref, out_ref):
        lane = lax.broadcasted_iota(I32, (1, LANES), 1)
        bk = bk_ref[...]
        rows = []
        for h in range(heads):
            dt = dt_ref[h]
            acc = jnp.zeros((1, LANES), F32)
            for b in range(REL_BUCKETS):
                acc = jnp.where(lane == b, jnp.sum(jnp.where(bk == b, dt, 0.0)), acc)
            rows.append(acc)
        out_ref[...] = jnp.concatenate(rows, axis=0)

    return _pallas(body, name="relbias_bwd", out_shape=jax.ShapeDtypeStruct((heads, LANES), F32))(dbias_tab, bucket)


def _t5_bucket(rel):
    nb = REL_BUCKETS // 2
    ret = jnp.where(rel > 0, nb, 0)
    n = jnp.abs(rel)
    max_exact = nb // 2
    nf = jnp.maximum(n, 1).astype(F32)
    large = max_exact + (jnp.log(nf / max_exact) / math.log(REL_MAX_DIST / max_exact) * (nb - max_exact)).astype(I32)
    large = jnp.minimum(large, nb - 1)
    return ret + jnp.where(n < max_exact, n, large)


def _band_tables(rel_bias):
    qi = jnp.arange(BLK)[:, None]
    kj = jnp.arange(3 * BLK)[None, :]
    rel = kj - BLK - qi
    bucket = _t5_bucket(rel).astype(I32)
    heads = rel_bias.shape[1]
    masked = jnp.where(jnp.abs(rel) <= BLK, bucket, -1)

    def body(rb_ref, bk_ref, out_ref):
        bk = bk_ref[...]
        for h in range(heads):
            tab = jnp.full(bk.shape, NEG, F32)
            for b in range(REL_BUCKETS):
                tab = jnp.where(bk == b, rb_ref[b, h], tab)
            out_ref[h] = tab

    bias_tab = _pallas(
        body,
        name="bias_table",
        in_specs=[pl.BlockSpec(memory_space=pltpu.SMEM), pl.BlockSpec(memory_space=pltpu.VMEM)],
        out_specs=pl.BlockSpec(memory_space=pltpu.VMEM),
        out_shape=jax.ShapeDtypeStruct((heads, BLK, 3 * BLK), F32),
    )(rel_bias.astype(F32), masked)
    return bias_tab, bucket


EW_BLOCK_ELEMS = 512 * 1024


def _ew_tiles(shape, elems=EW_BLOCK_ELEMS // 2):
    r, c = shape
    tn = c if c <= 2048 else _tile(c, (2048, 1920, 1536, 1408, 1024, 512))
    tm = _tile(r, [t for t in (1024, 512, 256, 128, 64, 32, 16, 8) if t * tn <= elems] or [8])
    return tm, tn


def _cast_into_full(name, qidx, w, kind, after=()):
    r, c = w.shape
    tm, tn = _ew_tiles(w.shape, EW_BLOCK_ELEMS)
    nbi, nbj = r // tm, c // tn
    if kind == "col":
        full, out_spec = (r, c * N_CHIPS), pl.BlockSpec((tm, tn), lambda i, j, q: (i, q[0] * nbj + j))
    else:
        full, out_spec = (r * N_CHIPS, c), pl.BlockSpec((tm, tn), lambda i, j, q: (q[0] * nbi + i, j))

    def body(q_ref, w_ref, *rest):
        del q_ref
        rest[-1][...] = w_ref[...].astype(BF16)

    return _pallas(
        body,
        name=name,
        grid_spec=pltpu.PrefetchScalarGridSpec(
            num_scalar_prefetch=1,
            grid=(nbi, nbj),
            in_specs=[pl.BlockSpec((tm, tn), lambda i, j, q: (i, j))] + [ANY] * len(after),
            out_specs=out_spec,
        ),
        out_shape=jax.ShapeDtypeStruct(full, BF16),
    )(qidx, w, *after)


def _adamw(name, w, g, m, v, after=()):
    tm, tn = _ew_tiles(w.shape, EW_BLOCK_ELEMS)
    if _nbytes(w.shape, F32) <= 1024 * 1024:
        tm, tn = w.shape
    spec = pl.BlockSpec((tm, tn), lambda i, j: (i, j))
    n_after = len(after)

    def body(w_ref, g_ref, m_ref, v_ref, *rest):
        d_ref, nm_ref, nv_ref, g_out_ref = rest[n_after:]
        gv = g_ref[...]
        g_out_ref[...] = gv
        nm = ADAM_B1 * m_ref[...] + (1.0 - ADAM_B1) * gv
        nv = ADAM_B2 * v_ref[...] + (1.0 - ADAM_B2) * (gv * gv)
        m_hat = nm / (1.0 - ADAM_B1**ADAM_STEP)
        v_hat = nv / (1.0 - ADAM_B2**ADAM_STEP)
        d_ref[...] = -ADAM_LR * (m_hat / (jnp.sqrt(v_hat) + ADAM_EPS) + ADAM_WD * w_ref[...])
        nm_ref[...] = nm
        nv_ref[...] = nv

    out = jax.ShapeDtypeStruct(w.shape, F32)
    return _pallas(
        body, name=name, grid=(w.shape[0] // tm, w.shape[1] // tn), in_specs=[spec] * 4 + [ANY] * n_after,
        out_specs=[spec] * 4, out_shape=[out, out, out, out],
        compiler_params=_params(_mm_vmem([((tm, tn), F32, 24)])),
    )(w, g, m, v, *after)


def _pair_add(name, cidx, g_full, r_sib, kind):
    hr, hc = r_sib.shape
    tm, tn = _ew_tiles((hr, hc), 2 * EW_BLOCK_ELEMS)
    nbi, nbj = hr // tm, hc // tn
    if kind == "col":
        g_spec = pl.BlockSpec((tm, tn), lambda i, j, c: (c[0] * nbi + i, j))
    else:
        g_spec = pl.BlockSpec((tm, tn), lambda i, j, c: (i, c[0] * nbj + j))
    spec = pl.BlockSpec((tm, tn), lambda i, j, c: (i, j))

    def body(c_ref, g_ref, r_ref, o_ref):
        del c_ref
        o_ref[...] = (g_ref[...].astype(F32) + r_ref[...].astype(F32)).astype(BF16)

    return _pallas(
        body,
        name=name,
        grid_spec=pltpu.PrefetchScalarGridSpec(num_scalar_prefetch=1, grid=(nbi, nbj), in_specs=[g_spec, spec], out_specs=spec),
        out_shape=jax.ShapeDtypeStruct((hr, hc), BF16),
        compiler_params=_params(_mm_vmem([((tm, tn), BF16, 6), ((tm, tn), F32, 3)])),
    )(cidx, g_full, r_sib)


def _chip_sum(name, qidx, c_half, r_ici, kind):
    _, pr, pc = r_ici.shape
    tm, tn = _ew_tiles((pr, pc), 2 * EW_BLOCK_ELEMS)
    nbi, nbj = pr // tm, pc // tn
    if kind == "col":
        own_spec = pl.BlockSpec((tm, tn), lambda i, j, q: (i, q[0] * nbj + j))
        full, out_spec = (2 * pr, pc), pl.BlockSpec((tm, tn), lambda i, j, q: (q[1] * nbi + i, j))
    else:
        own_spec = pl.BlockSpec((tm, tn), lambda i, j, q: (q[0] * nbi + i, j))
        full, out_spec = (pr, 2 * pc), pl.BlockSpec((tm, tn), lambda i, j, q: (i, q[1] * nbj + j))

    def body(q_ref, own_ref, r_ref, o_ref):
        q = q_ref[0]
        own = own_ref[...].astype(F32)
        recv = [r_ref[r].astype(F32) for r in range(3)]
        total = None
        for chip in range(N_CHIPS):
            d = chip ^ q
            term = jnp.where(d == 0, own, jnp.where(d == 2, recv[0], jnp.where(d == 1, recv[1], recv[2])))
            total = term if total is None else total + term
        o_ref[...] = total

    return _pallas(
        body,
        name=name,
        grid_spec=pltpu.PrefetchScalarGridSpec(
            num_scalar_prefetch=1,
            grid=(nbi, nbj),
            in_specs=[own_spec, pl.BlockSpec((3, tm, tn), lambda i, j, q: (0, i, j))],
            out_specs=out_spec,
        ),
        out_shape=jax.ShapeDtypeStruct(full, F32),
        compiler_params=_params(_mm_vmem([((tm, tn), BF16, 8), ((tm, tn), F32, 6)])),
    )(qidx, c_half, r_ici)


_REL_MASK = (2, 1, 3)


def _place():
    x, y, c = lax.axis_index("x"), lax.axis_index("y"), lax.axis_index("c")
    chips = [(1 - x, y), (x, 1 - y), (1 - x, 1 - y)]
    return x, y, c, 2 * x + y, chips


def _shard_view(ref, kind, chip):
    if kind == "col":
        w = ref.shape[1] // N_CHIPS
        return ref.at[:, pl.ds(pl.multiple_of(chip * w, LANES), w)]
    h = ref.shape[0] // N_CHIPS
    return ref.at[pl.ds(pl.multiple_of(chip * h, 16), h), :]


def _row_half(ref, half):
    h = ref.shape[0] // 2
    return ref.at[pl.ds(pl.multiple_of(half * h, 16), h), :]


def _pair_half(ref, kind, half):
    if kind == "col":
        return _row_half(ref, half)
    w = ref.shape[1] // 2
    return ref.at[:, pl.ds(pl.multiple_of(half * w, LANES), w)]


def _remote(src, dst, send_sem, recv_sem, dev):
    return pltpu.make_async_remote_copy(src_ref=src, dst_ref=dst, send_sem=send_sem, recv_sem=recv_sem, device_id=dev, device_id_type=MESH)


def _hbm(a):
    return pltpu.with_memory_space_constraint(a, pltpu.HBM)


def _gather_start(name, fulls, kinds, rels=(0, 1, 2), after=()):
    n_w = len(fulls)

    def body(*refs):
        g = refs[:n_w]
        send_sem, recv_sem = refs[n_w + len(after)], refs[n_w + len(after) + 1]
        token = refs[-1]
        _, _, c, q, chips = _place()
        for w in range(n_w):
            mine = _row_half(_shard_view(g[w], kinds[w], q), c)
            for r in rels if isinstance(rels, tuple) else rels[w]:
                _remote(mine, mine, send_sem.at[3 * w + r], recv_sem.at[3 * w + r], (*chips[r], c)).start()
        token[...] = jnp.zeros_like(token)

    res = _pallas(
        body,
        name=name,
        out_shape=(
            pltpu.SemaphoreType.DMA((3 * n_w,)),
            pltpu.SemaphoreType.DMA((3 * n_w,)),
            *[pltpu.HBM(f.shape, f.dtype) for f in fulls],
            jax.ShapeDtypeStruct((8, LANES), F32),
        ),
        in_specs=[HBM_SPEC] * n_w + [ANY] * len(after),
        out_specs=(SEM_SPEC, SEM_SPEC, *[HBM_SPEC] * n_w, pl.BlockSpec(memory_space=pltpu.VMEM)),
        input_output_aliases={w: w + 2 for w in range(n_w)},
        compiler_params=pltpu.CompilerParams(has_side_effects=EFFECT),
    )(*[_hbm(f) for f in fulls], *after)
    return res[0], res[1], list(res[2 : 2 + n_w]), res[-1]


def _relay_copies(kinds, waiting):
    def copies(refs, send_sem, recv_sem):
        _, _, c, q, chips = _place()
        out = []
        for i, kind in enumerate(kinds):
            for k, (src_rel, dst_rel) in enumerate(((0, 1), (1, 0))):
                held = _row_half(_row_half(_shard_view(refs[i], kind, q ^ _REL_MASK[src_rel]), c), k)
                far = _row_half(_row_half(_shard_view(refs[i], kind, q ^ _REL_MASK[2]), c), k)
                dst = far if waiting else held
                out.append(_remote(held, dst, send_sem.at[2 * i + k], recv_sem.at[2 * i + k], (*chips[dst_rel], c)))
        return out

    return copies


def _forward_copies(kinds, waiting):
    def copies(refs, send_sem, recv_sem):
        x, y, c, q, _ = _place()
        out = []
        for i, kind in enumerate(kinds):
            for r in range(3):
                quarter = _shard_view(refs[i], kind, q ^ _REL_MASK[r])
                landed = _row_half(quarter, c)
                dst = _row_half(quarter, 1 - c) if waiting else landed
                out.append(_remote(landed, dst, send_sem.at[3 * i + r], recv_sem.at[3 * i + r], (x, y, 1 - c)))
        return out

    return copies


def _gather_wait(name, fulls, kinds, w_ids, send_sem, recv_sem, after, rels=(0, 1, 2)):
    n = len(fulls)

    def body(*refs):
        g = refs[:n]
        s_sem, r_sem = refs[n], refs[n + 1]
        x, y, c, q, _ = _place()
        for i, w in enumerate(w_ids):
            mine = _row_half(_shard_view(g[i], kinds[i], q), c)
            for r in rels:
                landed = _row_half(_shard_view(g[i], kinds[i], q ^ _REL_MASK[r]), c)
                cp = _remote(mine, landed, s_sem.at[3 * w + r], r_sem.at[3 * w + r], (x, y, 1 - c))
                cp.wait_send()
                cp.wait_recv()

    res = _pallas(
        body,
        name=name,
        out_shape=[pltpu.HBM(f.shape, f.dtype) for f in fulls],
        in_specs=[HBM_SPEC] * n + [SEM_SPEC, SEM_SPEC, ANY],
        out_specs=[HBM_SPEC] * n,
        input_output_aliases={i: i for i in range(n)},
        compiler_params=pltpu.CompilerParams(has_side_effects=EFFECT),
    )(*fulls, send_sem, recv_sem, after)
    return list(res)


def _gather_forward(name, fulls, kinds):
    n = len(fulls)

    def body(*refs):
        g = refs[n : 2 * n]
        send, recv = refs[2 * n :]
        _sibling_handshake()
        x, y, c, q, _ = _place()
        sib = (x, y, 1 - c)
        cps = []
        for i in range(n):
            for r in range(3):
                landed = _row_half(_shard_view(g[i], kinds[i], q ^ _REL_MASK[r]), c)
                cps.append(_remote(landed, landed, send.at[i, r], recv.at[i, r], sib))
        for cp in cps:
            cp.start()
        for i in range(n):
            for r in range(3):
                other = _row_half(_shard_view(g[i], kinds[i], q ^ _REL_MASK[r]), 1 - c)
                _remote(other, other, send.at[i, r], recv.at[i, r], sib).wait_recv()
        for cp in cps:
            cp.wait_send()

    res = _pallas(
        body,
        name=name,
        in_specs=[ANY] * n,
        out_specs=[ANY] * n,
        out_shape=[jax.ShapeDtypeStruct(f.shape, f.dtype) for f in fulls],
        scratch_shapes=[pltpu.SemaphoreType.DMA((n, 3)), pltpu.SemaphoreType.DMA((n, 3))],
        input_output_aliases={i: i for i in range(n)},
        compiler_params=pltpu.CompilerParams(collective_id=SIBLING_BARRIER_ID),
    )(*fulls)
    return list(res)


SIBLING_BARRIER_ID = 1


def _sibling_handshake():
    sib = (lax.axis_index("x"), lax.axis_index("y"), 1 - lax.axis_index("c"))
    barrier = pltpu.get_barrier_semaphore()
    pl.semaphore_signal(barrier, inc=1, device_id=sib, device_id_type=MESH)
    pl.semaphore_wait(barrier, 1)


def _split_start(name, bufs, n_sems, copies, sibling_only=False):
    n = len(bufs)

    def body(*refs):
        if sibling_only:
            _sibling_handshake()
        for cp in copies(refs[:n], refs[n], refs[n + 1]):
            cp.start()

    extra = {"collective_id": SIBLING_BARRIER_ID} if sibling_only else {}

    res = _pallas(
        body,
        name=name,
        out_shape=(
            pltpu.SemaphoreType.DMA((n_sems,)),
            pltpu.SemaphoreType.DMA((n_sems,)),
            *[pltpu.HBM(b.shape, b.dtype) for b in bufs],
        ),
        in_specs=[HBM_SPEC] * n,
        out_specs=(SEM_SPEC, SEM_SPEC, *[HBM_SPEC] * n),
        input_output_aliases={i: i + 2 for i in range(n)},
        compiler_params=pltpu.CompilerParams(has_side_effects=EFFECT, **extra),
    )(*[_hbm(b) for b in bufs])
    return res[0], res[1], list(res[2:])


def _split_wait(name, bufs, send_sem, recv_sem, copies, after):
    n = len(bufs)

    def body(*refs):
        for cp in copies(refs[:n], refs[n], refs[n + 1]):
            cp.wait_send()
            cp.wait_recv()

    res = _pallas(
        body,
        name=name,
        out_shape=[pltpu.HBM(b.shape, b.dtype) for b in bufs],
        in_specs=[HBM_SPEC] * n + [SEM_SPEC, SEM_SPEC, ANY],
        out_specs=[HBM_SPEC] * n,
        input_output_aliases={i: i for i in range(n)},
        compiler_params=pltpu.CompilerParams(has_side_effects=EFFECT),
    )(*bufs, send_sem, recv_sem, after)
    return list(res)


def _pair_exchange_copies(kinds):
    n = len(kinds)

    def copies(refs, send_sem, recv_sem):
        x, y, c, _, _ = _place()
        return [
            _remote(_pair_half(refs[w], kinds[w], 1 - c), refs[n + w], send_sem.at[w], recv_sem.at[w], (x, y, 1 - c))
            for w in range(n)
        ]

    return copies


def _pair_share_copies(kinds, waiting):
    def copies(refs, send_sem, recv_sem):
        x, y, c, _, _ = _place()
        out = []
        for w, kind in enumerate(kinds):
            mine = _pair_half(refs[w], kind, c)
            dst = _pair_half(refs[w], kind, 1 - c) if waiting else mine
            out.append(_remote(mine, dst, send_sem.at[w], recv_sem.at[w], (x, y, 1 - c)))
        return out

    return copies


def _piece_shape(half_shape, kind):
    r, c = half_shape
    return (3, r, c // N_CHIPS) if kind == "col" else (3, r // N_CHIPS, c)


def _chip_send_start(name, halves, kinds):
    n = len(halves)
    lands = [lax.empty(_piece_shape(h.shape, k), BF16) for h, k in zip(halves, kinds)]

    def body(*refs):
        h, land = refs[:n], refs[n : 2 * n]
        send_sem, recv_sem = refs[2 * n], refs[2 * n + 1]
        _, _, c, q, chips = _place()
        for i in range(n):
            for r, chip in enumerate(chips):
                piece = _shard_view(h[i], kinds[i], q ^ _REL_MASK[r])
                _remote(piece, land[i].at[r], send_sem.at[3 * i + r], recv_sem.at[3 * i + r], (*chip, c)).start()

    res = _pallas(
        body,
        name=name,
        out_shape=(
            pltpu.SemaphoreType.DMA((3 * n,)),
            pltpu.SemaphoreType.DMA((3 * n,)),
            *[pltpu.HBM(a.shape, a.dtype) for a in halves],
            *[pltpu.HBM(a.shape, a.dtype) for a in lands],
        ),
        in_specs=[HBM_SPEC] * (2 * n),
        out_specs=(SEM_SPEC, SEM_SPEC, *[HBM_SPEC] * (2 * n)),
        input_output_aliases={i: i + 2 for i in range(2 * n)},
        compiler_params=pltpu.CompilerParams(has_side_effects=EFFECT),
    )(*[_hbm(a) for a in halves], *[_hbm(a) for a in lands])
    return res[0], res[1], list(res[2 : 2 + n]), list(res[2 + n :])


def _chip_send_wait(name, halves, lands, kinds, send_sem, recv_sem, after):
    n = len(halves)

    def body(*refs):
        h, land = refs[:n], refs[n : 2 * n]
        s_sem, r_sem = refs[2 * n], refs[2 * n + 1]
        x, y, c, q, _ = _place()
        for i in range(n):
            for r in range(3):
                piece = _shard_view(h[i], kinds[i], q ^ _REL_MASK[r])
                cp = _remote(piece, land[i].at[r], s_sem.at[3 * i + r], r_sem.at[3 * i + r], (x, y, 1 - c))
                cp.wait_send()
                cp.wait_recv()

    res = _pallas(
        body,
        name=name,
        out_shape=[pltpu.HBM(a.shape, a.dtype) for a in halves] + [pltpu.HBM(a.shape, a.dtype) for a in lands],
        in_specs=[HBM_SPEC] * (2 * n) + [SEM_SPEC, SEM_SPEC, ANY],
        out_specs=[HBM_SPEC] * (2 * n),
        input_output_aliases={i: i for i in range(2 * n)},
        compiler_params=pltpu.CompilerParams(has_side_effects=EFFECT),
    )(*halves, *lands, send_sem, recv_sem, after)
    return list(res[:n]), list(res[n:])


def _small_exchange_copies(waiting):
    def copies(refs, send_sem, recv_sem):
        p, land = refs
        x, y, c, q, _ = _place()
        me = 2 * q + c
        out = []
        for dd in range(1, 2 * N_CHIPS):
            dev = (x ^ ((dd >> 2) & 1), y ^ ((dd >> 1) & 1), c ^ (dd & 1))
            dst = land.at[me ^ dd] if waiting else land.at[me]
            out.append(_remote(p, dst, send_sem.at[dd - 1], recv_sem.at[dd - 1], dev))
        return out

    return copies


def _small_sum(name, me_idx, p, land):
    rows = p.shape[0]
    n_dev = 2 * N_CHIPS

    def body(me_ref, p_ref, land_ref, o_ref):
        me = me_ref[0]
        total = None
        for dev in range(n_dev):
            term = jnp.where(me == dev, p_ref[...], land_ref[dev])
            total = term if total is None else total + term
        o_ref[...] = total

    return _pallas(
        body,
        name=name,
        grid_spec=pltpu.PrefetchScalarGridSpec(
            num_scalar_prefetch=1,
            grid=(1,),
            in_specs=[pl.BlockSpec((rows, LANES), lambda i, m: (0, 0)), pl.BlockSpec((n_dev, rows, LANES), lambda i, m: (0, 0, 0))],
            out_specs=pl.BlockSpec((rows, LANES), lambda i, m: (0, 0)),
        ),
        out_shape=jax.ShapeDtypeStruct(p.shape, F32),
    )(me_idx, p, land)


def _pack(parts):
    rows = []
    for a in parts:
        flat = a.reshape(-1).astype(F32)
        n = flat.shape[0]
        padded = -(-n // (8 * LANES)) * (8 * LANES)
        rows.append(jnp.pad(flat, (0, padded - n)).reshape(-1, LANES))
    return jnp.concatenate(rows, axis=0)


def _unpack(packed, shapes):
    out, row = [], 0
    for shp in shapes:
        n = int(np.prod(shp))
        nrows = -(-n // (8 * LANES)) * 8
        out.append(packed[row : row + nrows].reshape(-1)[:n].reshape(shp))
        row += nrows
    return out


def kernel(x, w_in, norm_mix, sgu_v_gain, sgu_w_s, sgu_b_s, w_a_out, attn_sink, rel_bias, w_b_out, w_o, norm_ffn, w_gate, w_up, w_down, norm_final, loss_target, m_w_in, m_norm_mix, m_sgu_v_gain, m_sgu_w_s, m_sgu_b_s, m_w_a_out, m_attn_sink, m_rel_bias, m_w_b_out, m_w_o, m_norm_ffn, m_w_gate, m_w_up, m_w_down, m_norm_final, v_w_in, v_norm_mix, v_sgu_v_gain, v_sgu_w_s, v_sgu_b_s, v_w_a_out, v_attn_sink, v_rel_bias, v_w_b_out, v_w_o, v_norm_ffn, v_w_gate, v_w_up, v_w_down, v_norm_final):
    s, d = x.shape[1], x.shape[2]
    w_sgu = sgu_v_gain.shape[1]
    groups = sgu_w_s.shape[1]
    heads = attn_sink.shape[1]
    grp = heads // N_KV_HEADS
    w_att = heads * HEAD_DIM
    w_kv = N_KV_HEADS * HEAD_DIM
    d_ff = w_gate.shape[2] * N_CHIPS
    n_in = w_in.shape[2] * N_CHIPS
    off_q = 2 * w_sgu
    off_k = off_q + w_att
    off_g = off_k + 2 * w_kv
    assert n_in == off_g + 2 * d and groups * BLK == w_sgu and s % BLK == 0

    x2d = x.reshape(s, d)
    tgt = loss_target.reshape(s, d)
    c_idx = lax.axis_index("c").astype(I32).reshape(1)
    q_idx = (2 * lax.axis_index("x") + lax.axis_index("y")).astype(I32).reshape(1)
    qc_idx = jnp.concatenate([q_idx, c_idx])

    W_IN, W_A, W_B, W_O, W_GATE, W_UP, W_DOWN = range(7)
    names = ["w_in", "w_a", "w_b", "w_o", "w_gate", "w_up", "w_down"]
    kinds = ["col", "col", "col", "row", "col", "col", "row"]
    big_w = [w_in[0], w_a_out[0], w_b_out[0], w_o[0], w_gate[0], w_up[0], w_down[0]]
    big_m = [m_w_in[0], m_w_a_out[0], m_w_b_out[0], m_w_o[0], m_w_gate[0], m_w_up[0], m_w_down[0]]
    big_v = [v_w_in[0], v_w_a_out[0], v_w_b_out[0], v_w_o[0], v_w_gate[0], v_w_up[0], v_w_down[0]]
    full_in = _cast_into_full("cast_w_in", q_idx, big_w[W_IN], kinds[W_IN])
    in_send, in_recv, (full_in,), token = _gather_start("gather_start_in", [full_in], [kinds[W_IN]], rels=(0, 1))
    rest = [_cast_into_full("cast_" + names[i], q_idx, big_w[i], kinds[i], after=(token,)) for i in range(1, 7)]
    h1 = _rms_fwd("rms_mix", x2d, norm_mix, after=(token,))
    (full_in,) = _gather_wait("gather_wait_in", [full_in], [kinds[W_IN]], [0], in_send, in_recv, h1, rels=(0, 1))
    relay_send, relay_recv, (full_in,) = _split_start("gather_relay_in", [full_in], 2, _relay_copies([kinds[W_IN]], False))
    full_down = rest.pop()
    ag_send, ag_recv, rest, token = _gather_start("gather_start_rest", rest, kinds[1:6], rels=(0, 1), after=(full_in,))
    (full_in,) = _split_wait("gather_relay_wait_in", [full_in], relay_send, relay_recv, _relay_copies([kinds[W_IN]], True), token)
    (g_in,) = _gather_forward("gather_fwd_in", [full_in], [kinds[W_IN]])
    fulls = [g_in] + rest

    def relay_begin(tag, ids, after):
        ks = [kinds[i] for i in ids]
        bufs = _gather_wait("gather_wait_" + tag, [fulls[i] for i in ids], ks, [i - 1 for i in ids], ag_send, ag_recv, after,
                            rels=(0, 1))
        send, recv, bufs = _split_start("gather_relay_" + tag, bufs, 2 * len(ids), _relay_copies(ks, False))
        return tag, ks, send, recv, bufs

    def relay_end(state, after):
        tag, ks, send, recv, bufs = state
        bufs = _split_wait("gather_relay_wait_" + tag, bufs, send, recv, _relay_copies(ks, True), after)
        return _gather_forward("gather_fwd_" + tag, bufs, ks)

    def relay_end_async(state, after):
        tag, ks, send, recv, bufs = state
        bufs = _split_wait("gather_relay_wait_" + tag, bufs, send, recv, _relay_copies(ks, True), after)
        send, recv, bufs = _split_start("gather_fwd_start_" + tag, bufs, 3 * len(ks), _forward_copies(ks, False), sibling_only=True)
        return tag, ks, send, recv, bufs

    def forwarded(state, after):
        tag, ks, send, recv, bufs = state
        return _split_wait("gather_fwd_wait_" + tag, bufs, send, recv, _forward_copies(ks, True), after)

    ws_b = sgu_w_s[0].astype(BF16)
    wst_b = jnp.swapaxes(sgu_w_s[0], 1, 2).astype(BF16)
    b_col = sgu_b_s[0].reshape(groups, BLK, 1)
    bias_tab, bucket = _band_tables(rel_bias)
    sink = attn_sink[0]

    tm = _tile(s, (1024, 512, 256, 128))

    tn = _tile(n_in, (768, 640, 512))
    z = _mm(
        "mm_z", (s // tm, n_in // tn, 1), [h1, g_in],
        [pl.BlockSpec((tm, d), lambda i, j, k: (i, 0)), pl.BlockSpec((d, tn), lambda i, j, k: (0, j))],
        [jax.ShapeDtypeStruct((s, n_in), BF16)], [pl.BlockSpec((tm, tn), lambda i, j, k: (i, j))],
        [(0, 1, NN, 0)], 1, (tm, tn), 1, lambda ins, vals, outs, cs: _put(outs[0], cs, vals[0]),
        _mm_vmem([((tm, d), BF16, 2), ((d, tn), BF16, 2), ((tm, tn), F32, 3)]),
    )[0]
    mix_relay = relay_begin("mix", [W_A, W_B, W_O], z)

    a_act = _sgu_fwd(z, sgu_v_gain, ws_b, b_col, w_sgu, after=(mix_relay[4][0],))

    kv_b = z[:, off_k:off_g]
    k_pad = jnp.pad(kv_b[:, :w_kv], ((BLK, BLK), (0, 0)))
    v_pad = jnp.pad(kv_b[:, w_kv:], ((BLK, BLK), (0, 0)))
    q_blk0 = off_q // (grp * HEAD_DIM)
    att = _attn_fwd(sink, z, k_pad, v_pad, bias_tab, grp, q_blk0)

    gate_relay = relay_begin("gate", [W_GATE], att)
    up_relay = relay_begin("up", [W_UP], gate_relay[4][0])
    down_send, down_recv, (full_down,), token = _gather_start(
        "gather_start_down", [full_down], [kinds[W_DOWN]], after=(a_act, up_relay[4][0])
    )
    g_a, g_b, g_o = relay_end(mix_relay, token)

    tg = _tile(d, (512,))
    ga0, gb0 = off_g // tg, (off_g + d) // tg

    def ep_gate(ins, vals, outs, cs):
        sa, sb = _sigmoid(ins[4][:, cs].astype(F32)), _sigmoid(ins[5][:, cs].astype(F32))
        _put(outs[0], cs, sa * vals[0] + sb * vals[1])
        _put(outs[1], cs, vals[0])
        _put(outs[2], cs, vals[1])

    t_out = pl.BlockSpec((tm, tg), lambda i, j, k: (i, j))
    m_act, y_a, y_b = _mm(
        "mm_branches", (s // tm, d // tg, 1), [a_act, g_a, att, g_b, z, z],
        [pl.BlockSpec((tm, w_sgu), lambda i, j, k: (i, 0)), pl.BlockSpec((w_sgu, tg), lambda i, j, k: (0, j)),
         pl.BlockSpec((tm, w_att), lambda i, j, k: (i, 0)), pl.BlockSpec((w_att, tg), lambda i, j, k: (0, j)),
         pl.BlockSpec((tm, tg), lambda i, j, k: (i, ga0 + j)), pl.BlockSpec((tm, tg), lambda i, j, k: (i, gb0 + j))],
        [jax.ShapeDtypeStruct((s, d), BF16)] * 3,
        [t_out, t_out, t_out], [(0, 1, NN, 0), (2, 3, NN, 1)], 2, (tm, tg), 1, ep_gate,
        _mm_vmem([((tm, w_sgu), BF16, 4), ((w_sgu, tg), BF16, 4), ((tm, tg), F32, 12)]),    )

    tn = _tile(d, (1024, 512))

    def ep_residual(ins, vals, outs, cs):
        _put(outs[0], cs, ins[2][:, cs] + vals[0])

    gate_fwd = relay_end_async(gate_relay, m_act)
    x2 = _mm(
        "mm_wo", (s // tm, d // tn, 1), [m_act, g_o, x2d],
        [pl.BlockSpec((tm, d), lambda i, j, k: (i, 0)), pl.BlockSpec((d, tn), lambda i, j, k: (0, j)),
         pl.BlockSpec((tm, tn), lambda i, j, k: (i, j))],
        [jax.ShapeDtypeStruct((s, d), F32)], [pl.BlockSpec((tm, tn), lambda i, j, k: (i, j))],
        [(0, 1, NN, 0)], 1, (tm, tn), 1, ep_residual,
        _mm_vmem([((tm, d), BF16, 2), ((d, tn), BF16, 2), ((tm, tn), F32, 5)]),
        after=(gate_fwd[4][0],),
    )[0]
    (g_gate,) = forwarded(gate_fwd, x2)
    up_fwd = relay_end_async(up_relay, g_gate)
    h2 = _rms_fwd("rms_ffn", x2, norm_ffn, after=(up_fwd[4][0],))
    (g_up,) = forwarded(up_fwd, h2)

    tf = _tile(d_ff, (512,))

    def ep_swiglu(ins, vals, outs, cs):
        gt, up = vals
        _put(outs[0], cs, gt)
        _put(outs[1], cs, up)
        _put(outs[2], cs, (gt * _sigmoid(gt)) * up)

    f_out = pl.BlockSpec((tm, tf), lambda i, j, k: (i, j))
    gt, up, f_act = _mm(
        "mm_gate_up", (s // tm, d_ff // tf, 1), [h2, g_gate, g_up],
        [pl.BlockSpec((tm, d), lambda i, j, k: (i, 0)), pl.BlockSpec((d, tf), lambda i, j, k: (0, j)),
         pl.BlockSpec((d, tf), lambda i, j, k: (0, j))],
        [jax.ShapeDtypeStruct((s, d_ff), BF16)] * 3,
        [f_out, f_out, f_out], [(0, 1, NN, 0), (0, 2, NN, 1)], 2, (tm, tf), 1, ep_swiglu,
        _mm_vmem([((tm, d), BF16, 2), ((d, tf), BF16, 4), ((tm, tf), F32, 8)]),    )
    (g_down,) = _gather_forward(
        "gather_fwd_ffn_out",
        _gather_wait("gather_wait_ffn_out", [full_down], [kinds[W_DOWN]], [0], down_send, down_recv, f_act),
        [kinds[W_DOWN]],
    )

    tkf = _tile(d_ff, (1408, 1024, 512))
    tml, tnl = _tile(s, (512, 256, 128)), _tile(d, (512,))
    x3 = _mm(
        "mm_down", (s // tml, d // tnl, 1), [f_act, g_down, x2],
        [pl.BlockSpec((tml, d_ff), lambda i, j, k: (i, 0)), pl.BlockSpec((d_ff, tnl), lambda i, j, k: (0, j)),
         pl.BlockSpec((tml, tnl), lambda i, j, k: (i, j))],
        [jax.ShapeDtypeStruct((s, d), F32)], [pl.BlockSpec((tml, tnl), lambda i, j, k: (i, j))],
        [(0, 1, NN, 0)], 1, (tml, tnl), 1, ep_residual,
        _mm_vmem([((tml, d_ff), BF16, 2), ((d_ff, tnl), BF16, 2), ((tml, tnl), F32, 6)]),    )[0]

    dx3, dx3b, dg_final, loss_part = _head(x3, norm_final.reshape(1, d), tgt)

    def reduce_a(tag, ids, grads):
        ks = [kinds[i] for i in ids]
        lands = [lax.empty((g.shape[0] // 2, g.shape[1]) if k == "col" else (g.shape[0], g.shape[1] // 2), BF16)
                 for g, k in zip(grads, ks)]
        send, recv, bufs = _split_start("pair_send_" + tag, list(grads) + lands, len(ids), _pair_exchange_copies(ks), sibling_only=True)
        return {"tag": tag, "ids": ids, "ks": ks, "pair": (send, recv, bufs), "token": bufs[0]}

    def reduce_b(st, after):
        tag, ids, ks = st["tag"], st["ids"], st["ks"]
        send, recv, bufs = st["pair"]
        bufs = _split_wait("pair_wait_" + tag, bufs, send, recv, _pair_exchange_copies(ks), after)
        grads, from_sib = bufs[: len(ids)], bufs[len(ids) :]
        halves = [_pair_add("pair_add_" + names[i], c_idx, g, r, k) for i, g, r, k in zip(ids, grads, from_sib, ks)]
        st["chip"] = _chip_send_start("chip_send_" + tag, halves, ks)
        st["token"] = st["chip"][2][0]

    def reduce_c(st, after):
        tag, ids, ks = st["tag"], st["ids"], st["ks"]
        send, recv, halves, lands = st["chip"]
        halves, lands = _chip_send_wait("chip_wait_" + tag, halves, lands, ks, send, recv, after)
        pieces = [_chip_sum("chip_sum_" + names[i], qc_idx, h, r, k) for i, h, r, k in zip(ids, halves, lands, ks)]
        st["share"] = _split_start("share_send_" + tag, pieces, len(ids), _pair_share_copies(ks, False), sibling_only=True)
        st["token"] = st["share"][2][0]

    def reduce_d(st, after):
        send, recv, bufs = st["share"]
        return _split_wait("share_wait_" + st["tag"], bufs, send, recv, _pair_share_copies(st["ks"], True), after)

    grads_big, upd = [None] * 7, [None] * 7

    def finish(st, after):
        shared = reduce_d(st, after)
        after = shared[0]
        for i, g in zip(st["ids"], shared):
            upd[i] = _adamw("adamw_" + names[i], big_w[i], g, big_m[i], big_v[i], after=(after,))
            grads_big[i] = upd[i][3]
            after = upd[i][0]
        return after

    def ep_swiglu_bwd(ins, vals, outs, cs):
        df = vals[0]
        gtv, upv = ins[2][:, cs].astype(F32), ins[3][:, cs].astype(F32)
        sg = _sigmoid(gtv)
        _put(outs[0], cs, df * upv * (sg + gtv * sg * (1.0 - sg)))
        _put(outs[1], cs, df * (gtv * sg))

    dgt, dup = _mm(
        "mm_dswiglu", (s // tm, d_ff // tf, 1), [dx3b, g_down, gt, up],
        [pl.BlockSpec((tm, d), lambda i, j, k: (i, 0)), pl.BlockSpec((tf, d), lambda i, j, k: (j, 0)), f_out, f_out],
        [jax.ShapeDtypeStruct((s, d_ff), BF16), jax.ShapeDtypeStruct((s, d_ff), BF16)], [f_out, f_out],
        [(0, 1, NT, 0)], 1, (tm, tf), 1, ep_swiglu_bwd,
        _mm_vmem([((tm, d), BF16, 2), ((tf, d), BF16, 2), ((tm, tf), F32, 8)]),    )

    def ep_store(ins, vals, outs, cs):
        for o, v in zip(outs, vals):
            _put(o, cs, v)

    twn = _tile(d, (1024, 512))
    gw_down = _mm(
        "mm_gw_down", (d_ff // tkf, d // twn, 1), [f_act, dx3b],
        [pl.BlockSpec((s, tkf), lambda i, j, k: (0, i)), pl.BlockSpec((s, twn), lambda i, j, k: (0, j))],
        [jax.ShapeDtypeStruct((d_ff, d), BF16)], [pl.BlockSpec((tkf, twn), lambda i, j, k: (i, j))],
        [(0, 1, TN, 0)], 1, (tkf, twn), 1, ep_store,
        _mm_vmem([((s, tkf), BF16, 3), ((s, twn), BF16, 2), ((tkf, twn), F32, 3)]),
    )[0]
    red_down = reduce_a("down", [W_DOWN], [gw_down])

    tn2 = _tile(d, (256,))
    dh2_specs = [pl.BlockSpec((tm, d_ff), lambda i, j, k: (i, 0)), pl.BlockSpec((tn2, d_ff), lambda i, j, k: (j, 0))]
    dh2_tile = pl.BlockSpec((tm, tn2), lambda i, j, k: (i, j))
    dh2_vmem = _mm_vmem([((tm, d_ff), BF16, 2), ((tn2, d_ff), BF16, 2), ((tm, tn2), F32, 7)])
    dh2 = _mm(
        "mm_dh2_gate", (s // tm, d // tn2, 1), [dgt, g_gate], dh2_specs,
        [jax.ShapeDtypeStruct((s, d), F32)], [dh2_tile], [(0, 1, NT, 0)], 1, (tm, tn2), 1, ep_store, dh2_vmem,
        after=(red_down["token"],),
    )[0]
    dh2 = _mm(
        "mm_dh2_up", (s // tm, d // tn2, 1), [dup, g_up, dh2], dh2_specs + [dh2_tile],
        [jax.ShapeDtypeStruct((s, d), F32)], [dh2_tile], [(0, 1, NT, 0)], 1, (tm, tn2), 1, ep_residual, dh2_vmem,
    )[0]
    reduce_b(red_down, dh2)

    twr = _tile(d, (1024, 512))
    w_tile = pl.BlockSpec((twr, tf), lambda i, j, k: (i, j))
    gw_gate, gw_up = _mm(
        "mm_gw_gate_up", (d // twr, d_ff // tf, 1), [h2, dgt, dup],
        [pl.BlockSpec((s, twr), lambda i, j, k: (0, i)), pl.BlockSpec((s, tf), lambda i, j, k: (0, j)),
         pl.BlockSpec((s, tf), lambda i, j, k: (0, j))],
        [jax.ShapeDtypeStruct((d, d_ff), BF16), jax.ShapeDtypeStruct((d, d_ff), BF16)], [w_tile, w_tile],
        [(0, 1, TN, 0), (0, 2, TN, 1)], 2, (twr, tf), 1, ep_store,
        _mm_vmem([((s, twr), BF16, 3), ((s, tf), BF16, 4), ((twr, tf), F32, 6)]),
        after=(red_down["token"],),
    )
    red_ffn = reduce_a("ffn_in", [W_GATE, W_UP], [gw_gate, gw_up])

    dx2, dx2b, dg_ffn = _rms_bwd("rms_ffn_bwd", x2, norm_ffn, dh2, dx3, after=(red_ffn["token"],))

    nj = d // tg

    def lo(j):
        return jnp.minimum(j, nj - 1)

    def gate_bwd_body(dx_ref, wo_ref, ga_ref, gb_ref, ya_ref, yb_ref, dya_ref, dyb_ref, dz_ref, keep):
        j = pl.program_id(1)

        @pl.when(j < nj)
        def _():
            dm = lax.dot_general(dx_ref[...], wo_ref[...], NT, preferred_element_type=F32)
            sa, sb = _sigmoid(ga_ref[...].astype(F32)), _sigmoid(gb_ref[...].astype(F32))
            dya_ref[...] = (dm * sa).astype(BF16)
            dyb_ref[...] = (dm * sb).astype(BF16)
            dz_ref[...] = (dm * ya_ref[...].astype(F32) * (sa * (1.0 - sa))).astype(BF16)
            keep[lo(j)] = (dm * yb_ref[...].astype(F32) * (sb * (1.0 - sb))).astype(BF16)

        @pl.when(j >= nj)
        def _():
            dz_ref[...] = keep[jnp.maximum(j - nj, 0)]

    t_lo = pl.BlockSpec((tm, tg), lambda i, j: (i, lo(j)))
    dya, dyb, dz = _pallas(
        gate_bwd_body,
        name="mm_dgate",
        grid=(s // tm, 2 * nj),
        in_specs=[
            pl.BlockSpec((tm, d), lambda i, j: (i, 0)),
            pl.BlockSpec((tg, d), lambda i, j: (lo(j), 0)),
            pl.BlockSpec((tm, tg), lambda i, j: (i, ga0 + lo(j))),
            pl.BlockSpec((tm, tg), lambda i, j: (i, gb0 + lo(j))),
            t_lo,
            t_lo,
        ],
        out_specs=[t_lo, t_lo, pl.BlockSpec((tm, tg), lambda i, j: (i, ga0 + j))],
        out_shape=[jax.ShapeDtypeStruct((s, d), BF16), jax.ShapeDtypeStruct((s, d), BF16), jax.ShapeDtypeStruct((s, n_in), BF16)],
        scratch_shapes=[pltpu.VMEM((nj, tm, tg), BF16)],
        compiler_params=_params(_mm_vmem([((tm, d), BF16, 2), ((tg, d), BF16, 2), ((tm, tg), F32, 14), ((nj, tm, tg), BF16, 1)])),
    )(dx2b, g_o, z, z, y_a, y_b)
    reduce_b(red_ffn, dya)
    reduce_c(red_down, red_ffn["token"])

    gw_o = _mm(
        "mm_gw_o", (d // twr, d // twn, 1), [m_act, dx2b],
        [pl.BlockSpec((s, twr), lambda i, j, k: (0, i)), pl.BlockSpec((s, twn), lambda i, j, k: (0, j))],
        [jax.ShapeDtypeStruct((d, d), BF16)], [pl.BlockSpec((twr, twn), lambda i, j, k: (i, j))],
        [(0, 1, TN, 0)], 1, (twr, twn), 1, ep_store,
        _mm_vmem([((s, twr), BF16, 3), ((s, twn), BF16, 2), ((twr, twn), F32, 3)]),
        after=(red_down["token"],),
    )[0]
    after_down = finish(red_down, gw_o)

    tb = _tile(w_sgu, (1024, 512))
    b_out = pl.BlockSpec((tm, tb), lambda i, j, k: (i, j))

    da, datt = _mm(
        "mm_dbranches", (s // tm, w_sgu // tb, 1), [dya, g_a, dyb, g_b],
        [pl.BlockSpec((tm, d), lambda i, j, k: (i, 0)), pl.BlockSpec((tb, d), lambda i, j, k: (j, 0)),
         pl.BlockSpec((tm, d), lambda i, j, k: (i, 0)), pl.BlockSpec((tb, d), lambda i, j, k: (j, 0))],
        [jax.ShapeDtypeStruct((s, w_sgu), BF16), jax.ShapeDtypeStruct((s, w_att), BF16)], [b_out, b_out],
        [(0, 1, NT, 0), (2, 3, NT, 1)], 2, (tm, tb), 1, ep_store,
        _mm_vmem([((tm, d), BF16, 4), ((tb, d), BF16, 4), ((tm, tb), F32, 6)]),
        after=(after_down,),
    )

    wb_tile = pl.BlockSpec((tb, twn), lambda i, j, k: (i, j))
    gw_a, gw_b = _mm(
        "mm_gw_branches", (w_sgu // tb, d // twn, 1), [a_act, dya, att, dyb],
        [pl.BlockSpec((s, tb), lambda i, j, k: (0, i)), pl.BlockSpec((s, twn), lambda i, j, k: (0, j)),
         pl.BlockSpec((s, tb), lambda i, j, k: (0, i)), pl.BlockSpec((s, twn), lambda i, j, k: (0, j))],
        [jax.ShapeDtypeStruct((w_sgu, d), BF16), jax.ShapeDtypeStruct((w_att, d), BF16)], [wb_tile, wb_tile],
        [(0, 1, TN, 0), (2, 3, TN, 1)], 2, (tb, twn), 1, ep_store,
        _mm_vmem([((s, tb), BF16, 5), ((s, twn), BF16, 4), ((tb, twn), F32, 6)]),
        after=(da,),
    )
    red_mix = reduce_a("mix", [W_O, W_A, W_B], [gw_o, gw_a, gw_b])

    dz, dws, dbs, dgain = _sgu_bwd(z, da, sgu_v_gain, ws_b, wst_b, b_col, w_sgu, dz, after=(red_mix["token"],))
    dz, dk_pad, dv_pad, dbias_tab, dsink = _attn_bwd(sink, z, k_pad, v_pad, bias_tab, datt, dz, grp, q_blk0)
    dz = _dkv_to_dz(dk_pad, dv_pad, dz, off_k // (2 * w_kv))
    drel = _relbias_bwd(dbias_tab, bucket)
    reduce_b(red_mix, dz)
    reduce_c(red_ffn, red_mix["token"])

    small_w = [norm_mix, sgu_v_gain, sgu_w_s, sgu_b_s, attn_sink, rel_bias, norm_ffn, norm_final]
    small_m = [m_norm_mix, m_sgu_v_gain, m_sgu_w_s, m_sgu_b_s, m_attn_sink, m_rel_bias, m_norm_ffn, m_norm_final]
    small_v = [v_norm_mix, v_sgu_v_gain, v_sgu_w_s, v_sgu_b_s, v_attn_sink, v_rel_bias, v_norm_ffn, v_norm_final]
    small_shapes = [w.shape for w in small_w]
    early = [dgain, dws, dbs, dsink[:, 0], drel[:, :REL_BUCKETS].T, dg_ffn, dg_final]
    p_early = _pack([g.reshape(shp) for g, shp in zip(early, small_shapes[1:])] + [loss_part[0, :1]])
    land = jnp.zeros((2 * N_CHIPS,) + p_early.shape, F32)
    sm_send, sm_recv, (p_early, land) = _split_start("small_send", [p_early, land], 2 * N_CHIPS - 1, _small_exchange_copies(False))

    tzn = _tile(n_in, (768, 640, 512))
    gw_in = _mm(
        "mm_gw_in", (d // twr, n_in // tzn, 1), [h1, dz],
        [pl.BlockSpec((s, twr), lambda i, j, k: (0, i)), pl.BlockSpec((s, tzn), lambda i, j, k: (0, j))],
        [jax.ShapeDtypeStruct((d, n_in), BF16)], [pl.BlockSpec((twr, tzn), lambda i, j, k: (i, j))],
        [(0, 1, TN, 0)], 1, (twr, tzn), 1, ep_store,
        _mm_vmem([((s, twr), BF16, 3), ((s, tzn), BF16, 2), ((twr, tzn), F32, 3)]),
        after=(red_ffn["token"], p_early),
    )[0]
    red_in = reduce_a("w_in", [W_IN], [gw_in])

    reduce_c(red_mix, red_in["token"])
    reduce_b(red_in, finish(red_mix, red_in["token"]))

    dh1 = _mm(
        "mm_dh1", (s // tm, d // tn2, 1), [dz, g_in],
        [pl.BlockSpec((tm, n_in), lambda i, j, k: (i, 0)), pl.BlockSpec((tn2, n_in), lambda i, j, k: (j, 0))],
        [jax.ShapeDtypeStruct((s, d), F32)], [pl.BlockSpec((tm, tn2), lambda i, j, k: (i, j))],
        [(0, 1, NT, 0)], 1, (tm, tn2), 1, ep_store,
        _mm_vmem([((tm, n_in), BF16, 2), ((tn2, n_in), BF16, 2), ((tm, tn2), F32, 5)]),
        after=(red_in["token"],),
    )[0]

    grad_x, _, dg_mix = _rms_bwd("rms_mix_bwd", x2d, norm_mix, dh1, dx2)

    p_mix = _pack([dg_mix.reshape(small_shapes[0])])
    land_mix = jnp.zeros((2 * N_CHIPS,) + p_mix.shape, F32)
    mx_send, mx_recv, (p_mix, land_mix) = _split_start("mix_send", [p_mix, land_mix], 2 * N_CHIPS - 1, _small_exchange_copies(False))

    reduce_c(red_in, finish(red_ffn, p_mix))
    p_early, land = _split_wait("small_wait", [p_early, land], sm_send, sm_recv, _small_exchange_copies(True), red_in["token"])
    p_mix, land_mix = _split_wait("mix_wait", [p_mix, land_mix], mx_send, mx_recv, _small_exchange_copies(True), p_early)
    me_idx = 2 * q_idx + c_idx
    packed_g = jnp.concatenate([_small_sum("mix_sum", me_idx, p_mix, land_mix), _small_sum("small_sum", me_idx, p_early, land)], axis=0)
    g_small = _unpack(packed_g, small_shapes + [(1,)])
    loss = g_small[-1].reshape(())
    g_small = g_small[:-1]
    zero1 = jnp.zeros((1,), F32)
    pw, pg, pm, pv = _pack(small_w + [zero1]), _pack(g_small + [zero1]), _pack(small_m + [zero1]), _pack(small_v + [zero1])
    small_upd = _adamw("adamw_small", pw, pg, pm, pv)
    d_small, nm_small, nv_small = [_unpack(a, small_shapes) for a in small_upd[:3]]
    finish(red_in, small_upd[0])

    small_names = ["norm_mix", "sgu_v_gain", "sgu_w_s", "sgu_b_s", "attn_sink", "rel_bias", "norm_ffn", "norm_final"]
    table = {}
    for i, n in enumerate(names):
        table[n] = (grads_big[i][None], upd[i][0][None], upd[i][1][None], upd[i][2][None])
    for i, n in enumerate(small_names):
        table[n] = (g_small[i], d_small[i], nm_small[i], nv_small[i])
    order = ["w_in", "norm_mix", "sgu_v_gain", "sgu_w_s", "sgu_b_s", "w_a", "attn_sink", "rel_bias", "w_b", "w_o", "norm_ffn",
             "w_gate", "w_up", "w_down", "norm_final"]
    outs = [loss, grad_x.reshape(1, s, d)]
    for part in range(4):
        outs += [table[n][part] for n in order]
    return tuple(outs)
```

```python
import math

import jax
import jax.numpy as jnp
import numpy as np
from jax import lax
from jax.experimental import pallas as pl
from jax.experimental.pallas import tpu as pltpu

F32 = jnp.float32
BF16 = jnp.bfloat16
I32 = jnp.int32
MESH = pl.DeviceIdType.MESH

EPS = 1e-6
NEG = -1e30
BLK = 128
HEAD_DIM = 128
N_KV_HEADS = 2
REL_BUCKETS = 32
REL_MAX_DIST = 128
N_CHIPS = 4
ADAM_LR, ADAM_B1, ADAM_B2, ADAM_EPS, ADAM_WD, ADAM_STEP = 0.001, 0.9, 0.999, 1e-08, 0.01, 10

LANES = 128
VMEM_CAP = 60 * 1024 * 1024

NN = (((1,), (0,)), ((), ()))
NT = (((1,), (1,)), ((), ()))
TN = (((0,), (0,)), ((), ()))
ANY = pl.BlockSpec(memory_space=pl.ANY)
HBM_SPEC = pl.BlockSpec(memory_space=pltpu.HBM)
SEM_SPEC = pl.BlockSpec(memory_space=pltpu.SEMAPHORE)
EFFECT = pltpu.SideEffectType.DATAFLOW_SIDE_EFFECTING


def _tile(n, cands):
    for t in cands:
        if n % t == 0:
            return t
    return n


PIN_BYTES = 64 * 1024


def _pin_hbm(a):
    big = hasattr(a, "dtype") and jnp.issubdtype(a.dtype, jnp.floating) and _nbytes(a.shape, a.dtype) >= PIN_BYTES
    return pltpu.with_memory_space_constraint(a, pltpu.HBM) if big else a


def _pallas(body, *, out_shape, **kw):
    def pin(o):
        big = isinstance(o, jax.ShapeDtypeStruct) and jnp.issubdtype(o.dtype, jnp.floating) and _nbytes(o.shape, o.dtype) >= PIN_BYTES
        return pltpu.HBM(o.shape, o.dtype) if big else o

    shapes = type(out_shape)(pin(o) for o in out_shape) if isinstance(out_shape, (list, tuple)) else pin(out_shape)
    call = pl.pallas_call(body, out_shape=shapes, **kw)
    return lambda *args: call(*[_pin_hbm(a) for a in args])


def _params(vmem_bytes=None, **kw):
    if vmem_bytes is not None:
        kw["vmem_limit_bytes"] = int(min(max(vmem_bytes, 32 * 1024 * 1024), VMEM_CAP))
    return pltpu.CompilerParams(**kw)


def _nbytes(shape, dtype):
    return int(np.prod(shape)) * jnp.dtype(dtype).itemsize


def _sigmoid(x):
    return 1.0 / (1.0 + jnp.exp(-x))


_GC = 0.7978845608028654
_GA = 0.044715


def _gelu(x):
    return 0.5 * x * (1.0 + jnp.tanh(_GC * (x + _GA * (x * x * x))))


def _gelu_grad(x):
    t = jnp.tanh(_GC * (x + _GA * (x * x * x)))
    return 0.5 * (1.0 + t) + 0.5 * x * (1.0 - t * t) * (_GC * (1.0 + 3.0 * _GA * (x * x)))


def _bf(v):
    return v if v.dtype == BF16 else v.astype(BF16)


def _mm(name, grid, ins, in_specs, out_shape, out_specs, pairs, n_acc, tile, nk, epilogue, vmem_bytes, after=()):
    assert nk == 1
    n_in, n_out = len(ins) + len(after), len(out_shape)

    def body(*refs):
        in_refs, out_refs = refs[:n_in], refs[n_in : n_in + n_out]
        vals = [None] * n_acc
        for a_i, b_i, dn, acc_i in pairs:
            d = lax.dot_general(_bf(in_refs[a_i][...]), _bf(in_refs[b_i][...]), dn, preferred_element_type=F32)
            vals[acc_i] = d if vals[acc_i] is None else vals[acc_i] + d
        epilogue(in_refs, vals, out_refs, slice(None))

    return _pallas(
        body,
        name=name,
        grid=grid,
        in_specs=list(in_specs) + [ANY] * len(after),
        out_specs=out_specs,
        out_shape=out_shape,
        compiler_params=_params(vmem_bytes),
    )(*ins, *after)


def _put(ref, cs, v):
    ref[:, cs] = v.astype(ref.dtype)


def _mm_vmem(tiles):
    return sum(_nbytes(s, d) * c for s, d, c in tiles) + 4 * 1024 * 1024


def _rows8(v):
    r, d = v.shape
    return v.reshape(r // 8, 8, d).sum(axis=0)


def _rms_fwd(name, x, g, after=()):
    s, d = x.shape
    tm = _tile(s, (256, 128))

    def body(x_ref, g_ref, *rest):
        h_ref = rest[-1]
        xv = x_ref[...]
        r = lax.rsqrt(jnp.mean(xv * xv, axis=-1, keepdims=True) + EPS)
        h_ref[...] = ((xv * r) * g_ref[...]).astype(BF16)

    return _pallas(
        body,
        name=name,
        grid=(s // tm,),
        in_specs=[pl.BlockSpec((tm, d), lambda i: (i, 0)), pl.BlockSpec((1, d), lambda i: (0, 0))] + [ANY] * len(after),
        out_specs=pl.BlockSpec((tm, d), lambda i: (i, 0)),
        out_shape=jax.ShapeDtypeStruct((s, d), BF16),
    )(x, g, *after)


def _rms_bwd(name, x, g, dh, dres, after=()):
    s, d = x.shape
    tm = _tile(s, (256, 128))
    n = s // tm
    n_after = len(after)

    def body(x_ref, g_ref, dh_ref, dres_ref, *rest):
        dx_ref, dxb_ref, dg_ref, acc_ref = rest[n_after:]
        i = pl.program_id(0)
        xv = x_ref[...]
        r = lax.rsqrt(jnp.mean(xv * xv, axis=-1, keepdims=True) + EPS)
        xh = xv * r
        dhv = dh_ref[...]
        dxh = dhv * g_ref[...]
        dx = r * (dxh - xh * jnp.mean(dxh * xh, axis=-1, keepdims=True)) + dres_ref[...]
        dx_ref[...] = dx
        dxb_ref[...] = dx.astype(BF16)
        part = _rows8(dhv * xh)

        @pl.when(i == 0)
        def _():
            acc_ref[...] = part

        @pl.when(i > 0)
        def _():
            acc_ref[...] += part

        @pl.when(i == n - 1)
        def _():
            dg_ref[...] = jnp.sum(acc_ref[...], axis=0, keepdims=True)

    row = pl.BlockSpec((tm, d), lambda i: (i, 0))
    vec = pl.BlockSpec((1, d), lambda i: (0, 0))
    return _pallas(
        body,
        name=name,
        grid=(n,),
        in_specs=[row, vec, row, row] + [ANY] * n_after,
        out_specs=[row, row, vec],
        out_shape=[jax.ShapeDtypeStruct((s, d), F32), jax.ShapeDtypeStruct((s, d), BF16), jax.ShapeDtypeStruct((1, d), F32)],
        scratch_shapes=[pltpu.VMEM((8, d), F32)],
    )(x, g, dh, dres, *after)


def _head(x3, g, target):
    s, d = x3.shape
    tm = _tile(s, (256, 128))
    n = s // tm

    def body(x_ref, g_ref, t_ref, dx_ref, dxb_ref, dg_ref, loss_ref, acc_g, acc_l):
        i = pl.program_id(0)
        xv = x_ref[...]
        gv = g_ref[...]
        r = lax.rsqrt(jnp.mean(xv * xv, axis=-1, keepdims=True) + EPS)
        xh = xv * r
        e = xh * gv - t_ref[...]
        dy = e * (1.0 / d)
        dxh = dy * gv
        dx = r * (dxh - xh * jnp.mean(dxh * xh, axis=-1, keepdims=True))
        dx_ref[...] = dx
        dxb_ref[...] = dx.astype(BF16)
        pg = _rows8(dy * xh)
        plo = _rows8(e * e)

        @pl.when(i == 0)
        def _():
            acc_g[...] = pg
            acc_l[...] = plo

        @pl.when(i > 0)
        def _():
            acc_g[...] += pg
            acc_l[...] += plo

        @pl.when(i == n - 1)
        def _():
            dg_ref[...] = jnp.sum(acc_g[...], axis=0, keepdims=True)
            loss_ref[...] = jnp.full((1, LANES), (0.5 / d) * jnp.sum(acc_l[...]), F32)

    row = pl.BlockSpec((tm, d), lambda i: (i, 0))
    vec = pl.BlockSpec((1, d), lambda i: (0, 0))
    return _pallas(
        body,
        name="head",
        grid=(n,),
        in_specs=[row, vec, row],
        out_specs=[row, row, vec, pl.BlockSpec((1, LANES), lambda i: (0, 0))],
        out_shape=[
            jax.ShapeDtypeStruct((s, d), F32),
            jax.ShapeDtypeStruct((s, d), BF16),
            jax.ShapeDtypeStruct((1, d), F32),
            jax.ShapeDtypeStruct((1, LANES), F32),
        ],
        scratch_shapes=[pltpu.VMEM((8, d), F32), pltpu.VMEM((8, d), F32)],
    )(x3, g, target)


def _sgu_fwd(z, gain, ws_b, b_col, w_sgu, after=()):
    s = z.shape[0]
    groups = ws_b.shape[0]

    def body(zu_ref, zv_ref, gain_ref, ws_ref, b_ref, *rest):
        a_ref = rest[-1]
        vv = _gelu(zv_ref[...].astype(F32))
        r = lax.rsqrt(jnp.mean(vv * vv, axis=-1, keepdims=True) + EPS)
        vn = ((vv * r) * gain_ref[...]).astype(BF16)
        u = _gelu(zu_ref[...].astype(F32))
        for g in range(groups):
            sl = slice(g * BLK, (g + 1) * BLK)
            mixed = jnp.dot(ws_ref[g], vn[:, sl], preferred_element_type=F32) + b_ref[g]
            a_ref[:, sl] = (u[:, sl] * mixed).astype(BF16)

    return _pallas(
        body,
        name="sgu_fwd",
        grid=(s // BLK,),
        in_specs=[
            pl.BlockSpec((BLK, w_sgu), lambda c: (c, 0)),
            pl.BlockSpec((BLK, w_sgu), lambda c: (c, 1)),
            pl.BlockSpec((1, w_sgu), lambda c: (0, 0)),
            pl.BlockSpec((groups, BLK, BLK), lambda c: (0, 0, 0)),
            pl.BlockSpec((groups, BLK, 1), lambda c: (0, 0, 0)),
        ]
        + [ANY] * len(after),
        out_specs=pl.BlockSpec((BLK, w_sgu), lambda c: (c, 0)),
        out_shape=jax.ShapeDtypeStruct((s, w_sgu), BF16),
    )(z, z, gain, ws_b, b_col, *after)


def _sgu_bwd(z, da, gain, ws_b, wst_b, b_col, w_sgu, dz, after=()):
    s = z.shape[0]
    groups = ws_b.shape[0]
    n = s // BLK
    n_skip = 1 + len(after)

    def body(zu_ref, zv_ref, da_ref, gain_ref, ws_ref, wst_ref, b_ref, *rest):
        dz_ref, dws_ref, dbs_ref, dgain_ref, acc_gain, vv_s, gv_s, dxh_s = rest[n_skip:]
        c = pl.program_id(0)
        cols = [slice(g * BLK, (g + 1) * BLK) for g in range(groups)]

        ss = jnp.zeros((BLK, 1), F32)
        for sl in cols:
            zv = zv_ref[:, sl].astype(F32)
            vv = _gelu(zv)
            vv_s[:, sl] = vv
            gv_s[:, sl] = _gelu_grad(zv)
            ss = ss + jnp.sum(vv * vv, axis=-1, keepdims=True)
        r = lax.rsqrt(ss * (1.0 / w_sgu) + EPS)

        dot_dx = jnp.zeros((BLK, 1), F32)
        for g, sl in enumerate(cols):
            gain_g = gain_ref[:, sl]
            xh = vv_s[:, sl] * r
            vn = (xh * gain_g).astype(BF16)
            zu = zu_ref[:, sl].astype(F32)
            dav = da_ref[:, sl].astype(F32)
            dmix = dav * _gelu(zu)
            dmix_b = dmix.astype(BF16)
            mixed = jnp.dot(ws_ref[g], vn, preferred_element_type=F32) + b_ref[g]
            dz_ref[:, sl] = (dav * mixed * _gelu_grad(zu)).astype(BF16)
            dvn = jnp.dot(wst_ref[g], dmix_b, preferred_element_type=F32)
            dws_g = lax.dot_general(dmix_b, vn, NT, preferred_element_type=F32)
            dbs_g = jnp.sum(dmix, axis=1, keepdims=True)
            pg = _rows8(dvn * xh)

            @pl.when(c == 0)
            def _():
                dws_ref[g] = dws_g
                dbs_ref[g] = dbs_g
                acc_gain[:, sl] = pg

            @pl.when(c > 0)
            def _():
                dws_ref[g] += dws_g
                dbs_ref[g] += dbs_g
                acc_gain[:, sl] += pg

            dxh = dvn * gain_g
            dxh_s[:, sl] = dxh
            dot_dx = dot_dx + jnp.sum(dxh * xh, axis=-1, keepdims=True)

        mean_dx = dot_dx * (1.0 / w_sgu)
        for g, sl in enumerate(cols):
            dvv = r * (dxh_s[:, sl] - (vv_s[:, sl] * r) * mean_dx)
            dz_ref[:, w_sgu + g * BLK : w_sgu + (g + 1) * BLK] = (dvv * gv_s[:, sl]).astype(BF16)

        @pl.when(c == n - 1)
        def _():
            dgain_ref[...] = jnp.sum(acc_gain[...], axis=0, keepdims=True)

    full3 = pl.BlockSpec((groups, BLK, BLK), lambda c: (0, 0, 0))
    col3 = pl.BlockSpec((groups, BLK, 1), lambda c: (0, 0, 0))
    vec = pl.BlockSpec((1, w_sgu), lambda c: (0, 0))
    return _pallas(
        body,
        name="sgu_bwd",
        grid=(n,),
        in_specs=[
            pl.BlockSpec((BLK, w_sgu), lambda c: (c, 0)),
            pl.BlockSpec((BLK, w_sgu), lambda c: (c, 1)),
            pl.BlockSpec((BLK, w_sgu), lambda c: (c, 0)),
            vec,
            full3,
            full3,
            col3,
            ANY,
        ]
        + [ANY] * len(after),
        out_specs=[pl.BlockSpec((BLK, 2 * w_sgu), lambda c: (c, 0)), full3, col3, vec],
        out_shape=[
            jax.ShapeDtypeStruct(dz.shape, BF16),
            jax.ShapeDtypeStruct((groups, BLK, BLK), F32),
            jax.ShapeDtypeStruct((groups, BLK, 1), F32),
            jax.ShapeDtypeStruct((1, w_sgu), F32),
        ],
        scratch_shapes=[pltpu.VMEM((8, w_sgu), F32)] + [pltpu.VMEM((BLK, w_sgu), F32)] * 3,
        input_output_aliases={7: 0},
    )(z, z, da, gain, ws_b, wst_b, b_col, dz, *after)


def _attn_softmax(sink_ref, q_ref, k_ref, v_ref, bias_ref, s_len, grp):
    kv = pl.program_id(0)
    n = pl.program_id(1)
    start = pl.multiple_of(n * BLK, BLK)
    kb = k_ref[pl.ds(start, 3 * BLK), :]
    vb = v_ref[pl.ds(start, 3 * BLK), :]
    qv = q_ref[...]
    qs = jnp.concatenate([qv[:, g * HEAD_DIM : (g + 1) * HEAD_DIM] for g in range(grp)], axis=0).astype(BF16)
    sc = lax.dot_general(qs, kb, NT, preferred_element_type=F32) * (HEAD_DIM**-0.5)
    sc = sc + bias_ref[...].reshape(grp * BLK, 3 * BLK)
    kpos = start + lax.broadcasted_iota(I32, (1, 3 * BLK), 1) - BLK
    sc = jnp.where((kpos >= 0) & (kpos < s_len), sc, NEG)
    sink = jnp.concatenate([jnp.full((BLK, 1), sink_ref[kv * grp + g], F32) for g in range(grp)], axis=0)
    m = jnp.maximum(jnp.max(sc, axis=-1, keepdims=True), sink)
    p = jnp.exp(sc - m)
    esink = jnp.exp(sink - m)
    den = jnp.sum(p, axis=-1, keepdims=True) + esink
    return start, qs, kb, vb, p / den, esink / den


def _attn_specs(s, grp, q_blk0):
    qw = grp * HEAD_DIM
    return [
        pl.BlockSpec(memory_space=pltpu.SMEM),
        pl.BlockSpec((BLK, qw), lambda kv, n: (n, q_blk0 + kv)),
        pl.BlockSpec((s + 2 * BLK, HEAD_DIM), lambda kv, n: (0, kv)),
        pl.BlockSpec((s + 2 * BLK, HEAD_DIM), lambda kv, n: (0, kv)),
        pl.BlockSpec((grp, BLK, 3 * BLK), lambda kv, n: (kv, 0, 0)),
    ]


def _attn_fwd(sink, z, k_pad, v_pad, bias_tab, grp, q_blk0):
    s = z.shape[0]
    qw = grp * HEAD_DIM

    def body(sink_ref, q_ref, k_ref, v_ref, bias_ref, o_ref):
        _, _, _, vb, pn, _ = _attn_softmax(sink_ref, q_ref, k_ref, v_ref, bias_ref, s, grp)
        o = jnp.dot(pn.astype(BF16), vb, preferred_element_type=F32)
        for g in range(grp):
            o_ref[:, g * HEAD_DIM : (g + 1) * HEAD_DIM] = o[g * BLK : (g + 1) * BLK].astype(BF16)

    return _pallas(
        body,
        name="attn_fwd",
        grid=(N_KV_HEADS, s // BLK),
        in_specs=_attn_specs(s, grp, q_blk0),
        out_specs=pl.BlockSpec((BLK, qw), lambda kv, n: (n, kv)),
        out_shape=jax.ShapeDtypeStruct((s, N_KV_HEADS * qw), BF16),
    )(sink, z, k_pad, v_pad, bias_tab)


def _attn_bwd(sink, z, k_pad, v_pad, bias_tab, dout, dz, grp, q_blk0):
    s = z.shape[0]
    qw = grp * HEAD_DIM
    nb = s // BLK
    heads = N_KV_HEADS * grp

    def body(sink_ref, q_ref, k_ref, v_ref, bias_ref, do_ref, dz_in, dq_ref, dk_ref, dv_ref, dbias_ref, dsink_ref, dk_acc, dv_acc):
        del dz_in
        kv = pl.program_id(0)
        n = pl.program_id(1)
        start, qs, kb, vb, pn, psink = _attn_softmax(sink_ref, q_ref, k_ref, v_ref, bias_ref, s, grp)
        dov = do_ref[...]
        dos = jnp.concatenate([dov[:, g * HEAD_DIM : (g + 1) * HEAD_DIM] for g in range(grp)], axis=0)
        dp = lax.dot_general(dos, vb, NT, preferred_element_type=F32)
        dvb = lax.dot_general(pn.astype(BF16), dos, TN, preferred_element_type=F32)
        delta = jnp.sum(pn * dp, axis=-1, keepdims=True)
        ds = pn * (dp - delta)
        dsb = (ds * (HEAD_DIM**-0.5)).astype(BF16)
        dq = jnp.dot(dsb, kb, preferred_element_type=F32)
        dkb = lax.dot_general(dsb, qs, TN, preferred_element_type=F32)
        for g in range(grp):
            dq_ref[:, g * HEAD_DIM : (g + 1) * HEAD_DIM] = dq[g * BLK : (g + 1) * BLK].astype(BF16)

        @pl.when(n == 0)
        def _():
            dk_acc[...] = jnp.zeros_like(dk_acc)
            dv_acc[...] = jnp.zeros_like(dv_acc)
            dbias_ref[...] = jnp.zeros_like(dbias_ref)

        @pl.when((n == 0) & (kv == 0))
        def _():
            dsink_ref[...] = jnp.zeros_like(dsink_ref)

        dk_acc[pl.ds(start, 3 * BLK), :] += dkb
        dv_acc[pl.ds(start, 3 * BLK), :] += dvb
        dbias_ref[...] += ds.reshape(grp, BLK, 3 * BLK)
        row = lax.broadcasted_iota(I32, (heads, LANES), 0)
        sd = psink * delta
        upd = jnp.zeros((heads, LANES), F32)
        for g in range(grp):
            upd = jnp.where(row == kv * grp + g, -jnp.sum(sd[g * BLK : (g + 1) * BLK]), upd)
        dsink_ref[...] += upd

        @pl.when(n == nb - 1)
        def _():
            dk_ref[...] = dk_acc[...]
            dv_ref[...] = dv_acc[...]

    pad_spec = pl.BlockSpec((s + 2 * BLK, HEAD_DIM), lambda kv, n: (0, kv))
    kvw = N_KV_HEADS * HEAD_DIM
    return _pallas(
        body,
        name="attn_bwd",
        grid=(N_KV_HEADS, nb),
        in_specs=_attn_specs(s, grp, q_blk0) + [pl.BlockSpec((BLK, qw), lambda kv, n: (n, kv)), ANY],
        out_specs=[
            pl.BlockSpec((BLK, qw), lambda kv, n: (n, q_blk0 + kv)),
            pad_spec,
            pad_spec,
            pl.BlockSpec((grp, BLK, 3 * BLK), lambda kv, n: (kv, 0, 0)),
            pl.BlockSpec((heads, LANES), lambda kv, n: (0, 0)),
        ],
        out_shape=[
            jax.ShapeDtypeStruct(dz.shape, BF16),
            jax.ShapeDtypeStruct((s + 2 * BLK, kvw), F32),
            jax.ShapeDtypeStruct((s + 2 * BLK, kvw), F32),
            jax.ShapeDtypeStruct((heads, BLK, 3 * BLK), F32),
            jax.ShapeDtypeStruct((heads, LANES), F32),
        ],
        scratch_shapes=[pltpu.VMEM((s + 2 * BLK, HEAD_DIM), F32), pltpu.VMEM((s + 2 * BLK, HEAD_DIM), F32)],
        input_output_aliases={6: 0},
    )(sink, z, k_pad, v_pad, bias_tab, dout, dz)


def _dkv_to_dz(dk_pad, dv_pad, dz, blk_idx):
    s = dz.shape[0]
    kvw = dk_pad.shape[1]

    def body(dk_ref, dv_ref, dz_in, out_ref):
        del dz_in
        out_ref[:, :kvw] = dk_ref[...].astype(BF16)
        out_ref[:, kvw:] = dv_ref[...].astype(BF16)

    src = pl.BlockSpec((BLK, kvw), lambda i: (i + 1, 0))
    return _pallas(
        body,
        name="dkv_to_dz",
        grid=(s // BLK,),
        in_specs=[src, src, ANY],
        out_specs=pl.BlockSpec((BLK, 2 * kvw), lambda i: (i, blk_idx)),
        out_shape=jax.ShapeDtypeStruct(dz.shape, BF16),
        input_output_aliases={2: 0},
    )(dk_pad, dv_pad, dz)


def _relbias_bwd(dbias_tab, bucket):
    heads = dbias_tab.shape[0]

    def body(dt_ref, bk_ref, out_ref):
        lane = lax.broadcasted_iota(I32, (1, LANES), 1)
        bk = bk_ref[...]
        rows = []
        for h in range(heads):
            dt = dt_ref[h]
            acc = jnp.zeros((1, LANES), F32)
            for b in range(REL_BUCKETS):
                acc = jnp.where(lane == b, jnp.sum(jnp.where(bk == b, dt, 0.0)), acc)
            rows.append(acc)
        out_ref[...] = jnp.concatenate(rows, axis=0)

    return _pallas(body, name="relbias_bwd", out_shape=jax.ShapeDtypeStruct((heads, LANES), F32))(dbias_tab, bucket)


def _t5_bucket(rel):
    nb = REL_BUCKETS // 2
    ret = jnp.where(rel > 0, nb, 0)
    n = jnp.abs(rel)
    max_exact = nb // 2
    nf = jnp.maximum(n, 1).astype(F32)
    large = max_exact + (jnp.log(nf / max_exact) / math.log(REL_MAX_DIST / max_exact) * (nb - max_exact)).astype(I32)
    large = jnp.minimum(large, nb - 1)
    return ret + jnp.where(n < max_exact, n, large)


def _band_tables(rel_bias):
    qi = jnp.arange(BLK)[:, None]
    kj = jnp.arange(3 * BLK)[None, :]
    rel = kj - BLK - qi
    bucket = _t5_bucket(rel).astype(I32)
    heads = rel_bias.shape[1]
    masked = jnp.where(jnp.abs(rel) <= BLK, bucket, -1)

    def body(rb_ref, bk_ref, out_ref):
        bk = bk_ref[...]
        for h in range(heads):
            tab = jnp.full(bk.shape, NEG, F32)
            for b in range(REL_BUCKETS):
                tab = jnp.where(bk == b, rb_ref[b, h], tab)
            out_ref[h] = tab

    bias_tab = _pallas(
        body,
        name="bias_table",
        in_specs=[pl.BlockSpec(memory_space=pltpu.SMEM), pl.BlockSpec(memory_space=pltpu.VMEM)],
        out_specs=pl.BlockSpec(memory_space=pltpu.VMEM),
        out_shape=jax.ShapeDtypeStruct((heads, BLK, 3 * BLK), F32),
    )(rel_bias.astype(F32), masked)
    return bias_tab, bucket


EW_BLOCK_ELEMS = 512 * 1024


def _ew_tiles(shape, elems=EW_BLOCK_ELEMS // 2):
    r, c = shape
    tn = c if c <= 2048 else _tile(c, (2048, 1920, 1536, 1408, 1024, 512))
    tm = _tile(r, [t for t in (1024, 512, 256, 128, 64, 32, 16, 8) if t * tn <= elems] or [8])
    return tm, tn


def _cast_into_full(name, qidx, w, kind, after=()):
    r, c = w.shape
    tm, tn = _ew_tiles(w.shape, EW_BLOCK_ELEMS)
    nbi, nbj = r // tm, c // tn
    if kind == "col":
        full, out_spec = (r, c * N_CHIPS), pl.BlockSpec((tm, tn), lambda i, j, q: (i, q[0] * nbj + j))
    else:
        full, out_spec = (r * N_CHIPS, c), pl.BlockSpec((tm, tn), lambda i, j, q: (q[0] * nbi + i, j))

    def body(q_ref, w_ref, *rest):
        del q_ref
        rest[-1][...] = w_ref[...].astype(BF16)

    return _pallas(
        body,
        name=name,
        grid_spec=pltpu.PrefetchScalarGridSpec(
            num_scalar_prefetch=1,
            grid=(nbi, nbj),
            in_specs=[pl.BlockSpec((tm, tn), lambda i, j, q: (i, j))] + [ANY] * len(after),
            out_specs=out_spec,
        ),
        out_shape=jax.ShapeDtypeStruct(full, BF16),
    )(qidx, w, *after)


def _adamw(name, w, g, m, v, after=()):
    tm, tn = _ew_tiles(w.shape, EW_BLOCK_ELEMS)
    if _nbytes(w.shape, F32) <= 1024 * 1024:
        tm, tn = w.shape
    spec = pl.BlockSpec((tm, tn), lambda i, j: (i, j))
    n_after = len(after)

    def body(w_ref, g_ref, m_ref, v_ref, *rest):
        d_ref, nm_ref, nv_ref, g_out_ref = rest[n_after:]
        gv = g_ref[...]
        g_out_ref[...] = gv
        nm = ADAM_B1 * m_ref[...] + (1.0 - ADAM_B1) * gv
        nv = ADAM_B2 * v_ref[...] + (1.0 - ADAM_B2) * (gv * gv)
        m_hat = nm / (1.0 - ADAM_B1**ADAM_STEP)
        v_hat = nv / (1.0 - ADAM_B2**ADAM_STEP)
        d_ref[...] = -ADAM_LR * (m_hat / (jnp.sqrt(v_hat) + ADAM_EPS) + ADAM_WD * w_ref[...])
        nm_ref[...] = nm
        nv_ref[...] = nv

    out = jax.ShapeDtypeStruct(w.shape, F32)
    return _pallas(
        body, name=name, grid=(w.shape[0] // tm, w.shape[1] // tn), in_specs=[spec] * 4 + [ANY] * n_after,
        out_specs=[spec] * 4, out_shape=[out, out, out, out],
        compiler_params=_params(_mm_vmem([((tm, tn), F32, 24)])),
    )(w, g, m, v, *after)


def _pair_add(name, cidx, g_full, r_sib, kind):
    hr, hc = r_sib.shape
    tm, tn = _ew_tiles((hr, hc), 2 * EW_BLOCK_ELEMS)
    nbi, nbj = hr // tm, hc // tn
    if kind == "col":
        g_spec = pl.BlockSpec((tm, tn), lambda i, j, c: (c[0] * nbi + i, j))
    else:
        g_spec = pl.BlockSpec((tm, tn), lambda i, j, c: (i, c[0] * nbj + j))
    spec = pl.BlockSpec((tm, tn), lambda i, j, c: (i, j))

    def body(c_ref, g_ref, r_ref, o_ref):
        del c_ref
        o_ref[...] = (g_ref[...].astype(F32) + r_ref[...].astype(F32)).astype(BF16)

    return _pallas(
        body,
        name=name,
        grid_spec=pltpu.PrefetchScalarGridSpec(num_scalar_prefetch=1, grid=(nbi, nbj), in_specs=[g_spec, spec], out_specs=spec),
        out_shape=jax.ShapeDtypeStruct((hr, hc), BF16),
        compiler_params=_params(_mm_vmem([((tm, tn), BF16, 6), ((tm, tn), F32, 3)])),
    )(cidx, g_full, r_sib)


def _chip_sum(name, qidx, c_half, r_ici, kind):
    _, pr, pc = r_ici.shape
    tm, tn = _ew_tiles((pr, pc), 2 * EW_BLOCK_ELEMS)
    nbi, nbj = pr // tm, pc // tn
    if kind == "col":
        own_spec = pl.BlockSpec((tm, tn), lambda i, j, q: (i, q[0] * nbj + j))
        full, out_spec = (2 * pr, pc), pl.BlockSpec((tm, tn), lambda i, j, q: (q[1] * nbi + i, j))
    else:
        own_spec = pl.BlockSpec((tm, tn), lambda i, j, q: (q[0] * nbi + i, j))
        full, out_spec = (pr, 2 * pc), pl.BlockSpec((tm, tn), lambda i, j, q: (i, q[1] * nbj + j))

    def body(q_ref, own_ref, r_ref, o_ref):
        q = q_ref[0]
        own = own_ref[...].astype(F32)
        recv = [r_ref[r].astype(F32) for r in range(3)]
        total = None
        for chip in range(N_CHIPS):
            d = chip ^ q
            term = jnp.where(d == 0, own, jnp.where(d == 2, recv[0], jnp.where(d == 1, recv[1], recv[2])))
            total = term if total is None else total + term
        o_ref[...] = total

    return _pallas(
        body,
        name=name,
        grid_spec=pltpu.PrefetchScalarGridSpec(
            num_scalar_prefetch=1,
            grid=(nbi, nbj),
            in_specs=[own_spec, pl.BlockSpec((3, tm, tn), lambda i, j, q: (0, i, j))],
            out_specs=out_spec,
        ),
        out_shape=jax.ShapeDtypeStruct(full, F32),
        compiler_params=_params(_mm_vmem([((tm, tn), BF16, 8), ((tm, tn), F32, 6)])),
    )(qidx, c_half, r_ici)


_REL_MASK = (2, 1, 3)


def _place():
    x, y, c = lax.axis_index("x"), lax.axis_index("y"), lax.axis_index("c")
    chips = [(1 - x, y), (x, 1 - y), (1 - x, 1 - y)]
    return x, y, c, 2 * x + y, chips


def _shard_view(ref, kind, chip):
    if kind == "col":
        w = ref.shape[1] // N_CHIPS
        return ref.at[:, pl.ds(pl.multiple_of(chip * w, LANES), w)]
    h = ref.shape[0] // N_CHIPS
    return ref.at[pl.ds(pl.multiple_of(chip * h, 16), h), :]


def _row_half(ref, half):
    h = ref.shape[0] // 2
    return ref.at[pl.ds(pl.multiple_of(half * h, 16), h), :]


def _pair_half(ref, kind, half):
    if kind == "col":
        return _row_half(ref, half)
    w = ref.shape[1] // 2
    return ref.at[:, pl.ds(pl.multiple_of(half * w, LANES), w)]


def _remote(src, dst, send_sem, recv_sem, dev):
    return pltpu.make_async_remote_copy(src_ref=src, dst_ref=dst, send_sem=send_sem, recv_sem=recv_sem, device_id=dev, device_id_type=MESH)


def _hbm(a):
    return pltpu.with_memory_space_constraint(a, pltpu.HBM)


def _gather_start(name, fulls, kinds, rels=(0, 1, 2), after=()):
    n_w = len(fulls)

    def body(*refs):
        g = refs[:n_w]
        send_sem, recv_sem = refs[n_w + len(after)], refs[n_w + len(after) + 1]
        token = refs[-1]
        _, _, c, q, chips = _place()
        for w in range(n_w):
            mine = _row_half(_shard_view(g[w], kinds[w], q), c)
            for r in rels if isinstance(rels, tuple) else rels[w]:
                _remote(mine, mine, send_sem.at[3 * w + r], recv_sem.at[3 * w + r], (*chips[r], c)).start()
        token[...] = jnp.zeros_like(token)

    res = _pallas(
        body,
        name=name,
        out_shape=(
            pltpu.SemaphoreType.DMA((3 * n_w,)),
            pltpu.SemaphoreType.DMA((3 * n_w,)),
            *[pltpu.HBM(f.shape, f.dtype) for f in fulls],
            jax.ShapeDtypeStruct((8, LANES), F32),
        ),
        in_specs=[HBM_SPEC] * n_w + [ANY] * len(after),
        out_specs=(SEM_SPEC, SEM_SPEC, *[HBM_SPEC] * n_w, pl.BlockSpec(memory_space=pltpu.VMEM)),
        input_output_aliases={w: w + 2 for w in range(n_w)},
        compiler_params=pltpu.CompilerParams(has_side_effects=EFFECT),
    )(*[_hbm(f) for f in fulls], *after)
    return res[0], res[1], list(res[2 : 2 + n_w]), res[-1]


def _relay_copies(kinds, waiting):
    def copies(refs, send_sem, recv_sem):
        _, _, c, q, chips = _place()
        out = []
        for i, kind in enumerate(kinds):
            for k, (src_rel, dst_rel) in enumerate(((0, 1), (1, 0))):
                held = _row_half(_row_half(_shard_view(refs[i], kind, q ^ _REL_MASK[src_rel]), c), k)
                far = _row_half(_row_half(_shard_view(refs[i], kind, q ^ _REL_MASK[2]), c), k)
                dst = far if waiting else held
                out.append(_remote(held, dst, send_sem.at[2 * i + k], recv_sem.at[2 * i + k], (*chips[dst_rel], c)))
        return out

    return copies


def _forward_copies(kinds, waiting, rels=(0, 1, 2), sem0=0):
    def copies(refs, send_sem, recv_sem):
        x, y, c, q, _ = _place()
        out = []
        for i, kind in enumerate(kinds):
            for k, r in enumerate(rels):
                quarter = _shard_view(refs[i], kind, q ^ _REL_MASK[r])
                landed = _row_half(quarter, c)
                dst = _row_half(quarter, 1 - c) if waiting else landed
                sem = sem0 + len(rels) * i + k
                out.append(_remote(landed, dst, send_sem.at[sem], recv_sem.at[sem], (x, y, 1 - c)))
        return out

    return copies


def _relay_and_forward_copies(kinds, waiting):
    forward = _forward_copies(kinds, waiting, rels=(0, 1), sem0=2 * len(kinds))
    relay = _relay_copies(kinds, waiting)
    return lambda refs, send_sem, recv_sem: forward(refs, send_sem, recv_sem) + relay(refs, send_sem, recv_sem)


def _gather_wait(name, fulls, kinds, w_ids, send_sem, recv_sem, after, rels=(0, 1, 2)):
    n = len(fulls)

    def body(*refs):
        g = refs[:n]
        s_sem, r_sem = refs[n], refs[n + 1]
        x, y, c, q, _ = _place()
        for i, w in enumerate(w_ids):
            mine = _row_half(_shard_view(g[i], kinds[i], q), c)
            for r in rels:
                landed = _row_half(_shard_view(g[i], kinds[i], q ^ _REL_MASK[r]), c)
                cp = _remote(mine, landed, s_sem.at[3 * w + r], r_sem.at[3 * w + r], (x, y, 1 - c))
                cp.wait_send()
                cp.wait_recv()

    res = _pallas(
        body,
        name=name,
        out_shape=[pltpu.HBM(f.shape, f.dtype) for f in fulls],
        in_specs=[HBM_SPEC] * n + [SEM_SPEC, SEM_SPEC, ANY],
        out_specs=[HBM_SPEC] * n,
        input_output_aliases={i: i for i in range(n)},
        compiler_params=pltpu.CompilerParams(has_side_effects=EFFECT),
    )(*fulls, send_sem, recv_sem, after)
    return list(res)


def _gather_forward(name, fulls, kinds, rels=(0, 1, 2)):
    n = len(fulls)

    def body(*refs):
        g = refs[n : 2 * n]
        send, recv = refs[2 * n :]
        _sibling_handshake()
        x, y, c, q, _ = _place()
        sib = (x, y, 1 - c)
        cps = []
        for i in range(n):
            for r in rels:
                landed = _row_half(_shard_view(g[i], kinds[i], q ^ _REL_MASK[r]), c)
                cps.append(_remote(landed, landed, send.at[i, r], recv.at[i, r], sib))
        for cp in cps:
            cp.start()
        for i in range(n):
            for r in rels:
                other = _row_half(_shard_view(g[i], kinds[i], q ^ _REL_MASK[r]), 1 - c)
                _remote(other, other, send.at[i, r], recv.at[i, r], sib).wait_recv()
        for cp in cps:
            cp.wait_send()

    res = _pallas(
        body,
        name=name,
        in_specs=[ANY] * n,
        out_specs=[ANY] * n,
        out_shape=[jax.ShapeDtypeStruct(f.shape, f.dtype) for f in fulls],
        scratch_shapes=[pltpu.SemaphoreType.DMA((n, 3)), pltpu.SemaphoreType.DMA((n, 3))],
        input_output_aliases={i: i for i in range(n)},
        compiler_params=pltpu.CompilerParams(collective_id=SIBLING_BARRIER_ID),
    )(*fulls)
    return list(res)


SIBLING_BARRIER_ID = 1


def _sibling_handshake():
    sib = (lax.axis_index("x"), lax.axis_index("y"), 1 - lax.axis_index("c"))
    barrier = pltpu.get_barrier_semaphore()
    pl.semaphore_signal(barrier, inc=1, device_id=sib, device_id_type=MESH)
    pl.semaphore_wait(barrier, 1)


def _split_start(name, bufs, n_sems, copies, sibling_only=False):
    n = len(bufs)

    def body(*refs):
        if sibling_only:
            _sibling_handshake()
        for cp in copies(refs[:n], refs[n], refs[n + 1]):
            cp.start()

    extra = {"collective_id": SIBLING_BARRIER_ID} if sibling_only else {}

    res = _pallas(
        body,
        name=name,
        out_shape=(
            pltpu.SemaphoreType.DMA((n_sems,)),
            pltpu.SemaphoreType.DMA((n_sems,)),
            *[pltpu.HBM(b.shape, b.dtype) for b in bufs],
        ),
        in_specs=[HBM_SPEC] * n,
        out_specs=(SEM_SPEC, SEM_SPEC, *[HBM_SPEC] * n),
        input_output_aliases={i: i + 2 for i in range(n)},
        compiler_params=pltpu.CompilerParams(has_side_effects=EFFECT, **extra),
    )(*[_hbm(b) for b in bufs])
    return res[0], res[1], list(res[2:])


def _split_wait(name, bufs, send_sem, recv_sem, copies, after):
    n = len(bufs)

    def body(*refs):
        for cp in copies(refs[:n], refs[n], refs[n + 1]):
            cp.wait_send()
            cp.wait_recv()

    res = _pallas(
        body,
        name=name,
        out_shape=[pltpu.HBM(b.shape, b.dtype) for b in bufs],
        in_specs=[HBM_SPEC] * n + [SEM_SPEC, SEM_SPEC, ANY],
        out_specs=[HBM_SPEC] * n,
        input_output_aliases={i: i for i in range(n)},
        compiler_params=pltpu.CompilerParams(has_side_effects=EFFECT),
    )(*bufs, send_sem, recv_sem, after)
    return list(res)


def _pair_exchange_copies(kinds):
    n = len(kinds)

    def copies(refs, send_sem, recv_sem):
        x, y, c, _, _ = _place()
        return [
            _remote(_pair_half(refs[w], kinds[w], 1 - c), refs[n + w], send_sem.at[w], recv_sem.at[w], (x, y, 1 - c))
            for w in range(n)
        ]

    return copies


def _pair_share_copies(kinds, waiting):
    def copies(refs, send_sem, recv_sem):
        x, y, c, _, _ = _place()
        out = []
        for w, kind in enumerate(kinds):
            mine = _pair_half(refs[w], kind, c)
            dst = _pair_half(refs[w], kind, 1 - c) if waiting else mine
            out.append(_remote(mine, dst, send_sem.at[w], recv_sem.at[w], (x, y, 1 - c)))
        return out

    return copies


def _piece_shape(half_shape, kind):
    r, c = half_shape
    return (3, r, c // N_CHIPS) if kind == "col" else (3, r // N_CHIPS, c)


def _chip_send_start(name, halves, kinds):
    n = len(halves)
    lands = [lax.empty(_piece_shape(h.shape, k), BF16) for h, k in zip(halves, kinds)]

    def body(*refs):
        h, land = refs[:n], refs[n : 2 * n]
        send_sem, recv_sem = refs[2 * n], refs[2 * n + 1]
        _, _, c, q, chips = _place()
        for i in range(n):
            for r, chip in enumerate(chips):
                piece = _shard_view(h[i], kinds[i], q ^ _REL_MASK[r])
                _remote(piece, land[i].at[r], send_sem.at[3 * i + r], recv_sem.at[3 * i + r], (*chip, c)).start()

    res = _pallas(
        body,
        name=name,
        out_shape=(
            pltpu.SemaphoreType.DMA((3 * n,)),
            pltpu.SemaphoreType.DMA((3 * n,)),
            *[pltpu.HBM(a.shape, a.dtype) for a in halves],
            *[pltpu.HBM(a.shape, a.dtype) for a in lands],
        ),
        in_specs=[HBM_SPEC] * (2 * n),
        out_specs=(SEM_SPEC, SEM_SPEC, *[HBM_SPEC] * (2 * n)),
        input_output_aliases={i: i + 2 for i in range(2 * n)},
        compiler_params=pltpu.CompilerParams(has_side_effects=EFFECT),
    )(*[_hbm(a) for a in halves], *[_hbm(a) for a in lands])
    return res[0], res[1], list(res[2 : 2 + n]), list(res[2 + n :])


def _chip_send_wait(name, halves, lands, kinds, send_sem, recv_sem, after):
    n = len(halves)

    def body(*refs):
        h, land = refs[:n], refs[n : 2 * n]
        s_sem, r_sem = refs[2 * n], refs[2 * n + 1]
        x, y, c, q, _ = _place()
        for i in range(n):
            for r in range(3):
                piece = _shard_view(h[i], kinds[i], q ^ _REL_MASK[r])
                cp = _remote(piece, land[i].at[r], s_sem.at[3 * i + r], r_sem.at[3 * i + r], (x, y, 1 - c))
                cp.wait_send()
                cp.wait_recv()

    res = _pallas(
        body,
        name=name,
        out_shape=[pltpu.HBM(a.shape, a.dtype) for a in halves] + [pltpu.HBM(a.shape, a.dtype) for a in lands],
        in_specs=[HBM_SPEC] * (2 * n) + [SEM_SPEC, SEM_SPEC, ANY],
        out_specs=[HBM_SPEC] * (2 * n),
        input_output_aliases={i: i for i in range(2 * n)},
        compiler_params=pltpu.CompilerParams(has_side_effects=EFFECT),
    )(*halves, *lands, send_sem, recv_sem, after)
    return list(res[:n]), list(res[n:])


def _small_exchange_copies(waiting):
    def copies(refs, send_sem, recv_sem):
        p, land = refs
        x, y, c, q, _ = _place()
        me = 2 * q + c
        out = []
        for dd in range(1, 2 * N_CHIPS):
            dev = (x ^ ((dd >> 2) & 1), y ^ ((dd >> 1) & 1), c ^ (dd & 1))
            dst = land.at[me ^ dd] if waiting else land.at[me]
            out.append(_remote(p, dst, send_sem.at[dd - 1], recv_sem.at[dd - 1], dev))
        return out

    return copies


def _small_sum(name, me_idx, p, land):
    rows = p.shape[0]
    n_dev = 2 * N_CHIPS

    def body(me_ref, p_ref, land_ref, o_ref):
        me = me_ref[0]
        total = None
        for dev in range(n_dev):
            term = jnp.where(me == dev, p_ref[...], land_ref[dev])
            total = term if total is None else total + term
        o_ref[...] = total

    return _pallas(
        body,
        name=name,
        grid_spec=pltpu.PrefetchScalarGridSpec(
            num_scalar_prefetch=1,
            grid=(1,),
            in_specs=[pl.BlockSpec((rows, LANES), lambda i, m: (0, 0)), pl.BlockSpec((n_dev, rows, LANES), lambda i, m: (0, 0, 0))],
            out_specs=pl.BlockSpec((rows, LANES), lambda i, m: (0, 0)),
        ),
        out_shape=jax.ShapeDtypeStruct(p.shape, F32),
    )(me_idx, p, land)


def _pack(parts):
    rows = []
    for a in parts:
        flat = a.reshape(-1).astype(F32)
        n = flat.shape[0]
        padded = -(-n // (8 * LANES)) * (8 * LANES)
        rows.append(jnp.pad(flat, (0, padded - n)).reshape(-1, LANES))
    return jnp.concatenate(rows, axis=0)


def _unpack(packed, shapes):
    out, row = [], 0
    for shp in shapes:
        n = int(np.prod(shp))
        nrows = -(-n // (8 * LANES)) * 8
        out.append(packed[row : row + nrows].reshape(-1)[:n].reshape(shp))
        row += nrows
    return out


def kernel(x, w_in, norm_mix, sgu_v_gain, sgu_w_s, sgu_b_s, w_a_out, attn_sink, rel_bias, w_b_out, w_o, norm_ffn, w_gate, w_up, w_down, norm_final, loss_target, m_w_in, m_norm_mix, m_sgu_v_gain, m_sgu_w_s, m_sgu_b_s, m_w_a_out, m_attn_sink, m_rel_bias, m_w_b_out, m_w_o, m_norm_ffn, m_w_gate, m_w_up, m_w_down, m_norm_final, v_w_in, v_norm_mix, v_sgu_v_gain, v_sgu_w_s, v_sgu_b_s, v_w_a_out, v_attn_sink, v_rel_bias, v_w_b_out, v_w_o, v_norm_ffn, v_w_gate, v_w_up, v_w_down, v_norm_final):
    s, d = x.shape[1], x.shape[2]
    w_sgu = sgu_v_gain.shape[1]
    groups = sgu_w_s.shape[1]
    heads = attn_sink.shape[1]
    grp = heads // N_KV_HEADS
    w_att = heads * HEAD_DIM
    w_kv = N_KV_HEADS * HEAD_DIM
    d_ff = w_gate.shape[2] * N_CHIPS
    n_in = w_in.shape[2] * N_CHIPS
    off_q = 2 * w_sgu
    off_k = off_q + w_att
    off_g = off_k + 2 * w_kv
    assert n_in == off_g + 2 * d and groups * BLK == w_sgu and s % BLK == 0

    x2d = x.reshape(s, d)
    tgt = loss_target.reshape(s, d)
    c_idx = lax.axis_index("c").astype(I32).reshape(1)
    q_idx = (2 * lax.axis_index("x") + lax.axis_index("y")).astype(I32).reshape(1)
    qc_idx = jnp.concatenate([q_idx, c_idx])

    W_IN, W_A, W_B, W_O, W_GATE, W_UP, W_DOWN = range(7)
    names = ["w_in", "w_a", "w_b", "w_o", "w_gate", "w_up", "w_down"]
    kinds = ["col", "col", "col", "row", "col", "col", "row"]
    big_w = [w_in[0], w_a_out[0], w_b_out[0], w_o[0], w_gate[0], w_up[0], w_down[0]]
    big_m = [m_w_in[0], m_w_a_out[0], m_w_b_out[0], m_w_o[0], m_w_gate[0], m_w_up[0], m_w_down[0]]
    big_v = [v_w_in[0], v_w_a_out[0], v_w_b_out[0], v_w_o[0], v_w_gate[0], v_w_up[0], v_w_down[0]]
    full_in = _cast_into_full("cast_w_in", q_idx, big_w[W_IN], kinds[W_IN])
    in_send, in_recv, (full_in,), token = _gather_start("gather_start_in", [full_in], [kinds[W_IN]], rels=(0, 1))
    rest = [_cast_into_full("cast_" + names[i], q_idx, big_w[i], kinds[i], after=(token,)) for i in range(1, 7)]
    h1 = _rms_fwd("rms_mix", x2d, norm_mix, after=(token,))
    (full_in,) = _gather_wait("gather_wait_in", [full_in], [kinds[W_IN]], [0], in_send, in_recv, h1, rels=(0, 1))
    relay_send, relay_recv, (full_in,) = _split_start(
        "gather_relay_in", [full_in], 4, _relay_and_forward_copies([kinds[W_IN]], False)
    )
    full_down = rest.pop()
    ag_send, ag_recv, rest, token = _gather_start("gather_start_rest", rest, kinds[1:6], rels=(0, 1), after=(full_in,))
    (full_in,) = _split_wait(
        "gather_relay_wait_in", [full_in], relay_send, relay_recv, _relay_and_forward_copies([kinds[W_IN]], True), token
    )
    (g_in,) = _gather_forward("gather_fwd_in", [full_in], [kinds[W_IN]], rels=(2,))
    fulls = [g_in] + rest

    def relay_begin(tag, ids, after, copies=_relay_copies, sems_per_weight=2):
        ks = [kinds[i] for i in ids]
        bufs = _gather_wait("gather_wait_" + tag, [fulls[i] for i in ids], ks, [i - 1 for i in ids], ag_send, ag_recv, after,
                            rels=(0, 1))
        send, recv, bufs = _split_start("gather_relay_" + tag, bufs, sems_per_weight * len(ids), copies(ks, False))
        return tag, ks, send, recv, bufs

    def relay_end(state, after):
        tag, ks, send, recv, bufs = state
        bufs = _split_wait("gather_relay_wait_" + tag, bufs, send, recv, _relay_and_forward_copies(ks, True), after)
        return _gather_forward("gather_fwd_" + tag, bufs, ks, rels=(2,))

    def relay_end_async(state, after):
        tag, ks, send, recv, bufs = state
        bufs = _split_wait("gather_relay_wait_" + tag, bufs, send, recv, _relay_copies(ks, True), after)
        send, recv, bufs = _split_start("gather_fwd_start_" + tag, bufs, 3 * len(ks), _forward_copies(ks, False), sibling_only=True)
        return tag, ks, send, recv, bufs

    def forwarded(state, after):
        tag, ks, send, recv, bufs = state
        return _split_wait("gather_fwd_wait_" + tag, bufs, send, recv, _forward_copies(ks, True), after)

    ws_b = sgu_w_s[0].astype(BF16)
    wst_b = jnp.swapaxes(sgu_w_s[0], 1, 2).astype(BF16)
    b_col = sgu_b_s[0].reshape(groups, BLK, 1)
    bias_tab, bucket = _band_tables(rel_bias)
    sink = attn_sink[0]

    tm = _tile(s, (1024, 512, 256, 128))

    tn = _tile(n_in, (768, 640, 512))
    z = _mm(
        "mm_z", (s // tm, n_in // tn, 1), [h1, g_in],
        [pl.BlockSpec((tm, d), lambda i, j, k: (i, 0)), pl.BlockSpec((d, tn), lambda i, j, k: (0, j))],
        [jax.ShapeDtypeStruct((s, n_in), BF16)], [pl.BlockSpec((tm, tn), lambda i, j, k: (i, j))],
        [(0, 1, NN, 0)], 1, (tm, tn), 1, lambda ins, vals, outs, cs: _put(outs[0], cs, vals[0]),
        _mm_vmem([((tm, d), BF16, 2), ((d, tn), BF16, 2), ((tm, tn), F32, 3)]),
    )[0]
    mix_relay = relay_begin("mix", [W_A, W_B, W_O], z, copies=_relay_and_forward_copies, sems_per_weight=4)

    a_act = _sgu_fwd(z, sgu_v_gain, ws_b, b_col, w_sgu, after=(mix_relay[4][0],))

    kv_b = z[:, off_k:off_g]
    k_pad = jnp.pad(kv_b[:, :w_kv], ((BLK, BLK), (0, 0)))
    v_pad = jnp.pad(kv_b[:, w_kv:], ((BLK, BLK), (0, 0)))
    q_blk0 = off_q // (grp * HEAD_DIM)
    att = _attn_fwd(sink, z, k_pad, v_pad, bias_tab, grp, q_blk0)

    gate_relay = relay_begin("gate", [W_GATE], att)
    up_relay = relay_begin("up", [W_UP], gate_relay[4][0])
    down_send, down_recv, (full_down,), token = _gather_start(
        "gather_start_down", [full_down], [kinds[W_DOWN]], after=(a_act, up_relay[4][0])
    )
    g_a, g_b, g_o = relay_end(mix_relay, token)

    tg = _tile(d, (512,))
    ga0, gb0 = off_g // tg, (off_g + d) // tg

    def ep_gate(ins, vals, outs, cs):
        sa, sb = _sigmoid(ins[4][:, cs].astype(F32)), _sigmoid(ins[5][:, cs].astype(F32))
        _put(outs[0], cs, sa * vals[0] + sb * vals[1])
        _put(outs[1], cs, vals[0])
        _put(outs[2], cs, vals[1])

    t_out = pl.BlockSpec((tm, tg), lambda i, j, k: (i, j))
    m_act, y_a, y_b = _mm(
        "mm_branches", (s // tm, d // tg, 1), [a_act, g_a, att, g_b, z, z],
        [pl.BlockSpec((tm, w_sgu), lambda i, j, k: (i, 0)), pl.BlockSpec((w_sgu, tg), lambda i, j, k: (0, j)),
         pl.BlockSpec((tm, w_att), lambda i, j, k: (i, 0)), pl.BlockSpec((w_att, tg), lambda i, j, k: (0, j)),
         pl.BlockSpec((tm, tg), lambda i, j, k: (i, ga0 + j)), pl.BlockSpec((tm, tg), lambda i, j, k: (i, gb0 + j))],
        [jax.ShapeDtypeStruct((s, d), BF16)] * 3,
        [t_out, t_out, t_out], [(0, 1, NN, 0), (2, 3, NN, 1)], 2, (tm, tg), 1, ep_gate,
        _mm_vmem([((tm, w_sgu), BF16, 4), ((w_sgu, tg), BF16, 4), ((tm, tg), F32, 12)]),    )

    tn = _tile(d, (1024, 512))

    def ep_residual(ins, vals, outs, cs):
        _put(outs[0], cs, ins[2][:, cs] + vals[0])

    gate_fwd = relay_end_async(gate_relay, m_act)
    x2 = _mm(
        "mm_wo", (s // tm, d // tn, 1), [m_act, g_o, x2d],
        [pl.BlockSpec((tm, d), lambda i, j, k: (i, 0)), pl.BlockSpec((d, tn), lambda i, j, k: (0, j)),
         pl.BlockSpec((tm, tn), lambda i, j, k: (i, j))],
        [jax.ShapeDtypeStruct((s, d), F32)], [pl.BlockSpec((tm, tn), lambda i, j, k: (i, j))],
        [(0, 1, NN, 0)], 1, (tm, tn), 1, ep_residual,
        _mm_vmem([((tm, d), BF16, 2), ((d, tn), BF16, 2), ((tm, tn), F32, 5)]),
        after=(gate_fwd[4][0],),
    )[0]
    (g_gate,) = forwarded(gate_fwd, x2)
    up_fwd = relay_end_async(up_relay, g_gate)
    h2 = _rms_fwd("rms_ffn", x2, norm_ffn, after=(up_fwd[4][0],))
    (g_up,) = forwarded(up_fwd, h2)

    tf = _tile(d_ff, (512,))

    def ep_swiglu(ins, vals, outs, cs):
        gt, up = vals
        _put(outs[0], cs, gt)
        _put(outs[1], cs, up)
        _put(outs[2], cs, (gt * _sigmoid(gt)) * up)

    f_out = pl.BlockSpec((tm, tf), lambda i, j, k: (i, j))
    gt, up, f_act = _mm(
        "mm_gate_up", (s // tm, d_ff // tf, 1), [h2, g_gate, g_up],
        [pl.BlockSpec((tm, d), lambda i, j, k: (i, 0)), pl.BlockSpec((d, tf), lambda i, j, k: (0, j)),
         pl.BlockSpec((d, tf), lambda i, j, k: (0, j))],
        [jax.ShapeDtypeStruct((s, d_ff), BF16)] * 3,
        [f_out, f_out, f_out], [(0, 1, NN, 0), (0, 2, NN, 1)], 2, (tm, tf), 1, ep_swiglu,
        _mm_vmem([((tm, d), BF16, 2), ((d, tf), BF16, 4), ((tm, tf), F32, 8)]),    )
    (g_down,) = _gather_forward(
        "gather_fwd_ffn_out",
        _gather_wait("gather_wait_ffn_out", [full_down], [kinds[W_DOWN]], [0], down_send, down_recv, f_act),
        [kinds[W_DOWN]],
    )

    tkf = _tile(d_ff, (1408, 1024, 512))
    tml, tnl = _tile(s, (512, 256, 128)), _tile(d, (512,))
    x3 = _mm(
        "mm_down", (s // tml, d // tnl, 1), [f_act, g_down, x2],
        [pl.BlockSpec((tml, d_ff), lambda i, j, k: (i, 0)), pl.BlockSpec((d_ff, tnl), lambda i, j, k: (0, j)),
         pl.BlockSpec((tml, tnl), lambda i, j, k: (i, j))],
        [jax.ShapeDtypeStruct((s, d), F32)], [pl.BlockSpec((tml, tnl), lambda i, j, k: (i, j))],
        [(0, 1, NN, 0)], 1, (tml, tnl), 1, ep_residual,
        _mm_vmem([((tml, d_ff), BF16, 2), ((d_ff, tnl), BF16, 2), ((tml, tnl), F32, 6)]),    )[0]

    dx3, dx3b, dg_final, loss_part = _head(x3, norm_final.reshape(1, d), tgt)

    def reduce_a(tag, ids, grads):
        ks = [kinds[i] for i in ids]
        lands = [lax.empty((g.shape[0] // 2, g.shape[1]) if k == "col" else (g.shape[0], g.shape[1] // 2), BF16)
                 for g, k in zip(grads, ks)]
        send, recv, bufs = _split_start("pair_send_" + tag, list(grads) + lands, len(ids), _pair_exchange_copies(ks), sibling_only=True)
        return {"tag": tag, "ids": ids, "ks": ks, "pair": (send, recv, bufs), "token": bufs[0]}

    def reduce_b(st, after):
        tag, ids, ks = st["tag"], st["ids"], st["ks"]
        send, recv, bufs = st["pair"]
        bufs = _split_wait("pair_wait_" + tag, bufs, send, recv, _pair_exchange_copies(ks), after)
        grads, from_sib = bufs[: len(ids)], bufs[len(ids) :]
        halves = [_pair_add("pair_add_" + names[i], c_idx, g, r, k) for i, g, r, k in zip(ids, grads, from_sib, ks)]
        st["chip"] = _chip_send_start("chip_send_" + tag, halves, ks)
        st["token"] = st["chip"][2][0]

    def reduce_c(st, after):
        tag, ids, ks = st["tag"], st["ids"], st["ks"]
        send, recv, halves, lands = st["chip"]
        halves, lands = _chip_send_wait("chip_wait_" + tag, halves, lands, ks, send, recv, after)
        pieces = [_chip_sum("chip_sum_" + names[i], qc_idx, h, r, k) for i, h, r, k in zip(ids, halves, lands, ks)]
        st["share"] = _split_start("share_send_" + tag, pieces, len(ids), _pair_share_copies(ks, False), sibling_only=True)
        st["token"] = st["share"][2][0]

    def reduce_d(st, after):
        send, recv, bufs = st["share"]
        return _split_wait("share_wait_" + st["tag"], bufs, send, recv, _pair_share_copies(st["ks"], True), after)

    grads_big, upd = [None] * 7, [None] * 7

    def finish(st, after):
        shared = reduce_d(st, after)
        after = shared[0]
        for i, g in zip(st["ids"], shared):
            upd[i] = _adamw("adamw_" + names[i], big_w[i], g, big_m[i], big_v[i], after=(after,))
            grads_big[i] = upd[i][3]
            after = upd[i][0]
        return after

    def ep_swiglu_bwd(ins, vals, outs, cs):
        df = vals[0]
        gtv, upv = ins[2][:, cs].astype(F32), ins[3][:, cs].astype(F32)
        sg = _sigmoid(gtv)
        _put(outs[0], cs, df * upv * (sg + gtv * sg * (1.0 - sg)))
        _put(outs[1], cs, df * (gtv * sg))

    dgt, dup = _mm(
        "mm_dswiglu", (s // tm, d_ff // tf, 1), [dx3b, g_down, gt, up],
        [pl.BlockSpec((tm, d), lambda i, j, k: (i, 0)), pl.BlockSpec((tf, d), lambda i, j, k: (j, 0)), f_out, f_out],
        [jax.ShapeDtypeStruct((s, d_ff), BF16), jax.ShapeDtypeStruct((s, d_ff), BF16)], [f_out, f_out],
        [(0, 1, NT, 0)], 1, (tm, tf), 1, ep_swiglu_bwd,
        _mm_vmem([((tm, d), BF16, 2), ((tf, d), BF16, 2), ((tm, tf), F32, 8)]),    )

    def ep_store(ins, vals, outs, cs):
        for o, v in zip(outs, vals):
            _put(o, cs, v)

    twn = _tile(d, (1024, 512))
    gw_down = _mm(
        "mm_gw_down", (d_ff // tkf, d // twn, 1), [f_act, dx3b],
        [pl.BlockSpec((s, tkf), lambda i, j, k: (0, i)), pl.BlockSpec((s, twn), lambda i, j, k: (0, j))],
        [jax.ShapeDtypeStruct((d_ff, d), BF16)], [pl.BlockSpec((tkf, twn), lambda i, j, k: (i, j))],
        [(0, 1, TN, 0)], 1, (tkf, twn), 1, ep_store,
        _mm_vmem([((s, tkf), BF16, 3), ((s, twn), BF16, 2), ((tkf, twn), F32, 3)]),
    )[0]
    red_down = reduce_a("down", [W_DOWN], [gw_down])

    tn2 = _tile(d, (256,))
    dh2_specs = [pl.BlockSpec((tm, d_ff), lambda i, j, k: (i, 0)), pl.BlockSpec((tn2, d_ff), lambda i, j, k: (j, 0))]
    dh2_tile = pl.BlockSpec((tm, tn2), lambda i, j, k: (i, j))
    dh2_vmem = _mm_vmem([((tm, d_ff), BF16, 2), ((tn2, d_ff), BF16, 2), ((tm, tn2), F32, 7)])
    dh2 = _mm(
        "mm_dh2_gate", (s // tm, d // tn2, 1), [dgt, g_gate], dh2_specs,
        [jax.ShapeDtypeStruct((s, d), F32)], [dh2_tile], [(0, 1, NT, 0)], 1, (tm, tn2), 1, ep_store, dh2_vmem,
        after=(red_down["token"],),
    )[0]
    dh2 = _mm(
        "mm_dh2_up", (s // tm, d // tn2, 1), [dup, g_up, dh2], dh2_specs + [dh2_tile],
        [jax.ShapeDtypeStruct((s, d), F32)], [dh2_tile], [(0, 1, NT, 0)], 1, (tm, tn2), 1, ep_residual, dh2_vmem,
    )[0]
    reduce_b(red_down, dh2)

    twr = _tile(d, (1024, 512))
    w_tile = pl.BlockSpec((twr, tf), lambda i, j, k: (i, j))
    gw_gate, gw_up = _mm(
        "mm_gw_gate_up", (d // twr, d_ff // tf, 1), [h2, dgt, dup],
        [pl.BlockSpec((s, twr), lambda i, j, k: (0, i)), pl.BlockSpec((s, tf), lambda i, j, k: (0, j)),
         pl.BlockSpec((s, tf), lambda i, j, k: (0, j))],
        [jax.ShapeDtypeStruct((d, d_ff), BF16), jax.ShapeDtypeStruct((d, d_ff), BF16)], [w_tile, w_tile],
        [(0, 1, TN, 0), (0, 2, TN, 1)], 2, (twr, tf), 1, ep_store,
        _mm_vmem([((s, twr), BF16, 3), ((s, tf), BF16, 4), ((twr, tf), F32, 6)]),
        after=(red_down["token"],),
    )
    red_ffn = reduce_a("ffn_in", [W_GATE, W_UP], [gw_gate, gw_up])

    dx2, dx2b, dg_ffn = _rms_bwd("rms_ffn_bwd", x2, norm_ffn, dh2, dx3, after=(red_ffn["token"],))

    nj = d // tg

    def lo(j):
        return jnp.minimum(j, nj - 1)

    def gate_bwd_body(dx_ref, wo_ref, ga_ref, gb_ref, ya_ref, yb_ref, dya_ref, dyb_ref, dz_ref, keep):
        j = pl.program_id(1)

        @pl.when(j < nj)
        def _():
            dm = lax.dot_general(dx_ref[...], wo_ref[...], NT, preferred_element_type=F32)
            sa, sb = _sigmoid(ga_ref[...].astype(F32)), _sigmoid(gb_ref[...].astype(F32))
            dya_ref[...] = (dm * sa).astype(BF16)
            dyb_ref[...] = (dm * sb).astype(BF16)
            dz_ref[...] = (dm * ya_ref[...].astype(F32) * (sa * (1.0 - sa))).astype(BF16)
            keep[lo(j)] = (dm * yb_ref[...].astype(F32) * (sb * (1.0 - sb))).astype(BF16)

        @pl.when(j >= nj)
        def _():
            dz_ref[...] = keep[jnp.maximum(j - nj, 0)]

    t_lo = pl.BlockSpec((tm, tg), lambda i, j: (i, lo(j)))
    dya, dyb, dz = _pallas(
        gate_bwd_body,
        name="mm_dgate",
        grid=(s // tm, 2 * nj),
        in_specs=[
            pl.BlockSpec((tm, d), lambda i, j: (i, 0)),
            pl.BlockSpec((tg, d), lambda i, j: (lo(j), 0)),
            pl.BlockSpec((tm, tg), lambda i, j: (i, ga0 + lo(j))),
            pl.BlockSpec((tm, tg), lambda i, j: (i, gb0 + lo(j))),
            t_lo,
            t_lo,
        ],
        out_specs=[t_lo, t_lo, pl.BlockSpec((tm, tg), lambda i, j: (i, ga0 + j))],
        out_shape=[jax.ShapeDtypeStruct((s, d), BF16), jax.ShapeDtypeStruct((s, d), BF16), jax.ShapeDtypeStruct((s, n_in), BF16)],
        scratch_shapes=[pltpu.VMEM((nj, tm, tg), BF16)],
        compiler_params=_params(_mm_vmem([((tm, d), BF16, 2), ((tg, d), BF16, 2), ((tm, tg), F32, 14), ((nj, tm, tg), BF16, 1)])),
    )(dx2b, g_o, z, z, y_a, y_b)
    reduce_b(red_ffn, dya)
    reduce_c(red_down, red_ffn["token"])

    gw_o = _mm(
        "mm_gw_o", (d // twr, d // twn, 1), [m_act, dx2b],
        [pl.BlockSpec((s, twr), lambda i, j, k: (0, i)), pl.BlockSpec((s, twn), lambda i, j, k: (0, j))],
        [jax.ShapeDtypeStruct((d, d), BF16)], [pl.BlockSpec((twr, twn), lambda i, j, k: (i, j))],
        [(0, 1, TN, 0)], 1, (twr, twn), 1, ep_store,
        _mm_vmem([((s, twr), BF16, 3), ((s, twn), BF16, 2), ((twr, twn), F32, 3)]),
        after=(red_down["token"],),
    )[0]
    after_down = finish(red_down, gw_o)

    tb = _tile(w_sgu, (1024, 512))
    b_out = pl.BlockSpec((tm, tb), lambda i, j, k: (i, j))

    da, datt = _mm(
        "mm_dbranches", (s // tm, w_sgu // tb, 1), [dya, g_a, dyb, g_b],
        [pl.BlockSpec((tm, d), lambda i, j, k: (i, 0)), pl.BlockSpec((tb, d), lambda i, j, k: (j, 0)),
         pl.BlockSpec((tm, d), lambda i, j, k: (i, 0)), pl.BlockSpec((tb, d), lambda i, j, k: (j, 0))],
        [jax.ShapeDtypeStruct((s, w_sgu), BF16), jax.ShapeDtypeStruct((s, w_att), BF16)], [b_out, b_out],
        [(0, 1, NT, 0), (2, 3, NT, 1)], 2, (tm, tb), 1, ep_store,
        _mm_vmem([((tm, d), BF16, 4), ((tb, d), BF16, 4), ((tm, tb), F32, 6)]),
        after=(after_down,),
    )

    wb_tile = pl.BlockSpec((tb, twn), lambda i, j, k: (i, j))
    gw_a, gw_b = _mm(
        "mm_gw_branches", (w_sgu // tb, d // twn, 1), [a_act, dya, att, dyb],
        [pl.BlockSpec((s, tb), lambda i, j, k: (0, i)), pl.BlockSpec((s, twn), lambda i, j, k: (0, j)),
         pl.BlockSpec((s, tb), lambda i, j, k: (0, i)), pl.BlockSpec((s, twn), lambda i, j, k: (0, j))],
        [jax.ShapeDtypeStruct((w_sgu, d), BF16), jax.ShapeDtypeStruct((w_att, d), BF16)], [wb_tile, wb_tile],
        [(0, 1, TN, 0), (2, 3, TN, 1)], 2, (tb, twn), 1, ep_store,
        _mm_vmem([((s, tb), BF16, 5), ((s, twn), BF16, 4), ((tb, twn), F32, 6)]),
        after=(da,),
    )
    red_mix = reduce_a("mix", [W_O, W_A, W_B], [gw_o, gw_a, gw_b])

    dz, dws, dbs, dgain = _sgu_bwd(z, da, sgu_v_gain, ws_b, wst_b, b_col, w_sgu, dz, after=(red_mix["token"],))
    dz, dk_pad, dv_pad, dbias_tab, dsink = _attn_bwd(sink, z, k_pad, v_pad, bias_tab, datt, dz, grp, q_blk0)
    dz = _dkv_to_dz(dk_pad, dv_pad, dz, off_k // (2 * w_kv))
    drel = _relbias_bwd(dbias_tab, bucket)
    reduce_b(red_mix, dz)
    reduce_c(red_ffn, red_mix["token"])

    small_w = [norm_mix, sgu_v_gain, sgu_w_s, sgu_b_s, attn_sink, rel_bias, norm_ffn, norm_final]
    small_m = [m_norm_mix, m_sgu_v_gain, m_sgu_w_s, m_sgu_b_s, m_attn_sink, m_rel_bias, m_norm_ffn, m_norm_final]
    small_v = [v_norm_mix, v_sgu_v_gain, v_sgu_w_s, v_sgu_b_s, v_attn_sink, v_rel_bias, v_norm_ffn, v_norm_final]
    small_shapes = [w.shape for w in small_w]
    early = [dgain, dws, dbs, dsink[:, 0], drel[:, :REL_BUCKETS].T, dg_ffn, dg_final]
    p_early = _pack([g.reshape(shp) for g, shp in zip(early, small_shapes[1:])] + [loss_part[0, :1]])
    land = jnp.zeros((2 * N_CHIPS,) + p_early.shape, F32)
    sm_send, sm_recv, (p_early, land) = _split_start("small_send", [p_early, land], 2 * N_CHIPS - 1, _small_exchange_copies(False))

    tzn = _tile(n_in, (768, 640, 512))
    gw_in = _mm(
        "mm_gw_in", (d // twr, n_in // tzn, 1), [h1, dz],
        [pl.BlockSpec((s, twr), lambda i, j, k: (0, i)), pl.BlockSpec((s, tzn), lambda i, j, k: (0, j))],
        [jax.ShapeDtypeStruct((d, n_in), BF16)], [pl.BlockSpec((twr, tzn), lambda i, j, k: (i, j))],
        [(0, 1, TN, 0)], 1, (twr, tzn), 1, ep_store,
        _mm_vmem([((s, twr), BF16, 3), ((s, tzn), BF16, 2), ((twr, tzn), F32, 3)]),
        after=(red_ffn["token"], p_early),
    )[0]
    red_in = reduce_a("w_in", [W_IN], [gw_in])

    reduce_c(red_mix, red_in["token"])
    reduce_b(red_in, finish(red_mix, red_in["token"]))

    dh1 = _mm(
        "mm_dh1", (s // tm, d // tn2, 1), [dz, g_in],
        [pl.BlockSpec((tm, n_in), lambda i, j, k: (i, 0)), pl.BlockSpec((tn2, n_in), lambda i, j, k: (j, 0))],
        [jax.ShapeDtypeStruct((s, d), F32)], [pl.BlockSpec((tm, tn2), lambda i, j, k: (i, j))],
        [(0, 1, NT, 0)], 1, (tm, tn2), 1, ep_store,
        _mm_vmem([((tm, n_in), BF16, 2), ((tn2, n_in), BF16, 2), ((tm, tn2), F32, 5)]),
        after=(red_in["token"],),
    )[0]

    grad_x, _, dg_mix = _rms_bwd("rms_mix_bwd", x2d, norm_mix, dh1, dx2)

    p_mix = _pack([dg_mix.reshape(small_shapes[0])])
    land_mix = jnp.zeros((2 * N_CHIPS,) + p_mix.shape, F32)
    mx_send, mx_recv, (p_mix, land_mix) = _split_start("mix_send", [p_mix, land_mix], 2 * N_CHIPS - 1, _small_exchange_copies(False))

    reduce_c(red_in, finish(red_ffn, p_mix))
    p_early, land = _split_wait("small_wait", [p_early, land], sm_send, sm_recv, _small_exchange_copies(True), red_in["token"])
    p_mix, land_mix = _split_wait("mix_wait", [p_mix, land_mix], mx_send, mx_recv, _small_exchange_copies(True), p_early)
    me_idx = 2 * q_idx + c_idx
    packed_g = jnp.concatenate([_small_sum("mix_sum", me_idx, p_mix, land_mix), _small_sum("small_sum", me_idx, p_early, land)], axis=0)
    g_small = _unpack(packed_g, small_shapes + [(1,)])
    loss = g_small[-1].reshape(())
    g_small = g_small[:-1]
    zero1 = jnp.zeros((1,), F32)
    pw, pg, pm, pv = _pack(small_w + [zero1]), _pack(g_small + [zero1]), _pack(small_m + [zero1]), _pack(small_v + [zero1])
    small_upd = _adamw("adamw_small", pw, pg, pm, pv)
    d_small, nm_small, nv_small = [_unpack(a, small_shapes) for a in small_upd[:3]]
    finish(red_in, small_upd[0])

    small_names = ["norm_mix", "sgu_v_gain", "sgu_w_s", "sgu_b_s", "attn_sink", "rel_bias", "norm_ffn", "norm_final"]
    table = {}
    for i, n in enumerate(names):
        table[n] = (grads_big[i][None], upd[i][0][None], upd[i][1][None], upd[i][2][None])
    for i, n in enumerate(small_names):
        table[n] = (g_small[i], d_small[i], nm_small[i], nv_small[i])
    order = ["w_in", "norm_mix", "sgu_v_gain", "sgu_w_s", "sgu_b_s", "w_a", "attn_sink", "rel_bias", "w_b", "w_o", "norm_ffn",
             "w_gate", "w_up", "w_down", "norm_final"]
    outs = [loss, grad_x.reshape(1, s, d)]
    for part in range(4):
        outs += [table[n][part] for n in order]
    return tuple(outs)
```

```python
import math

import jax
import jax.numpy as jnp
import numpy as np
from jax import lax
from jax.experimental import pallas as pl
from jax.experimental.pallas import tpu as pltpu

F32 = jnp.float32
BF16 = jnp.bfloat16
I32 = jnp.int32
MESH = pl.DeviceIdType.MESH

EPS = 1e-6
NEG = -1e30
BLK = 128
HEAD_DIM = 128
N_KV_HEADS = 2
REL_BUCKETS = 32
REL_MAX_DIST = 128
N_CHIPS = 4
ADAM_LR, ADAM_B1, ADAM_B2, ADAM_EPS, ADAM_WD, ADAM_STEP = 0.001, 0.9, 0.999, 1e-08, 0.01, 10

LANES = 128
VMEM_CAP = 60 * 1024 * 1024

NN = (((1,), (0,)), ((), ()))
NT = (((1,), (1,)), ((), ()))
TN = (((0,), (0,)), ((), ()))
ANY = pl.BlockSpec(memory_space=pl.ANY)
HBM_SPEC = pl.BlockSpec(memory_space=pltpu.HBM)
SEM_SPEC = pl.BlockSpec(memory_space=pltpu.SEMAPHORE)
EFFECT = pltpu.SideEffectType.DATAFLOW_SIDE_EFFECTING


def _tile(n, cands):
    for t in cands:
        if n % t == 0:
            return t
    return n


PIN_BYTES = 64 * 1024


def _pin_hbm(a):
    big = hasattr(a, "dtype") and jnp.issubdtype(a.dtype, jnp.floating) and _nbytes(a.shape, a.dtype) >= PIN_BYTES
    return pltpu.with_memory_space_constraint(a, pltpu.HBM) if big else a


def _pallas(body, *, out_shape, **kw):
    def pin(o):
        big = isinstance(o, jax.ShapeDtypeStruct) and jnp.issubdtype(o.dtype, jnp.floating) and _nbytes(o.shape, o.dtype) >= PIN_BYTES
        return pltpu.HBM(o.shape, o.dtype) if big else o

    shapes = type(out_shape)(pin(o) for o in out_shape) if isinstance(out_shape, (list, tuple)) else pin(out_shape)
    call = pl.pallas_call(body, out_shape=shapes, **kw)
    return lambda *args: call(*[_pin_hbm(a) for a in args])


def _params(vmem_bytes=None, **kw):
    if vmem_bytes is not None:
        kw["vmem_limit_bytes"] = int(min(max(vmem_bytes, 32 * 1024 * 1024), VMEM_CAP))
    return pltpu.CompilerParams(**kw)


def _nbytes(shape, dtype):
    return int(np.prod(shape)) * jnp.dtype(dtype).itemsize


def _sigmoid(x):
    return 1.0 / (1.0 + jnp.exp(-x))


_GC = 0.7978845608028654
_GA = 0.044715


def _gelu(x):
    return 0.5 * x * (1.0 + jnp.tanh(_GC * (x + _GA * (x * x * x))))


def _gelu_grad(x):
    t = jnp.tanh(_GC * (x + _GA * (x * x * x)))
    return 0.5 * (1.0 + t) + 0.5 * x * (1.0 - t * t) * (_GC * (1.0 + 3.0 * _GA * (x * x)))


def _bf(v):
    return v if v.dtype == BF16 else v.astype(BF16)


def _mm(name, grid, ins, in_specs, out_shape, out_specs, pairs, n_acc, tile, nk, epilogue, vmem_bytes, after=()):
    assert nk == 1
    n_in, n_out = len(ins) + len(after), len(out_shape)

    def body(*refs):
        in_refs, out_refs = refs[:n_in], refs[n_in : n_in + n_out]
        vals = [None] * n_acc
        for a_i, b_i, dn, acc_i in pairs:
            d = lax.dot_general(_bf(in_refs[a_i][...]), _bf(in_refs[b_i][...]), dn, preferred_element_type=F32)
            vals[acc_i] = d if vals[acc_i] is None else vals[acc_i] + d
        epilogue(in_refs, vals, out_refs, slice(None))

    return _pallas(
        body,
        name=name,
        grid=grid,
        in_specs=list(in_specs) + [ANY] * len(after),
        out_specs=out_specs,
        out_shape=out_shape,
        compiler_params=_params(vmem_bytes),
    )(*ins, *after)


def _put(ref, cs, v):
    ref[:, cs] = v.astype(ref.dtype)


def _mm_vmem(tiles):
    return sum(_nbytes(s, d) * c for s, d, c in tiles) + 4 * 1024 * 1024


def _rows8(v):
    r, d = v.shape
    return v.reshape(r // 8, 8, d).sum(axis=0)


def _rms_fwd(name, x, g, after=()):
    s, d = x.shape
    tm = _tile(s, (256, 128))

    def body(x_ref, g_ref, *rest):
        h_ref = rest[-1]
        xv = x_ref[...]
        r = lax.rsqrt(jnp.mean(xv * xv, axis=-1, keepdims=True) + EPS)
        h_ref[...] = ((xv * r) * g_ref[...]).astype(BF16)

    return _pallas(
        body,
        name=name,
        grid=(s // tm,),
        in_specs=[pl.BlockSpec((tm, d), lambda i: (i, 0)), pl.BlockSpec((1, d), lambda i: (0, 0))] + [ANY] * len(after),
        out_specs=pl.BlockSpec((tm, d), lambda i: (i, 0)),
        out_shape=jax.ShapeDtypeStruct((s, d), BF16),
    )(x, g, *after)


def _rms_bwd(name, x, g, dh, dres, after=()):
    s, d = x.shape
    tm = _tile(s, (256, 128))
    n = s // tm
    n_after = len(after)

    def body(x_ref, g_ref, dh_ref, dres_ref, *rest):
        dx_ref, dxb_ref, dg_ref, acc_ref = rest[n_after:]
        i = pl.program_id(0)
        xv = x_ref[...]
        r = lax.rsqrt(jnp.mean(xv * xv, axis=-1, keepdims=True) + EPS)
        xh = xv * r
        dhv = dh_ref[...]
        dxh = dhv * g_ref[...]
        dx = r * (dxh - xh * jnp.mean(dxh * xh, axis=-1, keepdims=True)) + dres_ref[...]
        dx_ref[...] = dx
        dxb_ref[...] = dx.astype(BF16)
        part = _rows8(dhv * xh)

        @pl.when(i == 0)
        def _():
            acc_ref[...] = part

        @pl.when(i > 0)
        def _():
            acc_ref[...] += part

        @pl.when(i == n - 1)
        def _():
            dg_ref[...] = jnp.sum(acc_ref[...], axis=0, keepdims=True)

    row = pl.BlockSpec((tm, d), lambda i: (i, 0))
    vec = pl.BlockSpec((1, d), lambda i: (0, 0))
    return _pallas(
        body,
        name=name,
        grid=(n,),
        in_specs=[row, vec, row, row] + [ANY] * n_after,
        out_specs=[row, row, vec],
        out_shape=[jax.ShapeDtypeStruct((s, d), F32), jax.ShapeDtypeStruct((s, d), BF16), jax.ShapeDtypeStruct((1, d), F32)],
        scratch_shapes=[pltpu.VMEM((8, d), F32)],
    )(x, g, dh, dres, *after)


def _head(x3, g, target):
    s, d = x3.shape
    tm = _tile(s, (256, 128))
    n = s // tm

    def body(x_ref, g_ref, t_ref, dx_ref, dxb_ref, dg_ref, loss_ref, acc_g, acc_l):
        i = pl.program_id(0)
        xv = x_ref[...]
        gv = g_ref[...]
        r = lax.rsqrt(jnp.mean(xv * xv, axis=-1, keepdims=True) + EPS)
        xh = xv * r
        e = xh * gv - t_ref[...]
        dy = e * (1.0 / d)
        dxh = dy * gv
        dx = r * (dxh - xh * jnp.mean(dxh * xh, axis=-1, keepdims=True))
        dx_ref[...] = dx
        dxb_ref[...] = dx.astype(BF16)
        pg = _rows8(dy * xh)
        plo = _rows8(e * e)

        @pl.when(i == 0)
        def _():
            acc_g[...] = pg
            acc_l[...] = plo

        @pl.when(i > 0)
        def _():
            acc_g[...] += pg
            acc_l[...] += plo

        @pl.when(i == n - 1)
        def _():
            dg_ref[...] = jnp.sum(acc_g[...], axis=0, keepdims=True)
            loss_ref[...] = jnp.full((1, LANES), (0.5 / d) * jnp.sum(acc_l[...]), F32)

    row = pl.BlockSpec((tm, d), lambda i: (i, 0))
    vec = pl.BlockSpec((1, d), lambda i: (0, 0))
    return _pallas(
        body,
        name="head",
        grid=(n,),
        in_specs=[row, vec, row],
        out_specs=[row, row, vec, pl.BlockSpec((1, LANES), lambda i: (0, 0))],
        out_shape=[
            jax.ShapeDtypeStruct((s, d), F32),
            jax.ShapeDtypeStruct((s, d), BF16),
            jax.ShapeDtypeStruct((1, d), F32),
            jax.ShapeDtypeStruct((1, LANES), F32),
        ],
        scratch_shapes=[pltpu.VMEM((8, d), F32), pltpu.VMEM((8, d), F32)],
    )(x3, g, target)


def _sgu_fwd(z, gain, ws_b, b_col, w_sgu, after=()):
    s = z.shape[0]
    groups = ws_b.shape[0]

    def body(zu_ref, zv_ref, gain_ref, ws_ref, b_ref, *rest):
        a_ref = rest[-1]
        vv = _gelu(zv_ref[...].astype(F32))
        r = lax.rsqrt(jnp.mean(vv * vv, axis=-1, keepdims=True) + EPS)
        vn = ((vv * r) * gain_ref[...]).astype(BF16)
        u = _gelu(zu_ref[...].astype(F32))
        for g in range(groups):
            sl = slice(g * BLK, (g + 1) * BLK)
            mixed = jnp.dot(ws_ref[g], vn[:, sl], preferred_element_type=F32) + b_ref[g]
            a_ref[:, sl] = (u[:, sl] * mixed).astype(BF16)

    return _pallas(
        body,
        name="sgu_fwd",
        grid=(s // BLK,),
        in_specs=[
            pl.BlockSpec((BLK, w_sgu), lambda c: (c, 0)),
            pl.BlockSpec((BLK, w_sgu), lambda c: (c, 1)),
            pl.BlockSpec((1, w_sgu), lambda c: (0, 0)),
            pl.BlockSpec((groups, BLK, BLK), lambda c: (0, 0, 0)),
            pl.BlockSpec((groups, BLK, 1), lambda c: (0, 0, 0)),
        ]
        + [ANY] * len(after),
        out_specs=pl.BlockSpec((BLK, w_sgu), lambda c: (c, 0)),
        out_shape=jax.ShapeDtypeStruct((s, w_sgu), BF16),
    )(z, z, gain, ws_b, b_col, *after)


def _sgu_bwd(z, da, gain, ws_b, wst_b, b_col, w_sgu, dz, after=()):
    s = z.shape[0]
    groups = ws_b.shape[0]
    n = s // BLK
    n_skip = 1 + len(after)

    def body(zu_ref, zv_ref, da_ref, gain_ref, ws_ref, wst_ref, b_ref, *rest):
        dz_ref, dws_ref, dbs_ref, dgain_ref, acc_gain, vv_s, gv_s, dxh_s = rest[n_skip:]
        c = pl.program_id(0)
        cols = [slice(g * BLK, (g + 1) * BLK) for g in range(groups)]

        ss = jnp.zeros((BLK, 1), F32)
        for sl in cols:
            zv = zv_ref[:, sl].astype(F32)
            vv = _gelu(zv)
            vv_s[:, sl] = vv
            gv_s[:, sl] = _gelu_grad(zv)
            ss = ss + jnp.sum(vv * vv, axis=-1, keepdims=True)
        r = lax.rsqrt(ss * (1.0 / w_sgu) + EPS)

        dot_dx = jnp.zeros((BLK, 1), F32)
        for g, sl in enumerate(cols):
            gain_g = gain_ref[:, sl]
            xh = vv_s[:, sl] * r
            vn = (xh * gain_g).astype(BF16)
            zu = zu_ref[:, sl].astype(F32)
            dav = da_ref[:, sl].astype(F32)
            dmix = dav * _gelu(zu)
            dmix_b = dmix.astype(BF16)
            mixed = jnp.dot(ws_ref[g], vn, preferred_element_type=F32) + b_ref[g]
            dz_ref[:, sl] = (dav * mixed * _gelu_grad(zu)).astype(BF16)
            dvn = jnp.dot(wst_ref[g], dmix_b, preferred_element_type=F32)
            dws_g = lax.dot_general(dmix_b, vn, NT, preferred_element_type=F32)
            dbs_g = jnp.sum(dmix, axis=1, keepdims=True)
            pg = _rows8(dvn * xh)

            @pl.when(c == 0)
            def _():
                dws_ref[g] = dws_g
                dbs_ref[g] = dbs_g
                acc_gain[:, sl] = pg

            @pl.when(c > 0)
            def _():
                dws_ref[g] += dws_g
                dbs_ref[g] += dbs_g
                acc_gain[:, sl] += pg

            dxh = dvn * gain_g
            dxh_s[:, sl] = dxh
            dot_dx = dot_dx + jnp.sum(dxh * xh, axis=-1, keepdims=True)

        mean_dx = dot_dx * (1.0 / w_sgu)
        for g, sl in enumerate(cols):
            dvv = r * (dxh_s[:, sl] - (vv_s[:, sl] * r) * mean_dx)
            dz_ref[:, w_sgu + g * BLK : w_sgu + (g + 1) * BLK] = (dvv * gv_s[:, sl]).astype(BF16)

        @pl.when(c == n - 1)
        def _():
            dgain_ref[...] = jnp.sum(acc_gain[...], axis=0, keepdims=True)

    full3 = pl.BlockSpec((groups, BLK, BLK), lambda c: (0, 0, 0))
    col3 = pl.BlockSpec((groups, BLK, 1), lambda c: (0, 0, 0))
    vec = pl.BlockSpec((1, w_sgu), lambda c: (0, 0))
    return _pallas(
        body,
        name="sgu_bwd",
        grid=(n,),
        in_specs=[
            pl.BlockSpec((BLK, w_sgu), lambda c: (c, 0)),
            pl.BlockSpec((BLK, w_sgu), lambda c: (c, 1)),
            pl.BlockSpec((BLK, w_sgu), lambda c: (c, 0)),
            vec,
            full3,
            full3,
            col3,
            ANY,
        ]
        + [ANY] * len(after),
        out_specs=[pl.BlockSpec((BLK, 2 * w_sgu), lambda c: (c, 0)), full3, col3, vec],
        out_shape=[
            jax.ShapeDtypeStruct(dz.shape, BF16),
            jax.ShapeDtypeStruct((groups, BLK, BLK), F32),
            jax.ShapeDtypeStruct((groups, BLK, 1), F32),
            jax.ShapeDtypeStruct((1, w_sgu), F32),
        ],
        scratch_shapes=[pltpu.VMEM((8, w_sgu), F32)] + [pltpu.VMEM((BLK, w_sgu), F32)] * 3,
        input_output_aliases={7: 0},
    )(z, z, da, gain, ws_b, wst_b, b_col, dz, *after)


def _attn_softmax(sink_ref, q_ref, k_ref, v_ref, bias_ref, s_len, grp):
    kv = pl.program_id(0)
    n = pl.program_id(1)
    start = pl.multiple_of(n * BLK, BLK)
    kb = k_ref[pl.ds(start, 3 * BLK), :]
    vb = v_ref[pl.ds(start, 3 * BLK), :]
    qv = q_ref[...]
    qs = jnp.concatenate([qv[:, g * HEAD_DIM : (g + 1) * HEAD_DIM] for g in range(grp)], axis=0).astype(BF16)
    sc = lax.dot_general(qs, kb, NT, preferred_element_type=F32) * (HEAD_DIM**-0.5)
    sc = sc + bias_ref[...].reshape(grp * BLK, 3 * BLK)
    kpos = start + lax.broadcasted_iota(I32, (1, 3 * BLK), 1) - BLK
    sc = jnp.where((kpos >= 0) & (kpos < s_len), sc, NEG)
    sink = jnp.concatenate([jnp.full((BLK, 1), sink_ref[kv * grp + g], F32) for g in range(grp)], axis=0)
    m = jnp.maximum(jnp.max(sc, axis=-1, keepdims=True), sink)
    p = jnp.exp(sc - m)
    esink = jnp.exp(sink - m)
    den = jnp.sum(p, axis=-1, keepdims=True) + esink
    return start, qs, kb, vb, p / den, esink / den


def _attn_specs(s, grp, q_blk0):
    qw = grp * HEAD_DIM
    return [
        pl.BlockSpec(memory_space=pltpu.SMEM),
        pl.BlockSpec((BLK, qw), lambda kv, n: (n, q_blk0 + kv)),
        pl.BlockSpec((s + 2 * BLK, HEAD_DIM), lambda kv, n: (0, kv)),
        pl.BlockSpec((s + 2 * BLK, HEAD_DIM), lambda kv, n: (0, kv)),
        pl.BlockSpec((grp, BLK, 3 * BLK), lambda kv, n: (kv, 0, 0)),
    ]


def _attn_fwd(sink, z, k_pad, v_pad, bias_tab, grp, q_blk0):
    s = z.shape[0]
    qw = grp * HEAD_DIM

    def body(sink_ref, q_ref, k_ref, v_ref, bias_ref, o_ref):
        _, _, _, vb, pn, _ = _attn_softmax(sink_ref, q_ref, k_ref, v_ref, bias_ref, s, grp)
        o = jnp.dot(pn.astype(BF16), vb, preferred_element_type=F32)
        for g in range(grp):
            o_ref[:, g * HEAD_DIM : (g + 1) * HEAD_DIM] = o[g * BLK : (g + 1) * BLK].astype(BF16)

    return _pallas(
        body,
        name="attn_fwd",
        grid=(N_KV_HEADS, s // BLK),
        in_specs=_attn_specs(s, grp, q_blk0),
        out_specs=pl.BlockSpec((BLK, qw), lambda kv, n: (n, kv)),
        out_shape=jax.ShapeDtypeStruct((s, N_KV_HEADS * qw), BF16),
    )(sink, z, k_pad, v_pad, bias_tab)


def _attn_bwd(sink, z, k_pad, v_pad, bias_tab, dout, dz, grp, q_blk0):
    s = z.shape[0]
    qw = grp * HEAD_DIM
    nb = s // BLK
    heads = N_KV_HEADS * grp

    def body(sink_ref, q_ref, k_ref, v_ref, bias_ref, do_ref, dz_in, dq_ref, dk_ref, dv_ref, dbias_ref, dsink_ref, dk_acc, dv_acc):
        del dz_in
        kv = pl.program_id(0)
        n = pl.program_id(1)
        start, qs, kb, vb, pn, psink = _attn_softmax(sink_ref, q_ref, k_ref, v_ref, bias_ref, s, grp)
        dov = do_ref[...]
        dos = jnp.concatenate([dov[:, g * HEAD_DIM : (g + 1) * HEAD_DIM] for g in range(grp)], axis=0)
        dp = lax.dot_general(dos, vb, NT, preferred_element_type=F32)
        dvb = lax.dot_general(pn.astype(BF16), dos, TN, preferred_element_type=F32)
        delta = jnp.sum(pn * dp, axis=-1, keepdims=True)
        ds = pn * (dp - delta)
        dsb = (ds * (HEAD_DIM**-0.5)).astype(BF16)
        dq = jnp.dot(dsb, kb, preferred_element_type=F32)
        dkb = lax.dot_general(dsb, qs, TN, preferred_element_type=F32)
        for g in range(grp):
            dq_ref[:, g * HEAD_DIM : (g + 1) * HEAD_DIM] = dq[g * BLK : (g + 1) * BLK].astype(BF16)

        @pl.when(n == 0)
        def _():
            dk_acc[...] = jnp.zeros_like(dk_acc)
            dv_acc[...] = jnp.zeros_like(dv_acc)
            dbias_ref[...] = jnp.zeros_like(dbias_ref)

        @pl.when((n == 0) & (kv == 0))
        def _():
            dsink_ref[...] = jnp.zeros_like(dsink_ref)

        dk_acc[pl.ds(start, 3 * BLK), :] += dkb
        dv_acc[pl.ds(start, 3 * BLK), :] += dvb
        dbias_ref[...] += ds.reshape(grp, BLK, 3 * BLK)
        row = lax.broadcasted_iota(I32, (heads, LANES), 0)
        sd = psink * delta
        upd = jnp.zeros((heads, LANES), F32)
        for g in range(grp):
            upd = jnp.where(row == kv * grp + g, -jnp.sum(sd[g * BLK : (g + 1) * BLK]), upd)
        dsink_ref[...] += upd

        @pl.when(n == nb - 1)
        def _():
            dk_ref[...] = dk_acc[...]
            dv_ref[...] = dv_acc[...]

    pad_spec = pl.BlockSpec((s + 2 * BLK, HEAD_DIM), lambda kv, n: (0, kv))
    kvw = N_KV_HEADS * HEAD_DIM
    return _pallas(
        body,
        name="attn_bwd",
        grid=(N_KV_HEADS, nb),
        in_specs=_attn_specs(s, grp, q_blk0) + [pl.BlockSpec((BLK, qw), lambda kv, n: (n, kv)), ANY],
        out_specs=[
            pl.BlockSpec((BLK, qw), lambda kv, n: (n, q_blk0 + kv)),
            pad_spec,
            pad_spec,
            pl.BlockSpec((grp, BLK, 3 * BLK), lambda kv, n: (kv, 0, 0)),
            pl.BlockSpec((heads, LANES), lambda kv, n: (0, 0)),
        ],
        out_shape=[
            jax.ShapeDtypeStruct(dz.shape, BF16),
            jax.ShapeDtypeStruct((s + 2 * BLK, kvw), F32),
            jax.ShapeDtypeStruct((s + 2 * BLK, kvw), F32),
            jax.ShapeDtypeStruct((heads, BLK, 3 * BLK), F32),
            jax.ShapeDtypeStruct((heads, LANES), F32),
        ],
        scratch_shapes=[pltpu.VMEM((s + 2 * BLK, HEAD_DIM), F32), pltpu.VMEM((s + 2 * BLK, HEAD_DIM), F32)],
        input_output_aliases={6: 0},
    )(sink, z, k_pad, v_pad, bias_tab, dout, dz)


def _dkv_to_dz(dk_pad, dv_pad, dz, blk_idx):
    s = dz.shape[0]
    kvw = dk_pad.shape[1]

    def body(dk_ref, dv_ref, dz_in, out_ref):
        del dz_in
        out_ref[:, :kvw] = dk_ref[...].astype(BF16)
        out_ref[:, kvw:] = dv_ref[...].astype(BF16)

    src = pl.BlockSpec((BLK, kvw), lambda i: (i + 1, 0))
    return _pallas(
        body,
        name="dkv_to_dz",
        grid=(s // BLK,),
        in_specs=[src, src, ANY],
        out_specs=pl.BlockSpec((BLK, 2 * kvw), lambda i: (i, blk_idx)),
        out_shape=jax.ShapeDtypeStruct(dz.shape, BF16),
        input_output_aliases={2: 0},
    )(dk_pad, dv_pad, dz)


def _relbias_bwd(dbias_tab, bucket):
    heads = dbias_tab.shape[0]

    def body(dt_ref, bk_ref, out_ref):
        lane = lax.broadcasted_iota(I32, (1, LANES), 1)
        bk = bk_ref[...]
        rows = []
        for h in range(heads):
            dt = dt_ref[h]
            acc = jnp.zeros((1, LANES), F32)
            for b in range(REL_BUCKETS):
                acc = jnp.where(lane == b, jnp.sum(jnp.where(bk == b, dt, 0.0)), acc)
            rows.append(acc)
        out_ref[...] = jnp.concatenate(rows, axis=0)

    return _pallas(body, name="relbias_bwd", out_shape=jax.ShapeDtypeStruct((heads, LANES), F32))(dbias_tab, bucket)


def _t5_bucket(rel):
    nb = REL_BUCKETS // 2
    ret = jnp.where(rel > 0, nb, 0)
    n = jnp.abs(rel)
    max_exact = nb // 2
    nf = jnp.maximum(n, 1).astype(F32)
    large = max_exact + (jnp.log(nf / max_exact) / math.log(REL_MAX_DIST / max_exact) * (nb - max_exact)).astype(I32)
    large = jnp.minimum(large, nb - 1)
    return ret + jnp.where(n < max_exact, n, large)


def _band_tables(rel_bias):
    qi = jnp.arange(BLK)[:, None]
    kj = jnp.arange(3 * BLK)[None, :]
    rel = kj - BLK - qi
    bucket = _t5_bucket(rel).astype(I32)
    heads = rel_bias.shape[1]
    masked = jnp.where(jnp.abs(rel) <= BLK, bucket, -1)

    def body(rb_ref, bk_ref, out_ref):
        bk = bk_ref[...]
        for h in range(heads):
            tab = jnp.full(bk.shape, NEG, F32)
            for b in range(REL_BUCKETS):
                tab = jnp.where(bk == b, rb_ref[b, h], tab)
            out_ref[h] = tab

    bias_tab = _pallas(
        body,
        name="bias_table",
        in_specs=[pl.BlockSpec(memory_space=pltpu.SMEM), pl.BlockSpec(memory_space=pltpu.VMEM)],
        out_specs=pl.BlockSpec(memory_space=pltpu.VMEM),
        out_shape=jax.ShapeDtypeStruct((heads, BLK, 3 * BLK), F32),
    )(rel_bias.astype(F32), masked)
    return bias_tab, bucket


EW_BLOCK_ELEMS = 512 * 1024


def _ew_tiles(shape, elems=EW_BLOCK_ELEMS // 2):
    r, c = shape
    tn = c if c <= 2048 else _tile(c, (2048, 1920, 1536, 1408, 1024, 512))
    tm = _tile(r, [t for t in (1024, 512, 256, 128, 64, 32, 16, 8) if t * tn <= elems] or [8])
    return tm, tn


def _cast_into_full(name, qidx, w, kind, after=()):
    r, c = w.shape
    tm, tn = _ew_tiles(w.shape, EW_BLOCK_ELEMS)
    nbi, nbj = r // tm, c // tn
    if kind == "col":
        full, out_spec = (r, c * N_CHIPS), pl.BlockSpec((tm, tn), lambda i, j, q: (i, q[0] * nbj + j))
    else:
        full, out_spec = (r * N_CHIPS, c), pl.BlockSpec((tm, tn), lambda i, j, q: (q[0] * nbi + i, j))

    def body(q_ref, w_ref, *rest):
        del q_ref
        rest[-1][...] = w_ref[...].astype(BF16)

    return _pallas(
        body,
        name=name,
        grid_spec=pltpu.PrefetchScalarGridSpec(
            num_scalar_prefetch=1,
            grid=(nbi, nbj),
            in_specs=[pl.BlockSpec((tm, tn), lambda i, j, q: (i, j))] + [ANY] * len(after),
            out_specs=out_spec,
        ),
        out_shape=jax.ShapeDtypeStruct(full, BF16),
    )(qidx, w, *after)


def _adamw(name, w, g, m, v, after=()):
    tm, tn = _ew_tiles(w.shape, EW_BLOCK_ELEMS)
    if _nbytes(w.shape, F32) <= 1024 * 1024:
        tm, tn = w.shape
    spec = pl.BlockSpec((tm, tn), lambda i, j: (i, j))
    n_after = len(after)

    def body(w_ref, g_ref, m_ref, v_ref, *rest):
        d_ref, nm_ref, nv_ref, g_out_ref = rest[n_after:]
        gv = g_ref[...]
        g_out_ref[...] = gv
        nm = ADAM_B1 * m_ref[...] + (1.0 - ADAM_B1) * gv
        nv = ADAM_B2 * v_ref[...] + (1.0 - ADAM_B2) * (gv * gv)
        m_hat = nm / (1.0 - ADAM_B1**ADAM_STEP)
        v_hat = nv / (1.0 - ADAM_B2**ADAM_STEP)
        d_ref[...] = -ADAM_LR * (m_hat / (jnp.sqrt(v_hat) + ADAM_EPS) + ADAM_WD * w_ref[...])
        nm_ref[...] = nm
        nv_ref[...] = nv

    out = jax.ShapeDtypeStruct(w.shape, F32)
    return _pallas(
        body, name=name, grid=(w.shape[0] // tm, w.shape[1] // tn), in_specs=[spec] * 4 + [ANY] * n_after,
        out_specs=[spec] * 4, out_shape=[out, out, out, out],
        compiler_params=_params(_mm_vmem([((tm, tn), F32, 24)])),
    )(w, g, m, v, *after)


def _pair_add(name, cidx, g_full, r_sib, kind):
    hr, hc = r_sib.shape
    tm, tn = _ew_tiles((hr, hc), 2 * EW_BLOCK_ELEMS)
    nbi, nbj = hr // tm, hc // tn
    if kind == "col":
        g_spec = pl.BlockSpec((tm, tn), lambda i, j, c: (c[0] * nbi + i, j))
    else:
        g_spec = pl.BlockSpec((tm, tn), lambda i, j, c: (i, c[0] * nbj + j))
    spec = pl.BlockSpec((tm, tn), lambda i, j, c: (i, j))

    def body(c_ref, g_ref, r_ref, o_ref):
        del c_ref
        o_ref[...] = (g_ref[...].astype(F32) + r_ref[...].astype(F32)).astype(BF16)

    return _pallas(
        body,
        name=name,
        grid_spec=pltpu.PrefetchScalarGridSpec(num_scalar_prefetch=1, grid=(nbi, nbj), in_specs=[g_spec, spec], out_specs=spec),
        out_shape=jax.ShapeDtypeStruct((hr, hc), BF16),
        compiler_params=_params(_mm_vmem([((tm, tn), BF16, 6), ((tm, tn), F32, 3)])),
    )(cidx, g_full, r_sib)


def _chip_sum(name, qidx, c_half, r_ici, kind):
    _, pr, pc = r_ici.shape
    tm, tn = _ew_tiles((pr, pc), 2 * EW_BLOCK_ELEMS)
    nbi, nbj = pr // tm, pc // tn
    if kind == "col":
        own_spec = pl.BlockSpec((tm, tn), lambda i, j, q: (i, q[0] * nbj + j))
        full, out_spec = (2 * pr, pc), pl.BlockSpec((tm, tn), lambda i, j, q: (q[1] * nbi + i, j))
    else:
        own_spec = pl.BlockSpec((tm, tn), lambda i, j, q: (q[0] * nbi + i, j))
        full, out_spec = (pr, 2 * pc), pl.BlockSpec((tm, tn), lambda i, j, q: (i, q[1] * nbj + j))

    def body(q_ref, own_ref, r_ref, o_ref):
        q = q_ref[0]
        own = own_ref[...].astype(F32)
        recv = [r_ref[r].astype(F32) for r in range(3)]
        total = None
        for chip in range(N_CHIPS):
            d = chip ^ q
            term = jnp.where(d == 0, own, jnp.where(d == 2, recv[0], jnp.where(d == 1, recv[1], recv[2])))
            total = term if total is None else total + term
        o_ref[...] = total

    return _pallas(
        body,
        name=name,
        grid_spec=pltpu.PrefetchScalarGridSpec(
            num_scalar_prefetch=1,
            grid=(nbi, nbj),
            in_specs=[own_spec, pl.BlockSpec((3, tm, tn), lambda i, j, q: (0, i, j))],
            out_specs=out_spec,
        ),
        out_shape=jax.ShapeDtypeStruct(full, F32),
        compiler_params=_params(_mm_vmem([((tm, tn), BF16, 8), ((tm, tn), F32, 6)])),
    )(qidx, c_half, r_ici)


_REL_MASK = (2, 1, 3)


def _place():
    x, y, c = lax.axis_index("x"), lax.axis_index("y"), lax.axis_index("c")
    chips = [(1 - x, y), (x, 1 - y), (1 - x, 1 - y)]
    return x, y, c, 2 * x + y, chips


def _shard_view(ref, kind, chip):
    if kind == "col":
        w = ref.shape[1] // N_CHIPS
        return ref.at[:, pl.ds(pl.multiple_of(chip * w, LANES), w)]
    h = ref.shape[0] // N_CHIPS
    return ref.at[pl.ds(pl.multiple_of(chip * h, 16), h), :]


def _row_half(ref, half):
    h = ref.shape[0] // 2
    return ref.at[pl.ds(pl.multiple_of(half * h, 16), h), :]


def _pair_half(ref, kind, half):
    if kind == "col":
        return _row_half(ref, half)
    w = ref.shape[1] // 2
    return ref.at[:, pl.ds(pl.multiple_of(half * w, LANES), w)]


def _remote(src, dst, send_sem, recv_sem, dev):
    return pltpu.make_async_remote_copy(src_ref=src, dst_ref=dst, send_sem=send_sem, recv_sem=recv_sem, device_id=dev, device_id_type=MESH)


def _hbm(a):
    return pltpu.with_memory_space_constraint(a, pltpu.HBM)


def _gather_start(name, fulls, kinds, rels=(0, 1, 2), after=()):
    n_w = len(fulls)

    def body(*refs):
        g = refs[:n_w]
        send_sem, recv_sem = refs[n_w + len(after)], refs[n_w + len(after) + 1]
        token = refs[-1]
        _, _, c, q, chips = _place()
        for w in range(n_w):
            mine = _row_half(_shard_view(g[w], kinds[w], q), c)
            for r in rels if isinstance(rels, tuple) else rels[w]:
                _remote(mine, mine, send_sem.at[3 * w + r], recv_sem.at[3 * w + r], (*chips[r], c)).start()
        token[...] = jnp.zeros_like(token)

    res = _pallas(
        body,
        name=name,
        out_shape=(
            pltpu.SemaphoreType.DMA((3 * n_w,)),
            pltpu.SemaphoreType.DMA((3 * n_w,)),
            *[pltpu.HBM(f.shape, f.dtype) for f in fulls],
            jax.ShapeDtypeStruct((8, LANES), F32),
        ),
        in_specs=[HBM_SPEC] * n_w + [ANY] * len(after),
        out_specs=(SEM_SPEC, SEM_SPEC, *[HBM_SPEC] * n_w, pl.BlockSpec(memory_space=pltpu.VMEM)),
        input_output_aliases={w: w + 2 for w in range(n_w)},
        compiler_params=pltpu.CompilerParams(has_side_effects=EFFECT),
    )(*[_hbm(f) for f in fulls], *after)
    return res[0], res[1], list(res[2 : 2 + n_w]), res[-1]


def _relay_copies(kinds, waiting):
    def copies(refs, send_sem, recv_sem):
        _, _, c, q, chips = _place()
        out = []
        for i, kind in enumerate(kinds):
            for k, (src_rel, dst_rel) in enumerate(((0, 1), (1, 0))):
                held = _row_half(_row_half(_shard_view(refs[i], kind, q ^ _REL_MASK[src_rel]), c), k)
                far = _row_half(_row_half(_shard_view(refs[i], kind, q ^ _REL_MASK[2]), c), k)
                dst = far if waiting else held
                out.append(_remote(held, dst, send_sem.at[2 * i + k], recv_sem.at[2 * i + k], (*chips[dst_rel], c)))
        return out

    return copies


def _forward_copies(kinds, waiting, rels=(0, 1, 2), sem0=0):
    def copies(refs, send_sem, recv_sem):
        x, y, c, q, _ = _place()
        out = []
        for i, kind in enumerate(kinds):
            for k, r in enumerate(rels):
                quarter = _shard_view(refs[i], kind, q ^ _REL_MASK[r])
                landed = _row_half(quarter, c)
                dst = _row_half(quarter, 1 - c) if waiting else landed
                sem = sem0 + len(rels) * i + k
                out.append(_remote(landed, dst, send_sem.at[sem], recv_sem.at[sem], (x, y, 1 - c)))
        return out

    return copies


def _relay_and_forward_copies(kinds, waiting):
    forward = _forward_copies(kinds, waiting, rels=(0, 1), sem0=2 * len(kinds))
    relay = _relay_copies(kinds, waiting)
    return lambda refs, send_sem, recv_sem: forward(refs, send_sem, recv_sem) + relay(refs, send_sem, recv_sem)


def _gather_wait(name, fulls, kinds, w_ids, send_sem, recv_sem, after, rels=(0, 1, 2)):
    n = len(fulls)

    def body(*refs):
        g = refs[:n]
        s_sem, r_sem = refs[n], refs[n + 1]
        x, y, c, q, _ = _place()
        for i, w in enumerate(w_ids):
            mine = _row_half(_shard_view(g[i], kinds[i], q), c)
            for r in rels:
                landed = _row_half(_shard_view(g[i], kinds[i], q ^ _REL_MASK[r]), c)
                cp = _remote(mine, landed, s_sem.at[3 * w + r], r_sem.at[3 * w + r], (x, y, 1 - c))
                cp.wait_send()
                cp.wait_recv()

    res = _pallas(
        body,
        name=name,
        out_shape=[pltpu.HBM(f.shape, f.dtype) for f in fulls],
        in_specs=[HBM_SPEC] * n + [SEM_SPEC, SEM_SPEC, ANY],
        out_specs=[HBM_SPEC] * n,
        input_output_aliases={i: i for i in range(n)},
        compiler_params=pltpu.CompilerParams(has_side_effects=EFFECT),
    )(*fulls, send_sem, recv_sem, after)
    return list(res)


def _gather_forward(name, fulls, kinds, rels=(0, 1, 2)):
    n = len(fulls)

    def body(*refs):
        g = refs[n : 2 * n]
        send, recv = refs[2 * n :]
        _sibling_handshake()
        x, y, c, q, _ = _place()
        sib = (x, y, 1 - c)
        cps = []
        for i in range(n):
            for r in rels:
                landed = _row_half(_shard_view(g[i], kinds[i], q ^ _REL_MASK[r]), c)
                cps.append(_remote(landed, landed, send.at[i, r], recv.at[i, r], sib))
        for cp in cps:
            cp.start()
        for i in range(n):
            for r in rels:
                other = _row_half(_shard_view(g[i], kinds[i], q ^ _REL_MASK[r]), 1 - c)
                _remote(other, other, send.at[i, r], recv.at[i, r], sib).wait_recv()
        for cp in cps:
            cp.wait_send()

    res = _pallas(
        body,
        name=name,
        in_specs=[ANY] * n,
        out_specs=[ANY] * n,
        out_shape=[jax.ShapeDtypeStruct(f.shape, f.dtype) for f in fulls],
        scratch_shapes=[pltpu.SemaphoreType.DMA((n, 3)), pltpu.SemaphoreType.DMA((n, 3))],
        input_output_aliases={i: i for i in range(n)},
        compiler_params=pltpu.CompilerParams(collective_id=SIBLING_BARRIER_ID),
    )(*fulls)
    return list(res)


SIBLING_BARRIER_ID = 1


def _sibling_handshake():
    sib = (lax.axis_index("x"), lax.axis_index("y"), 1 - lax.axis_index("c"))
    barrier = pltpu.get_barrier_semaphore()
    pl.semaphore_signal(barrier, inc=1, device_id=sib, device_id_type=MESH)
    pl.semaphore_wait(barrier, 1)


def _split_start(name, bufs, n_sems, copies, sibling_only=False):
    n = len(bufs)

    def body(*refs):
        if sibling_only:
            _sibling_handshake()
        for cp in copies(refs[:n], refs[n], refs[n + 1]):
            cp.start()

    extra = {"collective_id": SIBLING_BARRIER_ID} if sibling_only else {}

    res = _pallas(
        body,
        name=name,
        out_shape=(
            pltpu.SemaphoreType.DMA((n_sems,)),
            pltpu.SemaphoreType.DMA((n_sems,)),
            *[pltpu.HBM(b.shape, b.dtype) for b in bufs],
        ),
        in_specs=[HBM_SPEC] * n,
        out_specs=(SEM_SPEC, SEM_SPEC, *[HBM_SPEC] * n),
        input_output_aliases={i: i + 2 for i in range(n)},
        compiler_params=pltpu.CompilerParams(has_side_effects=EFFECT, **extra),
    )(*[_hbm(b) for b in bufs])
    return res[0], res[1], list(res[2:])


def _split_wait(name, bufs, send_sem, recv_sem, copies, after):
    n = len(bufs)

    def body(*refs):
        for cp in copies(refs[:n], refs[n], refs[n + 1]):
            cp.wait_send()
            cp.wait_recv()

    res = _pallas(
        body,
        name=name,
        out_shape=[pltpu.HBM(b.shape, b.dtype) for b in bufs],
        in_specs=[HBM_SPEC] * n + [SEM_SPEC, SEM_SPEC, ANY],
        out_specs=[HBM_SPEC] * n,
        input_output_aliases={i: i for i in range(n)},
        compiler_params=pltpu.CompilerParams(has_side_effects=EFFECT),
    )(*bufs, send_sem, recv_sem, after)
    return list(res)


def _pair_exchange_copies(kinds):
    n = len(kinds)

    def copies(refs, send_sem, recv_sem):
        x, y, c, _, _ = _place()
        return [
            _remote(_pair_half(refs[w], kinds[w], 1 - c), refs[n + w], send_sem.at[w], recv_sem.at[w], (x, y, 1 - c))
            for w in range(n)
        ]

    return copies


def _pair_share_copies(kinds, waiting):
    def copies(refs, send_sem, recv_sem):
        x, y, c, _, _ = _place()
        out = []
        for w, kind in enumerate(kinds):
            mine = _pair_half(refs[w], kind, c)
            dst = _pair_half(refs[w], kind, 1 - c) if waiting else mine
            out.append(_remote(mine, dst, send_sem.at[w], recv_sem.at[w], (x, y, 1 - c)))
        return out

    return copies


def _piece_shape(half_shape, kind):
    r, c = half_shape
    return (3, r, c // N_CHIPS) if kind == "col" else (3, r // N_CHIPS, c)


def _chip_send_start(name, halves, kinds):
    n = len(halves)
    lands = [lax.empty(_piece_shape(h.shape, k), BF16) for h, k in zip(halves, kinds)]

    def body(*refs):
        h, land = refs[:n], refs[n : 2 * n]
        send_sem, recv_sem = refs[2 * n], refs[2 * n + 1]
        _, _, c, q, chips = _place()
        for i in range(n):
            for r, chip in enumerate(chips):
                piece = _shard_view(h[i], kinds[i], q ^ _REL_MASK[r])
                _remote(piece, land[i].at[r], send_sem.at[3 * i + r], recv_sem.at[3 * i + r], (*chip, c)).start()

    res = _pallas(
        body,
        name=name,
        out_shape=(
            pltpu.SemaphoreType.DMA((3 * n,)),
            pltpu.SemaphoreType.DMA((3 * n,)),
            *[pltpu.HBM(a.shape, a.dtype) for a in halves],
            *[pltpu.HBM(a.shape, a.dtype) for a in lands],
        ),
        in_specs=[HBM_SPEC] * (2 * n),
        out_specs=(SEM_SPEC, SEM_SPEC, *[HBM_SPEC] * (2 * n)),
        input_output_aliases={i: i + 2 for i in range(2 * n)},
        compiler_params=pltpu.CompilerParams(has_side_effects=EFFECT),
    )(*[_hbm(a) for a in halves], *[_hbm(a) for a in lands])
    return res[0], res[1], list(res[2 : 2 + n]), list(res[2 + n :])


def _chip_send_wait(name, halves, lands, kinds, send_sem, recv_sem, after):
    n = len(halves)

    def body(*refs):
        h, land = refs[:n], refs[n : 2 * n]
        s_sem, r_sem = refs[2 * n], refs[2 * n + 1]
        x, y, c, q, _ = _place()
        for i in range(n):
            for r in range(3):
                piece = _shard_view(h[i], kinds[i], q ^ _REL_MASK[r])
                cp = _remote(piece, land[i].at[r], s_sem.at[3 * i + r], r_sem.at[3 * i + r], (x, y, 1 - c))
                cp.wait_send()
                cp.wait_recv()

    res = _pallas(
        body,
        name=name,
        out_shape=[pltpu.HBM(a.shape, a.dtype) for a in halves] + [pltpu.HBM(a.shape, a.dtype) for a in lands],
        in_specs=[HBM_SPEC] * (2 * n) + [SEM_SPEC, SEM_SPEC, ANY],
        out_specs=[HBM_SPEC] * (2 * n),
        input_output_aliases={i: i for i in range(2 * n)},
        compiler_params=pltpu.CompilerParams(has_side_effects=EFFECT),
    )(*halves, *lands, send_sem, recv_sem, after)
    return list(res[:n]), list(res[n:])


def _small_exchange_copies(waiting):
    def copies(refs, send_sem, recv_sem):
        p, land = refs
        x, y, c, q, _ = _place()
        me = 2 * q + c
        out = []
        for dd in range(1, 2 * N_CHIPS):
            dev = (x ^ ((dd >> 2) & 1), y ^ ((dd >> 1) & 1), c ^ (dd & 1))
            dst = land.at[me ^ dd] if waiting else land.at[me]
            out.append(_remote(p, dst, send_sem.at[dd - 1], recv_sem.at[dd - 1], dev))
        return out

    return copies


def _small_sum(name, me_idx, p, land):
    rows = p.shape[0]
    n_dev = 2 * N_CHIPS

    def body(me_ref, p_ref, land_ref, o_ref):
        me = me_ref[0]
        total = None
        for dev in range(n_dev):
            term = jnp.where(me == dev, p_ref[...], land_ref[dev])
            total = term if total is None else total + term
        o_ref[...] = total

    return _pallas(
        body,
        name=name,
        grid_spec=pltpu.PrefetchScalarGridSpec(
            num_scalar_prefetch=1,
            grid=(1,),
            in_specs=[pl.BlockSpec((rows, LANES), lambda i, m: (0, 0)), pl.BlockSpec((n_dev, rows, LANES), lambda i, m: (0, 0, 0))],
            out_specs=pl.BlockSpec((rows, LANES), lambda i, m: (0, 0)),
        ),
        out_shape=jax.ShapeDtypeStruct(p.shape, F32),
    )(me_idx, p, land)


def _pack(parts):
    rows = []
    for a in parts:
        flat = a.reshape(-1).astype(F32)
        n = flat.shape[0]
        padded = -(-n // (8 * LANES)) * (8 * LANES)
        rows.append(jnp.pad(flat, (0, padded - n)).reshape(-1, LANES))
    return jnp.concatenate(rows, axis=0)


def _unpack(packed, shapes):
    out, row = [], 0
    for shp in shapes:
        n = int(np.prod(shp))
        nrows = -(-n // (8 * LANES)) * 8
        out.append(packed[row : row + nrows].reshape(-1)[:n].reshape(shp))
        row += nrows
    return out


def kernel(x, w_in, norm_mix, sgu_v_gain, sgu_w_s, sgu_b_s, w_a_out, attn_sink, rel_bias, w_b_out, w_o, norm_ffn, w_gate, w_up, w_down, norm_final, loss_target, m_w_in, m_norm_mix, m_sgu_v_gain, m_sgu_w_s, m_sgu_b_s, m_w_a_out, m_attn_sink, m_rel_bias, m_w_b_out, m_w_o, m_norm_ffn, m_w_gate, m_w_up, m_w_down, m_norm_final, v_w_in, v_norm_mix, v_sgu_v_gain, v_sgu_w_s, v_sgu_b_s, v_w_a_out, v_attn_sink, v_rel_bias, v_w_b_out, v_w_o, v_norm_ffn, v_w_gate, v_w_up, v_w_down, v_norm_final):
    s, d = x.shape[1], x.shape[2]
    w_sgu = sgu_v_gain.shape[1]
    groups = sgu_w_s.shape[1]
    heads = attn_sink.shape[1]
    grp = heads // N_KV_HEADS
    w_att = heads * HEAD_DIM
    w_kv = N_KV_HEADS * HEAD_DIM
    d_ff = w_gate.shape[2] * N_CHIPS
    n_in = w_in.shape[2] * N_CHIPS
    off_q = 2 * w_sgu
    off_k = off_q + w_att
    off_g = off_k + 2 * w_kv
    assert n_in == off_g + 2 * d and groups * BLK == w_sgu and s % BLK == 0

    x2d = x.reshape(s, d)
    tgt = loss_target.reshape(s, d)
    c_idx = lax.axis_index("c").astype(I32).reshape(1)
    q_idx = (2 * lax.axis_index("x") + lax.axis_index("y")).astype(I32).reshape(1)
    qc_idx = jnp.concatenate([q_idx, c_idx])

    W_IN, W_A, W_B, W_O, W_GATE, W_UP, W_DOWN = range(7)
    names = ["w_in", "w_a", "w_b", "w_o", "w_gate", "w_up", "w_down"]
    kinds = ["col", "col", "col", "row", "col", "col", "row"]
    big_w = [w_in[0], w_a_out[0], w_b_out[0], w_o[0], w_gate[0], w_up[0], w_down[0]]
    big_m = [m_w_in[0], m_w_a_out[0], m_w_b_out[0], m_w_o[0], m_w_gate[0], m_w_up[0], m_w_down[0]]
    big_v = [v_w_in[0], v_w_a_out[0], v_w_b_out[0], v_w_o[0], v_w_gate[0], v_w_up[0], v_w_down[0]]
    full_in = _cast_into_full("cast_w_in", q_idx, big_w[W_IN], kinds[W_IN])
    in_send, in_recv, (full_in,), token = _gather_start("gather_start_in", [full_in], [kinds[W_IN]], rels=(0, 1))
    rest = [_cast_into_full("cast_" + names[i], q_idx, big_w[i], kinds[i], after=(token,)) for i in range(1, 7)]
    h1 = _rms_fwd("rms_mix", x2d, norm_mix, after=(token,))
    (full_in,) = _gather_wait("gather_wait_in", [full_in], [kinds[W_IN]], [0], in_send, in_recv, h1, rels=(0, 1))
    relay_send, relay_recv, (full_in,) = _split_start(
        "gather_relay_in", [full_in], 4, _relay_and_forward_copies([kinds[W_IN]], False)
    )
    full_down = rest.pop()
    ag_send, ag_recv, rest, token = _gather_start("gather_start_rest", rest, kinds[1:6], rels=(0, 1), after=(full_in,))
    (full_in,) = _split_wait(
        "gather_relay_wait_in", [full_in], relay_send, relay_recv, _relay_and_forward_copies([kinds[W_IN]], True), token
    )
    (g_in,) = _gather_forward("gather_fwd_in", [full_in], [kinds[W_IN]], rels=(2,))
    fulls = [g_in] + rest

    def relay_begin(tag, ids, after, copies=_relay_copies, sems_per_weight=2):
        ks = [kinds[i] for i in ids]
        bufs = _gather_wait("gather_wait_" + tag, [fulls[i] for i in ids], ks, [i - 1 for i in ids], ag_send, ag_recv, after,
                            rels=(0, 1))
        send, recv, bufs = _split_start("gather_relay_" + tag, bufs, sems_per_weight * len(ids), copies(ks, False))
        return tag, ks, send, recv, bufs

    def relay_end(state, after):
        tag, ks, send, recv, bufs = state
        bufs = _split_wait("gather_relay_wait_" + tag, bufs, send, recv, _relay_and_forward_copies(ks, True), after)
        return _gather_forward("gather_fwd_" + tag, bufs, ks, rels=(2,))

    def relay_end_async(state, after):
        tag, ks, send, recv, bufs = state
        bufs = _split_wait("gather_relay_wait_" + tag, bufs, send, recv, _relay_copies(ks, True), after)
        send, recv, bufs = _split_start("gather_fwd_start_" + tag, bufs, 3 * len(ks), _forward_copies(ks, False), sibling_only=True)
        return tag, ks, send, recv, bufs

    def forwarded(state, after):
        tag, ks, send, recv, bufs = state
        return _split_wait("gather_fwd_wait_" + tag, bufs, send, recv, _forward_copies(ks, True), after)

    ws_b = sgu_w_s[0].astype(BF16)
    wst_b = jnp.swapaxes(sgu_w_s[0], 1, 2).astype(BF16)
    b_col = sgu_b_s[0].reshape(groups, BLK, 1)
    bias_tab, bucket = _band_tables(rel_bias)
    sink = attn_sink[0]

    tm = _tile(s, (1024, 512, 256, 128))

    tn = _tile(n_in, (768, 640, 512))
    z = _mm(
        "mm_z", (s // tm, n_in // tn, 1), [h1, g_in],
        [pl.BlockSpec((tm, d), lambda i, j, k: (i, 0)), pl.BlockSpec((d, tn), lambda i, j, k: (0, j))],
        [jax.ShapeDtypeStruct((s, n_in), BF16)], [pl.BlockSpec((tm, tn), lambda i, j, k: (i, j))],
        [(0, 1, NN, 0)], 1, (tm, tn), 1, lambda ins, vals, outs, cs: _put(outs[0], cs, vals[0]),
        _mm_vmem([((tm, d), BF16, 2), ((d, tn), BF16, 2), ((tm, tn), F32, 3)]),
    )[0]
    mix_relay = relay_begin("mix", [W_A, W_B, W_O], z, copies=_relay_and_forward_copies, sems_per_weight=4)

    a_act = _sgu_fwd(z, sgu_v_gain, ws_b, b_col, w_sgu, after=(mix_relay[4][0],))

    kv_b = z[:, off_k:off_g]
    k_pad = jnp.pad(kv_b[:, :w_kv], ((BLK, BLK), (0, 0)))
    v_pad = jnp.pad(kv_b[:, w_kv:], ((BLK, BLK), (0, 0)))
    q_blk0 = off_q // (grp * HEAD_DIM)
    att = _attn_fwd(sink, z, k_pad, v_pad, bias_tab, grp, q_blk0)

    gate_relay = relay_begin("gate", [W_GATE], att)
    up_relay = relay_begin("up", [W_UP], gate_relay[4][0])
    down_send, down_recv, (full_down,), token = _gather_start(
        "gather_start_down", [full_down], [kinds[W_DOWN]], after=(a_act, up_relay[4][0])
    )
    g_a, g_b, g_o = relay_end(mix_relay, token)

    tg = _tile(d, (512,))
    ga0, gb0 = off_g // tg, (off_g + d) // tg

    def ep_gate(ins, vals, outs, cs):
        sa, sb = _sigmoid(ins[4][:, cs].astype(F32)), _sigmoid(ins[5][:, cs].astype(F32))
        _put(outs[0], cs, sa * vals[0] + sb * vals[1])
        _put(outs[1], cs, vals[0])
        _put(outs[2], cs, vals[1])

    t_out = pl.BlockSpec((tm, tg), lambda i, j, k: (i, j))
    m_act, y_a, y_b = _mm(
        "mm_branches", (s // tm, d // tg, 1), [a_act, g_a, att, g_b, z, z],
        [pl.BlockSpec((tm, w_sgu), lambda i, j, k: (i, 0)), pl.BlockSpec((w_sgu, tg), lambda i, j, k: (0, j)),
         pl.BlockSpec((tm, w_att), lambda i, j, k: (i, 0)), pl.BlockSpec((w_att, tg), lambda i, j, k: (0, j)),
         pl.BlockSpec((tm, tg), lambda i, j, k: (i, ga0 + j)), pl.BlockSpec((tm, tg), lambda i, j, k: (i, gb0 + j))],
        [jax.ShapeDtypeStruct((s, d), BF16)] * 3,
        [t_out, t_out, t_out], [(0, 1, NN, 0), (2, 3, NN, 1)], 2, (tm, tg), 1, ep_gate,
        _mm_vmem([((tm, w_sgu), BF16, 4), ((w_sgu, tg), BF16, 4), ((tm, tg), F32, 12)]),    )

    tn = _tile(d, (1024, 512))

    def ep_residual(ins, vals, outs, cs):
        _put(outs[0], cs, ins[2][:, cs] + vals[0])

    gate_fwd = relay_end_async(gate_relay, m_act)
    x2 = _mm(
        "mm_wo", (s // tm, d // tn, 1), [m_act, g_o, x2d],
        [pl.BlockSpec((tm, d), lambda i, j, k: (i, 0)), pl.BlockSpec((d, tn), lambda i, j, k: (0, j)),
         pl.BlockSpec((tm, tn), lambda i, j, k: (i, j))],
        [jax.ShapeDtypeStruct((s, d), F32)], [pl.BlockSpec((tm, tn), lambda i, j, k: (i, j))],
        [(0, 1, NN, 0)], 1, (tm, tn), 1, ep_residual,
        _mm_vmem([((tm, d), BF16, 2), ((d, tn), BF16, 2), ((tm, tn), F32, 5)]),
        after=(gate_fwd[4][0],),
    )[0]
    (g_gate,) = forwarded(gate_fwd, x2)
    up_fwd = relay_end_async(up_relay, g_gate)
    h2 = _rms_fwd("rms_ffn", x2, norm_ffn, after=(up_fwd[4][0],))
    (g_up,) = forwarded(up_fwd, h2)

    tf = _tile(d_ff, (512,))

    def ep_swiglu(ins, vals, outs, cs):
        gt, up = vals
        _put(outs[0], cs, gt)
        _put(outs[1], cs, up)
        _put(outs[2], cs, (gt * _sigmoid(gt)) * up)

    f_out = pl.BlockSpec((tm, tf), lambda i, j, k: (i, j))
    gt, up, f_act = _mm(
        "mm_gate_up", (s // tm, d_ff // tf, 1), [h2, g_gate, g_up],
        [pl.BlockSpec((tm, d), lambda i, j, k: (i, 0)), pl.BlockSpec((d, tf), lambda i, j, k: (0, j)),
         pl.BlockSpec((d, tf), lambda i, j, k: (0, j))],
        [jax.ShapeDtypeStruct((s, d_ff), BF16)] * 3,
        [f_out, f_out, f_out], [(0, 1, NN, 0), (0, 2, NN, 1)], 2, (tm, tf), 1, ep_swiglu,
        _mm_vmem([((tm, d), BF16, 2), ((d, tf), BF16, 4), ((tm, tf), F32, 8)]),    )
    (g_down,) = _gather_forward(
        "gather_fwd_ffn_out",
        _gather_wait("gather_wait_ffn_out", [full_down], [kinds[W_DOWN]], [0], down_send, down_recv, f_act),
        [kinds[W_DOWN]],
    )

    tkf = _tile(d_ff, (1408, 1024, 512))
    tml, tnl = _tile(s, (512, 256, 128)), _tile(d, (512,))
    kh = d_ff // 2
    x3 = _mm(
        "mm_down", (s // tml, d // tnl, 1), [f_act, f_act, g_down, g_down, x2],
        [pl.BlockSpec((tml, kh), lambda i, j, k: (i, 0)), pl.BlockSpec((tml, kh), lambda i, j, k: (i, 1)),
         pl.BlockSpec((kh, tnl), lambda i, j, k: (0, j)), pl.BlockSpec((kh, tnl), lambda i, j, k: (1, j)),
         pl.BlockSpec((tml, tnl), lambda i, j, k: (i, j))],
        [jax.ShapeDtypeStruct((s, d), F32)], [pl.BlockSpec((tml, tnl), lambda i, j, k: (i, j))],
        [(0, 2, NN, 0), (1, 3, NN, 0)], 1, (tml, tnl), 1, lambda ins, vals, outs, cs: _put(outs[0], cs, ins[4][:, cs] + vals[0]),
        _mm_vmem([((tml, d_ff), BF16, 2), ((d_ff, tnl), BF16, 2), ((tml, tnl), F32, 6)]),    )[0]

    dx3, dx3b, dg_final, loss_part = _head(x3, norm_final.reshape(1, d), tgt)

    def reduce_a(tag, ids, grads):
        ks = [kinds[i] for i in ids]
        lands = [lax.empty((g.shape[0] // 2, g.shape[1]) if k == "col" else (g.shape[0], g.shape[1] // 2), BF16)
                 for g, k in zip(grads, ks)]
        send, recv, bufs = _split_start("pair_send_" + tag, list(grads) + lands, len(ids), _pair_exchange_copies(ks), sibling_only=True)
        return {"tag": tag, "ids": ids, "ks": ks, "pair": (send, recv, bufs), "token": bufs[0]}

    def reduce_b(st, after):
        tag, ids, ks = st["tag"], st["ids"], st["ks"]
        send, recv, bufs = st["pair"]
        bufs = _split_wait("pair_wait_" + tag, bufs, send, recv, _pair_exchange_copies(ks), after)
        grads, from_sib = bufs[: len(ids)], bufs[len(ids) :]
        halves = [_pair_add("pair_add_" + names[i], c_idx, g, r, k) for i, g, r, k in zip(ids, grads, from_sib, ks)]
        st["chip"] = _chip_send_start("chip_send_" + tag, halves, ks)
        st["token"] = st["chip"][2][0]

    def reduce_c(st, after):
        tag, ids, ks = st["tag"], st["ids"], st["ks"]
        send, recv, halves, lands = st["chip"]
        halves, lands = _chip_send_wait("chip_wait_" + tag, halves, lands, ks, send, recv, after)
        pieces = [_chip_sum("chip_sum_" + names[i], qc_idx, h, r, k) for i, h, r, k in zip(ids, halves, lands, ks)]
        st["share"] = _split_start("share_send_" + tag, pieces, len(ids), _pair_share_copies(ks, False), sibling_only=True)
        st["token"] = st["share"][2][0]

    def reduce_d(st, after):
        send, recv, bufs = st["share"]
        return _split_wait("share_wait_" + st["tag"], bufs, send, recv, _pair_share_copies(st["ks"], True), after)

    grads_big, upd = [None] * 7, [None] * 7

    def finish(st, after):
        shared = reduce_d(st, after)
        after = shared[0]
        for i, g in zip(st["ids"], shared):
            upd[i] = _adamw("adamw_" + names[i], big_w[i], g, big_m[i], big_v[i], after=(after,))
            grads_big[i] = upd[i][3]
            after = upd[i][0]
        return after

    def ep_swiglu_bwd(ins, vals, outs, cs):
        df = vals[0]
        gtv, upv = ins[2][:, cs].astype(F32), ins[3][:, cs].astype(F32)
        sg = _sigmoid(gtv)
        _put(outs[0], cs, df * upv * (sg + gtv * sg * (1.0 - sg)))
        _put(outs[1], cs, df * (gtv * sg))

    dgt, dup = _mm(
        "mm_dswiglu", (s // tm, d_ff // tf, 1), [dx3b, g_down, gt, up],
        [pl.BlockSpec((tm, d), lambda i, j, k: (i, 0)), pl.BlockSpec((tf, d), lambda i, j, k: (j, 0)), f_out, f_out],
        [jax.ShapeDtypeStruct((s, d_ff), BF16), jax.ShapeDtypeStruct((s, d_ff), BF16)], [f_out, f_out],
        [(0, 1, NT, 0)], 1, (tm, tf), 1, ep_swiglu_bwd,
        _mm_vmem([((tm, d), BF16, 2), ((tf, d), BF16, 2), ((tm, tf), F32, 8)]),    )

    def ep_store(ins, vals, outs, cs):
        for o, v in zip(outs, vals):
            _put(o, cs, v)

    twn = _tile(d, (1024, 512))
    gw_down = _mm(
        "mm_gw_down", (d_ff // tkf, d // twn, 1), [f_act, dx3b],
        [pl.BlockSpec((s, tkf), lambda i, j, k: (0, i)), pl.BlockSpec((s, twn), lambda i, j, k: (0, j))],
        [jax.ShapeDtypeStruct((d_ff, d), BF16)], [pl.BlockSpec((tkf, twn), lambda i, j, k: (i, j))],
        [(0, 1, TN, 0)], 1, (tkf, twn), 1, ep_store,
        _mm_vmem([((s, tkf), BF16, 3), ((s, twn), BF16, 2), ((tkf, twn), F32, 3)]),
    )[0]
    red_down = reduce_a("down", [W_DOWN], [gw_down])

    tn2 = _tile(d, (256,))
    dh2_specs = [pl.BlockSpec((tm, d_ff), lambda i, j, k: (i, 0)), pl.BlockSpec((tn2, d_ff), lambda i, j, k: (j, 0))]
    dh2_tile = pl.BlockSpec((tm, tn2), lambda i, j, k: (i, j))
    dh2_vmem = _mm_vmem([((tm, d_ff), BF16, 2), ((tn2, d_ff), BF16, 2), ((tm, tn2), F32, 7)])
    dh2 = _mm(
        "mm_dh2_gate", (s // tm, d // tn2, 1), [dgt, g_gate], dh2_specs,
        [jax.ShapeDtypeStruct((s, d), F32)], [dh2_tile], [(0, 1, NT, 0)], 1, (tm, tn2), 1, ep_store, dh2_vmem,
        after=(red_down["token"],),
    )[0]
    dh2 = _mm(
        "mm_dh2_up", (s // tm, d // tn2, 1), [dup, g_up, dh2], dh2_specs + [dh2_tile],
        [jax.ShapeDtypeStruct((s, d), F32)], [dh2_tile], [(0, 1, NT, 0)], 1, (tm, tn2), 1, ep_residual, dh2_vmem,
    )[0]
    reduce_b(red_down, dh2)

    twr = _tile(d, (1024, 512))
    w_tile = pl.BlockSpec((twr, tf), lambda i, j, k: (i, j))
    gw_gate, gw_up = _mm(
        "mm_gw_gate_up", (d // twr, d_ff // tf, 1), [h2, dgt, dup],
        [pl.BlockSpec((s, twr), lambda i, j, k: (0, i)), pl.BlockSpec((s, tf), lambda i, j, k: (0, j)),
         pl.BlockSpec((s, tf), lambda i, j, k: (0, j))],
        [jax.ShapeDtypeStruct((d, d_ff), BF16), jax.ShapeDtypeStruct((d, d_ff), BF16)], [w_tile, w_tile],
        [(0, 1, TN, 0), (0, 2, TN, 1)], 2, (twr, tf), 1, ep_store,
        _mm_vmem([((s, twr), BF16, 3), ((s, tf), BF16, 4), ((twr, tf), F32, 6)]),
        after=(red_down["token"],),
    )
    red_ffn = reduce_a("ffn_in", [W_GATE, W_UP], [gw_gate, gw_up])

    dx2, dx2b, dg_ffn = _rms_bwd("rms_ffn_bwd", x2, norm_ffn, dh2, dx3, after=(red_ffn["token"],))

    nj = d // tg

    def lo(j):
        return jnp.minimum(j, nj - 1)

    def gate_bwd_body(dx_ref, wo_ref, ga_ref, gb_ref, ya_ref, yb_ref, dya_ref, dyb_ref, dz_ref, keep):
        j = pl.program_id(1)

        @pl.when(j < nj)
        def _():
            dm = lax.dot_general(dx_ref[...], wo_ref[...], NT, preferred_element_type=F32)
            sa, sb = _sigmoid(ga_ref[...].astype(F32)), _sigmoid(gb_ref[...].astype(F32))
            dya_ref[...] = (dm * sa).astype(BF16)
            dyb_ref[...] = (dm * sb).astype(BF16)
            dz_ref[...] = (dm * ya_ref[...].astype(F32) * (sa * (1.0 - sa))).astype(BF16)
            keep[lo(j)] = (dm * yb_ref[...].astype(F32) * (sb * (1.0 - sb))).astype(BF16)

        @pl.when(j >= nj)
        def _():
            dz_ref[...] = keep[jnp.maximum(j - nj, 0)]

    t_lo = pl.BlockSpec((tm, tg), lambda i, j: (i, lo(j)))
    dya, dyb, dz = _pallas(
        gate_bwd_body,
        name="mm_dgate",
        grid=(s // tm, 2 * nj),
        in_specs=[
            pl.BlockSpec((tm, d), lambda i, j: (i, 0)),
            pl.BlockSpec((tg, d), lambda i, j: (lo(j), 0)),
            pl.BlockSpec((tm, tg), lambda i, j: (i, ga0 + lo(j))),
            pl.BlockSpec((tm, tg), lambda i, j: (i, gb0 + lo(j))),
            t_lo,
            t_lo,
        ],
        out_specs=[t_lo, t_lo, pl.BlockSpec((tm, tg), lambda i, j: (i, ga0 + j))],
        out_shape=[jax.ShapeDtypeStruct((s, d), BF16), jax.ShapeDtypeStruct((s, d), BF16), jax.ShapeDtypeStruct((s, n_in), BF16)],
        scratch_shapes=[pltpu.VMEM((nj, tm, tg), BF16)],
        compiler_params=_params(_mm_vmem([((tm, d), BF16, 2), ((tg, d), BF16, 2), ((tm, tg), F32, 14), ((nj, tm, tg), BF16, 1)])),
    )(dx2b, g_o, z, z, y_a, y_b)
    reduce_b(red_ffn, dya)
    reduce_c(red_down, red_ffn["token"])

    gw_o = _mm(
        "mm_gw_o", (d // twr, d // twn, 1), [m_act, dx2b],
        [pl.BlockSpec((s, twr), lambda i, j, k: (0, i)), pl.BlockSpec((s, twn), lambda i, j, k: (0, j))],
        [jax.ShapeDtypeStruct((d, d), BF16)], [pl.BlockSpec((twr, twn), lambda i, j, k: (i, j))],
        [(0, 1, TN, 0)], 1, (twr, twn), 1, ep_store,
        _mm_vmem([((s, twr), BF16, 3), ((s, twn), BF16, 2), ((twr, twn), F32, 3)]),
        after=(red_down["token"],),
    )[0]
    after_down = finish(red_down, gw_o)

    tb = _tile(w_sgu, (1024, 512))
    b_out = pl.BlockSpec((tm, tb), lambda i, j, k: (i, j))

    da, datt = _mm(
        "mm_dbranches", (s // tm, w_sgu // tb, 1), [dya, g_a, dyb, g_b],
        [pl.BlockSpec((tm, d), lambda i, j, k: (i, 0)), pl.BlockSpec((tb, d), lambda i, j, k: (j, 0)),
         pl.BlockSpec((tm, d), lambda i, j, k: (i, 0)), pl.BlockSpec((tb, d), lambda i, j, k: (j, 0))],
        [jax.ShapeDtypeStruct((s, w_sgu), BF16), jax.ShapeDtypeStruct((s, w_att), BF16)], [b_out, b_out],
        [(0, 1, NT, 0), (2, 3, NT, 1)], 2, (tm, tb), 1, ep_store,
        _mm_vmem([((tm, d), BF16, 4), ((tb, d), BF16, 4), ((tm, tb), F32, 6)]),
        after=(after_down,),
    )

    wb_tile = pl.BlockSpec((tb, twn), lambda i, j, k: (i, j))
    gw_a, gw_b = _mm(
        "mm_gw_branches", (w_sgu // tb, d // twn, 1), [a_act, dya, att, dyb],
        [pl.BlockSpec((s, tb), lambda i, j, k: (0, i)), pl.BlockSpec((s, twn), lambda i, j, k: (0, j)),
         pl.BlockSpec((s, tb), lambda i, j, k: (0, i)), pl.BlockSpec((s, twn), lambda i, j, k: (0, j))],
        [jax.ShapeDtypeStruct((w_sgu, d), BF16), jax.ShapeDtypeStruct((w_att, d), BF16)], [wb_tile, wb_tile],
        [(0, 1, TN, 0), (2, 3, TN, 1)], 2, (tb, twn), 1, ep_store,
        _mm_vmem([((s, tb), BF16, 5), ((s, twn), BF16, 4), ((tb, twn), F32, 6)]),
        after=(da,),
    )
    red_mix = reduce_a("mix", [W_O, W_A, W_B], [gw_o, gw_a, gw_b])

    dz, dws, dbs, dgain = _sgu_bwd(z, da, sgu_v_gain, ws_b, wst_b, b_col, w_sgu, dz, after=(red_mix["token"],))
    dz, dk_pad, dv_pad, dbias_tab, dsink = _attn_bwd(sink, z, k_pad, v_pad, bias_tab, datt, dz, grp, q_blk0)
    dz = _dkv_to_dz(dk_pad, dv_pad, dz, off_k // (2 * w_kv))
    drel = _relbias_bwd(dbias_tab, bucket)
    reduce_b(red_mix, dz)
    reduce_c(red_ffn, red_mix["token"])

    small_w = [norm_mix, sgu_v_gain, sgu_w_s, sgu_b_s, attn_sink, rel_bias, norm_ffn, norm_final]
    small_m = [m_norm_mix, m_sgu_v_gain, m_sgu_w_s, m_sgu_b_s, m_attn_sink, m_rel_bias, m_norm_ffn, m_norm_final]
    small_v = [v_norm_mix, v_sgu_v_gain, v_sgu_w_s, v_sgu_b_s, v_attn_sink, v_rel_bias, v_norm_ffn, v_norm_final]
    small_shapes = [w.shape for w in small_w]
    early = [dgain, dws, dbs, dsink[:, 0], drel[:, :REL_BUCKETS].T, dg_ffn, dg_final]
    p_early = _pack([g.reshape(shp) for g, shp in zip(early, small_shapes[1:])] + [loss_part[0, :1]])
    land = jnp.zeros((2 * N_CHIPS,) + p_early.shape, F32)
    sm_send, sm_recv, (p_early, land) = _split_start("small_send", [p_early, land], 2 * N_CHIPS - 1, _small_exchange_copies(False))

    tzn = _tile(n_in, (768, 640, 512))
    gw_in = _mm(
        "mm_gw_in", (d // twr, n_in // tzn, 1), [h1, dz],
        [pl.BlockSpec((s, twr), lambda i, j, k: (0, i)), pl.BlockSpec((s, tzn), lambda i, j, k: (0, j))],
        [jax.ShapeDtypeStruct((d, n_in), BF16)], [pl.BlockSpec((twr, tzn), lambda i, j, k: (i, j))],
        [(0, 1, TN, 0)], 1, (twr, tzn), 1, ep_store,
        _mm_vmem([((s, twr), BF16, 3), ((s, tzn), BF16, 2), ((twr, tzn), F32, 3)]),
        after=(red_ffn["token"], p_early),
    )[0]
    red_in = reduce_a("w_in", [W_IN], [gw_in])

    reduce_c(red_mix, red_in["token"])
    reduce_b(red_in, finish(red_mix, red_in["token"]))

    dh1 = _mm(
        "mm_dh1", (s // tm, d // tn2, 1), [dz, g_in],
        [pl.BlockSpec((tm, n_in), lambda i, j, k: (i, 0)), pl.BlockSpec((tn2, n_in), lambda i, j, k: (j, 0))],
        [jax.ShapeDtypeStruct((s, d), F32)], [pl.BlockSpec((tm, tn2), lambda i, j, k: (i, j))],
        [(0, 1, NT, 0)], 1, (tm, tn2), 1, ep_store,
        _mm_vmem([((tm, n_in), BF16, 2), ((tn2, n_in), BF16, 2), ((tm, tn2), F32, 5)]),
        after=(red_in["token"],),
    )[0]

    grad_x, _, dg_mix = _rms_bwd("rms_mix_bwd", x2d, norm_mix, dh1, dx2)

    p_mix = _pack([dg_mix.reshape(small_shapes[0])])
    land_mix = jnp.zeros((2 * N_CHIPS,) + p_mix.shape, F32)
    mx_send, mx_recv, (p_mix, land_mix) = _split_start("mix_send", [p_mix, land_mix], 2 * N_CHIPS - 1, _small_exchange_copies(False))

    reduce_c(red_in, finish(red_ffn, p_mix))
    p_early, land = _split_wait("small_wait", [p_early, land], sm_send, sm_recv, _small_exchange_copies(True), red_in["token"])
    p_mix, land_mix = _split_wait("mix_wait", [p_mix, land_mix], mx_send, mx_recv, _small_exchange_copies(True), p_early)
    me_idx = 2 * q_idx + c_idx
    packed_g = jnp.concatenate([_small_sum("mix_sum", me_idx, p_mix, land_mix), _small_sum("small_sum", me_idx, p_early, land)], axis=0)
    g_small = _unpack(packed_g, small_shapes + [(1,)])
    loss = g_small[-1].reshape(())
    g_small = g_small[:-1]
    zero1 = jnp.zeros((1,), F32)
    pw, pg, pm, pv = _pack(small_w + [zero1]), _pack(g_small + [zero1]), _pack(small_m + [zero1]), _pack(small_v + [zero1])
    small_upd = _adamw("adamw_small", pw, pg, pm, pv)
    d_small, nm_small, nv_small = [_unpack(a, small_shapes) for a in small_upd[:3]]
    finish(red_in, small_upd[0])

    small_names = ["norm_mix", "sgu_v_gain", "sgu_w_s", "sgu_b_s", "attn_sink", "rel_bias", "norm_ffn", "norm_final"]
    table = {}
    for i, n in enumerate(names):
        table[n] = (grads_big[i][None], upd[i][0][None], upd[i][1][None], upd[i][2][None])
    for i, n in enumerate(small_names):
        table[n] = (g_small[i], d_small[i], nm_small[i], nv_small[i])
    order = ["w_in", "norm_mix", "sgu_v_gain", "sgu_w_s", "sgu_b_s", "w_a", "attn_sink", "rel_bias", "w_b", "w_o", "norm_ffn",
             "w_gate", "w_up", "w_down", "norm_final"]
    outs = [loss, grad_x.reshape(1, s, d)]
    for part in range(4):
        outs += [table[n][part] for n in order]
    return tuple(outs)
```

```python
import math

import jax
import jax.numpy as jnp
import numpy as np
from jax import lax
from jax.experimental import pallas as pl
from jax.experimental.pallas import tpu as pltpu

F32 = jnp.float32
BF16 = jnp.bfloat16
I32 = jnp.int32
MESH = pl.DeviceIdType.MESH

EPS = 1e-6
NEG = -1e30
BLK = 128
HEAD_DIM = 128
N_KV_HEADS = 2
REL_BUCKETS = 32
REL_MAX_DIST = 128
N_CHIPS = 4
ADAM_LR, ADAM_B1, ADAM_B2, ADAM_EPS, ADAM_WD, ADAM_STEP = 0.001, 0.9, 0.999, 1e-08, 0.01, 10

LANES = 128
VMEM_CAP = 60 * 1024 * 1024

NN = (((1,), (0,)), ((), ()))
NT = (((1,), (1,)), ((), ()))
TN = (((0,), (0,)), ((), ()))
ANY = pl.BlockSpec(memory_space=pl.ANY)
HBM_SPEC = pl.BlockSpec(memory_space=pltpu.HBM)
SEM_SPEC = pl.BlockSpec(memory_space=pltpu.SEMAPHORE)
EFFECT = pltpu.SideEffectType.DATAFLOW_SIDE_EFFECTING


def _tile(n, cands):
    for t in cands:
        if n % t == 0:
            return t
    return n


PIN_BYTES = 64 * 1024


def _pin_hbm(a):
    big = hasattr(a, "dtype") and jnp.issubdtype(a.dtype, jnp.floating) and _nbytes(a.shape, a.dtype) >= PIN_BYTES
    return pltpu.with_memory_space_constraint(a, pltpu.HBM) if big else a


def _pallas(body, *, out_shape, **kw):
    def pin(o):
        big = isinstance(o, jax.ShapeDtypeStruct) and jnp.issubdtype(o.dtype, jnp.floating) and _nbytes(o.shape, o.dtype) >= PIN_BYTES
        return pltpu.HBM(o.shape, o.dtype) if big else o

    shapes = type(out_shape)(pin(o) for o in out_shape) if isinstance(out_shape, (list, tuple)) else pin(out_shape)
    call = pl.pallas_call(body, out_shape=shapes, **kw)
    return lambda *args: call(*[_pin_hbm(a) for a in args])


def _params(vmem_bytes=None, **kw):
    if vmem_bytes is not None:
        kw["vmem_limit_bytes"] = int(min(max(vmem_bytes, 32 * 1024 * 1024), VMEM_CAP))
    return pltpu.CompilerParams(**kw)


def _nbytes(shape, dtype):
    return int(np.prod(shape)) * jnp.dtype(dtype).itemsize


def _sigmoid(x):
    return 1.0 / (1.0 + jnp.exp(-x))


_GC = 0.7978845608028654
_GA = 0.044715


def _gelu(x):
    return 0.5 * x * (1.0 + jnp.tanh(_GC * (x + _GA * (x * x * x))))


def _gelu_grad(x):
    t = jnp.tanh(_GC * (x + _GA * (x * x * x)))
    return 0.5 * (1.0 + t) + 0.5 * x * (1.0 - t * t) * (_GC * (1.0 + 3.0 * _GA * (x * x)))


def _bf(v):
    return v if v.dtype == BF16 else v.astype(BF16)


def _mm(name, grid, ins, in_specs, out_shape, out_specs, pairs, n_acc, tile, nk, epilogue, vmem_bytes, after=()):
    assert nk == 1
    n_in, n_out = len(ins) + len(after), len(out_shape)

    def body(*refs):
        in_refs, out_refs = refs[:n_in], refs[n_in : n_in + n_out]
        vals = [None] * n_acc
        for a_i, b_i, dn, acc_i in pairs:
            d = lax.dot_general(_bf(in_refs[a_i][...]), _bf(in_refs[b_i][...]), dn, preferred_element_type=F32)
            vals[acc_i] = d if vals[acc_i] is None else vals[acc_i] + d
        epilogue(in_refs, vals, out_refs, slice(None))

    return _pallas(
        body,
        name=name,
        grid=grid,
        in_specs=list(in_specs) + [ANY] * len(after),
        out_specs=out_specs,
        out_shape=out_shape,
        compiler_params=_params(vmem_bytes),
    )(*ins, *after)


def _put(ref, cs, v):
    ref[:, cs] = v.astype(ref.dtype)


def _mm_vmem(tiles):
    return sum(_nbytes(s, d) * c for s, d, c in tiles) + 4 * 1024 * 1024


def _rows8(v):
    r, d = v.shape
    return v.reshape(r // 8, 8, d).sum(axis=0)


def _rms_fwd(name, x, g, after=()):
    s, d = x.shape
    tm = _tile(s, (256, 128))

    def body(x_ref, g_ref, *rest):
        h_ref = rest[-1]
        xv = x_ref[...]
        r = lax.rsqrt(jnp.mean(xv * xv, axis=-1, keepdims=True) + EPS)
        h_ref[...] = ((xv * r) * g_ref[...]).astype(BF16)

    return _pallas(
        body,
        name=name,
        grid=(s // tm,),
        in_specs=[pl.BlockSpec((tm, d), lambda i: (i, 0)), pl.BlockSpec((1, d), lambda i: (0, 0))] + [ANY] * len(after),
        out_specs=pl.BlockSpec((tm, d), lambda i: (i, 0)),
        out_shape=jax.ShapeDtypeStruct((s, d), BF16),
    )(x, g, *after)


def _rms_bwd(name, x, g, dh, dres, after=()):
    s, d = x.shape
    tm = _tile(s, (256, 128))
    n = s // tm
    n_after = len(after)

    def body(x_ref, g_ref, dh_ref, dres_ref, *rest):
        dx_ref, dxb_ref, dg_ref, acc_ref = rest[n_after:]
        i = pl.program_id(0)
        xv = x_ref[...]
        r = lax.rsqrt(jnp.mean(xv * xv, axis=-1, keepdims=True) + EPS)
        xh = xv * r
        dhv = dh_ref[...]
        dxh = dhv * g_ref[...]
        dx = r * (dxh - xh * jnp.mean(dxh * xh, axis=-1, keepdims=True)) + dres_ref[...]
        dx_ref[...] = dx
        dxb_ref[...] = dx.astype(BF16)
        part = _rows8(dhv * xh)

        @pl.when(i == 0)
        def _():
            acc_ref[...] = part

        @pl.when(i > 0)
        def _():
            acc_ref[...] += part

        @pl.when(i == n - 1)
        def _():
            dg_ref[...] = jnp.sum(acc_ref[...], axis=0, keepdims=True)

    row = pl.BlockSpec((tm, d), lambda i: (i, 0))
    vec = pl.BlockSpec((1, d), lambda i: (0, 0))
    return _pallas(
        body,
        name=name,
        grid=(n,),
        in_specs=[row, vec, row, row] + [ANY] * n_after,
        out_specs=[row, row, vec],
        out_shape=[jax.ShapeDtypeStruct((s, d), F32), jax.ShapeDtypeStruct((s, d), BF16), jax.ShapeDtypeStruct((1, d), F32)],
        scratch_shapes=[pltpu.VMEM((8, d), F32)],
    )(x, g, dh, dres, *after)


def _head(x3, g, target):
    s, d = x3.shape
    tm = _tile(s, (256, 128))
    n = s // tm

    def body(x_ref, g_ref, t_ref, dx_ref, dxb_ref, dg_ref, loss_ref, acc_g, acc_l):
        i = pl.program_id(0)
        xv = x_ref[...]
        gv = g_ref[...]
        r = lax.rsqrt(jnp.mean(xv * xv, axis=-1, keepdims=True) + EPS)
        xh = xv * r
        e = xh * gv - t_ref[...]
        dy = e * (1.0 / d)
        dxh = dy * gv
        dx = r * (dxh - xh * jnp.mean(dxh * xh, axis=-1, keepdims=True))
        dx_ref[...] = dx
        dxb_ref[...] = dx.astype(BF16)
        pg = _rows8(dy * xh)
        plo = _rows8(e * e)

        @pl.when(i == 0)
        def _():
            acc_g[...] = pg
            acc_l[...] = plo

        @pl.when(i > 0)
        def _():
            acc_g[...] += pg
            acc_l[...] += plo

        @pl.when(i == n - 1)
        def _():
            dg_ref[...] = jnp.sum(acc_g[...], axis=0, keepdims=True)
            loss_ref[...] = jnp.full((1, LANES), (0.5 / d) * jnp.sum(acc_l[...]), F32)

    row = pl.BlockSpec((tm, d), lambda i: (i, 0))
    vec = pl.BlockSpec((1, d), lambda i: (0, 0))
    return _pallas(
        body,
        name="head",
        grid=(n,),
        in_specs=[row, vec, row],
        out_specs=[row, row, vec, pl.BlockSpec((1, LANES), lambda i: (0, 0))],
        out_shape=[
            jax.ShapeDtypeStruct((s, d), F32),
            jax.ShapeDtypeStruct((s, d), BF16),
            jax.ShapeDtypeStruct((1, d), F32),
            jax.ShapeDtypeStruct((1, LANES), F32),
        ],
        scratch_shapes=[pltpu.VMEM((8, d), F32), pltpu.VMEM((8, d), F32)],
    )(x3, g, target)


def _sgu_fwd(z, gain, ws_b, b_col, w_sgu, after=()):
    s = z.shape[0]
    groups = ws_b.shape[0]

    def body(zu_ref, zv_ref, gain_ref, ws_ref, b_ref, *rest):
        a_ref = rest[-1]
        vv = _gelu(zv_ref[...].astype(F32))
        r = lax.rsqrt(jnp.mean(vv * vv, axis=-1, keepdims=True) + EPS)
        vn = ((vv * r) * gain_ref[...]).astype(BF16)
        u = _gelu(zu_ref[...].astype(F32))
        for g in range(groups):
            sl = slice(g * BLK, (g + 1) * BLK)
            mixed = jnp.dot(ws_ref[g], vn[:, sl], preferred_element_type=F32) + b_ref[g]
            a_ref[:, sl] = (u[:, sl] * mixed).astype(BF16)

    return _pallas(
        body,
        name="sgu_fwd",
        grid=(s // BLK,),
        in_specs=[
            pl.BlockSpec((BLK, w_sgu), lambda c: (c, 0)),
            pl.BlockSpec((BLK, w_sgu), lambda c: (c, 1)),
            pl.BlockSpec((1, w_sgu), lambda c: (0, 0)),
            pl.BlockSpec((groups, BLK, BLK), lambda c: (0, 0, 0)),
            pl.BlockSpec((groups, BLK, 1), lambda c: (0, 0, 0)),
        ]
        + [ANY] * len(after),
        out_specs=pl.BlockSpec((BLK, w_sgu), lambda c: (c, 0)),
        out_shape=jax.ShapeDtypeStruct((s, w_sgu), BF16),
    )(z, z, gain, ws_b, b_col, *after)


def _sgu_bwd(z, da, gain, ws_b, wst_b, b_col, w_sgu, dz, after=()):
    s = z.shape[0]
    groups = ws_b.shape[0]
    n = s // BLK
    n_skip = 1 + len(after)

    def body(zu_ref, zv_ref, da_ref, gain_ref, ws_ref, wst_ref, b_ref, *rest):
        dz_ref, dws_ref, dbs_ref, dgain_ref, acc_gain, vv_s, gv_s, dxh_s = rest[n_skip:]
        c = pl.program_id(0)
        cols = [slice(g * BLK, (g + 1) * BLK) for g in range(groups)]

        ss = jnp.zeros((BLK, 1), F32)
        for sl in cols:
            zv = zv_ref[:, sl].astype(F32)
            vv = _gelu(zv)
            vv_s[:, sl] = vv
            gv_s[:, sl] = _gelu_grad(zv)
            ss = ss + jnp.sum(vv * vv, axis=-1, keepdims=True)
        r = lax.rsqrt(ss * (1.0 / w_sgu) + EPS)

        dot_dx = jnp.zeros((BLK, 1), F32)
        for g, sl in enumerate(cols):
            gain_g = gain_ref[:, sl]
            xh = vv_s[:, sl] * r
            vn = (xh * gain_g).astype(BF16)
            zu = zu_ref[:, sl].astype(F32)
            dav = da_ref[:, sl].astype(F32)
            dmix = dav * _gelu(zu)
            dmix_b = dmix.astype(BF16)
            mixed = jnp.dot(ws_ref[g], vn, preferred_element_type=F32) + b_ref[g]
            dz_ref[:, sl] = (dav * mixed * _gelu_grad(zu)).astype(BF16)
            dvn = jnp.dot(wst_ref[g], dmix_b, preferred_element_type=F32)
            dws_g = lax.dot_general(dmix_b, vn, NT, preferred_element_type=F32)
            dbs_g = jnp.sum(dmix, axis=1, keepdims=True)
            pg = _rows8(dvn * xh)

            @pl.when(c == 0)
            def _():
                dws_ref[g] = dws_g
                dbs_ref[g] = dbs_g
                acc_gain[:, sl] = pg

            @pl.when(c > 0)
            def _():
                dws_ref[g] += dws_g
                dbs_ref[g] += dbs_g
                acc_gain[:, sl] += pg

            dxh = dvn * gain_g
            dxh_s[:, sl] = dxh
            dot_dx = dot_dx + jnp.sum(dxh * xh, axis=-1, keepdims=True)

        mean_dx = dot_dx * (1.0 / w_sgu)
        for g, sl in enumerate(cols):
            dvv = r * (dxh_s[:, sl] - (vv_s[:, sl] * r) * mean_dx)
            dz_ref[:, w_sgu + g * BLK : w_sgu + (g + 1) * BLK] = (dvv * gv_s[:, sl]).astype(BF16)

        @pl.when(c == n - 1)
        def _():
            dgain_ref[...] = jnp.sum(acc_gain[...], axis=0, keepdims=True)

    full3 = pl.BlockSpec((groups, BLK, BLK), lambda c: (0, 0, 0))
    col3 = pl.BlockSpec((groups, BLK, 1), lambda c: (0, 0, 0))
    vec = pl.BlockSpec((1, w_sgu), lambda c: (0, 0))
    return _pallas(
        body,
        name="sgu_bwd",
        grid=(n,),
        in_specs=[
            pl.BlockSpec((BLK, w_sgu), lambda c: (c, 0)),
            pl.BlockSpec((BLK, w_sgu), lambda c: (c, 1)),
            pl.BlockSpec((BLK, w_sgu), lambda c: (c, 0)),
            vec,
            full3,
            full3,
            col3,
            ANY,
        ]
        + [ANY] * len(after),
        out_specs=[pl.BlockSpec((BLK, 2 * w_sgu), lambda c: (c, 0)), full3, col3, vec],
        out_shape=[
            jax.ShapeDtypeStruct(dz.shape, BF16),
            jax.ShapeDtypeStruct((groups, BLK, BLK), F32),
            jax.ShapeDtypeStruct((groups, BLK, 1), F32),
            jax.ShapeDtypeStruct((1, w_sgu), F32),
        ],
        scratch_shapes=[pltpu.VMEM((8, w_sgu), F32)] + [pltpu.VMEM((BLK, w_sgu), F32)] * 3,
        input_output_aliases={7: 0},
    )(z, z, da, gain, ws_b, wst_b, b_col, dz, *after)


def _attn_softmax(sink_ref, q_ref, k_ref, v_ref, bias_ref, s_len, grp):
    kv = pl.program_id(0)
    n = pl.program_id(1)
    start = pl.multiple_of(n * BLK, BLK)
    kb = k_ref[pl.ds(start, 3 * BLK), :]
    vb = v_ref[pl.ds(start, 3 * BLK), :]
    qv = q_ref[...]
    qs = jnp.concatenate([qv[:, g * HEAD_DIM : (g + 1) * HEAD_DIM] for g in range(grp)], axis=0).astype(BF16)
    sc = lax.dot_general(qs, kb, NT, preferred_element_type=F32) * (HEAD_DIM**-0.5)
    sc = sc + bias_ref[...].reshape(grp * BLK, 3 * BLK)
    kpos = start + lax.broadcasted_iota(I32, (1, 3 * BLK), 1) - BLK
    sc = jnp.where((kpos >= 0) & (kpos < s_len), sc, NEG)
    sink = jnp.concatenate([jnp.full((BLK, 1), sink_ref[kv * grp + g], F32) for g in range(grp)], axis=0)
    m = jnp.maximum(jnp.max(sc, axis=-1, keepdims=True), sink)
    p = jnp.exp(sc - m)
    esink = jnp.exp(sink - m)
    den = jnp.sum(p, axis=-1, keepdims=True) + esink
    return start, qs, kb, vb, p / den, esink / den


def _attn_specs(s, grp, q_blk0):
    qw = grp * HEAD_DIM
    return [
        pl.BlockSpec(memory_space=pltpu.SMEM),
        pl.BlockSpec((BLK, qw), lambda kv, n: (n, q_blk0 + kv)),
        pl.BlockSpec((s + 2 * BLK, HEAD_DIM), lambda kv, n: (0, kv)),
        pl.BlockSpec((s + 2 * BLK, HEAD_DIM), lambda kv, n: (0, kv)),
        pl.BlockSpec((grp, BLK, 3 * BLK), lambda kv, n: (kv, 0, 0)),
    ]


def _attn_fwd(sink, z, k_pad, v_pad, bias_tab, grp, q_blk0):
    s = z.shape[0]
    qw = grp * HEAD_DIM

    def body(sink_ref, q_ref, k_ref, v_ref, bias_ref, o_ref):
        _, _, _, vb, pn, _ = _attn_softmax(sink_ref, q_ref, k_ref, v_ref, bias_ref, s, grp)
        o = jnp.dot(pn.astype(BF16), vb, preferred_element_type=F32)
        for g in range(grp):
            o_ref[:, g * HEAD_DIM : (g + 1) * HEAD_DIM] = o[g * BLK : (g + 1) * BLK].astype(BF16)

    return _pallas(
        body,
        name="attn_fwd",
        grid=(N_KV_HEADS, s // BLK),
        in_specs=_attn_specs(s, grp, q_blk0),
        out_specs=pl.BlockSpec((BLK, qw), lambda kv, n: (n, kv)),
        out_shape=jax.ShapeDtypeStruct((s, N_KV_HEADS * qw), BF16),
    )(sink, z, k_pad, v_pad, bias_tab)


def _attn_bwd(sink, z, k_pad, v_pad, bias_tab, dout, dz, grp, q_blk0):
    s = z.shape[0]
    qw = grp * HEAD_DIM
    nb = s // BLK
    heads = N_KV_HEADS * grp

    def body(sink_ref, q_ref, k_ref, v_ref, bias_ref, do_ref, dz_in, dq_ref, dk_ref, dv_ref, dbias_ref, dsink_ref, dk_acc, dv_acc):
        del dz_in
        kv = pl.program_id(0)
        n = pl.program_id(1)
        start, qs, kb, vb, pn, psink = _attn_softmax(sink_ref, q_ref, k_ref, v_ref, bias_ref, s, grp)
        dov = do_ref[...]
        dos = jnp.concatenate([dov[:, g * HEAD_DIM : (g + 1) * HEAD_DIM] for g in range(grp)], axis=0)
        dp = lax.dot_general(dos, vb, NT, preferred_element_type=F32)
        dvb = lax.dot_general(pn.astype(BF16), dos, TN, preferred_element_type=F32)
        delta = jnp.sum(pn * dp, axis=-1, keepdims=True)
        ds = pn * (dp - delta)
        dsb = (ds * (HEAD_DIM**-0.5)).astype(BF16)
        dq = jnp.dot(dsb, kb, preferred_element_type=F32)
        dkb = lax.dot_general(dsb, qs, TN, preferred_element_type=F32)
        for g in range(grp):
            dq_ref[:, g * HEAD_DIM : (g + 1) * HEAD_DIM] = dq[g * BLK : (g + 1) * BLK].astype(BF16)

        @pl.when(n == 0)
        def _():
            dk_acc[...] = jnp.zeros_like(dk_acc)
            dv_acc[...] = jnp.zeros_like(dv_acc)
            dbias_ref[...] = jnp.zeros_like(dbias_ref)

        @pl.when((n == 0) & (kv == 0))
        def _():
            dsink_ref[...] = jnp.zeros_like(dsink_ref)

        dk_acc[pl.ds(start, 3 * BLK), :] += dkb
        dv_acc[pl.ds(start, 3 * BLK), :] += dvb
        dbias_ref[...] += ds.reshape(grp, BLK, 3 * BLK)
        row = lax.broadcasted_iota(I32, (heads, LANES), 0)
        sd = psink * delta
        upd = jnp.zeros((heads, LANES), F32)
        for g in range(grp):
            upd = jnp.where(row == kv * grp + g, -jnp.sum(sd[g * BLK : (g + 1) * BLK]), upd)
        dsink_ref[...] += upd

        @pl.when(n == nb - 1)
        def _():
            dk_ref[...] = dk_acc[...]
            dv_ref[...] = dv_acc[...]

    pad_spec = pl.BlockSpec((s + 2 * BLK, HEAD_DIM), lambda kv, n: (0, kv))
    kvw = N_KV_HEADS * HEAD_DIM
    return _pallas(
        body,
        name="attn_bwd",
        grid=(N_KV_HEADS, nb),
        in_specs=_attn_specs(s, grp, q_blk0) + [pl.BlockSpec((BLK, qw), lambda kv, n: (n, kv)), ANY],
        out_specs=[
            pl.BlockSpec((BLK, qw), lambda kv, n: (n, q_blk0 + kv)),
            pad_spec,
            pad_spec,
            pl.BlockSpec((grp, BLK, 3 * BLK), lambda kv, n: (kv, 0, 0)),
            pl.BlockSpec((heads, LANES), lambda kv, n: (0, 0)),
        ],
        out_shape=[
            jax.ShapeDtypeStruct(dz.shape, BF16),
            jax.ShapeDtypeStruct((s + 2 * BLK, kvw), F32),
            jax.ShapeDtypeStruct((s + 2 * BLK, kvw), F32),
            jax.ShapeDtypeStruct((heads, BLK, 3 * BLK), F32),
            jax.ShapeDtypeStruct((heads, LANES), F32),
        ],
        scratch_shapes=[pltpu.VMEM((s + 2 * BLK, HEAD_DIM), F32), pltpu.VMEM((s + 2 * BLK, HEAD_DIM), F32)],
        input_output_aliases={6: 0},
    )(sink, z, k_pad, v_pad, bias_tab, dout, dz)


def _dkv_to_dz(dk_pad, dv_pad, dz, blk_idx):
    s = dz.shape[0]
    kvw = dk_pad.shape[1]

    def body(dk_ref, dv_ref, dz_in, out_ref):
        del dz_in
        out_ref[:, :kvw] = dk_ref[...].astype(BF16)
        out_ref[:, kvw:] = dv_ref[...].astype(BF16)

    src = pl.BlockSpec((BLK, kvw), lambda i: (i + 1, 0))
    return _pallas(
        body,
        name="dkv_to_dz",
        grid=(s // BLK,),
        in_specs=[src, src, ANY],
        out_specs=pl.BlockSpec((BLK, 2 * kvw), lambda i: (i, blk_idx)),
        out_shape=jax.ShapeDtypeStruct(dz.shape, BF16),
        input_output_aliases={2: 0},
    )(dk_pad, dv_pad, dz)


def _relbias_bwd(dbias_tab, bucket):
    heads = dbias_tab.shape[0]

    def body(dt_ref, bk_ref, out_ref):
        lane = lax.broadcasted_iota(I32, (1, LANES), 1)
        bk = bk_ref[...]
        rows = []
        for h in range(heads):
            dt = dt_ref[h]
            acc = jnp.zeros((1, LANES), F32)
            for b in range(REL_BUCKETS):
                acc = jnp.where(lane == b, jnp.sum(jnp.where(bk == b, dt, 0.0)), acc)
            rows.append(acc)
        out_ref[...] = jnp.concatenate(rows, axis=0)

    return _pallas(body, name="relbias_bwd", out_shape=jax.ShapeDtypeStruct((heads, LANES), F32))(dbias_tab, bucket)


def _t5_bucket(rel):
    nb = REL_BUCKETS // 2
    ret = jnp.where(rel > 0, nb, 0)
    n = jnp.abs(rel)
    max_exact = nb // 2
    nf = jnp.maximum(n, 1).astype(F32)
    large = max_exact + (jnp.log(nf / max_exact) / math.log(REL_MAX_DIST / max_exact) * (nb - max_exact)).astype(I32)
    large = jnp.minimum(large, nb - 1)
    return ret + jnp.where(n < max_exact, n, large)


def _band_tables(rel_bias):
    qi = jnp.arange(BLK)[:, None]
    kj = jnp.arange(3 * BLK)[None, :]
    rel = kj - BLK - qi
    bucket = _t5_bucket(rel).astype(I32)
    heads = rel_bias.shape[1]
    masked = jnp.where(jnp.abs(rel) <= BLK, bucket, -1)

    def body(rb_ref, bk_ref, out_ref):
        bk = bk_ref[...]
        for h in range(heads):
            tab = jnp.full(bk.shape, NEG, F32)
            for b in range(REL_BUCKETS):
                tab = jnp.where(bk == b, rb_ref[b, h], tab)
            out_ref[h] = tab

    bias_tab = _pallas(
        body,
        name="bias_table",
        in_specs=[pl.BlockSpec(memory_space=pltpu.SMEM), pl.BlockSpec(memory_space=pltpu.VMEM)],
        out_specs=pl.BlockSpec(memory_space=pltpu.VMEM),
        out_shape=jax.ShapeDtypeStruct((heads, BLK, 3 * BLK), F32),
    )(rel_bias.astype(F32), masked)
    return bias_tab, bucket


EW_BLOCK_ELEMS = 512 * 1024


def _ew_tiles(shape, elems=EW_BLOCK_ELEMS // 2):
    r, c = shape
    tn = c if c <= 2048 else _tile(c, (2048, 1920, 1536, 1408, 1024, 512))
    tm = _tile(r, [t for t in (1024, 512, 256, 128, 64, 32, 16, 8) if t * tn <= elems] or [8])
    return tm, tn


def _cast_into_full(name, qidx, w, kind, after=()):
    r, c = w.shape
    tm, tn = _ew_tiles(w.shape, EW_BLOCK_ELEMS)
    nbi, nbj = r // tm, c // tn
    if kind == "col":
        full, out_spec = (r, c * N_CHIPS), pl.BlockSpec((tm, tn), lambda i, j, q: (i, q[0] * nbj + j))
    else:
        full, out_spec = (r * N_CHIPS, c), pl.BlockSpec((tm, tn), lambda i, j, q: (q[0] * nbi + i, j))

    def body(q_ref, w_ref, *rest):
        del q_ref
        rest[-1][...] = w_ref[...].astype(BF16)

    return _pallas(
        body,
        name=name,
        grid_spec=pltpu.PrefetchScalarGridSpec(
            num_scalar_prefetch=1,
            grid=(nbi, nbj),
            in_specs=[pl.BlockSpec((tm, tn), lambda i, j, q: (i, j))] + [ANY] * len(after),
            out_specs=out_spec,
        ),
        out_shape=jax.ShapeDtypeStruct(full, BF16),
    )(qidx, w, *after)


def _adamw(name, w, g, m, v, after=()):
    tm, tn = _ew_tiles(w.shape, EW_BLOCK_ELEMS)
    if _nbytes(w.shape, F32) <= 1024 * 1024:
        tm, tn = w.shape
    spec = pl.BlockSpec((tm, tn), lambda i, j: (i, j))
    n_after = len(after)

    def body(w_ref, g_ref, m_ref, v_ref, *rest):
        d_ref, nm_ref, nv_ref, g_out_ref = rest[n_after:]
        gv = g_ref[...]
        g_out_ref[...] = gv
        nm = ADAM_B1 * m_ref[...] + (1.0 - ADAM_B1) * gv
        nv = ADAM_B2 * v_ref[...] + (1.0 - ADAM_B2) * (gv * gv)
        m_hat = nm / (1.0 - ADAM_B1**ADAM_STEP)
        v_hat = nv / (1.0 - ADAM_B2**ADAM_STEP)
        d_ref[...] = -ADAM_LR * (m_hat / (jnp.sqrt(v_hat) + ADAM_EPS) + ADAM_WD * w_ref[...])
        nm_ref[...] = nm
        nv_ref[...] = nv

    out = jax.ShapeDtypeStruct(w.shape, F32)
    return _pallas(
        body, name=name, grid=(w.shape[0] // tm, w.shape[1] // tn), in_specs=[spec] * 4 + [ANY] * n_after,
        out_specs=[spec] * 4, out_shape=[out, out, out, out],
        compiler_params=_params(_mm_vmem([((tm, tn), F32, 24)])),
    )(w, g, m, v, *after)


def _pair_add(name, cidx, g_full, r_sib, kind):
    hr, hc = r_sib.shape
    tm, tn = _ew_tiles((hr, hc), 2 * EW_BLOCK_ELEMS)
    nbi, nbj = hr // tm, hc // tn
    if kind == "col":
        g_spec = pl.BlockSpec((tm, tn), lambda i, j, c: (c[0] * nbi + i, j))
    else:
        g_spec = pl.BlockSpec((tm, tn), lambda i, j, c: (i, c[0] * nbj + j))
    spec = pl.BlockSpec((tm, tn), lambda i, j, c: (i, j))

    def body(c_ref, g_ref, r_ref, o_ref):
        del c_ref
        o_ref[...] = (g_ref[...].astype(F32) + r_ref[...].astype(F32)).astype(BF16)

    return _pallas(
        body,
        name=name,
        grid_spec=pltpu.PrefetchScalarGridSpec(num_scalar_prefetch=1, grid=(nbi, nbj), in_specs=[g_spec, spec], out_specs=spec),
        out_shape=jax.ShapeDtypeStruct((hr, hc), BF16),
        compiler_params=_params(_mm_vmem([((tm, tn), BF16, 6), ((tm, tn), F32, 3)])),
    )(cidx, g_full, r_sib)


def _chip_sum(name, qidx, c_half, r_ici, kind):
    _, pr, pc = r_ici.shape
    tm, tn = _ew_tiles((pr, pc), 2 * EW_BLOCK_ELEMS)
    nbi, nbj = pr // tm, pc // tn
    if kind == "col":
        own_spec = pl.BlockSpec((tm, tn), lambda i, j, q: (i, q[0] * nbj + j))
        full, out_spec = (2 * pr, pc), pl.BlockSpec((tm, tn), lambda i, j, q: (q[1] * nbi + i, j))
    else:
        own_spec = pl.BlockSpec((tm, tn), lambda i, j, q: (q[0] * nbi + i, j))
        full, out_spec = (pr, 2 * pc), pl.BlockSpec((tm, tn), lambda i, j, q: (i, q[1] * nbj + j))

    def body(q_ref, own_ref, r_ref, o_ref):
        q = q_ref[0]
        own = own_ref[...].astype(F32)
        recv = [r_ref[r].astype(F32) for r in range(3)]
        total = None
        for chip in range(N_CHIPS):
            d = chip ^ q
            term = jnp.where(d == 0, own, jnp.where(d == 2, recv[0], jnp.where(d == 1, recv[1], recv[2])))
            total = term if total is None else total + term
        o_ref[...] = total

    return _pallas(
        body,
        name=name,
        grid_spec=pltpu.PrefetchScalarGridSpec(
            num_scalar_prefetch=1,
            grid=(nbi, nbj),
            in_specs=[own_spec, pl.BlockSpec((3, tm, tn), lambda i, j, q: (0, i, j))],
            out_specs=out_spec,
        ),
        out_shape=jax.ShapeDtypeStruct(full, F32),
        compiler_params=_params(_mm_vmem([((tm, tn), BF16, 8), ((tm, tn), F32, 6)])),
    )(qidx, c_half, r_ici)


_REL_MASK = (2, 1, 3)


def _place():
    x, y, c = lax.axis_index("x"), lax.axis_index("y"), lax.axis_index("c")
    chips = [(1 - x, y), (x, 1 - y), (1 - x, 1 - y)]
    return x, y, c, 2 * x + y, chips


def _shard_view(ref, kind, chip):
    if kind == "col":
        w = ref.shape[1] // N_CHIPS
        return ref.at[:, pl.ds(pl.multiple_of(chip * w, LANES), w)]
    h = ref.shape[0] // N_CHIPS
    return ref.at[pl.ds(pl.multiple_of(chip * h, 16), h), :]


def _row_half(ref, half):
    h = ref.shape[0] // 2
    return ref.at[pl.ds(pl.multiple_of(half * h, 16), h), :]


def _pair_half(ref, kind, half):
    if kind == "col":
        return _row_half(ref, half)
    w = ref.shape[1] // 2
    return ref.at[:, pl.ds(pl.multiple_of(half * w, LANES), w)]


def _remote(src, dst, send_sem, recv_sem, dev):
    return pltpu.make_async_remote_copy(src_ref=src, dst_ref=dst, send_sem=send_sem, recv_sem=recv_sem, device_id=dev, device_id_type=MESH)


def _hbm(a):
    return pltpu.with_memory_space_constraint(a, pltpu.HBM)


def _gather_start(name, fulls, kinds, rels=(0, 1, 2), after=()):
    n_w = len(fulls)

    def body(*refs):
        g = refs[:n_w]
        send_sem, recv_sem = refs[n_w + len(after)], refs[n_w + len(after) + 1]
        token = refs[-1]
        _, _, c, q, chips = _place()
        for w in range(n_w):
            mine = _row_half(_shard_view(g[w], kinds[w], q), c)
            for r in rels if isinstance(rels, tuple) else rels[w]:
                _remote(mine, mine, send_sem.at[3 * w + r], recv_sem.at[3 * w + r], (*chips[r], c)).start()
        token[...] = jnp.zeros_like(token)

    res = _pallas(
        body,
        name=name,
        out_shape=(
            pltpu.SemaphoreType.DMA((3 * n_w,)),
            pltpu.SemaphoreType.DMA((3 * n_w,)),
            *[pltpu.HBM(f.shape, f.dtype) for f in fulls],
            jax.ShapeDtypeStruct((8, LANES), F32),
        ),
        in_specs=[HBM_SPEC] * n_w + [ANY] * len(after),
        out_specs=(SEM_SPEC, SEM_SPEC, *[HBM_SPEC] * n_w, pl.BlockSpec(memory_space=pltpu.VMEM)),
        input_output_aliases={w: w + 2 for w in range(n_w)},
        compiler_params=pltpu.CompilerParams(has_side_effects=EFFECT),
    )(*[_hbm(f) for f in fulls], *after)
    return res[0], res[1], list(res[2 : 2 + n_w]), res[-1]


def _relay_copies(kinds, waiting):
    def copies(refs, send_sem, recv_sem):
        _, _, c, q, chips = _place()
        out = []
        for i, kind in enumerate(kinds):
            for k, (src_rel, dst_rel) in enumerate(((0, 1), (1, 0))):
                held = _row_half(_row_half(_shard_view(refs[i], kind, q ^ _REL_MASK[src_rel]), c), k)
                far = _row_half(_row_half(_shard_view(refs[i], kind, q ^ _REL_MASK[2]), c), k)
                dst = far if waiting else held
                out.append(_remote(held, dst, send_sem.at[2 * i + k], recv_sem.at[2 * i + k], (*chips[dst_rel], c)))
        return out

    return copies


def _forward_copies(kinds, waiting, rels=(0, 1, 2), sem0=0):
    def copies(refs, send_sem, recv_sem):
        x, y, c, q, _ = _place()
        out = []
        for i, kind in enumerate(kinds):
            for k, r in enumerate(rels):
                quarter = _shard_view(refs[i], kind, q ^ _REL_MASK[r])
                landed = _row_half(quarter, c)
                dst = _row_half(quarter, 1 - c) if waiting else landed
                sem = sem0 + len(rels) * i + k
                out.append(_remote(landed, dst, send_sem.at[sem], recv_sem.at[sem], (x, y, 1 - c)))
        return out

    return copies


def _relay_and_forward_copies(kinds, waiting):
    forward = _forward_copies(kinds, waiting, rels=(0, 1), sem0=2 * len(kinds))
    relay = _relay_copies(kinds, waiting)
    return lambda refs, send_sem, recv_sem: forward(refs, send_sem, recv_sem) + relay(refs, send_sem, recv_sem)


def _gather_wait(name, fulls, kinds, w_ids, send_sem, recv_sem, after, rels=(0, 1, 2)):
    n = len(fulls)

    def body(*refs):
        g = refs[:n]
        s_sem, r_sem = refs[n], refs[n + 1]
        x, y, c, q, _ = _place()
        for i, w in enumerate(w_ids):
            mine = _row_half(_shard_view(g[i], kinds[i], q), c)
            for r in rels:
                landed = _row_half(_shard_view(g[i], kinds[i], q ^ _REL_MASK[r]), c)
                cp = _remote(mine, landed, s_sem.at[3 * w + r], r_sem.at[3 * w + r], (x, y, 1 - c))
                cp.wait_send()
                cp.wait_recv()

    res = _pallas(
        body,
        name=name,
        out_shape=[pltpu.HBM(f.shape, f.dtype) for f in fulls],
        in_specs=[HBM_SPEC] * n + [SEM_SPEC, SEM_SPEC, ANY],
        out_specs=[HBM_SPEC] * n,
        input_output_aliases={i: i for i in range(n)},
        compiler_params=pltpu.CompilerParams(has_side_effects=EFFECT),
    )(*fulls, send_sem, recv_sem, after)
    return list(res)


def _gather_forward(name, fulls, kinds, rels=(0, 1, 2)):
    n = len(fulls)

    def body(*refs):
        g = refs[n : 2 * n]
        send, recv = refs[2 * n :]
        _sibling_handshake()
        x, y, c, q, _ = _place()
        sib = (x, y, 1 - c)
        cps = []
        for i in range(n):
            for r in rels:
                landed = _row_half(_shard_view(g[i], kinds[i], q ^ _REL_MASK[r]), c)
                cps.append(_remote(landed, landed, send.at[i, r], recv.at[i, r], sib))
        for cp in cps:
            cp.start()
        for i in range(n):
            for r in rels:
                other = _row_half(_shard_view(g[i], kinds[i], q ^ _REL_MASK[r]), 1 - c)
                _remote(other, other, send.at[i, r], recv.at[i, r], sib).wait_recv()
        for cp in cps:
            cp.wait_send()

    res = _pallas(
        body,
        name=name,
        in_specs=[ANY] * n,
        out_specs=[ANY] * n,
        out_shape=[jax.ShapeDtypeStruct(f.shape, f.dtype) for f in fulls],
        scratch_shapes=[pltpu.SemaphoreType.DMA((n, 3)), pltpu.SemaphoreType.DMA((n, 3))],
        input_output_aliases={i: i for i in range(n)},
        compiler_params=pltpu.CompilerParams(collective_id=SIBLING_BARRIER_ID),
    )(*fulls)
    return list(res)


SIBLING_BARRIER_ID = 1


def _sibling_handshake():
    sib = (lax.axis_index("x"), lax.axis_index("y"), 1 - lax.axis_index("c"))
    barrier = pltpu.get_barrier_semaphore()
    pl.semaphore_signal(barrier, inc=1, device_id=sib, device_id_type=MESH)
    pl.semaphore_wait(barrier, 1)


def _split_start(name, bufs, n_sems, copies, sibling_only=False):
    n = len(bufs)

    def body(*refs):
        if sibling_only:
            _sibling_handshake()
        for cp in copies(refs[:n], refs[n], refs[n + 1]):
            cp.start()

    extra = {"collective_id": SIBLING_BARRIER_ID} if sibling_only else {}

    res = _pallas(
        body,
        name=name,
        out_shape=(
            pltpu.SemaphoreType.DMA((n_sems,)),
            pltpu.SemaphoreType.DMA((n_sems,)),
            *[pltpu.HBM(b.shape, b.dtype) for b in bufs],
        ),
        in_specs=[HBM_SPEC] * n,
        out_specs=(SEM_SPEC, SEM_SPEC, *[HBM_SPEC] * n),
        input_output_aliases={i: i + 2 for i in range(n)},
        compiler_params=pltpu.CompilerParams(has_side_effects=EFFECT, **extra),
    )(*[_hbm(b) for b in bufs])
    return res[0], res[1], list(res[2:])


def _split_wait(name, bufs, send_sem, recv_sem, copies, after):
    n = len(bufs)

    def body(*refs):
        for cp in copies(refs[:n], refs[n], refs[n + 1]):
            cp.wait_send()
            cp.wait_recv()

    res = _pallas(
        body,
        name=name,
        out_shape=[pltpu.HBM(b.shape, b.dtype) for b in bufs],
        in_specs=[HBM_SPEC] * n + [SEM_SPEC, SEM_SPEC, ANY],
        out_specs=[HBM_SPEC] * n,
        input_output_aliases={i: i for i in range(n)},
        compiler_params=pltpu.CompilerParams(has_side_effects=EFFECT),
    )(*bufs, send_sem, recv_sem, after)
    return list(res)


def _pair_exchange_copies(kinds):
    n = len(kinds)

    def copies(refs, send_sem, recv_sem):
        x, y, c, _, _ = _place()
        return [
            _remote(_pair_half(refs[w], kinds[w], 1 - c), refs[n + w], send_sem.at[w], recv_sem.at[w], (x, y, 1 - c))
            for w in range(n)
        ]

    return copies


def _pair_share_copies(kinds, waiting):
    def copies(refs, send_sem, recv_sem):
        x, y, c, _, _ = _place()
        out = []
        for w, kind in enumerate(kinds):
            mine = _pair_half(refs[w], kind, c)
            dst = _pair_half(refs[w], kind, 1 - c) if waiting else mine
            out.append(_remote(mine, dst, send_sem.at[w], recv_sem.at[w], (x, y, 1 - c)))
        return out

    return copies


def _piece_shape(half_shape, kind):
    r, c = half_shape
    return (3, r, c // N_CHIPS) if kind == "col" else (3, r // N_CHIPS, c)


def _chip_send_start(name, halves, kinds):
    n = len(halves)
    lands = [lax.empty(_piece_shape(h.shape, k), BF16) for h, k in zip(halves, kinds)]

    def body(*refs):
        h, land = refs[:n], refs[n : 2 * n]
        send_sem, recv_sem = refs[2 * n], refs[2 * n + 1]
        _, _, c, q, chips = _place()
        for i in range(n):
            for r, chip in enumerate(chips):
                piece = _shard_view(h[i], kinds[i], q ^ _REL_MASK[r])
                _remote(piece, land[i].at[r], send_sem.at[3 * i + r], recv_sem.at[3 * i + r], (*chip, c)).start()

    res = _pallas(
        body,
        name=name,
        out_shape=(
            pltpu.SemaphoreType.DMA((3 * n,)),
            pltpu.SemaphoreType.DMA((3 * n,)),
            *[pltpu.HBM(a.shape, a.dtype) for a in halves],
            *[pltpu.HBM(a.shape, a.dtype) for a in lands],
        ),
        in_specs=[HBM_SPEC] * (2 * n),
        out_specs=(SEM_SPEC, SEM_SPEC, *[HBM_SPEC] * (2 * n)),
        input_output_aliases={i: i + 2 for i in range(2 * n)},
        compiler_params=pltpu.CompilerParams(has_side_effects=EFFECT),
    )(*[_hbm(a) for a in halves], *[_hbm(a) for a in lands])
    return res[0], res[1], list(res[2 : 2 + n]), list(res[2 + n :])


def _chip_send_wait(name, halves, lands, kinds, send_sem, recv_sem, after):
    n = len(halves)

    def body(*refs):
        h, land = refs[:n], refs[n : 2 * n]
        s_sem, r_sem = refs[2 * n], refs[2 * n + 1]
        x, y, c, q, _ = _place()
        for i in range(n):
            for r in range(3):
                piece = _shard_view(h[i], kinds[i], q ^ _REL_MASK[r])
                cp = _remote(piece, land[i].at[r], s_sem.at[3 * i + r], r_sem.at[3 * i + r], (x, y, 1 - c))
                cp.wait_send()
                cp.wait_recv()

    res = _pallas(
        body,
        name=name,
        out_shape=[pltpu.HBM(a.shape, a.dtype) for a in halves] + [pltpu.HBM(a.shape, a.dtype) for a in lands],
        in_specs=[HBM_SPEC] * (2 * n) + [SEM_SPEC, SEM_SPEC, ANY],
        out_specs=[HBM_SPEC] * (2 * n),
        input_output_aliases={i: i for i in range(2 * n)},
        compiler_params=pltpu.CompilerParams(has_side_effects=EFFECT),
    )(*halves, *lands, send_sem, recv_sem, after)
    return list(res[:n]), list(res[n:])


def _small_exchange_copies(waiting):
    def copies(refs, send_sem, recv_sem):
        p, land = refs
        x, y, c, q, _ = _place()
        me = 2 * q + c
        out = []
        for dd in range(1, 2 * N_CHIPS):
            dev = (x ^ ((dd >> 2) & 1), y ^ ((dd >> 1) & 1), c ^ (dd & 1))
            dst = land.at[me ^ dd] if waiting else land.at[me]
            out.append(_remote(p, dst, send_sem.at[dd - 1], recv_sem.at[dd - 1], dev))
        return out

    return copies


def _small_sum(name, me_idx, p, land):
    rows = p.shape[0]
    n_dev = 2 * N_CHIPS

    def body(me_ref, p_ref, land_ref, o_ref):
        me = me_ref[0]
        total = None
        for dev in range(n_dev):
            term = jnp.where(me == dev, p_ref[...], land_ref[dev])
            total = term if total is None else total + term
        o_ref[...] = total

    return _pallas(
        body,
        name=name,
        grid_spec=pltpu.PrefetchScalarGridSpec(
            num_scalar_prefetch=1,
            grid=(1,),
            in_specs=[pl.BlockSpec((rows, LANES), lambda i, m: (0, 0)), pl.BlockSpec((n_dev, rows, LANES), lambda i, m: (0, 0, 0))],
            out_specs=pl.BlockSpec((rows, LANES), lambda i, m: (0, 0)),
        ),
        out_shape=jax.ShapeDtypeStruct(p.shape, F32),
    )(me_idx, p, land)


def _pack(parts):
    rows = []
    for a in parts:
        flat = a.reshape(-1).astype(F32)
        n = flat.shape[0]
        padded = -(-n // (8 * LANES)) * (8 * LANES)
        rows.append(jnp.pad(flat, (0, padded - n)).reshape(-1, LANES))
    return jnp.concatenate(rows, axis=0)


def _unpack(packed, shapes):
    out, row = [], 0
    for shp in shapes:
        n = int(np.prod(shp))
        nrows = -(-n // (8 * LANES)) * 8
        out.append(packed[row : row + nrows].reshape(-1)[:n].reshape(shp))
        row += nrows
    return out


def kernel(x, w_in, norm_mix, sgu_v_gain, sgu_w_s, sgu_b_s, w_a_out, attn_sink, rel_bias, w_b_out, w_o, norm_ffn, w_gate, w_up, w_down, norm_final, loss_target, m_w_in, m_norm_mix, m_sgu_v_gain, m_sgu_w_s, m_sgu_b_s, m_w_a_out, m_attn_sink, m_rel_bias, m_w_b_out, m_w_o, m_norm_ffn, m_w_gate, m_w_up, m_w_down, m_norm_final, v_w_in, v_norm_mix, v_sgu_v_gain, v_sgu_w_s, v_sgu_b_s, v_w_a_out, v_attn_sink, v_rel_bias, v_w_b_out, v_w_o, v_norm_ffn, v_w_gate, v_w_up, v_w_down, v_norm_final):
    s, d = x.shape[1], x.shape[2]
    w_sgu = sgu_v_gain.shape[1]
    groups = sgu_w_s.shape[1]
    heads = attn_sink.shape[1]
    grp = heads // N_KV_HEADS
    w_att = heads * HEAD_DIM
    w_kv = N_KV_HEADS * HEAD_DIM
    d_ff = w_gate.shape[2] * N_CHIPS
    n_in = w_in.shape[2] * N_CHIPS
    off_q = 2 * w_sgu
    off_k = off_q + w_att
    off_g = off_k + 2 * w_kv
    assert n_in == off_g + 2 * d and groups * BLK == w_sgu and s % BLK == 0

    x2d = x.reshape(s, d)
    tgt = loss_target.reshape(s, d)
    c_idx = lax.axis_index("c").astype(I32).reshape(1)
    q_idx = (2 * lax.axis_index("x") + lax.axis_index("y")).astype(I32).reshape(1)
    qc_idx = jnp.concatenate([q_idx, c_idx])

    W_IN, W_A, W_B, W_O, W_GATE, W_UP, W_DOWN = range(7)
    names = ["w_in", "w_a", "w_b", "w_o", "w_gate", "w_up", "w_down"]
    kinds = ["col", "col", "col", "row", "col", "col", "row"]
    big_w = [w_in[0], w_a_out[0], w_b_out[0], w_o[0], w_gate[0], w_up[0], w_down[0]]
    big_m = [m_w_in[0], m_w_a_out[0], m_w_b_out[0], m_w_o[0], m_w_gate[0], m_w_up[0], m_w_down[0]]
    big_v = [v_w_in[0], v_w_a_out[0], v_w_b_out[0], v_w_o[0], v_w_gate[0], v_w_up[0], v_w_down[0]]
    full_in = _cast_into_full("cast_w_in", q_idx, big_w[W_IN], kinds[W_IN])
    in_send, in_recv, (full_in,), token = _gather_start("gather_start_in", [full_in], [kinds[W_IN]], rels=(0, 1))
    rest = [_cast_into_full("cast_" + names[i], q_idx, big_w[i], kinds[i], after=(token,)) for i in range(1, 7)]

    ws_b = sgu_w_s[0].astype(BF16)
    wst_b = jnp.swapaxes(sgu_w_s[0], 1, 2).astype(BF16)
    b_col = sgu_b_s[0].reshape(groups, BLK, 1)
    bias_tab, bucket = _band_tables(rel_bias)
    sink = attn_sink[0]
    small_w = [norm_mix, sgu_v_gain, sgu_w_s, sgu_b_s, attn_sink, rel_bias, norm_ffn, norm_final]
    small_m = [m_norm_mix, m_sgu_v_gain, m_sgu_w_s, m_sgu_b_s, m_attn_sink, m_rel_bias, m_norm_ffn, m_norm_final]
    small_v = [v_norm_mix, v_sgu_v_gain, v_sgu_w_s, v_sgu_b_s, v_attn_sink, v_rel_bias, v_norm_ffn, v_norm_final]
    small_shapes = [w.shape for w in small_w]
    zero1 = jnp.zeros((1,), F32)
    pw, pm, pv = _pack(small_w + [zero1]), _pack(small_m + [zero1]), _pack(small_v + [zero1])
    h1 = _rms_fwd("rms_mix", x2d, norm_mix, after=(token, rest[-1], ws_b, wst_b, b_col, bias_tab, pw, pm, pv))
    (full_in,) = _gather_wait("gather_wait_in", [full_in], [kinds[W_IN]], [0], in_send, in_recv, h1, rels=(0, 1))
    relay_send, relay_recv, (full_in,) = _split_start(
        "gather_relay_in", [full_in], 4, _relay_and_forward_copies([kinds[W_IN]], False)
    )
    full_down = rest.pop()
    ag_send, ag_recv, rest, token = _gather_start("gather_start_rest", rest, kinds[1:6], rels=(0, 1), after=(full_in,))
    (full_in,) = _split_wait(
        "gather_relay_wait_in", [full_in], relay_send, relay_recv, _relay_and_forward_copies([kinds[W_IN]], True), token
    )
    (g_in,) = _gather_forward("gather_fwd_in", [full_in], [kinds[W_IN]], rels=(2,))
    fulls = [g_in] + rest

    def relay_begin(tag, ids, after, copies=_relay_copies, sems_per_weight=2):
        ks = [kinds[i] for i in ids]
        bufs = _gather_wait("gather_wait_" + tag, [fulls[i] for i in ids], ks, [i - 1 for i in ids], ag_send, ag_recv, after,
                            rels=(0, 1))
        send, recv, bufs = _split_start("gather_relay_" + tag, bufs, sems_per_weight * len(ids), copies(ks, False))
        return tag, ks, send, recv, bufs

    def relay_end(state, after):
        tag, ks, send, recv, bufs = state
        bufs = _split_wait("gather_relay_wait_" + tag, bufs, send, recv, _relay_and_forward_copies(ks, True), after)
        return _gather_forward("gather_fwd_" + tag, bufs, ks, rels=(2,))

    def relay_end_async(state, after):
        tag, ks, send, recv, bufs = state
        bufs = _split_wait("gather_relay_wait_" + tag, bufs, send, recv, _relay_copies(ks, True), after)
        send, recv, bufs = _split_start("gather_fwd_start_" + tag, bufs, 3 * len(ks), _forward_copies(ks, False), sibling_only=True)
        return tag, ks, send, recv, bufs

    def forwarded(state, after):
        tag, ks, send, recv, bufs = state
        return _split_wait("gather_fwd_wait_" + tag, bufs, send, recv, _forward_copies(ks, True), after)

    tm = _tile(s, (1024, 512, 256, 128))

    tn = _tile(n_in, (768, 640, 512))
    z = _mm(
        "mm_z", (s // tm, n_in // tn, 1), [h1, g_in],
        [pl.BlockSpec((tm, d), lambda i, j, k: (i, 0)), pl.BlockSpec((d, tn), lambda i, j, k: (0, j))],
        [jax.ShapeDtypeStruct((s, n_in), BF16)], [pl.BlockSpec((tm, tn), lambda i, j, k: (i, j))],
        [(0, 1, NN, 0)], 1, (tm, tn), 1, lambda ins, vals, outs, cs: _put(outs[0], cs, vals[0]),
        _mm_vmem([((tm, d), BF16, 2), ((d, tn), BF16, 2), ((tm, tn), F32, 3)]),
    )[0]
    mix_relay = relay_begin("mix", [W_A, W_B, W_O], z, copies=_relay_and_forward_copies, sems_per_weight=4)

    a_act = _sgu_fwd(z, sgu_v_gain, ws_b, b_col, w_sgu, after=(mix_relay[4][0],))

    kv_b = z[:, off_k:off_g]
    k_pad = jnp.pad(kv_b[:, :w_kv], ((BLK, BLK), (0, 0)))
    v_pad = jnp.pad(kv_b[:, w_kv:], ((BLK, BLK), (0, 0)))
    q_blk0 = off_q // (grp * HEAD_DIM)
    att = _attn_fwd(sink, z, k_pad, v_pad, bias_tab, grp, q_blk0)

    gate_relay = relay_begin("gate", [W_GATE], att)
    up_relay = relay_begin("up", [W_UP], gate_relay[4][0])
    down_send, down_recv, (full_down,), token = _gather_start(
        "gather_start_down", [full_down], [kinds[W_DOWN]], after=(a_act, up_relay[4][0])
    )
    g_a, g_b, g_o = relay_end(mix_relay, token)

    tg = _tile(d, (512,))
    ga0, gb0 = off_g // tg, (off_g + d) // tg

    def ep_gate(ins, vals, outs, cs):
        sa, sb = _sigmoid(ins[4][:, cs].astype(F32)), _sigmoid(ins[5][:, cs].astype(F32))
        _put(outs[0], cs, sa * vals[0] + sb * vals[1])
        _put(outs[1], cs, vals[0])
        _put(outs[2], cs, vals[1])

    t_out = pl.BlockSpec((tm, tg), lambda i, j, k: (i, j))
    m_act, y_a, y_b = _mm(
        "mm_branches", (s // tm, d // tg, 1), [a_act, g_a, att, g_b, z, z],
        [pl.BlockSpec((tm, w_sgu), lambda i, j, k: (i, 0)), pl.BlockSpec((w_sgu, tg), lambda i, j, k: (0, j)),
         pl.BlockSpec((tm, w_att), lambda i, j, k: (i, 0)), pl.BlockSpec((w_att, tg), lambda i, j, k: (0, j)),
         pl.BlockSpec((tm, tg), lambda i, j, k: (i, ga0 + j)), pl.BlockSpec((tm, tg), lambda i, j, k: (i, gb0 + j))],
        [jax.ShapeDtypeStruct((s, d), BF16)] * 3,
        [t_out, t_out, t_out], [(0, 1, NN, 0), (2, 3, NN, 1)], 2, (tm, tg), 1, ep_gate,
        _mm_vmem([((tm, w_sgu), BF16, 4), ((w_sgu, tg), BF16, 4), ((tm, tg), F32, 12)]),    )

    tn = _tile(d, (1024, 512))

    def ep_residual(ins, vals, outs, cs):
        _put(outs[0], cs, ins[2][:, cs] + vals[0])

    gate_fwd = relay_end_async(gate_relay, m_act)
    x2 = _mm(
        "mm_wo", (s // tm, d // tn, 1), [m_act, g_o, x2d],
        [pl.BlockSpec((tm, d), lambda i, j, k: (i, 0)), pl.BlockSpec((d, tn), lambda i, j, k: (0, j)),
         pl.BlockSpec((tm, tn), lambda i, j, k: (i, j))],
        [jax.ShapeDtypeStruct((s, d), F32)], [pl.BlockSpec((tm, tn), lambda i, j, k: (i, j))],
        [(0, 1, NN, 0)], 1, (tm, tn), 1, ep_residual,
        _mm_vmem([((tm, d), BF16, 2), ((d, tn), BF16, 2), ((tm, tn), F32, 5)]),
        after=(gate_fwd[4][0],),
    )[0]
    (g_gate,) = forwarded(gate_fwd, x2)
    up_fwd = relay_end_async(up_relay, g_gate)
    h2 = _rms_fwd("rms_ffn", x2, norm_ffn, after=(up_fwd[4][0],))
    (g_up,) = forwarded(up_fwd, h2)

    tf = _tile(d_ff, (512,))

    def ep_swiglu(ins, vals, outs, cs):
        gt, up = vals
        _put(outs[0], cs, gt)
        _put(outs[1], cs, up)
        _put(outs[2], cs, (gt * _sigmoid(gt)) * up)

    f_out = pl.BlockSpec((tm, tf), lambda i, j, k: (i, j))
    gt, up, f_act = _mm(
        "mm_gate_up", (s // tm, d_ff // tf, 1), [h2, g_gate, g_up],
        [pl.BlockSpec((tm, d), lambda i, j, k: (i, 0)), pl.BlockSpec((d, tf), lambda i, j, k: (0, j)),
         pl.BlockSpec((d, tf), lambda i, j, k: (0, j))],
        [jax.ShapeDtypeStruct((s, d_ff), BF16)] * 3,
        [f_out, f_out, f_out], [(0, 1, NN, 0), (0, 2, NN, 1)], 2, (tm, tf), 1, ep_swiglu,
        _mm_vmem([((tm, d), BF16, 2), ((d, tf), BF16, 4), ((tm, tf), F32, 8)]),    )
    (g_down,) = _gather_forward(
        "gather_fwd_ffn_out",
        _gather_wait("gather_wait_ffn_out", [full_down], [kinds[W_DOWN]], [0], down_send, down_recv, f_act),
        [kinds[W_DOWN]],
    )

    tkf = _tile(d_ff, (1408, 1024, 512))
    tml, tnl = _tile(s, (512, 256, 128)), _tile(d, (512,))
    x3 = _mm(
        "mm_down", (s // tml, d // tnl, 1), [f_act, g_down, x2],
        [pl.BlockSpec((tml, d_ff), lambda i, j, k: (i, 0)), pl.BlockSpec((d_ff, tnl), lambda i, j, k: (0, j)),
         pl.BlockSpec((tml, tnl), lambda i, j, k: (i, j))],
        [jax.ShapeDtypeStruct((s, d), F32)], [pl.BlockSpec((tml, tnl), lambda i, j, k: (i, j))],
        [(0, 1, NN, 0)], 1, (tml, tnl), 1, ep_residual,
        _mm_vmem([((tml, d_ff), BF16, 2), ((d_ff, tnl), BF16, 2), ((tml, tnl), F32, 6)]),    )[0]

    dx3, dx3b, dg_final, loss_part = _head(x3, norm_final.reshape(1, d), tgt)

    def reduce_a(tag, ids, grads):
        ks = [kinds[i] for i in ids]
        lands = [lax.empty((g.shape[0] // 2, g.shape[1]) if k == "col" else (g.shape[0], g.shape[1] // 2), BF16)
                 for g, k in zip(grads, ks)]
        send, recv, bufs = _split_start("pair_send_" + tag, list(grads) + lands, len(ids), _pair_exchange_copies(ks), sibling_only=True)
        return {"tag": tag, "ids": ids, "ks": ks, "pair": (send, recv, bufs), "token": bufs[0]}

    def reduce_b(st, after):
        tag, ids, ks = st["tag"], st["ids"], st["ks"]
        send, recv, bufs = st["pair"]
        bufs = _split_wait("pair_wait_" + tag, bufs, send, recv, _pair_exchange_copies(ks), after)
        grads, from_sib = bufs[: len(ids)], bufs[len(ids) :]
        halves = [_pair_add("pair_add_" + names[i], c_idx, g, r, k) for i, g, r, k in zip(ids, grads, from_sib, ks)]
        st["chip"] = _chip_send_start("chip_send_" + tag, halves, ks)
        st["token"] = st["chip"][2][0]

    def reduce_c(st, after):
        tag, ids, ks = st["tag"], st["ids"], st["ks"]
        send, recv, halves, lands = st["chip"]
        halves, lands = _chip_send_wait("chip_wait_" + tag, halves, lands, ks, send, recv, after)
        pieces = [_chip_sum("chip_sum_" + names[i], qc_idx, h, r, k) for i, h, r, k in zip(ids, halves, lands, ks)]
        st["share"] = _split_start("share_send_" + tag, pieces, len(ids), _pair_share_copies(ks, False), sibling_only=True)
        st["token"] = st["share"][2][0]

    def reduce_d(st, after):
        send, recv, bufs = st["share"]
        return _split_wait("share_wait_" + st["tag"], bufs, send, recv, _pair_share_copies(st["ks"], True), after)

    grads_big, upd = [None] * 7, [None] * 7

    def finish(st, after):
        shared = reduce_d(st, after)
        after = shared[0]
        for i, g in zip(st["ids"], shared):
            upd[i] = _adamw("adamw_" + names[i], big_w[i], g, big_m[i], big_v[i], after=(after,))
            grads_big[i] = upd[i][3]
            after = upd[i][0]
        return after

    def ep_swiglu_bwd(ins, vals, outs, cs):
        df = vals[0]
        gtv, upv = ins[2][:, cs].astype(F32), ins[3][:, cs].astype(F32)
        sg = _sigmoid(gtv)
        _put(outs[0], cs, df * upv * (sg + gtv * sg * (1.0 - sg)))
        _put(outs[1], cs, df * (gtv * sg))

    dgt, dup = _mm(
        "mm_dswiglu", (s // tm, d_ff // tf, 1), [dx3b, g_down, gt, up],
        [pl.BlockSpec((tm, d), lambda i, j, k: (i, 0)), pl.BlockSpec((tf, d), lambda i, j, k: (j, 0)), f_out, f_out],
        [jax.ShapeDtypeStruct((s, d_ff), BF16), jax.ShapeDtypeStruct((s, d_ff), BF16)], [f_out, f_out],
        [(0, 1, NT, 0)], 1, (tm, tf), 1, ep_swiglu_bwd,
        _mm_vmem([((tm, d), BF16, 2), ((tf, d), BF16, 2), ((tm, tf), F32, 8)]),    )

    def ep_store(ins, vals, outs, cs):
        for o, v in zip(outs, vals):
            _put(o, cs, v)

    twn = _tile(d, (1024, 512))
    gw_down = _mm(
        "mm_gw_down", (d_ff // tkf, d // twn, 1), [f_act, dx3b],
        [pl.BlockSpec((s, tkf), lambda i, j, k: (0, i)), pl.BlockSpec((s, twn), lambda i, j, k: (0, j))],
        [jax.ShapeDtypeStruct((d_ff, d), BF16)], [pl.BlockSpec((tkf, twn), lambda i, j, k: (i, j))],
        [(0, 1, TN, 0)], 1, (tkf, twn), 1, ep_store,
        _mm_vmem([((s, tkf), BF16, 3), ((s, twn), BF16, 2), ((tkf, twn), F32, 3)]),
    )[0]
    red_down = reduce_a("down", [W_DOWN], [gw_down])

    tn2 = _tile(d, (256,))
    dh2_specs = [pl.BlockSpec((tm, d_ff), lambda i, j, k: (i, 0)), pl.BlockSpec((tn2, d_ff), lambda i, j, k: (j, 0))]
    dh2_tile = pl.BlockSpec((tm, tn2), lambda i, j, k: (i, j))
    dh2_vmem = _mm_vmem([((tm, d_ff), BF16, 2), ((tn2, d_ff), BF16, 2), ((tm, tn2), F32, 7)])
    dh2 = _mm(
        "mm_dh2_gate", (s // tm, d // tn2, 1), [dgt, g_gate], dh2_specs,
        [jax.ShapeDtypeStruct((s, d), F32)], [dh2_tile], [(0, 1, NT, 0)], 1, (tm, tn2), 1, ep_store, dh2_vmem,
        after=(red_down["token"],),
    )[0]
    dh2 = _mm(
        "mm_dh2_up", (s // tm, d // tn2, 1), [dup, g_up, dh2], dh2_specs + [dh2_tile],
        [jax.ShapeDtypeStruct((s, d), F32)], [dh2_tile], [(0, 1, NT, 0)], 1, (tm, tn2), 1, ep_residual, dh2_vmem,
    )[0]
    reduce_b(red_down, dh2)

    twr = _tile(d, (1024, 512))
    w_tile = pl.BlockSpec((twr, tf), lambda i, j, k: (i, j))
    gw_gate, gw_up = _mm(
        "mm_gw_gate_up", (d // twr, d_ff // tf, 1), [h2, dgt, dup],
        [pl.BlockSpec((s, twr), lambda i, j, k: (0, i)), pl.BlockSpec((s, tf), lambda i, j, k: (0, j)),
         pl.BlockSpec((s, tf), lambda i, j, k: (0, j))],
        [jax.ShapeDtypeStruct((d, d_ff), BF16), jax.ShapeDtypeStruct((d, d_ff), BF16)], [w_tile, w_tile],
        [(0, 1, TN, 0), (0, 2, TN, 1)], 2, (twr, tf), 1, ep_store,
        _mm_vmem([((s, twr), BF16, 3), ((s, tf), BF16, 4), ((twr, tf), F32, 6)]),
        after=(red_down["token"],),
    )
    red_ffn = reduce_a("ffn_in", [W_GATE, W_UP], [gw_gate, gw_up])

    dx2, dx2b, dg_ffn = _rms_bwd("rms_ffn_bwd", x2, norm_ffn, dh2, dx3, after=(red_ffn["token"],))

    nj = d // tg

    def lo(j):
        return jnp.minimum(j, nj - 1)

    def gate_bwd_body(dx_ref, wo_ref, ga_ref, gb_ref, ya_ref, yb_ref, dya_ref, dyb_ref, dz_ref, keep):
        j = pl.program_id(1)

        @pl.when(j < nj)
        def _():
            dm = lax.dot_general(dx_ref[...], wo_ref[...], NT, preferred_element_type=F32)
            sa, sb = _sigmoid(ga_ref[...].astype(F32)), _sigmoid(gb_ref[...].astype(F32))
            dya_ref[...] = (dm * sa).astype(BF16)
            dyb_ref[...] = (dm * sb).astype(BF16)
            dz_ref[...] = (dm * ya_ref[...].astype(F32) * (sa * (1.0 - sa))).astype(BF16)
            keep[lo(j)] = (dm * yb_ref[...].astype(F32) * (sb * (1.0 - sb))).astype(BF16)

        @pl.when(j >= nj)
        def _():
            dz_ref[...] = keep[jnp.maximum(j - nj, 0)]

    t_lo = pl.BlockSpec((tm, tg), lambda i, j: (i, lo(j)))
    dya, dyb, dz = _pallas(
        gate_bwd_body,
        name="mm_dgate",
        grid=(s // tm, 2 * nj),
        in_specs=[
            pl.BlockSpec((tm, d), lambda i, j: (i, 0)),
            pl.BlockSpec((tg, d), lambda i, j: (lo(j), 0)),
            pl.BlockSpec((tm, tg), lambda i, j: (i, ga0 + lo(j))),
            pl.BlockSpec((tm, tg), lambda i, j: (i, gb0 + lo(j))),
            t_lo,
            t_lo,
        ],
        out_specs=[t_lo, t_lo, pl.BlockSpec((tm, tg), lambda i, j: (i, ga0 + j))],
        out_shape=[jax.ShapeDtypeStruct((s, d), BF16), jax.ShapeDtypeStruct((s, d), BF16), jax.ShapeDtypeStruct((s, n_in), BF16)],
        scratch_shapes=[pltpu.VMEM((nj, tm, tg), BF16)],
        compiler_params=_params(_mm_vmem([((tm, d), BF16, 2), ((tg, d), BF16, 2), ((tm, tg), F32, 14), ((nj, tm, tg), BF16, 1)])),
    )(dx2b, g_o, z, z, y_a, y_b)
    reduce_b(red_ffn, dya)
    reduce_c(red_down, red_ffn["token"])

    gw_o = _mm(
        "mm_gw_o", (d // twr, d // twn, 1), [m_act, dx2b],
        [pl.BlockSpec((s, twr), lambda i, j, k: (0, i)), pl.BlockSpec((s, twn), lambda i, j, k: (0, j))],
        [jax.ShapeDtypeStruct((d, d), BF16)], [pl.BlockSpec((twr, twn), lambda i, j, k: (i, j))],
        [(0, 1, TN, 0)], 1, (twr, twn), 1, ep_store,
        _mm_vmem([((s, twr), BF16, 3), ((s, twn), BF16, 2), ((twr, twn), F32, 3)]),
        after=(red_down["token"],),
    )[0]
    after_down = finish(red_down, gw_o)

    tb = _tile(w_sgu, (1024, 512))
    b_out = pl.BlockSpec((tm, tb), lambda i, j, k: (i, j))

    da, datt = _mm(
        "mm_dbranches", (s // tm, w_sgu // tb, 1), [dya, g_a, dyb, g_b],
        [pl.BlockSpec((tm, d), lambda i, j, k: (i, 0)), pl.BlockSpec((tb, d), lambda i, j, k: (j, 0)),
         pl.BlockSpec((tm, d), lambda i, j, k: (i, 0)), pl.BlockSpec((tb, d), lambda i, j, k: (j, 0))],
        [jax.ShapeDtypeStruct((s, w_sgu), BF16), jax.ShapeDtypeStruct((s, w_att), BF16)], [b_out, b_out],
        [(0, 1, NT, 0), (2, 3, NT, 1)], 2, (tm, tb), 1, ep_store,
        _mm_vmem([((tm, d), BF16, 4), ((tb, d), BF16, 4), ((tm, tb), F32, 6)]),
        after=(after_down,),
    )

    wb_tile = pl.BlockSpec((tb, twn), lambda i, j, k: (i, j))
    gw_a, gw_b = _mm(
        "mm_gw_branches", (w_sgu // tb, d // twn, 1), [a_act, dya, att, dyb],
        [pl.BlockSpec((s, tb), lambda i, j, k: (0, i)), pl.BlockSpec((s, twn), lambda i, j, k: (0, j)),
         pl.BlockSpec((s, tb), lambda i, j, k: (0, i)), pl.BlockSpec((s, twn), lambda i, j, k: (0, j))],
        [jax.ShapeDtypeStruct((w_sgu, d), BF16), jax.ShapeDtypeStruct((w_att, d), BF16)], [wb_tile, wb_tile],
        [(0, 1, TN, 0), (2, 3, TN, 1)], 2, (tb, twn), 1, ep_store,
        _mm_vmem([((s, tb), BF16, 5), ((s, twn), BF16, 4), ((tb, twn), F32, 6)]),
        after=(da,),
    )
    red_mix = reduce_a("mix", [W_O, W_A, W_B], [gw_o, gw_a, gw_b])

    dz, dws, dbs, dgain = _sgu_bwd(z, da, sgu_v_gain, ws_b, wst_b, b_col, w_sgu, dz, after=(red_mix["token"],))
    dz, dk_pad, dv_pad, dbias_tab, dsink = _attn_bwd(sink, z, k_pad, v_pad, bias_tab, datt, dz, grp, q_blk0)
    dz = _dkv_to_dz(dk_pad, dv_pad, dz, off_k // (2 * w_kv))
    drel = _relbias_bwd(dbias_tab, bucket)
    reduce_b(red_mix, dz)
    reduce_c(red_ffn, red_mix["token"])

    early = [dgain, dws, dbs, dsink[:, 0], drel[:, :REL_BUCKETS].T, dg_ffn, dg_final]
    p_early = _pack([g.reshape(shp) for g, shp in zip(early, small_shapes[1:])] + [loss_part[0, :1]])
    land = jnp.zeros((2 * N_CHIPS,) + p_early.shape, F32)
    sm_send, sm_recv, (p_early, land) = _split_start("small_send", [p_early, land], 2 * N_CHIPS - 1, _small_exchange_copies(False))

    tzn = _tile(n_in, (768, 640, 512))
    gw_in = _mm(
        "mm_gw_in", (d // twr, n_in // tzn, 1), [h1, dz],
        [pl.BlockSpec((s, twr), lambda i, j, k: (0, i)), pl.BlockSpec((s, tzn), lambda i, j, k: (0, j))],
        [jax.ShapeDtypeStruct((d, n_in), BF16)], [pl.BlockSpec((twr, tzn), lambda i, j, k: (i, j))],
        [(0, 1, TN, 0)], 1, (twr, tzn), 1, ep_store,
        _mm_vmem([((s, twr), BF16, 3), ((s, tzn), BF16, 2), ((twr, tzn), F32, 3)]),
        after=(red_ffn["token"], p_early),
    )[0]
    red_in = reduce_a("w_in", [W_IN], [gw_in])

    reduce_c(red_mix, red_in["token"])
    reduce_b(red_in, finish(red_mix, red_in["token"]))

    dh1 = _mm(
        "mm_dh1", (s // tm, d // tn2, 1), [dz, g_in],
        [pl.BlockSpec((tm, n_in), lambda i, j, k: (i, 0)), pl.BlockSpec((tn2, n_in), lambda i, j, k: (j, 0))],
        [jax.ShapeDtypeStruct((s, d), F32)], [pl.BlockSpec((tm, tn2), lambda i, j, k: (i, j))],
        [(0, 1, NT, 0)], 1, (tm, tn2), 1, ep_store,
        _mm_vmem([((tm, n_in), BF16, 2), ((tn2, n_in), BF16, 2), ((tm, tn2), F32, 5)]),
        after=(red_in["token"],),
    )[0]

    grad_x, _, dg_mix = _rms_bwd("rms_mix_bwd", x2d, norm_mix, dh1, dx2)

    p_mix = _pack([dg_mix.reshape(small_shapes[0])])
    land_mix = jnp.zeros((2 * N_CHIPS,) + p_mix.shape, F32)
    mx_send, mx_recv, (p_mix, land_mix) = _split_start("mix_send", [p_mix, land_mix], 2 * N_CHIPS - 1, _small_exchange_copies(False))

    reduce_c(red_in, finish(red_ffn, p_mix))
    p_early, land = _split_wait("small_wait", [p_early, land], sm_send, sm_recv, _small_exchange_copies(True), red_in["token"])
    p_mix, land_mix = _split_wait("mix_wait", [p_mix, land_mix], mx_send, mx_recv, _small_exchange_copies(True), p_early)
    me_idx = 2 * q_idx + c_idx
    packed_g = jnp.concatenate([_small_sum("mix_sum", me_idx, p_mix, land_mix), _small_sum("small_sum", me_idx, p_early, land)], axis=0)
    g_small = _unpack(packed_g, small_shapes + [(1,)])
    loss = g_small[-1].reshape(())
    g_small = g_small[:-1]
    pg = _pack(g_small + [zero1])
    small_upd = _adamw("adamw_small", pw, pg, pm, pv)
    d_small, nm_small, nv_small = [_unpack(a, small_shapes) for a in small_upd[:3]]
    finish(red_in, small_upd[0])

    small_names = ["norm_mix", "sgu_v_gain", "sgu_w_s", "sgu_b_s", "attn_sink", "rel_bias", "norm_ffn", "norm_final"]
    table = {}
    for i, n in enumerate(names):
        table[n] = (grads_big[i][None], upd[i][0][None], upd[i][1][None], upd[i][2][None])
    for i, n in enumerate(small_names):
        table[n] = (g_small[i], d_small[i], nm_small[i], nv_small[i])
    order = ["w_in", "norm_mix", "sgu_v_gain", "sgu_w_s", "sgu_b_s", "w_a", "attn_sink", "rel_bias", "w_b", "w_o", "norm_ffn",
             "w_gate", "w_up", "w_down", "norm_final"]
    outs = [loss, grad_x.reshape(1, s, d)]
    for part in range(4):
        outs += [table[n][part] for n in order]
    return tuple(outs)
```

```python
import math

import jax
import jax.numpy as jnp
import numpy as np
from jax import lax
from jax.experimental import pallas as pl
from jax.experimental.pallas import tpu as pltpu

F32 = jnp.float32
BF16 = jnp.bfloat16
I32 = jnp.int32
MESH = pl.DeviceIdType.MESH

EPS = 1e-6
NEG = -1e30
BLK = 128
HEAD_DIM = 128
N_KV_HEADS = 2
REL_BUCKETS = 32
REL_MAX_DIST = 128
N_CHIPS = 4
ADAM_LR, ADAM_B1, ADAM_B2, ADAM_EPS, ADAM_WD, ADAM_STEP = 0.001, 0.9, 0.999, 1e-08, 0.01, 10

LANES = 128
VMEM_CAP = 60 * 1024 * 1024

NN = (((1,), (0,)), ((), ()))
NT = (((1,), (1,)), ((), ()))
TN = (((0,), (0,)), ((), ()))
ANY = pl.BlockSpec(memory_space=pl.ANY)
HBM_SPEC = pl.BlockSpec(memory_space=pltpu.HBM)
SEM_SPEC = pl.BlockSpec(memory_space=pltpu.SEMAPHORE)
EFFECT = pltpu.SideEffectType.DATAFLOW_SIDE_EFFECTING


def _tile(n, cands):
    for t in cands:
        if n % t == 0:
            return t
    return n


PIN_BYTES = 64 * 1024


def _pin_hbm(a):
    big = hasattr(a, "dtype") and jnp.issubdtype(a.dtype, jnp.floating) and _nbytes(a.shape, a.dtype) >= PIN_BYTES
    return pltpu.with_memory_space_constraint(a, pltpu.HBM) if big else a


def _pallas(body, *, out_shape, **kw):
    def pin(o):
        big = isinstance(o, jax.ShapeDtypeStruct) and jnp.issubdtype(o.dtype, jnp.floating) and _nbytes(o.shape, o.dtype) >= PIN_BYTES
        return pltpu.HBM(o.shape, o.dtype) if big else o

    shapes = type(out_shape)(pin(o) for o in out_shape) if isinstance(out_shape, (list, tuple)) else pin(out_shape)
    call = pl.pallas_call(body, out_shape=shapes, **kw)
    return lambda *args: call(*[_pin_hbm(a) for a in args])


def _params(vmem_bytes=None, **kw):
    if vmem_bytes is not None:
        kw["vmem_limit_bytes"] = int(min(max(vmem_bytes, 32 * 1024 * 1024), VMEM_CAP))
    return pltpu.CompilerParams(**kw)


def _nbytes(shape, dtype):
    return int(np.prod(shape)) * jnp.dtype(dtype).itemsize


def _sigmoid(x):
    return 1.0 / (1.0 + jnp.exp(-x))


_GC = 0.7978845608028654
_GA = 0.044715


def _gelu(x):
    return 0.5 * x * (1.0 + jnp.tanh(_GC * (x + _GA * (x * x * x))))


def _gelu_grad(x):
    t = jnp.tanh(_GC * (x + _GA * (x * x * x)))
    return 0.5 * (1.0 + t) + 0.5 * x * (1.0 - t * t) * (_GC * (1.0 + 3.0 * _GA * (x * x)))


def _bf(v):
    return v if v.dtype == BF16 else v.astype(BF16)


def _mm(name, grid, ins, in_specs, out_shape, out_specs, pairs, n_acc, tile, nk, epilogue, vmem_bytes, after=()):
    assert nk == 1
    n_in, n_out = len(ins) + len(after), len(out_shape)

    def body(*refs):
        in_refs, out_refs = refs[:n_in], refs[n_in : n_in + n_out]
        vals = [None] * n_acc
        for a_i, b_i, dn, acc_i in pairs:
            d = lax.dot_general(_bf(in_refs[a_i][...]), _bf(in_refs[b_i][...]), dn, preferred_element_type=F32)
            vals[acc_i] = d if vals[acc_i] is None else vals[acc_i] + d
        epilogue(in_refs, vals, out_refs, slice(None))

    return _pallas(
        body,
        name=name,
        grid=grid,
        in_specs=list(in_specs) + [ANY] * len(after),
        out_specs=out_specs,
        out_shape=out_shape,
        compiler_params=_params(vmem_bytes),
    )(*ins, *after)


def _put(ref, cs, v):
    ref[:, cs] = v.astype(ref.dtype)


def _mm_vmem(tiles):
    return sum(_nbytes(s, d) * c for s, d, c in tiles) + 4 * 1024 * 1024


def _rows8(v):
    r, d = v.shape
    return v.reshape(r // 8, 8, d).sum(axis=0)


def _rms_fwd(name, x, g, after=()):
    s, d = x.shape
    tm = _tile(s, (256, 128))

    def body(x_ref, g_ref, *rest):
        h_ref = rest[-1]
        xv = x_ref[...]
        r = lax.rsqrt(jnp.mean(xv * xv, axis=-1, keepdims=True) + EPS)
        h_ref[...] = ((xv * r) * g_ref[...]).astype(BF16)

    return _pallas(
        body,
        name=name,
        grid=(s // tm,),
        in_specs=[pl.BlockSpec((tm, d), lambda i: (i, 0)), pl.BlockSpec((1, d), lambda i: (0, 0))] + [ANY] * len(after),
        out_specs=pl.BlockSpec((tm, d), lambda i: (i, 0)),
        out_shape=jax.ShapeDtypeStruct((s, d), BF16),
    )(x, g, *after)


def _rms_bwd(name, x, g, dh, dres, after=()):
    s, d = x.shape
    tm = _tile(s, (256, 128))
    n = s // tm
    n_after = len(after)

    def body(x_ref, g_ref, dh_ref, dres_ref, *rest):
        dx_ref, dxb_ref, dg_ref, acc_ref = rest[n_after:]
        i = pl.program_id(0)
        xv = x_ref[...]
        r = lax.rsqrt(jnp.mean(xv * xv, axis=-1, keepdims=True) + EPS)
        xh = xv * r
        dhv = dh_ref[...]
        dxh = dhv * g_ref[...]
        dx = r * (dxh - xh * jnp.mean(dxh * xh, axis=-1, keepdims=True)) + dres_ref[...]
        dx_ref[...] = dx
        dxb_ref[...] = dx.astype(BF16)
        part = _rows8(dhv * xh)

        @pl.when(i == 0)
        def _():
            acc_ref[...] = part

        @pl.when(i > 0)
        def _():
            acc_ref[...] += part

        @pl.when(i == n - 1)
        def _():
            dg_ref[...] = jnp.sum(acc_ref[...], axis=0, keepdims=True)

    row = pl.BlockSpec((tm, d), lambda i: (i, 0))
    vec = pl.BlockSpec((1, d), lambda i: (0, 0))
    return _pallas(
        body,
        name=name,
        grid=(n,),
        in_specs=[row, vec, row, row] + [ANY] * n_after,
        out_specs=[row, row, vec],
        out_shape=[jax.ShapeDtypeStruct((s, d), F32), jax.ShapeDtypeStruct((s, d), BF16), jax.ShapeDtypeStruct((1, d), F32)],
        scratch_shapes=[pltpu.VMEM((8, d), F32)],
    )(x, g, dh, dres, *after)


def _head(x3, g, target):
    s, d = x3.shape
    tm = _tile(s, (256, 128))
    n = s // tm

    def body(x_ref, g_ref, t_ref, dx_ref, dxb_ref, dg_ref, loss_ref, acc_g, acc_l):
        i = pl.program_id(0)
        xv = x_ref[...]
        gv = g_ref[...]
        r = lax.rsqrt(jnp.mean(xv * xv, axis=-1, keepdims=True) + EPS)
        xh = xv * r
        e = xh * gv - t_ref[...]
        dy = e * (1.0 / d)
        dxh = dy * gv
        dx = r * (dxh - xh * jnp.mean(dxh * xh, axis=-1, keepdims=True))
        dx_ref[...] = dx
        dxb_ref[...] = dx.astype(BF16)
        pg = _rows8(dy * xh)
        plo = _rows8(e * e)

        @pl.when(i == 0)
        def _():
            acc_g[...] = pg
            acc_l[...] = plo

        @pl.when(i > 0)
        def _():
            acc_g[...] += pg
            acc_l[...] += plo

        @pl.when(i == n - 1)
        def _():
            dg_ref[...] = jnp.sum(acc_g[...], axis=0, keepdims=True)
            loss_ref[...] = jnp.full((1, LANES), (0.5 / d) * jnp.sum(acc_l[...]), F32)

    row = pl.BlockSpec((tm, d), lambda i: (i, 0))
    vec = pl.BlockSpec((1, d), lambda i: (0, 0))
    return _pallas(
        body,
        name="head",
        grid=(n,),
        in_specs=[row, vec, row],
        out_specs=[row, row, vec, pl.BlockSpec((1, LANES), lambda i: (0, 0))],
        out_shape=[
            jax.ShapeDtypeStruct((s, d), F32),
            jax.ShapeDtypeStruct((s, d), BF16),
            jax.ShapeDtypeStruct((1, d), F32),
            jax.ShapeDtypeStruct((1, LANES), F32),
        ],
        scratch_shapes=[pltpu.VMEM((8, d), F32), pltpu.VMEM((8, d), F32)],
    )(x3, g, target)


def _sgu_fwd(z, gain, ws_b, b_col, w_sgu, after=()):
    s = z.shape[0]
    groups = ws_b.shape[0]

    def body(zu_ref, zv_ref, gain_ref, ws_ref, b_ref, *rest):
        a_ref = rest[-1]
        vv = _gelu(zv_ref[...].astype(F32))
        r = lax.rsqrt(jnp.mean(vv * vv, axis=-1, keepdims=True) + EPS)
        vn = ((vv * r) * gain_ref[...]).astype(BF16)
        u = _gelu(zu_ref[...].astype(F32))
        for g in range(groups):
            sl = slice(g * BLK, (g + 1) * BLK)
            mixed = jnp.dot(ws_ref[g], vn[:, sl], preferred_element_type=F32) + b_ref[g]
            a_ref[:, sl] = (u[:, sl] * mixed).astype(BF16)

    return _pallas(
        body,
        name="sgu_fwd",
        grid=(s // BLK,),
        in_specs=[
            pl.BlockSpec((BLK, w_sgu), lambda c: (c, 0)),
            pl.BlockSpec((BLK, w_sgu), lambda c: (c, 1)),
            pl.BlockSpec((1, w_sgu), lambda c: (0, 0)),
            pl.BlockSpec((groups, BLK, BLK), lambda c: (0, 0, 0)),
            pl.BlockSpec((groups, BLK, 1), lambda c: (0, 0, 0)),
        ]
        + [ANY] * len(after),
        out_specs=pl.BlockSpec((BLK, w_sgu), lambda c: (c, 0)),
        out_shape=jax.ShapeDtypeStruct((s, w_sgu), BF16),
    )(z, z, gain, ws_b, b_col, *after)


def _sgu_bwd(z, da, gain, ws_b, wst_b, b_col, w_sgu, dz, after=()):
    s = z.shape[0]
    groups = ws_b.shape[0]
    n = s // BLK
    n_skip = 1 + len(after)

    def body(zu_ref, zv_ref, da_ref, gain_ref, ws_ref, wst_ref, b_ref, *rest):
        dz_ref, dws_ref, dbs_ref, dgain_ref, acc_gain, vv_s, gv_s, dxh_s = rest[n_skip:]
        c = pl.program_id(0)
        cols = [slice(g * BLK, (g + 1) * BLK) for g in range(groups)]

        ss = jnp.zeros((BLK, 1), F32)
        for sl in cols:
            zv = zv_ref[:, sl].astype(F32)
            vv = _gelu(zv)
            vv_s[:, sl] = vv
            gv_s[:, sl] = _gelu_grad(zv)
            ss = ss + jnp.sum(vv * vv, axis=-1, keepdims=True)
        r = lax.rsqrt(ss * (1.0 / w_sgu) + EPS)

        dot_dx = jnp.zeros((BLK, 1), F32)
        for g, sl in enumerate(cols):
            gain_g = gain_ref[:, sl]
            xh = vv_s[:, sl] * r
            vn = (xh * gain_g).astype(BF16)
            zu = zu_ref[:, sl].astype(F32)
            dav = da_ref[:, sl].astype(F32)
            dmix = dav * _gelu(zu)
            dmix_b = dmix.astype(BF16)
            mixed = jnp.dot(ws_ref[g], vn, preferred_element_type=F32) + b_ref[g]
            dz_ref[:, sl] = (dav * mixed * _gelu_grad(zu)).astype(BF16)
            dvn = jnp.dot(wst_ref[g], dmix_b, preferred_element_type=F32)
            dws_g = lax.dot_general(dmix_b, vn, NT, preferred_element_type=F32)
            dbs_g = jnp.sum(dmix, axis=1, keepdims=True)
            pg = _rows8(dvn * xh)

            @pl.when(c == 0)
            def _():
                dws_ref[g] = dws_g
                dbs_ref[g] = dbs_g
                acc_gain[:, sl] = pg

            @pl.when(c > 0)
            def _():
                dws_ref[g] += dws_g
                dbs_ref[g] += dbs_g
                acc_gain[:, sl] += pg

            dxh = dvn * gain_g
            dxh_s[:, sl] = dxh
            dot_dx = dot_dx + jnp.sum(dxh * xh, axis=-1, keepdims=True)

        mean_dx = dot_dx * (1.0 / w_sgu)
        for g, sl in enumerate(cols):
            dvv = r * (dxh_s[:, sl] - (vv_s[:, sl] * r) * mean_dx)
            dz_ref[:, w_sgu + g * BLK : w_sgu + (g + 1) * BLK] = (dvv * gv_s[:, sl]).astype(BF16)

        @pl.when(c == n - 1)
        def _():
            dgain_ref[...] = jnp.sum(acc_gain[...], axis=0, keepdims=True)

    full3 = pl.BlockSpec((groups, BLK, BLK), lambda c: (0, 0, 0))
    col3 = pl.BlockSpec((groups, BLK, 1), lambda c: (0, 0, 0))
    vec = pl.BlockSpec((1, w_sgu), lambda c: (0, 0))
    return _pallas(
        body,
        name="sgu_bwd",
        grid=(n,),
        in_specs=[
            pl.BlockSpec((BLK, w_sgu), lambda c: (c, 0)),
            pl.BlockSpec((BLK, w_sgu), lambda c: (c, 1)),
            pl.BlockSpec((BLK, w_sgu), lambda c: (c, 0)),
            vec,
            full3,
            full3,
            col3,
            ANY,
        ]
        + [ANY] * len(after),
        out_specs=[pl.BlockSpec((BLK, 2 * w_sgu), lambda c: (c, 0)), full3, col3, vec],
        out_shape=[
            jax.ShapeDtypeStruct(dz.shape, BF16),
            jax.ShapeDtypeStruct((groups, BLK, BLK), F32),
            jax.ShapeDtypeStruct((groups, BLK, 1), F32),
            jax.ShapeDtypeStruct((1, w_sgu), F32),
        ],
        scratch_shapes=[pltpu.VMEM((8, w_sgu), F32)] + [pltpu.VMEM((BLK, w_sgu), F32)] * 3,
        input_output_aliases={7: 0},
    )(z, z, da, gain, ws_b, wst_b, b_col, dz, *after)


def _attn_softmax(sink_ref, q_ref, k_ref, v_ref, bias_ref, s_len, grp):
    kv = pl.program_id(0)
    n = pl.program_id(1)
    start = pl.multiple_of(n * BLK, BLK)
    kb = k_ref[pl.ds(start, 3 * BLK), :]
    vb = v_ref[pl.ds(start, 3 * BLK), :]
    qv = q_ref[...]
    qs = jnp.concatenate([qv[:, g * HEAD_DIM : (g + 1) * HEAD_DIM] for g in range(grp)], axis=0).astype(BF16)
    sc = lax.dot_general(qs, kb, NT, preferred_element_type=F32) * (HEAD_DIM**-0.5)
    sc = sc + bias_ref[...].reshape(grp * BLK, 3 * BLK)
    kpos = start + lax.broadcasted_iota(I32, (1, 3 * BLK), 1) - BLK
    sc = jnp.where((kpos >= 0) & (kpos < s_len), sc, NEG)
    sink = jnp.concatenate([jnp.full((BLK, 1), sink_ref[kv * grp + g], F32) for g in range(grp)], axis=0)
    m = jnp.maximum(jnp.max(sc, axis=-1, keepdims=True), sink)
    p = jnp.exp(sc - m)
    esink = jnp.exp(sink - m)
    den = jnp.sum(p, axis=-1, keepdims=True) + esink
    return start, qs, kb, vb, p / den, esink / den


def _attn_specs(s, grp, q_blk0):
    qw = grp * HEAD_DIM
    return [
        pl.BlockSpec(memory_space=pltpu.SMEM),
        pl.BlockSpec((BLK, qw), lambda kv, n: (n, q_blk0 + kv)),
        pl.BlockSpec((s + 2 * BLK, HEAD_DIM), lambda kv, n: (0, kv)),
        pl.BlockSpec((s + 2 * BLK, HEAD_DIM), lambda kv, n: (0, kv)),
        pl.BlockSpec((grp, BLK, 3 * BLK), lambda kv, n: (kv, 0, 0)),
    ]


def _attn_fwd(sink, z, k_pad, v_pad, bias_tab, grp, q_blk0):
    s = z.shape[0]
    qw = grp * HEAD_DIM

    def body(sink_ref, q_ref, k_ref, v_ref, bias_ref, o_ref):
        _, _, _, vb, pn, _ = _attn_softmax(sink_ref, q_ref, k_ref, v_ref, bias_ref, s, grp)
        o = jnp.dot(pn.astype(BF16), vb, preferred_element_type=F32)
        for g in range(grp):
            o_ref[:, g * HEAD_DIM : (g + 1) * HEAD_DIM] = o[g * BLK : (g + 1) * BLK].astype(BF16)

    return _pallas(
        body,
        name="attn_fwd",
        grid=(N_KV_HEADS, s // BLK),
        in_specs=_attn_specs(s, grp, q_blk0),
        out_specs=pl.BlockSpec((BLK, qw), lambda kv, n: (n, kv)),
        out_shape=jax.ShapeDtypeStruct((s, N_KV_HEADS * qw), BF16),
    )(sink, z, k_pad, v_pad, bias_tab)


def _attn_bwd(sink, z, k_pad, v_pad, bias_tab, dout, dz, grp, q_blk0):
    s = z.shape[0]
    qw = grp * HEAD_DIM
    nb = s // BLK
    heads = N_KV_HEADS * grp

    def body(sink_ref, q_ref, k_ref, v_ref, bias_ref, do_ref, dz_in, dq_ref, dk_ref, dv_ref, dbias_ref, dsink_ref, dk_acc, dv_acc):
        del dz_in
        kv = pl.program_id(0)
        n = pl.program_id(1)
        start, qs, kb, vb, pn, psink = _attn_softmax(sink_ref, q_ref, k_ref, v_ref, bias_ref, s, grp)
        dov = do_ref[...]
        dos = jnp.concatenate([dov[:, g * HEAD_DIM : (g + 1) * HEAD_DIM] for g in range(grp)], axis=0)
        dp = lax.dot_general(dos, vb, NT, preferred_element_type=F32)
        dvb = lax.dot_general(pn.astype(BF16), dos, TN, preferred_element_type=F32)
        delta = jnp.sum(pn * dp, axis=-1, keepdims=True)
        ds = pn * (dp - delta)
        dsb = (ds * (HEAD_DIM**-0.5)).astype(BF16)
        dq = jnp.dot(dsb, kb, preferred_element_type=F32)
        dkb = lax.dot_general(dsb, qs, TN, preferred_element_type=F32)
        for g in range(grp):
            dq_ref[:, g * HEAD_DIM : (g + 1) * HEAD_DIM] = dq[g * BLK : (g + 1) * BLK].astype(BF16)

        @pl.when(n == 0)
        def _():
            dk_acc[...] = jnp.zeros_like(dk_acc)
            dv_acc[...] = jnp.zeros_like(dv_acc)
            dbias_ref[...] = jnp.zeros_like(dbias_ref)

        @pl.when((n == 0) & (kv == 0))
        def _():
            dsink_ref[...] = jnp.zeros_like(dsink_ref)

        dk_acc[pl.ds(start, 3 * BLK), :] += dkb
        dv_acc[pl.ds(start, 3 * BLK), :] += dvb
        dbias_ref[...] += ds.reshape(grp, BLK, 3 * BLK)
        row = lax.broadcasted_iota(I32, (heads, LANES), 0)
        sd = psink * delta
        upd = jnp.zeros((heads, LANES), F32)
        for g in range(grp):
            upd = jnp.where(row == kv * grp + g, -jnp.sum(sd[g * BLK : (g + 1) * BLK]), upd)
        dsink_ref[...] += upd

        @pl.when(n == nb - 1)
        def _():
            dk_ref[...] = dk_acc[...]
            dv_ref[...] = dv_acc[...]

    pad_spec = pl.BlockSpec((s + 2 * BLK, HEAD_DIM), lambda kv, n: (0, kv))
    kvw = N_KV_HEADS * HEAD_DIM
    return _pallas(
        body,
        name="attn_bwd",
        grid=(N_KV_HEADS, nb),
        in_specs=_attn_specs(s, grp, q_blk0) + [pl.BlockSpec((BLK, qw), lambda kv, n: (n, kv)), ANY],
        out_specs=[
            pl.BlockSpec((BLK, qw), lambda kv, n: (n, q_blk0 + kv)),
            pad_spec,
            pad_spec,
            pl.BlockSpec((grp, BLK, 3 * BLK), lambda kv, n: (kv, 0, 0)),
            pl.BlockSpec((heads, LANES), lambda kv, n: (0, 0)),
        ],
        out_shape=[
            jax.ShapeDtypeStruct(dz.shape, BF16),
            jax.ShapeDtypeStruct((s + 2 * BLK, kvw), F32),
            jax.ShapeDtypeStruct((s + 2 * BLK, kvw), F32),
            jax.ShapeDtypeStruct((heads, BLK, 3 * BLK), F32),
            jax.ShapeDtypeStruct((heads, LANES), F32),
        ],
        scratch_shapes=[pltpu.VMEM((s + 2 * BLK, HEAD_DIM), F32), pltpu.VMEM((s + 2 * BLK, HEAD_DIM), F32)],
        input_output_aliases={6: 0},
    )(sink, z, k_pad, v_pad, bias_tab, dout, dz)


def _dkv_to_dz(dk_pad, dv_pad, dz, blk_idx):
    s = dz.shape[0]
    kvw = dk_pad.shape[1]

    def body(dk_ref, dv_ref, dz_in, out_ref):
        del dz_in
        out_ref[:, :kvw] = dk_ref[...].astype(BF16)
        out_ref[:, kvw:] = dv_ref[...].astype(BF16)

    src = pl.BlockSpec((BLK, kvw), lambda i: (i + 1, 0))
    return _pallas(
        body,
        name="dkv_to_dz",
        grid=(s // BLK,),
        in_specs=[src, src, ANY],
        out_specs=pl.BlockSpec((BLK, 2 * kvw), lambda i: (i, blk_idx)),
        out_shape=jax.ShapeDtypeStruct(dz.shape, BF16),
        input_output_aliases={2: 0},
    )(dk_pad, dv_pad, dz)


def _relbias_bwd(dbias_tab, bucket):
    heads = dbias_tab.shape[0]

    def body(dt_ref, bk_ref, out_ref):
        lane = lax.broadcasted_iota(I32, (1, LANES), 1)
        bk = bk_ref[...]
        rows = []
        for h in range(heads):
            dt = dt_ref[h]
            acc = jnp.zeros((1, LANES), F32)
            for b in range(REL_BUCKETS):
                acc = jnp.where(lane == b, jnp.sum(jnp.where(bk == b, dt, 0.0)), acc)
            rows.append(acc)
        out_ref[...] = jnp.concatenate(rows, axis=0)

    return _pallas(body, name="relbias_bwd", out_shape=jax.ShapeDtypeStruct((heads, LANES), F32))(dbias_tab, bucket)


def _t5_bucket(rel):
    nb = REL_BUCKETS // 2
    ret = jnp.where(rel > 0, nb, 0)
    n = jnp.abs(rel)
    max_exact = nb // 2
    nf = jnp.maximum(n, 1).astype(F32)
    large = max_exact + (jnp.log(nf / max_exact) / math.log(REL_MAX_DIST / max_exact) * (nb - max_exact)).astype(I32)
    large = jnp.minimum(large, nb - 1)
    return ret + jnp.where(n < max_exact, n, large)


def _band_tables(rel_bias):
    qi = jnp.arange(BLK)[:, None]
    kj = jnp.arange(3 * BLK)[None, :]
    rel = kj - BLK - qi
    bucket = _t5_bucket(rel).astype(I32)
    heads = rel_bias.shape[1]
    masked = jnp.where(jnp.abs(rel) <= BLK, bucket, -1)

    def body(rb_ref, bk_ref, out_ref):
        bk = bk_ref[...]
        for h in range(heads):
            tab = jnp.full(bk.shape, NEG, F32)
            for b in range(REL_BUCKETS):
                tab = jnp.where(bk == b, rb_ref[b, h], tab)
            out_ref[h] = tab

    bias_tab = _pallas(
        body,
        name="bias_table",
        in_specs=[pl.BlockSpec(memory_space=pltpu.SMEM), pl.BlockSpec(memory_space=pltpu.VMEM)],
        out_specs=pl.BlockSpec(memory_space=pltpu.VMEM),
        out_shape=jax.ShapeDtypeStruct((heads, BLK, 3 * BLK), F32),
    )(rel_bias.astype(F32), masked)
    return bias_tab, bucket


EW_BLOCK_ELEMS = 512 * 1024


def _ew_tiles(shape, elems=EW_BLOCK_ELEMS // 2):
    r, c = shape
    tn = c if c <= 2048 else _tile(c, (2048, 1920, 1536, 1408, 1024, 512))
    tm = _tile(r, [t for t in (1024, 512, 256, 128, 64, 32, 16, 8) if t * tn <= elems] or [8])
    return tm, tn


def _cast_into_full(name, qidx, w, kind, after=()):
    r, c = w.shape
    tm, tn = _ew_tiles(w.shape, EW_BLOCK_ELEMS)
    nbi, nbj = r // tm, c // tn
    if kind == "col":
        full, out_spec = (r, c * N_CHIPS), pl.BlockSpec((tm, tn), lambda i, j, q: (i, q[0] * nbj + j))
    else:
        full, out_spec = (r * N_CHIPS, c), pl.BlockSpec((tm, tn), lambda i, j, q: (q[0] * nbi + i, j))

    def body(q_ref, w_ref, *rest):
        del q_ref
        rest[-1][...] = w_ref[...].astype(BF16)

    return _pallas(
        body,
        name=name,
        grid_spec=pltpu.PrefetchScalarGridSpec(
            num_scalar_prefetch=1,
            grid=(nbi, nbj),
            in_specs=[pl.BlockSpec((tm, tn), lambda i, j, q: (i, j))] + [ANY] * len(after),
            out_specs=out_spec,
        ),
        out_shape=jax.ShapeDtypeStruct(full, BF16),
    )(qidx, w, *after)


def _adamw(name, w, g, m, v, after=()):
    tm, tn = _ew_tiles(w.shape, EW_BLOCK_ELEMS)
    if _nbytes(w.shape, F32) <= 1024 * 1024:
        tm, tn = w.shape
    spec = pl.BlockSpec((tm, tn), lambda i, j: (i, j))
    n_after = len(after)

    def body(w_ref, g_ref, m_ref, v_ref, *rest):
        d_ref, nm_ref, nv_ref, g_out_ref = rest[n_after:]
        gv = g_ref[...]
        g_out_ref[...] = gv
        nm = ADAM_B1 * m_ref[...] + (1.0 - ADAM_B1) * gv
        nv = ADAM_B2 * v_ref[...] + (1.0 - ADAM_B2) * (gv * gv)
        m_hat = nm / (1.0 - ADAM_B1**ADAM_STEP)
        v_hat = nv / (1.0 - ADAM_B2**ADAM_STEP)
        d_ref[...] = -ADAM_LR * (m_hat / (jnp.sqrt(v_hat) + ADAM_EPS) + ADAM_WD * w_ref[...])
        nm_ref[...] = nm
        nv_ref[...] = nv

    out = jax.ShapeDtypeStruct(w.shape, F32)
    return _pallas(
        body, name=name, grid=(w.shape[0] // tm, w.shape[1] // tn), in_specs=[spec] * 4 + [ANY] * n_after,
        out_specs=[spec] * 4, out_shape=[out, out, out, out],
        compiler_params=_params(_mm_vmem([((tm, tn), F32, 24)])),
    )(w, g, m, v, *after)


def _pair_add(name, cidx, g_full, r_sib, kind):
    hr, hc = r_sib.shape
    tm, tn = _ew_tiles((hr, hc), 2 * EW_BLOCK_ELEMS)
    nbi, nbj = hr // tm, hc // tn
    if kind == "col":
        g_spec = pl.BlockSpec((tm, tn), lambda i, j, c: (c[0] * nbi + i, j))
    else:
        g_spec = pl.BlockSpec((tm, tn), lambda i, j, c: (i, c[0] * nbj + j))
    spec = pl.BlockSpec((tm, tn), lambda i, j, c: (i, j))

    def body(c_ref, g_ref, r_ref, o_ref):
        del c_ref
        o_ref[...] = (g_ref[...].astype(F32) + r_ref[...].astype(F32)).astype(BF16)

    return _pallas(
        body,
        name=name,
        grid_spec=pltpu.PrefetchScalarGridSpec(num_scalar_prefetch=1, grid=(nbi, nbj), in_specs=[g_spec, spec], out_specs=spec),
        out_shape=jax.ShapeDtypeStruct((hr, hc), BF16),
        compiler_params=_params(_mm_vmem([((tm, tn), BF16, 6), ((tm, tn), F32, 3)])),
    )(cidx, g_full, r_sib)


def _chip_sum(name, qidx, c_half, r_ici, kind):
    _, pr, pc = r_ici.shape
    tm, tn = _ew_tiles((pr, pc), 2 * EW_BLOCK_ELEMS)
    nbi, nbj = pr // tm, pc // tn
    if kind == "col":
        own_spec = pl.BlockSpec((tm, tn), lambda i, j, q: (i, q[0] * nbj + j))
        full, out_spec = (2 * pr, pc), pl.BlockSpec((tm, tn), lambda i, j, q: (q[1] * nbi + i, j))
    else:
        own_spec = pl.BlockSpec((tm, tn), lambda i, j, q: (q[0] * nbi + i, j))
        full, out_spec = (pr, 2 * pc), pl.BlockSpec((tm, tn), lambda i, j, q: (i, q[1] * nbj + j))

    def body(q_ref, own_ref, r_ref, o_ref):
        q = q_ref[0]
        own = own_ref[...].astype(F32)
        recv = [r_ref[r].astype(F32) for r in range(3)]
        total = None
        for chip in range(N_CHIPS):
            d = chip ^ q
            term = jnp.where(d == 0, own, jnp.where(d == 2, recv[0], jnp.where(d == 1, recv[1], recv[2])))
            total = term if total is None else total + term
        o_ref[...] = total

    return _pallas(
        body,
        name=name,
        grid_spec=pltpu.PrefetchScalarGridSpec(
            num_scalar_prefetch=1,
            grid=(nbi, nbj),
            in_specs=[own_spec, pl.BlockSpec((3, tm, tn), lambda i, j, q: (0, i, j))],
            out_specs=out_spec,
        ),
        out_shape=jax.ShapeDtypeStruct(full, F32),
        compiler_params=_params(_mm_vmem([((tm, tn), BF16, 8), ((tm, tn), F32, 6)])),
    )(qidx, c_half, r_ici)


_REL_MASK = (2, 1, 3)


def _place():
    x, y, c = lax.axis_index("x"), lax.axis_index("y"), lax.axis_index("c")
    chips = [(1 - x, y), (x, 1 - y), (1 - x, 1 - y)]
    return x, y, c, 2 * x + y, chips


def _shard_view(ref, kind, chip):
    if kind == "col":
        w = ref.shape[1] // N_CHIPS
        return ref.at[:, pl.ds(pl.multiple_of(chip * w, LANES), w)]
    h = ref.shape[0] // N_CHIPS
    return ref.at[pl.ds(pl.multiple_of(chip * h, 16), h), :]


def _row_half(ref, half):
    h = ref.shape[0] // 2
    return ref.at[pl.ds(pl.multiple_of(half * h, 16), h), :]


def _pair_half(ref, kind, half):
    if kind == "col":
        return _row_half(ref, half)
    w = ref.shape[1] // 2
    return ref.at[:, pl.ds(pl.multiple_of(half * w, LANES), w)]


def _remote(src, dst, send_sem, recv_sem, dev):
    return pltpu.make_async_remote_copy(src_ref=src, dst_ref=dst, send_sem=send_sem, recv_sem=recv_sem, device_id=dev, device_id_type=MESH)


def _hbm(a):
    return pltpu.with_memory_space_constraint(a, pltpu.HBM)


def _gather_start(name, fulls, kinds, rels=(0, 1, 2), after=()):
    n_w = len(fulls)

    def body(*refs):
        g = refs[:n_w]
        send_sem, recv_sem = refs[n_w + len(after)], refs[n_w + len(after) + 1]
        token = refs[-1]
        _, _, c, q, chips = _place()
        for w in range(n_w):
            mine = _row_half(_shard_view(g[w], kinds[w], q), c)
            for r in rels if isinstance(rels, tuple) else rels[w]:
                _remote(mine, mine, send_sem.at[3 * w + r], recv_sem.at[3 * w + r], (*chips[r], c)).start()
        token[...] = jnp.zeros_like(token)

    res = _pallas(
        body,
        name=name,
        out_shape=(
            pltpu.SemaphoreType.DMA((3 * n_w,)),
            pltpu.SemaphoreType.DMA((3 * n_w,)),
            *[pltpu.HBM(f.shape, f.dtype) for f in fulls],
            jax.ShapeDtypeStruct((8, LANES), F32),
        ),
        in_specs=[HBM_SPEC] * n_w + [ANY] * len(after),
        out_specs=(SEM_SPEC, SEM_SPEC, *[HBM_SPEC] * n_w, pl.BlockSpec(memory_space=pltpu.VMEM)),
        input_output_aliases={w: w + 2 for w in range(n_w)},
        compiler_params=pltpu.CompilerParams(has_side_effects=EFFECT),
    )(*[_hbm(f) for f in fulls], *after)
    return res[0], res[1], list(res[2 : 2 + n_w]), res[-1]


def _relay_copies(kinds, waiting):
    def copies(refs, send_sem, recv_sem):
        _, _, c, q, chips = _place()
        out = []
        for i, kind in enumerate(kinds):
            for k, (src_rel, dst_rel) in enumerate(((0, 1), (1, 0))):
                held = _row_half(_row_half(_shard_view(refs[i], kind, q ^ _REL_MASK[src_rel]), c), k)
                far = _row_half(_row_half(_shard_view(refs[i], kind, q ^ _REL_MASK[2]), c), k)
                dst = far if waiting else held
                out.append(_remote(held, dst, send_sem.at[2 * i + k], recv_sem.at[2 * i + k], (*chips[dst_rel], c)))
        return out

    return copies


def _forward_copies(kinds, waiting, rels=(0, 1, 2), sem0=0):
    def copies(refs, send_sem, recv_sem):
        x, y, c, q, _ = _place()
        out = []
        for i, kind in enumerate(kinds):
            for k, r in enumerate(rels):
                quarter = _shard_view(refs[i], kind, q ^ _REL_MASK[r])
                landed = _row_half(quarter, c)
                dst = _row_half(quarter, 1 - c) if waiting else landed
                sem = sem0 + len(rels) * i + k
                out.append(_remote(landed, dst, send_sem.at[sem], recv_sem.at[sem], (x, y, 1 - c)))
        return out

    return copies


def _relay_and_forward_copies(kinds, waiting):
    forward = _forward_copies(kinds, waiting, rels=(0, 1), sem0=2 * len(kinds))
    relay = _relay_copies(kinds, waiting)
    return lambda refs, send_sem, recv_sem: forward(refs, send_sem, recv_sem) + relay(refs, send_sem, recv_sem)


def _gather_wait(name, fulls, kinds, w_ids, send_sem, recv_sem, after, rels=(0, 1, 2)):
    n = len(fulls)

    def body(*refs):
        g = refs[:n]
        s_sem, r_sem = refs[n], refs[n + 1]
        x, y, c, q, _ = _place()
        for i, w in enumerate(w_ids):
            mine = _row_half(_shard_view(g[i], kinds[i], q), c)
            for r in rels:
                landed = _row_half(_shard_view(g[i], kinds[i], q ^ _REL_MASK[r]), c)
                cp = _remote(mine, landed, s_sem.at[3 * w + r], r_sem.at[3 * w + r], (x, y, 1 - c))
                cp.wait_send()
                cp.wait_recv()

    res = _pallas(
        body,
        name=name,
        out_shape=[pltpu.HBM(f.shape, f.dtype) for f in fulls],
        in_specs=[HBM_SPEC] * n + [SEM_SPEC, SEM_SPEC, ANY],
        out_specs=[HBM_SPEC] * n,
        input_output_aliases={i: i for i in range(n)},
        compiler_params=pltpu.CompilerParams(has_side_effects=EFFECT),
    )(*fulls, send_sem, recv_sem, after)
    return list(res)


def _gather_forward(name, fulls, kinds, rels=(0, 1, 2)):
    n = len(fulls)

    def body(*refs):
        g = refs[n : 2 * n]
        send, recv = refs[2 * n :]
        _sibling_handshake()
        x, y, c, q, _ = _place()
        sib = (x, y, 1 - c)
        cps = []
        for i in range(n):
            for r in rels:
                landed = _row_half(_shard_view(g[i], kinds[i], q ^ _REL_MASK[r]), c)
                cps.append(_remote(landed, landed, send.at[i, r], recv.at[i, r], sib))
        for cp in cps:
            cp.start()
        for i in range(n):
            for r in rels:
                other = _row_half(_shard_view(g[i], kinds[i], q ^ _REL_MASK[r]), 1 - c)
                _remote(other, other, send.at[i, r], recv.at[i, r], sib).wait_recv()
        for cp in cps:
            cp.wait_send()

    res = _pallas(
        body,
        name=name,
        in_specs=[ANY] * n,
        out_specs=[ANY] * n,
        out_shape=[jax.ShapeDtypeStruct(f.shape, f.dtype) for f in fulls],
        scratch_shapes=[pltpu.SemaphoreType.DMA((n, 3)), pltpu.SemaphoreType.DMA((n, 3))],
        input_output_aliases={i: i for i in range(n)},
        compiler_params=pltpu.CompilerParams(collective_id=SIBLING_BARRIER_ID),
    )(*fulls)
    return list(res)


SIBLING_BARRIER_ID = 1


def _sibling_handshake():
    sib = (lax.axis_index("x"), lax.axis_index("y"), 1 - lax.axis_index("c"))
    barrier = pltpu.get_barrier_semaphore()
    pl.semaphore_signal(barrier, inc=1, device_id=sib, device_id_type=MESH)
    pl.semaphore_wait(barrier, 1)


def _split_start(name, bufs, n_sems, copies, sibling_only=False):
    n = len(bufs)

    def body(*refs):
        if sibling_only:
            _sibling_handshake()
        for cp in copies(refs[:n], refs[n], refs[n + 1]):
            cp.start()

    extra = {"collective_id": SIBLING_BARRIER_ID} if sibling_only else {}

    res = _pallas(
        body,
        name=name,
        out_shape=(
            pltpu.SemaphoreType.DMA((n_sems,)),
            pltpu.SemaphoreType.DMA((n_sems,)),
            *[pltpu.HBM(b.shape, b.dtype) for b in bufs],
        ),
        in_specs=[HBM_SPEC] * n,
        out_specs=(SEM_SPEC, SEM_SPEC, *[HBM_SPEC] * n),
        input_output_aliases={i: i + 2 for i in range(n)},
        compiler_params=pltpu.CompilerParams(has_side_effects=EFFECT, **extra),
    )(*[_hbm(b) for b in bufs])
    return res[0], res[1], list(res[2:])


def _split_wait(name, bufs, send_sem, recv_sem, copies, after):
    n = len(bufs)

    def body(*refs):
        for cp in copies(refs[:n], refs[n], refs[n + 1]):
            cp.wait_send()
            cp.wait_recv()

    res = _pallas(
        body,
        name=name,
        out_shape=[pltpu.HBM(b.shape, b.dtype) for b in bufs],
        in_specs=[HBM_SPEC] * n + [SEM_SPEC, SEM_SPEC, ANY],
        out_specs=[HBM_SPEC] * n,
        input_output_aliases={i: i for i in range(n)},
        compiler_params=pltpu.CompilerParams(has_side_effects=EFFECT),
    )(*bufs, send_sem, recv_sem, after)
    return list(res)


def _pair_exchange_copies(kinds):
    n = len(kinds)

    def copies(refs, send_sem, recv_sem):
        x, y, c, _, _ = _place()
        return [
            _remote(_pair_half(refs[w], kinds[w], 1 - c), refs[n + w], send_sem.at[w], recv_sem.at[w], (x, y, 1 - c))
            for w in range(n)
        ]

    return copies


def _pair_share_copies(kinds, waiting):
    def copies(refs, send_sem, recv_sem):
        x, y, c, _, _ = _place()
        out = []
        for w, kind in enumerate(kinds):
            mine = _pair_half(refs[w], kind, c)
            dst = _pair_half(refs[w], kind, 1 - c) if waiting else mine
            out.append(_remote(mine, dst, send_sem.at[w], recv_sem.at[w], (x, y, 1 - c)))
        return out

    return copies


def _piece_shape(half_shape, kind):
    r, c = half_shape
    return (3, r, c // N_CHIPS) if kind == "col" else (3, r // N_CHIPS, c)


def _chip_send_start(name, halves, kinds):
    n = len(halves)
    lands = [lax.empty(_piece_shape(h.shape, k), BF16) for h, k in zip(halves, kinds)]

    def body(*refs):
        h, land = refs[:n], refs[n : 2 * n]
        send_sem, recv_sem = refs[2 * n], refs[2 * n + 1]
        _, _, c, q, chips = _place()
        for i in range(n):
            for r, chip in enumerate(chips):
                piece = _shard_view(h[i], kinds[i], q ^ _REL_MASK[r])
                _remote(piece, land[i].at[r], send_sem.at[3 * i + r], recv_sem.at[3 * i + r], (*chip, c)).start()

    res = _pallas(
        body,
        name=name,
        out_shape=(
            pltpu.SemaphoreType.DMA((3 * n,)),
            pltpu.SemaphoreType.DMA((3 * n,)),
            *[pltpu.HBM(a.shape, a.dtype) for a in halves],
            *[pltpu.HBM(a.shape, a.dtype) for a in lands],
        ),
        in_specs=[HBM_SPEC] * (2 * n),
        out_specs=(SEM_SPEC, SEM_SPEC, *[HBM_SPEC] * (2 * n)),
        input_output_aliases={i: i + 2 for i in range(2 * n)},
        compiler_params=pltpu.CompilerParams(has_side_effects=EFFECT),
    )(*[_hbm(a) for a in halves], *[_hbm(a) for a in lands])
    return res[0], res[1], list(res[2 : 2 + n]), list(res[2 + n :])


def _chip_send_wait(name, halves, lands, kinds, send_sem, recv_sem, after):
    n = len(halves)

    def body(*refs):
        h, land = refs[:n], refs[n : 2 * n]
        s_sem, r_sem = refs[2 * n], refs[2 * n + 1]
        x, y, c, q, _ = _place()
        for i in range(n):
            for r in range(3):
                piece = _shard_view(h[i], kinds[i], q ^ _REL_MASK[r])
                cp = _remote(piece, land[i].at[r], s_sem.at[3 * i + r], r_sem.at[3 * i + r], (x, y, 1 - c))
                cp.wait_send()
                cp.wait_recv()

    res = _pallas(
        body,
        name=name,
        out_shape=[pltpu.HBM(a.shape, a.dtype) for a in halves] + [pltpu.HBM(a.shape, a.dtype) for a in lands],
        in_specs=[HBM_SPEC] * (2 * n) + [SEM_SPEC, SEM_SPEC, ANY],
        out_specs=[HBM_SPEC] * (2 * n),
        input_output_aliases={i: i for i in range(2 * n)},
        compiler_params=pltpu.CompilerParams(has_side_effects=EFFECT),
    )(*halves, *lands, send_sem, recv_sem, after)
    return list(res[:n]), list(res[n:])


def _small_exchange_copies(waiting):
    def copies(refs, send_sem, recv_sem):
        p, land = refs
        x, y, c, q, _ = _place()
        me = 2 * q + c
        out = []
        for dd in range(1, 2 * N_CHIPS):
            dev = (x ^ ((dd >> 2) & 1), y ^ ((dd >> 1) & 1), c ^ (dd & 1))
            dst = land.at[me ^ dd] if waiting else land.at[me]
            out.append(_remote(p, dst, send_sem.at[dd - 1], recv_sem.at[dd - 1], dev))
        return out

    return copies


def _small_sum(name, me_idx, p, land):
    rows = p.shape[0]
    n_dev = 2 * N_CHIPS

    def body(me_ref, p_ref, land_ref, o_ref):
        me = me_ref[0]
        total = None
        for dev in range(n_dev):
            term = jnp.where(me == dev, p_ref[...], land_ref[dev])
            total = term if total is None else total + term
        o_ref[...] = total

    return _pallas(
        body,
        name=name,
        grid_spec=pltpu.PrefetchScalarGridSpec(
            num_scalar_prefetch=1,
            grid=(1,),
            in_specs=[pl.BlockSpec((rows, LANES), lambda i, m: (0, 0)), pl.BlockSpec((n_dev, rows, LANES), lambda i, m: (0, 0, 0))],
            out_specs=pl.BlockSpec((rows, LANES), lambda i, m: (0, 0)),
        ),
        out_shape=jax.ShapeDtypeStruct(p.shape, F32),
    )(me_idx, p, land)


def _pack(parts):
    rows = []
    for a in parts:
        flat = a.reshape(-1).astype(F32)
        n = flat.shape[0]
        padded = -(-n // (8 * LANES)) * (8 * LANES)
        rows.append(jnp.pad(flat, (0, padded - n)).reshape(-1, LANES))
    return jnp.concatenate(rows, axis=0)


def _unpack(packed, shapes):
    out, row = [], 0
    for shp in shapes:
        n = int(np.prod(shp))
        nrows = -(-n // (8 * LANES)) * 8
        out.append(packed[row : row + nrows].reshape(-1)[:n].reshape(shp))
        row += nrows
    return out


def kernel(x, w_in, norm_mix, sgu_v_gain, sgu_w_s, sgu_b_s, w_a_out, attn_sink, rel_bias, w_b_out, w_o, norm_ffn, w_gate, w_up, w_down, norm_final, loss_target, m_w_in, m_norm_mix, m_sgu_v_gain, m_sgu_w_s, m_sgu_b_s, m_w_a_out, m_attn_sink, m_rel_bias, m_w_b_out, m_w_o, m_norm_ffn, m_w_gate, m_w_up, m_w_down, m_norm_final, v_w_in, v_norm_mix, v_sgu_v_gain, v_sgu_w_s, v_sgu_b_s, v_w_a_out, v_attn_sink, v_rel_bias, v_w_b_out, v_w_o, v_norm_ffn, v_w_gate, v_w_up, v_w_down, v_norm_final):
    s, d = x.shape[1], x.shape[2]
    w_sgu = sgu_v_gain.shape[1]
    groups = sgu_w_s.shape[1]
    heads = attn_sink.shape[1]
    grp = heads // N_KV_HEADS
    w_att = heads * HEAD_DIM
    w_kv = N_KV_HEADS * HEAD_DIM
    d_ff = w_gate.shape[2] * N_CHIPS
    n_in = w_in.shape[2] * N_CHIPS
    off_q = 2 * w_sgu
    off_k = off_q + w_att
    off_g = off_k + 2 * w_kv
    assert n_in == off_g + 2 * d and groups * BLK == w_sgu and s % BLK == 0

    x2d = x.reshape(s, d)
    tgt = loss_target.reshape(s, d)
    c_idx = lax.axis_index("c").astype(I32).reshape(1)
    q_idx = (2 * lax.axis_index("x") + lax.axis_index("y")).astype(I32).reshape(1)
    qc_idx = jnp.concatenate([q_idx, c_idx])

    W_IN, W_A, W_B, W_O, W_GATE, W_UP, W_DOWN = range(7)
    names = ["w_in", "w_a", "w_b", "w_o", "w_gate", "w_up", "w_down"]
    kinds = ["col", "col", "col", "row", "col", "col", "row"]
    big_w = [w_in[0], w_a_out[0], w_b_out[0], w_o[0], w_gate[0], w_up[0], w_down[0]]
    big_m = [m_w_in[0], m_w_a_out[0], m_w_b_out[0], m_w_o[0], m_w_gate[0], m_w_up[0], m_w_down[0]]
    big_v = [v_w_in[0], v_w_a_out[0], v_w_b_out[0], v_w_o[0], v_w_gate[0], v_w_up[0], v_w_down[0]]
    full_in = _cast_into_full("cast_w_in", q_idx, big_w[W_IN], kinds[W_IN])
    in_send, in_recv, (full_in,), token = _gather_start("gather_start_in", [full_in], [kinds[W_IN]], rels=(0, 1))
    rest = [_cast_into_full("cast_" + names[i], q_idx, big_w[i], kinds[i], after=(token,)) for i in range(1, 7)]

    ws_b = sgu_w_s[0].astype(BF16)
    wst_b = jnp.swapaxes(sgu_w_s[0], 1, 2).astype(BF16)
    b_col = sgu_b_s[0].reshape(groups, BLK, 1)
    bias_tab, bucket = _band_tables(rel_bias)
    sink = attn_sink[0]
    small_w = [norm_mix, sgu_v_gain, sgu_w_s, sgu_b_s, attn_sink, rel_bias, norm_ffn, norm_final]
    small_m = [m_norm_mix, m_sgu_v_gain, m_sgu_w_s, m_sgu_b_s, m_attn_sink, m_rel_bias, m_norm_ffn, m_norm_final]
    small_v = [v_norm_mix, v_sgu_v_gain, v_sgu_w_s, v_sgu_b_s, v_attn_sink, v_rel_bias, v_norm_ffn, v_norm_final]
    small_shapes = [w.shape for w in small_w]
    zero1 = jnp.zeros((1,), F32)
    pw, pm, pv = _pack(small_w + [zero1]), _pack(small_m + [zero1]), _pack(small_v + [zero1])
    h1 = _rms_fwd("rms_mix", x2d, norm_mix, after=(token, rest[-1], ws_b, wst_b, b_col, bias_tab, pw, pm, pv))
    (full_in,) = _gather_wait("gather_wait_in", [full_in], [kinds[W_IN]], [0], in_send, in_recv, h1, rels=(0, 1))
    relay_send, relay_recv, (full_in,) = _split_start(
        "gather_relay_in", [full_in], 4, _relay_and_forward_copies([kinds[W_IN]], False)
    )
    full_down = rest.pop()
    full_up = rest.pop()
    ag_send, ag_recv, rest, token = _gather_start("gather_start_rest", rest, kinds[1:5], rels=(0, 1), after=(full_in,))
    (full_in,) = _split_wait(
        "gather_relay_wait_in", [full_in], relay_send, relay_recv, _relay_and_forward_copies([kinds[W_IN]], True), token
    )
    (g_in,) = _gather_forward("gather_fwd_in", [full_in], [kinds[W_IN]], rels=(2,))
    fulls = [g_in] + rest

    def relay_begin(tag, ids, after, copies=_relay_copies, sems_per_weight=2, source=None):
        ks = [kinds[i] for i in ids]
        send, recv, bufs, w_ids = source or (ag_send, ag_recv, [fulls[i] for i in ids], [i - 1 for i in ids])
        bufs = _gather_wait("gather_wait_" + tag, bufs, ks, w_ids, send, recv, after, rels=(0, 1))
        send, recv, bufs = _split_start("gather_relay_" + tag, bufs, sems_per_weight * len(ids), copies(ks, False))
        return tag, ks, send, recv, bufs

    def relay_end(state, after):
        tag, ks, send, recv, bufs = state
        bufs = _split_wait("gather_relay_wait_" + tag, bufs, send, recv, _relay_and_forward_copies(ks, True), after)
        return _gather_forward("gather_fwd_" + tag, bufs, ks, rels=(2,))

    def relay_end_async(state, after):
        tag, ks, send, recv, bufs = state
        bufs = _split_wait("gather_relay_wait_" + tag, bufs, send, recv, _relay_copies(ks, True), after)
        send, recv, bufs = _split_start("gather_fwd_start_" + tag, bufs, 3 * len(ks), _forward_copies(ks, False), sibling_only=True)
        return tag, ks, send, recv, bufs

    def forwarded(state, after):
        tag, ks, send, recv, bufs = state
        return _split_wait("gather_fwd_wait_" + tag, bufs, send, recv, _forward_copies(ks, True), after)

    tm = _tile(s, (1024, 512, 256, 128))

    tn = _tile(n_in, (768, 640, 512))
    z = _mm(
        "mm_z", (s // tm, n_in // tn, 1), [h1, g_in],
        [pl.BlockSpec((tm, d), lambda i, j, k: (i, 0)), pl.BlockSpec((d, tn), lambda i, j, k: (0, j))],
        [jax.ShapeDtypeStruct((s, n_in), BF16)], [pl.BlockSpec((tm, tn), lambda i, j, k: (i, j))],
        [(0, 1, NN, 0)], 1, (tm, tn), 1, lambda ins, vals, outs, cs: _put(outs[0], cs, vals[0]),
        _mm_vmem([((tm, d), BF16, 2), ((d, tn), BF16, 2), ((tm, tn), F32, 3)]),
    )[0]
    mix_relay = relay_begin("mix", [W_A, W_B, W_O], z, copies=_relay_and_forward_copies, sems_per_weight=4)
    up_send, up_recv, (full_up,), token = _gather_start(
        "gather_start_up", [full_up], [kinds[W_UP]], rels=(0, 1), after=(mix_relay[4][0],)
    )

    a_act = _sgu_fwd(z, sgu_v_gain, ws_b, b_col, w_sgu, after=(token,))

    kv_b = z[:, off_k:off_g]
    k_pad = jnp.pad(kv_b[:, :w_kv], ((BLK, BLK), (0, 0)))
    v_pad = jnp.pad(kv_b[:, w_kv:], ((BLK, BLK), (0, 0)))
    q_blk0 = off_q // (grp * HEAD_DIM)
    att = _attn_fwd(sink, z, k_pad, v_pad, bias_tab, grp, q_blk0)

    gate_relay = relay_begin("gate", [W_GATE], att)
    g_a, g_b, g_o = relay_end(mix_relay, gate_relay[4][0])

    tg = _tile(d, (512,))
    ga0, gb0 = off_g // tg, (off_g + d) // tg

    def ep_gate(ins, vals, outs, cs):
        sa, sb = _sigmoid(ins[4][:, cs].astype(F32)), _sigmoid(ins[5][:, cs].astype(F32))
        _put(outs[0], cs, sa * vals[0] + sb * vals[1])
        _put(outs[1], cs, vals[0])
        _put(outs[2], cs, vals[1])

    t_out = pl.BlockSpec((tm, tg), lambda i, j, k: (i, j))
    m_act, y_a, y_b = _mm(
        "mm_branches", (s // tm, d // tg, 1), [a_act, g_a, att, g_b, z, z],
        [pl.BlockSpec((tm, w_sgu), lambda i, j, k: (i, 0)), pl.BlockSpec((w_sgu, tg), lambda i, j, k: (0, j)),
         pl.BlockSpec((tm, w_att), lambda i, j, k: (i, 0)), pl.BlockSpec((w_att, tg), lambda i, j, k: (0, j)),
         pl.BlockSpec((tm, tg), lambda i, j, k: (i, ga0 + j)), pl.BlockSpec((tm, tg), lambda i, j, k: (i, gb0 + j))],
        [jax.ShapeDtypeStruct((s, d), BF16)] * 3,
        [t_out, t_out, t_out], [(0, 1, NN, 0), (2, 3, NN, 1)], 2, (tm, tg), 1, ep_gate,
        _mm_vmem([((tm, w_sgu), BF16, 4), ((w_sgu, tg), BF16, 4), ((tm, tg), F32, 12)]),    )

    tn = _tile(d, (1024, 512))

    def ep_residual(ins, vals, outs, cs):
        _put(outs[0], cs, ins[2][:, cs] + vals[0])

    up_relay = relay_begin("up", [W_UP], m_act, source=(up_send, up_recv, [full_up], [0]))
    down_send, down_recv, (full_down,), token = _gather_start(
        "gather_start_down", [full_down], [kinds[W_DOWN]], after=(up_relay[4][0],)
    )
    gate_fwd = relay_end_async(gate_relay, token)
    x2 = _mm(
        "mm_wo", (s // tm, d // tn, 1), [m_act, g_o, x2d],
        [pl.BlockSpec((tm, d), lambda i, j, k: (i, 0)), pl.BlockSpec((d, tn), lambda i, j, k: (0, j)),
         pl.BlockSpec((tm, tn), lambda i, j, k: (i, j))],
        [jax.ShapeDtypeStruct((s, d), F32)], [pl.BlockSpec((tm, tn), lambda i, j, k: (i, j))],
        [(0, 1, NN, 0)], 1, (tm, tn), 1, ep_residual,
        _mm_vmem([((tm, d), BF16, 2), ((d, tn), BF16, 2), ((tm, tn), F32, 5)]),
        after=(gate_fwd[4][0],),
    )[0]
    (g_gate,) = forwarded(gate_fwd, x2)
    up_fwd = relay_end_async(up_relay, g_gate)
    h2 = _rms_fwd("rms_ffn", x2, norm_ffn, after=(up_fwd[4][0],))
    (g_up,) = forwarded(up_fwd, h2)

    tf = _tile(d_ff, (512,))

    def ep_swiglu(ins, vals, outs, cs):
        gt, up = vals
        _put(outs[0], cs, gt)
        _put(outs[1], cs, up)
        _put(outs[2], cs, (gt * _sigmoid(gt)) * up)

    f_out = pl.BlockSpec((tm, tf), lambda i, j, k: (i, j))
    gt, up, f_act = _mm(
        "mm_gate_up", (s // tm, d_ff // tf, 1), [h2, g_gate, g_up],
        [pl.BlockSpec((tm, d), lambda i, j, k: (i, 0)), pl.BlockSpec((d, tf), lambda i, j, k: (0, j)),
         pl.BlockSpec((d, tf), lambda i, j, k: (0, j))],
        [jax.ShapeDtypeStruct((s, d_ff), BF16)] * 3,
        [f_out, f_out, f_out], [(0, 1, NN, 0), (0, 2, NN, 1)], 2, (tm, tf), 1, ep_swiglu,
        _mm_vmem([((tm, d), BF16, 2), ((d, tf), BF16, 4), ((tm, tf), F32, 8)]),    )
    (g_down,) = _gather_forward(
        "gather_fwd_ffn_out",
        _gather_wait("gather_wait_ffn_out", [full_down], [kinds[W_DOWN]], [0], down_send, down_recv, f_act),
        [kinds[W_DOWN]],
    )

    tkf = _tile(d_ff, (1408, 1024, 512))
    tml, tnl = _tile(s, (512, 256, 128)), _tile(d, (512,))
    x3 = _mm(
        "mm_down", (s // tml, d // tnl, 1), [f_act, g_down, x2],
        [pl.BlockSpec((tml, d_ff), lambda i, j, k: (i, 0)), pl.BlockSpec((d_ff, tnl), lambda i, j, k: (0, j)),
         pl.BlockSpec((tml, tnl), lambda i, j, k: (i, j))],
        [jax.ShapeDtypeStruct((s, d), F32)], [pl.BlockSpec((tml, tnl), lambda i, j, k: (i, j))],
        [(0, 1, NN, 0)], 1, (tml, tnl), 1, ep_residual,
        _mm_vmem([((tml, d_ff), BF16, 2), ((d_ff, tnl), BF16, 2), ((tml, tnl), F32, 6)]),    )[0]

    dx3, dx3b, dg_final, loss_part = _head(x3, norm_final.reshape(1, d), tgt)

    def reduce_a(tag, ids, grads):
        ks = [kinds[i] for i in ids]
        lands = [lax.empty((g.shape[0] // 2, g.shape[1]) if k == "col" else (g.shape[0], g.shape[1] // 2), BF16)
                 for g, k in zip(grads, ks)]
        send, recv, bufs = _split_start("pair_send_" + tag, list(grads) + lands, len(ids), _pair_exchange_copies(ks), sibling_only=True)
        return {"tag": tag, "ids": ids, "ks": ks, "pair": (send, recv, bufs), "token": bufs[0]}

    def reduce_b(st, after):
        tag, ids, ks = st["tag"], st["ids"], st["ks"]
        send, recv, bufs = st["pair"]
        bufs = _split_wait("pair_wait_" + tag, bufs, send, recv, _pair_exchange_copies(ks), after)
        grads, from_sib = bufs[: len(ids)], bufs[len(ids) :]
        halves = [_pair_add("pair_add_" + names[i], c_idx, g, r, k) for i, g, r, k in zip(ids, grads, from_sib, ks)]
        st["chip"] = _chip_send_start("chip_send_" + tag, halves, ks)
        st["token"] = st["chip"][2][0]

    def reduce_c(st, after):
        tag, ids, ks = st["tag"], st["ids"], st["ks"]
        send, recv, halves, lands = st["chip"]
        halves, lands = _chip_send_wait("chip_wait_" + tag, halves, lands, ks, send, recv, after)
        pieces = [_chip_sum("chip_sum_" + names[i], qc_idx, h, r, k) for i, h, r, k in zip(ids, halves, lands, ks)]
        st["share"] = _split_start("share_send_" + tag, pieces, len(ids), _pair_share_copies(ks, False), sibling_only=True)
        st["token"] = st["share"][2][0]

    def reduce_d(st, after):
        send, recv, bufs = st["share"]
        return _split_wait("share_wait_" + st["tag"], bufs, send, recv, _pair_share_copies(st["ks"], True), after)

    grads_big, upd = [None] * 7, [None] * 7

    def finish(st, after):
        shared = reduce_d(st, after)
        after = shared[0]
        for i, g in zip(st["ids"], shared):
            upd[i] = _adamw("adamw_" + names[i], big_w[i], g, big_m[i], big_v[i], after=(after,))
            grads_big[i] = upd[i][3]
            after = upd[i][0]
        return after

    def ep_swiglu_bwd(ins, vals, outs, cs):
        df = vals[0]
        gtv, upv = ins[2][:, cs].astype(F32), ins[3][:, cs].astype(F32)
        sg = _sigmoid(gtv)
        _put(outs[0], cs, df * upv * (sg + gtv * sg * (1.0 - sg)))
        _put(outs[1], cs, df * (gtv * sg))

    dgt, dup = _mm(
        "mm_dswiglu", (s // tm, d_ff // tf, 1), [dx3b, g_down, gt, up],
        [pl.BlockSpec((tm, d), lambda i, j, k: (i, 0)), pl.BlockSpec((tf, d), lambda i, j, k: (j, 0)), f_out, f_out],
        [jax.ShapeDtypeStruct((s, d_ff), BF16), jax.ShapeDtypeStruct((s, d_ff), BF16)], [f_out, f_out],
        [(0, 1, NT, 0)], 1, (tm, tf), 1, ep_swiglu_bwd,
        _mm_vmem([((tm, d), BF16, 2), ((tf, d), BF16, 2), ((tm, tf), F32, 8)]),    )

    def ep_store(ins, vals, outs, cs):
        for o, v in zip(outs, vals):
            _put(o, cs, v)

    twn = _tile(d, (1024, 512))
    gw_down = _mm(
        "mm_gw_down", (d_ff // tkf, d // twn, 1), [f_act, dx3b],
        [pl.BlockSpec((s, tkf), lambda i, j, k: (0, i)), pl.BlockSpec((s, twn), lambda i, j, k: (0, j))],
        [jax.ShapeDtypeStruct((d_ff, d), BF16)], [pl.BlockSpec((tkf, twn), lambda i, j, k: (i, j))],
        [(0, 1, TN, 0)], 1, (tkf, twn), 1, ep_store,
        _mm_vmem([((s, tkf), BF16, 3), ((s, twn), BF16, 2), ((tkf, twn), F32, 3)]),
    )[0]
    red_down = reduce_a("down", [W_DOWN], [gw_down])

    tn2 = _tile(d, (256,))
    dh2_specs = [pl.BlockSpec((tm, d_ff), lambda i, j, k: (i, 0)), pl.BlockSpec((tn2, d_ff), lambda i, j, k: (j, 0))]
    dh2_tile = pl.BlockSpec((tm, tn2), lambda i, j, k: (i, j))
    dh2_vmem = _mm_vmem([((tm, d_ff), BF16, 2), ((tn2, d_ff), BF16, 2), ((tm, tn2), F32, 7)])
    dh2 = _mm(
        "mm_dh2_gate", (s // tm, d // tn2, 1), [dgt, g_gate], dh2_specs,
        [jax.ShapeDtypeStruct((s, d), F32)], [dh2_tile], [(0, 1, NT, 0)], 1, (tm, tn2), 1, ep_store, dh2_vmem,
        after=(red_down["token"],),
    )[0]
    dh2 = _mm(
        "mm_dh2_up", (s // tm, d // tn2, 1), [dup, g_up, dh2], dh2_specs + [dh2_tile],
        [jax.ShapeDtypeStruct((s, d), F32)], [dh2_tile], [(0, 1, NT, 0)], 1, (tm, tn2), 1, ep_residual, dh2_vmem,
    )[0]
    reduce_b(red_down, dh2)

    twr = _tile(d, (1024, 512))
    w_tile = pl.BlockSpec((twr, tf), lambda i, j, k: (i, j))
    gw_gate, gw_up = _mm(
        "mm_gw_gate_up", (d // twr, d_ff // tf, 1), [h2, dgt, dup],
        [pl.BlockSpec((s, twr), lambda i, j, k: (0, i)), pl.BlockSpec((s, tf), lambda i, j, k: (0, j)),
         pl.BlockSpec((s, tf), lambda i, j, k: (0, j))],
        [jax.ShapeDtypeStruct((d, d_ff), BF16), jax.ShapeDtypeStruct((d, d_ff), BF16)], [w_tile, w_tile],
        [(0, 1, TN, 0), (0, 2, TN, 1)], 2, (twr, tf), 1, ep_store,
        _mm_vmem([((s, twr), BF16, 3), ((s, tf), BF16, 4), ((twr, tf), F32, 6)]),
        after=(red_down["token"],),
    )
    red_ffn = reduce_a("ffn_in", [W_GATE, W_UP], [gw_gate, gw_up])

    dx2, dx2b, dg_ffn = _rms_bwd("rms_ffn_bwd", x2, norm_ffn, dh2, dx3, after=(red_ffn["token"],))

    nj = d // tg

    def lo(j):
        return jnp.minimum(j, nj - 1)

    def gate_bwd_body(dx_ref, wo_ref, ga_ref, gb_ref, ya_ref, yb_ref, dya_ref, dyb_ref, dz_ref, keep):
        j = pl.program_id(1)

        @pl.when(j < nj)
        def _():
            dm = lax.dot_general(dx_ref[...], wo_ref[...], NT, preferred_element_type=F32)
            sa, sb = _sigmoid(ga_ref[...].astype(F32)), _sigmoid(gb_ref[...].astype(F32))
            dya_ref[...] = (dm * sa).astype(BF16)
            dyb_ref[...] = (dm * sb).astype(BF16)
            dz_ref[...] = (dm * ya_ref[...].astype(F32) * (sa * (1.0 - sa))).astype(BF16)
            keep[lo(j)] = (dm * yb_ref[...].astype(F32) * (sb * (1.0 - sb))).astype(BF16)

        @pl.when(j >= nj)
        def _():
            dz_ref[...] = keep[jnp.maximum(j - nj, 0)]

    t_lo = pl.BlockSpec((tm, tg), lambda i, j: (i, lo(j)))
    dya, dyb, dz = _pallas(
        gate_bwd_body,
        name="mm_dgate",
        grid=(s // tm, 2 * nj),
        in_specs=[
            pl.BlockSpec((tm, d), lambda i, j: (i, 0)),
            pl.BlockSpec((tg, d), lambda i, j: (lo(j), 0)),
            pl.BlockSpec((tm, tg), lambda i, j: (i, ga0 + lo(j))),
            pl.BlockSpec((tm, tg), lambda i, j: (i, gb0 + lo(j))),
            t_lo,
            t_lo,
        ],
        out_specs=[t_lo, t_lo, pl.BlockSpec((tm, tg), lambda i, j: (i, ga0 + j))],
        out_shape=[jax.ShapeDtypeStruct((s, d), BF16), jax.ShapeDtypeStruct((s, d), BF16), jax.ShapeDtypeStruct((s, n_in), BF16)],
        scratch_shapes=[pltpu.VMEM((nj, tm, tg), BF16)],
        compiler_params=_params(_mm_vmem([((tm, d), BF16, 2), ((tg, d), BF16, 2), ((tm, tg), F32, 14), ((nj, tm, tg), BF16, 1)])),
    )(dx2b, g_o, z, z, y_a, y_b)
    reduce_b(red_ffn, dya)
    reduce_c(red_down, red_ffn["token"])

    gw_o = _mm(
        "mm_gw_o", (d // twr, d // twn, 1), [m_act, dx2b],
        [pl.BlockSpec((s, twr), lambda i, j, k: (0, i)), pl.BlockSpec((s, twn), lambda i, j, k: (0, j))],
        [jax.ShapeDtypeStruct((d, d), BF16)], [pl.BlockSpec((twr, twn), lambda i, j, k: (i, j))],
        [(0, 1, TN, 0)], 1, (twr, twn), 1, ep_store,
        _mm_vmem([((s, twr), BF16, 3), ((s, twn), BF16, 2), ((twr, twn), F32, 3)]),
        after=(red_down["token"],),
    )[0]
    after_down = finish(red_down, gw_o)

    tb = _tile(w_sgu, (1024, 512))
    b_out = pl.BlockSpec((tm, tb), lambda i, j, k: (i, j))

    da, datt = _mm(
        "mm_dbranches", (s // tm, w_sgu // tb, 1), [dya, g_a, dyb, g_b],
        [pl.BlockSpec((tm, d), lambda i, j, k: (i, 0)), pl.BlockSpec((tb, d), lambda i, j, k: (j, 0)),
         pl.BlockSpec((tm, d), lambda i, j, k: (i, 0)), pl.BlockSpec((tb, d), lambda i, j, k: (j, 0))],
        [jax.ShapeDtypeStruct((s, w_sgu), BF16), jax.ShapeDtypeStruct((s, w_att), BF16)], [b_out, b_out],
        [(0, 1, NT, 0), (2, 3, NT, 1)], 2, (tm, tb), 1, ep_store,
        _mm_vmem([((tm, d), BF16, 4), ((tb, d), BF16, 4), ((tm, tb), F32, 6)]),
        after=(after_down,),
    )

    wb_tile = pl.BlockSpec((tb, twn), lambda i, j, k: (i, j))
    gw_a, gw_b = _mm(
        "mm_gw_branches", (w_sgu // tb, d // twn, 1), [a_act, dya, att, dyb],
        [pl.BlockSpec((s, tb), lambda i, j, k: (0, i)), pl.BlockSpec((s, twn), lambda i, j, k: (0, j)),
         pl.BlockSpec((s, tb), lambda i, j, k: (0, i)), pl.BlockSpec((s, twn), lambda i, j, k: (0, j))],
        [jax.ShapeDtypeStruct((w_sgu, d), BF16), jax.ShapeDtypeStruct((w_att, d), BF16)], [wb_tile, wb_tile],
        [(0, 1, TN, 0), (2, 3, TN, 1)], 2, (tb, twn), 1, ep_store,
        _mm_vmem([((s, tb), BF16, 5), ((s, twn), BF16, 4), ((tb, twn), F32, 6)]),
        after=(da,),
    )
    red_mix = reduce_a("mix", [W_O, W_A, W_B], [gw_o, gw_a, gw_b])

    dz, dws, dbs, dgain = _sgu_bwd(z, da, sgu_v_gain, ws_b, wst_b, b_col, w_sgu, dz, after=(red_mix["token"],))
    dz, dk_pad, dv_pad, dbias_tab, dsink = _attn_bwd(sink, z, k_pad, v_pad, bias_tab, datt, dz, grp, q_blk0)
    dz = _dkv_to_dz(dk_pad, dv_pad, dz, off_k // (2 * w_kv))
    drel = _relbias_bwd(dbias_tab, bucket)
    reduce_b(red_mix, dz)
    reduce_c(red_ffn, red_mix["token"])

    early = [dgain, dws, dbs, dsink[:, 0], drel[:, :REL_BUCKETS].T, dg_ffn, dg_final]
    p_early = _pack([g.reshape(shp) for g, shp in zip(early, small_shapes[1:])] + [loss_part[0, :1]])
    land = jnp.zeros((2 * N_CHIPS,) + p_early.shape, F32)
    sm_send, sm_recv, (p_early, land) = _split_start("small_send", [p_early, land], 2 * N_CHIPS - 1, _small_exchange_copies(False))

    tzn = _tile(n_in, (768, 640, 512))
    gw_in = _mm(
        "mm_gw_in", (d // twr, n_in // tzn, 1), [h1, dz],
        [pl.BlockSpec((s, twr), lambda i, j, k: (0, i)), pl.BlockSpec((s, tzn), lambda i, j, k: (0, j))],
        [jax.ShapeDtypeStruct((d, n_in), BF16)], [pl.BlockSpec((twr, tzn), lambda i, j, k: (i, j))],
        [(0, 1, TN, 0)], 1, (twr, tzn), 1, ep_store,
        _mm_vmem([((s, twr), BF16, 3), ((s, tzn), BF16, 2), ((twr, tzn), F32, 3)]),
        after=(red_ffn["token"], p_early),
    )[0]
    red_in = reduce_a("w_in", [W_IN], [gw_in])

    reduce_c(red_mix, red_in["token"])
    reduce_b(red_in, finish(red_mix, red_in["token"]))

    dh1 = _mm(
        "mm_dh1", (s // tm, d // tn2, 1), [dz, g_in],
        [pl.BlockSpec((tm, n_in), lambda i, j, k: (i, 0)), pl.BlockSpec((tn2, n_in), lambda i, j, k: (j, 0))],
        [jax.ShapeDtypeStruct((s, d), F32)], [pl.BlockSpec((tm, tn2), lambda i, j, k: (i, j))],
        [(0, 1, NT, 0)], 1, (tm, tn2), 1, ep_store,
        _mm_vmem([((tm, n_in), BF16, 2), ((tn2, n_in), BF16, 2), ((tm, tn2), F32, 5)]),
        after=(red_in["token"],),
    )[0]

    grad_x, _, dg_mix = _rms_bwd("rms_mix_bwd", x2d, norm_mix, dh1, dx2)

    p_mix = _pack([dg_mix.reshape(small_shapes[0])])
    land_mix = jnp.zeros((2 * N_CHIPS,) + p_mix.shape, F32)
    mx_send, mx_recv, (p_mix, land_mix) = _split_start("mix_send", [p_mix, land_mix], 2 * N_CHIPS - 1, _small_exchange_copies(False))

    reduce_c(red_in, finish(red_ffn, p_mix))
    p_early, land = _split_wait("small_wait", [p_early, land], sm_send, sm_recv, _small_exchange_copies(True), red_in["token"])
    p_mix, land_mix = _split_wait("mix_wait", [p_mix, land_mix], mx_send, mx_recv, _small_exchange_copies(True), p_early)
    me_idx = 2 * q_idx + c_idx
    packed_g = jnp.concatenate([_small_sum("mix_sum", me_idx, p_mix, land_mix), _small_sum("small_sum", me_idx, p_early, land)], axis=0)
    g_small = _unpack(packed_g, small_shapes + [(1,)])
    loss = g_small[-1].reshape(())
    g_small = g_small[:-1]
    pg = _pack(g_small + [zero1])
    small_upd = _adamw("adamw_small", pw, pg, pm, pv)
    d_small, nm_small, nv_small = [_unpack(a, small_shapes) for a in small_upd[:3]]
    finish(red_in, small_upd[0])

    small_names = ["norm_mix", "sgu_v_gain", "sgu_w_s", "sgu_b_s", "attn_sink", "rel_bias", "norm_ffn", "norm_final"]
    table = {}
    for i, n in enumerate(names):
        table[n] = (grads_big[i][None], upd[i][0][None], upd[i][1][None], upd[i][2][None])
    for i, n in enumerate(small_names):
        table[n] = (g_small[i], d_small[i], nm_small[i], nv_small[i])
    order = ["w_in", "norm_mix", "sgu_v_gain", "sgu_w_s", "sgu_b_s", "w_a", "attn_sink", "rel_bias", "w_b", "w_o", "norm_ffn",
             "w_gate", "w_up", "w_down", "norm_final"]
    outs = [loss, grad_x.reshape(1, s, d)]
    for part in range(4):
        outs += [table[n][part] for n in order]
    return tuple(outs)
```

```python
import math

import jax
import jax.numpy as jnp
import numpy as np
from jax import lax
from jax.experimental import pallas as pl
from jax.experimental.pallas import tpu as pltpu

F32 = jnp.float32
BF16 = jnp.bfloat16
I32 = jnp.int32
MESH = pl.DeviceIdType.MESH

EPS = 1e-6
NEG = -1e30
BLK = 128
HEAD_DIM = 128
N_KV_HEADS = 2
REL_BUCKETS = 32
REL_MAX_DIST = 128
N_CHIPS = 4
ADAM_LR, ADAM_B1, ADAM_B2, ADAM_EPS, ADAM_WD, ADAM_STEP = 0.001, 0.9, 0.999, 1e-08, 0.01, 10

LANES = 128
VMEM_CAP = 60 * 1024 * 1024

NN = (((1,), (0,)), ((), ()))
NT = (((1,), (1,)), ((), ()))
TN = (((0,), (0,)), ((), ()))
ANY = pl.BlockSpec(memory_space=pl.ANY)
HBM_SPEC = pl.BlockSpec(memory_space=pltpu.HBM)
SEM_SPEC = pl.BlockSpec(memory_space=pltpu.SEMAPHORE)
EFFECT = pltpu.SideEffectType.DATAFLOW_SIDE_EFFECTING


def _tile(n, cands):
    for t in cands:
        if n % t == 0:
            return t
    return n


PIN_BYTES = 64 * 1024


def _pin_hbm(a):
    big = hasattr(a, "dtype") and jnp.issubdtype(a.dtype, jnp.floating) and _nbytes(a.shape, a.dtype) >= PIN_BYTES
    return pltpu.with_memory_space_constraint(a, pltpu.HBM) if big else a


def _pallas(body, *, out_shape, **kw):
    def pin(o):
        big = isinstance(o, jax.ShapeDtypeStruct) and jnp.issubdtype(o.dtype, jnp.floating) and _nbytes(o.shape, o.dtype) >= PIN_BYTES
        return pltpu.HBM(o.shape, o.dtype) if big else o

    shapes = type(out_shape)(pin(o) for o in out_shape) if isinstance(out_shape, (list, tuple)) else pin(out_shape)
    call = pl.pallas_call(body, out_shape=shapes, **kw)
    return lambda *args: call(*[_pin_hbm(a) for a in args])


def _params(vmem_bytes=None, **kw):
    if vmem_bytes is not None:
        kw["vmem_limit_bytes"] = int(min(max(vmem_bytes, 32 * 1024 * 1024), VMEM_CAP))
    return pltpu.CompilerParams(**kw)


def _nbytes(shape, dtype):
    return int(np.prod(shape)) * jnp.dtype(dtype).itemsize


def _sigmoid(x):
    return 1.0 / (1.0 + jnp.exp(-x))


_GC = 0.7978845608028654
_GA = 0.044715


def _gelu(x):
    return 0.5 * x * (1.0 + jnp.tanh(_GC * (x + _GA * (x * x * x))))


def _gelu_grad(x):
    t = jnp.tanh(_GC * (x + _GA * (x * x * x)))
    return 0.5 * (1.0 + t) + 0.5 * x * (1.0 - t * t) * (_GC * (1.0 + 3.0 * _GA * (x * x)))


def _bf(v):
    return v if v.dtype == BF16 else v.astype(BF16)


def _mm(name, grid, ins, in_specs, out_shape, out_specs, pairs, n_acc, tile, nk, epilogue, vmem_bytes, after=()):
    assert nk == 1
    n_in, n_out = len(ins) + len(after), len(out_shape)

    def body(*refs):
        in_refs, out_refs = refs[:n_in], refs[n_in : n_in + n_out]
        vals = [None] * n_acc
        for a_i, b_i, dn, acc_i in pairs:
            d = lax.dot_general(_bf(in_refs[a_i][...]), _bf(in_refs[b_i][...]), dn, preferred_element_type=F32)
            vals[acc_i] = d if vals[acc_i] is None else vals[acc_i] + d
        epilogue(in_refs, vals, out_refs, slice(None))

    return _pallas(
        body,
        name=name,
        grid=grid,
        in_specs=list(in_specs) + [ANY] * len(after),
        out_specs=out_specs,
        out_shape=out_shape,
        compiler_params=_params(vmem_bytes),
    )(*ins, *after)


def _put(ref, cs, v):
    ref[:, cs] = v.astype(ref.dtype)


def _mm_vmem(tiles):
    return sum(_nbytes(s, d) * c for s, d, c in tiles) + 4 * 1024 * 1024


def _rows8(v):
    r, d = v.shape
    return v.reshape(r // 8, 8, d).sum(axis=0)


def _rms_fwd(name, x, g, after=()):
    s, d = x.shape
    tm = _tile(s, (256, 128))

    def body(x_ref, g_ref, *rest):
        h_ref = rest[-1]
        xv = x_ref[...]
        r = lax.rsqrt(jnp.mean(xv * xv, axis=-1, keepdims=True) + EPS)
        h_ref[...] = ((xv * r) * g_ref[...]).astype(BF16)

    return _pallas(
        body,
        name=name,
        grid=(s // tm,),
        in_specs=[pl.BlockSpec((tm, d), lambda i: (i, 0)), pl.BlockSpec((1, d), lambda i: (0, 0))] + [ANY] * len(after),
        out_specs=pl.BlockSpec((tm, d), lambda i: (i, 0)),
        out_shape=jax.ShapeDtypeStruct((s, d), BF16),
    )(x, g, *after)


def _rms_bwd(name, x, g, dh, dres, after=()):
    s, d = x.shape
    tm = _tile(s, (256, 128))
    n = s // tm
    n_after = len(after)

    def body(x_ref, g_ref, dh_ref, dres_ref, *rest):
        dx_ref, dxb_ref, dg_ref, acc_ref = rest[n_after:]
        i = pl.program_id(0)
        xv = x_ref[...]
        r = lax.rsqrt(jnp.mean(xv * xv, axis=-1, keepdims=True) + EPS)
        xh = xv * r
        dhv = dh_ref[...]
        dxh = dhv * g_ref[...]
        dx = r * (dxh - xh * jnp.mean(dxh * xh, axis=-1, keepdims=True)) + dres_ref[...]
        dx_ref[...] = dx
        dxb_ref[...] = dx.astype(BF16)
        part = _rows8(dhv * xh)

        @pl.when(i == 0)
        def _():
            acc_ref[...] = part

        @pl.when(i > 0)
        def _():
            acc_ref[...] += part

        @pl.when(i == n - 1)
        def _():
            dg_ref[...] = jnp.sum(acc_ref[...], axis=0, keepdims=True)

    row = pl.BlockSpec((tm, d), lambda i: (i, 0))
    vec = pl.BlockSpec((1, d), lambda i: (0, 0))
    return _pallas(
        body,
        name=name,
        grid=(n,),
        in_specs=[row, vec, row, row] + [ANY] * n_after,
        out_specs=[row, row, vec],
        out_shape=[jax.ShapeDtypeStruct((s, d), F32), jax.ShapeDtypeStruct((s, d), BF16), jax.ShapeDtypeStruct((1, d), F32)],
        scratch_shapes=[pltpu.VMEM((8, d), F32)],
    )(x, g, dh, dres, *after)


def _head(x3, g, target):
    s, d = x3.shape
    tm = _tile(s, (256, 128))
    n = s // tm

    def body(x_ref, g_ref, t_ref, dx_ref, dxb_ref, dg_ref, loss_ref, acc_g, acc_l):
        i = pl.program_id(0)
        xv = x_ref[...]
        gv = g_ref[...]
        r = lax.rsqrt(jnp.mean(xv * xv, axis=-1, keepdims=True) + EPS)
        xh = xv * r
        e = xh * gv - t_ref[...]
        dy = e * (1.0 / d)
        dxh = dy * gv
        dx = r * (dxh - xh * jnp.mean(dxh * xh, axis=-1, keepdims=True))
        dx_ref[...] = dx
        dxb_ref[...] = dx.astype(BF16)
        pg = _rows8(dy * xh)
        plo = _rows8(e * e)

        @pl.when(i == 0)
        def _():
            acc_g[...] = pg
            acc_l[...] = plo

        @pl.when(i > 0)
        def _():
            acc_g[...] += pg
            acc_l[...] += plo

        @pl.when(i == n - 1)
        def _():
            dg_ref[...] = jnp.sum(acc_g[...], axis=0, keepdims=True)
            loss_ref[...] = jnp.full((1, LANES), (0.5 / d) * jnp.sum(acc_l[...]), F32)

    row = pl.BlockSpec((tm, d), lambda i: (i, 0))
    vec = pl.BlockSpec((1, d), lambda i: (0, 0))
    return _pallas(
        body,
        name="head",
        grid=(n,),
        in_specs=[row, vec, row],
        out_specs=[row, row, vec, pl.BlockSpec((1, LANES), lambda i: (0, 0))],
        out_shape=[
            jax.ShapeDtypeStruct((s, d), F32),
            jax.ShapeDtypeStruct((s, d), BF16),
            jax.ShapeDtypeStruct((1, d), F32),
            jax.ShapeDtypeStruct((1, LANES), F32),
        ],
        scratch_shapes=[pltpu.VMEM((8, d), F32), pltpu.VMEM((8, d), F32)],
    )(x3, g, target)


def _sgu_fwd(z, gain, ws_b, b_col, w_sgu, after=()):
    s = z.shape[0]
    groups = ws_b.shape[0]

    def body(zu_ref, zv_ref, gain_ref, ws_ref, b_ref, *rest):
        a_ref = rest[-1]
        vv = _gelu(zv_ref[...].astype(F32))
        r = lax.rsqrt(jnp.mean(vv * vv, axis=-1, keepdims=True) + EPS)
        vn = ((vv * r) * gain_ref[...]).astype(BF16)
        u = _gelu(zu_ref[...].astype(F32))
        for g in range(groups):
            sl = slice(g * BLK, (g + 1) * BLK)
            mixed = jnp.dot(ws_ref[g], vn[:, sl], preferred_element_type=F32) + b_ref[g]
            a_ref[:, sl] = (u[:, sl] * mixed).astype(BF16)

    return _pallas(
        body,
        name="sgu_fwd",
        grid=(s // BLK,),
        in_specs=[
            pl.BlockSpec((BLK, w_sgu), lambda c: (c, 0)),
            pl.BlockSpec((BLK, w_sgu), lambda c: (c, 1)),
            pl.BlockSpec((1, w_sgu), lambda c: (0, 0)),
            pl.BlockSpec((groups, BLK, BLK), lambda c: (0, 0, 0)),
            pl.BlockSpec((groups, BLK, 1), lambda c: (0, 0, 0)),
        ]
        + [ANY] * len(after),
        out_specs=pl.BlockSpec((BLK, w_sgu), lambda c: (c, 0)),
        out_shape=jax.ShapeDtypeStruct((s, w_sgu), BF16),
    )(z, z, gain, ws_b, b_col, *after)


def _sgu_bwd(z, da, gain, ws_b, wst_b, b_col, w_sgu, dz, after=()):
    s = z.shape[0]
    groups = ws_b.shape[0]
    n = s // BLK
    n_skip = 1 + len(after)

    def body(zu_ref, zv_ref, da_ref, gain_ref, ws_ref, wst_ref, b_ref, *rest):
        dz_ref, dws_ref, dbs_ref, dgain_ref, acc_gain, vv_s, gv_s, dxh_s = rest[n_skip:]
        c = pl.program_id(0)
        cols = [slice(g * BLK, (g + 1) * BLK) for g in range(groups)]

        ss = jnp.zeros((BLK, 1), F32)
        for sl in cols:
            zv = zv_ref[:, sl].astype(F32)
            vv = _gelu(zv)
            vv_s[:, sl] = vv
            gv_s[:, sl] = _gelu_grad(zv)
            ss = ss + jnp.sum(vv * vv, axis=-1, keepdims=True)
        r = lax.rsqrt(ss * (1.0 / w_sgu) + EPS)

        dot_dx = jnp.zeros((BLK, 1), F32)
        for g, sl in enumerate(cols):
            gain_g = gain_ref[:, sl]
            xh = vv_s[:, sl] * r
            vn = (xh * gain_g).astype(BF16)
            zu = zu_ref[:, sl].astype(F32)
            dav = da_ref[:, sl].astype(F32)
            dmix = dav * _gelu(zu)
            dmix_b = dmix.astype(BF16)
            mixed = jnp.dot(ws_ref[g], vn, preferred_element_type=F32) + b_ref[g]
            dz_ref[:, sl] = (dav * mixed * _gelu_grad(zu)).astype(BF16)
            dvn = jnp.dot(wst_ref[g], dmix_b, preferred_element_type=F32)
            dws_g = lax.dot_general(dmix_b, vn, NT, preferred_element_type=F32)
            dbs_g = jnp.sum(dmix, axis=1, keepdims=True)
            pg = _rows8(dvn * xh)

            @pl.when(c == 0)
            def _():
                dws_ref[g] = dws_g
                dbs_ref[g] = dbs_g
                acc_gain[:, sl] = pg

            @pl.when(c > 0)
            def _():
                dws_ref[g] += dws_g
                dbs_ref[g] += dbs_g
                acc_gain[:, sl] += pg

            dxh = dvn * gain_g
            dxh_s[:, sl] = dxh
            dot_dx = dot_dx + jnp.sum(dxh * xh, axis=-1, keepdims=True)

        mean_dx = dot_dx * (1.0 / w_sgu)
        for g, sl in enumerate(cols):
            dvv = r * (dxh_s[:, sl] - (vv_s[:, sl] * r) * mean_dx)
            dz_ref[:, w_sgu + g * BLK : w_sgu + (g + 1) * BLK] = (dvv * gv_s[:, sl]).astype(BF16)

        @pl.when(c == n - 1)
        def _():
            dgain_ref[...] = jnp.sum(acc_gain[...], axis=0, keepdims=True)

    full3 = pl.BlockSpec((groups, BLK, BLK), lambda c: (0, 0, 0))
    col3 = pl.BlockSpec((groups, BLK, 1), lambda c: (0, 0, 0))
    vec = pl.BlockSpec((1, w_sgu), lambda c: (0, 0))
    return _pallas(
        body,
        name="sgu_bwd",
        grid=(n,),
        in_specs=[
            pl.BlockSpec((BLK, w_sgu), lambda c: (c, 0)),
            pl.BlockSpec((BLK, w_sgu), lambda c: (c, 1)),
            pl.BlockSpec((BLK, w_sgu), lambda c: (c, 0)),
            vec,
            full3,
            full3,
            col3,
            ANY,
        ]
        + [ANY] * len(after),
        out_specs=[pl.BlockSpec((BLK, 2 * w_sgu), lambda c: (c, 0)), full3, col3, vec],
        out_shape=[
            jax.ShapeDtypeStruct(dz.shape, BF16),
            jax.ShapeDtypeStruct((groups, BLK, BLK), F32),
            jax.ShapeDtypeStruct((groups, BLK, 1), F32),
            jax.ShapeDtypeStruct((1, w_sgu), F32),
        ],
        scratch_shapes=[pltpu.VMEM((8, w_sgu), F32)] + [pltpu.VMEM((BLK, w_sgu), F32)] * 3,
        input_output_aliases={7: 0},
    )(z, z, da, gain, ws_b, wst_b, b_col, dz, *after)


def _attn_softmax(sink_ref, q_ref, k_ref, v_ref, bias_ref, s_len, grp):
    kv = pl.program_id(0)
    n = pl.program_id(1)
    start = pl.multiple_of(n * BLK, BLK)
    kb = k_ref[pl.ds(start, 3 * BLK), :]
    vb = v_ref[pl.ds(start, 3 * BLK), :]
    qv = q_ref[...]
    qs = jnp.concatenate([qv[:, g * HEAD_DIM : (g + 1) * HEAD_DIM] for g in range(grp)], axis=0).astype(BF16)
    sc = lax.dot_general(qs, kb, NT, preferred_element_type=F32) * (HEAD_DIM**-0.5)
    sc = sc + bias_ref[...].reshape(grp * BLK, 3 * BLK)
    kpos = start + lax.broadcasted_iota(I32, (1, 3 * BLK), 1) - BLK
    sc = jnp.where((kpos >= 0) & (kpos < s_len), sc, NEG)
    sink = jnp.concatenate([jnp.full((BLK, 1), sink_ref[kv * grp + g], F32) for g in range(grp)], axis=0)
    m = jnp.maximum(jnp.max(sc, axis=-1, keepdims=True), sink)
    p = jnp.exp(sc - m)
    esink = jnp.exp(sink - m)
    den = jnp.sum(p, axis=-1, keepdims=True) + esink
    return start, qs, kb, vb, p / den, esink / den


def _attn_specs(s, grp, q_blk0):
    qw = grp * HEAD_DIM
    return [
        pl.BlockSpec(memory_space=pltpu.SMEM),
        pl.BlockSpec((BLK, qw), lambda kv, n: (n, q_blk0 + kv)),
        pl.BlockSpec((s + 2 * BLK, HEAD_DIM), lambda kv, n: (0, kv)),
        pl.BlockSpec((s + 2 * BLK, HEAD_DIM), lambda kv, n: (0, kv)),
        pl.BlockSpec((grp, BLK, 3 * BLK), lambda kv, n: (kv, 0, 0)),
    ]


def _attn_fwd(sink, z, k_pad, v_pad, bias_tab, grp, q_blk0):
    s = z.shape[0]
    qw = grp * HEAD_DIM

    def body(sink_ref, q_ref, k_ref, v_ref, bias_ref, o_ref):
        _, _, _, vb, pn, _ = _attn_softmax(sink_ref, q_ref, k_ref, v_ref, bias_ref, s, grp)
        o = jnp.dot(pn.astype(BF16), vb, preferred_element_type=F32)
        for g in range(grp):
            o_ref[:, g * HEAD_DIM : (g + 1) * HEAD_DIM] = o[g * BLK : (g + 1) * BLK].astype(BF16)

    return _pallas(
        body,
        name="attn_fwd",
        grid=(N_KV_HEADS, s // BLK),
        in_specs=_attn_specs(s, grp, q_blk0),
        out_specs=pl.BlockSpec((BLK, qw), lambda kv, n: (n, kv)),
        out_shape=jax.ShapeDtypeStruct((s, N_KV_HEADS * qw), BF16),
    )(sink, z, k_pad, v_pad, bias_tab)


def _attn_bwd(sink, z, k_pad, v_pad, bias_tab, dout, dz, grp, q_blk0):
    s = z.shape[0]
    qw = grp * HEAD_DIM
    nb = s // BLK
    heads = N_KV_HEADS * grp

    def body(sink_ref, q_ref, k_ref, v_ref, bias_ref, do_ref, dz_in, dq_ref, dk_ref, dv_ref, dbias_ref, dsink_ref, dk_acc, dv_acc):
        del dz_in
        kv = pl.program_id(0)
        n = pl.program_id(1)
        start, qs, kb, vb, pn, psink = _attn_softmax(sink_ref, q_ref, k_ref, v_ref, bias_ref, s, grp)
        dov = do_ref[...]
        dos = jnp.concatenate([dov[:, g * HEAD_DIM : (g + 1) * HEAD_DIM] for g in range(grp)], axis=0)
        dp = lax.dot_general(dos, vb, NT, preferred_element_type=F32)
        dvb = lax.dot_general(pn.astype(BF16), dos, TN, preferred_element_type=F32)
        delta = jnp.sum(pn * dp, axis=-1, keepdims=True)
        ds = pn * (dp - delta)
        dsb = (ds * (HEAD_DIM**-0.5)).astype(BF16)
        dq = jnp.dot(dsb, kb, preferred_element_type=F32)
        dkb = lax.dot_general(dsb, qs, TN, preferred_element_type=F32)
        for g in range(grp):
            dq_ref[:, g * HEAD_DIM : (g + 1) * HEAD_DIM] = dq[g * BLK : (g + 1) * BLK].astype(BF16)

        @pl.when(n == 0)
        def _():
            dk_acc[...] = jnp.zeros_like(dk_acc)
            dv_acc[...] = jnp.zeros_like(dv_acc)
            dbias_ref[...] = jnp.zeros_like(dbias_ref)

        @pl.when((n == 0) & (kv == 0))
        def _():
            dsink_ref[...] = jnp.zeros_like(dsink_ref)

        dk_acc[pl.ds(start, 3 * BLK), :] += dkb
        dv_acc[pl.ds(start, 3 * BLK), :] += dvb
        dbias_ref[...] += ds.reshape(grp, BLK, 3 * BLK)
        row = lax.broadcasted_iota(I32, (heads, LANES), 0)
        sd = psink * delta
        upd = jnp.zeros((heads, LANES), F32)
        for g in range(grp):
            upd = jnp.where(row == kv * grp + g, -jnp.sum(sd[g * BLK : (g + 1) * BLK]), upd)
        dsink_ref[...] += upd

        @pl.when(n == nb - 1)
        def _():
            dk_ref[...] = dk_acc[...]
            dv_ref[...] = dv_acc[...]

    pad_spec = pl.BlockSpec((s + 2 * BLK, HEAD_DIM), lambda kv, n: (0, kv))
    kvw = N_KV_HEADS * HEAD_DIM
    return _pallas(
        body,
        name="attn_bwd",
        grid=(N_KV_HEADS, nb),
        in_specs=_attn_specs(s, grp, q_blk0) + [pl.BlockSpec((BLK, qw), lambda kv, n: (n, kv)), ANY],
        out_specs=[
            pl.BlockSpec((BLK, qw), lambda kv, n: (n, q_blk0 + kv)),
            pad_spec,
            pad_spec,
            pl.BlockSpec((grp, BLK, 3 * BLK), lambda kv, n: (kv, 0, 0)),
            pl.BlockSpec((heads, LANES), lambda kv, n: (0, 0)),
        ],
        out_shape=[
            jax.ShapeDtypeStruct(dz.shape, BF16),
            jax.ShapeDtypeStruct((s + 2 * BLK, kvw), F32),
            jax.ShapeDtypeStruct((s + 2 * BLK, kvw), F32),
            jax.ShapeDtypeStruct((heads, BLK, 3 * BLK), F32),
            jax.ShapeDtypeStruct((heads, LANES), F32),
        ],
        scratch_shapes=[pltpu.VMEM((s + 2 * BLK, HEAD_DIM), F32), pltpu.VMEM((s + 2 * BLK, HEAD_DIM), F32)],
        input_output_aliases={6: 0},
    )(sink, z, k_pad, v_pad, bias_tab, dout, dz)


def _dkv_to_dz(dk_pad, dv_pad, dz, blk_idx):
    s = dz.shape[0]
    kvw = dk_pad.shape[1]

    def body(dk_ref, dv_ref, dz_in, out_ref):
        del dz_in
        out_ref[:, :kvw] = dk_ref[...].astype(BF16)
        out_ref[:, kvw:] = dv_ref[...].astype(BF16)

    src = pl.BlockSpec((BLK, kvw), lambda i: (i + 1, 0))
    return _pallas(
        body,
        name="dkv_to_dz",
        grid=(s // BLK,),
        in_specs=[src, src, ANY],
        out_specs=pl.BlockSpec((BLK, 2 * kvw), lambda i: (i, blk_idx)),
        out_shape=jax.ShapeDtypeStruct(dz.shape, BF16),
        input_output_aliases={2: 0},
    )(dk_pad, dv_pad, dz)


def _relbias_bwd(dbias_tab, bucket):
    heads = dbias_tab.shape[0]

    def body(dt_ref, bk_ref, out_ref):
        lane = lax.broadcasted_iota(I32, (1, LANES), 1)
        bk = bk_ref[...]
        rows = []
        for h in range(heads):
            dt = dt_ref[h]
            acc = jnp.zeros((1, LANES), F32)
            for b in range(REL_BUCKETS):
                acc = jnp.where(lane == b, jnp.sum(jnp.where(bk == b, dt, 0.0)), acc)
            rows.append(acc)
        out_ref[...] = jnp.concatenate(rows, axis=0)

    return _pallas(body, name="relbias_bwd", out_shape=jax.ShapeDtypeStruct((heads, LANES), F32))(dbias_tab, bucket)


def _t5_bucket(rel):
    nb = REL_BUCKETS // 2
    ret = jnp.where(rel > 0, nb, 0)
    n = jnp.abs(rel)
    max_exact = nb // 2
    nf = jnp.maximum(n, 1).astype(F32)
    large = max_exact + (jnp.log(nf / max_exact) / math.log(REL_MAX_DIST / max_exact) * (nb - max_exact)).astype(I32)
    large = jnp.minimum(large, nb - 1)
    return ret + jnp.where(n < max_exact, n, large)


def _band_tables(rel_bias):
    qi = jnp.arange(BLK)[:, None]
    kj = jnp.arange(3 * BLK)[None, :]
    rel = kj - BLK - qi
    bucket = _t5_bucket(rel).astype(I32)
    heads = rel_bias.shape[1]
    masked = jnp.where(jnp.abs(rel) <= BLK, bucket, -1)

    def body(rb_ref, bk_ref, out_ref):
        bk = bk_ref[...]
        for h in range(heads):
            tab = jnp.full(bk.shape, NEG, F32)
            for b in range(REL_BUCKETS):
                tab = jnp.where(bk == b, rb_ref[b, h], tab)
            out_ref[h] = tab

    bias_tab = _pallas(
        body,
        name="bias_table",
        in_specs=[pl.BlockSpec(memory_space=pltpu.SMEM), pl.BlockSpec(memory_space=pltpu.VMEM)],
        out_specs=pl.BlockSpec(memory_space=pltpu.VMEM),
        out_shape=jax.ShapeDtypeStruct((heads, BLK, 3 * BLK), F32),
    )(rel_bias.astype(F32), masked)
    return bias_tab, bucket


EW_BLOCK_ELEMS = 512 * 1024


def _ew_tiles(shape, elems=EW_BLOCK_ELEMS // 2):
    r, c = shape
    tn = c if c <= 2048 else _tile(c, (2048, 1920, 1536, 1408, 1024, 512))
    tm = _tile(r, [t for t in (1024, 512, 256, 128, 64, 32, 16, 8) if t * tn <= elems] or [8])
    return tm, tn


def _cast_into_full(name, qidx, w, kind, after=()):
    r, c = w.shape
    tm, tn = _ew_tiles(w.shape, EW_BLOCK_ELEMS)
    nbi, nbj = r // tm, c // tn
    if kind == "col":
        full, out_spec = (r, c * N_CHIPS), pl.BlockSpec((tm, tn), lambda i, j, q: (i, q[0] * nbj + j))
    else:
        full, out_spec = (r * N_CHIPS, c), pl.BlockSpec((tm, tn), lambda i, j, q: (q[0] * nbi + i, j))

    def body(q_ref, w_ref, *rest):
        del q_ref
        rest[-1][...] = w_ref[...].astype(BF16)

    return _pallas(
        body,
        name=name,
        grid_spec=pltpu.PrefetchScalarGridSpec(
            num_scalar_prefetch=1,
            grid=(nbi, nbj),
            in_specs=[pl.BlockSpec((tm, tn), lambda i, j, q: (i, j))] + [ANY] * len(after),
            out_specs=out_spec,
        ),
        out_shape=jax.ShapeDtypeStruct(full, BF16),
    )(qidx, w, *after)


def _adamw(name, w, g, m, v, after=()):
    tm, tn = _ew_tiles(w.shape, EW_BLOCK_ELEMS)
    if _nbytes(w.shape, F32) <= 1024 * 1024:
        tm, tn = w.shape
    spec = pl.BlockSpec((tm, tn), lambda i, j: (i, j))
    n_after = len(after)

    def body(w_ref, g_ref, m_ref, v_ref, *rest):
        d_ref, nm_ref, nv_ref, g_out_ref = rest[n_after:]
        gv = g_ref[...]
        g_out_ref[...] = gv
        nm = ADAM_B1 * m_ref[...] + (1.0 - ADAM_B1) * gv
        nv = ADAM_B2 * v_ref[...] + (1.0 - ADAM_B2) * (gv * gv)
        m_hat = nm / (1.0 - ADAM_B1**ADAM_STEP)
        v_hat = nv / (1.0 - ADAM_B2**ADAM_STEP)
        d_ref[...] = -ADAM_LR * (m_hat / (jnp.sqrt(v_hat) + ADAM_EPS) + ADAM_WD * w_ref[...])
        nm_ref[...] = nm
        nv_ref[...] = nv

    out = jax.ShapeDtypeStruct(w.shape, F32)
    return _pallas(
        body, name=name, grid=(w.shape[0] // tm, w.shape[1] // tn), in_specs=[spec] * 4 + [ANY] * n_after,
        out_specs=[spec] * 4, out_shape=[out, out, out, out],
        compiler_params=_params(_mm_vmem([((tm, tn), F32, 24)])),
    )(w, g, m, v, *after)


def _pair_add(name, cidx, g_full, r_sib, kind):
    hr, hc = r_sib.shape
    tm, tn = _ew_tiles((hr, hc), 2 * EW_BLOCK_ELEMS)
    nbi, nbj = hr // tm, hc // tn
    if kind == "col":
        g_spec = pl.BlockSpec((tm, tn), lambda i, j, c: (c[0] * nbi + i, j))
    else:
        g_spec = pl.BlockSpec((tm, tn), lambda i, j, c: (i, c[0] * nbj + j))
    spec = pl.BlockSpec((tm, tn), lambda i, j, c: (i, j))

    def body(c_ref, g_ref, r_ref, o_ref):
        del c_ref
        o_ref[...] = (g_ref[...].astype(F32) + r_ref[...].astype(F32)).astype(BF16)

    return _pallas(
        body,
        name=name,
        grid_spec=pltpu.PrefetchScalarGridSpec(num_scalar_prefetch=1, grid=(nbi, nbj), in_specs=[g_spec, spec], out_specs=spec),
        out_shape=jax.ShapeDtypeStruct((hr, hc), BF16),
        compiler_params=_params(_mm_vmem([((tm, tn), BF16, 6), ((tm, tn), F32, 3)])),
    )(cidx, g_full, r_sib)


def _chip_sum(name, qidx, c_half, r_ici, kind):
    _, pr, pc = r_ici.shape
    tm, tn = _ew_tiles((pr, pc), 2 * EW_BLOCK_ELEMS)
    nbi, nbj = pr // tm, pc // tn
    if kind == "col":
        own_spec = pl.BlockSpec((tm, tn), lambda i, j, q: (i, q[0] * nbj + j))
        full, out_spec = (2 * pr, pc), pl.BlockSpec((tm, tn), lambda i, j, q: (q[1] * nbi + i, j))
    else:
        own_spec = pl.BlockSpec((tm, tn), lambda i, j, q: (q[0] * nbi + i, j))
        full, out_spec = (pr, 2 * pc), pl.BlockSpec((tm, tn), lambda i, j, q: (i, q[1] * nbj + j))

    def body(q_ref, own_ref, r_ref, o_ref):
        q = q_ref[0]
        own = own_ref[...].astype(F32)
        recv = [r_ref[r].astype(F32) for r in range(3)]
        total = None
        for chip in range(N_CHIPS):
            d = chip ^ q
            term = jnp.where(d == 0, own, jnp.where(d == 2, recv[0], jnp.where(d == 1, recv[1], recv[2])))
            total = term if total is None else total + term
        o_ref[...] = total

    return _pallas(
        body,
        name=name,
        grid_spec=pltpu.PrefetchScalarGridSpec(
            num_scalar_prefetch=1,
            grid=(nbi, nbj),
            in_specs=[own_spec, pl.BlockSpec((3, tm, tn), lambda i, j, q: (0, i, j))],
            out_specs=out_spec,
        ),
        out_shape=jax.ShapeDtypeStruct(full, F32),
        compiler_params=_params(_mm_vmem([((tm, tn), BF16, 8), ((tm, tn), F32, 6)])),
    )(qidx, c_half, r_ici)


_REL_MASK = (2, 1, 3)


def _place():
    x, y, c = lax.axis_index("x"), lax.axis_index("y"), lax.axis_index("c")
    chips = [(1 - x, y), (x, 1 - y), (1 - x, 1 - y)]
    return x, y, c, 2 * x + y, chips


def _shard_view(ref, kind, chip):
    if kind == "col":
        w = ref.shape[1] // N_CHIPS
        return ref.at[:, pl.ds(pl.multiple_of(chip * w, LANES), w)]
    h = ref.shape[0] // N_CHIPS
    return ref.at[pl.ds(pl.multiple_of(chip * h, 16), h), :]


def _row_half(ref, half):
    h = ref.shape[0] // 2
    return ref.at[pl.ds(pl.multiple_of(half * h, 16), h), :]


def _pair_half(ref, kind, half):
    if kind == "col":
        return _row_half(ref, half)
    w = ref.shape[1] // 2
    return ref.at[:, pl.ds(pl.multiple_of(half * w, LANES), w)]


def _remote(src, dst, send_sem, recv_sem, dev):
    return pltpu.make_async_remote_copy(src_ref=src, dst_ref=dst, send_sem=send_sem, recv_sem=recv_sem, device_id=dev, device_id_type=MESH)


def _hbm(a):
    return pltpu.with_memory_space_constraint(a, pltpu.HBM)


def _gather_start(name, fulls, kinds, rels=(0, 1, 2), after=()):
    n_w = len(fulls)

    def body(*refs):
        g = refs[:n_w]
        send_sem, recv_sem = refs[n_w + len(after)], refs[n_w + len(after) + 1]
        token = refs[-1]
        _, _, c, q, chips = _place()
        for w in range(n_w):
            mine = _row_half(_shard_view(g[w], kinds[w], q), c)
            for r in rels if isinstance(rels, tuple) else rels[w]:
                _remote(mine, mine, send_sem.at[3 * w + r], recv_sem.at[3 * w + r], (*chips[r], c)).start()
        token[...] = jnp.zeros_like(token)

    res = _pallas(
        body,
        name=name,
        out_shape=(
            pltpu.SemaphoreType.DMA((3 * n_w,)),
            pltpu.SemaphoreType.DMA((3 * n_w,)),
            *[pltpu.HBM(f.shape, f.dtype) for f in fulls],
            jax.ShapeDtypeStruct((8, LANES), F32),
        ),
        in_specs=[HBM_SPEC] * n_w + [ANY] * len(after),
        out_specs=(SEM_SPEC, SEM_SPEC, *[HBM_SPEC] * n_w, pl.BlockSpec(memory_space=pltpu.VMEM)),
        input_output_aliases={w: w + 2 for w in range(n_w)},
        compiler_params=pltpu.CompilerParams(has_side_effects=EFFECT),
    )(*[_hbm(f) for f in fulls], *after)
    return res[0], res[1], list(res[2 : 2 + n_w]), res[-1]


def _relay_copies(kinds, waiting):
    def copies(refs, send_sem, recv_sem):
        _, _, c, q, chips = _place()
        out = []
        for i, kind in enumerate(kinds):
            for k, (src_rel, dst_rel) in enumerate(((0, 1), (1, 0))):
                held = _row_half(_row_half(_shard_view(refs[i], kind, q ^ _REL_MASK[src_rel]), c), k)
                far = _row_half(_row_half(_shard_view(refs[i], kind, q ^ _REL_MASK[2]), c), k)
                dst = far if waiting else held
                out.append(_remote(held, dst, send_sem.at[2 * i + k], recv_sem.at[2 * i + k], (*chips[dst_rel], c)))
        return out

    return copies


def _forward_copies(kinds, waiting, rels=(0, 1, 2), sem0=0):
    def copies(refs, send_sem, recv_sem):
        x, y, c, q, _ = _place()
        out = []
        for i, kind in enumerate(kinds):
            for k, r in enumerate(rels):
                quarter = _shard_view(refs[i], kind, q ^ _REL_MASK[r])
                landed = _row_half(quarter, c)
                dst = _row_half(quarter, 1 - c) if waiting else landed
                sem = sem0 + len(rels) * i + k
                out.append(_remote(landed, dst, send_sem.at[sem], recv_sem.at[sem], (x, y, 1 - c)))
        return out

    return copies


def _relay_and_forward_copies(kinds, waiting):
    forward = _forward_copies(kinds, waiting, rels=(0, 1), sem0=2 * len(kinds))
    relay = _relay_copies(kinds, waiting)
    return lambda refs, send_sem, recv_sem: forward(refs, send_sem, recv_sem) + relay(refs, send_sem, recv_sem)


def _gather_wait(name, fulls, kinds, w_ids, send_sem, recv_sem, after, rels=(0, 1, 2)):
    n = len(fulls)

    def body(*refs):
        g = refs[:n]
        s_sem, r_sem = refs[n], refs[n + 1]
        x, y, c, q, _ = _place()
        for i, w in enumerate(w_ids):
            mine = _row_half(_shard_view(g[i], kinds[i], q), c)
            for r in rels:
                landed = _row_half(_shard_view(g[i], kinds[i], q ^ _REL_MASK[r]), c)
                cp = _remote(mine, landed, s_sem.at[3 * w + r], r_sem.at[3 * w + r], (x, y, 1 - c))
                cp.wait_send()
                cp.wait_recv()

    res = _pallas(
        body,
        name=name,
        out_shape=[pltpu.HBM(f.shape, f.dtype) for f in fulls],
        in_specs=[HBM_SPEC] * n + [SEM_SPEC, SEM_SPEC, ANY],
        out_specs=[HBM_SPEC] * n,
        input_output_aliases={i: i for i in range(n)},
        compiler_params=pltpu.CompilerParams(has_side_effects=EFFECT),
    )(*fulls, send_sem, recv_sem, after)
    return list(res)


def _gather_forward(name, fulls, kinds, rels=(0, 1, 2)):
    n = len(fulls)

    def body(*refs):
        g = refs[n : 2 * n]
        send, recv = refs[2 * n :]
        _sibling_handshake()
        x, y, c, q, _ = _place()
        sib = (x, y, 1 - c)
        cps = []
        for i in range(n):
            for r in rels:
                landed = _row_half(_shard_view(g[i], kinds[i], q ^ _REL_MASK[r]), c)
                cps.append(_remote(landed, landed, send.at[i, r], recv.at[i, r], sib))
        for cp in cps:
            cp.start()
        for i in range(n):
            for r in rels:
                other = _row_half(_shard_view(g[i], kinds[i], q ^ _REL_MASK[r]), 1 - c)
                _remote(other, other, send.at[i, r], recv.at[i, r], sib).wait_recv()
        for cp in cps:
            cp.wait_send()

    res = _pallas(
        body,
        name=name,
        in_specs=[ANY] * n,
        out_specs=[ANY] * n,
        out_shape=[jax.ShapeDtypeStruct(f.shape, f.dtype) for f in fulls],
        scratch_shapes=[pltpu.SemaphoreType.DMA((n, 3)), pltpu.SemaphoreType.DMA((n, 3))],
        input_output_aliases={i: i for i in range(n)},
        compiler_params=pltpu.CompilerParams(collective_id=SIBLING_BARRIER_ID),
    )(*fulls)
    return list(res)


SIBLING_BARRIER_ID = 1


def _sibling_handshake():
    sib = (lax.axis_index("x"), lax.axis_index("y"), 1 - lax.axis_index("c"))
    barrier = pltpu.get_barrier_semaphore()
    pl.semaphore_signal(barrier, inc=1, device_id=sib, device_id_type=MESH)
    pl.semaphore_wait(barrier, 1)


def _split_start(name, bufs, n_sems, copies, sibling_only=False):
    n = len(bufs)

    def body(*refs):
        if sibling_only:
            _sibling_handshake()
        for cp in copies(refs[:n], refs[n], refs[n + 1]):
            cp.start()

    extra = {"collective_id": SIBLING_BARRIER_ID} if sibling_only else {}

    res = _pallas(
        body,
        name=name,
        out_shape=(
            pltpu.SemaphoreType.DMA((n_sems,)),
            pltpu.SemaphoreType.DMA((n_sems,)),
            *[pltpu.HBM(b.shape, b.dtype) for b in bufs],
        ),
        in_specs=[HBM_SPEC] * n,
        out_specs=(SEM_SPEC, SEM_SPEC, *[HBM_SPEC] * n),
        input_output_aliases={i: i + 2 for i in range(n)},
        compiler_params=pltpu.CompilerParams(has_side_effects=EFFECT, **extra),
    )(*[_hbm(b) for b in bufs])
    return res[0], res[1], list(res[2:])


def _split_wait(name, bufs, send_sem, recv_sem, copies, after):
    n = len(bufs)

    def body(*refs):
        for cp in copies(refs[:n], refs[n], refs[n + 1]):
            cp.wait_send()
            cp.wait_recv()

    res = _pallas(
        body,
        name=name,
        out_shape=[pltpu.HBM(b.shape, b.dtype) for b in bufs],
        in_specs=[HBM_SPEC] * n + [SEM_SPEC, SEM_SPEC, ANY],
        out_specs=[HBM_SPEC] * n,
        input_output_aliases={i: i for i in range(n)},
        compiler_params=pltpu.CompilerParams(has_side_effects=EFFECT),
    )(*bufs, send_sem, recv_sem, after)
    return list(res)


def _pair_exchange_copies(kinds):
    n = len(kinds)

    def copies(refs, send_sem, recv_sem):
        x, y, c, _, _ = _place()
        return [
            _remote(_pair_half(refs[w], kinds[w], 1 - c), refs[n + w], send_sem.at[w], recv_sem.at[w], (x, y, 1 - c))
            for w in range(n)
        ]

    return copies


def _pair_share_copies(kinds, waiting):
    def copies(refs, send_sem, recv_sem):
        x, y, c, _, _ = _place()
        out = []
        for w, kind in enumerate(kinds):
            mine = _pair_half(refs[w], kind, c)
            dst = _pair_half(refs[w], kind, 1 - c) if waiting else mine
            out.append(_remote(mine, dst, send_sem.at[w], recv_sem.at[w], (x, y, 1 - c)))
        return out

    return copies


def _piece_shape(half_shape, kind):
    r, c = half_shape
    return (3, r, c // N_CHIPS) if kind == "col" else (3, r // N_CHIPS, c)


def _chip_send_start(name, halves, kinds):
    n = len(halves)
    lands = [lax.empty(_piece_shape(h.shape, k), BF16) for h, k in zip(halves, kinds)]

    def body(*refs):
        h, land = refs[:n], refs[n : 2 * n]
        send_sem, recv_sem = refs[2 * n], refs[2 * n + 1]
        _, _, c, q, chips = _place()
        for i in range(n):
            for r, chip in enumerate(chips):
                piece = _shard_view(h[i], kinds[i], q ^ _REL_MASK[r])
                _remote(piece, land[i].at[r], send_sem.at[3 * i + r], recv_sem.at[3 * i + r], (*chip, c)).start()

    res = _pallas(
        body,
        name=name,
        out_shape=(
            pltpu.SemaphoreType.DMA((3 * n,)),
            pltpu.SemaphoreType.DMA((3 * n,)),
            *[pltpu.HBM(a.shape, a.dtype) for a in halves],
            *[pltpu.HBM(a.shape, a.dtype) for a in lands],
        ),
        in_specs=[HBM_SPEC] * (2 * n),
        out_specs=(SEM_SPEC, SEM_SPEC, *[HBM_SPEC] * (2 * n)),
        input_output_aliases={i: i + 2 for i in range(2 * n)},
        compiler_params=pltpu.CompilerParams(has_side_effects=EFFECT),
    )(*[_hbm(a) for a in halves], *[_hbm(a) for a in lands])
    return res[0], res[1], list(res[2 : 2 + n]), list(res[2 + n :])


def _chip_send_wait(name, halves, lands, kinds, send_sem, recv_sem, after):
    n = len(halves)

    def body(*refs):
        h, land = refs[:n], refs[n : 2 * n]
        s_sem, r_sem = refs[2 * n], refs[2 * n + 1]
        x, y, c, q, _ = _place()
        for i in range(n):
            for r in range(3):
                piece = _shard_view(h[i], kinds[i], q ^ _REL_MASK[r])
                cp = _remote(piece, land[i].at[r], s_sem.at[3 * i + r], r_sem.at[3 * i + r], (x, y, 1 - c))
                cp.wait_send()
                cp.wait_recv()

    res = _pallas(
        body,
        name=name,
        out_shape=[pltpu.HBM(a.shape, a.dtype) for a in halves] + [pltpu.HBM(a.shape, a.dtype) for a in lands],
        in_specs=[HBM_SPEC] * (2 * n) + [SEM_SPEC, SEM_SPEC, ANY],
        out_specs=[HBM_SPEC] * (2 * n),
        input_output_aliases={i: i for i in range(2 * n)},
        compiler_params=pltpu.CompilerParams(has_side_effects=EFFECT),
    )(*halves, *lands, send_sem, recv_sem, after)
    return list(res[:n]), list(res[n:])


def _small_exchange_copies(waiting):
    def copies(refs, send_sem, recv_sem):
        p, land = refs
        x, y, c, q, _ = _place()
        me = 2 * q + c
        out = []
        for dd in range(1, 2 * N_CHIPS):
            dev = (x ^ ((dd >> 2) & 1), y ^ ((dd >> 1) & 1), c ^ (dd & 1))
            dst = land.at[me ^ dd] if waiting else land.at[me]
            out.append(_remote(p, dst, send_sem.at[dd - 1], recv_sem.at[dd - 1], dev))
        return out

    return copies


def _small_sum(name, me_idx, p, land):
    rows = p.shape[0]
    n_dev = 2 * N_CHIPS

    def body(me_ref, p_ref, land_ref, o_ref):
        me = me_ref[0]
        total = None
        for dev in range(n_dev):
            term = jnp.where(me == dev, p_ref[...], land_ref[dev])
            total = term if total is None else total + term
        o_ref[...] = total

    return _pallas(
        body,
        name=name,
        grid_spec=pltpu.PrefetchScalarGridSpec(
            num_scalar_prefetch=1,
            grid=(1,),
            in_specs=[pl.BlockSpec((rows, LANES), lambda i, m: (0, 0)), pl.BlockSpec((n_dev, rows, LANES), lambda i, m: (0, 0, 0))],
            out_specs=pl.BlockSpec((rows, LANES), lambda i, m: (0, 0)),
        ),
        out_shape=jax.ShapeDtypeStruct(p.shape, F32),
    )(me_idx, p, land)


def _pack(parts):
    rows = []
    for a in parts:
        flat = a.reshape(-1).astype(F32)
        n = flat.shape[0]
        padded = -(-n // (8 * LANES)) * (8 * LANES)
        rows.append(jnp.pad(flat, (0, padded - n)).reshape(-1, LANES))
    return jnp.concatenate(rows, axis=0)


def _unpack(packed, shapes):
    out, row = [], 0
    for shp in shapes:
        n = int(np.prod(shp))
        nrows = -(-n // (8 * LANES)) * 8
        out.append(packed[row : row + nrows].reshape(-1)[:n].reshape(shp))
        row += nrows
    return out


def kernel(x, w_in, norm_mix, sgu_v_gain, sgu_w_s, sgu_b_s, w_a_out, attn_sink, rel_bias, w_b_out, w_o, norm_ffn, w_gate, w_up, w_down, norm_final, loss_target, m_w_in, m_norm_mix, m_sgu_v_gain, m_sgu_w_s, m_sgu_b_s, m_w_a_out, m_attn_sink, m_rel_bias, m_w_b_out, m_w_o, m_norm_ffn, m_w_gate, m_w_up, m_w_down, m_norm_final, v_w_in, v_norm_mix, v_sgu_v_gain, v_sgu_w_s, v_sgu_b_s, v_w_a_out, v_attn_sink, v_rel_bias, v_w_b_out, v_w_o, v_norm_ffn, v_w_gate, v_w_up, v_w_down, v_norm_final):
    s, d = x.shape[1], x.shape[2]
    w_sgu = sgu_v_gain.shape[1]
    groups = sgu_w_s.shape[1]
    heads = attn_sink.shape[1]
    grp = heads // N_KV_HEADS
    w_att = heads * HEAD_DIM
    w_kv = N_KV_HEADS * HEAD_DIM
    d_ff = w_gate.shape[2] * N_CHIPS
    n_in = w_in.shape[2] * N_CHIPS
    off_q = 2 * w_sgu
    off_k = off_q + w_att
    off_g = off_k + 2 * w_kv
    assert n_in == off_g + 2 * d and groups * BLK == w_sgu and s % BLK == 0

    x2d = x.reshape(s, d)
    tgt = loss_target.reshape(s, d)
    c_idx = lax.axis_index("c").astype(I32).reshape(1)
    q_idx = (2 * lax.axis_index("x") + lax.axis_index("y")).astype(I32).reshape(1)
    qc_idx = jnp.concatenate([q_idx, c_idx])

    W_IN, W_A, W_B, W_O, W_GATE, W_UP, W_DOWN = range(7)
    names = ["w_in", "w_a", "w_b", "w_o", "w_gate", "w_up", "w_down"]
    kinds = ["col", "col", "col", "row", "col", "col", "row"]
    big_w = [w_in[0], w_a_out[0], w_b_out[0], w_o[0], w_gate[0], w_up[0], w_down[0]]
    big_m = [m_w_in[0], m_w_a_out[0], m_w_b_out[0], m_w_o[0], m_w_gate[0], m_w_up[0], m_w_down[0]]
    big_v = [v_w_in[0], v_w_a_out[0], v_w_b_out[0], v_w_o[0], v_w_gate[0], v_w_up[0], v_w_down[0]]
    full_in = _cast_into_full("cast_w_in", q_idx, big_w[W_IN], kinds[W_IN])
    in_send, in_recv, (full_in,), token = _gather_start("gather_start_in", [full_in], [kinds[W_IN]], rels=(0, 1))
    rest = [_cast_into_full("cast_" + names[i], q_idx, big_w[i], kinds[i], after=(token,)) for i in range(1, 7)]

    ws_b = sgu_w_s[0].astype(BF16)
    wst_b = jnp.swapaxes(sgu_w_s[0], 1, 2).astype(BF16)
    b_col = sgu_b_s[0].reshape(groups, BLK, 1)
    bias_tab, bucket = _band_tables(rel_bias)
    sink = attn_sink[0]
    small_w = [norm_mix, sgu_v_gain, sgu_w_s, sgu_b_s, attn_sink, rel_bias, norm_ffn, norm_final]
    small_m = [m_norm_mix, m_sgu_v_gain, m_sgu_w_s, m_sgu_b_s, m_attn_sink, m_rel_bias, m_norm_ffn, m_norm_final]
    small_v = [v_norm_mix, v_sgu_v_gain, v_sgu_w_s, v_sgu_b_s, v_attn_sink, v_rel_bias, v_norm_ffn, v_norm_final]
    small_shapes = [w.shape for w in small_w]
    zero1 = jnp.zeros((1,), F32)
    pw, pm, pv = _pack(small_w + [zero1]), _pack(small_m + [zero1]), _pack(small_v + [zero1])
    h1 = _rms_fwd("rms_mix", x2d, norm_mix, after=(token, rest[-1], ws_b, wst_b, b_col, bias_tab, pw, pm, pv))
    (full_in,) = _gather_wait("gather_wait_in", [full_in], [kinds[W_IN]], [0], in_send, in_recv, h1, rels=(0, 1))
    relay_send, relay_recv, (full_in,) = _split_start(
        "gather_relay_in", [full_in], 4, _relay_and_forward_copies([kinds[W_IN]], False)
    )
    full_down = rest.pop()
    full_up = rest.pop()
    ag_send, ag_recv, rest, token = _gather_start("gather_start_rest", rest, kinds[1:5], rels=(0, 1), after=(full_in, full_up))
    (full_in,) = _split_wait(
        "gather_relay_wait_in", [full_in], relay_send, relay_recv, _relay_and_forward_copies([kinds[W_IN]], True), token
    )
    (g_in,) = _gather_forward("gather_fwd_in", [full_in], [kinds[W_IN]], rels=(2,))
    fulls = [g_in] + rest

    def relay_begin(tag, ids, after, copies=_relay_copies, sems_per_weight=2, source=None):
        ks = [kinds[i] for i in ids]
        send, recv, bufs, w_ids = source or (ag_send, ag_recv, [fulls[i] for i in ids], [i - 1 for i in ids])
        bufs = _gather_wait("gather_wait_" + tag, bufs, ks, w_ids, send, recv, after, rels=(0, 1))
        send, recv, bufs = _split_start("gather_relay_" + tag, bufs, sems_per_weight * len(ids), copies(ks, False))
        return tag, ks, send, recv, bufs

    def relay_end(state, after):
        tag, ks, send, recv, bufs = state
        bufs = _split_wait("gather_relay_wait_" + tag, bufs, send, recv, _relay_and_forward_copies(ks, True), after)
        return _gather_forward("gather_fwd_" + tag, bufs, ks, rels=(2,))

    def relay_end_async(state, after):
        tag, ks, send, recv, bufs = state
        bufs = _split_wait("gather_relay_wait_" + tag, bufs, send, recv, _relay_copies(ks, True), after)
        send, recv, bufs = _split_start("gather_fwd_start_" + tag, bufs, 3 * len(ks), _forward_copies(ks, False), sibling_only=True)
        return tag, ks, send, recv, bufs

    def forwarded(state, after):
        tag, ks, send, recv, bufs = state
        return _split_wait("gather_fwd_wait_" + tag, bufs, send, recv, _forward_copies(ks, True), after)

    tm = _tile(s, (1024, 512, 256, 128))

    tn = _tile(n_in, (768, 640, 512))
    z = _mm(
        "mm_z", (s // tm, n_in // tn, 1), [h1, g_in],
        [pl.BlockSpec((tm, d), lambda i, j, k: (i, 0)), pl.BlockSpec((d, tn), lambda i, j, k: (0, j))],
        [jax.ShapeDtypeStruct((s, n_in), BF16)], [pl.BlockSpec((tm, tn), lambda i, j, k: (i, j))],
        [(0, 1, NN, 0)], 1, (tm, tn), 1, lambda ins, vals, outs, cs: _put(outs[0], cs, vals[0]),
        _mm_vmem([((tm, d), BF16, 2), ((d, tn), BF16, 2), ((tm, tn), F32, 3)]),
    )[0]
    mix_relay = relay_begin("mix", [W_A, W_B, W_O], z, copies=_relay_and_forward_copies, sems_per_weight=4)
    up_send, up_recv, (full_up,), token = _gather_start(
        "gather_start_up", [full_up], [kinds[W_UP]], rels=(0, 1), after=(mix_relay[4][0],)
    )

    a_act = _sgu_fwd(z, sgu_v_gain, ws_b, b_col, w_sgu, after=(token,))

    kv_b = z[:, off_k:off_g]
    k_pad = jnp.pad(kv_b[:, :w_kv], ((BLK, BLK), (0, 0)))
    v_pad = jnp.pad(kv_b[:, w_kv:], ((BLK, BLK), (0, 0)))
    q_blk0 = off_q // (grp * HEAD_DIM)
    att = _attn_fwd(sink, z, k_pad, v_pad, bias_tab, grp, q_blk0)

    gate_relay = relay_begin("gate", [W_GATE], att)
    g_a, g_b, g_o = relay_end(mix_relay, gate_relay[4][0])

    tg = _tile(d, (512,))
    ga0, gb0 = off_g // tg, (off_g + d) // tg

    def ep_gate(ins, vals, outs, cs):
        sa, sb = _sigmoid(ins[4][:, cs].astype(F32)), _sigmoid(ins[5][:, cs].astype(F32))
        _put(outs[0], cs, sa * vals[0] + sb * vals[1])
        _put(outs[1], cs, vals[0])
        _put(outs[2], cs, vals[1])

    t_out = pl.BlockSpec((tm, tg), lambda i, j, k: (i, j))
    m_act, y_a, y_b = _mm(
        "mm_branches", (s // tm, d // tg, 1), [a_act, g_a, att, g_b, z, z],
        [pl.BlockSpec((tm, w_sgu), lambda i, j, k: (i, 0)), pl.BlockSpec((w_sgu, tg), lambda i, j, k: (0, j)),
         pl.BlockSpec((tm, w_att), lambda i, j, k: (i, 0)), pl.BlockSpec((w_att, tg), lambda i, j, k: (0, j)),
         pl.BlockSpec((tm, tg), lambda i, j, k: (i, ga0 + j)), pl.BlockSpec((tm, tg), lambda i, j, k: (i, gb0 + j))],
        [jax.ShapeDtypeStruct((s, d), BF16)] * 3,
        [t_out, t_out, t_out], [(0, 1, NN, 0), (2, 3, NN, 1)], 2, (tm, tg), 1, ep_gate,
        _mm_vmem([((tm, w_sgu), BF16, 4), ((w_sgu, tg), BF16, 4), ((tm, tg), F32, 12)]),    )

    tn = _tile(d, (1024, 512))

    def ep_residual(ins, vals, outs, cs):
        _put(outs[0], cs, ins[2][:, cs] + vals[0])

    up_relay = relay_begin("up", [W_UP], m_act, source=(up_send, up_recv, [full_up], [0]))
    down_send, down_recv, (full_down,), token = _gather_start(
        "gather_start_down", [full_down], [kinds[W_DOWN]], after=(up_relay[4][0],)
    )
    gate_fwd = relay_end_async(gate_relay, token)
    x2 = _mm(
        "mm_wo", (s // tm, d // tn, 1), [m_act, g_o, x2d],
        [pl.BlockSpec((tm, d), lambda i, j, k: (i, 0)), pl.BlockSpec((d, tn), lambda i, j, k: (0, j)),
         pl.BlockSpec((tm, tn), lambda i, j, k: (i, j))],
        [jax.ShapeDtypeStruct((s, d), F32)], [pl.BlockSpec((tm, tn), lambda i, j, k: (i, j))],
        [(0, 1, NN, 0)], 1, (tm, tn), 1, ep_residual,
        _mm_vmem([((tm, d), BF16, 2), ((d, tn), BF16, 2), ((tm, tn), F32, 5)]),
        after=(gate_fwd[4][0],),
    )[0]
    (g_gate,) = forwarded(gate_fwd, x2)
    up_fwd = relay_end_async(up_relay, g_gate)
    h2 = _rms_fwd("rms_ffn", x2, norm_ffn, after=(up_fwd[4][0],))
    (g_up,) = forwarded(up_fwd, h2)

    tf = _tile(d_ff, (512,))

    def ep_swiglu(ins, vals, outs, cs):
        gt, up = vals
        _put(outs[0], cs, gt)
        _put(outs[1], cs, up)
        _put(outs[2], cs, (gt * _sigmoid(gt)) * up)

    f_out = pl.BlockSpec((tm, tf), lambda i, j, k: (i, j))
    gt, up, f_act = _mm(
        "mm_gate_up", (s // tm, d_ff // tf, 1), [h2, g_gate, g_up],
        [pl.BlockSpec((tm, d), lambda i, j, k: (i, 0)), pl.BlockSpec((d, tf), lambda i, j, k: (0, j)),
         pl.BlockSpec((d, tf), lambda i, j, k: (0, j))],
        [jax.ShapeDtypeStruct((s, d_ff), BF16)] * 3,
        [f_out, f_out, f_out], [(0, 1, NN, 0), (0, 2, NN, 1)], 2, (tm, tf), 1, ep_swiglu,
        _mm_vmem([((tm, d), BF16, 2), ((d, tf), BF16, 4), ((tm, tf), F32, 8)]),    )
    (g_down,) = _gather_forward(
        "gather_fwd_ffn_out",
        _gather_wait("gather_wait_ffn_out", [full_down], [kinds[W_DOWN]], [0], down_send, down_recv, f_act),
        [kinds[W_DOWN]],
    )

    tkf = _tile(d_ff, (1408, 1024, 512))
    tml, tnl = _tile(s, (512, 256, 128)), _tile(d, (512,))
    x3 = _mm(
        "mm_down", (s // tml, d // tnl, 1), [f_act, g_down, x2],
        [pl.BlockSpec((tml, d_ff), lambda i, j, k: (i, 0)), pl.BlockSpec((d_ff, tnl), lambda i, j, k: (0, j)),
         pl.BlockSpec((tml, tnl), lambda i, j, k: (i, j))],
        [jax.ShapeDtypeStruct((s, d), F32)], [pl.BlockSpec((tml, tnl), lambda i, j, k: (i, j))],
        [(0, 1, NN, 0)], 1, (tml, tnl), 1, ep_residual,
        _mm_vmem([((tml, d_ff), BF16, 2), ((d_ff, tnl), BF16, 2), ((tml, tnl), F32, 6)]),    )[0]

    dx3, dx3b, dg_final, loss_part = _head(x3, norm_final.reshape(1, d), tgt)

    def reduce_a(tag, ids, grads):
        ks = [kinds[i] for i in ids]
        lands = [lax.empty((g.shape[0] // 2, g.shape[1]) if k == "col" else (g.shape[0], g.shape[1] // 2), BF16)
                 for g, k in zip(grads, ks)]
        send, recv, bufs = _split_start("pair_send_" + tag, list(grads) + lands, len(ids), _pair_exchange_copies(ks), sibling_only=True)
        return {"tag": tag, "ids": ids, "ks": ks, "pair": (send, recv, bufs), "token": bufs[0]}

    def reduce_b(st, after):
        tag, ids, ks = st["tag"], st["ids"], st["ks"]
        send, recv, bufs = st["pair"]
        bufs = _split_wait("pair_wait_" + tag, bufs, send, recv, _pair_exchange_copies(ks), after)
        grads, from_sib = bufs[: len(ids)], bufs[len(ids) :]
        halves = [_pair_add("pair_add_" + names[i], c_idx, g, r, k) for i, g, r, k in zip(ids, grads, from_sib, ks)]
        st["chip"] = _chip_send_start("chip_send_" + tag, halves, ks)
        st["token"] = st["chip"][2][0]

    def reduce_c(st, after):
        tag, ids, ks = st["tag"], st["ids"], st["ks"]
        send, recv, halves, lands = st["chip"]
        halves, lands = _chip_send_wait("chip_wait_" + tag, halves, lands, ks, send, recv, after)
        pieces = [_chip_sum("chip_sum_" + names[i], qc_idx, h, r, k) for i, h, r, k in zip(ids, halves, lands, ks)]
        st["share"] = _split_start("share_send_" + tag, pieces, len(ids), _pair_share_copies(ks, False), sibling_only=True)
        st["token"] = st["share"][2][0]

    def reduce_d(st, after):
        send, recv, bufs = st["share"]
        return _split_wait("share_wait_" + st["tag"], bufs, send, recv, _pair_share_copies(st["ks"], True), after)

    grads_big, upd = [None] * 7, [None] * 7

    def update(i, g, after=()):
        upd[i] = _adamw("adamw_" + names[i], big_w[i], g, big_m[i], big_v[i], after=after)
        grads_big[i] = upd[i][3]
        return upd[i][0]

    def finish(st, after):
        shared = reduce_d(st, after)
        after = shared[0]
        for i, g in zip(st["ids"], shared):
            after = update(i, g, (after,))
        return after

    def ep_swiglu_bwd(ins, vals, outs, cs):
        df = vals[0]
        gtv, upv = ins[2][:, cs].astype(F32), ins[3][:, cs].astype(F32)
        sg = _sigmoid(gtv)
        _put(outs[0], cs, df * upv * (sg + gtv * sg * (1.0 - sg)))
        _put(outs[1], cs, df * (gtv * sg))

    dgt, dup = _mm(
        "mm_dswiglu", (s // tm, d_ff // tf, 1), [dx3b, g_down, gt, up],
        [pl.BlockSpec((tm, d), lambda i, j, k: (i, 0)), pl.BlockSpec((tf, d), lambda i, j, k: (j, 0)), f_out, f_out],
        [jax.ShapeDtypeStruct((s, d_ff), BF16), jax.ShapeDtypeStruct((s, d_ff), BF16)], [f_out, f_out],
        [(0, 1, NT, 0)], 1, (tm, tf), 1, ep_swiglu_bwd,
        _mm_vmem([((tm, d), BF16, 2), ((tf, d), BF16, 2), ((tm, tf), F32, 8)]),    )

    def ep_store(ins, vals, outs, cs):
        for o, v in zip(outs, vals):
            _put(o, cs, v)

    twn = _tile(d, (1024, 512))
    gw_down = _mm(
        "mm_gw_down", (d_ff // tkf, d // twn, 1), [f_act, dx3b],
        [pl.BlockSpec((s, tkf), lambda i, j, k: (0, i)), pl.BlockSpec((s, twn), lambda i, j, k: (0, j))],
        [jax.ShapeDtypeStruct((d_ff, d), BF16)], [pl.BlockSpec((tkf, twn), lambda i, j, k: (i, j))],
        [(0, 1, TN, 0)], 1, (tkf, twn), 1, ep_store,
        _mm_vmem([((s, tkf), BF16, 3), ((s, twn), BF16, 2), ((tkf, twn), F32, 3)]),
    )[0]
    red_down = reduce_a("down", [W_DOWN], [gw_down])

    tn2 = _tile(d, (256,))
    dh2_specs = [pl.BlockSpec((tm, d_ff), lambda i, j, k: (i, 0)), pl.BlockSpec((tn2, d_ff), lambda i, j, k: (j, 0))]
    dh2_tile = pl.BlockSpec((tm, tn2), lambda i, j, k: (i, j))
    dh2_vmem = _mm_vmem([((tm, d_ff), BF16, 2), ((tn2, d_ff), BF16, 2), ((tm, tn2), F32, 7)])
    dh2 = _mm(
        "mm_dh2_gate", (s // tm, d // tn2, 1), [dgt, g_gate], dh2_specs,
        [jax.ShapeDtypeStruct((s, d), F32)], [dh2_tile], [(0, 1, NT, 0)], 1, (tm, tn2), 1, ep_store, dh2_vmem,
        after=(red_down["token"],),
    )[0]
    dh2 = _mm(
        "mm_dh2_up", (s // tm, d // tn2, 1), [dup, g_up, dh2], dh2_specs + [dh2_tile],
        [jax.ShapeDtypeStruct((s, d), F32)], [dh2_tile], [(0, 1, NT, 0)], 1, (tm, tn2), 1, ep_residual, dh2_vmem,
    )[0]
    reduce_b(red_down, dh2)

    twr = _tile(d, (1024, 512))
    w_tile = pl.BlockSpec((twr, tf), lambda i, j, k: (i, j))
    gw_gate, gw_up = _mm(
        "mm_gw_gate_up", (d // twr, d_ff // tf, 1), [h2, dgt, dup],
        [pl.BlockSpec((s, twr), lambda i, j, k: (0, i)), pl.BlockSpec((s, tf), lambda i, j, k: (0, j)),
         pl.BlockSpec((s, tf), lambda i, j, k: (0, j))],
        [jax.ShapeDtypeStruct((d, d_ff), BF16), jax.ShapeDtypeStruct((d, d_ff), BF16)], [w_tile, w_tile],
        [(0, 1, TN, 0), (0, 2, TN, 1)], 2, (twr, tf), 1, ep_store,
        _mm_vmem([((s, twr), BF16, 3), ((s, tf), BF16, 4), ((twr, tf), F32, 6)]),
        after=(red_down["token"],),
    )
    red_ffn = reduce_a("ffn_in", [W_GATE, W_UP], [gw_gate, gw_up])

    dx2, dx2b, dg_ffn = _rms_bwd("rms_ffn_bwd", x2, norm_ffn, dh2, dx3, after=(red_ffn["token"],))

    nj = d // tg

    def lo(j):
        return jnp.minimum(j, nj - 1)

    def gate_bwd_body(dx_ref, wo_ref, ga_ref, gb_ref, ya_ref, yb_ref, dya_ref, dyb_ref, dz_ref, keep):
        j = pl.program_id(1)

        @pl.when(j < nj)
        def _():
            dm = lax.dot_general(dx_ref[...], wo_ref[...], NT, preferred_element_type=F32)
            sa, sb = _sigmoid(ga_ref[...].astype(F32)), _sigmoid(gb_ref[...].astype(F32))
            dya_ref[...] = (dm * sa).astype(BF16)
            dyb_ref[...] = (dm * sb).astype(BF16)
            dz_ref[...] = (dm * ya_ref[...].astype(F32) * (sa * (1.0 - sa))).astype(BF16)
            keep[lo(j)] = (dm * yb_ref[...].astype(F32) * (sb * (1.0 - sb))).astype(BF16)

        @pl.when(j >= nj)
        def _():
            dz_ref[...] = keep[jnp.maximum(j - nj, 0)]

    t_lo = pl.BlockSpec((tm, tg), lambda i, j: (i, lo(j)))
    dya, dyb, dz = _pallas(
        gate_bwd_body,
        name="mm_dgate",
        grid=(s // tm, 2 * nj),
        in_specs=[
            pl.BlockSpec((tm, d), lambda i, j: (i, 0)),
            pl.BlockSpec((tg, d), lambda i, j: (lo(j), 0)),
            pl.BlockSpec((tm, tg), lambda i, j: (i, ga0 + lo(j))),
            pl.BlockSpec((tm, tg), lambda i, j: (i, gb0 + lo(j))),
            t_lo,
            t_lo,
        ],
        out_specs=[t_lo, t_lo, pl.BlockSpec((tm, tg), lambda i, j: (i, ga0 + j))],
        out_shape=[jax.ShapeDtypeStruct((s, d), BF16), jax.ShapeDtypeStruct((s, d), BF16), jax.ShapeDtypeStruct((s, n_in), BF16)],
        scratch_shapes=[pltpu.VMEM((nj, tm, tg), BF16)],
        compiler_params=_params(_mm_vmem([((tm, d), BF16, 2), ((tg, d), BF16, 2), ((tm, tg), F32, 14), ((nj, tm, tg), BF16, 1)])),
    )(dx2b, g_o, z, z, y_a, y_b)
    reduce_b(red_ffn, dya)
    reduce_c(red_down, red_ffn["token"])

    gw_o = _mm(
        "mm_gw_o", (d // twr, d // twn, 1), [m_act, dx2b],
        [pl.BlockSpec((s, twr), lambda i, j, k: (0, i)), pl.BlockSpec((s, twn), lambda i, j, k: (0, j))],
        [jax.ShapeDtypeStruct((d, d), BF16)], [pl.BlockSpec((twr, twn), lambda i, j, k: (i, j))],
        [(0, 1, TN, 0)], 1, (twr, twn), 1, ep_store,
        _mm_vmem([((s, twr), BF16, 3), ((s, twn), BF16, 2), ((twr, twn), F32, 3)]),
        after=(red_down["token"],),
    )[0]
    after_down = finish(red_down, gw_o)

    tb = _tile(w_sgu, (1024, 512))
    b_out = pl.BlockSpec((tm, tb), lambda i, j, k: (i, j))

    da, datt = _mm(
        "mm_dbranches", (s // tm, w_sgu // tb, 1), [dya, g_a, dyb, g_b],
        [pl.BlockSpec((tm, d), lambda i, j, k: (i, 0)), pl.BlockSpec((tb, d), lambda i, j, k: (j, 0)),
         pl.BlockSpec((tm, d), lambda i, j, k: (i, 0)), pl.BlockSpec((tb, d), lambda i, j, k: (j, 0))],
        [jax.ShapeDtypeStruct((s, w_sgu), BF16), jax.ShapeDtypeStruct((s, w_att), BF16)], [b_out, b_out],
        [(0, 1, NT, 0), (2, 3, NT, 1)], 2, (tm, tb), 1, ep_store,
        _mm_vmem([((tm, d), BF16, 4), ((tb, d), BF16, 4), ((tm, tb), F32, 6)]),
        after=(after_down,),
    )

    wb_tile = pl.BlockSpec((tb, twn), lambda i, j, k: (i, j))
    gw_a, gw_b = _mm(
        "mm_gw_branches", (w_sgu // tb, d // twn, 1), [a_act, dya, att, dyb],
        [pl.BlockSpec((s, tb), lambda i, j, k: (0, i)), pl.BlockSpec((s, twn), lambda i, j, k: (0, j)),
         pl.BlockSpec((s, tb), lambda i, j, k: (0, i)), pl.BlockSpec((s, twn), lambda i, j, k: (0, j))],
        [jax.ShapeDtypeStruct((w_sgu, d), BF16), jax.ShapeDtypeStruct((w_att, d), BF16)], [wb_tile, wb_tile],
        [(0, 1, TN, 0), (2, 3, TN, 1)], 2, (tb, twn), 1, ep_store,
        _mm_vmem([((s, tb), BF16, 5), ((s, twn), BF16, 4), ((tb, twn), F32, 6)]),
        after=(da,),
    )
    red_mix = reduce_a("mix", [W_O, W_A, W_B], [gw_o, gw_a, gw_b])

    dz, dws, dbs, dgain = _sgu_bwd(z, da, sgu_v_gain, ws_b, wst_b, b_col, w_sgu, dz, after=(red_mix["token"],))
    dz, dk_pad, dv_pad, dbias_tab, dsink = _attn_bwd(sink, z, k_pad, v_pad, bias_tab, datt, dz, grp, q_blk0)
    dz = _dkv_to_dz(dk_pad, dv_pad, dz, off_k // (2 * w_kv))
    drel = _relbias_bwd(dbias_tab, bucket)
    reduce_b(red_mix, dz)
    reduce_c(red_ffn, red_mix["token"])

    early = [dgain, dws, dbs, dsink[:, 0], drel[:, :REL_BUCKETS].T, dg_ffn, dg_final]
    p_early = _pack([g.reshape(shp) for g, shp in zip(early, small_shapes[1:])] + [loss_part[0, :1]])
    land = jnp.zeros((2 * N_CHIPS,) + p_early.shape, F32)
    sm_send, sm_recv, (p_early, land) = _split_start("small_send", [p_early, land], 2 * N_CHIPS - 1, _small_exchange_copies(False))

    tzn = _tile(n_in, (768, 640, 512))
    gw_in = _mm(
        "mm_gw_in", (d // twr, n_in // tzn, 1), [h1, dz],
        [pl.BlockSpec((s, twr), lambda i, j, k: (0, i)), pl.BlockSpec((s, tzn), lambda i, j, k: (0, j))],
        [jax.ShapeDtypeStruct((d, n_in), BF16)], [pl.BlockSpec((twr, tzn), lambda i, j, k: (i, j))],
        [(0, 1, TN, 0)], 1, (twr, tzn), 1, ep_store,
        _mm_vmem([((s, twr), BF16, 3), ((s, tzn), BF16, 2), ((twr, tzn), F32, 3)]),
        after=(red_ffn["token"], p_early),
    )[0]
    red_in = reduce_a("w_in", [W_IN], [gw_in])

    g_gate, g_up = reduce_d(red_ffn, red_in["token"])
    reduce_b(red_in, update(W_GATE, g_gate))

    dh1 = _mm(
        "mm_dh1", (s // tm, d // tn2, 1), [dz, g_in],
        [pl.BlockSpec((tm, n_in), lambda i, j, k: (i, 0)), pl.BlockSpec((tn2, n_in), lambda i, j, k: (j, 0))],
        [jax.ShapeDtypeStruct((s, d), F32)], [pl.BlockSpec((tm, tn2), lambda i, j, k: (i, j))],
        [(0, 1, NT, 0)], 1, (tm, tn2), 1, ep_store,
        _mm_vmem([((tm, n_in), BF16, 2), ((tn2, n_in), BF16, 2), ((tm, tn2), F32, 5)]),
        after=(red_in["token"],),
    )[0]

    reduce_c(red_mix, dh1)
    grad_x, _, dg_mix = _rms_bwd("rms_mix_bwd", x2d, norm_mix, dh1, dx2, after=(red_mix["token"],))

    p_mix = _pack([dg_mix.reshape(small_shapes[0])])
    land_mix = jnp.zeros((2 * N_CHIPS,) + p_mix.shape, F32)
    mx_send, mx_recv, (p_mix, land_mix) = _split_start("mix_send", [p_mix, land_mix], 2 * N_CHIPS - 1, _small_exchange_copies(False))

    reduce_c(red_in, update(W_UP, g_up, (finish(red_mix, p_mix),)))
    p_early, land = _split_wait("small_wait", [p_early, land], sm_send, sm_recv, _small_exchange_copies(True), red_in["token"])
    p_mix, land_mix = _split_wait("mix_wait", [p_mix, land_mix], mx_send, mx_recv, _small_exchange_copies(True), p_early)
    me_idx = 2 * q_idx + c_idx
    packed_g = jnp.concatenate([_small_sum("mix_sum", me_idx, p_mix, land_mix), _small_sum("small_sum", me_idx, p_early, land)], axis=0)
    g_small = _unpack(packed_g, small_shapes + [(1,)])
    loss = g_small[-1].reshape(())
    g_small = g_small[:-1]
    pg = _pack(g_small + [zero1])
    small_upd = _adamw("adamw_small", pw, pg, pm, pv)
    d_small, nm_small, nv_small = [_unpack(a, small_shapes) for a in small_upd[:3]]
    finish(red_in, small_upd[0])

    small_names = ["norm_mix", "sgu_v_gain", "sgu_w_s", "sgu_b_s", "attn_sink", "rel_bias", "norm_ffn", "norm_final"]
    table = {}
    for i, n in enumerate(names):
        table[n] = (grads_big[i][None], upd[i][0][None], upd[i][1][None], upd[i][2][None])
    for i, n in enumerate(small_names):
        table[n] = (g_small[i], d_small[i], nm_small[i], nv_small[i])
    order = ["w_in", "norm_mix", "sgu_v_gain", "sgu_w_s", "sgu_b_s", "w_a", "attn_sink", "rel_bias", "w_b", "w_o", "norm_ffn",
             "w_gate", "w_up", "w_down", "norm_final"]
    outs = [loss, grad_x.reshape(1, s, d)]
    for part in range(4):
        outs += [table[n][part] for n in order]
    return tuple(outs)
```

```python
import math

import jax
import jax.numpy as jnp
import numpy as np
from jax import lax
from jax.experimental import pallas as pl
from jax.experimental.pallas import tpu as pltpu

F32 = jnp.float32
BF16 = jnp.bfloat16
I32 = jnp.int32
MESH = pl.DeviceIdType.MESH

EPS = 1e-6
NEG = -1e30
BLK = 128
HEAD_DIM = 128
N_KV_HEADS = 2
REL_BUCKETS = 32
REL_MAX_DIST = 128
N_CHIPS = 4
ADAM_LR, ADAM_B1, ADAM_B2, ADAM_EPS, ADAM_WD, ADAM_STEP = 0.001, 0.9, 0.999, 1e-08, 0.01, 10

LANES = 128
VMEM_CAP = 60 * 1024 * 1024

NN = (((1,), (0,)), ((), ()))
NT = (((1,), (1,)), ((), ()))
TN = (((0,), (0,)), ((), ()))
ANY = pl.BlockSpec(memory_space=pl.ANY)
HBM_SPEC = pl.BlockSpec(memory_space=pltpu.HBM)
SEM_SPEC = pl.BlockSpec(memory_space=pltpu.SEMAPHORE)
EFFECT = pltpu.SideEffectType.DATAFLOW_SIDE_EFFECTING


def _tile(n, cands):
    for t in cands:
        if n % t == 0:
            return t
    return n


PIN_BYTES = 64 * 1024


def _pin_hbm(a):
    big = hasattr(a, "dtype") and jnp.issubdtype(a.dtype, jnp.floating) and _nbytes(a.shape, a.dtype) >= PIN_BYTES
    return pltpu.with_memory_space_constraint(a, pltpu.HBM) if big else a


def _pallas(body, *, out_shape, **kw):
    def pin(o):
        big = isinstance(o, jax.ShapeDtypeStruct) and jnp.issubdtype(o.dtype, jnp.floating) and _nbytes(o.shape, o.dtype) >= PIN_BYTES
        return pltpu.HBM(o.shape, o.dtype) if big else o

    shapes = type(out_shape)(pin(o) for o in out_shape) if isinstance(out_shape, (list, tuple)) else pin(out_shape)
    call = pl.pallas_call(body, out_shape=shapes, **kw)
    return lambda *args: call(*[_pin_hbm(a) for a in args])


def _params(vmem_bytes=None, **kw):
    if vmem_bytes is not None:
        kw["vmem_limit_bytes"] = int(min(max(vmem_bytes, 32 * 1024 * 1024), VMEM_CAP))
    return pltpu.CompilerParams(**kw)


def _nbytes(shape, dtype):
    return int(np.prod(shape)) * jnp.dtype(dtype).itemsize


def _sigmoid(x):
    return 1.0 / (1.0 + jnp.exp(-x))


_GC = 0.7978845608028654
_GA = 0.044715


def _gelu(x):
    return 0.5 * x * (1.0 + jnp.tanh(_GC * (x + _GA * (x * x * x))))


def _gelu_grad(x):
    t = jnp.tanh(_GC * (x + _GA * (x * x * x)))
    return 0.5 * (1.0 + t) + 0.5 * x * (1.0 - t * t) * (_GC * (1.0 + 3.0 * _GA * (x * x)))


def _bf(v):
    return v if v.dtype == BF16 else v.astype(BF16)


def _mm(name, grid, ins, in_specs, out_shape, out_specs, pairs, n_acc, tile, nk, epilogue, vmem_bytes, after=()):
    assert nk == 1
    n_in, n_out = len(ins) + len(after), len(out_shape)

    def body(*refs):
        in_refs, out_refs = refs[:n_in], refs[n_in : n_in + n_out]
        vals = [None] * n_acc
        for a_i, b_i, dn, acc_i in pairs:
            d = lax.dot_general(_bf(in_refs[a_i][...]), _bf(in_refs[b_i][...]), dn, preferred_element_type=F32)
            vals[acc_i] = d if vals[acc_i] is None else vals[acc_i] + d
        epilogue(in_refs, vals, out_refs, slice(None))

    return _pallas(
        body,
        name=name,
        grid=grid,
        in_specs=list(in_specs) + [ANY] * len(after),
        out_specs=out_specs,
        out_shape=out_shape,
        compiler_params=_params(vmem_bytes),
    )(*ins, *after)


def _put(ref, cs, v):
    ref[:, cs] = v.astype(ref.dtype)


def _mm_vmem(tiles):
    return sum(_nbytes(s, d) * c for s, d, c in tiles) + 4 * 1024 * 1024


def _rows8(v):
    r, d = v.shape
    return v.reshape(r // 8, 8, d).sum(axis=0)


def _rms_fwd(name, x, g, after=()):
    s, d = x.shape
    tm = _tile(s, (256, 128))

    def body(x_ref, g_ref, *rest):
        h_ref = rest[-1]
        xv = x_ref[...]
        r = lax.rsqrt(jnp.mean(xv * xv, axis=-1, keepdims=True) + EPS)
        h_ref[...] = ((xv * r) * g_ref[...]).astype(BF16)

    return _pallas(
        body,
        name=name,
        grid=(s // tm,),
        in_specs=[pl.BlockSpec((tm, d), lambda i: (i, 0)), pl.BlockSpec((1, d), lambda i: (0, 0))] + [ANY] * len(after),
        out_specs=pl.BlockSpec((tm, d), lambda i: (i, 0)),
        out_shape=jax.ShapeDtypeStruct((s, d), BF16),
    )(x, g, *after)


def _rms_bwd(name, x, g, dh, dres, after=()):
    s, d = x.shape
    tm = _tile(s, (256, 128))
    n = s // tm
    n_after = len(after)

    def body(x_ref, g_ref, dh_ref, dres_ref, *rest):
        dx_ref, dxb_ref, dg_ref, acc_ref = rest[n_after:]
        i = pl.program_id(0)
        xv = x_ref[...]
        r = lax.rsqrt(jnp.mean(xv * xv, axis=-1, keepdims=True) + EPS)
        xh = xv * r
        dhv = dh_ref[...]
        dxh = dhv * g_ref[...]
        dx = r * (dxh - xh * jnp.mean(dxh * xh, axis=-1, keepdims=True)) + dres_ref[...]
        dx_ref[...] = dx
        dxb_ref[...] = dx.astype(BF16)
        part = _rows8(dhv * xh)

        @pl.when(i == 0)
        def _():
            acc_ref[...] = part

        @pl.when(i > 0)
        def _():
            acc_ref[...] += part

        @pl.when(i == n - 1)
        def _():
            dg_ref[...] = jnp.sum(acc_ref[...], axis=0, keepdims=True)

    row = pl.BlockSpec((tm, d), lambda i: (i, 0))
    vec = pl.BlockSpec((1, d), lambda i: (0, 0))
    return _pallas(
        body,
        name=name,
        grid=(n,),
        in_specs=[row, vec, row, row] + [ANY] * n_after,
        out_specs=[row, row, vec],
        out_shape=[jax.ShapeDtypeStruct((s, d), F32), jax.ShapeDtypeStruct((s, d), BF16), jax.ShapeDtypeStruct((1, d), F32)],
        scratch_shapes=[pltpu.VMEM((8, d), F32)],
    )(x, g, dh, dres, *after)


def _head(x3, g, target):
    s, d = x3.shape
    tm = _tile(s, (256, 128))
    n = s // tm

    def body(x_ref, g_ref, t_ref, dx_ref, dxb_ref, dg_ref, loss_ref, acc_g, acc_l):
        i = pl.program_id(0)
        xv = x_ref[...]
        gv = g_ref[...]
        r = lax.rsqrt(jnp.mean(xv * xv, axis=-1, keepdims=True) + EPS)
        xh = xv * r
        e = xh * gv - t_ref[...]
        dy = e * (1.0 / d)
        dxh = dy * gv
        dx = r * (dxh - xh * jnp.mean(dxh * xh, axis=-1, keepdims=True))
        dx_ref[...] = dx
        dxb_ref[...] = dx.astype(BF16)
        pg = _rows8(dy * xh)
        plo = _rows8(e * e)

        @pl.when(i == 0)
        def _():
            acc_g[...] = pg
            acc_l[...] = plo

        @pl.when(i > 0)
        def _():
            acc_g[...] += pg
            acc_l[...] += plo

        @pl.when(i == n - 1)
        def _():
            dg_ref[...] = jnp.sum(acc_g[...], axis=0, keepdims=True)
            loss_ref[...] = jnp.full((1, LANES), (0.5 / d) * jnp.sum(acc_l[...]), F32)

    row = pl.BlockSpec((tm, d), lambda i: (i, 0))
    vec = pl.BlockSpec((1, d), lambda i: (0, 0))
    return _pallas(
        body,
        name="head",
        grid=(n,),
        in_specs=[row, vec, row],
        out_specs=[row, row, vec, pl.BlockSpec((1, LANES), lambda i: (0, 0))],
        out_shape=[
            jax.ShapeDtypeStruct((s, d), F32),
            jax.ShapeDtypeStruct((s, d), BF16),
            jax.ShapeDtypeStruct((1, d), F32),
            jax.ShapeDtypeStruct((1, LANES), F32),
        ],
        scratch_shapes=[pltpu.VMEM((8, d), F32), pltpu.VMEM((8, d), F32)],
    )(x3, g, target)


def _sgu_fwd(z, gain, ws_b, b_col, w_sgu, after=()):
    s = z.shape[0]
    groups = ws_b.shape[0]

    def body(zu_ref, zv_ref, gain_ref, ws_ref, b_ref, *rest):
        a_ref = rest[-1]
        vv = _gelu(zv_ref[...].astype(F32))
        r = lax.rsqrt(jnp.mean(vv * vv, axis=-1, keepdims=True) + EPS)
        vn = ((vv * r) * gain_ref[...]).astype(BF16)
        u = _gelu(zu_ref[...].astype(F32))
        for g in range(groups):
            sl = slice(g * BLK, (g + 1) * BLK)
            mixed = jnp.dot(ws_ref[g], vn[:, sl], preferred_element_type=F32) + b_ref[g]
            a_ref[:, sl] = (u[:, sl] * mixed).astype(BF16)

    return _pallas(
        body,
        name="sgu_fwd",
        grid=(s // BLK,),
        in_specs=[
            pl.BlockSpec((BLK, w_sgu), lambda c: (c, 0)),
            pl.BlockSpec((BLK, w_sgu), lambda c: (c, 1)),
            pl.BlockSpec((1, w_sgu), lambda c: (0, 0)),
            pl.BlockSpec((groups, BLK, BLK), lambda c: (0, 0, 0)),
            pl.BlockSpec((groups, BLK, 1), lambda c: (0, 0, 0)),
        ]
        + [ANY] * len(after),
        out_specs=pl.BlockSpec((BLK, w_sgu), lambda c: (c, 0)),
        out_shape=jax.ShapeDtypeStruct((s, w_sgu), BF16),
    )(z, z, gain, ws_b, b_col, *after)


def _sgu_bwd(z, da, gain, ws_b, wst_b, b_col, w_sgu, dz, after=()):
    s = z.shape[0]
    groups = ws_b.shape[0]
    n = s // BLK
    n_skip = 1 + len(after)

    def body(zu_ref, zv_ref, da_ref, gain_ref, ws_ref, wst_ref, b_ref, *rest):
        dz_ref, dws_ref, dbs_ref, dgain_ref, acc_gain, vv_s, gv_s, dxh_s = rest[n_skip:]
        c = pl.program_id(0)
        cols = [slice(g * BLK, (g + 1) * BLK) for g in range(groups)]

        ss = jnp.zeros((BLK, 1), F32)
        for sl in cols:
            zv = zv_ref[:, sl].astype(F32)
            vv = _gelu(zv)
            vv_s[:, sl] = vv
            gv_s[:, sl] = _gelu_grad(zv)
            ss = ss + jnp.sum(vv * vv, axis=-1, keepdims=True)
        r = lax.rsqrt(ss * (1.0 / w_sgu) + EPS)

        dot_dx = jnp.zeros((BLK, 1), F32)
        for g, sl in enumerate(cols):
            gain_g = gain_ref[:, sl]
            xh = vv_s[:, sl] * r
            vn = (xh * gain_g).astype(BF16)
            zu = zu_ref[:, sl].astype(F32)
            dav = da_ref[:, sl].astype(F32)
            dmix = dav * _gelu(zu)
            dmix_b = dmix.astype(BF16)
            mixed = jnp.dot(ws_ref[g], vn, preferred_element_type=F32) + b_ref[g]
            dz_ref[:, sl] = (dav * mixed * _gelu_grad(zu)).astype(BF16)
            dvn = jnp.dot(wst_ref[g], dmix_b, preferred_element_type=F32)
            dws_g = lax.dot_general(dmix_b, vn, NT, preferred_element_type=F32)
            dbs_g = jnp.sum(dmix, axis=1, keepdims=True)
            pg = _rows8(dvn * xh)

            @pl.when(c == 0)
            def _():
                dws_ref[g] = dws_g
                dbs_ref[g] = dbs_g
                acc_gain[:, sl] = pg

            @pl.when(c > 0)
            def _():
                dws_ref[g] += dws_g
                dbs_ref[g] += dbs_g
                acc_gain[:, sl] += pg

            dxh = dvn * gain_g
            dxh_s[:, sl] = dxh
            dot_dx = dot_dx + jnp.sum(dxh * xh, axis=-1, keepdims=True)

        mean_dx = dot_dx * (1.0 / w_sgu)
        for g, sl in enumerate(cols):
            dvv = r * (dxh_s[:, sl] - (vv_s[:, sl] * r) * mean_dx)
            dz_ref[:, w_sgu + g * BLK : w_sgu + (g + 1) * BLK] = (dvv * gv_s[:, sl]).astype(BF16)

        @pl.when(c == n - 1)
        def _():
            dgain_ref[...] = jnp.sum(acc_gain[...], axis=0, keepdims=True)

    full3 = pl.BlockSpec((groups, BLK, BLK), lambda c: (0, 0, 0))
    col3 = pl.BlockSpec((groups, BLK, 1), lambda c: (0, 0, 0))
    vec = pl.BlockSpec((1, w_sgu), lambda c: (0, 0))
    return _pallas(
        body,
        name="sgu_bwd",
        grid=(n,),
        in_specs=[
            pl.BlockSpec((BLK, w_sgu), lambda c: (c, 0)),
            pl.BlockSpec((BLK, w_sgu), lambda c: (c, 1)),
            pl.BlockSpec((BLK, w_sgu), lambda c: (c, 0)),
            vec,
            full3,
            full3,
            col3,
            ANY,
        ]
        + [ANY] * len(after),
        out_specs=[pl.BlockSpec((BLK, 2 * w_sgu), lambda c: (c, 0)), full3, col3, vec],
        out_shape=[
            jax.ShapeDtypeStruct(dz.shape, BF16),
            jax.ShapeDtypeStruct((groups, BLK, BLK), F32),
            jax.ShapeDtypeStruct((groups, BLK, 1), F32),
            jax.ShapeDtypeStruct((1, w_sgu), F32),
        ],
        scratch_shapes=[pltpu.VMEM((8, w_sgu), F32)] + [pltpu.VMEM((BLK, w_sgu), F32)] * 3,
        input_output_aliases={7: 0},
    )(z, z, da, gain, ws_b, wst_b, b_col, dz, *after)


def _attn_softmax(sink_ref, q_ref, k_ref, v_ref, bias_ref, s_len, grp):
    kv = pl.program_id(0)
    n = pl.program_id(1)
    start = pl.multiple_of(n * BLK, BLK)
    kb = k_ref[pl.ds(start, 3 * BLK), :]
    vb = v_ref[pl.ds(start, 3 * BLK), :]
    qv = q_ref[...]
    qs = jnp.concatenate([qv[:, g * HEAD_DIM : (g + 1) * HEAD_DIM] for g in range(grp)], axis=0).astype(BF16)
    sc = lax.dot_general(qs, kb, NT, preferred_element_type=F32) * (HEAD_DIM**-0.5)
    sc = sc + bias_ref[...].reshape(grp * BLK, 3 * BLK)
    kpos = start + lax.broadcasted_iota(I32, (1, 3 * BLK), 1) - BLK
    sc = jnp.where((kpos >= 0) & (kpos < s_len), sc, NEG)
    sink = jnp.concatenate([jnp.full((BLK, 1), sink_ref[kv * grp + g], F32) for g in range(grp)], axis=0)
    m = jnp.maximum(jnp.max(sc, axis=-1, keepdims=True), sink)
    p = jnp.exp(sc - m)
    esink = jnp.exp(sink - m)
    den = jnp.sum(p, axis=-1, keepdims=True) + esink
    return start, qs, kb, vb, p / den, esink / den


def _attn_specs(s, grp, q_blk0):
    qw = grp * HEAD_DIM
    return [
        pl.BlockSpec(memory_space=pltpu.SMEM),
        pl.BlockSpec((BLK, qw), lambda kv, n: (n, q_blk0 + kv)),
        pl.BlockSpec((s + 2 * BLK, HEAD_DIM), lambda kv, n: (0, kv)),
        pl.BlockSpec((s + 2 * BLK, HEAD_DIM), lambda kv, n: (0, kv)),
        pl.BlockSpec((grp, BLK, 3 * BLK), lambda kv, n: (kv, 0, 0)),
    ]


def _attn_fwd(sink, z, k_pad, v_pad, bias_tab, grp, q_blk0):
    s = z.shape[0]
    qw = grp * HEAD_DIM

    def body(sink_ref, q_ref, k_ref, v_ref, bias_ref, o_ref):
        _, _, _, vb, pn, _ = _attn_softmax(sink_ref, q_ref, k_ref, v_ref, bias_ref, s, grp)
        o = jnp.dot(pn.astype(BF16), vb, preferred_element_type=F32)
        for g in range(grp):
            o_ref[:, g * HEAD_DIM : (g + 1) * HEAD_DIM] = o[g * BLK : (g + 1) * BLK].astype(BF16)

    return _pallas(
        body,
        name="attn_fwd",
        grid=(N_KV_HEADS, s // BLK),
        in_specs=_attn_specs(s, grp, q_blk0),
        out_specs=pl.BlockSpec((BLK, qw), lambda kv, n: (n, kv)),
        out_shape=jax.ShapeDtypeStruct((s, N_KV_HEADS * qw), BF16),
    )(sink, z, k_pad, v_pad, bias_tab)


def _attn_bwd(sink, z, k_pad, v_pad, bias_tab, dout, dz, grp, q_blk0):
    s = z.shape[0]
    qw = grp * HEAD_DIM
    nb = s // BLK
    heads = N_KV_HEADS * grp

    def body(sink_ref, q_ref, k_ref, v_ref, bias_ref, do_ref, dz_in, dq_ref, dk_ref, dv_ref, dbias_ref, dsink_ref, dk_acc, dv_acc):
        del dz_in
        kv = pl.program_id(0)
        n = pl.program_id(1)
        start, qs, kb, vb, pn, psink = _attn_softmax(sink_ref, q_ref, k_ref, v_ref, bias_ref, s, grp)
        dov = do_ref[...]
        dos = jnp.concatenate([dov[:, g * HEAD_DIM : (g + 1) * HEAD_DIM] for g in range(grp)], axis=0)
        dp = lax.dot_general(dos, vb, NT, preferred_element_type=F32)
        dvb = lax.dot_general(pn.astype(BF16), dos, TN, preferred_element_type=F32)
        delta = jnp.sum(pn * dp, axis=-1, keepdims=True)
        ds = pn * (dp - delta)
        dsb = (ds * (HEAD_DIM**-0.5)).astype(BF16)
        dq = jnp.dot(dsb, kb, preferred_element_type=F32)
        dkb = lax.dot_general(dsb, qs, TN, preferred_element_type=F32)
        for g in range(grp):
            dq_ref[:, g * HEAD_DIM : (g + 1) * HEAD_DIM] = dq[g * BLK : (g + 1) * BLK].astype(BF16)

        @pl.when(n == 0)
        def _():
            dk_acc[...] = jnp.zeros_like(dk_acc)
            dv_acc[...] = jnp.zeros_like(dv_acc)
            dbias_ref[...] = jnp.zeros_like(dbias_ref)

        @pl.when((n == 0) & (kv == 0))
        def _():
            dsink_ref[...] = jnp.zeros_like(dsink_ref)

        dk_acc[pl.ds(start, 3 * BLK), :] += dkb
        dv_acc[pl.ds(start, 3 * BLK), :] += dvb
        dbias_ref[...] += ds.reshape(grp, BLK, 3 * BLK)
        row = lax.broadcasted_iota(I32, (heads, LANES), 0)
        sd = psink * delta
        upd = jnp.zeros((heads, LANES), F32)
        for g in range(grp):
            upd = jnp.where(row == kv * grp + g, -jnp.sum(sd[g * BLK : (g + 1) * BLK]), upd)
        dsink_ref[...] += upd

        @pl.when(n == nb - 1)
        def _():
            dk_ref[...] = dk_acc[...]
            dv_ref[...] = dv_acc[...]

    pad_spec = pl.BlockSpec((s + 2 * BLK, HEAD_DIM), lambda kv, n: (0, kv))
    kvw = N_KV_HEADS * HEAD_DIM
    return _pallas(
        body,
        name="attn_bwd",
        grid=(N_KV_HEADS, nb),
        in_specs=_attn_specs(s, grp, q_blk0) + [pl.BlockSpec((BLK, qw), lambda kv, n: (n, kv)), ANY],
        out_specs=[
            pl.BlockSpec((BLK, qw), lambda kv, n: (n, q_blk0 + kv)),
            pad_spec,
            pad_spec,
            pl.BlockSpec((grp, BLK, 3 * BLK), lambda kv, n: (kv, 0, 0)),
            pl.BlockSpec((heads, LANES), lambda kv, n: (0, 0)),
        ],
        out_shape=[
            jax.ShapeDtypeStruct(dz.shape, BF16),
            jax.ShapeDtypeStruct((s + 2 * BLK, kvw), F32),
            jax.ShapeDtypeStruct((s + 2 * BLK, kvw), F32),
            jax.ShapeDtypeStruct((heads, BLK, 3 * BLK), F32),
            jax.ShapeDtypeStruct((heads, LANES), F32),
        ],
        scratch_shapes=[pltpu.VMEM((s + 2 * BLK, HEAD_DIM), F32), pltpu.VMEM((s + 2 * BLK, HEAD_DIM), F32)],
        input_output_aliases={6: 0},
    )(sink, z, k_pad, v_pad, bias_tab, dout, dz)


def _dkv_to_dz(dk_pad, dv_pad, dz, blk_idx):
    s = dz.shape[0]
    kvw = dk_pad.shape[1]

    def body(dk_ref, dv_ref, dz_in, out_ref):
        del dz_in
        out_ref[:, :kvw] = dk_ref[...].astype(BF16)
        out_ref[:, kvw:] = dv_ref[...].astype(BF16)

    src = pl.BlockSpec((BLK, kvw), lambda i: (i + 1, 0))
    return _pallas(
        body,
        name="dkv_to_dz",
        grid=(s // BLK,),
        in_specs=[src, src, ANY],
        out_specs=pl.BlockSpec((BLK, 2 * kvw), lambda i: (i, blk_idx)),
        out_shape=jax.ShapeDtypeStruct(dz.shape, BF16),
        input_output_aliases={2: 0},
    )(dk_pad, dv_pad, dz)


def _relbias_bwd(dbias_tab, bucket):
    heads = dbias_tab.shape[0]

    def body(dt_ref, bk_ref, out_ref):
        lane = lax.broadcasted_iota(I32, (1, LANES), 1)
        bk = bk_ref[...]
        rows = []
        for h in range(heads):
            dt = dt_ref[h]
            acc = jnp.zeros((1, LANES), F32)
            for b in range(REL_BUCKETS):
                acc = jnp.where(lane == b, jnp.sum(jnp.where(bk == b, dt, 0.0)), acc)
            rows.append(acc)
        out_ref[...] = jnp.concatenate(rows, axis=0)

    return _pallas(body, name="relbias_bwd", out_shape=jax.ShapeDtypeStruct((heads, LANES), F32))(dbias_tab, bucket)


def _t5_bucket(rel):
    nb = REL_BUCKETS // 2
    ret = jnp.where(rel > 0, nb, 0)
    n = jnp.abs(rel)
    max_exact = nb // 2
    nf = jnp.maximum(n, 1).astype(F32)
    large = max_exact + (jnp.log(nf / max_exact) / math.log(REL_MAX_DIST / max_exact) * (nb - max_exact)).astype(I32)
    large = jnp.minimum(large, nb - 1)
    return ret + jnp.where(n < max_exact, n, large)


def _band_tables(rel_bias):
    qi = jnp.arange(BLK)[:, None]
    kj = jnp.arange(3 * BLK)[None, :]
    rel = kj - BLK - qi
    bucket = _t5_bucket(rel).astype(I32)
    heads = rel_bias.shape[1]
    masked = jnp.where(jnp.abs(rel) <= BLK, bucket, -1)

    def body(rb_ref, bk_ref, out_ref):
        bk = bk_ref[...]
        for h in range(heads):
            tab = jnp.full(bk.shape, NEG, F32)
            for b in range(REL_BUCKETS):
                tab = jnp.where(bk == b, rb_ref[b, h], tab)
            out_ref[h] = tab

    bias_tab = _pallas(
        body,
        name="bias_table",
        in_specs=[pl.BlockSpec(memory_space=pltpu.SMEM), pl.BlockSpec(memory_space=pltpu.VMEM)],
        out_specs=pl.BlockSpec(memory_space=pltpu.VMEM),
        out_shape=jax.ShapeDtypeStruct((heads, BLK, 3 * BLK), F32),
    )(rel_bias.astype(F32), masked)
    return bias_tab, bucket


EW_BLOCK_ELEMS = 512 * 1024


def _ew_tiles(shape, elems=EW_BLOCK_ELEMS // 2):
    r, c = shape
    tn = c if c <= 2048 else _tile(c, (2048, 1920, 1536, 1408, 1024, 512))
    tm = _tile(r, [t for t in (1024, 512, 256, 128, 64, 32, 16, 8) if t * tn <= elems] or [8])
    return tm, tn


def _cast_into_full(name, qidx, w, kind, after=()):
    r, c = w.shape
    tm, tn = _ew_tiles(w.shape, EW_BLOCK_ELEMS)
    nbi, nbj = r // tm, c // tn
    if kind == "col":
        full, out_spec = (r, c * N_CHIPS), pl.BlockSpec((tm, tn), lambda i, j, q: (i, q[0] * nbj + j))
    else:
        full, out_spec = (r * N_CHIPS, c), pl.BlockSpec((tm, tn), lambda i, j, q: (q[0] * nbi + i, j))

    def body(q_ref, w_ref, *rest):
        del q_ref
        rest[-1][...] = w_ref[...].astype(BF16)

    return _pallas(
        body,
        name=name,
        grid_spec=pltpu.PrefetchScalarGridSpec(
            num_scalar_prefetch=1,
            grid=(nbi, nbj),
            in_specs=[pl.BlockSpec((tm, tn), lambda i, j, q: (i, j))] + [ANY] * len(after),
            out_specs=out_spec,
        ),
        out_shape=jax.ShapeDtypeStruct(full, BF16),
    )(qidx, w, *after)


def _adamw(name, w, g, m, v, after=()):
    tm, tn = _ew_tiles(w.shape, EW_BLOCK_ELEMS)
    if _nbytes(w.shape, F32) <= 1024 * 1024:
        tm, tn = w.shape
    spec = pl.BlockSpec((tm, tn), lambda i, j: (i, j))
    n_after = len(after)

    def body(w_ref, g_ref, m_ref, v_ref, *rest):
        d_ref, nm_ref, nv_ref, g_out_ref = rest[n_after:]
        gv = g_ref[...]
        g_out_ref[...] = gv
        nm = ADAM_B1 * m_ref[...] + (1.0 - ADAM_B1) * gv
        nv = ADAM_B2 * v_ref[...] + (1.0 - ADAM_B2) * (gv * gv)
        m_hat = nm / (1.0 - ADAM_B1**ADAM_STEP)
        v_hat = nv / (1.0 - ADAM_B2**ADAM_STEP)
        d_ref[...] = -ADAM_LR * (m_hat / (jnp.sqrt(v_hat) + ADAM_EPS) + ADAM_WD * w_ref[...])
        nm_ref[...] = nm
        nv_ref[...] = nv

    out = jax.ShapeDtypeStruct(w.shape, F32)
    return _pallas(
        body, name=name, grid=(w.shape[0] // tm, w.shape[1] // tn), in_specs=[spec] * 4 + [ANY] * n_after,
        out_specs=[spec] * 4, out_shape=[out, out, out, out],
        compiler_params=_params(_mm_vmem([((tm, tn), F32, 24)])),
    )(w, g, m, v, *after)


def _pair_add(name, cidx, g_full, r_sib, kind):
    hr, hc = r_sib.shape
    tm, tn = _ew_tiles((hr, hc), 2 * EW_BLOCK_ELEMS)
    nbi, nbj = hr // tm, hc // tn
    if kind == "col":
        g_spec = pl.BlockSpec((tm, tn), lambda i, j, c: (c[0] * nbi + i, j))
    else:
        g_spec = pl.BlockSpec((tm, tn), lambda i, j, c: (i, c[0] * nbj + j))
    spec = pl.BlockSpec((tm, tn), lambda i, j, c: (i, j))

    def body(c_ref, g_ref, r_ref, o_ref):
        del c_ref
        o_ref[...] = (g_ref[...].astype(F32) + r_ref[...].astype(F32)).astype(BF16)

    return _pallas(
        body,
        name=name,
        grid_spec=pltpu.PrefetchScalarGridSpec(num_scalar_prefetch=1, grid=(nbi, nbj), in_specs=[g_spec, spec], out_specs=spec),
        out_shape=jax.ShapeDtypeStruct((hr, hc), BF16),
        compiler_params=_params(_mm_vmem([((tm, tn), BF16, 6), ((tm, tn), F32, 3)])),
    )(cidx, g_full, r_sib)


def _chip_sum(name, qidx, c_half, r_ici, kind):
    _, pr, pc = r_ici.shape
    tm, tn = _ew_tiles((pr, pc), 2 * EW_BLOCK_ELEMS)
    nbi, nbj = pr // tm, pc // tn
    if kind == "col":
        own_spec = pl.BlockSpec((tm, tn), lambda i, j, q: (i, q[0] * nbj + j))
        full, out_spec = (2 * pr, pc), pl.BlockSpec((tm, tn), lambda i, j, q: (q[1] * nbi + i, j))
    else:
        own_spec = pl.BlockSpec((tm, tn), lambda i, j, q: (q[0] * nbi + i, j))
        full, out_spec = (pr, 2 * pc), pl.BlockSpec((tm, tn), lambda i, j, q: (i, q[1] * nbj + j))

    def body(q_ref, own_ref, r_ref, o_ref):
        q = q_ref[0]
        own = own_ref[...].astype(F32)
        recv = [r_ref[r].astype(F32) for r in range(3)]
        total = None
        for chip in range(N_CHIPS):
            d = chip ^ q
            term = jnp.where(d == 0, own, jnp.where(d == 2, recv[0], jnp.where(d == 1, recv[1], recv[2])))
            total = term if total is None else total + term
        o_ref[...] = total

    return _pallas(
        body,
        name=name,
        grid_spec=pltpu.PrefetchScalarGridSpec(
            num_scalar_prefetch=1,
            grid=(nbi, nbj),
            in_specs=[own_spec, pl.BlockSpec((3, tm, tn), lambda i, j, q: (0, i, j))],
            out_specs=out_spec,
        ),
        out_shape=jax.ShapeDtypeStruct(full, F32),
        compiler_params=_params(_mm_vmem([((tm, tn), BF16, 8), ((tm, tn), F32, 6)])),
    )(qidx, c_half, r_ici)


_REL_MASK = (2, 1, 3)


def _place():
    x, y, c = lax.axis_index("x"), lax.axis_index("y"), lax.axis_index("c")
    chips = [(1 - x, y), (x, 1 - y), (1 - x, 1 - y)]
    return x, y, c, 2 * x + y, chips


def _shard_view(ref, kind, chip):
    if kind == "col":
        w = ref.shape[1] // N_CHIPS
        return ref.at[:, pl.ds(pl.multiple_of(chip * w, LANES), w)]
    h = ref.shape[0] // N_CHIPS
    return ref.at[pl.ds(pl.multiple_of(chip * h, 16), h), :]


def _row_half(ref, half):
    h = ref.shape[0] // 2
    return ref.at[pl.ds(pl.multiple_of(half * h, 16), h), :]


def _pair_half(ref, kind, half):
    if kind == "col":
        return _row_half(ref, half)
    w = ref.shape[1] // 2
    return ref.at[:, pl.ds(pl.multiple_of(half * w, LANES), w)]


def _remote(src, dst, send_sem, recv_sem, dev):
    return pltpu.make_async_remote_copy(src_ref=src, dst_ref=dst, send_sem=send_sem, recv_sem=recv_sem, device_id=dev, device_id_type=MESH)


def _hbm(a):
    return pltpu.with_memory_space_constraint(a, pltpu.HBM)


def _gather_start(name, fulls, kinds, rels=(0, 1, 2), after=()):
    n_w = len(fulls)

    def body(*refs):
        g = refs[:n_w]
        send_sem, recv_sem = refs[n_w + len(after)], refs[n_w + len(after) + 1]
        token = refs[-1]
        _, _, c, q, chips = _place()
        for w in range(n_w):
            mine = _row_half(_shard_view(g[w], kinds[w], q), c)
            for r in rels if isinstance(rels, tuple) else rels[w]:
                _remote(mine, mine, send_sem.at[3 * w + r], recv_sem.at[3 * w + r], (*chips[r], c)).start()
        token[...] = jnp.zeros_like(token)

    res = _pallas(
        body,
        name=name,
        out_shape=(
            pltpu.SemaphoreType.DMA((3 * n_w,)),
            pltpu.SemaphoreType.DMA((3 * n_w,)),
            *[pltpu.HBM(f.shape, f.dtype) for f in fulls],
            jax.ShapeDtypeStruct((8, LANES), F32),
        ),
        in_specs=[HBM_SPEC] * n_w + [ANY] * len(after),
        out_specs=(SEM_SPEC, SEM_SPEC, *[HBM_SPEC] * n_w, pl.BlockSpec(memory_space=pltpu.VMEM)),
        input_output_aliases={w: w + 2 for w in range(n_w)},
        compiler_params=pltpu.CompilerParams(has_side_effects=EFFECT),
    )(*[_hbm(f) for f in fulls], *after)
    return res[0], res[1], list(res[2 : 2 + n_w]), res[-1]


def _relay_copies(kinds, waiting):
    def copies(refs, send_sem, recv_sem):
        _, _, c, q, chips = _place()
        out = []
        for i, kind in enumerate(kinds):
            for k, (src_rel, dst_rel) in enumerate(((0, 1), (1, 0))):
                held = _row_half(_row_half(_shard_view(refs[i], kind, q ^ _REL_MASK[src_rel]), c), k)
                far = _row_half(_row_half(_shard_view(refs[i], kind, q ^ _REL_MASK[2]), c), k)
                dst = far if waiting else held
                out.append(_remote(held, dst, send_sem.at[2 * i + k], recv_sem.at[2 * i + k], (*chips[dst_rel], c)))
        return out

    return copies


def _forward_copies(kinds, waiting, rels=(0, 1, 2), sem0=0):
    def copies(refs, send_sem, recv_sem):
        x, y, c, q, _ = _place()
        out = []
        for i, kind in enumerate(kinds):
            for k, r in enumerate(rels):
                quarter = _shard_view(refs[i], kind, q ^ _REL_MASK[r])
                landed = _row_half(quarter, c)
                dst = _row_half(quarter, 1 - c) if waiting else landed
                sem = sem0 + len(rels) * i + k
                out.append(_remote(landed, dst, send_sem.at[sem], recv_sem.at[sem], (x, y, 1 - c)))
        return out

    return copies


def _relay_and_forward_copies(kinds, waiting):
    forward = _forward_copies(kinds, waiting, rels=(0, 1), sem0=2 * len(kinds))
    relay = _relay_copies(kinds, waiting)
    return lambda refs, send_sem, recv_sem: forward(refs, send_sem, recv_sem) + relay(refs, send_sem, recv_sem)


def _gather_wait(name, fulls, kinds, w_ids, send_sem, recv_sem, after, rels=(0, 1, 2)):
    n = len(fulls)

    def body(*refs):
        g = refs[:n]
        s_sem, r_sem = refs[n], refs[n + 1]
        x, y, c, q, _ = _place()
        for i, w in enumerate(w_ids):
            mine = _row_half(_shard_view(g[i], kinds[i], q), c)
            for r in rels:
                landed = _row_half(_shard_view(g[i], kinds[i], q ^ _REL_MASK[r]), c)
                cp = _remote(mine, landed, s_sem.at[3 * w + r], r_sem.at[3 * w + r], (x, y, 1 - c))
                cp.wait_send()
                cp.wait_recv()

    res = _pallas(
        body,
        name=name,
        out_shape=[pltpu.HBM(f.shape, f.dtype) for f in fulls],
        in_specs=[HBM_SPEC] * n + [SEM_SPEC, SEM_SPEC, ANY],
        out_specs=[HBM_SPEC] * n,
        input_output_aliases={i: i for i in range(n)},
        compiler_params=pltpu.CompilerParams(has_side_effects=EFFECT),
    )(*fulls, send_sem, recv_sem, after)
    return list(res)


def _gather_forward(name, fulls, kinds, rels=(0, 1, 2)):
    n = len(fulls)

    def body(*refs):
        g = refs[n : 2 * n]
        send, recv = refs[2 * n :]
        _sibling_handshake()
        x, y, c, q, _ = _place()
        sib = (x, y, 1 - c)
        cps = []
        for i in range(n):
            for r in rels:
                landed = _row_half(_shard_view(g[i], kinds[i], q ^ _REL_MASK[r]), c)
                cps.append(_remote(landed, landed, send.at[i, r], recv.at[i, r], sib))
        for cp in cps:
            cp.start()
        for i in range(n):
            for r in rels:
                other = _row_half(_shard_view(g[i], kinds[i], q ^ _REL_MASK[r]), 1 - c)
                _remote(other, other, send.at[i, r], recv.at[i, r], sib).wait_recv()
        for cp in cps:
            cp.wait_send()

    res = _pallas(
        body,
        name=name,
        in_specs=[ANY] * n,
        out_specs=[ANY] * n,
        out_shape=[jax.ShapeDtypeStruct(f.shape, f.dtype) for f in fulls],
        scratch_shapes=[pltpu.SemaphoreType.DMA((n, 3)), pltpu.SemaphoreType.DMA((n, 3))],
        input_output_aliases={i: i for i in range(n)},
        compiler_params=pltpu.CompilerParams(collective_id=SIBLING_BARRIER_ID),
    )(*fulls)
    return list(res)


SIBLING_BARRIER_ID = 1


def _sibling_handshake():
    sib = (lax.axis_index("x"), lax.axis_index("y"), 1 - lax.axis_index("c"))
    barrier = pltpu.get_barrier_semaphore()
    pl.semaphore_signal(barrier, inc=1, device_id=sib, device_id_type=MESH)
    pl.semaphore_wait(barrier, 1)


def _split_start(name, bufs, n_sems, copies, sibling_only=False):
    n = len(bufs)

    def body(*refs):
        if sibling_only:
            _sibling_handshake()
        for cp in copies(refs[:n], refs[n], refs[n + 1]):
            cp.start()

    extra = {"collective_id": SIBLING_BARRIER_ID} if sibling_only else {}

    res = _pallas(
        body,
        name=name,
        out_shape=(
            pltpu.SemaphoreType.DMA((n_sems,)),
            pltpu.SemaphoreType.DMA((n_sems,)),
            *[pltpu.HBM(b.shape, b.dtype) for b in bufs],
        ),
        in_specs=[HBM_SPEC] * n,
        out_specs=(SEM_SPEC, SEM_SPEC, *[HBM_SPEC] * n),
        input_output_aliases={i: i + 2 for i in range(n)},
        compiler_params=pltpu.CompilerParams(has_side_effects=EFFECT, **extra),
    )(*[_hbm(b) for b in bufs])
    return res[0], res[1], list(res[2:])


def _split_wait(name, bufs, send_sem, recv_sem, copies, after):
    n = len(bufs)

    def body(*refs):
        for cp in copies(refs[:n], refs[n], refs[n + 1]):
            cp.wait_send()
            cp.wait_recv()

    res = _pallas(
        body,
        name=name,
        out_shape=[pltpu.HBM(b.shape, b.dtype) for b in bufs],
        in_specs=[HBM_SPEC] * n + [SEM_SPEC, SEM_SPEC, ANY],
        out_specs=[HBM_SPEC] * n,
        input_output_aliases={i: i for i in range(n)},
        compiler_params=pltpu.CompilerParams(has_side_effects=EFFECT),
    )(*bufs, send_sem, recv_sem, after)
    return list(res)


def _pair_exchange_copies(kinds):
    n = len(kinds)

    def copies(refs, send_sem, recv_sem):
        x, y, c, _, _ = _place()
        return [
            _remote(_pair_half(refs[w], kinds[w], 1 - c), refs[n + w], send_sem.at[w], recv_sem.at[w], (x, y, 1 - c))
            for w in range(n)
        ]

    return copies


def _pair_share_copies(kinds, waiting):
    def copies(refs, send_sem, recv_sem):
        x, y, c, _, _ = _place()
        out = []
        for w, kind in enumerate(kinds):
            mine = _pair_half(refs[w], kind, c)
            dst = _pair_half(refs[w], kind, 1 - c) if waiting else mine
            out.append(_remote(mine, dst, send_sem.at[w], recv_sem.at[w], (x, y, 1 - c)))
        return out

    return copies


def _piece_shape(half_shape, kind):
    r, c = half_shape
    return (3, r, c // N_CHIPS) if kind == "col" else (3, r // N_CHIPS, c)


def _chip_send_start(name, halves, kinds):
    n = len(halves)
    lands = [lax.empty(_piece_shape(h.shape, k), BF16) for h, k in zip(halves, kinds)]

    def body(*refs):
        h, land = refs[:n], refs[n : 2 * n]
        send_sem, recv_sem = refs[2 * n], refs[2 * n + 1]
        _, _, c, q, chips = _place()
        for i in range(n):
            for r, chip in enumerate(chips):
                piece = _shard_view(h[i], kinds[i], q ^ _REL_MASK[r])
                _remote(piece, land[i].at[r], send_sem.at[3 * i + r], recv_sem.at[3 * i + r], (*chip, c)).start()

    res = _pallas(
        body,
        name=name,
        out_shape=(
            pltpu.SemaphoreType.DMA((3 * n,)),
            pltpu.SemaphoreType.DMA((3 * n,)),
            *[pltpu.HBM(a.shape, a.dtype) for a in halves],
            *[pltpu.HBM(a.shape, a.dtype) for a in lands],
        ),
        in_specs=[HBM_SPEC] * (2 * n),
        out_specs=(SEM_SPEC, SEM_SPEC, *[HBM_SPEC] * (2 * n)),
        input_output_aliases={i: i + 2 for i in range(2 * n)},
        compiler_params=pltpu.CompilerParams(has_side_effects=EFFECT),
    )(*[_hbm(a) for a in halves], *[_hbm(a) for a in lands])
    return res[0], res[1], list(res[2 : 2 + n]), list(res[2 + n :])


def _chip_send_wait(name, halves, lands, kinds, send_sem, recv_sem, after):
    n = len(halves)

    def body(*refs):
        h, land = refs[:n], refs[n : 2 * n]
        s_sem, r_sem = refs[2 * n], refs[2 * n + 1]
        x, y, c, q, _ = _place()
        for i in range(n):
            for r in range(3):
                piece = _shard_view(h[i], kinds[i], q ^ _REL_MASK[r])
                cp = _remote(piece, land[i].at[r], s_sem.at[3 * i + r], r_sem.at[3 * i + r], (x, y, 1 - c))
                cp.wait_send()
                cp.wait_recv()

    res = _pallas(
        body,
        name=name,
        out_shape=[pltpu.HBM(a.shape, a.dtype) for a in halves] + [pltpu.HBM(a.shape, a.dtype) for a in lands],
        in_specs=[HBM_SPEC] * (2 * n) + [SEM_SPEC, SEM_SPEC, ANY],
        out_specs=[HBM_SPEC] * (2 * n),
        input_output_aliases={i: i for i in range(2 * n)},
        compiler_params=pltpu.CompilerParams(has_side_effects=EFFECT),
    )(*halves, *lands, send_sem, recv_sem, after)
    return list(res[:n]), list(res[n:])


def _small_exchange_copies(waiting):
    def copies(refs, send_sem, recv_sem):
        p, land = refs
        x, y, c, q, _ = _place()
        me = 2 * q + c
        out = []
        for dd in range(1, 2 * N_CHIPS):
            dev = (x ^ ((dd >> 2) & 1), y ^ ((dd >> 1) & 1), c ^ (dd & 1))
            dst = land.at[me ^ dd] if waiting else land.at[me]
            out.append(_remote(p, dst, send_sem.at[dd - 1], recv_sem.at[dd - 1], dev))
        return out

    return copies


def _small_sum(name, me_idx, p, land):
    rows = p.shape[0]
    n_dev = 2 * N_CHIPS

    def body(me_ref, p_ref, land_ref, o_ref):
        me = me_ref[0]
        total = None
        for dev in range(n_dev):
            term = jnp.where(me == dev, p_ref[...], land_ref[dev])
            total = term if total is None else total + term
        o_ref[...] = total

    return _pallas(
        body,
        name=name,
        grid_spec=pltpu.PrefetchScalarGridSpec(
            num_scalar_prefetch=1,
            grid=(1,),
            in_specs=[pl.BlockSpec((rows, LANES), lambda i, m: (0, 0)), pl.BlockSpec((n_dev, rows, LANES), lambda i, m: (0, 0, 0))],
            out_specs=pl.BlockSpec((rows, LANES), lambda i, m: (0, 0)),
        ),
        out_shape=jax.ShapeDtypeStruct(p.shape, F32),
    )(me_idx, p, land)


def _pack(parts):
    rows = []
    for a in parts:
        flat = a.reshape(-1).astype(F32)
        n = flat.shape[0]
        padded = -(-n // (8 * LANES)) * (8 * LANES)
        rows.append(jnp.pad(flat, (0, padded - n)).reshape(-1, LANES))
    return jnp.concatenate(rows, axis=0)


def _unpack(packed, shapes):
    out, row = [], 0
    for shp in shapes:
        n = int(np.prod(shp))
        nrows = -(-n // (8 * LANES)) * 8
        out.append(packed[row : row + nrows].reshape(-1)[:n].reshape(shp))
        row += nrows
    return out


def kernel(x, w_in, norm_mix, sgu_v_gain, sgu_w_s, sgu_b_s, w_a_out, attn_sink, rel_bias, w_b_out, w_o, norm_ffn, w_gate, w_up, w_down, norm_final, loss_target, m_w_in, m_norm_mix, m_sgu_v_gain, m_sgu_w_s, m_sgu_b_s, m_w_a_out, m_attn_sink, m_rel_bias, m_w_b_out, m_w_o, m_norm_ffn, m_w_gate, m_w_up, m_w_down, m_norm_final, v_w_in, v_norm_mix, v_sgu_v_gain, v_sgu_w_s, v_sgu_b_s, v_w_a_out, v_attn_sink, v_rel_bias, v_w_b_out, v_w_o, v_norm_ffn, v_w_gate, v_w_up, v_w_down, v_norm_final):
    s, d = x.shape[1], x.shape[2]
    w_sgu = sgu_v_gain.shape[1]
    groups = sgu_w_s.shape[1]
    heads = attn_sink.shape[1]
    grp = heads // N_KV_HEADS
    w_att = heads * HEAD_DIM
    w_kv = N_KV_HEADS * HEAD_DIM
    d_ff = w_gate.shape[2] * N_CHIPS
    n_in = w_in.shape[2] * N_CHIPS
    off_q = 2 * w_sgu
    off_k = off_q + w_att
    off_g = off_k + 2 * w_kv
    assert n_in == off_g + 2 * d and groups * BLK == w_sgu and s % BLK == 0

    x2d = x.reshape(s, d)
    tgt = loss_target.reshape(s, d)
    c_idx = lax.axis_index("c").astype(I32).reshape(1)
    q_idx = (2 * lax.axis_index("x") + lax.axis_index("y")).astype(I32).reshape(1)
    qc_idx = jnp.concatenate([q_idx, c_idx])

    W_IN, W_A, W_B, W_O, W_GATE, W_UP, W_DOWN = range(7)
    names = ["w_in", "w_a", "w_b", "w_o", "w_gate", "w_up", "w_down"]
    kinds = ["col", "col", "col", "row", "col", "col", "row"]
    big_w = [w_in[0], w_a_out[0], w_b_out[0], w_o[0], w_gate[0], w_up[0], w_down[0]]
    big_m = [m_w_in[0], m_w_a_out[0], m_w_b_out[0], m_w_o[0], m_w_gate[0], m_w_up[0], m_w_down[0]]
    big_v = [v_w_in[0], v_w_a_out[0], v_w_b_out[0], v_w_o[0], v_w_gate[0], v_w_up[0], v_w_down[0]]
    full_in = _cast_into_full("cast_w_in", q_idx, big_w[W_IN], kinds[W_IN])
    in_send, in_recv, (full_in,), token = _gather_start("gather_start_in", [full_in], [kinds[W_IN]], rels=(0, 1))
    rest = [_cast_into_full("cast_" + names[i], q_idx, big_w[i], kinds[i], after=(token,)) for i in range(1, 7)]

    ws_b = sgu_w_s[0].astype(BF16)
    wst_b = jnp.swapaxes(sgu_w_s[0], 1, 2).astype(BF16)
    b_col = sgu_b_s[0].reshape(groups, BLK, 1)
    bias_tab, bucket = _band_tables(rel_bias)
    sink = attn_sink[0]
    small_w = [norm_mix, sgu_v_gain, sgu_w_s, sgu_b_s, attn_sink, rel_bias, norm_ffn, norm_final]
    small_m = [m_norm_mix, m_sgu_v_gain, m_sgu_w_s, m_sgu_b_s, m_attn_sink, m_rel_bias, m_norm_ffn, m_norm_final]
    small_v = [v_norm_mix, v_sgu_v_gain, v_sgu_w_s, v_sgu_b_s, v_attn_sink, v_rel_bias, v_norm_ffn, v_norm_final]
    small_shapes = [w.shape for w in small_w]
    zero1 = jnp.zeros((1,), F32)
    pw, pm, pv = _pack(small_w + [zero1]), _pack(small_m + [zero1]), _pack(small_v + [zero1])
    h1 = _rms_fwd("rms_mix", x2d, norm_mix, after=(token, rest[-1], ws_b, wst_b, b_col, bias_tab, pw, pm, pv))
    (full_in,) = _gather_wait("gather_wait_in", [full_in], [kinds[W_IN]], [0], in_send, in_recv, h1, rels=(0, 1))
    relay_send, relay_recv, (full_in,) = _split_start(
        "gather_relay_in", [full_in], 4, _relay_and_forward_copies([kinds[W_IN]], False)
    )
    full_down = rest.pop()
    full_up = rest.pop()
    ag_send, ag_recv, rest, token = _gather_start("gather_start_rest", rest, kinds[1:5], rels=(0, 1), after=(full_in, full_up))
    (full_in,) = _split_wait(
        "gather_relay_wait_in", [full_in], relay_send, relay_recv, _relay_and_forward_copies([kinds[W_IN]], True), token
    )
    (g_in,) = _gather_forward("gather_fwd_in", [full_in], [kinds[W_IN]], rels=(2,))
    fulls = [g_in] + rest

    def relay_begin(tag, ids, after, source=None):
        ks = [kinds[i] for i in ids]
        send, recv, bufs, w_ids = source or (ag_send, ag_recv, [fulls[i] for i in ids], [i - 1 for i in ids])
        bufs = _gather_wait("gather_wait_" + tag, bufs, ks, w_ids, send, recv, after, rels=(0, 1))
        send, recv, bufs = _split_start("gather_relay_" + tag, bufs, 4 * len(ids), _relay_and_forward_copies(ks, False))
        return tag, ks, send, recv, bufs

    def relay_end(state, after):
        tag, ks, send, recv, bufs = state
        bufs = _split_wait("gather_relay_wait_" + tag, bufs, send, recv, _relay_and_forward_copies(ks, True), after)
        return _gather_forward("gather_fwd_" + tag, bufs, ks, rels=(2,))

    def relay_end_async(state, after):
        tag, ks, send, recv, bufs = state
        bufs = _split_wait("gather_relay_wait_" + tag, bufs, send, recv, _relay_and_forward_copies(ks, True), after)
        send, recv, bufs = _split_start(
            "gather_fwd_start_" + tag, bufs, len(ks), _forward_copies(ks, False, rels=(2,)), sibling_only=True
        )
        return tag, ks, send, recv, bufs

    def forwarded(state, after):
        tag, ks, send, recv, bufs = state
        return _split_wait("gather_fwd_wait_" + tag, bufs, send, recv, _forward_copies(ks, True, rels=(2,)), after)

    tm = _tile(s, (1024, 512, 256, 128))

    tn = _tile(n_in, (768, 640, 512))
    z = _mm(
        "mm_z", (s // tm, n_in // tn, 1), [h1, g_in],
        [pl.BlockSpec((tm, d), lambda i, j, k: (i, 0)), pl.BlockSpec((d, tn), lambda i, j, k: (0, j))],
        [jax.ShapeDtypeStruct((s, n_in), BF16)], [pl.BlockSpec((tm, tn), lambda i, j, k: (i, j))],
        [(0, 1, NN, 0)], 1, (tm, tn), 1, lambda ins, vals, outs, cs: _put(outs[0], cs, vals[0]),
        _mm_vmem([((tm, d), BF16, 2), ((d, tn), BF16, 2), ((tm, tn), F32, 3)]),
    )[0]
    mix_relay = relay_begin("mix", [W_A, W_B, W_O], z)
    up_send, up_recv, (full_up,), token = _gather_start(
        "gather_start_up", [full_up], [kinds[W_UP]], rels=(0, 1), after=(mix_relay[4][0],)
    )

    a_act = _sgu_fwd(z, sgu_v_gain, ws_b, b_col, w_sgu, after=(token,))

    kv_b = z[:, off_k:off_g]
    k_pad = jnp.pad(kv_b[:, :w_kv], ((BLK, BLK), (0, 0)))
    v_pad = jnp.pad(kv_b[:, w_kv:], ((BLK, BLK), (0, 0)))
    q_blk0 = off_q // (grp * HEAD_DIM)
    att = _attn_fwd(sink, z, k_pad, v_pad, bias_tab, grp, q_blk0)

    gate_relay = relay_begin("gate", [W_GATE], att)
    g_a, g_b, g_o = relay_end(mix_relay, gate_relay[4][0])

    tg = _tile(d, (512,))
    ga0, gb0 = off_g // tg, (off_g + d) // tg

    def ep_gate(ins, vals, outs, cs):
        sa, sb = _sigmoid(ins[4][:, cs].astype(F32)), _sigmoid(ins[5][:, cs].astype(F32))
        _put(outs[0], cs, sa * vals[0] + sb * vals[1])
        _put(outs[1], cs, vals[0])
        _put(outs[2], cs, vals[1])

    t_out = pl.BlockSpec((tm, tg), lambda i, j, k: (i, j))
    m_act, y_a, y_b = _mm(
        "mm_branches", (s // tm, d // tg, 1), [a_act, g_a, att, g_b, z, z],
        [pl.BlockSpec((tm, w_sgu), lambda i, j, k: (i, 0)), pl.BlockSpec((w_sgu, tg), lambda i, j, k: (0, j)),
         pl.BlockSpec((tm, w_att), lambda i, j, k: (i, 0)), pl.BlockSpec((w_att, tg), lambda i, j, k: (0, j)),
         pl.BlockSpec((tm, tg), lambda i, j, k: (i, ga0 + j)), pl.BlockSpec((tm, tg), lambda i, j, k: (i, gb0 + j))],
        [jax.ShapeDtypeStruct((s, d), BF16)] * 3,
        [t_out, t_out, t_out], [(0, 1, NN, 0), (2, 3, NN, 1)], 2, (tm, tg), 1, ep_gate,
        _mm_vmem([((tm, w_sgu), BF16, 4), ((w_sgu, tg), BF16, 4), ((tm, tg), F32, 12)]),    )

    tn = _tile(d, (1024, 512))

    def ep_residual(ins, vals, outs, cs):
        _put(outs[0], cs, ins[2][:, cs] + vals[0])

    up_relay = relay_begin("up", [W_UP], m_act, source=(up_send, up_recv, [full_up], [0]))
    down_send, down_recv, (full_down,), token = _gather_start(
        "gather_start_down", [full_down], [kinds[W_DOWN]], after=(up_relay[4][0],)
    )
    x2 = _mm(
        "mm_wo", (s // tm, d // tn, 1), [m_act, g_o, x2d],
        [pl.BlockSpec((tm, d), lambda i, j, k: (i, 0)), pl.BlockSpec((d, tn), lambda i, j, k: (0, j)),
         pl.BlockSpec((tm, tn), lambda i, j, k: (i, j))],
        [jax.ShapeDtypeStruct((s, d), F32)], [pl.BlockSpec((tm, tn), lambda i, j, k: (i, j))],
        [(0, 1, NN, 0)], 1, (tm, tn), 1, ep_residual,
        _mm_vmem([((tm, d), BF16, 2), ((d, tn), BF16, 2), ((tm, tn), F32, 5)]),
        after=(token,),
    )[0]
    gate_fwd = relay_end_async(gate_relay, x2)
    up_fwd = relay_end_async(up_relay, gate_fwd[4][0])
    h2 = _rms_fwd("rms_ffn", x2, norm_ffn, after=(up_fwd[4][0],))
    (g_gate,) = forwarded(gate_fwd, h2)
    (g_up,) = forwarded(up_fwd, g_gate)

    tf = _tile(d_ff, (512,))

    def ep_swiglu(ins, vals, outs, cs):
        gt, up = vals
        _put(outs[0], cs, gt)
        _put(outs[1], cs, up)
        _put(outs[2], cs, (gt * _sigmoid(gt)) * up)

    f_out = pl.BlockSpec((tm, tf), lambda i, j, k: (i, j))
    gt, up, f_act = _mm(
        "mm_gate_up", (s // tm, d_ff // tf, 1), [h2, g_gate, g_up],
        [pl.BlockSpec((tm, d), lambda i, j, k: (i, 0)), pl.BlockSpec((d, tf), lambda i, j, k: (0, j)),
         pl.BlockSpec((d, tf), lambda i, j, k: (0, j))],
        [jax.ShapeDtypeStruct((s, d_ff), BF16)] * 3,
        [f_out, f_out, f_out], [(0, 1, NN, 0), (0, 2, NN, 1)], 2, (tm, tf), 1, ep_swiglu,
        _mm_vmem([((tm, d), BF16, 2), ((d, tf), BF16, 4), ((tm, tf), F32, 8)]),    )
    (g_down,) = _gather_forward(
        "gather_fwd_ffn_out",
        _gather_wait("gather_wait_ffn_out", [full_down], [kinds[W_DOWN]], [0], down_send, down_recv, f_act),
        [kinds[W_DOWN]],
    )

    tkf = _tile(d_ff, (1408, 1024, 512))
    tml, tnl = _tile(s, (512, 256, 128)), _tile(d, (512,))
    x3 = _mm(
        "mm_down", (s // tml, d // tnl, 1), [f_act, g_down, x2],
        [pl.BlockSpec((tml, d_ff), lambda i, j, k: (i, 0)), pl.BlockSpec((d_ff, tnl), lambda i, j, k: (0, j)),
         pl.BlockSpec((tml, tnl), lambda i, j, k: (i, j))],
        [jax.ShapeDtypeStruct((s, d), F32)], [pl.BlockSpec((tml, tnl), lambda i, j, k: (i, j))],
        [(0, 1, NN, 0)], 1, (tml, tnl), 1, ep_residual,
        _mm_vmem([((tml, d_ff), BF16, 2), ((d_ff, tnl), BF16, 2), ((tml, tnl), F32, 6)]),    )[0]

    dx3, dx3b, dg_final, loss_part = _head(x3, norm_final.reshape(1, d), tgt)

    def reduce_a(tag, ids, grads):
        ks = [kinds[i] for i in ids]
        lands = [lax.empty((g.shape[0] // 2, g.shape[1]) if k == "col" else (g.shape[0], g.shape[1] // 2), BF16)
                 for g, k in zip(grads, ks)]
        send, recv, bufs = _split_start("pair_send_" + tag, list(grads) + lands, len(ids), _pair_exchange_copies(ks), sibling_only=True)
        return {"tag": tag, "ids": ids, "ks": ks, "pair": (send, recv, bufs), "token": bufs[0]}

    def reduce_b(st, after):
        tag, ids, ks = st["tag"], st["ids"], st["ks"]
        send, recv, bufs = st["pair"]
        bufs = _split_wait("pair_wait_" + tag, bufs, send, recv, _pair_exchange_copies(ks), after)
        grads, from_sib = bufs[: len(ids)], bufs[len(ids) :]
        halves = [_pair_add("pair_add_" + names[i], c_idx, g, r, k) for i, g, r, k in zip(ids, grads, from_sib, ks)]
        st["chip"] = _chip_send_start("chip_send_" + tag, halves, ks)
        st["token"] = st["chip"][2][0]

    def reduce_c(st, after):
        tag, ids, ks = st["tag"], st["ids"], st["ks"]
        send, recv, halves, lands = st["chip"]
        halves, lands = _chip_send_wait("chip_wait_" + tag, halves, lands, ks, send, recv, after)
        pieces = [_chip_sum("chip_sum_" + names[i], qc_idx, h, r, k) for i, h, r, k in zip(ids, halves, lands, ks)]
        st["share"] = _split_start("share_send_" + tag, pieces, len(ids), _pair_share_copies(ks, False), sibling_only=True)
        st["token"] = st["share"][2][0]

    def reduce_d(st, after):
        send, recv, bufs = st["share"]
        return _split_wait("share_wait_" + st["tag"], bufs, send, recv, _pair_share_copies(st["ks"], True), after)

    grads_big, upd = [None] * 7, [None] * 7

    def update(i, g, after=()):
        upd[i] = _adamw("adamw_" + names[i], big_w[i], g, big_m[i], big_v[i], after=after)
        grads_big[i] = upd[i][3]
        return upd[i][0]

    def finish(st, after):
        shared = reduce_d(st, after)
        after = shared[0]
        for i, g in zip(st["ids"], shared):
            after = update(i, g, (after,))
        return after

    def ep_swiglu_bwd(ins, vals, outs, cs):
        df = vals[0]
        gtv, upv = ins[2][:, cs].astype(F32), ins[3][:, cs].astype(F32)
        sg = _sigmoid(gtv)
        _put(outs[0], cs, df * upv * (sg + gtv * sg * (1.0 - sg)))
        _put(outs[1], cs, df * (gtv * sg))

    dgt, dup = _mm(
        "mm_dswiglu", (s // tm, d_ff // tf, 1), [dx3b, g_down, gt, up],
        [pl.BlockSpec((tm, d), lambda i, j, k: (i, 0)), pl.BlockSpec((tf, d), lambda i, j, k: (j, 0)), f_out, f_out],
        [jax.ShapeDtypeStruct((s, d_ff), BF16), jax.ShapeDtypeStruct((s, d_ff), BF16)], [f_out, f_out],
        [(0, 1, NT, 0)], 1, (tm, tf), 1, ep_swiglu_bwd,
        _mm_vmem([((tm, d), BF16, 2), ((tf, d), BF16, 2), ((tm, tf), F32, 8)]),    )

    def ep_store(ins, vals, outs, cs):
        for o, v in zip(outs, vals):
            _put(o, cs, v)

    twn = _tile(d, (1024, 512))
    gw_down = _mm(
        "mm_gw_down", (d_ff // tkf, d // twn, 1), [f_act, dx3b],
        [pl.BlockSpec((s, tkf), lambda i, j, k: (0, i)), pl.BlockSpec((s, twn), lambda i, j, k: (0, j))],
        [jax.ShapeDtypeStruct((d_ff, d), BF16)], [pl.BlockSpec((tkf, twn), lambda i, j, k: (i, j))],
        [(0, 1, TN, 0)], 1, (tkf, twn), 1, ep_store,
        _mm_vmem([((s, tkf), BF16, 3), ((s, twn), BF16, 2), ((tkf, twn), F32, 3)]),
    )[0]
    red_down = reduce_a("down", [W_DOWN], [gw_down])

    tn2 = _tile(d, (256,))
    dh2_specs = [pl.BlockSpec((tm, d_ff), lambda i, j, k: (i, 0)), pl.BlockSpec((tn2, d_ff), lambda i, j, k: (j, 0))]
    dh2_tile = pl.BlockSpec((tm, tn2), lambda i, j, k: (i, j))
    dh2_vmem = _mm_vmem([((tm, d_ff), BF16, 2), ((tn2, d_ff), BF16, 2), ((tm, tn2), F32, 7)])
    dh2 = _mm(
        "mm_dh2_gate", (s // tm, d // tn2, 1), [dgt, g_gate], dh2_specs,
        [jax.ShapeDtypeStruct((s, d), F32)], [dh2_tile], [(0, 1, NT, 0)], 1, (tm, tn2), 1, ep_store, dh2_vmem,
        after=(red_down["token"],),
    )[0]
    dh2 = _mm(
        "mm_dh2_up", (s // tm, d // tn2, 1), [dup, g_up, dh2], dh2_specs + [dh2_tile],
        [jax.ShapeDtypeStruct((s, d), F32)], [dh2_tile], [(0, 1, NT, 0)], 1, (tm, tn2), 1, ep_residual, dh2_vmem,
    )[0]
    reduce_b(red_down, dh2)

    twr = _tile(d, (1024, 512))
    w_tile = pl.BlockSpec((twr, tf), lambda i, j, k: (i, j))
    gw_gate, gw_up = _mm(
        "mm_gw_gate_up", (d // twr, d_ff // tf, 1), [h2, dgt, dup],
        [pl.BlockSpec((s, twr), lambda i, j, k: (0, i)), pl.BlockSpec((s, tf), lambda i, j, k: (0, j)),
         pl.BlockSpec((s, tf), lambda i, j, k: (0, j))],
        [jax.ShapeDtypeStruct((d, d_ff), BF16), jax.ShapeDtypeStruct((d, d_ff), BF16)], [w_tile, w_tile],
        [(0, 1, TN, 0), (0, 2, TN, 1)], 2, (twr, tf), 1, ep_store,
        _mm_vmem([((s, twr), BF16, 3), ((s, tf), BF16, 4), ((twr, tf), F32, 6)]),
        after=(red_down["token"],),
    )
    red_ffn = reduce_a("ffn_in", [W_GATE, W_UP], [gw_gate, gw_up])

    dx2, dx2b, dg_ffn = _rms_bwd("rms_ffn_bwd", x2, norm_ffn, dh2, dx3, after=(red_ffn["token"],))

    nj = d // tg

    def lo(j):
        return jnp.minimum(j, nj - 1)

    def gate_bwd_body(dx_ref, wo_ref, ga_ref, gb_ref, ya_ref, yb_ref, dya_ref, dyb_ref, dz_ref, keep):
        j = pl.program_id(1)

        @pl.when(j < nj)
        def _():
            dm = lax.dot_general(dx_ref[...], wo_ref[...], NT, preferred_element_type=F32)
            sa, sb = _sigmoid(ga_ref[...].astype(F32)), _sigmoid(gb_ref[...].astype(F32))
            dya_ref[...] = (dm * sa).astype(BF16)
            dyb_ref[...] = (dm * sb).astype(BF16)
            dz_ref[...] = (dm * ya_ref[...].astype(F32) * (sa * (1.0 - sa))).astype(BF16)
            keep[lo(j)] = (dm * yb_ref[...].astype(F32) * (sb * (1.0 - sb))).astype(BF16)

        @pl.when(j >= nj)
        def _():
            dz_ref[...] = keep[jnp.maximum(j - nj, 0)]

    t_lo = pl.BlockSpec((tm, tg), lambda i, j: (i, lo(j)))
    dya, dyb, dz = _pallas(
        gate_bwd_body,
        name="mm_dgate",
        grid=(s // tm, 2 * nj),
        in_specs=[
            pl.BlockSpec((tm, d), lambda i, j: (i, 0)),
            pl.BlockSpec((tg, d), lambda i, j: (lo(j), 0)),
            pl.BlockSpec((tm, tg), lambda i, j: (i, ga0 + lo(j))),
            pl.BlockSpec((tm, tg), lambda i, j: (i, gb0 + lo(j))),
            t_lo,
            t_lo,
        ],
        out_specs=[t_lo, t_lo, pl.BlockSpec((tm, tg), lambda i, j: (i, ga0 + j))],
        out_shape=[jax.ShapeDtypeStruct((s, d), BF16), jax.ShapeDtypeStruct((s, d), BF16), jax.ShapeDtypeStruct((s, n_in), BF16)],
        scratch_shapes=[pltpu.VMEM((nj, tm, tg), BF16)],
        compiler_params=_params(_mm_vmem([((tm, d), BF16, 2), ((tg, d), BF16, 2), ((tm, tg), F32, 14), ((nj, tm, tg), BF16, 1)])),
    )(dx2b, g_o, z, z, y_a, y_b)
    reduce_b(red_ffn, dya)
    reduce_c(red_down, red_ffn["token"])

    gw_o = _mm(
        "mm_gw_o", (d // twr, d // twn, 1), [m_act, dx2b],
        [pl.BlockSpec((s, twr), lambda i, j, k: (0, i)), pl.BlockSpec((s, twn), lambda i, j, k: (0, j))],
        [jax.ShapeDtypeStruct((d, d), BF16)], [pl.BlockSpec((twr, twn), lambda i, j, k: (i, j))],
        [(0, 1, TN, 0)], 1, (twr, twn), 1, ep_store,
        _mm_vmem([((s, twr), BF16, 3), ((s, twn), BF16, 2), ((twr, twn), F32, 3)]),
        after=(red_down["token"],),
    )[0]
    after_down = finish(red_down, gw_o)

    tb = _tile(w_sgu, (1024, 512))
    b_out = pl.BlockSpec((tm, tb), lambda i, j, k: (i, j))

    da, datt = _mm(
        "mm_dbranches", (s // tm, w_sgu // tb, 1), [dya, g_a, dyb, g_b],
        [pl.BlockSpec((tm, d), lambda i, j, k: (i, 0)), pl.BlockSpec((tb, d), lambda i, j, k: (j, 0)),
         pl.BlockSpec((tm, d), lambda i, j, k: (i, 0)), pl.BlockSpec((tb, d), lambda i, j, k: (j, 0))],
        [jax.ShapeDtypeStruct((s, w_sgu), BF16), jax.ShapeDtypeStruct((s, w_att), BF16)], [b_out, b_out],
        [(0, 1, NT, 0), (2, 3, NT, 1)], 2, (tm, tb), 1, ep_store,
        _mm_vmem([((tm, d), BF16, 4), ((tb, d), BF16, 4), ((tm, tb), F32, 6)]),
        after=(after_down,),
    )

    wb_tile = pl.BlockSpec((tb, twn), lambda i, j, k: (i, j))
    gw_a, gw_b = _mm(
        "mm_gw_branches", (w_sgu // tb, d // twn, 1), [a_act, dya, att, dyb],
        [pl.BlockSpec((s, tb), lambda i, j, k: (0, i)), pl.BlockSpec((s, twn), lambda i, j, k: (0, j)),
         pl.BlockSpec((s, tb), lambda i, j, k: (0, i)), pl.BlockSpec((s, twn), lambda i, j, k: (0, j))],
        [jax.ShapeDtypeStruct((w_sgu, d), BF16), jax.ShapeDtypeStruct((w_att, d), BF16)], [wb_tile, wb_tile],
        [(0, 1, TN, 0), (2, 3, TN, 1)], 2, (tb, twn), 1, ep_store,
        _mm_vmem([((s, tb), BF16, 5), ((s, twn), BF16, 4), ((tb, twn), F32, 6)]),
        after=(da,),
    )
    red_mix = reduce_a("mix", [W_O, W_A, W_B], [gw_o, gw_a, gw_b])

    dz, dws, dbs, dgain = _sgu_bwd(z, da, sgu_v_gain, ws_b, wst_b, b_col, w_sgu, dz, after=(red_mix["token"],))
    dz, dk_pad, dv_pad, dbias_tab, dsink = _attn_bwd(sink, z, k_pad, v_pad, bias_tab, datt, dz, grp, q_blk0)
    dz = _dkv_to_dz(dk_pad, dv_pad, dz, off_k // (2 * w_kv))
    drel = _relbias_bwd(dbias_tab, bucket)
    reduce_b(red_mix, dz)
    reduce_c(red_ffn, red_mix["token"])

    early = [dgain, dws, dbs, dsink[:, 0], drel[:, :REL_BUCKETS].T, dg_ffn, dg_final]
    p_early = _pack([g.reshape(shp) for g, shp in zip(early, small_shapes[1:])] + [loss_part[0, :1]])
    land = jnp.zeros((2 * N_CHIPS,) + p_early.shape, F32)
    sm_send, sm_recv, (p_early, land) = _split_start("small_send", [p_early, land], 2 * N_CHIPS - 1, _small_exchange_copies(False))

    tzn = _tile(n_in, (768, 640, 512))
    gw_in = _mm(
        "mm_gw_in", (d // twr, n_in // tzn, 1), [h1, dz],
        [pl.BlockSpec((s, twr), lambda i, j, k: (0, i)), pl.BlockSpec((s, tzn), lambda i, j, k: (0, j))],
        [jax.ShapeDtypeStruct((d, n_in), BF16)], [pl.BlockSpec((twr, tzn), lambda i, j, k: (i, j))],
        [(0, 1, TN, 0)], 1, (twr, tzn), 1, ep_store,
        _mm_vmem([((s, twr), BF16, 3), ((s, tzn), BF16, 2), ((twr, tzn), F32, 3)]),
        after=(red_ffn["token"], p_early),
    )[0]
    red_in = reduce_a("w_in", [W_IN], [gw_in])

    g_gate, g_up = reduce_d(red_ffn, red_in["token"])
    reduce_b(red_in, update(W_GATE, g_gate))

    dh1 = _mm(
        "mm_dh1", (s // tm, d // tn2, 1), [dz, g_in],
        [pl.BlockSpec((tm, n_in), lambda i, j, k: (i, 0)), pl.BlockSpec((tn2, n_in), lambda i, j, k: (j, 0))],
        [jax.ShapeDtypeStruct((s, d), F32)], [pl.BlockSpec((tm, tn2), lambda i, j, k: (i, j))],
        [(0, 1, NT, 0)], 1, (tm, tn2), 1, ep_store,
        _mm_vmem([((tm, n_in), BF16, 2), ((tn2, n_in), BF16, 2), ((tm, tn2), F32, 5)]),
        after=(red_in["token"],),
    )[0]

    reduce_c(red_mix, dh1)
    grad_x, _, dg_mix = _rms_bwd("rms_mix_bwd", x2d, norm_mix, dh1, dx2, after=(red_mix["token"],))

    p_mix = _pack([dg_mix.reshape(small_shapes[0])])
    land_mix = jnp.zeros((2 * N_CHIPS,) + p_mix.shape, F32)
    mx_send, mx_recv, (p_mix, land_mix) = _split_start("mix_send", [p_mix, land_mix], 2 * N_CHIPS - 1, _small_exchange_copies(False))

    reduce_c(red_in, update(W_UP, g_up, (finish(red_mix, p_mix),)))
    p_early, land = _split_wait("small_wait", [p_early, land], sm_send, sm_recv, _small_exchange_copies(True), red_in["token"])
    p_mix, land_mix = _split_wait("mix_wait", [p_mix, land_mix], mx_send, mx_recv, _small_exchange_copies(True), p_early)
    me_idx = 2 * q_idx + c_idx
    packed_g = jnp.concatenate([_small_sum("mix_sum", me_idx, p_mix, land_mix), _small_sum("small_sum", me_idx, p_early, land)], axis=0)
    g_small = _unpack(packed_g, small_shapes + [(1,)])
    loss = g_small[-1].reshape(())
    g_small = g_small[:-1]
    pg = _pack(g_small + [zero1])
    small_upd = _adamw("adamw_small", pw, pg, pm, pv)
    d_small, nm_small, nv_small = [_unpack(a, small_shapes) for a in small_upd[:3]]
    finish(red_in, small_upd[0])

    small_names = ["norm_mix", "sgu_v_gain", "sgu_w_s", "sgu_b_s", "attn_sink", "rel_bias", "norm_ffn", "norm_final"]
    table = {}
    for i, n in enumerate(names):
        table[n] = (grads_big[i][None], upd[i][0][None], upd[i][1][None], upd[i][2][None])
    for i, n in enumerate(small_names):
        table[n] = (g_small[i], d_small[i], nm_small[i], nv_small[i])
    order = ["w_in", "norm_mix", "sgu_v_gain", "sgu_w_s", "sgu_b_s", "w_a", "attn_sink", "rel_bias", "w_b", "w_o", "norm_ffn",
             "w_gate", "w_up", "w_down", "norm_final"]
    outs = [loss, grad_x.reshape(1, s, d)]
    for part in range(4):
        outs += [table[n][part] for n in order]
    return tuple(outs)
```

```python
import math

import jax
import jax.numpy as jnp
import numpy as np
from jax import lax
from jax.experimental import pallas as pl
from jax.experimental.pallas import tpu as pltpu

F32 = jnp.float32
BF16 = jnp.bfloat16
I32 = jnp.int32
MESH = pl.DeviceIdType.MESH

EPS = 1e-6
NEG = -1e30
BLK = 128
HEAD_DIM = 128
N_KV_HEADS = 2
REL_BUCKETS = 32
REL_MAX_DIST = 128
N_CHIPS = 4
ADAM_LR, ADAM_B1, ADAM_B2, ADAM_EPS, ADAM_WD, ADAM_STEP = 0.001, 0.9, 0.999, 1e-08, 0.01, 10

LANES = 128
VMEM_CAP = 60 * 1024 * 1024

NN = (((1,), (0,)), ((), ()))
NT = (((1,), (1,)), ((), ()))
TN = (((0,), (0,)), ((), ()))
ANY = pl.BlockSpec(memory_space=pl.ANY)
HBM_SPEC = pl.BlockSpec(memory_space=pltpu.HBM)
SEM_SPEC = pl.BlockSpec(memory_space=pltpu.SEMAPHORE)
EFFECT = pltpu.SideEffectType.DATAFLOW_SIDE_EFFECTING


def _tile(n, cands):
    for t in cands:
        if n % t == 0:
            return t
    return n


PIN_BYTES = 64 * 1024


def _pin_hbm(a):
    big = hasattr(a, "dtype") and jnp.issubdtype(a.dtype, jnp.floating) and _nbytes(a.shape, a.dtype) >= PIN_BYTES
    return pltpu.with_memory_space_constraint(a, pltpu.HBM) if big else a


def _pallas(body, *, out_shape, **kw):
    def pin(o):
        big = isinstance(o, jax.ShapeDtypeStruct) and jnp.issubdtype(o.dtype, jnp.floating) and _nbytes(o.shape, o.dtype) >= PIN_BYTES
        return pltpu.HBM(o.shape, o.dtype) if big else o

    shapes = type(out_shape)(pin(o) for o in out_shape) if isinstance(out_shape, (list, tuple)) else pin(out_shape)
    call = pl.pallas_call(body, out_shape=shapes, **kw)
    return lambda *args: call(*[_pin_hbm(a) for a in args])


def _params(vmem_bytes=None, **kw):
    if vmem_bytes is not None:
        kw["vmem_limit_bytes"] = int(min(max(vmem_bytes, 32 * 1024 * 1024), VMEM_CAP))
    return pltpu.CompilerParams(**kw)


def _nbytes(shape, dtype):
    return int(np.prod(shape)) * jnp.dtype(dtype).itemsize


def _sigmoid(x):
    return 1.0 / (1.0 + jnp.exp(-x))


_GC = 0.7978845608028654
_GA = 0.044715


def _gelu(x):
    return 0.5 * x * (1.0 + jnp.tanh(_GC * (x + _GA * (x * x * x))))


def _gelu_grad(x):
    t = jnp.tanh(_GC * (x + _GA * (x * x * x)))
    return 0.5 * (1.0 + t) + 0.5 * x * (1.0 - t * t) * (_GC * (1.0 + 3.0 * _GA * (x * x)))


def _bf(v):
    return v if v.dtype == BF16 else v.astype(BF16)


def _mm(name, grid, ins, in_specs, out_shape, out_specs, pairs, n_acc, tile, nk, epilogue, vmem_bytes, after=()):
    assert nk == 1
    n_in, n_out = len(ins) + len(after), len(out_shape)

    def body(*refs):
        in_refs, out_refs = refs[:n_in], refs[n_in : n_in + n_out]
        vals = [None] * n_acc
        for a_i, b_i, dn, acc_i in pairs:
            d = lax.dot_general(_bf(in_refs[a_i][...]), _bf(in_refs[b_i][...]), dn, preferred_element_type=F32)
            vals[acc_i] = d if vals[acc_i] is None else vals[acc_i] + d
        epilogue(in_refs, vals, out_refs, slice(None))

    return _pallas(
        body,
        name=name,
        grid=grid,
        in_specs=list(in_specs) + [ANY] * len(after),
        out_specs=out_specs,
        out_shape=out_shape,
        compiler_params=_params(vmem_bytes),
    )(*ins, *after)


def _put(ref, cs, v):
    ref[:, cs] = v.astype(ref.dtype)


def _mm_vmem(tiles):
    return sum(_nbytes(s, d) * c for s, d, c in tiles) + 4 * 1024 * 1024


def _rows8(v):
    r, d = v.shape
    return v.reshape(r // 8, 8, d).sum(axis=0)


def _rms_fwd(name, x, g, after=()):
    s, d = x.shape
    tm = _tile(s, (256, 128))

    def body(x_ref, g_ref, *rest):
        h_ref = rest[-1]
        xv = x_ref[...]
        r = lax.rsqrt(jnp.mean(xv * xv, axis=-1, keepdims=True) + EPS)
        h_ref[...] = ((xv * r) * g_ref[...]).astype(BF16)

    return _pallas(
        body,
        name=name,
        grid=(s // tm,),
        in_specs=[pl.BlockSpec((tm, d), lambda i: (i, 0)), pl.BlockSpec((1, d), lambda i: (0, 0))] + [ANY] * len(after),
        out_specs=pl.BlockSpec((tm, d), lambda i: (i, 0)),
        out_shape=jax.ShapeDtypeStruct((s, d), BF16),
    )(x, g, *after)


def _rms_bwd(name, x, g, dh, dres, after=()):
    s, d = x.shape
    tm = _tile(s, (256, 128))
    n = s // tm
    n_after = len(after)

    def body(x_ref, g_ref, dh_ref, dres_ref, *rest):
        dx_ref, dxb_ref, dg_ref, acc_ref = rest[n_after:]
        i = pl.program_id(0)
        xv = x_ref[...]
        r = lax.rsqrt(jnp.mean(xv * xv, axis=-1, keepdims=True) + EPS)
        xh = xv * r
        dhv = dh_ref[...]
        dxh = dhv * g_ref[...]
        dx = r * (dxh - xh * jnp.mean(dxh * xh, axis=-1, keepdims=True)) + dres_ref[...]
        dx_ref[...] = dx
        dxb_ref[...] = dx.astype(BF16)
        part = _rows8(dhv * xh)

        @pl.when(i == 0)
        def _():
            acc_ref[...] = part

        @pl.when(i > 0)
        def _():
            acc_ref[...] += part

        @pl.when(i == n - 1)
        def _():
            dg_ref[...] = jnp.sum(acc_ref[...], axis=0, keepdims=True)

    row = pl.BlockSpec((tm, d), lambda i: (i, 0))
    vec = pl.BlockSpec((1, d), lambda i: (0, 0))
    return _pallas(
        body,
        name=name,
        grid=(n,),
        in_specs=[row, vec, row, row] + [ANY] * n_after,
        out_specs=[row, row, vec],
        out_shape=[jax.ShapeDtypeStruct((s, d), F32), jax.ShapeDtypeStruct((s, d), BF16), jax.ShapeDtypeStruct((1, d), F32)],
        scratch_shapes=[pltpu.VMEM((8, d), F32)],
    )(x, g, dh, dres, *after)


def _head(x3, g, target):
    s, d = x3.shape
    tm = _tile(s, (256, 128))
    n = s // tm

    def body(x_ref, g_ref, t_ref, dx_ref, dxb_ref, dg_ref, loss_ref, acc_g, acc_l):
        i = pl.program_id(0)
        xv = x_ref[...]
        gv = g_ref[...]
        r = lax.rsqrt(jnp.mean(xv * xv, axis=-1, keepdims=True) + EPS)
        xh = xv * r
        e = xh * gv - t_ref[...]
        dy = e * (1.0 / d)
        dxh = dy * gv
        dx = r * (dxh - xh * jnp.mean(dxh * xh, axis=-1, keepdims=True))
        dx_ref[...] = dx
        dxb_ref[...] = dx.astype(BF16)
        pg = _rows8(dy * xh)
        plo = _rows8(e * e)

        @pl.when(i == 0)
        def _():
            acc_g[...] = pg
            acc_l[...] = plo

        @pl.when(i > 0)
        def _():
            acc_g[...] += pg
            acc_l[...] += plo

        @pl.when(i == n - 1)
        def _():
            dg_ref[...] = jnp.sum(acc_g[...], axis=0, keepdims=True)
            loss_ref[...] = jnp.full((1, LANES), (0.5 / d) * jnp.sum(acc_l[...]), F32)

    row = pl.BlockSpec((tm, d), lambda i: (i, 0))
    vec = pl.BlockSpec((1, d), lambda i: (0, 0))
    return _pallas(
        body,
        name="head",
        grid=(n,),
        in_specs=[row, vec, row],
        out_specs=[row, row, vec, pl.BlockSpec((1, LANES), lambda i: (0, 0))],
        out_shape=[
            jax.ShapeDtypeStruct((s, d), F32),
            jax.ShapeDtypeStruct((s, d), BF16),
            jax.ShapeDtypeStruct((1, d), F32),
            jax.ShapeDtypeStruct((1, LANES), F32),
        ],
        scratch_shapes=[pltpu.VMEM((8, d), F32), pltpu.VMEM((8, d), F32)],
    )(x3, g, target)


def _sgu_fwd(z, gain, ws_b, b_col, w_sgu, after=()):
    s = z.shape[0]
    groups = ws_b.shape[0]

    def body(zu_ref, zv_ref, gain_ref, ws_ref, b_ref, *rest):
        a_ref = rest[-1]
        vv = _gelu(zv_ref[...].astype(F32))
        r = lax.rsqrt(jnp.mean(vv * vv, axis=-1, keepdims=True) + EPS)
        vn = ((vv * r) * gain_ref[...]).astype(BF16)
        u = _gelu(zu_ref[...].astype(F32))
        for g in range(groups):
            sl = slice(g * BLK, (g + 1) * BLK)
            mixed = jnp.dot(ws_ref[g], vn[:, sl], preferred_element_type=F32) + b_ref[g]
            a_ref[:, sl] = (u[:, sl] * mixed).astype(BF16)

    return _pallas(
        body,
        name="sgu_fwd",
        grid=(s // BLK,),
        in_specs=[
            pl.BlockSpec((BLK, w_sgu), lambda c: (c, 0)),
            pl.BlockSpec((BLK, w_sgu), lambda c: (c, 1)),
            pl.BlockSpec((1, w_sgu), lambda c: (0, 0)),
            pl.BlockSpec((groups, BLK, BLK), lambda c: (0, 0, 0)),
            pl.BlockSpec((groups, BLK, 1), lambda c: (0, 0, 0)),
        ]
        + [ANY] * len(after),
        out_specs=pl.BlockSpec((BLK, w_sgu), lambda c: (c, 0)),
        out_shape=jax.ShapeDtypeStruct((s, w_sgu), BF16),
    )(z, z, gain, ws_b, b_col, *after)


def _sgu_bwd(z, da, gain, ws_b, wst_b, b_col, w_sgu, dz, after=()):
    s = z.shape[0]
    groups = ws_b.shape[0]
    n = s // BLK
    n_skip = 1 + len(after)

    def body(zu_ref, zv_ref, da_ref, gain_ref, ws_ref, wst_ref, b_ref, *rest):
        dz_ref, dws_ref, dbs_ref, dgain_ref, acc_gain, vv_s, gv_s, dxh_s = rest[n_skip:]
        c = pl.program_id(0)
        cols = [slice(g * BLK, (g + 1) * BLK) for g in range(groups)]

        ss = jnp.zeros((BLK, 1), F32)
        for sl in cols:
            zv = zv_ref[:, sl].astype(F32)
            vv = _gelu(zv)
            vv_s[:, sl] = vv
            gv_s[:, sl] = _gelu_grad(zv)
            ss = ss + jnp.sum(vv * vv, axis=-1, keepdims=True)
        r = lax.rsqrt(ss * (1.0 / w_sgu) + EPS)

        dot_dx = jnp.zeros((BLK, 1), F32)
        for g, sl in enumerate(cols):
            gain_g = gain_ref[:, sl]
            xh = vv_s[:, sl] * r
            vn = (xh * gain_g).astype(BF16)
            zu = zu_ref[:, sl].astype(F32)
            dav = da_ref[:, sl].astype(F32)
            dmix = dav * _gelu(zu)
            dmix_b = dmix.astype(BF16)
            mixed = jnp.dot(ws_ref[g], vn, preferred_element_type=F32) + b_ref[g]
            dz_ref[:, sl] = (dav * mixed * _gelu_grad(zu)).astype(BF16)
            dvn = jnp.dot(wst_ref[g], dmix_b, preferred_element_type=F32)
            dws_g = lax.dot_general(dmix_b, vn, NT, preferred_element_type=F32)
            dbs_g = jnp.sum(dmix, axis=1, keepdims=True)
            pg = _rows8(dvn * xh)

            @pl.when(c == 0)
            def _():
                dws_ref[g] = dws_g
                dbs_ref[g] = dbs_g
                acc_gain[:, sl] = pg

            @pl.when(c > 0)
            def _():
                dws_ref[g] += dws_g
                dbs_ref[g] += dbs_g
                acc_gain[:, sl] += pg

            dxh = dvn * gain_g
            dxh_s[:, sl] = dxh
            dot_dx = dot_dx + jnp.sum(dxh * xh, axis=-1, keepdims=True)

        mean_dx = dot_dx * (1.0 / w_sgu)
        for g, sl in enumerate(cols):
            dvv = r * (dxh_s[:, sl] - (vv_s[:, sl] * r) * mean_dx)
            dz_ref[:, w_sgu + g * BLK : w_sgu + (g + 1) * BLK] = (dvv * gv_s[:, sl]).astype(BF16)

        @pl.when(c == n - 1)
        def _():
            dgain_ref[...] = jnp.sum(acc_gain[...], axis=0, keepdims=True)

    full3 = pl.BlockSpec((groups, BLK, BLK), lambda c: (0, 0, 0))
    col3 = pl.BlockSpec((groups, BLK, 1), lambda c: (0, 0, 0))
    vec = pl.BlockSpec((1, w_sgu), lambda c: (0, 0))
    return _pallas(
        body,
        name="sgu_bwd",
        grid=(n,),
        in_specs=[
            pl.BlockSpec((BLK, w_sgu), lambda c: (c, 0)),
            pl.BlockSpec((BLK, w_sgu), lambda c: (c, 1)),
            pl.BlockSpec((BLK, w_sgu), lambda c: (c, 0)),
            vec,
            full3,
            full3,
            col3,
            ANY,
        ]
        + [ANY] * len(after),
        out_specs=[pl.BlockSpec((BLK, 2 * w_sgu), lambda c: (c, 0)), full3, col3, vec],
        out_shape=[
            jax.ShapeDtypeStruct(dz.shape, BF16),
            jax.ShapeDtypeStruct((groups, BLK, BLK), F32),
            jax.ShapeDtypeStruct((groups, BLK, 1), F32),
            jax.ShapeDtypeStruct((1, w_sgu), F32),
        ],
        scratch_shapes=[pltpu.VMEM((8, w_sgu), F32)] + [pltpu.VMEM((BLK, w_sgu), F32)] * 3,
        input_output_aliases={7: 0},
    )(z, z, da, gain, ws_b, wst_b, b_col, dz, *after)


def _attn_softmax(sink_ref, q_ref, k_ref, v_ref, bias_ref, s_len, grp):
    kv = pl.program_id(0)
    n = pl.program_id(1)
    start = pl.multiple_of(n * BLK, BLK)
    kb = k_ref[pl.ds(start, 3 * BLK), :]
    vb = v_ref[pl.ds(start, 3 * BLK), :]
    qv = q_ref[...]
    qs = jnp.concatenate([qv[:, g * HEAD_DIM : (g + 1) * HEAD_DIM] for g in range(grp)], axis=0).astype(BF16)
    sc = lax.dot_general(qs, kb, NT, preferred_element_type=F32) * (HEAD_DIM**-0.5)
    sc = sc + bias_ref[...].reshape(grp * BLK, 3 * BLK)
    kpos = start + lax.broadcasted_iota(I32, (1, 3 * BLK), 1) - BLK
    sc = jnp.where((kpos >= 0) & (kpos < s_len), sc, NEG)
    sink = jnp.concatenate([jnp.full((BLK, 1), sink_ref[kv * grp + g], F32) for g in range(grp)], axis=0)
    m = jnp.maximum(jnp.max(sc, axis=-1, keepdims=True), sink)
    p = jnp.exp(sc - m)
    esink = jnp.exp(sink - m)
    den = jnp.sum(p, axis=-1, keepdims=True) + esink
    return start, qs, kb, vb, p / den, esink / den


def _attn_specs(s, grp, q_blk0):
    qw = grp * HEAD_DIM
    return [
        pl.BlockSpec(memory_space=pltpu.SMEM),
        pl.BlockSpec((BLK, qw), lambda kv, n: (n, q_blk0 + kv)),
        pl.BlockSpec((s + 2 * BLK, HEAD_DIM), lambda kv, n: (0, kv)),
        pl.BlockSpec((s + 2 * BLK, HEAD_DIM), lambda kv, n: (0, kv)),
        pl.BlockSpec((grp, BLK, 3 * BLK), lambda kv, n: (kv, 0, 0)),
    ]


def _attn_fwd(sink, z, k_pad, v_pad, bias_tab, grp, q_blk0):
    s = z.shape[0]
    qw = grp * HEAD_DIM

    def body(sink_ref, q_ref, k_ref, v_ref, bias_ref, o_ref):
        _, _, _, vb, pn, _ = _attn_softmax(sink_ref, q_ref, k_ref, v_ref, bias_ref, s, grp)
        o = jnp.dot(pn.astype(BF16), vb, preferred_element_type=F32)
        for g in range(grp):
            o_ref[:, g * HEAD_DIM : (g + 1) * HEAD_DIM] = o[g * BLK : (g + 1) * BLK].astype(BF16)

    return _pallas(
        body,
        name="attn_fwd",
        grid=(N_KV_HEADS, s // BLK),
        in_specs=_attn_specs(s, grp, q_blk0),
        out_specs=pl.BlockSpec((BLK, qw), lambda kv, n: (n, kv)),
        out_shape=jax.ShapeDtypeStruct((s, N_KV_HEADS * qw), BF16),
    )(sink, z, k_pad, v_pad, bias_tab)


def _attn_bwd(sink, z, k_pad, v_pad, bias_tab, dout, dz, grp, q_blk0):
    s = z.shape[0]
    qw = grp * HEAD_DIM
    nb = s // BLK
    heads = N_KV_HEADS * grp

    def body(sink_ref, q_ref, k_ref, v_ref, bias_ref, do_ref, dz_in, dq_ref, dk_ref, dv_ref, dbias_ref, dsink_ref, dk_acc, dv_acc):
        del dz_in
        kv = pl.program_id(0)
        n = pl.program_id(1)
        start, qs, kb, vb, pn, psink = _attn_softmax(sink_ref, q_ref, k_ref, v_ref, bias_ref, s, grp)
        dov = do_ref[...]
        dos = jnp.concatenate([dov[:, g * HEAD_DIM : (g + 1) * HEAD_DIM] for g in range(grp)], axis=0)
        dp = lax.dot_general(dos, vb, NT, preferred_element_type=F32)
        dvb = lax.dot_general(pn.astype(BF16), dos, TN, preferred_element_type=F32)
        delta = jnp.sum(pn * dp, axis=-1, keepdims=True)
        ds = pn * (dp - delta)
        dsb = (ds * (HEAD_DIM**-0.5)).astype(BF16)
        dq = jnp.dot(dsb, kb, preferred_element_type=F32)
        dkb = lax.dot_general(dsb, qs, TN, preferred_element_type=F32)
        for g in range(grp):
            dq_ref[:, g * HEAD_DIM : (g + 1) * HEAD_DIM] = dq[g * BLK : (g + 1) * BLK].astype(BF16)

        @pl.when(n == 0)
        def _():
            dk_acc[...] = jnp.zeros_like(dk_acc)
            dv_acc[...] = jnp.zeros_like(dv_acc)
            dbias_ref[...] = jnp.zeros_like(dbias_ref)

        @pl.when((n == 0) & (kv == 0))
        def _():
            dsink_ref[...] = jnp.zeros_like(dsink_ref)

        dk_acc[pl.ds(start, 3 * BLK), :] += dkb
        dv_acc[pl.ds(start, 3 * BLK), :] += dvb
        dbias_ref[...] += ds.reshape(grp, BLK, 3 * BLK)
        row = lax.broadcasted_iota(I32, (heads, LANES), 0)
        sd = psink * delta
        upd = jnp.zeros((heads, LANES), F32)
        for g in range(grp):
            upd = jnp.where(row == kv * grp + g, -jnp.sum(sd[g * BLK : (g + 1) * BLK]), upd)
        dsink_ref[...] += upd

        @pl.when(n == nb - 1)
        def _():
            dk_ref[...] = dk_acc[...]
            dv_ref[...] = dv_acc[...]

    pad_spec = pl.BlockSpec((s + 2 * BLK, HEAD_DIM), lambda kv, n: (0, kv))
    kvw = N_KV_HEADS * HEAD_DIM
    return _pallas(
        body,
        name="attn_bwd",
        grid=(N_KV_HEADS, nb),
        in_specs=_attn_specs(s, grp, q_blk0) + [pl.BlockSpec((BLK, qw), lambda kv, n: (n, kv)), ANY],
        out_specs=[
            pl.BlockSpec((BLK, qw), lambda kv, n: (n, q_blk0 + kv)),
            pad_spec,
            pad_spec,
            pl.BlockSpec((grp, BLK, 3 * BLK), lambda kv, n: (kv, 0, 0)),
            pl.BlockSpec((heads, LANES), lambda kv, n: (0, 0)),
        ],
        out_shape=[
            jax.ShapeDtypeStruct(dz.shape, BF16),
            jax.ShapeDtypeStruct((s + 2 * BLK, kvw), F32),
            jax.ShapeDtypeStruct((s + 2 * BLK, kvw), F32),
            jax.ShapeDtypeStruct((heads, BLK, 3 * BLK), F32),
            jax.ShapeDtypeStruct((heads, LANES), F32),
        ],
        scratch_shapes=[pltpu.VMEM((s + 2 * BLK, HEAD_DIM), F32), pltpu.VMEM((s + 2 * BLK, HEAD_DIM), F32)],
        input_output_aliases={6: 0},
    )(sink, z, k_pad, v_pad, bias_tab, dout, dz)


def _dkv_to_dz(dk_pad, dv_pad, dz, blk_idx):
    s = dz.shape[0]
    kvw = dk_pad.shape[1]

    def body(dk_ref, dv_ref, dz_in, out_ref):
        del dz_in
        out_ref[:, :kvw] = dk_ref[...].astype(BF16)
        out_ref[:, kvw:] = dv_ref[...].astype(BF16)

    src = pl.BlockSpec((BLK, kvw), lambda i: (i + 1, 0))
    return _pallas(
        body,
        name="dkv_to_dz",
        grid=(s // BLK,),
        in_specs=[src, src, ANY],
        out_specs=pl.BlockSpec((BLK, 2 * kvw), lambda i: (i, blk_idx)),
        out_shape=jax.ShapeDtypeStruct(dz.shape, BF16),
        input_output_aliases={2: 0},
    )(dk_pad, dv_pad, dz)


def _relbias_bwd(dbias_tab, bucket):
    heads = dbias_tab.shape[0]

    def body(dt_ref, bk_ref, out_ref):
        lane = lax.broadcasted_iota(I32, (1, LANES), 1)
        bk = bk_ref[...]
        rows = []
        for h in range(heads):
            dt = dt_ref[h]
            acc = jnp.zeros((1, LANES), F32)
            for b in range(REL_BUCKETS):
                acc = jnp.where(lane == b, jnp.sum(jnp.where(bk == b, dt, 0.0)), acc)
            rows.append(acc)
        out_ref[...] = jnp.concatenate(rows, axis=0)

    return _pallas(body, name="relbias_bwd", out_shape=jax.ShapeDtypeStruct((heads, LANES), F32))(dbias_tab, bucket)


def _t5_bucket(rel):
    nb = REL_BUCKETS // 2
    ret = jnp.where(rel > 0, nb, 0)
    n = jnp.abs(rel)
    max_exact = nb // 2
    nf = jnp.maximum(n, 1).astype(F32)
    large = max_exact + (jnp.log(nf / max_exact) / math.log(REL_MAX_DIST / max_exact) * (nb - max_exact)).astype(I32)
    large = jnp.minimum(large, nb - 1)
    return ret + jnp.where(n < max_exact, n, large)


def _band_tables(rel_bias):
    qi = jnp.arange(BLK)[:, None]
    kj = jnp.arange(3 * BLK)[None, :]
    rel = kj - BLK - qi
    bucket = _t5_bucket(rel).astype(I32)
    heads = rel_bias.shape[1]
    masked = jnp.where(jnp.abs(rel) <= BLK, bucket, -1)

    def body(rb_ref, bk_ref, out_ref):
        bk = bk_ref[...]
        for h in range(heads):
            tab = jnp.full(bk.shape, NEG, F32)
            for b in range(REL_BUCKETS):
                tab = jnp.where(bk == b, rb_ref[b, h], tab)
            out_ref[h] = tab

    bias_tab = _pallas(
        body,
        name="bias_table",
        in_specs=[pl.BlockSpec(memory_space=pltpu.SMEM), pl.BlockSpec(memory_space=pltpu.VMEM)],
        out_specs=pl.BlockSpec(memory_space=pltpu.VMEM),
        out_shape=jax.ShapeDtypeStruct((heads, BLK, 3 * BLK), F32),
    )(rel_bias.astype(F32), masked)
    return bias_tab, bucket


EW_BLOCK_ELEMS = 512 * 1024


def _ew_tiles(shape, elems=EW_BLOCK_ELEMS // 2):
    r, c = shape
    tn = c if c <= 2048 else _tile(c, (2048, 1920, 1536, 1408, 1024, 512))
    tm = _tile(r, [t for t in (1024, 512, 256, 128, 64, 32, 16, 8) if t * tn <= elems] or [8])
    return tm, tn


def _cast_into_full(name, qidx, w, kind, after=()):
    r, c = w.shape
    tm, tn = _ew_tiles(w.shape, EW_BLOCK_ELEMS)
    nbi, nbj = r // tm, c // tn
    if kind == "col":
        full, out_spec = (r, c * N_CHIPS), pl.BlockSpec((tm, tn), lambda i, j, q: (i, q[0] * nbj + j))
    else:
        full, out_spec = (r * N_CHIPS, c), pl.BlockSpec((tm, tn), lambda i, j, q: (q[0] * nbi + i, j))

    def body(q_ref, w_ref, *rest):
        del q_ref
        rest[-1][...] = w_ref[...].astype(BF16)

    return _pallas(
        body,
        name=name,
        grid_spec=pltpu.PrefetchScalarGridSpec(
            num_scalar_prefetch=1,
            grid=(nbi, nbj),
            in_specs=[pl.BlockSpec((tm, tn), lambda i, j, q: (i, j))] + [ANY] * len(after),
            out_specs=out_spec,
        ),
        out_shape=jax.ShapeDtypeStruct(full, BF16),
    )(qidx, w, *after)


def _adamw(name, w, g, m, v, after=()):
    tm, tn = _ew_tiles(w.shape, EW_BLOCK_ELEMS)
    if _nbytes(w.shape, F32) <= 1024 * 1024:
        tm, tn = w.shape
    spec = pl.BlockSpec((tm, tn), lambda i, j: (i, j))
    n_after = len(after)

    def body(w_ref, g_ref, m_ref, v_ref, *rest):
        d_ref, nm_ref, nv_ref, g_out_ref = rest[n_after:]
        gv = g_ref[...]
        g_out_ref[...] = gv
        nm = ADAM_B1 * m_ref[...] + (1.0 - ADAM_B1) * gv
        nv = ADAM_B2 * v_ref[...] + (1.0 - ADAM_B2) * (gv * gv)
        m_hat = nm / (1.0 - ADAM_B1**ADAM_STEP)
        v_hat = nv / (1.0 - ADAM_B2**ADAM_STEP)
        d_ref[...] = -ADAM_LR * (m_hat / (jnp.sqrt(v_hat) + ADAM_EPS) + ADAM_WD * w_ref[...])
        nm_ref[...] = nm
        nv_ref[...] = nv

    out = jax.ShapeDtypeStruct(w.shape, F32)
    return _pallas(
        body, name=name, grid=(w.shape[0] // tm, w.shape[1] // tn), in_specs=[spec] * 4 + [ANY] * n_after,
        out_specs=[spec] * 4, out_shape=[out, out, out, out],
        compiler_params=_params(_mm_vmem([((tm, tn), F32, 24)])),
    )(w, g, m, v, *after)


def _pair_add(name, cidx, g_full, r_sib, kind):
    hr, hc = r_sib.shape
    tm, tn = _ew_tiles((hr, hc), 2 * EW_BLOCK_ELEMS)
    nbi, nbj = hr // tm, hc // tn
    if kind == "col":
        g_spec = pl.BlockSpec((tm, tn), lambda i, j, c: (c[0] * nbi + i, j))
    else:
        g_spec = pl.BlockSpec((tm, tn), lambda i, j, c: (i, c[0] * nbj + j))
    spec = pl.BlockSpec((tm, tn), lambda i, j, c: (i, j))

    def body(c_ref, g_ref, r_ref, o_ref):
        del c_ref
        o_ref[...] = (g_ref[...].astype(F32) + r_ref[...].astype(F32)).astype(BF16)

    return _pallas(
        body,
        name=name,
        grid_spec=pltpu.PrefetchScalarGridSpec(num_scalar_prefetch=1, grid=(nbi, nbj), in_specs=[g_spec, spec], out_specs=spec),
        out_shape=jax.ShapeDtypeStruct((hr, hc), BF16),
        compiler_params=_params(_mm_vmem([((tm, tn), BF16, 6), ((tm, tn), F32, 3)])),
    )(cidx, g_full, r_sib)


def _chip_sum(name, qidx, c_half, r_ici, kind):
    _, pr, pc = r_ici.shape
    tm, tn = _ew_tiles((pr, pc), 2 * EW_BLOCK_ELEMS)
    nbi, nbj = pr // tm, pc // tn
    if kind == "col":
        own_spec = pl.BlockSpec((tm, tn), lambda i, j, q: (i, q[0] * nbj + j))
        full, out_spec = (2 * pr, pc), pl.BlockSpec((tm, tn), lambda i, j, q: (q[1] * nbi + i, j))
    else:
        own_spec = pl.BlockSpec((tm, tn), lambda i, j, q: (q[0] * nbi + i, j))
        full, out_spec = (pr, 2 * pc), pl.BlockSpec((tm, tn), lambda i, j, q: (i, q[1] * nbj + j))

    def body(q_ref, own_ref, r_ref, o_ref):
        q = q_ref[0]
        own = own_ref[...].astype(F32)
        recv = [r_ref[r].astype(F32) for r in range(3)]
        total = None
        for chip in range(N_CHIPS):
            d = chip ^ q
            term = jnp.where(d == 0, own, jnp.where(d == 2, recv[0], jnp.where(d == 1, recv[1], recv[2])))
            total = term if total is None else total + term
        o_ref[...] = total

    return _pallas(
        body,
        name=name,
        grid_spec=pltpu.PrefetchScalarGridSpec(
            num_scalar_prefetch=1,
            grid=(nbi, nbj),
            in_specs=[own_spec, pl.BlockSpec((3, tm, tn), lambda i, j, q: (0, i, j))],
            out_specs=out_spec,
        ),
        out_shape=jax.ShapeDtypeStruct(full, F32),
        compiler_params=_params(_mm_vmem([((tm, tn), BF16, 8), ((tm, tn), F32, 6)])),
    )(qidx, c_half, r_ici)


_REL_MASK = (2, 1, 3)


def _place():
    x, y, c = lax.axis_index("x"), lax.axis_index("y"), lax.axis_index("c")
    chips = [(1 - x, y), (x, 1 - y), (1 - x, 1 - y)]
    return x, y, c, 2 * x + y, chips


def _shard_view(ref, kind, chip):
    if kind == "col":
        w = ref.shape[1] // N_CHIPS
        return ref.at[:, pl.ds(pl.multiple_of(chip * w, LANES), w)]
    h = ref.shape[0] // N_CHIPS
    return ref.at[pl.ds(pl.multiple_of(chip * h, 16), h), :]


def _row_half(ref, half):
    h = ref.shape[0] // 2
    return ref.at[pl.ds(pl.multiple_of(half * h, 16), h), :]


def _pair_half(ref, kind, half):
    if kind == "col":
        return _row_half(ref, half)
    w = ref.shape[1] // 2
    return ref.at[:, pl.ds(pl.multiple_of(half * w, LANES), w)]


def _remote(src, dst, send_sem, recv_sem, dev):
    return pltpu.make_async_remote_copy(src_ref=src, dst_ref=dst, send_sem=send_sem, recv_sem=recv_sem, device_id=dev, device_id_type=MESH)


def _hbm(a):
    return pltpu.with_memory_space_constraint(a, pltpu.HBM)


def _gather_start(name, fulls, kinds, rels=(0, 1, 2), after=()):
    n_w = len(fulls)

    def body(*refs):
        g = refs[:n_w]
        send_sem, recv_sem = refs[n_w + len(after)], refs[n_w + len(after) + 1]
        token = refs[-1]
        _, _, c, q, chips = _place()
        for w in range(n_w):
            mine = _row_half(_shard_view(g[w], kinds[w], q), c)
            for r in rels if isinstance(rels, tuple) else rels[w]:
                _remote(mine, mine, send_sem.at[3 * w + r], recv_sem.at[3 * w + r], (*chips[r], c)).start()
        token[...] = jnp.zeros_like(token)

    res = _pallas(
        body,
        name=name,
        out_shape=(
            pltpu.SemaphoreType.DMA((3 * n_w,)),
            pltpu.SemaphoreType.DMA((3 * n_w,)),
            *[pltpu.HBM(f.shape, f.dtype) for f in fulls],
            jax.ShapeDtypeStruct((8, LANES), F32),
        ),
        in_specs=[HBM_SPEC] * n_w + [ANY] * len(after),
        out_specs=(SEM_SPEC, SEM_SPEC, *[HBM_SPEC] * n_w, pl.BlockSpec(memory_space=pltpu.VMEM)),
        input_output_aliases={w: w + 2 for w in range(n_w)},
        compiler_params=pltpu.CompilerParams(has_side_effects=EFFECT),
    )(*[_hbm(f) for f in fulls], *after)
    return res[0], res[1], list(res[2 : 2 + n_w]), res[-1]


def _relay_copies(kinds, waiting):
    def copies(refs, send_sem, recv_sem):
        _, _, c, q, chips = _place()
        out = []
        for i, kind in enumerate(kinds):
            for k, (src_rel, dst_rel) in enumerate(((0, 1), (1, 0))):
                held = _row_half(_row_half(_shard_view(refs[i], kind, q ^ _REL_MASK[src_rel]), c), k)
                far = _row_half(_row_half(_shard_view(refs[i], kind, q ^ _REL_MASK[2]), c), k)
                dst = far if waiting else held
                out.append(_remote(held, dst, send_sem.at[2 * i + k], recv_sem.at[2 * i + k], (*chips[dst_rel], c)))
        return out

    return copies


def _forward_copies(kinds, waiting, rels=(0, 1, 2), sem0=0):
    def copies(refs, send_sem, recv_sem):
        x, y, c, q, _ = _place()
        out = []
        for i, kind in enumerate(kinds):
            for k, r in enumerate(rels):
                quarter = _shard_view(refs[i], kind, q ^ _REL_MASK[r])
                landed = _row_half(quarter, c)
                dst = _row_half(quarter, 1 - c) if waiting else landed
                sem = sem0 + len(rels) * i + k
                out.append(_remote(landed, dst, send_sem.at[sem], recv_sem.at[sem], (x, y, 1 - c)))
        return out

    return copies


def _relay_and_forward_copies(kinds, waiting):
    forward = _forward_copies(kinds, waiting, rels=(0, 1), sem0=2 * len(kinds))
    relay = _relay_copies(kinds, waiting)
    return lambda refs, send_sem, recv_sem: forward(refs, send_sem, recv_sem) + relay(refs, send_sem, recv_sem)


def _gather_wait(name, fulls, kinds, w_ids, send_sem, recv_sem, after, rels=(0, 1, 2)):
    n = len(fulls)

    def body(*refs):
        g = refs[:n]
        s_sem, r_sem = refs[n], refs[n + 1]
        x, y, c, q, _ = _place()
        for i, w in enumerate(w_ids):
            mine = _row_half(_shard_view(g[i], kinds[i], q), c)
            for r in rels:
                landed = _row_half(_shard_view(g[i], kinds[i], q ^ _REL_MASK[r]), c)
                cp = _remote(mine, landed, s_sem.at[3 * w + r], r_sem.at[3 * w + r], (x, y, 1 - c))
                cp.wait_send()
                cp.wait_recv()

    res = _pallas(
        body,
        name=name,
        out_shape=[pltpu.HBM(f.shape, f.dtype) for f in fulls],
        in_specs=[HBM_SPEC] * n + [SEM_SPEC, SEM_SPEC, ANY],
        out_specs=[HBM_SPEC] * n,
        input_output_aliases={i: i for i in range(n)},
        compiler_params=pltpu.CompilerParams(has_side_effects=EFFECT),
    )(*fulls, send_sem, recv_sem, after)
    return list(res)


def _gather_forward(name, fulls, kinds, rels=(0, 1, 2)):
    n = len(fulls)

    def body(*refs):
        g = refs[n : 2 * n]
        send, recv = refs[2 * n :]
        _sibling_handshake()
        x, y, c, q, _ = _place()
        sib = (x, y, 1 - c)
        cps = []
        for i in range(n):
            for r in rels:
                landed = _row_half(_shard_view(g[i], kinds[i], q ^ _REL_MASK[r]), c)
                cps.append(_remote(landed, landed, send.at[i, r], recv.at[i, r], sib))
        for cp in cps:
            cp.start()
        for i in range(n):
            for r in rels:
                other = _row_half(_shard_view(g[i], kinds[i], q ^ _REL_MASK[r]), 1 - c)
                _remote(other, other, send.at[i, r], recv.at[i, r], sib).wait_recv()
        for cp in cps:
            cp.wait_send()

    res = _pallas(
        body,
        name=name,
        in_specs=[ANY] * n,
        out_specs=[ANY] * n,
        out_shape=[jax.ShapeDtypeStruct(f.shape, f.dtype) for f in fulls],
        scratch_shapes=[pltpu.SemaphoreType.DMA((n, 3)), pltpu.SemaphoreType.DMA((n, 3))],
        input_output_aliases={i: i for i in range(n)},
        compiler_params=pltpu.CompilerParams(collective_id=SIBLING_BARRIER_ID),
    )(*fulls)
    return list(res)


SIBLING_BARRIER_ID = 1


def _sibling_handshake():
    sib = (lax.axis_index("x"), lax.axis_index("y"), 1 - lax.axis_index("c"))
    barrier = pltpu.get_barrier_semaphore()
    pl.semaphore_signal(barrier, inc=1, device_id=sib, device_id_type=MESH)
    pl.semaphore_wait(barrier, 1)


def _split_start(name, bufs, n_sems, copies, sibling_only=False):
    n = len(bufs)

    def body(*refs):
        if sibling_only:
            _sibling_handshake()
        for cp in copies(refs[:n], refs[n], refs[n + 1]):
            cp.start()

    extra = {"collective_id": SIBLING_BARRIER_ID} if sibling_only else {}

    res = _pallas(
        body,
        name=name,
        out_shape=(
            pltpu.SemaphoreType.DMA((n_sems,)),
            pltpu.SemaphoreType.DMA((n_sems,)),
            *[pltpu.HBM(b.shape, b.dtype) for b in bufs],
        ),
        in_specs=[HBM_SPEC] * n,
        out_specs=(SEM_SPEC, SEM_SPEC, *[HBM_SPEC] * n),
        input_output_aliases={i: i + 2 for i in range(n)},
        compiler_params=pltpu.CompilerParams(has_side_effects=EFFECT, **extra),
    )(*[_hbm(b) for b in bufs])
    return res[0], res[1], list(res[2:])


def _split_wait(name, bufs, send_sem, recv_sem, copies, after):
    n = len(bufs)

    def body(*refs):
        for cp in copies(refs[:n], refs[n], refs[n + 1]):
            cp.wait_send()
            cp.wait_recv()

    res = _pallas(
        body,
        name=name,
        out_shape=[pltpu.HBM(b.shape, b.dtype) for b in bufs],
        in_specs=[HBM_SPEC] * n + [SEM_SPEC, SEM_SPEC, ANY],
        out_specs=[HBM_SPEC] * n,
        input_output_aliases={i: i for i in range(n)},
        compiler_params=pltpu.CompilerParams(has_side_effects=EFFECT),
    )(*bufs, send_sem, recv_sem, after)
    return list(res)


def _pair_exchange_copies(kinds):
    n = len(kinds)

    def copies(refs, send_sem, recv_sem):
        x, y, c, _, _ = _place()
        return [
            _remote(_pair_half(refs[w], kinds[w], 1 - c), refs[n + w], send_sem.at[w], recv_sem.at[w], (x, y, 1 - c))
            for w in range(n)
        ]

    return copies


def _pair_share_copies(kinds, waiting):
    def copies(refs, send_sem, recv_sem):
        x, y, c, _, _ = _place()
        out = []
        for w, kind in enumerate(kinds):
            mine = _pair_half(refs[w], kind, c)
            dst = _pair_half(refs[w], kind, 1 - c) if waiting else mine
            out.append(_remote(mine, dst, send_sem.at[w], recv_sem.at[w], (x, y, 1 - c)))
        return out

    return copies


def _piece_shape(half_shape, kind):
    r, c = half_shape
    return (3, r, c // N_CHIPS) if kind == "col" else (3, r // N_CHIPS, c)


def _chip_send_start(name, halves, kinds):
    n = len(halves)
    lands = [lax.empty(_piece_shape(h.shape, k), BF16) for h, k in zip(halves, kinds)]

    def body(*refs):
        h, land = refs[:n], refs[n : 2 * n]
        send_sem, recv_sem = refs[2 * n], refs[2 * n + 1]
        _, _, c, q, chips = _place()
        for i in range(n):
            for r, chip in enumerate(chips):
                piece = _shard_view(h[i], kinds[i], q ^ _REL_MASK[r])
                _remote(piece, land[i].at[r], send_sem.at[3 * i + r], recv_sem.at[3 * i + r], (*chip, c)).start()

    res = _pallas(
        body,
        name=name,
        out_shape=(
            pltpu.SemaphoreType.DMA((3 * n,)),
            pltpu.SemaphoreType.DMA((3 * n,)),
            *[pltpu.HBM(a.shape, a.dtype) for a in halves],
            *[pltpu.HBM(a.shape, a.dtype) for a in lands],
        ),
        in_specs=[HBM_SPEC] * (2 * n),
        out_specs=(SEM_SPEC, SEM_SPEC, *[HBM_SPEC] * (2 * n)),
        input_output_aliases={i: i + 2 for i in range(2 * n)},
        compiler_params=pltpu.CompilerParams(has_side_effects=EFFECT),
    )(*[_hbm(a) for a in halves], *[_hbm(a) for a in lands])
    return res[0], res[1], list(res[2 : 2 + n]), list(res[2 + n :])


def _chip_send_wait(name, halves, lands, kinds, send_sem, recv_sem, after):
    n = len(halves)

    def body(*refs):
        h, land = refs[:n], refs[n : 2 * n]
        s_sem, r_sem = refs[2 * n], refs[2 * n + 1]
        x, y, c, q, _ = _place()
        for i in range(n):
            for r in range(3):
                piece = _shard_view(h[i], kinds[i], q ^ _REL_MASK[r])
                cp = _remote(piece, land[i].at[r], s_sem.at[3 * i + r], r_sem.at[3 * i + r], (x, y, 1 - c))
                cp.wait_send()
                cp.wait_recv()

    res = _pallas(
        body,
        name=name,
        out_shape=[pltpu.HBM(a.shape, a.dtype) for a in halves] + [pltpu.HBM(a.shape, a.dtype) for a in lands],
        in_specs=[HBM_SPEC] * (2 * n) + [SEM_SPEC, SEM_SPEC, ANY],
        out_specs=[HBM_SPEC] * (2 * n),
        input_output_aliases={i: i for i in range(2 * n)},
        compiler_params=pltpu.CompilerParams(has_side_effects=EFFECT),
    )(*halves, *lands, send_sem, recv_sem, after)
    return list(res[:n]), list(res[n:])


def _small_exchange_copies(waiting):
    def copies(refs, send_sem, recv_sem):
        p, land = refs
        x, y, c, q, _ = _place()
        me = 2 * q + c
        out = []
        for dd in range(1, 2 * N_CHIPS):
            dev = (x ^ ((dd >> 2) & 1), y ^ ((dd >> 1) & 1), c ^ (dd & 1))
            dst = land.at[me ^ dd] if waiting else land.at[me]
            out.append(_remote(p, dst, send_sem.at[dd - 1], recv_sem.at[dd - 1], dev))
        return out

    return copies


def _small_sum(name, me_idx, p, land):
    rows = p.shape[0]
    n_dev = 2 * N_CHIPS

    def body(me_ref, p_ref, land_ref, o_ref):
        me = me_ref[0]
        total = None
        for dev in range(n_dev):
            term = jnp.where(me == dev, p_ref[...], land_ref[dev])
            total = term if total is None else total + term
        o_ref[...] = total

    return _pallas(
        body,
        name=name,
        grid_spec=pltpu.PrefetchScalarGridSpec(
            num_scalar_prefetch=1,
            grid=(1,),
            in_specs=[pl.BlockSpec((rows, LANES), lambda i, m: (0, 0)), pl.BlockSpec((n_dev, rows, LANES), lambda i, m: (0, 0, 0))],
            out_specs=pl.BlockSpec((rows, LANES), lambda i, m: (0, 0)),
        ),
        out_shape=jax.ShapeDtypeStruct(p.shape, F32),
    )(me_idx, p, land)


def _pack(parts):
    rows = []
    for a in parts:
        flat = a.reshape(-1).astype(F32)
        n = flat.shape[0]
        padded = -(-n // (8 * LANES)) * (8 * LANES)
        rows.append(jnp.pad(flat, (0, padded - n)).reshape(-1, LANES))
    return jnp.concatenate(rows, axis=0)


def _unpack(packed, shapes):
    out, row = [], 0
    for shp in shapes:
        n = int(np.prod(shp))
        nrows = -(-n // (8 * LANES)) * 8
        out.append(packed[row : row + nrows].reshape(-1)[:n].reshape(shp))
        row += nrows
    return out


def kernel(x, w_in, norm_mix, sgu_v_gain, sgu_w_s, sgu_b_s, w_a_out, attn_sink, rel_bias, w_b_out, w_o, norm_ffn, w_gate, w_up, w_down, norm_final, loss_target, m_w_in, m_norm_mix, m_sgu_v_gain, m_sgu_w_s, m_sgu_b_s, m_w_a_out, m_attn_sink, m_rel_bias, m_w_b_out, m_w_o, m_norm_ffn, m_w_gate, m_w_up, m_w_down, m_norm_final, v_w_in, v_norm_mix, v_sgu_v_gain, v_sgu_w_s, v_sgu_b_s, v_w_a_out, v_attn_sink, v_rel_bias, v_w_b_out, v_w_o, v_norm_ffn, v_w_gate, v_w_up, v_w_down, v_norm_final):
    s, d = x.shape[1], x.shape[2]
    w_sgu = sgu_v_gain.shape[1]
    groups = sgu_w_s.shape[1]
    heads = attn_sink.shape[1]
    grp = heads // N_KV_HEADS
    w_att = heads * HEAD_DIM
    w_kv = N_KV_HEADS * HEAD_DIM
    d_ff = w_gate.shape[2] * N_CHIPS
    n_in = w_in.shape[2] * N_CHIPS
    off_q = 2 * w_sgu
    off_k = off_q + w_att
    off_g = off_k + 2 * w_kv
    assert n_in == off_g + 2 * d and groups * BLK == w_sgu and s % BLK == 0

    x2d = x.reshape(s, d)
    tgt = loss_target.reshape(s, d)
    c_idx = lax.axis_index("c").astype(I32).reshape(1)
    q_idx = (2 * lax.axis_index("x") + lax.axis_index("y")).astype(I32).reshape(1)
    qc_idx = jnp.concatenate([q_idx, c_idx])

    W_IN, W_A, W_B, W_O, W_GATE, W_UP, W_DOWN = range(7)
    names = ["w_in", "w_a", "w_b", "w_o", "w_gate", "w_up", "w_down"]
    kinds = ["col", "col", "col", "row", "col", "col", "row"]
    big_w = [w_in[0], w_a_out[0], w_b_out[0], w_o[0], w_gate[0], w_up[0], w_down[0]]
    big_m = [m_w_in[0], m_w_a_out[0], m_w_b_out[0], m_w_o[0], m_w_gate[0], m_w_up[0], m_w_down[0]]
    big_v = [v_w_in[0], v_w_a_out[0], v_w_b_out[0], v_w_o[0], v_w_gate[0], v_w_up[0], v_w_down[0]]
    full_in = _cast_into_full("cast_w_in", q_idx, big_w[W_IN], kinds[W_IN])
    in_send, in_recv, (full_in,), token = _gather_start("gather_start_in", [full_in], [kinds[W_IN]], rels=(0, 1))
    rest = [_cast_into_full("cast_" + names[i], q_idx, big_w[i], kinds[i], after=(token,)) for i in range(1, 7)]

    ws_b = sgu_w_s[0].astype(BF16)
    wst_b = jnp.swapaxes(sgu_w_s[0], 1, 2).astype(BF16)
    b_col = sgu_b_s[0].reshape(groups, BLK, 1)
    bias_tab, bucket = _band_tables(rel_bias)
    sink = attn_sink[0]
    small_w = [norm_mix, sgu_v_gain, sgu_w_s, sgu_b_s, attn_sink, rel_bias, norm_ffn, norm_final]
    small_m = [m_norm_mix, m_sgu_v_gain, m_sgu_w_s, m_sgu_b_s, m_attn_sink, m_rel_bias, m_norm_ffn, m_norm_final]
    small_v = [v_norm_mix, v_sgu_v_gain, v_sgu_w_s, v_sgu_b_s, v_attn_sink, v_rel_bias, v_norm_ffn, v_norm_final]
    small_shapes = [w.shape for w in small_w]
    zero1 = jnp.zeros((1,), F32)
    pw, pm, pv = _pack(small_w + [zero1]), _pack(small_m + [zero1]), _pack(small_v + [zero1])
    h1 = _rms_fwd("rms_mix", x2d, norm_mix, after=(token, rest[-1], ws_b, wst_b, b_col, bias_tab, pw, pm, pv))
    (full_in,) = _gather_wait("gather_wait_in", [full_in], [kinds[W_IN]], [0], in_send, in_recv, h1, rels=(0, 1))
    relay_send, relay_recv, (full_in,) = _split_start(
        "gather_relay_in", [full_in], 4, _relay_and_forward_copies([kinds[W_IN]], False)
    )
    full_down = rest.pop()
    full_up = rest.pop()
    ag_send, ag_recv, rest, token = _gather_start("gather_start_rest", rest, kinds[1:5], rels=(0, 1), after=(full_in, full_up))
    (full_in,) = _split_wait(
        "gather_relay_wait_in", [full_in], relay_send, relay_recv, _relay_and_forward_copies([kinds[W_IN]], True), token
    )
    (g_in,) = _gather_forward("gather_fwd_in", [full_in], [kinds[W_IN]], rels=(2,))
    fulls = [g_in] + rest

    def relay_begin(tag, ids, after, source=None):
        ks = [kinds[i] for i in ids]
        send, recv, bufs, w_ids = source or (ag_send, ag_recv, [fulls[i] for i in ids], [i - 1 for i in ids])
        bufs = _gather_wait("gather_wait_" + tag, bufs, ks, w_ids, send, recv, after, rels=(0, 1))
        send, recv, bufs = _split_start("gather_relay_" + tag, bufs, 4 * len(ids), _relay_and_forward_copies(ks, False))
        return tag, ks, send, recv, bufs

    def relay_end(state, after):
        tag, ks, send, recv, bufs = state
        bufs = _split_wait("gather_relay_wait_" + tag, bufs, send, recv, _relay_and_forward_copies(ks, True), after)
        return _gather_forward("gather_fwd_" + tag, bufs, ks, rels=(2,))

    def relay_end_async(state, after):
        tag, ks, send, recv, bufs = state
        bufs = _split_wait("gather_relay_wait_" + tag, bufs, send, recv, _relay_and_forward_copies(ks, True), after)
        send, recv, bufs = _split_start(
            "gather_fwd_start_" + tag, bufs, len(ks), _forward_copies(ks, False, rels=(2,)), sibling_only=True
        )
        return tag, ks, send, recv, bufs

    def forwarded(state, after):
        tag, ks, send, recv, bufs = state
        return _split_wait("gather_fwd_wait_" + tag, bufs, send, recv, _forward_copies(ks, True, rels=(2,)), after)

    tm = _tile(s, (1024, 512, 256, 128))

    tn = _tile(n_in, (768, 640, 512))
    z = _mm(
        "mm_z", (s // tm, n_in // tn, 1), [h1, g_in],
        [pl.BlockSpec((tm, d), lambda i, j, k: (i, 0)), pl.BlockSpec((d, tn), lambda i, j, k: (0, j))],
        [jax.ShapeDtypeStruct((s, n_in), BF16)], [pl.BlockSpec((tm, tn), lambda i, j, k: (i, j))],
        [(0, 1, NN, 0)], 1, (tm, tn), 1, lambda ins, vals, outs, cs: _put(outs[0], cs, vals[0]),
        _mm_vmem([((tm, d), BF16, 2), ((d, tn), BF16, 2), ((tm, tn), F32, 3)]),
    )[0]
    mix_relay = relay_begin("mix", [W_A, W_B, W_O], z)
    up_send, up_recv, (full_up,), token = _gather_start(
        "gather_start_up", [full_up], [kinds[W_UP]], rels=(0, 1), after=(mix_relay[4][0],)
    )

    a_act = _sgu_fwd(z, sgu_v_gain, ws_b, b_col, w_sgu, after=(token,))

    kv_b = z[:, off_k:off_g]
    k_pad = jnp.pad(kv_b[:, :w_kv], ((BLK, BLK), (0, 0)))
    v_pad = jnp.pad(kv_b[:, w_kv:], ((BLK, BLK), (0, 0)))
    q_blk0 = off_q // (grp * HEAD_DIM)
    att = _attn_fwd(sink, z, k_pad, v_pad, bias_tab, grp, q_blk0)

    g_a, g_b, g_o = relay_end(mix_relay, att)
    gate_relay = relay_begin("gate", [W_GATE], g_a)

    tg = _tile(d, (512,))
    ga0, gb0 = off_g // tg, (off_g + d) // tg

    def ep_gate(ins, vals, outs, cs):
        sa, sb = _sigmoid(ins[4][:, cs].astype(F32)), _sigmoid(ins[5][:, cs].astype(F32))
        _put(outs[0], cs, sa * vals[0] + sb * vals[1])
        _put(outs[1], cs, vals[0])
        _put(outs[2], cs, vals[1])

    t_out = pl.BlockSpec((tm, tg), lambda i, j, k: (i, j))
    m_act, y_a, y_b = _mm(
        "mm_branches", (s // tm, d // tg, 1), [a_act, g_a, att, g_b, z, z],
        [pl.BlockSpec((tm, w_sgu), lambda i, j, k: (i, 0)), pl.BlockSpec((w_sgu, tg), lambda i, j, k: (0, j)),
         pl.BlockSpec((tm, w_att), lambda i, j, k: (i, 0)), pl.BlockSpec((w_att, tg), lambda i, j, k: (0, j)),
         pl.BlockSpec((tm, tg), lambda i, j, k: (i, ga0 + j)), pl.BlockSpec((tm, tg), lambda i, j, k: (i, gb0 + j))],
        [jax.ShapeDtypeStruct((s, d), BF16)] * 3,
        [t_out, t_out, t_out], [(0, 1, NN, 0), (2, 3, NN, 1)], 2, (tm, tg), 1, ep_gate,
        _mm_vmem([((tm, w_sgu), BF16, 4), ((w_sgu, tg), BF16, 4), ((tm, tg), F32, 12)]),    )

    tn = _tile(d, (1024, 512))

    def ep_residual(ins, vals, outs, cs):
        _put(outs[0], cs, ins[2][:, cs] + vals[0])

    up_relay = relay_begin("up", [W_UP], m_act, source=(up_send, up_recv, [full_up], [0]))
    down_send, down_recv, (full_down,), token = _gather_start(
        "gather_start_down", [full_down], [kinds[W_DOWN]], after=(up_relay[4][0],)
    )
    x2 = _mm(
        "mm_wo", (s // tm, d // tn, 1), [m_act, g_o, x2d],
        [pl.BlockSpec((tm, d), lambda i, j, k: (i, 0)), pl.BlockSpec((d, tn), lambda i, j, k: (0, j)),
         pl.BlockSpec((tm, tn), lambda i, j, k: (i, j))],
        [jax.ShapeDtypeStruct((s, d), F32)], [pl.BlockSpec((tm, tn), lambda i, j, k: (i, j))],
        [(0, 1, NN, 0)], 1, (tm, tn), 1, ep_residual,
        _mm_vmem([((tm, d), BF16, 2), ((d, tn), BF16, 2), ((tm, tn), F32, 5)]),
        after=(token,),
    )[0]
    gate_fwd = relay_end_async(gate_relay, x2)
    up_fwd = relay_end_async(up_relay, gate_fwd[4][0])
    h2 = _rms_fwd("rms_ffn", x2, norm_ffn, after=(up_fwd[4][0],))
    (g_gate,) = forwarded(gate_fwd, h2)
    (g_up,) = forwarded(up_fwd, g_gate)

    tf = _tile(d_ff, (512,))

    def ep_swiglu(ins, vals, outs, cs):
        gt, up = vals
        _put(outs[0], cs, gt)
        _put(outs[1], cs, up)
        _put(outs[2], cs, (gt * _sigmoid(gt)) * up)

    f_out = pl.BlockSpec((tm, tf), lambda i, j, k: (i, j))
    gt, up, f_act = _mm(
        "mm_gate_up", (s // tm, d_ff // tf, 1), [h2, g_gate, g_up],
        [pl.BlockSpec((tm, d), lambda i, j, k: (i, 0)), pl.BlockSpec((d, tf), lambda i, j, k: (0, j)),
         pl.BlockSpec((d, tf), lambda i, j, k: (0, j))],
        [jax.ShapeDtypeStruct((s, d_ff), BF16)] * 3,
        [f_out, f_out, f_out], [(0, 1, NN, 0), (0, 2, NN, 1)], 2, (tm, tf), 1, ep_swiglu,
        _mm_vmem([((tm, d), BF16, 2), ((d, tf), BF16, 4), ((tm, tf), F32, 8)]),    )
    (g_down,) = _gather_forward(
        "gather_fwd_ffn_out",
        _gather_wait("gather_wait_ffn_out", [full_down], [kinds[W_DOWN]], [0], down_send, down_recv, f_act),
        [kinds[W_DOWN]],
    )

    tkf = _tile(d_ff, (1408, 1024, 512))
    tml, tnl = _tile(s, (512, 256, 128)), _tile(d, (512,))
    x3 = _mm(
        "mm_down", (s // tml, d // tnl, 1), [f_act, g_down, x2],
        [pl.BlockSpec((tml, d_ff), lambda i, j, k: (i, 0)), pl.BlockSpec((d_ff, tnl), lambda i, j, k: (0, j)),
         pl.BlockSpec((tml, tnl), lambda i, j, k: (i, j))],
        [jax.ShapeDtypeStruct((s, d), F32)], [pl.BlockSpec((tml, tnl), lambda i, j, k: (i, j))],
        [(0, 1, NN, 0)], 1, (tml, tnl), 1, ep_residual,
        _mm_vmem([((tml, d_ff), BF16, 2), ((d_ff, tnl), BF16, 2), ((tml, tnl), F32, 6)]),    )[0]

    dx3, dx3b, dg_final, loss_part = _head(x3, norm_final.reshape(1, d), tgt)

    def reduce_a(tag, ids, grads):
        ks = [kinds[i] for i in ids]
        lands = [lax.empty((g.shape[0] // 2, g.shape[1]) if k == "col" else (g.shape[0], g.shape[1] // 2), BF16)
                 for g, k in zip(grads, ks)]
        send, recv, bufs = _split_start("pair_send_" + tag, list(grads) + lands, len(ids), _pair_exchange_copies(ks), sibling_only=True)
        return {"tag": tag, "ids": ids, "ks": ks, "pair": (send, recv, bufs), "token": bufs[0]}

    def reduce_b(st, after):
        tag, ids, ks = st["tag"], st["ids"], st["ks"]
        send, recv, bufs = st["pair"]
        bufs = _split_wait("pair_wait_" + tag, bufs, send, recv, _pair_exchange_copies(ks), after)
        grads, from_sib = bufs[: len(ids)], bufs[len(ids) :]
        halves = [_pair_add("pair_add_" + names[i], c_idx, g, r, k) for i, g, r, k in zip(ids, grads, from_sib, ks)]
        st["chip"] = _chip_send_start("chip_send_" + tag, halves, ks)
        st["token"] = st["chip"][2][0]

    def reduce_c(st, after):
        tag, ids, ks = st["tag"], st["ids"], st["ks"]
        send, recv, halves, lands = st["chip"]
        halves, lands = _chip_send_wait("chip_wait_" + tag, halves, lands, ks, send, recv, after)
        pieces = [_chip_sum("chip_sum_" + names[i], qc_idx, h, r, k) for i, h, r, k in zip(ids, halves, lands, ks)]
        st["share"] = _split_start("share_send_" + tag, pieces, len(ids), _pair_share_copies(ks, False), sibling_only=True)
        st["token"] = st["share"][2][0]

    def reduce_d(st, after):
        send, recv, bufs = st["share"]
        return _split_wait("share_wait_" + st["tag"], bufs, send, recv, _pair_share_copies(st["ks"], True), after)

    grads_big, upd = [None] * 7, [None] * 7

    def update(i, g, after=()):
        upd[i] = _adamw("adamw_" + names[i], big_w[i], g, big_m[i], big_v[i], after=after)
        grads_big[i] = upd[i][3]
        return upd[i][0]

    def finish(st, after):
        shared = reduce_d(st, after)
        after = shared[0]
        for i, g in zip(st["ids"], shared):
            after = update(i, g, (after,))
        return after

    def ep_swiglu_bwd(ins, vals, outs, cs):
        df = vals[0]
        gtv, upv = ins[2][:, cs].astype(F32), ins[3][:, cs].astype(F32)
        sg = _sigmoid(gtv)
        _put(outs[0], cs, df * upv * (sg + gtv * sg * (1.0 - sg)))
        _put(outs[1], cs, df * (gtv * sg))

    dgt, dup = _mm(
        "mm_dswiglu", (s // tm, d_ff // tf, 1), [dx3b, g_down, gt, up],
        [pl.BlockSpec((tm, d), lambda i, j, k: (i, 0)), pl.BlockSpec((tf, d), lambda i, j, k: (j, 0)), f_out, f_out],
        [jax.ShapeDtypeStruct((s, d_ff), BF16), jax.ShapeDtypeStruct((s, d_ff), BF16)], [f_out, f_out],
        [(0, 1, NT, 0)], 1, (tm, tf), 1, ep_swiglu_bwd,
        _mm_vmem([((tm, d), BF16, 2), ((tf, d), BF16, 2), ((tm, tf), F32, 8)]),    )

    def ep_store(ins, vals, outs, cs):
        for o, v in zip(outs, vals):
            _put(o, cs, v)

    twn = _tile(d, (1024, 512))
    gw_down = _mm(
        "mm_gw_down", (d_ff // tkf, d // twn, 1), [f_act, dx3b],
        [pl.BlockSpec((s, tkf), lambda i, j, k: (0, i)), pl.BlockSpec((s, twn), lambda i, j, k: (0, j))],
        [jax.ShapeDtypeStruct((d_ff, d), BF16)], [pl.BlockSpec((tkf, twn), lambda i, j, k: (i, j))],
        [(0, 1, TN, 0)], 1, (tkf, twn), 1, ep_store,
        _mm_vmem([((s, tkf), BF16, 3), ((s, twn), BF16, 2), ((tkf, twn), F32, 3)]),
    )[0]
    red_down = reduce_a("down", [W_DOWN], [gw_down])

    tn2 = _tile(d, (256,))
    dh2_specs = [pl.BlockSpec((tm, d_ff), lambda i, j, k: (i, 0)), pl.BlockSpec((tn2, d_ff), lambda i, j, k: (j, 0))]
    dh2_tile = pl.BlockSpec((tm, tn2), lambda i, j, k: (i, j))
    dh2_vmem = _mm_vmem([((tm, d_ff), BF16, 2), ((tn2, d_ff), BF16, 2), ((tm, tn2), F32, 7)])
    dh2 = _mm(
        "mm_dh2_gate", (s // tm, d // tn2, 1), [dgt, g_gate], dh2_specs,
        [jax.ShapeDtypeStruct((s, d), F32)], [dh2_tile], [(0, 1, NT, 0)], 1, (tm, tn2), 1, ep_store, dh2_vmem,
        after=(red_down["token"],),
    )[0]
    dh2 = _mm(
        "mm_dh2_up", (s // tm, d // tn2, 1), [dup, g_up, dh2], dh2_specs + [dh2_tile],
        [jax.ShapeDtypeStruct((s, d), F32)], [dh2_tile], [(0, 1, NT, 0)], 1, (tm, tn2), 1, ep_residual, dh2_vmem,
    )[0]
    reduce_b(red_down, dh2)

    twr = _tile(d, (1024, 512))
    w_tile = pl.BlockSpec((twr, tf), lambda i, j, k: (i, j))
    gw_gate, gw_up = _mm(
        "mm_gw_gate_up", (d // twr, d_ff // tf, 1), [h2, dgt, dup],
        [pl.BlockSpec((s, twr), lambda i, j, k: (0, i)), pl.BlockSpec((s, tf), lambda i, j, k: (0, j)),
         pl.BlockSpec((s, tf), lambda i, j, k: (0, j))],
        [jax.ShapeDtypeStruct((d, d_ff), BF16), jax.ShapeDtypeStruct((d, d_ff), BF16)], [w_tile, w_tile],
        [(0, 1, TN, 0), (0, 2, TN, 1)], 2, (twr, tf), 1, ep_store,
        _mm_vmem([((s, twr), BF16, 3), ((s, tf), BF16, 4), ((twr, tf), F32, 6)]),
        after=(red_down["token"],),
    )
    red_ffn = reduce_a("ffn_in", [W_GATE, W_UP], [gw_gate, gw_up])

    dx2, dx2b, dg_ffn = _rms_bwd("rms_ffn_bwd", x2, norm_ffn, dh2, dx3, after=(red_ffn["token"],))

    nj = d // tg

    def lo(j):
        return jnp.minimum(j, nj - 1)

    def gate_bwd_body(dx_ref, wo_ref, ga_ref, gb_ref, ya_ref, yb_ref, dya_ref, dyb_ref, dz_ref, keep):
        j = pl.program_id(1)

        @pl.when(j < nj)
        def _():
            dm = lax.dot_general(dx_ref[...], wo_ref[...], NT, preferred_element_type=F32)
            sa, sb = _sigmoid(ga_ref[...].astype(F32)), _sigmoid(gb_ref[...].astype(F32))
            dya_ref[...] = (dm * sa).astype(BF16)
            dyb_ref[...] = (dm * sb).astype(BF16)
            dz_ref[...] = (dm * ya_ref[...].astype(F32) * (sa * (1.0 - sa))).astype(BF16)
            keep[lo(j)] = (dm * yb_ref[...].astype(F32) * (sb * (1.0 - sb))).astype(BF16)

        @pl.when(j >= nj)
        def _():
            dz_ref[...] = keep[jnp.maximum(j - nj, 0)]

    t_lo = pl.BlockSpec((tm, tg), lambda i, j: (i, lo(j)))
    dya, dyb, dz = _pallas(
        gate_bwd_body,
        name="mm_dgate",
        grid=(s // tm, 2 * nj),
        in_specs=[
            pl.BlockSpec((tm, d), lambda i, j: (i, 0)),
            pl.BlockSpec((tg, d), lambda i, j: (lo(j), 0)),
            pl.BlockSpec((tm, tg), lambda i, j: (i, ga0 + lo(j))),
            pl.BlockSpec((tm, tg), lambda i, j: (i, gb0 + lo(j))),
            t_lo,
            t_lo,
        ],
        out_specs=[t_lo, t_lo, pl.BlockSpec((tm, tg), lambda i, j: (i, ga0 + j))],
        out_shape=[jax.ShapeDtypeStruct((s, d), BF16), jax.ShapeDtypeStruct((s, d), BF16), jax.ShapeDtypeStruct((s, n_in), BF16)],
        scratch_shapes=[pltpu.VMEM((nj, tm, tg), BF16)],
        compiler_params=_params(_mm_vmem([((tm, d), BF16, 2), ((tg, d), BF16, 2), ((tm, tg), F32, 14), ((nj, tm, tg), BF16, 1)])),
    )(dx2b, g_o, z, z, y_a, y_b)
    reduce_b(red_ffn, dya)
    reduce_c(red_down, red_ffn["token"])

    gw_o = _mm(
        "mm_gw_o", (d // twr, d // twn, 1), [m_act, dx2b],
        [pl.BlockSpec((s, twr), lambda i, j, k: (0, i)), pl.BlockSpec((s, twn), lambda i, j, k: (0, j))],
        [jax.ShapeDtypeStruct((d, d), BF16)], [pl.BlockSpec((twr, twn), lambda i, j, k: (i, j))],
        [(0, 1, TN, 0)], 1, (twr, twn), 1, ep_store,
        _mm_vmem([((s, twr), BF16, 3), ((s, twn), BF16, 2), ((twr, twn), F32, 3)]),
        after=(red_down["token"],),
    )[0]
    after_down = finish(red_down, gw_o)

    tb = _tile(w_sgu, (1024, 512))
    b_out = pl.BlockSpec((tm, tb), lambda i, j, k: (i, j))

    da, datt = _mm(
        "mm_dbranches", (s // tm, w_sgu // tb, 1), [dya, g_a, dyb, g_b],
        [pl.BlockSpec((tm, d), lambda i, j, k: (i, 0)), pl.BlockSpec((tb, d), lambda i, j, k: (j, 0)),
         pl.BlockSpec((tm, d), lambda i, j, k: (i, 0)), pl.BlockSpec((tb, d), lambda i, j, k: (j, 0))],
        [jax.ShapeDtypeStruct((s, w_sgu), BF16), jax.ShapeDtypeStruct((s, w_att), BF16)], [b_out, b_out],
        [(0, 1, NT, 0), (2, 3, NT, 1)], 2, (tm, tb), 1, ep_store,
        _mm_vmem([((tm, d), BF16, 4), ((tb, d), BF16, 4), ((tm, tb), F32, 6)]),
        after=(after_down,),
    )

    wb_tile = pl.BlockSpec((tb, twn), lambda i, j, k: (i, j))
    gw_a, gw_b = _mm(
        "mm_gw_branches", (w_sgu // tb, d // twn, 1), [a_act, dya, att, dyb],
        [pl.BlockSpec((s, tb), lambda i, j, k: (0, i)), pl.BlockSpec((s, twn), lambda i, j, k: (0, j)),
         pl.BlockSpec((s, tb), lambda i, j, k: (0, i)), pl.BlockSpec((s, twn), lambda i, j, k: (0, j))],
        [jax.ShapeDtypeStruct((w_sgu, d), BF16), jax.ShapeDtypeStruct((w_att, d), BF16)], [wb_tile, wb_tile],
        [(0, 1, TN, 0), (2, 3, TN, 1)], 2, (tb, twn), 1, ep_store,
        _mm_vmem([((s, tb), BF16, 5), ((s, twn), BF16, 4), ((tb, twn), F32, 6)]),
        after=(da,),
    )
    red_mix = reduce_a("mix", [W_O, W_A, W_B], [gw_o, gw_a, gw_b])

    dz, dws, dbs, dgain = _sgu_bwd(z, da, sgu_v_gain, ws_b, wst_b, b_col, w_sgu, dz, after=(red_mix["token"],))
    dz, dk_pad, dv_pad, dbias_tab, dsink = _attn_bwd(sink, z, k_pad, v_pad, bias_tab, datt, dz, grp, q_blk0)
    dz = _dkv_to_dz(dk_pad, dv_pad, dz, off_k // (2 * w_kv))
    drel = _relbias_bwd(dbias_tab, bucket)
    reduce_b(red_mix, dz)
    reduce_c(red_ffn, red_mix["token"])

    early = [dgain, dws, dbs, dsink[:, 0], drel[:, :REL_BUCKETS].T, dg_ffn, dg_final]
    p_early = _pack([g.reshape(shp) for g, shp in zip(early, small_shapes[1:])] + [loss_part[0, :1]])
    land = jnp.zeros((2 * N_CHIPS,) + p_early.shape, F32)
    sm_send, sm_recv, (p_early, land) = _split_start("small_send", [p_early, land], 2 * N_CHIPS - 1, _small_exchange_copies(False))

    tzn = _tile(n_in, (768, 640, 512))
    gw_in = _mm(
        "mm_gw_in", (d // twr, n_in // tzn, 1), [h1, dz],
        [pl.BlockSpec((s, twr), lambda i, j, k: (0, i)), pl.BlockSpec((s, tzn), lambda i, j, k: (0, j))],
        [jax.ShapeDtypeStruct((d, n_in), BF16)], [pl.BlockSpec((twr, tzn), lambda i, j, k: (i, j))],
        [(0, 1, TN, 0)], 1, (twr, tzn), 1, ep_store,
        _mm_vmem([((s, twr), BF16, 3), ((s, tzn), BF16, 2), ((twr, tzn), F32, 3)]),
        after=(red_ffn["token"], p_early),
    )[0]
    red_in = reduce_a("w_in", [W_IN], [gw_in])

    g_gate, g_up = reduce_d(red_ffn, red_in["token"])
    reduce_b(red_in, update(W_GATE, g_gate))

    dh1 = _mm(
        "mm_dh1", (s // tm, d // tn2, 1), [dz, g_in],
        [pl.BlockSpec((tm, n_in), lambda i, j, k: (i, 0)), pl.BlockSpec((tn2, n_in), lambda i, j, k: (j, 0))],
        [jax.ShapeDtypeStruct((s, d), F32)], [pl.BlockSpec((tm, tn2), lambda i, j, k: (i, j))],
        [(0, 1, NT, 0)], 1, (tm, tn2), 1, ep_store,
        _mm_vmem([((tm, n_in), BF16, 2), ((tn2, n_in), BF16, 2), ((tm, tn2), F32, 5)]),
        after=(red_in["token"],),
    )[0]

    reduce_c(red_mix, dh1)
    grad_x, _, dg_mix = _rms_bwd("rms_mix_bwd", x2d, norm_mix, dh1, dx2, after=(red_mix["token"],))

    p_mix = _pack([dg_mix.reshape(small_shapes[0])])
    land_mix = jnp.zeros((2 * N_CHIPS,) + p_mix.shape, F32)
    mx_send, mx_recv, (p_mix, land_mix) = _split_start("mix_send", [p_mix, land_mix], 2 * N_CHIPS - 1, _small_exchange_copies(False))

    reduce_c(red_in, update(W_UP, g_up, (finish(red_mix, p_mix),)))
    p_early, land = _split_wait("small_wait", [p_early, land], sm_send, sm_recv, _small_exchange_copies(True), red_in["token"])
    p_mix, land_mix = _split_wait("mix_wait", [p_mix, land_mix], mx_send, mx_recv, _small_exchange_copies(True), p_early)
    me_idx = 2 * q_idx + c_idx
    packed_g = jnp.concatenate([_small_sum("mix_sum", me_idx, p_mix, land_mix), _small_sum("small_sum", me_idx, p_early, land)], axis=0)
    g_small = _unpack(packed_g, small_shapes + [(1,)])
    loss = g_small[-1].reshape(())
    g_small = g_small[:-1]
    pg = _pack(g_small + [zero1])
    small_upd = _adamw("adamw_small", pw, pg, pm, pv)
    d_small, nm_small, nv_small = [_unpack(a, small_shapes) for a in small_upd[:3]]
    finish(red_in, small_upd[0])

    small_names = ["norm_mix", "sgu_v_gain", "sgu_w_s", "sgu_b_s", "attn_sink", "rel_bias", "norm_ffn", "norm_final"]
    table = {}
    for i, n in enumerate(names):
        table[n] = (grads_big[i][None], upd[i][0][None], upd[i][1][None], upd[i][2][None])
    for i, n in enumerate(small_names):
        table[n] = (g_small[i], d_small[i], nm_small[i], nv_small[i])
    order = ["w_in", "norm_mix", "sgu_v_gain", "sgu_w_s", "sgu_b_s", "w_a", "attn_sink", "rel_bias", "w_b", "w_o", "norm_ffn",
             "w_gate", "w_up", "w_down", "norm_final"]
    outs = [loss, grad_x.reshape(1, s, d)]
    for part in range(4):
        outs += [table[n][part] for n in order]
    return tuple(outs)
```

```python
import math

import jax
import jax.numpy as jnp
import numpy as np
from jax import lax
from jax.experimental import pallas as pl
from jax.experimental.pallas import tpu as pltpu

F32 = jnp.float32
BF16 = jnp.bfloat16
I32 = jnp.int32
MESH = pl.DeviceIdType.MESH

EPS = 1e-6
NEG = -1e30
BLK = 128
HEAD_DIM = 128
N_KV_HEADS = 2
REL_BUCKETS = 32
REL_MAX_DIST = 128
N_CHIPS = 4
ADAM_LR, ADAM_B1, ADAM_B2, ADAM_EPS, ADAM_WD, ADAM_STEP = 0.001, 0.9, 0.999, 1e-08, 0.01, 10

LANES = 128
VMEM_CAP = 60 * 1024 * 1024

NN = (((1,), (0,)), ((), ()))
NT = (((1,), (1,)), ((), ()))
TN = (((0,), (0,)), ((), ()))
ANY = pl.BlockSpec(memory_space=pl.ANY)
HBM_SPEC = pl.BlockSpec(memory_space=pltpu.HBM)
SEM_SPEC = pl.BlockSpec(memory_space=pltpu.SEMAPHORE)
EFFECT = pltpu.SideEffectType.DATAFLOW_SIDE_EFFECTING


def _tile(n, cands):
    for t in cands:
        if n % t == 0:
            return t
    return n


PIN_BYTES = 64 * 1024


def _pin_hbm(a):
    big = hasattr(a, "dtype") and jnp.issubdtype(a.dtype, jnp.floating) and _nbytes(a.shape, a.dtype) >= PIN_BYTES
    return pltpu.with_memory_space_constraint(a, pltpu.HBM) if big else a


def _pallas(body, *, out_shape, **kw):
    def pin(o):
        big = isinstance(o, jax.ShapeDtypeStruct) and jnp.issubdtype(o.dtype, jnp.floating) and _nbytes(o.shape, o.dtype) >= PIN_BYTES
        return pltpu.HBM(o.shape, o.dtype) if big else o

    shapes = type(out_shape)(pin(o) for o in out_shape) if isinstance(out_shape, (list, tuple)) else pin(out_shape)
    call = pl.pallas_call(body, out_shape=shapes, **kw)
    return lambda *args: call(*[_pin_hbm(a) for a in args])


def _params(vmem_bytes=None, **kw):
    if vmem_bytes is not None:
        kw["vmem_limit_bytes"] = int(min(max(vmem_bytes, 32 * 1024 * 1024), VMEM_CAP))
    return pltpu.CompilerParams(**kw)


def _nbytes(shape, dtype):
    return int(np.prod(shape)) * jnp.dtype(dtype).itemsize


def _sigmoid(x):
    return 1.0 / (1.0 + jnp.exp(-x))


_GC = 0.7978845608028654
_GA = 0.044715


def _gelu(x):
    return 0.5 * x * (1.0 + jnp.tanh(_GC * (x + _GA * (x * x * x))))


def _gelu_grad(x):
    t = jnp.tanh(_GC * (x + _GA * (x * x * x)))
    return 0.5 * (1.0 + t) + 0.5 * x * (1.0 - t * t) * (_GC * (1.0 + 3.0 * _GA * (x * x)))


def _bf(v):
    return v if v.dtype == BF16 else v.astype(BF16)


def _mm(name, grid, ins, in_specs, out_shape, out_specs, pairs, n_acc, tile, nk, epilogue, vmem_bytes, after=()):
    assert nk == 1
    n_in, n_out = len(ins) + len(after), len(out_shape)

    def body(*refs):
        in_refs, out_refs = refs[:n_in], refs[n_in : n_in + n_out]
        vals = [None] * n_acc
        for a_i, b_i, dn, acc_i in pairs:
            d = lax.dot_general(_bf(in_refs[a_i][...]), _bf(in_refs[b_i][...]), dn, preferred_element_type=F32)
            vals[acc_i] = d if vals[acc_i] is None else vals[acc_i] + d
        epilogue(in_refs, vals, out_refs, slice(None))

    return _pallas(
        body,
        name=name,
        grid=grid,
        in_specs=list(in_specs) + [ANY] * len(after),
        out_specs=out_specs,
        out_shape=out_shape,
        compiler_params=_params(vmem_bytes),
    )(*ins, *after)


def _put(ref, cs, v):
    ref[:, cs] = v.astype(ref.dtype)


def _mm_vmem(tiles):
    return sum(_nbytes(s, d) * c for s, d, c in tiles) + 4 * 1024 * 1024


def _rows8(v):
    r, d = v.shape
    return v.reshape(r // 8, 8, d).sum(axis=0)


def _rms_fwd(name, x, g, after=()):
    s, d = x.shape
    tm = _tile(s, (256, 128))

    def body(x_ref, g_ref, *rest):
        h_ref = rest[-1]
        xv = x_ref[...]
        r = lax.rsqrt(jnp.mean(xv * xv, axis=-1, keepdims=True) + EPS)
        h_ref[...] = ((xv * r) * g_ref[...]).astype(BF16)

    return _pallas(
        body,
        name=name,
        grid=(s // tm,),
        in_specs=[pl.BlockSpec((tm, d), lambda i: (i, 0)), pl.BlockSpec((1, d), lambda i: (0, 0))] + [ANY] * len(after),
        out_specs=pl.BlockSpec((tm, d), lambda i: (i, 0)),
        out_shape=jax.ShapeDtypeStruct((s, d), BF16),
    )(x, g, *after)


def _rms_bwd(name, x, g, dh, dres, after=()):
    s, d = x.shape
    tm = _tile(s, (256, 128))
    n = s // tm
    n_after = len(after)

    def body(x_ref, g_ref, dh_ref, dres_ref, *rest):
        dx_ref, dxb_ref, dg_ref, acc_ref = rest[n_after:]
        i = pl.program_id(0)
        xv = x_ref[...]
        r = lax.rsqrt(jnp.mean(xv * xv, axis=-1, keepdims=True) + EPS)
        xh = xv * r
        dhv = dh_ref[...]
        dxh = dhv * g_ref[...]
        dx = r * (dxh - xh * jnp.mean(dxh * xh, axis=-1, keepdims=True)) + dres_ref[...]
        dx_ref[...] = dx
        dxb_ref[...] = dx.astype(BF16)
        part = _rows8(dhv * xh)

        @pl.when(i == 0)
        def _():
            acc_ref[...] = part

        @pl.when(i > 0)
        def _():
            acc_ref[...] += part

        @pl.when(i == n - 1)
        def _():
            dg_ref[...] = jnp.sum(acc_ref[...], axis=0, keepdims=True)

    row = pl.BlockSpec((tm, d), lambda i: (i, 0))
    vec = pl.BlockSpec((1, d), lambda i: (0, 0))
    return _pallas(
        body,
        name=name,
        grid=(n,),
        in_specs=[row, vec, row, row] + [ANY] * n_after,
        out_specs=[row, row, vec],
        out_shape=[jax.ShapeDtypeStruct((s, d), F32), jax.ShapeDtypeStruct((s, d), BF16), jax.ShapeDtypeStruct((1, d), F32)],
        scratch_shapes=[pltpu.VMEM((8, d), F32)],
    )(x, g, dh, dres, *after)


def _head(x3, g, target):
    s, d = x3.shape
    tm = _tile(s, (256, 128))
    n = s // tm

    def body(x_ref, g_ref, t_ref, dx_ref, dxb_ref, dg_ref, loss_ref, acc_g, acc_l):
        i = pl.program_id(0)
        xv = x_ref[...]
        gv = g_ref[...]
        r = lax.rsqrt(jnp.mean(xv * xv, axis=-1, keepdims=True) + EPS)
        xh = xv * r
        e = xh * gv - t_ref[...]
        dy = e * (1.0 / d)
        dxh = dy * gv
        dx = r * (dxh - xh * jnp.mean(dxh * xh, axis=-1, keepdims=True))
        dx_ref[...] = dx
        dxb_ref[...] = dx.astype(BF16)
        pg = _rows8(dy * xh)
        plo = _rows8(e * e)

        @pl.when(i == 0)
        def _():
            acc_g[...] = pg
            acc_l[...] = plo

        @pl.when(i > 0)
        def _():
            acc_g[...] += pg
            acc_l[...] += plo

        @pl.when(i == n - 1)
        def _():
            dg_ref[...] = jnp.sum(acc_g[...], axis=0, keepdims=True)
            loss_ref[...] = jnp.full((1, LANES), (0.5 / d) * jnp.sum(acc_l[...]), F32)

    row = pl.BlockSpec((tm, d), lambda i: (i, 0))
    vec = pl.BlockSpec((1, d), lambda i: (0, 0))
    return _pallas(
        body,
        name="head",
        grid=(n,),
        in_specs=[row, vec, row],
        out_specs=[row, row, vec, pl.BlockSpec((1, LANES), lambda i: (0, 0))],
        out_shape=[
            jax.ShapeDtypeStruct((s, d), F32),
            jax.ShapeDtypeStruct((s, d), BF16),
            jax.ShapeDtypeStruct((1, d), F32),
            jax.ShapeDtypeStruct((1, LANES), F32),
        ],
        scratch_shapes=[pltpu.VMEM((8, d), F32), pltpu.VMEM((8, d), F32)],
    )(x3, g, target)


def _sgu_fwd(z, gain, ws_b, b_col, w_sgu, after=()):
    s = z.shape[0]
    groups = ws_b.shape[0]

    def body(zu_ref, zv_ref, gain_ref, ws_ref, b_ref, *rest):
        a_ref = rest[-1]
        vv = _gelu(zv_ref[...].astype(F32))
        r = lax.rsqrt(jnp.mean(vv * vv, axis=-1, keepdims=True) + EPS)
        vn = ((vv * r) * gain_ref[...]).astype(BF16)
        u = _gelu(zu_ref[...].astype(F32))
        for g in range(groups):
            sl = slice(g * BLK, (g + 1) * BLK)
            mixed = jnp.dot(ws_ref[g], vn[:, sl], preferred_element_type=F32) + b_ref[g]
            a_ref[:, sl] = (u[:, sl] * mixed).astype(BF16)

    return _pallas(
        body,
        name="sgu_fwd",
        grid=(s // BLK,),
        in_specs=[
            pl.BlockSpec((BLK, w_sgu), lambda c: (c, 0)),
            pl.BlockSpec((BLK, w_sgu), lambda c: (c, 1)),
            pl.BlockSpec((1, w_sgu), lambda c: (0, 0)),
            pl.BlockSpec((groups, BLK, BLK), lambda c: (0, 0, 0)),
            pl.BlockSpec((groups, BLK, 1), lambda c: (0, 0, 0)),
        ]
        + [ANY] * len(after),
        out_specs=pl.BlockSpec((BLK, w_sgu), lambda c: (c, 0)),
        out_shape=jax.ShapeDtypeStruct((s, w_sgu), BF16),
    )(z, z, gain, ws_b, b_col, *after)


def _sgu_bwd(z, da, gain, ws_b, wst_b, b_col, w_sgu, dz, after=()):
    s = z.shape[0]
    groups = ws_b.shape[0]
    n = s // BLK
    n_skip = 1 + len(after)

    def body(zu_ref, zv_ref, da_ref, gain_ref, ws_ref, wst_ref, b_ref, *rest):
        dz_ref, dws_ref, dbs_ref, dgain_ref, acc_gain, vv_s, gv_s, dxh_s = rest[n_skip:]
        c = pl.program_id(0)
        cols = [slice(g * BLK, (g + 1) * BLK) for g in range(groups)]

        ss = jnp.zeros((BLK, 1), F32)
        for sl in cols:
            zv = zv_ref[:, sl].astype(F32)
            vv = _gelu(zv)
            vv_s[:, sl] = vv
            gv_s[:, sl] = _gelu_grad(zv)
            ss = ss + jnp.sum(vv * vv, axis=-1, keepdims=True)
        r = lax.rsqrt(ss * (1.0 / w_sgu) + EPS)

        dot_dx = jnp.zeros((BLK, 1), F32)
        for g, sl in enumerate(cols):
            gain_g = gain_ref[:, sl]
            xh = vv_s[:, sl] * r
            vn = (xh * gain_g).astype(BF16)
            zu = zu_ref[:, sl].astype(F32)
            dav = da_ref[:, sl].astype(F32)
            dmix = dav * _gelu(zu)
            dmix_b = dmix.astype(BF16)
            mixed = jnp.dot(ws_ref[g], vn, preferred_element_type=F32) + b_ref[g]
            dz_ref[:, sl] = (dav * mixed * _gelu_grad(zu)).astype(BF16)
            dvn = jnp.dot(wst_ref[g], dmix_b, preferred_element_type=F32)
            dws_g = lax.dot_general(dmix_b, vn, NT, preferred_element_type=F32)
            dbs_g = jnp.sum(dmix, axis=1, keepdims=True)
            pg = _rows8(dvn * xh)

            @pl.when(c == 0)
            def _():
                dws_ref[g] = dws_g
                dbs_ref[g] = dbs_g
                acc_gain[:, sl] = pg

            @pl.when(c > 0)
            def _():
                dws_ref[g] += dws_g
                dbs_ref[g] += dbs_g
                acc_gain[:, sl] += pg

            dxh = dvn * gain_g
            dxh_s[:, sl] = dxh
            dot_dx = dot_dx + jnp.sum(dxh * xh, axis=-1, keepdims=True)

        mean_dx = dot_dx * (1.0 / w_sgu)
        for g, sl in enumerate(cols):
            dvv = r * (dxh_s[:, sl] - (vv_s[:, sl] * r) * mean_dx)
            dz_ref[:, w_sgu + g * BLK : w_sgu + (g + 1) * BLK] = (dvv * gv_s[:, sl]).astype(BF16)

        @pl.when(c == n - 1)
        def _():
            dgain_ref[...] = jnp.sum(acc_gain[...], axis=0, keepdims=True)

    full3 = pl.BlockSpec((groups, BLK, BLK), lambda c: (0, 0, 0))
    col3 = pl.BlockSpec((groups, BLK, 1), lambda c: (0, 0, 0))
    vec = pl.BlockSpec((1, w_sgu), lambda c: (0, 0))
    return _pallas(
        body,
        name="sgu_bwd",
        grid=(n,),
        in_specs=[
            pl.BlockSpec((BLK, w_sgu), lambda c: (c, 0)),
            pl.BlockSpec((BLK, w_sgu), lambda c: (c, 1)),
            pl.BlockSpec((BLK, w_sgu), lambda c: (c, 0)),
            vec,
            full3,
            full3,
            col3,
            ANY,
        ]
        + [ANY] * len(after),
        out_specs=[pl.BlockSpec((BLK, 2 * w_sgu), lambda c: (c, 0)), full3, col3, vec],
        out_shape=[
            jax.ShapeDtypeStruct(dz.shape, BF16),
            jax.ShapeDtypeStruct((groups, BLK, BLK), F32),
            jax.ShapeDtypeStruct((groups, BLK, 1), F32),
            jax.ShapeDtypeStruct((1, w_sgu), F32),
        ],
        scratch_shapes=[pltpu.VMEM((8, w_sgu), F32)] + [pltpu.VMEM((BLK, w_sgu), F32)] * 3,
        input_output_aliases={7: 0},
    )(z, z, da, gain, ws_b, wst_b, b_col, dz, *after)


def _attn_softmax(sink_ref, q_ref, k_ref, v_ref, bias_ref, s_len, grp):
    kv = pl.program_id(0)
    n = pl.program_id(1)
    start = pl.multiple_of(n * BLK, BLK)
    kb = k_ref[pl.ds(start, 3 * BLK), :]
    vb = v_ref[pl.ds(start, 3 * BLK), :]
    qv = q_ref[...]
    qs = jnp.concatenate([qv[:, g * HEAD_DIM : (g + 1) * HEAD_DIM] for g in range(grp)], axis=0).astype(BF16)
    sc = lax.dot_general(qs, kb, NT, preferred_element_type=F32) * (HEAD_DIM**-0.5)
    sc = sc + bias_ref[...].reshape(grp * BLK, 3 * BLK)
    kpos = start + lax.broadcasted_iota(I32, (1, 3 * BLK), 1) - BLK
    sc = jnp.where((kpos >= 0) & (kpos < s_len), sc, NEG)
    sink = jnp.concatenate([jnp.full((BLK, 1), sink_ref[kv * grp + g], F32) for g in range(grp)], axis=0)
    m = jnp.maximum(jnp.max(sc, axis=-1, keepdims=True), sink)
    p = jnp.exp(sc - m)
    esink = jnp.exp(sink - m)
    den = jnp.sum(p, axis=-1, keepdims=True) + esink
    return start, qs, kb, vb, p / den, esink / den


def _attn_specs(s, grp, q_blk0):
    qw = grp * HEAD_DIM
    return [
        pl.BlockSpec(memory_space=pltpu.SMEM),
        pl.BlockSpec((BLK, qw), lambda kv, n: (n, q_blk0 + kv)),
        pl.BlockSpec((s + 2 * BLK, HEAD_DIM), lambda kv, n: (0, kv)),
        pl.BlockSpec((s + 2 * BLK, HEAD_DIM), lambda kv, n: (0, kv)),
        pl.BlockSpec((grp, BLK, 3 * BLK), lambda kv, n: (kv, 0, 0)),
    ]


def _attn_fwd(sink, z, k_pad, v_pad, bias_tab, grp, q_blk0):
    s = z.shape[0]
    qw = grp * HEAD_DIM

    def body(sink_ref, q_ref, k_ref, v_ref, bias_ref, o_ref):
        _, _, _, vb, pn, _ = _attn_softmax(sink_ref, q_ref, k_ref, v_ref, bias_ref, s, grp)
        o = jnp.dot(pn.astype(BF16), vb, preferred_element_type=F32)
        for g in range(grp):
            o_ref[:, g * HEAD_DIM : (g + 1) * HEAD_DIM] = o[g * BLK : (g + 1) * BLK].astype(BF16)

    return _pallas(
        body,
        name="attn_fwd",
        grid=(N_KV_HEADS, s // BLK),
        in_specs=_attn_specs(s, grp, q_blk0),
        out_specs=pl.BlockSpec((BLK, qw), lambda kv, n: (n, kv)),
        out_shape=jax.ShapeDtypeStruct((s, N_KV_HEADS * qw), BF16),
    )(sink, z, k_pad, v_pad, bias_tab)


def _attn_bwd(sink, z, k_pad, v_pad, bias_tab, dout, dz, grp, q_blk0):
    s = z.shape[0]
    qw = grp * HEAD_DIM
    nb = s // BLK
    heads = N_KV_HEADS * grp

    def body(sink_ref, q_ref, k_ref, v_ref, bias_ref, do_ref, dz_in, dq_ref, dk_ref, dv_ref, dbias_ref, dsink_ref, dk_acc, dv_acc):
        del dz_in
        kv = pl.program_id(0)
        n = pl.program_id(1)
        start, qs, kb, vb, pn, psink = _attn_softmax(sink_ref, q_ref, k_ref, v_ref, bias_ref, s, grp)
        dov = do_ref[...]
        dos = jnp.concatenate([dov[:, g * HEAD_DIM : (g + 1) * HEAD_DIM] for g in range(grp)], axis=0)
        dp = lax.dot_general(dos, vb, NT, preferred_element_type=F32)
        dvb = lax.dot_general(pn.astype(BF16), dos, TN, preferred_element_type=F32)
        delta = jnp.sum(pn * dp, axis=-1, keepdims=True)
        ds = pn * (dp - delta)
        dsb = (ds * (HEAD_DIM**-0.5)).astype(BF16)
        dq = jnp.dot(dsb, kb, preferred_element_type=F32)
        dkb = lax.dot_general(dsb, qs, TN, preferred_element_type=F32)
        for g in range(grp):
            dq_ref[:, g * HEAD_DIM : (g + 1) * HEAD_DIM] = dq[g * BLK : (g + 1) * BLK].astype(BF16)

        @pl.when(n == 0)
        def _():
            dk_acc[...] = jnp.zeros_like(dk_acc)
            dv_acc[...] = jnp.zeros_like(dv_acc)
            dbias_ref[...] = jnp.zeros_like(dbias_ref)

        @pl.when((n == 0) & (kv == 0))
        def _():
            dsink_ref[...] = jnp.zeros_like(dsink_ref)

        dk_acc[pl.ds(start, 3 * BLK), :] += dkb
        dv_acc[pl.ds(start, 3 * BLK), :] += dvb
        dbias_ref[...] += ds.reshape(grp, BLK, 3 * BLK)
        row = lax.broadcasted_iota(I32, (heads, LANES), 0)
        sd = psink * delta
        upd = jnp.zeros((heads, LANES), F32)
        for g in range(grp):
            upd = jnp.where(row == kv * grp + g, -jnp.sum(sd[g * BLK : (g + 1) * BLK]), upd)
        dsink_ref[...] += upd

        @pl.when(n == nb - 1)
        def _():
            dk_ref[...] = dk_acc[...]
            dv_ref[...] = dv_acc[...]

    pad_spec = pl.BlockSpec((s + 2 * BLK, HEAD_DIM), lambda kv, n: (0, kv))
    kvw = N_KV_HEADS * HEAD_DIM
    return _pallas(
        body,
        name="attn_bwd",
        grid=(N_KV_HEADS, nb),
        in_specs=_attn_specs(s, grp, q_blk0) + [pl.BlockSpec((BLK, qw), lambda kv, n: (n, kv)), ANY],
        out_specs=[
            pl.BlockSpec((BLK, qw), lambda kv, n: (n, q_blk0 + kv)),
            pad_spec,
            pad_spec,
            pl.BlockSpec((grp, BLK, 3 * BLK), lambda kv, n: (kv, 0, 0)),
            pl.BlockSpec((heads, LANES), lambda kv, n: (0, 0)),
        ],
        out_shape=[
            jax.ShapeDtypeStruct(dz.shape, BF16),
            jax.ShapeDtypeStruct((s + 2 * BLK, kvw), F32),
            jax.ShapeDtypeStruct((s + 2 * BLK, kvw), F32),
            jax.ShapeDtypeStruct((heads, BLK, 3 * BLK), F32),
            jax.ShapeDtypeStruct((heads, LANES), F32),
        ],
        scratch_shapes=[pltpu.VMEM((s + 2 * BLK, HEAD_DIM), F32), pltpu.VMEM((s + 2 * BLK, HEAD_DIM), F32)],
        input_output_aliases={6: 0},
    )(sink, z, k_pad, v_pad, bias_tab, dout, dz)


def _dkv_to_dz(dk_pad, dv_pad, dz, blk_idx):
    s = dz.shape[0]
    kvw = dk_pad.shape[1]

    def body(dk_ref, dv_ref, dz_in, out_ref):
        del dz_in
        out_ref[:, :kvw] = dk_ref[...].astype(BF16)
        out_ref[:, kvw:] = dv_ref[...].astype(BF16)

    src = pl.BlockSpec((BLK, kvw), lambda i: (i + 1, 0))
    return _pallas(
        body,
        name="dkv_to_dz",
        grid=(s // BLK,),
        in_specs=[src, src, ANY],
        out_specs=pl.BlockSpec((BLK, 2 * kvw), lambda i: (i, blk_idx)),
        out_shape=jax.ShapeDtypeStruct(dz.shape, BF16),
        input_output_aliases={2: 0},
    )(dk_pad, dv_pad, dz)


def _relbias_bwd(dbias_tab, bucket):
    heads = dbias_tab.shape[0]

    def body(dt_ref, bk_ref, out_ref):
        lane = lax.broadcasted_iota(I32, (1, LANES), 1)
        bk = bk_ref[...]
        rows = []
        for h in range(heads):
            dt = dt_ref[h]
            acc = jnp.zeros((1, LANES), F32)
            for b in range(REL_BUCKETS):
                acc = jnp.where(lane == b, jnp.sum(jnp.where(bk == b, dt, 0.0)), acc)
            rows.append(acc)
        out_ref[...] = jnp.concatenate(rows, axis=0)

    return _pallas(body, name="relbias_bwd", out_shape=jax.ShapeDtypeStruct((heads, LANES), F32))(dbias_tab, bucket)


def _t5_bucket(rel):
    nb = REL_BUCKETS // 2
    ret = jnp.where(rel > 0, nb, 0)
    n = jnp.abs(rel)
    max_exact = nb // 2
    nf = jnp.maximum(n, 1).astype(F32)
    large = max_exact + (jnp.log(nf / max_exact) / math.log(REL_MAX_DIST / max_exact) * (nb - max_exact)).astype(I32)
    large = jnp.minimum(large, nb - 1)
    return ret + jnp.where(n < max_exact, n, large)


def _band_tables(rel_bias):
    qi = jnp.arange(BLK)[:, None]
    kj = jnp.arange(3 * BLK)[None, :]
    rel = kj - BLK - qi
    bucket = _t5_bucket(rel).astype(I32)
    heads = rel_bias.shape[1]
    masked = jnp.where(jnp.abs(rel) <= BLK, bucket, -1)

    def body(rb_ref, bk_ref, out_ref):
        bk = bk_ref[...]
        for h in range(heads):
            tab = jnp.full(bk.shape, NEG, F32)
            for b in range(REL_BUCKETS):
                tab = jnp.where(bk == b, rb_ref[b, h], tab)
            out_ref[h] = tab

    bias_tab = _pallas(
        body,
        name="bias_table",
        in_specs=[pl.BlockSpec(memory_space=pltpu.SMEM), pl.BlockSpec(memory_space=pltpu.VMEM)],
        out_specs=pl.BlockSpec(memory_space=pltpu.VMEM),
        out_shape=jax.ShapeDtypeStruct((heads, BLK, 3 * BLK), F32),
    )(rel_bias.astype(F32), masked)
    return bias_tab, bucket


EW_BLOCK_ELEMS = 512 * 1024


def _ew_tiles(shape, elems=EW_BLOCK_ELEMS // 2):
    r, c = shape
    tn = c if c <= 2048 else _tile(c, (2048, 1920, 1536, 1408, 1024, 512))
    tm = _tile(r, [t for t in (1024, 512, 256, 128, 64, 32, 16, 8) if t * tn <= elems] or [8])
    return tm, tn


def _cast_into_full(name, qidx, w, kind, after=()):
    r, c = w.shape
    tm, tn = _ew_tiles(w.shape, EW_BLOCK_ELEMS)
    nbi, nbj = r // tm, c // tn
    if kind == "col":
        full, out_spec = (r, c * N_CHIPS), pl.BlockSpec((tm, tn), lambda i, j, q: (i, q[0] * nbj + j))
    else:
        full, out_spec = (r * N_CHIPS, c), pl.BlockSpec((tm, tn), lambda i, j, q: (q[0] * nbi + i, j))

    def body(q_ref, w_ref, *rest):
        del q_ref
        rest[-1][...] = w_ref[...].astype(BF16)

    return _pallas(
        body,
        name=name,
        grid_spec=pltpu.PrefetchScalarGridSpec(
            num_scalar_prefetch=1,
            grid=(nbi, nbj),
            in_specs=[pl.BlockSpec((tm, tn), lambda i, j, q: (i, j))] + [ANY] * len(after),
            out_specs=out_spec,
        ),
        out_shape=jax.ShapeDtypeStruct(full, BF16),
    )(qidx, w, *after)


def _adamw(name, w, g, m, v, after=()):
    tm, tn = _ew_tiles(w.shape, EW_BLOCK_ELEMS)
    if _nbytes(w.shape, F32) <= 1024 * 1024:
        tm, tn = w.shape
    spec = pl.BlockSpec((tm, tn), lambda i, j: (i, j))
    n_after = len(after)

    def body(w_ref, g_ref, m_ref, v_ref, *rest):
        d_ref, nm_ref, nv_ref, g_out_ref = rest[n_after:]
        gv = g_ref[...]
        g_out_ref[...] = gv
        nm = ADAM_B1 * m_ref[...] + (1.0 - ADAM_B1) * gv
        nv = ADAM_B2 * v_ref[...] + (1.0 - ADAM_B2) * (gv * gv)
        m_hat = nm / (1.0 - ADAM_B1**ADAM_STEP)
        v_hat = nv / (1.0 - ADAM_B2**ADAM_STEP)
        d_ref[...] = -ADAM_LR * (m_hat / (jnp.sqrt(v_hat) + ADAM_EPS) + ADAM_WD * w_ref[...])
        nm_ref[...] = nm
        nv_ref[...] = nv

    out = jax.ShapeDtypeStruct(w.shape, F32)
    return _pallas(
        body, name=name, grid=(w.shape[0] // tm, w.shape[1] // tn), in_specs=[spec] * 4 + [ANY] * n_after,
        out_specs=[spec] * 4, out_shape=[out, out, out, out],
        compiler_params=_params(_mm_vmem([((tm, tn), F32, 24)])),
    )(w, g, m, v, *after)


def _pair_add(name, cidx, g_full, r_sib, kind):
    hr, hc = r_sib.shape
    tm, tn = _ew_tiles((hr, hc), 2 * EW_BLOCK_ELEMS)
    nbi, nbj = hr // tm, hc // tn
    if kind == "col":
        g_spec = pl.BlockSpec((tm, tn), lambda i, j, c: (c[0] * nbi + i, j))
    else:
        g_spec = pl.BlockSpec((tm, tn), lambda i, j, c: (i, c[0] * nbj + j))
    spec = pl.BlockSpec((tm, tn), lambda i, j, c: (i, j))

    def body(c_ref, g_ref, r_ref, o_ref):
        del c_ref
        o_ref[...] = (g_ref[...].astype(F32) + r_ref[...].astype(F32)).astype(BF16)

    return _pallas(
        body,
        name=name,
        grid_spec=pltpu.PrefetchScalarGridSpec(num_scalar_prefetch=1, grid=(nbi, nbj), in_specs=[g_spec, spec], out_specs=spec),
        out_shape=jax.ShapeDtypeStruct((hr, hc), BF16),
        compiler_params=_params(_mm_vmem([((tm, tn), BF16, 6), ((tm, tn), F32, 3)])),
    )(cidx, g_full, r_sib)


def _chip_sum(name, qidx, c_half, r_ici, kind):
    _, pr, pc = r_ici.shape
    tm, tn = _ew_tiles((pr, pc), 2 * EW_BLOCK_ELEMS)
    nbi, nbj = pr // tm, pc // tn
    if kind == "col":
        own_spec = pl.BlockSpec((tm, tn), lambda i, j, q: (i, q[0] * nbj + j))
        full, out_spec = (2 * pr, pc), pl.BlockSpec((tm, tn), lambda i, j, q: (q[1] * nbi + i, j))
    else:
        own_spec = pl.BlockSpec((tm, tn), lambda i, j, q: (q[0] * nbi + i, j))
        full, out_spec = (pr, 2 * pc), pl.BlockSpec((tm, tn), lambda i, j, q: (i, q[1] * nbj + j))

    def body(q_ref, own_ref, r_ref, o_ref):
        q = q_ref[0]
        own = own_ref[...].astype(F32)
        recv = [r_ref[r].astype(F32) for r in range(3)]
        total = None
        for chip in range(N_CHIPS):
            d = chip ^ q
            term = jnp.where(d == 0, own, jnp.where(d == 2, recv[0], jnp.where(d == 1, recv[1], recv[2])))
            total = term if total is None else total + term
        o_ref[...] = total

    return _pallas(
        body,
        name=name,
        grid_spec=pltpu.PrefetchScalarGridSpec(
            num_scalar_prefetch=1,
            grid=(nbi, nbj),
            in_specs=[own_spec, pl.BlockSpec((3, tm, tn), lambda i, j, q: (0, i, j))],
            out_specs=out_spec,
        ),
        out_shape=jax.ShapeDtypeStruct(full, F32),
        compiler_params=_params(_mm_vmem([((tm, tn), BF16, 8), ((tm, tn), F32, 6)])),
    )(qidx, c_half, r_ici)


_REL_MASK = (2, 1, 3)


def _place():
    x, y, c = lax.axis_index("x"), lax.axis_index("y"), lax.axis_index("c")
    chips = [(1 - x, y), (x, 1 - y), (1 - x, 1 - y)]
    return x, y, c, 2 * x + y, chips


def _shard_view(ref, kind, chip):
    if kind == "col":
        w = ref.shape[1] // N_CHIPS
        return ref.at[:, pl.ds(pl.multiple_of(chip * w, LANES), w)]
    h = ref.shape[0] // N_CHIPS
    return ref.at[pl.ds(pl.multiple_of(chip * h, 16), h), :]


def _row_half(ref, half):
    h = ref.shape[0] // 2
    return ref.at[pl.ds(pl.multiple_of(half * h, 16), h), :]


def _pair_half(ref, kind, half):
    if kind == "col":
        return _row_half(ref, half)
    w = ref.shape[1] // 2
    return ref.at[:, pl.ds(pl.multiple_of(half * w, LANES), w)]


def _remote(src, dst, send_sem, recv_sem, dev):
    return pltpu.make_async_remote_copy(src_ref=src, dst_ref=dst, send_sem=send_sem, recv_sem=recv_sem, device_id=dev, device_id_type=MESH)


def _hbm(a):
    return pltpu.with_memory_space_constraint(a, pltpu.HBM)


def _gather_start(name, fulls, kinds, rels=(0, 1, 2), after=()):
    n_w = len(fulls)

    def body(*refs):
        g = refs[:n_w]
        send_sem, recv_sem = refs[n_w + len(after)], refs[n_w + len(after) + 1]
        token = refs[-1]
        _, _, c, q, chips = _place()
        for w in range(n_w):
            mine = _row_half(_shard_view(g[w], kinds[w], q), c)
            for r in rels if isinstance(rels, tuple) else rels[w]:
                _remote(mine, mine, send_sem.at[3 * w + r], recv_sem.at[3 * w + r], (*chips[r], c)).start()
        token[...] = jnp.zeros_like(token)

    res = _pallas(
        body,
        name=name,
        out_shape=(
            pltpu.SemaphoreType.DMA((3 * n_w,)),
            pltpu.SemaphoreType.DMA((3 * n_w,)),
            *[pltpu.HBM(f.shape, f.dtype) for f in fulls],
            jax.ShapeDtypeStruct((8, LANES), F32),
        ),
        in_specs=[HBM_SPEC] * n_w + [ANY] * len(after),
        out_specs=(SEM_SPEC, SEM_SPEC, *[HBM_SPEC] * n_w, pl.BlockSpec(memory_space=pltpu.VMEM)),
        input_output_aliases={w: w + 2 for w in range(n_w)},
        compiler_params=pltpu.CompilerParams(has_side_effects=EFFECT),
    )(*[_hbm(f) for f in fulls], *after)
    return res[0], res[1], list(res[2 : 2 + n_w]), res[-1]


def _relay_copies(kinds, waiting):
    def copies(refs, send_sem, recv_sem):
        _, _, c, q, chips = _place()
        out = []
        for i, kind in enumerate(kinds):
            for k, (src_rel, dst_rel) in enumerate(((0, 1), (1, 0))):
                held = _row_half(_row_half(_shard_view(refs[i], kind, q ^ _REL_MASK[src_rel]), c), k)
                far = _row_half(_row_half(_shard_view(refs[i], kind, q ^ _REL_MASK[2]), c), k)
                dst = far if waiting else held
                out.append(_remote(held, dst, send_sem.at[2 * i + k], recv_sem.at[2 * i + k], (*chips[dst_rel], c)))
        return out

    return copies


def _forward_copies(kinds, waiting, rels=(0, 1, 2), sem0=0):
    def copies(refs, send_sem, recv_sem):
        x, y, c, q, _ = _place()
        out = []
        for i, kind in enumerate(kinds):
            for k, r in enumerate(rels):
                quarter = _shard_view(refs[i], kind, q ^ _REL_MASK[r])
                landed = _row_half(quarter, c)
                dst = _row_half(quarter, 1 - c) if waiting else landed
                sem = sem0 + len(rels) * i + k
                out.append(_remote(landed, dst, send_sem.at[sem], recv_sem.at[sem], (x, y, 1 - c)))
        return out

    return copies


def _relay_and_forward_copies(kinds, waiting):
    forward = _forward_copies(kinds, waiting, rels=(0, 1), sem0=2 * len(kinds))
    relay = _relay_copies(kinds, waiting)
    return lambda refs, send_sem, recv_sem: forward(refs, send_sem, recv_sem) + relay(refs, send_sem, recv_sem)


def _gather_wait(name, fulls, kinds, w_ids, send_sem, recv_sem, after, rels=(0, 1, 2)):
    n = len(fulls)

    def body(*refs):
        g = refs[:n]
        s_sem, r_sem = refs[n], refs[n + 1]
        x, y, c, q, _ = _place()
        for i, w in enumerate(w_ids):
            mine = _row_half(_shard_view(g[i], kinds[i], q), c)
            for r in rels:
                landed = _row_half(_shard_view(g[i], kinds[i], q ^ _REL_MASK[r]), c)
                cp = _remote(mine, landed, s_sem.at[3 * w + r], r_sem.at[3 * w + r], (x, y, 1 - c))
                cp.wait_send()
                cp.wait_recv()

    res = _pallas(
        body,
        name=name,
        out_shape=[pltpu.HBM(f.shape, f.dtype) for f in fulls],
        in_specs=[HBM_SPEC] * n + [SEM_SPEC, SEM_SPEC, ANY],
        out_specs=[HBM_SPEC] * n,
        input_output_aliases={i: i for i in range(n)},
        compiler_params=pltpu.CompilerParams(has_side_effects=EFFECT),
    )(*fulls, send_sem, recv_sem, after)
    return list(res)


def _gather_forward(name, fulls, kinds, rels=(0, 1, 2)):
    n = len(fulls)

    def body(*refs):
        g = refs[n : 2 * n]
        send, recv = refs[2 * n :]
        _sibling_handshake()
        x, y, c, q, _ = _place()
        sib = (x, y, 1 - c)
        cps = []
        for i in range(n):
            for r in rels:
                landed = _row_half(_shard_view(g[i], kinds[i], q ^ _REL_MASK[r]), c)
                cps.append(_remote(landed, landed, send.at[i, r], recv.at[i, r], sib))
        for cp in cps:
            cp.start()
        for i in range(n):
            for r in rels:
                other = _row_half(_shard_view(g[i], kinds[i], q ^ _REL_MASK[r]), 1 - c)
                _remote(other, other, send.at[i, r], recv.at[i, r], sib).wait_recv()
        for cp in cps:
            cp.wait_send()

    res = _pallas(
        body,
        name=name,
        in_specs=[ANY] * n,
        out_specs=[ANY] * n,
        out_shape=[jax.ShapeDtypeStruct(f.shape, f.dtype) for f in fulls],
        scratch_shapes=[pltpu.SemaphoreType.DMA((n, 3)), pltpu.SemaphoreType.DMA((n, 3))],
        input_output_aliases={i: i for i in range(n)},
        compiler_params=pltpu.CompilerParams(collective_id=SIBLING_BARRIER_ID),
    )(*fulls)
    return list(res)


SIBLING_BARRIER_ID = 1


def _sibling_handshake():
    sib = (lax.axis_index("x"), lax.axis_index("y"), 1 - lax.axis_index("c"))
    barrier = pltpu.get_barrier_semaphore()
    pl.semaphore_signal(barrier, inc=1, device_id=sib, device_id_type=MESH)
    pl.semaphore_wait(barrier, 1)


def _split_start(name, bufs, n_sems, copies, sibling_only=False):
    n = len(bufs)

    def body(*refs):
        if sibling_only:
            _sibling_handshake()
        for cp in copies(refs[:n], refs[n], refs[n + 1]):
            cp.start()

    extra = {"collective_id": SIBLING_BARRIER_ID} if sibling_only else {}

    res = _pallas(
        body,
        name=name,
        out_shape=(
            pltpu.SemaphoreType.DMA((n_sems,)),
            pltpu.SemaphoreType.DMA((n_sems,)),
            *[pltpu.HBM(b.shape, b.dtype) for b in bufs],
        ),
        in_specs=[HBM_SPEC] * n,
        out_specs=(SEM_SPEC, SEM_SPEC, *[HBM_SPEC] * n),
        input_output_aliases={i: i + 2 for i in range(n)},
        compiler_params=pltpu.CompilerParams(has_side_effects=EFFECT, **extra),
    )(*[_hbm(b) for b in bufs])
    return res[0], res[1], list(res[2:])


def _split_wait(name, bufs, send_sem, recv_sem, copies, after):
    n = len(bufs)

    def body(*refs):
        for cp in copies(refs[:n], refs[n], refs[n + 1]):
            cp.wait_send()
            cp.wait_recv()

    res = _pallas(
        body,
        name=name,
        out_shape=[pltpu.HBM(b.shape, b.dtype) for b in bufs],
        in_specs=[HBM_SPEC] * n + [SEM_SPEC, SEM_SPEC, ANY],
        out_specs=[HBM_SPEC] * n,
        input_output_aliases={i: i for i in range(n)},
        compiler_params=pltpu.CompilerParams(has_side_effects=EFFECT),
    )(*bufs, send_sem, recv_sem, after)
    return list(res)


def _pair_exchange_copies(kinds):
    n = len(kinds)

    def copies(refs, send_sem, recv_sem):
        x, y, c, _, _ = _place()
        return [
            _remote(_pair_half(refs[w], kinds[w], 1 - c), refs[n + w], send_sem.at[w], recv_sem.at[w], (x, y, 1 - c))
            for w in range(n)
        ]

    return copies


def _pair_share_copies(kinds, waiting):
    def copies(refs, send_sem, recv_sem):
        x, y, c, _, _ = _place()
        out = []
        for w, kind in enumerate(kinds):
            mine = _pair_half(refs[w], kind, c)
            dst = _pair_half(refs[w], kind, 1 - c) if waiting else mine
            out.append(_remote(mine, dst, send_sem.at[w], recv_sem.at[w], (x, y, 1 - c)))
        return out

    return copies


def _piece_shape(half_shape, kind):
    r, c = half_shape
    return (3, r, c // N_CHIPS) if kind == "col" else (3, r // N_CHIPS, c)


def _chip_send_start(name, halves, kinds):
    n = len(halves)
    lands = [lax.empty(_piece_shape(h.shape, k), BF16) for h, k in zip(halves, kinds)]

    def body(*refs):
        h, land = refs[:n], refs[n : 2 * n]
        send_sem, recv_sem = refs[2 * n], refs[2 * n + 1]
        _, _, c, q, chips = _place()
        for i in range(n):
            for r, chip in enumerate(chips):
                piece = _shard_view(h[i], kinds[i], q ^ _REL_MASK[r])
                _remote(piece, land[i].at[r], send_sem.at[3 * i + r], recv_sem.at[3 * i + r], (*chip, c)).start()

    res = _pallas(
        body,
        name=name,
        out_shape=(
            pltpu.SemaphoreType.DMA((3 * n,)),
            pltpu.SemaphoreType.DMA((3 * n,)),
            *[pltpu.HBM(a.shape, a.dtype) for a in halves],
            *[pltpu.HBM(a.shape, a.dtype) for a in lands],
        ),
        in_specs=[HBM_SPEC] * (2 * n),
        out_specs=(SEM_SPEC, SEM_SPEC, *[HBM_SPEC] * (2 * n)),
        input_output_aliases={i: i + 2 for i in range(2 * n)},
        compiler_params=pltpu.CompilerParams(has_side_effects=EFFECT),
    )(*[_hbm(a) for a in halves], *[_hbm(a) for a in lands])
    return res[0], res[1], list(res[2 : 2 + n]), list(res[2 + n :])


def _chip_send_wait(name, halves, lands, kinds, send_sem, recv_sem, after):
    n = len(halves)

    def body(*refs):
        h, land = refs[:n], refs[n : 2 * n]
        s_sem, r_sem = refs[2 * n], refs[2 * n + 1]
        x, y, c, q, _ = _place()
        for i in range(n):
            for r in range(3):
                piece = _shard_view(h[i], kinds[i], q ^ _REL_MASK[r])
                cp = _remote(piece, land[i].at[r], s_sem.at[3 * i + r], r_sem.at[3 * i + r], (x, y, 1 - c))
                cp.wait_send()
                cp.wait_recv()

    res = _pallas(
        body,
        name=name,
        out_shape=[pltpu.HBM(a.shape, a.dtype) for a in halves] + [pltpu.HBM(a.shape, a.dtype) for a in lands],
        in_specs=[HBM_SPEC] * (2 * n) + [SEM_SPEC, SEM_SPEC, ANY],
        out_specs=[HBM_SPEC] * (2 * n),
        input_output_aliases={i: i for i in range(2 * n)},
        compiler_params=pltpu.CompilerParams(has_side_effects=EFFECT),
    )(*halves, *lands, send_sem, recv_sem, after)
    return list(res[:n]), list(res[n:])


def _small_exchange_copies(waiting):
    def copies(refs, send_sem, recv_sem):
        p, land = refs
        x, y, c, q, _ = _place()
        me = 2 * q + c
        out = []
        for dd in range(1, 2 * N_CHIPS):
            dev = (x ^ ((dd >> 2) & 1), y ^ ((dd >> 1) & 1), c ^ (dd & 1))
            dst = land.at[me ^ dd] if waiting else land.at[me]
            out.append(_remote(p, dst, send_sem.at[dd - 1], recv_sem.at[dd - 1], dev))
        return out

    return copies


def _small_sum(name, me_idx, p, land):
    rows = p.shape[0]
    n_dev = 2 * N_CHIPS

    def body(me_ref, p_ref, land_ref, o_ref):
        me = me_ref[0]
        total = None
        for dev in range(n_dev):
            term = jnp.where(me == dev, p_ref[...], land_ref[dev])
            total = term if total is None else total + term
        o_ref[...] = total

    return _pallas(
        body,
        name=name,
        grid_spec=pltpu.PrefetchScalarGridSpec(
            num_scalar_prefetch=1,
            grid=(1,),
            in_specs=[pl.BlockSpec((rows, LANES), lambda i, m: (0, 0)), pl.BlockSpec((n_dev, rows, LANES), lambda i, m: (0, 0, 0))],
            out_specs=pl.BlockSpec((rows, LANES), lambda i, m: (0, 0)),
        ),
        out_shape=jax.ShapeDtypeStruct(p.shape, F32),
    )(me_idx, p, land)


def _pack(parts):
    rows = []
    for a in parts:
        flat = a.reshape(-1).astype(F32)
        n = flat.shape[0]
        padded = -(-n // (8 * LANES)) * (8 * LANES)
        rows.append(jnp.pad(flat, (0, padded - n)).reshape(-1, LANES))
    return jnp.concatenate(rows, axis=0)


def _unpack(packed, shapes):
    out, row = [], 0
    for shp in shapes:
        n = int(np.prod(shp))
        nrows = -(-n // (8 * LANES)) * 8
        out.append(packed[row : row + nrows].reshape(-1)[:n].reshape(shp))
        row += nrows
    return out


def kernel(x, w_in, norm_mix, sgu_v_gain, sgu_w_s, sgu_b_s, w_a_out, attn_sink, rel_bias, w_b_out, w_o, norm_ffn, w_gate, w_up, w_down, norm_final, loss_target, m_w_in, m_norm_mix, m_sgu_v_gain, m_sgu_w_s, m_sgu_b_s, m_w_a_out, m_attn_sink, m_rel_bias, m_w_b_out, m_w_o, m_norm_ffn, m_w_gate, m_w_up, m_w_down, m_norm_final, v_w_in, v_norm_mix, v_sgu_v_gain, v_sgu_w_s, v_sgu_b_s, v_w_a_out, v_attn_sink, v_rel_bias, v_w_b_out, v_w_o, v_norm_ffn, v_w_gate, v_w_up, v_w_down, v_norm_final):
    s, d = x.shape[1], x.shape[2]
    w_sgu = sgu_v_gain.shape[1]
    groups = sgu_w_s.shape[1]
    heads = attn_sink.shape[1]
    grp = heads // N_KV_HEADS
    w_att = heads * HEAD_DIM
    w_kv = N_KV_HEADS * HEAD_DIM
    d_ff = w_gate.shape[2] * N_CHIPS
    n_in = w_in.shape[2] * N_CHIPS
    off_q = 2 * w_sgu
    off_k = off_q + w_att
    off_g = off_k + 2 * w_kv
    assert n_in == off_g + 2 * d and groups * BLK == w_sgu and s % BLK == 0

    x2d = x.reshape(s, d)
    tgt = loss_target.reshape(s, d)
    c_idx = lax.axis_index("c").astype(I32).reshape(1)
    q_idx = (2 * lax.axis_index("x") + lax.axis_index("y")).astype(I32).reshape(1)
    qc_idx = jnp.concatenate([q_idx, c_idx])

    W_IN, W_A, W_B, W_O, W_GATE, W_UP, W_DOWN = range(7)
    names = ["w_in", "w_a", "w_b", "w_o", "w_gate", "w_up", "w_down"]
    kinds = ["col", "col", "col", "row", "col", "col", "row"]
    big_w = [w_in[0], w_a_out[0], w_b_out[0], w_o[0], w_gate[0], w_up[0], w_down[0]]
    big_m = [m_w_in[0], m_w_a_out[0], m_w_b_out[0], m_w_o[0], m_w_gate[0], m_w_up[0], m_w_down[0]]
    big_v = [v_w_in[0], v_w_a_out[0], v_w_b_out[0], v_w_o[0], v_w_gate[0], v_w_up[0], v_w_down[0]]
    full_in = _cast_into_full("cast_w_in", q_idx, big_w[W_IN], kinds[W_IN])
    in_send, in_recv, (full_in,), token = _gather_start("gather_start_in", [full_in], [kinds[W_IN]], rels=(0, 1))
    rest = [_cast_into_full("cast_" + names[i], q_idx, big_w[i], kinds[i], after=(token,)) for i in range(1, 7)]

    ws_b = sgu_w_s[0].astype(BF16)
    wst_b = jnp.swapaxes(sgu_w_s[0], 1, 2).astype(BF16)
    b_col = sgu_b_s[0].reshape(groups, BLK, 1)
    bias_tab, bucket = _band_tables(rel_bias)
    sink = attn_sink[0]
    small_w = [norm_mix, sgu_v_gain, sgu_w_s, sgu_b_s, attn_sink, rel_bias, norm_ffn, norm_final]
    small_m = [m_norm_mix, m_sgu_v_gain, m_sgu_w_s, m_sgu_b_s, m_attn_sink, m_rel_bias, m_norm_ffn, m_norm_final]
    small_v = [v_norm_mix, v_sgu_v_gain, v_sgu_w_s, v_sgu_b_s, v_attn_sink, v_rel_bias, v_norm_ffn, v_norm_final]
    small_shapes = [w.shape for w in small_w]
    zero1 = jnp.zeros((1,), F32)
    pw, pm, pv = _pack(small_w + [zero1]), _pack(small_m + [zero1]), _pack(small_v + [zero1])
    h1 = _rms_fwd("rms_mix", x2d, norm_mix, after=(token, rest[-1], ws_b, wst_b, b_col, bias_tab, pw, pm, pv))
    (full_in,) = _gather_wait("gather_wait_in", [full_in], [kinds[W_IN]], [0], in_send, in_recv, h1, rels=(0, 1))
    relay_send, relay_recv, (full_in,) = _split_start(
        "gather_relay_in", [full_in], 4, _relay_and_forward_copies([kinds[W_IN]], False)
    )
    full_down = rest.pop()
    full_up = rest.pop()
    ag_send, ag_recv, rest, token = _gather_start("gather_start_rest", rest, kinds[1:5], rels=(0, 1), after=(full_in, full_up))
    (full_in,) = _split_wait(
        "gather_relay_wait_in", [full_in], relay_send, relay_recv, _relay_and_forward_copies([kinds[W_IN]], True), token
    )
    (g_in,) = _gather_forward("gather_fwd_in", [full_in], [kinds[W_IN]], rels=(2,))
    fulls = [g_in] + rest

    def relay_begin(tag, ids, after, source=None):
        ks = [kinds[i] for i in ids]
        send, recv, bufs, w_ids = source or (ag_send, ag_recv, [fulls[i] for i in ids], [i - 1 for i in ids])
        bufs = _gather_wait("gather_wait_" + tag, bufs, ks, w_ids, send, recv, after, rels=(0, 1))
        send, recv, bufs = _split_start("gather_relay_" + tag, bufs, 4 * len(ids), _relay_and_forward_copies(ks, False))
        return tag, ks, send, recv, bufs

    def relay_end(state, after):
        tag, ks, send, recv, bufs = state
        bufs = _split_wait("gather_relay_wait_" + tag, bufs, send, recv, _relay_and_forward_copies(ks, True), after)
        return _gather_forward("gather_fwd_" + tag, bufs, ks, rels=(2,))

    def relay_end_async(state, after):
        tag, ks, send, recv, bufs = state
        bufs = _split_wait("gather_relay_wait_" + tag, bufs, send, recv, _relay_and_forward_copies(ks, True), after)
        send, recv, bufs = _split_start(
            "gather_fwd_start_" + tag, bufs, len(ks), _forward_copies(ks, False, rels=(2,)), sibling_only=True
        )
        return tag, ks, send, recv, bufs

    def forwarded(state, after):
        tag, ks, send, recv, bufs = state
        return _split_wait("gather_fwd_wait_" + tag, bufs, send, recv, _forward_copies(ks, True, rels=(2,)), after)

    tm = _tile(s, (1024, 512, 256, 128))

    tn = _tile(n_in, (768, 640, 512))
    z = _mm(
        "mm_z", (s // tm, n_in // tn, 1), [h1, g_in],
        [pl.BlockSpec((tm, d), lambda i, j, k: (i, 0)), pl.BlockSpec((d, tn), lambda i, j, k: (0, j))],
        [jax.ShapeDtypeStruct((s, n_in), BF16)], [pl.BlockSpec((tm, tn), lambda i, j, k: (i, j))],
        [(0, 1, NN, 0)], 1, (tm, tn), 1, lambda ins, vals, outs, cs: _put(outs[0], cs, vals[0]),
        _mm_vmem([((tm, d), BF16, 2), ((d, tn), BF16, 2), ((tm, tn), F32, 3)]),
    )[0]
    mix_relay = relay_begin("mix", [W_A, W_B, W_O], z)
    up_send, up_recv, (full_up,), token = _gather_start(
        "gather_start_up", [full_up], [kinds[W_UP]], rels=(0, 1), after=(mix_relay[4][0],)
    )

    a_act = _sgu_fwd(z, sgu_v_gain, ws_b, b_col, w_sgu, after=(token,))

    kv_b = z[:, off_k:off_g]
    k_pad = jnp.pad(kv_b[:, :w_kv], ((BLK, BLK), (0, 0)))
    v_pad = jnp.pad(kv_b[:, w_kv:], ((BLK, BLK), (0, 0)))
    q_blk0 = off_q // (grp * HEAD_DIM)
    att = _attn_fwd(sink, z, k_pad, v_pad, bias_tab, grp, q_blk0)

    g_a, g_b, g_o = relay_end(mix_relay, att)
    gate_relay = relay_begin("gate", [W_GATE], g_a)

    tg = _tile(d, (512,))
    ga0, gb0 = off_g // tg, (off_g + d) // tg

    def ep_gate(ins, vals, outs, cs):
        sa, sb = _sigmoid(ins[4][:, cs].astype(F32)), _sigmoid(ins[5][:, cs].astype(F32))
        _put(outs[0], cs, sa * vals[0] + sb * vals[1])
        _put(outs[1], cs, vals[0])
        _put(outs[2], cs, vals[1])

    t_out = pl.BlockSpec((tm, tg), lambda i, j, k: (i, j))
    m_act, y_a, y_b = _mm(
        "mm_branches", (s // tm, d // tg, 1), [a_act, g_a, att, g_b, z, z],
        [pl.BlockSpec((tm, w_sgu), lambda i, j, k: (i, 0)), pl.BlockSpec((w_sgu, tg), lambda i, j, k: (0, j)),
         pl.BlockSpec((tm, w_att), lambda i, j, k: (i, 0)), pl.BlockSpec((w_att, tg), lambda i, j, k: (0, j)),
         pl.BlockSpec((tm, tg), lambda i, j, k: (i, ga0 + j)), pl.BlockSpec((tm, tg), lambda i, j, k: (i, gb0 + j))],
        [jax.ShapeDtypeStruct((s, d), BF16)] * 3,
        [t_out, t_out, t_out], [(0, 1, NN, 0), (2, 3, NN, 1)], 2, (tm, tg), 1, ep_gate,
        _mm_vmem([((tm, w_sgu), BF16, 4), ((w_sgu, tg), BF16, 4), ((tm, tg), F32, 12)]),
        after=(gate_relay[4][0],),
    )

    tn = _tile(d, (1024, 512))

    def ep_residual(ins, vals, outs, cs):
        _put(outs[0], cs, ins[2][:, cs] + vals[0])

    up_relay = relay_begin("up", [W_UP], m_act, source=(up_send, up_recv, [full_up], [0]))
    down_send, down_recv, (full_down,), token = _gather_start(
        "gather_start_down", [full_down], [kinds[W_DOWN]], after=(up_relay[4][0],)
    )
    x2 = _mm(
        "mm_wo", (s // tm, d // tn, 1), [m_act, g_o, x2d],
        [pl.BlockSpec((tm, d), lambda i, j, k: (i, 0)), pl.BlockSpec((d, tn), lambda i, j, k: (0, j)),
         pl.BlockSpec((tm, tn), lambda i, j, k: (i, j))],
        [jax.ShapeDtypeStruct((s, d), F32)], [pl.BlockSpec((tm, tn), lambda i, j, k: (i, j))],
        [(0, 1, NN, 0)], 1, (tm, tn), 1, ep_residual,
        _mm_vmem([((tm, d), BF16, 2), ((d, tn), BF16, 2), ((tm, tn), F32, 5)]),
        after=(token,),
    )[0]
    gate_fwd = relay_end_async(gate_relay, x2)
    up_fwd = relay_end_async(up_relay, gate_fwd[4][0])
    h2 = _rms_fwd("rms_ffn", x2, norm_ffn, after=(up_fwd[4][0],))
    (g_gate,) = forwarded(gate_fwd, h2)
    (g_up,) = forwarded(up_fwd, g_gate)

    tf = _tile(d_ff, (512,))

    def ep_swiglu(ins, vals, outs, cs):
        gt, up = vals
        _put(outs[0], cs, gt)
        _put(outs[1], cs, up)
        _put(outs[2], cs, (gt * _sigmoid(gt)) * up)

    f_out = pl.BlockSpec((tm, tf), lambda i, j, k: (i, j))
    gt, up, f_act = _mm(
        "mm_gate_up", (s // tm, d_ff // tf, 1), [h2, g_gate, g_up],
        [pl.BlockSpec((tm, d), lambda i, j, k: (i, 0)), pl.BlockSpec((d, tf), lambda i, j, k: (0, j)),
         pl.BlockSpec((d, tf), lambda i, j, k: (0, j))],
        [jax.ShapeDtypeStruct((s, d_ff), BF16)] * 3,
        [f_out, f_out, f_out], [(0, 1, NN, 0), (0, 2, NN, 1)], 2, (tm, tf), 1, ep_swiglu,
        _mm_vmem([((tm, d), BF16, 2), ((d, tf), BF16, 4), ((tm, tf), F32, 8)]),    )
    (g_down,) = _gather_forward(
        "gather_fwd_ffn_out",
        _gather_wait("gather_wait_ffn_out", [full_down], [kinds[W_DOWN]], [0], down_send, down_recv, f_act),
        [kinds[W_DOWN]],
    )

    tkf = _tile(d_ff, (1408, 1024, 512))
    tml, tnl = _tile(s, (512, 256, 128)), _tile(d, (512,))
    x3 = _mm(
        "mm_down", (s // tml, d // tnl, 1), [f_act, g_down, x2],
        [pl.BlockSpec((tml, d_ff), lambda i, j, k: (i, 0)), pl.BlockSpec((d_ff, tnl), lambda i, j, k: (0, j)),
         pl.BlockSpec((tml, tnl), lambda i, j, k: (i, j))],
        [jax.ShapeDtypeStruct((s, d), F32)], [pl.BlockSpec((tml, tnl), lambda i, j, k: (i, j))],
        [(0, 1, NN, 0)], 1, (tml, tnl), 1, ep_residual,
        _mm_vmem([((tml, d_ff), BF16, 2), ((d_ff, tnl), BF16, 2), ((tml, tnl), F32, 6)]),    )[0]

    dx3, dx3b, dg_final, loss_part = _head(x3, norm_final.reshape(1, d), tgt)

    def reduce_a(tag, ids, grads):
        ks = [kinds[i] for i in ids]
        lands = [lax.empty((g.shape[0] // 2, g.shape[1]) if k == "col" else (g.shape[0], g.shape[1] // 2), BF16)
                 for g, k in zip(grads, ks)]
        send, recv, bufs = _split_start("pair_send_" + tag, list(grads) + lands, len(ids), _pair_exchange_copies(ks), sibling_only=True)
        return {"tag": tag, "ids": ids, "ks": ks, "pair": (send, recv, bufs), "token": bufs[0]}

    def reduce_b(st, after):
        tag, ids, ks = st["tag"], st["ids"], st["ks"]
        send, recv, bufs = st["pair"]
        bufs = _split_wait("pair_wait_" + tag, bufs, send, recv, _pair_exchange_copies(ks), after)
        grads, from_sib = bufs[: len(ids)], bufs[len(ids) :]
        halves = [_pair_add("pair_add_" + names[i], c_idx, g, r, k) for i, g, r, k in zip(ids, grads, from_sib, ks)]
        st["chip"] = _chip_send_start("chip_send_" + tag, halves, ks)
        st["token"] = st["chip"][2][0]

    def reduce_c(st, after):
        tag, ids, ks = st["tag"], st["ids"], st["ks"]
        send, recv, halves, lands = st["chip"]
        halves, lands = _chip_send_wait("chip_wait_" + tag, halves, lands, ks, send, recv, after)
        pieces = [_chip_sum("chip_sum_" + names[i], qc_idx, h, r, k) for i, h, r, k in zip(ids, halves, lands, ks)]
        st["share"] = _split_start("share_send_" + tag, pieces, len(ids), _pair_share_copies(ks, False), sibling_only=True)
        st["token"] = st["share"][2][0]

    def reduce_d(st, after):
        send, recv, bufs = st["share"]
        return _split_wait("share_wait_" + st["tag"], bufs, send, recv, _pair_share_copies(st["ks"], True), after)

    grads_big, upd = [None] * 7, [None] * 7

    def update(i, g, after=()):
        upd[i] = _adamw("adamw_" + names[i], big_w[i], g, big_m[i], big_v[i], after=after)
        grads_big[i] = upd[i][3]
        return upd[i][0]

    def finish(st, after):
        shared = reduce_d(st, after)
        after = shared[0]
        for i, g in zip(st["ids"], shared):
            after = update(i, g, (after,))
        return after

    def ep_swiglu_bwd(ins, vals, outs, cs):
        df = vals[0]
        gtv, upv = ins[2][:, cs].astype(F32), ins[3][:, cs].astype(F32)
        sg = _sigmoid(gtv)
        _put(outs[0], cs, df * upv * (sg + gtv * sg * (1.0 - sg)))
        _put(outs[1], cs, df * (gtv * sg))

    dgt, dup = _mm(
        "mm_dswiglu", (s // tm, d_ff // tf, 1), [dx3b, g_down, gt, up],
        [pl.BlockSpec((tm, d), lambda i, j, k: (i, 0)), pl.BlockSpec((tf, d), lambda i, j, k: (j, 0)), f_out, f_out],
        [jax.ShapeDtypeStruct((s, d_ff), BF16), jax.ShapeDtypeStruct((s, d_ff), BF16)], [f_out, f_out],
        [(0, 1, NT, 0)], 1, (tm, tf), 1, ep_swiglu_bwd,
        _mm_vmem([((tm, d), BF16, 2), ((tf, d), BF16, 2), ((tm, tf), F32, 8)]),    )

    def ep_store(ins, vals, outs, cs):
        for o, v in zip(outs, vals):
            _put(o, cs, v)

    twn = _tile(d, (1024, 512))
    gw_down = _mm(
        "mm_gw_down", (d_ff // tkf, d // twn, 1), [f_act, dx3b],
        [pl.BlockSpec((s, tkf), lambda i, j, k: (0, i)), pl.BlockSpec((s, twn), lambda i, j, k: (0, j))],
        [jax.ShapeDtypeStruct((d_ff, d), BF16)], [pl.BlockSpec((tkf, twn), lambda i, j, k: (i, j))],
        [(0, 1, TN, 0)], 1, (tkf, twn), 1, ep_store,
        _mm_vmem([((s, tkf), BF16, 3), ((s, twn), BF16, 2), ((tkf, twn), F32, 3)]),
    )[0]
    red_down = reduce_a("down", [W_DOWN], [gw_down])

    tn2 = _tile(d, (256,))
    dh2_specs = [pl.BlockSpec((tm, d_ff), lambda i, j, k: (i, 0)), pl.BlockSpec((tn2, d_ff), lambda i, j, k: (j, 0))]
    dh2_tile = pl.BlockSpec((tm, tn2), lambda i, j, k: (i, j))
    dh2_vmem = _mm_vmem([((tm, d_ff), BF16, 2), ((tn2, d_ff), BF16, 2), ((tm, tn2), F32, 7)])
    dh2 = _mm(
        "mm_dh2_gate", (s // tm, d // tn2, 1), [dgt, g_gate], dh2_specs,
        [jax.ShapeDtypeStruct((s, d), F32)], [dh2_tile], [(0, 1, NT, 0)], 1, (tm, tn2), 1, ep_store, dh2_vmem,
        after=(red_down["token"],),
    )[0]
    dh2 = _mm(
        "mm_dh2_up", (s // tm, d // tn2, 1), [dup, g_up, dh2], dh2_specs + [dh2_tile],
        [jax.ShapeDtypeStruct((s, d), F32)], [dh2_tile], [(0, 1, NT, 0)], 1, (tm, tn2), 1, ep_residual, dh2_vmem,
    )[0]
    reduce_b(red_down, dh2)

    twr = _tile(d, (1024, 512))
    w_tile = pl.BlockSpec((twr, tf), lambda i, j, k: (i, j))
    gw_gate, gw_up = _mm(
        "mm_gw_gate_up", (d // twr, d_ff // tf, 1), [h2, dgt, dup],
        [pl.BlockSpec((s, twr), lambda i, j, k: (0, i)), pl.BlockSpec((s, tf), lambda i, j, k: (0, j)),
         pl.BlockSpec((s, tf), lambda i, j, k: (0, j))],
        [jax.ShapeDtypeStruct((d, d_ff), BF16), jax.ShapeDtypeStruct((d, d_ff), BF16)], [w_tile, w_tile],
        [(0, 1, TN, 0), (0, 2, TN, 1)], 2, (twr, tf), 1, ep_store,
        _mm_vmem([((s, twr), BF16, 3), ((s, tf), BF16, 4), ((twr, tf), F32, 6)]),
        after=(red_down["token"],),
    )
    red_ffn = reduce_a("ffn_in", [W_GATE, W_UP], [gw_gate, gw_up])

    dx2, dx2b, dg_ffn = _rms_bwd("rms_ffn_bwd", x2, norm_ffn, dh2, dx3, after=(red_ffn["token"],))

    nj = d // tg

    def lo(j):
        return jnp.minimum(j, nj - 1)

    def gate_bwd_body(dx_ref, wo_ref, ga_ref, gb_ref, ya_ref, yb_ref, dya_ref, dyb_ref, dz_ref, keep):
        j = pl.program_id(1)

        @pl.when(j < nj)
        def _():
            dm = lax.dot_general(dx_ref[...], wo_ref[...], NT, preferred_element_type=F32)
            sa, sb = _sigmoid(ga_ref[...].astype(F32)), _sigmoid(gb_ref[...].astype(F32))
            dya_ref[...] = (dm * sa).astype(BF16)
            dyb_ref[...] = (dm * sb).astype(BF16)
            dz_ref[...] = (dm * ya_ref[...].astype(F32) * (sa * (1.0 - sa))).astype(BF16)
            keep[lo(j)] = (dm * yb_ref[...].astype(F32) * (sb * (1.0 - sb))).astype(BF16)

        @pl.when(j >= nj)
        def _():
            dz_ref[...] = keep[jnp.maximum(j - nj, 0)]

    t_lo = pl.BlockSpec((tm, tg), lambda i, j: (i, lo(j)))
    dya, dyb, dz = _pallas(
        gate_bwd_body,
        name="mm_dgate",
        grid=(s // tm, 2 * nj),
        in_specs=[
            pl.BlockSpec((tm, d), lambda i, j: (i, 0)),
            pl.BlockSpec((tg, d), lambda i, j: (lo(j), 0)),
            pl.BlockSpec((tm, tg), lambda i, j: (i, ga0 + lo(j))),
            pl.BlockSpec((tm, tg), lambda i, j: (i, gb0 + lo(j))),
            t_lo,
            t_lo,
        ],
        out_specs=[t_lo, t_lo, pl.BlockSpec((tm, tg), lambda i, j: (i, ga0 + j))],
        out_shape=[jax.ShapeDtypeStruct((s, d), BF16), jax.ShapeDtypeStruct((s, d), BF16), jax.ShapeDtypeStruct((s, n_in), BF16)],
        scratch_shapes=[pltpu.VMEM((nj, tm, tg), BF16)],
        compiler_params=_params(_mm_vmem([((tm, d), BF16, 2), ((tg, d), BF16, 2), ((tm, tg), F32, 14), ((nj, tm, tg), BF16, 1)])),
    )(dx2b, g_o, z, z, y_a, y_b)
    reduce_b(red_ffn, dya)
    reduce_c(red_down, red_ffn["token"])

    gw_o = _mm(
        "mm_gw_o", (d // twr, d // twn, 1), [m_act, dx2b],
        [pl.BlockSpec((s, twr), lambda i, j, k: (0, i)), pl.BlockSpec((s, twn), lambda i, j, k: (0, j))],
        [jax.ShapeDtypeStruct((d, d), BF16)], [pl.BlockSpec((twr, twn), lambda i, j, k: (i, j))],
        [(0, 1, TN, 0)], 1, (twr, twn), 1, ep_store,
        _mm_vmem([((s, twr), BF16, 3), ((s, twn), BF16, 2), ((twr, twn), F32, 3)]),
        after=(red_down["token"],),
    )[0]
    after_down = finish(red_down, gw_o)

    tb = _tile(w_sgu, (1024, 512))
    b_out = pl.BlockSpec((tm, tb), lambda i, j, k: (i, j))

    da, datt = _mm(
        "mm_dbranches", (s // tm, w_sgu // tb, 1), [dya, g_a, dyb, g_b],
        [pl.BlockSpec((tm, d), lambda i, j, k: (i, 0)), pl.BlockSpec((tb, d), lambda i, j, k: (j, 0)),
         pl.BlockSpec((tm, d), lambda i, j, k: (i, 0)), pl.BlockSpec((tb, d), lambda i, j, k: (j, 0))],
        [jax.ShapeDtypeStruct((s, w_sgu), BF16), jax.ShapeDtypeStruct((s, w_att), BF16)], [b_out, b_out],
        [(0, 1, NT, 0), (2, 3, NT, 1)], 2, (tm, tb), 1, ep_store,
        _mm_vmem([((tm, d), BF16, 4), ((tb, d), BF16, 4), ((tm, tb), F32, 6)]),
        after=(after_down,),
    )

    wb_tile = pl.BlockSpec((tb, twn), lambda i, j, k: (i, j))
    gw_a, gw_b = _mm(
        "mm_gw_branches", (w_sgu // tb, d // twn, 1), [a_act, dya, att, dyb],
        [pl.BlockSpec((s, tb), lambda i, j, k: (0, i)), pl.BlockSpec((s, twn), lambda i, j, k: (0, j)),
         pl.BlockSpec((s, tb), lambda i, j, k: (0, i)), pl.BlockSpec((s, twn), lambda i, j, k: (0, j))],
        [jax.ShapeDtypeStruct((w_sgu, d), BF16), jax.ShapeDtypeStruct((w_att, d), BF16)], [wb_tile, wb_tile],
        [(0, 1, TN, 0), (2, 3, TN, 1)], 2, (tb, twn), 1, ep_store,
        _mm_vmem([((s, tb), BF16, 5), ((s, twn), BF16, 4), ((tb, twn), F32, 6)]),
        after=(da,),
    )
    red_mix = reduce_a("mix", [W_O, W_A, W_B], [gw_o, gw_a, gw_b])

    dz, dws, dbs, dgain = _sgu_bwd(z, da, sgu_v_gain, ws_b, wst_b, b_col, w_sgu, dz, after=(red_mix["token"],))
    dz, dk_pad, dv_pad, dbias_tab, dsink = _attn_bwd(sink, z, k_pad, v_pad, bias_tab, datt, dz, grp, q_blk0)
    dz = _dkv_to_dz(dk_pad, dv_pad, dz, off_k // (2 * w_kv))
    drel = _relbias_bwd(dbias_tab, bucket)
    reduce_b(red_mix, dz)
    reduce_c(red_ffn, red_mix["token"])

    early = [dgain, dws, dbs, dsink[:, 0], drel[:, :REL_BUCKETS].T, dg_ffn, dg_final]
    p_early = _pack([g.reshape(shp) for g, shp in zip(early, small_shapes[1:])] + [loss_part[0, :1]])
    land = jnp.zeros((2 * N_CHIPS,) + p_early.shape, F32)
    sm_send, sm_recv, (p_early, land) = _split_start("small_send", [p_early, land], 2 * N_CHIPS - 1, _small_exchange_copies(False))

    tzn = _tile(n_in, (768, 640, 512))
    gw_in = _mm(
        "mm_gw_in", (d // twr, n_in // tzn, 1), [h1, dz],
        [pl.BlockSpec((s, twr), lambda i, j, k: (0, i)), pl.BlockSpec((s, tzn), lambda i, j, k: (0, j))],
        [jax.ShapeDtypeStruct((d, n_in), BF16)], [pl.BlockSpec((twr, tzn), lambda i, j, k: (i, j))],
        [(0, 1, TN, 0)], 1, (twr, tzn), 1, ep_store,
        _mm_vmem([((s, twr), BF16, 3), ((s, tzn), BF16, 2), ((twr, tzn), F32, 3)]),
        after=(red_ffn["token"], p_early),
    )[0]
    red_in = reduce_a("w_in", [W_IN], [gw_in])

    g_gate, g_up = reduce_d(red_ffn, red_in["token"])
    reduce_b(red_in, update(W_GATE, g_gate))

    dh1 = _mm(
        "mm_dh1", (s // tm, d // tn2, 1), [dz, g_in],
        [pl.BlockSpec((tm, n_in), lambda i, j, k: (i, 0)), pl.BlockSpec((tn2, n_in), lambda i, j, k: (j, 0))],
        [jax.ShapeDtypeStruct((s, d), F32)], [pl.BlockSpec((tm, tn2), lambda i, j, k: (i, j))],
        [(0, 1, NT, 0)], 1, (tm, tn2), 1, ep_store,
        _mm_vmem([((tm, n_in), BF16, 2), ((tn2, n_in), BF16, 2), ((tm, tn2), F32, 5)]),
        after=(red_in["token"],),
    )[0]

    reduce_c(red_mix, dh1)
    grad_x, _, dg_mix = _rms_bwd("rms_mix_bwd", x2d, norm_mix, dh1, dx2, after=(red_mix["token"],))

    p_mix = _pack([dg_mix.reshape(small_shapes[0])])
    land_mix = jnp.zeros((2 * N_CHIPS,) + p_mix.shape, F32)
    mx_send, mx_recv, (p_mix, land_mix) = _split_start("mix_send", [p_mix, land_mix], 2 * N_CHIPS - 1, _small_exchange_copies(False))

    reduce_c(red_in, update(W_UP, g_up, (finish(red_mix, p_mix),)))
    p_early, land = _split_wait("small_wait", [p_early, land], sm_send, sm_recv, _small_exchange_copies(True), red_in["token"])
    p_mix, land_mix = _split_wait("mix_wait", [p_mix, land_mix], mx_send, mx_recv, _small_exchange_copies(True), p_early)
    me_idx = 2 * q_idx + c_idx
    packed_g = jnp.concatenate([_small_sum("mix_sum", me_idx, p_mix, land_mix), _small_sum("small_sum", me_idx, p_early, land)], axis=0)
    g_small = _unpack(packed_g, small_shapes + [(1,)])
    loss = g_small[-1].reshape(())
    g_small = g_small[:-1]
    pg = _pack(g_small + [zero1])
    small_upd = _adamw("adamw_small", pw, pg, pm, pv)
    d_small, nm_small, nv_small = [_unpack(a, small_shapes) for a in small_upd[:3]]
    finish(red_in, small_upd[0])

    small_names = ["norm_mix", "sgu_v_gain", "sgu_w_s", "sgu_b_s", "attn_sink", "rel_bias", "norm_ffn", "norm_final"]
    table = {}
    for i, n in enumerate(names):
        table[n] = (grads_big[i][None], upd[i][0][None], upd[i][1][None], upd[i][2][None])
    for i, n in enumerate(small_names):
        table[n] = (g_small[i], d_small[i], nm_small[i], nv_small[i])
    order = ["w_in", "norm_mix", "sgu_v_gain", "sgu_w_s", "sgu_b_s", "w_a", "attn_sink", "rel_bias", "w_b", "w_o", "norm_ffn",
             "w_gate", "w_up", "w_down", "norm_final"]
    outs = [loss, grad_x.reshape(1, s, d)]
    for part in range(4):
        outs += [table[n][part] for n in order]
    return tuple(outs)
```

```python
import math

import jax
import jax.numpy as jnp
import numpy as np
from jax import lax
from jax.experimental import pallas as pl
from jax.experimental.pallas import tpu as pltpu

F32 = jnp.float32
BF16 = jnp.bfloat16
I32 = jnp.int32
MESH = pl.DeviceIdType.MESH

EPS = 1e-6
NEG = -1e30
BLK = 128
HEAD_DIM = 128
N_KV_HEADS = 2
REL_BUCKETS = 32
REL_MAX_DIST = 128
N_CHIPS = 4
ADAM_LR, ADAM_B1, ADAM_B2, ADAM_EPS, ADAM_WD, ADAM_STEP = 0.001, 0.9, 0.999, 1e-08, 0.01, 10

LANES = 128
VMEM_CAP = 60 * 1024 * 1024

NN = (((1,), (0,)), ((), ()))
NT = (((1,), (1,)), ((), ()))
TN = (((0,), (0,)), ((), ()))
ANY = pl.BlockSpec(memory_space=pl.ANY)
HBM_SPEC = pl.BlockSpec(memory_space=pltpu.HBM)
SEM_SPEC = pl.BlockSpec(memory_space=pltpu.SEMAPHORE)
EFFECT = pltpu.SideEffectType.DATAFLOW_SIDE_EFFECTING


def _tile(n, cands):
    for t in cands:
        if n % t == 0:
            return t
    return n


PIN_BYTES = 64 * 1024


def _pin_hbm(a):
    big = hasattr(a, "dtype") and jnp.issubdtype(a.dtype, jnp.floating) and _nbytes(a.shape, a.dtype) >= PIN_BYTES
    return pltpu.with_memory_space_constraint(a, pltpu.HBM) if big else a


def _pallas(body, *, out_shape, **kw):
    def pin(o):
        big = isinstance(o, jax.ShapeDtypeStruct) and jnp.issubdtype(o.dtype, jnp.floating) and _nbytes(o.shape, o.dtype) >= PIN_BYTES
        return pltpu.HBM(o.shape, o.dtype) if big else o

    shapes = type(out_shape)(pin(o) for o in out_shape) if isinstance(out_shape, (list, tuple)) else pin(out_shape)
    call = pl.pallas_call(body, out_shape=shapes, **kw)
    return lambda *args: call(*[_pin_hbm(a) for a in args])


def _params(vmem_bytes=None, **kw):
    if vmem_bytes is not None:
        kw["vmem_limit_bytes"] = int(min(max(vmem_bytes, 32 * 1024 * 1024), VMEM_CAP))
    return pltpu.CompilerParams(**kw)


def _nbytes(shape, dtype):
    return int(np.prod(shape)) * jnp.dtype(dtype).itemsize


def _sigmoid(x):
    return 1.0 / (1.0 + jnp.exp(-x))


_GC = 0.7978845608028654
_GA = 0.044715


def _gelu(x):
    return 0.5 * x * (1.0 + jnp.tanh(_GC * (x + _GA * (x * x * x))))


def _gelu_grad(x):
    t = jnp.tanh(_GC * (x + _GA * (x * x * x)))
    return 0.5 * (1.0 + t) + 0.5 * x * (1.0 - t * t) * (_GC * (1.0 + 3.0 * _GA * (x * x)))


def _bf(v):
    return v if v.dtype == BF16 else v.astype(BF16)


def _mm(name, grid, ins, in_specs, out_shape, out_specs, pairs, n_acc, tile, nk, epilogue, vmem_bytes, after=()):
    assert nk == 1
    n_in, n_out = len(ins) + len(after), len(out_shape)

    def body(*refs):
        in_refs, out_refs = refs[:n_in], refs[n_in : n_in + n_out]
        vals = [None] * n_acc
        for a_i, b_i, dn, acc_i in pairs:
            d = lax.dot_general(_bf(in_refs[a_i][...]), _bf(in_refs[b_i][...]), dn, preferred_element_type=F32)
            vals[acc_i] = d if vals[acc_i] is None else vals[acc_i] + d
        epilogue(in_refs, vals, out_refs, slice(None))

    return _pallas(
        body,
        name=name,
        grid=grid,
        in_specs=list(in_specs) + [ANY] * len(after),
        out_specs=out_specs,
        out_shape=out_shape,
        compiler_params=_params(vmem_bytes),
    )(*ins, *after)


def _put(ref, cs, v):
    ref[:, cs] = v.astype(ref.dtype)


def _mm_vmem(tiles):
    return sum(_nbytes(s, d) * c for s, d, c in tiles) + 4 * 1024 * 1024


def _rows8(v):
    r, d = v.shape
    return v.reshape(r // 8, 8, d).sum(axis=0)


def _rms_fwd(name, x, g, after=()):
    s, d = x.shape
    tm = _tile(s, (256, 128))

    def body(x_ref, g_ref, *rest):
        h_ref = rest[-1]
        xv = x_ref[...]
        r = lax.rsqrt(jnp.mean(xv * xv, axis=-1, keepdims=True) + EPS)
        h_ref[...] = ((xv * r) * g_ref[...]).astype(BF16)

    return _pallas(
        body,
        name=name,
        grid=(s // tm,),
        in_specs=[pl.BlockSpec((tm, d), lambda i: (i, 0)), pl.BlockSpec((1, d), lambda i: (0, 0))] + [ANY] * len(after),
        out_specs=pl.BlockSpec((tm, d), lambda i: (i, 0)),
        out_shape=jax.ShapeDtypeStruct((s, d), BF16),
    )(x, g, *after)


def _rms_bwd(name, x, g, dh, dres, after=()):
    s, d = x.shape
    tm = _tile(s, (256, 128))
    n = s // tm
    n_after = len(after)

    def body(x_ref, g_ref, dh_ref, dres_ref, *rest):
        dx_ref, dxb_ref, dg_ref, acc_ref = rest[n_after:]
        i = pl.program_id(0)
        xv = x_ref[...]
        r = lax.rsqrt(jnp.mean(xv * xv, axis=-1, keepdims=True) + EPS)
        xh = xv * r
        dhv = dh_ref[...]
        dxh = dhv * g_ref[...]
        dx = r * (dxh - xh * jnp.mean(dxh * xh, axis=-1, keepdims=True)) + dres_ref[...]
        dx_ref[...] = dx
        dxb_ref[...] = dx.astype(BF16)
        part = _rows8(dhv * xh)

        @pl.when(i == 0)
        def _():
            acc_ref[...] = part

        @pl.when(i > 0)
        def _():
            acc_ref[...] += part

        @pl.when(i == n - 1)
        def _():
            dg_ref[...] = jnp.sum(acc_ref[...], axis=0, keepdims=True)

    row = pl.BlockSpec((tm, d), lambda i: (i, 0))
    vec = pl.BlockSpec((1, d), lambda i: (0, 0))
    return _pallas(
        body,
        name=name,
        grid=(n,),
        in_specs=[row, vec, row, row] + [ANY] * n_after,
        out_specs=[row, row, vec],
        out_shape=[jax.ShapeDtypeStruct((s, d), F32), jax.ShapeDtypeStruct((s, d), BF16), jax.ShapeDtypeStruct((1, d), F32)],
        scratch_shapes=[pltpu.VMEM((8, d), F32)],
    )(x, g, dh, dres, *after)


def _head(x3, g, target):
    s, d = x3.shape
    tm = _tile(s, (256, 128))
    n = s // tm

    def body(x_ref, g_ref, t_ref, dx_ref, dxb_ref, dg_ref, loss_ref, acc_g, acc_l):
        i = pl.program_id(0)
        xv = x_ref[...]
        gv = g_ref[...]
        r = lax.rsqrt(jnp.mean(xv * xv, axis=-1, keepdims=True) + EPS)
        xh = xv * r
        e = xh * gv - t_ref[...]
        dy = e * (1.0 / d)
        dxh = dy * gv
        dx = r * (dxh - xh * jnp.mean(dxh * xh, axis=-1, keepdims=True))
        dx_ref[...] = dx
        dxb_ref[...] = dx.astype(BF16)
        pg = _rows8(dy * xh)
        plo = _rows8(e * e)

        @pl.when(i == 0)
        def _():
            acc_g[...] = pg
            acc_l[...] = plo

        @pl.when(i > 0)
        def _():
            acc_g[...] += pg
            acc_l[...] += plo

        @pl.when(i == n - 1)
        def _():
            dg_ref[...] = jnp.sum(acc_g[...], axis=0, keepdims=True)
            loss_ref[...] = jnp.full((1, LANES), (0.5 / d) * jnp.sum(acc_l[...]), F32)

    row = pl.BlockSpec((tm, d), lambda i: (i, 0))
    vec = pl.BlockSpec((1, d), lambda i: (0, 0))
    return _pallas(
        body,
        name="head",
        grid=(n,),
        in_specs=[row, vec, row],
        out_specs=[row, row, vec, pl.BlockSpec((1, LANES), lambda i: (0, 0))],
        out_shape=[
            jax.ShapeDtypeStruct((s, d), F32),
            jax.ShapeDtypeStruct((s, d), BF16),
            jax.ShapeDtypeStruct((1, d), F32),
            jax.ShapeDtypeStruct((1, LANES), F32),
        ],
        scratch_shapes=[pltpu.VMEM((8, d), F32), pltpu.VMEM((8, d), F32)],
    )(x3, g, target)


def _sgu_fwd(z, gain, ws_b, b_col, w_sgu, after=()):
    s = z.shape[0]
    groups = ws_b.shape[0]

    def body(zu_ref, zv_ref, gain_ref, ws_ref, b_ref, *rest):
        a_ref = rest[-1]
        vv = _gelu(zv_ref[...].astype(F32))
        r = lax.rsqrt(jnp.mean(vv * vv, axis=-1, keepdims=True) + EPS)
        vn = ((vv * r) * gain_ref[...]).astype(BF16)
        u = _gelu(zu_ref[...].astype(F32))
        for g in range(groups):
            sl = slice(g * BLK, (g + 1) * BLK)
            mixed = jnp.dot(ws_ref[g], vn[:, sl], preferred_element_type=F32) + b_ref[g]
            a_ref[:, sl] = (u[:, sl] * mixed).astype(BF16)

    return _pallas(
        body,
        name="sgu_fwd",
        grid=(s // BLK,),
        in_specs=[
            pl.BlockSpec((BLK, w_sgu), lambda c: (c, 0)),
            pl.BlockSpec((BLK, w_sgu), lambda c: (c, 1)),
            pl.BlockSpec((1, w_sgu), lambda c: (0, 0)),
            pl.BlockSpec((groups, BLK, BLK), lambda c: (0, 0, 0)),
            pl.BlockSpec((groups, BLK, 1), lambda c: (0, 0, 0)),
        ]
        + [ANY] * len(after),
        out_specs=pl.BlockSpec((BLK, w_sgu), lambda c: (c, 0)),
        out_shape=jax.ShapeDtypeStruct((s, w_sgu), BF16),
    )(z, z, gain, ws_b, b_col, *after)


def _sgu_bwd(z, da, gain, ws_b, wst_b, b_col, w_sgu, dz, after=()):
    s = z.shape[0]
    groups = ws_b.shape[0]
    n = s // BLK
    n_skip = 1 + len(after)

    def body(zu_ref, zv_ref, da_ref, gain_ref, ws_ref, wst_ref, b_ref, *rest):
        dz_ref, dws_ref, dbs_ref, dgain_ref, acc_gain, vv_s, gv_s, dxh_s = rest[n_skip:]
        c = pl.program_id(0)
        cols = [slice(g * BLK, (g + 1) * BLK) for g in range(groups)]

        ss = jnp.zeros((BLK, 1), F32)
        for sl in cols:
            zv = zv_ref[:, sl].astype(F32)
            vv = _gelu(zv)
            vv_s[:, sl] = vv
            gv_s[:, sl] = _gelu_grad(zv)
            ss = ss + jnp.sum(vv * vv, axis=-1, keepdims=True)
        r = lax.rsqrt(ss * (1.0 / w_sgu) + EPS)

        dot_dx = jnp.zeros((BLK, 1), F32)
        for g, sl in enumerate(cols):
            gain_g = gain_ref[:, sl]
            xh = vv_s[:, sl] * r
            vn = (xh * gain_g).astype(BF16)
            zu = zu_ref[:, sl].astype(F32)
            dav = da_ref[:, sl].astype(F32)
            dmix = dav * _gelu(zu)
            dmix_b = dmix.astype(BF16)
            mixed = jnp.dot(ws_ref[g], vn, preferred_element_type=F32) + b_ref[g]
            dz_ref[:, sl] = (dav * mixed * _gelu_grad(zu)).astype(BF16)
            dvn = jnp.dot(wst_ref[g], dmix_b, preferred_element_type=F32)
            dws_g = lax.dot_general(dmix_b, vn, NT, preferred_element_type=F32)
            dbs_g = jnp.sum(dmix, axis=1, keepdims=True)
            pg = _rows8(dvn * xh)

            @pl.when(c == 0)
            def _():
                dws_ref[g] = dws_g
                dbs_ref[g] = dbs_g
                acc_gain[:, sl] = pg

            @pl.when(c > 0)
            def _():
                dws_ref[g] += dws_g
                dbs_ref[g] += dbs_g
                acc_gain[:, sl] += pg

            dxh = dvn * gain_g
            dxh_s[:, sl] = dxh
            dot_dx = dot_dx + jnp.sum(dxh * xh, axis=-1, keepdims=True)

        mean_dx = dot_dx * (1.0 / w_sgu)
        for g, sl in enumerate(cols):
            dvv = r * (dxh_s[:, sl] - (vv_s[:, sl] * r) * mean_dx)
            dz_ref[:, w_sgu + g * BLK : w_sgu + (g + 1) * BLK] = (dvv * gv_s[:, sl]).astype(BF16)

        @pl.when(c == n - 1)
        def _():
            dgain_ref[...] = jnp.sum(acc_gain[...], axis=0, keepdims=True)

    full3 = pl.BlockSpec((groups, BLK, BLK), lambda c: (0, 0, 0))
    col3 = pl.BlockSpec((groups, BLK, 1), lambda c: (0, 0, 0))
    vec = pl.BlockSpec((1, w_sgu), lambda c: (0, 0))
    return _pallas(
        body,
        name="sgu_bwd",
        grid=(n,),
        in_specs=[
            pl.BlockSpec((BLK, w_sgu), lambda c: (c, 0)),
            pl.BlockSpec((BLK, w_sgu), lambda c: (c, 1)),
            pl.BlockSpec((BLK, w_sgu), lambda c: (c, 0)),
            vec,
            full3,
            full3,
            col3,
            ANY,
        ]
        + [ANY] * len(after),
        out_specs=[pl.BlockSpec((BLK, 2 * w_sgu), lambda c: (c, 0)), full3, col3, vec],
        out_shape=[
            jax.ShapeDtypeStruct(dz.shape, BF16),
            jax.ShapeDtypeStruct((groups, BLK, BLK), F32),
            jax.ShapeDtypeStruct((groups, BLK, 1), F32),
            jax.ShapeDtypeStruct((1, w_sgu), F32),
        ],
        scratch_shapes=[pltpu.VMEM((8, w_sgu), F32)] + [pltpu.VMEM((BLK, w_sgu), F32)] * 3,
        input_output_aliases={7: 0},
    )(z, z, da, gain, ws_b, wst_b, b_col, dz, *after)


def _attn_softmax(sink_ref, q_ref, k_ref, v_ref, bias_ref, s_len, grp):
    kv = pl.program_id(0)
    n = pl.program_id(1)
    start = pl.multiple_of(n * BLK, BLK)
    kb = k_ref[pl.ds(start, 3 * BLK), :]
    vb = v_ref[pl.ds(start, 3 * BLK), :]
    qv = q_ref[...]
    qs = jnp.concatenate([qv[:, g * HEAD_DIM : (g + 1) * HEAD_DIM] for g in range(grp)], axis=0).astype(BF16)
    sc = lax.dot_general(qs, kb, NT, preferred_element_type=F32) * (HEAD_DIM**-0.5)
    sc = sc + bias_ref[...].reshape(grp * BLK, 3 * BLK)
    kpos = start + lax.broadcasted_iota(I32, (1, 3 * BLK), 1) - BLK
    sc = jnp.where((kpos >= 0) & (kpos < s_len), sc, NEG)
    sink = jnp.concatenate([jnp.full((BLK, 1), sink_ref[kv * grp + g], F32) for g in range(grp)], axis=0)
    m = jnp.maximum(jnp.max(sc, axis=-1, keepdims=True), sink)
    p = jnp.exp(sc - m)
    esink = jnp.exp(sink - m)
    den = jnp.sum(p, axis=-1, keepdims=True) + esink
    return start, qs, kb, vb, p / den, esink / den


def _attn_specs(s, grp, q_blk0):
    qw = grp * HEAD_DIM
    return [
        pl.BlockSpec(memory_space=pltpu.SMEM),
        pl.BlockSpec((BLK, qw), lambda kv, n: (n, q_blk0 + kv)),
        pl.BlockSpec((s + 2 * BLK, HEAD_DIM), lambda kv, n: (0, kv)),
        pl.BlockSpec((s + 2 * BLK, HEAD_DIM), lambda kv, n: (0, kv)),
        pl.BlockSpec((grp, BLK, 3 * BLK), lambda kv, n: (kv, 0, 0)),
    ]


def _attn_fwd(sink, z, k_pad, v_pad, bias_tab, grp, q_blk0):
    s = z.shape[0]
    qw = grp * HEAD_DIM

    def body(sink_ref, q_ref, k_ref, v_ref, bias_ref, o_ref):
        _, _, _, vb, pn, _ = _attn_softmax(sink_ref, q_ref, k_ref, v_ref, bias_ref, s, grp)
        o = jnp.dot(pn.astype(BF16), vb, preferred_element_type=F32)
        for g in range(grp):
            o_ref[:, g * HEAD_DIM : (g + 1) * HEAD_DIM] = o[g * BLK : (g + 1) * BLK].astype(BF16)

    return _pallas(
        body,
        name="attn_fwd",
        grid=(N_KV_HEADS, s // BLK),
        in_specs=_attn_specs(s, grp, q_blk0),
        out_specs=pl.BlockSpec((BLK, qw), lambda kv, n: (n, kv)),
        out_shape=jax.ShapeDtypeStruct((s, N_KV_HEADS * qw), BF16),
    )(sink, z, k_pad, v_pad, bias_tab)


def _attn_bwd(sink, z, k_pad, v_pad, bias_tab, dout, dz, grp, q_blk0):
    s = z.shape[0]
    qw = grp * HEAD_DIM
    nb = s // BLK
    heads = N_KV_HEADS * grp

    def body(sink_ref, q_ref, k_ref, v_ref, bias_ref, do_ref, dz_in, dq_ref, dk_ref, dv_ref, dbias_ref, dsink_ref, dk_acc, dv_acc):
        del dz_in
        kv = pl.program_id(0)
        n = pl.program_id(1)
        start, qs, kb, vb, pn, psink = _attn_softmax(sink_ref, q_ref, k_ref, v_ref, bias_ref, s, grp)
        dov = do_ref[...]
        dos = jnp.concatenate([dov[:, g * HEAD_DIM : (g + 1) * HEAD_DIM] for g in range(grp)], axis=0)
        dp = lax.dot_general(dos, vb, NT, preferred_element_type=F32)
        dvb = lax.dot_general(pn.astype(BF16), dos, TN, preferred_element_type=F32)
        delta = jnp.sum(pn * dp, axis=-1, keepdims=True)
        ds = pn * (dp - delta)
        dsb = (ds * (HEAD_DIM**-0.5)).astype(BF16)
        dq = jnp.dot(dsb, kb, preferred_element_type=F32)
        dkb = lax.dot_general(dsb, qs, TN, preferred_element_type=F32)
        for g in range(grp):
            dq_ref[:, g * HEAD_DIM : (g + 1) * HEAD_DIM] = dq[g * BLK : (g + 1) * BLK].astype(BF16)

        @pl.when(n == 0)
        def _():
            dk_acc[...] = jnp.zeros_like(dk_acc)
            dv_acc[...] = jnp.zeros_like(dv_acc)
            dbias_ref[...] = jnp.zeros_like(dbias_ref)

        @pl.when((n == 0) & (kv == 0))
        def _():
            dsink_ref[...] = jnp.zeros_like(dsink_ref)

        dk_acc[pl.ds(start, 3 * BLK), :] += dkb
        dv_acc[pl.ds(start, 3 * BLK), :] += dvb
        dbias_ref[...] += ds.reshape(grp, BLK, 3 * BLK)
        row = lax.broadcasted_iota(I32, (heads, LANES), 0)
        sd = psink * delta
        upd = jnp.zeros((heads, LANES), F32)
        for g in range(grp):
            upd = jnp.where(row == kv * grp + g, -jnp.sum(sd[g * BLK : (g + 1) * BLK]), upd)
        dsink_ref[...] += upd

        @pl.when(n == nb - 1)
        def _():
            dk_ref[...] = dk_acc[...]
            dv_ref[...] = dv_acc[...]

    pad_spec = pl.BlockSpec((s + 2 * BLK, HEAD_DIM), lambda kv, n: (0, kv))
    kvw = N_KV_HEADS * HEAD_DIM
    return _pallas(
        body,
        name="attn_bwd",
        grid=(N_KV_HEADS, nb),
        in_specs=_attn_specs(s, grp, q_blk0) + [pl.BlockSpec((BLK, qw), lambda kv, n: (n, kv)), ANY],
        out_specs=[
            pl.BlockSpec((BLK, qw), lambda kv, n: (n, q_blk0 + kv)),
            pad_spec,
            pad_spec,
            pl.BlockSpec((grp, BLK, 3 * BLK), lambda kv, n: (kv, 0, 0)),
            pl.BlockSpec((heads, LANES), lambda kv, n: (0, 0)),
        ],
        out_shape=[
            jax.ShapeDtypeStruct(dz.shape, BF16),
            jax.ShapeDtypeStruct((s + 2 * BLK, kvw), F32),
            jax.ShapeDtypeStruct((s + 2 * BLK, kvw), F32),
            jax.ShapeDtypeStruct((heads, BLK, 3 * BLK), F32),
            jax.ShapeDtypeStruct((heads, LANES), F32),
        ],
        scratch_shapes=[pltpu.VMEM((s + 2 * BLK, HEAD_DIM), F32), pltpu.VMEM((s + 2 * BLK, HEAD_DIM), F32)],
        input_output_aliases={6: 0},
    )(sink, z, k_pad, v_pad, bias_tab, dout, dz)


def _dkv_to_dz(dk_pad, dv_pad, dz, blk_idx):
    s = dz.shape[0]
    kvw = dk_pad.shape[1]

    def body(dk_ref, dv_ref, dz_in, out_ref):
        del dz_in
        out_ref[:, :kvw] = dk_ref[...].astype(BF16)
        out_ref[:, kvw:] = dv_ref[...].astype(BF16)

    src = pl.BlockSpec((BLK, kvw), lambda i: (i + 1, 0))
    return _pallas(
        body,
        name="dkv_to_dz",
        grid=(s // BLK,),
        in_specs=[src, src, ANY],
        out_specs=pl.BlockSpec((BLK, 2 * kvw), lambda i: (i, blk_idx)),
        out_shape=jax.ShapeDtypeStruct(dz.shape, BF16),
        input_output_aliases={2: 0},
    )(dk_pad, dv_pad, dz)


def _relbias_bwd(dbias_tab, bucket):
    heads = dbias_tab.shape[0]

    def body(dt_ref, bk_ref, out_ref):
        lane = lax.broadcasted_iota(I32, (1, LANES), 1)
        bk = bk_ref[...]
        rows = []
        for h in range(heads):
            dt = dt_ref[h]
            acc = jnp.zeros((1, LANES), F32)
            for b in range(REL_BUCKETS):
                acc = jnp.where(lane == b, jnp.sum(jnp.where(bk == b, dt, 0.0)), acc)
            rows.append(acc)
        out_ref[...] = jnp.concatenate(rows, axis=0)

    return _pallas(body, name="relbias_bwd", out_shape=jax.ShapeDtypeStruct((heads, LANES), F32))(dbias_tab, bucket)


def _t5_bucket(rel):
    nb = REL_BUCKETS // 2
    ret = jnp.where(rel > 0, nb, 0)
    n = jnp.abs(rel)
    max_exact = nb // 2
    nf = jnp.maximum(n, 1).astype(F32)
    large = max_exact + (jnp.log(nf / max_exact) / math.log(REL_MAX_DIST / max_exact) * (nb - max_exact)).astype(I32)
    large = jnp.minimum(large, nb - 1)
    return ret + jnp.where(n < max_exact, n, large)


def _band_tables(rel_bias):
    qi = jnp.arange(BLK)[:, None]
    kj = jnp.arange(3 * BLK)[None, :]
    rel = kj - BLK - qi
    bucket = _t5_bucket(rel).astype(I32)
    heads = rel_bias.shape[1]
    masked = jnp.where(jnp.abs(rel) <= BLK, bucket, -1)

    def body(rb_ref, bk_ref, out_ref):
        bk = bk_ref[...]
        for h in range(heads):
            tab = jnp.full(bk.shape, NEG, F32)
            for b in range(REL_BUCKETS):
                tab = jnp.where(bk == b, rb_ref[b, h], tab)
            out_ref[h] = tab

    bias_tab = _pallas(
        body,
        name="bias_table",
        in_specs=[pl.BlockSpec(memory_space=pltpu.SMEM), pl.BlockSpec(memory_space=pltpu.VMEM)],
        out_specs=pl.BlockSpec(memory_space=pltpu.VMEM),
        out_shape=jax.ShapeDtypeStruct((heads, BLK, 3 * BLK), F32),
    )(rel_bias.astype(F32), masked)
    return bias_tab, bucket


EW_BLOCK_ELEMS = 512 * 1024


def _ew_tiles(shape, elems=EW_BLOCK_ELEMS // 2):
    r, c = shape
    tn = c if c <= 2048 else _tile(c, (2048, 1920, 1536, 1408, 1024, 512))
    tm = max([t for t in range(16, r + 1, 16) if r % t == 0 and t * tn <= elems] or [8])
    return tm, tn


def _cast_into_full(name, qidx, w, kind, after=()):
    r, c = w.shape
    tm, tn = _ew_tiles(w.shape, EW_BLOCK_ELEMS)
    nbi, nbj = r // tm, c // tn
    if kind == "col":
        full, out_spec = (r, c * N_CHIPS), pl.BlockSpec((tm, tn), lambda i, j, q: (i, q[0] * nbj + j))
    else:
        full, out_spec = (r * N_CHIPS, c), pl.BlockSpec((tm, tn), lambda i, j, q: (q[0] * nbi + i, j))

    def body(q_ref, w_ref, *rest):
        del q_ref
        rest[-1][...] = w_ref[...].astype(BF16)

    return _pallas(
        body,
        name=name,
        grid_spec=pltpu.PrefetchScalarGridSpec(
            num_scalar_prefetch=1,
            grid=(nbi, nbj),
            in_specs=[pl.BlockSpec((tm, tn), lambda i, j, q: (i, j))] + [ANY] * len(after),
            out_specs=out_spec,
        ),
        out_shape=jax.ShapeDtypeStruct(full, BF16),
    )(qidx, w, *after)


def _adamw(name, w, g, m, v, after=()):
    tm, tn = _ew_tiles(w.shape, EW_BLOCK_ELEMS)
    if _nbytes(w.shape, F32) <= 1024 * 1024:
        tm, tn = w.shape
    spec = pl.BlockSpec((tm, tn), lambda i, j: (i, j))
    n_after = len(after)

    def body(w_ref, g_ref, m_ref, v_ref, *rest):
        d_ref, nm_ref, nv_ref, g_out_ref = rest[n_after:]
        gv = g_ref[...]
        g_out_ref[...] = gv
        nm = ADAM_B1 * m_ref[...] + (1.0 - ADAM_B1) * gv
        nv = ADAM_B2 * v_ref[...] + (1.0 - ADAM_B2) * (gv * gv)
        m_hat = nm / (1.0 - ADAM_B1**ADAM_STEP)
        v_hat = nv / (1.0 - ADAM_B2**ADAM_STEP)
        d_ref[...] = -ADAM_LR * (m_hat / (jnp.sqrt(v_hat) + ADAM_EPS) + ADAM_WD * w_ref[...])
        nm_ref[...] = nm
        nv_ref[...] = nv

    out = jax.ShapeDtypeStruct(w.shape, F32)
    return _pallas(
        body, name=name, grid=(w.shape[0] // tm, w.shape[1] // tn), in_specs=[spec] * 4 + [ANY] * n_after,
        out_specs=[spec] * 4, out_shape=[out, out, out, out],
        compiler_params=_params(_mm_vmem([((tm, tn), F32, 24)])),
    )(w, g, m, v, *after)


def _pair_add(name, cidx, g_full, r_sib, kind):
    hr, hc = r_sib.shape
    tm, tn = _ew_tiles((hr, hc), 2 * EW_BLOCK_ELEMS)
    nbi, nbj = hr // tm, hc // tn
    if kind == "col":
        g_spec = pl.BlockSpec((tm, tn), lambda i, j, c: (c[0] * nbi + i, j))
    else:
        g_spec = pl.BlockSpec((tm, tn), lambda i, j, c: (i, c[0] * nbj + j))
    spec = pl.BlockSpec((tm, tn), lambda i, j, c: (i, j))

    def body(c_ref, g_ref, r_ref, o_ref):
        del c_ref
        o_ref[...] = (g_ref[...].astype(F32) + r_ref[...].astype(F32)).astype(BF16)

    return _pallas(
        body,
        name=name,
        grid_spec=pltpu.PrefetchScalarGridSpec(num_scalar_prefetch=1, grid=(nbi, nbj), in_specs=[g_spec, spec], out_specs=spec),
        out_shape=jax.ShapeDtypeStruct((hr, hc), BF16),
        compiler_params=_params(_mm_vmem([((tm, tn), BF16, 6), ((tm, tn), F32, 3)])),
    )(cidx, g_full, r_sib)


def _chip_sum(name, qidx, c_half, r_ici, kind):
    _, pr, pc = r_ici.shape
    tm, tn = _ew_tiles((pr, pc), 2 * EW_BLOCK_ELEMS)
    nbi, nbj = pr // tm, pc // tn
    if kind == "col":
        own_spec = pl.BlockSpec((tm, tn), lambda i, j, q: (i, q[0] * nbj + j))
        full, out_spec = (2 * pr, pc), pl.BlockSpec((tm, tn), lambda i, j, q: (q[1] * nbi + i, j))
    else:
        own_spec = pl.BlockSpec((tm, tn), lambda i, j, q: (q[0] * nbi + i, j))
        full, out_spec = (pr, 2 * pc), pl.BlockSpec((tm, tn), lambda i, j, q: (i, q[1] * nbj + j))

    def body(q_ref, own_ref, r_ref, o_ref):
        q = q_ref[0]
        own = own_ref[...].astype(F32)
        recv = [r_ref[r].astype(F32) for r in range(3)]
        total = None
        for chip in range(N_CHIPS):
            d = chip ^ q
            term = jnp.where(d == 0, own, jnp.where(d == 2, recv[0], jnp.where(d == 1, recv[1], recv[2])))
            total = term if total is None else total + term
        o_ref[...] = total

    return _pallas(
        body,
        name=name,
        grid_spec=pltpu.PrefetchScalarGridSpec(
            num_scalar_prefetch=1,
            grid=(nbi, nbj),
            in_specs=[own_spec, pl.BlockSpec((3, tm, tn), lambda i, j, q: (0, i, j))],
            out_specs=out_spec,
        ),
        out_shape=jax.ShapeDtypeStruct(full, F32),
        compiler_params=_params(_mm_vmem([((tm, tn), BF16, 8), ((tm, tn), F32, 6)])),
    )(qidx, c_half, r_ici)


_REL_MASK = (2, 1, 3)


def _place():
    x, y, c = lax.axis_index("x"), lax.axis_index("y"), lax.axis_index("c")
    chips = [(1 - x, y), (x, 1 - y), (1 - x, 1 - y)]
    return x, y, c, 2 * x + y, chips


def _shard_view(ref, kind, chip):
    if kind == "col":
        w = ref.shape[1] // N_CHIPS
        return ref.at[:, pl.ds(pl.multiple_of(chip * w, LANES), w)]
    h = ref.shape[0] // N_CHIPS
    return ref.at[pl.ds(pl.multiple_of(chip * h, 16), h), :]


def _row_half(ref, half):
    h = ref.shape[0] // 2
    return ref.at[pl.ds(pl.multiple_of(half * h, 16), h), :]


def _pair_half(ref, kind, half):
    if kind == "col":
        return _row_half(ref, half)
    w = ref.shape[1] // 2
    return ref.at[:, pl.ds(pl.multiple_of(half * w, LANES), w)]


def _remote(src, dst, send_sem, recv_sem, dev):
    return pltpu.make_async_remote_copy(src_ref=src, dst_ref=dst, send_sem=send_sem, recv_sem=recv_sem, device_id=dev, device_id_type=MESH)


def _hbm(a):
    return pltpu.with_memory_space_constraint(a, pltpu.HBM)


def _gather_start(name, fulls, kinds, rels=(0, 1, 2), after=()):
    n_w = len(fulls)

    def body(*refs):
        g = refs[:n_w]
        send_sem, recv_sem = refs[n_w + len(after)], refs[n_w + len(after) + 1]
        token = refs[-1]
        _, _, c, q, chips = _place()
        for w in range(n_w):
            mine = _row_half(_shard_view(g[w], kinds[w], q), c)
            for r in rels if isinstance(rels, tuple) else rels[w]:
                _remote(mine, mine, send_sem.at[3 * w + r], recv_sem.at[3 * w + r], (*chips[r], c)).start()
        token[...] = jnp.zeros_like(token)

    res = _pallas(
        body,
        name=name,
        out_shape=(
            pltpu.SemaphoreType.DMA((3 * n_w,)),
            pltpu.SemaphoreType.DMA((3 * n_w,)),
            *[pltpu.HBM(f.shape, f.dtype) for f in fulls],
            jax.ShapeDtypeStruct((8, LANES), F32),
        ),
        in_specs=[HBM_SPEC] * n_w + [ANY] * len(after),
        out_specs=(SEM_SPEC, SEM_SPEC, *[HBM_SPEC] * n_w, pl.BlockSpec(memory_space=pltpu.VMEM)),
        input_output_aliases={w: w + 2 for w in range(n_w)},
        compiler_params=pltpu.CompilerParams(has_side_effects=EFFECT),
    )(*[_hbm(f) for f in fulls], *after)
    return res[0], res[1], list(res[2 : 2 + n_w]), res[-1]


def _relay_copies(kinds, waiting):
    def copies(refs, send_sem, recv_sem):
        _, _, c, q, chips = _place()
        out = []
        for i, kind in enumerate(kinds):
            for k, (src_rel, dst_rel) in enumerate(((0, 1), (1, 0))):
                held = _row_half(_row_half(_shard_view(refs[i], kind, q ^ _REL_MASK[src_rel]), c), k)
                far = _row_half(_row_half(_shard_view(refs[i], kind, q ^ _REL_MASK[2]), c), k)
                dst = far if waiting else held
                out.append(_remote(held, dst, send_sem.at[2 * i + k], recv_sem.at[2 * i + k], (*chips[dst_rel], c)))
        return out

    return copies


def _forward_copies(kinds, waiting, rels=(0, 1, 2), sem0=0):
    def copies(refs, send_sem, recv_sem):
        x, y, c, q, _ = _place()
        out = []
        for i, kind in enumerate(kinds):
            for k, r in enumerate(rels):
                quarter = _shard_view(refs[i], kind, q ^ _REL_MASK[r])
                landed = _row_half(quarter, c)
                dst = _row_half(quarter, 1 - c) if waiting else landed
                sem = sem0 + len(rels) * i + k
                out.append(_remote(landed, dst, send_sem.at[sem], recv_sem.at[sem], (x, y, 1 - c)))
        return out

    return copies


def _relay_and_forward_copies(kinds, waiting):
    forward = _forward_copies(kinds, waiting, rels=(0, 1), sem0=2 * len(kinds))
    relay = _relay_copies(kinds, waiting)
    return lambda refs, send_sem, recv_sem: forward(refs, send_sem, recv_sem) + relay(refs, send_sem, recv_sem)


def _gather_wait(name, fulls, kinds, w_ids, send_sem, recv_sem, after, rels=(0, 1, 2)):
    n = len(fulls)

    def body(*refs):
        g = refs[:n]
        s_sem, r_sem = refs[n], refs[n + 1]
        x, y, c, q, _ = _place()
        for i, w in enumerate(w_ids):
            mine = _row_half(_shard_view(g[i], kinds[i], q), c)
            for r in rels:
                landed = _row_half(_shard_view(g[i], kinds[i], q ^ _REL_MASK[r]), c)
                cp = _remote(mine, landed, s_sem.at[3 * w + r], r_sem.at[3 * w + r], (x, y, 1 - c))
                cp.wait_send()
                cp.wait_recv()

    res = _pallas(
        body,
        name=name,
        out_shape=[pltpu.HBM(f.shape, f.dtype) for f in fulls],
        in_specs=[HBM_SPEC] * n + [SEM_SPEC, SEM_SPEC, ANY],
        out_specs=[HBM_SPEC] * n,
        input_output_aliases={i: i for i in range(n)},
        compiler_params=pltpu.CompilerParams(has_side_effects=EFFECT),
    )(*fulls, send_sem, recv_sem, after)
    return list(res)


def _gather_forward(name, fulls, kinds, rels=(0, 1, 2)):
    n = len(fulls)

    def body(*refs):
        g = refs[n : 2 * n]
        send, recv = refs[2 * n :]
        _sibling_handshake()
        x, y, c, q, _ = _place()
        sib = (x, y, 1 - c)
        cps = []
        for i in range(n):
            for r in rels:
                landed = _row_half(_shard_view(g[i], kinds[i], q ^ _REL_MASK[r]), c)
                cps.append(_remote(landed, landed, send.at[i, r], recv.at[i, r], sib))
        for cp in cps:
            cp.start()
        for i in range(n):
            for r in rels:
                other = _row_half(_shard_view(g[i], kinds[i], q ^ _REL_MASK[r]), 1 - c)
                _remote(other, other, send.at[i, r], recv.at[i, r], sib).wait_recv()
        for cp in cps:
            cp.wait_send()

    res = _pallas(
        body,
        name=name,
        in_specs=[ANY] * n,
        out_specs=[ANY] * n,
        out_shape=[jax.ShapeDtypeStruct(f.shape, f.dtype) for f in fulls],
        scratch_shapes=[pltpu.SemaphoreType.DMA((n, 3)), pltpu.SemaphoreType.DMA((n, 3))],
        input_output_aliases={i: i for i in range(n)},
        compiler_params=pltpu.CompilerParams(collective_id=SIBLING_BARRIER_ID),
    )(*fulls)
    return list(res)


SIBLING_BARRIER_ID = 1


def _sibling_handshake():
    sib = (lax.axis_index("x"), lax.axis_index("y"), 1 - lax.axis_index("c"))
    barrier = pltpu.get_barrier_semaphore()
    pl.semaphore_signal(barrier, inc=1, device_id=sib, device_id_type=MESH)
    pl.semaphore_wait(barrier, 1)


def _split_start(name, bufs, n_sems, copies, sibling_only=False):
    n = len(bufs)

    def body(*refs):
        if sibling_only:
            _sibling_handshake()
        for cp in copies(refs[:n], refs[n], refs[n + 1]):
            cp.start()

    extra = {"collective_id": SIBLING_BARRIER_ID} if sibling_only else {}

    res = _pallas(
        body,
        name=name,
        out_shape=(
            pltpu.SemaphoreType.DMA((n_sems,)),
            pltpu.SemaphoreType.DMA((n_sems,)),
            *[pltpu.HBM(b.shape, b.dtype) for b in bufs],
        ),
        in_specs=[HBM_SPEC] * n,
        out_specs=(SEM_SPEC, SEM_SPEC, *[HBM_SPEC] * n),
        input_output_aliases={i: i + 2 for i in range(n)},
        compiler_params=pltpu.CompilerParams(has_side_effects=EFFECT, **extra),
    )(*[_hbm(b) for b in bufs])
    return res[0], res[1], list(res[2:])


def _split_wait(name, bufs, send_sem, recv_sem, copies, after):
    n = len(bufs)

    def body(*refs):
        for cp in copies(refs[:n], refs[n], refs[n + 1]):
            cp.wait_send()
            cp.wait_recv()

    res = _pallas(
        body,
        name=name,
        out_shape=[pltpu.HBM(b.shape, b.dtype) for b in bufs],
        in_specs=[HBM_SPEC] * n + [SEM_SPEC, SEM_SPEC, ANY],
        out_specs=[HBM_SPEC] * n,
        input_output_aliases={i: i for i in range(n)},
        compiler_params=pltpu.CompilerParams(has_side_effects=EFFECT),
    )(*bufs, send_sem, recv_sem, after)
    return list(res)


def _pair_exchange_copies(kinds):
    n = len(kinds)

    def copies(refs, send_sem, recv_sem):
        x, y, c, _, _ = _place()
        return [
            _remote(_pair_half(refs[w], kinds[w], 1 - c), refs[n + w], send_sem.at[w], recv_sem.at[w], (x, y, 1 - c))
            for w in range(n)
        ]

    return copies


def _pair_share_copies(kinds, waiting):
    def copies(refs, send_sem, recv_sem):
        x, y, c, _, _ = _place()
        out = []
        for w, kind in enumerate(kinds):
            mine = _pair_half(refs[w], kind, c)
            dst = _pair_half(refs[w], kind, 1 - c) if waiting else mine
            out.append(_remote(mine, dst, send_sem.at[w], recv_sem.at[w], (x, y, 1 - c)))
        return out

    return copies


def _piece_shape(half_shape, kind):
    r, c = half_shape
    return (3, r, c // N_CHIPS) if kind == "col" else (3, r // N_CHIPS, c)


def _chip_send_start(name, halves, kinds):
    n = len(halves)
    lands = [lax.empty(_piece_shape(h.shape, k), BF16) for h, k in zip(halves, kinds)]

    def body(*refs):
        h, land = refs[:n], refs[n : 2 * n]
        send_sem, recv_sem = refs[2 * n], refs[2 * n + 1]
        _, _, c, q, chips = _place()
        for i in range(n):
            for r, chip in enumerate(chips):
                piece = _shard_view(h[i], kinds[i], q ^ _REL_MASK[r])
                _remote(piece, land[i].at[r], send_sem.at[3 * i + r], recv_sem.at[3 * i + r], (*chip, c)).start()

    res = _pallas(
        body,
        name=name,
        out_shape=(
            pltpu.SemaphoreType.DMA((3 * n,)),
            pltpu.SemaphoreType.DMA((3 * n,)),
            *[pltpu.HBM(a.shape, a.dtype) for a in halves],
            *[pltpu.HBM(a.shape, a.dtype) for a in lands],
        ),
        in_specs=[HBM_SPEC] * (2 * n),
        out_specs=(SEM_SPEC, SEM_SPEC, *[HBM_SPEC] * (2 * n)),
        input_output_aliases={i: i + 2 for i in range(2 * n)},
        compiler_params=pltpu.CompilerParams(has_side_effects=EFFECT),
    )(*[_hbm(a) for a in halves], *[_hbm(a) for a in lands])
    return res[0], res[1], list(res[2 : 2 + n]), list(res[2 + n :])


def _chip_send_wait(name, halves, lands, kinds, send_sem, recv_sem, after):
    n = len(halves)

    def body(*refs):
        h, land = refs[:n], refs[n : 2 * n]
        s_sem, r_sem = refs[2 * n], refs[2 * n + 1]
        x, y, c, q, _ = _place()
        for i in range(n):
            for r in range(3):
                piece = _shard_view(h[i], kinds[i], q ^ _REL_MASK[r])
                cp = _remote(piece, land[i].at[r], s_sem.at[3 * i + r], r_sem.at[3 * i + r], (x, y, 1 - c))
                cp.wait_send()
                cp.wait_recv()

    res = _pallas(
        body,
        name=name,
        out_shape=[pltpu.HBM(a.shape, a.dtype) for a in halves] + [pltpu.HBM(a.shape, a.dtype) for a in lands],
        in_specs=[HBM_SPEC] * (2 * n) + [SEM_SPEC, SEM_SPEC, ANY],
        out_specs=[HBM_SPEC] * (2 * n),
        input_output_aliases={i: i for i in range(2 * n)},
        compiler_params=pltpu.CompilerParams(has_side_effects=EFFECT),
    )(*halves, *lands, send_sem, recv_sem, after)
    return list(res[:n]), list(res[n:])


def _small_exchange_copies(waiting):
    def copies(refs, send_sem, recv_sem):
        p, land = refs
        x, y, c, q, _ = _place()
        me = 2 * q + c
        out = []
        for dd in range(1, 2 * N_CHIPS):
            dev = (x ^ ((dd >> 2) & 1), y ^ ((dd >> 1) & 1), c ^ (dd & 1))
            dst = land.at[me ^ dd] if waiting else land.at[me]
            out.append(_remote(p, dst, send_sem.at[dd - 1], recv_sem.at[dd - 1], dev))
        return out

    return copies


def _small_sum(name, me_idx, p, land):
    rows = p.shape[0]
    n_dev = 2 * N_CHIPS

    def body(me_ref, p_ref, land_ref, o_ref):
        me = me_ref[0]
        total = None
        for dev in range(n_dev):
            term = jnp.where(me == dev, p_ref[...], land_ref[dev])
            total = term if total is None else total + term
        o_ref[...] = total

    return _pallas(
        body,
        name=name,
        grid_spec=pltpu.PrefetchScalarGridSpec(
            num_scalar_prefetch=1,
            grid=(1,),
            in_specs=[pl.BlockSpec((rows, LANES), lambda i, m: (0, 0)), pl.BlockSpec((n_dev, rows, LANES), lambda i, m: (0, 0, 0))],
            out_specs=pl.BlockSpec((rows, LANES), lambda i, m: (0, 0)),
        ),
        out_shape=jax.ShapeDtypeStruct(p.shape, F32),
    )(me_idx, p, land)


def _pack(parts):
    rows = []
    for a in parts:
        flat = a.reshape(-1).astype(F32)
        n = flat.shape[0]
        padded = -(-n // (8 * LANES)) * (8 * LANES)
        rows.append(jnp.pad(flat, (0, padded - n)).reshape(-1, LANES))
    return jnp.concatenate(rows, axis=0)


def _unpack(packed, shapes):
    out, row = [], 0
    for shp in shapes:
        n = int(np.prod(shp))
        nrows = -(-n // (8 * LANES)) * 8
        out.append(packed[row : row + nrows].reshape(-1)[:n].reshape(shp))
        row += nrows
    return out


def kernel(x, w_in, norm_mix, sgu_v_gain, sgu_w_s, sgu_b_s, w_a_out, attn_sink, rel_bias, w_b_out, w_o, norm_ffn, w_gate, w_up, w_down, norm_final, loss_target, m_w_in, m_norm_mix, m_sgu_v_gain, m_sgu_w_s, m_sgu_b_s, m_w_a_out, m_attn_sink, m_rel_bias, m_w_b_out, m_w_o, m_norm_ffn, m_w_gate, m_w_up, m_w_down, m_norm_final, v_w_in, v_norm_mix, v_sgu_v_gain, v_sgu_w_s, v_sgu_b_s, v_w_a_out, v_attn_sink, v_rel_bias, v_w_b_out, v_w_o, v_norm_ffn, v_w_gate, v_w_up, v_w_down, v_norm_final):
    s, d = x.shape[1], x.shape[2]
    w_sgu = sgu_v_gain.shape[1]
    groups = sgu_w_s.shape[1]
    heads = attn_sink.shape[1]
    grp = heads // N_KV_HEADS
    w_att = heads * HEAD_DIM
    w_kv = N_KV_HEADS * HEAD_DIM
    d_ff = w_gate.shape[2] * N_CHIPS
    n_in = w_in.shape[2] * N_CHIPS
    off_q = 2 * w_sgu
    off_k = off_q + w_att
    off_g = off_k + 2 * w_kv
    assert n_in == off_g + 2 * d and groups * BLK == w_sgu and s % BLK == 0

    x2d = x.reshape(s, d)
    tgt = loss_target.reshape(s, d)
    c_idx = lax.axis_index("c").astype(I32).reshape(1)
    q_idx = (2 * lax.axis_index("x") + lax.axis_index("y")).astype(I32).reshape(1)
    qc_idx = jnp.concatenate([q_idx, c_idx])

    W_IN, W_A, W_B, W_O, W_GATE, W_UP, W_DOWN = range(7)
    names = ["w_in", "w_a", "w_b", "w_o", "w_gate", "w_up", "w_down"]
    kinds = ["col", "col", "col", "row", "col", "col", "row"]
    big_w = [w_in[0], w_a_out[0], w_b_out[0], w_o[0], w_gate[0], w_up[0], w_down[0]]
    big_m = [m_w_in[0], m_w_a_out[0], m_w_b_out[0], m_w_o[0], m_w_gate[0], m_w_up[0], m_w_down[0]]
    big_v = [v_w_in[0], v_w_a_out[0], v_w_b_out[0], v_w_o[0], v_w_gate[0], v_w_up[0], v_w_down[0]]
    full_in = _cast_into_full("cast_w_in", q_idx, big_w[W_IN], kinds[W_IN])
    in_send, in_recv, (full_in,), token = _gather_start("gather_start_in", [full_in], [kinds[W_IN]], rels=(0, 1))
    rest = [_cast_into_full("cast_" + names[i], q_idx, big_w[i], kinds[i], after=(token,)) for i in range(1, 7)]

    ws_b = sgu_w_s[0].astype(BF16)
    wst_b = jnp.swapaxes(sgu_w_s[0], 1, 2).astype(BF16)
    b_col = sgu_b_s[0].reshape(groups, BLK, 1)
    bias_tab, bucket = _band_tables(rel_bias)
    sink = attn_sink[0]
    small_w = [norm_mix, sgu_v_gain, sgu_w_s, sgu_b_s, attn_sink, rel_bias, norm_ffn, norm_final]
    small_m = [m_norm_mix, m_sgu_v_gain, m_sgu_w_s, m_sgu_b_s, m_attn_sink, m_rel_bias, m_norm_ffn, m_norm_final]
    small_v = [v_norm_mix, v_sgu_v_gain, v_sgu_w_s, v_sgu_b_s, v_attn_sink, v_rel_bias, v_norm_ffn, v_norm_final]
    small_shapes = [w.shape for w in small_w]
    zero1 = jnp.zeros((1,), F32)
    pw, pm, pv = _pack(small_w + [zero1]), _pack(small_m + [zero1]), _pack(small_v + [zero1])
    h1 = _rms_fwd("rms_mix", x2d, norm_mix, after=(token, rest[-1], ws_b, wst_b, b_col, bias_tab, pw, pm, pv))
    (full_in,) = _gather_wait("gather_wait_in", [full_in], [kinds[W_IN]], [0], in_send, in_recv, h1, rels=(0, 1))
    relay_send, relay_recv, (full_in,) = _split_start(
        "gather_relay_in", [full_in], 4, _relay_and_forward_copies([kinds[W_IN]], False)
    )
    full_down = rest.pop()
    full_up = rest.pop()
    ag_send, ag_recv, rest, token = _gather_start("gather_start_rest", rest, kinds[1:5], rels=(0, 1), after=(full_in, full_up))
    (full_in,) = _split_wait(
        "gather_relay_wait_in", [full_in], relay_send, relay_recv, _relay_and_forward_copies([kinds[W_IN]], True), token
    )
    (g_in,) = _gather_forward("gather_fwd_in", [full_in], [kinds[W_IN]], rels=(2,))
    fulls = [g_in] + rest

    def relay_begin(tag, ids, after, source=None):
        ks = [kinds[i] for i in ids]
        send, recv, bufs, w_ids = source or (ag_send, ag_recv, [fulls[i] for i in ids], [i - 1 for i in ids])
        bufs = _gather_wait("gather_wait_" + tag, bufs, ks, w_ids, send, recv, after, rels=(0, 1))
        send, recv, bufs = _split_start("gather_relay_" + tag, bufs, 4 * len(ids), _relay_and_forward_copies(ks, False))
        return tag, ks, send, recv, bufs

    def relay_end(state, after):
        tag, ks, send, recv, bufs = state
        bufs = _split_wait("gather_relay_wait_" + tag, bufs, send, recv, _relay_and_forward_copies(ks, True), after)
        return _gather_forward("gather_fwd_" + tag, bufs, ks, rels=(2,))

    def relay_end_async(state, after):
        tag, ks, send, recv, bufs = state
        bufs = _split_wait("gather_relay_wait_" + tag, bufs, send, recv, _relay_and_forward_copies(ks, True), after)
        send, recv, bufs = _split_start(
            "gather_fwd_start_" + tag, bufs, len(ks), _forward_copies(ks, False, rels=(2,)), sibling_only=True
        )
        return tag, ks, send, recv, bufs

    def forwarded(state, after):
        tag, ks, send, recv, bufs = state
        return _split_wait("gather_fwd_wait_" + tag, bufs, send, recv, _forward_copies(ks, True, rels=(2,)), after)

    tm = _tile(s, (1024, 512, 256, 128))

    tn = _tile(n_in, (768, 640, 512))
    z = _mm(
        "mm_z", (s // tm, n_in // tn, 1), [h1, g_in],
        [pl.BlockSpec((tm, d), lambda i, j, k: (i, 0)), pl.BlockSpec((d, tn), lambda i, j, k: (0, j))],
        [jax.ShapeDtypeStruct((s, n_in), BF16)], [pl.BlockSpec((tm, tn), lambda i, j, k: (i, j))],
        [(0, 1, NN, 0)], 1, (tm, tn), 1, lambda ins, vals, outs, cs: _put(outs[0], cs, vals[0]),
        _mm_vmem([((tm, d), BF16, 2), ((d, tn), BF16, 2), ((tm, tn), F32, 3)]),
    )[0]
    mix_relay = relay_begin("mix", [W_A, W_B, W_O], z)
    up_send, up_recv, (full_up,), token = _gather_start(
        "gather_start_up", [full_up], [kinds[W_UP]], rels=(0, 1), after=(mix_relay[4][0],)
    )

    a_act = _sgu_fwd(z, sgu_v_gain, ws_b, b_col, w_sgu, after=(token,))

    kv_b = z[:, off_k:off_g]
    k_pad = jnp.pad(kv_b[:, :w_kv], ((BLK, BLK), (0, 0)))
    v_pad = jnp.pad(kv_b[:, w_kv:], ((BLK, BLK), (0, 0)))
    q_blk0 = off_q // (grp * HEAD_DIM)
    att = _attn_fwd(sink, z, k_pad, v_pad, bias_tab, grp, q_blk0)

    g_a, g_b, g_o = relay_end(mix_relay, att)
    gate_relay = relay_begin("gate", [W_GATE], g_a)

    tg = _tile(d, (512,))
    ga0, gb0 = off_g // tg, (off_g + d) // tg

    def ep_gate(ins, vals, outs, cs):
        sa, sb = _sigmoid(ins[4][:, cs].astype(F32)), _sigmoid(ins[5][:, cs].astype(F32))
        _put(outs[0], cs, sa * vals[0] + sb * vals[1])
        _put(outs[1], cs, vals[0])
        _put(outs[2], cs, vals[1])

    t_out = pl.BlockSpec((tm, tg), lambda i, j, k: (i, j))
    m_act, y_a, y_b = _mm(
        "mm_branches", (s // tm, d // tg, 1), [a_act, g_a, att, g_b, z, z],
        [pl.BlockSpec((tm, w_sgu), lambda i, j, k: (i, 0)), pl.BlockSpec((w_sgu, tg), lambda i, j, k: (0, j)),
         pl.BlockSpec((tm, w_att), lambda i, j, k: (i, 0)), pl.BlockSpec((w_att, tg), lambda i, j, k: (0, j)),
         pl.BlockSpec((tm, tg), lambda i, j, k: (i, ga0 + j)), pl.BlockSpec((tm, tg), lambda i, j, k: (i, gb0 + j))],
        [jax.ShapeDtypeStruct((s, d), BF16)] * 3,
        [t_out, t_out, t_out], [(0, 1, NN, 0), (2, 3, NN, 1)], 2, (tm, tg), 1, ep_gate,
        _mm_vmem([((tm, w_sgu), BF16, 4), ((w_sgu, tg), BF16, 4), ((tm, tg), F32, 12)]),
        after=(gate_relay[4][0],),
    )

    tn = _tile(d, (1024, 512))

    def ep_residual(ins, vals, outs, cs):
        _put(outs[0], cs, ins[2][:, cs] + vals[0])

    up_relay = relay_begin("up", [W_UP], m_act, source=(up_send, up_recv, [full_up], [0]))
    down_send, down_recv, (full_down,), token = _gather_start(
        "gather_start_down", [full_down], [kinds[W_DOWN]], after=(up_relay[4][0],)
    )
    x2 = _mm(
        "mm_wo", (s // tm, d // tn, 1), [m_act, g_o, x2d],
        [pl.BlockSpec((tm, d), lambda i, j, k: (i, 0)), pl.BlockSpec((d, tn), lambda i, j, k: (0, j)),
         pl.BlockSpec((tm, tn), lambda i, j, k: (i, j))],
        [jax.ShapeDtypeStruct((s, d), F32)], [pl.BlockSpec((tm, tn), lambda i, j, k: (i, j))],
        [(0, 1, NN, 0)], 1, (tm, tn), 1, ep_residual,
        _mm_vmem([((tm, d), BF16, 2), ((d, tn), BF16, 2), ((tm, tn), F32, 5)]),
        after=(token,),
    )[0]
    gate_fwd = relay_end_async(gate_relay, x2)
    up_fwd = relay_end_async(up_relay, gate_fwd[4][0])
    h2 = _rms_fwd("rms_ffn", x2, norm_ffn, after=(up_fwd[4][0],))
    (g_gate,) = forwarded(gate_fwd, h2)
    (g_up,) = forwarded(up_fwd, g_gate)

    tf = _tile(d_ff, (512,))

    def ep_swiglu(ins, vals, outs, cs):
        gt, up = vals
        _put(outs[0], cs, gt)
        _put(outs[1], cs, up)
        _put(outs[2], cs, (gt * _sigmoid(gt)) * up)

    f_out = pl.BlockSpec((tm, tf), lambda i, j, k: (i, j))
    gt, up, f_act = _mm(
        "mm_gate_up", (s // tm, d_ff // tf, 1), [h2, g_gate, g_up],
        [pl.BlockSpec((tm, d), lambda i, j, k: (i, 0)), pl.BlockSpec((d, tf), lambda i, j, k: (0, j)),
         pl.BlockSpec((d, tf), lambda i, j, k: (0, j))],
        [jax.ShapeDtypeStruct((s, d_ff), BF16)] * 3,
        [f_out, f_out, f_out], [(0, 1, NN, 0), (0, 2, NN, 1)], 2, (tm, tf), 1, ep_swiglu,
        _mm_vmem([((tm, d), BF16, 2), ((d, tf), BF16, 4), ((tm, tf), F32, 8)]),    )
    (g_down,) = _gather_forward(
        "gather_fwd_ffn_out",
        _gather_wait("gather_wait_ffn_out", [full_down], [kinds[W_DOWN]], [0], down_send, down_recv, f_act),
        [kinds[W_DOWN]],
    )

    tkf = _tile(d_ff, (1408, 1024, 512))
    tml, tnl = _tile(s, (512, 256, 128)), _tile(d, (512,))
    x3 = _mm(
        "mm_down", (s // tml, d // tnl, 1), [f_act, g_down, x2],
        [pl.BlockSpec((tml, d_ff), lambda i, j, k: (i, 0)), pl.BlockSpec((d_ff, tnl), lambda i, j, k: (0, j)),
         pl.BlockSpec((tml, tnl), lambda i, j, k: (i, j))],
        [jax.ShapeDtypeStruct((s, d), F32)], [pl.BlockSpec((tml, tnl), lambda i, j, k: (i, j))],
        [(0, 1, NN, 0)], 1, (tml, tnl), 1, ep_residual,
        _mm_vmem([((tml, d_ff), BF16, 2), ((d_ff, tnl), BF16, 2), ((tml, tnl), F32, 6)]),    )[0]

    dx3, dx3b, dg_final, loss_part = _head(x3, norm_final.reshape(1, d), tgt)

    def reduce_a(tag, ids, grads):
        ks = [kinds[i] for i in ids]
        lands = [lax.empty((g.shape[0] // 2, g.shape[1]) if k == "col" else (g.shape[0], g.shape[1] // 2), BF16)
                 for g, k in zip(grads, ks)]
        send, recv, bufs = _split_start("pair_send_" + tag, list(grads) + lands, len(ids), _pair_exchange_copies(ks), sibling_only=True)
        return {"tag": tag, "ids": ids, "ks": ks, "pair": (send, recv, bufs), "token": bufs[0]}

    def reduce_b(st, after):
        tag, ids, ks = st["tag"], st["ids"], st["ks"]
        send, recv, bufs = st["pair"]
        bufs = _split_wait("pair_wait_" + tag, bufs, send, recv, _pair_exchange_copies(ks), after)
        grads, from_sib = bufs[: len(ids)], bufs[len(ids) :]
        halves = [_pair_add("pair_add_" + names[i], c_idx, g, r, k) for i, g, r, k in zip(ids, grads, from_sib, ks)]
        st["chip"] = _chip_send_start("chip_send_" + tag, halves, ks)
        st["token"] = st["chip"][2][0]

    def reduce_c(st, after):
        tag, ids, ks = st["tag"], st["ids"], st["ks"]
        send, recv, halves, lands = st["chip"]
        halves, lands = _chip_send_wait("chip_wait_" + tag, halves, lands, ks, send, recv, after)
        pieces = [_chip_sum("chip_sum_" + names[i], qc_idx, h, r, k) for i, h, r, k in zip(ids, halves, lands, ks)]
        st["share"] = _split_start("share_send_" + tag, pieces, len(ids), _pair_share_copies(ks, False), sibling_only=True)
        st["token"] = st["share"][2][0]

    def reduce_d(st, after):
        send, recv, bufs = st["share"]
        return _split_wait("share_wait_" + st["tag"], bufs, send, recv, _pair_share_copies(st["ks"], True), after)

    grads_big, upd = [None] * 7, [None] * 7

    def update(i, g, after=()):
        upd[i] = _adamw("adamw_" + names[i], big_w[i], g, big_m[i], big_v[i], after=after)
        grads_big[i] = upd[i][3]
        return upd[i][0]

    def finish(st, after):
        shared = reduce_d(st, after)
        after = shared[0]
        for i, g in zip(st["ids"], shared):
            after = update(i, g, (after,))
        return after

    def ep_swiglu_bwd(ins, vals, outs, cs):
        df = vals[0]
        gtv, upv = ins[2][:, cs].astype(F32), ins[3][:, cs].astype(F32)
        sg = _sigmoid(gtv)
        _put(outs[0], cs, df * upv * (sg + gtv * sg * (1.0 - sg)))
        _put(outs[1], cs, df * (gtv * sg))

    dgt, dup = _mm(
        "mm_dswiglu", (s // tm, d_ff // tf, 1), [dx3b, g_down, gt, up],
        [pl.BlockSpec((tm, d), lambda i, j, k: (i, 0)), pl.BlockSpec((tf, d), lambda i, j, k: (j, 0)), f_out, f_out],
        [jax.ShapeDtypeStruct((s, d_ff), BF16), jax.ShapeDtypeStruct((s, d_ff), BF16)], [f_out, f_out],
        [(0, 1, NT, 0)], 1, (tm, tf), 1, ep_swiglu_bwd,
        _mm_vmem([((tm, d), BF16, 2), ((tf, d), BF16, 2), ((tm, tf), F32, 8)]),    )

    def ep_store(ins, vals, outs, cs):
        for o, v in zip(outs, vals):
            _put(o, cs, v)

    twn = _tile(d, (1024, 512))
    gw_down = _mm(
        "mm_gw_down", (d_ff // tkf, d // twn, 1), [f_act, dx3b],
        [pl.BlockSpec((s, tkf), lambda i, j, k: (0, i)), pl.BlockSpec((s, twn), lambda i, j, k: (0, j))],
        [jax.ShapeDtypeStruct((d_ff, d), BF16)], [pl.BlockSpec((tkf, twn), lambda i, j, k: (i, j))],
        [(0, 1, TN, 0)], 1, (tkf, twn), 1, ep_store,
        _mm_vmem([((s, tkf), BF16, 3), ((s, twn), BF16, 2), ((tkf, twn), F32, 3)]),
    )[0]
    red_down = reduce_a("down", [W_DOWN], [gw_down])

    tn2 = _tile(d, (256,))
    dh2_specs = [pl.BlockSpec((tm, d_ff), lambda i, j, k: (i, 0)), pl.BlockSpec((tn2, d_ff), lambda i, j, k: (j, 0))]
    dh2_tile = pl.BlockSpec((tm, tn2), lambda i, j, k: (i, j))
    dh2_vmem = _mm_vmem([((tm, d_ff), BF16, 2), ((tn2, d_ff), BF16, 2), ((tm, tn2), F32, 7)])
    dh2 = _mm(
        "mm_dh2_gate", (s // tm, d // tn2, 1), [dgt, g_gate], dh2_specs,
        [jax.ShapeDtypeStruct((s, d), F32)], [dh2_tile], [(0, 1, NT, 0)], 1, (tm, tn2), 1, ep_store, dh2_vmem,
        after=(red_down["token"],),
    )[0]
    dh2 = _mm(
        "mm_dh2_up", (s // tm, d // tn2, 1), [dup, g_up, dh2], dh2_specs + [dh2_tile],
        [jax.ShapeDtypeStruct((s, d), F32)], [dh2_tile], [(0, 1, NT, 0)], 1, (tm, tn2), 1, ep_residual, dh2_vmem,
    )[0]
    reduce_b(red_down, dh2)

    twr = _tile(d, (1024, 512))
    w_tile = pl.BlockSpec((twr, tf), lambda i, j, k: (i, j))
    gw_gate, gw_up = _mm(
        "mm_gw_gate_up", (d // twr, d_ff // tf, 1), [h2, dgt, dup],
        [pl.BlockSpec((s, twr), lambda i, j, k: (0, i)), pl.BlockSpec((s, tf), lambda i, j, k: (0, j)),
         pl.BlockSpec((s, tf), lambda i, j, k: (0, j))],
        [jax.ShapeDtypeStruct((d, d_ff), BF16), jax.ShapeDtypeStruct((d, d_ff), BF16)], [w_tile, w_tile],
        [(0, 1, TN, 0), (0, 2, TN, 1)], 2, (twr, tf), 1, ep_store,
        _mm_vmem([((s, twr), BF16, 3), ((s, tf), BF16, 4), ((twr, tf), F32, 6)]),
        after=(red_down["token"],),
    )
    red_ffn = reduce_a("ffn_in", [W_GATE, W_UP], [gw_gate, gw_up])

    dx2, dx2b, dg_ffn = _rms_bwd("rms_ffn_bwd", x2, norm_ffn, dh2, dx3, after=(red_ffn["token"],))

    nj = d // tg

    def lo(j):
        return jnp.minimum(j, nj - 1)

    def gate_bwd_body(dx_ref, wo_ref, ga_ref, gb_ref, ya_ref, yb_ref, dya_ref, dyb_ref, dz_ref, keep):
        j = pl.program_id(1)

        @pl.when(j < nj)
        def _():
            dm = lax.dot_general(dx_ref[...], wo_ref[...], NT, preferred_element_type=F32)
            sa, sb = _sigmoid(ga_ref[...].astype(F32)), _sigmoid(gb_ref[...].astype(F32))
            dya_ref[...] = (dm * sa).astype(BF16)
            dyb_ref[...] = (dm * sb).astype(BF16)
            dz_ref[...] = (dm * ya_ref[...].astype(F32) * (sa * (1.0 - sa))).astype(BF16)
            keep[lo(j)] = (dm * yb_ref[...].astype(F32) * (sb * (1.0 - sb))).astype(BF16)

        @pl.when(j >= nj)
        def _():
            dz_ref[...] = keep[jnp.maximum(j - nj, 0)]

    t_lo = pl.BlockSpec((tm, tg), lambda i, j: (i, lo(j)))
    dya, dyb, dz = _pallas(
        gate_bwd_body,
        name="mm_dgate",
        grid=(s // tm, 2 * nj),
        in_specs=[
            pl.BlockSpec((tm, d), lambda i, j: (i, 0)),
            pl.BlockSpec((tg, d), lambda i, j: (lo(j), 0)),
            pl.BlockSpec((tm, tg), lambda i, j: (i, ga0 + lo(j))),
            pl.BlockSpec((tm, tg), lambda i, j: (i, gb0 + lo(j))),
            t_lo,
            t_lo,
        ],
        out_specs=[t_lo, t_lo, pl.BlockSpec((tm, tg), lambda i, j: (i, ga0 + j))],
        out_shape=[jax.ShapeDtypeStruct((s, d), BF16), jax.ShapeDtypeStruct((s, d), BF16), jax.ShapeDtypeStruct((s, n_in), BF16)],
        scratch_shapes=[pltpu.VMEM((nj, tm, tg), BF16)],
        compiler_params=_params(_mm_vmem([((tm, d), BF16, 2), ((tg, d), BF16, 2), ((tm, tg), F32, 14), ((nj, tm, tg), BF16, 1)])),
    )(dx2b, g_o, z, z, y_a, y_b)
    reduce_b(red_ffn, dya)
    reduce_c(red_down, red_ffn["token"])

    gw_o = _mm(
        "mm_gw_o", (d // twr, d // twn, 1), [m_act, dx2b],
        [pl.BlockSpec((s, twr), lambda i, j, k: (0, i)), pl.BlockSpec((s, twn), lambda i, j, k: (0, j))],
        [jax.ShapeDtypeStruct((d, d), BF16)], [pl.BlockSpec((twr, twn), lambda i, j, k: (i, j))],
        [(0, 1, TN, 0)], 1, (twr, twn), 1, ep_store,
        _mm_vmem([((s, twr), BF16, 3), ((s, twn), BF16, 2), ((twr, twn), F32, 3)]),
        after=(red_down["token"],),
    )[0]
    after_down = finish(red_down, gw_o)

    tb = _tile(w_sgu, (1024, 512))
    b_out = pl.BlockSpec((tm, tb), lambda i, j, k: (i, j))

    da, datt = _mm(
        "mm_dbranches", (s // tm, w_sgu // tb, 1), [dya, g_a, dyb, g_b],
        [pl.BlockSpec((tm, d), lambda i, j, k: (i, 0)), pl.BlockSpec((tb, d), lambda i, j, k: (j, 0)),
         pl.BlockSpec((tm, d), lambda i, j, k: (i, 0)), pl.BlockSpec((tb, d), lambda i, j, k: (j, 0))],
        [jax.ShapeDtypeStruct((s, w_sgu), BF16), jax.ShapeDtypeStruct((s, w_att), BF16)], [b_out, b_out],
        [(0, 1, NT, 0), (2, 3, NT, 1)], 2, (tm, tb), 1, ep_store,
        _mm_vmem([((tm, d), BF16, 4), ((tb, d), BF16, 4), ((tm, tb), F32, 6)]),
        after=(after_down,),
    )

    wb_tile = pl.BlockSpec((tb, twn), lambda i, j, k: (i, j))
    gw_a, gw_b = _mm(
        "mm_gw_branches", (w_sgu // tb, d // twn, 1), [a_act, dya, att, dyb],
        [pl.BlockSpec((s, tb), lambda i, j, k: (0, i)), pl.BlockSpec((s, twn), lambda i, j, k: (0, j)),
         pl.BlockSpec((s, tb), lambda i, j, k: (0, i)), pl.BlockSpec((s, twn), lambda i, j, k: (0, j))],
        [jax.ShapeDtypeStruct((w_sgu, d), BF16), jax.ShapeDtypeStruct((w_att, d), BF16)], [wb_tile, wb_tile],
        [(0, 1, TN, 0), (2, 3, TN, 1)], 2, (tb, twn), 1, ep_store,
        _mm_vmem([((s, tb), BF16, 5), ((s, twn), BF16, 4), ((tb, twn), F32, 6)]),
        after=(da,),
    )
    red_mix = reduce_a("mix", [W_O, W_A, W_B], [gw_o, gw_a, gw_b])

    dz, dws, dbs, dgain = _sgu_bwd(z, da, sgu_v_gain, ws_b, wst_b, b_col, w_sgu, dz, after=(red_mix["token"],))
    dz, dk_pad, dv_pad, dbias_tab, dsink = _attn_bwd(sink, z, k_pad, v_pad, bias_tab, datt, dz, grp, q_blk0)
    dz = _dkv_to_dz(dk_pad, dv_pad, dz, off_k // (2 * w_kv))
    drel = _relbias_bwd(dbias_tab, bucket)
    reduce_b(red_mix, dz)
    reduce_c(red_ffn, red_mix["token"])

    early = [dgain, dws, dbs, dsink[:, 0], drel[:, :REL_BUCKETS].T, dg_ffn, dg_final]
    p_early = _pack([g.reshape(shp) for g, shp in zip(early, small_shapes[1:])] + [loss_part[0, :1]])
    land = jnp.zeros((2 * N_CHIPS,) + p_early.shape, F32)
    sm_send, sm_recv, (p_early, land) = _split_start("small_send", [p_early, land], 2 * N_CHIPS - 1, _small_exchange_copies(False))

    tzn = _tile(n_in, (768, 640, 512))
    gw_in = _mm(
        "mm_gw_in", (d // twr, n_in // tzn, 1), [h1, dz],
        [pl.BlockSpec((s, twr), lambda i, j, k: (0, i)), pl.BlockSpec((s, tzn), lambda i, j, k: (0, j))],
        [jax.ShapeDtypeStruct((d, n_in), BF16)], [pl.BlockSpec((twr, tzn), lambda i, j, k: (i, j))],
        [(0, 1, TN, 0)], 1, (twr, tzn), 1, ep_store,
        _mm_vmem([((s, twr), BF16, 3), ((s, tzn), BF16, 2), ((twr, tzn), F32, 3)]),
        after=(red_ffn["token"], p_early),
    )[0]
    red_in = reduce_a("w_in", [W_IN], [gw_in])

    g_gate, g_up = reduce_d(red_ffn, red_in["token"])
    reduce_b(red_in, update(W_GATE, g_gate))

    dh1 = _mm(
        "mm_dh1", (s // tm, d // tn2, 1), [dz, g_in],
        [pl.BlockSpec((tm, n_in), lambda i, j, k: (i, 0)), pl.BlockSpec((tn2, n_in), lambda i, j, k: (j, 0))],
        [jax.ShapeDtypeStruct((s, d), F32)], [pl.BlockSpec((tm, tn2), lambda i, j, k: (i, j))],
        [(0, 1, NT, 0)], 1, (tm, tn2), 1, ep_store,
        _mm_vmem([((tm, n_in), BF16, 2), ((tn2, n_in), BF16, 2), ((tm, tn2), F32, 5)]),
        after=(red_in["token"],),
    )[0]

    reduce_c(red_mix, dh1)
    grad_x, _, dg_mix = _rms_bwd("rms_mix_bwd", x2d, norm_mix, dh1, dx2, after=(red_mix["token"],))

    p_mix = _pack([dg_mix.reshape(small_shapes[0])])
    land_mix = jnp.zeros((2 * N_CHIPS,) + p_mix.shape, F32)
    mx_send, mx_recv, (p_mix, land_mix) = _split_start("mix_send", [p_mix, land_mix], 2 * N_CHIPS - 1, _small_exchange_copies(False))

    reduce_c(red_in, update(W_UP, g_up, (finish(red_mix, p_mix),)))
    p_early, land = _split_wait("small_wait", [p_early, land], sm_send, sm_recv, _small_exchange_copies(True), red_in["token"])
    p_mix, land_mix = _split_wait("mix_wait", [p_mix, land_mix], mx_send, mx_recv, _small_exchange_copies(True), p_early)
    me_idx = 2 * q_idx + c_idx
    packed_g = jnp.concatenate([_small_sum("mix_sum", me_idx, p_mix, land_mix), _small_sum("small_sum", me_idx, p_early, land)], axis=0)
    g_small = _unpack(packed_g, small_shapes + [(1,)])
    loss = g_small[-1].reshape(())
    g_small = g_small[:-1]
    pg = _pack(g_small + [zero1])
    small_upd = _adamw("adamw_small", pw, pg, pm, pv)
    d_small, nm_small, nv_small = [_unpack(a, small_shapes) for a in small_upd[:3]]
    finish(red_in, small_upd[0])

    small_names = ["norm_mix", "sgu_v_gain", "sgu_w_s", "sgu_b_s", "attn_sink", "rel_bias", "norm_ffn", "norm_final"]
    table = {}
    for i, n in enumerate(names):
        table[n] = (grads_big[i][None], upd[i][0][None], upd[i][1][None], upd[i][2][None])
    for i, n in enumerate(small_names):
        table[n] = (g_small[i], d_small[i], nm_small[i], nv_small[i])
    order = ["w_in", "norm_mix", "sgu_v_gain", "sgu_w_s", "sgu_b_s", "w_a", "attn_sink", "rel_bias", "w_b", "w_o", "norm_ffn",
             "w_gate", "w_up", "w_down", "norm_final"]
    outs = [loss, grad_x.reshape(1, s, d)]
    for part in range(4):
        outs += [table[n][part] for n in order]
    return tuple(outs)
```

```python
import math

import jax
import jax.numpy as jnp
import numpy as np
from jax import lax
from jax.experimental import pallas as pl
from jax.experimental.pallas import tpu as pltpu

F32 = jnp.float32
BF16 = jnp.bfloat16
I32 = jnp.int32
MESH = pl.DeviceIdType.MESH

EPS = 1e-6
NEG = -1e30
BLK = 128
HEAD_DIM = 128
N_KV_HEADS = 2
REL_BUCKETS = 32
REL_MAX_DIST = 128
N_CHIPS = 4
ADAM_LR, ADAM_B1, ADAM_B2, ADAM_EPS, ADAM_WD, ADAM_STEP = 0.001, 0.9, 0.999, 1e-08, 0.01, 10

LANES = 128
VMEM_CAP = 60 * 1024 * 1024

NN = (((1,), (0,)), ((), ()))
NT = (((1,), (1,)), ((), ()))
TN = (((0,), (0,)), ((), ()))
ANY = pl.BlockSpec(memory_space=pl.ANY)
HBM_SPEC = pl.BlockSpec(memory_space=pltpu.HBM)
SEM_SPEC = pl.BlockSpec(memory_space=pltpu.SEMAPHORE)
EFFECT = pltpu.SideEffectType.DATAFLOW_SIDE_EFFECTING


def _tile(n, cands):
    for t in cands:
        if n % t == 0:
            return t
    return n


PIN_BYTES = 64 * 1024


def _pin_hbm(a):
    big = hasattr(a, "dtype") and jnp.issubdtype(a.dtype, jnp.floating) and _nbytes(a.shape, a.dtype) >= PIN_BYTES
    return pltpu.with_memory_space_constraint(a, pltpu.HBM) if big else a


def _pallas(body, *, out_shape, **kw):
    def pin(o):
        big = isinstance(o, jax.ShapeDtypeStruct) and jnp.issubdtype(o.dtype, jnp.floating) and _nbytes(o.shape, o.dtype) >= PIN_BYTES
        return pltpu.HBM(o.shape, o.dtype) if big else o

    shapes = type(out_shape)(pin(o) for o in out_shape) if isinstance(out_shape, (list, tuple)) else pin(out_shape)
    call = pl.pallas_call(body, out_shape=shapes, **kw)
    return lambda *args: call(*[_pin_hbm(a) for a in args])


def _params(vmem_bytes=None, **kw):
    if vmem_bytes is not None:
        kw["vmem_limit_bytes"] = int(min(max(vmem_bytes, 32 * 1024 * 1024), VMEM_CAP))
    return pltpu.CompilerParams(**kw)


def _nbytes(shape, dtype):
    return int(np.prod(shape)) * jnp.dtype(dtype).itemsize


def _sigmoid(x):
    return 1.0 / (1.0 + jnp.exp(-x))


_GC = 0.7978845608028654
_GA = 0.044715


def _gelu(x):
    return 0.5 * x * (1.0 + jnp.tanh(_GC * (x + _GA * (x * x * x))))


def _gelu_grad(x):
    t = jnp.tanh(_GC * (x + _GA * (x * x * x)))
    return 0.5 * (1.0 + t) + 0.5 * x * (1.0 - t * t) * (_GC * (1.0 + 3.0 * _GA * (x * x)))


def _bf(v):
    return v if v.dtype == BF16 else v.astype(BF16)


def _mm(name, grid, ins, in_specs, out_shape, out_specs, pairs, n_acc, tile, nk, epilogue, vmem_bytes, after=(), lookahead=()):
    assert nk == 1
    n_in, n_out = len(ins) + len(after), len(out_shape)

    def body(*refs):
        in_refs, out_refs = refs[:n_in], refs[n_in : n_in + n_out]
        vals = [None] * n_acc
        for a_i, b_i, dn, acc_i in pairs:
            d = lax.dot_general(_bf(in_refs[a_i][...]), _bf(in_refs[b_i][...]), dn, preferred_element_type=F32)
            vals[acc_i] = d if vals[acc_i] is None else vals[acc_i] + d
        epilogue(in_refs, vals, out_refs, slice(None))

    if lookahead:
        ahead = pl.Buffered(2, use_lookahead=True)
        specs = [pl.BlockSpec(sp.block_shape, sp.index_map, pipeline_mode=ahead) if i in lookahead else sp for i, sp in enumerate(in_specs)]
        n_read = len(ins)

        def piped(*refs):
            def step(*blocks):
                body(*blocks[:n_read], *[None] * len(after), *blocks[n_read:])

            pltpu.emit_pipeline(step, grid=grid, in_specs=specs, out_specs=list(out_specs))(*refs[:n_read], *refs[n_in:])

        return _pallas(
            piped,
            name=name,
            in_specs=[ANY] * n_in,
            out_specs=[ANY] * n_out,
            out_shape=out_shape,
            compiler_params=_params(vmem_bytes),
        )(*ins, *after)

    return _pallas(
        body,
        name=name,
        grid=grid,
        in_specs=list(in_specs) + [ANY] * len(after),
        out_specs=out_specs,
        out_shape=out_shape,
        compiler_params=_params(vmem_bytes),
    )(*ins, *after)


def _put(ref, cs, v):
    ref[:, cs] = v.astype(ref.dtype)


def _mm_vmem(tiles):
    return sum(_nbytes(s, d) * c for s, d, c in tiles) + 4 * 1024 * 1024


def _rows8(v):
    r, d = v.shape
    return v.reshape(r // 8, 8, d).sum(axis=0)


def _rms_fwd(name, x, g, after=()):
    s, d = x.shape
    tm = _tile(s, (256, 128))

    def body(x_ref, g_ref, *rest):
        h_ref = rest[-1]
        xv = x_ref[...]
        r = lax.rsqrt(jnp.mean(xv * xv, axis=-1, keepdims=True) + EPS)
        h_ref[...] = ((xv * r) * g_ref[...]).astype(BF16)

    return _pallas(
        body,
        name=name,
        grid=(s // tm,),
        in_specs=[pl.BlockSpec((tm, d), lambda i: (i, 0)), pl.BlockSpec((1, d), lambda i: (0, 0))] + [ANY] * len(after),
        out_specs=pl.BlockSpec((tm, d), lambda i: (i, 0)),
        out_shape=jax.ShapeDtypeStruct((s, d), BF16),
    )(x, g, *after)


def _rms_bwd(name, x, g, dh, dres, after=()):
    s, d = x.shape
    tm = _tile(s, (256, 128))
    n = s // tm
    n_after = len(after)

    def body(x_ref, g_ref, dh_ref, dres_ref, *rest):
        dx_ref, dxb_ref, dg_ref, acc_ref = rest[n_after:]
        i = pl.program_id(0)
        xv = x_ref[...]
        r = lax.rsqrt(jnp.mean(xv * xv, axis=-1, keepdims=True) + EPS)
        xh = xv * r
        dhv = dh_ref[...]
        dxh = dhv * g_ref[...]
        dx = r * (dxh - xh * jnp.mean(dxh * xh, axis=-1, keepdims=True)) + dres_ref[...]
        dx_ref[...] = dx
        dxb_ref[...] = dx.astype(BF16)
        part = _rows8(dhv * xh)

        @pl.when(i == 0)
        def _():
            acc_ref[...] = part

        @pl.when(i > 0)
        def _():
            acc_ref[...] += part

        @pl.when(i == n - 1)
        def _():
            dg_ref[...] = jnp.sum(acc_ref[...], axis=0, keepdims=True)

    row = pl.BlockSpec((tm, d), lambda i: (i, 0))
    vec = pl.BlockSpec((1, d), lambda i: (0, 0))
    return _pallas(
        body,
        name=name,
        grid=(n,),
        in_specs=[row, vec, row, row] + [ANY] * n_after,
        out_specs=[row, row, vec],
        out_shape=[jax.ShapeDtypeStruct((s, d), F32), jax.ShapeDtypeStruct((s, d), BF16), jax.ShapeDtypeStruct((1, d), F32)],
        scratch_shapes=[pltpu.VMEM((8, d), F32)],
    )(x, g, dh, dres, *after)


def _head(x3, g, target):
    s, d = x3.shape
    tm = _tile(s, (256, 128))
    n = s // tm

    def body(x_ref, g_ref, t_ref, dx_ref, dxb_ref, dg_ref, loss_ref, acc_g, acc_l):
        i = pl.program_id(0)
        xv = x_ref[...]
        gv = g_ref[...]
        r = lax.rsqrt(jnp.mean(xv * xv, axis=-1, keepdims=True) + EPS)
        xh = xv * r
        e = xh * gv - t_ref[...]
        dy = e * (1.0 / d)
        dxh = dy * gv
        dx = r * (dxh - xh * jnp.mean(dxh * xh, axis=-1, keepdims=True))
        dx_ref[...] = dx
        dxb_ref[...] = dx.astype(BF16)
        pg = _rows8(dy * xh)
        plo = _rows8(e * e)

        @pl.when(i == 0)
        def _():
            acc_g[...] = pg
            acc_l[...] = plo

        @pl.when(i > 0)
        def _():
            acc_g[...] += pg
            acc_l[...] += plo

        @pl.when(i == n - 1)
        def _():
            dg_ref[...] = jnp.sum(acc_g[...], axis=0, keepdims=True)
            loss_ref[...] = jnp.full((1, LANES), (0.5 / d) * jnp.sum(acc_l[...]), F32)

    row = pl.BlockSpec((tm, d), lambda i: (i, 0))
    vec = pl.BlockSpec((1, d), lambda i: (0, 0))
    return _pallas(
        body,
        name="head",
        grid=(n,),
        in_specs=[row, vec, row],
        out_specs=[row, row, vec, pl.BlockSpec((1, LANES), lambda i: (0, 0))],
        out_shape=[
            jax.ShapeDtypeStruct((s, d), F32),
            jax.ShapeDtypeStruct((s, d), BF16),
            jax.ShapeDtypeStruct((1, d), F32),
            jax.ShapeDtypeStruct((1, LANES), F32),
        ],
        scratch_shapes=[pltpu.VMEM((8, d), F32), pltpu.VMEM((8, d), F32)],
    )(x3, g, target)


def _sgu_fwd(z, gain, ws_b, b_col, w_sgu, after=()):
    s = z.shape[0]
    groups = ws_b.shape[0]

    def body(zu_ref, zv_ref, gain_ref, ws_ref, b_ref, *rest):
        a_ref = rest[-1]
        vv = _gelu(zv_ref[...].astype(F32))
        r = lax.rsqrt(jnp.mean(vv * vv, axis=-1, keepdims=True) + EPS)
        vn = ((vv * r) * gain_ref[...]).astype(BF16)
        u = _gelu(zu_ref[...].astype(F32))
        for g in range(groups):
            sl = slice(g * BLK, (g + 1) * BLK)
            mixed = jnp.dot(ws_ref[g], vn[:, sl], preferred_element_type=F32) + b_ref[g]
            a_ref[:, sl] = (u[:, sl] * mixed).astype(BF16)

    return _pallas(
        body,
        name="sgu_fwd",
        grid=(s // BLK,),
        in_specs=[
            pl.BlockSpec((BLK, w_sgu), lambda c: (c, 0)),
            pl.BlockSpec((BLK, w_sgu), lambda c: (c, 1)),
            pl.BlockSpec((1, w_sgu), lambda c: (0, 0)),
            pl.BlockSpec((groups, BLK, BLK), lambda c: (0, 0, 0)),
            pl.BlockSpec((groups, BLK, 1), lambda c: (0, 0, 0)),
        ]
        + [ANY] * len(after),
        out_specs=pl.BlockSpec((BLK, w_sgu), lambda c: (c, 0)),
        out_shape=jax.ShapeDtypeStruct((s, w_sgu), BF16),
    )(z, z, gain, ws_b, b_col, *after)


def _sgu_bwd(z, da, gain, ws_b, wst_b, b_col, w_sgu, dz, after=()):
    s = z.shape[0]
    groups = ws_b.shape[0]
    n = s // BLK
    n_skip = 1 + len(after)

    def body(zu_ref, zv_ref, da_ref, gain_ref, ws_ref, wst_ref, b_ref, *rest):
        dz_ref, dws_ref, dbs_ref, dgain_ref, acc_gain, vv_s, gv_s, dxh_s = rest[n_skip:]
        c = pl.program_id(0)
        cols = [slice(g * BLK, (g + 1) * BLK) for g in range(groups)]

        ss = jnp.zeros((BLK, 1), F32)
        for sl in cols:
            zv = zv_ref[:, sl].astype(F32)
            vv = _gelu(zv)
            vv_s[:, sl] = vv
            gv_s[:, sl] = _gelu_grad(zv)
            ss = ss + jnp.sum(vv * vv, axis=-1, keepdims=True)
        r = lax.rsqrt(ss * (1.0 / w_sgu) + EPS)

        dot_dx = jnp.zeros((BLK, 1), F32)
        for g, sl in enumerate(cols):
            gain_g = gain_ref[:, sl]
            xh = vv_s[:, sl] * r
            vn = (xh * gain_g).astype(BF16)
            zu = zu_ref[:, sl].astype(F32)
            dav = da_ref[:, sl].astype(F32)
            dmix = dav * _gelu(zu)
            dmix_b = dmix.astype(BF16)
            mixed = jnp.dot(ws_ref[g], vn, preferred_element_type=F32) + b_ref[g]
            dz_ref[:, sl] = (dav * mixed * _gelu_grad(zu)).astype(BF16)
            dvn = jnp.dot(wst_ref[g], dmix_b, preferred_element_type=F32)
            dws_g = lax.dot_general(dmix_b, vn, NT, preferred_element_type=F32)
            dbs_g = jnp.sum(dmix, axis=1, keepdims=True)
            pg = _rows8(dvn * xh)

            @pl.when(c == 0)
            def _():
                dws_ref[g] = dws_g
                dbs_ref[g] = dbs_g
                acc_gain[:, sl] = pg

            @pl.when(c > 0)
            def _():
                dws_ref[g] += dws_g
                dbs_ref[g] += dbs_g
                acc_gain[:, sl] += pg

            dxh = dvn * gain_g
            dxh_s[:, sl] = dxh
            dot_dx = dot_dx + jnp.sum(dxh * xh, axis=-1, keepdims=True)

        mean_dx = dot_dx * (1.0 / w_sgu)
        for g, sl in enumerate(cols):
            dvv = r * (dxh_s[:, sl] - (vv_s[:, sl] * r) * mean_dx)
            dz_ref[:, w_sgu + g * BLK : w_sgu + (g + 1) * BLK] = (dvv * gv_s[:, sl]).astype(BF16)

        @pl.when(c == n - 1)
        def _():
            dgain_ref[...] = jnp.sum(acc_gain[...], axis=0, keepdims=True)

    full3 = pl.BlockSpec((groups, BLK, BLK), lambda c: (0, 0, 0))
    col3 = pl.BlockSpec((groups, BLK, 1), lambda c: (0, 0, 0))
    vec = pl.BlockSpec((1, w_sgu), lambda c: (0, 0))
    return _pallas(
        body,
        name="sgu_bwd",
        grid=(n,),
        in_specs=[
            pl.BlockSpec((BLK, w_sgu), lambda c: (c, 0)),
            pl.BlockSpec((BLK, w_sgu), lambda c: (c, 1)),
            pl.BlockSpec((BLK, w_sgu), lambda c: (c, 0)),
            vec,
            full3,
            full3,
            col3,
            ANY,
        ]
        + [ANY] * len(after),
        out_specs=[pl.BlockSpec((BLK, 2 * w_sgu), lambda c: (c, 0)), full3, col3, vec],
        out_shape=[
            jax.ShapeDtypeStruct(dz.shape, BF16),
            jax.ShapeDtypeStruct((groups, BLK, BLK), F32),
            jax.ShapeDtypeStruct((groups, BLK, 1), F32),
            jax.ShapeDtypeStruct((1, w_sgu), F32),
        ],
        scratch_shapes=[pltpu.VMEM((8, w_sgu), F32)] + [pltpu.VMEM((BLK, w_sgu), F32)] * 3,
        input_output_aliases={7: 0},
    )(z, z, da, gain, ws_b, wst_b, b_col, dz, *after)


def _attn_softmax(sink_ref, q_ref, k_ref, v_ref, bias_ref, s_len, grp):
    kv = pl.program_id(0)
    n = pl.program_id(1)
    start = pl.multiple_of(n * BLK, BLK)
    kb = k_ref[pl.ds(start, 3 * BLK), :]
    vb = v_ref[pl.ds(start, 3 * BLK), :]
    qv = q_ref[...]
    qs = jnp.concatenate([qv[:, g * HEAD_DIM : (g + 1) * HEAD_DIM] for g in range(grp)], axis=0).astype(BF16)
    sc = lax.dot_general(qs, kb, NT, preferred_element_type=F32) * (HEAD_DIM**-0.5)
    sc = sc + bias_ref[...].reshape(grp * BLK, 3 * BLK)
    kpos = start + lax.broadcasted_iota(I32, (1, 3 * BLK), 1) - BLK
    sc = jnp.where((kpos >= 0) & (kpos < s_len), sc, NEG)
    sink = jnp.concatenate([jnp.full((BLK, 1), sink_ref[kv * grp + g], F32) for g in range(grp)], axis=0)
    m = jnp.maximum(jnp.max(sc, axis=-1, keepdims=True), sink)
    p = jnp.exp(sc - m)
    esink = jnp.exp(sink - m)
    den = jnp.sum(p, axis=-1, keepdims=True) + esink
    return start, qs, kb, vb, p / den, esink / den


def _attn_specs(s, grp, q_blk0):
    qw = grp * HEAD_DIM
    return [
        pl.BlockSpec(memory_space=pltpu.SMEM),
        pl.BlockSpec((BLK, qw), lambda kv, n: (n, q_blk0 + kv)),
        pl.BlockSpec((s + 2 * BLK, HEAD_DIM), lambda kv, n: (0, kv)),
        pl.BlockSpec((s + 2 * BLK, HEAD_DIM), lambda kv, n: (0, kv)),
        pl.BlockSpec((grp, BLK, 3 * BLK), lambda kv, n: (kv, 0, 0)),
    ]


def _attn_fwd(sink, z, k_pad, v_pad, bias_tab, grp, q_blk0):
    s = z.shape[0]
    qw = grp * HEAD_DIM

    def body(sink_ref, q_ref, k_ref, v_ref, bias_ref, o_ref):
        _, _, _, vb, pn, _ = _attn_softmax(sink_ref, q_ref, k_ref, v_ref, bias_ref, s, grp)
        o = jnp.dot(pn.astype(BF16), vb, preferred_element_type=F32)
        for g in range(grp):
            o_ref[:, g * HEAD_DIM : (g + 1) * HEAD_DIM] = o[g * BLK : (g + 1) * BLK].astype(BF16)

    return _pallas(
        body,
        name="attn_fwd",
        grid=(N_KV_HEADS, s // BLK),
        in_specs=_attn_specs(s, grp, q_blk0),
        out_specs=pl.BlockSpec((BLK, qw), lambda kv, n: (n, kv)),
        out_shape=jax.ShapeDtypeStruct((s, N_KV_HEADS * qw), BF16),
    )(sink, z, k_pad, v_pad, bias_tab)


def _attn_bwd(sink, z, k_pad, v_pad, bias_tab, dout, dz, grp, q_blk0):
    s = z.shape[0]
    qw = grp * HEAD_DIM
    nb = s // BLK
    heads = N_KV_HEADS * grp

    def body(sink_ref, q_ref, k_ref, v_ref, bias_ref, do_ref, dz_in, dq_ref, dk_ref, dv_ref, dbias_ref, dsink_ref, dk_acc, dv_acc):
        del dz_in
        kv = pl.program_id(0)
        n = pl.program_id(1)
        start, qs, kb, vb, pn, psink = _attn_softmax(sink_ref, q_ref, k_ref, v_ref, bias_ref, s, grp)
        dov = do_ref[...]
        dos = jnp.concatenate([dov[:, g * HEAD_DIM : (g + 1) * HEAD_DIM] for g in range(grp)], axis=0)
        dp = lax.dot_general(dos, vb, NT, preferred_element_type=F32)
        dvb = lax.dot_general(pn.astype(BF16), dos, TN, preferred_element_type=F32)
        delta = jnp.sum(pn * dp, axis=-1, keepdims=True)
        ds = pn * (dp - delta)
        dsb = (ds * (HEAD_DIM**-0.5)).astype(BF16)
        dq = jnp.dot(dsb, kb, preferred_element_type=F32)
        dkb = lax.dot_general(dsb, qs, TN, preferred_element_type=F32)
        for g in range(grp):
            dq_ref[:, g * HEAD_DIM : (g + 1) * HEAD_DIM] = dq[g * BLK : (g + 1) * BLK].astype(BF16)

        @pl.when(n == 0)
        def _():
            dk_acc[...] = jnp.zeros_like(dk_acc)
            dv_acc[...] = jnp.zeros_like(dv_acc)
            dbias_ref[...] = jnp.zeros_like(dbias_ref)

        @pl.when((n == 0) & (kv == 0))
        def _():
            dsink_ref[...] = jnp.zeros_like(dsink_ref)

        dk_acc[pl.ds(start, 3 * BLK), :] += dkb
        dv_acc[pl.ds(start, 3 * BLK), :] += dvb
        dbias_ref[...] += ds.reshape(grp, BLK, 3 * BLK)
        row = lax.broadcasted_iota(I32, (heads, LANES), 0)
        sd = psink * delta
        upd = jnp.zeros((heads, LANES), F32)
        for g in range(grp):
            upd = jnp.where(row == kv * grp + g, -jnp.sum(sd[g * BLK : (g + 1) * BLK]), upd)
        dsink_ref[...] += upd

        @pl.when(n == nb - 1)
        def _():
            dk_ref[...] = dk_acc[...]
            dv_ref[...] = dv_acc[...]

    pad_spec = pl.BlockSpec((s + 2 * BLK, HEAD_DIM), lambda kv, n: (0, kv))
    kvw = N_KV_HEADS * HEAD_DIM
    return _pallas(
        body,
        name="attn_bwd",
        grid=(N_KV_HEADS, nb),
        in_specs=_attn_specs(s, grp, q_blk0) + [pl.BlockSpec((BLK, qw), lambda kv, n: (n, kv)), ANY],
        out_specs=[
            pl.BlockSpec((BLK, qw), lambda kv, n: (n, q_blk0 + kv)),
            pad_spec,
            pad_spec,
            pl.BlockSpec((grp, BLK, 3 * BLK), lambda kv, n: (kv, 0, 0)),
            pl.BlockSpec((heads, LANES), lambda kv, n: (0, 0)),
        ],
        out_shape=[
            jax.ShapeDtypeStruct(dz.shape, BF16),
            jax.ShapeDtypeStruct((s + 2 * BLK, kvw), F32),
            jax.ShapeDtypeStruct((s + 2 * BLK, kvw), F32),
            jax.ShapeDtypeStruct((heads, BLK, 3 * BLK), F32),
            jax.ShapeDtypeStruct((heads, LANES), F32),
        ],
        scratch_shapes=[pltpu.VMEM((s + 2 * BLK, HEAD_DIM), F32), pltpu.VMEM((s + 2 * BLK, HEAD_DIM), F32)],
        input_output_aliases={6: 0},
    )(sink, z, k_pad, v_pad, bias_tab, dout, dz)


def _dkv_to_dz(dk_pad, dv_pad, dz, blk_idx):
    s = dz.shape[0]
    kvw = dk_pad.shape[1]

    def body(dk_ref, dv_ref, dz_in, out_ref):
        del dz_in
        out_ref[:, :kvw] = dk_ref[...].astype(BF16)
        out_ref[:, kvw:] = dv_ref[...].astype(BF16)

    src = pl.BlockSpec((BLK, kvw), lambda i: (i + 1, 0))
    return _pallas(
        body,
        name="dkv_to_dz",
        grid=(s // BLK,),
        in_specs=[src, src, ANY],
        out_specs=pl.BlockSpec((BLK, 2 * kvw), lambda i: (i, blk_idx)),
        out_shape=jax.ShapeDtypeStruct(dz.shape, BF16),
        input_output_aliases={2: 0},
    )(dk_pad, dv_pad, dz)


def _relbias_bwd(dbias_tab, bucket):
    heads = dbias_tab.shape[0]

    def body(dt_ref, bk_ref, out_ref):
        lane = lax.broadcasted_iota(I32, (1, LANES), 1)
        bk = bk_ref[...]
        rows = []
        for h in range(heads):
            dt = dt_ref[h]
            acc = jnp.zeros((1, LANES), F32)
            for b in range(REL_BUCKETS):
                acc = jnp.where(lane == b, jnp.sum(jnp.where(bk == b, dt, 0.0)), acc)
            rows.append(acc)
        out_ref[...] = jnp.concatenate(rows, axis=0)

    return _pallas(body, name="relbias_bwd", out_shape=jax.ShapeDtypeStruct((heads, LANES), F32))(dbias_tab, bucket)


def _t5_bucket(rel):
    nb = REL_BUCKETS // 2
    ret = jnp.where(rel > 0, nb, 0)
    n = jnp.abs(rel)
    max_exact = nb // 2
    nf = jnp.maximum(n, 1).astype(F32)
    large = max_exact + (jnp.log(nf / max_exact) / math.log(REL_MAX_DIST / max_exact) * (nb - max_exact)).astype(I32)
    large = jnp.minimum(large, nb - 1)
    return ret + jnp.where(n < max_exact, n, large)


def _band_tables(rel_bias):
    qi = jnp.arange(BLK)[:, None]
    kj = jnp.arange(3 * BLK)[None, :]
    rel = kj - BLK - qi
    bucket = _t5_bucket(rel).astype(I32)
    heads = rel_bias.shape[1]
    masked = jnp.where(jnp.abs(rel) <= BLK, bucket, -1)

    def body(rb_ref, bk_ref, out_ref):
        bk = bk_ref[...]
        for h in range(heads):
            tab = jnp.full(bk.shape, NEG, F32)
            for b in range(REL_BUCKETS):
                tab = jnp.where(bk == b, rb_ref[b, h], tab)
            out_ref[h] = tab

    bias_tab = _pallas(
        body,
        name="bias_table",
        in_specs=[pl.BlockSpec(memory_space=pltpu.SMEM), pl.BlockSpec(memory_space=pltpu.VMEM)],
        out_specs=pl.BlockSpec(memory_space=pltpu.VMEM),
        out_shape=jax.ShapeDtypeStruct((heads, BLK, 3 * BLK), F32),
    )(rel_bias.astype(F32), masked)
    return bias_tab, bucket


EW_BLOCK_ELEMS = 512 * 1024


def _ew_tiles(shape, elems=EW_BLOCK_ELEMS // 2):
    r, c = shape
    tn = c if c <= 2048 else _tile(c, (2048, 1920, 1536, 1408, 1024, 512))
    tm = max([t for t in range(16, r + 1, 16) if r % t == 0 and t * tn <= elems] or [8])
    return tm, tn


def _cast_into_full(name, qidx, w, kind, after=()):
    r, c = w.shape
    tm, tn = _ew_tiles(w.shape, EW_BLOCK_ELEMS)
    nbi, nbj = r // tm, c // tn
    if kind == "col":
        full, out_spec = (r, c * N_CHIPS), pl.BlockSpec((tm, tn), lambda i, j, q: (i, q[0] * nbj + j))
    else:
        full, out_spec = (r * N_CHIPS, c), pl.BlockSpec((tm, tn), lambda i, j, q: (q[0] * nbi + i, j))

    def body(q_ref, w_ref, *rest):
        del q_ref
        rest[-1][...] = w_ref[...].astype(BF16)

    return _pallas(
        body,
        name=name,
        grid_spec=pltpu.PrefetchScalarGridSpec(
            num_scalar_prefetch=1,
            grid=(nbi, nbj),
            in_specs=[pl.BlockSpec((tm, tn), lambda i, j, q: (i, j))] + [ANY] * len(after),
            out_specs=out_spec,
        ),
        out_shape=jax.ShapeDtypeStruct(full, BF16),
    )(qidx, w, *after)


def _adamw(name, w, g, m, v, after=()):
    tm, tn = _ew_tiles(w.shape, EW_BLOCK_ELEMS)
    if _nbytes(w.shape, F32) <= 1024 * 1024:
        tm, tn = w.shape
    spec = pl.BlockSpec((tm, tn), lambda i, j: (i, j))
    n_after = len(after)

    def body(w_ref, g_ref, m_ref, v_ref, *rest):
        d_ref, nm_ref, nv_ref, g_out_ref = rest[n_after:]
        gv = g_ref[...]
        g_out_ref[...] = gv
        nm = ADAM_B1 * m_ref[...] + (1.0 - ADAM_B1) * gv
        nv = ADAM_B2 * v_ref[...] + (1.0 - ADAM_B2) * (gv * gv)
        m_hat = nm / (1.0 - ADAM_B1**ADAM_STEP)
        v_hat = nv / (1.0 - ADAM_B2**ADAM_STEP)
        d_ref[...] = -ADAM_LR * (m_hat / (jnp.sqrt(v_hat) + ADAM_EPS) + ADAM_WD * w_ref[...])
        nm_ref[...] = nm
        nv_ref[...] = nv

    out = jax.ShapeDtypeStruct(w.shape, F32)
    return _pallas(
        body, name=name, grid=(w.shape[0] // tm, w.shape[1] // tn), in_specs=[spec] * 4 + [ANY] * n_after,
        out_specs=[spec] * 4, out_shape=[out, out, out, out],
        compiler_params=_params(_mm_vmem([((tm, tn), F32, 24)])),
    )(w, g, m, v, *after)


def _pair_add(name, cidx, g_full, r_sib, kind):
    hr, hc = r_sib.shape
    tm, tn = _ew_tiles((hr, hc), 2 * EW_BLOCK_ELEMS)
    nbi, nbj = hr // tm, hc // tn
    if kind == "col":
        g_spec = pl.BlockSpec((tm, tn), lambda i, j, c: (c[0] * nbi + i, j))
    else:
        g_spec = pl.BlockSpec((tm, tn), lambda i, j, c: (i, c[0] * nbj + j))
    spec = pl.BlockSpec((tm, tn), lambda i, j, c: (i, j))

    def body(c_ref, g_ref, r_ref, o_ref):
        del c_ref
        o_ref[...] = (g_ref[...].astype(F32) + r_ref[...].astype(F32)).astype(BF16)

    return _pallas(
        body,
        name=name,
        grid_spec=pltpu.PrefetchScalarGridSpec(num_scalar_prefetch=1, grid=(nbi, nbj), in_specs=[g_spec, spec], out_specs=spec),
        out_shape=jax.ShapeDtypeStruct((hr, hc), BF16),
        compiler_params=_params(_mm_vmem([((tm, tn), BF16, 6), ((tm, tn), F32, 3)])),
    )(cidx, g_full, r_sib)


def _chip_sum(name, qidx, c_half, r_ici, kind):
    _, pr, pc = r_ici.shape
    tm, tn = _ew_tiles((pr, pc), 2 * EW_BLOCK_ELEMS)
    nbi, nbj = pr // tm, pc // tn
    if kind == "col":
        own_spec = pl.BlockSpec((tm, tn), lambda i, j, q: (i, q[0] * nbj + j))
        full, out_spec = (2 * pr, pc), pl.BlockSpec((tm, tn), lambda i, j, q: (q[1] * nbi + i, j))
    else:
        own_spec = pl.BlockSpec((tm, tn), lambda i, j, q: (q[0] * nbi + i, j))
        full, out_spec = (pr, 2 * pc), pl.BlockSpec((tm, tn), lambda i, j, q: (i, q[1] * nbj + j))

    def body(q_ref, own_ref, r_ref, o_ref):
        q = q_ref[0]
        own = own_ref[...].astype(F32)
        recv = [r_ref[r].astype(F32) for r in range(3)]
        total = None
        for chip in range(N_CHIPS):
            d = chip ^ q
            term = jnp.where(d == 0, own, jnp.where(d == 2, recv[0], jnp.where(d == 1, recv[1], recv[2])))
            total = term if total is None else total + term
        o_ref[...] = total

    return _pallas(
        body,
        name=name,
        grid_spec=pltpu.PrefetchScalarGridSpec(
            num_scalar_prefetch=1,
            grid=(nbi, nbj),
            in_specs=[own_spec, pl.BlockSpec((3, tm, tn), lambda i, j, q: (0, i, j))],
            out_specs=out_spec,
        ),
        out_shape=jax.ShapeDtypeStruct(full, F32),
        compiler_params=_params(_mm_vmem([((tm, tn), BF16, 8), ((tm, tn), F32, 6)])),
    )(qidx, c_half, r_ici)


_REL_MASK = (2, 1, 3)


def _place():
    x, y, c = lax.axis_index("x"), lax.axis_index("y"), lax.axis_index("c")
    chips = [(1 - x, y), (x, 1 - y), (1 - x, 1 - y)]
    return x, y, c, 2 * x + y, chips


def _shard_view(ref, kind, chip):
    if kind == "col":
        w = ref.shape[1] // N_CHIPS
        return ref.at[:, pl.ds(pl.multiple_of(chip * w, LANES), w)]
    h = ref.shape[0] // N_CHIPS
    return ref.at[pl.ds(pl.multiple_of(chip * h, 16), h), :]


def _row_half(ref, half):
    h = ref.shape[0] // 2
    return ref.at[pl.ds(pl.multiple_of(half * h, 16), h), :]


def _pair_half(ref, kind, half):
    if kind == "col":
        return _row_half(ref, half)
    w = ref.shape[1] // 2
    return ref.at[:, pl.ds(pl.multiple_of(half * w, LANES), w)]


def _remote(src, dst, send_sem, recv_sem, dev):
    return pltpu.make_async_remote_copy(src_ref=src, dst_ref=dst, send_sem=send_sem, recv_sem=recv_sem, device_id=dev, device_id_type=MESH)


def _hbm(a):
    return pltpu.with_memory_space_constraint(a, pltpu.HBM)


def _gather_start(name, fulls, kinds, rels=(0, 1, 2), after=()):
    n_w = len(fulls)

    def body(*refs):
        g = refs[:n_w]
        send_sem, recv_sem = refs[n_w + len(after)], refs[n_w + len(after) + 1]
        token = refs[-1]
        _, _, c, q, chips = _place()
        for w in range(n_w):
            mine = _row_half(_shard_view(g[w], kinds[w], q), c)
            for r in rels if isinstance(rels, tuple) else rels[w]:
                _remote(mine, mine, send_sem.at[3 * w + r], recv_sem.at[3 * w + r], (*chips[r], c)).start()
        token[...] = jnp.zeros_like(token)

    res = _pallas(
        body,
        name=name,
        out_shape=(
            pltpu.SemaphoreType.DMA((3 * n_w,)),
            pltpu.SemaphoreType.DMA((3 * n_w,)),
            *[pltpu.HBM(f.shape, f.dtype) for f in fulls],
            jax.ShapeDtypeStruct((8, LANES), F32),
        ),
        in_specs=[HBM_SPEC] * n_w + [ANY] * len(after),
        out_specs=(SEM_SPEC, SEM_SPEC, *[HBM_SPEC] * n_w, pl.BlockSpec(memory_space=pltpu.VMEM)),
        input_output_aliases={w: w + 2 for w in range(n_w)},
        compiler_params=pltpu.CompilerParams(has_side_effects=EFFECT),
    )(*[_hbm(f) for f in fulls], *after)
    return res[0], res[1], list(res[2 : 2 + n_w]), res[-1]


def _relay_copies(kinds, waiting):
    def copies(refs, send_sem, recv_sem):
        _, _, c, q, chips = _place()
        out = []
        for i, kind in enumerate(kinds):
            for k, (src_rel, dst_rel) in enumerate(((0, 1), (1, 0))):
                held = _row_half(_row_half(_shard_view(refs[i], kind, q ^ _REL_MASK[src_rel]), c), k)
                far = _row_half(_row_half(_shard_view(refs[i], kind, q ^ _REL_MASK[2]), c), k)
                dst = far if waiting else held
                out.append(_remote(held, dst, send_sem.at[2 * i + k], recv_sem.at[2 * i + k], (*chips[dst_rel], c)))
        return out

    return copies


def _forward_copies(kinds, waiting, rels=(0, 1, 2), sem0=0):
    def copies(refs, send_sem, recv_sem):
        x, y, c, q, _ = _place()
        out = []
        for i, kind in enumerate(kinds):
            for k, r in enumerate(rels):
                quarter = _shard_view(refs[i], kind, q ^ _REL_MASK[r])
                landed = _row_half(quarter, c)
                dst = _row_half(quarter, 1 - c) if waiting else landed
                sem = sem0 + len(rels) * i + k
                out.append(_remote(landed, dst, send_sem.at[sem], recv_sem.at[sem], (x, y, 1 - c)))
        return out

    return copies


def _relay_and_forward_copies(kinds, waiting):
    forward = _forward_copies(kinds, waiting, rels=(0, 1), sem0=2 * len(kinds))
    relay = _relay_copies(kinds, waiting)
    return lambda refs, send_sem, recv_sem: forward(refs, send_sem, recv_sem) + relay(refs, send_sem, recv_sem)


def _gather_wait(name, fulls, kinds, w_ids, send_sem, recv_sem, after, rels=(0, 1, 2)):
    n = len(fulls)

    def body(*refs):
        g = refs[:n]
        s_sem, r_sem = refs[n], refs[n + 1]
        x, y, c, q, _ = _place()
        for i, w in enumerate(w_ids):
            mine = _row_half(_shard_view(g[i], kinds[i], q), c)
            for r in rels:
                landed = _row_half(_shard_view(g[i], kinds[i], q ^ _REL_MASK[r]), c)
                cp = _remote(mine, landed, s_sem.at[3 * w + r], r_sem.at[3 * w + r], (x, y, 1 - c))
                cp.wait_send()
                cp.wait_recv()

    res = _pallas(
        body,
        name=name,
        out_shape=[pltpu.HBM(f.shape, f.dtype) for f in fulls],
        in_specs=[HBM_SPEC] * n + [SEM_SPEC, SEM_SPEC, ANY],
        out_specs=[HBM_SPEC] * n,
        input_output_aliases={i: i for i in range(n)},
        compiler_params=pltpu.CompilerParams(has_side_effects=EFFECT),
    )(*fulls, send_sem, recv_sem, after)
    return list(res)


def _gather_forward(name, fulls, kinds, rels=(0, 1, 2)):
    n = len(fulls)

    def body(*refs):
        g = refs[n : 2 * n]
        send, recv = refs[2 * n :]
        _sibling_handshake()
        x, y, c, q, _ = _place()
        sib = (x, y, 1 - c)
        cps = []
        for i in range(n):
            for r in rels:
                landed = _row_half(_shard_view(g[i], kinds[i], q ^ _REL_MASK[r]), c)
                cps.append(_remote(landed, landed, send.at[i, r], recv.at[i, r], sib))
        for cp in cps:
            cp.start()
        for i in range(n):
            for r in rels:
                other = _row_half(_shard_view(g[i], kinds[i], q ^ _REL_MASK[r]), 1 - c)
                _remote(other, other, send.at[i, r], recv.at[i, r], sib).wait_recv()
        for cp in cps:
            cp.wait_send()

    res = _pallas(
        body,
        name=name,
        in_specs=[ANY] * n,
        out_specs=[ANY] * n,
        out_shape=[jax.ShapeDtypeStruct(f.shape, f.dtype) for f in fulls],
        scratch_shapes=[pltpu.SemaphoreType.DMA((n, 3)), pltpu.SemaphoreType.DMA((n, 3))],
        input_output_aliases={i: i for i in range(n)},
        compiler_params=pltpu.CompilerParams(collective_id=SIBLING_BARRIER_ID),
    )(*fulls)
    return list(res)


SIBLING_BARRIER_ID = 1


def _sibling_handshake():
    sib = (lax.axis_index("x"), lax.axis_index("y"), 1 - lax.axis_index("c"))
    barrier = pltpu.get_barrier_semaphore()
    pl.semaphore_signal(barrier, inc=1, device_id=sib, device_id_type=MESH)
    pl.semaphore_wait(barrier, 1)


def _split_start(name, bufs, n_sems, copies, sibling_only=False):
    n = len(bufs)

    def body(*refs):
        if sibling_only:
            _sibling_handshake()
        for cp in copies(refs[:n], refs[n], refs[n + 1]):
            cp.start()

    extra = {"collective_id": SIBLING_BARRIER_ID} if sibling_only else {}

    res = _pallas(
        body,
        name=name,
        out_shape=(
            pltpu.SemaphoreType.DMA((n_sems,)),
            pltpu.SemaphoreType.DMA((n_sems,)),
            *[pltpu.HBM(b.shape, b.dtype) for b in bufs],
        ),
        in_specs=[HBM_SPEC] * n,
        out_specs=(SEM_SPEC, SEM_SPEC, *[HBM_SPEC] * n),
        input_output_aliases={i: i + 2 for i in range(n)},
        compiler_params=pltpu.CompilerParams(has_side_effects=EFFECT, **extra),
    )(*[_hbm(b) for b in bufs])
    return res[0], res[1], list(res[2:])


def _split_wait(name, bufs, send_sem, recv_sem, copies, after):
    n = len(bufs)

    def body(*refs):
        for cp in copies(refs[:n], refs[n], refs[n + 1]):
            cp.wait_send()
            cp.wait_recv()

    res = _pallas(
        body,
        name=name,
        out_shape=[pltpu.HBM(b.shape, b.dtype) for b in bufs],
        in_specs=[HBM_SPEC] * n + [SEM_SPEC, SEM_SPEC, ANY],
        out_specs=[HBM_SPEC] * n,
        input_output_aliases={i: i for i in range(n)},
        compiler_params=pltpu.CompilerParams(has_side_effects=EFFECT),
    )(*bufs, send_sem, recv_sem, after)
    return list(res)


def _pair_exchange_copies(kinds):
    n = len(kinds)

    def copies(refs, send_sem, recv_sem):
        x, y, c, _, _ = _place()
        return [
            _remote(_pair_half(refs[w], kinds[w], 1 - c), refs[n + w], send_sem.at[w], recv_sem.at[w], (x, y, 1 - c))
            for w in range(n)
        ]

    return copies


def _pair_share_copies(kinds, waiting):
    def copies(refs, send_sem, recv_sem):
        x, y, c, _, _ = _place()
        out = []
        for w, kind in enumerate(kinds):
            mine = _pair_half(refs[w], kind, c)
            dst = _pair_half(refs[w], kind, 1 - c) if waiting else mine
            out.append(_remote(mine, dst, send_sem.at[w], recv_sem.at[w], (x, y, 1 - c)))
        return out

    return copies


def _piece_shape(half_shape, kind):
    r, c = half_shape
    return (3, r, c // N_CHIPS) if kind == "col" else (3, r // N_CHIPS, c)


def _chip_send_start(name, halves, kinds):
    n = len(halves)
    lands = [lax.empty(_piece_shape(h.shape, k), BF16) for h, k in zip(halves, kinds)]

    def body(*refs):
        h, land = refs[:n], refs[n : 2 * n]
        send_sem, recv_sem = refs[2 * n], refs[2 * n + 1]
        _, _, c, q, chips = _place()
        for i in range(n):
            for r, chip in enumerate(chips):
                piece = _shard_view(h[i], kinds[i], q ^ _REL_MASK[r])
                _remote(piece, land[i].at[r], send_sem.at[3 * i + r], recv_sem.at[3 * i + r], (*chip, c)).start()

    res = _pallas(
        body,
        name=name,
        out_shape=(
            pltpu.SemaphoreType.DMA((3 * n,)),
            pltpu.SemaphoreType.DMA((3 * n,)),
            *[pltpu.HBM(a.shape, a.dtype) for a in halves],
            *[pltpu.HBM(a.shape, a.dtype) for a in lands],
        ),
        in_specs=[HBM_SPEC] * (2 * n),
        out_specs=(SEM_SPEC, SEM_SPEC, *[HBM_SPEC] * (2 * n)),
        input_output_aliases={i: i + 2 for i in range(2 * n)},
        compiler_params=pltpu.CompilerParams(has_side_effects=EFFECT),
    )(*[_hbm(a) for a in halves], *[_hbm(a) for a in lands])
    return res[0], res[1], list(res[2 : 2 + n]), list(res[2 + n :])


def _chip_send_wait(name, halves, lands, kinds, send_sem, recv_sem, after):
    n = len(halves)

    def body(*refs):
        h, land = refs[:n], refs[n : 2 * n]
        s_sem, r_sem = refs[2 * n], refs[2 * n + 1]
        x, y, c, q, _ = _place()
        for i in range(n):
            for r in range(3):
                piece = _shard_view(h[i], kinds[i], q ^ _REL_MASK[r])
                cp = _remote(piece, land[i].at[r], s_sem.at[3 * i + r], r_sem.at[3 * i + r], (x, y, 1 - c))
                cp.wait_send()
                cp.wait_recv()

    res = _pallas(
        body,
        name=name,
        out_shape=[pltpu.HBM(a.shape, a.dtype) for a in halves] + [pltpu.HBM(a.shape, a.dtype) for a in lands],
        in_specs=[HBM_SPEC] * (2 * n) + [SEM_SPEC, SEM_SPEC, ANY],
        out_specs=[HBM_SPEC] * (2 * n),
        input_output_aliases={i: i for i in range(2 * n)},
        compiler_params=pltpu.CompilerParams(has_side_effects=EFFECT),
    )(*halves, *lands, send_sem, recv_sem, after)
    return list(res[:n]), list(res[n:])


def _small_exchange_copies(waiting):
    def copies(refs, send_sem, recv_sem):
        p, land = refs
        x, y, c, q, _ = _place()
        me = 2 * q + c
        out = []
        for dd in range(1, 2 * N_CHIPS):
            dev = (x ^ ((dd >> 2) & 1), y ^ ((dd >> 1) & 1), c ^ (dd & 1))
            dst = land.at[me ^ dd] if waiting else land.at[me]
            out.append(_remote(p, dst, send_sem.at[dd - 1], recv_sem.at[dd - 1], dev))
        return out

    return copies


def _small_sum(name, me_idx, p, land):
    rows = p.shape[0]
    n_dev = 2 * N_CHIPS

    def body(me_ref, p_ref, land_ref, o_ref):
        me = me_ref[0]
        total = None
        for dev in range(n_dev):
            term = jnp.where(me == dev, p_ref[...], land_ref[dev])
            total = term if total is None else total + term
        o_ref[...] = total

    return _pallas(
        body,
        name=name,
        grid_spec=pltpu.PrefetchScalarGridSpec(
            num_scalar_prefetch=1,
            grid=(1,),
            in_specs=[pl.BlockSpec((rows, LANES), lambda i, m: (0, 0)), pl.BlockSpec((n_dev, rows, LANES), lambda i, m: (0, 0, 0))],
            out_specs=pl.BlockSpec((rows, LANES), lambda i, m: (0, 0)),
        ),
        out_shape=jax.ShapeDtypeStruct(p.shape, F32),
    )(me_idx, p, land)


def _pack(parts):
    rows = []
    for a in parts:
        flat = a.reshape(-1).astype(F32)
        n = flat.shape[0]
        padded = -(-n // (8 * LANES)) * (8 * LANES)
        rows.append(jnp.pad(flat, (0, padded - n)).reshape(-1, LANES))
    return jnp.concatenate(rows, axis=0)


def _unpack(packed, shapes):
    out, row = [], 0
    for shp in shapes:
        n = int(np.prod(shp))
        nrows = -(-n // (8 * LANES)) * 8
        out.append(packed[row : row + nrows].reshape(-1)[:n].reshape(shp))
        row += nrows
    return out


def kernel(x, w_in, norm_mix, sgu_v_gain, sgu_w_s, sgu_b_s, w_a_out, attn_sink, rel_bias, w_b_out, w_o, norm_ffn, w_gate, w_up, w_down, norm_final, loss_target, m_w_in, m_norm_mix, m_sgu_v_gain, m_sgu_w_s, m_sgu_b_s, m_w_a_out, m_attn_sink, m_rel_bias, m_w_b_out, m_w_o, m_norm_ffn, m_w_gate, m_w_up, m_w_down, m_norm_final, v_w_in, v_norm_mix, v_sgu_v_gain, v_sgu_w_s, v_sgu_b_s, v_w_a_out, v_attn_sink, v_rel_bias, v_w_b_out, v_w_o, v_norm_ffn, v_w_gate, v_w_up, v_w_down, v_norm_final):
    s, d = x.shape[1], x.shape[2]
    w_sgu = sgu_v_gain.shape[1]
    groups = sgu_w_s.shape[1]
    heads = attn_sink.shape[1]
    grp = heads // N_KV_HEADS
    w_att = heads * HEAD_DIM
    w_kv = N_KV_HEADS * HEAD_DIM
    d_ff = w_gate.shape[2] * N_CHIPS
    n_in = w_in.shape[2] * N_CHIPS
    off_q = 2 * w_sgu
    off_k = off_q + w_att
    off_g = off_k + 2 * w_kv
    assert n_in == off_g + 2 * d and groups * BLK == w_sgu and s % BLK == 0

    x2d = x.reshape(s, d)
    tgt = loss_target.reshape(s, d)
    c_idx = lax.axis_index("c").astype(I32).reshape(1)
    q_idx = (2 * lax.axis_index("x") + lax.axis_index("y")).astype(I32).reshape(1)
    qc_idx = jnp.concatenate([q_idx, c_idx])

    W_IN, W_A, W_B, W_O, W_GATE, W_UP, W_DOWN = range(7)
    names = ["w_in", "w_a", "w_b", "w_o", "w_gate", "w_up", "w_down"]
    kinds = ["col", "col", "col", "row", "col", "col", "row"]
    big_w = [w_in[0], w_a_out[0], w_b_out[0], w_o[0], w_gate[0], w_up[0], w_down[0]]
    big_m = [m_w_in[0], m_w_a_out[0], m_w_b_out[0], m_w_o[0], m_w_gate[0], m_w_up[0], m_w_down[0]]
    big_v = [v_w_in[0], v_w_a_out[0], v_w_b_out[0], v_w_o[0], v_w_gate[0], v_w_up[0], v_w_down[0]]
    full_in = _cast_into_full("cast_w_in", q_idx, big_w[W_IN], kinds[W_IN])
    in_send, in_recv, (full_in,), token = _gather_start("gather_start_in", [full_in], [kinds[W_IN]], rels=(0, 1))
    rest = [_cast_into_full("cast_" + names[i], q_idx, big_w[i], kinds[i], after=(token,)) for i in range(1, 7)]

    ws_b = sgu_w_s[0].astype(BF16)
    wst_b = jnp.swapaxes(sgu_w_s[0], 1, 2).astype(BF16)
    b_col = sgu_b_s[0].reshape(groups, BLK, 1)
    bias_tab, bucket = _band_tables(rel_bias)
    sink = attn_sink[0]
    small_w = [norm_mix, sgu_v_gain, sgu_w_s, sgu_b_s, attn_sink, rel_bias, norm_ffn, norm_final]
    small_m = [m_norm_mix, m_sgu_v_gain, m_sgu_w_s, m_sgu_b_s, m_attn_sink, m_rel_bias, m_norm_ffn, m_norm_final]
    small_v = [v_norm_mix, v_sgu_v_gain, v_sgu_w_s, v_sgu_b_s, v_attn_sink, v_rel_bias, v_norm_ffn, v_norm_final]
    small_shapes = [w.shape for w in small_w]
    zero1 = jnp.zeros((1,), F32)
    pw, pm, pv = _pack(small_w + [zero1]), _pack(small_m + [zero1]), _pack(small_v + [zero1])
    h1 = _rms_fwd("rms_mix", x2d, norm_mix, after=(token, rest[-1], ws_b, wst_b, b_col, bias_tab, pw, pm, pv))
    (full_in,) = _gather_wait("gather_wait_in", [full_in], [kinds[W_IN]], [0], in_send, in_recv, h1, rels=(0, 1))
    relay_send, relay_recv, (full_in,) = _split_start(
        "gather_relay_in", [full_in], 4, _relay_and_forward_copies([kinds[W_IN]], False)
    )
    full_down = rest.pop()
    full_up = rest.pop()
    ag_send, ag_recv, rest, token = _gather_start("gather_start_rest", rest, kinds[1:5], rels=(0, 1), after=(full_in, full_up))
    (full_in,) = _split_wait(
        "gather_relay_wait_in", [full_in], relay_send, relay_recv, _relay_and_forward_copies([kinds[W_IN]], True), token
    )
    (g_in,) = _gather_forward("gather_fwd_in", [full_in], [kinds[W_IN]], rels=(2,))
    fulls = [g_in] + rest

    def relay_begin(tag, ids, after, source=None):
        ks = [kinds[i] for i in ids]
        send, recv, bufs, w_ids = source or (ag_send, ag_recv, [fulls[i] for i in ids], [i - 1 for i in ids])
        bufs = _gather_wait("gather_wait_" + tag, bufs, ks, w_ids, send, recv, after, rels=(0, 1))
        send, recv, bufs = _split_start("gather_relay_" + tag, bufs, 4 * len(ids), _relay_and_forward_copies(ks, False))
        return tag, ks, send, recv, bufs

    def relay_end(state, after):
        tag, ks, send, recv, bufs = state
        bufs = _split_wait("gather_relay_wait_" + tag, bufs, send, recv, _relay_and_forward_copies(ks, True), after)
        return _gather_forward("gather_fwd_" + tag, bufs, ks, rels=(2,))

    def relay_end_async(state, after):
        tag, ks, send, recv, bufs = state
        bufs = _split_wait("gather_relay_wait_" + tag, bufs, send, recv, _relay_and_forward_copies(ks, True), after)
        send, recv, bufs = _split_start(
            "gather_fwd_start_" + tag, bufs, len(ks), _forward_copies(ks, False, rels=(2,)), sibling_only=True
        )
        return tag, ks, send, recv, bufs

    def forwarded(state, after):
        tag, ks, send, recv, bufs = state
        return _split_wait("gather_fwd_wait_" + tag, bufs, send, recv, _forward_copies(ks, True, rels=(2,)), after)

    tm = _tile(s, (1024, 512, 256, 128))

    tn = _tile(n_in, (768, 640, 512))
    z = _mm(
        "mm_z", (s // tm, n_in // tn, 1), [h1, g_in],
        [pl.BlockSpec((tm, d), lambda i, j, k: (i, 0)), pl.BlockSpec((d, tn), lambda i, j, k: (0, j))],
        [jax.ShapeDtypeStruct((s, n_in), BF16)], [pl.BlockSpec((tm, tn), lambda i, j, k: (i, j))],
        [(0, 1, NN, 0)], 1, (tm, tn), 1, lambda ins, vals, outs, cs: _put(outs[0], cs, vals[0]),
        _mm_vmem([((tm, d), BF16, 2), ((d, tn), BF16, 2), ((tm, tn), F32, 3)]),
    )[0]
    mix_relay = relay_begin("mix", [W_A, W_B, W_O], z)
    up_send, up_recv, (full_up,), token = _gather_start(
        "gather_start_up", [full_up], [kinds[W_UP]], rels=(0, 1), after=(mix_relay[4][0],)
    )

    a_act = _sgu_fwd(z, sgu_v_gain, ws_b, b_col, w_sgu, after=(token,))

    kv_b = z[:, off_k:off_g]
    k_pad = jnp.pad(kv_b[:, :w_kv], ((BLK, BLK), (0, 0)))
    v_pad = jnp.pad(kv_b[:, w_kv:], ((BLK, BLK), (0, 0)))
    q_blk0 = off_q // (grp * HEAD_DIM)
    att = _attn_fwd(sink, z, k_pad, v_pad, bias_tab, grp, q_blk0)

    g_a, g_b, g_o = relay_end(mix_relay, att)
    gate_relay = relay_begin("gate", [W_GATE], g_a)

    tg = _tile(d, (512,))
    ga0, gb0 = off_g // tg, (off_g + d) // tg

    def ep_gate(ins, vals, outs, cs):
        sa, sb = _sigmoid(ins[4][:, cs].astype(F32)), _sigmoid(ins[5][:, cs].astype(F32))
        _put(outs[0], cs, sa * vals[0] + sb * vals[1])
        _put(outs[1], cs, vals[0])
        _put(outs[2], cs, vals[1])

    t_out = pl.BlockSpec((tm, tg), lambda i, j, k: (i, j))
    m_act, y_a, y_b = _mm(
        "mm_branches", (s // tm, d // tg, 1), [a_act, g_a, att, g_b, z, z],
        [pl.BlockSpec((tm, w_sgu), lambda i, j, k: (i, 0)), pl.BlockSpec((w_sgu, tg), lambda i, j, k: (0, j)),
         pl.BlockSpec((tm, w_att), lambda i, j, k: (i, 0)), pl.BlockSpec((w_att, tg), lambda i, j, k: (0, j)),
         pl.BlockSpec((tm, tg), lambda i, j, k: (i, ga0 + j)), pl.BlockSpec((tm, tg), lambda i, j, k: (i, gb0 + j))],
        [jax.ShapeDtypeStruct((s, d), BF16)] * 3,
        [t_out, t_out, t_out], [(0, 1, NN, 0), (2, 3, NN, 1)], 2, (tm, tg), 1, ep_gate,
        _mm_vmem([((tm, w_sgu), BF16, 4), ((w_sgu, tg), BF16, 4), ((tm, tg), F32, 12)]),
        after=(gate_relay[4][0],),
    )

    tn = _tile(d, (1024, 512))

    def ep_residual(ins, vals, outs, cs):
        _put(outs[0], cs, ins[2][:, cs] + vals[0])

    up_relay = relay_begin("up", [W_UP], m_act, source=(up_send, up_recv, [full_up], [0]))
    down_send, down_recv, (full_down,), token = _gather_start(
        "gather_start_down", [full_down], [kinds[W_DOWN]], after=(up_relay[4][0],)
    )
    x2 = _mm(
        "mm_wo", (s // tm, d // tn, 1), [m_act, g_o, x2d],
        [pl.BlockSpec((tm, d), lambda i, j, k: (i, 0)), pl.BlockSpec((d, tn), lambda i, j, k: (0, j)),
         pl.BlockSpec((tm, tn), lambda i, j, k: (i, j))],
        [jax.ShapeDtypeStruct((s, d), F32)], [pl.BlockSpec((tm, tn), lambda i, j, k: (i, j))],
        [(0, 1, NN, 0)], 1, (tm, tn), 1, ep_residual,
        _mm_vmem([((tm, d), BF16, 2), ((d, tn), BF16, 2), ((tm, tn), F32, 5)]),
        after=(token,),
    )[0]
    gate_fwd = relay_end_async(gate_relay, x2)
    up_fwd = relay_end_async(up_relay, gate_fwd[4][0])
    h2 = _rms_fwd("rms_ffn", x2, norm_ffn, after=(up_fwd[4][0],))
    (g_gate,) = forwarded(gate_fwd, h2)
    (g_up,) = forwarded(up_fwd, g_gate)

    tf = _tile(d_ff, (512,))

    def ep_swiglu(ins, vals, outs, cs):
        gt, up = vals
        _put(outs[0], cs, gt)
        _put(outs[1], cs, up)
        _put(outs[2], cs, (gt * _sigmoid(gt)) * up)

    f_out = pl.BlockSpec((tm, tf), lambda i, j, k: (i, j))
    gt, up, f_act = _mm(
        "mm_gate_up", (s // tm, d_ff // tf, 1), [h2, g_gate, g_up],
        [pl.BlockSpec((tm, d), lambda i, j, k: (i, 0)), pl.BlockSpec((d, tf), lambda i, j, k: (0, j)),
         pl.BlockSpec((d, tf), lambda i, j, k: (0, j))],
        [jax.ShapeDtypeStruct((s, d_ff), BF16)] * 3,
        [f_out, f_out, f_out], [(0, 1, NN, 0), (0, 2, NN, 1)], 2, (tm, tf), 1, ep_swiglu,
        _mm_vmem([((tm, d), BF16, 2), ((d, tf), BF16, 4), ((tm, tf), F32, 8)]),    )
    (g_down,) = _gather_forward(
        "gather_fwd_ffn_out",
        _gather_wait("gather_wait_ffn_out", [full_down], [kinds[W_DOWN]], [0], down_send, down_recv, f_act),
        [kinds[W_DOWN]],
    )

    tkf = _tile(d_ff, (1408, 1024, 512))
    tml, tnl = _tile(s, (512, 256, 128)), _tile(d, (512,))
    x3 = _mm(
        "mm_down", (s // tml, d // tnl, 1), [f_act, g_down, x2],
        [pl.BlockSpec((tml, d_ff), lambda i, j, k: (i, 0)), pl.BlockSpec((d_ff, tnl), lambda i, j, k: (0, j)),
         pl.BlockSpec((tml, tnl), lambda i, j, k: (i, j))],
        [jax.ShapeDtypeStruct((s, d), F32)], [pl.BlockSpec((tml, tnl), lambda i, j, k: (i, j))],
        [(0, 1, NN, 0)], 1, (tml, tnl), 1, ep_residual,
        _mm_vmem([((tml, d_ff), BF16, 2), ((d_ff, tnl), BF16, 2), ((tml, tnl), F32, 6)]),
        lookahead=(0,),
    )[0]

    dx3, dx3b, dg_final, loss_part = _head(x3, norm_final.reshape(1, d), tgt)

    def reduce_a(tag, ids, grads):
        ks = [kinds[i] for i in ids]
        lands = [lax.empty((g.shape[0] // 2, g.shape[1]) if k == "col" else (g.shape[0], g.shape[1] // 2), BF16)
                 for g, k in zip(grads, ks)]
        send, recv, bufs = _split_start("pair_send_" + tag, list(grads) + lands, len(ids), _pair_exchange_copies(ks), sibling_only=True)
        return {"tag": tag, "ids": ids, "ks": ks, "pair": (send, recv, bufs), "token": bufs[0]}

    def reduce_b(st, after):
        tag, ids, ks = st["tag"], st["ids"], st["ks"]
        send, recv, bufs = st["pair"]
        bufs = _split_wait("pair_wait_" + tag, bufs, send, recv, _pair_exchange_copies(ks), after)
        grads, from_sib = bufs[: len(ids)], bufs[len(ids) :]
        halves = [_pair_add("pair_add_" + names[i], c_idx, g, r, k) for i, g, r, k in zip(ids, grads, from_sib, ks)]
        st["chip"] = _chip_send_start("chip_send_" + tag, halves, ks)
        st["token"] = st["chip"][2][0]

    def reduce_c(st, after):
        tag, ids, ks = st["tag"], st["ids"], st["ks"]
        send, recv, halves, lands = st["chip"]
        halves, lands = _chip_send_wait("chip_wait_" + tag, halves, lands, ks, send, recv, after)
        pieces = [_chip_sum("chip_sum_" + names[i], qc_idx, h, r, k) for i, h, r, k in zip(ids, halves, lands, ks)]
        st["share"] = _split_start("share_send_" + tag, pieces, len(ids), _pair_share_copies(ks, False), sibling_only=True)
        st["token"] = st["share"][2][0]

    def reduce_d(st, after):
        send, recv, bufs = st["share"]
        return _split_wait("share_wait_" + st["tag"], bufs, send, recv, _pair_share_copies(st["ks"], True), after)

    grads_big, upd = [None] * 7, [None] * 7

    def update(i, g, after=()):
        upd[i] = _adamw("adamw_" + names[i], big_w[i], g, big_m[i], big_v[i], after=after)
        grads_big[i] = upd[i][3]
        return upd[i][0]

    def finish(st, after):
        shared = reduce_d(st, after)
        after = shared[0]
        for i, g in zip(st["ids"], shared):
            after = update(i, g, (after,))
        return after

    def ep_swiglu_bwd(ins, vals, outs, cs):
        df = vals[0]
        gtv, upv = ins[2][:, cs].astype(F32), ins[3][:, cs].astype(F32)
        sg = _sigmoid(gtv)
        _put(outs[0], cs, df * upv * (sg + gtv * sg * (1.0 - sg)))
        _put(outs[1], cs, df * (gtv * sg))

    dgt, dup = _mm(
        "mm_dswiglu", (s // tm, d_ff // tf, 1), [dx3b, g_down, gt, up],
        [pl.BlockSpec((tm, d), lambda i, j, k: (i, 0)), pl.BlockSpec((tf, d), lambda i, j, k: (j, 0)), f_out, f_out],
        [jax.ShapeDtypeStruct((s, d_ff), BF16), jax.ShapeDtypeStruct((s, d_ff), BF16)], [f_out, f_out],
        [(0, 1, NT, 0)], 1, (tm, tf), 1, ep_swiglu_bwd,
        _mm_vmem([((tm, d), BF16, 2), ((tf, d), BF16, 2), ((tm, tf), F32, 8)]),    )

    def ep_store(ins, vals, outs, cs):
        for o, v in zip(outs, vals):
            _put(o, cs, v)

    twn = _tile(d, (1024, 512))
    gw_down = _mm(
        "mm_gw_down", (d_ff // tkf, d // twn, 1), [f_act, dx3b],
        [pl.BlockSpec((s, tkf), lambda i, j, k: (0, i)), pl.BlockSpec((s, twn), lambda i, j, k: (0, j))],
        [jax.ShapeDtypeStruct((d_ff, d), BF16)], [pl.BlockSpec((tkf, twn), lambda i, j, k: (i, j))],
        [(0, 1, TN, 0)], 1, (tkf, twn), 1, ep_store,
        _mm_vmem([((s, tkf), BF16, 3), ((s, twn), BF16, 2), ((tkf, twn), F32, 3)]),
    )[0]
    red_down = reduce_a("down", [W_DOWN], [gw_down])

    tn2 = _tile(d, (256,))
    dh2_specs = [pl.BlockSpec((tm, d_ff), lambda i, j, k: (i, 0)), pl.BlockSpec((tn2, d_ff), lambda i, j, k: (j, 0))]
    dh2_tile = pl.BlockSpec((tm, tn2), lambda i, j, k: (i, j))
    dh2_vmem = _mm_vmem([((tm, d_ff), BF16, 2), ((tn2, d_ff), BF16, 2), ((tm, tn2), F32, 7)])
    dh2 = _mm(
        "mm_dh2_gate", (s // tm, d // tn2, 1), [dgt, g_gate], dh2_specs,
        [jax.ShapeDtypeStruct((s, d), F32)], [dh2_tile], [(0, 1, NT, 0)], 1, (tm, tn2), 1, ep_store, dh2_vmem,
        after=(red_down["token"],), lookahead=(0,),
    )[0]
    dh2 = _mm(
        "mm_dh2_up", (s // tm, d // tn2, 1), [dup, g_up, dh2], dh2_specs + [dh2_tile],
        [jax.ShapeDtypeStruct((s, d), F32)], [dh2_tile], [(0, 1, NT, 0)], 1, (tm, tn2), 1, ep_residual, dh2_vmem,
        lookahead=(0,),
    )[0]
    reduce_b(red_down, dh2)

    twr = _tile(d, (1024, 512))
    w_tile = pl.BlockSpec((twr, tf), lambda i, j, k: (i, j))
    gw_gate, gw_up = _mm(
        "mm_gw_gate_up", (d // twr, d_ff // tf, 1), [h2, dgt, dup],
        [pl.BlockSpec((s, twr), lambda i, j, k: (0, i)), pl.BlockSpec((s, tf), lambda i, j, k: (0, j)),
         pl.BlockSpec((s, tf), lambda i, j, k: (0, j))],
        [jax.ShapeDtypeStruct((d, d_ff), BF16), jax.ShapeDtypeStruct((d, d_ff), BF16)], [w_tile, w_tile],
        [(0, 1, TN, 0), (0, 2, TN, 1)], 2, (twr, tf), 1, ep_store,
        _mm_vmem([((s, twr), BF16, 3), ((s, tf), BF16, 4), ((twr, tf), F32, 6)]),
        after=(red_down["token"],),
    )
    red_ffn = reduce_a("ffn_in", [W_GATE, W_UP], [gw_gate, gw_up])

    dx2, dx2b, dg_ffn = _rms_bwd("rms_ffn_bwd", x2, norm_ffn, dh2, dx3, after=(red_ffn["token"],))

    nj = d // tg

    def lo(j):
        return jnp.minimum(j, nj - 1)

    def gate_bwd_body(dx_ref, wo_ref, ga_ref, gb_ref, ya_ref, yb_ref, dya_ref, dyb_ref, dz_ref, keep):
        j = pl.program_id(1)

        @pl.when(j < nj)
        def _():
            dm = lax.dot_general(dx_ref[...], wo_ref[...], NT, preferred_element_type=F32)
            sa, sb = _sigmoid(ga_ref[...].astype(F32)), _sigmoid(gb_ref[...].astype(F32))
            dya_ref[...] = (dm * sa).astype(BF16)
            dyb_ref[...] = (dm * sb).astype(BF16)
            dz_ref[...] = (dm * ya_ref[...].astype(F32) * (sa * (1.0 - sa))).astype(BF16)
            keep[lo(j)] = (dm * yb_ref[...].astype(F32) * (sb * (1.0 - sb))).astype(BF16)

        @pl.when(j >= nj)
        def _():
            dz_ref[...] = keep[jnp.maximum(j - nj, 0)]

    t_lo = pl.BlockSpec((tm, tg), lambda i, j: (i, lo(j)))
    dya, dyb, dz = _pallas(
        gate_bwd_body,
        name="mm_dgate",
        grid=(s // tm, 2 * nj),
        in_specs=[
            pl.BlockSpec((tm, d), lambda i, j: (i, 0)),
            pl.BlockSpec((tg, d), lambda i, j: (lo(j), 0)),
            pl.BlockSpec((tm, tg), lambda i, j: (i, ga0 + lo(j))),
            pl.BlockSpec((tm, tg), lambda i, j: (i, gb0 + lo(j))),
            t_lo,
            t_lo,
        ],
        out_specs=[t_lo, t_lo, pl.BlockSpec((tm, tg), lambda i, j: (i, ga0 + j))],
        out_shape=[jax.ShapeDtypeStruct((s, d), BF16), jax.ShapeDtypeStruct((s, d), BF16), jax.ShapeDtypeStruct((s, n_in), BF16)],
        scratch_shapes=[pltpu.VMEM((nj, tm, tg), BF16)],
        compiler_params=_params(_mm_vmem([((tm, d), BF16, 2), ((tg, d), BF16, 2), ((tm, tg), F32, 14), ((nj, tm, tg), BF16, 1)])),
    )(dx2b, g_o, z, z, y_a, y_b)
    reduce_b(red_ffn, dya)
    reduce_c(red_down, red_ffn["token"])

    gw_o = _mm(
        "mm_gw_o", (d // twr, d // twn, 1), [m_act, dx2b],
        [pl.BlockSpec((s, twr), lambda i, j, k: (0, i)), pl.BlockSpec((s, twn), lambda i, j, k: (0, j))],
        [jax.ShapeDtypeStruct((d, d), BF16)], [pl.BlockSpec((twr, twn), lambda i, j, k: (i, j))],
        [(0, 1, TN, 0)], 1, (twr, twn), 1, ep_store,
        _mm_vmem([((s, twr), BF16, 3), ((s, twn), BF16, 2), ((twr, twn), F32, 3)]),
        after=(red_down["token"],),
    )[0]
    after_down = finish(red_down, gw_o)

    tb = _tile(w_sgu, (1024, 512))
    b_out = pl.BlockSpec((tm, tb), lambda i, j, k: (i, j))

    da, datt = _mm(
        "mm_dbranches", (s // tm, w_sgu // tb, 1), [dya, g_a, dyb, g_b],
        [pl.BlockSpec((tm, d), lambda i, j, k: (i, 0)), pl.BlockSpec((tb, d), lambda i, j, k: (j, 0)),
         pl.BlockSpec((tm, d), lambda i, j, k: (i, 0)), pl.BlockSpec((tb, d), lambda i, j, k: (j, 0))],
        [jax.ShapeDtypeStruct((s, w_sgu), BF16), jax.ShapeDtypeStruct((s, w_att), BF16)], [b_out, b_out],
        [(0, 1, NT, 0), (2, 3, NT, 1)], 2, (tm, tb), 1, ep_store,
        _mm_vmem([((tm, d), BF16, 4), ((tb, d), BF16, 4), ((tm, tb), F32, 6)]),
        after=(after_down,),
    )

    wb_tile = pl.BlockSpec((tb, twn), lambda i, j, k: (i, j))
    gw_a, gw_b = _mm(
        "mm_gw_branches", (w_sgu // tb, d // twn, 1), [a_act, dya, att, dyb],
        [pl.BlockSpec((s, tb), lambda i, j, k: (0, i)), pl.BlockSpec((s, twn), lambda i, j, k: (0, j)),
         pl.BlockSpec((s, tb), lambda i, j, k: (0, i)), pl.BlockSpec((s, twn), lambda i, j, k: (0, j))],
        [jax.ShapeDtypeStruct((w_sgu, d), BF16), jax.ShapeDtypeStruct((w_att, d), BF16)], [wb_tile, wb_tile],
        [(0, 1, TN, 0), (2, 3, TN, 1)], 2, (tb, twn), 1, ep_store,
        _mm_vmem([((s, tb), BF16, 5), ((s, twn), BF16, 4), ((tb, twn), F32, 6)]),
        after=(da,),
    )
    red_mix = reduce_a("mix", [W_O, W_A, W_B], [gw_o, gw_a, gw_b])

    dz, dws, dbs, dgain = _sgu_bwd(z, da, sgu_v_gain, ws_b, wst_b, b_col, w_sgu, dz, after=(red_mix["token"],))
    dz, dk_pad, dv_pad, dbias_tab, dsink = _attn_bwd(sink, z, k_pad, v_pad, bias_tab, datt, dz, grp, q_blk0)
    dz = _dkv_to_dz(dk_pad, dv_pad, dz, off_k // (2 * w_kv))
    drel = _relbias_bwd(dbias_tab, bucket)
    reduce_b(red_mix, dz)
    reduce_c(red_ffn, red_mix["token"])

    early = [dgain, dws, dbs, dsink[:, 0], drel[:, :REL_BUCKETS].T, dg_ffn, dg_final]
    p_early = _pack([g.reshape(shp) for g, shp in zip(early, small_shapes[1:])] + [loss_part[0, :1]])
    land = jnp.zeros((2 * N_CHIPS,) + p_early.shape, F32)
    sm_send, sm_recv, (p_early, land) = _split_start("small_send", [p_early, land], 2 * N_CHIPS - 1, _small_exchange_copies(False))

    tzn = _tile(n_in, (768, 640, 512))
    gw_in = _mm(
        "mm_gw_in", (d // twr, n_in // tzn, 1), [h1, dz],
        [pl.BlockSpec((s, twr), lambda i, j, k: (0, i)), pl.BlockSpec((s, tzn), lambda i, j, k: (0, j))],
        [jax.ShapeDtypeStruct((d, n_in), BF16)], [pl.BlockSpec((twr, tzn), lambda i, j, k: (i, j))],
        [(0, 1, TN, 0)], 1, (twr, tzn), 1, ep_store,
        _mm_vmem([((s, twr), BF16, 3), ((s, tzn), BF16, 2), ((twr, tzn), F32, 3)]),
        after=(red_ffn["token"], p_early),
    )[0]
    red_in = reduce_a("w_in", [W_IN], [gw_in])

    g_gate, g_up = reduce_d(red_ffn, red_in["token"])
    reduce_b(red_in, update(W_GATE, g_gate))

    dh1 = _mm(
        "mm_dh1", (s // tm, d // tn2, 1), [dz, g_in],
        [pl.BlockSpec((tm, n_in), lambda i, j, k: (i, 0)), pl.BlockSpec((tn2, n_in), lambda i, j, k: (j, 0))],
        [jax.ShapeDtypeStruct((s, d), F32)], [pl.BlockSpec((tm, tn2), lambda i, j, k: (i, j))],
        [(0, 1, NT, 0)], 1, (tm, tn2), 1, ep_store,
        _mm_vmem([((tm, n_in), BF16, 2), ((tn2, n_in), BF16, 2), ((tm, tn2), F32, 5)]),
        after=(red_in["token"],), lookahead=(0,),
    )[0]

    reduce_c(red_mix, dh1)
    grad_x, _, dg_mix = _rms_bwd("rms_mix_bwd", x2d, norm_mix, dh1, dx2, after=(red_mix["token"],))

    p_mix = _pack([dg_mix.reshape(small_shapes[0])])
    land_mix = jnp.zeros((2 * N_CHIPS,) + p_mix.shape, F32)
    mx_send, mx_recv, (p_mix, land_mix) = _split_start("mix_send", [p_mix, land_mix], 2 * N_CHIPS - 1, _small_exchange_copies(False))

    reduce_c(red_in, update(W_UP, g_up, (finish(red_mix, p_mix),)))
    p_early, land = _split_wait("small_wait", [p_early, land], sm_send, sm_recv, _small_exchange_copies(True), red_in["token"])
    p_mix, land_mix = _split_wait("mix_wait", [p_mix, land_mix], mx_send, mx_recv, _small_exchange_copies(True), p_early)
    me_idx = 2 * q_idx + c_idx
    packed_g = jnp.concatenate([_small_sum("mix_sum", me_idx, p_mix, land_mix), _small_sum("small_sum", me_idx, p_early, land)], axis=0)
    g_small = _unpack(packed_g, small_shapes + [(1,)])
    loss = g_small[-1].reshape(())
    g_small = g_small[:-1]
    pg = _pack(g_small + [zero1])
    small_upd = _adamw("adamw_small", pw, pg, pm, pv)
    d_small, nm_small, nv_small = [_unpack(a, small_shapes) for a in small_upd[:3]]
    finish(red_in, small_upd[0])

    small_names = ["norm_mix", "sgu_v_gain", "sgu_w_s", "sgu_b_s", "attn_sink", "rel_bias", "norm_ffn", "norm_final"]
    table = {}
    for i, n in enumerate(names):
        table[n] = (grads_big[i][None], upd[i][0][None], upd[i][1][None], upd[i][2][None])
    for i, n in enumerate(small_names):
        table[n] = (g_small[i], d_small[i], nm_small[i], nv_small[i])
    order = ["w_in", "norm_mix", "sgu_v_gain", "sgu_w_s", "sgu_b_s", "w_a", "attn_sink", "rel_bias", "w_b", "w_o", "norm_ffn",
             "w_gate", "w_up", "w_down", "norm_final"]
    outs = [loss, grad_x.reshape(1, s, d)]
    for part in range(4):
        outs += [table[n][part] for n in order]
    return tuple(outs)
```

```python
import math

import jax
import jax.numpy as jnp
import numpy as np
from jax import lax
from jax.experimental import pallas as pl
from jax.experimental.pallas import tpu as pltpu

F32 = jnp.float32
BF16 = jnp.bfloat16
I32 = jnp.int32
MESH = pl.DeviceIdType.MESH

EPS = 1e-6
NEG = -1e30
BLK = 128
HEAD_DIM = 128
N_KV_HEADS = 2
REL_BUCKETS = 32
REL_MAX_DIST = 128
N_CHIPS = 4
ADAM_LR, ADAM_B1, ADAM_B2, ADAM_EPS, ADAM_WD, ADAM_STEP = 0.001, 0.9, 0.999, 1e-08, 0.01, 10

LANES = 128
VMEM_CAP = 60 * 1024 * 1024

NN = (((1,), (0,)), ((), ()))
NT = (((1,), (1,)), ((), ()))
TN = (((0,), (0,)), ((), ()))
ANY = pl.BlockSpec(memory_space=pl.ANY)
HBM_SPEC = pl.BlockSpec(memory_space=pltpu.HBM)
SEM_SPEC = pl.BlockSpec(memory_space=pltpu.SEMAPHORE)
EFFECT = pltpu.SideEffectType.DATAFLOW_SIDE_EFFECTING


def _tile(n, cands):
    for t in cands:
        if n % t == 0:
            return t
    return n


PIN_BYTES = 64 * 1024


def _pin_hbm(a):
    big = hasattr(a, "dtype") and jnp.issubdtype(a.dtype, jnp.floating) and _nbytes(a.shape, a.dtype) >= PIN_BYTES
    return pltpu.with_memory_space_constraint(a, pltpu.HBM) if big else a


def _pallas(body, *, out_shape, **kw):
    def pin(o):
        big = isinstance(o, jax.ShapeDtypeStruct) and jnp.issubdtype(o.dtype, jnp.floating) and _nbytes(o.shape, o.dtype) >= PIN_BYTES
        return pltpu.HBM(o.shape, o.dtype) if big else o

    shapes = type(out_shape)(pin(o) for o in out_shape) if isinstance(out_shape, (list, tuple)) else pin(out_shape)
    call = pl.pallas_call(body, out_shape=shapes, **kw)
    return lambda *args: call(*[_pin_hbm(a) for a in args])


def _params(vmem_bytes=None, **kw):
    if vmem_bytes is not None:
        kw["vmem_limit_bytes"] = int(min(max(vmem_bytes, 32 * 1024 * 1024), VMEM_CAP))
    return pltpu.CompilerParams(**kw)


def _nbytes(shape, dtype):
    return int(np.prod(shape)) * jnp.dtype(dtype).itemsize


def _sigmoid(x):
    return 1.0 / (1.0 + jnp.exp(-x))


_GC = 0.7978845608028654
_GA = 0.044715


def _gelu(x):
    return 0.5 * x * (1.0 + jnp.tanh(_GC * (x + _GA * (x * x * x))))


def _gelu_grad(x):
    t = jnp.tanh(_GC * (x + _GA * (x * x * x)))
    return 0.5 * (1.0 + t) + 0.5 * x * (1.0 - t * t) * (_GC * (1.0 + 3.0 * _GA * (x * x)))


def _bf(v):
    return v if v.dtype == BF16 else v.astype(BF16)


def _mm(name, grid, ins, in_specs, out_shape, out_specs, pairs, n_acc, tile, nk, epilogue, vmem_bytes, after=(), lookahead=None):
    assert nk == 1
    n_in, n_out = len(ins) + len(after), len(out_shape)

    def body(*refs):
        in_refs, out_refs = refs[:n_in], refs[n_in : n_in + n_out]
        vals = [None] * n_acc
        for a_i, b_i, dn, acc_i in pairs:
            d = lax.dot_general(_bf(in_refs[a_i][...]), _bf(in_refs[b_i][...]), dn, preferred_element_type=F32)
            vals[acc_i] = d if vals[acc_i] is None else vals[acc_i] + d
        epilogue(in_refs, vals, out_refs, slice(None))

    if lookahead is None:
        at = lambda sp, i, j: tuple(sp.index_map(i, j, 0))
        rows_only = lambda sp: at(sp, 0, 0) == at(sp, 0, 1) and at(sp, 0, 0) != at(sp, 1, 0)
        lookahead = tuple(n for n, sp in enumerate(in_specs) if min(grid[:2]) > 1 and rows_only(sp))
    if lookahead:
        ahead = pl.Buffered(2, use_lookahead=True)
        specs = [pl.BlockSpec(sp.block_shape, sp.index_map, pipeline_mode=ahead) if i in lookahead else sp for i, sp in enumerate(in_specs)]
        n_read = len(ins)

        def piped(*refs):
            def step(*blocks):
                body(*blocks[:n_read], *[None] * len(after), *blocks[n_read:])

            pltpu.emit_pipeline(step, grid=grid, in_specs=specs, out_specs=list(out_specs))(*refs[:n_read], *refs[n_in:])

        return _pallas(
            piped,
            name=name,
            in_specs=[ANY] * n_in,
            out_specs=[ANY] * n_out,
            out_shape=out_shape,
            compiler_params=_params(vmem_bytes),
        )(*ins, *after)

    return _pallas(
        body,
        name=name,
        grid=grid,
        in_specs=list(in_specs) + [ANY] * len(after),
        out_specs=out_specs,
        out_shape=out_shape,
        compiler_params=_params(vmem_bytes),
    )(*ins, *after)


def _put(ref, cs, v):
    ref[:, cs] = v.astype(ref.dtype)


def _mm_vmem(tiles):
    return sum(_nbytes(s, d) * c for s, d, c in tiles) + 4 * 1024 * 1024


def _rows8(v):
    r, d = v.shape
    return v.reshape(r // 8, 8, d).sum(axis=0)


def _rms_fwd(name, x, g, after=()):
    s, d = x.shape
    tm = _tile(s, (256, 128))

    def body(x_ref, g_ref, *rest):
        h_ref = rest[-1]
        xv = x_ref[...]
        r = lax.rsqrt(jnp.mean(xv * xv, axis=-1, keepdims=True) + EPS)
        h_ref[...] = ((xv * r) * g_ref[...]).astype(BF16)

    return _pallas(
        body,
        name=name,
        grid=(s // tm,),
        in_specs=[pl.BlockSpec((tm, d), lambda i: (i, 0)), pl.BlockSpec((1, d), lambda i: (0, 0))] + [ANY] * len(after),
        out_specs=pl.BlockSpec((tm, d), lambda i: (i, 0)),
        out_shape=jax.ShapeDtypeStruct((s, d), BF16),
    )(x, g, *after)


def _rms_bwd(name, x, g, dh, dres, after=()):
    s, d = x.shape
    tm = _tile(s, (256, 128))
    n = s // tm
    n_after = len(after)

    def body(x_ref, g_ref, dh_ref, dres_ref, *rest):
        dx_ref, dxb_ref, dg_ref, acc_ref = rest[n_after:]
        i = pl.program_id(0)
        xv = x_ref[...]
        r = lax.rsqrt(jnp.mean(xv * xv, axis=-1, keepdims=True) + EPS)
        xh = xv * r
        dhv = dh_ref[...]
        dxh = dhv * g_ref[...]
        dx = r * (dxh - xh * jnp.mean(dxh * xh, axis=-1, keepdims=True)) + dres_ref[...]
        dx_ref[...] = dx
        dxb_ref[...] = dx.astype(BF16)
        part = _rows8(dhv * xh)

        @pl.when(i == 0)
        def _():
            acc_ref[...] = part

        @pl.when(i > 0)
        def _():
            acc_ref[...] += part

        @pl.when(i == n - 1)
        def _():
            dg_ref[...] = jnp.sum(acc_ref[...], axis=0, keepdims=True)

    row = pl.BlockSpec((tm, d), lambda i: (i, 0))
    vec = pl.BlockSpec((1, d), lambda i: (0, 0))
    return _pallas(
        body,
        name=name,
        grid=(n,),
        in_specs=[row, vec, row, row] + [ANY] * n_after,
        out_specs=[row, row, vec],
        out_shape=[jax.ShapeDtypeStruct((s, d), F32), jax.ShapeDtypeStruct((s, d), BF16), jax.ShapeDtypeStruct((1, d), F32)],
        scratch_shapes=[pltpu.VMEM((8, d), F32)],
    )(x, g, dh, dres, *after)


def _head(x3, g, target):
    s, d = x3.shape
    tm = _tile(s, (256, 128))
    n = s // tm

    def body(x_ref, g_ref, t_ref, dx_ref, dxb_ref, dg_ref, loss_ref, acc_g, acc_l):
        i = pl.program_id(0)
        xv = x_ref[...]
        gv = g_ref[...]
        r = lax.rsqrt(jnp.mean(xv * xv, axis=-1, keepdims=True) + EPS)
        xh = xv * r
        e = xh * gv - t_ref[...]
        dy = e * (1.0 / d)
        dxh = dy * gv
        dx = r * (dxh - xh * jnp.mean(dxh * xh, axis=-1, keepdims=True))
        dx_ref[...] = dx
        dxb_ref[...] = dx.astype(BF16)
        pg = _rows8(dy * xh)
        plo = _rows8(e * e)

        @pl.when(i == 0)
        def _():
            acc_g[...] = pg
            acc_l[...] = plo

        @pl.when(i > 0)
        def _():
            acc_g[...] += pg
            acc_l[...] += plo

        @pl.when(i == n - 1)
        def _():
            dg_ref[...] = jnp.sum(acc_g[...], axis=0, keepdims=True)
            loss_ref[...] = jnp.full((1, LANES), (0.5 / d) * jnp.sum(acc_l[...]), F32)

    row = pl.BlockSpec((tm, d), lambda i: (i, 0))
    vec = pl.BlockSpec((1, d), lambda i: (0, 0))
    return _pallas(
        body,
        name="head",
        grid=(n,),
        in_specs=[row, vec, row],
        out_specs=[row, row, vec, pl.BlockSpec((1, LANES), lambda i: (0, 0))],
        out_shape=[
            jax.ShapeDtypeStruct((s, d), F32),
            jax.ShapeDtypeStruct((s, d), BF16),
            jax.ShapeDtypeStruct((1, d), F32),
            jax.ShapeDtypeStruct((1, LANES), F32),
        ],
        scratch_shapes=[pltpu.VMEM((8, d), F32), pltpu.VMEM((8, d), F32)],
    )(x3, g, target)


def _sgu_fwd(z, gain, ws_b, b_col, w_sgu, after=()):
    s = z.shape[0]
    groups = ws_b.shape[0]

    def body(zu_ref, zv_ref, gain_ref, ws_ref, b_ref, *rest):
        a_ref = rest[-1]
        vv = _gelu(zv_ref[...].astype(F32))
        r = lax.rsqrt(jnp.mean(vv * vv, axis=-1, keepdims=True) + EPS)
        vn = ((vv * r) * gain_ref[...]).astype(BF16)
        u = _gelu(zu_ref[...].astype(F32))
        for g in range(groups):
            sl = slice(g * BLK, (g + 1) * BLK)
            mixed = jnp.dot(ws_ref[g], vn[:, sl], preferred_element_type=F32) + b_ref[g]
            a_ref[:, sl] = (u[:, sl] * mixed).astype(BF16)

    return _pallas(
        body,
        name="sgu_fwd",
        grid=(s // BLK,),
        in_specs=[
            pl.BlockSpec((BLK, w_sgu), lambda c: (c, 0)),
            pl.BlockSpec((BLK, w_sgu), lambda c: (c, 1)),
            pl.BlockSpec((1, w_sgu), lambda c: (0, 0)),
            pl.BlockSpec((groups, BLK, BLK), lambda c: (0, 0, 0)),
            pl.BlockSpec((groups, BLK, 1), lambda c: (0, 0, 0)),
        ]
        + [ANY] * len(after),
        out_specs=pl.BlockSpec((BLK, w_sgu), lambda c: (c, 0)),
        out_shape=jax.ShapeDtypeStruct((s, w_sgu), BF16),
    )(z, z, gain, ws_b, b_col, *after)


def _sgu_bwd(z, da, gain, ws_b, wst_b, b_col, w_sgu, dz, after=()):
    s = z.shape[0]
    groups = ws_b.shape[0]
    n = s // BLK
    n_skip = 1 + len(after)

    def body(zu_ref, zv_ref, da_ref, gain_ref, ws_ref, wst_ref, b_ref, *rest):
        dz_ref, dws_ref, dbs_ref, dgain_ref, acc_gain, vv_s, gv_s, dxh_s = rest[n_skip:]
        c = pl.program_id(0)
        cols = [slice(g * BLK, (g + 1) * BLK) for g in range(groups)]

        ss = jnp.zeros((BLK, 1), F32)
        for sl in cols:
            zv = zv_ref[:, sl].astype(F32)
            vv = _gelu(zv)
            vv_s[:, sl] = vv
            gv_s[:, sl] = _gelu_grad(zv)
            ss = ss + jnp.sum(vv * vv, axis=-1, keepdims=True)
        r = lax.rsqrt(ss * (1.0 / w_sgu) + EPS)

        dot_dx = jnp.zeros((BLK, 1), F32)
        for g, sl in enumerate(cols):
            gain_g = gain_ref[:, sl]
            xh = vv_s[:, sl] * r
            vn = (xh * gain_g).astype(BF16)
            zu = zu_ref[:, sl].astype(F32)
            dav = da_ref[:, sl].astype(F32)
            dmix = dav * _gelu(zu)
            dmix_b = dmix.astype(BF16)
            mixed = jnp.dot(ws_ref[g], vn, preferred_element_type=F32) + b_ref[g]
            dz_ref[:, sl] = (dav * mixed * _gelu_grad(zu)).astype(BF16)
            dvn = jnp.dot(wst_ref[g], dmix_b, preferred_element_type=F32)
            dws_g = lax.dot_general(dmix_b, vn, NT, preferred_element_type=F32)
            dbs_g = jnp.sum(dmix, axis=1, keepdims=True)
            pg = _rows8(dvn * xh)

            @pl.when(c == 0)
            def _():
                dws_ref[g] = dws_g
                dbs_ref[g] = dbs_g
                acc_gain[:, sl] = pg

            @pl.when(c > 0)
            def _():
                dws_ref[g] += dws_g
                dbs_ref[g] += dbs_g
                acc_gain[:, sl] += pg

            dxh = dvn * gain_g
            dxh_s[:, sl] = dxh
            dot_dx = dot_dx + jnp.sum(dxh * xh, axis=-1, keepdims=True)

        mean_dx = dot_dx * (1.0 / w_sgu)
        for g, sl in enumerate(cols):
            dvv = r * (dxh_s[:, sl] - (vv_s[:, sl] * r) * mean_dx)
            dz_ref[:, w_sgu + g * BLK : w_sgu + (g + 1) * BLK] = (dvv * gv_s[:, sl]).astype(BF16)

        @pl.when(c == n - 1)
        def _():
            dgain_ref[...] = jnp.sum(acc_gain[...], axis=0, keepdims=True)

    full3 = pl.BlockSpec((groups, BLK, BLK), lambda c: (0, 0, 0))
    col3 = pl.BlockSpec((groups, BLK, 1), lambda c: (0, 0, 0))
    vec = pl.BlockSpec((1, w_sgu), lambda c: (0, 0))
    return _pallas(
        body,
        name="sgu_bwd",
        grid=(n,),
        in_specs=[
            pl.BlockSpec((BLK, w_sgu), lambda c: (c, 0)),
            pl.BlockSpec((BLK, w_sgu), lambda c: (c, 1)),
            pl.BlockSpec((BLK, w_sgu), lambda c: (c, 0)),
            vec,
            full3,
            full3,
            col3,
            ANY,
        ]
        + [ANY] * len(after),
        out_specs=[pl.BlockSpec((BLK, 2 * w_sgu), lambda c: (c, 0)), full3, col3, vec],
        out_shape=[
            jax.ShapeDtypeStruct(dz.shape, BF16),
            jax.ShapeDtypeStruct((groups, BLK, BLK), F32),
            jax.ShapeDtypeStruct((groups, BLK, 1), F32),
            jax.ShapeDtypeStruct((1, w_sgu), F32),
        ],
        scratch_shapes=[pltpu.VMEM((8, w_sgu), F32)] + [pltpu.VMEM((BLK, w_sgu), F32)] * 3,
        input_output_aliases={7: 0},
    )(z, z, da, gain, ws_b, wst_b, b_col, dz, *after)


def _attn_softmax(sink_ref, q_ref, k_ref, v_ref, bias_ref, s_len, grp):
    kv = pl.program_id(0)
    n = pl.program_id(1)
    start = pl.multiple_of(n * BLK, BLK)
    kb = k_ref[pl.ds(start, 3 * BLK), :]
    vb = v_ref[pl.ds(start, 3 * BLK), :]
    qv = q_ref[...]
    qs = jnp.concatenate([qv[:, g * HEAD_DIM : (g + 1) * HEAD_DIM] for g in range(grp)], axis=0).astype(BF16)
    sc = lax.dot_general(qs, kb, NT, preferred_element_type=F32) * (HEAD_DIM**-0.5)
    sc = sc + bias_ref[...].reshape(grp * BLK, 3 * BLK)
    kpos = start + lax.broadcasted_iota(I32, (1, 3 * BLK), 1) - BLK
    sc = jnp.where((kpos >= 0) & (kpos < s_len), sc, NEG)
    sink = jnp.concatenate([jnp.full((BLK, 1), sink_ref[kv * grp + g], F32) for g in range(grp)], axis=0)
    m = jnp.maximum(jnp.max(sc, axis=-1, keepdims=True), sink)
    p = jnp.exp(sc - m)
    esink = jnp.exp(sink - m)
    den = jnp.sum(p, axis=-1, keepdims=True) + esink
    return start, qs, kb, vb, p / den, esink / den


def _attn_specs(s, grp, q_blk0):
    qw = grp * HEAD_DIM
    return [
        pl.BlockSpec(memory_space=pltpu.SMEM),
        pl.BlockSpec((BLK, qw), lambda kv, n: (n, q_blk0 + kv)),
        pl.BlockSpec((s + 2 * BLK, HEAD_DIM), lambda kv, n: (0, kv)),
        pl.BlockSpec((s + 2 * BLK, HEAD_DIM), lambda kv, n: (0, kv)),
        pl.BlockSpec((grp, BLK, 3 * BLK), lambda kv, n: (kv, 0, 0)),
    ]


def _attn_fwd(sink, z, k_pad, v_pad, bias_tab, grp, q_blk0):
    s = z.shape[0]
    qw = grp * HEAD_DIM

    def body(sink_ref, q_ref, k_ref, v_ref, bias_ref, o_ref):
        _, _, _, vb, pn, _ = _attn_softmax(sink_ref, q_ref, k_ref, v_ref, bias_ref, s, grp)
        o = jnp.dot(pn.astype(BF16), vb, preferred_element_type=F32)
        for g in range(grp):
            o_ref[:, g * HEAD_DIM : (g + 1) * HEAD_DIM] = o[g * BLK : (g + 1) * BLK].astype(BF16)

    return _pallas(
        body,
        name="attn_fwd",
        grid=(N_KV_HEADS, s // BLK),
        in_specs=_attn_specs(s, grp, q_blk0),
        out_specs=pl.BlockSpec((BLK, qw), lambda kv, n: (n, kv)),
        out_shape=jax.ShapeDtypeStruct((s, N_KV_HEADS * qw), BF16),
    )(sink, z, k_pad, v_pad, bias_tab)


def _attn_bwd(sink, z, k_pad, v_pad, bias_tab, dout, dz, grp, q_blk0):
    s = z.shape[0]
    qw = grp * HEAD_DIM
    nb = s // BLK
    heads = N_KV_HEADS * grp

    def body(sink_ref, q_ref, k_ref, v_ref, bias_ref, do_ref, dz_in, dq_ref, dk_ref, dv_ref, dbias_ref, dsink_ref, dk_acc, dv_acc):
        del dz_in
        kv = pl.program_id(0)
        n = pl.program_id(1)
        start, qs, kb, vb, pn, psink = _attn_softmax(sink_ref, q_ref, k_ref, v_ref, bias_ref, s, grp)
        dov = do_ref[...]
        dos = jnp.concatenate([dov[:, g * HEAD_DIM : (g + 1) * HEAD_DIM] for g in range(grp)], axis=0)
        dp = lax.dot_general(dos, vb, NT, preferred_element_type=F32)
        dvb = lax.dot_general(pn.astype(BF16), dos, TN, preferred_element_type=F32)
        delta = jnp.sum(pn * dp, axis=-1, keepdims=True)
        ds = pn * (dp - delta)
        dsb = (ds * (HEAD_DIM**-0.5)).astype(BF16)
        dq = jnp.dot(dsb, kb, preferred_element_type=F32)
        dkb = lax.dot_general(dsb, qs, TN, preferred_element_type=F32)
        for g in range(grp):
            dq_ref[:, g * HEAD_DIM : (g + 1) * HEAD_DIM] = dq[g * BLK : (g + 1) * BLK].astype(BF16)

        @pl.when(n == 0)
        def _():
            dk_acc[...] = jnp.zeros_like(dk_acc)
            dv_acc[...] = jnp.zeros_like(dv_acc)
            dbias_ref[...] = jnp.zeros_like(dbias_ref)

        @pl.when((n == 0) & (kv == 0))
        def _():
            dsink_ref[...] = jnp.zeros_like(dsink_ref)

        dk_acc[pl.ds(start, 3 * BLK), :] += dkb
        dv_acc[pl.ds(start, 3 * BLK), :] += dvb
        dbias_ref[...] += ds.reshape(grp, BLK, 3 * BLK)
        row = lax.broadcasted_iota(I32, (heads, LANES), 0)
        sd = psink * delta
        upd = jnp.zeros((heads, LANES), F32)
        for g in range(grp):
            upd = jnp.where(row == kv * grp + g, -jnp.sum(sd[g * BLK : (g + 1) * BLK]), upd)
        dsink_ref[...] += upd

        @pl.when(n == nb - 1)
        def _():
            dk_ref[...] = dk_acc[...]
            dv_ref[...] = dv_acc[...]

    pad_spec = pl.BlockSpec((s + 2 * BLK, HEAD_DIM), lambda kv, n: (0, kv))
    kvw = N_KV_HEADS * HEAD_DIM
    return _pallas(
        body,
        name="attn_bwd",
        grid=(N_KV_HEADS, nb),
        in_specs=_attn_specs(s, grp, q_blk0) + [pl.BlockSpec((BLK, qw), lambda kv, n: (n, kv)), ANY],
        out_specs=[
            pl.BlockSpec((BLK, qw), lambda kv, n: (n, q_blk0 + kv)),
            pad_spec,
            pad_spec,
            pl.BlockSpec((grp, BLK, 3 * BLK), lambda kv, n: (kv, 0, 0)),
            pl.BlockSpec((heads, LANES), lambda kv, n: (0, 0)),
        ],
        out_shape=[
            jax.ShapeDtypeStruct(dz.shape, BF16),
            jax.ShapeDtypeStruct((s + 2 * BLK, kvw), F32),
            jax.ShapeDtypeStruct((s + 2 * BLK, kvw), F32),
            jax.ShapeDtypeStruct((heads, BLK, 3 * BLK), F32),
            jax.ShapeDtypeStruct((heads, LANES), F32),
        ],
        scratch_shapes=[pltpu.VMEM((s + 2 * BLK, HEAD_DIM), F32), pltpu.VMEM((s + 2 * BLK, HEAD_DIM), F32)],
        input_output_aliases={6: 0},
    )(sink, z, k_pad, v_pad, bias_tab, dout, dz)


def _dkv_to_dz(dk_pad, dv_pad, dz, blk_idx):
    s = dz.shape[0]
    kvw = dk_pad.shape[1]

    def body(dk_ref, dv_ref, dz_in, out_ref):
        del dz_in
        out_ref[:, :kvw] = dk_ref[...].astype(BF16)
        out_ref[:, kvw:] = dv_ref[...].astype(BF16)

    src = pl.BlockSpec((BLK, kvw), lambda i: (i + 1, 0))
    return _pallas(
        body,
        name="dkv_to_dz",
        grid=(s // BLK,),
        in_specs=[src, src, ANY],
        out_specs=pl.BlockSpec((BLK, 2 * kvw), lambda i: (i, blk_idx)),
        out_shape=jax.ShapeDtypeStruct(dz.shape, BF16),
        input_output_aliases={2: 0},
    )(dk_pad, dv_pad, dz)


def _relbias_bwd(dbias_tab, bucket):
    heads = dbias_tab.shape[0]

    def body(dt_ref, bk_ref, out_ref):
        lane = lax.broadcasted_iota(I32, (1, LANES), 1)
        bk = bk_ref[...]
        rows = []
        for h in range(heads):
            dt = dt_ref[h]
            acc = jnp.zeros((1, LANES), F32)
            for b in range(REL_BUCKETS):
                acc = jnp.where(lane == b, jnp.sum(jnp.where(bk == b, dt, 0.0)), acc)
            rows.append(acc)
        out_ref[...] = jnp.concatenate(rows, axis=0)

    return _pallas(body, name="relbias_bwd", out_shape=jax.ShapeDtypeStruct((heads, LANES), F32))(dbias_tab, bucket)


def _t5_bucket(rel):
    nb = REL_BUCKETS // 2
    ret = jnp.where(rel > 0, nb, 0)
    n = jnp.abs(rel)
    max_exact = nb // 2
    nf = jnp.maximum(n, 1).astype(F32)
    large = max_exact + (jnp.log(nf / max_exact) / math.log(REL_MAX_DIST / max_exact) * (nb - max_exact)).astype(I32)
    large = jnp.minimum(large, nb - 1)
    return ret + jnp.where(n < max_exact, n, large)


def _band_tables(rel_bias):
    qi = jnp.arange(BLK)[:, None]
    kj = jnp.arange(3 * BLK)[None, :]
    rel = kj - BLK - qi
    bucket = _t5_bucket(rel).astype(I32)
    heads = rel_bias.shape[1]
    masked = jnp.where(jnp.abs(rel) <= BLK, bucket, -1)

    def body(rb_ref, bk_ref, out_ref):
        bk = bk_ref[...]
        for h in range(heads):
            tab = jnp.full(bk.shape, NEG, F32)
            for b in range(REL_BUCKETS):
                tab = jnp.where(bk == b, rb_ref[b, h], tab)
            out_ref[h] = tab

    bias_tab = _pallas(
        body,
        name="bias_table",
        in_specs=[pl.BlockSpec(memory_space=pltpu.SMEM), pl.BlockSpec(memory_space=pltpu.VMEM)],
        out_specs=pl.BlockSpec(memory_space=pltpu.VMEM),
        out_shape=jax.ShapeDtypeStruct((heads, BLK, 3 * BLK), F32),
    )(rel_bias.astype(F32), masked)
    return bias_tab, bucket


EW_BLOCK_ELEMS = 512 * 1024


def _ew_tiles(shape, elems=EW_BLOCK_ELEMS // 2):
    r, c = shape
    tn = c if c <= 2048 else _tile(c, (2048, 1920, 1536, 1408, 1024, 512))
    tm = max([t for t in range(16, r + 1, 16) if r % t == 0 and t * tn <= elems] or [8])
    return tm, tn


def _cast_into_full(name, qidx, w, kind, after=()):
    r, c = w.shape
    tm, tn = _ew_tiles(w.shape, EW_BLOCK_ELEMS)
    nbi, nbj = r // tm, c // tn
    if kind == "col":
        full, out_spec = (r, c * N_CHIPS), pl.BlockSpec((tm, tn), lambda i, j, q: (i, q[0] * nbj + j))
    else:
        full, out_spec = (r * N_CHIPS, c), pl.BlockSpec((tm, tn), lambda i, j, q: (q[0] * nbi + i, j))

    def body(q_ref, w_ref, *rest):
        del q_ref
        rest[-1][...] = w_ref[...].astype(BF16)

    return _pallas(
        body,
        name=name,
        grid_spec=pltpu.PrefetchScalarGridSpec(
            num_scalar_prefetch=1,
            grid=(nbi, nbj),
            in_specs=[pl.BlockSpec((tm, tn), lambda i, j, q: (i, j))] + [ANY] * len(after),
            out_specs=out_spec,
        ),
        out_shape=jax.ShapeDtypeStruct(full, BF16),
    )(qidx, w, *after)


def _adamw(name, w, g, m, v, after=()):
    tm, tn = _ew_tiles(w.shape, EW_BLOCK_ELEMS)
    if _nbytes(w.shape, F32) <= 1024 * 1024:
        tm, tn = w.shape
    spec = pl.BlockSpec((tm, tn), lambda i, j: (i, j))
    n_after = len(after)

    def body(w_ref, g_ref, m_ref, v_ref, *rest):
        d_ref, nm_ref, nv_ref, g_out_ref = rest[n_after:]
        gv = g_ref[...]
        g_out_ref[...] = gv
        nm = ADAM_B1 * m_ref[...] + (1.0 - ADAM_B1) * gv
        nv = ADAM_B2 * v_ref[...] + (1.0 - ADAM_B2) * (gv * gv)
        m_hat = nm / (1.0 - ADAM_B1**ADAM_STEP)
        v_hat = nv / (1.0 - ADAM_B2**ADAM_STEP)
        d_ref[...] = -ADAM_LR * (m_hat / (jnp.sqrt(v_hat) + ADAM_EPS) + ADAM_WD * w_ref[...])
        nm_ref[...] = nm
        nv_ref[...] = nv

    out = jax.ShapeDtypeStruct(w.shape, F32)
    return _pallas(
        body, name=name, grid=(w.shape[0] // tm, w.shape[1] // tn), in_specs=[spec] * 4 + [ANY] * n_after,
        out_specs=[spec] * 4, out_shape=[out, out, out, out],
        compiler_params=_params(_mm_vmem([((tm, tn), F32, 24)])),
    )(w, g, m, v, *after)


def _pair_add(name, cidx, g_full, r_sib, kind):
    hr, hc = r_sib.shape
    tm, tn = _ew_tiles((hr, hc), 2 * EW_BLOCK_ELEMS)
    nbi, nbj = hr // tm, hc // tn
    if kind == "col":
        g_spec = pl.BlockSpec((tm, tn), lambda i, j, c: (c[0] * nbi + i, j))
    else:
        g_spec = pl.BlockSpec((tm, tn), lambda i, j, c: (i, c[0] * nbj + j))
    spec = pl.BlockSpec((tm, tn), lambda i, j, c: (i, j))

    def body(c_ref, g_ref, r_ref, o_ref):
        del c_ref
        o_ref[...] = (g_ref[...].astype(F32) + r_ref[...].astype(F32)).astype(BF16)

    return _pallas(
        body,
        name=name,
        grid_spec=pltpu.PrefetchScalarGridSpec(num_scalar_prefetch=1, grid=(nbi, nbj), in_specs=[g_spec, spec], out_specs=spec),
        out_shape=jax.ShapeDtypeStruct((hr, hc), BF16),
        compiler_params=_params(_mm_vmem([((tm, tn), BF16, 6), ((tm, tn), F32, 3)])),
    )(cidx, g_full, r_sib)


def _chip_sum(name, qidx, c_half, r_ici, kind):
    _, pr, pc = r_ici.shape
    tm, tn = _ew_tiles((pr, pc), 2 * EW_BLOCK_ELEMS)
    nbi, nbj = pr // tm, pc // tn
    if kind == "col":
        own_spec = pl.BlockSpec((tm, tn), lambda i, j, q: (i, q[0] * nbj + j))
        full, out_spec = (2 * pr, pc), pl.BlockSpec((tm, tn), lambda i, j, q: (q[1] * nbi + i, j))
    else:
        own_spec = pl.BlockSpec((tm, tn), lambda i, j, q: (q[0] * nbi + i, j))
        full, out_spec = (pr, 2 * pc), pl.BlockSpec((tm, tn), lambda i, j, q: (i, q[1] * nbj + j))

    def body(q_ref, own_ref, r_ref, o_ref):
        q = q_ref[0]
        own = own_ref[...].astype(F32)
        recv = [r_ref[r].astype(F32) for r in range(3)]
        total = None
        for chip in range(N_CHIPS):
            d = chip ^ q
            term = jnp.where(d == 0, own, jnp.where(d == 2, recv[0], jnp.where(d == 1, recv[1], recv[2])))
            total = term if total is None else total + term
        o_ref[...] = total

    return _pallas(
        body,
        name=name,
        grid_spec=pltpu.PrefetchScalarGridSpec(
            num_scalar_prefetch=1,
            grid=(nbi, nbj),
            in_specs=[own_spec, pl.BlockSpec((3, tm, tn), lambda i, j, q: (0, i, j))],
            out_specs=out_spec,
        ),
        out_shape=jax.ShapeDtypeStruct(full, F32),
        compiler_params=_params(_mm_vmem([((tm, tn), BF16, 8), ((tm, tn), F32, 6)])),
    )(qidx, c_half, r_ici)


_REL_MASK = (2, 1, 3)


def _place():
    x, y, c = lax.axis_index("x"), lax.axis_index("y"), lax.axis_index("c")
    chips = [(1 - x, y), (x, 1 - y), (1 - x, 1 - y)]
    return x, y, c, 2 * x + y, chips


def _shard_view(ref, kind, chip):
    if kind == "col":
        w = ref.shape[1] // N_CHIPS
        return ref.at[:, pl.ds(pl.multiple_of(chip * w, LANES), w)]
    h = ref.shape[0] // N_CHIPS
    return ref.at[pl.ds(pl.multiple_of(chip * h, 16), h), :]


def _row_half(ref, half):
    h = ref.shape[0] // 2
    return ref.at[pl.ds(pl.multiple_of(half * h, 16), h), :]


def _pair_half(ref, kind, half):
    if kind == "col":
        return _row_half(ref, half)
    w = ref.shape[1] // 2
    return ref.at[:, pl.ds(pl.multiple_of(half * w, LANES), w)]


def _remote(src, dst, send_sem, recv_sem, dev):
    return pltpu.make_async_remote_copy(src_ref=src, dst_ref=dst, send_sem=send_sem, recv_sem=recv_sem, device_id=dev, device_id_type=MESH)


def _hbm(a):
    return pltpu.with_memory_space_constraint(a, pltpu.HBM)


def _gather_start(name, fulls, kinds, rels=(0, 1, 2), after=()):
    n_w = len(fulls)

    def body(*refs):
        g = refs[:n_w]
        send_sem, recv_sem = refs[n_w + len(after)], refs[n_w + len(after) + 1]
        token = refs[-1]
        _, _, c, q, chips = _place()
        for w in range(n_w):
            mine = _row_half(_shard_view(g[w], kinds[w], q), c)
            for r in rels if isinstance(rels, tuple) else rels[w]:
                _remote(mine, mine, send_sem.at[3 * w + r], recv_sem.at[3 * w + r], (*chips[r], c)).start()
        token[...] = jnp.zeros_like(token)

    res = _pallas(
        body,
        name=name,
        out_shape=(
            pltpu.SemaphoreType.DMA((3 * n_w,)),
            pltpu.SemaphoreType.DMA((3 * n_w,)),
            *[pltpu.HBM(f.shape, f.dtype) for f in fulls],
            jax.ShapeDtypeStruct((8, LANES), F32),
        ),
        in_specs=[HBM_SPEC] * n_w + [ANY] * len(after),
        out_specs=(SEM_SPEC, SEM_SPEC, *[HBM_SPEC] * n_w, pl.BlockSpec(memory_space=pltpu.VMEM)),
        input_output_aliases={w: w + 2 for w in range(n_w)},
        compiler_params=pltpu.CompilerParams(has_side_effects=EFFECT),
    )(*[_hbm(f) for f in fulls], *after)
    return res[0], res[1], list(res[2 : 2 + n_w]), res[-1]


def _relay_copies(kinds, waiting):
    def copies(refs, send_sem, recv_sem):
        _, _, c, q, chips = _place()
        out = []
        for i, kind in enumerate(kinds):
            for k, (src_rel, dst_rel) in enumerate(((0, 1), (1, 0))):
                held = _row_half(_row_half(_shard_view(refs[i], kind, q ^ _REL_MASK[src_rel]), c), k)
                far = _row_half(_row_half(_shard_view(refs[i], kind, q ^ _REL_MASK[2]), c), k)
                dst = far if waiting else held
                out.append(_remote(held, dst, send_sem.at[2 * i + k], recv_sem.at[2 * i + k], (*chips[dst_rel], c)))
        return out

    return copies


def _forward_copies(kinds, waiting, rels=(0, 1, 2), sem0=0):
    def copies(refs, send_sem, recv_sem):
        x, y, c, q, _ = _place()
        out = []
        for i, kind in enumerate(kinds):
            for k, r in enumerate(rels):
                quarter = _shard_view(refs[i], kind, q ^ _REL_MASK[r])
                landed = _row_half(quarter, c)
                dst = _row_half(quarter, 1 - c) if waiting else landed
                sem = sem0 + len(rels) * i + k
                out.append(_remote(landed, dst, send_sem.at[sem], recv_sem.at[sem], (x, y, 1 - c)))
        return out

    return copies


def _relay_and_forward_copies(kinds, waiting):
    forward = _forward_copies(kinds, waiting, rels=(0, 1), sem0=2 * len(kinds))
    relay = _relay_copies(kinds, waiting)
    return lambda refs, send_sem, recv_sem: forward(refs, send_sem, recv_sem) + relay(refs, send_sem, recv_sem)


def _gather_wait(name, fulls, kinds, w_ids, send_sem, recv_sem, after, rels=(0, 1, 2)):
    n = len(fulls)

    def body(*refs):
        g = refs[:n]
        s_sem, r_sem = refs[n], refs[n + 1]
        x, y, c, q, _ = _place()
        for i, w in enumerate(w_ids):
            mine = _row_half(_shard_view(g[i], kinds[i], q), c)
            for r in rels:
                landed = _row_half(_shard_view(g[i], kinds[i], q ^ _REL_MASK[r]), c)
                cp = _remote(mine, landed, s_sem.at[3 * w + r], r_sem.at[3 * w + r], (x, y, 1 - c))
                cp.wait_send()
                cp.wait_recv()

    res = _pallas(
        body,
        name=name,
        out_shape=[pltpu.HBM(f.shape, f.dtype) for f in fulls],
        in_specs=[HBM_SPEC] * n + [SEM_SPEC, SEM_SPEC, ANY],
        out_specs=[HBM_SPEC] * n,
        input_output_aliases={i: i for i in range(n)},
        compiler_params=pltpu.CompilerParams(has_side_effects=EFFECT),
    )(*fulls, send_sem, recv_sem, after)
    return list(res)


def _gather_forward(name, fulls, kinds, rels=(0, 1, 2)):
    n = len(fulls)

    def body(*refs):
        g = refs[n : 2 * n]
        send, recv = refs[2 * n :]
        _sibling_handshake()
        x, y, c, q, _ = _place()
        sib = (x, y, 1 - c)
        cps = []
        for i in range(n):
            for r in rels:
                landed = _row_half(_shard_view(g[i], kinds[i], q ^ _REL_MASK[r]), c)
                cps.append(_remote(landed, landed, send.at[i, r], recv.at[i, r], sib))
        for cp in cps:
            cp.start()
        for i in range(n):
            for r in rels:
                other = _row_half(_shard_view(g[i], kinds[i], q ^ _REL_MASK[r]), 1 - c)
                _remote(other, other, send.at[i, r], recv.at[i, r], sib).wait_recv()
        for cp in cps:
            cp.wait_send()

    res = _pallas(
        body,
        name=name,
        in_specs=[ANY] * n,
        out_specs=[ANY] * n,
        out_shape=[jax.ShapeDtypeStruct(f.shape, f.dtype) for f in fulls],
        scratch_shapes=[pltpu.SemaphoreType.DMA((n, 3)), pltpu.SemaphoreType.DMA((n, 3))],
        input_output_aliases={i: i for i in range(n)},
        compiler_params=pltpu.CompilerParams(collective_id=SIBLING_BARRIER_ID),
    )(*fulls)
    return list(res)


SIBLING_BARRIER_ID = 1


def _sibling_handshake():
    sib = (lax.axis_index("x"), lax.axis_index("y"), 1 - lax.axis_index("c"))
    barrier = pltpu.get_barrier_semaphore()
    pl.semaphore_signal(barrier, inc=1, device_id=sib, device_id_type=MESH)
    pl.semaphore_wait(barrier, 1)


def _split_start(name, bufs, n_sems, copies, sibling_only=False):
    n = len(bufs)

    def body(*refs):
        if sibling_only:
            _sibling_handshake()
        for cp in copies(refs[:n], refs[n], refs[n + 1]):
            cp.start()

    extra = {"collective_id": SIBLING_BARRIER_ID} if sibling_only else {}

    res = _pallas(
        body,
        name=name,
        out_shape=(
            pltpu.SemaphoreType.DMA((n_sems,)),
            pltpu.SemaphoreType.DMA((n_sems,)),
            *[pltpu.HBM(b.shape, b.dtype) for b in bufs],
        ),
        in_specs=[HBM_SPEC] * n,
        out_specs=(SEM_SPEC, SEM_SPEC, *[HBM_SPEC] * n),
        input_output_aliases={i: i + 2 for i in range(n)},
        compiler_params=pltpu.CompilerParams(has_side_effects=EFFECT, **extra),
    )(*[_hbm(b) for b in bufs])
    return res[0], res[1], list(res[2:])


def _split_wait(name, bufs, send_sem, recv_sem, copies, after):
    n = len(bufs)

    def body(*refs):
        for cp in copies(refs[:n], refs[n], refs[n + 1]):
            cp.wait_send()
            cp.wait_recv()

    res = _pallas(
        body,
        name=name,
        out_shape=[pltpu.HBM(b.shape, b.dtype) for b in bufs],
        in_specs=[HBM_SPEC] * n + [SEM_SPEC, SEM_SPEC, ANY],
        out_specs=[HBM_SPEC] * n,
        input_output_aliases={i: i for i in range(n)},
        compiler_params=pltpu.CompilerParams(has_side_effects=EFFECT),
    )(*bufs, send_sem, recv_sem, after)
    return list(res)


def _pair_exchange_copies(kinds):
    n = len(kinds)

    def copies(refs, send_sem, recv_sem):
        x, y, c, _, _ = _place()
        return [
            _remote(_pair_half(refs[w], kinds[w], 1 - c), refs[n + w], send_sem.at[w], recv_sem.at[w], (x, y, 1 - c))
            for w in range(n)
        ]

    return copies


def _pair_share_copies(kinds, waiting):
    def copies(refs, send_sem, recv_sem):
        x, y, c, _, _ = _place()
        out = []
        for w, kind in enumerate(kinds):
            mine = _pair_half(refs[w], kind, c)
            dst = _pair_half(refs[w], kind, 1 - c) if waiting else mine
            out.append(_remote(mine, dst, send_sem.at[w], recv_sem.at[w], (x, y, 1 - c)))
        return out

    return copies


def _piece_shape(half_shape, kind):
    r, c = half_shape
    return (3, r, c // N_CHIPS) if kind == "col" else (3, r // N_CHIPS, c)


def _chip_send_start(name, halves, kinds):
    n = len(halves)
    lands = [lax.empty(_piece_shape(h.shape, k), BF16) for h, k in zip(halves, kinds)]

    def body(*refs):
        h, land = refs[:n], refs[n : 2 * n]
        send_sem, recv_sem = refs[2 * n], refs[2 * n + 1]
        _, _, c, q, chips = _place()
        for i in range(n):
            for r, chip in enumerate(chips):
                piece = _shard_view(h[i], kinds[i], q ^ _REL_MASK[r])
                _remote(piece, land[i].at[r], send_sem.at[3 * i + r], recv_sem.at[3 * i + r], (*chip, c)).start()

    res = _pallas(
        body,
        name=name,
        out_shape=(
            pltpu.SemaphoreType.DMA((3 * n,)),
            pltpu.SemaphoreType.DMA((3 * n,)),
            *[pltpu.HBM(a.shape, a.dtype) for a in halves],
            *[pltpu.HBM(a.shape, a.dtype) for a in lands],
        ),
        in_specs=[HBM_SPEC] * (2 * n),
        out_specs=(SEM_SPEC, SEM_SPEC, *[HBM_SPEC] * (2 * n)),
        input_output_aliases={i: i + 2 for i in range(2 * n)},
        compiler_params=pltpu.CompilerParams(has_side_effects=EFFECT),
    )(*[_hbm(a) for a in halves], *[_hbm(a) for a in lands])
    return res[0], res[1], list(res[2 : 2 + n]), list(res[2 + n :])


def _chip_send_wait(name, halves, lands, kinds, send_sem, recv_sem, after):
    n = len(halves)

    def body(*refs):
        h, land = refs[:n], refs[n : 2 * n]
        s_sem, r_sem = refs[2 * n], refs[2 * n + 1]
        x, y, c, q, _ = _place()
        for i in range(n):
            for r in range(3):
                piece = _shard_view(h[i], kinds[i], q ^ _REL_MASK[r])
                cp = _remote(piece, land[i].at[r], s_sem.at[3 * i + r], r_sem.at[3 * i + r], (x, y, 1 - c))
                cp.wait_send()
                cp.wait_recv()

    res = _pallas(
        body,
        name=name,
        out_shape=[pltpu.HBM(a.shape, a.dtype) for a in halves] + [pltpu.HBM(a.shape, a.dtype) for a in lands],
        in_specs=[HBM_SPEC] * (2 * n) + [SEM_SPEC, SEM_SPEC, ANY],
        out_specs=[HBM_SPEC] * (2 * n),
        input_output_aliases={i: i for i in range(2 * n)},
        compiler_params=pltpu.CompilerParams(has_side_effects=EFFECT),
    )(*halves, *lands, send_sem, recv_sem, after)
    return list(res[:n]), list(res[n:])


def _small_exchange_copies(waiting):
    def copies(refs, send_sem, recv_sem):
        p, land = refs
        x, y, c, q, _ = _place()
        me = 2 * q + c
        out = []
        for dd in range(1, 2 * N_CHIPS):
            dev = (x ^ ((dd >> 2) & 1), y ^ ((dd >> 1) & 1), c ^ (dd & 1))
            dst = land.at[me ^ dd] if waiting else land.at[me]
            out.append(_remote(p, dst, send_sem.at[dd - 1], recv_sem.at[dd - 1], dev))
        return out

    return copies


def _small_sum(name, me_idx, p, land):
    rows = p.shape[0]
    n_dev = 2 * N_CHIPS

    def body(me_ref, p_ref, land_ref, o_ref):
        me = me_ref[0]
        total = None
        for dev in range(n_dev):
            term = jnp.where(me == dev, p_ref[...], land_ref[dev])
            total = term if total is None else total + term
        o_ref[...] = total

    return _pallas(
        body,
        name=name,
        grid_spec=pltpu.PrefetchScalarGridSpec(
            num_scalar_prefetch=1,
            grid=(1,),
            in_specs=[pl.BlockSpec((rows, LANES), lambda i, m: (0, 0)), pl.BlockSpec((n_dev, rows, LANES), lambda i, m: (0, 0, 0))],
            out_specs=pl.BlockSpec((rows, LANES), lambda i, m: (0, 0)),
        ),
        out_shape=jax.ShapeDtypeStruct(p.shape, F32),
    )(me_idx, p, land)


def _pack(parts):
    rows = []
    for a in parts:
        flat = a.reshape(-1).astype(F32)
        n = flat.shape[0]
        padded = -(-n // (8 * LANES)) * (8 * LANES)
        rows.append(jnp.pad(flat, (0, padded - n)).reshape(-1, LANES))
    return jnp.concatenate(rows, axis=0)


def _unpack(packed, shapes):
    out, row = [], 0
    for shp in shapes:
        n = int(np.prod(shp))
        nrows = -(-n // (8 * LANES)) * 8
        out.append(packed[row : row + nrows].reshape(-1)[:n].reshape(shp))
        row += nrows
    return out


def kernel(x, w_in, norm_mix, sgu_v_gain, sgu_w_s, sgu_b_s, w_a_out, attn_sink, rel_bias, w_b_out, w_o, norm_ffn, w_gate, w_up, w_down, norm_final, loss_target, m_w_in, m_norm_mix, m_sgu_v_gain, m_sgu_w_s, m_sgu_b_s, m_w_a_out, m_attn_sink, m_rel_bias, m_w_b_out, m_w_o, m_norm_ffn, m_w_gate, m_w_up, m_w_down, m_norm_final, v_w_in, v_norm_mix, v_sgu_v_gain, v_sgu_w_s, v_sgu_b_s, v_w_a_out, v_attn_sink, v_rel_bias, v_w_b_out, v_w_o, v_norm_ffn, v_w_gate, v_w_up, v_w_down, v_norm_final):
    s, d = x.shape[1], x.shape[2]
    w_sgu = sgu_v_gain.shape[1]
    groups = sgu_w_s.shape[1]
    heads = attn_sink.shape[1]
    grp = heads // N_KV_HEADS
    w_att = heads * HEAD_DIM
    w_kv = N_KV_HEADS * HEAD_DIM
    d_ff = w_gate.shape[2] * N_CHIPS
    n_in = w_in.shape[2] * N_CHIPS
    off_q = 2 * w_sgu
    off_k = off_q + w_att
    off_g = off_k + 2 * w_kv
    assert n_in == off_g + 2 * d and groups * BLK == w_sgu and s % BLK == 0

    x2d = x.reshape(s, d)
    tgt = loss_target.reshape(s, d)
    c_idx = lax.axis_index("c").astype(I32).reshape(1)
    q_idx = (2 * lax.axis_index("x") + lax.axis_index("y")).astype(I32).reshape(1)
    qc_idx = jnp.concatenate([q_idx, c_idx])

    W_IN, W_A, W_B, W_O, W_GATE, W_UP, W_DOWN = range(7)
    names = ["w_in", "w_a", "w_b", "w_o", "w_gate", "w_up", "w_down"]
    kinds = ["col", "col", "col", "row", "col", "col", "row"]
    big_w = [w_in[0], w_a_out[0], w_b_out[0], w_o[0], w_gate[0], w_up[0], w_down[0]]
    big_m = [m_w_in[0], m_w_a_out[0], m_w_b_out[0], m_w_o[0], m_w_gate[0], m_w_up[0], m_w_down[0]]
    big_v = [v_w_in[0], v_w_a_out[0], v_w_b_out[0], v_w_o[0], v_w_gate[0], v_w_up[0], v_w_down[0]]
    full_in = _cast_into_full("cast_w_in", q_idx, big_w[W_IN], kinds[W_IN])
    in_send, in_recv, (full_in,), token = _gather_start("gather_start_in", [full_in], [kinds[W_IN]], rels=(0, 1))
    rest = [_cast_into_full("cast_" + names[i], q_idx, big_w[i], kinds[i], after=(token,)) for i in range(1, 7)]

    ws_b = sgu_w_s[0].astype(BF16)
    wst_b = jnp.swapaxes(sgu_w_s[0], 1, 2).astype(BF16)
    b_col = sgu_b_s[0].reshape(groups, BLK, 1)
    bias_tab, bucket = _band_tables(rel_bias)
    sink = attn_sink[0]
    small_w = [norm_mix, sgu_v_gain, sgu_w_s, sgu_b_s, attn_sink, rel_bias, norm_ffn, norm_final]
    small_m = [m_norm_mix, m_sgu_v_gain, m_sgu_w_s, m_sgu_b_s, m_attn_sink, m_rel_bias, m_norm_ffn, m_norm_final]
    small_v = [v_norm_mix, v_sgu_v_gain, v_sgu_w_s, v_sgu_b_s, v_attn_sink, v_rel_bias, v_norm_ffn, v_norm_final]
    small_shapes = [w.shape for w in small_w]
    zero1 = jnp.zeros((1,), F32)
    pw, pm, pv = _pack(small_w + [zero1]), _pack(small_m + [zero1]), _pack(small_v + [zero1])
    h1 = _rms_fwd("rms_mix", x2d, norm_mix, after=(token, rest[-1], ws_b, wst_b, b_col, bias_tab, pw, pm, pv))
    (full_in,) = _gather_wait("gather_wait_in", [full_in], [kinds[W_IN]], [0], in_send, in_recv, h1, rels=(0, 1))
    relay_send, relay_recv, (full_in,) = _split_start(
        "gather_relay_in", [full_in], 4, _relay_and_forward_copies([kinds[W_IN]], False)
    )
    full_down = rest.pop()
    full_up = rest.pop()
    ag_send, ag_recv, rest, token = _gather_start("gather_start_rest", rest, kinds[1:5], rels=(0, 1), after=(full_in, full_up))
    (full_in,) = _split_wait(
        "gather_relay_wait_in", [full_in], relay_send, relay_recv, _relay_and_forward_copies([kinds[W_IN]], True), token
    )
    (g_in,) = _gather_forward("gather_fwd_in", [full_in], [kinds[W_IN]], rels=(2,))
    fulls = [g_in] + rest

    def relay_begin(tag, ids, after, source=None):
        ks = [kinds[i] for i in ids]
        send, recv, bufs, w_ids = source or (ag_send, ag_recv, [fulls[i] for i in ids], [i - 1 for i in ids])
        bufs = _gather_wait("gather_wait_" + tag, bufs, ks, w_ids, send, recv, after, rels=(0, 1))
        send, recv, bufs = _split_start("gather_relay_" + tag, bufs, 4 * len(ids), _relay_and_forward_copies(ks, False))
        return tag, ks, send, recv, bufs

    def relay_end(state, after):
        tag, ks, send, recv, bufs = state
        bufs = _split_wait("gather_relay_wait_" + tag, bufs, send, recv, _relay_and_forward_copies(ks, True), after)
        return _gather_forward("gather_fwd_" + tag, bufs, ks, rels=(2,))

    def relay_end_async(state, after):
        tag, ks, send, recv, bufs = state
        bufs = _split_wait("gather_relay_wait_" + tag, bufs, send, recv, _relay_and_forward_copies(ks, True), after)
        send, recv, bufs = _split_start(
            "gather_fwd_start_" + tag, bufs, len(ks), _forward_copies(ks, False, rels=(2,)), sibling_only=True
        )
        return tag, ks, send, recv, bufs

    def forwarded(state, after):
        tag, ks, send, recv, bufs = state
        return _split_wait("gather_fwd_wait_" + tag, bufs, send, recv, _forward_copies(ks, True, rels=(2,)), after)

    tm = _tile(s, (1024, 512, 256, 128))

    tn = _tile(n_in, (768, 640, 512))
    z = _mm(
        "mm_z", (s // tm, n_in // tn, 1), [h1, g_in],
        [pl.BlockSpec((tm, d), lambda i, j, k: (i, 0)), pl.BlockSpec((d, tn), lambda i, j, k: (0, j))],
        [jax.ShapeDtypeStruct((s, n_in), BF16)], [pl.BlockSpec((tm, tn), lambda i, j, k: (i, j))],
        [(0, 1, NN, 0)], 1, (tm, tn), 1, lambda ins, vals, outs, cs: _put(outs[0], cs, vals[0]),
        _mm_vmem([((tm, d), BF16, 2), ((d, tn), BF16, 2), ((tm, tn), F32, 3)]),
    )[0]
    mix_relay = relay_begin("mix", [W_A, W_B, W_O], z)
    up_send, up_recv, (full_up,), token = _gather_start(
        "gather_start_up", [full_up], [kinds[W_UP]], rels=(0, 1), after=(mix_relay[4][0],)
    )

    a_act = _sgu_fwd(z, sgu_v_gain, ws_b, b_col, w_sgu, after=(token,))

    kv_b = z[:, off_k:off_g]
    k_pad = jnp.pad(kv_b[:, :w_kv], ((BLK, BLK), (0, 0)))
    v_pad = jnp.pad(kv_b[:, w_kv:], ((BLK, BLK), (0, 0)))
    q_blk0 = off_q // (grp * HEAD_DIM)
    att = _attn_fwd(sink, z, k_pad, v_pad, bias_tab, grp, q_blk0)

    g_a, g_b, g_o = relay_end(mix_relay, att)
    gate_relay = relay_begin("gate", [W_GATE], g_a)

    tg = _tile(d, (512,))
    ga0, gb0 = off_g // tg, (off_g + d) // tg

    def ep_gate(ins, vals, outs, cs):
        sa, sb = _sigmoid(ins[4][:, cs].astype(F32)), _sigmoid(ins[5][:, cs].astype(F32))
        _put(outs[0], cs, sa * vals[0] + sb * vals[1])
        _put(outs[1], cs, vals[0])
        _put(outs[2], cs, vals[1])

    t_out = pl.BlockSpec((tm, tg), lambda i, j, k: (i, j))
    m_act, y_a, y_b = _mm(
        "mm_branches", (s // tm, d // tg, 1), [a_act, g_a, att, g_b, z, z],
        [pl.BlockSpec((tm, w_sgu), lambda i, j, k: (i, 0)), pl.BlockSpec((w_sgu, tg), lambda i, j, k: (0, j)),
         pl.BlockSpec((tm, w_att), lambda i, j, k: (i, 0)), pl.BlockSpec((w_att, tg), lambda i, j, k: (0, j)),
         pl.BlockSpec((tm, tg), lambda i, j, k: (i, ga0 + j)), pl.BlockSpec((tm, tg), lambda i, j, k: (i, gb0 + j))],
        [jax.ShapeDtypeStruct((s, d), BF16)] * 3,
        [t_out, t_out, t_out], [(0, 1, NN, 0), (2, 3, NN, 1)], 2, (tm, tg), 1, ep_gate,
        _mm_vmem([((tm, w_sgu), BF16, 4), ((w_sgu, tg), BF16, 4), ((tm, tg), F32, 12)]),
        after=(gate_relay[4][0],),
    )

    tn = _tile(d, (1024, 512))

    def ep_residual(ins, vals, outs, cs):
        _put(outs[0], cs, ins[2][:, cs] + vals[0])

    up_relay = relay_begin("up", [W_UP], m_act, source=(up_send, up_recv, [full_up], [0]))
    down_send, down_recv, (full_down,), token = _gather_start(
        "gather_start_down", [full_down], [kinds[W_DOWN]], after=(up_relay[4][0],)
    )
    x2 = _mm(
        "mm_wo", (s // tm, d // tn, 1), [m_act, g_o, x2d],
        [pl.BlockSpec((tm, d), lambda i, j, k: (i, 0)), pl.BlockSpec((d, tn), lambda i, j, k: (0, j)),
         pl.BlockSpec((tm, tn), lambda i, j, k: (i, j))],
        [jax.ShapeDtypeStruct((s, d), F32)], [pl.BlockSpec((tm, tn), lambda i, j, k: (i, j))],
        [(0, 1, NN, 0)], 1, (tm, tn), 1, ep_residual,
        _mm_vmem([((tm, d), BF16, 2), ((d, tn), BF16, 2), ((tm, tn), F32, 5)]),
        after=(token,),
    )[0]
    gate_fwd = relay_end_async(gate_relay, x2)
    up_fwd = relay_end_async(up_relay, gate_fwd[4][0])
    h2 = _rms_fwd("rms_ffn", x2, norm_ffn, after=(up_fwd[4][0],))
    (g_gate,) = forwarded(gate_fwd, h2)
    (g_up,) = forwarded(up_fwd, g_gate)

    tf = _tile(d_ff, (512,))

    def ep_swiglu(ins, vals, outs, cs):
        gt, up = vals
        _put(outs[0], cs, gt)
        _put(outs[1], cs, up)
        _put(outs[2], cs, (gt * _sigmoid(gt)) * up)

    f_out = pl.BlockSpec((tm, tf), lambda i, j, k: (i, j))
    gt, up, f_act = _mm(
        "mm_gate_up", (s // tm, d_ff // tf, 1), [h2, g_gate, g_up],
        [pl.BlockSpec((tm, d), lambda i, j, k: (i, 0)), pl.BlockSpec((d, tf), lambda i, j, k: (0, j)),
         pl.BlockSpec((d, tf), lambda i, j, k: (0, j))],
        [jax.ShapeDtypeStruct((s, d_ff), BF16)] * 3,
        [f_out, f_out, f_out], [(0, 1, NN, 0), (0, 2, NN, 1)], 2, (tm, tf), 1, ep_swiglu,
        _mm_vmem([((tm, d), BF16, 2), ((d, tf), BF16, 4), ((tm, tf), F32, 8)]),    )
    (g_down,) = _gather_forward(
        "gather_fwd_ffn_out",
        _gather_wait("gather_wait_ffn_out", [full_down], [kinds[W_DOWN]], [0], down_send, down_recv, f_act),
        [kinds[W_DOWN]],
    )

    tkf = _tile(d_ff, (1408, 1024, 512))
    tml, tnl = _tile(s, (512, 256, 128)), _tile(d, (512,))
    x3 = _mm(
        "mm_down", (s // tml, d // tnl, 1), [f_act, g_down, x2],
        [pl.BlockSpec((tml, d_ff), lambda i, j, k: (i, 0)), pl.BlockSpec((d_ff, tnl), lambda i, j, k: (0, j)),
         pl.BlockSpec((tml, tnl), lambda i, j, k: (i, j))],
        [jax.ShapeDtypeStruct((s, d), F32)], [pl.BlockSpec((tml, tnl), lambda i, j, k: (i, j))],
        [(0, 1, NN, 0)], 1, (tml, tnl), 1, ep_residual,
        _mm_vmem([((tml, d_ff), BF16, 2), ((d_ff, tnl), BF16, 2), ((tml, tnl), F32, 6)]),
        lookahead=(0,),
    )[0]

    dx3, dx3b, dg_final, loss_part = _head(x3, norm_final.reshape(1, d), tgt)

    def reduce_a(tag, ids, grads):
        ks = [kinds[i] for i in ids]
        lands = [lax.empty((g.shape[0] // 2, g.shape[1]) if k == "col" else (g.shape[0], g.shape[1] // 2), BF16)
                 for g, k in zip(grads, ks)]
        send, recv, bufs = _split_start("pair_send_" + tag, list(grads) + lands, len(ids), _pair_exchange_copies(ks), sibling_only=True)
        return {"tag": tag, "ids": ids, "ks": ks, "pair": (send, recv, bufs), "token": bufs[0]}

    def reduce_b(st, after):
        tag, ids, ks = st["tag"], st["ids"], st["ks"]
        send, recv, bufs = st["pair"]
        bufs = _split_wait("pair_wait_" + tag, bufs, send, recv, _pair_exchange_copies(ks), after)
        grads, from_sib = bufs[: len(ids)], bufs[len(ids) :]
        halves = [_pair_add("pair_add_" + names[i], c_idx, g, r, k) for i, g, r, k in zip(ids, grads, from_sib, ks)]
        st["chip"] = _chip_send_start("chip_send_" + tag, halves, ks)
        st["token"] = st["chip"][2][0]

    def reduce_c(st, after):
        tag, ids, ks = st["tag"], st["ids"], st["ks"]
        send, recv, halves, lands = st["chip"]
        halves, lands = _chip_send_wait("chip_wait_" + tag, halves, lands, ks, send, recv, after)
        pieces = [_chip_sum("chip_sum_" + names[i], qc_idx, h, r, k) for i, h, r, k in zip(ids, halves, lands, ks)]
        st["share"] = _split_start("share_send_" + tag, pieces, len(ids), _pair_share_copies(ks, False), sibling_only=True)
        st["token"] = st["share"][2][0]

    def reduce_d(st, after):
        send, recv, bufs = st["share"]
        return _split_wait("share_wait_" + st["tag"], bufs, send, recv, _pair_share_copies(st["ks"], True), after)

    grads_big, upd = [None] * 7, [None] * 7

    def update(i, g, after=()):
        upd[i] = _adamw("adamw_" + names[i], big_w[i], g, big_m[i], big_v[i], after=after)
        grads_big[i] = upd[i][3]
        return upd[i][0]

    def finish(st, after):
        shared = reduce_d(st, after)
        after = shared[0]
        for i, g in zip(st["ids"], shared):
            after = update(i, g, (after,))
        return after

    def ep_swiglu_bwd(ins, vals, outs, cs):
        df = vals[0]
        gtv, upv = ins[2][:, cs].astype(F32), ins[3][:, cs].astype(F32)
        sg = _sigmoid(gtv)
        _put(outs[0], cs, df * upv * (sg + gtv * sg * (1.0 - sg)))
        _put(outs[1], cs, df * (gtv * sg))

    dgt, dup = _mm(
        "mm_dswiglu", (s // tm, d_ff // tf, 1), [dx3b, g_down, gt, up],
        [pl.BlockSpec((tm, d), lambda i, j, k: (i, 0)), pl.BlockSpec((tf, d), lambda i, j, k: (j, 0)), f_out, f_out],
        [jax.ShapeDtypeStruct((s, d_ff), BF16), jax.ShapeDtypeStruct((s, d_ff), BF16)], [f_out, f_out],
        [(0, 1, NT, 0)], 1, (tm, tf), 1, ep_swiglu_bwd,
        _mm_vmem([((tm, d), BF16, 2), ((tf, d), BF16, 2), ((tm, tf), F32, 8)]),    )

    def ep_store(ins, vals, outs, cs):
        for o, v in zip(outs, vals):
            _put(o, cs, v)

    twn = _tile(d, (1024, 512))
    gw_down = _mm(
        "mm_gw_down", (d_ff // tkf, d // twn, 1), [f_act, dx3b],
        [pl.BlockSpec((s, tkf), lambda i, j, k: (0, i)), pl.BlockSpec((s, twn), lambda i, j, k: (0, j))],
        [jax.ShapeDtypeStruct((d_ff, d), BF16)], [pl.BlockSpec((tkf, twn), lambda i, j, k: (i, j))],
        [(0, 1, TN, 0)], 1, (tkf, twn), 1, ep_store,
        _mm_vmem([((s, tkf), BF16, 3), ((s, twn), BF16, 2), ((tkf, twn), F32, 3)]),
    )[0]
    red_down = reduce_a("down", [W_DOWN], [gw_down])

    tn2 = _tile(d, (256,))
    dh2_specs = [pl.BlockSpec((tm, d_ff), lambda i, j, k: (i, 0)), pl.BlockSpec((tn2, d_ff), lambda i, j, k: (j, 0))]
    dh2_tile = pl.BlockSpec((tm, tn2), lambda i, j, k: (i, j))
    dh2_vmem = _mm_vmem([((tm, d_ff), BF16, 2), ((tn2, d_ff), BF16, 2), ((tm, tn2), F32, 7)])
    dh2 = _mm(
        "mm_dh2_gate", (s // tm, d // tn2, 1), [dgt, g_gate], dh2_specs,
        [jax.ShapeDtypeStruct((s, d), F32)], [dh2_tile], [(0, 1, NT, 0)], 1, (tm, tn2), 1, ep_store, dh2_vmem,
        after=(red_down["token"],), lookahead=(0,),
    )[0]
    dh2 = _mm(
        "mm_dh2_up", (s // tm, d // tn2, 1), [dup, g_up, dh2], dh2_specs + [dh2_tile],
        [jax.ShapeDtypeStruct((s, d), F32)], [dh2_tile], [(0, 1, NT, 0)], 1, (tm, tn2), 1, ep_residual, dh2_vmem,
        lookahead=(0,),
    )[0]
    reduce_b(red_down, dh2)

    twr = _tile(d, (1024, 512))
    w_tile = pl.BlockSpec((twr, tf), lambda i, j, k: (i, j))
    gw_gate, gw_up = _mm(
        "mm_gw_gate_up", (d // twr, d_ff // tf, 1), [h2, dgt, dup],
        [pl.BlockSpec((s, twr), lambda i, j, k: (0, i)), pl.BlockSpec((s, tf), lambda i, j, k: (0, j)),
         pl.BlockSpec((s, tf), lambda i, j, k: (0, j))],
        [jax.ShapeDtypeStruct((d, d_ff), BF16), jax.ShapeDtypeStruct((d, d_ff), BF16)], [w_tile, w_tile],
        [(0, 1, TN, 0), (0, 2, TN, 1)], 2, (twr, tf), 1, ep_store,
        _mm_vmem([((s, twr), BF16, 3), ((s, tf), BF16, 4), ((twr, tf), F32, 6)]),
        after=(red_down["token"],),
    )
    red_ffn = reduce_a("ffn_in", [W_GATE, W_UP], [gw_gate, gw_up])

    dx2, dx2b, dg_ffn = _rms_bwd("rms_ffn_bwd", x2, norm_ffn, dh2, dx3, after=(red_ffn["token"],))

    nj = d // tg

    def lo(j):
        return jnp.minimum(j, nj - 1)

    def gate_bwd_body(dx_ref, wo_ref, ga_ref, gb_ref, ya_ref, yb_ref, dya_ref, dyb_ref, dz_ref, keep):
        j = pl.program_id(1)

        @pl.when(j < nj)
        def _():
            dm = lax.dot_general(dx_ref[...], wo_ref[...], NT, preferred_element_type=F32)
            sa, sb = _sigmoid(ga_ref[...].astype(F32)), _sigmoid(gb_ref[...].astype(F32))
            dya_ref[...] = (dm * sa).astype(BF16)
            dyb_ref[...] = (dm * sb).astype(BF16)
            dz_ref[...] = (dm * ya_ref[...].astype(F32) * (sa * (1.0 - sa))).astype(BF16)
            keep[lo(j)] = (dm * yb_ref[...].astype(F32) * (sb * (1.0 - sb))).astype(BF16)

        @pl.when(j >= nj)
        def _():
            dz_ref[...] = keep[jnp.maximum(j - nj, 0)]

    t_lo = pl.BlockSpec((tm, tg), lambda i, j: (i, lo(j)))
    dya, dyb, dz = _pallas(
        gate_bwd_body,
        name="mm_dgate",
        grid=(s // tm, 2 * nj),
        in_specs=[
            pl.BlockSpec((tm, d), lambda i, j: (i, 0)),
            pl.BlockSpec((tg, d), lambda i, j: (lo(j), 0)),
            pl.BlockSpec((tm, tg), lambda i, j: (i, ga0 + lo(j))),
            pl.BlockSpec((tm, tg), lambda i, j: (i, gb0 + lo(j))),
            t_lo,
            t_lo,
        ],
        out_specs=[t_lo, t_lo, pl.BlockSpec((tm, tg), lambda i, j: (i, ga0 + j))],
        out_shape=[jax.ShapeDtypeStruct((s, d), BF16), jax.ShapeDtypeStruct((s, d), BF16), jax.ShapeDtypeStruct((s, n_in), BF16)],
        scratch_shapes=[pltpu.VMEM((nj, tm, tg), BF16)],
        compiler_params=_params(_mm_vmem([((tm, d), BF16, 2), ((tg, d), BF16, 2), ((tm, tg), F32, 14), ((nj, tm, tg), BF16, 1)])),
    )(dx2b, g_o, z, z, y_a, y_b)
    reduce_b(red_ffn, dya)
    reduce_c(red_down, red_ffn["token"])

    gw_o = _mm(
        "mm_gw_o", (d // twr, d // twn, 1), [m_act, dx2b],
        [pl.BlockSpec((s, twr), lambda i, j, k: (0, i)), pl.BlockSpec((s, twn), lambda i, j, k: (0, j))],
        [jax.ShapeDtypeStruct((d, d), BF16)], [pl.BlockSpec((twr, twn), lambda i, j, k: (i, j))],
        [(0, 1, TN, 0)], 1, (twr, twn), 1, ep_store,
        _mm_vmem([((s, twr), BF16, 3), ((s, twn), BF16, 2), ((twr, twn), F32, 3)]),
        after=(red_down["token"],),
    )[0]
    after_down = finish(red_down, gw_o)

    tb = _tile(w_sgu, (1024, 512))
    b_out = pl.BlockSpec((tm, tb), lambda i, j, k: (i, j))

    da, datt = _mm(
        "mm_dbranches", (s // tm, w_sgu // tb, 1), [dya, g_a, dyb, g_b],
        [pl.BlockSpec((tm, d), lambda i, j, k: (i, 0)), pl.BlockSpec((tb, d), lambda i, j, k: (j, 0)),
         pl.BlockSpec((tm, d), lambda i, j, k: (i, 0)), pl.BlockSpec((tb, d), lambda i, j, k: (j, 0))],
        [jax.ShapeDtypeStruct((s, w_sgu), BF16), jax.ShapeDtypeStruct((s, w_att), BF16)], [b_out, b_out],
        [(0, 1, NT, 0), (2, 3, NT, 1)], 2, (tm, tb), 1, ep_store,
        _mm_vmem([((tm, d), BF16, 4), ((tb, d), BF16, 4), ((tm, tb), F32, 6)]),
        after=(after_down,),
    )

    wb_tile = pl.BlockSpec((tb, twn), lambda i, j, k: (i, j))
    gw_a, gw_b = _mm(
        "mm_gw_branches", (w_sgu // tb, d // twn, 1), [a_act, dya, att, dyb],
        [pl.BlockSpec((s, tb), lambda i, j, k: (0, i)), pl.BlockSpec((s, twn), lambda i, j, k: (0, j)),
         pl.BlockSpec((s, tb), lambda i, j, k: (0, i)), pl.BlockSpec((s, twn), lambda i, j, k: (0, j))],
        [jax.ShapeDtypeStruct((w_sgu, d), BF16), jax.ShapeDtypeStruct((w_att, d), BF16)], [wb_tile, wb_tile],
        [(0, 1, TN, 0), (2, 3, TN, 1)], 2, (tb, twn), 1, ep_store,
        _mm_vmem([((s, tb), BF16, 5), ((s, twn), BF16, 4), ((tb, twn), F32, 6)]),
        after=(da,),
    )
    red_mix = reduce_a("mix", [W_O, W_A, W_B], [gw_o, gw_a, gw_b])

    dz, dws, dbs, dgain = _sgu_bwd(z, da, sgu_v_gain, ws_b, wst_b, b_col, w_sgu, dz, after=(red_mix["token"],))
    dz, dk_pad, dv_pad, dbias_tab, dsink = _attn_bwd(sink, z, k_pad, v_pad, bias_tab, datt, dz, grp, q_blk0)
    dz = _dkv_to_dz(dk_pad, dv_pad, dz, off_k // (2 * w_kv))
    drel = _relbias_bwd(dbias_tab, bucket)
    reduce_b(red_mix, dz)
    reduce_c(red_ffn, red_mix["token"])

    early = [dgain, dws, dbs, dsink[:, 0], drel[:, :REL_BUCKETS].T, dg_ffn, dg_final]
    p_early = _pack([g.reshape(shp) for g, shp in zip(early, small_shapes[1:])] + [loss_part[0, :1]])
    land = jnp.zeros((2 * N_CHIPS,) + p_early.shape, F32)
    sm_send, sm_recv, (p_early, land) = _split_start("small_send", [p_early, land], 2 * N_CHIPS - 1, _small_exchange_copies(False))

    tzn = _tile(n_in, (768, 640, 512))
    gw_in = _mm(
        "mm_gw_in", (d // twr, n_in // tzn, 1), [h1, dz],
        [pl.BlockSpec((s, twr), lambda i, j, k: (0, i)), pl.BlockSpec((s, tzn), lambda i, j, k: (0, j))],
        [jax.ShapeDtypeStruct((d, n_in), BF16)], [pl.BlockSpec((twr, tzn), lambda i, j, k: (i, j))],
        [(0, 1, TN, 0)], 1, (twr, tzn), 1, ep_store,
        _mm_vmem([((s, twr), BF16, 3), ((s, tzn), BF16, 2), ((twr, tzn), F32, 3)]),
        after=(red_ffn["token"], p_early),
    )[0]
    red_in = reduce_a("w_in", [W_IN], [gw_in])

    g_gate, g_up = reduce_d(red_ffn, red_in["token"])
    reduce_b(red_in, update(W_GATE, g_gate))

    dh1 = _mm(
        "mm_dh1", (s // tm, d // tn2, 1), [dz, g_in],
        [pl.BlockSpec((tm, n_in), lambda i, j, k: (i, 0)), pl.BlockSpec((tn2, n_in), lambda i, j, k: (j, 0))],
        [jax.ShapeDtypeStruct((s, d), F32)], [pl.BlockSpec((tm, tn2), lambda i, j, k: (i, j))],
        [(0, 1, NT, 0)], 1, (tm, tn2), 1, ep_store,
        _mm_vmem([((tm, n_in), BF16, 2), ((tn2, n_in), BF16, 2), ((tm, tn2), F32, 5)]),
        after=(red_in["token"],), lookahead=(0,),
    )[0]

    reduce_c(red_mix, dh1)
    grad_x, _, dg_mix = _rms_bwd("rms_mix_bwd", x2d, norm_mix, dh1, dx2, after=(red_mix["token"],))

    p_mix = _pack([dg_mix.reshape(small_shapes[0])])
    land_mix = jnp.zeros((2 * N_CHIPS,) + p_mix.shape, F32)
    mx_send, mx_recv, (p_mix, land_mix) = _split_start("mix_send", [p_mix, land_mix], 2 * N_CHIPS - 1, _small_exchange_copies(False))

    reduce_c(red_in, update(W_UP, g_up, (finish(red_mix, p_mix),)))
    p_early, land = _split_wait("small_wait", [p_early, land], sm_send, sm_recv, _small_exchange_copies(True), red_in["token"])
    p_mix, land_mix = _split_wait("mix_wait", [p_mix, land_mix], mx_send, mx_recv, _small_exchange_copies(True), p_early)
    me_idx = 2 * q_idx + c_idx
    packed_g = jnp.concatenate([_small_sum("mix_sum", me_idx, p_mix, land_mix), _small_sum("small_sum", me_idx, p_early, land)], axis=0)
    g_small = _unpack(packed_g, small_shapes + [(1,)])
    loss = g_small[-1].reshape(())
    g_small = g_small[:-1]
    pg = _pack(g_small + [zero1])
    small_upd = _adamw("adamw_small", pw, pg, pm, pv)
    d_small, nm_small, nv_small = [_unpack(a, small_shapes) for a in small_upd[:3]]
    finish(red_in, small_upd[0])

    small_names = ["norm_mix", "sgu_v_gain", "sgu_w_s", "sgu_b_s", "attn_sink", "rel_bias", "norm_ffn", "norm_final"]
    table = {}
    for i, n in enumerate(names):
        table[n] = (grads_big[i][None], upd[i][0][None], upd[i][1][None], upd[i][2][None])
    for i, n in enumerate(small_names):
        table[n] = (g_small[i], d_small[i], nm_small[i], nv_small[i])
    order = ["w_in", "norm_mix", "sgu_v_gain", "sgu_w_s", "sgu_b_s", "w_a", "attn_sink", "rel_bias", "w_b", "w_o", "norm_ffn",
             "w_gate", "w_up", "w_down", "norm_final"]
    outs = [loss, grad_x.reshape(1, s, d)]
    for part in range(4):
        outs += [table[n][part] for n in order]
    return tuple(outs)
```

```python
import math

import jax
import jax.numpy as jnp
import numpy as np
from jax import lax
from jax.experimental import pallas as pl
from jax.experimental.pallas import tpu as pltpu

F32 = jnp.float32
BF16 = jnp.bfloat16
I32 = jnp.int32
MESH = pl.DeviceIdType.MESH

EPS = 1e-6
NEG = -1e30
BLK = 128
HEAD_DIM = 128
N_KV_HEADS = 2
REL_BUCKETS = 32
REL_MAX_DIST = 128
N_CHIPS = 4
ADAM_LR, ADAM_B1, ADAM_B2, ADAM_EPS, ADAM_WD, ADAM_STEP = 0.001, 0.9, 0.999, 1e-08, 0.01, 10

LANES = 128
VMEM_CAP = 60 * 1024 * 1024

NN = (((1,), (0,)), ((), ()))
NT = (((1,), (1,)), ((), ()))
TN = (((0,), (0,)), ((), ()))
ANY = pl.BlockSpec(memory_space=pl.ANY)
HBM_SPEC = pl.BlockSpec(memory_space=pltpu.HBM)
SEM_SPEC = pl.BlockSpec(memory_space=pltpu.SEMAPHORE)
EFFECT = pltpu.SideEffectType.DATAFLOW_SIDE_EFFECTING


def _tile(n, cands):
    for t in cands:
        if n % t == 0:
            return t
    return n


PIN_BYTES = 64 * 1024


def _pin_hbm(a):
    big = hasattr(a, "dtype") and jnp.issubdtype(a.dtype, jnp.floating) and _nbytes(a.shape, a.dtype) >= PIN_BYTES
    return pltpu.with_memory_space_constraint(a, pltpu.HBM) if big else a


def _pallas(body, *, out_shape, **kw):
    def pin(o):
        big = isinstance(o, jax.ShapeDtypeStruct) and jnp.issubdtype(o.dtype, jnp.floating) and _nbytes(o.shape, o.dtype) >= PIN_BYTES
        return pltpu.HBM(o.shape, o.dtype) if big else o

    shapes = type(out_shape)(pin(o) for o in out_shape) if isinstance(out_shape, (list, tuple)) else pin(out_shape)
    call = pl.pallas_call(body, out_shape=shapes, **kw)
    return lambda *args: call(*[_pin_hbm(a) for a in args])


def _params(vmem_bytes=None, **kw):
    if vmem_bytes is not None:
        kw["vmem_limit_bytes"] = int(min(max(vmem_bytes, 32 * 1024 * 1024), VMEM_CAP))
    return pltpu.CompilerParams(**kw)


def _nbytes(shape, dtype):
    return int(np.prod(shape)) * jnp.dtype(dtype).itemsize


def _sigmoid(x):
    return 1.0 / (1.0 + jnp.exp(-x))


_GC = 0.7978845608028654
_GA = 0.044715


def _gelu(x):
    return 0.5 * x * (1.0 + jnp.tanh(_GC * (x + _GA * (x * x * x))))


def _gelu_grad(x):
    t = jnp.tanh(_GC * (x + _GA * (x * x * x)))
    return 0.5 * (1.0 + t) + 0.5 * x * (1.0 - t * t) * (_GC * (1.0 + 3.0 * _GA * (x * x)))


def _bf(v):
    return v if v.dtype == BF16 else v.astype(BF16)


def _mm(name, grid, ins, in_specs, out_shape, out_specs, pairs, n_acc, tile, nk, epilogue, vmem_bytes, after=(), lookahead=()):
    assert nk == 1
    n_in, n_out = len(ins) + len(after), len(out_shape)

    def body(*refs):
        in_refs, out_refs = refs[:n_in], refs[n_in : n_in + n_out]
        vals = [None] * n_acc
        for a_i, b_i, dn, acc_i in pairs:
            d = lax.dot_general(_bf(in_refs[a_i][...]), _bf(in_refs[b_i][...]), dn, preferred_element_type=F32)
            vals[acc_i] = d if vals[acc_i] is None else vals[acc_i] + d
        epilogue(in_refs, vals, out_refs, slice(None))

    if lookahead:
        ahead = pl.Buffered(2, use_lookahead=True)
        specs = [pl.BlockSpec(sp.block_shape, sp.index_map, pipeline_mode=ahead) if i in lookahead else sp for i, sp in enumerate(in_specs)]
        n_read = len(ins)

        def piped(*refs):
            def step(*blocks):
                body(*blocks[:n_read], *[None] * len(after), *blocks[n_read:])

            pltpu.emit_pipeline(step, grid=grid, in_specs=specs, out_specs=list(out_specs))(*refs[:n_read], *refs[n_in:])

        return _pallas(
            piped,
            name=name,
            in_specs=[ANY] * n_in,
            out_specs=[ANY] * n_out,
            out_shape=out_shape,
            compiler_params=_params(vmem_bytes),
        )(*ins, *after)

    return _pallas(
        body,
        name=name,
        grid=grid,
        in_specs=list(in_specs) + [ANY] * len(after),
        out_specs=out_specs,
        out_shape=out_shape,
        compiler_params=_params(vmem_bytes),
    )(*ins, *after)


def _put(ref, cs, v):
    ref[:, cs] = v.astype(ref.dtype)


def _mm_vmem(tiles):
    return sum(_nbytes(s, d) * c for s, d, c in tiles) + 4 * 1024 * 1024


def _rows8(v):
    r, d = v.shape
    return v.reshape(r // 8, 8, d).sum(axis=0)


def _rms_fwd(name, x, g, after=()):
    s, d = x.shape
    tm = _tile(s, (256, 128))

    def body(x_ref, g_ref, *rest):
        h_ref = rest[-1]
        xv = x_ref[...]
        r = lax.rsqrt(jnp.mean(xv * xv, axis=-1, keepdims=True) + EPS)
        h_ref[...] = ((xv * r) * g_ref[...]).astype(BF16)

    return _pallas(
        body,
        name=name,
        grid=(s // tm,),
        in_specs=[pl.BlockSpec((tm, d), lambda i: (i, 0)), pl.BlockSpec((1, d), lambda i: (0, 0))] + [ANY] * len(after),
        out_specs=pl.BlockSpec((tm, d), lambda i: (i, 0)),
        out_shape=jax.ShapeDtypeStruct((s, d), BF16),
    )(x, g, *after)


def _rms_bwd(name, x, g, dh, dres, after=()):
    s, d = x.shape
    tm = _tile(s, (256, 128))
    n = s // tm
    n_after = len(after)

    def body(x_ref, g_ref, dh_ref, dres_ref, *rest):
        dx_ref, dxb_ref, dg_ref, acc_ref = rest[n_after:]
        i = pl.program_id(0)
        xv = x_ref[...]
        r = lax.rsqrt(jnp.mean(xv * xv, axis=-1, keepdims=True) + EPS)
        xh = xv * r
        dhv = dh_ref[...]
        dxh = dhv * g_ref[...]
        dx = r * (dxh - xh * jnp.mean(dxh * xh, axis=-1, keepdims=True)) + dres_ref[...]
        dx_ref[...] = dx
        dxb_ref[...] = dx.astype(BF16)
        part = _rows8(dhv * xh)

        @pl.when(i == 0)
        def _():
            acc_ref[...] = part

        @pl.when(i > 0)
        def _():
            acc_ref[...] += part

        @pl.when(i == n - 1)
        def _():
            dg_ref[...] = jnp.sum(acc_ref[...], axis=0, keepdims=True)

    row = pl.BlockSpec((tm, d), lambda i: (i, 0))
    vec = pl.BlockSpec((1, d), lambda i: (0, 0))
    return _pallas(
        body,
        name=name,
        grid=(n,),
        in_specs=[row, vec, row, row] + [ANY] * n_after,
        out_specs=[row, row, vec],
        out_shape=[jax.ShapeDtypeStruct((s, d), F32), jax.ShapeDtypeStruct((s, d), BF16), jax.ShapeDtypeStruct((1, d), F32)],
        scratch_shapes=[pltpu.VMEM((8, d), F32)],
    )(x, g, dh, dres, *after)


def _head(x3, g, target):
    s, d = x3.shape
    tm = _tile(s, (256, 128))
    n = s // tm

    def body(x_ref, g_ref, t_ref, dx_ref, dxb_ref, dg_ref, loss_ref, acc_g, acc_l):
        i = pl.program_id(0)
        xv = x_ref[...]
        gv = g_ref[...]
        r = lax.rsqrt(jnp.mean(xv * xv, axis=-1, keepdims=True) + EPS)
        xh = xv * r
        e = xh * gv - t_ref[...]
        dy = e * (1.0 / d)
        dxh = dy * gv
        dx = r * (dxh - xh * jnp.mean(dxh * xh, axis=-1, keepdims=True))
        dx_ref[...] = dx
        dxb_ref[...] = dx.astype(BF16)
        pg = _rows8(dy * xh)
        plo = _rows8(e * e)

        @pl.when(i == 0)
        def _():
            acc_g[...] = pg
            acc_l[...] = plo

        @pl.when(i > 0)
        def _():
            acc_g[...] += pg
            acc_l[...] += plo

        @pl.when(i == n - 1)
        def _():
            dg_ref[...] = jnp.sum(acc_g[...], axis=0, keepdims=True)
            loss_ref[...] = jnp.full((1, LANES), (0.5 / d) * jnp.sum(acc_l[...]), F32)

    row = pl.BlockSpec((tm, d), lambda i: (i, 0))
    vec = pl.BlockSpec((1, d), lambda i: (0, 0))
    return _pallas(
        body,
        name="head",
        grid=(n,),
        in_specs=[row, vec, row],
        out_specs=[row, row, vec, pl.BlockSpec((1, LANES), lambda i: (0, 0))],
        out_shape=[
            jax.ShapeDtypeStruct((s, d), F32),
            jax.ShapeDtypeStruct((s, d), BF16),
            jax.ShapeDtypeStruct((1, d), F32),
            jax.ShapeDtypeStruct((1, LANES), F32),
        ],
        scratch_shapes=[pltpu.VMEM((8, d), F32), pltpu.VMEM((8, d), F32)],
    )(x3, g, target)


def _sgu_fwd(z, gain, ws_b, b_col, w_sgu, after=()):
    s = z.shape[0]
    groups = ws_b.shape[0]

    def body(zu_ref, zv_ref, gain_ref, ws_ref, b_ref, *rest):
        a_ref = rest[-1]
        vv = _gelu(zv_ref[...].astype(F32))
        r = lax.rsqrt(jnp.mean(vv * vv, axis=-1, keepdims=True) + EPS)
        vn = ((vv * r) * gain_ref[...]).astype(BF16)
        u = _gelu(zu_ref[...].astype(F32))
        for g in range(groups):
            sl = slice(g * BLK, (g + 1) * BLK)
            mixed = jnp.dot(ws_ref[g], vn[:, sl], preferred_element_type=F32) + b_ref[g]
            a_ref[:, sl] = (u[:, sl] * mixed).astype(BF16)

    return _pallas(
        body,
        name="sgu_fwd",
        grid=(s // BLK,),
        in_specs=[
            pl.BlockSpec((BLK, w_sgu), lambda c: (c, 0)),
            pl.BlockSpec((BLK, w_sgu), lambda c: (c, 1)),
            pl.BlockSpec((1, w_sgu), lambda c: (0, 0)),
            pl.BlockSpec((groups, BLK, BLK), lambda c: (0, 0, 0)),
            pl.BlockSpec((groups, BLK, 1), lambda c: (0, 0, 0)),
        ]
        + [ANY] * len(after),
        out_specs=pl.BlockSpec((BLK, w_sgu), lambda c: (c, 0)),
        out_shape=jax.ShapeDtypeStruct((s, w_sgu), BF16),
    )(z, z, gain, ws_b, b_col, *after)


def _sgu_bwd(z, da, gain, ws_b, wst_b, b_col, w_sgu, dz, after=()):
    s = z.shape[0]
    groups = ws_b.shape[0]
    n = s // BLK
    n_skip = 1 + len(after)

    def body(zu_ref, zv_ref, da_ref, gain_ref, ws_ref, wst_ref, b_ref, *rest):
        dz_ref, dws_ref, dbs_ref, dgain_ref, acc_gain, vv_s, gv_s, dxh_s = rest[n_skip:]
        c = pl.program_id(0)
        cols = [slice(g * BLK, (g + 1) * BLK) for g in range(groups)]

        ss = jnp.zeros((BLK, 1), F32)
        for sl in cols:
            zv = zv_ref[:, sl].astype(F32)
            vv = _gelu(zv)
            vv_s[:, sl] = vv
            gv_s[:, sl] = _gelu_grad(zv)
            ss = ss + jnp.sum(vv * vv, axis=-1, keepdims=True)
        r = lax.rsqrt(ss * (1.0 / w_sgu) + EPS)

        dot_dx = jnp.zeros((BLK, 1), F32)
        for g, sl in enumerate(cols):
            gain_g = gain_ref[:, sl]
            xh = vv_s[:, sl] * r
            vn = (xh * gain_g).astype(BF16)
            zu = zu_ref[:, sl].astype(F32)
            dav = da_ref[:, sl].astype(F32)
            dmix = dav * _gelu(zu)
            dmix_b = dmix.astype(BF16)
            mixed = jnp.dot(ws_ref[g], vn, preferred_element_type=F32) + b_ref[g]
            dz_ref[:, sl] = (dav * mixed * _gelu_grad(zu)).astype(BF16)
            dvn = jnp.dot(wst_ref[g], dmix_b, preferred_element_type=F32)
            dws_g = lax.dot_general(dmix_b, vn, NT, preferred_element_type=F32)
            dbs_g = jnp.sum(dmix, axis=1, keepdims=True)
            pg = _rows8(dvn * xh)

            @pl.when(c == 0)
            def _():
                dws_ref[g] = dws_g
                dbs_ref[g] = dbs_g
                acc_gain[:, sl] = pg

            @pl.when(c > 0)
            def _():
                dws_ref[g] += dws_g
                dbs_ref[g] += dbs_g
                acc_gain[:, sl] += pg

            dxh = dvn * gain_g
            dxh_s[:, sl] = dxh
            dot_dx = dot_dx + jnp.sum(dxh * xh, axis=-1, keepdims=True)

        mean_dx = dot_dx * (1.0 / w_sgu)
        for g, sl in enumerate(cols):
            dvv = r * (dxh_s[:, sl] - (vv_s[:, sl] * r) * mean_dx)
            dz_ref[:, w_sgu + g * BLK : w_sgu + (g + 1) * BLK] = (dvv * gv_s[:, sl]).astype(BF16)

        @pl.when(c == n - 1)
        def _():
            dgain_ref[...] = jnp.sum(acc_gain[...], axis=0, keepdims=True)

    full3 = pl.BlockSpec((groups, BLK, BLK), lambda c: (0, 0, 0))
    col3 = pl.BlockSpec((groups, BLK, 1), lambda c: (0, 0, 0))
    vec = pl.BlockSpec((1, w_sgu), lambda c: (0, 0))
    return _pallas(
        body,
        name="sgu_bwd",
        grid=(n,),
        in_specs=[
            pl.BlockSpec((BLK, w_sgu), lambda c: (c, 0)),
            pl.BlockSpec((BLK, w_sgu), lambda c: (c, 1)),
            pl.BlockSpec((BLK, w_sgu), lambda c: (c, 0)),
            vec,
            full3,
            full3,
            col3,
            ANY,
        ]
        + [ANY] * len(after),
        out_specs=[pl.BlockSpec((BLK, 2 * w_sgu), lambda c: (c, 0)), full3, col3, vec],
        out_shape=[
            jax.ShapeDtypeStruct(dz.shape, BF16),
            jax.ShapeDtypeStruct((groups, BLK, BLK), F32),
            jax.ShapeDtypeStruct((groups, BLK, 1), F32),
            jax.ShapeDtypeStruct((1, w_sgu), F32),
        ],
        scratch_shapes=[pltpu.VMEM((8, w_sgu), F32)] + [pltpu.VMEM((BLK, w_sgu), F32)] * 3,
        input_output_aliases={7: 0},
    )(z, z, da, gain, ws_b, wst_b, b_col, dz, *after)


def _attn_softmax(sink_ref, q_ref, k_ref, v_ref, bias_ref, s_len, grp):
    kv = pl.program_id(0)
    n = pl.program_id(1)
    start = pl.multiple_of(n * BLK, BLK)
    kb = k_ref[pl.ds(start, 3 * BLK), :]
    vb = v_ref[pl.ds(start, 3 * BLK), :]
    qv = q_ref[...]
    qs = jnp.concatenate([qv[:, g * HEAD_DIM : (g + 1) * HEAD_DIM] for g in range(grp)], axis=0).astype(BF16)
    sc = lax.dot_general(qs, kb, NT, preferred_element_type=F32) * (HEAD_DIM**-0.5)
    sc = sc + bias_ref[...].reshape(grp * BLK, 3 * BLK)
    kpos = start + lax.broadcasted_iota(I32, (1, 3 * BLK), 1) - BLK
    sc = jnp.where((kpos >= 0) & (kpos < s_len), sc, NEG)
    sink = jnp.concatenate([jnp.full((BLK, 1), sink_ref[kv * grp + g], F32) for g in range(grp)], axis=0)
    m = jnp.maximum(jnp.max(sc, axis=-1, keepdims=True), sink)
    p = jnp.exp(sc - m)
    esink = jnp.exp(sink - m)
    den = jnp.sum(p, axis=-1, keepdims=True) + esink
    return start, qs, kb, vb, p / den, esink / den


def _attn_specs(s, grp, q_blk0):
    qw = grp * HEAD_DIM
    return [
        pl.BlockSpec(memory_space=pltpu.SMEM),
        pl.BlockSpec((BLK, qw), lambda kv, n: (n, q_blk0 + kv)),
        pl.BlockSpec((s + 2 * BLK, HEAD_DIM), lambda kv, n: (0, kv)),
        pl.BlockSpec((s + 2 * BLK, HEAD_DIM), lambda kv, n: (0, kv)),
        pl.BlockSpec((grp, BLK, 3 * BLK), lambda kv, n: (kv, 0, 0)),
    ]


def _attn_fwd(sink, z, k_pad, v_pad, bias_tab, grp, q_blk0):
    s = z.shape[0]
    qw = grp * HEAD_DIM

    def body(sink_ref, q_ref, k_ref, v_ref, bias_ref, o_ref):
        _, _, _, vb, pn, _ = _attn_softmax(sink_ref, q_ref, k_ref, v_ref, bias_ref, s, grp)
        o = jnp.dot(pn.astype(BF16), vb, preferred_element_type=F32)
        for g in range(grp):
            o_ref[:, g * HEAD_DIM : (g + 1) * HEAD_DIM] = o[g * BLK : (g + 1) * BLK].astype(BF16)

    return _pallas(
        body,
        name="attn_fwd",
        grid=(N_KV_HEADS, s // BLK),
        in_specs=_attn_specs(s, grp, q_blk0),
        out_specs=pl.BlockSpec((BLK, qw), lambda kv, n: (n, kv)),
        out_shape=jax.ShapeDtypeStruct((s, N_KV_HEADS * qw), BF16),
    )(sink, z, k_pad, v_pad, bias_tab)


def _attn_bwd(sink, z, k_pad, v_pad, bias_tab, dout, dz, grp, q_blk0):
    s = z.shape[0]
    qw = grp * HEAD_DIM
    nb = s // BLK
    heads = N_KV_HEADS * grp

    def body(sink_ref, q_ref, k_ref, v_ref, bias_ref, do_ref, dz_in, dq_ref, dk_ref, dv_ref, dbias_ref, dsink_ref, dk_acc, dv_acc):
        del dz_in
        kv = pl.program_id(0)
        n = pl.program_id(1)
        start, qs, kb, vb, pn, psink = _attn_softmax(sink_ref, q_ref, k_ref, v_ref, bias_ref, s, grp)
        dov = do_ref[...]
        dos = jnp.concatenate([dov[:, g * HEAD_DIM : (g + 1) * HEAD_DIM] for g in range(grp)], axis=0)
        dp = lax.dot_general(dos, vb, NT, preferred_element_type=F32)
        dvb = lax.dot_general(pn.astype(BF16), dos, TN, preferred_element_type=F32)
        delta = jnp.sum(pn * dp, axis=-1, keepdims=True)
        ds = pn * (dp - delta)
        dsb = (ds * (HEAD_DIM**-0.5)).astype(BF16)
        dq = jnp.dot(dsb, kb, preferred_element_type=F32)
        dkb = lax.dot_general(dsb, qs, TN, preferred_element_type=F32)
        for g in range(grp):
            dq_ref[:, g * HEAD_DIM : (g + 1) * HEAD_DIM] = dq[g * BLK : (g + 1) * BLK].astype(BF16)

        @pl.when(n == 0)
        def _():
            dk_acc[...] = jnp.zeros_like(dk_acc)
            dv_acc[...] = jnp.zeros_like(dv_acc)
            dbias_ref[...] = jnp.zeros_like(dbias_ref)

        @pl.when((n == 0) & (kv == 0))
        def _():
            dsink_ref[...] = jnp.zeros_like(dsink_ref)

        dk_acc[pl.ds(start, 3 * BLK), :] += dkb
        dv_acc[pl.ds(start, 3 * BLK), :] += dvb
        dbias_ref[...] += ds.reshape(grp, BLK, 3 * BLK)
        row = lax.broadcasted_iota(I32, (heads, LANES), 0)
        sd = psink * delta
        upd = jnp.zeros((heads, LANES), F32)
        for g in range(grp):
            upd = jnp.where(row == kv * grp + g, -jnp.sum(sd[g * BLK : (g + 1) * BLK]), upd)
        dsink_ref[...] += upd

        @pl.when(n == nb - 1)
        def _():
            dk_ref[...] = dk_acc[...]
            dv_ref[...] = dv_acc[...]

    pad_spec = pl.BlockSpec((s + 2 * BLK, HEAD_DIM), lambda kv, n: (0, kv))
    kvw = N_KV_HEADS * HEAD_DIM
    return _pallas(
        body,
        name="attn_bwd",
        grid=(N_KV_HEADS, nb),
        in_specs=_attn_specs(s, grp, q_blk0) + [pl.BlockSpec((BLK, qw), lambda kv, n: (n, kv)), ANY],
        out_specs=[
            pl.BlockSpec((BLK, qw), lambda kv, n: (n, q_blk0 + kv)),
            pad_spec,
            pad_spec,
            pl.BlockSpec((grp, BLK, 3 * BLK), lambda kv, n: (kv, 0, 0)),
            pl.BlockSpec((heads, LANES), lambda kv, n: (0, 0)),
        ],
        out_shape=[
            jax.ShapeDtypeStruct(dz.shape, BF16),
            jax.ShapeDtypeStruct((s + 2 * BLK, kvw), F32),
            jax.ShapeDtypeStruct((s + 2 * BLK, kvw), F32),
            jax.ShapeDtypeStruct((heads, BLK, 3 * BLK), F32),
            jax.ShapeDtypeStruct((heads, LANES), F32),
        ],
        scratch_shapes=[pltpu.VMEM((s + 2 * BLK, HEAD_DIM), F32), pltpu.VMEM((s + 2 * BLK, HEAD_DIM), F32)],
        input_output_aliases={6: 0},
    )(sink, z, k_pad, v_pad, bias_tab, dout, dz)


def _dkv_to_dz(dk_pad, dv_pad, dz, blk_idx):
    s = dz.shape[0]
    kvw = dk_pad.shape[1]

    def body(dk_ref, dv_ref, dz_in, out_ref):
        del dz_in
        out_ref[:, :kvw] = dk_ref[...].astype(BF16)
        out_ref[:, kvw:] = dv_ref[...].astype(BF16)

    src = pl.BlockSpec((BLK, kvw), lambda i: (i + 1, 0))
    return _pallas(
        body,
        name="dkv_to_dz",
        grid=(s // BLK,),
        in_specs=[src, src, ANY],
        out_specs=pl.BlockSpec((BLK, 2 * kvw), lambda i: (i, blk_idx)),
        out_shape=jax.ShapeDtypeStruct(dz.shape, BF16),
        input_output_aliases={2: 0},
    )(dk_pad, dv_pad, dz)


def _relbias_bwd(dbias_tab, bucket):
    heads = dbias_tab.shape[0]

    def body(dt_ref, bk_ref, out_ref):
        lane = lax.broadcasted_iota(I32, (1, LANES), 1)
        bk = bk_ref[...]
        rows = []
        for h in range(heads):
            dt = dt_ref[h]
            acc = jnp.zeros((1, LANES), F32)
            for b in range(REL_BUCKETS):
                acc = jnp.where(lane == b, jnp.sum(jnp.where(bk == b, dt, 0.0)), acc)
            rows.append(acc)
        out_ref[...] = jnp.concatenate(rows, axis=0)

    return _pallas(body, name="relbias_bwd", out_shape=jax.ShapeDtypeStruct((heads, LANES), F32))(dbias_tab, bucket)


def _t5_bucket(rel):
    nb = REL_BUCKETS // 2
    ret = jnp.where(rel > 0, nb, 0)
    n = jnp.abs(rel)
    max_exact = nb // 2
    nf = jnp.maximum(n, 1).astype(F32)
    large = max_exact + (jnp.log(nf / max_exact) / math.log(REL_MAX_DIST / max_exact) * (nb - max_exact)).astype(I32)
    large = jnp.minimum(large, nb - 1)
    return ret + jnp.where(n < max_exact, n, large)


def _band_tables(rel_bias):
    qi = jnp.arange(BLK)[:, None]
    kj = jnp.arange(3 * BLK)[None, :]
    rel = kj - BLK - qi
    bucket = _t5_bucket(rel).astype(I32)
    heads = rel_bias.shape[1]
    masked = jnp.where(jnp.abs(rel) <= BLK, bucket, -1)

    def body(rb_ref, bk_ref, out_ref):
        bk = bk_ref[...]
        for h in range(heads):
            tab = jnp.full(bk.shape, NEG, F32)
            for b in range(REL_BUCKETS):
                tab = jnp.where(bk == b, rb_ref[b, h], tab)
            out_ref[h] = tab

    bias_tab = _pallas(
        body,
        name="bias_table",
        in_specs=[pl.BlockSpec(memory_space=pltpu.SMEM), pl.BlockSpec(memory_space=pltpu.VMEM)],
        out_specs=pl.BlockSpec(memory_space=pltpu.VMEM),
        out_shape=jax.ShapeDtypeStruct((heads, BLK, 3 * BLK), F32),
    )(rel_bias.astype(F32), masked)
    return bias_tab, bucket


EW_BLOCK_ELEMS = 512 * 1024


def _ew_tiles(shape, elems=EW_BLOCK_ELEMS // 2):
    r, c = shape
    tn = c if c <= 2048 else _tile(c, (2048, 1920, 1536, 1408, 1024, 512))
    tm = max([t for t in range(16, r + 1, 16) if r % t == 0 and t * tn <= elems] or [8])
    return tm, tn


def _cast_into_full(name, qidx, w, kind, after=()):
    r, c = w.shape
    tm, tn = _ew_tiles(w.shape, EW_BLOCK_ELEMS)
    nbi, nbj = r // tm, c // tn
    if kind == "col":
        full, out_spec = (r, c * N_CHIPS), pl.BlockSpec((tm, tn), lambda i, j, q: (i, q[0] * nbj + j))
    else:
        full, out_spec = (r * N_CHIPS, c), pl.BlockSpec((tm, tn), lambda i, j, q: (q[0] * nbi + i, j))

    def body(q_ref, w_ref, *rest):
        del q_ref
        rest[-1][...] = w_ref[...].astype(BF16)

    return _pallas(
        body,
        name=name,
        grid_spec=pltpu.PrefetchScalarGridSpec(
            num_scalar_prefetch=1,
            grid=(nbi, nbj),
            in_specs=[pl.BlockSpec((tm, tn), lambda i, j, q: (i, j))] + [ANY] * len(after),
            out_specs=out_spec,
        ),
        out_shape=jax.ShapeDtypeStruct(full, BF16),
    )(qidx, w, *after)


def _adamw(name, w, g, m, v, after=()):
    tm, tn = _ew_tiles(w.shape, EW_BLOCK_ELEMS)
    if _nbytes(w.shape, F32) <= 1024 * 1024:
        tm, tn = w.shape
    spec = pl.BlockSpec((tm, tn), lambda i, j: (i, j))
    n_after = len(after)

    def body(w_ref, g_ref, m_ref, v_ref, *rest):
        d_ref, nm_ref, nv_ref, g_out_ref = rest[n_after:]
        gv = g_ref[...]
        g_out_ref[...] = gv
        nm = ADAM_B1 * m_ref[...] + (1.0 - ADAM_B1) * gv
        nv = ADAM_B2 * v_ref[...] + (1.0 - ADAM_B2) * (gv * gv)
        m_hat = nm / (1.0 - ADAM_B1**ADAM_STEP)
        v_hat = nv / (1.0 - ADAM_B2**ADAM_STEP)
        d_ref[...] = -ADAM_LR * (m_hat / (jnp.sqrt(v_hat) + ADAM_EPS) + ADAM_WD * w_ref[...])
        nm_ref[...] = nm
        nv_ref[...] = nv

    out = jax.ShapeDtypeStruct(w.shape, F32)
    return _pallas(
        body, name=name, grid=(w.shape[0] // tm, w.shape[1] // tn), in_specs=[spec] * 4 + [ANY] * n_after,
        out_specs=[spec] * 4, out_shape=[out, out, out, out],
        compiler_params=_params(_mm_vmem([((tm, tn), F32, 24)])),
    )(w, g, m, v, *after)


def _pair_add(name, cidx, g_full, r_sib, kind):
    hr, hc = r_sib.shape
    tm, tn = _ew_tiles((hr, hc), 2 * EW_BLOCK_ELEMS)
    nbi, nbj = hr // tm, hc // tn
    if kind == "col":
        g_spec = pl.BlockSpec((tm, tn), lambda i, j, c: (c[0] * nbi + i, j))
    else:
        g_spec = pl.BlockSpec((tm, tn), lambda i, j, c: (i, c[0] * nbj + j))
    spec = pl.BlockSpec((tm, tn), lambda i, j, c: (i, j))

    def body(c_ref, g_ref, r_ref, o_ref):
        del c_ref
        o_ref[...] = (g_ref[...].astype(F32) + r_ref[...].astype(F32)).astype(BF16)

    return _pallas(
        body,
        name=name,
        grid_spec=pltpu.PrefetchScalarGridSpec(num_scalar_prefetch=1, grid=(nbi, nbj), in_specs=[g_spec, spec], out_specs=spec),
        out_shape=jax.ShapeDtypeStruct((hr, hc), BF16),
        compiler_params=_params(_mm_vmem([((tm, tn), BF16, 6), ((tm, tn), F32, 3)])),
    )(cidx, g_full, r_sib)


def _chip_sum(name, qidx, c_half, r_ici, kind):
    _, pr, pc = r_ici.shape
    tm, tn = _ew_tiles((pr, pc), 2 * EW_BLOCK_ELEMS)
    nbi, nbj = pr // tm, pc // tn
    if kind == "col":
        own_spec = pl.BlockSpec((tm, tn), lambda i, j, q: (i, q[0] * nbj + j))
        full, out_spec = (2 * pr, pc), pl.BlockSpec((tm, tn), lambda i, j, q: (q[1] * nbi + i, j))
    else:
        own_spec = pl.BlockSpec((tm, tn), lambda i, j, q: (q[0] * nbi + i, j))
        full, out_spec = (pr, 2 * pc), pl.BlockSpec((tm, tn), lambda i, j, q: (i, q[1] * nbj + j))

    def body(q_ref, own_ref, r_ref, o_ref):
        q = q_ref[0]
        own = own_ref[...].astype(F32)
        recv = [r_ref[r].astype(F32) for r in range(3)]
        total = None
        for chip in range(N_CHIPS):
            d = chip ^ q
            term = jnp.where(d == 0, own, jnp.where(d == 2, recv[0], jnp.where(d == 1, recv[1], recv[2])))
            total = term if total is None else total + term
        o_ref[...] = total

    return _pallas(
        body,
        name=name,
        grid_spec=pltpu.PrefetchScalarGridSpec(
            num_scalar_prefetch=1,
            grid=(nbi, nbj),
            in_specs=[own_spec, pl.BlockSpec((3, tm, tn), lambda i, j, q: (0, i, j))],
            out_specs=out_spec,
        ),
        out_shape=jax.ShapeDtypeStruct(full, F32),
        compiler_params=_params(_mm_vmem([((tm, tn), BF16, 8), ((tm, tn), F32, 6)])),
    )(qidx, c_half, r_ici)


_REL_MASK = (2, 1, 3)


def _place():
    x, y, c = lax.axis_index("x"), lax.axis_index("y"), lax.axis_index("c")
    chips = [(1 - x, y), (x, 1 - y), (1 - x, 1 - y)]
    return x, y, c, 2 * x + y, chips


def _shard_view(ref, kind, chip):
    if kind == "col":
        w = ref.shape[1] // N_CHIPS
        return ref.at[:, pl.ds(pl.multiple_of(chip * w, LANES), w)]
    h = ref.shape[0] // N_CHIPS
    return ref.at[pl.ds(pl.multiple_of(chip * h, 16), h), :]


def _row_half(ref, half):
    h = ref.shape[0] // 2
    return ref.at[pl.ds(pl.multiple_of(half * h, 16), h), :]


def _pair_half(ref, kind, half):
    if kind == "col":
        return _row_half(ref, half)
    w = ref.shape[1] // 2
    return ref.at[:, pl.ds(pl.multiple_of(half * w, LANES), w)]


def _remote(src, dst, send_sem, recv_sem, dev):
    return pltpu.make_async_remote_copy(src_ref=src, dst_ref=dst, send_sem=send_sem, recv_sem=recv_sem, device_id=dev, device_id_type=MESH)


def _hbm(a):
    return pltpu.with_memory_space_constraint(a, pltpu.HBM)


def _gather_start(name, fulls, kinds, rels=(0, 1, 2), after=()):
    n_w = len(fulls)

    def body(*refs):
        g = refs[:n_w]
        send_sem, recv_sem = refs[n_w + len(after)], refs[n_w + len(after) + 1]
        token = refs[-1]
        _, _, c, q, chips = _place()
        for w in range(n_w):
            mine = _row_half(_shard_view(g[w], kinds[w], q), c)
            for r in rels if isinstance(rels, tuple) else rels[w]:
                _remote(mine, mine, send_sem.at[3 * w + r], recv_sem.at[3 * w + r], (*chips[r], c)).start()
        token[...] = jnp.zeros_like(token)

    res = _pallas(
        body,
        name=name,
        out_shape=(
            pltpu.SemaphoreType.DMA((3 * n_w,)),
            pltpu.SemaphoreType.DMA((3 * n_w,)),
            *[pltpu.HBM(f.shape, f.dtype) for f in fulls],
            jax.ShapeDtypeStruct((8, LANES), F32),
        ),
        in_specs=[HBM_SPEC] * n_w + [ANY] * len(after),
        out_specs=(SEM_SPEC, SEM_SPEC, *[HBM_SPEC] * n_w, pl.BlockSpec(memory_space=pltpu.VMEM)),
        input_output_aliases={w: w + 2 for w in range(n_w)},
        compiler_params=pltpu.CompilerParams(has_side_effects=EFFECT),
    )(*[_hbm(f) for f in fulls], *after)
    return res[0], res[1], list(res[2 : 2 + n_w]), res[-1]


def _relay_copies(kinds, waiting):
    def copies(refs, send_sem, recv_sem):
        _, _, c, q, chips = _place()
        out = []
        for i, kind in enumerate(kinds):
            for k, (src_rel, dst_rel) in enumerate(((0, 1), (1, 0))):
                held = _row_half(_row_half(_shard_view(refs[i], kind, q ^ _REL_MASK[src_rel]), c), k)
                far = _row_half(_row_half(_shard_view(refs[i], kind, q ^ _REL_MASK[2]), c), k)
                dst = far if waiting else held
                out.append(_remote(held, dst, send_sem.at[2 * i + k], recv_sem.at[2 * i + k], (*chips[dst_rel], c)))
        return out

    return copies


def _forward_copies(kinds, waiting, rels=(0, 1, 2), sem0=0):
    def copies(refs, send_sem, recv_sem):
        x, y, c, q, _ = _place()
        out = []
        for i, kind in enumerate(kinds):
            for k, r in enumerate(rels):
                quarter = _shard_view(refs[i], kind, q ^ _REL_MASK[r])
                landed = _row_half(quarter, c)
                dst = _row_half(quarter, 1 - c) if waiting else landed
                sem = sem0 + len(rels) * i + k
                out.append(_remote(landed, dst, send_sem.at[sem], recv_sem.at[sem], (x, y, 1 - c)))
        return out

    return copies


def _relay_and_forward_copies(kinds, waiting):
    forward = _forward_copies(kinds, waiting, rels=(0, 1), sem0=2 * len(kinds))
    relay = _relay_copies(kinds, waiting)
    return lambda refs, send_sem, recv_sem: forward(refs, send_sem, recv_sem) + relay(refs, send_sem, recv_sem)


def _gather_wait(name, fulls, kinds, w_ids, send_sem, recv_sem, after, rels=(0, 1, 2)):
    n = len(fulls)

    def body(*refs):
        g = refs[:n]
        s_sem, r_sem = refs[n], refs[n + 1]
        x, y, c, q, _ = _place()
        for i, w in enumerate(w_ids):
            mine = _row_half(_shard_view(g[i], kinds[i], q), c)
            for r in rels:
                landed = _row_half(_shard_view(g[i], kinds[i], q ^ _REL_MASK[r]), c)
                cp = _remote(mine, landed, s_sem.at[3 * w + r], r_sem.at[3 * w + r], (x, y, 1 - c))
                cp.wait_send()
                cp.wait_recv()

    res = _pallas(
        body,
        name=name,
        out_shape=[pltpu.HBM(f.shape, f.dtype) for f in fulls],
        in_specs=[HBM_SPEC] * n + [SEM_SPEC, SEM_SPEC, ANY],
        out_specs=[HBM_SPEC] * n,
        input_output_aliases={i: i for i in range(n)},
        compiler_params=pltpu.CompilerParams(has_side_effects=EFFECT),
    )(*fulls, send_sem, recv_sem, after)
    return list(res)


def _gather_forward(name, fulls, kinds, rels=(0, 1, 2)):
    n = len(fulls)

    def body(*refs):
        g = refs[n : 2 * n]
        send, recv = refs[2 * n :]
        _sibling_handshake()
        x, y, c, q, _ = _place()
        sib = (x, y, 1 - c)
        cps = []
        for i in range(n):
            for r in rels:
                landed = _row_half(_shard_view(g[i], kinds[i], q ^ _REL_MASK[r]), c)
                cps.append(_remote(landed, landed, send.at[i, r], recv.at[i, r], sib))
        for cp in cps:
            cp.start()
        for i in range(n):
            for r in rels:
                other = _row_half(_shard_view(g[i], kinds[i], q ^ _REL_MASK[r]), 1 - c)
                _remote(other, other, send.at[i, r], recv.at[i, r], sib).wait_recv()
        for cp in cps:
            cp.wait_send()

    res = _pallas(
        body,
        name=name,
        in_specs=[ANY] * n,
        out_specs=[ANY] * n,
        out_shape=[jax.ShapeDtypeStruct(f.shape, f.dtype) for f in fulls],
        scratch_shapes=[pltpu.SemaphoreType.DMA((n, 3)), pltpu.SemaphoreType.DMA((n, 3))],
        input_output_aliases={i: i for i in range(n)},
        compiler_params=pltpu.CompilerParams(collective_id=SIBLING_BARRIER_ID),
    )(*fulls)
    return list(res)


SIBLING_BARRIER_ID = 1


def _sibling_handshake():
    sib = (lax.axis_index("x"), lax.axis_index("y"), 1 - lax.axis_index("c"))
    barrier = pltpu.get_barrier_semaphore()
    pl.semaphore_signal(barrier, inc=1, device_id=sib, device_id_type=MESH)
    pl.semaphore_wait(barrier, 1)


def _split_start(name, bufs, n_sems, copies, sibling_only=False):
    n = len(bufs)

    def body(*refs):
        if sibling_only:
            _sibling_handshake()
        for cp in copies(refs[:n], refs[n], refs[n + 1]):
            cp.start()

    extra = {"collective_id": SIBLING_BARRIER_ID} if sibling_only else {}

    res = _pallas(
        body,
        name=name,
        out_shape=(
            pltpu.SemaphoreType.DMA((n_sems,)),
            pltpu.SemaphoreType.DMA((n_sems,)),
            *[pltpu.HBM(b.shape, b.dtype) for b in bufs],
        ),
        in_specs=[HBM_SPEC] * n,
        out_specs=(SEM_SPEC, SEM_SPEC, *[HBM_SPEC] * n),
        input_output_aliases={i: i + 2 for i in range(n)},
        compiler_params=pltpu.CompilerParams(has_side_effects=EFFECT, **extra),
    )(*[_hbm(b) for b in bufs])
    return res[0], res[1], list(res[2:])


def _split_wait(name, bufs, send_sem, recv_sem, copies, after):
    n = len(bufs)

    def body(*refs):
        for cp in copies(refs[:n], refs[n], refs[n + 1]):
            cp.wait_send()
            cp.wait_recv()

    res = _pallas(
        body,
        name=name,
        out_shape=[pltpu.HBM(b.shape, b.dtype) for b in bufs],
        in_specs=[HBM_SPEC] * n + [SEM_SPEC, SEM_SPEC, ANY],
        out_specs=[HBM_SPEC] * n,
        input_output_aliases={i: i for i in range(n)},
        compiler_params=pltpu.CompilerParams(has_side_effects=EFFECT),
    )(*bufs, send_sem, recv_sem, after)
    return list(res)


def _pair_exchange_copies(kinds):
    n = len(kinds)

    def copies(refs, send_sem, recv_sem):
        x, y, c, _, _ = _place()
        return [
            _remote(_pair_half(refs[w], kinds[w], 1 - c), refs[n + w], send_sem.at[w], recv_sem.at[w], (x, y, 1 - c))
            for w in range(n)
        ]

    return copies


def _pair_share_copies(kinds, waiting):
    def copies(refs, send_sem, recv_sem):
        x, y, c, _, _ = _place()
        out = []
        for w, kind in enumerate(kinds):
            mine = _pair_half(refs[w], kind, c)
            dst = _pair_half(refs[w], kind, 1 - c) if waiting else mine
            out.append(_remote(mine, dst, send_sem.at[w], recv_sem.at[w], (x, y, 1 - c)))
        return out

    return copies


def _piece_shape(half_shape, kind):
    r, c = half_shape
    return (3, r, c // N_CHIPS) if kind == "col" else (3, r // N_CHIPS, c)


def _chip_send_start(name, halves, kinds):
    n = len(halves)
    lands = [lax.empty(_piece_shape(h.shape, k), BF16) for h, k in zip(halves, kinds)]

    def body(*refs):
        h, land = refs[:n], refs[n : 2 * n]
        send_sem, recv_sem = refs[2 * n], refs[2 * n + 1]
        _, _, c, q, chips = _place()
        for i in range(n):
            for r, chip in enumerate(chips):
                piece = _shard_view(h[i], kinds[i], q ^ _REL_MASK[r])
                _remote(piece, land[i].at[r], send_sem.at[3 * i + r], recv_sem.at[3 * i + r], (*chip, c)).start()

    res = _pallas(
        body,
        name=name,
        out_shape=(
            pltpu.SemaphoreType.DMA((3 * n,)),
            pltpu.SemaphoreType.DMA((3 * n,)),
            *[pltpu.HBM(a.shape, a.dtype) for a in halves],
            *[pltpu.HBM(a.shape, a.dtype) for a in lands],
        ),
        in_specs=[HBM_SPEC] * (2 * n),
        out_specs=(SEM_SPEC, SEM_SPEC, *[HBM_SPEC] * (2 * n)),
        input_output_aliases={i: i + 2 for i in range(2 * n)},
        compiler_params=pltpu.CompilerParams(has_side_effects=EFFECT),
    )(*[_hbm(a) for a in halves], *[_hbm(a) for a in lands])
    return res[0], res[1], list(res[2 : 2 + n]), list(res[2 + n :])


def _chip_send_wait(name, halves, lands, kinds, send_sem, recv_sem, after):
    n = len(halves)

    def body(*refs):
        h, land = refs[:n], refs[n : 2 * n]
        s_sem, r_sem = refs[2 * n], refs[2 * n + 1]
        x, y, c, q, _ = _place()
        for i in range(n):
            for r in range(3):
                piece = _shard_view(h[i], kinds[i], q ^ _REL_MASK[r])
                cp = _remote(piece, land[i].at[r], s_sem.at[3 * i + r], r_sem.at[3 * i + r], (x, y, 1 - c))
                cp.wait_send()
                cp.wait_recv()

    res = _pallas(
        body,
        name=name,
        out_shape=[pltpu.HBM(a.shape, a.dtype) for a in halves] + [pltpu.HBM(a.shape, a.dtype) for a in lands],
        in_specs=[HBM_SPEC] * (2 * n) + [SEM_SPEC, SEM_SPEC, ANY],
        out_specs=[HBM_SPEC] * (2 * n),
        input_output_aliases={i: i for i in range(2 * n)},
        compiler_params=pltpu.CompilerParams(has_side_effects=EFFECT),
    )(*halves, *lands, send_sem, recv_sem, after)
    return list(res[:n]), list(res[n:])


def _small_exchange_copies(waiting):
    def copies(refs, send_sem, recv_sem):
        p, land = refs
        x, y, c, q, _ = _place()
        me = 2 * q + c
        out = []
        for dd in range(1, 2 * N_CHIPS):
            dev = (x ^ ((dd >> 2) & 1), y ^ ((dd >> 1) & 1), c ^ (dd & 1))
            dst = land.at[me ^ dd] if waiting else land.at[me]
            out.append(_remote(p, dst, send_sem.at[dd - 1], recv_sem.at[dd - 1], dev))
        return out

    return copies


def _small_sum(name, me_idx, p, land):
    rows = p.shape[0]
    n_dev = 2 * N_CHIPS

    def body(me_ref, p_ref, land_ref, o_ref):
        me = me_ref[0]
        total = None
        for dev in range(n_dev):
            term = jnp.where(me == dev, p_ref[...], land_ref[dev])
            total = term if total is None else total + term
        o_ref[...] = total

    return _pallas(
        body,
        name=name,
        grid_spec=pltpu.PrefetchScalarGridSpec(
            num_scalar_prefetch=1,
            grid=(1,),
            in_specs=[pl.BlockSpec((rows, LANES), lambda i, m: (0, 0)), pl.BlockSpec((n_dev, rows, LANES), lambda i, m: (0, 0, 0))],
            out_specs=pl.BlockSpec((rows, LANES), lambda i, m: (0, 0)),
        ),
        out_shape=jax.ShapeDtypeStruct(p.shape, F32),
    )(me_idx, p, land)


def _pack(parts):
    rows = []
    for a in parts:
        flat = a.reshape(-1).astype(F32)
        n = flat.shape[0]
        padded = -(-n // (8 * LANES)) * (8 * LANES)
        rows.append(jnp.pad(flat, (0, padded - n)).reshape(-1, LANES))
    return jnp.concatenate(rows, axis=0)


def _unpack(packed, shapes):
    out, row = [], 0
    for shp in shapes:
        n = int(np.prod(shp))
        nrows = -(-n // (8 * LANES)) * 8
        out.append(packed[row : row + nrows].reshape(-1)[:n].reshape(shp))
        row += nrows
    return out


def kernel(x, w_in, norm_mix, sgu_v_gain, sgu_w_s, sgu_b_s, w_a_out, attn_sink, rel_bias, w_b_out, w_o, norm_ffn, w_gate, w_up, w_down, norm_final, loss_target, m_w_in, m_norm_mix, m_sgu_v_gain, m_sgu_w_s, m_sgu_b_s, m_w_a_out, m_attn_sink, m_rel_bias, m_w_b_out, m_w_o, m_norm_ffn, m_w_gate, m_w_up, m_w_down, m_norm_final, v_w_in, v_norm_mix, v_sgu_v_gain, v_sgu_w_s, v_sgu_b_s, v_w_a_out, v_attn_sink, v_rel_bias, v_w_b_out, v_w_o, v_norm_ffn, v_w_gate, v_w_up, v_w_down, v_norm_final):
    s, d = x.shape[1], x.shape[2]
    w_sgu = sgu_v_gain.shape[1]
    groups = sgu_w_s.shape[1]
    heads = attn_sink.shape[1]
    grp = heads // N_KV_HEADS
    w_att = heads * HEAD_DIM
    w_kv = N_KV_HEADS * HEAD_DIM
    d_ff = w_gate.shape[2] * N_CHIPS
    n_in = w_in.shape[2] * N_CHIPS
    off_q = 2 * w_sgu
    off_k = off_q + w_att
    off_g = off_k + 2 * w_kv
    assert n_in == off_g + 2 * d and groups * BLK == w_sgu and s % BLK == 0

    x2d = x.reshape(s, d)
    tgt = loss_target.reshape(s, d)
    c_idx = lax.axis_index("c").astype(I32).reshape(1)
    q_idx = (2 * lax.axis_index("x") + lax.axis_index("y")).astype(I32).reshape(1)
    qc_idx = jnp.concatenate([q_idx, c_idx])

    W_IN, W_A, W_B, W_O, W_GATE, W_UP, W_DOWN = range(7)
    names = ["w_in", "w_a", "w_b", "w_o", "w_gate", "w_up", "w_down"]
    kinds = ["col", "col", "col", "row", "col", "col", "row"]
    big_w = [w_in[0], w_a_out[0], w_b_out[0], w_o[0], w_gate[0], w_up[0], w_down[0]]
    big_m = [m_w_in[0], m_w_a_out[0], m_w_b_out[0], m_w_o[0], m_w_gate[0], m_w_up[0], m_w_down[0]]
    big_v = [v_w_in[0], v_w_a_out[0], v_w_b_out[0], v_w_o[0], v_w_gate[0], v_w_up[0], v_w_down[0]]
    full_in = _cast_into_full("cast_w_in", q_idx, big_w[W_IN], kinds[W_IN])
    in_send, in_recv, (full_in,), token = _gather_start("gather_start_in", [full_in], [kinds[W_IN]], rels=(0, 1))
    rest = [_cast_into_full("cast_" + names[i], q_idx, big_w[i], kinds[i], after=(token,)) for i in range(1, 7)]

    ws_b = sgu_w_s[0].astype(BF16)
    wst_b = jnp.swapaxes(sgu_w_s[0], 1, 2).astype(BF16)
    b_col = sgu_b_s[0].reshape(groups, BLK, 1)
    bias_tab, bucket = _band_tables(rel_bias)
    sink = attn_sink[0]
    small_w = [norm_mix, sgu_v_gain, sgu_w_s, sgu_b_s, attn_sink, rel_bias, norm_ffn, norm_final]
    small_m = [m_norm_mix, m_sgu_v_gain, m_sgu_w_s, m_sgu_b_s, m_attn_sink, m_rel_bias, m_norm_ffn, m_norm_final]
    small_v = [v_norm_mix, v_sgu_v_gain, v_sgu_w_s, v_sgu_b_s, v_attn_sink, v_rel_bias, v_norm_ffn, v_norm_final]
    small_shapes = [w.shape for w in small_w]
    zero1 = jnp.zeros((1,), F32)
    pw, pm, pv = _pack(small_w + [zero1]), _pack(small_m + [zero1]), _pack(small_v + [zero1])
    h1 = _rms_fwd("rms_mix", x2d, norm_mix, after=(token, rest[-1], ws_b, wst_b, b_col, bias_tab, pw, pm, pv))
    (full_in,) = _gather_wait("gather_wait_in", [full_in], [kinds[W_IN]], [0], in_send, in_recv, h1, rels=(0, 1))
    relay_send, relay_recv, (full_in,) = _split_start(
        "gather_relay_in", [full_in], 4, _relay_and_forward_copies([kinds[W_IN]], False)
    )
    full_down = rest.pop()
    full_up = rest.pop()
    ag_send, ag_recv, rest, token = _gather_start("gather_start_rest", rest, kinds[1:5], rels=(0, 1), after=(full_in, full_up))
    (full_in,) = _split_wait(
        "gather_relay_wait_in", [full_in], relay_send, relay_recv, _relay_and_forward_copies([kinds[W_IN]], True), token
    )
    (g_in,) = _gather_forward("gather_fwd_in", [full_in], [kinds[W_IN]], rels=(2,))
    fulls = [g_in] + rest

    def relay_begin(tag, ids, after, source=None):
        ks = [kinds[i] for i in ids]
        send, recv, bufs, w_ids = source or (ag_send, ag_recv, [fulls[i] for i in ids], [i - 1 for i in ids])
        bufs = _gather_wait("gather_wait_" + tag, bufs, ks, w_ids, send, recv, after, rels=(0, 1))
        send, recv, bufs = _split_start("gather_relay_" + tag, bufs, 4 * len(ids), _relay_and_forward_copies(ks, False))
        return tag, ks, send, recv, bufs

    def relay_end(state, after):
        tag, ks, send, recv, bufs = state
        bufs = _split_wait("gather_relay_wait_" + tag, bufs, send, recv, _relay_and_forward_copies(ks, True), after)
        return _gather_forward("gather_fwd_" + tag, bufs, ks, rels=(2,))

    def relay_end_async(state, after):
        tag, ks, send, recv, bufs = state
        bufs = _split_wait("gather_relay_wait_" + tag, bufs, send, recv, _relay_and_forward_copies(ks, True), after)
        send, recv, bufs = _split_start(
            "gather_fwd_start_" + tag, bufs, len(ks), _forward_copies(ks, False, rels=(2,)), sibling_only=True
        )
        return tag, ks, send, recv, bufs

    def forwarded(state, after):
        tag, ks, send, recv, bufs = state
        return _split_wait("gather_fwd_wait_" + tag, bufs, send, recv, _forward_copies(ks, True, rels=(2,)), after)

    tm = _tile(s, (1024, 512, 256, 128))

    tn = _tile(n_in, (768, 640, 512))
    z = _mm(
        "mm_z", (s // tm, n_in // tn, 1), [h1, g_in],
        [pl.BlockSpec((tm, d), lambda i, j, k: (i, 0)), pl.BlockSpec((d, tn), lambda i, j, k: (0, j))],
        [jax.ShapeDtypeStruct((s, n_in), BF16)], [pl.BlockSpec((tm, tn), lambda i, j, k: (i, j))],
        [(0, 1, NN, 0)], 1, (tm, tn), 1, lambda ins, vals, outs, cs: _put(outs[0], cs, vals[0]),
        _mm_vmem([((tm, d), BF16, 2), ((d, tn), BF16, 2), ((tm, tn), F32, 3)]),
    )[0]
    mix_relay = relay_begin("mix", [W_A, W_B, W_O], z)
    up_send, up_recv, (full_up,), token = _gather_start(
        "gather_start_up", [full_up], [kinds[W_UP]], rels=(0, 1), after=(mix_relay[4][0],)
    )

    a_act = _sgu_fwd(z, sgu_v_gain, ws_b, b_col, w_sgu, after=(token,))

    kv_b = z[:, off_k:off_g]
    k_pad = jnp.pad(kv_b[:, :w_kv], ((BLK, BLK), (0, 0)))
    v_pad = jnp.pad(kv_b[:, w_kv:], ((BLK, BLK), (0, 0)))
    q_blk0 = off_q // (grp * HEAD_DIM)
    att = _attn_fwd(sink, z, k_pad, v_pad, bias_tab, grp, q_blk0)

    g_a, g_b, g_o = relay_end(mix_relay, att)
    gate_relay = relay_begin("gate", [W_GATE], g_a)

    tg = _tile(d, (512,))
    ga0, gb0 = off_g // tg, (off_g + d) // tg

    def ep_gate(ins, vals, outs, cs):
        sa, sb = _sigmoid(ins[4][:, cs].astype(F32)), _sigmoid(ins[5][:, cs].astype(F32))
        _put(outs[0], cs, sa * vals[0] + sb * vals[1])
        _put(outs[1], cs, vals[0])
        _put(outs[2], cs, vals[1])

    t_out = pl.BlockSpec((tm, tg), lambda i, j, k: (i, j))
    m_act, y_a, y_b = _mm(
        "mm_branches", (s // tm, d // tg, 1), [a_act, g_a, att, g_b, z, z],
        [pl.BlockSpec((tm, w_sgu), lambda i, j, k: (i, 0)), pl.BlockSpec((w_sgu, tg), lambda i, j, k: (0, j)),
         pl.BlockSpec((tm, w_att), lambda i, j, k: (i, 0)), pl.BlockSpec((w_att, tg), lambda i, j, k: (0, j)),
         pl.BlockSpec((tm, tg), lambda i, j, k: (i, ga0 + j)), pl.BlockSpec((tm, tg), lambda i, j, k: (i, gb0 + j))],
        [jax.ShapeDtypeStruct((s, d), BF16)] * 3,
        [t_out, t_out, t_out], [(0, 1, NN, 0), (2, 3, NN, 1)], 2, (tm, tg), 1, ep_gate,
        _mm_vmem([((tm, w_sgu), BF16, 4), ((w_sgu, tg), BF16, 4), ((tm, tg), F32, 12)]),
        after=(gate_relay[4][0],),
    )

    tn = _tile(d, (1024, 512))

    def ep_residual(ins, vals, outs, cs):
        _put(outs[0], cs, ins[2][:, cs] + vals[0])

    up_relay = relay_begin("up", [W_UP], m_act, source=(up_send, up_recv, [full_up], [0]))
    down_send, down_recv, (full_down,), token = _gather_start(
        "gather_start_down", [full_down], [kinds[W_DOWN]], after=(up_relay[4][0],)
    )
    x2 = _mm(
        "mm_wo", (s // tm, d // tn, 1), [m_act, g_o, x2d],
        [pl.BlockSpec((tm, d), lambda i, j, k: (i, 0)), pl.BlockSpec((d, tn), lambda i, j, k: (0, j)),
         pl.BlockSpec((tm, tn), lambda i, j, k: (i, j))],
        [jax.ShapeDtypeStruct((s, d), F32)], [pl.BlockSpec((tm, tn), lambda i, j, k: (i, j))],
        [(0, 1, NN, 0)], 1, (tm, tn), 1, ep_residual,
        _mm_vmem([((tm, d), BF16, 2), ((d, tn), BF16, 2), ((tm, tn), F32, 5)]),
        after=(token,), lookahead=(0,),
    )[0]
    gate_fwd = relay_end_async(gate_relay, x2)
    up_fwd = relay_end_async(up_relay, gate_fwd[4][0])
    h2 = _rms_fwd("rms_ffn", x2, norm_ffn, after=(up_fwd[4][0],))
    (g_gate,) = forwarded(gate_fwd, h2)
    (g_up,) = forwarded(up_fwd, g_gate)

    tf = _tile(d_ff, (512,))

    def ep_swiglu(ins, vals, outs, cs):
        gt, up = vals
        _put(outs[0], cs, gt)
        _put(outs[1], cs, up)
        _put(outs[2], cs, (gt * _sigmoid(gt)) * up)

    f_out = pl.BlockSpec((tm, tf), lambda i, j, k: (i, j))
    gt, up, f_act = _mm(
        "mm_gate_up", (s // tm, d_ff // tf, 1), [h2, g_gate, g_up],
        [pl.BlockSpec((tm, d), lambda i, j, k: (i, 0)), pl.BlockSpec((d, tf), lambda i, j, k: (0, j)),
         pl.BlockSpec((d, tf), lambda i, j, k: (0, j))],
        [jax.ShapeDtypeStruct((s, d_ff), BF16)] * 3,
        [f_out, f_out, f_out], [(0, 1, NN, 0), (0, 2, NN, 1)], 2, (tm, tf), 1, ep_swiglu,
        _mm_vmem([((tm, d), BF16, 2), ((d, tf), BF16, 4), ((tm, tf), F32, 8)]),    )
    (g_down,) = _gather_forward(
        "gather_fwd_ffn_out",
        _gather_wait("gather_wait_ffn_out", [full_down], [kinds[W_DOWN]], [0], down_send, down_recv, f_act),
        [kinds[W_DOWN]],
    )

    tkf = _tile(d_ff, (1408, 1024, 512))
    tml, tnl = _tile(s, (512, 256, 128)), _tile(d, (512,))
    x3 = _mm(
        "mm_down", (s // tml, d // tnl, 1), [f_act, g_down, x2],
        [pl.BlockSpec((tml, d_ff), lambda i, j, k: (i, 0)), pl.BlockSpec((d_ff, tnl), lambda i, j, k: (0, j)),
         pl.BlockSpec((tml, tnl), lambda i, j, k: (i, j))],
        [jax.ShapeDtypeStruct((s, d), F32)], [pl.BlockSpec((tml, tnl), lambda i, j, k: (i, j))],
        [(0, 1, NN, 0)], 1, (tml, tnl), 1, ep_residual,
        _mm_vmem([((tml, d_ff), BF16, 2), ((d_ff, tnl), BF16, 2), ((tml, tnl), F32, 6)]),
        lookahead=(0,),
    )[0]

    dx3, dx3b, dg_final, loss_part = _head(x3, norm_final.reshape(1, d), tgt)

    def reduce_a(tag, ids, grads):
        ks = [kinds[i] for i in ids]
        lands = [lax.empty((g.shape[0] // 2, g.shape[1]) if k == "col" else (g.shape[0], g.shape[1] // 2), BF16)
                 for g, k in zip(grads, ks)]
        send, recv, bufs = _split_start("pair_send_" + tag, list(grads) + lands, len(ids), _pair_exchange_copies(ks), sibling_only=True)
        return {"tag": tag, "ids": ids, "ks": ks, "pair": (send, recv, bufs), "token": bufs[0]}

    def reduce_b(st, after):
        tag, ids, ks = st["tag"], st["ids"], st["ks"]
        send, recv, bufs = st["pair"]
        bufs = _split_wait("pair_wait_" + tag, bufs, send, recv, _pair_exchange_copies(ks), after)
        grads, from_sib = bufs[: len(ids)], bufs[len(ids) :]
        halves = [_pair_add("pair_add_" + names[i], c_idx, g, r, k) for i, g, r, k in zip(ids, grads, from_sib, ks)]
        st["chip"] = _chip_send_start("chip_send_" + tag, halves, ks)
        st["token"] = st["chip"][2][0]

    def reduce_c(st, after):
        tag, ids, ks = st["tag"], st["ids"], st["ks"]
        send, recv, halves, lands = st["chip"]
        halves, lands = _chip_send_wait("chip_wait_" + tag, halves, lands, ks, send, recv, after)
        pieces = [_chip_sum("chip_sum_" + names[i], qc_idx, h, r, k) for i, h, r, k in zip(ids, halves, lands, ks)]
        st["share"] = _split_start("share_send_" + tag, pieces, len(ids), _pair_share_copies(ks, False), sibling_only=True)
        st["token"] = st["share"][2][0]

    def reduce_d(st, after):
        send, recv, bufs = st["share"]
        return _split_wait("share_wait_" + st["tag"], bufs, send, recv, _pair_share_copies(st["ks"], True), after)

    grads_big, upd = [None] * 7, [None] * 7

    def update(i, g, after=()):
        upd[i] = _adamw("adamw_" + names[i], big_w[i], g, big_m[i], big_v[i], after=after)
        grads_big[i] = upd[i][3]
        return upd[i][0]

    def finish(st, after):
        shared = reduce_d(st, after)
        after = shared[0]
        for i, g in zip(st["ids"], shared):
            after = update(i, g, (after,))
        return after

    def ep_swiglu_bwd(ins, vals, outs, cs):
        df = vals[0]
        gtv, upv = ins[2][:, cs].astype(F32), ins[3][:, cs].astype(F32)
        sg = _sigmoid(gtv)
        _put(outs[0], cs, df * upv * (sg + gtv * sg * (1.0 - sg)))
        _put(outs[1], cs, df * (gtv * sg))

    dgt, dup = _mm(
        "mm_dswiglu", (s // tm, d_ff // tf, 1), [dx3b, g_down, gt, up],
        [pl.BlockSpec((tm, d), lambda i, j, k: (i, 0)), pl.BlockSpec((tf, d), lambda i, j, k: (j, 0)), f_out, f_out],
        [jax.ShapeDtypeStruct((s, d_ff), BF16), jax.ShapeDtypeStruct((s, d_ff), BF16)], [f_out, f_out],
        [(0, 1, NT, 0)], 1, (tm, tf), 1, ep_swiglu_bwd,
        _mm_vmem([((tm, d), BF16, 2), ((tf, d), BF16, 2), ((tm, tf), F32, 8)]),    )

    def ep_store(ins, vals, outs, cs):
        for o, v in zip(outs, vals):
            _put(o, cs, v)

    twn = _tile(d, (1024, 512))
    gw_down = _mm(
        "mm_gw_down", (d_ff // tkf, d // twn, 1), [f_act, dx3b],
        [pl.BlockSpec((s, tkf), lambda i, j, k: (0, i)), pl.BlockSpec((s, twn), lambda i, j, k: (0, j))],
        [jax.ShapeDtypeStruct((d_ff, d), BF16)], [pl.BlockSpec((tkf, twn), lambda i, j, k: (i, j))],
        [(0, 1, TN, 0)], 1, (tkf, twn), 1, ep_store,
        _mm_vmem([((s, tkf), BF16, 3), ((s, twn), BF16, 2), ((tkf, twn), F32, 3)]),
    )[0]
    red_down = reduce_a("down", [W_DOWN], [gw_down])

    tn2 = _tile(d, (256,))
    dh2_specs = [pl.BlockSpec((tm, d_ff), lambda i, j, k: (i, 0)), pl.BlockSpec((tn2, d_ff), lambda i, j, k: (j, 0))]
    dh2_tile = pl.BlockSpec((tm, tn2), lambda i, j, k: (i, j))
    dh2_vmem = _mm_vmem([((tm, d_ff), BF16, 2), ((tn2, d_ff), BF16, 2), ((tm, tn2), F32, 7)])
    dh2 = _mm(
        "mm_dh2_gate", (s // tm, d // tn2, 1), [dgt, g_gate], dh2_specs,
        [jax.ShapeDtypeStruct((s, d), F32)], [dh2_tile], [(0, 1, NT, 0)], 1, (tm, tn2), 1, ep_store, dh2_vmem,
        after=(red_down["token"],),
    )[0]
    dh2 = _mm(
        "mm_dh2_up", (s // tm, d // tn2, 1), [dup, g_up, dh2], dh2_specs + [dh2_tile],
        [jax.ShapeDtypeStruct((s, d), F32)], [dh2_tile], [(0, 1, NT, 0)], 1, (tm, tn2), 1, ep_residual, dh2_vmem,
        lookahead=(0,),
    )[0]
    reduce_b(red_down, dh2)

    twr = _tile(d, (1024, 512))
    w_tile = pl.BlockSpec((twr, tf), lambda i, j, k: (i, j))
    gw_gate, gw_up = _mm(
        "mm_gw_gate_up", (d // twr, d_ff // tf, 1), [h2, dgt, dup],
        [pl.BlockSpec((s, twr), lambda i, j, k: (0, i)), pl.BlockSpec((s, tf), lambda i, j, k: (0, j)),
         pl.BlockSpec((s, tf), lambda i, j, k: (0, j))],
        [jax.ShapeDtypeStruct((d, d_ff), BF16), jax.ShapeDtypeStruct((d, d_ff), BF16)], [w_tile, w_tile],
        [(0, 1, TN, 0), (0, 2, TN, 1)], 2, (twr, tf), 1, ep_store,
        _mm_vmem([((s, twr), BF16, 3), ((s, tf), BF16, 4), ((twr, tf), F32, 6)]),
        after=(red_down["token"],),
    )
    red_ffn = reduce_a("ffn_in", [W_GATE, W_UP], [gw_gate, gw_up])

    dx2, dx2b, dg_ffn = _rms_bwd("rms_ffn_bwd", x2, norm_ffn, dh2, dx3, after=(red_ffn["token"],))

    nj = d // tg

    def lo(j):
        return jnp.minimum(j, nj - 1)

    def gate_bwd_body(dx_ref, wo_ref, ga_ref, gb_ref, ya_ref, yb_ref, dya_ref, dyb_ref, dz_ref, keep):
        j = pl.program_id(1)

        @pl.when(j < nj)
        def _():
            dm = lax.dot_general(dx_ref[...], wo_ref[...], NT, preferred_element_type=F32)
            sa, sb = _sigmoid(ga_ref[...].astype(F32)), _sigmoid(gb_ref[...].astype(F32))
            dya_ref[...] = (dm * sa).astype(BF16)
            dyb_ref[...] = (dm * sb).astype(BF16)
            dz_ref[...] = (dm * ya_ref[...].astype(F32) * (sa * (1.0 - sa))).astype(BF16)
            keep[lo(j)] = (dm * yb_ref[...].astype(F32) * (sb * (1.0 - sb))).astype(BF16)

        @pl.when(j >= nj)
        def _():
            dz_ref[...] = keep[jnp.maximum(j - nj, 0)]

    t_lo = pl.BlockSpec((tm, tg), lambda i, j: (i, lo(j)))
    dya, dyb, dz = _pallas(
        gate_bwd_body,
        name="mm_dgate",
        grid=(s // tm, 2 * nj),
        in_specs=[
            pl.BlockSpec((tm, d), lambda i, j: (i, 0)),
            pl.BlockSpec((tg, d), lambda i, j: (lo(j), 0)),
            pl.BlockSpec((tm, tg), lambda i, j: (i, ga0 + lo(j))),
            pl.BlockSpec((tm, tg), lambda i, j: (i, gb0 + lo(j))),
            t_lo,
            t_lo,
        ],
        out_specs=[t_lo, t_lo, pl.BlockSpec((tm, tg), lambda i, j: (i, ga0 + j))],
        out_shape=[jax.ShapeDtypeStruct((s, d), BF16), jax.ShapeDtypeStruct((s, d), BF16), jax.ShapeDtypeStruct((s, n_in), BF16)],
        scratch_shapes=[pltpu.VMEM((nj, tm, tg), BF16)],
        compiler_params=_params(_mm_vmem([((tm, d), BF16, 2), ((tg, d), BF16, 2), ((tm, tg), F32, 14), ((nj, tm, tg), BF16, 1)])),
    )(dx2b, g_o, z, z, y_a, y_b)
    reduce_b(red_ffn, dya)
    reduce_c(red_down, red_ffn["token"])

    gw_o = _mm(
        "mm_gw_o", (d // twr, d // twn, 1), [m_act, dx2b],
        [pl.BlockSpec((s, twr), lambda i, j, k: (0, i)), pl.BlockSpec((s, twn), lambda i, j, k: (0, j))],
        [jax.ShapeDtypeStruct((d, d), BF16)], [pl.BlockSpec((twr, twn), lambda i, j, k: (i, j))],
        [(0, 1, TN, 0)], 1, (twr, twn), 1, ep_store,
        _mm_vmem([((s, twr), BF16, 3), ((s, twn), BF16, 2), ((twr, twn), F32, 3)]),
        after=(red_down["token"],),
    )[0]
    after_down = finish(red_down, gw_o)

    tb = _tile(w_sgu, (1024, 512))
    b_out = pl.BlockSpec((tm, tb), lambda i, j, k: (i, j))

    da, datt = _mm(
        "mm_dbranches", (s // tm, w_sgu // tb, 1), [dya, g_a, dyb, g_b],
        [pl.BlockSpec((tm, d), lambda i, j, k: (i, 0)), pl.BlockSpec((tb, d), lambda i, j, k: (j, 0)),
         pl.BlockSpec((tm, d), lambda i, j, k: (i, 0)), pl.BlockSpec((tb, d), lambda i, j, k: (j, 0))],
        [jax.ShapeDtypeStruct((s, w_sgu), BF16), jax.ShapeDtypeStruct((s, w_att), BF16)], [b_out, b_out],
        [(0, 1, NT, 0), (2, 3, NT, 1)], 2, (tm, tb), 1, ep_store,
        _mm_vmem([((tm, d), BF16, 4), ((tb, d), BF16, 4), ((tm, tb), F32, 6)]),
        after=(after_down,),
    )

    wb_tile = pl.BlockSpec((tb, twn), lambda i, j, k: (i, j))
    gw_a, gw_b = _mm(
        "mm_gw_branches", (w_sgu // tb, d // twn, 1), [a_act, dya, att, dyb],
        [pl.BlockSpec((s, tb), lambda i, j, k: (0, i)), pl.BlockSpec((s, twn), lambda i, j, k: (0, j)),
         pl.BlockSpec((s, tb), lambda i, j, k: (0, i)), pl.BlockSpec((s, twn), lambda i, j, k: (0, j))],
        [jax.ShapeDtypeStruct((w_sgu, d), BF16), jax.ShapeDtypeStruct((w_att, d), BF16)], [wb_tile, wb_tile],
        [(0, 1, TN, 0), (2, 3, TN, 1)], 2, (tb, twn), 1, ep_store,
        _mm_vmem([((s, tb), BF16, 5), ((s, twn), BF16, 4), ((tb, twn), F32, 6)]),
        after=(da,),
    )
    red_mix = reduce_a("mix", [W_O, W_A, W_B], [gw_o, gw_a, gw_b])

    dz, dws, dbs, dgain = _sgu_bwd(z, da, sgu_v_gain, ws_b, wst_b, b_col, w_sgu, dz, after=(red_mix["token"],))
    dz, dk_pad, dv_pad, dbias_tab, dsink = _attn_bwd(sink, z, k_pad, v_pad, bias_tab, datt, dz, grp, q_blk0)
    dz = _dkv_to_dz(dk_pad, dv_pad, dz, off_k // (2 * w_kv))
    drel = _relbias_bwd(dbias_tab, bucket)
    reduce_b(red_mix, dz)
    reduce_c(red_ffn, red_mix["token"])

    early = [dgain, dws, dbs, dsink[:, 0], drel[:, :REL_BUCKETS].T, dg_ffn, dg_final]
    p_early = _pack([g.reshape(shp) for g, shp in zip(early, small_shapes[1:])] + [loss_part[0, :1]])
    land = jnp.zeros((2 * N_CHIPS,) + p_early.shape, F32)
    sm_send, sm_recv, (p_early, land) = _split_start("small_send", [p_early, land], 2 * N_CHIPS - 1, _small_exchange_copies(False))

    tzn = _tile(n_in, (768, 640, 512))
    gw_in = _mm(
        "mm_gw_in", (d // twr, n_in // tzn, 1), [h1, dz],
        [pl.BlockSpec((s, twr), lambda i, j, k: (0, i)), pl.BlockSpec((s, tzn), lambda i, j, k: (0, j))],
        [jax.ShapeDtypeStruct((d, n_in), BF16)], [pl.BlockSpec((twr, tzn), lambda i, j, k: (i, j))],
        [(0, 1, TN, 0)], 1, (twr, tzn), 1, ep_store,
        _mm_vmem([((s, twr), BF16, 3), ((s, tzn), BF16, 2), ((twr, tzn), F32, 3)]),
        after=(red_ffn["token"], p_early), lookahead=(0,),
    )[0]
    red_in = reduce_a("w_in", [W_IN], [gw_in])

    g_gate, g_up = reduce_d(red_ffn, red_in["token"])
    reduce_b(red_in, update(W_GATE, g_gate))

    dh1 = _mm(
        "mm_dh1", (s // tm, d // tn2, 1), [dz, g_in],
        [pl.BlockSpec((tm, n_in), lambda i, j, k: (i, 0)), pl.BlockSpec((tn2, n_in), lambda i, j, k: (j, 0))],
        [jax.ShapeDtypeStruct((s, d), F32)], [pl.BlockSpec((tm, tn2), lambda i, j, k: (i, j))],
        [(0, 1, NT, 0)], 1, (tm, tn2), 1, ep_store,
        _mm_vmem([((tm, n_in), BF16, 2), ((tn2, n_in), BF16, 2), ((tm, tn2), F32, 5)]),
        after=(red_in["token"],), lookahead=(0,),
    )[0]

    reduce_c(red_mix, dh1)
    grad_x, _, dg_mix = _rms_bwd("rms_mix_bwd", x2d, norm_mix, dh1, dx2, after=(red_mix["token"],))

    p_mix = _pack([dg_mix.reshape(small_shapes[0])])
    land_mix = jnp.zeros((2 * N_CHIPS,) + p_mix.shape, F32)
    mx_send, mx_recv, (p_mix, land_mix) = _split_start("mix_send", [p_mix, land_mix], 2 * N_CHIPS - 1, _small_exchange_copies(False))

    reduce_c(red_in, update(W_UP, g_up, (finish(red_mix, p_mix),)))
    p_early, land = _split_wait("small_wait", [p_early, land], sm_send, sm_recv, _small_exchange_copies(True), red_in["token"])
    p_mix, land_mix = _split_wait("mix_wait", [p_mix, land_mix], mx_send, mx_recv, _small_exchange_copies(True), p_early)
    me_idx = 2 * q_idx + c_idx
    packed_g = jnp.concatenate([_small_sum("mix_sum", me_idx, p_mix, land_mix), _small_sum("small_sum", me_idx, p_early, land)], axis=0)
    g_small = _unpack(packed_g, small_shapes + [(1,)])
    loss = g_small[-1].reshape(())
    g_small = g_small[:-1]
    pg = _pack(g_small + [zero1])
    small_upd = _adamw("adamw_small", pw, pg, pm, pv)
    d_small, nm_small, nv_small = [_unpack(a, small_shapes) for a in small_upd[:3]]
    finish(red_in, small_upd[0])

    small_names = ["norm_mix", "sgu_v_gain", "sgu_w_s", "sgu_b_s", "attn_sink", "rel_bias", "norm_ffn", "norm_final"]
    table = {}
    for i, n in enumerate(names):
        table[n] = (grads_big[i][None], upd[i][0][None], upd[i][1][None], upd[i][2][None])
    for i, n in enumerate(small_names):
        table[n] = (g_small[i], d_small[i], nm_small[i], nv_small[i])
    order = ["w_in", "norm_mix", "sgu_v_gain", "sgu_w_s", "sgu_b_s", "w_a", "attn_sink", "rel_bias", "w_b", "w_o", "norm_ffn",
             "w_gate", "w_up", "w_down", "norm_final"]
    outs = [loss, grad_x.reshape(1, s, d)]
    for part in range(4):
        outs += [table[n][part] for n in order]
    return tuple(outs)
```
